```python
import math
import jax, jax.numpy as jnp
from jax import lax
import numpy as np

D_MODEL = 1024
BATCH = 8
SEQ = 4096
DEPTH = 2

GRID_W = 64
ATTN_Q_HEADS = 8
ATTN_KV_HEADS = 2
ATTN_HEAD_DIM = 64
ATTN_WIDTH = ATTN_Q_HEADS * ATTN_HEAD_DIM
ATTN_KV_WIDTH = ATTN_KV_HEADS * ATTN_HEAD_DIM
Q_BLOCK = 128
RET_HEADS = 4
RET_HEAD_DIM = 128
RET_WIDTH = RET_HEADS * RET_HEAD_DIM
RET_CHUNK = 128
N_BRANCHES = 2
ROPE_THETA = 10000.0
EPS = 1e-6
IN_SPLITS = (ATTN_WIDTH, ATTN_KV_WIDTH, ATTN_KV_WIDTH, ATTN_WIDTH,
             RET_WIDTH, RET_WIDTH, RET_WIDTH, RET_WIDTH, N_BRANCHES * D_MODEL)
D_IN = sum(IN_SPLITS)

kernel_name = "hybrid_gqa_retention_gated_encoder"


def rms_norm(x, g):
    xf = x.astype(jnp.float32)
    y = xf * lax.rsqrt(jnp.mean(xf * xf, axis=-1, keepdims=True) + EPS)
    return (y * g.astype(jnp.float32)).astype(x.dtype)


def head_group_norm(o, w):
    mu = jnp.mean(o, axis=-1, keepdims=True)
    var = jnp.mean(jnp.square(o - mu), axis=-1, keepdims=True)
    y = (o - mu) * lax.rsqrt(var + EPS)
    b, s, h, d = o.shape
    return y.reshape(b, s, h * d) * w.astype(jnp.float32)


def axial_rope_tables(seq_len, head_dim):
    n_rows = seq_len // GRID_W
    row = jnp.repeat(jnp.arange(n_rows, dtype=jnp.float32), GRID_W)
    col = jnp.tile(jnp.arange(GRID_W, dtype=jnp.float32), n_rows)
    d_axis = head_dim // 2
    inv_freq = ROPE_THETA ** (-jnp.arange(0, d_axis, 2, dtype=jnp.float32) / d_axis)
    ang = jnp.concatenate([row[:, None] * inv_freq, col[:, None] * inv_freq], axis=-1)
    return jnp.cos(ang), jnp.sin(ang)


def apply_axial_rope(x, cos, sin):
    b, s, h, hd = x.shape
    q = hd // 4
    xf = x.astype(jnp.float32).reshape(b, s, h, 2, 2, q)
    x1 = xf[..., 0, :]
    x2 = xf[..., 1, :]
    c = cos.reshape(s, 1, 2, q)
    sn = sin.reshape(s, 1, 2, q)
    out = jnp.stack([x1 * c - x2 * sn, x1 * sn + x2 * c], axis=-2)
    return out.reshape(b, s, h, hd).astype(x.dtype)


def block_attention(q, k, v):
    b, s, hq, hd = q.shape
    hkv = k.shape[2]
    g = hq // hkv
    nb = s // Q_BLOCK
    qb = q.reshape(b, nb, Q_BLOCK, hkv, g, hd).transpose(1, 0, 2, 3, 4, 5)
    scale = hd ** -0.5

    def one_block(q_blk):
        sc = jnp.einsum('bqkgd,bskd->bkgqs', q_blk, k,
                        preferred_element_type=jnp.float32) * scale
        p = jax.nn.softmax(sc, axis=-1).astype(v.dtype)
        return jnp.einsum('bkgqs,bskd->bqkgd', p, v)

    o = lax.map(one_block, qb)
    return o.transpose(1, 0, 2, 3, 4, 5).reshape(b, s, hq * hd)


def retention_scan(q, k, v, log_gamma, strict):
    b, s, h, dk = q.shape
    dv = v.shape[-1]
    c = RET_CHUNK
    nc = s // c

    def to_chunks(t):
        return t.reshape(b, nc, c, h, t.shape[-1]).transpose(1, 0, 3, 2, 4)

    lg = log_gamma.astype(jnp.float32)[:, None]
    idx = jnp.arange(c, dtype=jnp.float32)
    diff = idx[:, None] - idx[None, :]
    mask = (diff > 0) if strict else (diff >= 0)
    dmat = jnp.where(mask[None], jnp.exp(jnp.maximum(diff, 0.0)[None] * lg[:, :, None]), 0.0)
    q_decay = jnp.exp((idx + 1.0)[None] * lg)
    k_decay = jnp.exp((c - 1.0 - idx)[None] * lg)
    chunk_decay = jnp.exp(c * lg[:, 0])

    def step(state, inp):
        qi, ki, vi = inp
        qf = qi.astype(jnp.float32)
        kf = ki.astype(jnp.float32)
        vf = vi.astype(jnp.float32)
        intra = jnp.einsum('bhij,bhje->bhie', jnp.einsum('bhid,bhjd->bhij', qf, kf) * dmat, vf)
        cross = jnp.einsum('bhid,bhde->bhie', qf, state) * q_decay[..., None]
        state = state * chunk_decay[:, None, None] + jnp.einsum(
            'bhjd,bhje->bhde', kf * k_decay[..., None], vf)
        return state, intra + cross

    state0 = jnp.zeros((b, h, dk, dv), jnp.float32)
    _, o = lax.scan(step, state0, (to_chunks(q), to_chunks(k), to_chunks(v)))
    return o.transpose(1, 0, 3, 2, 4).reshape(b, s, h, dv)


def bidirectional_retention(q, k, v, log_gamma_fwd, log_gamma_bwd):
    fwd = retention_scan(q, k, v, log_gamma_fwd, strict=False)
    flip = lambda t: jnp.flip(t, axis=1)
    bwd = flip(retention_scan(flip(q), flip(k), flip(v), log_gamma_bwd, strict=True))
    return fwd + bwd


def split_columns(z):
    offs = []
    acc = 0
    for w in IN_SPLITS[:-1]:
        acc += w
        offs.append(acc)
    return jnp.split(z, offs, axis=-1)


def _fwd_setup_inputs(seed: int = 0) -> dict:
    key = jax.random.key(seed)
    ks = jax.random.split(key, 14)
    f32 = jnp.float32
    x = jax.random.normal(ks[0], (BATCH, SEQ, D_MODEL), f32)
    norm_g = 1.0 + 0.02 * jax.random.normal(ks[1], (DEPTH, D_MODEL), f32)
    w_in = jax.random.normal(ks[2], (DEPTH, D_MODEL, D_IN), f32) * D_MODEL ** -0.5
    attn_q_norm = 1.0 + 0.02 * jax.random.normal(ks[3], (DEPTH, ATTN_HEAD_DIM), f32)
    attn_k_norm = 1.0 + 0.02 * jax.random.normal(ks[4], (DEPTH, ATTN_HEAD_DIM), f32)
    base_logit = jnp.log(2.0 ** (5.0 + jnp.arange(RET_HEADS, dtype=f32)) - 1.0)
    ret_decay_fwd = base_logit[None] + 0.1 * jax.random.normal(ks[5], (DEPTH, RET_HEADS), f32)
    ret_decay_bwd = base_logit[None] + 0.1 * jax.random.normal(ks[6], (DEPTH, RET_HEADS), f32)
    ret_gn_w = 1.0 + 0.02 * jax.random.normal(ks[7], (DEPTH, RET_WIDTH), f32)
    w_branch_attn = jax.random.normal(ks[8], (DEPTH, ATTN_WIDTH, D_MODEL), f32) * ATTN_WIDTH ** -0.5
    w_branch_ret = jax.random.normal(ks[9], (DEPTH, RET_WIDTH, D_MODEL), f32) * RET_WIDTH ** -0.5
    w_out = jax.random.normal(ks[10], (DEPTH, D_MODEL, D_MODEL), f32) * D_MODEL ** -0.5
    final_norm_g = 1.0 + 0.02 * jax.random.normal(ks[11], (D_MODEL,), f32)
    return {"x": x, "norm_g": norm_g, "w_in": w_in, "attn_q_norm": attn_q_norm,
            "attn_k_norm": attn_k_norm, "ret_decay_fwd": ret_decay_fwd,
            "ret_decay_bwd": ret_decay_bwd, "ret_gn_w": ret_gn_w,
            "w_branch_attn": w_branch_attn, "w_branch_ret": w_branch_ret,
            "w_out": w_out, "final_norm_g": final_norm_g}


def _fwd_reference(x, norm_g, w_in, attn_q_norm, attn_k_norm, ret_decay_fwd, ret_decay_bwd,
              ret_gn_w, w_branch_attn, w_branch_ret, w_out, final_norm_g):
    b, s, d = x.shape
    dt = x.dtype
    cos_a, sin_a = axial_rope_tables(s, ATTN_HEAD_DIM)
    cos_r, sin_r = axial_rope_tables(s, RET_HEAD_DIM)

    for layer in range(DEPTH):
        h = rms_norm(x, norm_g[layer])
        z = h @ w_in[layer]
        qa, ka, va, ga, qr, kr, vr, gr, gm = split_columns(z)

        qa = rms_norm(qa.reshape(b, s, ATTN_Q_HEADS, ATTN_HEAD_DIM), attn_q_norm[layer])
        ka = rms_norm(ka.reshape(b, s, ATTN_KV_HEADS, ATTN_HEAD_DIM), attn_k_norm[layer])
        qa = apply_axial_rope(qa, cos_a, sin_a)
        ka = apply_axial_rope(ka, cos_a, sin_a)
        va = va.reshape(b, s, ATTN_KV_HEADS, ATTN_HEAD_DIM)
        oa = block_attention(qa, ka, va)
        ya = (jax.nn.silu(ga) * oa) @ w_branch_attn[layer]

        qr = apply_axial_rope(qr.reshape(b, s, RET_HEADS, RET_HEAD_DIM), cos_r, sin_r)
        kr = apply_axial_rope(kr.reshape(b, s, RET_HEADS, RET_HEAD_DIM), cos_r, sin_r)
        kr = kr * (RET_HEAD_DIM ** -0.5)
        vr = vr.reshape(b, s, RET_HEADS, RET_HEAD_DIM)
        lg_f = jax.nn.log_sigmoid(ret_decay_fwd[layer].astype(jnp.float32))
        lg_b = jax.nn.log_sigmoid(ret_decay_bwd[layer].astype(jnp.float32))
        orr = bidirectional_retention(qr, kr, vr, lg_f, lg_b)
        orr = head_group_norm(orr, ret_gn_w[layer]).astype(dt)
        yb = (jax.nn.silu(gr) * orr) @ w_branch_ret[layer]

        gates = jax.nn.sigmoid(gm.astype(jnp.float32)).astype(dt).reshape(b, s, N_BRANCHES, d)
        merged = gates[:, :, 0] * ya + gates[:, :, 1] * yb
        x = x + merged @ w_out[layer]

    return rms_norm(x, final_norm_g)


import jax as _jax
import jax.numpy as _jnp

TWIN_FORMAT = 'train_step'
FWD_PARAMS = ['x', 'norm_g', 'w_in', 'attn_q_norm', 'attn_k_norm', 'ret_decay_fwd', 'ret_decay_bwd', 'ret_gn_w', 'w_branch_attn', 'w_branch_ret', 'w_out', 'final_norm_g']
TWIN_WEIGHTS = ['norm_g', 'w_in', 'attn_q_norm', 'attn_k_norm', 'ret_decay_fwd', 'ret_decay_bwd', 'ret_gn_w', 'w_branch_attn', 'w_branch_ret', 'w_out', 'final_norm_g']
TWIN_DIFF_INPUT = 'x'
TWIN_INPUTS = ['x', 'norm_g', 'w_in', 'attn_q_norm', 'attn_k_norm', 'ret_decay_fwd', 'ret_decay_bwd', 'ret_gn_w', 'w_branch_attn', 'w_branch_ret', 'w_out', 'final_norm_g', 'loss_target', 'm_norm_g', 'm_w_in', 'm_attn_q_norm', 'm_attn_k_norm', 'm_ret_decay_fwd', 'm_ret_decay_bwd', 'm_ret_gn_w', 'm_w_branch_attn', 'm_w_branch_ret', 'm_w_out', 'm_final_norm_g', 'v_norm_g', 'v_w_in', 'v_attn_q_norm', 'v_attn_k_norm', 'v_ret_decay_fwd', 'v_ret_decay_bwd', 'v_ret_gn_w', 'v_w_branch_attn', 'v_w_branch_ret', 'v_w_out', 'v_final_norm_g']
TWIN_OUTPUTS = ['loss', 'grad_x', 'grad_norm_g', 'grad_w_in', 'grad_attn_q_norm', 'grad_attn_k_norm', 'grad_ret_decay_fwd', 'grad_ret_decay_bwd', 'grad_ret_gn_w', 'grad_w_branch_attn', 'grad_w_branch_ret', 'grad_w_out', 'grad_final_norm_g', 'delta_norm_g', 'delta_w_in', 'delta_attn_q_norm', 'delta_attn_k_norm', 'delta_ret_decay_fwd', 'delta_ret_decay_bwd', 'delta_ret_gn_w', 'delta_w_branch_attn', 'delta_w_branch_ret', 'delta_w_out', 'delta_final_norm_g', 'new_m_norm_g', 'new_m_w_in', 'new_m_attn_q_norm', 'new_m_attn_k_norm', 'new_m_ret_decay_fwd', 'new_m_ret_decay_bwd', 'new_m_ret_gn_w', 'new_m_w_branch_attn', 'new_m_w_branch_ret', 'new_m_w_out', 'new_m_final_norm_g', 'new_v_norm_g', 'new_v_w_in', 'new_v_attn_q_norm', 'new_v_attn_k_norm', 'new_v_ret_decay_fwd', 'new_v_ret_decay_bwd', 'new_v_ret_gn_w', 'new_v_w_branch_attn', 'new_v_w_branch_ret', 'new_v_w_out', 'new_v_final_norm_g']
TWIN_LEAF_KINDS = {'loss': 'loss', 'grad_x': 'grad_x', 'grad_norm_g': 'grad_w', 'grad_w_in': 'grad_w', 'grad_attn_q_norm': 'grad_w', 'grad_attn_k_norm': 'grad_w', 'grad_ret_decay_fwd': 'grad_w', 'grad_ret_decay_bwd': 'grad_w', 'grad_ret_gn_w': 'grad_w', 'grad_w_branch_attn': 'grad_w', 'grad_w_branch_ret': 'grad_w', 'grad_w_out': 'grad_w', 'grad_final_norm_g': 'grad_w', 'delta_norm_g': 'delta_w', 'delta_w_in': 'delta_w', 'delta_attn_q_norm': 'delta_w', 'delta_attn_k_norm': 'delta_w', 'delta_ret_decay_fwd': 'delta_w', 'delta_ret_decay_bwd': 'delta_w', 'delta_ret_gn_w': 'delta_w', 'delta_w_branch_attn': 'delta_w', 'delta_w_branch_ret': 'delta_w', 'delta_w_out': 'delta_w', 'delta_final_norm_g': 'delta_w', 'new_m_norm_g': 'new_m', 'new_m_w_in': 'new_m', 'new_m_attn_q_norm': 'new_m', 'new_m_attn_k_norm': 'new_m', 'new_m_ret_decay_fwd': 'new_m', 'new_m_ret_decay_bwd': 'new_m', 'new_m_ret_gn_w': 'new_m', 'new_m_w_branch_attn': 'new_m', 'new_m_w_branch_ret': 'new_m', 'new_m_w_out': 'new_m', 'new_m_final_norm_g': 'new_m', 'new_v_norm_g': 'new_v', 'new_v_w_in': 'new_v', 'new_v_attn_q_norm': 'new_v', 'new_v_attn_k_norm': 'new_v', 'new_v_ret_decay_fwd': 'new_v', 'new_v_ret_decay_bwd': 'new_v', 'new_v_ret_gn_w': 'new_v', 'new_v_w_branch_attn': 'new_v', 'new_v_w_branch_ret': 'new_v', 'new_v_w_out': 'new_v', 'new_v_final_norm_g': 'new_v'}


def _forward(args):
    return _fwd_reference(*[args[k] for k in FWD_PARAMS])


def _output_shape():
    out = _jax.eval_shape(lambda: _forward(_fwd_setup_inputs(0)))
    return out.shape, out.dtype

N_MICROBATCH = 1
ADAM_LR = 0.001
ADAM_B1 = 0.9
ADAM_B2 = 0.999
ADAM_EPS = 1e-08
ADAM_WD = 0.01
ADAM_STEP = 10
PER_EXAMPLE_BATCH_AXIS = {'x': 0, 'loss_target': 0}
SHARED_INPUTS = []
_WEIGHT_DTYPES = {'norm_g': _jnp.float32, 'w_in': _jnp.float32, 'attn_q_norm': _jnp.float32, 'attn_k_norm': _jnp.float32, 'ret_decay_fwd': _jnp.float32, 'ret_decay_bwd': _jnp.float32, 'ret_gn_w': _jnp.float32, 'w_branch_attn': _jnp.float32, 'w_branch_ret': _jnp.float32, 'w_out': _jnp.float32, 'final_norm_g': _jnp.float32}
MOMENT_SCALE = {'norm_g': 1.247667e-01, 'w_in': 5.135403e-02, 'attn_q_norm': 2.347523e-02, 'attn_k_norm': 2.449546e-02, 'ret_decay_fwd': 4.551888e-01, 'ret_decay_bwd': 1.064844e+00, 'ret_gn_w': 8.309088e-02, 'w_branch_attn': 5.810470e-03, 'w_branch_ret': 5.705182e-02, 'w_out': 5.734732e-02, 'final_norm_g': 3.199230e+01}


def _to_microbatches(a, axis):
    t = _jnp.moveaxis(a, axis, 0)
    t = t.reshape((N_MICROBATCH, t.shape[0] // N_MICROBATCH) + t.shape[1:])
    return _jnp.moveaxis(t, 1, axis + 1)


def setup_inputs(seed: int = 0) -> dict:
    inp = _fwd_setup_inputs(seed)
    key = _jax.random.fold_in(_jax.random.key(seed), 7919)
    shape, _ = _output_shape()
    out = dict(inp)
    out["loss_target"] = _jax.random.normal(_jax.random.fold_in(key, 0), shape, _jnp.float32)
    for i, name in enumerate(TWIN_WEIGHTS):
        w = inp[name].astype(_jnp.float32)
        if MOMENT_SCALE is None:
            s = _jnp.sqrt(_jnp.mean(_jnp.square(w)) + 1e-30)
        else:
            s = MOMENT_SCALE[name]
        km, kv = _jax.random.split(_jax.random.fold_in(key, i + 1))
        out[name] = w
        out["m_" + name] = s * _jax.random.normal(km, w.shape, _jnp.float32)
        out["v_" + name] = (s * s) * _jax.random.uniform(kv, w.shape, _jnp.float32, 0.5, 1.5)
    if N_MICROBATCH > 1:
        for name, axis in PER_EXAMPLE_BATCH_AXIS.items():
            out[name] = _to_microbatches(out[name], axis)
    return {'x': out['x'], 'norm_g': out['norm_g'], 'w_in': out['w_in'], 'attn_q_norm': out['attn_q_norm'], 'attn_k_norm': out['attn_k_norm'], 'ret_decay_fwd': out['ret_decay_fwd'], 'ret_decay_bwd': out['ret_decay_bwd'], 'ret_gn_w': out['ret_gn_w'], 'w_branch_attn': out['w_branch_attn'], 'w_branch_ret': out['w_branch_ret'], 'w_out': out['w_out'], 'final_norm_g': out['final_norm_g'], 'loss_target': out['loss_target'], 'm_norm_g': out['m_norm_g'], 'm_w_in': out['m_w_in'], 'm_attn_q_norm': out['m_attn_q_norm'], 'm_attn_k_norm': out['m_attn_k_norm'], 'm_ret_decay_fwd': out['m_ret_decay_fwd'], 'm_ret_decay_bwd': out['m_ret_decay_bwd'], 'm_ret_gn_w': out['m_ret_gn_w'], 'm_w_branch_attn': out['m_w_branch_attn'], 'm_w_branch_ret': out['m_w_branch_ret'], 'm_w_out': out['m_w_out'], 'm_final_norm_g': out['m_final_norm_g'], 'v_norm_g': out['v_norm_g'], 'v_w_in': out['v_w_in'], 'v_attn_q_norm': out['v_attn_q_norm'], 'v_attn_k_norm': out['v_attn_k_norm'], 'v_ret_decay_fwd': out['v_ret_decay_fwd'], 'v_ret_decay_bwd': out['v_ret_decay_bwd'], 'v_ret_gn_w': out['v_ret_gn_w'], 'v_w_branch_attn': out['v_w_branch_attn'], 'v_w_branch_ret': out['v_w_branch_ret'], 'v_w_out': out['v_w_out'], 'v_final_norm_g': out['v_final_norm_g']}


def _loss(weights, diff, rest, loss_target):
    with _jax.named_scope("forward"):
        args = {**rest, TWIN_DIFF_INPUT: diff, **{k: w.astype(_WEIGHT_DTYPES[k]) for k, w in weights.items()}}
        y = _forward(args)
    with _jax.named_scope("loss_head"):
        err = _jnp.square(y.astype(_jnp.float32) - loss_target)
        return 0.5 * _jnp.sum(_jnp.mean(err, axis=-1)) if err.ndim else 0.5 * err


def _adamw(w, g, m, v):
    m = ADAM_B1 * m + (1.0 - ADAM_B1) * g
    v = ADAM_B2 * v + (1.0 - ADAM_B2) * _jnp.square(g)
    m_hat = m / (1.0 - ADAM_B1 ** ADAM_STEP)
    v_hat = v / (1.0 - ADAM_B2 ** ADAM_STEP)
    delta = -ADAM_LR * (m_hat / (_jnp.sqrt(v_hat) + ADAM_EPS) + ADAM_WD * w)
    return delta, m, v


def reference(x, norm_g, w_in, attn_q_norm, attn_k_norm, ret_decay_fwd, ret_decay_bwd, ret_gn_w, w_branch_attn, w_branch_ret, w_out, final_norm_g, loss_target, m_norm_g, m_w_in, m_attn_q_norm, m_attn_k_norm, m_ret_decay_fwd, m_ret_decay_bwd, m_ret_gn_w, m_w_branch_attn, m_w_branch_ret, m_w_out, m_final_norm_g, v_norm_g, v_w_in, v_attn_q_norm, v_attn_k_norm, v_ret_decay_fwd, v_ret_decay_bwd, v_ret_gn_w, v_w_branch_attn, v_w_branch_ret, v_w_out, v_final_norm_g):
    given = dict(x=x, norm_g=norm_g, w_in=w_in, attn_q_norm=attn_q_norm, attn_k_norm=attn_k_norm, ret_decay_fwd=ret_decay_fwd, ret_decay_bwd=ret_decay_bwd, ret_gn_w=ret_gn_w, w_branch_attn=w_branch_attn, w_branch_ret=w_branch_ret, w_out=w_out, final_norm_g=final_norm_g, loss_target=loss_target, m_norm_g=m_norm_g, m_w_in=m_w_in, m_attn_q_norm=m_attn_q_norm, m_attn_k_norm=m_attn_k_norm, m_ret_decay_fwd=m_ret_decay_fwd, m_ret_decay_bwd=m_ret_decay_bwd, m_ret_gn_w=m_ret_gn_w, m_w_branch_attn=m_w_branch_attn, m_w_branch_ret=m_w_branch_ret, m_w_out=m_w_out, m_final_norm_g=m_final_norm_g, v_norm_g=v_norm_g, v_w_in=v_w_in, v_attn_q_norm=v_attn_q_norm, v_attn_k_norm=v_attn_k_norm, v_ret_decay_fwd=v_ret_decay_fwd, v_ret_decay_bwd=v_ret_decay_bwd, v_ret_gn_w=v_ret_gn_w, v_w_branch_attn=v_w_branch_attn, v_w_branch_ret=v_w_branch_ret, v_w_out=v_w_out, v_final_norm_g=v_final_norm_g)
    weights = {n: given[n] for n in TWIN_WEIGHTS}
    shared = {n: given[n] for n in SHARED_INPUTS}
    per_example = {n: given[n] for n in ['x']}
    grad_fn = _jax.value_and_grad(_loss, argnums=(0, 1))

    def one_microbatch(ex, loss_target):
        ex = dict(ex)
        diff = ex.pop(TWIN_DIFF_INPUT)
        return grad_fn(weights, diff, {**shared, **ex}, loss_target)

    if N_MICROBATCH == 1:
        loss, (grad_w, grad_x) = one_microbatch(per_example, given["loss_target"])
    else:
        def body(carry, xs):
            loss_sum, grad_sum = carry
            l_k, (gw_k, gx_k) = one_microbatch(xs[0], xs[1])
            with _jax.named_scope("update"):
                return (loss_sum + l_k, _jax.tree.map(_jnp.add, grad_sum, gw_k)), gx_k

        init = (_jnp.zeros((), _jnp.float32), _jax.tree.map(_jnp.zeros_like, weights))
        (loss, grad_w), grad_x = _jax.lax.scan(body, init, (per_example, given["loss_target"]))
    with _jax.named_scope("update"):
        delta_w, new_m, new_v = {}, {}, {}
        for n in TWIN_WEIGHTS:
            delta_w[n], new_m[n], new_v[n] = _adamw(weights[n], grad_w[n], given["m_" + n], given["v_" + n])
    return (loss, grad_x, *[grad_w[n] for n in TWIN_WEIGHTS], *[delta_w[n] for n in TWIN_WEIGHTS],
            *[new_m[n] for n in TWIN_WEIGHTS], *[new_v[n] for n in TWIN_WEIGHTS])
```

```python
import functools

import jax
import jax.numpy as jnp
from jax import lax
from jax.experimental import pallas as pl
from jax.experimental.pallas import tpu as pltpu

F32 = jnp.float32
BF16 = jnp.bfloat16
SDS = jax.ShapeDtypeStruct

D_MODEL = 1024
DEPTH = 2
GRID_W = 64
ATTN_Q_HEADS = 8
ATTN_KV_HEADS = 2
ATTN_HEAD_DIM = 64
ATTN_WIDTH = 512
ATTN_KV_WIDTH = 128
RET_HEADS = 4
RET_HEAD_DIM = 128
RET_WIDTH = 512
RET_CHUNK = 128
ROPE_THETA = 10000.0
EPS = 1e-6
D_IN = 5376
N_DEV = 8

ADAM_LR = 0.001
ADAM_B1 = 0.9
ADAM_B2 = 0.999
ADAM_EPS = 1e-08
ADAM_WD = 0.01
ADAM_STEP = 10

SEG = {
    "qa": (0, 512, 0),
    "ga": (768, 512, 512),
    "qr": (1280, 512, 1024),
    "kr": (1792, 512, 1536),
    "vr": (2304, 512, 2048),
    "gr": (2816, 512, 2560),
    "gm": (3328, 2048, 3072),
    "ka": (512, 128, 5120),
    "va": (640, 128, 5248),
}

VMEM_LIMIT = 60 * 1024 * 1024
NT = (((1,), (1,)), ((), ()))
TN = (((0,), (0,)), ((), ()))
MESH_ID = pl.DeviceIdType.MESH
ANY = pl.BlockSpec(memory_space=pl.ANY)


def _params(sem=None, vmem=VMEM_LIMIT):
    return pltpu.CompilerParams(dimension_semantics=sem, vmem_limit_bytes=vmem)


def _dot(a, b, dims=None):
    if dims is None:
        return jnp.dot(a, b, preferred_element_type=F32)
    return lax.dot_general(a, b, dims, preferred_element_type=F32)


def _sigmoid(x):
    return 1.0 / (1.0 + jnp.exp(-x))


def _swap_halves(x, q):
    n = x.shape[-1]
    axis = x.ndim - 1
    lane = lax.broadcasted_iota(jnp.int32, x.shape, axis)
    first = (lane % (2 * q)) < q
    return jnp.where(first, pltpu.roll(x, n - q, axis), pltpu.roll(x, q, axis))


def _rope(x, cos, sin_signed, q):
    return x * cos + _swap_halves(x, q) * sin_signed


def _rope_bwd(dy, cos, sin_signed, q):
    return dy * cos - _swap_halves(dy, q) * sin_signed


def _group_mean(v, ones_bd):
    hi = v.astype(BF16)
    r1 = v - hi.astype(F32)
    mid = r1.astype(BF16)
    lo = (r1 - mid.astype(F32)).astype(BF16)
    return _dot(hi, ones_bd) + _dot(mid, ones_bd) + _dot(lo, ones_bd)


def _rope_tables(t, head_dim):
    n_rows = t // GRID_W
    row = jnp.repeat(jnp.arange(n_rows, dtype=F32), GRID_W)
    col = jnp.tile(jnp.arange(GRID_W, dtype=F32), n_rows)
    d_axis = head_dim // 2
    inv_freq = ROPE_THETA ** (-jnp.arange(0, d_axis, 2, dtype=F32) / d_axis)
    ar = row[:, None] * inv_freq
    ac = col[:, None] * inv_freq
    cr, sr, cc, sc = jnp.cos(ar), jnp.sin(ar), jnp.cos(ac), jnp.sin(ac)
    return jnp.concatenate([cr, cr, cc, cc], axis=-1), jnp.concatenate([-sr, sr, -sc, sc], axis=-1)


def _in_proj(x, g, w_t):
    t, d = x.shape
    tm = min(256, t)

    def body(x_ref, g_ref, w_ref, z_ref, ht_ref):
        xv = x_ref[...]
        r = lax.rsqrt(jnp.mean(xv * xv, axis=-1, keepdims=True) + EPS)
        h = xv * r * g_ref[...]
        ht_ref[...] = h.T.astype(BF16)
        hb = h.astype(BF16)
        for nat, w, off in SEG.values():
            z_ref[:, off:off + w] = _dot(hb, w_ref[nat:nat + w, :], NT)

    return pl.pallas_call(
        body, name="in_proj", grid=(t // tm,),
        in_specs=[pl.BlockSpec((tm, d), lambda i: (i, 0)), pl.BlockSpec((1, d), lambda i: (0, 0)),
                  pl.BlockSpec((D_IN, d), lambda i: (0, 0))],
        out_specs=[pl.BlockSpec((tm, D_IN), lambda i: (i, 0)), pl.BlockSpec((d, tm), lambda i: (0, i))],
        out_shape=[SDS((t, D_IN), F32), SDS((d, t), BF16)],
        compiler_params=_params(("parallel",)),
    )(x, g, w_t)


def _attn_prep(z, qn, kn, cos, sin, ones_bd):
    t = z.shape[0]
    tm = min(512, t)

    def body(zq_ref, zkv_ref, qn_ref, kn_ref, c_ref, s_ref, b_ref, q_out, k_out, v_out):
        bd = b_ref[...]
        c2, s2 = c_ref[...], s_ref[...]
        cq = jnp.concatenate([c2] * 4, axis=-1)
        sq = jnp.concatenate([s2] * 4, axis=-1)
        xq = zq_ref[...]
        yq = xq * lax.rsqrt(_group_mean(xq * xq, bd) + EPS) * qn_ref[...]
        yq = _rope(yq, cq, sq, ATTN_HEAD_DIM // 4)
        for h in range(ATTN_Q_HEADS):
            q_out[h] = yq[:, h * ATTN_HEAD_DIM:(h + 1) * ATTN_HEAD_DIM].astype(BF16)
        zkv = zkv_ref[...]
        xk, xv = zkv[:, :ATTN_KV_WIDTH], zkv[:, ATTN_KV_WIDTH:]
        yk = xk * lax.rsqrt(_group_mean(xk * xk, bd[:ATTN_KV_WIDTH, :ATTN_KV_WIDTH]) + EPS) * kn_ref[...]
        yk = _rope(yk, c2, s2, ATTN_HEAD_DIM // 4)
        for h in range(ATTN_KV_HEADS):
            k_out[h] = yk[:, h * ATTN_HEAD_DIM:(h + 1) * ATTN_HEAD_DIM].astype(BF16)
            v_out[h] = xv[:, h * ATTN_HEAD_DIM:(h + 1) * ATTN_HEAD_DIM].astype(BF16)

    kv_blk = SEG["ka"][2] // 256
    return pl.pallas_call(
        body, name="attn_prep", grid=(t // tm,),
        in_specs=[pl.BlockSpec((tm, 512), lambda i: (i, 0)), pl.BlockSpec((tm, 256), lambda i: (i, kv_blk)),
                  pl.BlockSpec((1, 512), lambda i: (0, 0)), pl.BlockSpec((1, 128), lambda i: (0, 0)),
                  pl.BlockSpec((tm, 128), lambda i: (i, 0)), pl.BlockSpec((tm, 128), lambda i: (i, 0)),
                  pl.BlockSpec((512, 512), lambda i: (0, 0))],
        out_specs=[pl.BlockSpec((ATTN_Q_HEADS, tm, ATTN_HEAD_DIM), lambda i: (0, i, 0)),
                   pl.BlockSpec((ATTN_KV_HEADS, tm, ATTN_HEAD_DIM), lambda i: (0, i, 0)),
                   pl.BlockSpec((ATTN_KV_HEADS, tm, ATTN_HEAD_DIM), lambda i: (0, i, 0))],
        out_shape=[SDS((ATTN_Q_HEADS, t, ATTN_HEAD_DIM), BF16), SDS((ATTN_KV_HEADS, t, ATTN_HEAD_DIM), BF16),
                   SDS((ATTN_KV_HEADS, t, ATTN_HEAD_DIM), BF16)],
        compiler_params=_params(("parallel",)),
    )(z, z, qn, kn, cos, sin, ones_bd)


def _attn_fwd(q, k, v):
    t = q.shape[1]
    tq = min(256, t)
    scale = ATTN_HEAD_DIM ** -0.5

    def body(q_ref, k_ref, v_ref, o_ref):
        kk, vv = k_ref[0], v_ref[0]
        outs = []
        for j in range(2):
            s = _dot(q_ref[j], kk, NT) * scale
            e = jnp.exp(s - jnp.max(s, axis=-1, keepdims=True))
            l = jnp.sum(e, axis=-1, keepdims=True)
            outs.append(_dot(e.astype(BF16), vv) / l)
        o_ref[...] = jnp.concatenate(outs, axis=-1)

    return pl.pallas_call(
        body, name="attn_fwd", grid=(4, t // tq),
        in_specs=[pl.BlockSpec((2, tq, ATTN_HEAD_DIM), lambda p, i: (p, i, 0)),
                  pl.BlockSpec((1, t, ATTN_HEAD_DIM), lambda p, i: (p // 2, 0, 0)),
                  pl.BlockSpec((1, t, ATTN_HEAD_DIM), lambda p, i: (p // 2, 0, 0))],
        out_specs=pl.BlockSpec((tq, 128), lambda p, i: (i, p)),
        out_shape=SDS((t, ATTN_WIDTH), F32),
        compiler_params=_params(("parallel", "parallel")),
    )(q, k, v)


class _Dir:
    def __init__(self, lg, strict_future):
        c = RET_CHUNK
        ia = lax.broadcasted_iota(jnp.int32, (c, c), 0).astype(F32)
        ib = lax.broadcasted_iota(jnp.int32, (c, c), 1).astype(F32)
        col = lax.broadcasted_iota(jnp.int32, (c, 1), 0).astype(F32)
        row = lax.broadcasted_iota(jnp.int32, (1, c), 1).astype(F32)
        if strict_future:
            dist = ib - ia
            mask = dist > 0
            self.wq, self.wk, wk_row = c - col, col, row
        else:
            dist = ia - ib
            mask = dist >= 0
            self.wq, self.wk, wk_row = col + 1.0, c - 1.0 - col, c - 1.0 - row
        self.dist = jnp.maximum(dist, 0.0)
        self.d = jnp.where(mask, jnp.exp(self.dist * lg), 0.0)
        self.qd = jnp.exp(self.wq * lg)
        self.kd_col = jnp.exp(self.wk * lg)
        self.kd_row = jnp.exp(wk_row * lg)
        self.cd = jnp.exp(jnp.full((1, 1), float(c), F32) * lg)


def _ret_fwd(z, lgf, lgb, gnw, cos, sin):
    t = z.shape[0]
    c = RET_CHUNK
    nc = t // c
    hd = RET_HEAD_DIM

    def body(lgf_ref, lgb_ref, q_ref, k_ref, v_ref, c_ref, s_ref, w_ref,
             qo_ref, ko_ref, vo_ref, orr_ref, on_ref, kt, of, ob):
        h = pl.program_id(0)
        fw = _Dir(lgf_ref[h], False)
        bw = _Dir(lgb_ref[h], True)
        cc, ss = c_ref[...], s_ref[...]
        qo_ref[...] = _rope(q_ref[...], cc, ss, hd // 4).astype(BF16)
        kr = _rope(k_ref[...], cc, ss, hd // 4) * (hd ** -0.5)
        ko_ref[...] = kr.astype(BF16)
        vo_ref[...] = v_ref[...].astype(BF16)
        for i in range(nc):
            kt[i] = kr[i * c:(i + 1) * c, :].T.astype(BF16)

        def one(ci, s, p, out):
            sl = pl.ds(pl.multiple_of(ci * c, c), c)
            qq, kk, vv = qo_ref[sl, :], ko_ref[sl, :], vo_ref[sl, :]
            a = _dot(qq, kk, NT)
            intra = _dot((a * p.d).astype(BF16), vv)
            cross = _dot(qq, s.astype(BF16)) * p.qd
            out[sl, :] = intra + cross
            return s * p.cd + _dot((kt[ci].astype(F32) * p.kd_row).astype(BF16), vv)

        def step(i, carry):
            sf, sb = carry
            return one(i, sf, fw, of), one(nc - 1 - i, sb, bw, ob)

        zero = jnp.zeros((hd, hd), F32)
        lax.fori_loop(0, nc, step, (zero, zero))
        o = of[...] + ob[...]
        orr_ref[...] = o
        xc = o - jnp.mean(o, axis=-1, keepdims=True)
        var = jnp.mean(xc * xc, axis=-1, keepdims=True)
        on_ref[...] = xc * lax.rsqrt(var + EPS) * w_ref[...]

    smem = pl.BlockSpec(memory_space=pltpu.SMEM)
    col = lambda name: (lambda h: (0, SEG[name][2] // 128 + h))
    head = pl.BlockSpec((t, 128), lambda h: (0, h))
    full = pl.BlockSpec((t, 128), lambda h: (0, 0))
    return pl.pallas_call(
        body, name="ret_fwd", grid=(RET_HEADS,),
        in_specs=[smem, smem, pl.BlockSpec((t, 128), col("qr")), pl.BlockSpec((t, 128), col("kr")),
                  pl.BlockSpec((t, 128), col("vr")), full, full, pl.BlockSpec((1, 128), lambda h: (0, h))],
        out_specs=[head, head, head, head, head],
        out_shape=[SDS((t, RET_WIDTH), BF16)] * 3 + [SDS((t, RET_WIDTH), F32)] * 2,
        scratch_shapes=[pltpu.VMEM((nc, hd, c), BF16), pltpu.VMEM((t, hd), F32), pltpu.VMEM((t, hd), F32)],
        compiler_params=_params(("parallel",)),
    )(lgf, lgb, z, z, z, cos, sin, gnw)


def _merge_fwd(x, z, oa, on, wb_t, wout):
    t, d = x.shape
    tm = min(256, t)

    def body(x_ref, ga_ref, gr_ref, gm0_ref, gm1_ref, oa_ref, on_ref, wb_ref, wo_ref, xn_ref, ya_ref, yb_ref):
        ga, gr = ga_ref[...], gr_ref[...]
        ua = ga * _sigmoid(ga) * oa_ref[...]
        ub = gr * _sigmoid(gr) * on_ref[...]
        ya = _dot(ua.astype(BF16), wb_ref[:, :512], NT)
        yb = _dot(ub.astype(BF16), wb_ref[:, 512:], NT)
        ya_ref[...] = ya
        yb_ref[...] = yb
        merged = _sigmoid(gm0_ref[...]) * ya + _sigmoid(gm1_ref[...]) * yb
        xn_ref[...] = x_ref[...] + _dot(merged.astype(BF16), wo_ref[...])

    row = lambda w, j: pl.BlockSpec((tm, w), lambda i: (i, j))
    const = lambda shape: pl.BlockSpec(shape, lambda i: (0, 0))
    return pl.pallas_call(
        body, name="merge_fwd", grid=(t // tm,),
        in_specs=[row(d, 0), row(512, SEG["ga"][2] // 512), row(512, SEG["gr"][2] // 512),
                  row(1024, SEG["gm"][2] // 1024), row(1024, SEG["gm"][2] // 1024 + 1),
                  row(512, 0), row(512, 0), const((d, 1024)), const((d, d))],
        out_specs=[row(d, 0), row(d, 0), row(d, 0)],
        out_shape=[SDS((t, d), F32)] * 3,
        compiler_params=_params(("parallel",)),
    )(x, z, z, z, z, oa, on, wb_t, wout)


def _final_loss(x, g, target):
    t, d = x.shape
    tm = min(512, t)
    n = t // tm

    def body(x_ref, g_ref, t_ref, dx_ref, dg_ref, loss_ref, acc_g, acc_l):
        i = pl.program_id(0)

        @pl.when(i == 0)
        def _():
            acc_g[...] = jnp.zeros_like(acc_g)
            acc_l[...] = jnp.zeros_like(acc_l)

        xv, gv = x_ref[...], g_ref[...]
        r = lax.rsqrt(jnp.mean(xv * xv, axis=-1, keepdims=True) + EPS)
        xh = xv * r
        err = xh * gv - t_ref[...]
        dy = err * (1.0 / d)
        gy = dy * gv
        dx_ref[...] = r * (gy - xh * jnp.mean(gy * xh, axis=-1, keepdims=True))
        acc_g[...] += jnp.sum((dy * xh).reshape(tm // 8, 8, d), axis=0)
        acc_l[...] += jnp.sum((err * err).reshape(tm // 8, 8, d), axis=0)

        @pl.when(i == n - 1)
        def _():
            dg_ref[...] = jnp.sum(acc_g[...], axis=0, keepdims=True)
            tot = jnp.sum(jnp.sum(acc_l[...], axis=0, keepdims=True), axis=1, keepdims=True)
            loss_ref[...] = jnp.broadcast_to(tot * (0.5 / d), (1, 128))

    return pl.pallas_call(
        body, name="final_loss", grid=(n,),
        in_specs=[pl.BlockSpec((tm, d), lambda i: (i, 0)), pl.BlockSpec((1, d), lambda i: (0, 0)),
                  pl.BlockSpec((tm, d), lambda i: (i, 0))],
        out_specs=[pl.BlockSpec((tm, d), lambda i: (i, 0)), pl.BlockSpec((1, d), lambda i: (0, 0)),
                   pl.BlockSpec((1, 128), lambda i: (0, 0))],
        out_shape=[SDS((t, d), F32), SDS((1, d), F32), SDS((1, 128), F32)],
        scratch_shapes=[pltpu.VMEM((8, d), F32), pltpu.VMEM((8, d), F32)],
        compiler_params=_params(("arbitrary",)),
    )(x, g, target)


def _merge_bwd(dxo, z, oa, on, ya, yb, wb_t, wout):
    t, d = dxo.shape
    tm = min(256, t)
    n = t // tm

    def body(dx_ref, ga_ref, gr_ref, gm0_ref, gm1_ref, oa_ref, on_ref, ya_ref, yb_ref, wb_ref, wo_ref,
             doa_ref, don_ref, dz_ref, dwo_ref, dwb_ref, acc_o, acc_b):
        i = pl.program_id(0)

        @pl.when(i == 0)
        def _():
            acc_o[...] = jnp.zeros_like(acc_o)
            acc_b[...] = jnp.zeros_like(acc_b)

        dxb = dx_ref[...].astype(BF16)
        ya, yb = ya_ref[...], yb_ref[...]
        g0, g1 = _sigmoid(gm0_ref[...]), _sigmoid(gm1_ref[...])
        mb = (g0 * ya + g1 * yb).astype(BF16)
        dm = _dot(dxb, wo_ref[...], NT)
        dya = (dm * g0).astype(BF16)
        dyb = (dm * g1).astype(BF16)
        dz_ref[:, 1024:2048] = (dm * ya * g0 * (1.0 - g0)).astype(BF16)
        dz_ref[:, 2048:3072] = (dm * yb * g1 * (1.0 - g1)).astype(BF16)

        def branch(g_ref, o_ref, dy, w, do_ref, lo):
            gv, ov = g_ref[...], o_ref[...]
            sg = _sigmoid(gv)
            silu = gv * sg
            du = _dot(dy, w)
            do_ref[...] = du * silu
            dz_ref[:, lo:lo + 512] = (du * ov * (sg * (1.0 + gv * (1.0 - sg)))).astype(BF16)
            acc_b[:, lo:lo + 512] += _dot(dy, (silu * ov).astype(BF16), TN)

        branch(ga_ref, oa_ref, dya, wb_ref[:, :512], doa_ref, 0)
        branch(gr_ref, on_ref, dyb, wb_ref[:, 512:], don_ref, 512)
        acc_o[...] += _dot(mb, dxb, TN)

        @pl.when(i == n - 1)
        def _():
            dwo_ref[...] = acc_o[...].astype(BF16)
            dwb_ref[...] = acc_b[...].astype(BF16)

    row = lambda w, j: pl.BlockSpec((tm, w), lambda i: (i, j))
    const = lambda shape: pl.BlockSpec(shape, lambda i: (0, 0))
    return pl.pallas_call(
        body, name="merge_bwd", grid=(n,),
        in_specs=[row(d, 0), row(512, SEG["ga"][2] // 512), row(512, SEG["gr"][2] // 512),
                  row(1024, SEG["gm"][2] // 1024), row(1024, SEG["gm"][2] // 1024 + 1),
                  row(512, 0), row(512, 0), row(d, 0), row(d, 0), const((d, 1024)), const((d, d))],
        out_specs=[row(512, 0), row(512, 0), row(3072, 0), const((d, d)), const((d, 1024))],
        out_shape=[SDS((t, 512), F32), SDS((t, 512), F32), SDS((t, 3072), BF16), SDS((d, d), BF16),
                   SDS((d, 1024), BF16)],
        scratch_shapes=[pltpu.VMEM((d, d), F32), pltpu.VMEM((d, 1024), F32)],
        compiler_params=_params(("arbitrary",)),
    )(dxo, z, z, z, z, oa, on, ya, yb, wb_t, wout)


def _ret_bwd(qrot, krot, vb, orr, don, gnw, lgf, lgb):
    t = qrot.shape[0]
    c = RET_CHUNK
    nc = t // c
    hd = RET_HEAD_DIM

    def body(lgf_ref, lgb_ref, q_ref, k_ref, v_ref, o_ref, dn_ref, w_ref,
             dq_ref, dk_ref, dv_ref, dw_ref, dlf_ref, dlb_ref, qt, kt, dob, sfa, sba):
        h = pl.program_id(0)
        fw = _Dir(lgf_ref[h], False)
        bw = _Dir(lgb_ref[h], True)
        fw.dt, bw.dt = fw.d.T, bw.d.T

        o = o_ref[...]
        xc = o - jnp.mean(o, axis=-1, keepdims=True)
        r = lax.rsqrt(jnp.mean(xc * xc, axis=-1, keepdims=True) + EPS)
        xh = xc * r
        dn = dn_ref[...]
        gy = dn * w_ref[...]
        d_o = r * (gy - jnp.mean(gy, axis=-1, keepdims=True) - xh * jnp.mean(gy * xh, axis=-1, keepdims=True))
        dw_ref[...] = jnp.sum(dn * xh, axis=0, keepdims=True)
        dob[...] = d_o.astype(BF16)
        for i in range(nc):
            qt[i] = q_ref[i * c:(i + 1) * c, :].astype(F32).T.astype(BF16)
            kt[i] = k_ref[i * c:(i + 1) * c, :].astype(F32).T.astype(BF16)
        dq_ref[...] = jnp.zeros_like(dq_ref)
        dk_ref[...] = jnp.zeros_like(dk_ref)
        dv_ref[...] = jnp.zeros_like(dv_ref)

        def load(ci):
            sl = pl.ds(pl.multiple_of(ci * c, c), c)
            return sl, q_ref[sl, :], k_ref[sl, :], v_ref[sl, :], dob[sl, :]

        def pass1(ci, s, acc, p, s_all):
            sl, qq, kk, vv, do = load(ci)
            a = _dot(qq, kk, NT)
            bm = _dot(do, vv, NT)
            doq = (do.astype(F32) * p.qd).astype(BF16)
            sb = s.astype(BF16)
            dqc = _dot(doq, sb, NT)
            dq_ref[sl, :] += _dot((bm * p.d).astype(BF16), kk) + dqc
            s_all[ci] = sb
            acc = acc + p.dist * p.d * a * bm + p.wq * qq.astype(F32) * dqc
            s = s * p.cd + _dot((kt[ci].astype(F32) * p.kd_row).astype(BF16), vv)
            return s, acc

        def pass2(ci, g, acc, p, s_all):
            sl, qq, kk, vv, do = load(ci)
            at = _dot(kk, qq, NT)
            bt = _dot(vv, do, NT)
            gb = g.astype(BF16)
            kkd = (kk.astype(F32) * p.kd_col).astype(BF16)
            dv_ref[sl, :] += _dot((at * p.dt).astype(BF16), do) + _dot(kkd, gb)
            dk2 = _dot(vv, gb, NT) * p.kd_col
            dk_ref[sl, :] += _dot((bt * p.dt).astype(BF16), qq) + dk2
            acc = acc + p.wk * kk.astype(F32) * dk2 + (float(c) * p.cd) * g * s_all[ci].astype(F32)
            doq = (do.astype(F32) * p.qd).astype(BF16)
            g = g * p.cd + _dot(qt[ci], doq)
            return g, acc

        zero = jnp.zeros((hd, hd), F32)

        def step1(i, carry):
            sf, af, sb, ab = carry
            sf, af = pass1(i, sf, af, fw, sfa)
            sb, ab = pass1(nc - 1 - i, sb, ab, bw, sba)
            return sf, af, sb, ab

        _, af, _, ab = lax.fori_loop(0, nc, step1, (zero, zero, zero, zero))

        def step2(i, carry):
            gf, af, gb, ab = carry
            gf, af = pass2(nc - 1 - i, gf, af, fw, sfa)
            gb, ab = pass2(i, gb, ab, bw, sba)
            return gf, af, gb, ab

        _, af, _, ab = lax.fori_loop(0, nc, step2, (zero, af, zero, ab))
        tot = lambda m: jnp.sum(jnp.sum(m, axis=0, keepdims=True), axis=1, keepdims=True)
        dlf_ref[...] = jnp.broadcast_to(tot(af).reshape(1, 1, 1), (1, 8, 128))
        dlb_ref[...] = jnp.broadcast_to(tot(ab).reshape(1, 1, 1), (1, 8, 128))

    smem = pl.BlockSpec(memory_space=pltpu.SMEM)
    head = pl.BlockSpec((t, 128), lambda h: (0, h))
    vec = pl.BlockSpec((1, 128), lambda h: (0, h))
    scal = pl.BlockSpec((1, 8, 128), lambda h: (h, 0, 0))
    return pl.pallas_call(
        body, name="ret_bwd", grid=(RET_HEADS,),
        in_specs=[smem, smem, head, head, head, head, head, vec],
        out_specs=[head, head, head, vec, scal, scal],
        out_shape=[SDS((t, RET_WIDTH), F32)] * 3 + [SDS((1, RET_WIDTH), F32), SDS((RET_HEADS, 8, 128), F32),
                                                   SDS((RET_HEADS, 8, 128), F32)],
        scratch_shapes=[pltpu.VMEM((nc, hd, c), BF16), pltpu.VMEM((nc, hd, c), BF16), pltpu.VMEM((t, hd), BF16),
                        pltpu.VMEM((nc, hd, hd), BF16), pltpu.VMEM((nc, hd, hd), BF16)],
        compiler_params=_params(("parallel",)),
    )(lgf, lgb, qrot, krot, vb, orr, don, gnw)


def _ret_post_bwd(dq, dk, dv, cos, sin):
    t = dq.shape[0]
    tm = min(512, t)
    hd = RET_HEAD_DIM

    def body(dq_ref, dk_ref, dv_ref, c_ref, s_ref, oq_ref, ok_ref, ov_ref):
        cc = jnp.concatenate([c_ref[...]] * 4, axis=-1)
        ss = jnp.concatenate([s_ref[...]] * 4, axis=-1)
        oq_ref[...] = _rope_bwd(dq_ref[...], cc, ss, hd // 4).astype(BF16)
        ok_ref[...] = (_rope_bwd(dk_ref[...], cc, ss, hd // 4) * (hd ** -0.5)).astype(BF16)
        ov_ref[...] = dv_ref[...].astype(BF16)

    blk = pl.BlockSpec((tm, 512), lambda i: (i, 0))
    tab = pl.BlockSpec((tm, 128), lambda i: (i, 0))
    return pl.pallas_call(
        body, name="ret_post_bwd", grid=(t // tm,),
        in_specs=[blk, blk, blk, tab, tab], out_specs=[blk, blk, blk],
        out_shape=[SDS((t, 512), BF16)] * 3,
        compiler_params=_params(("parallel",)),
    )(dq, dk, dv, cos, sin)


def _attn_bwd(q, k, v, doa):
    t = q.shape[1]
    tq = min(256, t)
    nq = t // tq
    scale = ATTN_HEAD_DIM ** -0.5
    hd = ATTN_HEAD_DIM

    def body(q_ref, k_ref, v_ref, do_ref, dq_ref, dk_ref, dv_ref):
        p, i = pl.program_id(0), pl.program_id(1)

        @pl.when(jnp.logical_and(p % 2 == 0, i == 0))
        def _():
            dk_ref[...] = jnp.zeros_like(dk_ref)
            dv_ref[...] = jnp.zeros_like(dv_ref)

        kk, vv = k_ref[0], v_ref[0]
        dov = do_ref[...]
        outs = []
        for j in range(2):
            qq = q_ref[j]
            do = dov[:, j * hd:(j + 1) * hd].astype(BF16)
            st = _dot(kk, qq, NT) * scale
            e = jnp.exp(st - jnp.max(st, axis=0, keepdims=True))
            pt = e / jnp.sum(e, axis=0, keepdims=True)
            dpt = _dot(vv, do, NT)
            dr = jnp.sum(pt * dpt, axis=0, keepdims=True)
            dst = (pt * (dpt - dr) * scale).astype(BF16)
            dv_ref[0] += _dot(pt.astype(BF16), do)
            dk_ref[0] += _dot(dst, qq)
            outs.append(_dot(dst, kk, TN))
        dq_ref[...] = jnp.concatenate(outs, axis=-1)

    kv = pl.BlockSpec((1, t, hd), lambda p, i: (p // 2, 0, 0))
    return pl.pallas_call(
        body, name="attn_bwd", grid=(4, nq),
        in_specs=[pl.BlockSpec((2, tq, hd), lambda p, i: (p, i, 0)), kv, kv,
                  pl.BlockSpec((tq, 128), lambda p, i: (i, p))],
        out_specs=[pl.BlockSpec((tq, 128), lambda p, i: (i, p)), kv, kv],
        out_shape=[SDS((t, ATTN_WIDTH), F32), SDS((ATTN_KV_HEADS, t, hd), F32), SDS((ATTN_KV_HEADS, t, hd), F32)],
        compiler_params=_params(("arbitrary", "arbitrary")),
    )(q, k, v, doa)


def _attn_post_bwd(dq, dk, dv, z, qn, kn, cos, sin, ones_bd):
    t = z.shape[0]
    tm = min(512, t)
    n = t // tm
    hd = ATTN_HEAD_DIM

    def body(dq_ref, dk_ref, dv_ref, zq_ref, zkv_ref, qn_ref, kn_ref, c_ref, s_ref, b_ref,
             dz_ref, dqn_ref, dkn_ref, acc_q, acc_k):
        i = pl.program_id(0)

        @pl.when(i == 0)
        def _():
            acc_q[...] = jnp.zeros_like(acc_q)
            acc_k[...] = jnp.zeros_like(acc_k)

        bd = b_ref[...]
        c2, s2 = c_ref[...], s_ref[...]

        def norm_bwd(dy, x, w, ones, cos_t, sin_t, acc):
            dyr = _rope_bwd(dy, cos_t, sin_t, hd // 4)
            r = lax.rsqrt(_group_mean(x * x, ones) + EPS)
            xh = x * r
            gy = dyr * w
            acc[...] += jnp.sum((dyr * xh).reshape(tm // 8, 8, x.shape[-1]), axis=0)
            return r * (gy - xh * _group_mean(gy * xh, ones))

        cq = jnp.concatenate([c2] * 4, axis=-1)
        sq = jnp.concatenate([s2] * 4, axis=-1)
        dz_ref[:, :512] = norm_bwd(dq_ref[...], zq_ref[...], qn_ref[...], bd, cq, sq, acc_q).astype(BF16)
        zkv = zkv_ref[...]
        dkk = jnp.concatenate([dk_ref[0], dk_ref[1]], axis=-1)
        dz_ref[:, 512:640] = norm_bwd(dkk, zkv[:, :128], kn_ref[...], bd[:128, :128], c2, s2, acc_k).astype(BF16)
        dz_ref[:, 640:768] = jnp.concatenate([dv_ref[0], dv_ref[1]], axis=-1).astype(BF16)

        @pl.when(i == n - 1)
        def _():
            dqn_ref[...] = jnp.sum(acc_q[...], axis=0, keepdims=True)
            dkn_ref[...] = jnp.sum(acc_k[...], axis=0, keepdims=True)

    kv_blk = SEG["ka"][2] // 256
    kvs = pl.BlockSpec((ATTN_KV_HEADS, tm, hd), lambda i: (0, i, 0))
    const = lambda shape: pl.BlockSpec(shape, lambda i: (0, 0))
    return pl.pallas_call(
        body, name="attn_post_bwd", grid=(n,),
        in_specs=[pl.BlockSpec((tm, 512), lambda i: (i, 0)), kvs, kvs,
                  pl.BlockSpec((tm, 512), lambda i: (i, 0)), pl.BlockSpec((tm, 256), lambda i: (i, kv_blk)),
                  const((1, 512)), const((1, 128)),
                  pl.BlockSpec((tm, 128), lambda i: (i, 0)), pl.BlockSpec((tm, 128), lambda i: (i, 0)),
                  const((512, 512))],
        out_specs=[pl.BlockSpec((tm, 768), lambda i: (i, 0)), const((1, 512)), const((1, 128))],
        out_shape=[SDS((t, 768), BF16), SDS((1, 512), F32), SDS((1, 128), F32)],
        scratch_shapes=[pltpu.VMEM((8, 512), F32), pltpu.VMEM((8, 128), F32)],
        compiler_params=_params(("arbitrary",)),
    )(dq, dk, dv, z, z, qn, kn, cos, sin, ones_bd)


def _in_bwd(dxo, x, g, w_t, dz_a, dz_m, dqr, dkr, dvr):
    t, d = x.shape
    tm = min(256, t)
    n = t // tm
    parts = [(0, 0, 768, 0), (1, 0, 512, SEG["ga"][0]), (2, 0, 512, SEG["qr"][0]), (3, 0, 512, SEG["kr"][0]),
             (4, 0, 512, SEG["vr"][0]), (1, 512, 2560, SEG["gr"][0])]

    def body(dx_ref, x_ref, g_ref, w_ref, a_ref, m_ref, q_ref, k_ref, v_ref, o_ref, dg_ref, acc):
        i = pl.program_id(0)

        @pl.when(i == 0)
        def _():
            acc[...] = jnp.zeros_like(acc)

        pieces = [a_ref, m_ref, q_ref, k_ref, v_ref]
        dh = jnp.zeros((tm, d), F32)
        for pi, lo, w, row in parts:
            dh = dh + _dot(pieces[pi][:, lo:lo + w], w_ref[row:row + w, :])
        xv = x_ref[...]
        r = lax.rsqrt(jnp.mean(xv * xv, axis=-1, keepdims=True) + EPS)
        xh = xv * r
        gy = dh * g_ref[...]
        o_ref[...] = dx_ref[...] + r * (gy - xh * jnp.mean(gy * xh, axis=-1, keepdims=True))
        acc[...] += jnp.sum((dh * xh).reshape(tm // 8, 8, d), axis=0)

        @pl.when(i == n - 1)
        def _():
            dg_ref[...] = jnp.sum(acc[...], axis=0, keepdims=True)

    row = lambda w: pl.BlockSpec((tm, w), lambda i: (i, 0))
    const = lambda shape: pl.BlockSpec(shape, lambda i: (0, 0))
    return pl.pallas_call(
        body, name="in_bwd", grid=(n,),
        in_specs=[row(d), row(d), const((1, d)), const((D_IN, d)), row(768), row(3072), row(512), row(512),
                  row(512)],
        out_specs=[row(d), const((1, d))],
        out_shape=[SDS((t, d), F32), SDS((1, d), F32)],
        scratch_shapes=[pltpu.VMEM((8, d), F32)],
        compiler_params=_params(("arbitrary",)),
    )(dxo, x, g, w_t, dz_a, dz_m, dqr, dkr, dvr)


def _dw_in(h_t, piece, col0, width, row0, buf):
    d, t = h_t.shape
    tn = 256
    c0, r0 = col0 // tn, row0 // tn

    def body(*refs):
        h_ref, p_ref, o_ref = refs[0], refs[1], refs[-1]
        o_ref[...] = _dot(h_ref[...], p_ref[...]).T.astype(BF16)

    in_specs = [pl.BlockSpec((d, t), lambda j: (0, 0)), pl.BlockSpec((t, tn), lambda j: (0, c0 + j))]
    args = [h_t, piece]
    aliases = {}
    if buf is not None:
        in_specs.append(ANY)
        args.append(buf)
        aliases = {2: 0}
    return pl.pallas_call(
        body, name="dw_in", grid=(width // tn,),
        in_specs=in_specs, out_specs=pl.BlockSpec((tn, d), lambda j: (r0 + j, 0)),
        out_shape=SDS((D_IN, d), BF16), input_output_aliases=aliases,
        compiler_params=_params(("parallel",)),
    )(*args)


def _adamw(w, g, m, v):
    rows, cols = w.shape
    tr = 256 if rows % 256 == 0 else rows

    def body(w_ref, g_ref, m_ref, v_ref, d_ref, mo_ref, vo_ref):
        gv = g_ref[...]
        mn = ADAM_B1 * m_ref[...] + (1.0 - ADAM_B1) * gv
        vn = ADAM_B2 * v_ref[...] + (1.0 - ADAM_B2) * (gv * gv)
        m_hat = mn / (1.0 - ADAM_B1 ** ADAM_STEP)
        v_hat = vn / (1.0 - ADAM_B2 ** ADAM_STEP)
        d_ref[...] = -ADAM_LR * (m_hat / (jnp.sqrt(v_hat) + ADAM_EPS) + ADAM_WD * w_ref[...])
        mo_ref[...] = mn
        vo_ref[...] = vn

    blk = pl.BlockSpec((tr, cols), lambda i: (i, 0))
    return pl.pallas_call(
        body, name="adamw", grid=(rows // tr,),
        in_specs=[blk] * 4, out_specs=[blk] * 3, out_shape=[SDS((rows, cols), F32)] * 3,
        compiler_params=_params(("parallel",)),
    )(w, g, m, v)


def _me():
    return lax.axis_index("x"), lax.axis_index("y"), lax.axis_index("c")


def _flip(k):
    x, y, c = _me()
    px = 1 - x if k & 4 else x
    py = 1 - y if k & 2 else y
    pc = 1 - c if k & 1 else c
    return (px, py, pc), 4 * px + 2 * py + pc


def _all_gather(shards):
    na = len(shards)
    chips = (4, 2, 6)

    def body(*refs):
        ins, outs = refs[:na], refs[na:2 * na]
        send_sems, recv_sems, local_sems = refs[2 * na:]
        _, mine = _flip(0)

        def rows(a, idx):
            r = shards[a].shape[0]
            return outs[a].at[pl.ds(pl.multiple_of(idx * r, 16), r), :]

        def copy(a, slot, block_idx, to, src=None):
            return pltpu.make_async_remote_copy(
                src_ref=rows(a, block_idx) if src is None else src, dst_ref=rows(a, block_idx),
                send_sem=send_sems.at[a, slot], recv_sem=recv_sems.at[a, slot],
                device_id=to, device_id_type=MESH_ID)

        sibling, sibling_idx = _flip(1)
        local, started = [], []
        for a in range(na):
            cp = pltpu.make_async_copy(ins[a], rows(a, mine), local_sems.at[a])
            cp.start()
            local.append(cp)
            first = [copy(a, 0, mine, sibling, src=ins[a])]
            first += [copy(a, 1 + j, mine, _flip(k)[0], src=ins[a]) for j, k in enumerate(chips)]
            for cp in first:
                cp.start()
            started += first
        for a in range(na):
            for j, k in enumerate(chips):
                _, theirs = _flip(k)
                copy(a, 1 + j, theirs, _flip(0)[0]).wait_recv()
                fwd = copy(a, 4 + j, theirs, sibling)
                fwd.start()
                started.append(fwd)
        for a in range(na):
            copy(a, 0, sibling_idx, _flip(0)[0]).wait_recv()
            for j, k in enumerate(chips):
                _, theirs = _flip(k | 1)
                copy(a, 4 + j, theirs, _flip(0)[0]).wait_recv()
        for cp in started:
            cp.wait_send()
        for cp in local:
            cp.wait()

    return pl.pallas_call(
        body, name="all_gather_weights",
        in_specs=[ANY] * na, out_specs=[ANY] * na,
        out_shape=[SDS((N_DEV * s.shape[0], s.shape[1]), s.dtype) for s in shards],
        scratch_shapes=[pltpu.SemaphoreType.DMA((na, 7)), pltpu.SemaphoreType.DMA((na, 7)),
                        pltpu.SemaphoreType.DMA((na,))],
        compiler_params=pltpu.CompilerParams(has_side_effects=True),
    )(*shards)


def _scatter_blocks(grads):
    na = len(grads)

    def body(*refs):
        ins, outs = refs[:na], refs[na:2 * na]
        send_sems, recv_sems, local_sems = refs[2 * na:]
        me, mine = _flip(0)
        pending = []
        for a in range(na):
            r = grads[a].shape[0] // N_DEV

            def block(idx, a=a, r=r):
                return ins[a].at[pl.ds(pl.multiple_of(idx * r, 16), r), :]

            cp = pltpu.make_async_copy(block(mine), outs[a].at[mine], local_sems.at[a])
            cp.start()
            pending.append(cp)
            for k in range(1, N_DEV):
                peer, theirs = _flip(k)
                cp = pltpu.make_async_remote_copy(
                    src_ref=block(theirs), dst_ref=outs[a].at[mine],
                    send_sem=send_sems.at[a, k - 1], recv_sem=recv_sems.at[a, k - 1],
                    device_id=peer, device_id_type=MESH_ID)
                cp.start()
                pending.append((cp, a, k))
        for item in pending:
            if isinstance(item, tuple):
                cp, a, k = item
                cp.wait_send()
                _, theirs = _flip(k)
                pltpu.make_async_remote_copy(
                    src_ref=outs[a].at[theirs], dst_ref=outs[a].at[theirs],
                    send_sem=send_sems.at[a, k - 1], recv_sem=recv_sems.at[a, k - 1],
                    device_id=me, device_id_type=MESH_ID).wait_recv()
            else:
                item.wait()

    return pl.pallas_call(
        body, name="scatter_grads",
        in_specs=[ANY] * na, out_specs=[ANY] * na,
        out_shape=[SDS((N_DEV, g.shape[0] // N_DEV, g.shape[1]), g.dtype) for g in grads],
        scratch_shapes=[pltpu.SemaphoreType.DMA((na, 7)), pltpu.SemaphoreType.DMA((na, 7)),
                        pltpu.SemaphoreType.DMA((na,))],
        compiler_params=pltpu.CompilerParams(has_side_effects=True),
    )(*grads)


def _sum_slots(recv):
    _, r, w = recv.shape
    tr = 128 if r % 128 == 0 else r

    def body(r_ref, o_ref):
        acc = r_ref[0].astype(F32)
        for s in range(1, N_DEV):
            acc = acc + r_ref[s].astype(F32)
        o_ref[...] = acc

    return pl.pallas_call(
        body, name="sum_slots", grid=(r // tr,),
        in_specs=[pl.BlockSpec((N_DEV, tr, w), lambda i: (0, i, 0))],
        out_specs=pl.BlockSpec((tr, w), lambda i: (i, 0)),
        out_shape=SDS((r, w), F32),
        compiler_params=_params(("parallel",)),
    )(recv)


def _all_reduce_small(packed):
    shape = packed.shape

    def body(p_ref, o_ref, slots, send_sems, recv_sems):
        me, mine = _flip(0)
        slots[mine] = p_ref[...]
        sends = []
        for k in range(1, N_DEV):
            peer, _ = _flip(k)
            cp = pltpu.make_async_remote_copy(
                src_ref=p_ref, dst_ref=slots.at[mine], send_sem=send_sems.at[k - 1], recv_sem=recv_sems.at[k - 1],
                device_id=peer, device_id_type=MESH_ID)
            cp.start()
            sends.append(cp)
        for k in range(1, N_DEV):
            _, theirs = _flip(k)
            pltpu.make_async_remote_copy(
                src_ref=p_ref, dst_ref=slots.at[theirs], send_sem=send_sems.at[k - 1],
                recv_sem=recv_sems.at[k - 1], device_id=me, device_id_type=MESH_ID).wait_recv()
        for cp in sends:
            cp.wait_send()
        acc = slots[0]
        for s in range(1, N_DEV):
            acc = acc + slots[s]
        o_ref[...] = acc

    vm = pl.BlockSpec(memory_space=pltpu.VMEM)
    return pl.pallas_call(
        body, name="all_reduce_small", in_specs=[vm], out_specs=vm, out_shape=SDS(shape, F32),
        scratch_shapes=[pltpu.VMEM((N_DEV,) + shape, F32), pltpu.SemaphoreType.DMA((7,)),
                        pltpu.SemaphoreType.DMA((7,))],
        compiler_params=pltpu.CompilerParams(has_side_effects=True),
    )(packed)


def _layer_fwd(x, p, tabs):
    z, h_t = _in_proj(x, p["norm_g"], p["w_in_t"])
    q, k, v = _attn_prep(z, p["qn"], p["kn"], tabs["ca"], tabs["sa"], tabs["ones"])
    oa = _attn_fwd(q, k, v)
    qrot, krot, vb, orr, on = _ret_fwd(z, p["lgf"], p["lgb"], p["gnw"], tabs["cr"], tabs["sr"])
    xn, ya, yb = _merge_fwd(x, z, oa, on, p["wb_t"], p["w_out"])
    saved = dict(x=x, z=z, h_t=h_t, q=q, k=k, v=v, oa=oa, qrot=qrot, krot=krot, vb=vb, orr=orr, on=on, ya=ya, yb=yb)
    return xn, saved


def _layer_bwd(dxo, s, p, tabs):
    doa, don, dz_m, d_wout, d_wb_t = _merge_bwd(dxo, s["z"], s["oa"], s["on"], s["ya"], s["yb"], p["wb_t"], p["w_out"])
    dq_r, dk_r, dv_r, d_gnw, d_lgf, d_lgb = _ret_bwd(s["qrot"], s["krot"], s["vb"], s["orr"], don, p["gnw"],
                                                     p["lgf"], p["lgb"])
    dqr, dkr, dvr = _ret_post_bwd(dq_r, dk_r, dv_r, tabs["cr"], tabs["sr"])
    dq_a, dk_a, dv_a = _attn_bwd(s["q"], s["k"], s["v"], doa)
    dz_a, d_qn, d_kn = _attn_post_bwd(dq_a, dk_a, dv_a, s["z"], p["qn"], p["kn"], tabs["ca"], tabs["sa"],
                                      tabs["ones"])
    dx, d_norm_g = _in_bwd(dxo, s["x"], p["norm_g"], p["w_in_t"], dz_a, dz_m, dqr, dkr, dvr)
    buf = None
    for piece, col0, width, row0 in [(dz_a, 0, 768, 0), (dz_m, 0, 512, SEG["ga"][0]), (dqr, 0, 512, SEG["qr"][0]),
                                     (dkr, 0, 512, SEG["kr"][0]), (dvr, 0, 512, SEG["vr"][0]),
                                     (dz_m, 512, 2560, SEG["gr"][0])]:
        buf = _dw_in(s["h_t"], piece, col0, width, row0, buf)
    grads = dict(w_in_t=buf, wb_t=d_wb_t, w_out=d_wout, norm_g=d_norm_g, gnw=d_gnw,
                 qn=d_qn.reshape(ATTN_Q_HEADS, ATTN_HEAD_DIM).sum(axis=0),
                 kn=d_kn.reshape(ATTN_KV_HEADS, ATTN_HEAD_DIM).sum(axis=0),
                 lgf=d_lgf[:, 0, 0], lgb=d_lgb[:, 0, 0])
    return dx, grads


def _adamw_nd(w, g, m, v):
    shape = w.shape
    two_d = (1, shape[0]) if w.ndim == 1 else (-1, shape[-1])
    out = _adamw(w.reshape(two_d), g.reshape(two_d), m.reshape(two_d), v.reshape(two_d))
    return tuple(o.reshape(shape) for o in out)


def kernel(x, norm_g, w_in, attn_q_norm, attn_k_norm, ret_decay_fwd, ret_decay_bwd, ret_gn_w, w_branch_attn, w_branch_ret, w_out, final_norm_g, loss_target, m_norm_g, m_w_in, m_attn_q_norm, m_attn_k_norm, m_ret_decay_fwd, m_ret_decay_bwd, m_ret_gn_w, m_w_branch_attn, m_w_branch_ret, m_w_out, m_final_norm_g, v_norm_g, v_w_in, v_attn_q_norm, v_attn_k_norm, v_ret_decay_fwd, v_ret_decay_bwd, v_ret_gn_w, v_w_branch_attn, v_w_branch_ret, v_w_out, v_final_norm_g):
    t, d = x.shape[1], x.shape[2]
    x2, target = x[0], loss_target[0]

    shards = []
    for l in range(DEPTH):
        shards.append(jnp.swapaxes(w_in[l], 0, 1).astype(BF16))
        shards.append(jnp.concatenate([w_branch_attn[l].T, w_branch_ret[l].T], axis=1).astype(BF16))
        shards.append(w_out[l].astype(BF16))
    full = _all_gather(shards)

    ca, sa = _rope_tables(t, ATTN_HEAD_DIM)
    cr, sr = _rope_tables(t, RET_HEAD_DIM)
    grp = jnp.arange(ATTN_WIDTH) // ATTN_HEAD_DIM
    tabs = dict(ca=jnp.tile(ca, (1, 2)), sa=jnp.tile(sa, (1, 2)), cr=cr, sr=sr,
                ones=jnp.where(grp[:, None] == grp[None, :], 1.0 / ATTN_HEAD_DIM, 0.0).astype(BF16))
    layers = []
    for l in range(DEPTH):
        layers.append(dict(
            w_in_t=full[3 * l], wb_t=full[3 * l + 1], w_out=full[3 * l + 2],
            norm_g=norm_g[l][None], qn=jnp.tile(attn_q_norm[l], ATTN_Q_HEADS)[None],
            kn=jnp.tile(attn_k_norm[l], ATTN_KV_HEADS)[None], gnw=ret_gn_w[l][None],
            lgf=jax.nn.log_sigmoid(ret_decay_fwd[l]), lgb=jax.nn.log_sigmoid(ret_decay_bwd[l])))

    h = x2
    saved = []
    for l in range(DEPTH):
        h, s = _layer_fwd(h, layers[l], tabs)
        saved.append(s)
    dx, d_final_g, loss_part = _final_loss(h, final_norm_g[None], target)
    grads = [None] * DEPTH
    for l in reversed(range(DEPTH)):
        dx, grads[l] = _layer_bwd(dx, saved[l], layers[l], tabs)

    recv = _scatter_blocks([grads[l][name] for l in range(DEPTH) for name in ("w_in_t", "wb_t", "w_out")])
    summed = [_sum_slots(r) for r in recv]
    g_w_in = jnp.stack([summed[3 * l].T for l in range(DEPTH)])
    g_wba = jnp.stack([summed[3 * l + 1][:, :512].T for l in range(DEPTH)])
    g_wbr = jnp.stack([summed[3 * l + 1][:, 512:].T for l in range(DEPTH)])
    g_wout = jnp.stack([summed[3 * l + 2] for l in range(DEPTH)])

    packed = jnp.zeros((8, 1024), F32)
    for l in range(DEPTH):
        gl = grads[l]
        packed = packed.at[l].set(gl["norm_g"][0])
        packed = packed.at[2, 512 * l:512 * (l + 1)].set(gl["gnw"][0])
        packed = packed.at[4, 128 * l:128 * l + 64].set(gl["qn"])
        packed = packed.at[4, 256 + 128 * l:256 + 128 * l + 64].set(gl["kn"])
        packed = packed.at[4, 512 + 128 * l:512 + 128 * l + 4].set(gl["lgf"])
        packed = packed.at[4, 768 + 128 * l:768 + 128 * l + 4].set(gl["lgb"])
    packed = packed.at[3].set(d_final_g[0])
    packed = packed.at[5, 0].set(loss_part[0, 0])
    red = _all_reduce_small(packed)
    loss = red[5, 0]
    g_norm_g = red[0:2]
    g_gnw = red[2].reshape(DEPTH, RET_WIDTH)
    g_final = red[3]
    g_qn = jnp.stack([red[4, 128 * l:128 * l + 64] for l in range(DEPTH)])
    g_kn = jnp.stack([red[4, 256 + 128 * l:256 + 128 * l + 64] for l in range(DEPTH)])
    g_lgf = jnp.stack([red[4, 512 + 128 * l:512 + 128 * l + 4] for l in range(DEPTH)])
    g_lgb = jnp.stack([red[4, 768 + 128 * l:768 + 128 * l + 4] for l in range(DEPTH)])
    g_df = g_lgf * jax.nn.sigmoid(-ret_decay_fwd)
    g_db = g_lgb * jax.nn.sigmoid(-ret_decay_bwd)

    grad_w = [g_norm_g, g_w_in, g_qn, g_kn, g_df, g_db, g_gnw, g_wba, g_wbr, g_wout, g_final]
    weights = [norm_g, w_in, attn_q_norm, attn_k_norm, ret_decay_fwd, ret_decay_bwd, ret_gn_w, w_branch_attn,
               w_branch_ret, w_out, final_norm_g]
    ms = [m_norm_g, m_w_in, m_attn_q_norm, m_attn_k_norm, m_ret_decay_fwd, m_ret_decay_bwd, m_ret_gn_w,
          m_w_branch_attn, m_w_branch_ret, m_w_out, m_final_norm_g]
    vs = [v_norm_g, v_w_in, v_attn_q_norm, v_attn_k_norm, v_ret_decay_fwd, v_ret_decay_bwd, v_ret_gn_w,
          v_w_branch_attn, v_w_branch_ret, v_w_out, v_final_norm_g]
    upd = [_adamw_nd(w, g, m, v) for w, g, m, v in zip(weights, grad_w, ms, vs)]
    return (loss, dx[None], *grad_w, *[u[0] for u in upd], *[u[1] for u in upd], *[u[2] for u in upd])
```

```python
import functools

import jax
import jax.numpy as jnp
from jax import lax
from jax.experimental import pallas as pl
from jax.experimental.pallas import tpu as pltpu

F32 = jnp.float32
BF16 = jnp.bfloat16
SDS = jax.ShapeDtypeStruct

D_MODEL = 1024
DEPTH = 2
GRID_W = 64
ATTN_Q_HEADS = 8
ATTN_KV_HEADS = 2
ATTN_HEAD_DIM = 64
ATTN_WIDTH = 512
ATTN_KV_WIDTH = 128
RET_HEADS = 4
RET_HEAD_DIM = 128
RET_WIDTH = 512
RET_CHUNK = 128
ATTN_KEY_CHUNK = 512
ROPE_THETA = 10000.0
EPS = 1e-6
D_IN = 5376
N_DEV = 8

ADAM_LR = 0.001
ADAM_B1 = 0.9
ADAM_B2 = 0.999
ADAM_EPS = 1e-08
ADAM_WD = 0.01
ADAM_STEP = 10

SEG = {
    "qa": (0, 512, 0),
    "ga": (768, 512, 512),
    "qr": (1280, 512, 1024),
    "kr": (1792, 512, 1536),
    "vr": (2304, 512, 2048),
    "gr": (2816, 512, 2560),
    "gm": (3328, 2048, 3072),
    "ka": (512, 128, 5120),
    "va": (640, 128, 5248),
}

VMEM_LIMIT = 60 * 1024 * 1024
NT = (((1,), (1,)), ((), ()))
TN = (((0,), (0,)), ((), ()))
MESH_ID = pl.DeviceIdType.MESH
ANY = pl.BlockSpec(memory_space=pl.ANY)


def _params(sem=None, vmem=VMEM_LIMIT):
    return pltpu.CompilerParams(dimension_semantics=sem, vmem_limit_bytes=vmem)


def _dot(a, b, dims=None):
    if dims is None:
        return jnp.dot(a, b, preferred_element_type=F32)
    return lax.dot_general(a, b, dims, preferred_element_type=F32)


def _sigmoid(x):
    return 1.0 / (1.0 + jnp.exp(-x))


def _swap_halves(x, q):
    n = x.shape[-1]
    axis = x.ndim - 1
    lane = lax.broadcasted_iota(jnp.int32, x.shape, axis)
    first = (lane % (2 * q)) < q
    return jnp.where(first, pltpu.roll(x, n - q, axis), pltpu.roll(x, q, axis))


def _rope(x, cos, sin_signed, q):
    return x * cos + _swap_halves(x, q) * sin_signed


def _rope_bwd(dy, cos, sin_signed, q):
    return dy * cos - _swap_halves(dy, q) * sin_signed


def _group_mean(v, ones_bd):
    hi = v.astype(BF16)
    r1 = v - hi.astype(F32)
    mid = r1.astype(BF16)
    lo = (r1 - mid.astype(F32)).astype(BF16)
    return _dot(hi, ones_bd) + _dot(mid, ones_bd) + _dot(lo, ones_bd)


def _rope_tables(t, head_dim):
    n_rows = t // GRID_W
    row = jnp.repeat(jnp.arange(n_rows, dtype=F32), GRID_W)
    col = jnp.tile(jnp.arange(GRID_W, dtype=F32), n_rows)
    d_axis = head_dim // 2
    inv_freq = ROPE_THETA ** (-jnp.arange(0, d_axis, 2, dtype=F32) / d_axis)
    ar = row[:, None] * inv_freq
    ac = col[:, None] * inv_freq
    cr, sr, cc, sc = jnp.cos(ar), jnp.sin(ar), jnp.cos(ac), jnp.sin(ac)
    return jnp.concatenate([cr, cr, cc, cc], axis=-1), jnp.concatenate([-sr, sr, -sc, sc], axis=-1)


def _in_proj(x, g, w_t):
    t, d = x.shape
    tm = min(256, t)

    def body(x_ref, g_ref, w_ref, z_ref, ht_ref):
        xv = x_ref[...]
        r = lax.rsqrt(jnp.mean(xv * xv, axis=-1, keepdims=True) + EPS)
        h = xv * r * g_ref[...]
        ht_ref[...] = h.T.astype(BF16)
        hb = h.astype(BF16)
        for nat, w, off in SEG.values():
            z_ref[:, off:off + w] = _dot(hb, w_ref[nat:nat + w, :], NT)

    return pl.pallas_call(
        body, name="in_proj", grid=(t // tm,),
        in_specs=[pl.BlockSpec((tm, d), lambda i: (i, 0)), pl.BlockSpec((1, d), lambda i: (0, 0)),
                  pl.BlockSpec((D_IN, d), lambda i: (0, 0))],
        out_specs=[pl.BlockSpec((tm, D_IN), lambda i: (i, 0)), pl.BlockSpec((d, tm), lambda i: (0, i))],
        out_shape=[SDS((t, D_IN), F32), SDS((d, t), BF16)],
        compiler_params=_params(("parallel",)),
    )(x, g, w_t)


def _attn_prep(z, qn, kn, cos, sin, ones_bd):
    t = z.shape[0]
    tm = min(ATTN_KEY_CHUNK, t)
    hd = ATTN_HEAD_DIM

    def body(zq_ref, zkv_ref, qn_ref, kn_ref, c_ref, s_ref, b_ref, q_out, k_out, v_out, kt_out, vt_out):
        bd = b_ref[...]
        c2, s2 = c_ref[...], s_ref[...]
        cq = jnp.concatenate([c2] * 4, axis=-1)
        sq = jnp.concatenate([s2] * 4, axis=-1)
        xq = zq_ref[...]
        yq = xq * lax.rsqrt(_group_mean(xq * xq, bd) + EPS) * qn_ref[...]
        yq = _rope(yq, cq, sq, hd // 4) * (hd ** -0.5)
        for h in range(ATTN_Q_HEADS):
            q_out[h] = yq[:, h * hd:(h + 1) * hd].astype(BF16)
        zkv = zkv_ref[...]
        xk, xv = zkv[:, :ATTN_KV_WIDTH], zkv[:, ATTN_KV_WIDTH:]
        yk = xk * lax.rsqrt(_group_mean(xk * xk, bd[:ATTN_KV_WIDTH, :ATTN_KV_WIDTH]) + EPS) * kn_ref[...]
        yk = _rope(yk, c2, s2, hd // 4)
        ykt, xvt = yk.T, xv.T
        ones = jnp.ones((hd, tm), F32)
        for h in range(ATTN_KV_HEADS):
            k_out[h] = yk[:, h * hd:(h + 1) * hd].astype(BF16)
            v_out[h] = xv[:, h * hd:(h + 1) * hd].astype(BF16)
            kt_out[h, 0] = ykt[h * hd:(h + 1) * hd, :].astype(BF16)
            vt_out[h, 0] = jnp.concatenate([xvt[h * hd:(h + 1) * hd, :], ones], axis=0).astype(BF16)

    kv_blk = SEG["ka"][2] // 256
    nk = t // tm
    return pl.pallas_call(
        body, name="attn_prep", grid=(nk,),
        in_specs=[pl.BlockSpec((tm, 512), lambda i: (i, 0)), pl.BlockSpec((tm, 256), lambda i: (i, kv_blk)),
                  pl.BlockSpec((1, 512), lambda i: (0, 0)), pl.BlockSpec((1, 128), lambda i: (0, 0)),
                  pl.BlockSpec((tm, 128), lambda i: (i, 0)), pl.BlockSpec((tm, 128), lambda i: (i, 0)),
                  pl.BlockSpec((512, 512), lambda i: (0, 0))],
        out_specs=[pl.BlockSpec((ATTN_Q_HEADS, tm, hd), lambda i: (0, i, 0)),
                   pl.BlockSpec((ATTN_KV_HEADS, tm, hd), lambda i: (0, i, 0)),
                   pl.BlockSpec((ATTN_KV_HEADS, tm, hd), lambda i: (0, i, 0)),
                   pl.BlockSpec((ATTN_KV_HEADS, 1, hd, tm), lambda i: (0, i, 0, 0)),
                   pl.BlockSpec((ATTN_KV_HEADS, 1, 2 * hd, tm), lambda i: (0, i, 0, 0))],
        out_shape=[SDS((ATTN_Q_HEADS, t, hd), BF16), SDS((ATTN_KV_HEADS, t, hd), BF16),
                   SDS((ATTN_KV_HEADS, t, hd), BF16), SDS((ATTN_KV_HEADS, nk, hd, tm), BF16),
                   SDS((ATTN_KV_HEADS, nk, 2 * hd, tm), BF16)],
        compiler_params=_params(("parallel",)),
    )(z, z, qn, kn, cos, sin, ones_bd)


def _attn_fwd(q, k, vt):
    t = q.shape[1]
    tq = min(256, t)
    nk, tk = vt.shape[1], vt.shape[3]
    hd = ATTN_HEAD_DIM
    g = ATTN_Q_HEADS // ATTN_KV_HEADS

    def body(q_ref, k_ref, vt_ref, o_ref, lse_ref, s_scr):
        def pass_a(h, c, m8):
            half = tk // 2
            for lo in (c * tk, c * tk + half):
                st = _dot(k_ref[0, lo:lo + half, :], q_ref[h], NT)
                s_scr[h % 2, lo:lo + half, :] = st
                m8 = jnp.maximum(m8, jnp.max(st.reshape(half // 8, 8, tq), axis=0))
            return m8

        def pass_b(h, c, m, acc):
            e = jnp.exp(s_scr[h % 2, c * tk:(c + 1) * tk, :] - m).astype(BF16)
            return acc + _dot(vt_ref[0, c], e)

        neg = jnp.full((8, tq), -jnp.inf, F32)
        m8 = neg
        for c in range(nk):
            m8 = pass_a(0, c, m8)
        outs = []
        for h in range(g):
            m = jnp.max(m8, axis=0, keepdims=True)
            acc = jnp.zeros((2 * hd, tq), F32)
            m8 = neg
            for c in range(nk):
                if h + 1 < g:
                    m8 = pass_a(h + 1, c, m8)
                acc = pass_b(h, c, m, acc)
            l = acc[hd:hd + 1, :]
            outs.append((acc[:hd, :] / l).T)
            lse_ref[h] = m + jnp.log(l)
        o_ref[...] = jnp.concatenate(outs, axis=-1)

    return pl.pallas_call(
        body, name="attn_fwd", grid=(ATTN_KV_HEADS, t // tq),
        in_specs=[pl.BlockSpec((g, tq, hd), lambda p, i: (p, i, 0)),
                  pl.BlockSpec((1, t, hd), lambda p, i: (p, 0, 0)),
                  pl.BlockSpec((1, nk, 2 * hd, tk), lambda p, i: (p, 0, 0, 0))],
        out_specs=[pl.BlockSpec((tq, g * hd), lambda p, i: (i, p)), pl.BlockSpec((g, 1, tq), lambda p, i: (p, 0, i))],
        out_shape=[SDS((t, ATTN_WIDTH), F32), SDS((ATTN_Q_HEADS, 1, t), F32)],
        scratch_shapes=[pltpu.VMEM((2, t, tq), F32)],
        compiler_params=_params(("parallel", "parallel")),
    )(q, k, vt)


class _Dir:
    def __init__(self, lg, strict_future):
        c = RET_CHUNK
        ia = lax.broadcasted_iota(jnp.int32, (c, c), 0).astype(F32)
        ib = lax.broadcasted_iota(jnp.int32, (c, c), 1).astype(F32)
        col = lax.broadcasted_iota(jnp.int32, (c, 1), 0).astype(F32)
        row = lax.broadcasted_iota(jnp.int32, (1, c), 1).astype(F32)
        if strict_future:
            dist = ib - ia
            mask = dist > 0
            self.wq, self.wk, wk_row = c - col, col, row
        else:
            dist = ia - ib
            mask = dist >= 0
            self.wq, self.wk, wk_row = col + 1.0, c - 1.0 - col, c - 1.0 - row
        self.dist = jnp.maximum(dist, 0.0)
        self.d = jnp.where(mask, jnp.exp(self.dist * lg), 0.0)
        self.qd = jnp.exp(self.wq * lg)
        self.kd_col = jnp.exp(self.wk * lg)
        self.kd_row = jnp.exp(wk_row * lg)
        self.cd = jnp.exp(jnp.full((1, 1), float(c), F32) * lg)


def _ret_fwd(z, lgf, lgb, gnw, cos, sin):
    t = z.shape[0]
    c = RET_CHUNK
    nc = t // c
    hd = RET_HEAD_DIM

    def body(lgf_ref, lgb_ref, q_ref, k_ref, v_ref, c_ref, s_ref, w_ref,
             qo_ref, ko_ref, vo_ref, orr_ref, on_ref, kt, of, ob):
        h = pl.program_id(0)
        fw = _Dir(lgf_ref[h], False)
        bw = _Dir(lgb_ref[h], True)
        cc, ss = c_ref[...], s_ref[...]
        qo_ref[...] = _rope(q_ref[...], cc, ss, hd // 4).astype(BF16)
        kr = _rope(k_ref[...], cc, ss, hd // 4) * (hd ** -0.5)
        ko_ref[...] = kr.astype(BF16)
        vo_ref[...] = v_ref[...].astype(BF16)
        for i in range(nc):
            kt[i] = kr[i * c:(i + 1) * c, :].T.astype(BF16)

        def one(ci, s, p, out):
            sl = pl.ds(pl.multiple_of(ci * c, c), c)
            qq, kk, vv = qo_ref[sl, :], ko_ref[sl, :], vo_ref[sl, :]
            a = _dot(qq, kk, NT)
            intra = _dot((a * p.d).astype(BF16), vv)
            cross = _dot(qq, s.astype(BF16)) * p.qd
            out[sl, :] = intra + cross
            return s * p.cd + _dot((kt[ci].astype(F32) * p.kd_row).astype(BF16), vv)

        def step(i, carry):
            sf, sb = carry
            return one(i, sf, fw, of), one(nc - 1 - i, sb, bw, ob)

        zero = jnp.zeros((hd, hd), F32)
        lax.fori_loop(0, nc, step, (zero, zero))
        o = of[...] + ob[...]
        orr_ref[...] = o
        xc = o - jnp.mean(o, axis=-1, keepdims=True)
        var = jnp.mean(xc * xc, axis=-1, keepdims=True)
        on_ref[...] = xc * lax.rsqrt(var + EPS) * w_ref[...]

    smem = pl.BlockSpec(memory_space=pltpu.SMEM)
    col = lambda name: (lambda h: (0, SEG[name][2] // 128 + h))
    head = pl.BlockSpec((t, 128), lambda h: (0, h))
    full = pl.BlockSpec((t, 128), lambda h: (0, 0))
    return pl.pallas_call(
        body, name="ret_fwd", grid=(RET_HEADS,),
        in_specs=[smem, smem, pl.BlockSpec((t, 128), col("qr")), pl.BlockSpec((t, 128), col("kr")),
                  pl.BlockSpec((t, 128), col("vr")), full, full, pl.BlockSpec((1, 128), lambda h: (0, h))],
        out_specs=[head, head, head, head, head],
        out_shape=[SDS((t, RET_WIDTH), BF16)] * 3 + [SDS((t, RET_WIDTH), F32)] * 2,
        scratch_shapes=[pltpu.VMEM((nc, hd, c), BF16), pltpu.VMEM((t, hd), F32), pltpu.VMEM((t, hd), F32)],
        compiler_params=_params(("parallel",)),
    )(lgf, lgb, z, z, z, cos, sin, gnw)


def _merge_fwd(x, z, oa, on, wb_t, wout):
    t, d = x.shape
    tm = min(256, t)

    def body(x_ref, ga_ref, gr_ref, gm0_ref, gm1_ref, oa_ref, on_ref, wb_ref, wo_ref, xn_ref, ya_ref, yb_ref):
        ga, gr = ga_ref[...], gr_ref[...]
        ua = ga * _sigmoid(ga) * oa_ref[...]
        ub = gr * _sigmoid(gr) * on_ref[...]
        ya = _dot(ua.astype(BF16), wb_ref[:, :512], NT)
        yb = _dot(ub.astype(BF16), wb_ref[:, 512:], NT)
        ya_ref[...] = ya
        yb_ref[...] = yb
        merged = _sigmoid(gm0_ref[...]) * ya + _sigmoid(gm1_ref[...]) * yb
        xn_ref[...] = x_ref[...] + _dot(merged.astype(BF16), wo_ref[...])

    row = lambda w, j: pl.BlockSpec((tm, w), lambda i: (i, j))
    const = lambda shape: pl.BlockSpec(shape, lambda i: (0, 0))
    return pl.pallas_call(
        body, name="merge_fwd", grid=(t // tm,),
        in_specs=[row(d, 0), row(512, SEG["ga"][2] // 512), row(512, SEG["gr"][2] // 512),
                  row(1024, SEG["gm"][2] // 1024), row(1024, SEG["gm"][2] // 1024 + 1),
                  row(512, 0), row(512, 0), const((d, 1024)), const((d, d))],
        out_specs=[row(d, 0), row(d, 0), row(d, 0)],
        out_shape=[SDS((t, d), F32)] * 3,
        compiler_params=_params(("parallel",)),
    )(x, z, z, z, z, oa, on, wb_t, wout)


def _final_loss(x, g, target):
    t, d = x.shape
    tm = min(512, t)
    n = t // tm

    def body(x_ref, g_ref, t_ref, dx_ref, dg_ref, loss_ref, acc_g, acc_l):
        i = pl.program_id(0)

        @pl.when(i == 0)
        def _():
            acc_g[...] = jnp.zeros_like(acc_g)
            acc_l[...] = jnp.zeros_like(acc_l)

        xv, gv = x_ref[...], g_ref[...]
        r = lax.rsqrt(jnp.mean(xv * xv, axis=-1, keepdims=True) + EPS)
        xh = xv * r
        err = xh * gv - t_ref[...]
        dy = err * (1.0 / d)
        gy = dy * gv
        dx_ref[...] = r * (gy - xh * jnp.mean(gy * xh, axis=-1, keepdims=True))
        acc_g[...] += jnp.sum((dy * xh).reshape(tm // 8, 8, d), axis=0)
        acc_l[...] += jnp.sum((err * err).reshape(tm // 8, 8, d), axis=0)

        @pl.when(i == n - 1)
        def _():
            dg_ref[...] = jnp.sum(acc_g[...], axis=0, keepdims=True)
            tot = jnp.sum(jnp.sum(acc_l[...], axis=0, keepdims=True), axis=1, keepdims=True)
            loss_ref[...] = jnp.broadcast_to(tot * (0.5 / d), (1, 128))

    return pl.pallas_call(
        body, name="final_loss", grid=(n,),
        in_specs=[pl.BlockSpec((tm, d), lambda i: (i, 0)), pl.BlockSpec((1, d), lambda i: (0, 0)),
                  pl.BlockSpec((tm, d), lambda i: (i, 0))],
        out_specs=[pl.BlockSpec((tm, d), lambda i: (i, 0)), pl.BlockSpec((1, d), lambda i: (0, 0)),
                   pl.BlockSpec((1, 128), lambda i: (0, 0))],
        out_shape=[SDS((t, d), F32), SDS((1, d), F32), SDS((1, 128), F32)],
        scratch_shapes=[pltpu.VMEM((8, d), F32), pltpu.VMEM((8, d), F32)],
        compiler_params=_params(("arbitrary",)),
    )(x, g, target)


def _merge_bwd(dxo, z, oa, on, ya, yb, wb_t, wout):
    t, d = dxo.shape
    tm = min(256, t)
    n = t // tm

    def body(dx_ref, ga_ref, gr_ref, gm0_ref, gm1_ref, oa_ref, on_ref, ya_ref, yb_ref, wb_ref, wo_ref,
             doa_ref, don_ref, dz_ref, dwo_ref, dwb_ref, acc_o, acc_b):
        i = pl.program_id(0)

        @pl.when(i == 0)
        def _():
            acc_o[...] = jnp.zeros_like(acc_o)
            acc_b[...] = jnp.zeros_like(acc_b)

        dxb = dx_ref[...].astype(BF16)
        ya, yb = ya_ref[...], yb_ref[...]
        g0, g1 = _sigmoid(gm0_ref[...]), _sigmoid(gm1_ref[...])
        mb = (g0 * ya + g1 * yb).astype(BF16)
        dm = _dot(dxb, wo_ref[...], NT)
        dya = (dm * g0).astype(BF16)
        dyb = (dm * g1).astype(BF16)
        dz_ref[:, 1024:2048] = (dm * ya * g0 * (1.0 - g0)).astype(BF16)
        dz_ref[:, 2048:3072] = (dm * yb * g1 * (1.0 - g1)).astype(BF16)

        def branch(g_ref, o_ref, dy, w, do_ref, lo):
            gv, ov = g_ref[...], o_ref[...]
            sg = _sigmoid(gv)
            silu = gv * sg
            du = _dot(dy, w)
            do_ref[...] = du * silu
            dz_ref[:, lo:lo + 512] = (du * ov * (sg * (1.0 + gv * (1.0 - sg)))).astype(BF16)
            acc_b[:, lo:lo + 512] += _dot(dy, (silu * ov).astype(BF16), TN)

        branch(ga_ref, oa_ref, dya, wb_ref[:, :512], doa_ref, 0)
        branch(gr_ref, on_ref, dyb, wb_ref[:, 512:], don_ref, 512)
        acc_o[...] += _dot(mb, dxb, TN)

        @pl.when(i == n - 1)
        def _():
            dwo_ref[...] = acc_o[...].astype(BF16)
            dwb_ref[...] = acc_b[...].astype(BF16)

    row = lambda w, j: pl.BlockSpec((tm, w), lambda i: (i, j))
    const = lambda shape: pl.BlockSpec(shape, lambda i: (0, 0))
    return pl.pallas_call(
        body, name="merge_bwd", grid=(n,),
        in_specs=[row(d, 0), row(512, SEG["ga"][2] // 512), row(512, SEG["gr"][2] // 512),
                  row(1024, SEG["gm"][2] // 1024), row(1024, SEG["gm"][2] // 1024 + 1),
                  row(512, 0), row(512, 0), row(d, 0), row(d, 0), const((d, 1024)), const((d, d))],
        out_specs=[row(512, 0), row(512, 0), row(3072, 0), const((d, d)), const((d, 1024))],
        out_shape=[SDS((t, 512), F32), SDS((t, 512), F32), SDS((t, 3072), BF16), SDS((d, d), BF16),
                   SDS((d, 1024), BF16)],
        scratch_shapes=[pltpu.VMEM((d, d), F32), pltpu.VMEM((d, 1024), F32)],
        compiler_params=_params(("arbitrary",)),
    )(dxo, z, z, z, z, oa, on, ya, yb, wb_t, wout)


def _ret_bwd(qrot, krot, vb, orr, don, gnw, lgf, lgb):
    t = qrot.shape[0]
    c = RET_CHUNK
    nc = t // c
    hd = RET_HEAD_DIM

    def body(lgf_ref, lgb_ref, q_ref, k_ref, v_ref, o_ref, dn_ref, w_ref,
             dq_ref, dk_ref, dv_ref, dw_ref, dlf_ref, dlb_ref, qt, kt, dob, sfa, sba):
        h = pl.program_id(0)
        fw = _Dir(lgf_ref[h], False)
        bw = _Dir(lgb_ref[h], True)
        fw.dt, bw.dt = fw.d.T, bw.d.T

        o = o_ref[...]
        xc = o - jnp.mean(o, axis=-1, keepdims=True)
        r = lax.rsqrt(jnp.mean(xc * xc, axis=-1, keepdims=True) + EPS)
        xh = xc * r
        dn = dn_ref[...]
        gy = dn * w_ref[...]
        d_o = r * (gy - jnp.mean(gy, axis=-1, keepdims=True) - xh * jnp.mean(gy * xh, axis=-1, keepdims=True))
        dw_ref[...] = jnp.sum(dn * xh, axis=0, keepdims=True)
        dob[...] = d_o.astype(BF16)
        for i in range(nc):
            qt[i] = q_ref[i * c:(i + 1) * c, :].astype(F32).T.astype(BF16)
            kt[i] = k_ref[i * c:(i + 1) * c, :].astype(F32).T.astype(BF16)
        dq_ref[...] = jnp.zeros_like(dq_ref)
        dk_ref[...] = jnp.zeros_like(dk_ref)
        dv_ref[...] = jnp.zeros_like(dv_ref)

        def load(ci):
            sl = pl.ds(pl.multiple_of(ci * c, c), c)
            return sl, q_ref[sl, :], k_ref[sl, :], v_ref[sl, :], dob[sl, :]

        def pass1(ci, s, acc, p, s_all):
            sl, qq, kk, vv, do = load(ci)
            a = _dot(qq, kk, NT)
            bm = _dot(do, vv, NT)
            doq = (do.astype(F32) * p.qd).astype(BF16)
            sb = s.astype(BF16)
            dqc = _dot(doq, sb, NT)
            dq_ref[sl, :] += _dot((bm * p.d).astype(BF16), kk) + dqc
            s_all[ci] = sb
            acc = acc + p.dist * p.d * a * bm + p.wq * qq.astype(F32) * dqc
            s = s * p.cd + _dot((kt[ci].astype(F32) * p.kd_row).astype(BF16), vv)
            return s, acc

        def pass2(ci, g, acc, p, s_all):
            sl, qq, kk, vv, do = load(ci)
            at = _dot(kk, qq, NT)
            bt = _dot(vv, do, NT)
            gb = g.astype(BF16)
            kkd = (kk.astype(F32) * p.kd_col).astype(BF16)
            dv_ref[sl, :] += _dot((at * p.dt).astype(BF16), do) + _dot(kkd, gb)
            dk2 = _dot(vv, gb, NT) * p.kd_col
            dk_ref[sl, :] += _dot((bt * p.dt).astype(BF16), qq) + dk2
            acc = acc + p.wk * kk.astype(F32) * dk2 + (float(c) * p.cd) * g * s_all[ci].astype(F32)
            doq = (do.astype(F32) * p.qd).astype(BF16)
            g = g * p.cd + _dot(qt[ci], doq)
            return g, acc

        zero = jnp.zeros((hd, hd), F32)

        def step1(i, carry):
            sf, af, sb, ab = carry
            sf, af = pass1(i, sf, af, fw, sfa)
            sb, ab = pass1(nc - 1 - i, sb, ab, bw, sba)
            return sf, af, sb, ab

        _, af, _, ab = lax.fori_loop(0, nc, step1, (zero, zero, zero, zero))

        def step2(i, carry):
            gf, af, gb, ab = carry
            gf, af = pass2(nc - 1 - i, gf, af, fw, sfa)
            gb, ab = pass2(i, gb, ab, bw, sba)
            return gf, af, gb, ab

        _, af, _, ab = lax.fori_loop(0, nc, step2, (zero, af, zero, ab))
        tot = lambda m: jnp.sum(jnp.sum(m, axis=0, keepdims=True), axis=1, keepdims=True)
        dlf_ref[...] = jnp.broadcast_to(tot(af).reshape(1, 1, 1), (1, 8, 128))
        dlb_ref[...] = jnp.broadcast_to(tot(ab).reshape(1, 1, 1), (1, 8, 128))

    smem = pl.BlockSpec(memory_space=pltpu.SMEM)
    head = pl.BlockSpec((t, 128), lambda h: (0, h))
    vec = pl.BlockSpec((1, 128), lambda h: (0, h))
    scal = pl.BlockSpec((1, 8, 128), lambda h: (h, 0, 0))
    return pl.pallas_call(
        body, name="ret_bwd", grid=(RET_HEADS,),
        in_specs=[smem, smem, head, head, head, head, head, vec],
        out_specs=[head, head, head, vec, scal, scal],
        out_shape=[SDS((t, RET_WIDTH), F32)] * 3 + [SDS((1, RET_WIDTH), F32), SDS((RET_HEADS, 8, 128), F32),
                                                   SDS((RET_HEADS, 8, 128), F32)],
        scratch_shapes=[pltpu.VMEM((nc, hd, c), BF16), pltpu.VMEM((nc, hd, c), BF16), pltpu.VMEM((t, hd), BF16),
                        pltpu.VMEM((nc, hd, hd), BF16), pltpu.VMEM((nc, hd, hd), BF16)],
        compiler_params=_params(("parallel",)),
    )(lgf, lgb, qrot, krot, vb, orr, don, gnw)


def _ret_post_bwd(dq, dk, dv, cos, sin):
    t = dq.shape[0]
    tm = min(512, t)
    hd = RET_HEAD_DIM

    def body(dq_ref, dk_ref, dv_ref, c_ref, s_ref, oq_ref, ok_ref, ov_ref):
        cc = jnp.concatenate([c_ref[...]] * 4, axis=-1)
        ss = jnp.concatenate([s_ref[...]] * 4, axis=-1)
        oq_ref[...] = _rope_bwd(dq_ref[...], cc, ss, hd // 4).astype(BF16)
        ok_ref[...] = (_rope_bwd(dk_ref[...], cc, ss, hd // 4) * (hd ** -0.5)).astype(BF16)
        ov_ref[...] = dv_ref[...].astype(BF16)

    blk = pl.BlockSpec((tm, 512), lambda i: (i, 0))
    tab = pl.BlockSpec((tm, 128), lambda i: (i, 0))
    return pl.pallas_call(
        body, name="ret_post_bwd", grid=(t // tm,),
        in_specs=[blk, blk, blk, tab, tab], out_specs=[blk, blk, blk],
        out_shape=[SDS((t, 512), BF16)] * 3,
        compiler_params=_params(("parallel",)),
    )(dq, dk, dv, cos, sin)


def _attn_bwd(q, k, kt, v, doa, oa, lse):
    t = q.shape[1]
    tq = min(256, t)
    nq = t // tq
    nk, tk = kt.shape[1], kt.shape[3]
    hd = ATTN_HEAD_DIM
    scale = hd ** -0.5

    def body(q_ref, k_ref, kt_ref, v_ref, do_ref, o_ref, lse_ref, dq_ref, dk_ref, dv_ref):
        p, i = pl.program_id(0), pl.program_id(1)

        @pl.when(jnp.logical_and(p % 2 == 0, i == 0))
        def _():
            dk_ref[...] = jnp.zeros_like(dk_ref)
            dv_ref[...] = jnp.zeros_like(dv_ref)

        dov, ov = do_ref[...], o_ref[...]
        outs = []
        for j in range(2):
            qq = q_ref[j]
            do32 = dov[:, j * hd:(j + 1) * hd]
            do = do32.astype(BF16)
            dd = jnp.sum((do32 * ov[:, j * hd:(j + 1) * hd]).T, axis=0, keepdims=True)
            lse_j = lse_ref[j]

            def step(c, dqt, qq=qq, do=do, dd=dd, lse_j=lse_j):
                sl = pl.ds(pl.multiple_of(c * tk, tk), tk)
                pt = jnp.exp(_dot(k_ref[0, sl, :], qq, NT) - lse_j)
                dpt = _dot(v_ref[0, sl, :], do, NT)
                dst = (pt * (dpt - dd)).astype(BF16)
                dv_ref[0, sl, :] += _dot(pt.astype(BF16), do)
                dk_ref[0, sl, :] += _dot(dst, qq)
                return dqt + _dot(kt_ref[0, c], dst)

            dqt = lax.fori_loop(0, nk, step, jnp.zeros((hd, tq), F32), unroll=True)
            outs.append(dqt.T * scale)
        dq_ref[...] = jnp.concatenate(outs, axis=-1)

    kv = pl.BlockSpec((1, t, hd), lambda p, i: (p // 2, 0, 0))
    pair = pl.BlockSpec((tq, 128), lambda p, i: (i, p))
    return pl.pallas_call(
        body, name="attn_bwd", grid=(4, nq),
        in_specs=[pl.BlockSpec((2, tq, hd), lambda p, i: (p, i, 0)), kv,
                  pl.BlockSpec((1, nk, hd, tk), lambda p, i: (p // 2, 0, 0, 0)), kv, pair, pair,
                  pl.BlockSpec((2, 1, tq), lambda p, i: (p, 0, i))],
        out_specs=[pair, kv, kv],
        out_shape=[SDS((t, ATTN_WIDTH), F32), SDS((ATTN_KV_HEADS, t, hd), F32), SDS((ATTN_KV_HEADS, t, hd), F32)],
        compiler_params=_params(("arbitrary", "arbitrary")),
    )(q, k, kt, v, doa, oa, lse)


def _attn_post_bwd(dq, dk, dv, z, qn, kn, cos, sin, ones_bd):
    t = z.shape[0]
    tm = min(512, t)
    n = t // tm
    hd = ATTN_HEAD_DIM

    def body(dq_ref, dk_ref, dv_ref, zq_ref, zkv_ref, qn_ref, kn_ref, c_ref, s_ref, b_ref,
             dz_ref, dqn_ref, dkn_ref, acc_q, acc_k):
        i = pl.program_id(0)

        @pl.when(i == 0)
        def _():
            acc_q[...] = jnp.zeros_like(acc_q)
            acc_k[...] = jnp.zeros_like(acc_k)

        bd = b_ref[...]
        c2, s2 = c_ref[...], s_ref[...]

        def norm_bwd(dy, x, w, ones, cos_t, sin_t, acc):
            dyr = _rope_bwd(dy, cos_t, sin_t, hd // 4)
            r = lax.rsqrt(_group_mean(x * x, ones) + EPS)
            xh = x * r
            gy = dyr * w
            acc[...] += jnp.sum((dyr * xh).reshape(tm // 8, 8, x.shape[-1]), axis=0)
            return r * (gy - xh * _group_mean(gy * xh, ones))

        cq = jnp.concatenate([c2] * 4, axis=-1)
        sq = jnp.concatenate([s2] * 4, axis=-1)
        dz_ref[:, :512] = norm_bwd(dq_ref[...], zq_ref[...], qn_ref[...], bd, cq, sq, acc_q).astype(BF16)
        zkv = zkv_ref[...]
        dkk = jnp.concatenate([dk_ref[0], dk_ref[1]], axis=-1)
        dz_ref[:, 512:640] = norm_bwd(dkk, zkv[:, :128], kn_ref[...], bd[:128, :128], c2, s2, acc_k).astype(BF16)
        dz_ref[:, 640:768] = jnp.concatenate([dv_ref[0], dv_ref[1]], axis=-1).astype(BF16)

        @pl.when(i == n - 1)
        def _():
            dqn_ref[...] = jnp.sum(acc_q[...], axis=0, keepdims=True)
            dkn_ref[...] = jnp.sum(acc_k[...], axis=0, keepdims=True)

    kv_blk = SEG["ka"][2] // 256
    kvs = pl.BlockSpec((ATTN_KV_HEADS, tm, hd), lambda i: (0, i, 0))
    const = lambda shape: pl.BlockSpec(shape, lambda i: (0, 0))
    return pl.pallas_call(
        body, name="attn_post_bwd", grid=(n,),
        in_specs=[pl.BlockSpec((tm, 512), lambda i: (i, 0)), kvs, kvs,
                  pl.BlockSpec((tm, 512), lambda i: (i, 0)), pl.BlockSpec((tm, 256), lambda i: (i, kv_blk)),
                  const((1, 512)), const((1, 128)),
                  pl.BlockSpec((tm, 128), lambda i: (i, 0)), pl.BlockSpec((tm, 128), lambda i: (i, 0)),
                  const((512, 512))],
        out_specs=[pl.BlockSpec((tm, 768), lambda i: (i, 0)), const((1, 512)), const((1, 128))],
        out_shape=[SDS((t, 768), BF16), SDS((1, 512), F32), SDS((1, 128), F32)],
        scratch_shapes=[pltpu.VMEM((8, 512), F32), pltpu.VMEM((8, 128), F32)],
        compiler_params=_params(("arbitrary",)),
    )(dq, dk, dv, z, z, qn, kn, cos, sin, ones_bd)


def _in_bwd(dxo, x, g, w_t, dz_a, dz_m, dqr, dkr, dvr):
    t, d = x.shape
    tm = min(256, t)
    n = t // tm
    parts = [(0, 0, 768, 0), (1, 0, 512, SEG["ga"][0]), (2, 0, 512, SEG["qr"][0]), (3, 0, 512, SEG["kr"][0]),
             (4, 0, 512, SEG["vr"][0]), (1, 512, 2560, SEG["gr"][0])]

    def body(dx_ref, x_ref, g_ref, w_ref, a_ref, m_ref, q_ref, k_ref, v_ref, o_ref, dg_ref, acc):
        i = pl.program_id(0)

        @pl.when(i == 0)
        def _():
            acc[...] = jnp.zeros_like(acc)

        pieces = [a_ref, m_ref, q_ref, k_ref, v_ref]
        dh = jnp.zeros((tm, d), F32)
        for pi, lo, w, row in parts:
            dh = dh + _dot(pieces[pi][:, lo:lo + w], w_ref[row:row + w, :])
        xv = x_ref[...]
        r = lax.rsqrt(jnp.mean(xv * xv, axis=-1, keepdims=True) + EPS)
        xh = xv * r
        gy = dh * g_ref[...]
        o_ref[...] = dx_ref[...] + r * (gy - xh * jnp.mean(gy * xh, axis=-1, keepdims=True))
        acc[...] += jnp.sum((dh * xh).reshape(tm // 8, 8, d), axis=0)

        @pl.when(i == n - 1)
        def _():
            dg_ref[...] = jnp.sum(acc[...], axis=0, keepdims=True)

    row = lambda w: pl.BlockSpec((tm, w), lambda i: (i, 0))
    const = lambda shape: pl.BlockSpec(shape, lambda i: (0, 0))
    return pl.pallas_call(
        body, name="in_bwd", grid=(n,),
        in_specs=[row(d), row(d), const((1, d)), const((D_IN, d)), row(768), row(3072), row(512), row(512),
                  row(512)],
        out_specs=[row(d), const((1, d))],
        out_shape=[SDS((t, d), F32), SDS((1, d), F32)],
        scratch_shapes=[pltpu.VMEM((8, d), F32)],
        compiler_params=_params(("arbitrary",)),
    )(dxo, x, g, w_t, dz_a, dz_m, dqr, dkr, dvr)


def _dw_in(h_t, piece, col0, width, row0, buf):
    d, t = h_t.shape
    tn = 256
    c0, r0 = col0 // tn, row0 // tn

    def body(*refs):
        h_ref, p_ref, o_ref = refs[0], refs[1], refs[-1]
        o_ref[...] = _dot(h_ref[...], p_ref[...]).T.astype(BF16)

    in_specs = [pl.BlockSpec((d, t), lambda j: (0, 0)), pl.BlockSpec((t, tn), lambda j: (0, c0 + j))]
    args = [h_t, piece]
    aliases = {}
    if buf is not None:
        in_specs.append(ANY)
        args.append(buf)
        aliases = {2: 0}
    return pl.pallas_call(
        body, name="dw_in", grid=(width // tn,),
        in_specs=in_specs, out_specs=pl.BlockSpec((tn, d), lambda j: (r0 + j, 0)),
        out_shape=SDS((D_IN, d), BF16), input_output_aliases=aliases,
        compiler_params=_params(("parallel",)),
    )(*args)


def _adamw(w, g, m, v):
    rows, cols = w.shape
    tr = 256 if rows % 256 == 0 else rows

    def body(w_ref, g_ref, m_ref, v_ref, d_ref, mo_ref, vo_ref):
        gv = g_ref[...]
        mn = ADAM_B1 * m_ref[...] + (1.0 - ADAM_B1) * gv
        vn = ADAM_B2 * v_ref[...] + (1.0 - ADAM_B2) * (gv * gv)
        m_hat = mn / (1.0 - ADAM_B1 ** ADAM_STEP)
        v_hat = vn / (1.0 - ADAM_B2 ** ADAM_STEP)
        d_ref[...] = -ADAM_LR * (m_hat / (jnp.sqrt(v_hat) + ADAM_EPS) + ADAM_WD * w_ref[...])
        mo_ref[...] = mn
        vo_ref[...] = vn

    blk = pl.BlockSpec((tr, cols), lambda i: (i, 0))
    return pl.pallas_call(
        body, name="adamw", grid=(rows // tr,),
        in_specs=[blk] * 4, out_specs=[blk] * 3, out_shape=[SDS((rows, cols), F32)] * 3,
        compiler_params=_params(("parallel",)),
    )(w, g, m, v)


def _me():
    return lax.axis_index("x"), lax.axis_index("y"), lax.axis_index("c")


def _flip(k):
    x, y, c = _me()
    px = 1 - x if k & 4 else x
    py = 1 - y if k & 2 else y
    pc = 1 - c if k & 1 else c
    return (px, py, pc), 4 * px + 2 * py + pc


def _all_gather(shards):
    na = len(shards)
    chips = (4, 2, 6)

    def body(*refs):
        ins, outs = refs[:na], refs[na:2 * na]
        send_sems, recv_sems, local_sems = refs[2 * na:]
        _, mine = _flip(0)

        def rows(a, idx):
            r = shards[a].shape[0]
            return outs[a].at[pl.ds(pl.multiple_of(idx * r, 16), r), :]

        def copy(a, slot, block_idx, to, src=None):
            return pltpu.make_async_remote_copy(
                src_ref=rows(a, block_idx) if src is None else src, dst_ref=rows(a, block_idx),
                send_sem=send_sems.at[a, slot], recv_sem=recv_sems.at[a, slot],
                device_id=to, device_id_type=MESH_ID)

        sibling, sibling_idx = _flip(1)
        local, started = [], []
        for a in range(na):
            cp = pltpu.make_async_copy(ins[a], rows(a, mine), local_sems.at[a])
            cp.start()
            local.append(cp)
            first = [copy(a, 0, mine, sibling, src=ins[a])]
            first += [copy(a, 1 + j, mine, _flip(k)[0], src=ins[a]) for j, k in enumerate(chips)]
            for cp in first:
                cp.start()
            started += first
        for a in range(na):
            for j, k in enumerate(chips):
                _, theirs = _flip(k)
                copy(a, 1 + j, theirs, _flip(0)[0]).wait_recv()
                fwd = copy(a, 4 + j, theirs, sibling)
                fwd.start()
                started.append(fwd)
        for a in range(na):
            copy(a, 0, sibling_idx, _flip(0)[0]).wait_recv()
            for j, k in enumerate(chips):
                _, theirs = _flip(k | 1)
                copy(a, 4 + j, theirs, _flip(0)[0]).wait_recv()
        for cp in started:
            cp.wait_send()
        for cp in local:
            cp.wait()

    return pl.pallas_call(
        body, name="all_gather_weights",
        in_specs=[ANY] * na, out_specs=[ANY] * na,
        out_shape=[SDS((N_DEV * s.shape[0], s.shape[1]), s.dtype) for s in shards],
        scratch_shapes=[pltpu.SemaphoreType.DMA((na, 7)), pltpu.SemaphoreType.DMA((na, 7)),
                        pltpu.SemaphoreType.DMA((na,))],
        compiler_params=pltpu.CompilerParams(has_side_effects=True),
    )(*shards)


def _scatter_blocks(grads):
    na = len(grads)

    def body(*refs):
        ins, outs = refs[:na], refs[na:2 * na]
        send_sems, recv_sems, local_sems = refs[2 * na:]
        me, mine = _flip(0)
        pending = []
        for a in range(na):
            r = grads[a].shape[0] // N_DEV

            def block(idx, a=a, r=r):
                return ins[a].at[pl.ds(pl.multiple_of(idx * r, 16), r), :]

            cp = pltpu.make_async_copy(block(mine), outs[a].at[mine], local_sems.at[a])
            cp.start()
            pending.append(cp)
            for k in range(1, N_DEV):
                peer, theirs = _flip(k)
                cp = pltpu.make_async_remote_copy(
                    src_ref=block(theirs), dst_ref=outs[a].at[mine],
                    send_sem=send_sems.at[a, k - 1], recv_sem=recv_sems.at[a, k - 1],
                    device_id=peer, device_id_type=MESH_ID)
                cp.start()
                pending.append((cp, a, k))
        for item in pending:
            if isinstance(item, tuple):
                cp, a, k = item
                cp.wait_send()
                _, theirs = _flip(k)
                pltpu.make_async_remote_copy(
                    src_ref=outs[a].at[theirs], dst_ref=outs[a].at[theirs],
                    send_sem=send_sems.at[a, k - 1], recv_sem=recv_sems.at[a, k - 1],
                    device_id=me, device_id_type=MESH_ID).wait_recv()
            else:
                item.wait()

    return pl.pallas_call(
        body, name="scatter_grads",
        in_specs=[ANY] * na, out_specs=[ANY] * na,
        out_shape=[SDS((N_DEV, g.shape[0] // N_DEV, g.shape[1]), g.dtype) for g in grads],
        scratch_shapes=[pltpu.SemaphoreType.DMA((na, 7)), pltpu.SemaphoreType.DMA((na, 7)),
                        pltpu.SemaphoreType.DMA((na,))],
        compiler_params=pltpu.CompilerParams(has_side_effects=True),
    )(*grads)


def _sum_slots(recv):
    _, r, w = recv.shape
    tr = 128 if r % 128 == 0 else r

    def body(r_ref, o_ref):
        acc = r_ref[0].astype(F32)
        for s in range(1, N_DEV):
            acc = acc + r_ref[s].astype(F32)
        o_ref[...] = acc

    return pl.pallas_call(
        body, name="sum_slots", grid=(r // tr,),
        in_specs=[pl.BlockSpec((N_DEV, tr, w), lambda i: (0, i, 0))],
        out_specs=pl.BlockSpec((tr, w), lambda i: (i, 0)),
        out_shape=SDS((r, w), F32),
        compiler_params=_params(("parallel",)),
    )(recv)


def _all_reduce_small(packed):
    shape = packed.shape

    def body(p_ref, o_ref, slots, send_sems, recv_sems):
        me, mine = _flip(0)
        slots[mine] = p_ref[...]
        sends = []
        for k in range(1, N_DEV):
            peer, _ = _flip(k)
            cp = pltpu.make_async_remote_copy(
                src_ref=p_ref, dst_ref=slots.at[mine], send_sem=send_sems.at[k - 1], recv_sem=recv_sems.at[k - 1],
                device_id=peer, device_id_type=MESH_ID)
            cp.start()
            sends.append(cp)
        for k in range(1, N_DEV):
            _, theirs = _flip(k)
            pltpu.make_async_remote_copy(
                src_ref=p_ref, dst_ref=slots.at[theirs], send_sem=send_sems.at[k - 1],
                recv_sem=recv_sems.at[k - 1], device_id=me, device_id_type=MESH_ID).wait_recv()
        for cp in sends:
            cp.wait_send()
        acc = slots[0]
        for s in range(1, N_DEV):
            acc = acc + slots[s]
        o_ref[...] = acc

    vm = pl.BlockSpec(memory_space=pltpu.VMEM)
    return pl.pallas_call(
        body, name="all_reduce_small", in_specs=[vm], out_specs=vm, out_shape=SDS(shape, F32),
        scratch_shapes=[pltpu.VMEM((N_DEV,) + shape, F32), pltpu.SemaphoreType.DMA((7,)),
                        pltpu.SemaphoreType.DMA((7,))],
        compiler_params=pltpu.CompilerParams(has_side_effects=True),
    )(packed)


def _layer_fwd(x, p, tabs):
    z, h_t = _in_proj(x, p["norm_g"], p["w_in_t"])
    q, k, v, kt, vt = _attn_prep(z, p["qn"], p["kn"], tabs["ca"], tabs["sa"], tabs["ones"])
    oa, lse = _attn_fwd(q, k, vt)
    qrot, krot, vb, orr, on = _ret_fwd(z, p["lgf"], p["lgb"], p["gnw"], tabs["cr"], tabs["sr"])
    xn, ya, yb = _merge_fwd(x, z, oa, on, p["wb_t"], p["w_out"])
    saved = dict(x=x, z=z, h_t=h_t, q=q, k=k, v=v, kt=kt, lse=lse, oa=oa, qrot=qrot, krot=krot, vb=vb, orr=orr, on=on, ya=ya, yb=yb)
    return xn, saved


def _layer_bwd(dxo, s, p, tabs):
    doa, don, dz_m, d_wout, d_wb_t = _merge_bwd(dxo, s["z"], s["oa"], s["on"], s["ya"], s["yb"], p["wb_t"], p["w_out"])
    dq_r, dk_r, dv_r, d_gnw, d_lgf, d_lgb = _ret_bwd(s["qrot"], s["krot"], s["vb"], s["orr"], don, p["gnw"],
                                                     p["lgf"], p["lgb"])
    dqr, dkr, dvr = _ret_post_bwd(dq_r, dk_r, dv_r, tabs["cr"], tabs["sr"])
    dq_a, dk_a, dv_a = _attn_bwd(s["q"], s["k"], s["kt"], s["v"], doa, s["oa"], s["lse"])
    dz_a, d_qn, d_kn = _attn_post_bwd(dq_a, dk_a, dv_a, s["z"], p["qn"], p["kn"], tabs["ca"], tabs["sa"],
                                      tabs["ones"])
    dx, d_norm_g = _in_bwd(dxo, s["x"], p["norm_g"], p["w_in_t"], dz_a, dz_m, dqr, dkr, dvr)
    buf = None
    for piece, col0, width, row0 in [(dz_a, 0, 768, 0), (dz_m, 0, 512, SEG["ga"][0]), (dqr, 0, 512, SEG["qr"][0]),
                                     (dkr, 0, 512, SEG["kr"][0]), (dvr, 0, 512, SEG["vr"][0]),
                                     (dz_m, 512, 2560, SEG["gr"][0])]:
        buf = _dw_in(s["h_t"], piece, col0, width, row0, buf)
    grads = dict(w_in_t=buf, wb_t=d_wb_t, w_out=d_wout, norm_g=d_norm_g, gnw=d_gnw,
                 qn=d_qn.reshape(ATTN_Q_HEADS, ATTN_HEAD_DIM).sum(axis=0),
                 kn=d_kn.reshape(ATTN_KV_HEADS, ATTN_HEAD_DIM).sum(axis=0),
                 lgf=d_lgf[:, 0, 0], lgb=d_lgb[:, 0, 0])
    return dx, grads


def _adamw_nd(w, g, m, v):
    shape = w.shape
    two_d = (1, shape[0]) if w.ndim == 1 else (-1, shape[-1])
    out = _adamw(w.reshape(two_d), g.reshape(two_d), m.reshape(two_d), v.reshape(two_d))
    return tuple(o.reshape(shape) for o in out)


def kernel(x, norm_g, w_in, attn_q_norm, attn_k_norm, ret_decay_fwd, ret_decay_bwd, ret_gn_w, w_branch_attn, w_branch_ret, w_out, final_norm_g, loss_target, m_norm_g, m_w_in, m_attn_q_norm, m_attn_k_norm, m_ret_decay_fwd, m_ret_decay_bwd, m_ret_gn_w, m_w_branch_attn, m_w_branch_ret, m_w_out, m_final_norm_g, v_norm_g, v_w_in, v_attn_q_norm, v_attn_k_norm, v_ret_decay_fwd, v_ret_decay_bwd, v_ret_gn_w, v_w_branch_attn, v_w_branch_ret, v_w_out, v_final_norm_g):
    t, d = x.shape[1], x.shape[2]
    x2, target = x[0], loss_target[0]

    shards = []
    for l in range(DEPTH):
        shards.append(jnp.swapaxes(w_in[l], 0, 1).astype(BF16))
        shards.append(jnp.concatenate([w_branch_attn[l].T, w_branch_ret[l].T], axis=1).astype(BF16))
        shards.append(w_out[l].astype(BF16))
    full = _all_gather(shards)

    ca, sa = _rope_tables(t, ATTN_HEAD_DIM)
    cr, sr = _rope_tables(t, RET_HEAD_DIM)
    grp = jnp.arange(ATTN_WIDTH) // ATTN_HEAD_DIM
    tabs = dict(ca=jnp.tile(ca, (1, 2)), sa=jnp.tile(sa, (1, 2)), cr=cr, sr=sr,
                ones=jnp.where(grp[:, None] == grp[None, :], 1.0 / ATTN_HEAD_DIM, 0.0).astype(BF16))
    layers = []
    for l in range(DEPTH):
        layers.append(dict(
            w_in_t=full[3 * l], wb_t=full[3 * l + 1], w_out=full[3 * l + 2],
            norm_g=norm_g[l][None], qn=jnp.tile(attn_q_norm[l], ATTN_Q_HEADS)[None],
            kn=jnp.tile(attn_k_norm[l], ATTN_KV_HEADS)[None], gnw=ret_gn_w[l][None],
            lgf=jax.nn.log_sigmoid(ret_decay_fwd[l]), lgb=jax.nn.log_sigmoid(ret_decay_bwd[l])))

    h = x2
    saved = []
    for l in range(DEPTH):
        h, s = _layer_fwd(h, layers[l], tabs)
        saved.append(s)
    dx, d_final_g, loss_part = _final_loss(h, final_norm_g[None], target)
    grads = [None] * DEPTH
    for l in reversed(range(DEPTH)):
        dx, grads[l] = _layer_bwd(dx, saved[l], layers[l], tabs)

    recv = _scatter_blocks([grads[l][name] for l in range(DEPTH) for name in ("w_in_t", "wb_t", "w_out")])
    summed = [_sum_slots(r) for r in recv]
    g_w_in = jnp.stack([summed[3 * l].T for l in range(DEPTH)])
    g_wba = jnp.stack([summed[3 * l + 1][:, :512].T for l in range(DEPTH)])
    g_wbr = jnp.stack([summed[3 * l + 1][:, 512:].T for l in range(DEPTH)])
    g_wout = jnp.stack([summed[3 * l + 2] for l in range(DEPTH)])

    packed = jnp.zeros((8, 1024), F32)
    for l in range(DEPTH):
        gl = grads[l]
        packed = packed.at[l].set(gl["norm_g"][0])
        packed = packed.at[2, 512 * l:512 * (l + 1)].set(gl["gnw"][0])
        packed = packed.at[4, 128 * l:128 * l + 64].set(gl["qn"])
        packed = packed.at[4, 256 + 128 * l:256 + 128 * l + 64].set(gl["kn"])
        packed = packed.at[4, 512 + 128 * l:512 + 128 * l + 4].set(gl["lgf"])
        packed = packed.at[4, 768 + 128 * l:768 + 128 * l + 4].set(gl["lgb"])
    packed = packed.at[3].set(d_final_g[0])
    packed = packed.at[5, 0].set(loss_part[0, 0])
    red = _all_reduce_small(packed)
    loss = red[5, 0]
    g_norm_g = red[0:2]
    g_gnw = red[2].reshape(DEPTH, RET_WIDTH)
    g_final = red[3]
    g_qn = jnp.stack([red[4, 128 * l:128 * l + 64] for l in range(DEPTH)])
    g_kn = jnp.stack([red[4, 256 + 128 * l:256 + 128 * l + 64] for l in range(DEPTH)])
    g_lgf = jnp.stack([red[4, 512 + 128 * l:512 + 128 * l + 4] for l in range(DEPTH)])
    g_lgb = jnp.stack([red[4, 768 + 128 * l:768 + 128 * l + 4] for l in range(DEPTH)])
    g_df = g_lgf * jax.nn.sigmoid(-ret_decay_fwd)
    g_db = g_lgb * jax.nn.sigmoid(-ret_decay_bwd)

    grad_w = [g_norm_g, g_w_in, g_qn, g_kn, g_df, g_db, g_gnw, g_wba, g_wbr, g_wout, g_final]
    weights = [norm_g, w_in, attn_q_norm, attn_k_norm, ret_decay_fwd, ret_decay_bwd, ret_gn_w, w_branch_attn,
               w_branch_ret, w_out, final_norm_g]
    ms = [m_norm_g, m_w_in, m_attn_q_norm, m_attn_k_norm, m_ret_decay_fwd, m_ret_decay_bwd, m_ret_gn_w,
          m_w_branch_attn, m_w_branch_ret, m_w_out, m_final_norm_g]
    vs = [v_norm_g, v_w_in, v_attn_q_norm, v_attn_k_norm, v_ret_decay_fwd, v_ret_decay_bwd, v_ret_gn_w,
          v_w_branch_attn, v_w_branch_ret, v_w_out, v_final_norm_g]
    upd = [_adamw_nd(w, g, m, v) for w, g, m, v in zip(weights, grad_w, ms, vs)]
    return (loss, dx[None], *grad_w, *[u[0] for u in upd], *[u[1] for u in upd], *[u[2] for u in upd])
```

```python
import functools

import jax
import jax.numpy as jnp
from jax import lax
from jax.experimental import pallas as pl
from jax.experimental.pallas import tpu as pltpu

F32 = jnp.float32
BF16 = jnp.bfloat16
SDS = jax.ShapeDtypeStruct

D_MODEL = 1024
DEPTH = 2
GRID_W = 64
ATTN_Q_HEADS = 8
ATTN_KV_HEADS = 2
ATTN_HEAD_DIM = 64
ATTN_WIDTH = 512
ATTN_KV_WIDTH = 128
RET_HEADS = 4
RET_HEAD_DIM = 128
RET_WIDTH = 512
RET_CHUNK = 128
ATTN_KEY_CHUNK = 512
ROPE_THETA = 10000.0
EPS = 1e-6
D_IN = 5376
N_DEV = 8

ADAM_LR = 0.001
ADAM_B1 = 0.9
ADAM_B2 = 0.999
ADAM_EPS = 1e-08
ADAM_WD = 0.01
ADAM_STEP = 10

SEG = {
    "qa": (0, 512, 0),
    "ga": (768, 512, 512),
    "qr": (1280, 512, 1024),
    "kr": (1792, 512, 1536),
    "vr": (2304, 512, 2048),
    "gr": (2816, 512, 2560),
    "gm": (3328, 2048, 3072),
    "ka": (512, 128, 5120),
    "va": (640, 128, 5248),
}

VMEM_LIMIT = 60 * 1024 * 1024
NT = (((1,), (1,)), ((), ()))
TN = (((0,), (0,)), ((), ()))
MESH_ID = pl.DeviceIdType.MESH
ANY = pl.BlockSpec(memory_space=pl.ANY)


def _params(sem=None, vmem=VMEM_LIMIT):
    return pltpu.CompilerParams(dimension_semantics=sem, vmem_limit_bytes=vmem)


def _dot(a, b, dims=None):
    if dims is None:
        return jnp.dot(a, b, preferred_element_type=F32)
    return lax.dot_general(a, b, dims, preferred_element_type=F32)


def _sigmoid(x):
    return 1.0 / (1.0 + jnp.exp(-x))


def _swap_halves(x, q):
    n = x.shape[-1]
    axis = x.ndim - 1
    lane = lax.broadcasted_iota(jnp.int32, x.shape, axis)
    first = (lane % (2 * q)) < q
    return jnp.where(first, pltpu.roll(x, n - q, axis), pltpu.roll(x, q, axis))


def _rope(x, cos, sin_signed, q):
    return x * cos + _swap_halves(x, q) * sin_signed


def _rope_bwd(dy, cos, sin_signed, q):
    return dy * cos - _swap_halves(dy, q) * sin_signed


def _group_mean(v, ones_bd):
    hi = v.astype(BF16)
    r1 = v - hi.astype(F32)
    mid = r1.astype(BF16)
    lo = (r1 - mid.astype(F32)).astype(BF16)
    return _dot(hi, ones_bd) + _dot(mid, ones_bd) + _dot(lo, ones_bd)


def _rope_tables(t, head_dim):
    n_rows = t // GRID_W
    row = jnp.repeat(jnp.arange(n_rows, dtype=F32), GRID_W)
    col = jnp.tile(jnp.arange(GRID_W, dtype=F32), n_rows)
    d_axis = head_dim // 2
    inv_freq = ROPE_THETA ** (-jnp.arange(0, d_axis, 2, dtype=F32) / d_axis)
    ar = row[:, None] * inv_freq
    ac = col[:, None] * inv_freq
    cr, sr, cc, sc = jnp.cos(ar), jnp.sin(ar), jnp.cos(ac), jnp.sin(ac)
    return jnp.concatenate([cr, cr, cc, cc], axis=-1), jnp.concatenate([-sr, sr, -sc, sc], axis=-1)


def _me():
    return lax.axis_index("x"), lax.axis_index("y"), lax.axis_index("c")


def _flip(k):
    x, y, c = _me()
    px = 1 - x if k & 4 else x
    py = 1 - y if k & 2 else y
    pc = 1 - c if k & 1 else c
    return (px, py, pc), 4 * px + 2 * py + pc


class _Exchange:
    def __init__(self, kind, srcs):
        self.kind, self.srcs, self.n = kind, list(srcs), len(srcs)
        self.rows = [a.shape[0] if kind == "gather" else a.shape[0] // N_DEV for a in srcs]
        if kind == "gather":
            self.out_shape = [SDS((N_DEV * a.shape[0], a.shape[1]), a.dtype) for a in srcs]
        else:
            self.out_shape = [SDS((N_DEV, a.shape[0] // N_DEV, a.shape[1]), a.dtype) for a in srcs]
        self.scratch = [pltpu.SemaphoreType.DMA((self.n, N_DEV - 1)), pltpu.SemaphoreType.DMA((self.n, N_DEV - 1)),
                        pltpu.SemaphoreType.DMA((self.n,))]

    def _block(self, ref, a, idx):
        r = self.rows[a]
        return ref.at[pl.ds(pl.multiple_of(idx * r, 16), r), :]

    def _src(self, ins, a, idx):
        return ins[a] if self.kind == "gather" else self._block(ins[a], a, idx)

    def _dst(self, outs, a, idx):
        return self._block(outs[a], a, idx) if self.kind == "gather" else outs[a].at[idx]

    def _copies(self, ins, outs, sems):
        send_sems, recv_sems, local_sems = sems
        me, mine = _flip(0)
        local, sends, recvs = [], [], []
        for a in range(self.n):
            local.append(pltpu.make_async_copy(self._src(ins, a, mine), self._dst(outs, a, mine), local_sems.at[a]))
            for k in range(1, N_DEV):
                peer, theirs = _flip(k)
                sem = dict(send_sem=send_sems.at[a, k - 1], recv_sem=recv_sems.at[a, k - 1])
                sends.append(pltpu.make_async_remote_copy(
                    src_ref=self._src(ins, a, theirs), dst_ref=self._dst(outs, a, mine),
                    device_id=peer, device_id_type=MESH_ID, **sem))
                recvs.append(pltpu.make_async_remote_copy(
                    src_ref=self._dst(outs, a, theirs), dst_ref=self._dst(outs, a, theirs),
                    device_id=me, device_id_type=MESH_ID, **sem))
        return local, sends, recvs

    def start(self, ins, outs, sems):
        local, sends, _ = self._copies(ins, outs, sems)
        for cp in local + sends:
            cp.start()

    def wait(self, ins, outs, sems):
        local, sends, recvs = self._copies(ins, outs, sems)
        for cp in sends:
            cp.wait_send()
        for cp in recvs:
            cp.wait_recv()
        for cp in local:
            cp.wait()


def _with_exchange(body, n_in, n_out, n_scratch, ex, first, last):
    if ex is None:
        return body

    def wrapped(*refs):
        ins = refs[:n_in]
        ex_ins = refs[n_in:n_in + ex.n]
        outs = refs[n_in + ex.n:n_in + ex.n + n_out]
        ex_outs = refs[n_in + ex.n + n_out:n_in + 2 * ex.n + n_out]
        rest = refs[n_in + 2 * ex.n + n_out:]
        scratch, sems = rest[:n_scratch], rest[n_scratch:]

        @pl.when(first())
        def _():
            ex.start(ex_ins, ex_outs, sems)

        body(*ins, *outs, *scratch)

        @pl.when(last())
        def _():
            ex.wait(ex_ins, ex_outs, sems)

    return wrapped


def _ex_args(ex):
    if ex is None:
        return [], [], [], [], []
    return [ANY] * ex.n, [ANY] * ex.n, list(ex.out_shape), list(ex.scratch), list(ex.srcs)


def _in_proj(x, g, w_t):
    t, d = x.shape
    tm = min(256, t)

    def body(x_ref, g_ref, w_ref, z_ref, ht_ref):
        xv = x_ref[...]
        r = lax.rsqrt(jnp.mean(xv * xv, axis=-1, keepdims=True) + EPS)
        h = xv * r * g_ref[...]
        ht_ref[...] = h.T.astype(BF16)
        hb = h.astype(BF16)
        for nat, w, off in SEG.values():
            z_ref[:, off:off + w] = _dot(hb, w_ref[nat:nat + w, :], NT)

    return pl.pallas_call(
        body, name="in_proj", grid=(t // tm,),
        in_specs=[pl.BlockSpec((tm, d), lambda i: (i, 0)), pl.BlockSpec((1, d), lambda i: (0, 0)),
                  pl.BlockSpec((D_IN, d), lambda i: (0, 0))],
        out_specs=[pl.BlockSpec((tm, D_IN), lambda i: (i, 0)), pl.BlockSpec((d, tm), lambda i: (0, i))],
        out_shape=[SDS((t, D_IN), F32), SDS((d, t), BF16)],
        compiler_params=_params(("parallel",)),
    )(x, g, w_t)


def _attn_prep(z, qn, kn, cos, sin, ones_bd):
    t = z.shape[0]
    tm = min(ATTN_KEY_CHUNK, t)
    hd = ATTN_HEAD_DIM

    def body(zq_ref, zkv_ref, qn_ref, kn_ref, c_ref, s_ref, b_ref, q_out, k_out, v_out, kt_out, vt_out):
        bd = b_ref[...]
        c2, s2 = c_ref[...], s_ref[...]
        cq = jnp.concatenate([c2] * 4, axis=-1)
        sq = jnp.concatenate([s2] * 4, axis=-1)
        xq = zq_ref[...]
        yq = xq * lax.rsqrt(_group_mean(xq * xq, bd) + EPS) * qn_ref[...]
        yq = _rope(yq, cq, sq, hd // 4) * (hd ** -0.5)
        for h in range(ATTN_Q_HEADS):
            q_out[h] = yq[:, h * hd:(h + 1) * hd].astype(BF16)
        zkv = zkv_ref[...]
        xk, xv = zkv[:, :ATTN_KV_WIDTH], zkv[:, ATTN_KV_WIDTH:]
        yk = xk * lax.rsqrt(_group_mean(xk * xk, bd[:ATTN_KV_WIDTH, :ATTN_KV_WIDTH]) + EPS) * kn_ref[...]
        yk = _rope(yk, c2, s2, hd // 4)
        ykt, xvt = yk.T, xv.T
        ones = jnp.ones((hd, tm), F32)
        for h in range(ATTN_KV_HEADS):
            k_out[h] = yk[:, h * hd:(h + 1) * hd].astype(BF16)
            v_out[h] = xv[:, h * hd:(h + 1) * hd].astype(BF16)
            kt_out[h, 0] = ykt[h * hd:(h + 1) * hd, :].astype(BF16)
            vt_out[h, 0] = jnp.concatenate([xvt[h * hd:(h + 1) * hd, :], ones], axis=0).astype(BF16)

    kv_blk = SEG["ka"][2] // 256
    nk = t // tm
    return pl.pallas_call(
        body, name="attn_prep", grid=(nk,),
        in_specs=[pl.BlockSpec((tm, 512), lambda i: (i, 0)), pl.BlockSpec((tm, 256), lambda i: (i, kv_blk)),
                  pl.BlockSpec((1, 512), lambda i: (0, 0)), pl.BlockSpec((1, 128), lambda i: (0, 0)),
                  pl.BlockSpec((tm, 128), lambda i: (i, 0)), pl.BlockSpec((tm, 128), lambda i: (i, 0)),
                  pl.BlockSpec((512, 512), lambda i: (0, 0))],
        out_specs=[pl.BlockSpec((ATTN_Q_HEADS, tm, hd), lambda i: (0, i, 0)),
                   pl.BlockSpec((ATTN_KV_HEADS, tm, hd), lambda i: (0, i, 0)),
                   pl.BlockSpec((ATTN_KV_HEADS, tm, hd), lambda i: (0, i, 0)),
                   pl.BlockSpec((ATTN_KV_HEADS, 1, hd, tm), lambda i: (0, i, 0, 0)),
                   pl.BlockSpec((ATTN_KV_HEADS, 1, 2 * hd, tm), lambda i: (0, i, 0, 0))],
        out_shape=[SDS((ATTN_Q_HEADS, t, hd), BF16), SDS((ATTN_KV_HEADS, t, hd), BF16),
                   SDS((ATTN_KV_HEADS, t, hd), BF16), SDS((ATTN_KV_HEADS, nk, hd, tm), BF16),
                   SDS((ATTN_KV_HEADS, nk, 2 * hd, tm), BF16)],
        compiler_params=_params(("parallel",)),
    )(z, z, qn, kn, cos, sin, ones_bd)


def _attn_fwd(q, k, vt, ex=None):
    t = q.shape[1]
    tq = min(256, t)
    nk, tk = vt.shape[1], vt.shape[3]
    hd = ATTN_HEAD_DIM
    g = ATTN_Q_HEADS // ATTN_KV_HEADS

    def body(q_ref, k_ref, vt_ref, o_ref, lse_ref, s_scr):
        def pass_a(h, c, m8):
            half = tk // 2
            for lo in (c * tk, c * tk + half):
                st = _dot(k_ref[0, lo:lo + half, :], q_ref[h], NT)
                s_scr[h % 2, lo:lo + half, :] = st
                m8 = jnp.maximum(m8, jnp.max(st.reshape(half // 8, 8, tq), axis=0))
            return m8

        def pass_b(h, c, m, acc):
            e = jnp.exp(s_scr[h % 2, c * tk:(c + 1) * tk, :] - m).astype(BF16)
            return acc + _dot(vt_ref[0, c], e)

        neg = jnp.full((8, tq), -jnp.inf, F32)
        m8 = neg
        for c in range(nk):
            m8 = pass_a(0, c, m8)
        outs = []
        for h in range(g):
            m = jnp.max(m8, axis=0, keepdims=True)
            acc = jnp.zeros((2 * hd, tq), F32)
            m8 = neg
            for c in range(nk):
                if h + 1 < g:
                    m8 = pass_a(h + 1, c, m8)
                acc = pass_b(h, c, m, acc)
            l = acc[hd:hd + 1, :]
            outs.append((acc[:hd, :] / l).T)
            lse_ref[h] = m + jnp.log(l)
        o_ref[...] = jnp.concatenate(outs, axis=-1)

    nq = t // tq
    first = lambda: jnp.logical_and(pl.program_id(0) == 0, pl.program_id(1) == 0)
    last = lambda: jnp.logical_and(pl.program_id(0) == ATTN_KV_HEADS - 1, pl.program_id(1) == nq - 1)
    xi, xo, xs, xscr, xargs = _ex_args(ex)
    return pl.pallas_call(
        _with_exchange(body, 3, 2, 1, ex, first, last), name="attn_fwd", grid=(ATTN_KV_HEADS, nq),
        in_specs=[pl.BlockSpec((g, tq, hd), lambda p, i: (p, i, 0)),
                  pl.BlockSpec((1, t, hd), lambda p, i: (p, 0, 0)),
                  pl.BlockSpec((1, nk, 2 * hd, tk), lambda p, i: (p, 0, 0, 0))] + xi,
        out_specs=[pl.BlockSpec((tq, g * hd), lambda p, i: (i, p)),
                   pl.BlockSpec((g, 1, tq), lambda p, i: (p, 0, i))] + xo,
        out_shape=[SDS((t, ATTN_WIDTH), F32), SDS((ATTN_Q_HEADS, 1, t), F32)] + xs,
        scratch_shapes=[pltpu.VMEM((2, t, tq), F32)] + xscr,
        compiler_params=_params(("arbitrary", "arbitrary")),
    )(q, k, vt, *xargs)


class _Dir:
    def __init__(self, lg, strict_future):
        c = RET_CHUNK
        ia = lax.broadcasted_iota(jnp.int32, (c, c), 0).astype(F32)
        ib = lax.broadcasted_iota(jnp.int32, (c, c), 1).astype(F32)
        col = lax.broadcasted_iota(jnp.int32, (c, 1), 0).astype(F32)
        row = lax.broadcasted_iota(jnp.int32, (1, c), 1).astype(F32)
        if strict_future:
            dist = ib - ia
            mask = dist > 0
            self.wq, self.wk, wk_row = c - col, col, row
        else:
            dist = ia - ib
            mask = dist >= 0
            self.wq, self.wk, wk_row = col + 1.0, c - 1.0 - col, c - 1.0 - row
        self.dist = jnp.maximum(dist, 0.0)
        self.d = jnp.where(mask, jnp.exp(self.dist * lg), 0.0)
        self.qd = jnp.exp(self.wq * lg)
        self.kd_col = jnp.exp(self.wk * lg)
        self.kd_row = jnp.exp(wk_row * lg)
        self.cd = jnp.exp(jnp.full((1, 1), float(c), F32) * lg)


def _ret_fwd(z, lgf, lgb, gnw, cos, sin):
    t = z.shape[0]
    c = RET_CHUNK
    nc = t // c
    hd = RET_HEAD_DIM

    def body(lgf_ref, lgb_ref, q_ref, k_ref, v_ref, c_ref, s_ref, w_ref,
             qo_ref, ko_ref, vo_ref, orr_ref, on_ref, kt, of, ob):
        h = pl.program_id(0)
        fw = _Dir(lgf_ref[h], False)
        bw = _Dir(lgb_ref[h], True)
        cc, ss = c_ref[...], s_ref[...]
        qo_ref[...] = _rope(q_ref[...], cc, ss, hd // 4).astype(BF16)
        kr = _rope(k_ref[...], cc, ss, hd // 4) * (hd ** -0.5)
        ko_ref[...] = kr.astype(BF16)
        vo_ref[...] = v_ref[...].astype(BF16)
        for i in range(nc):
            kt[i] = kr[i * c:(i + 1) * c, :].T.astype(BF16)

        def one(ci, s, p, out):
            sl = pl.ds(pl.multiple_of(ci * c, c), c)
            qq, kk, vv = qo_ref[sl, :], ko_ref[sl, :], vo_ref[sl, :]
            a = _dot(qq, kk, NT)
            intra = _dot((a * p.d).astype(BF16), vv)
            cross = _dot(qq, s.astype(BF16)) * p.qd
            out[sl, :] = intra + cross
            return s * p.cd + _dot((kt[ci].astype(F32) * p.kd_row).astype(BF16), vv)

        def step(i, carry):
            sf, sb = carry
            return one(i, sf, fw, of), one(nc - 1 - i, sb, bw, ob)

        zero = jnp.zeros((hd, hd), F32)
        lax.fori_loop(0, nc, step, (zero, zero))
        o = of[...] + ob[...]
        orr_ref[...] = o
        xc = o - jnp.mean(o, axis=-1, keepdims=True)
        var = jnp.mean(xc * xc, axis=-1, keepdims=True)
        on_ref[...] = xc * lax.rsqrt(var + EPS) * w_ref[...]

    smem = pl.BlockSpec(memory_space=pltpu.SMEM)
    col = lambda name: (lambda h: (0, SEG[name][2] // 128 + h))
    head = pl.BlockSpec((t, 128), lambda h: (0, h))
    full = pl.BlockSpec((t, 128), lambda h: (0, 0))
    return pl.pallas_call(
        body, name="ret_fwd", grid=(RET_HEADS,),
        in_specs=[smem, smem, pl.BlockSpec((t, 128), col("qr")), pl.BlockSpec((t, 128), col("kr")),
                  pl.BlockSpec((t, 128), col("vr")), full, full, pl.BlockSpec((1, 128), lambda h: (0, h))],
        out_specs=[head, head, head, head, head],
        out_shape=[SDS((t, RET_WIDTH), BF16)] * 3 + [SDS((t, RET_WIDTH), F32)] * 2,
        scratch_shapes=[pltpu.VMEM((nc, hd, c), BF16), pltpu.VMEM((t, hd), F32), pltpu.VMEM((t, hd), F32)],
        compiler_params=_params(("parallel",)),
    )(lgf, lgb, z, z, z, cos, sin, gnw)


def _merge_fwd(x, z, oa, on, wb_t, wout):
    t, d = x.shape
    tm = min(256, t)

    def body(x_ref, ga_ref, gr_ref, gm0_ref, gm1_ref, oa_ref, on_ref, wb_ref, wo_ref, xn_ref, ya_ref, yb_ref):
        ga, gr = ga_ref[...], gr_ref[...]
        ua = ga * _sigmoid(ga) * oa_ref[...]
        ub = gr * _sigmoid(gr) * on_ref[...]
        ya = _dot(ua.astype(BF16), wb_ref[:, :512], NT)
        yb = _dot(ub.astype(BF16), wb_ref[:, 512:], NT)
        ya_ref[...] = ya
        yb_ref[...] = yb
        merged = _sigmoid(gm0_ref[...]) * ya + _sigmoid(gm1_ref[...]) * yb
        xn_ref[...] = x_ref[...] + _dot(merged.astype(BF16), wo_ref[...])

    row = lambda w, j: pl.BlockSpec((tm, w), lambda i: (i, j))
    const = lambda shape: pl.BlockSpec(shape, lambda i: (0, 0))
    return pl.pallas_call(
        body, name="merge_fwd", grid=(t // tm,),
        in_specs=[row(d, 0), row(512, SEG["ga"][2] // 512), row(512, SEG["gr"][2] // 512),
                  row(1024, SEG["gm"][2] // 1024), row(1024, SEG["gm"][2] // 1024 + 1),
                  row(512, 0), row(512, 0), const((d, 1024)), const((d, d))],
        out_specs=[row(d, 0), row(d, 0), row(d, 0)],
        out_shape=[SDS((t, d), F32)] * 3,
        compiler_params=_params(("parallel",)),
    )(x, z, z, z, z, oa, on, wb_t, wout)


def _final_loss(x, g, target):
    t, d = x.shape
    tm = min(512, t)
    n = t // tm

    def body(x_ref, g_ref, t_ref, dx_ref, dg_ref, loss_ref, acc_g, acc_l):
        i = pl.program_id(0)

        @pl.when(i == 0)
        def _():
            acc_g[...] = jnp.zeros_like(acc_g)
            acc_l[...] = jnp.zeros_like(acc_l)

        xv, gv = x_ref[...], g_ref[...]
        r = lax.rsqrt(jnp.mean(xv * xv, axis=-1, keepdims=True) + EPS)
        xh = xv * r
        err = xh * gv - t_ref[...]
        dy = err * (1.0 / d)
        gy = dy * gv
        dx_ref[...] = r * (gy - xh * jnp.mean(gy * xh, axis=-1, keepdims=True))
        acc_g[...] += jnp.sum((dy * xh).reshape(tm // 8, 8, d), axis=0)
        acc_l[...] += jnp.sum((err * err).reshape(tm // 8, 8, d), axis=0)

        @pl.when(i == n - 1)
        def _():
            dg_ref[...] = jnp.sum(acc_g[...], axis=0, keepdims=True)
            tot = jnp.sum(jnp.sum(acc_l[...], axis=0, keepdims=True), axis=1, keepdims=True)
            loss_ref[...] = jnp.broadcast_to(tot * (0.5 / d), (1, 128))

    return pl.pallas_call(
        body, name="final_loss", grid=(n,),
        in_specs=[pl.BlockSpec((tm, d), lambda i: (i, 0)), pl.BlockSpec((1, d), lambda i: (0, 0)),
                  pl.BlockSpec((tm, d), lambda i: (i, 0))],
        out_specs=[pl.BlockSpec((tm, d), lambda i: (i, 0)), pl.BlockSpec((1, d), lambda i: (0, 0)),
                   pl.BlockSpec((1, 128), lambda i: (0, 0))],
        out_shape=[SDS((t, d), F32), SDS((1, d), F32), SDS((1, 128), F32)],
        scratch_shapes=[pltpu.VMEM((8, d), F32), pltpu.VMEM((8, d), F32)],
        compiler_params=_params(("arbitrary",)),
    )(x, g, target)


def _merge_bwd(dxo, z, oa, on, ya, yb, wb_t, wout):
    t, d = dxo.shape
    tm = min(256, t)
    n = t // tm

    def body(dx_ref, ga_ref, gr_ref, gm0_ref, gm1_ref, oa_ref, on_ref, ya_ref, yb_ref, wb_ref, wo_ref,
             doa_ref, don_ref, dz_ref, dwo_ref, dwb_ref, acc_o, acc_b):
        i = pl.program_id(0)

        @pl.when(i == 0)
        def _():
            acc_o[...] = jnp.zeros_like(acc_o)
            acc_b[...] = jnp.zeros_like(acc_b)

        dxb = dx_ref[...].astype(BF16)
        ya, yb = ya_ref[...], yb_ref[...]
        g0, g1 = _sigmoid(gm0_ref[...]), _sigmoid(gm1_ref[...])
        mb = (g0 * ya + g1 * yb).astype(BF16)
        dm = _dot(dxb, wo_ref[...], NT)
        dya = (dm * g0).astype(BF16)
        dyb = (dm * g1).astype(BF16)
        dz_ref[:, 1024:2048] = (dm * ya * g0 * (1.0 - g0)).astype(BF16)
        dz_ref[:, 2048:3072] = (dm * yb * g1 * (1.0 - g1)).astype(BF16)

        def branch(g_ref, o_ref, dy, w, do_ref, lo):
            gv, ov = g_ref[...], o_ref[...]
            sg = _sigmoid(gv)
            silu = gv * sg
            du = _dot(dy, w)
            do_ref[...] = du * silu
            dz_ref[:, lo:lo + 512] = (du * ov * (sg * (1.0 + gv * (1.0 - sg)))).astype(BF16)
            acc_b[:, lo:lo + 512] += _dot(dy, (silu * ov).astype(BF16), TN)

        branch(ga_ref, oa_ref, dya, wb_ref[:, :512], doa_ref, 0)
        branch(gr_ref, on_ref, dyb, wb_ref[:, 512:], don_ref, 512)
        acc_o[...] += _dot(mb, dxb, TN)

        @pl.when(i == n - 1)
        def _():
            dwo_ref[...] = acc_o[...].astype(BF16)
            dwb_ref[...] = acc_b[...].astype(BF16)

    row = lambda w, j: pl.BlockSpec((tm, w), lambda i: (i, j))
    const = lambda shape: pl.BlockSpec(shape, lambda i: (0, 0))
    return pl.pallas_call(
        body, name="merge_bwd", grid=(n,),
        in_specs=[row(d, 0), row(512, SEG["ga"][2] // 512), row(512, SEG["gr"][2] // 512),
                  row(1024, SEG["gm"][2] // 1024), row(1024, SEG["gm"][2] // 1024 + 1),
                  row(512, 0), row(512, 0), row(d, 0), row(d, 0), const((d, 1024)), const((d, d))],
        out_specs=[row(512, 0), row(512, 0), row(3072, 0), const((d, d)), const((d, 1024))],
        out_shape=[SDS((t, 512), F32), SDS((t, 512), F32), SDS((t, 3072), BF16), SDS((d, d), BF16),
                   SDS((d, 1024), BF16)],
        scratch_shapes=[pltpu.VMEM((d, d), F32), pltpu.VMEM((d, 1024), F32)],
        compiler_params=_params(("arbitrary",)),
    )(dxo, z, z, z, z, oa, on, ya, yb, wb_t, wout)


def _ret_bwd(qrot, krot, vb, orr, don, gnw, lgf, lgb):
    t = qrot.shape[0]
    c = RET_CHUNK
    nc = t // c
    hd = RET_HEAD_DIM

    def body(lgf_ref, lgb_ref, q_ref, k_ref, v_ref, o_ref, dn_ref, w_ref,
             dq_ref, dk_ref, dv_ref, dw_ref, dlf_ref, dlb_ref, qt, kt, dob, sfa, sba):
        h = pl.program_id(0)
        fw = _Dir(lgf_ref[h], False)
        bw = _Dir(lgb_ref[h], True)
        fw.dt, bw.dt = fw.d.T, bw.d.T

        o = o_ref[...]
        xc = o - jnp.mean(o, axis=-1, keepdims=True)
        r = lax.rsqrt(jnp.mean(xc * xc, axis=-1, keepdims=True) + EPS)
        xh = xc * r
        dn = dn_ref[...]
        gy = dn * w_ref[...]
        d_o = r * (gy - jnp.mean(gy, axis=-1, keepdims=True) - xh * jnp.mean(gy * xh, axis=-1, keepdims=True))
        dw_ref[...] = jnp.sum(dn * xh, axis=0, keepdims=True)
        dob[...] = d_o.astype(BF16)
        for i in range(nc):
            qt[i] = q_ref[i * c:(i + 1) * c, :].astype(F32).T.astype(BF16)
            kt[i] = k_ref[i * c:(i + 1) * c, :].astype(F32).T.astype(BF16)
        dq_ref[...] = jnp.zeros_like(dq_ref)
        dk_ref[...] = jnp.zeros_like(dk_ref)
        dv_ref[...] = jnp.zeros_like(dv_ref)

        def load(ci):
            sl = pl.ds(pl.multiple_of(ci * c, c), c)
            return sl, q_ref[sl, :], k_ref[sl, :], v_ref[sl, :], dob[sl, :]

        def pass1(ci, s, acc, p, s_all):
            sl, qq, kk, vv, do = load(ci)
            a = _dot(qq, kk, NT)
            bm = _dot(do, vv, NT)
            doq = (do.astype(F32) * p.qd).astype(BF16)
            sb = s.astype(BF16)
            dqc = _dot(doq, sb, NT)
            dq_ref[sl, :] += _dot((bm * p.d).astype(BF16), kk) + dqc
            s_all[ci] = sb
            acc = acc + p.dist * p.d * a * bm + p.wq * qq.astype(F32) * dqc
            s = s * p.cd + _dot((kt[ci].astype(F32) * p.kd_row).astype(BF16), vv)
            return s, acc

        def pass2(ci, g, acc, p, s_all):
            sl, qq, kk, vv, do = load(ci)
            at = _dot(kk, qq, NT)
            bt = _dot(vv, do, NT)
            gb = g.astype(BF16)
            kkd = (kk.astype(F32) * p.kd_col).astype(BF16)
            dv_ref[sl, :] += _dot((at * p.dt).astype(BF16), do) + _dot(kkd, gb)
            dk2 = _dot(vv, gb, NT) * p.kd_col
            dk_ref[sl, :] += _dot((bt * p.dt).astype(BF16), qq) + dk2
            acc = acc + p.wk * kk.astype(F32) * dk2 + (float(c) * p.cd) * g * s_all[ci].astype(F32)
            doq = (do.astype(F32) * p.qd).astype(BF16)
            g = g * p.cd + _dot(qt[ci], doq)
            return g, acc

        zero = jnp.zeros((hd, hd), F32)

        def step1(i, carry):
            sf, af, sb, ab = carry
            sf, af = pass1(i, sf, af, fw, sfa)
            sb, ab = pass1(nc - 1 - i, sb, ab, bw, sba)
            return sf, af, sb, ab

        _, af, _, ab = lax.fori_loop(0, nc, step1, (zero, zero, zero, zero))

        def step2(i, carry):
            gf, af, gb, ab = carry
            gf, af = pass2(nc - 1 - i, gf, af, fw, sfa)
            gb, ab = pass2(i, gb, ab, bw, sba)
            return gf, af, gb, ab

        _, af, _, ab = lax.fori_loop(0, nc, step2, (zero, af, zero, ab))
        tot = lambda m: jnp.sum(jnp.sum(m, axis=0, keepdims=True), axis=1, keepdims=True)
        dlf_ref[...] = jnp.broadcast_to(tot(af).reshape(1, 1, 1), (1, 8, 128))
        dlb_ref[...] = jnp.broadcast_to(tot(ab).reshape(1, 1, 1), (1, 8, 128))

    smem = pl.BlockSpec(memory_space=pltpu.SMEM)
    head = pl.BlockSpec((t, 128), lambda h: (0, h))
    vec = pl.BlockSpec((1, 128), lambda h: (0, h))
    scal = pl.BlockSpec((1, 8, 128), lambda h: (h, 0, 0))
    return pl.pallas_call(
        body, name="ret_bwd", grid=(RET_HEADS,),
        in_specs=[smem, smem, head, head, head, head, head, vec],
        out_specs=[head, head, head, vec, scal, scal],
        out_shape=[SDS((t, RET_WIDTH), F32)] * 3 + [SDS((1, RET_WIDTH), F32), SDS((RET_HEADS, 8, 128), F32),
                                                   SDS((RET_HEADS, 8, 128), F32)],
        scratch_shapes=[pltpu.VMEM((nc, hd, c), BF16), pltpu.VMEM((nc, hd, c), BF16), pltpu.VMEM((t, hd), BF16),
                        pltpu.VMEM((nc, hd, hd), BF16), pltpu.VMEM((nc, hd, hd), BF16)],
        compiler_params=_params(("parallel",)),
    )(lgf, lgb, qrot, krot, vb, orr, don, gnw)


def _ret_post_bwd(dq, dk, dv, cos, sin):
    t = dq.shape[0]
    tm = min(512, t)
    hd = RET_HEAD_DIM

    def body(dq_ref, dk_ref, dv_ref, c_ref, s_ref, oq_ref, ok_ref, ov_ref):
        cc = jnp.concatenate([c_ref[...]] * 4, axis=-1)
        ss = jnp.concatenate([s_ref[...]] * 4, axis=-1)
        oq_ref[...] = _rope_bwd(dq_ref[...], cc, ss, hd // 4).astype(BF16)
        ok_ref[...] = (_rope_bwd(dk_ref[...], cc, ss, hd // 4) * (hd ** -0.5)).astype(BF16)
        ov_ref[...] = dv_ref[...].astype(BF16)

    blk = pl.BlockSpec((tm, 512), lambda i: (i, 0))
    tab = pl.BlockSpec((tm, 128), lambda i: (i, 0))
    return pl.pallas_call(
        body, name="ret_post_bwd", grid=(t // tm,),
        in_specs=[blk, blk, blk, tab, tab], out_specs=[blk, blk, blk],
        out_shape=[SDS((t, 512), BF16)] * 3,
        compiler_params=_params(("parallel",)),
    )(dq, dk, dv, cos, sin)


def _attn_bwd(q, k, kt, v, doa, oa, lse, ex=None):
    t = q.shape[1]
    tq = min(256, t)
    nq = t // tq
    nk, tk = kt.shape[1], kt.shape[3]
    hd = ATTN_HEAD_DIM
    scale = hd ** -0.5

    def body(q_ref, k_ref, kt_ref, v_ref, do_ref, o_ref, lse_ref, dq_ref, dk_ref, dv_ref):
        p, i = pl.program_id(0), pl.program_id(1)

        @pl.when(jnp.logical_and(p % 2 == 0, i == 0))
        def _():
            dk_ref[...] = jnp.zeros_like(dk_ref)
            dv_ref[...] = jnp.zeros_like(dv_ref)

        dov, ov = do_ref[...], o_ref[...]
        outs = []
        for j in range(2):
            qq = q_ref[j]
            do32 = dov[:, j * hd:(j + 1) * hd]
            do = do32.astype(BF16)
            dd = jnp.sum((do32 * ov[:, j * hd:(j + 1) * hd]).T, axis=0, keepdims=True)
            lse_j = lse_ref[j]

            def step(c, dqt, qq=qq, do=do, dd=dd, lse_j=lse_j):
                sl = pl.ds(pl.multiple_of(c * tk, tk), tk)
                pt = jnp.exp(_dot(k_ref[0, sl, :], qq, NT) - lse_j)
                dpt = _dot(v_ref[0, sl, :], do, NT)
                dst = (pt * (dpt - dd)).astype(BF16)
                dv_ref[0, sl, :] += _dot(pt.astype(BF16), do)
                dk_ref[0, sl, :] += _dot(dst, qq)
                return dqt + _dot(kt_ref[0, c], dst)

            dqt = lax.fori_loop(0, nk, step, jnp.zeros((hd, tq), F32), unroll=True)
            outs.append(dqt.T * scale)
        dq_ref[...] = jnp.concatenate(outs, axis=-1)

    kv = pl.BlockSpec((1, t, hd), lambda p, i: (p // 2, 0, 0))
    pair = pl.BlockSpec((tq, 128), lambda p, i: (i, p))
    first = lambda: jnp.logical_and(pl.program_id(0) == 0, pl.program_id(1) == 0)
    last = lambda: jnp.logical_and(pl.program_id(0) == 3, pl.program_id(1) == nq - 1)
    xi, xo, xs, xscr, xargs = _ex_args(ex)
    return pl.pallas_call(
        _with_exchange(body, 7, 3, 0, ex, first, last), name="attn_bwd", grid=(4, nq),
        in_specs=[pl.BlockSpec((2, tq, hd), lambda p, i: (p, i, 0)), kv,
                  pl.BlockSpec((1, nk, hd, tk), lambda p, i: (p // 2, 0, 0, 0)), kv, pair, pair,
                  pl.BlockSpec((2, 1, tq), lambda p, i: (p, 0, i))] + xi,
        out_specs=[pair, kv, kv] + xo,
        out_shape=[SDS((t, ATTN_WIDTH), F32), SDS((ATTN_KV_HEADS, t, hd), F32),
                   SDS((ATTN_KV_HEADS, t, hd), F32)] + xs,
        scratch_shapes=xscr,
        compiler_params=_params(("arbitrary", "arbitrary")),
    )(q, k, kt, v, doa, oa, lse, *xargs)


def _attn_post_bwd(dq, dk, dv, z, qn, kn, cos, sin, ones_bd):
    t = z.shape[0]
    tm = min(512, t)
    n = t // tm
    hd = ATTN_HEAD_DIM

    def body(dq_ref, dk_ref, dv_ref, zq_ref, zkv_ref, qn_ref, kn_ref, c_ref, s_ref, b_ref,
             dz_ref, dqn_ref, dkn_ref, acc_q, acc_k):
        i = pl.program_id(0)

        @pl.when(i == 0)
        def _():
            acc_q[...] = jnp.zeros_like(acc_q)
            acc_k[...] = jnp.zeros_like(acc_k)

        bd = b_ref[...]
        c2, s2 = c_ref[...], s_ref[...]

        def norm_bwd(dy, x, w, ones, cos_t, sin_t, acc):
            dyr = _rope_bwd(dy, cos_t, sin_t, hd // 4)
            r = lax.rsqrt(_group_mean(x * x, ones) + EPS)
            xh = x * r
            gy = dyr * w
            acc[...] += jnp.sum((dyr * xh).reshape(tm // 8, 8, x.shape[-1]), axis=0)
            return r * (gy - xh * _group_mean(gy * xh, ones))

        cq = jnp.concatenate([c2] * 4, axis=-1)
        sq = jnp.concatenate([s2] * 4, axis=-1)
        dz_ref[:, :512] = norm_bwd(dq_ref[...], zq_ref[...], qn_ref[...], bd, cq, sq, acc_q).astype(BF16)
        zkv = zkv_ref[...]
        dkk = jnp.concatenate([dk_ref[0], dk_ref[1]], axis=-1)
        dz_ref[:, 512:640] = norm_bwd(dkk, zkv[:, :128], kn_ref[...], bd[:128, :128], c2, s2, acc_k).astype(BF16)
        dz_ref[:, 640:768] = jnp.concatenate([dv_ref[0], dv_ref[1]], axis=-1).astype(BF16)

        @pl.when(i == n - 1)
        def _():
            dqn_ref[...] = jnp.sum(acc_q[...], axis=0, keepdims=True)
            dkn_ref[...] = jnp.sum(acc_k[...], axis=0, keepdims=True)

    kv_blk = SEG["ka"][2] // 256
    kvs = pl.BlockSpec((ATTN_KV_HEADS, tm, hd), lambda i: (0, i, 0))
    const = lambda shape: pl.BlockSpec(shape, lambda i: (0, 0))
    return pl.pallas_call(
        body, name="attn_post_bwd", grid=(n,),
        in_specs=[pl.BlockSpec((tm, 512), lambda i: (i, 0)), kvs, kvs,
                  pl.BlockSpec((tm, 512), lambda i: (i, 0)), pl.BlockSpec((tm, 256), lambda i: (i, kv_blk)),
                  const((1, 512)), const((1, 128)),
                  pl.BlockSpec((tm, 128), lambda i: (i, 0)), pl.BlockSpec((tm, 128), lambda i: (i, 0)),
                  const((512, 512))],
        out_specs=[pl.BlockSpec((tm, 768), lambda i: (i, 0)), const((1, 512)), const((1, 128))],
        out_shape=[SDS((t, 768), BF16), SDS((1, 512), F32), SDS((1, 128), F32)],
        scratch_shapes=[pltpu.VMEM((8, 512), F32), pltpu.VMEM((8, 128), F32)],
        compiler_params=_params(("arbitrary",)),
    )(dq, dk, dv, z, z, qn, kn, cos, sin, ones_bd)


def _in_bwd(dxo, x, g, w_t, dz_a, dz_m, dqr, dkr, dvr, ex=None):
    t, d = x.shape
    tm = min(256, t)
    n = t // tm
    parts = [(0, 0, 768, 0), (1, 0, 512, SEG["ga"][0]), (2, 0, 512, SEG["qr"][0]), (3, 0, 512, SEG["kr"][0]),
             (4, 0, 512, SEG["vr"][0]), (1, 512, 2560, SEG["gr"][0])]

    def body(dx_ref, x_ref, g_ref, w_ref, a_ref, m_ref, q_ref, k_ref, v_ref, o_ref, dg_ref, acc):
        i = pl.program_id(0)

        @pl.when(i == 0)
        def _():
            acc[...] = jnp.zeros_like(acc)

        pieces = [a_ref, m_ref, q_ref, k_ref, v_ref]
        dh = jnp.zeros((tm, d), F32)
        for pi, lo, w, row in parts:
            dh = dh + _dot(pieces[pi][:, lo:lo + w], w_ref[row:row + w, :])
        xv = x_ref[...]
        r = lax.rsqrt(jnp.mean(xv * xv, axis=-1, keepdims=True) + EPS)
        xh = xv * r
        gy = dh * g_ref[...]
        o_ref[...] = dx_ref[...] + r * (gy - xh * jnp.mean(gy * xh, axis=-1, keepdims=True))
        acc[...] += jnp.sum((dh * xh).reshape(tm // 8, 8, d), axis=0)

        @pl.when(i == n - 1)
        def _():
            dg_ref[...] = jnp.sum(acc[...], axis=0, keepdims=True)

    row = lambda w: pl.BlockSpec((tm, w), lambda i: (i, 0))
    const = lambda shape: pl.BlockSpec(shape, lambda i: (0, 0))
    xi, xo, xs, xscr, xargs = _ex_args(ex)
    return pl.pallas_call(
        _with_exchange(body, 9, 2, 1, ex, lambda: pl.program_id(0) == 0, lambda: pl.program_id(0) == n - 1),
        name="in_bwd", grid=(n,),
        in_specs=[row(d), row(d), const((1, d)), const((D_IN, d)), row(768), row(3072), row(512), row(512),
                  row(512)] + xi,
        out_specs=[row(d), const((1, d))] + xo,
        out_shape=[SDS((t, d), F32), SDS((1, d), F32)] + xs,
        scratch_shapes=[pltpu.VMEM((8, d), F32)] + xscr,
        compiler_params=_params(("arbitrary",)),
    )(dxo, x, g, w_t, dz_a, dz_m, dqr, dkr, dvr, *xargs)


def _dw_in(h_t, piece, col0, width, row0, buf):
    d, t = h_t.shape
    tn = 256
    c0, r0 = col0 // tn, row0 // tn

    def body(*refs):
        h_ref, p_ref, o_ref = refs[0], refs[1], refs[-1]
        o_ref[...] = _dot(h_ref[...], p_ref[...]).T.astype(BF16)

    in_specs = [pl.BlockSpec((d, t), lambda j: (0, 0)), pl.BlockSpec((t, tn), lambda j: (0, c0 + j))]
    args = [h_t, piece]
    aliases = {}
    if buf is not None:
        in_specs.append(ANY)
        args.append(buf)
        aliases = {2: 0}
    return pl.pallas_call(
        body, name="dw_in", grid=(width // tn,),
        in_specs=in_specs, out_specs=pl.BlockSpec((tn, d), lambda j: (r0 + j, 0)),
        out_shape=SDS((D_IN, d), BF16), input_output_aliases=aliases,
        compiler_params=_params(("parallel",)),
    )(*args)


def _adamw(w, g, m, v):
    rows, cols = w.shape
    tr = 256 if rows % 256 == 0 else rows

    def body(w_ref, g_ref, m_ref, v_ref, d_ref, mo_ref, vo_ref):
        gv = g_ref[...]
        mn = ADAM_B1 * m_ref[...] + (1.0 - ADAM_B1) * gv
        vn = ADAM_B2 * v_ref[...] + (1.0 - ADAM_B2) * (gv * gv)
        m_hat = mn / (1.0 - ADAM_B1 ** ADAM_STEP)
        v_hat = vn / (1.0 - ADAM_B2 ** ADAM_STEP)
        d_ref[...] = -ADAM_LR * (m_hat / (jnp.sqrt(v_hat) + ADAM_EPS) + ADAM_WD * w_ref[...])
        mo_ref[...] = mn
        vo_ref[...] = vn

    blk = pl.BlockSpec((tr, cols), lambda i: (i, 0))
    return pl.pallas_call(
        body, name="adamw", grid=(rows // tr,),
        in_specs=[blk] * 4, out_specs=[blk] * 3, out_shape=[SDS((rows, cols), F32)] * 3,
        compiler_params=_params(("parallel",)),
    )(w, g, m, v)


def _all_gather(shards):
    na = len(shards)
    chips = (4, 2, 6)

    def body(*refs):
        ins, outs = refs[:na], refs[na:2 * na]
        send_sems, recv_sems, local_sems = refs[2 * na:]
        _, mine = _flip(0)

        def rows(a, idx):
            r = shards[a].shape[0]
            return outs[a].at[pl.ds(pl.multiple_of(idx * r, 16), r), :]

        def copy(a, slot, block_idx, to, src=None):
            return pltpu.make_async_remote_copy(
                src_ref=rows(a, block_idx) if src is None else src, dst_ref=rows(a, block_idx),
                send_sem=send_sems.at[a, slot], recv_sem=recv_sems.at[a, slot],
                device_id=to, device_id_type=MESH_ID)

        sibling, sibling_idx = _flip(1)
        local, started = [], []
        for a in range(na):
            cp = pltpu.make_async_copy(ins[a], rows(a, mine), local_sems.at[a])
            cp.start()
            local.append(cp)
            first = [copy(a, 0, mine, sibling, src=ins[a])]
            first += [copy(a, 1 + j, mine, _flip(k)[0], src=ins[a]) for j, k in enumerate(chips)]
            for cp in first:
                cp.start()
            started += first
        for a in range(na):
            for j, k in enumerate(chips):
                _, theirs = _flip(k)
                copy(a, 1 + j, theirs, _flip(0)[0]).wait_recv()
                fwd = copy(a, 4 + j, theirs, sibling)
                fwd.start()
                started.append(fwd)
        for a in range(na):
            copy(a, 0, sibling_idx, _flip(0)[0]).wait_recv()
            for j, k in enumerate(chips):
                _, theirs = _flip(k | 1)
                copy(a, 4 + j, theirs, _flip(0)[0]).wait_recv()
        for cp in started:
            cp.wait_send()
        for cp in local:
            cp.wait()

    return pl.pallas_call(
        body, name="all_gather_weights",
        in_specs=[ANY] * na, out_specs=[ANY] * na,
        out_shape=[SDS((N_DEV * s.shape[0], s.shape[1]), s.dtype) for s in shards],
        scratch_shapes=[pltpu.SemaphoreType.DMA((na, 7)), pltpu.SemaphoreType.DMA((na, 7)),
                        pltpu.SemaphoreType.DMA((na,))],
        compiler_params=pltpu.CompilerParams(has_side_effects=True),
    )(*shards)


def _sum_slots(recv):
    _, r, w = recv.shape
    tr = 128 if r % 128 == 0 else r

    def body(r_ref, o_ref):
        acc = r_ref[0].astype(F32)
        for s in range(1, N_DEV):
            acc = acc + r_ref[s].astype(F32)
        o_ref[...] = acc

    return pl.pallas_call(
        body, name="sum_slots", grid=(r // tr,),
        in_specs=[pl.BlockSpec((N_DEV, tr, w), lambda i: (0, i, 0))],
        out_specs=pl.BlockSpec((tr, w), lambda i: (i, 0)),
        out_shape=SDS((r, w), F32),
        compiler_params=_params(("parallel",)),
    )(recv)


def _all_reduce_small(packed):
    shape = packed.shape

    def body(p_ref, o_ref, slots, send_sems, recv_sems):
        me, mine = _flip(0)
        slots[mine] = p_ref[...]
        sends = []
        for k in range(1, N_DEV):
            peer, _ = _flip(k)
            cp = pltpu.make_async_remote_copy(
                src_ref=p_ref, dst_ref=slots.at[mine], send_sem=send_sems.at[k - 1], recv_sem=recv_sems.at[k - 1],
                device_id=peer, device_id_type=MESH_ID)
            cp.start()
            sends.append(cp)
        for k in range(1, N_DEV):
            _, theirs = _flip(k)
            pltpu.make_async_remote_copy(
                src_ref=p_ref, dst_ref=slots.at[theirs], send_sem=send_sems.at[k - 1],
                recv_sem=recv_sems.at[k - 1], device_id=me, device_id_type=MESH_ID).wait_recv()
        for cp in sends:
            cp.wait_send()
        acc = slots[0]
        for s in range(1, N_DEV):
            acc = acc + slots[s]
        o_ref[...] = acc

    vm = pl.BlockSpec(memory_space=pltpu.VMEM)
    return pl.pallas_call(
        body, name="all_reduce_small", in_specs=[vm], out_specs=vm, out_shape=SDS(shape, F32),
        scratch_shapes=[pltpu.VMEM((N_DEV,) + shape, F32), pltpu.SemaphoreType.DMA((7,)),
                        pltpu.SemaphoreType.DMA((7,))],
        compiler_params=pltpu.CompilerParams(has_side_effects=True),
    )(packed)


def _layer_fwd(x, p, tabs, ex):
    z, h_t = _in_proj(x, p["norm_g"], p["w_in_t"])
    q, k, v, kt, vt = _attn_prep(z, p["qn"], p["kn"], tabs["ca"], tabs["sa"], tabs["ones"])
    oa, lse, *gathered = _attn_fwd(q, k, vt, ex)
    qrot, krot, vb, orr, on = _ret_fwd(z, p["lgf"], p["lgb"], p["gnw"], tabs["cr"], tabs["sr"])
    return z, h_t, q, k, v, kt, lse, oa, qrot, krot, vb, orr, on, gathered


def _layer_bwd(dxo, s, p, tabs, ex_attn, make_ex_in):
    doa, don, dz_m, d_wout, d_wb_t = _merge_bwd(dxo, s["z"], s["oa"], s["on"], s["ya"], s["yb"], p["wb_t"], p["w_out"])
    dq_a, dk_a, dv_a, *recv_attn = _attn_bwd(s["q"], s["k"], s["kt"], s["v"], doa, s["oa"], s["lse"],
                                              ex_attn(d_wb_t, d_wout))
    dz_a, d_qn, d_kn = _attn_post_bwd(dq_a, dk_a, dv_a, s["z"], p["qn"], p["kn"], tabs["ca"], tabs["sa"],
                                      tabs["ones"])
    dq_r, dk_r, dv_r, d_gnw, d_lgf, d_lgb = _ret_bwd(s["qrot"], s["krot"], s["vb"], s["orr"], don, p["gnw"],
                                                     p["lgf"], p["lgb"])
    dqr, dkr, dvr = _ret_post_bwd(dq_r, dk_r, dv_r, tabs["cr"], tabs["sr"])
    buf = None
    for piece, col0, width, row0 in [(dz_a, 0, 768, 0), (dz_m, 0, 512, SEG["ga"][0]), (dqr, 0, 512, SEG["qr"][0]),
                                     (dkr, 0, 512, SEG["kr"][0]), (dvr, 0, 512, SEG["vr"][0]),
                                     (dz_m, 512, 2560, SEG["gr"][0])]:
        buf = _dw_in(s["h_t"], piece, col0, width, row0, buf)
    dx, d_norm_g, *recv_in = _in_bwd(dxo, s["x"], p["norm_g"], p["w_in_t"], dz_a, dz_m, dqr, dkr, dvr,
                                     make_ex_in(buf))
    grads = dict(w_in_t=buf, wb_t=d_wb_t, w_out=d_wout, norm_g=d_norm_g, gnw=d_gnw,
                 qn=d_qn.reshape(ATTN_Q_HEADS, ATTN_HEAD_DIM).sum(axis=0),
                 kn=d_kn.reshape(ATTN_KV_HEADS, ATTN_HEAD_DIM).sum(axis=0),
                 lgf=d_lgf[:, 0, 0], lgb=d_lgb[:, 0, 0])
    return dx, grads, recv_attn, recv_in


def _adamw_nd(w, g, m, v):
    shape = w.shape
    two_d = (1, shape[0]) if w.ndim == 1 else (-1, shape[-1])
    out = _adamw(w.reshape(two_d), g.reshape(two_d), m.reshape(two_d), v.reshape(two_d))
    return tuple(o.reshape(shape) for o in out)


def kernel(x, norm_g, w_in, attn_q_norm, attn_k_norm, ret_decay_fwd, ret_decay_bwd, ret_gn_w, w_branch_attn, w_branch_ret, w_out, final_norm_g, loss_target, m_norm_g, m_w_in, m_attn_q_norm, m_attn_k_norm, m_ret_decay_fwd, m_ret_decay_bwd, m_ret_gn_w, m_w_branch_attn, m_w_branch_ret, m_w_out, m_final_norm_g, v_norm_g, v_w_in, v_attn_q_norm, v_attn_k_norm, v_ret_decay_fwd, v_ret_decay_bwd, v_ret_gn_w, v_w_branch_attn, v_w_branch_ret, v_w_out, v_final_norm_g):
    t, d = x.shape[1], x.shape[2]
    x2, target = x[0], loss_target[0]

    w_in_sh, wb_sh, wout_sh = [], [], []
    for l in range(DEPTH):
        w_in_sh.append(jnp.swapaxes(w_in[l], 0, 1).astype(BF16))
        wb_sh.append(jnp.concatenate([w_branch_attn[l].T, w_branch_ret[l].T], axis=1).astype(BF16))
        wout_sh.append(w_out[l].astype(BF16))

    ca, sa = _rope_tables(t, ATTN_HEAD_DIM)
    cr, sr = _rope_tables(t, RET_HEAD_DIM)
    grp = jnp.arange(ATTN_WIDTH) // ATTN_HEAD_DIM
    tabs = dict(ca=jnp.tile(ca, (1, 2)), sa=jnp.tile(sa, (1, 2)), cr=cr, sr=sr,
                ones=jnp.where(grp[:, None] == grp[None, :], 1.0 / ATTN_HEAD_DIM, 0.0).astype(BF16))
    layers = []
    for l in range(DEPTH):
        layers.append(dict(
            norm_g=norm_g[l][None], qn=jnp.tile(attn_q_norm[l], ATTN_Q_HEADS)[None],
            kn=jnp.tile(attn_k_norm[l], ATTN_KV_HEADS)[None], gnw=ret_gn_w[l][None],
            lgf=jax.nn.log_sigmoid(ret_decay_fwd[l]), lgb=jax.nn.log_sigmoid(ret_decay_bwd[l])))

    layers[0]["w_in_t"], = _all_gather([w_in_sh[0]])
    gathers = [_Exchange("gather", [wb_sh[0], wout_sh[0], w_in_sh[1]]), _Exchange("gather", [wb_sh[1], wout_sh[1]])]
    h = x2
    saved = []
    for l in range(DEPTH):
        p = layers[l]
        z, h_t, q, k, v, kt, lse, oa, qrot, krot, vb, orr, on, got = _layer_fwd(h, p, tabs, gathers[l])
        p["wb_t"], p["w_out"] = got[0], got[1]
        if l == 0:
            layers[1]["w_in_t"] = got[2]
        xn, ya, yb = _merge_fwd(h, z, oa, on, p["wb_t"], p["w_out"])
        saved.append(dict(x=h, z=z, h_t=h_t, q=q, k=k, v=v, kt=kt, lse=lse, oa=oa, qrot=qrot, krot=krot, vb=vb,
                          orr=orr, on=on, ya=ya, yb=yb))
        h = xn
    dx, d_final_g, loss_part = _final_loss(h, final_norm_g[None], target)

    grads = [None] * DEPTH
    none = lambda *a: None
    dx, grads[1], _, _ = _layer_bwd(dx, saved[1], layers[1], tabs, none, none)
    g1 = grads[1]
    ex_attn = lambda d_wb_t, d_wout: _Exchange("scatter", [g1["w_in_t"], g1["wb_t"], g1["w_out"], d_wb_t, d_wout])
    ex_in = lambda d_w_in_t: _Exchange("scatter", [d_w_in_t])
    dx, grads[0], recv_attn, recv_in = _layer_bwd(dx, saved[0], layers[0], tabs, ex_attn, ex_in)
    recv = [recv_in[0], recv_attn[3], recv_attn[4], recv_attn[0], recv_attn[1], recv_attn[2]]
    summed = [_sum_slots(r) for r in recv]
    g_w_in = jnp.stack([summed[3 * l].T for l in range(DEPTH)])
    g_wba = jnp.stack([summed[3 * l + 1][:, :512].T for l in range(DEPTH)])
    g_wbr = jnp.stack([summed[3 * l + 1][:, 512:].T for l in range(DEPTH)])
    g_wout = jnp.stack([summed[3 * l + 2] for l in range(DEPTH)])

    packed = jnp.zeros((8, 1024), F32)
    for l in range(DEPTH):
        gl = grads[l]
        packed = packed.at[l].set(gl["norm_g"][0])
        packed = packed.at[2, 512 * l:512 * (l + 1)].set(gl["gnw"][0])
        packed = packed.at[4, 128 * l:128 * l + 64].set(gl["qn"])
        packed = packed.at[4, 256 + 128 * l:256 + 128 * l + 64].set(gl["kn"])
        packed = packed.at[4, 512 + 128 * l:512 + 128 * l + 4].set(gl["lgf"])
        packed = packed.at[4, 768 + 128 * l:768 + 128 * l + 4].set(gl["lgb"])
    packed = packed.at[3].set(d_final_g[0])
    packed = packed.at[5, 0].set(loss_part[0, 0])
    red = _all_reduce_small(packed)
    loss = red[5, 0]
    g_norm_g = red[0:2]
    g_gnw = red[2].reshape(DEPTH, RET_WIDTH)
    g_final = red[3]
    g_qn = jnp.stack([red[4, 128 * l:128 * l + 64] for l in range(DEPTH)])
    g_kn = jnp.stack([red[4, 256 + 128 * l:256 + 128 * l + 64] for l in range(DEPTH)])
    g_lgf = jnp.stack([red[4, 512 + 128 * l:512 + 128 * l + 4] for l in range(DEPTH)])
    g_lgb = jnp.stack([red[4, 768 + 128 * l:768 + 128 * l + 4] for l in range(DEPTH)])
    g_df = g_lgf * jax.nn.sigmoid(-ret_decay_fwd)
    g_db = g_lgb * jax.nn.sigmoid(-ret_decay_bwd)

    grad_w = [g_norm_g, g_w_in, g_qn, g_kn, g_df, g_db, g_gnw, g_wba, g_wbr, g_wout, g_final]
    weights = [norm_g, w_in, attn_q_norm, attn_k_norm, ret_decay_fwd, ret_decay_bwd, ret_gn_w, w_branch_attn,
               w_branch_ret, w_out, final_norm_g]
    ms = [m_norm_g, m_w_in, m_attn_q_norm, m_attn_k_norm, m_ret_decay_fwd, m_ret_decay_bwd, m_ret_gn_w,
          m_w_branch_attn, m_w_branch_ret, m_w_out, m_final_norm_g]
    vs = [v_norm_g, v_w_in, v_attn_q_norm, v_attn_k_norm, v_ret_decay_fwd, v_ret_decay_bwd, v_ret_gn_w,
          v_w_branch_attn, v_w_branch_ret, v_w_out, v_final_norm_g]
    upd = [_adamw_nd(w, g, m, v) for w, g, m, v in zip(weights, grad_w, ms, vs)]
    return (loss, dx[None], *grad_w, *[u[0] for u in upd], *[u[1] for u in upd], *[u[2] for u in upd])
```

```python
import functools

import jax
import jax.numpy as jnp
from jax import lax
from jax.experimental import pallas as pl
from jax.experimental.pallas import tpu as pltpu

F32 = jnp.float32
BF16 = jnp.bfloat16
SDS = jax.ShapeDtypeStruct

D_MODEL = 1024
DEPTH = 2
GRID_W = 64
ATTN_Q_HEADS = 8
ATTN_KV_HEADS = 2
ATTN_HEAD_DIM = 64
ATTN_WIDTH = 512
ATTN_KV_WIDTH = 128
RET_HEADS = 4
RET_HEAD_DIM = 128
RET_WIDTH = 512
RET_CHUNK = 128
ATTN_KEY_CHUNK = 512
ROPE_THETA = 10000.0
EPS = 1e-6
D_IN = 5376
N_DEV = 8

ADAM_LR = 0.001
ADAM_B1 = 0.9
ADAM_B2 = 0.999
ADAM_EPS = 1e-08
ADAM_WD = 0.01
ADAM_STEP = 10

SEG = {
    "qa": (0, 512, 0),
    "ga": (768, 512, 512),
    "qr": (1280, 512, 1024),
    "kr": (1792, 512, 1536),
    "vr": (2304, 512, 2048),
    "gr": (2816, 512, 2560),
    "gm": (3328, 2048, 3072),
    "ka": (512, 128, 5120),
    "va": (640, 128, 5248),
}

VMEM_LIMIT = 60 * 1024 * 1024
NT = (((1,), (1,)), ((), ()))
TN = (((0,), (0,)), ((), ()))
MESH_ID = pl.DeviceIdType.MESH
ANY = pl.BlockSpec(memory_space=pl.ANY)


def _params(sem=None, vmem=VMEM_LIMIT):
    return pltpu.CompilerParams(dimension_semantics=sem, vmem_limit_bytes=vmem)


def _dot(a, b, dims=None):
    if dims is None:
        return jnp.dot(a, b, preferred_element_type=F32)
    return lax.dot_general(a, b, dims, preferred_element_type=F32)


def _sigmoid(x):
    return 1.0 / (1.0 + jnp.exp(-x))


def _swap_halves(x, q):
    n = x.shape[-1]
    axis = x.ndim - 1
    lane = lax.broadcasted_iota(jnp.int32, x.shape, axis)
    first = (lane % (2 * q)) < q
    return jnp.where(first, pltpu.roll(x, n - q, axis), pltpu.roll(x, q, axis))


def _rope(x, cos, sin_signed, q):
    return x * cos + _swap_halves(x, q) * sin_signed


def _rope_bwd(dy, cos, sin_signed, q):
    return dy * cos - _swap_halves(dy, q) * sin_signed


def _group_mean(v, ones_bd):
    hi = v.astype(BF16)
    r1 = v - hi.astype(F32)
    mid = r1.astype(BF16)
    lo = (r1 - mid.astype(F32)).astype(BF16)
    return _dot(hi, ones_bd) + _dot(mid, ones_bd) + _dot(lo, ones_bd)


def _rope_tables(t, head_dim):
    n_rows = t // GRID_W
    row = jnp.repeat(jnp.arange(n_rows, dtype=F32), GRID_W)
    col = jnp.tile(jnp.arange(GRID_W, dtype=F32), n_rows)
    d_axis = head_dim // 2
    inv_freq = ROPE_THETA ** (-jnp.arange(0, d_axis, 2, dtype=F32) / d_axis)
    ar = row[:, None] * inv_freq
    ac = col[:, None] * inv_freq
    cr, sr, cc, sc = jnp.cos(ar), jnp.sin(ar), jnp.cos(ac), jnp.sin(ac)
    return jnp.concatenate([cr, cr, cc, cc], axis=-1), jnp.concatenate([-sr, sr, -sc, sc], axis=-1)


def _me():
    return lax.axis_index("x"), lax.axis_index("y"), lax.axis_index("c")


def _flip(k):
    x, y, c = _me()
    px = 1 - x if k & 4 else x
    py = 1 - y if k & 2 else y
    pc = 1 - c if k & 1 else c
    return (px, py, pc), 4 * px + 2 * py + pc


class _Exchange:
    def __init__(self, kind, srcs):
        self.kind, self.srcs, self.n = kind, list(srcs), len(srcs)
        self.rows = [a.shape[0] if kind == "gather" else a.shape[0] // N_DEV for a in srcs]
        if kind == "gather":
            self.out_shape = [SDS((N_DEV * a.shape[0], a.shape[1]), a.dtype) for a in srcs]
        else:
            self.out_shape = [SDS((N_DEV, a.shape[0] // N_DEV, a.shape[1]), a.dtype) for a in srcs]
        self.scratch = [pltpu.SemaphoreType.DMA((self.n, N_DEV - 1)), pltpu.SemaphoreType.DMA((self.n, N_DEV - 1)),
                        pltpu.SemaphoreType.DMA((self.n,))]

    def _block(self, ref, a, idx):
        r = self.rows[a]
        return ref.at[pl.ds(pl.multiple_of(idx * r, 16), r), :]

    def _src(self, ins, a, idx):
        return ins[a] if self.kind == "gather" else self._block(ins[a], a, idx)

    def _dst(self, outs, a, idx):
        return self._block(outs[a], a, idx) if self.kind == "gather" else outs[a].at[idx]

    def _copies(self, ins, outs, sems):
        send_sems, recv_sems, local_sems = sems
        me, mine = _flip(0)
        local, sends, recvs = [], [], []
        for a in range(self.n):
            local.append(pltpu.make_async_copy(self._src(ins, a, mine), self._dst(outs, a, mine), local_sems.at[a]))
            for k in range(1, N_DEV):
                peer, theirs = _flip(k)
                sem = dict(send_sem=send_sems.at[a, k - 1], recv_sem=recv_sems.at[a, k - 1])
                sends.append(pltpu.make_async_remote_copy(
                    src_ref=self._src(ins, a, theirs), dst_ref=self._dst(outs, a, mine),
                    device_id=peer, device_id_type=MESH_ID, **sem))
                recvs.append(pltpu.make_async_remote_copy(
                    src_ref=self._dst(outs, a, theirs), dst_ref=self._dst(outs, a, theirs),
                    device_id=me, device_id_type=MESH_ID, **sem))
        return local, sends, recvs

    def start(self, ins, outs, sems):
        local, sends, _ = self._copies(ins, outs, sems)
        for cp in local + sends:
            cp.start()

    def wait(self, ins, outs, sems):
        local, sends, recvs = self._copies(ins, outs, sems)
        for cp in sends:
            cp.wait_send()
        for cp in recvs:
            cp.wait_recv()
        for cp in local:
            cp.wait()


def _with_exchange(body, n_in, n_out, n_scratch, ex, first, last):
    if ex is None:
        return body

    def wrapped(*refs):
        ins = refs[:n_in]
        ex_ins = refs[n_in:n_in + ex.n]
        outs = refs[n_in + ex.n:n_in + ex.n + n_out]
        ex_outs = refs[n_in + ex.n + n_out:n_in + 2 * ex.n + n_out]
        rest = refs[n_in + 2 * ex.n + n_out:]
        scratch, sems = rest[:n_scratch], rest[n_scratch:]

        @pl.when(first())
        def _():
            ex.start(ex_ins, ex_outs, sems)

        body(*ins, *outs, *scratch)

        @pl.when(last())
        def _():
            ex.wait(ex_ins, ex_outs, sems)

    return wrapped


def _ex_args(ex):
    if ex is None:
        return [], [], [], [], []
    return [ANY] * ex.n, [ANY] * ex.n, list(ex.out_shape), list(ex.scratch), list(ex.srcs)


def _in_proj(x, g, w_t):
    t, d = x.shape
    tm = min(256, t)

    def body(x_ref, g_ref, w_ref, z_ref, ht_ref):
        xv = x_ref[...]
        r = lax.rsqrt(jnp.mean(xv * xv, axis=-1, keepdims=True) + EPS)
        h = xv * r * g_ref[...]
        ht_ref[...] = h.T.astype(BF16)
        hb = h.astype(BF16)
        for nat, w, off in SEG.values():
            z_ref[:, off:off + w] = _dot(hb, w_ref[nat:nat + w, :], NT)

    return pl.pallas_call(
        body, name="in_proj", grid=(t // tm,),
        in_specs=[pl.BlockSpec((tm, d), lambda i: (i, 0)), pl.BlockSpec((1, d), lambda i: (0, 0)),
                  pl.BlockSpec((D_IN, d), lambda i: (0, 0))],
        out_specs=[pl.BlockSpec((tm, D_IN), lambda i: (i, 0)), pl.BlockSpec((d, tm), lambda i: (0, i))],
        out_shape=[SDS((t, D_IN), F32), SDS((d, t), BF16)],
        compiler_params=_params(("parallel",)),
    )(x, g, w_t)


def _attn_prep(z, qn, kn, cos, sin, ones_bd):
    t = z.shape[0]
    tm = min(ATTN_KEY_CHUNK, t)
    hd = ATTN_HEAD_DIM

    def body(zq_ref, zkv_ref, qn_ref, kn_ref, c_ref, s_ref, b_ref, q_out, k_out, v_out, kt_out, vt_out):
        bd = b_ref[...]
        c2, s2 = c_ref[...], s_ref[...]
        cq = jnp.concatenate([c2] * 4, axis=-1)
        sq = jnp.concatenate([s2] * 4, axis=-1)
        xq = zq_ref[...]
        yq = xq * lax.rsqrt(_group_mean(xq * xq, bd) + EPS) * qn_ref[...]
        yq = _rope(yq, cq, sq, hd // 4) * (hd ** -0.5)
        for h in range(ATTN_Q_HEADS):
            q_out[h] = yq[:, h * hd:(h + 1) * hd].astype(BF16)
        zkv = zkv_ref[...]
        xk, xv = zkv[:, :ATTN_KV_WIDTH], zkv[:, ATTN_KV_WIDTH:]
        yk = xk * lax.rsqrt(_group_mean(xk * xk, bd[:ATTN_KV_WIDTH, :ATTN_KV_WIDTH]) + EPS) * kn_ref[...]
        yk = _rope(yk, c2, s2, hd // 4)
        ykt, xvt = yk.T, xv.T
        ones = jnp.ones((hd, tm), F32)
        for h in range(ATTN_KV_HEADS):
            k_out[h] = yk[:, h * hd:(h + 1) * hd].astype(BF16)
            v_out[h] = xv[:, h * hd:(h + 1) * hd].astype(BF16)
            kt_out[h, 0] = ykt[h * hd:(h + 1) * hd, :].astype(BF16)
            vt_out[h, 0] = jnp.concatenate([xvt[h * hd:(h + 1) * hd, :], ones], axis=0).astype(BF16)

    kv_blk = SEG["ka"][2] // 256
    nk = t // tm
    return pl.pallas_call(
        body, name="attn_prep", grid=(nk,),
        in_specs=[pl.BlockSpec((tm, 512), lambda i: (i, 0)), pl.BlockSpec((tm, 256), lambda i: (i, kv_blk)),
                  pl.BlockSpec((1, 512), lambda i: (0, 0)), pl.BlockSpec((1, 128), lambda i: (0, 0)),
                  pl.BlockSpec((tm, 128), lambda i: (i, 0)), pl.BlockSpec((tm, 128), lambda i: (i, 0)),
                  pl.BlockSpec((512, 512), lambda i: (0, 0))],
        out_specs=[pl.BlockSpec((ATTN_Q_HEADS, tm, hd), lambda i: (0, i, 0)),
                   pl.BlockSpec((ATTN_KV_HEADS, tm, hd), lambda i: (0, i, 0)),
                   pl.BlockSpec((ATTN_KV_HEADS, tm, hd), lambda i: (0, i, 0)),
                   pl.BlockSpec((ATTN_KV_HEADS, 1, hd, tm), lambda i: (0, i, 0, 0)),
                   pl.BlockSpec((ATTN_KV_HEADS, 1, 2 * hd, tm), lambda i: (0, i, 0, 0))],
        out_shape=[SDS((ATTN_Q_HEADS, t, hd), BF16), SDS((ATTN_KV_HEADS, t, hd), BF16),
                   SDS((ATTN_KV_HEADS, t, hd), BF16), SDS((ATTN_KV_HEADS, nk, hd, tm), BF16),
                   SDS((ATTN_KV_HEADS, nk, 2 * hd, tm), BF16)],
        compiler_params=_params(("parallel",)),
    )(z, z, qn, kn, cos, sin, ones_bd)


def _attn_fwd(q, k, vt, ex=None):
    t = q.shape[1]
    tq = min(256, t)
    nk, tk = vt.shape[1], vt.shape[3]
    hd = ATTN_HEAD_DIM
    g = ATTN_Q_HEADS // ATTN_KV_HEADS

    def body(q_ref, k_ref, vt_ref, o_ref, lse_ref, s_scr):
        def pass_a(h, c, m8):
            half = tk // 2
            for lo in (c * tk, c * tk + half):
                st = _dot(k_ref[0, lo:lo + half, :], q_ref[h], NT)
                s_scr[h % 2, lo:lo + half, :] = st
                m8 = jnp.maximum(m8, jnp.max(st.reshape(half // 8, 8, tq), axis=0))
            return m8

        def pass_b(h, c, m, acc):
            e = jnp.exp(s_scr[h % 2, c * tk:(c + 1) * tk, :] - m).astype(BF16)
            return acc + _dot(vt_ref[0, c], e)

        neg = jnp.full((8, tq), -jnp.inf, F32)
        m8 = neg
        for c in range(nk):
            m8 = pass_a(0, c, m8)
        outs = []
        for h in range(g):
            m = jnp.max(m8, axis=0, keepdims=True)
            acc = jnp.zeros((2 * hd, tq), F32)
            m8 = neg
            for c in range(nk):
                if h + 1 < g:
                    m8 = pass_a(h + 1, c, m8)
                acc = pass_b(h, c, m, acc)
            l = acc[hd:hd + 1, :]
            outs.append((acc[:hd, :] / l).T)
            lse_ref[h] = m + jnp.log(l)
        o_ref[...] = jnp.concatenate(outs, axis=-1)

    nq = t // tq
    first = lambda: jnp.logical_and(pl.program_id(0) == 0, pl.program_id(1) == 0)
    last = lambda: jnp.logical_and(pl.program_id(0) == ATTN_KV_HEADS - 1, pl.program_id(1) == nq - 1)
    xi, xo, xs, xscr, xargs = _ex_args(ex)
    return pl.pallas_call(
        _with_exchange(body, 3, 2, 1, ex, first, last), name="attn_fwd", grid=(ATTN_KV_HEADS, nq),
        in_specs=[pl.BlockSpec((g, tq, hd), lambda p, i: (p, i, 0)),
                  pl.BlockSpec((1, t, hd), lambda p, i: (p, 0, 0)),
                  pl.BlockSpec((1, nk, 2 * hd, tk), lambda p, i: (p, 0, 0, 0))] + xi,
        out_specs=[pl.BlockSpec((tq, g * hd), lambda p, i: (i, p)),
                   pl.BlockSpec((g, 1, tq), lambda p, i: (p, 0, i))] + xo,
        out_shape=[SDS((t, ATTN_WIDTH), F32), SDS((ATTN_Q_HEADS, 1, t), F32)] + xs,
        scratch_shapes=[pltpu.VMEM((2, t, tq), F32)] + xscr,
        compiler_params=_params(("arbitrary", "arbitrary")),
    )(q, k, vt, *xargs)


class _Dir:
    def __init__(self, lg, strict_future):
        c = RET_CHUNK
        ia = lax.broadcasted_iota(jnp.int32, (c, c), 0).astype(F32)
        ib = lax.broadcasted_iota(jnp.int32, (c, c), 1).astype(F32)
        col = lax.broadcasted_iota(jnp.int32, (c, 1), 0).astype(F32)
        row = lax.broadcasted_iota(jnp.int32, (1, c), 1).astype(F32)
        if strict_future:
            dist = ib - ia
            mask = dist > 0
            self.wq, self.wk, wk_row = c - col, col, row
        else:
            dist = ia - ib
            mask = dist >= 0
            self.wq, self.wk, wk_row = col + 1.0, c - 1.0 - col, c - 1.0 - row
        self.dist = jnp.maximum(dist, 0.0)
        self.d = jnp.where(mask, jnp.exp(self.dist * lg), 0.0)
        self.qd = jnp.exp(self.wq * lg)
        self.kd_col = jnp.exp(self.wk * lg)
        self.kd_row = jnp.exp(wk_row * lg)
        self.cd = jnp.exp(jnp.full((1, 1), float(c), F32) * lg)


def _ret_fwd(z, lgf, lgb, gnw, cos, sin):
    t = z.shape[0]
    c = RET_CHUNK
    nc = t // c
    hd = RET_HEAD_DIM
    unroll = 4 if nc % 4 == 0 else 1

    def body(lgf_ref, lgb_ref, q_ref, k_ref, v_ref, c_ref, s_ref, w_ref,
             qo_ref, ko_ref, vo_ref, orr_ref, on_ref, kt, uf, ub, sfa, sba):
        h = pl.program_id(0)
        fw = _Dir(lgf_ref[h], False)
        bw = _Dir(lgb_ref[h], True)
        cc, ss = c_ref[...], s_ref[...]
        qo_ref[...] = _rope(q_ref[...], cc, ss, hd // 4).astype(BF16)
        kr = _rope(k_ref[...], cc, ss, hd // 4) * (hd ** -0.5)
        ko_ref[...] = kr.astype(BF16)
        vo_ref[...] = v_ref[...].astype(BF16)
        for i in range(nc):
            kt[i] = kr[i * c:(i + 1) * c, :].T.astype(BF16)

        def rows(ci):
            return pl.ds(pl.multiple_of(ci * c, c), c)

        def kv_products(ci, carry):
            vv = vo_ref[rows(ci), :]
            ktf = kt[ci].astype(F32)
            uf[ci] = _dot((ktf * fw.kd_row).astype(BF16), vv)
            ub[ci] = _dot((ktf * bw.kd_row).astype(BF16), vv)
            return carry

        lax.fori_loop(0, nc, kv_products, 0, unroll=unroll)

        def scan(i, carry):
            sf, sb = carry
            j = nc - 1 - i
            sfa[i] = sf.astype(BF16)
            sba[j] = sb.astype(BF16)
            return sf * fw.cd + uf[i], sb * bw.cd + ub[j]

        zero = jnp.zeros((hd, hd), F32)
        lax.fori_loop(0, nc, scan, (zero, zero))
        gw = w_ref[...]

        def outputs(ci, carry):
            sl = rows(ci)
            qq, kk, vv = qo_ref[sl, :], ko_ref[sl, :], vo_ref[sl, :]
            a = _dot(qq, kk, NT)
            o = (_dot((a * fw.d).astype(BF16), vv) + _dot(qq, sfa[ci]) * fw.qd
                 + _dot((a * bw.d).astype(BF16), vv) + _dot(qq, sba[ci]) * bw.qd)
            orr_ref[sl, :] = o
            xc = o - jnp.mean(o, axis=-1, keepdims=True)
            var = jnp.mean(xc * xc, axis=-1, keepdims=True)
            on_ref[sl, :] = xc * lax.rsqrt(var + EPS) * gw
            return carry

        lax.fori_loop(0, nc, outputs, 0, unroll=unroll)

    smem = pl.BlockSpec(memory_space=pltpu.SMEM)
    col = lambda name: (lambda h: (0, SEG[name][2] // 128 + h))
    head = pl.BlockSpec((t, 128), lambda h: (0, h))
    full = pl.BlockSpec((t, 128), lambda h: (0, 0))
    return pl.pallas_call(
        body, name="ret_fwd", grid=(RET_HEADS,),
        in_specs=[smem, smem, pl.BlockSpec((t, 128), col("qr")), pl.BlockSpec((t, 128), col("kr")),
                  pl.BlockSpec((t, 128), col("vr")), full, full, pl.BlockSpec((1, 128), lambda h: (0, h))],
        out_specs=[head, head, head, head, head],
        out_shape=[SDS((t, RET_WIDTH), BF16)] * 3 + [SDS((t, RET_WIDTH), F32)] * 2,
        scratch_shapes=[pltpu.VMEM((nc, hd, c), BF16), pltpu.VMEM((nc, hd, hd), F32), pltpu.VMEM((nc, hd, hd), F32),
                        pltpu.VMEM((nc, hd, hd), BF16), pltpu.VMEM((nc, hd, hd), BF16)],
        compiler_params=_params(("parallel",)),
    )(lgf, lgb, z, z, z, cos, sin, gnw)


def _merge_fwd(x, z, oa, on, wb_t, wout):
    t, d = x.shape
    tm = min(256, t)

    def body(x_ref, ga_ref, gr_ref, gm0_ref, gm1_ref, oa_ref, on_ref, wb_ref, wo_ref, xn_ref, ya_ref, yb_ref):
        ga, gr = ga_ref[...], gr_ref[...]
        ua = ga * _sigmoid(ga) * oa_ref[...]
        ub = gr * _sigmoid(gr) * on_ref[...]
        ya = _dot(ua.astype(BF16), wb_ref[:, :512], NT)
        yb = _dot(ub.astype(BF16), wb_ref[:, 512:], NT)
        ya_ref[...] = ya
        yb_ref[...] = yb
        merged = _sigmoid(gm0_ref[...]) * ya + _sigmoid(gm1_ref[...]) * yb
        xn_ref[...] = x_ref[...] + _dot(merged.astype(BF16), wo_ref[...])

    row = lambda w, j: pl.BlockSpec((tm, w), lambda i: (i, j))
    const = lambda shape: pl.BlockSpec(shape, lambda i: (0, 0))
    return pl.pallas_call(
        body, name="merge_fwd", grid=(t // tm,),
        in_specs=[row(d, 0), row(512, SEG["ga"][2] // 512), row(512, SEG["gr"][2] // 512),
                  row(1024, SEG["gm"][2] // 1024), row(1024, SEG["gm"][2] // 1024 + 1),
                  row(512, 0), row(512, 0), const((d, 1024)), const((d, d))],
        out_specs=[row(d, 0), row(d, 0), row(d, 0)],
        out_shape=[SDS((t, d), F32)] * 3,
        compiler_params=_params(("parallel",)),
    )(x, z, z, z, z, oa, on, wb_t, wout)


def _final_loss(x, g, target):
    t, d = x.shape
    tm = min(512, t)
    n = t // tm

    def body(x_ref, g_ref, t_ref, dx_ref, dg_ref, loss_ref, acc_g, acc_l):
        i = pl.program_id(0)

        @pl.when(i == 0)
        def _():
            acc_g[...] = jnp.zeros_like(acc_g)
            acc_l[...] = jnp.zeros_like(acc_l)

        xv, gv = x_ref[...], g_ref[...]
        r = lax.rsqrt(jnp.mean(xv * xv, axis=-1, keepdims=True) + EPS)
        xh = xv * r
        err = xh * gv - t_ref[...]
        dy = err * (1.0 / d)
        gy = dy * gv
        dx_ref[...] = r * (gy - xh * jnp.mean(gy * xh, axis=-1, keepdims=True))
        acc_g[...] += jnp.sum((dy * xh).reshape(tm // 8, 8, d), axis=0)
        acc_l[...] += jnp.sum((err * err).reshape(tm // 8, 8, d), axis=0)

        @pl.when(i == n - 1)
        def _():
            dg_ref[...] = jnp.sum(acc_g[...], axis=0, keepdims=True)
            tot = jnp.sum(jnp.sum(acc_l[...], axis=0, keepdims=True), axis=1, keepdims=True)
            loss_ref[...] = jnp.broadcast_to(tot * (0.5 / d), (1, 128))

    return pl.pallas_call(
        body, name="final_loss", grid=(n,),
        in_specs=[pl.BlockSpec((tm, d), lambda i: (i, 0)), pl.BlockSpec((1, d), lambda i: (0, 0)),
                  pl.BlockSpec((tm, d), lambda i: (i, 0))],
        out_specs=[pl.BlockSpec((tm, d), lambda i: (i, 0)), pl.BlockSpec((1, d), lambda i: (0, 0)),
                   pl.BlockSpec((1, 128), lambda i: (0, 0))],
        out_shape=[SDS((t, d), F32), SDS((1, d), F32), SDS((1, 128), F32)],
        scratch_shapes=[pltpu.VMEM((8, d), F32), pltpu.VMEM((8, d), F32)],
        compiler_params=_params(("arbitrary",)),
    )(x, g, target)


def _merge_bwd(dxo, z, oa, on, ya, yb, wb_t, wout):
    t, d = dxo.shape
    tm = min(256, t)
    n = t // tm

    def body(dx_ref, ga_ref, gr_ref, gm0_ref, gm1_ref, oa_ref, on_ref, ya_ref, yb_ref, wb_ref, wo_ref,
             doa_ref, don_ref, dz_ref, dwo_ref, dwb_ref, acc_o, acc_b):
        i = pl.program_id(0)

        @pl.when(i == 0)
        def _():
            acc_o[...] = jnp.zeros_like(acc_o)
            acc_b[...] = jnp.zeros_like(acc_b)

        dxb = dx_ref[...].astype(BF16)
        ya, yb = ya_ref[...], yb_ref[...]
        g0, g1 = _sigmoid(gm0_ref[...]), _sigmoid(gm1_ref[...])
        mb = (g0 * ya + g1 * yb).astype(BF16)
        dm = _dot(dxb, wo_ref[...], NT)
        dya = (dm * g0).astype(BF16)
        dyb = (dm * g1).astype(BF16)
        dz_ref[:, 1024:2048] = (dm * ya * g0 * (1.0 - g0)).astype(BF16)
        dz_ref[:, 2048:3072] = (dm * yb * g1 * (1.0 - g1)).astype(BF16)

        def branch(g_ref, o_ref, dy, w, do_ref, lo):
            gv, ov = g_ref[...], o_ref[...]
            sg = _sigmoid(gv)
            silu = gv * sg
            du = _dot(dy, w)
            do_ref[...] = du * silu
            dz_ref[:, lo:lo + 512] = (du * ov * (sg * (1.0 + gv * (1.0 - sg)))).astype(BF16)
            acc_b[:, lo:lo + 512] += _dot(dy, (silu * ov).astype(BF16), TN)

        branch(ga_ref, oa_ref, dya, wb_ref[:, :512], doa_ref, 0)
        branch(gr_ref, on_ref, dyb, wb_ref[:, 512:], don_ref, 512)
        acc_o[...] += _dot(mb, dxb, TN)

        @pl.when(i == n - 1)
        def _():
            dwo_ref[...] = acc_o[...].astype(BF16)
            dwb_ref[...] = acc_b[...].astype(BF16)

    row = lambda w, j: pl.BlockSpec((tm, w), lambda i: (i, j))
    const = lambda shape: pl.BlockSpec(shape, lambda i: (0, 0))
    return pl.pallas_call(
        body, name="merge_bwd", grid=(n,),
        in_specs=[row(d, 0), row(512, SEG["ga"][2] // 512), row(512, SEG["gr"][2] // 512),
                  row(1024, SEG["gm"][2] // 1024), row(1024, SEG["gm"][2] // 1024 + 1),
                  row(512, 0), row(512, 0), row(d, 0), row(d, 0), const((d, 1024)), const((d, d))],
        out_specs=[row(512, 0), row(512, 0), row(3072, 0), const((d, d)), const((d, 1024))],
        out_shape=[SDS((t, 512), F32), SDS((t, 512), F32), SDS((t, 3072), BF16), SDS((d, d), BF16),
                   SDS((d, 1024), BF16)],
        scratch_shapes=[pltpu.VMEM((d, d), F32), pltpu.VMEM((d, 1024), F32)],
        compiler_params=_params(("arbitrary",)),
    )(dxo, z, z, z, z, oa, on, ya, yb, wb_t, wout)


def _ret_bwd(qrot, krot, vb, orr, don, gnw, lgf, lgb):
    t = qrot.shape[0]
    c = RET_CHUNK
    nc = t // c
    hd = RET_HEAD_DIM
    unroll = 2 if nc % 2 == 0 else 1

    def body(lgf_ref, lgb_ref, q_ref, k_ref, v_ref, o_ref, dn_ref, w_ref,
             dq_ref, dk_ref, dv_ref, dw_ref, dlf_ref, dlb_ref, qt, kt, dob, uf, ub, wf, wb, sfa, sba, gfa, gba):
        h = pl.program_id(0)
        fw = _Dir(lgf_ref[h], False)
        bw = _Dir(lgb_ref[h], True)
        fw.dt, bw.dt = fw.d.T, bw.d.T

        o = o_ref[...]
        xc = o - jnp.mean(o, axis=-1, keepdims=True)
        r = lax.rsqrt(jnp.mean(xc * xc, axis=-1, keepdims=True) + EPS)
        xh = xc * r
        dn = dn_ref[...]
        gy = dn * w_ref[...]
        d_o = r * (gy - jnp.mean(gy, axis=-1, keepdims=True) - xh * jnp.mean(gy * xh, axis=-1, keepdims=True))
        dw_ref[...] = jnp.sum(dn * xh, axis=0, keepdims=True)
        dob[...] = d_o.astype(BF16)
        for i in range(nc):
            qt[i] = q_ref[i * c:(i + 1) * c, :].astype(F32).T.astype(BF16)
            kt[i] = k_ref[i * c:(i + 1) * c, :].astype(F32).T.astype(BF16)

        def rows(ci):
            return pl.ds(pl.multiple_of(ci * c, c), c)

        def products(ci, carry):
            sl = rows(ci)
            vv, do32 = v_ref[sl, :], dob[sl, :].astype(F32)
            ktf = kt[ci].astype(F32)
            uf[ci] = _dot((ktf * fw.kd_row).astype(BF16), vv)
            ub[ci] = _dot((ktf * bw.kd_row).astype(BF16), vv)
            wf[ci] = _dot(qt[ci], (do32 * fw.qd).astype(BF16))
            wb[ci] = _dot(qt[ci], (do32 * bw.qd).astype(BF16))
            return carry

        lax.fori_loop(0, nc, products, 0, unroll=unroll)

        def scan(i, carry):
            sf, sb, gf, gb = carry
            j = nc - 1 - i
            sfa[i] = sf.astype(BF16)
            sba[j] = sb.astype(BF16)
            gfa[j] = gf.astype(BF16)
            gba[i] = gb.astype(BF16)
            return sf * fw.cd + uf[i], sb * bw.cd + ub[j], gf * fw.cd + wf[j], gb * bw.cd + wb[i]

        zero = jnp.zeros((hd, hd), F32)
        lax.fori_loop(0, nc, scan, (zero, zero, zero, zero))

        def one_dir(p, s_all, g_all, ci, qq, kk, vv, do, a, bm, at, bt):
            sb, gb = s_all[ci], g_all[ci]
            doq = (do.astype(F32) * p.qd).astype(BF16)
            dqc = _dot(doq, sb, NT)
            dq = _dot((bm * p.d).astype(BF16), kk) + dqc
            kkd = (kk.astype(F32) * p.kd_col).astype(BF16)
            dv = _dot((at * p.dt).astype(BF16), do) + _dot(kkd, gb)
            dk2 = _dot(vv, gb, NT) * p.kd_col
            dk = _dot((bt * p.dt).astype(BF16), qq) + dk2
            terms = (p.dist * p.d * a * bm + p.wq * qq.astype(F32) * dqc + p.wk * kk.astype(F32) * dk2
                     + (float(c) * p.cd) * gb.astype(F32) * sb.astype(F32))
            return dq, dk, dv, terms

        def chunk(ci, carry):
            af, ab = carry
            sl = rows(ci)
            qq, kk, vv, do = q_ref[sl, :], k_ref[sl, :], v_ref[sl, :], dob[sl, :]
            a, bm = _dot(qq, kk, NT), _dot(do, vv, NT)
            at, bt = _dot(kk, qq, NT), _dot(vv, do, NT)
            dqf, dkf, dvf, tf = one_dir(fw, sfa, gfa, ci, qq, kk, vv, do, a, bm, at, bt)
            dqb, dkb, dvb, tb = one_dir(bw, sba, gba, ci, qq, kk, vv, do, a, bm, at, bt)
            dq_ref[sl, :] = dqf + dqb
            dk_ref[sl, :] = dkf + dkb
            dv_ref[sl, :] = dvf + dvb
            return af + tf, ab + tb

        af, ab = lax.fori_loop(0, nc, chunk, (zero, zero), unroll=unroll)
        tot = lambda m: jnp.sum(jnp.sum(m, axis=0, keepdims=True), axis=1, keepdims=True)
        dlf_ref[...] = jnp.broadcast_to(tot(af).reshape(1, 1, 1), (1, 8, 128))
        dlb_ref[...] = jnp.broadcast_to(tot(ab).reshape(1, 1, 1), (1, 8, 128))

    smem = pl.BlockSpec(memory_space=pltpu.SMEM)
    head = pl.BlockSpec((t, 128), lambda h: (0, h))
    vec = pl.BlockSpec((1, 128), lambda h: (0, h))
    scal = pl.BlockSpec((1, 8, 128), lambda h: (h, 0, 0))
    mats = lambda dt: pltpu.VMEM((nc, hd, hd), dt)
    return pl.pallas_call(
        body, name="ret_bwd", grid=(RET_HEADS,),
        in_specs=[smem, smem, head, head, head, head, head, vec],
        out_specs=[head, head, head, vec, scal, scal],
        out_shape=[SDS((t, RET_WIDTH), F32)] * 3 + [SDS((1, RET_WIDTH), F32), SDS((RET_HEADS, 8, 128), F32),
                                                   SDS((RET_HEADS, 8, 128), F32)],
        scratch_shapes=[pltpu.VMEM((nc, hd, c), BF16), pltpu.VMEM((nc, hd, c), BF16), pltpu.VMEM((t, hd), BF16),
                        mats(F32), mats(F32), mats(F32), mats(F32), mats(BF16), mats(BF16), mats(BF16), mats(BF16)],
        compiler_params=_params(("parallel",)),
    )(lgf, lgb, qrot, krot, vb, orr, don, gnw)


def _ret_post_bwd(dq, dk, dv, cos, sin):
    t = dq.shape[0]
    tm = min(512, t)
    hd = RET_HEAD_DIM

    def body(dq_ref, dk_ref, dv_ref, c_ref, s_ref, oq_ref, ok_ref, ov_ref):
        cc = jnp.concatenate([c_ref[...]] * 4, axis=-1)
        ss = jnp.concatenate([s_ref[...]] * 4, axis=-1)
        oq_ref[...] = _rope_bwd(dq_ref[...], cc, ss, hd // 4).astype(BF16)
        ok_ref[...] = (_rope_bwd(dk_ref[...], cc, ss, hd // 4) * (hd ** -0.5)).astype(BF16)
        ov_ref[...] = dv_ref[...].astype(BF16)

    blk = pl.BlockSpec((tm, 512), lambda i: (i, 0))
    tab = pl.BlockSpec((tm, 128), lambda i: (i, 0))
    return pl.pallas_call(
        body, name="ret_post_bwd", grid=(t // tm,),
        in_specs=[blk, blk, blk, tab, tab], out_specs=[blk, blk, blk],
        out_shape=[SDS((t, 512), BF16)] * 3,
        compiler_params=_params(("parallel",)),
    )(dq, dk, dv, cos, sin)


def _attn_bwd(q, k, kt, v, doa, oa, lse, ex=None):
    t = q.shape[1]
    tq = min(256, t)
    nq = t // tq
    nk, tk = kt.shape[1], kt.shape[3]
    hd = ATTN_HEAD_DIM
    scale = hd ** -0.5

    def body(q_ref, k_ref, kt_ref, v_ref, do_ref, o_ref, lse_ref, dq_ref, dk_ref, dv_ref):
        p, i = pl.program_id(0), pl.program_id(1)

        @pl.when(jnp.logical_and(p % 2 == 0, i == 0))
        def _():
            dk_ref[...] = jnp.zeros_like(dk_ref)
            dv_ref[...] = jnp.zeros_like(dv_ref)

        dov, ov = do_ref[...], o_ref[...]
        outs = []
        for j in range(2):
            qq = q_ref[j]
            do32 = dov[:, j * hd:(j + 1) * hd]
            do = do32.astype(BF16)
            dd = jnp.sum((do32 * ov[:, j * hd:(j + 1) * hd]).T, axis=0, keepdims=True)
            lse_j = lse_ref[j]

            def step(c, dqt, qq=qq, do=do, dd=dd, lse_j=lse_j):
                sl = pl.ds(pl.multiple_of(c * tk, tk), tk)
                pt = jnp.exp(_dot(k_ref[0, sl, :], qq, NT) - lse_j)
                dpt = _dot(v_ref[0, sl, :], do, NT)
                dst = (pt * (dpt - dd)).astype(BF16)
                dv_ref[0, sl, :] += _dot(pt.astype(BF16), do)
                dk_ref[0, sl, :] += _dot(dst, qq)
                return dqt + _dot(kt_ref[0, c], dst)

            dqt = lax.fori_loop(0, nk, step, jnp.zeros((hd, tq), F32), unroll=True)
            outs.append(dqt.T * scale)
        dq_ref[...] = jnp.concatenate(outs, axis=-1)

    kv = pl.BlockSpec((1, t, hd), lambda p, i: (p // 2, 0, 0))
    pair = pl.BlockSpec((tq, 128), lambda p, i: (i, p))
    first = lambda: jnp.logical_and(pl.program_id(0) == 0, pl.program_id(1) == 0)
    last = lambda: jnp.logical_and(pl.program_id(0) == 3, pl.program_id(1) == nq - 1)
    xi, xo, xs, xscr, xargs = _ex_args(ex)
    return pl.pallas_call(
        _with_exchange(body, 7, 3, 0, ex, first, last), name="attn_bwd", grid=(4, nq),
        in_specs=[pl.BlockSpec((2, tq, hd), lambda p, i: (p, i, 0)), kv,
                  pl.BlockSpec((1, nk, hd, tk), lambda p, i: (p // 2, 0, 0, 0)), kv, pair, pair,
                  pl.BlockSpec((2, 1, tq), lambda p, i: (p, 0, i))] + xi,
        out_specs=[pair, kv, kv] + xo,
        out_shape=[SDS((t, ATTN_WIDTH), F32), SDS((ATTN_KV_HEADS, t, hd), F32),
                   SDS((ATTN_KV_HEADS, t, hd), F32)] + xs,
        scratch_shapes=xscr,
        compiler_params=_params(("arbitrary", "arbitrary")),
    )(q, k, kt, v, doa, oa, lse, *xargs)


def _attn_post_bwd(dq, dk, dv, z, qn, kn, cos, sin, ones_bd):
    t = z.shape[0]
    tm = min(512, t)
    n = t // tm
    hd = ATTN_HEAD_DIM

    def body(dq_ref, dk_ref, dv_ref, zq_ref, zkv_ref, qn_ref, kn_ref, c_ref, s_ref, b_ref,
             dz_ref, dqn_ref, dkn_ref, acc_q, acc_k):
        i = pl.program_id(0)

        @pl.when(i == 0)
        def _():
            acc_q[...] = jnp.zeros_like(acc_q)
            acc_k[...] = jnp.zeros_like(acc_k)

        bd = b_ref[...]
        c2, s2 = c_ref[...], s_ref[...]

        def norm_bwd(dy, x, w, ones, cos_t, sin_t, acc):
            dyr = _rope_bwd(dy, cos_t, sin_t, hd // 4)
            r = lax.rsqrt(_group_mean(x * x, ones) + EPS)
            xh = x * r
            gy = dyr * w
            acc[...] += jnp.sum((dyr * xh).reshape(tm // 8, 8, x.shape[-1]), axis=0)
            return r * (gy - xh * _group_mean(gy * xh, ones))

        cq = jnp.concatenate([c2] * 4, axis=-1)
        sq = jnp.concatenate([s2] * 4, axis=-1)
        dz_ref[:, :512] = norm_bwd(dq_ref[...], zq_ref[...], qn_ref[...], bd, cq, sq, acc_q).astype(BF16)
        zkv = zkv_ref[...]
        dkk = jnp.concatenate([dk_ref[0], dk_ref[1]], axis=-1)
        dz_ref[:, 512:640] = norm_bwd(dkk, zkv[:, :128], kn_ref[...], bd[:128, :128], c2, s2, acc_k).astype(BF16)
        dz_ref[:, 640:768] = jnp.concatenate([dv_ref[0], dv_ref[1]], axis=-1).astype(BF16)

        @pl.when(i == n - 1)
        def _():
            dqn_ref[...] = jnp.sum(acc_q[...], axis=0, keepdims=True)
            dkn_ref[...] = jnp.sum(acc_k[...], axis=0, keepdims=True)

    kv_blk = SEG["ka"][2] // 256
    kvs = pl.BlockSpec((ATTN_KV_HEADS, tm, hd), lambda i: (0, i, 0))
    const = lambda shape: pl.BlockSpec(shape, lambda i: (0, 0))
    return pl.pallas_call(
        body, name="attn_post_bwd", grid=(n,),
        in_specs=[pl.BlockSpec((tm, 512), lambda i: (i, 0)), kvs, kvs,
                  pl.BlockSpec((tm, 512), lambda i: (i, 0)), pl.BlockSpec((tm, 256), lambda i: (i, kv_blk)),
                  const((1, 512)), const((1, 128)),
                  pl.BlockSpec((tm, 128), lambda i: (i, 0)), pl.BlockSpec((tm, 128), lambda i: (i, 0)),
                  const((512, 512))],
        out_specs=[pl.BlockSpec((tm, 768), lambda i: (i, 0)), const((1, 512)), const((1, 128))],
        out_shape=[SDS((t, 768), BF16), SDS((1, 512), F32), SDS((1, 128), F32)],
        scratch_shapes=[pltpu.VMEM((8, 512), F32), pltpu.VMEM((8, 128), F32)],
        compiler_params=_params(("arbitrary",)),
    )(dq, dk, dv, z, z, qn, kn, cos, sin, ones_bd)


def _in_bwd(dxo, x, g, w_t, dz_a, dz_m, dqr, dkr, dvr, ex=None):
    t, d = x.shape
    tm = min(256, t)
    n = t // tm
    parts = [(0, 0, 768, 0), (1, 0, 512, SEG["ga"][0]), (2, 0, 512, SEG["qr"][0]), (3, 0, 512, SEG["kr"][0]),
             (4, 0, 512, SEG["vr"][0]), (1, 512, 2560, SEG["gr"][0])]

    def body(dx_ref, x_ref, g_ref, w_ref, a_ref, m_ref, q_ref, k_ref, v_ref, o_ref, dg_ref, acc):
        i = pl.program_id(0)

        @pl.when(i == 0)
        def _():
            acc[...] = jnp.zeros_like(acc)

        pieces = [a_ref, m_ref, q_ref, k_ref, v_ref]
        dh = jnp.zeros((tm, d), F32)
        for pi, lo, w, row in parts:
            dh = dh + _dot(pieces[pi][:, lo:lo + w], w_ref[row:row + w, :])
        xv = x_ref[...]
        r = lax.rsqrt(jnp.mean(xv * xv, axis=-1, keepdims=True) + EPS)
        xh = xv * r
        gy = dh * g_ref[...]
        o_ref[...] = dx_ref[...] + r * (gy - xh * jnp.mean(gy * xh, axis=-1, keepdims=True))
        acc[...] += jnp.sum((dh * xh).reshape(tm // 8, 8, d), axis=0)

        @pl.when(i == n - 1)
        def _():
            dg_ref[...] = jnp.sum(acc[...], axis=0, keepdims=True)

    row = lambda w: pl.BlockSpec((tm, w), lambda i: (i, 0))
    const = lambda shape: pl.BlockSpec(shape, lambda i: (0, 0))
    xi, xo, xs, xscr, xargs = _ex_args(ex)
    return pl.pallas_call(
        _with_exchange(body, 9, 2, 1, ex, lambda: pl.program_id(0) == 0, lambda: pl.program_id(0) == n - 1),
        name="in_bwd", grid=(n,),
        in_specs=[row(d), row(d), const((1, d)), const((D_IN, d)), row(768), row(3072), row(512), row(512),
                  row(512)] + xi,
        out_specs=[row(d), const((1, d))] + xo,
        out_shape=[SDS((t, d), F32), SDS((1, d), F32)] + xs,
        scratch_shapes=[pltpu.VMEM((8, d), F32)] + xscr,
        compiler_params=_params(("arbitrary",)),
    )(dxo, x, g, w_t, dz_a, dz_m, dqr, dkr, dvr, *xargs)


def _dw_in(h_t, piece, col0, width, row0, buf):
    d, t = h_t.shape
    tn = 256
    c0, r0 = col0 // tn, row0 // tn

    def body(*refs):
        h_ref, p_ref, o_ref = refs[0], refs[1], refs[-1]
        o_ref[...] = _dot(h_ref[...], p_ref[...]).T.astype(BF16)

    in_specs = [pl.BlockSpec((d, t), lambda j: (0, 0)), pl.BlockSpec((t, tn), lambda j: (0, c0 + j))]
    args = [h_t, piece]
    aliases = {}
    if buf is not None:
        in_specs.append(ANY)
        args.append(buf)
        aliases = {2: 0}
    return pl.pallas_call(
        body, name="dw_in", grid=(width // tn,),
        in_specs=in_specs, out_specs=pl.BlockSpec((tn, d), lambda j: (r0 + j, 0)),
        out_shape=SDS((D_IN, d), BF16), input_output_aliases=aliases,
        compiler_params=_params(("parallel",)),
    )(*args)


def _adamw_math(w, g, m, v):
    mn = ADAM_B1 * m + (1.0 - ADAM_B1) * g
    vn = ADAM_B2 * v + (1.0 - ADAM_B2) * (g * g)
    m_hat = mn / (1.0 - ADAM_B1 ** ADAM_STEP)
    v_hat = vn / (1.0 - ADAM_B2 ** ADAM_STEP)
    return -ADAM_LR * (m_hat / (jnp.sqrt(v_hat) + ADAM_EPS) + ADAM_WD * w), mn, vn


def _sum_adamw_w_in(recvs, w, m, v):
    depth, d, cols = w.shape
    tn = 256
    pad = -cols % 128

    def body(r0_ref, r1_ref, w_ref, m_ref, v_ref, g_ref, d_ref, mo_ref, vo_ref):
        def run(r_ref):
            acc = r_ref[0].astype(F32)
            for s in range(1, N_DEV):
                acc = acc + r_ref[s].astype(F32)
            g = jnp.concatenate([acc, jnp.zeros((pad, tn), F32)], axis=0).T[:, :cols]
            g_ref[0] = g
            d_ref[0], mo_ref[0], vo_ref[0] = _adamw_math(w_ref[0], g, m_ref[0], v_ref[0])

        for l, r_ref in enumerate((r0_ref, r1_ref)):
            pl.when(pl.program_id(0) == l)(functools.partial(run, r_ref))

    slots = pl.BlockSpec((N_DEV, cols, tn), lambda l, j: (0, 0, j))
    blk = pl.BlockSpec((1, tn, cols), lambda l, j: (l, j, 0))
    return pl.pallas_call(
        body, name="sum_adamw_w_in", grid=(depth, d // tn),
        in_specs=[slots, slots, blk, blk, blk], out_specs=[blk] * 4, out_shape=[SDS(w.shape, F32)] * 4,
        compiler_params=_params(("parallel", "parallel")),
    )(recvs[0], recvs[1], w, m, v)


def _adamw(w, g, m, v):
    rows, cols = w.shape
    tr = 256 if rows % 256 == 0 else rows

    def body(w_ref, g_ref, m_ref, v_ref, d_ref, mo_ref, vo_ref):
        d_ref[...], mo_ref[...], vo_ref[...] = _adamw_math(w_ref[...], g_ref[...], m_ref[...], v_ref[...])

    blk = pl.BlockSpec((tr, cols), lambda i: (i, 0))
    return pl.pallas_call(
        body, name="adamw", grid=(rows // tr,),
        in_specs=[blk] * 4, out_specs=[blk] * 3, out_shape=[SDS((rows, cols), F32)] * 3,
        compiler_params=_params(("parallel",)),
    )(w, g, m, v)


def _all_gather(shards):
    na = len(shards)
    chips = (4, 2, 6)

    def body(*refs):
        ins, outs = refs[:na], refs[na:2 * na]
        send_sems, recv_sems, local_sems = refs[2 * na:]
        _, mine = _flip(0)

        def rows(a, idx):
            r = shards[a].shape[0]
            return outs[a].at[pl.ds(pl.multiple_of(idx * r, 16), r), :]

        def copy(a, slot, block_idx, to, src=None):
            return pltpu.make_async_remote_copy(
                src_ref=rows(a, block_idx) if src is None else src, dst_ref=rows(a, block_idx),
                send_sem=send_sems.at[a, slot], recv_sem=recv_sems.at[a, slot],
                device_id=to, device_id_type=MESH_ID)

        sibling, sibling_idx = _flip(1)
        local, started = [], []
        for a in range(na):
            cp = pltpu.make_async_copy(ins[a], rows(a, mine), local_sems.at[a])
            cp.start()
            local.append(cp)
            first = [copy(a, 0, mine, sibling, src=ins[a])]
            first += [copy(a, 1 + j, mine, _flip(k)[0], src=ins[a]) for j, k in enumerate(chips)]
            for cp in first:
                cp.start()
            started += first
        for a in range(na):
            for j, k in enumerate(chips):
                _, theirs = _flip(k)
                copy(a, 1 + j, theirs, _flip(0)[0]).wait_recv()
                fwd = copy(a, 4 + j, theirs, sibling)
                fwd.start()
                started.append(fwd)
        for a in range(na):
            copy(a, 0, sibling_idx, _flip(0)[0]).wait_recv()
            for j, k in enumerate(chips):
                _, theirs = _flip(k | 1)
                copy(a, 4 + j, theirs, _flip(0)[0]).wait_recv()
        for cp in started:
            cp.wait_send()
        for cp in local:
            cp.wait()

    return pl.pallas_call(
        body, name="all_gather_weights",
        in_specs=[ANY] * na, out_specs=[ANY] * na,
        out_shape=[SDS((N_DEV * s.shape[0], s.shape[1]), s.dtype) for s in shards],
        scratch_shapes=[pltpu.SemaphoreType.DMA((na, 7)), pltpu.SemaphoreType.DMA((na, 7)),
                        pltpu.SemaphoreType.DMA((na,))],
        compiler_params=pltpu.CompilerParams(has_side_effects=True),
    )(*shards)


def _sum_slots(recv):
    _, r, w = recv.shape
    tr = 128 if r % 128 == 0 else r

    def body(r_ref, o_ref):
        acc = r_ref[0].astype(F32)
        for s in range(1, N_DEV):
            acc = acc + r_ref[s].astype(F32)
        o_ref[...] = acc

    return pl.pallas_call(
        body, name="sum_slots", grid=(r // tr,),
        in_specs=[pl.BlockSpec((N_DEV, tr, w), lambda i: (0, i, 0))],
        out_specs=pl.BlockSpec((tr, w), lambda i: (i, 0)),
        out_shape=SDS((r, w), F32),
        compiler_params=_params(("parallel",)),
    )(recv)


def _all_reduce_small(packed):
    shape = packed.shape

    def body(p_ref, o_ref, slots, send_sems, recv_sems):
        me, mine = _flip(0)
        slots[mine] = p_ref[...]
        sends = []
        for k in range(1, N_DEV):
            peer, _ = _flip(k)
            cp = pltpu.make_async_remote_copy(
                src_ref=p_ref, dst_ref=slots.at[mine], send_sem=send_sems.at[k - 1], recv_sem=recv_sems.at[k - 1],
                device_id=peer, device_id_type=MESH_ID)
            cp.start()
            sends.append(cp)
        for k in range(1, N_DEV):
            _, theirs = _flip(k)
            pltpu.make_async_remote_copy(
                src_ref=p_ref, dst_ref=slots.at[theirs], send_sem=send_sems.at[k - 1],
                recv_sem=recv_sems.at[k - 1], device_id=me, device_id_type=MESH_ID).wait_recv()
        for cp in sends:
            cp.wait_send()
        acc = slots[0]
        for s in range(1, N_DEV):
            acc = acc + slots[s]
        o_ref[...] = acc

    vm = pl.BlockSpec(memory_space=pltpu.VMEM)
    return pl.pallas_call(
        body, name="all_reduce_small", in_specs=[vm], out_specs=vm, out_shape=SDS(shape, F32),
        scratch_shapes=[pltpu.VMEM((N_DEV,) + shape, F32), pltpu.SemaphoreType.DMA((7,)),
                        pltpu.SemaphoreType.DMA((7,))],
        compiler_params=pltpu.CompilerParams(has_side_effects=True),
    )(packed)


def _layer_fwd(x, p, tabs, ex):
    z, h_t = _in_proj(x, p["norm_g"], p["w_in_t"])
    q, k, v, kt, vt = _attn_prep(z, p["qn"], p["kn"], tabs["ca"], tabs["sa"], tabs["ones"])
    oa, lse, *gathered = _attn_fwd(q, k, vt, ex)
    qrot, krot, vb, orr, on = _ret_fwd(z, p["lgf"], p["lgb"], p["gnw"], tabs["cr"], tabs["sr"])
    return z, h_t, q, k, v, kt, lse, oa, qrot, krot, vb, orr, on, gathered


def _layer_bwd(dxo, s, p, tabs, ex_attn, make_ex_in):
    doa, don, dz_m, d_wout, d_wb_t = _merge_bwd(dxo, s["z"], s["oa"], s["on"], s["ya"], s["yb"], p["wb_t"], p["w_out"])
    dq_a, dk_a, dv_a, *recv_attn = _attn_bwd(s["q"], s["k"], s["kt"], s["v"], doa, s["oa"], s["lse"],
                                              ex_attn(d_wb_t, d_wout))
    dz_a, d_qn, d_kn = _attn_post_bwd(dq_a, dk_a, dv_a, s["z"], p["qn"], p["kn"], tabs["ca"], tabs["sa"],
                                      tabs["ones"])
    dq_r, dk_r, dv_r, d_gnw, d_lgf, d_lgb = _ret_bwd(s["qrot"], s["krot"], s["vb"], s["orr"], don, p["gnw"],
                                                     p["lgf"], p["lgb"])
    dqr, dkr, dvr = _ret_post_bwd(dq_r, dk_r, dv_r, tabs["cr"], tabs["sr"])
    buf = None
    for piece, col0, width, row0 in [(dz_a, 0, 768, 0), (dz_m, 0, 512, SEG["ga"][0]), (dqr, 0, 512, SEG["qr"][0]),
                                     (dkr, 0, 512, SEG["kr"][0]), (dvr, 0, 512, SEG["vr"][0]),
                                     (dz_m, 512, 2560, SEG["gr"][0])]:
        buf = _dw_in(s["h_t"], piece, col0, width, row0, buf)
    dx, d_norm_g, *recv_in = _in_bwd(dxo, s["x"], p["norm_g"], p["w_in_t"], dz_a, dz_m, dqr, dkr, dvr,
                                     make_ex_in(buf))
    grads = dict(w_in_t=buf, wb_t=d_wb_t, w_out=d_wout, norm_g=d_norm_g, gnw=d_gnw,
                 qn=d_qn.reshape(ATTN_Q_HEADS, ATTN_HEAD_DIM).sum(axis=0),
                 kn=d_kn.reshape(ATTN_KV_HEADS, ATTN_HEAD_DIM).sum(axis=0),
                 lgf=d_lgf[:, 0, 0], lgb=d_lgb[:, 0, 0])
    return dx, grads, recv_attn, recv_in


def _adamw_nd(w, g, m, v):
    shape = w.shape
    two_d = (1, shape[0]) if w.ndim == 1 else (-1, shape[-1])
    out = _adamw(w.reshape(two_d), g.reshape(two_d), m.reshape(two_d), v.reshape(two_d))
    return tuple(o.reshape(shape) for o in out)


def kernel(x, norm_g, w_in, attn_q_norm, attn_k_norm, ret_decay_fwd, ret_decay_bwd, ret_gn_w, w_branch_attn, w_branch_ret, w_out, final_norm_g, loss_target, m_norm_g, m_w_in, m_attn_q_norm, m_attn_k_norm, m_ret_decay_fwd, m_ret_decay_bwd, m_ret_gn_w, m_w_branch_attn, m_w_branch_ret, m_w_out, m_final_norm_g, v_norm_g, v_w_in, v_attn_q_norm, v_attn_k_norm, v_ret_decay_fwd, v_ret_decay_bwd, v_ret_gn_w, v_w_branch_attn, v_w_branch_ret, v_w_out, v_final_norm_g):
    t, d = x.shape[1], x.shape[2]
    x2, target = x[0], loss_target[0]

    w_in_sh, wb_sh, wout_sh = [], [], []
    for l in range(DEPTH):
        w_in_sh.append(jnp.swapaxes(w_in[l], 0, 1).astype(BF16))
        wb_sh.append(jnp.concatenate([w_branch_attn[l].T, w_branch_ret[l].T], axis=1).astype(BF16))
        wout_sh.append(w_out[l].astype(BF16))

    ca, sa = _rope_tables(t, ATTN_HEAD_DIM)
    cr, sr = _rope_tables(t, RET_HEAD_DIM)
    grp = jnp.arange(ATTN_WIDTH) // ATTN_HEAD_DIM
    tabs = dict(ca=jnp.tile(ca, (1, 2)), sa=jnp.tile(sa, (1, 2)), cr=cr, sr=sr,
                ones=jnp.where(grp[:, None] == grp[None, :], 1.0 / ATTN_HEAD_DIM, 0.0).astype(BF16))
    layers = []
    for l in range(DEPTH):
        layers.append(dict(
            norm_g=norm_g[l][None], qn=jnp.tile(attn_q_norm[l], ATTN_Q_HEADS)[None],
            kn=jnp.tile(attn_k_norm[l], ATTN_KV_HEADS)[None], gnw=ret_gn_w[l][None],
            lgf=jax.nn.log_sigmoid(ret_decay_fwd[l]), lgb=jax.nn.log_sigmoid(ret_decay_bwd[l])))

    layers[0]["w_in_t"], = _all_gather([w_in_sh[0]])
    gathers = [_Exchange("gather", [wb_sh[0], wout_sh[0], w_in_sh[1]]), _Exchange("gather", [wb_sh[1], wout_sh[1]])]
    h = x2
    saved = []
    for l in range(DEPTH):
        p = layers[l]
        z, h_t, q, k, v, kt, lse, oa, qrot, krot, vb, orr, on, got = _layer_fwd(h, p, tabs, gathers[l])
        p["wb_t"], p["w_out"] = got[0], got[1]
        if l == 0:
            layers[1]["w_in_t"] = got[2]
        xn, ya, yb = _merge_fwd(h, z, oa, on, p["wb_t"], p["w_out"])
        saved.append(dict(x=h, z=z, h_t=h_t, q=q, k=k, v=v, kt=kt, lse=lse, oa=oa, qrot=qrot, krot=krot, vb=vb,
                          orr=orr, on=on, ya=ya, yb=yb))
        h = xn
    dx, d_final_g, loss_part = _final_loss(h, final_norm_g[None], target)

    grads = [None] * DEPTH
    none = lambda *a: None
    dx, grads[1], _, _ = _layer_bwd(dx, saved[1], layers[1], tabs, none, none)
    g1 = grads[1]
    ex_attn = lambda d_wb_t, d_wout: _Exchange("scatter", [g1["w_in_t"], g1["wb_t"], g1["w_out"], d_wb_t, d_wout])
    ex_in = lambda d_w_in_t: _Exchange("scatter", [d_w_in_t])
    dx, grads[0], recv_attn, recv_in = _layer_bwd(dx, saved[0], layers[0], tabs, ex_attn, ex_in)
    recv = [recv_in[0], recv_attn[3], recv_attn[4], recv_attn[0], recv_attn[1], recv_attn[2]]
    g_w_in, *upd_w_in = _sum_adamw_w_in([recv[0], recv[3]], w_in, m_w_in, v_w_in)
    summed = {i: _sum_slots(recv[i]) for i in (1, 2, 4, 5)}
    g_wba = jnp.stack([summed[3 * l + 1][:, :512].T for l in range(DEPTH)])
    g_wbr = jnp.stack([summed[3 * l + 1][:, 512:].T for l in range(DEPTH)])
    g_wout = jnp.stack([summed[3 * l + 2] for l in range(DEPTH)])

    packed = jnp.zeros((8, 1024), F32)
    for l in range(DEPTH):
        gl = grads[l]
        packed = packed.at[l].set(gl["norm_g"][0])
        packed = packed.at[2, 512 * l:512 * (l + 1)].set(gl["gnw"][0])
        packed = packed.at[4, 128 * l:128 * l + 64].set(gl["qn"])
        packed = packed.at[4, 256 + 128 * l:256 + 128 * l + 64].set(gl["kn"])
        packed = packed.at[4, 512 + 128 * l:512 + 128 * l + 4].set(gl["lgf"])
        packed = packed.at[4, 768 + 128 * l:768 + 128 * l + 4].set(gl["lgb"])
    packed = packed.at[3].set(d_final_g[0])
    packed = packed.at[5, 0].set(loss_part[0, 0])
    red = _all_reduce_small(packed)
    loss = red[5, 0]
    g_norm_g = red[0:2]
    g_gnw = red[2].reshape(DEPTH, RET_WIDTH)
    g_final = red[3]
    g_qn = jnp.stack([red[4, 128 * l:128 * l + 64] for l in range(DEPTH)])
    g_kn = jnp.stack([red[4, 256 + 128 * l:256 + 128 * l + 64] for l in range(DEPTH)])
    g_lgf = jnp.stack([red[4, 512 + 128 * l:512 + 128 * l + 4] for l in range(DEPTH)])
    g_lgb = jnp.stack([red[4, 768 + 128 * l:768 + 128 * l + 4] for l in range(DEPTH)])
    g_df = g_lgf * jax.nn.sigmoid(-ret_decay_fwd)
    g_db = g_lgb * jax.nn.sigmoid(-ret_decay_bwd)

    grad_w = [g_norm_g, g_w_in, g_qn, g_kn, g_df, g_db, g_gnw, g_wba, g_wbr, g_wout, g_final]
    weights = [norm_g, w_in, attn_q_norm, attn_k_norm, ret_decay_fwd, ret_decay_bwd, ret_gn_w, w_branch_attn,
               w_branch_ret, w_out, final_norm_g]
    ms = [m_norm_g, m_w_in, m_attn_q_norm, m_attn_k_norm, m_ret_decay_fwd, m_ret_decay_bwd, m_ret_gn_w,
          m_w_branch_attn, m_w_branch_ret, m_w_out, m_final_norm_g]
    vs = [v_norm_g, v_w_in, v_attn_q_norm, v_attn_k_norm, v_ret_decay_fwd, v_ret_decay_bwd, v_ret_gn_w,
          v_w_branch_attn, v_w_branch_ret, v_w_out, v_final_norm_g]
    upd = [upd_w_in if w is w_in else _adamw_nd(w, g, m, v) for w, g, m, v in zip(weights, grad_w, ms, vs)]
    return (loss, dx[None], *grad_w, *[u[0] for u in upd], *[u[1] for u in upd], *[u[2] for u in upd])
```

```python
import functools

import jax
import jax.numpy as jnp
from jax import lax
from jax.experimental import pallas as pl
from jax.experimental.pallas import tpu as pltpu

F32 = jnp.float32
BF16 = jnp.bfloat16
SDS = jax.ShapeDtypeStruct

D_MODEL = 1024
DEPTH = 2
GRID_W = 64
ATTN_Q_HEADS = 8
ATTN_KV_HEADS = 2
ATTN_HEAD_DIM = 64
ATTN_WIDTH = 512
ATTN_KV_WIDTH = 128
RET_HEADS = 4
RET_HEAD_DIM = 128
RET_WIDTH = 512
RET_CHUNK = 128
ATTN_KEY_CHUNK = 512
EXP_LAG = 3
ROPE_THETA = 10000.0
EPS = 1e-6
D_IN = 5376
N_DEV = 8

ADAM_LR = 0.001
ADAM_B1 = 0.9
ADAM_B2 = 0.999
ADAM_EPS = 1e-08
ADAM_WD = 0.01
ADAM_STEP = 10

SEG = {
    "qa": (0, 512, 0),
    "ga": (768, 512, 512),
    "qr": (1280, 512, 1024),
    "kr": (1792, 512, 1536),
    "vr": (2304, 512, 2048),
    "gr": (2816, 512, 2560),
    "gm": (3328, 2048, 3072),
    "ka": (512, 128, 5120),
    "va": (640, 128, 5248),
}

VMEM_LIMIT = 60 * 1024 * 1024
NT = (((1,), (1,)), ((), ()))
TN = (((0,), (0,)), ((), ()))
MESH_ID = pl.DeviceIdType.MESH
ANY = pl.BlockSpec(memory_space=pl.ANY)


def _params(sem=None, vmem=VMEM_LIMIT):
    return pltpu.CompilerParams(dimension_semantics=sem, vmem_limit_bytes=vmem)


def _dot(a, b, dims=None):
    if dims is None:
        return jnp.dot(a, b, preferred_element_type=F32)
    return lax.dot_general(a, b, dims, preferred_element_type=F32)


def _sigmoid(x):
    return 1.0 / (1.0 + jnp.exp(-x))


def _swap_halves(x, q):
    n = x.shape[-1]
    axis = x.ndim - 1
    lane = lax.broadcasted_iota(jnp.int32, x.shape, axis)
    first = (lane % (2 * q)) < q
    return jnp.where(first, pltpu.roll(x, n - q, axis), pltpu.roll(x, q, axis))


def _rope(x, cos, sin_signed, q):
    return x * cos + _swap_halves(x, q) * sin_signed


def _rope_bwd(dy, cos, sin_signed, q):
    return dy * cos - _swap_halves(dy, q) * sin_signed


def _group_mean(v, ones_bd):
    hi = v.astype(BF16)
    r1 = v - hi.astype(F32)
    mid = r1.astype(BF16)
    lo = (r1 - mid.astype(F32)).astype(BF16)
    return _dot(hi, ones_bd) + _dot(mid, ones_bd) + _dot(lo, ones_bd)


def _rope_tables(t, head_dim):
    n_rows = t // GRID_W
    row = jnp.repeat(jnp.arange(n_rows, dtype=F32), GRID_W)
    col = jnp.tile(jnp.arange(GRID_W, dtype=F32), n_rows)
    d_axis = head_dim // 2
    inv_freq = ROPE_THETA ** (-jnp.arange(0, d_axis, 2, dtype=F32) / d_axis)
    ar = row[:, None] * inv_freq
    ac = col[:, None] * inv_freq
    cr, sr, cc, sc = jnp.cos(ar), jnp.sin(ar), jnp.cos(ac), jnp.sin(ac)
    return jnp.concatenate([cr, cr, cc, cc], axis=-1), jnp.concatenate([-sr, sr, -sc, sc], axis=-1)


def _me():
    return lax.axis_index("x"), lax.axis_index("y"), lax.axis_index("c")


def _flip(k):
    x, y, c = _me()
    px = 1 - x if k & 4 else x
    py = 1 - y if k & 2 else y
    pc = 1 - c if k & 1 else c
    return (px, py, pc), 4 * px + 2 * py + pc


class _Exchange:
    def __init__(self, kind, srcs):
        self.kind, self.srcs, self.n = kind, list(srcs), len(srcs)
        self.rows = [a.shape[0] if kind == "gather" else a.shape[0] // N_DEV for a in srcs]
        if kind == "gather":
            self.out_shape = [SDS((N_DEV * a.shape[0], a.shape[1]), a.dtype) for a in srcs]
        else:
            self.out_shape = [SDS((N_DEV, a.shape[0] // N_DEV, a.shape[1]), a.dtype) for a in srcs]
        self.scratch = [pltpu.SemaphoreType.DMA((self.n, N_DEV - 1)), pltpu.SemaphoreType.DMA((self.n, N_DEV - 1)),
                        pltpu.SemaphoreType.DMA((self.n,))]

    def _block(self, ref, a, idx):
        r = self.rows[a]
        return ref.at[pl.ds(pl.multiple_of(idx * r, 16), r), :]

    def _src(self, ins, a, idx):
        return ins[a] if self.kind == "gather" else self._block(ins[a], a, idx)

    def _dst(self, outs, a, idx):
        return self._block(outs[a], a, idx) if self.kind == "gather" else outs[a].at[idx]

    def _copies(self, ins, outs, sems):
        send_sems, recv_sems, local_sems = sems
        me, mine = _flip(0)
        local, sends, recvs = [], [], []
        for a in range(self.n):
            local.append(pltpu.make_async_copy(self._src(ins, a, mine), self._dst(outs, a, mine), local_sems.at[a]))
            for k in range(1, N_DEV):
                peer, theirs = _flip(k)
                sem = dict(send_sem=send_sems.at[a, k - 1], recv_sem=recv_sems.at[a, k - 1])
                sends.append(pltpu.make_async_remote_copy(
                    src_ref=self._src(ins, a, theirs), dst_ref=self._dst(outs, a, mine),
                    device_id=peer, device_id_type=MESH_ID, **sem))
                recvs.append(pltpu.make_async_remote_copy(
                    src_ref=self._dst(outs, a, theirs), dst_ref=self._dst(outs, a, theirs),
                    device_id=me, device_id_type=MESH_ID, **sem))
        return local, sends, recvs

    def start(self, ins, outs, sems):
        local, sends, _ = self._copies(ins, outs, sems)
        for cp in local + sends:
            cp.start()

    def wait(self, ins, outs, sems):
        local, sends, recvs = self._copies(ins, outs, sems)
        for cp in sends:
            cp.wait_send()
        for cp in recvs:
            cp.wait_recv()
        for cp in local:
            cp.wait()


def _with_exchange(body, n_in, n_out, n_scratch, ex, first, last):
    if ex is None:
        return body

    def wrapped(*refs):
        ins = refs[:n_in]
        ex_ins = refs[n_in:n_in + ex.n]
        outs = refs[n_in + ex.n:n_in + ex.n + n_out]
        ex_outs = refs[n_in + ex.n + n_out:n_in + 2 * ex.n + n_out]
        rest = refs[n_in + 2 * ex.n + n_out:]
        scratch, sems = rest[:n_scratch], rest[n_scratch:]

        @pl.when(first())
        def _():
            ex.start(ex_ins, ex_outs, sems)

        body(*ins, *outs, *scratch)

        @pl.when(last())
        def _():
            ex.wait(ex_ins, ex_outs, sems)

    return wrapped


def _ex_args(ex):
    if ex is None:
        return [], [], [], [], []
    return [ANY] * ex.n, [ANY] * ex.n, list(ex.out_shape), list(ex.scratch), list(ex.srcs)


def _in_proj(x, g, w_t):
    t, d = x.shape
    tm = min(256, t)

    def body(x_ref, g_ref, w_ref, z_ref, ht_ref):
        xv = x_ref[...]
        r = lax.rsqrt(jnp.mean(xv * xv, axis=-1, keepdims=True) + EPS)
        h = xv * r * g_ref[...]
        ht_ref[...] = h.T.astype(BF16)
        hb = h.astype(BF16)
        for nat, w, off in SEG.values():
            z_ref[:, off:off + w] = _dot(hb, w_ref[nat:nat + w, :], NT)

    return pl.pallas_call(
        body, name="in_proj", grid=(t // tm,),
        in_specs=[pl.BlockSpec((tm, d), lambda i: (i, 0)), pl.BlockSpec((1, d), lambda i: (0, 0)),
                  pl.BlockSpec((D_IN, d), lambda i: (0, 0))],
        out_specs=[pl.BlockSpec((tm, D_IN), lambda i: (i, 0)), pl.BlockSpec((d, tm), lambda i: (0, i))],
        out_shape=[SDS((t, D_IN), F32), SDS((d, t), BF16)],
        compiler_params=_params(("parallel",)),
    )(x, g, w_t)


def _attn_prep(z, qn, kn, cos, sin, ones_bd):
    t = z.shape[0]
    tm = min(ATTN_KEY_CHUNK, t)
    hd = ATTN_HEAD_DIM

    def body(zq_ref, zkv_ref, qn_ref, kn_ref, c_ref, s_ref, b_ref, q_out, k_out, v_out, kt_out, vt_out):
        bd = b_ref[...]
        c2, s2 = c_ref[...], s_ref[...]
        cq = jnp.concatenate([c2] * 4, axis=-1)
        sq = jnp.concatenate([s2] * 4, axis=-1)
        xq = zq_ref[...]
        yq = xq * lax.rsqrt(_group_mean(xq * xq, bd) + EPS) * qn_ref[...]
        yq = _rope(yq, cq, sq, hd // 4) * (hd ** -0.5)
        for h in range(ATTN_Q_HEADS):
            q_out[h] = yq[:, h * hd:(h + 1) * hd].astype(BF16)
        zkv = zkv_ref[...]
        xk, xv = zkv[:, :ATTN_KV_WIDTH], zkv[:, ATTN_KV_WIDTH:]
        yk = xk * lax.rsqrt(_group_mean(xk * xk, bd[:ATTN_KV_WIDTH, :ATTN_KV_WIDTH]) + EPS) * kn_ref[...]
        yk = _rope(yk, c2, s2, hd // 4)
        ykt, xvt = yk.T, xv.T
        ones = jnp.ones((hd, tm), F32)
        for h in range(ATTN_KV_HEADS):
            k_out[h] = yk[:, h * hd:(h + 1) * hd].astype(BF16)
            v_out[h] = xv[:, h * hd:(h + 1) * hd].astype(BF16)
            kt_out[h, 0] = ykt[h * hd:(h + 1) * hd, :].astype(BF16)
            vt_out[h, 0] = jnp.concatenate([xvt[h * hd:(h + 1) * hd, :], ones], axis=0).astype(BF16)

    kv_blk = SEG["ka"][2] // 256
    nk = t // tm
    return pl.pallas_call(
        body, name="attn_prep", grid=(nk,),
        in_specs=[pl.BlockSpec((tm, 512), lambda i: (i, 0)), pl.BlockSpec((tm, 256), lambda i: (i, kv_blk)),
                  pl.BlockSpec((1, 512), lambda i: (0, 0)), pl.BlockSpec((1, 128), lambda i: (0, 0)),
                  pl.BlockSpec((tm, 128), lambda i: (i, 0)), pl.BlockSpec((tm, 128), lambda i: (i, 0)),
                  pl.BlockSpec((512, 512), lambda i: (0, 0))],
        out_specs=[pl.BlockSpec((ATTN_Q_HEADS, tm, hd), lambda i: (0, i, 0)),
                   pl.BlockSpec((ATTN_KV_HEADS, tm, hd), lambda i: (0, i, 0)),
                   pl.BlockSpec((ATTN_KV_HEADS, tm, hd), lambda i: (0, i, 0)),
                   pl.BlockSpec((ATTN_KV_HEADS, 1, hd, tm), lambda i: (0, i, 0, 0)),
                   pl.BlockSpec((ATTN_KV_HEADS, 1, 2 * hd, tm), lambda i: (0, i, 0, 0))],
        out_shape=[SDS((ATTN_Q_HEADS, t, hd), BF16), SDS((ATTN_KV_HEADS, t, hd), BF16),
                   SDS((ATTN_KV_HEADS, t, hd), BF16), SDS((ATTN_KV_HEADS, nk, hd, tm), BF16),
                   SDS((ATTN_KV_HEADS, nk, 2 * hd, tm), BF16)],
        compiler_params=_params(("parallel",)),
    )(z, z, qn, kn, cos, sin, ones_bd)


def _attn_fwd(q, k, vt, ex=None):
    t = q.shape[1]
    tq = min(256, t)
    nk, tk = vt.shape[1], vt.shape[3]
    hd = ATTN_HEAD_DIM
    g = ATTN_Q_HEADS // ATTN_KV_HEADS

    def body(q_ref, k_ref, vt_ref, o_ref, lse_ref, s_scr):
        def pass_a(h, c, m8):
            half = tk // 2
            for lo in (c * tk, c * tk + half):
                st = _dot(k_ref[0, lo:lo + half, :], q_ref[h], NT)
                s_scr[h % 2, lo:lo + half, :] = st
                m8 = jnp.maximum(m8, jnp.max(st.reshape(half // 8, 8, tq), axis=0))
            return m8

        def pass_b(h, c, m, acc, after):
            e = jnp.exp(s_scr[h % 2, c * tk:(c + 1) * tk, :] - (m + after * 0.0)).astype(BF16)
            return acc + _dot(vt_ref[0, c], e)

        neg = jnp.full((8, tq), -jnp.inf, F32)
        m8 = neg
        for c in range(nk):
            m8 = pass_a(0, c, m8)
        outs = []
        for h in range(g):
            m = jnp.max(m8, axis=0, keepdims=True)
            acc = jnp.zeros((2 * hd, tq), F32)
            m8 = neg
            done = [m] * EXP_LAG
            for c in range(nk):
                if h + 1 < g:
                    m8 = pass_a(h + 1, c, m8)
                acc = pass_b(h, c, m, acc, done[-EXP_LAG])
                done.append(m8[0:1, :] if h + 1 < g else acc[hd:hd + 1, :])
            l = acc[hd:hd + 1, :]
            outs.append((acc[:hd, :] / l).T)
            lse_ref[h] = m + jnp.log(l)
        o_ref[...] = jnp.concatenate(outs, axis=-1)

    nq = t // tq
    first = lambda: jnp.logical_and(pl.program_id(0) == 0, pl.program_id(1) == 0)
    last = lambda: jnp.logical_and(pl.program_id(0) == ATTN_KV_HEADS - 1, pl.program_id(1) == nq - 1)
    xi, xo, xs, xscr, xargs = _ex_args(ex)
    return pl.pallas_call(
        _with_exchange(body, 3, 2, 1, ex, first, last), name="attn_fwd", grid=(ATTN_KV_HEADS, nq),
        in_specs=[pl.BlockSpec((g, tq, hd), lambda p, i: (p, i, 0)),
                  pl.BlockSpec((1, t, hd), lambda p, i: (p, 0, 0)),
                  pl.BlockSpec((1, nk, 2 * hd, tk), lambda p, i: (p, 0, 0, 0))] + xi,
        out_specs=[pl.BlockSpec((tq, g * hd), lambda p, i: (i, p)),
                   pl.BlockSpec((g, 1, tq), lambda p, i: (p, 0, i))] + xo,
        out_shape=[SDS((t, ATTN_WIDTH), F32), SDS((ATTN_Q_HEADS, 1, t), F32)] + xs,
        scratch_shapes=[pltpu.VMEM((2, t, tq), F32)] + xscr,
        compiler_params=_params(("arbitrary", "arbitrary")),
    )(q, k, vt, *xargs)


class _Dir:
    def __init__(self, lg, strict_future):
        c = RET_CHUNK
        ia = lax.broadcasted_iota(jnp.int32, (c, c), 0).astype(F32)
        ib = lax.broadcasted_iota(jnp.int32, (c, c), 1).astype(F32)
        col = lax.broadcasted_iota(jnp.int32, (c, 1), 0).astype(F32)
        row = lax.broadcasted_iota(jnp.int32, (1, c), 1).astype(F32)
        if strict_future:
            dist = ib - ia
            mask = dist > 0
            self.wq, self.wk, wk_row = c - col, col, row
        else:
            dist = ia - ib
            mask = dist >= 0
            self.wq, self.wk, wk_row = col + 1.0, c - 1.0 - col, c - 1.0 - row
        self.dist = jnp.maximum(dist, 0.0)
        self.d = jnp.where(mask, jnp.exp(self.dist * lg), 0.0)
        self.qd = jnp.exp(self.wq * lg)
        self.kd_col = jnp.exp(self.wk * lg)
        self.kd_row = jnp.exp(wk_row * lg)
        self.cd = jnp.exp(jnp.full((1, 1), float(c), F32) * lg)


def _ret_fwd(z, lgf, lgb, gnw, cos, sin):
    t = z.shape[0]
    c = RET_CHUNK
    nc = t // c
    hd = RET_HEAD_DIM
    unroll = 4 if nc % 4 == 0 else 1

    def body(lgf_ref, lgb_ref, q_ref, k_ref, v_ref, c_ref, s_ref, w_ref,
             qo_ref, ko_ref, vo_ref, orr_ref, on_ref, kt, uf, ub, sfa, sba):
        h = pl.program_id(0)
        fw = _Dir(lgf_ref[h], False)
        bw = _Dir(lgb_ref[h], True)
        cc, ss = c_ref[...], s_ref[...]
        qo_ref[...] = _rope(q_ref[...], cc, ss, hd // 4).astype(BF16)
        kr = _rope(k_ref[...], cc, ss, hd // 4) * (hd ** -0.5)
        ko_ref[...] = kr.astype(BF16)
        vo_ref[...] = v_ref[...].astype(BF16)
        for i in range(nc):
            kt[i] = kr[i * c:(i + 1) * c, :].T.astype(BF16)

        def rows(ci):
            return pl.ds(pl.multiple_of(ci * c, c), c)

        def kv_products(ci, carry):
            vv = vo_ref[rows(ci), :]
            ktf = kt[ci].astype(F32)
            uf[ci] = _dot((ktf * fw.kd_row).astype(BF16), vv)
            ub[ci] = _dot((ktf * bw.kd_row).astype(BF16), vv)
            return carry

        lax.fori_loop(0, nc, kv_products, 0, unroll=unroll)

        def scan(i, carry):
            sf, sb = carry
            j = nc - 1 - i
            sfa[i] = sf.astype(BF16)
            sba[j] = sb.astype(BF16)
            return sf * fw.cd + uf[i], sb * bw.cd + ub[j]

        zero = jnp.zeros((hd, hd), F32)
        lax.fori_loop(0, nc, scan, (zero, zero))
        gw = w_ref[...]

        def outputs(ci, carry):
            sl = rows(ci)
            qq, kk, vv = qo_ref[sl, :], ko_ref[sl, :], vo_ref[sl, :]
            a = _dot(qq, kk, NT)
            o = (_dot((a * fw.d).astype(BF16), vv) + _dot(qq, sfa[ci]) * fw.qd
                 + _dot((a * bw.d).astype(BF16), vv) + _dot(qq, sba[ci]) * bw.qd)
            orr_ref[sl, :] = o
            xc = o - jnp.mean(o, axis=-1, keepdims=True)
            var = jnp.mean(xc * xc, axis=-1, keepdims=True)
            on_ref[sl, :] = xc * lax.rsqrt(var + EPS) * gw
            return carry

        lax.fori_loop(0, nc, outputs, 0, unroll=unroll)

    smem = pl.BlockSpec(memory_space=pltpu.SMEM)
    col = lambda name: (lambda h: (0, SEG[name][2] // 128 + h))
    head = pl.BlockSpec((t, 128), lambda h: (0, h))
    full = pl.BlockSpec((t, 128), lambda h: (0, 0))
    return pl.pallas_call(
        body, name="ret_fwd", grid=(RET_HEADS,),
        in_specs=[smem, smem, pl.BlockSpec((t, 128), col("qr")), pl.BlockSpec((t, 128), col("kr")),
                  pl.BlockSpec((t, 128), col("vr")), full, full, pl.BlockSpec((1, 128), lambda h: (0, h))],
        out_specs=[head, head, head, head, head],
        out_shape=[SDS((t, RET_WIDTH), BF16)] * 3 + [SDS((t, RET_WIDTH), F32)] * 2,
        scratch_shapes=[pltpu.VMEM((nc, hd, c), BF16), pltpu.VMEM((nc, hd, hd), F32), pltpu.VMEM((nc, hd, hd), F32),
                        pltpu.VMEM((nc, hd, hd), BF16), pltpu.VMEM((nc, hd, hd), BF16)],
        compiler_params=_params(("parallel",)),
    )(lgf, lgb, z, z, z, cos, sin, gnw)


def _merge_fwd(x, z, oa, on, wb_t, wout):
    t, d = x.shape
    tm = min(256, t)

    def body(x_ref, ga_ref, gr_ref, gm0_ref, gm1_ref, oa_ref, on_ref, wb_ref, wo_ref, xn_ref, ya_ref, yb_ref):
        ga, gr = ga_ref[...], gr_ref[...]
        ua = ga * _sigmoid(ga) * oa_ref[...]
        ub = gr * _sigmoid(gr) * on_ref[...]
        ya = _dot(ua.astype(BF16), wb_ref[:, :512], NT)
        yb = _dot(ub.astype(BF16), wb_ref[:, 512:], NT)
        ya_ref[...] = ya
        yb_ref[...] = yb
        merged = _sigmoid(gm0_ref[...]) * ya + _sigmoid(gm1_ref[...]) * yb
        xn_ref[...] = x_ref[...] + _dot(merged.astype(BF16), wo_ref[...])

    row = lambda w, j: pl.BlockSpec((tm, w), lambda i: (i, j))
    const = lambda shape: pl.BlockSpec(shape, lambda i: (0, 0))
    return pl.pallas_call(
        body, name="merge_fwd", grid=(t // tm,),
        in_specs=[row(d, 0), row(512, SEG["ga"][2] // 512), row(512, SEG["gr"][2] // 512),
                  row(1024, SEG["gm"][2] // 1024), row(1024, SEG["gm"][2] // 1024 + 1),
                  row(512, 0), row(512, 0), const((d, 1024)), const((d, d))],
        out_specs=[row(d, 0), row(d, 0), row(d, 0)],
        out_shape=[SDS((t, d), F32)] * 3,
        compiler_params=_params(("parallel",)),
    )(x, z, z, z, z, oa, on, wb_t, wout)


def _final_loss(x, g, target):
    t, d = x.shape
    tm = min(512, t)
    n = t // tm

    def body(x_ref, g_ref, t_ref, dx_ref, dg_ref, loss_ref, acc_g, acc_l):
        i = pl.program_id(0)

        @pl.when(i == 0)
        def _():
            acc_g[...] = jnp.zeros_like(acc_g)
            acc_l[...] = jnp.zeros_like(acc_l)

        xv, gv = x_ref[...], g_ref[...]
        r = lax.rsqrt(jnp.mean(xv * xv, axis=-1, keepdims=True) + EPS)
        xh = xv * r
        err = xh * gv - t_ref[...]
        dy = err * (1.0 / d)
        gy = dy * gv
        dx_ref[...] = r * (gy - xh * jnp.mean(gy * xh, axis=-1, keepdims=True))
        acc_g[...] += jnp.sum((dy * xh).reshape(tm // 8, 8, d), axis=0)
        acc_l[...] += jnp.sum((err * err).reshape(tm // 8, 8, d), axis=0)

        @pl.when(i == n - 1)
        def _():
            dg_ref[...] = jnp.sum(acc_g[...], axis=0, keepdims=True)
            tot = jnp.sum(jnp.sum(acc_l[...], axis=0, keepdims=True), axis=1, keepdims=True)
            loss_ref[...] = jnp.broadcast_to(tot * (0.5 / d), (1, 128))

    return pl.pallas_call(
        body, name="final_loss", grid=(n,),
        in_specs=[pl.BlockSpec((tm, d), lambda i: (i, 0)), pl.BlockSpec((1, d), lambda i: (0, 0)),
                  pl.BlockSpec((tm, d), lambda i: (i, 0))],
        out_specs=[pl.BlockSpec((tm, d), lambda i: (i, 0)), pl.BlockSpec((1, d), lambda i: (0, 0)),
                   pl.BlockSpec((1, 128), lambda i: (0, 0))],
        out_shape=[SDS((t, d), F32), SDS((1, d), F32), SDS((1, 128), F32)],
        scratch_shapes=[pltpu.VMEM((8, d), F32), pltpu.VMEM((8, d), F32)],
        compiler_params=_params(("arbitrary",)),
    )(x, g, target)


def _merge_bwd(dxo, z, oa, on, ya, yb, wb_t, wout):
    t, d = dxo.shape
    tm = min(256, t)
    n = t // tm

    def body(dx_ref, ga_ref, gr_ref, gm0_ref, gm1_ref, oa_ref, on_ref, ya_ref, yb_ref, wb_ref, wo_ref,
             doa_ref, don_ref, dz_ref, dwo_ref, dwb_ref, acc_o, acc_b):
        i = pl.program_id(0)

        @pl.when(i == 0)
        def _():
            acc_o[...] = jnp.zeros_like(acc_o)
            acc_b[...] = jnp.zeros_like(acc_b)

        dxb = dx_ref[...].astype(BF16)
        ya, yb = ya_ref[...], yb_ref[...]
        g0, g1 = _sigmoid(gm0_ref[...]), _sigmoid(gm1_ref[...])
        mb = (g0 * ya + g1 * yb).astype(BF16)
        dm = _dot(dxb, wo_ref[...], NT)
        dya = (dm * g0).astype(BF16)
        dyb = (dm * g1).astype(BF16)
        dz_ref[:, 1024:2048] = (dm * ya * g0 * (1.0 - g0)).astype(BF16)
        dz_ref[:, 2048:3072] = (dm * yb * g1 * (1.0 - g1)).astype(BF16)

        def branch(g_ref, o_ref, dy, w, do_ref, lo):
            gv, ov = g_ref[...], o_ref[...]
            sg = _sigmoid(gv)
            silu = gv * sg
            du = _dot(dy, w)
            do_ref[...] = du * silu
            dz_ref[:, lo:lo + 512] = (du * ov * (sg * (1.0 + gv * (1.0 - sg)))).astype(BF16)
            acc_b[:, lo:lo + 512] += _dot(dy, (silu * ov).astype(BF16), TN)

        branch(ga_ref, oa_ref, dya, wb_ref[:, :512], doa_ref, 0)
        branch(gr_ref, on_ref, dyb, wb_ref[:, 512:], don_ref, 512)
        acc_o[...] += _dot(mb, dxb, TN)

        @pl.when(i == n - 1)
        def _():
            dwo_ref[...] = acc_o[...].astype(BF16)
            dwb_ref[...] = acc_b[...].astype(BF16)

    row = lambda w, j: pl.BlockSpec((tm, w), lambda i: (i, j))
    const = lambda shape: pl.BlockSpec(shape, lambda i: (0, 0))
    return pl.pallas_call(
        body, name="merge_bwd", grid=(n,),
        in_specs=[row(d, 0), row(512, SEG["ga"][2] // 512), row(512, SEG["gr"][2] // 512),
                  row(1024, SEG["gm"][2] // 1024), row(1024, SEG["gm"][2] // 1024 + 1),
                  row(512, 0), row(512, 0), row(d, 0), row(d, 0), const((d, 1024)), const((d, d))],
        out_specs=[row(512, 0), row(512, 0), row(3072, 0), const((d, d)), const((d, 1024))],
        out_shape=[SDS((t, 512), F32), SDS((t, 512), F32), SDS((t, 3072), BF16), SDS((d, d), BF16),
                   SDS((d, 1024), BF16)],
        scratch_shapes=[pltpu.VMEM((d, d), F32), pltpu.VMEM((d, 1024), F32)],
        compiler_params=_params(("arbitrary",)),
    )(dxo, z, z, z, z, oa, on, ya, yb, wb_t, wout)


def _ret_bwd(qrot, krot, vb, orr, don, gnw, lgf, lgb):
    t = qrot.shape[0]
    c = RET_CHUNK
    nc = t // c
    hd = RET_HEAD_DIM
    unroll = 2 if nc % 2 == 0 else 1

    def body(lgf_ref, lgb_ref, q_ref, k_ref, v_ref, o_ref, dn_ref, w_ref,
             dq_ref, dk_ref, dv_ref, dw_ref, dlf_ref, dlb_ref, qt, kt, dob, uf, ub, wf, wb, sfa, sba, gfa, gba):
        h = pl.program_id(0)
        fw = _Dir(lgf_ref[h], False)
        bw = _Dir(lgb_ref[h], True)
        fw.dt, bw.dt = fw.d.T, bw.d.T

        o = o_ref[...]
        xc = o - jnp.mean(o, axis=-1, keepdims=True)
        r = lax.rsqrt(jnp.mean(xc * xc, axis=-1, keepdims=True) + EPS)
        xh = xc * r
        dn = dn_ref[...]
        gy = dn * w_ref[...]
        d_o = r * (gy - jnp.mean(gy, axis=-1, keepdims=True) - xh * jnp.mean(gy * xh, axis=-1, keepdims=True))
        dw_ref[...] = jnp.sum(dn * xh, axis=0, keepdims=True)
        dob[...] = d_o.astype(BF16)
        for i in range(nc):
            qt[i] = q_ref[i * c:(i + 1) * c, :].astype(F32).T.astype(BF16)
            kt[i] = k_ref[i * c:(i + 1) * c, :].astype(F32).T.astype(BF16)

        def rows(ci):
            return pl.ds(pl.multiple_of(ci * c, c), c)

        def products(ci, carry):
            sl = rows(ci)
            vv, do32 = v_ref[sl, :], dob[sl, :].astype(F32)
            ktf = kt[ci].astype(F32)
            uf[ci] = _dot((ktf * fw.kd_row).astype(BF16), vv)
            ub[ci] = _dot((ktf * bw.kd_row).astype(BF16), vv)
            wf[ci] = _dot(qt[ci], (do32 * fw.qd).astype(BF16))
            wb[ci] = _dot(qt[ci], (do32 * bw.qd).astype(BF16))
            return carry

        lax.fori_loop(0, nc, products, 0, unroll=unroll)

        def scan(i, carry):
            sf, sb, gf, gb = carry
            j = nc - 1 - i
            sfa[i] = sf.astype(BF16)
            sba[j] = sb.astype(BF16)
            gfa[j] = gf.astype(BF16)
            gba[i] = gb.astype(BF16)
            return sf * fw.cd + uf[i], sb * bw.cd + ub[j], gf * fw.cd + wf[j], gb * bw.cd + wb[i]

        zero = jnp.zeros((hd, hd), F32)
        lax.fori_loop(0, nc, scan, (zero, zero, zero, zero))

        def one_dir(p, s_all, g_all, ci, qq, kk, vv, do, a, bm, at, bt):
            sb, gb = s_all[ci], g_all[ci]
            doq = (do.astype(F32) * p.qd).astype(BF16)
            dqc = _dot(doq, sb, NT)
            dq = _dot((bm * p.d).astype(BF16), kk) + dqc
            kkd = (kk.astype(F32) * p.kd_col).astype(BF16)
            dv = _dot((at * p.dt).astype(BF16), do) + _dot(kkd, gb)
            dk2 = _dot(vv, gb, NT) * p.kd_col
            dk = _dot((bt * p.dt).astype(BF16), qq) + dk2
            terms = (p.dist * p.d * a * bm + p.wq * qq.astype(F32) * dqc + p.wk * kk.astype(F32) * dk2
                     + (float(c) * p.cd) * gb.astype(F32) * sb.astype(F32))
            return dq, dk, dv, terms

        def chunk(ci, carry):
            af, ab = carry
            sl = rows(ci)
            qq, kk, vv, do = q_ref[sl, :], k_ref[sl, :], v_ref[sl, :], dob[sl, :]
            a, bm = _dot(qq, kk, NT), _dot(do, vv, NT)
            at, bt = _dot(kk, qq, NT), _dot(vv, do, NT)
            dqf, dkf, dvf, tf = one_dir(fw, sfa, gfa, ci, qq, kk, vv, do, a, bm, at, bt)
            dqb, dkb, dvb, tb = one_dir(bw, sba, gba, ci, qq, kk, vv, do, a, bm, at, bt)
            dq_ref[sl, :] = dqf + dqb
            dk_ref[sl, :] = dkf + dkb
            dv_ref[sl, :] = dvf + dvb
            return af + tf, ab + tb

        af, ab = lax.fori_loop(0, nc, chunk, (zero, zero), unroll=unroll)
        tot = lambda m: jnp.sum(jnp.sum(m, axis=0, keepdims=True), axis=1, keepdims=True)
        dlf_ref[...] = jnp.broadcast_to(tot(af).reshape(1, 1, 1), (1, 8, 128))
        dlb_ref[...] = jnp.broadcast_to(tot(ab).reshape(1, 1, 1), (1, 8, 128))

    smem = pl.BlockSpec(memory_space=pltpu.SMEM)
    head = pl.BlockSpec((t, 128), lambda h: (0, h))
    vec = pl.BlockSpec((1, 128), lambda h: (0, h))
    scal = pl.BlockSpec((1, 8, 128), lambda h: (h, 0, 0))
    mats = lambda dt: pltpu.VMEM((nc, hd, hd), dt)
    return pl.pallas_call(
        body, name="ret_bwd", grid=(RET_HEADS,),
        in_specs=[smem, smem, head, head, head, head, head, vec],
        out_specs=[head, head, head, vec, scal, scal],
        out_shape=[SDS((t, RET_WIDTH), F32)] * 3 + [SDS((1, RET_WIDTH), F32), SDS((RET_HEADS, 8, 128), F32),
                                                   SDS((RET_HEADS, 8, 128), F32)],
        scratch_shapes=[pltpu.VMEM((nc, hd, c), BF16), pltpu.VMEM((nc, hd, c), BF16), pltpu.VMEM((t, hd), BF16),
                        mats(F32), mats(F32), mats(F32), mats(F32), mats(BF16), mats(BF16), mats(BF16), mats(BF16)],
        compiler_params=_params(("parallel",)),
    )(lgf, lgb, qrot, krot, vb, orr, don, gnw)


def _ret_post_bwd(dq, dk, dv, cos, sin):
    t = dq.shape[0]
    tm = min(512, t)
    hd = RET_HEAD_DIM

    def body(dq_ref, dk_ref, dv_ref, c_ref, s_ref, oq_ref, ok_ref, ov_ref):
        cc = jnp.concatenate([c_ref[...]] * 4, axis=-1)
        ss = jnp.concatenate([s_ref[...]] * 4, axis=-1)
        oq_ref[...] = _rope_bwd(dq_ref[...], cc, ss, hd // 4).astype(BF16)
        ok_ref[...] = (_rope_bwd(dk_ref[...], cc, ss, hd // 4) * (hd ** -0.5)).astype(BF16)
        ov_ref[...] = dv_ref[...].astype(BF16)

    blk = pl.BlockSpec((tm, 512), lambda i: (i, 0))
    tab = pl.BlockSpec((tm, 128), lambda i: (i, 0))
    return pl.pallas_call(
        body, name="ret_post_bwd", grid=(t // tm,),
        in_specs=[blk, blk, blk, tab, tab], out_specs=[blk, blk, blk],
        out_shape=[SDS((t, 512), BF16)] * 3,
        compiler_params=_params(("parallel",)),
    )(dq, dk, dv, cos, sin)


def _attn_bwd(q, k, kt, v, doa, oa, lse, ex=None):
    t = q.shape[1]
    tq = min(256, t)
    nq = t // tq
    nk, tk = kt.shape[1], kt.shape[3]
    hd = ATTN_HEAD_DIM
    scale = hd ** -0.5

    def body(q_ref, k_ref, kt_ref, v_ref, do_ref, o_ref, lse_ref, dq_ref, dk_ref, dv_ref):
        p, i = pl.program_id(0), pl.program_id(1)

        @pl.when(jnp.logical_and(p % 2 == 0, i == 0))
        def _():
            dk_ref[...] = jnp.zeros_like(dk_ref)
            dv_ref[...] = jnp.zeros_like(dv_ref)

        dov, ov = do_ref[...], o_ref[...]
        outs = []
        for j in range(2):
            qq = q_ref[j]
            do32 = dov[:, j * hd:(j + 1) * hd]
            do = do32.astype(BF16)
            dd = jnp.sum((do32 * ov[:, j * hd:(j + 1) * hd]).T, axis=0, keepdims=True)
            lse_j = lse_ref[j]

            def step(c, dqt, qq=qq, do=do, dd=dd, lse_j=lse_j):
                sl = pl.ds(pl.multiple_of(c * tk, tk), tk)
                pt = jnp.exp(_dot(k_ref[0, sl, :], qq, NT) - lse_j)
                dpt = _dot(v_ref[0, sl, :], do, NT)
                dst = (pt * (dpt - dd)).astype(BF16)
                dv_ref[0, sl, :] += _dot(pt.astype(BF16), do)
                dk_ref[0, sl, :] += _dot(dst, qq)
                return dqt + _dot(kt_ref[0, c], dst)

            dqt = lax.fori_loop(0, nk, step, jnp.zeros((hd, tq), F32), unroll=True)
            outs.append(dqt.T * scale)
        dq_ref[...] = jnp.concatenate(outs, axis=-1)

    kv = pl.BlockSpec((1, t, hd), lambda p, i: (p // 2, 0, 0))
    pair = pl.BlockSpec((tq, 128), lambda p, i: (i, p))
    first = lambda: jnp.logical_and(pl.program_id(0) == 0, pl.program_id(1) == 0)
    last = lambda: jnp.logical_and(pl.program_id(0) == 3, pl.program_id(1) == nq - 1)
    xi, xo, xs, xscr, xargs = _ex_args(ex)
    return pl.pallas_call(
        _with_exchange(body, 7, 3, 0, ex, first, last), name="attn_bwd", grid=(4, nq),
        in_specs=[pl.BlockSpec((2, tq, hd), lambda p, i: (p, i, 0)), kv,
                  pl.BlockSpec((1, nk, hd, tk), lambda p, i: (p // 2, 0, 0, 0)), kv, pair, pair,
                  pl.BlockSpec((2, 1, tq), lambda p, i: (p, 0, i))] + xi,
        out_specs=[pair, kv, kv] + xo,
        out_shape=[SDS((t, ATTN_WIDTH), F32), SDS((ATTN_KV_HEADS, t, hd), F32),
                   SDS((ATTN_KV_HEADS, t, hd), F32)] + xs,
        scratch_shapes=xscr,
        compiler_params=_params(("arbitrary", "arbitrary")),
    )(q, k, kt, v, doa, oa, lse, *xargs)


def _attn_post_bwd(dq, dk, dv, z, qn, kn, cos, sin, ones_bd):
    t = z.shape[0]
    tm = min(512, t)
    n = t // tm
    hd = ATTN_HEAD_DIM

    def body(dq_ref, dk_ref, dv_ref, zq_ref, zkv_ref, qn_ref, kn_ref, c_ref, s_ref, b_ref,
             dz_ref, dqn_ref, dkn_ref, acc_q, acc_k):
        i = pl.program_id(0)

        @pl.when(i == 0)
        def _():
            acc_q[...] = jnp.zeros_like(acc_q)
            acc_k[...] = jnp.zeros_like(acc_k)

        bd = b_ref[...]
        c2, s2 = c_ref[...], s_ref[...]

        def norm_bwd(dy, x, w, ones, cos_t, sin_t, acc):
            dyr = _rope_bwd(dy, cos_t, sin_t, hd // 4)
            r = lax.rsqrt(_group_mean(x * x, ones) + EPS)
            xh = x * r
            gy = dyr * w
            acc[...] += jnp.sum((dyr * xh).reshape(tm // 8, 8, x.shape[-1]), axis=0)
            return r * (gy - xh * _group_mean(gy * xh, ones))

        cq = jnp.concatenate([c2] * 4, axis=-1)
        sq = jnp.concatenate([s2] * 4, axis=-1)
        dz_ref[:, :512] = norm_bwd(dq_ref[...], zq_ref[...], qn_ref[...], bd, cq, sq, acc_q).astype(BF16)
        zkv = zkv_ref[...]
        dkk = jnp.concatenate([dk_ref[0], dk_ref[1]], axis=-1)
        dz_ref[:, 512:640] = norm_bwd(dkk, zkv[:, :128], kn_ref[...], bd[:128, :128], c2, s2, acc_k).astype(BF16)
        dz_ref[:, 640:768] = jnp.concatenate([dv_ref[0], dv_ref[1]], axis=-1).astype(BF16)

        @pl.when(i == n - 1)
        def _():
            dqn_ref[...] = jnp.sum(acc_q[...], axis=0, keepdims=True)
            dkn_ref[...] = jnp.sum(acc_k[...], axis=0, keepdims=True)

    kv_blk = SEG["ka"][2] // 256
    kvs = pl.BlockSpec((ATTN_KV_HEADS, tm, hd), lambda i: (0, i, 0))
    const = lambda shape: pl.BlockSpec(shape, lambda i: (0, 0))
    return pl.pallas_call(
        body, name="attn_post_bwd", grid=(n,),
        in_specs=[pl.BlockSpec((tm, 512), lambda i: (i, 0)), kvs, kvs,
                  pl.BlockSpec((tm, 512), lambda i: (i, 0)), pl.BlockSpec((tm, 256), lambda i: (i, kv_blk)),
                  const((1, 512)), const((1, 128)),
                  pl.BlockSpec((tm, 128), lambda i: (i, 0)), pl.BlockSpec((tm, 128), lambda i: (i, 0)),
                  const((512, 512))],
        out_specs=[pl.BlockSpec((tm, 768), lambda i: (i, 0)), const((1, 512)), const((1, 128))],
        out_shape=[SDS((t, 768), BF16), SDS((1, 512), F32), SDS((1, 128), F32)],
        scratch_shapes=[pltpu.VMEM((8, 512), F32), pltpu.VMEM((8, 128), F32)],
        compiler_params=_params(("arbitrary",)),
    )(dq, dk, dv, z, z, qn, kn, cos, sin, ones_bd)


def _in_bwd(dxo, x, g, w_t, dz_a, dz_m, dqr, dkr, dvr, ex=None):
    t, d = x.shape
    tm = min(256, t)
    n = t // tm
    parts = [(0, 0, 768, 0), (1, 0, 512, SEG["ga"][0]), (2, 0, 512, SEG["qr"][0]), (3, 0, 512, SEG["kr"][0]),
             (4, 0, 512, SEG["vr"][0]), (1, 512, 2560, SEG["gr"][0])]

    def body(dx_ref, x_ref, g_ref, w_ref, a_ref, m_ref, q_ref, k_ref, v_ref, o_ref, dg_ref, acc):
        i = pl.program_id(0)

        @pl.when(i == 0)
        def _():
            acc[...] = jnp.zeros_like(acc)

        pieces = [a_ref, m_ref, q_ref, k_ref, v_ref]
        dh = jnp.zeros((tm, d), F32)
        for pi, lo, w, row in parts:
            dh = dh + _dot(pieces[pi][:, lo:lo + w], w_ref[row:row + w, :])
        xv = x_ref[...]
        r = lax.rsqrt(jnp.mean(xv * xv, axis=-1, keepdims=True) + EPS)
        xh = xv * r
        gy = dh * g_ref[...]
        o_ref[...] = dx_ref[...] + r * (gy - xh * jnp.mean(gy * xh, axis=-1, keepdims=True))
        acc[...] += jnp.sum((dh * xh).reshape(tm // 8, 8, d), axis=0)

        @pl.when(i == n - 1)
        def _():
            dg_ref[...] = jnp.sum(acc[...], axis=0, keepdims=True)

    row = lambda w: pl.BlockSpec((tm, w), lambda i: (i, 0))
    const = lambda shape: pl.BlockSpec(shape, lambda i: (0, 0))
    xi, xo, xs, xscr, xargs = _ex_args(ex)
    return pl.pallas_call(
        _with_exchange(body, 9, 2, 1, ex, lambda: pl.program_id(0) == 0, lambda: pl.program_id(0) == n - 1),
        name="in_bwd", grid=(n,),
        in_specs=[row(d), row(d), const((1, d)), const((D_IN, d)), row(768), row(3072), row(512), row(512),
                  row(512)] + xi,
        out_specs=[row(d), const((1, d))] + xo,
        out_shape=[SDS((t, d), F32), SDS((1, d), F32)] + xs,
        scratch_shapes=[pltpu.VMEM((8, d), F32)] + xscr,
        compiler_params=_params(("arbitrary",)),
    )(dxo, x, g, w_t, dz_a, dz_m, dqr, dkr, dvr, *xargs)


def _dw_in(h_t, dz_a, dz_m, dqr, dkr, dvr):
    d, t = h_t.shape
    tn = 256
    parts = [(0, 0, 0, 3), (1, 0, SEG["ga"][0] // tn, 2), (2, 0, SEG["qr"][0] // tn, 2),
             (3, 0, SEG["kr"][0] // tn, 2), (4, 0, SEG["vr"][0] // tn, 2), (1, 2, SEG["gr"][0] // tn, 10)]
    pieces = [dz_a, dz_m, dqr, dkr, dvr]

    def col_block(pi):
        mine = [(c0, r0, n) for q, c0, r0, n in parts if q == pi]

        def index(j):
            c0, r0, n = mine[0]
            blk = c0 + jnp.clip(j - r0, 0, n - 1)
            for c0, r0, n in mine[1:]:
                blk = jnp.where(j >= r0, c0 + jnp.clip(j - r0, 0, n - 1), blk)
            return 0, blk

        return index

    def body(h_ref, *refs):
        o_ref = refs[-1]
        j = pl.program_id(0)
        for pi, _, r0, n in parts:
            @pl.when(jnp.logical_and(j >= r0, j < r0 + n))
            def _(p_ref=refs[pi]):
                o_ref[...] = _dot(h_ref[...], p_ref[...]).T.astype(BF16)

    return pl.pallas_call(
        body, name="dw_in", grid=(D_IN // tn,),
        in_specs=[pl.BlockSpec((d, t), lambda j: (0, 0))] + [pl.BlockSpec((t, tn), col_block(pi)) for pi in range(5)],
        out_specs=pl.BlockSpec((tn, d), lambda j: (j, 0)),
        out_shape=SDS((D_IN, d), BF16),
        compiler_params=_params(("arbitrary",)),
    )(h_t, *pieces)


def _adamw_math(w, g, m, v):
    mn = ADAM_B1 * m + (1.0 - ADAM_B1) * g
    vn = ADAM_B2 * v + (1.0 - ADAM_B2) * (g * g)
    m_hat = mn / (1.0 - ADAM_B1 ** ADAM_STEP)
    v_hat = vn / (1.0 - ADAM_B2 ** ADAM_STEP)
    return -ADAM_LR * (m_hat / (jnp.sqrt(v_hat) + ADAM_EPS) + ADAM_WD * w), mn, vn


def _sum_adamw(recvs, w, m, v, lane0, tn):
    depth, r, c = w.shape
    j0 = lane0 // tn

    def body(r0_ref, r1_ref, w_ref, m_ref, v_ref, g_ref, d_ref, mo_ref, vo_ref):
        def run(r_ref):
            g = r_ref[0].astype(F32)
            for s in range(1, N_DEV):
                g = g + r_ref[s].astype(F32)
            g_ref[0] = g
            d_ref[0], mo_ref[0], vo_ref[0] = _adamw_math(w_ref[0], g, m_ref[0], v_ref[0])

        for l, r_ref in enumerate((r0_ref, r1_ref)):
            pl.when(pl.program_id(0) == l)(functools.partial(run, r_ref))

    slots = pl.BlockSpec((N_DEV, r, tn), lambda l, j: (0, 0, j0 + j))
    blk = pl.BlockSpec((1, r, tn), lambda l, j: (l, 0, j))
    return pl.pallas_call(
        body, name="sum_adamw", grid=(depth, c // tn),
        in_specs=[slots, slots, blk, blk, blk], out_specs=[blk] * 4, out_shape=[SDS(w.shape, F32)] * 4,
        compiler_params=_params(("parallel", "parallel")),
    )(recvs[0], recvs[1], w, m, v)


def _adamw(w, g, m, v):
    rows, cols = w.shape
    tr = 256 if rows % 256 == 0 else rows

    def body(w_ref, g_ref, m_ref, v_ref, d_ref, mo_ref, vo_ref):
        d_ref[...], mo_ref[...], vo_ref[...] = _adamw_math(w_ref[...], g_ref[...], m_ref[...], v_ref[...])

    blk = pl.BlockSpec((tr, cols), lambda i: (i, 0))
    return pl.pallas_call(
        body, name="adamw", grid=(rows // tr,),
        in_specs=[blk] * 4, out_specs=[blk] * 3, out_shape=[SDS((rows, cols), F32)] * 3,
        compiler_params=_params(("parallel",)),
    )(w, g, m, v)


def _all_gather(shards):
    na = len(shards)
    chips = (4, 2, 6)

    def body(*refs):
        ins, outs = refs[:na], refs[na:2 * na]
        send_sems, recv_sems, local_sems = refs[2 * na:]
        _, mine = _flip(0)

        def rows(a, idx):
            r = shards[a].shape[0]
            return outs[a].at[pl.ds(pl.multiple_of(idx * r, 16), r), :]

        def copy(a, slot, block_idx, to, src=None):
            return pltpu.make_async_remote_copy(
                src_ref=rows(a, block_idx) if src is None else src, dst_ref=rows(a, block_idx),
                send_sem=send_sems.at[a, slot], recv_sem=recv_sems.at[a, slot],
                device_id=to, device_id_type=MESH_ID)

        sibling, sibling_idx = _flip(1)
        local, started = [], []
        for a in range(na):
            cp = pltpu.make_async_copy(ins[a], rows(a, mine), local_sems.at[a])
            cp.start()
            local.append(cp)
            first = [copy(a, 0, mine, sibling, src=ins[a])]
            first += [copy(a, 1 + j, mine, _flip(k)[0], src=ins[a]) for j, k in enumerate(chips)]
            for cp in first:
                cp.start()
            started += first
        for a in range(na):
            for j, k in enumerate(chips):
                _, theirs = _flip(k)
                copy(a, 1 + j, theirs, _flip(0)[0]).wait_recv()
                fwd = copy(a, 4 + j, theirs, sibling)
                fwd.start()
                started.append(fwd)
        for a in range(na):
            copy(a, 0, sibling_idx, _flip(0)[0]).wait_recv()
            for j, k in enumerate(chips):
                _, theirs = _flip(k | 1)
                copy(a, 4 + j, theirs, _flip(0)[0]).wait_recv()
        for cp in started:
            cp.wait_send()
        for cp in local:
            cp.wait()

    return pl.pallas_call(
        body, name="all_gather_weights",
        in_specs=[ANY] * na, out_specs=[ANY] * na,
        out_shape=[SDS((N_DEV * s.shape[0], s.shape[1]), s.dtype) for s in shards],
        scratch_shapes=[pltpu.SemaphoreType.DMA((na, 7)), pltpu.SemaphoreType.DMA((na, 7)),
                        pltpu.SemaphoreType.DMA((na,))],
        compiler_params=pltpu.CompilerParams(has_side_effects=True),
    )(*shards)


def _all_reduce_small(packed):
    shape = packed.shape

    def body(p_ref, o_ref, slots, send_sems, recv_sems):
        me, mine = _flip(0)
        slots[mine] = p_ref[...]
        sends = []
        for k in range(1, N_DEV):
            peer, _ = _flip(k)
            cp = pltpu.make_async_remote_copy(
                src_ref=p_ref, dst_ref=slots.at[mine], send_sem=send_sems.at[k - 1], recv_sem=recv_sems.at[k - 1],
                device_id=peer, device_id_type=MESH_ID)
            cp.start()
            sends.append(cp)
        for k in range(1, N_DEV):
            _, theirs = _flip(k)
            pltpu.make_async_remote_copy(
                src_ref=p_ref, dst_ref=slots.at[theirs], send_sem=send_sems.at[k - 1],
                recv_sem=recv_sems.at[k - 1], device_id=me, device_id_type=MESH_ID).wait_recv()
        for cp in sends:
            cp.wait_send()
        acc = slots[0]
        for s in range(1, N_DEV):
            acc = acc + slots[s]
        o_ref[...] = acc

    vm = pl.BlockSpec(memory_space=pltpu.VMEM)
    return pl.pallas_call(
        body, name="all_reduce_small", in_specs=[vm], out_specs=vm, out_shape=SDS(shape, F32),
        scratch_shapes=[pltpu.VMEM((N_DEV,) + shape, F32), pltpu.SemaphoreType.DMA((7,)),
                        pltpu.SemaphoreType.DMA((7,))],
        compiler_params=pltpu.CompilerParams(has_side_effects=True),
    )(packed)


def _layer_fwd(x, p, tabs, ex):
    z, h_t = _in_proj(x, p["norm_g"], p["w_in_t"])
    q, k, v, kt, vt = _attn_prep(z, p["qn"], p["kn"], tabs["ca"], tabs["sa"], tabs["ones"])
    oa, lse, *gathered = _attn_fwd(q, k, vt, ex)
    qrot, krot, vb, orr, on = _ret_fwd(z, p["lgf"], p["lgb"], p["gnw"], tabs["cr"], tabs["sr"])
    return z, h_t, q, k, v, kt, lse, oa, qrot, krot, vb, orr, on, gathered


def _layer_bwd(dxo, s, p, tabs, ex_attn, make_ex_in):
    doa, don, dz_m, d_wout, d_wb_t = _merge_bwd(dxo, s["z"], s["oa"], s["on"], s["ya"], s["yb"], p["wb_t"], p["w_out"])
    dq_a, dk_a, dv_a, *recv_attn = _attn_bwd(s["q"], s["k"], s["kt"], s["v"], doa, s["oa"], s["lse"],
                                              ex_attn(d_wb_t, d_wout))
    dz_a, d_qn, d_kn = _attn_post_bwd(dq_a, dk_a, dv_a, s["z"], p["qn"], p["kn"], tabs["ca"], tabs["sa"],
                                      tabs["ones"])
    dq_r, dk_r, dv_r, d_gnw, d_lgf, d_lgb = _ret_bwd(s["qrot"], s["krot"], s["vb"], s["orr"], don, p["gnw"],
                                                     p["lgf"], p["lgb"])
    dqr, dkr, dvr = _ret_post_bwd(dq_r, dk_r, dv_r, tabs["cr"], tabs["sr"])
    buf = _dw_in(s["h_t"], dz_a, dz_m, dqr, dkr, dvr)
    dx, d_norm_g, *recv_in = _in_bwd(dxo, s["x"], p["norm_g"], p["w_in_t"], dz_a, dz_m, dqr, dkr, dvr,
                                     make_ex_in(buf))
    grads = dict(w_in_t=buf, wb_t=d_wb_t, w_out=d_wout, norm_g=d_norm_g, gnw=d_gnw,
                 qn=d_qn.reshape(ATTN_Q_HEADS, ATTN_HEAD_DIM).sum(axis=0),
                 kn=d_kn.reshape(ATTN_KV_HEADS, ATTN_HEAD_DIM).sum(axis=0),
                 lgf=d_lgf[:, 0, 0], lgb=d_lgb[:, 0, 0])
    return dx, grads, recv_attn, recv_in


def _adamw_nd(w, g, m, v):
    shape = w.shape
    two_d = (1, shape[0]) if w.ndim == 1 else (-1, shape[-1])
    out = _adamw(w.reshape(two_d), g.reshape(two_d), m.reshape(two_d), v.reshape(two_d))
    return tuple(o.reshape(shape) for o in out)


def kernel(x, norm_g, w_in, attn_q_norm, attn_k_norm, ret_decay_fwd, ret_decay_bwd, ret_gn_w, w_branch_attn, w_branch_ret, w_out, final_norm_g, loss_target, m_norm_g, m_w_in, m_attn_q_norm, m_attn_k_norm, m_ret_decay_fwd, m_ret_decay_bwd, m_ret_gn_w, m_w_branch_attn, m_w_branch_ret, m_w_out, m_final_norm_g, v_norm_g, v_w_in, v_attn_q_norm, v_attn_k_norm, v_ret_decay_fwd, v_ret_decay_bwd, v_ret_gn_w, v_w_branch_attn, v_w_branch_ret, v_w_out, v_final_norm_g):
    t, d = x.shape[1], x.shape[2]
    x2, target = x[0], loss_target[0]

    w_in_sh, wb_sh, wout_sh = [], [], []
    for l in range(DEPTH):
        w_in_sh.append(jnp.swapaxes(w_in[l], 0, 1).astype(BF16))
        wb_sh.append(jnp.concatenate([w_branch_attn[l].T, w_branch_ret[l].T], axis=1).astype(BF16))
        wout_sh.append(w_out[l].astype(BF16))

    ca, sa = _rope_tables(t, ATTN_HEAD_DIM)
    cr, sr = _rope_tables(t, RET_HEAD_DIM)
    grp = jnp.arange(ATTN_WIDTH) // ATTN_HEAD_DIM
    tabs = dict(ca=jnp.tile(ca, (1, 2)), sa=jnp.tile(sa, (1, 2)), cr=cr, sr=sr,
                ones=jnp.where(grp[:, None] == grp[None, :], 1.0 / ATTN_HEAD_DIM, 0.0).astype(BF16))
    layers = []
    for l in range(DEPTH):
        layers.append(dict(
            norm_g=norm_g[l][None], qn=jnp.tile(attn_q_norm[l], ATTN_Q_HEADS)[None],
            kn=jnp.tile(attn_k_norm[l], ATTN_KV_HEADS)[None], gnw=ret_gn_w[l][None],
            lgf=jax.nn.log_sigmoid(ret_decay_fwd[l]), lgb=jax.nn.log_sigmoid(ret_decay_bwd[l])))

    layers[0]["w_in_t"], = _all_gather([w_in_sh[0]])
    gathers = [_Exchange("gather", [wb_sh[0], wout_sh[0], w_in_sh[1]]), _Exchange("gather", [wb_sh[1], wout_sh[1]])]
    h = x2
    saved = []
    for l in range(DEPTH):
        p = layers[l]
        z, h_t, q, k, v, kt, lse, oa, qrot, krot, vb, orr, on, got = _layer_fwd(h, p, tabs, gathers[l])
        p["wb_t"], p["w_out"] = got[0], got[1]
        if l == 0:
            layers[1]["w_in_t"] = got[2]
        xn, ya, yb = _merge_fwd(h, z, oa, on, p["wb_t"], p["w_out"])
        saved.append(dict(x=h, z=z, h_t=h_t, q=q, k=k, v=v, kt=kt, lse=lse, oa=oa, qrot=qrot, krot=krot, vb=vb,
                          orr=orr, on=on, ya=ya, yb=yb))
        h = xn
    dx, d_final_g, loss_part = _final_loss(h, final_norm_g[None], target)

    grads = [None] * DEPTH
    none = lambda *a: None
    dx, grads[1], _, _ = _layer_bwd(dx, saved[1], layers[1], tabs, none, none)
    g1 = grads[1]
    ex_attn = lambda d_wb_t, d_wout: _Exchange("scatter", [g1["w_in_t"], g1["wb_t"], g1["w_out"], d_wb_t, d_wout])
    ex_in = lambda d_w_in_t: _Exchange("scatter", [d_w_in_t])
    dx, grads[0], recv_attn, recv_in = _layer_bwd(dx, saved[0], layers[0], tabs, ex_attn, ex_in)
    recv = [recv_in[0], recv_attn[3], recv_attn[4], recv_attn[0], recv_attn[1], recv_attn[2]]
    tr = lambda a: jnp.swapaxes(a, 1, 2)
    sharded = {}
    sharded[id(w_in)] = [tr(o) for o in _sum_adamw([recv[0], recv[3]], tr(w_in), tr(m_w_in), tr(v_w_in), 0, 256)]
    sharded[id(w_branch_attn)] = [tr(o) for o in _sum_adamw(
        [recv[1], recv[4]], tr(w_branch_attn), tr(m_w_branch_attn), tr(v_w_branch_attn), 0, 512)]
    sharded[id(w_branch_ret)] = [tr(o) for o in _sum_adamw(
        [recv[1], recv[4]], tr(w_branch_ret), tr(m_w_branch_ret), tr(v_w_branch_ret), 512, 512)]
    sharded[id(w_out)] = _sum_adamw([recv[2], recv[5]], w_out, m_w_out, v_w_out, 0, 256)
    g_w_in, g_wba, g_wbr, g_wout = (sharded[id(w)][0] for w in (w_in, w_branch_attn, w_branch_ret, w_out))

    packed = jnp.zeros((8, 1024), F32)
    for l in range(DEPTH):
        gl = grads[l]
        packed = packed.at[l].set(gl["norm_g"][0])
        packed = packed.at[2, 512 * l:512 * (l + 1)].set(gl["gnw"][0])
        packed = packed.at[4, 128 * l:128 * l + 64].set(gl["qn"])
        packed = packed.at[4, 256 + 128 * l:256 + 128 * l + 64].set(gl["kn"])
        packed = packed.at[4, 512 + 128 * l:512 + 128 * l + 4].set(gl["lgf"])
        packed = packed.at[4, 768 + 128 * l:768 + 128 * l + 4].set(gl["lgb"])
    packed = packed.at[3].set(d_final_g[0])
    packed = packed.at[5, 0].set(loss_part[0, 0])
    red = _all_reduce_small(packed)
    loss = red[5, 0]
    g_norm_g = red[0:2]
    g_gnw = red[2].reshape(DEPTH, RET_WIDTH)
    g_final = red[3]
    g_qn = jnp.stack([red[4, 128 * l:128 * l + 64] for l in range(DEPTH)])
    g_kn = jnp.stack([red[4, 256 + 128 * l:256 + 128 * l + 64] for l in range(DEPTH)])
    g_lgf = jnp.stack([red[4, 512 + 128 * l:512 + 128 * l + 4] for l in range(DEPTH)])
    g_lgb = jnp.stack([red[4, 768 + 128 * l:768 + 128 * l + 4] for l in range(DEPTH)])
    g_df = g_lgf * jax.nn.sigmoid(-ret_decay_fwd)
    g_db = g_lgb * jax.nn.sigmoid(-ret_decay_bwd)

    grad_w = [g_norm_g, g_w_in, g_qn, g_kn, g_df, g_db, g_gnw, g_wba, g_wbr, g_wout, g_final]
    weights = [norm_g, w_in, attn_q_norm, attn_k_norm, ret_decay_fwd, ret_decay_bwd, ret_gn_w, w_branch_attn,
               w_branch_ret, w_out, final_norm_g]
    ms = [m_norm_g, m_w_in, m_attn_q_norm, m_attn_k_norm, m_ret_decay_fwd, m_ret_decay_bwd, m_ret_gn_w,
          m_w_branch_attn, m_w_branch_ret, m_w_out, m_final_norm_g]
    vs = [v_norm_g, v_w_in, v_attn_q_norm, v_attn_k_norm, v_ret_decay_fwd, v_ret_decay_bwd, v_ret_gn_w,
          v_w_branch_attn, v_w_branch_ret, v_w_out, v_final_norm_g]
    upd = [sharded[id(w)][1:] if id(w) in sharded else _adamw_nd(w, g, m, v)
           for w, g, m, v in zip(weights, grad_w, ms, vs)]
    return (loss, dx[None], *grad_w, *[u[0] for u in upd], *[u[1] for u in upd], *[u[2] for u in upd])
```

```python
import functools

import jax
import jax.numpy as jnp
from jax import lax
from jax.experimental import pallas as pl
from jax.experimental.pallas import tpu as pltpu

F32 = jnp.float32
BF16 = jnp.bfloat16
SDS = jax.ShapeDtypeStruct

D_MODEL = 1024
DEPTH = 2
GRID_W = 64
ATTN_Q_HEADS = 8
ATTN_KV_HEADS = 2
ATTN_HEAD_DIM = 64
ATTN_WIDTH = 512
ATTN_KV_WIDTH = 128
RET_HEADS = 4
RET_HEAD_DIM = 128
RET_WIDTH = 512
RET_CHUNK = 128
ATTN_KEY_CHUNK = 512
ATTN_BWD_KEY_CHUNK = 1024
EXP_LAG = 3
ROPE_THETA = 10000.0
EPS = 1e-6
D_IN = 5376
N_DEV = 8

ADAM_LR = 0.001
ADAM_B1 = 0.9
ADAM_B2 = 0.999
ADAM_EPS = 1e-08
ADAM_WD = 0.01
ADAM_STEP = 10

SEG = {
    "qa": (0, 512, 0),
    "ga": (768, 512, 512),
    "qr": (1280, 512, 1024),
    "kr": (1792, 512, 1536),
    "vr": (2304, 512, 2048),
    "gr": (2816, 512, 2560),
    "gm": (3328, 2048, 3072),
    "ka": (512, 128, 5120),
    "va": (640, 128, 5248),
}

VMEM_LIMIT = 60 * 1024 * 1024
NT = (((1,), (1,)), ((), ()))
TN = (((0,), (0,)), ((), ()))
MESH_ID = pl.DeviceIdType.MESH
ANY = pl.BlockSpec(memory_space=pl.ANY)


def _params(sem=None, vmem=VMEM_LIMIT):
    return pltpu.CompilerParams(dimension_semantics=sem, vmem_limit_bytes=vmem)


def _dot(a, b, dims=None):
    if dims is None:
        return jnp.dot(a, b, preferred_element_type=F32)
    return lax.dot_general(a, b, dims, preferred_element_type=F32)


def _sigmoid(x):
    return 1.0 / (1.0 + jnp.exp(-x))


def _swap_halves(x, q):
    n = x.shape[-1]
    axis = x.ndim - 1
    lane = lax.broadcasted_iota(jnp.int32, x.shape, axis)
    first = (lane % (2 * q)) < q
    return jnp.where(first, pltpu.roll(x, n - q, axis), pltpu.roll(x, q, axis))


def _rope(x, cos, sin_signed, q):
    return x * cos + _swap_halves(x, q) * sin_signed


def _rope_bwd(dy, cos, sin_signed, q):
    return dy * cos - _swap_halves(dy, q) * sin_signed


def _group_mean(v, ones_bd):
    hi = v.astype(BF16)
    r1 = v - hi.astype(F32)
    mid = r1.astype(BF16)
    lo = (r1 - mid.astype(F32)).astype(BF16)
    return _dot(hi, ones_bd) + _dot(mid, ones_bd) + _dot(lo, ones_bd)


def _rope_tables(t, head_dim):
    n_rows = t // GRID_W
    d_axis = head_dim // 2
    inv_freq = ROPE_THETA ** (-jnp.arange(0, d_axis, 2, dtype=F32) / d_axis)
    ar = jnp.arange(n_rows, dtype=F32)[:, None] * inv_freq
    ac = jnp.arange(GRID_W, dtype=F32)[:, None] * inv_freq
    by_row = lambda a: jnp.repeat(a, GRID_W, axis=0)
    by_col = lambda a: jnp.tile(a, (n_rows, 1))
    cr, sr, cc, sc = by_row(jnp.cos(ar)), by_row(jnp.sin(ar)), by_col(jnp.cos(ac)), by_col(jnp.sin(ac))
    return jnp.concatenate([cr, cr, cc, cc], axis=-1), jnp.concatenate([-sr, sr, -sc, sc], axis=-1)


def _me():
    return lax.axis_index("x"), lax.axis_index("y"), lax.axis_index("c")


def _flip(k):
    x, y, c = _me()
    px = 1 - x if k & 4 else x
    py = 1 - y if k & 2 else y
    pc = 1 - c if k & 1 else c
    return (px, py, pc), 4 * px + 2 * py + pc


class _Exchange:
    def __init__(self, kind, srcs):
        self.kind, self.srcs, self.n = kind, list(srcs), len(srcs)
        self.rows = [a.shape[0] if kind == "gather" else a.shape[0] // N_DEV for a in srcs]
        if kind == "gather":
            self.out_shape = [SDS((N_DEV * a.shape[0], a.shape[1]), a.dtype) for a in srcs]
        else:
            self.out_shape = [SDS((N_DEV, a.shape[0] // N_DEV, a.shape[1]), a.dtype) for a in srcs]
        self.scratch = [pltpu.SemaphoreType.DMA((self.n, N_DEV - 1)), pltpu.SemaphoreType.DMA((self.n, N_DEV - 1)),
                        pltpu.SemaphoreType.DMA((self.n,))]

    def _block(self, ref, a, idx):
        r = self.rows[a]
        return ref.at[pl.ds(pl.multiple_of(idx * r, 16), r), :]

    def _src(self, ins, a, idx):
        return ins[a] if self.kind == "gather" else self._block(ins[a], a, idx)

    def _dst(self, outs, a, idx):
        return self._block(outs[a], a, idx) if self.kind == "gather" else outs[a].at[idx]

    def _copies(self, ins, outs, sems):
        send_sems, recv_sems, local_sems = sems
        me, mine = _flip(0)
        local, sends, recvs = [], [], []
        for a in range(self.n):
            local.append(pltpu.make_async_copy(self._src(ins, a, mine), self._dst(outs, a, mine), local_sems.at[a]))
            for k in range(1, N_DEV):
                peer, theirs = _flip(k)
                sem = dict(send_sem=send_sems.at[a, k - 1], recv_sem=recv_sems.at[a, k - 1])
                sends.append(pltpu.make_async_remote_copy(
                    src_ref=self._src(ins, a, theirs), dst_ref=self._dst(outs, a, mine),
                    device_id=peer, device_id_type=MESH_ID, **sem))
                recvs.append(pltpu.make_async_remote_copy(
                    src_ref=self._dst(outs, a, theirs), dst_ref=self._dst(outs, a, theirs),
                    device_id=me, device_id_type=MESH_ID, **sem))
        return local, sends, recvs

    def start(self, ins, outs, sems):
        local, sends, _ = self._copies(ins, outs, sems)
        for cp in local + sends:
            cp.start()

    def wait(self, ins, outs, sems):
        local, sends, recvs = self._copies(ins, outs, sems)
        for cp in sends:
            cp.wait_send()
        for cp in recvs:
            cp.wait_recv()
        for cp in local:
            cp.wait()


def _with_exchange(body, n_in, n_out, n_scratch, ex, first, last):
    if ex is None:
        return body

    def wrapped(*refs):
        ins = refs[:n_in]
        ex_ins = refs[n_in:n_in + ex.n]
        outs = refs[n_in + ex.n:n_in + ex.n + n_out]
        ex_outs = refs[n_in + ex.n + n_out:n_in + 2 * ex.n + n_out]
        rest = refs[n_in + 2 * ex.n + n_out:]
        scratch, sems = rest[:n_scratch], rest[n_scratch:]

        @pl.when(first())
        def _():
            ex.start(ex_ins, ex_outs, sems)

        body(*ins, *outs, *scratch)

        @pl.when(last())
        def _():
            ex.wait(ex_ins, ex_outs, sems)

    return wrapped


def _ex_args(ex):
    if ex is None:
        return [], [], [], [], []
    return [ANY] * ex.n, [ANY] * ex.n, list(ex.out_shape), list(ex.scratch), list(ex.srcs)


def _in_proj(x, g, w_t):
    t, d = x.shape
    tm = min(256, t)

    def body(x_ref, g_ref, w_ref, z_ref, ht_ref):
        xv = x_ref[...]
        r = lax.rsqrt(jnp.mean(xv * xv, axis=-1, keepdims=True) + EPS)
        h = xv * r * g_ref[...]
        ht_ref[...] = h.T.astype(BF16)
        hb = h.astype(BF16)
        for nat, w, off in SEG.values():
            z_ref[:, off:off + w] = _dot(hb, w_ref[nat:nat + w, :], NT)

    return pl.pallas_call(
        body, name="in_proj", grid=(t // tm,),
        in_specs=[pl.BlockSpec((tm, d), lambda i: (i, 0)), pl.BlockSpec((1, d), lambda i: (0, 0)),
                  pl.BlockSpec((D_IN, d), lambda i: (0, 0))],
        out_specs=[pl.BlockSpec((tm, D_IN), lambda i: (i, 0)), pl.BlockSpec((d, tm), lambda i: (0, i))],
        out_shape=[SDS((t, D_IN), F32), SDS((d, t), BF16)],
        compiler_params=_params(("parallel",)),
    )(x, g, w_t)


def _attn_prep(z, qn, kn, cos, sin, ones_bd):
    t = z.shape[0]
    tm = min(ATTN_KEY_CHUNK, t)
    hd = ATTN_HEAD_DIM

    def body(zq_ref, zkv_ref, qn_ref, kn_ref, c_ref, s_ref, b_ref, q_out, qt_out, k_out, v_out, vt_out):
        bd = b_ref[...]
        c2, s2 = c_ref[...], s_ref[...]
        cq = jnp.concatenate([c2] * 4, axis=-1)
        sq = jnp.concatenate([s2] * 4, axis=-1)
        xq = zq_ref[...]
        yq = xq * lax.rsqrt(_group_mean(xq * xq, bd) + EPS) * qn_ref[...]
        yq = _rope(yq, cq, sq, hd // 4) * (hd ** -0.5)
        yqt = yq.T
        for h in range(ATTN_Q_HEADS):
            q_out[h] = yq[:, h * hd:(h + 1) * hd].astype(BF16)
            qt_out[h] = yqt[h * hd:(h + 1) * hd, :].astype(BF16)
        zkv = zkv_ref[...]
        xk, xv = zkv[:, :ATTN_KV_WIDTH], zkv[:, ATTN_KV_WIDTH:]
        yk = xk * lax.rsqrt(_group_mean(xk * xk, bd[:ATTN_KV_WIDTH, :ATTN_KV_WIDTH]) + EPS) * kn_ref[...]
        yk = _rope(yk, c2, s2, hd // 4)
        xvt = xv.T
        ones = jnp.ones((hd, tm), F32)
        for h in range(ATTN_KV_HEADS):
            k_out[h] = yk[:, h * hd:(h + 1) * hd].astype(BF16)
            v_out[h] = xv[:, h * hd:(h + 1) * hd].astype(BF16)
            vt_out[h, 0] = jnp.concatenate([xvt[h * hd:(h + 1) * hd, :], ones], axis=0).astype(BF16)

    kv_blk = SEG["ka"][2] // 256
    nk = t // tm
    return pl.pallas_call(
        body, name="attn_prep", grid=(nk,),
        in_specs=[pl.BlockSpec((tm, 512), lambda i: (i, 0)), pl.BlockSpec((tm, 256), lambda i: (i, kv_blk)),
                  pl.BlockSpec((1, 512), lambda i: (0, 0)), pl.BlockSpec((1, 128), lambda i: (0, 0)),
                  pl.BlockSpec((tm, 128), lambda i: (i, 0)), pl.BlockSpec((tm, 128), lambda i: (i, 0)),
                  pl.BlockSpec((512, 512), lambda i: (0, 0))],
        out_specs=[pl.BlockSpec((ATTN_Q_HEADS, tm, hd), lambda i: (0, i, 0)),
                   pl.BlockSpec((ATTN_Q_HEADS, hd, tm), lambda i: (0, 0, i)),
                   pl.BlockSpec((ATTN_KV_HEADS, tm, hd), lambda i: (0, i, 0)),
                   pl.BlockSpec((ATTN_KV_HEADS, tm, hd), lambda i: (0, i, 0)),
                   pl.BlockSpec((ATTN_KV_HEADS, 1, 2 * hd, tm), lambda i: (0, i, 0, 0))],
        out_shape=[SDS((ATTN_Q_HEADS, t, hd), BF16), SDS((ATTN_Q_HEADS, hd, t), BF16),
                   SDS((ATTN_KV_HEADS, t, hd), BF16), SDS((ATTN_KV_HEADS, t, hd), BF16),
                   SDS((ATTN_KV_HEADS, nk, 2 * hd, tm), BF16)],
        compiler_params=_params(("parallel",)),
    )(z, z, qn, kn, cos, sin, ones_bd)


def _attn_fwd(q, k, vt, ex=None):
    t = q.shape[1]
    tq = min(256, t)
    nk, tk = vt.shape[1], vt.shape[3]
    hd = ATTN_HEAD_DIM
    g = ATTN_Q_HEADS // ATTN_KV_HEADS

    def body(q_ref, k_ref, vt_ref, o_ref, lse_ref, s_scr):
        def pass_a(h, c, m8):
            half = tk // 2
            for lo in (c * tk, c * tk + half):
                st = _dot(k_ref[0, lo:lo + half, :], q_ref[h], NT)
                s_scr[h % 2, lo:lo + half, :] = st
                m8 = jnp.maximum(m8, jnp.max(st.reshape(half // 8, 8, tq), axis=0))
            return m8

        def pass_b(h, c, m, acc, after):
            e = jnp.exp(s_scr[h % 2, c * tk:(c + 1) * tk, :] - (m + after * 0.0)).astype(BF16)
            return acc + _dot(vt_ref[0, c], e)

        neg = jnp.full((8, tq), -jnp.inf, F32)
        m8 = neg
        for c in range(nk):
            m8 = pass_a(0, c, m8)
        outs = []
        for h in range(g):
            m = jnp.max(m8, axis=0, keepdims=True)
            acc = jnp.zeros((2 * hd, tq), F32)
            m8 = neg
            done = [m] * EXP_LAG
            for c in range(nk):
                if h + 1 < g:
                    m8 = pass_a(h + 1, c, m8)
                acc = pass_b(h, c, m, acc, done[-EXP_LAG])
                done.append(m8[0:1, :] if h + 1 < g else acc[hd:hd + 1, :])
            l = acc[hd:hd + 1, :]
            outs.append((acc[:hd, :] / l).T)
            lse_ref[h] = m + jnp.log(l)
        o_ref[...] = jnp.concatenate(outs, axis=-1)

    nq = t // tq
    first = lambda: jnp.logical_and(pl.program_id(0) == 0, pl.program_id(1) == 0)
    last = lambda: jnp.logical_and(pl.program_id(0) == ATTN_KV_HEADS - 1, pl.program_id(1) == nq - 1)
    xi, xo, xs, xscr, xargs = _ex_args(ex)
    return pl.pallas_call(
        _with_exchange(body, 3, 2, 1, ex, first, last), name="attn_fwd", grid=(ATTN_KV_HEADS, nq),
        in_specs=[pl.BlockSpec((g, tq, hd), lambda p, i: (p, i, 0)),
                  pl.BlockSpec((1, t, hd), lambda p, i: (p, 0, 0)),
                  pl.BlockSpec((1, nk, 2 * hd, tk), lambda p, i: (p, 0, 0, 0))] + xi,
        out_specs=[pl.BlockSpec((tq, g * hd), lambda p, i: (i, p)),
                   pl.BlockSpec((g, 1, tq), lambda p, i: (p, 0, i))] + xo,
        out_shape=[SDS((t, ATTN_WIDTH), F32), SDS((ATTN_Q_HEADS, 1, t), F32)] + xs,
        scratch_shapes=[pltpu.VMEM((2, t, tq), F32)] + xscr,
        compiler_params=_params(("arbitrary", "arbitrary")),
    )(q, k, vt, *xargs)


class _Dir:
    def __init__(self, lg, strict_future):
        c = RET_CHUNK
        ia = lax.broadcasted_iota(jnp.int32, (c, c), 0).astype(F32)
        ib = lax.broadcasted_iota(jnp.int32, (c, c), 1).astype(F32)
        col = lax.broadcasted_iota(jnp.int32, (c, 1), 0).astype(F32)
        row = lax.broadcasted_iota(jnp.int32, (1, c), 1).astype(F32)
        if strict_future:
            dist = ib - ia
            mask = dist > 0
            self.wq, self.wk, wk_row = c - col, col, row
        else:
            dist = ia - ib
            mask = dist >= 0
            self.wq, self.wk, wk_row = col + 1.0, c - 1.0 - col, c - 1.0 - row
        self.dist = jnp.maximum(dist, 0.0)
        self.d = jnp.where(mask, jnp.exp(self.dist * lg), 0.0)
        self.qd = jnp.exp(self.wq * lg)
        self.kd_col = jnp.exp(self.wk * lg)
        self.kd_row = jnp.exp(wk_row * lg)
        self.cd = jnp.exp(jnp.full((1, 1), float(c), F32) * lg)


def _ret_fwd(z, lgf, lgb, gnw, cos, sin):
    t = z.shape[0]
    c = RET_CHUNK
    nc = t // c
    hd = RET_HEAD_DIM
    unroll = 4 if nc % 4 == 0 else 1

    def body(lgf_ref, lgb_ref, q_ref, k_ref, v_ref, c_ref, s_ref, w_ref,
             qo_ref, ko_ref, vo_ref, orr_ref, on_ref, kt, uf, ub, sfa, sba):
        h = pl.program_id(0)
        fw = _Dir(lgf_ref[h], False)
        bw = _Dir(lgb_ref[h], True)
        cc, ss = c_ref[...], s_ref[...]
        qo_ref[...] = _rope(q_ref[...], cc, ss, hd // 4).astype(BF16)
        kr = _rope(k_ref[...], cc, ss, hd // 4) * (hd ** -0.5)
        ko_ref[...] = kr.astype(BF16)
        vo_ref[...] = v_ref[...].astype(BF16)
        for i in range(nc):
            kt[i] = kr[i * c:(i + 1) * c, :].T.astype(BF16)

        def rows(ci):
            return pl.ds(pl.multiple_of(ci * c, c), c)

        def kv_products(ci, carry):
            vv = vo_ref[rows(ci), :]
            ktf = kt[ci].astype(F32)
            uf[ci] = _dot((ktf * fw.kd_row).astype(BF16), vv)
            ub[ci] = _dot((ktf * bw.kd_row).astype(BF16), vv)
            return carry

        lax.fori_loop(0, nc, kv_products, 0, unroll=unroll)

        def scan(i, carry):
            sf, sb = carry
            j = nc - 1 - i
            sfa[i] = sf.astype(BF16)
            sba[j] = sb.astype(BF16)
            return sf * fw.cd + uf[i], sb * bw.cd + ub[j]

        zero = jnp.zeros((hd, hd), F32)
        lax.fori_loop(0, nc, scan, (zero, zero))
        gw = w_ref[...]

        def outputs(ci, carry):
            sl = rows(ci)
            qq, kk, vv = qo_ref[sl, :], ko_ref[sl, :], vo_ref[sl, :]
            a = _dot(qq, kk, NT)
            o = (_dot((a * fw.d).astype(BF16), vv) + _dot(qq, sfa[ci]) * fw.qd
                 + _dot((a * bw.d).astype(BF16), vv) + _dot(qq, sba[ci]) * bw.qd)
            orr_ref[sl, :] = o
            xc = o - jnp.mean(o, axis=-1, keepdims=True)
            var = jnp.mean(xc * xc, axis=-1, keepdims=True)
            on_ref[sl, :] = xc * lax.rsqrt(var + EPS) * gw
            return carry

        lax.fori_loop(0, nc, outputs, 0, unroll=unroll)

    smem = pl.BlockSpec(memory_space=pltpu.SMEM)
    col = lambda name: (lambda h: (0, SEG[name][2] // 128 + h))
    head = pl.BlockSpec((t, 128), lambda h: (0, h))
    full = pl.BlockSpec((t, 128), lambda h: (0, 0))
    return pl.pallas_call(
        body, name="ret_fwd", grid=(RET_HEADS,),
        in_specs=[smem, smem, pl.BlockSpec((t, 128), col("qr")), pl.BlockSpec((t, 128), col("kr")),
                  pl.BlockSpec((t, 128), col("vr")), full, full, pl.BlockSpec((1, 128), lambda h: (0, h))],
        out_specs=[head, head, head, head, head],
        out_shape=[SDS((t, RET_WIDTH), BF16)] * 3 + [SDS((t, RET_WIDTH), F32)] * 2,
        scratch_shapes=[pltpu.VMEM((nc, hd, c), BF16), pltpu.VMEM((nc, hd, hd), F32), pltpu.VMEM((nc, hd, hd), F32),
                        pltpu.VMEM((nc, hd, hd), BF16), pltpu.VMEM((nc, hd, hd), BF16)],
        compiler_params=_params(("parallel",)),
    )(lgf, lgb, z, z, z, cos, sin, gnw)


def _merge_fwd(x, z, oa, on, wb_t, wout):
    t, d = x.shape
    tm = min(256, t)

    def body(x_ref, ga_ref, gr_ref, gm0_ref, gm1_ref, oa_ref, on_ref, wb_ref, wo_ref, xn_ref, ya_ref, yb_ref):
        ga, gr = ga_ref[...], gr_ref[...]
        ua = ga * _sigmoid(ga) * oa_ref[...]
        ub = gr * _sigmoid(gr) * on_ref[...]
        ya = _dot(ua.astype(BF16), wb_ref[:, :512], NT)
        yb = _dot(ub.astype(BF16), wb_ref[:, 512:], NT)
        ya_ref[...] = ya
        yb_ref[...] = yb
        merged = _sigmoid(gm0_ref[...]) * ya + _sigmoid(gm1_ref[...]) * yb
        xn_ref[...] = x_ref[...] + _dot(merged.astype(BF16), wo_ref[...])

    row = lambda w, j: pl.BlockSpec((tm, w), lambda i: (i, j))
    const = lambda shape: pl.BlockSpec(shape, lambda i: (0, 0))
    return pl.pallas_call(
        body, name="merge_fwd", grid=(t // tm,),
        in_specs=[row(d, 0), row(512, SEG["ga"][2] // 512), row(512, SEG["gr"][2] // 512),
                  row(1024, SEG["gm"][2] // 1024), row(1024, SEG["gm"][2] // 1024 + 1),
                  row(512, 0), row(512, 0), const((d, 1024)), const((d, d))],
        out_specs=[row(d, 0), row(d, 0), row(d, 0)],
        out_shape=[SDS((t, d), F32)] * 3,
        compiler_params=_params(("parallel",)),
    )(x, z, z, z, z, oa, on, wb_t, wout)


def _final_loss(x, g, target):
    t, d = x.shape
    tm = min(512, t)
    n = t // tm

    def body(x_ref, g_ref, t_ref, dx_ref, dg_ref, loss_ref, acc_g, acc_l):
        i = pl.program_id(0)

        @pl.when(i == 0)
        def _():
            acc_g[...] = jnp.zeros_like(acc_g)
            acc_l[...] = jnp.zeros_like(acc_l)

        xv, gv = x_ref[...], g_ref[...]
        r = lax.rsqrt(jnp.mean(xv * xv, axis=-1, keepdims=True) + EPS)
        xh = xv * r
        err = xh * gv - t_ref[...]
        dy = err * (1.0 / d)
        gy = dy * gv
        dx_ref[...] = r * (gy - xh * jnp.mean(gy * xh, axis=-1, keepdims=True))
        acc_g[...] += jnp.sum((dy * xh).reshape(tm // 8, 8, d), axis=0)
        acc_l[...] += jnp.sum((err * err).reshape(tm // 8, 8, d), axis=0)

        @pl.when(i == n - 1)
        def _():
            dg_ref[...] = jnp.sum(acc_g[...], axis=0, keepdims=True)
            tot = jnp.sum(jnp.sum(acc_l[...], axis=0, keepdims=True), axis=1, keepdims=True)
            loss_ref[...] = jnp.broadcast_to(tot * (0.5 / d), (1, 128))

    return pl.pallas_call(
        body, name="final_loss", grid=(n,),
        in_specs=[pl.BlockSpec((tm, d), lambda i: (i, 0)), pl.BlockSpec((1, d), lambda i: (0, 0)),
                  pl.BlockSpec((tm, d), lambda i: (i, 0))],
        out_specs=[pl.BlockSpec((tm, d), lambda i: (i, 0)), pl.BlockSpec((1, d), lambda i: (0, 0)),
                   pl.BlockSpec((1, 128), lambda i: (0, 0))],
        out_shape=[SDS((t, d), F32), SDS((1, d), F32), SDS((1, 128), F32)],
        scratch_shapes=[pltpu.VMEM((8, d), F32), pltpu.VMEM((8, d), F32)],
        compiler_params=_params(("arbitrary",)),
    )(x, g, target)


def _merge_bwd(dxo, z, oa, on, ya, yb, wb_t, wout):
    t, d = dxo.shape
    tm = min(256, t)
    n = t // tm

    def body(dx_ref, ga_ref, gr_ref, gm0_ref, gm1_ref, oa_ref, on_ref, ya_ref, yb_ref, wb_ref, wo_ref,
             doa_ref, don_ref, dz_ref, dwo_ref, dwb_ref, acc_o, acc_b):
        i = pl.program_id(0)

        @pl.when(i == 0)
        def _():
            acc_o[...] = jnp.zeros_like(acc_o)
            acc_b[...] = jnp.zeros_like(acc_b)

        dxb = dx_ref[...].astype(BF16)
        ya, yb = ya_ref[...], yb_ref[...]
        g0, g1 = _sigmoid(gm0_ref[...]), _sigmoid(gm1_ref[...])
        mb = (g0 * ya + g1 * yb).astype(BF16)
        dm = _dot(dxb, wo_ref[...], NT)
        dya = (dm * g0).astype(BF16)
        dyb = (dm * g1).astype(BF16)
        dz_ref[:, 1024:2048] = (dm * ya * g0 * (1.0 - g0)).astype(BF16)
        dz_ref[:, 2048:3072] = (dm * yb * g1 * (1.0 - g1)).astype(BF16)

        def branch(g_ref, o_ref, dy, w, do_ref, lo):
            gv, ov = g_ref[...], o_ref[...]
            sg = _sigmoid(gv)
            silu = gv * sg
            du = _dot(dy, w)
            do_ref[...] = du * silu
            dz_ref[:, lo:lo + 512] = (du * ov * (sg * (1.0 + gv * (1.0 - sg)))).astype(BF16)
            acc_b[:, lo:lo + 512] += _dot(dy, (silu * ov).astype(BF16), TN)

        branch(ga_ref, oa_ref, dya, wb_ref[:, :512], doa_ref, 0)
        branch(gr_ref, on_ref, dyb, wb_ref[:, 512:], don_ref, 512)
        acc_o[...] += _dot(mb, dxb, TN)

        @pl.when(i == n - 1)
        def _():
            dwo_ref[...] = acc_o[...].astype(BF16)
            dwb_ref[...] = acc_b[...].astype(BF16)

    row = lambda w, j: pl.BlockSpec((tm, w), lambda i: (i, j))
    const = lambda shape: pl.BlockSpec(shape, lambda i: (0, 0))
    return pl.pallas_call(
        body, name="merge_bwd", grid=(n,),
        in_specs=[row(d, 0), row(512, SEG["ga"][2] // 512), row(512, SEG["gr"][2] // 512),
                  row(1024, SEG["gm"][2] // 1024), row(1024, SEG["gm"][2] // 1024 + 1),
                  row(512, 0), row(512, 0), row(d, 0), row(d, 0), const((d, 1024)), const((d, d))],
        out_specs=[row(512, 0), row(512, 0), row(3072, 0), const((d, d)), const((d, 1024))],
        out_shape=[SDS((t, 512), F32), SDS((t, 512), F32), SDS((t, 3072), BF16), SDS((d, d), BF16),
                   SDS((d, 1024), BF16)],
        scratch_shapes=[pltpu.VMEM((d, d), F32), pltpu.VMEM((d, 1024), F32)],
        compiler_params=_params(("arbitrary",)),
    )(dxo, z, z, z, z, oa, on, ya, yb, wb_t, wout)


def _ret_bwd(qrot, krot, vb, orr, don, gnw, lgf, lgb):
    t = qrot.shape[0]
    c = RET_CHUNK
    nc = t // c
    hd = RET_HEAD_DIM
    unroll = 2 if nc % 2 == 0 else 1

    def body(lgf_ref, lgb_ref, q_ref, k_ref, v_ref, o_ref, dn_ref, w_ref,
             dq_ref, dk_ref, dv_ref, dw_ref, dlf_ref, dlb_ref, qt, kt, dob, uf, ub, wf, wb, sfa, sba, gfa, gba):
        h = pl.program_id(0)
        fw = _Dir(lgf_ref[h], False)
        bw = _Dir(lgb_ref[h], True)
        fw.dt, bw.dt = fw.d.T, bw.d.T

        o = o_ref[...]
        xc = o - jnp.mean(o, axis=-1, keepdims=True)
        r = lax.rsqrt(jnp.mean(xc * xc, axis=-1, keepdims=True) + EPS)
        xh = xc * r
        dn = dn_ref[...]
        gy = dn * w_ref[...]
        d_o = r * (gy - jnp.mean(gy, axis=-1, keepdims=True) - xh * jnp.mean(gy * xh, axis=-1, keepdims=True))
        dw_ref[...] = jnp.sum(dn * xh, axis=0, keepdims=True)
        dob[...] = d_o.astype(BF16)
        for i in range(nc):
            qt[i] = q_ref[i * c:(i + 1) * c, :].astype(F32).T.astype(BF16)
            kt[i] = k_ref[i * c:(i + 1) * c, :].astype(F32).T.astype(BF16)

        def rows(ci):
            return pl.ds(pl.multiple_of(ci * c, c), c)

        def products(ci, carry):
            sl = rows(ci)
            vv, do32 = v_ref[sl, :], dob[sl, :].astype(F32)
            ktf = kt[ci].astype(F32)
            uf[ci] = _dot((ktf * fw.kd_row).astype(BF16), vv)
            ub[ci] = _dot((ktf * bw.kd_row).astype(BF16), vv)
            wf[ci] = _dot(qt[ci], (do32 * fw.qd).astype(BF16))
            wb[ci] = _dot(qt[ci], (do32 * bw.qd).astype(BF16))
            return carry

        lax.fori_loop(0, nc, products, 0, unroll=unroll)

        def scan(i, carry):
            sf, sb, gf, gb = carry
            j = nc - 1 - i
            sfa[i] = sf.astype(BF16)
            sba[j] = sb.astype(BF16)
            gfa[j] = gf.astype(BF16)
            gba[i] = gb.astype(BF16)
            return sf * fw.cd + uf[i], sb * bw.cd + ub[j], gf * fw.cd + wf[j], gb * bw.cd + wb[i]

        zero = jnp.zeros((hd, hd), F32)
        lax.fori_loop(0, nc, scan, (zero, zero, zero, zero))

        def one_dir(p, s_all, g_all, ci, qq, kk, vv, do, a, bm, at, bt):
            sb, gb = s_all[ci], g_all[ci]
            doq = (do.astype(F32) * p.qd).astype(BF16)
            dqc = _dot(doq, sb, NT)
            dq = _dot((bm * p.d).astype(BF16), kk) + dqc
            kkd = (kk.astype(F32) * p.kd_col).astype(BF16)
            dv = _dot((at * p.dt).astype(BF16), do) + _dot(kkd, gb)
            dk2 = _dot(vv, gb, NT) * p.kd_col
            dk = _dot((bt * p.dt).astype(BF16), qq) + dk2
            terms = (p.dist * p.d * a * bm + p.wq * qq.astype(F32) * dqc + p.wk * kk.astype(F32) * dk2
                     + (float(c) * p.cd) * gb.astype(F32) * sb.astype(F32))
            return dq, dk, dv, terms

        def chunk(ci, carry):
            af, ab = carry
            sl = rows(ci)
            qq, kk, vv, do = q_ref[sl, :], k_ref[sl, :], v_ref[sl, :], dob[sl, :]
            a, bm = _dot(qq, kk, NT), _dot(do, vv, NT)
            at, bt = _dot(kk, qq, NT), _dot(vv, do, NT)
            dqf, dkf, dvf, tf = one_dir(fw, sfa, gfa, ci, qq, kk, vv, do, a, bm, at, bt)
            dqb, dkb, dvb, tb = one_dir(bw, sba, gba, ci, qq, kk, vv, do, a, bm, at, bt)
            dq_ref[sl, :] = dqf + dqb
            dk_ref[sl, :] = dkf + dkb
            dv_ref[sl, :] = dvf + dvb
            return af + tf, ab + tb

        af, ab = lax.fori_loop(0, nc, chunk, (zero, zero), unroll=unroll)
        tot = lambda m: jnp.sum(jnp.sum(m, axis=0, keepdims=True), axis=1, keepdims=True)
        dlf_ref[...] = jnp.broadcast_to(tot(af).reshape(1, 1, 1), (1, 8, 128))
        dlb_ref[...] = jnp.broadcast_to(tot(ab).reshape(1, 1, 1), (1, 8, 128))

    smem = pl.BlockSpec(memory_space=pltpu.SMEM)
    head = pl.BlockSpec((t, 128), lambda h: (0, h))
    vec = pl.BlockSpec((1, 128), lambda h: (0, h))
    scal = pl.BlockSpec((1, 8, 128), lambda h: (h, 0, 0))
    mats = lambda dt: pltpu.VMEM((nc, hd, hd), dt)
    return pl.pallas_call(
        body, name="ret_bwd", grid=(RET_HEADS,),
        in_specs=[smem, smem, head, head, head, head, head, vec],
        out_specs=[head, head, head, vec, scal, scal],
        out_shape=[SDS((t, RET_WIDTH), F32)] * 3 + [SDS((1, RET_WIDTH), F32), SDS((RET_HEADS, 8, 128), F32),
                                                   SDS((RET_HEADS, 8, 128), F32)],
        scratch_shapes=[pltpu.VMEM((nc, hd, c), BF16), pltpu.VMEM((nc, hd, c), BF16), pltpu.VMEM((t, hd), BF16),
                        mats(F32), mats(F32), mats(F32), mats(F32), mats(BF16), mats(BF16), mats(BF16), mats(BF16)],
        compiler_params=_params(("parallel",)),
    )(lgf, lgb, qrot, krot, vb, orr, don, gnw)


def _ret_post_bwd(dq, dk, dv, cos, sin):
    t = dq.shape[0]
    tm = min(512, t)
    hd = RET_HEAD_DIM

    def body(dq_ref, dk_ref, dv_ref, c_ref, s_ref, oq_ref, ok_ref, ov_ref):
        cc = jnp.concatenate([c_ref[...]] * 4, axis=-1)
        ss = jnp.concatenate([s_ref[...]] * 4, axis=-1)
        oq_ref[...] = _rope_bwd(dq_ref[...], cc, ss, hd // 4).astype(BF16)
        ok_ref[...] = (_rope_bwd(dk_ref[...], cc, ss, hd // 4) * (hd ** -0.5)).astype(BF16)
        ov_ref[...] = dv_ref[...].astype(BF16)

    blk = pl.BlockSpec((tm, 512), lambda i: (i, 0))
    tab = pl.BlockSpec((tm, 128), lambda i: (i, 0))
    return pl.pallas_call(
        body, name="ret_post_bwd", grid=(t // tm,),
        in_specs=[blk, blk, blk, tab, tab], out_specs=[blk, blk, blk],
        out_shape=[SDS((t, 512), BF16)] * 3,
        compiler_params=_params(("parallel",)),
    )(dq, dk, dv, cos, sin)


def _attn_bwd(q, qt, k, v, doa, oa, lse, ex=None):
    t = q.shape[1]
    tq = min(256, t)
    nq = t // tq
    tk = min(ATTN_BWD_KEY_CHUNK, t)
    nk = t // tk
    hd = ATTN_HEAD_DIM
    scale = hd ** -0.5

    def body(q_ref, qt_ref, k_ref, v_ref, do_ref, o_ref, lse_ref, dq_ref, dkt_ref, dvt_ref):
        p, i = pl.program_id(0), pl.program_id(1)

        @pl.when(jnp.logical_and(p % 2 == 0, i == 0))
        def _():
            dkt_ref[...] = jnp.zeros_like(dkt_ref)
            dvt_ref[...] = jnp.zeros_like(dvt_ref)

        dov, ov = do_ref[...], o_ref[...]
        dovt = dov.T
        lanes = lambda col: jnp.concatenate([col] * (tk // 128), axis=1)
        outs = []
        for j in range(2):
            qq, qqt = q_ref[j], qt_ref[j]
            do32 = dov[:, j * hd:(j + 1) * hd]
            do, dot_ = do32.astype(BF16), dovt[j * hd:(j + 1) * hd, :].astype(BF16)
            dd = lanes(jnp.broadcast_to(jnp.sum(do32 * ov[:, j * hd:(j + 1) * hd], axis=1, keepdims=True), (tq, 128)))
            lse_j = lanes(jnp.broadcast_to(lse_ref[j], (128, tq)).T)
            dq = jnp.zeros((tq, hd), F32)
            for c in range(nk):
                sl = slice(c * tk, (c + 1) * tk)
                kc, vc = k_ref[0, sl, :], v_ref[0, sl, :]
                pr = jnp.exp(_dot(qq, kc, NT) - lse_j)
                ds = (pr * (_dot(do, vc, NT) - dd)).astype(BF16)
                dvt_ref[0, :, sl] += _dot(dot_, pr.astype(BF16))
                dkt_ref[0, :, sl] += _dot(qqt, ds)
                dq = dq + _dot(ds, kc)
            outs.append(dq * scale)
        dq_ref[...] = jnp.concatenate(outs, axis=-1)

    kv = pl.BlockSpec((1, t, hd), lambda p, i: (p // 2, 0, 0))
    kvt = pl.BlockSpec((1, hd, t), lambda p, i: (p // 2, 0, 0))
    pair = pl.BlockSpec((tq, 128), lambda p, i: (i, p))
    first = lambda: jnp.logical_and(pl.program_id(0) == 0, pl.program_id(1) == 0)
    last = lambda: jnp.logical_and(pl.program_id(0) == 3, pl.program_id(1) == nq - 1)
    xi, xo, xs, xscr, xargs = _ex_args(ex)
    return pl.pallas_call(
        _with_exchange(body, 7, 3, 0, ex, first, last), name="attn_bwd", grid=(4, nq),
        in_specs=[pl.BlockSpec((2, tq, hd), lambda p, i: (p, i, 0)), pl.BlockSpec((2, hd, tq), lambda p, i: (p, 0, i)),
                  kv, kv, pair, pair, pl.BlockSpec((2, 1, tq), lambda p, i: (p, 0, i))] + xi,
        out_specs=[pair, kvt, kvt] + xo,
        out_shape=[SDS((t, ATTN_WIDTH), F32), SDS((ATTN_KV_HEADS, hd, t), F32),
                   SDS((ATTN_KV_HEADS, hd, t), F32)] + xs,
        scratch_shapes=xscr,
        compiler_params=_params(("arbitrary", "arbitrary")),
    )(q, qt, k, v, doa, oa, lse, *xargs)


def _attn_post_bwd(dq, dk, dv, z, qn, kn, cos, sin, ones_bd):
    t = z.shape[0]
    tm = min(512, t)
    n = t // tm
    hd = ATTN_HEAD_DIM

    def body(dq_ref, dk_ref, dv_ref, zq_ref, zkv_ref, qn_ref, kn_ref, c_ref, s_ref, b_ref,
             dz_ref, dqn_ref, dkn_ref, acc_q, acc_k):
        i = pl.program_id(0)

        @pl.when(i == 0)
        def _():
            acc_q[...] = jnp.zeros_like(acc_q)
            acc_k[...] = jnp.zeros_like(acc_k)

        bd = b_ref[...]
        c2, s2 = c_ref[...], s_ref[...]

        def norm_bwd(dy, x, w, ones, cos_t, sin_t, acc):
            dyr = _rope_bwd(dy, cos_t, sin_t, hd // 4)
            r = lax.rsqrt(_group_mean(x * x, ones) + EPS)
            xh = x * r
            gy = dyr * w
            acc[...] += jnp.sum((dyr * xh).reshape(tm // 8, 8, x.shape[-1]), axis=0)
            return r * (gy - xh * _group_mean(gy * xh, ones))

        cq = jnp.concatenate([c2] * 4, axis=-1)
        sq = jnp.concatenate([s2] * 4, axis=-1)
        dz_ref[:, :512] = norm_bwd(dq_ref[...], zq_ref[...], qn_ref[...], bd, cq, sq, acc_q).astype(BF16)
        zkv = zkv_ref[...]
        dkk = jnp.concatenate([dk_ref[0], dk_ref[1]], axis=0).T
        dz_ref[:, 512:640] = norm_bwd(dkk, zkv[:, :128], kn_ref[...], bd[:128, :128], c2, s2, acc_k).astype(BF16)
        dz_ref[:, 640:768] = jnp.concatenate([dv_ref[0], dv_ref[1]], axis=0).T.astype(BF16)

        @pl.when(i == n - 1)
        def _():
            dqn_ref[...] = jnp.sum(acc_q[...], axis=0, keepdims=True)
            dkn_ref[...] = jnp.sum(acc_k[...], axis=0, keepdims=True)

    kv_blk = SEG["ka"][2] // 256
    kvs = pl.BlockSpec((ATTN_KV_HEADS, hd, tm), lambda i: (0, 0, i))
    const = lambda shape: pl.BlockSpec(shape, lambda i: (0, 0))
    return pl.pallas_call(
        body, name="attn_post_bwd", grid=(n,),
        in_specs=[pl.BlockSpec((tm, 512), lambda i: (i, 0)), kvs, kvs,
                  pl.BlockSpec((tm, 512), lambda i: (i, 0)), pl.BlockSpec((tm, 256), lambda i: (i, kv_blk)),
                  const((1, 512)), const((1, 128)),
                  pl.BlockSpec((tm, 128), lambda i: (i, 0)), pl.BlockSpec((tm, 128), lambda i: (i, 0)),
                  const((512, 512))],
        out_specs=[pl.BlockSpec((tm, 768), lambda i: (i, 0)), const((1, 512)), const((1, 128))],
        out_shape=[SDS((t, 768), BF16), SDS((1, 512), F32), SDS((1, 128), F32)],
        scratch_shapes=[pltpu.VMEM((8, 512), F32), pltpu.VMEM((8, 128), F32)],
        compiler_params=_params(("arbitrary",)),
    )(dq, dk, dv, z, z, qn, kn, cos, sin, ones_bd)


def _in_bwd(dxo, x, g, w_t, dz_a, dz_m, dqr, dkr, dvr, ex=None):
    t, d = x.shape
    tm = min(256, t)
    n = t // tm
    parts = [(0, 0, 768, 0), (1, 0, 512, SEG["ga"][0]), (2, 0, 512, SEG["qr"][0]), (3, 0, 512, SEG["kr"][0]),
             (4, 0, 512, SEG["vr"][0]), (1, 512, 2560, SEG["gr"][0])]

    def body(dx_ref, x_ref, g_ref, w_ref, a_ref, m_ref, q_ref, k_ref, v_ref, o_ref, dg_ref, acc):
        i = pl.program_id(0)

        @pl.when(i == 0)
        def _():
            acc[...] = jnp.zeros_like(acc)

        pieces = [a_ref, m_ref, q_ref, k_ref, v_ref]
        dh = jnp.zeros((tm, d), F32)
        for pi, lo, w, row in parts:
            dh = dh + _dot(pieces[pi][:, lo:lo + w], w_ref[row:row + w, :])
        xv = x_ref[...]
        r = lax.rsqrt(jnp.mean(xv * xv, axis=-1, keepdims=True) + EPS)
        xh = xv * r
        gy = dh * g_ref[...]
        o_ref[...] = dx_ref[...] + r * (gy - xh * jnp.mean(gy * xh, axis=-1, keepdims=True))
        acc[...] += jnp.sum((dh * xh).reshape(tm // 8, 8, d), axis=0)

        @pl.when(i == n - 1)
        def _():
            dg_ref[...] = jnp.sum(acc[...], axis=0, keepdims=True)

    row = lambda w: pl.BlockSpec((tm, w), lambda i: (i, 0))
    const = lambda shape: pl.BlockSpec(shape, lambda i: (0, 0))
    xi, xo, xs, xscr, xargs = _ex_args(ex)
    return pl.pallas_call(
        _with_exchange(body, 9, 2, 1, ex, lambda: pl.program_id(0) == 0, lambda: pl.program_id(0) == n - 1),
        name="in_bwd", grid=(n,),
        in_specs=[row(d), row(d), const((1, d)), const((D_IN, d)), row(768), row(3072), row(512), row(512),
                  row(512)] + xi,
        out_specs=[row(d), const((1, d))] + xo,
        out_shape=[SDS((t, d), F32), SDS((1, d), F32)] + xs,
        scratch_shapes=[pltpu.VMEM((8, d), F32)] + xscr,
        compiler_params=_params(("arbitrary",)),
    )(dxo, x, g, w_t, dz_a, dz_m, dqr, dkr, dvr, *xargs)


def _dw_in(h_t, dz_a, dz_m, dqr, dkr, dvr):
    d, t = h_t.shape
    tn = 256
    parts = [(0, 0, 0, 3), (1, 0, SEG["ga"][0] // tn, 2), (2, 0, SEG["qr"][0] // tn, 2),
             (3, 0, SEG["kr"][0] // tn, 2), (4, 0, SEG["vr"][0] // tn, 2), (1, 2, SEG["gr"][0] // tn, 10)]
    pieces = [dz_a, dz_m, dqr, dkr, dvr]

    def col_block(pi):
        mine = [(c0, r0, n) for q, c0, r0, n in parts if q == pi]

        def index(j):
            c0, r0, n = mine[0]
            blk = c0 + jnp.clip(j - r0, 0, n - 1)
            for c0, r0, n in mine[1:]:
                blk = jnp.where(j >= r0, c0 + jnp.clip(j - r0, 0, n - 1), blk)
            return 0, blk

        return index

    def body(h_ref, *refs):
        o_ref = refs[-1]
        j = pl.program_id(0)
        for pi, _, r0, n in parts:
            @pl.when(jnp.logical_and(j >= r0, j < r0 + n))
            def _(p_ref=refs[pi]):
                o_ref[...] = _dot(h_ref[...], p_ref[...]).T.astype(BF16)

    return pl.pallas_call(
        body, name="dw_in", grid=(D_IN // tn,),
        in_specs=[pl.BlockSpec((d, t), lambda j: (0, 0))] + [pl.BlockSpec((t, tn), col_block(pi)) for pi in range(5)],
        out_specs=pl.BlockSpec((tn, d), lambda j: (j, 0)),
        out_shape=SDS((D_IN, d), BF16),
        compiler_params=_params(("arbitrary",)),
    )(h_t, *pieces)


def _adamw_math(w, g, m, v):
    mn = ADAM_B1 * m + (1.0 - ADAM_B1) * g
    vn = ADAM_B2 * v + (1.0 - ADAM_B2) * (g * g)
    m_hat = mn / (1.0 - ADAM_B1 ** ADAM_STEP)
    v_hat = vn / (1.0 - ADAM_B2 ** ADAM_STEP)
    return -ADAM_LR * (m_hat / (jnp.sqrt(v_hat) + ADAM_EPS) + ADAM_WD * w), mn, vn


def _sum_adamw(recvs, w, m, v, lane0, tn):
    depth, r, c = w.shape
    j0 = lane0 // tn

    def body(r0_ref, r1_ref, w_ref, m_ref, v_ref, g_ref, d_ref, mo_ref, vo_ref):
        def run(r_ref):
            g = r_ref[0].astype(F32)
            for s in range(1, N_DEV):
                g = g + r_ref[s].astype(F32)
            g_ref[0] = g
            d_ref[0], mo_ref[0], vo_ref[0] = _adamw_math(w_ref[0], g, m_ref[0], v_ref[0])

        for l, r_ref in enumerate((r0_ref, r1_ref)):
            pl.when(pl.program_id(0) == l)(functools.partial(run, r_ref))

    slots = pl.BlockSpec((N_DEV, r, tn), lambda l, j: (0, 0, j0 + j))
    blk = pl.BlockSpec((1, r, tn), lambda l, j: (l, 0, j))
    return pl.pallas_call(
        body, name="sum_adamw", grid=(depth, c // tn),
        in_specs=[slots, slots, blk, blk, blk], out_specs=[blk] * 4, out_shape=[SDS(w.shape, F32)] * 4,
        compiler_params=_params(("parallel", "parallel")),
    )(recvs[0], recvs[1], w, m, v)


def _adamw(w, g, m, v):
    rows, cols = w.shape
    tr = 256 if rows % 256 == 0 else rows

    def body(w_ref, g_ref, m_ref, v_ref, d_ref, mo_ref, vo_ref):
        d_ref[...], mo_ref[...], vo_ref[...] = _adamw_math(w_ref[...], g_ref[...], m_ref[...], v_ref[...])

    blk = pl.BlockSpec((tr, cols), lambda i: (i, 0))
    return pl.pallas_call(
        body, name="adamw", grid=(rows // tr,),
        in_specs=[blk] * 4, out_specs=[blk] * 3, out_shape=[SDS((rows, cols), F32)] * 3,
        compiler_params=_params(("parallel",)),
    )(w, g, m, v)


def _all_gather(shards):
    na = len(shards)
    chips = (4, 2, 6)

    def body(*refs):
        ins, outs = refs[:na], refs[na:2 * na]
        send_sems, recv_sems, local_sems = refs[2 * na:]
        _, mine = _flip(0)

        def rows(a, idx):
            r = shards[a].shape[0]
            return outs[a].at[pl.ds(pl.multiple_of(idx * r, 16), r), :]

        def copy(a, slot, block_idx, to, src=None):
            return pltpu.make_async_remote_copy(
                src_ref=rows(a, block_idx) if src is None else src, dst_ref=rows(a, block_idx),
                send_sem=send_sems.at[a, slot], recv_sem=recv_sems.at[a, slot],
                device_id=to, device_id_type=MESH_ID)

        sibling, sibling_idx = _flip(1)
        local, started = [], []
        for a in range(na):
            cp = pltpu.make_async_copy(ins[a], rows(a, mine), local_sems.at[a])
            cp.start()
            local.append(cp)
            first = [copy(a, 0, mine, sibling, src=ins[a])]
            first += [copy(a, 1 + j, mine, _flip(k)[0], src=ins[a]) for j, k in enumerate(chips)]
            for cp in first:
                cp.start()
            started += first
        for a in range(na):
            for j, k in enumerate(chips):
                _, theirs = _flip(k)
                copy(a, 1 + j, theirs, _flip(0)[0]).wait_recv()
                fwd = copy(a, 4 + j, theirs, sibling)
                fwd.start()
                started.append(fwd)
        for a in range(na):
            copy(a, 0, sibling_idx, _flip(0)[0]).wait_recv()
            for j, k in enumerate(chips):
                _, theirs = _flip(k | 1)
                copy(a, 4 + j, theirs, _flip(0)[0]).wait_recv()
        for cp in started:
            cp.wait_send()
        for cp in local:
            cp.wait()

    return pl.pallas_call(
        body, name="all_gather_weights",
        in_specs=[ANY] * na, out_specs=[ANY] * na,
        out_shape=[SDS((N_DEV * s.shape[0], s.shape[1]), s.dtype) for s in shards],
        scratch_shapes=[pltpu.SemaphoreType.DMA((na, 7)), pltpu.SemaphoreType.DMA((na, 7)),
                        pltpu.SemaphoreType.DMA((na,))],
        compiler_params=pltpu.CompilerParams(has_side_effects=True),
    )(*shards)


def _all_reduce_small(packed):
    shape = packed.shape

    def body(p_ref, o_ref, slots, send_sems, recv_sems):
        me, mine = _flip(0)
        slots[mine] = p_ref[...]
        sends = []
        for k in range(1, N_DEV):
            peer, _ = _flip(k)
            cp = pltpu.make_async_remote_copy(
                src_ref=p_ref, dst_ref=slots.at[mine], send_sem=send_sems.at[k - 1], recv_sem=recv_sems.at[k - 1],
                device_id=peer, device_id_type=MESH_ID)
            cp.start()
            sends.append(cp)
        for k in range(1, N_DEV):
            _, theirs = _flip(k)
            pltpu.make_async_remote_copy(
                src_ref=p_ref, dst_ref=slots.at[theirs], send_sem=send_sems.at[k - 1],
                recv_sem=recv_sems.at[k - 1], device_id=me, device_id_type=MESH_ID).wait_recv()
        for cp in sends:
            cp.wait_send()
        acc = slots[0]
        for s in range(1, N_DEV):
            acc = acc + slots[s]
        o_ref[...] = acc

    vm = pl.BlockSpec(memory_space=pltpu.VMEM)
    return pl.pallas_call(
        body, name="all_reduce_small", in_specs=[vm], out_specs=vm, out_shape=SDS(shape, F32),
        scratch_shapes=[pltpu.VMEM((N_DEV,) + shape, F32), pltpu.SemaphoreType.DMA((7,)),
                        pltpu.SemaphoreType.DMA((7,))],
        compiler_params=pltpu.CompilerParams(has_side_effects=True),
    )(packed)


def _layer_fwd(x, p, tabs, ex):
    z, h_t = _in_proj(x, p["norm_g"], p["w_in_t"])
    q, qt, k, v, vt = _attn_prep(z, p["qn"], p["kn"], tabs["ca"], tabs["sa"], tabs["ones"])
    oa, lse, *gathered = _attn_fwd(q, k, vt, ex)
    qrot, krot, vb, orr, on = _ret_fwd(z, p["lgf"], p["lgb"], p["gnw"], tabs["cr"], tabs["sr"])
    return z, h_t, q, qt, k, v, lse, oa, qrot, krot, vb, orr, on, gathered


def _layer_bwd(dxo, s, p, tabs, ex_attn, make_ex_in):
    doa, don, dz_m, d_wout, d_wb_t = _merge_bwd(dxo, s["z"], s["oa"], s["on"], s["ya"], s["yb"], p["wb_t"], p["w_out"])
    dq_a, dk_a, dv_a, *recv_attn = _attn_bwd(s["q"], s["qt"], s["k"], s["v"], doa, s["oa"], s["lse"],
                                              ex_attn(d_wb_t, d_wout))
    dz_a, d_qn, d_kn = _attn_post_bwd(dq_a, dk_a, dv_a, s["z"], p["qn"], p["kn"], tabs["ca"], tabs["sa"],
                                      tabs["ones"])
    dq_r, dk_r, dv_r, d_gnw, d_lgf, d_lgb = _ret_bwd(s["qrot"], s["krot"], s["vb"], s["orr"], don, p["gnw"],
                                                     p["lgf"], p["lgb"])
    dqr, dkr, dvr = _ret_post_bwd(dq_r, dk_r, dv_r, tabs["cr"], tabs["sr"])
    buf = _dw_in(s["h_t"], dz_a, dz_m, dqr, dkr, dvr)
    dx, d_norm_g, *recv_in = _in_bwd(dxo, s["x"], p["norm_g"], p["w_in_t"], dz_a, dz_m, dqr, dkr, dvr,
                                     make_ex_in(buf))
    grads = dict(w_in_t=buf, wb_t=d_wb_t, w_out=d_wout, norm_g=d_norm_g, gnw=d_gnw,
                 qn=d_qn.reshape(ATTN_Q_HEADS, ATTN_HEAD_DIM).sum(axis=0),
                 kn=d_kn.reshape(ATTN_KV_HEADS, ATTN_HEAD_DIM).sum(axis=0),
                 lgf=d_lgf[:, 0, 0], lgb=d_lgb[:, 0, 0])
    return dx, grads, recv_attn, recv_in


def _adamw_nd(w, g, m, v):
    shape = w.shape
    two_d = (1, shape[0]) if w.ndim == 1 else (-1, shape[-1])
    out = _adamw(w.reshape(two_d), g.reshape(two_d), m.reshape(two_d), v.reshape(two_d))
    return tuple(o.reshape(shape) for o in out)


def kernel(x, norm_g, w_in, attn_q_norm, attn_k_norm, ret_decay_fwd, ret_decay_bwd, ret_gn_w, w_branch_attn, w_branch_ret, w_out, final_norm_g, loss_target, m_norm_g, m_w_in, m_attn_q_norm, m_attn_k_norm, m_ret_decay_fwd, m_ret_decay_bwd, m_ret_gn_w, m_w_branch_attn, m_w_branch_ret, m_w_out, m_final_norm_g, v_norm_g, v_w_in, v_attn_q_norm, v_attn_k_norm, v_ret_decay_fwd, v_ret_decay_bwd, v_ret_gn_w, v_w_branch_attn, v_w_branch_ret, v_w_out, v_final_norm_g):
    t, d = x.shape[1], x.shape[2]
    x2, target = x[0], loss_target[0]

    w_in_sh, wb_sh, wout_sh = [], [], []
    for l in range(DEPTH):
        w_in_sh.append(jnp.swapaxes(w_in[l], 0, 1).astype(BF16))
        wb_sh.append(jnp.concatenate([w_branch_attn[l].T, w_branch_ret[l].T], axis=1).astype(BF16))
        wout_sh.append(w_out[l].astype(BF16))

    ca, sa = _rope_tables(t, ATTN_HEAD_DIM)
    cr, sr = _rope_tables(t, RET_HEAD_DIM)
    grp = jnp.arange(ATTN_WIDTH) // ATTN_HEAD_DIM
    tabs = dict(ca=jnp.tile(ca, (1, 2)), sa=jnp.tile(sa, (1, 2)), cr=cr, sr=sr,
                ones=jnp.where(grp[:, None] == grp[None, :], 1.0 / ATTN_HEAD_DIM, 0.0).astype(BF16))
    layers = []
    for l in range(DEPTH):
        layers.append(dict(
            norm_g=norm_g[l][None], qn=jnp.tile(attn_q_norm[l], ATTN_Q_HEADS)[None],
            kn=jnp.tile(attn_k_norm[l], ATTN_KV_HEADS)[None], gnw=ret_gn_w[l][None],
            lgf=jax.nn.log_sigmoid(ret_decay_fwd[l]), lgb=jax.nn.log_sigmoid(ret_decay_bwd[l])))

    layers[0]["w_in_t"], = _all_gather([w_in_sh[0]])
    gathers = [_Exchange("gather", [wb_sh[0], wout_sh[0], w_in_sh[1]]), _Exchange("gather", [wb_sh[1], wout_sh[1]])]
    h = x2
    saved = []
    for l in range(DEPTH):
        p = layers[l]
        z, h_t, q, qt, k, v, lse, oa, qrot, krot, vb, orr, on, got = _layer_fwd(h, p, tabs, gathers[l])
        p["wb_t"], p["w_out"] = got[0], got[1]
        if l == 0:
            layers[1]["w_in_t"] = got[2]
        xn, ya, yb = _merge_fwd(h, z, oa, on, p["wb_t"], p["w_out"])
        saved.append(dict(x=h, z=z, h_t=h_t, q=q, qt=qt, k=k, v=v, lse=lse, oa=oa, qrot=qrot, krot=krot, vb=vb,
                          orr=orr, on=on, ya=ya, yb=yb))
        h = xn
    dx, d_final_g, loss_part = _final_loss(h, final_norm_g[None], target)

    grads = [None] * DEPTH
    none = lambda *a: None
    dx, grads[1], _, _ = _layer_bwd(dx, saved[1], layers[1], tabs, none, none)
    g1 = grads[1]
    ex_attn = lambda d_wb_t, d_wout: _Exchange("scatter", [g1["w_in_t"], g1["wb_t"], g1["w_out"], d_wb_t, d_wout])
    ex_in = lambda d_w_in_t: _Exchange("scatter", [d_w_in_t])
    dx, grads[0], recv_attn, recv_in = _layer_bwd(dx, saved[0], layers[0], tabs, ex_attn, ex_in)
    recv = [recv_in[0], recv_attn[3], recv_attn[4], recv_attn[0], recv_attn[1], recv_attn[2]]
    tr = lambda a: jnp.swapaxes(a, 1, 2)
    sharded = {}
    sharded[id(w_in)] = [tr(o) for o in _sum_adamw([recv[0], recv[3]], tr(w_in), tr(m_w_in), tr(v_w_in), 0, 256)]
    sharded[id(w_branch_attn)] = [tr(o) for o in _sum_adamw(
        [recv[1], recv[4]], tr(w_branch_attn), tr(m_w_branch_attn), tr(v_w_branch_attn), 0, 512)]
    sharded[id(w_branch_ret)] = [tr(o) for o in _sum_adamw(
        [recv[1], recv[4]], tr(w_branch_ret), tr(m_w_branch_ret), tr(v_w_branch_ret), 512, 512)]
    sharded[id(w_out)] = _sum_adamw([recv[2], recv[5]], w_out, m_w_out, v_w_out, 0, 256)
    g_w_in, g_wba, g_wbr, g_wout = (sharded[id(w)][0] for w in (w_in, w_branch_attn, w_branch_ret, w_out))

    packed = jnp.zeros((8, 1024), F32)
    for l in range(DEPTH):
        gl = grads[l]
        packed = packed.at[l].set(gl["norm_g"][0])
        packed = packed.at[2, 512 * l:512 * (l + 1)].set(gl["gnw"][0])
        packed = packed.at[4, 128 * l:128 * l + 64].set(gl["qn"])
        packed = packed.at[4, 256 + 128 * l:256 + 128 * l + 64].set(gl["kn"])
        packed = packed.at[4, 512 + 128 * l:512 + 128 * l + 4].set(gl["lgf"])
        packed = packed.at[4, 768 + 128 * l:768 + 128 * l + 4].set(gl["lgb"])
    packed = packed.at[3].set(d_final_g[0])
    packed = packed.at[5, 0].set(loss_part[0, 0])
    red = _all_reduce_small(packed)
    loss = red[5, 0]
    g_norm_g = red[0:2]
    g_gnw = red[2].reshape(DEPTH, RET_WIDTH)
    g_final = red[3]
    g_qn = jnp.stack([red[4, 128 * l:128 * l + 64] for l in range(DEPTH)])
    g_kn = jnp.stack([red[4, 256 + 128 * l:256 + 128 * l + 64] for l in range(DEPTH)])
    g_lgf = jnp.stack([red[4, 512 + 128 * l:512 + 128 * l + 4] for l in range(DEPTH)])
    g_lgb = jnp.stack([red[4, 768 + 128 * l:768 + 128 * l + 4] for l in range(DEPTH)])
    g_df = g_lgf * jax.nn.sigmoid(-ret_decay_fwd)
    g_db = g_lgb * jax.nn.sigmoid(-ret_decay_bwd)

    grad_w = [g_norm_g, g_w_in, g_qn, g_kn, g_df, g_db, g_gnw, g_wba, g_wbr, g_wout, g_final]
    weights = [norm_g, w_in, attn_q_norm, attn_k_norm, ret_decay_fwd, ret_decay_bwd, ret_gn_w, w_branch_attn,
               w_branch_ret, w_out, final_norm_g]
    ms = [m_norm_g, m_w_in, m_attn_q_norm, m_attn_k_norm, m_ret_decay_fwd, m_ret_decay_bwd, m_ret_gn_w,
          m_w_branch_attn, m_w_branch_ret, m_w_out, m_final_norm_g]
    vs = [v_norm_g, v_w_in, v_attn_q_norm, v_attn_k_norm, v_ret_decay_fwd, v_ret_decay_bwd, v_ret_gn_w,
          v_w_branch_attn, v_w_branch_ret, v_w_out, v_final_norm_g]
    upd = [sharded[id(w)][1:] if id(w) in sharded else _adamw_nd(w, g, m, v)
           for w, g, m, v in zip(weights, grad_w, ms, vs)]
    return (loss, dx[None], *grad_w, *[u[0] for u in upd], *[u[1] for u in upd], *[u[2] for u in upd])
```

```python
import functools

import jax
import jax.numpy as jnp
from jax import lax
from jax.experimental import pallas as pl
from jax.experimental.pallas import tpu as pltpu

F32 = jnp.float32
BF16 = jnp.bfloat16
SDS = jax.ShapeDtypeStruct

D_MODEL = 1024
DEPTH = 2
GRID_W = 64
ATTN_Q_HEADS = 8
ATTN_KV_HEADS = 2
ATTN_HEAD_DIM = 64
ATTN_WIDTH = 512
ATTN_KV_WIDTH = 128
RET_HEADS = 4
RET_HEAD_DIM = 128
RET_WIDTH = 512
RET_CHUNK = 128
ATTN_KEY_CHUNK = 512
ATTN_BWD_KEY_CHUNK = 1024
QK_DOTS_PER_CHUNK = 4
EXP_LAG = 3
ROPE_THETA = 10000.0
EPS = 1e-6
D_IN = 5376
N_DEV = 8

ADAM_LR = 0.001
ADAM_B1 = 0.9
ADAM_B2 = 0.999
ADAM_EPS = 1e-08
ADAM_WD = 0.01
ADAM_STEP = 10

SEG = {
    "qa": (0, 512, 0),
    "ga": (768, 512, 512),
    "qr": (1280, 512, 1024),
    "kr": (1792, 512, 1536),
    "vr": (2304, 512, 2048),
    "gr": (2816, 512, 2560),
    "gm": (3328, 2048, 3072),
    "ka": (512, 128, 5120),
    "va": (640, 128, 5248),
}

VMEM_LIMIT = 60 * 1024 * 1024
NT = (((1,), (1,)), ((), ()))
TN = (((0,), (0,)), ((), ()))
MESH_ID = pl.DeviceIdType.MESH
ANY = pl.BlockSpec(memory_space=pl.ANY)


def _params(sem=None, vmem=VMEM_LIMIT):
    return pltpu.CompilerParams(dimension_semantics=sem, vmem_limit_bytes=vmem)


def _dot(a, b, dims=None):
    if dims is None:
        return jnp.dot(a, b, preferred_element_type=F32)
    return lax.dot_general(a, b, dims, preferred_element_type=F32)


def _sigmoid(x):
    return 1.0 / (1.0 + jnp.exp(-x))


def _swap_halves(x, q):
    n = x.shape[-1]
    axis = x.ndim - 1
    lane = lax.broadcasted_iota(jnp.int32, x.shape, axis)
    first = (lane % (2 * q)) < q
    return jnp.where(first, pltpu.roll(x, n - q, axis), pltpu.roll(x, q, axis))


def _rope(x, cos, sin_signed, q):
    return x * cos + _swap_halves(x, q) * sin_signed


def _rope_bwd(dy, cos, sin_signed, q):
    return dy * cos - _swap_halves(dy, q) * sin_signed


def _group_mean(v, ones_bd):
    hi = v.astype(BF16)
    r1 = v - hi.astype(F32)
    mid = r1.astype(BF16)
    lo = (r1 - mid.astype(F32)).astype(BF16)
    return _dot(hi, ones_bd) + _dot(mid, ones_bd) + _dot(lo, ones_bd)


def _rope_tables(t, head_dim):
    n_rows = t // GRID_W
    d_axis = head_dim // 2
    inv_freq = ROPE_THETA ** (-jnp.arange(0, d_axis, 2, dtype=F32) / d_axis)
    ar = jnp.arange(n_rows, dtype=F32)[:, None] * inv_freq
    ac = jnp.arange(GRID_W, dtype=F32)[:, None] * inv_freq
    by_row = lambda a: jnp.repeat(a, GRID_W, axis=0)
    by_col = lambda a: jnp.tile(a, (n_rows, 1))
    cr, sr, cc, sc = by_row(jnp.cos(ar)), by_row(jnp.sin(ar)), by_col(jnp.cos(ac)), by_col(jnp.sin(ac))
    return jnp.concatenate([cr, cr, cc, cc], axis=-1), jnp.concatenate([-sr, sr, -sc, sc], axis=-1)


def _me():
    return lax.axis_index("x"), lax.axis_index("y"), lax.axis_index("c")


def _flip(k):
    x, y, c = _me()
    px = 1 - x if k & 4 else x
    py = 1 - y if k & 2 else y
    pc = 1 - c if k & 1 else c
    return (px, py, pc), 4 * px + 2 * py + pc


class _Exchange:
    def __init__(self, kind, srcs):
        self.kind, self.srcs, self.n = kind, list(srcs), len(srcs)
        self.rows = [a.shape[0] if kind == "gather" else a.shape[0] // N_DEV for a in srcs]
        if kind == "gather":
            self.out_shape = [SDS((N_DEV * a.shape[0], a.shape[1]), a.dtype) for a in srcs]
        else:
            self.out_shape = [SDS((N_DEV, a.shape[0] // N_DEV, a.shape[1]), a.dtype) for a in srcs]
        self.scratch = [pltpu.SemaphoreType.DMA((self.n, N_DEV - 1)), pltpu.SemaphoreType.DMA((self.n, N_DEV - 1)),
                        pltpu.SemaphoreType.DMA((self.n,))]

    def _block(self, ref, a, idx):
        r = self.rows[a]
        return ref.at[pl.ds(pl.multiple_of(idx * r, 16), r), :]

    def _src(self, ins, a, idx):
        return ins[a] if self.kind == "gather" else self._block(ins[a], a, idx)

    def _dst(self, outs, a, idx):
        return self._block(outs[a], a, idx) if self.kind == "gather" else outs[a].at[idx]

    def _copies(self, ins, outs, sems):
        send_sems, recv_sems, local_sems = sems
        me, mine = _flip(0)
        local, sends, recvs = [], [], []
        for a in range(self.n):
            local.append(pltpu.make_async_copy(self._src(ins, a, mine), self._dst(outs, a, mine), local_sems.at[a]))
            for k in range(1, N_DEV):
                peer, theirs = _flip(k)
                sem = dict(send_sem=send_sems.at[a, k - 1], recv_sem=recv_sems.at[a, k - 1])
                sends.append(pltpu.make_async_remote_copy(
                    src_ref=self._src(ins, a, theirs), dst_ref=self._dst(outs, a, mine),
                    device_id=peer, device_id_type=MESH_ID, **sem))
                recvs.append(pltpu.make_async_remote_copy(
                    src_ref=self._dst(outs, a, theirs), dst_ref=self._dst(outs, a, theirs),
                    device_id=me, device_id_type=MESH_ID, **sem))
        return local, sends, recvs

    def start(self, ins, outs, sems):
        local, sends, _ = self._copies(ins, outs, sems)
        for cp in local + sends:
            cp.start()

    def wait(self, ins, outs, sems):
        local, sends, recvs = self._copies(ins, outs, sems)
        for cp in sends:
            cp.wait_send()
        for cp in recvs:
            cp.wait_recv()
        for cp in local:
            cp.wait()


def _with_exchange(body, n_in, n_out, n_scratch, ex, first, last):
    if ex is None:
        return body

    def wrapped(*refs):
        ins = refs[:n_in]
        ex_ins = refs[n_in:n_in + ex.n]
        outs = refs[n_in + ex.n:n_in + ex.n + n_out]
        ex_outs = refs[n_in + ex.n + n_out:n_in + 2 * ex.n + n_out]
        rest = refs[n_in + 2 * ex.n + n_out:]
        scratch, sems = rest[:n_scratch], rest[n_scratch:]

        @pl.when(first())
        def _():
            ex.start(ex_ins, ex_outs, sems)

        body(*ins, *outs, *scratch)

        @pl.when(last())
        def _():
            ex.wait(ex_ins, ex_outs, sems)

    return wrapped


def _ex_args(ex):
    if ex is None:
        return [], [], [], [], []
    return [ANY] * ex.n, [ANY] * ex.n, list(ex.out_shape), list(ex.scratch), list(ex.srcs)


def _in_proj(x, g, w_t):
    t, d = x.shape
    tm = min(256, t)

    def body(x_ref, g_ref, w_ref, z_ref, ht_ref):
        xv = x_ref[...]
        r = lax.rsqrt(jnp.mean(xv * xv, axis=-1, keepdims=True) + EPS)
        h = xv * r * g_ref[...]
        ht_ref[...] = h.T.astype(BF16)
        hb = h.astype(BF16)
        for nat, w, off in SEG.values():
            z_ref[:, off:off + w] = _dot(hb, w_ref[nat:nat + w, :], NT)

    return pl.pallas_call(
        body, name="in_proj", grid=(t // tm,),
        in_specs=[pl.BlockSpec((tm, d), lambda i: (i, 0)), pl.BlockSpec((1, d), lambda i: (0, 0)),
                  pl.BlockSpec((D_IN, d), lambda i: (0, 0))],
        out_specs=[pl.BlockSpec((tm, D_IN), lambda i: (i, 0)), pl.BlockSpec((d, tm), lambda i: (0, i))],
        out_shape=[SDS((t, D_IN), F32), SDS((d, t), BF16)],
        compiler_params=_params(("parallel",)),
    )(x, g, w_t)


def _attn_prep(z, qn, kn, cos, sin, ones_bd):
    t = z.shape[0]
    tm = min(ATTN_KEY_CHUNK, t)
    hd = ATTN_HEAD_DIM

    def body(zq_ref, zkv_ref, qn_ref, kn_ref, c_ref, s_ref, b_ref, q_out, qt_out, k_out, v_out, vt_out):
        bd = b_ref[...]
        c2, s2 = c_ref[...], s_ref[...]
        cq = jnp.concatenate([c2] * 4, axis=-1)
        sq = jnp.concatenate([s2] * 4, axis=-1)
        xq = zq_ref[...]
        yq = xq * lax.rsqrt(_group_mean(xq * xq, bd) + EPS) * qn_ref[...]
        yq = _rope(yq, cq, sq, hd // 4) * (hd ** -0.5)
        yqt = yq.T
        for h in range(ATTN_Q_HEADS):
            q_out[h] = yq[:, h * hd:(h + 1) * hd].astype(BF16)
            qt_out[h] = yqt[h * hd:(h + 1) * hd, :].astype(BF16)
        zkv = zkv_ref[...]
        xk, xv = zkv[:, :ATTN_KV_WIDTH], zkv[:, ATTN_KV_WIDTH:]
        yk = xk * lax.rsqrt(_group_mean(xk * xk, bd[:ATTN_KV_WIDTH, :ATTN_KV_WIDTH]) + EPS) * kn_ref[...]
        yk = _rope(yk, c2, s2, hd // 4)
        xvt = xv.T
        ones = jnp.ones((hd, tm), F32)
        for h in range(ATTN_KV_HEADS):
            k_out[h] = yk[:, h * hd:(h + 1) * hd].astype(BF16)
            v_out[h] = xv[:, h * hd:(h + 1) * hd].astype(BF16)
            vt_out[h, 0] = jnp.concatenate([xvt[h * hd:(h + 1) * hd, :], ones], axis=0).astype(BF16)

    kv_blk = SEG["ka"][2] // 256
    nk = t // tm
    return pl.pallas_call(
        body, name="attn_prep", grid=(nk,),
        in_specs=[pl.BlockSpec((tm, 512), lambda i: (i, 0)), pl.BlockSpec((tm, 256), lambda i: (i, kv_blk)),
                  pl.BlockSpec((1, 512), lambda i: (0, 0)), pl.BlockSpec((1, 128), lambda i: (0, 0)),
                  pl.BlockSpec((tm, 128), lambda i: (i, 0)), pl.BlockSpec((tm, 128), lambda i: (i, 0)),
                  pl.BlockSpec((512, 512), lambda i: (0, 0))],
        out_specs=[pl.BlockSpec((ATTN_Q_HEADS, tm, hd), lambda i: (0, i, 0)),
                   pl.BlockSpec((ATTN_Q_HEADS, hd, tm), lambda i: (0, 0, i)),
                   pl.BlockSpec((ATTN_KV_HEADS, tm, hd), lambda i: (0, i, 0)),
                   pl.BlockSpec((ATTN_KV_HEADS, tm, hd), lambda i: (0, i, 0)),
                   pl.BlockSpec((ATTN_KV_HEADS, 1, 2 * hd, tm), lambda i: (0, i, 0, 0))],
        out_shape=[SDS((ATTN_Q_HEADS, t, hd), BF16), SDS((ATTN_Q_HEADS, hd, t), BF16),
                   SDS((ATTN_KV_HEADS, t, hd), BF16), SDS((ATTN_KV_HEADS, t, hd), BF16),
                   SDS((ATTN_KV_HEADS, nk, 2 * hd, tm), BF16)],
        compiler_params=_params(("parallel",)),
    )(z, z, qn, kn, cos, sin, ones_bd)


def _attn_fwd(q, k, vt, ex=None):
    t = q.shape[1]
    tq = min(256, t)
    nk, tk = vt.shape[1], vt.shape[3]
    hd = ATTN_HEAD_DIM
    g = ATTN_Q_HEADS // ATTN_KV_HEADS

    def body(q_ref, k_ref, vt_ref, o_ref, lse_ref, s_scr):
        def pass_a(h, c, m8):
            part = tk // QK_DOTS_PER_CHUNK
            for lo in range(c * tk, (c + 1) * tk, part):
                st = _dot(k_ref[0, lo:lo + part, :], q_ref[h], NT)
                s_scr[h % 2, lo:lo + part, :] = st
                m8 = jnp.maximum(m8, jnp.max(st.reshape(part // 8, 8, tq), axis=0))
            return m8

        def pass_b(h, c, m, acc, after):
            e = jnp.exp(s_scr[h % 2, c * tk:(c + 1) * tk, :] - (m + after * 0.0)).astype(BF16)
            return acc + _dot(vt_ref[0, c], e)

        neg = jnp.full((8, tq), -jnp.inf, F32)
        m8 = neg
        for c in range(nk):
            m8 = pass_a(0, c, m8)
        outs = []
        for h in range(g):
            m = jnp.max(m8, axis=0, keepdims=True)
            acc = jnp.zeros((2 * hd, tq), F32)
            m8 = neg
            done = [m] * EXP_LAG
            for c in range(nk):
                if h + 1 < g:
                    m8 = pass_a(h + 1, c, m8)
                acc = pass_b(h, c, m, acc, done[-EXP_LAG])
                done.append(m8[0:1, :] if h + 1 < g else acc[hd:hd + 1, :])
            l = acc[hd:hd + 1, :]
            outs.append((acc[:hd, :] / l).T)
            lse_ref[h] = m + jnp.log(l)
        o_ref[...] = jnp.concatenate(outs, axis=-1)

    nq = t // tq
    first = lambda: jnp.logical_and(pl.program_id(0) == 0, pl.program_id(1) == 0)
    last = lambda: jnp.logical_and(pl.program_id(0) == ATTN_KV_HEADS - 1, pl.program_id(1) == nq - 1)
    xi, xo, xs, xscr, xargs = _ex_args(ex)
    return pl.pallas_call(
        _with_exchange(body, 3, 2, 1, ex, first, last), name="attn_fwd", grid=(ATTN_KV_HEADS, nq),
        in_specs=[pl.BlockSpec((g, tq, hd), lambda p, i: (p, i, 0)),
                  pl.BlockSpec((1, t, hd), lambda p, i: (p, 0, 0)),
                  pl.BlockSpec((1, nk, 2 * hd, tk), lambda p, i: (p, 0, 0, 0))] + xi,
        out_specs=[pl.BlockSpec((tq, g * hd), lambda p, i: (i, p)),
                   pl.BlockSpec((g, 1, tq), lambda p, i: (p, 0, i))] + xo,
        out_shape=[SDS((t, ATTN_WIDTH), F32), SDS((ATTN_Q_HEADS, 1, t), F32)] + xs,
        scratch_shapes=[pltpu.VMEM((2, t, tq), F32)] + xscr,
        compiler_params=_params(("arbitrary", "arbitrary")),
    )(q, k, vt, *xargs)


class _Dir:
    def __init__(self, lg, strict_future):
        c = RET_CHUNK
        ia = lax.broadcasted_iota(jnp.int32, (c, c), 0).astype(F32)
        ib = lax.broadcasted_iota(jnp.int32, (c, c), 1).astype(F32)
        col = lax.broadcasted_iota(jnp.int32, (c, 1), 0).astype(F32)
        row = lax.broadcasted_iota(jnp.int32, (1, c), 1).astype(F32)
        if strict_future:
            dist = ib - ia
            mask = dist > 0
            self.wq, self.wk, wk_row = c - col, col, row
        else:
            dist = ia - ib
            mask = dist >= 0
            self.wq, self.wk, wk_row = col + 1.0, c - 1.0 - col, c - 1.0 - row
        self.dist = jnp.maximum(dist, 0.0)
        self.d = jnp.where(mask, jnp.exp(self.dist * lg), 0.0)
        self.qd = jnp.exp(self.wq * lg)
        self.kd_col = jnp.exp(self.wk * lg)
        self.kd_row = jnp.exp(wk_row * lg)
        self.cd = jnp.exp(jnp.full((1, 1), float(c), F32) * lg)


def _ret_fwd(z, lgf, lgb, gnw, cos, sin):
    t = z.shape[0]
    c = RET_CHUNK
    nc = t // c
    hd = RET_HEAD_DIM
    unroll = 4 if nc % 4 == 0 else 1

    def body(lgf_ref, lgb_ref, q_ref, k_ref, v_ref, c_ref, s_ref, w_ref,
             qo_ref, ko_ref, vo_ref, orr_ref, on_ref, kt, uf, ub, sfa, sba):
        h = pl.program_id(0)
        fw = _Dir(lgf_ref[h], False)
        bw = _Dir(lgb_ref[h], True)
        cc, ss = c_ref[...], s_ref[...]
        qo_ref[...] = _rope(q_ref[...], cc, ss, hd // 4).astype(BF16)
        kr = _rope(k_ref[...], cc, ss, hd // 4) * (hd ** -0.5)
        ko_ref[...] = kr.astype(BF16)
        vo_ref[...] = v_ref[...].astype(BF16)
        for i in range(nc):
            kt[i] = kr[i * c:(i + 1) * c, :].T.astype(BF16)

        def rows(ci):
            return pl.ds(pl.multiple_of(ci * c, c), c)

        def kv_products(ci, carry):
            vv = vo_ref[rows(ci), :]
            ktf = kt[ci].astype(F32)
            uf[ci] = _dot((ktf * fw.kd_row).astype(BF16), vv)
            ub[ci] = _dot((ktf * bw.kd_row).astype(BF16), vv)
            return carry

        lax.fori_loop(0, nc, kv_products, 0, unroll=unroll)

        def scan(i, carry):
            sf, sb = carry
            j = nc - 1 - i
            sfa[i] = sf.astype(BF16)
            sba[j] = sb.astype(BF16)
            return sf * fw.cd + uf[i], sb * bw.cd + ub[j]

        zero = jnp.zeros((hd, hd), F32)
        lax.fori_loop(0, nc, scan, (zero, zero))
        gw = w_ref[...]

        def outputs(ci, carry):
            sl = rows(ci)
            qq, kk, vv = qo_ref[sl, :], ko_ref[sl, :], vo_ref[sl, :]
            a = _dot(qq, kk, NT)
            o = (_dot((a * fw.d).astype(BF16), vv) + _dot(qq, sfa[ci]) * fw.qd
                 + _dot((a * bw.d).astype(BF16), vv) + _dot(qq, sba[ci]) * bw.qd)
            orr_ref[sl, :] = o
            xc = o - jnp.mean(o, axis=-1, keepdims=True)
            var = jnp.mean(xc * xc, axis=-1, keepdims=True)
            on_ref[sl, :] = xc * lax.rsqrt(var + EPS) * gw
            return carry

        lax.fori_loop(0, nc, outputs, 0, unroll=unroll)

    smem = pl.BlockSpec(memory_space=pltpu.SMEM)
    col = lambda name: (lambda h: (0, SEG[name][2] // 128 + h))
    head = pl.BlockSpec((t, 128), lambda h: (0, h))
    full = pl.BlockSpec((t, 128), lambda h: (0, 0))
    return pl.pallas_call(
        body, name="ret_fwd", grid=(RET_HEADS,),
        in_specs=[smem, smem, pl.BlockSpec((t, 128), col("qr")), pl.BlockSpec((t, 128), col("kr")),
                  pl.BlockSpec((t, 128), col("vr")), full, full, pl.BlockSpec((1, 128), lambda h: (0, h))],
        out_specs=[head, head, head, head, head],
        out_shape=[SDS((t, RET_WIDTH), BF16)] * 3 + [SDS((t, RET_WIDTH), F32)] * 2,
        scratch_shapes=[pltpu.VMEM((nc, hd, c), BF16), pltpu.VMEM((nc, hd, hd), F32), pltpu.VMEM((nc, hd, hd), F32),
                        pltpu.VMEM((nc, hd, hd), BF16), pltpu.VMEM((nc, hd, hd), BF16)],
        compiler_params=_params(("parallel",)),
    )(lgf, lgb, z, z, z, cos, sin, gnw)


def _merge_fwd(x, z, oa, on, wb_t, wout):
    t, d = x.shape
    tm = min(256, t)

    def body(x_ref, ga_ref, gr_ref, gm0_ref, gm1_ref, oa_ref, on_ref, wb_ref, wo_ref, xn_ref, ya_ref, yb_ref):
        ga, gr = ga_ref[...], gr_ref[...]
        ua = ga * _sigmoid(ga) * oa_ref[...]
        ub = gr * _sigmoid(gr) * on_ref[...]
        ya = _dot(ua.astype(BF16), wb_ref[:, :512], NT)
        yb = _dot(ub.astype(BF16), wb_ref[:, 512:], NT)
        ya_ref[...] = ya
        yb_ref[...] = yb
        merged = _sigmoid(gm0_ref[...]) * ya + _sigmoid(gm1_ref[...]) * yb
        xn_ref[...] = x_ref[...] + _dot(merged.astype(BF16), wo_ref[...])

    row = lambda w, j: pl.BlockSpec((tm, w), lambda i: (i, j))
    const = lambda shape: pl.BlockSpec(shape, lambda i: (0, 0))
    return pl.pallas_call(
        body, name="merge_fwd", grid=(t // tm,),
        in_specs=[row(d, 0), row(512, SEG["ga"][2] // 512), row(512, SEG["gr"][2] // 512),
                  row(1024, SEG["gm"][2] // 1024), row(1024, SEG["gm"][2] // 1024 + 1),
                  row(512, 0), row(512, 0), const((d, 1024)), const((d, d))],
        out_specs=[row(d, 0), row(d, 0), row(d, 0)],
        out_shape=[SDS((t, d), F32)] * 3,
        compiler_params=_params(("parallel",)),
    )(x, z, z, z, z, oa, on, wb_t, wout)


def _final_loss(x, g, target):
    t, d = x.shape
    tm = min(512, t)
    n = t // tm

    def body(x_ref, g_ref, t_ref, dx_ref, dg_ref, loss_ref, acc_g, acc_l):
        i = pl.program_id(0)

        @pl.when(i == 0)
        def _():
            acc_g[...] = jnp.zeros_like(acc_g)
            acc_l[...] = jnp.zeros_like(acc_l)

        xv, gv = x_ref[...], g_ref[...]
        r = lax.rsqrt(jnp.mean(xv * xv, axis=-1, keepdims=True) + EPS)
        xh = xv * r
        err = xh * gv - t_ref[...]
        dy = err * (1.0 / d)
        gy = dy * gv
        dx_ref[...] = r * (gy - xh * jnp.mean(gy * xh, axis=-1, keepdims=True))
        acc_g[...] += jnp.sum((dy * xh).reshape(tm // 8, 8, d), axis=0)
        acc_l[...] += jnp.sum((err * err).reshape(tm // 8, 8, d), axis=0)

        @pl.when(i == n - 1)
        def _():
            dg_ref[...] = jnp.sum(acc_g[...], axis=0, keepdims=True)
            tot = jnp.sum(jnp.sum(acc_l[...], axis=0, keepdims=True), axis=1, keepdims=True)
            loss_ref[...] = jnp.broadcast_to(tot * (0.5 / d), (1, 128))

    return pl.pallas_call(
        body, name="final_loss", grid=(n,),
        in_specs=[pl.BlockSpec((tm, d), lambda i: (i, 0)), pl.BlockSpec((1, d), lambda i: (0, 0)),
                  pl.BlockSpec((tm, d), lambda i: (i, 0))],
        out_specs=[pl.BlockSpec((tm, d), lambda i: (i, 0)), pl.BlockSpec((1, d), lambda i: (0, 0)),
                   pl.BlockSpec((1, 128), lambda i: (0, 0))],
        out_shape=[SDS((t, d), F32), SDS((1, d), F32), SDS((1, 128), F32)],
        scratch_shapes=[pltpu.VMEM((8, d), F32), pltpu.VMEM((8, d), F32)],
        compiler_params=_params(("arbitrary",)),
    )(x, g, target)


def _merge_bwd(dxo, z, oa, on, ya, yb, wb_t, wout):
    t, d = dxo.shape
    tm = min(256, t)
    n = t // tm

    def body(dx_ref, ga_ref, gr_ref, gm0_ref, gm1_ref, oa_ref, on_ref, ya_ref, yb_ref, wb_ref, wo_ref,
             doa_ref, don_ref, dz_ref, dwo_ref, dwb_ref, acc_o, acc_b):
        i = pl.program_id(0)

        @pl.when(i == 0)
        def _():
            acc_o[...] = jnp.zeros_like(acc_o)
            acc_b[...] = jnp.zeros_like(acc_b)

        dxb = dx_ref[...].astype(BF16)
        ya, yb = ya_ref[...], yb_ref[...]
        g0, g1 = _sigmoid(gm0_ref[...]), _sigmoid(gm1_ref[...])
        mb = (g0 * ya + g1 * yb).astype(BF16)
        dm = _dot(dxb, wo_ref[...], NT)
        dya = (dm * g0).astype(BF16)
        dyb = (dm * g1).astype(BF16)
        dz_ref[:, 1024:2048] = (dm * ya * g0 * (1.0 - g0)).astype(BF16)
        dz_ref[:, 2048:3072] = (dm * yb * g1 * (1.0 - g1)).astype(BF16)

        def branch(g_ref, o_ref, dy, w, do_ref, lo):
            gv, ov = g_ref[...], o_ref[...]
            sg = _sigmoid(gv)
            silu = gv * sg
            du = _dot(dy, w)
            do_ref[...] = du * silu
            dz_ref[:, lo:lo + 512] = (du * ov * (sg * (1.0 + gv * (1.0 - sg)))).astype(BF16)
            acc_b[:, lo:lo + 512] += _dot(dy, (silu * ov).astype(BF16), TN)

        branch(ga_ref, oa_ref, dya, wb_ref[:, :512], doa_ref, 0)
        branch(gr_ref, on_ref, dyb, wb_ref[:, 512:], don_ref, 512)
        acc_o[...] += _dot(mb, dxb, TN)

        @pl.when(i == n - 1)
        def _():
            dwo_ref[...] = acc_o[...].astype(BF16)
            dwb_ref[...] = acc_b[...].astype(BF16)

    row = lambda w, j: pl.BlockSpec((tm, w), lambda i: (i, j))
    const = lambda shape: pl.BlockSpec(shape, lambda i: (0, 0))
    return pl.pallas_call(
        body, name="merge_bwd", grid=(n,),
        in_specs=[row(d, 0), row(512, SEG["ga"][2] // 512), row(512, SEG["gr"][2] // 512),
                  row(1024, SEG["gm"][2] // 1024), row(1024, SEG["gm"][2] // 1024 + 1),
                  row(512, 0), row(512, 0), row(d, 0), row(d, 0), const((d, 1024)), const((d, d))],
        out_specs=[row(512, 0), row(512, 0), row(3072, 0), const((d, d)), const((d, 1024))],
        out_shape=[SDS((t, 512), F32), SDS((t, 512), F32), SDS((t, 3072), BF16), SDS((d, d), BF16),
                   SDS((d, 1024), BF16)],
        scratch_shapes=[pltpu.VMEM((d, d), F32), pltpu.VMEM((d, 1024), F32)],
        compiler_params=_params(("arbitrary",)),
    )(dxo, z, z, z, z, oa, on, ya, yb, wb_t, wout)


def _ret_bwd(qrot, krot, vb, orr, don, gnw, lgf, lgb):
    t = qrot.shape[0]
    c = RET_CHUNK
    nc = t // c
    hd = RET_HEAD_DIM
    unroll = 2 if nc % 2 == 0 else 1

    def body(lgf_ref, lgb_ref, q_ref, k_ref, v_ref, o_ref, dn_ref, w_ref,
             dq_ref, dk_ref, dv_ref, dw_ref, dlf_ref, dlb_ref, qt, kt, dob, uf, ub, wf, wb, sfa, sba, gfa, gba):
        h = pl.program_id(0)
        fw = _Dir(lgf_ref[h], False)
        bw = _Dir(lgb_ref[h], True)
        fw.dt, bw.dt = fw.d.T, bw.d.T

        o = o_ref[...]
        xc = o - jnp.mean(o, axis=-1, keepdims=True)
        r = lax.rsqrt(jnp.mean(xc * xc, axis=-1, keepdims=True) + EPS)
        xh = xc * r
        dn = dn_ref[...]
        gy = dn * w_ref[...]
        d_o = r * (gy - jnp.mean(gy, axis=-1, keepdims=True) - xh * jnp.mean(gy * xh, axis=-1, keepdims=True))
        dw_ref[...] = jnp.sum(dn * xh, axis=0, keepdims=True)
        dob[...] = d_o.astype(BF16)
        for i in range(nc):
            qt[i] = q_ref[i * c:(i + 1) * c, :].astype(F32).T.astype(BF16)
            kt[i] = k_ref[i * c:(i + 1) * c, :].astype(F32).T.astype(BF16)

        def rows(ci):
            return pl.ds(pl.multiple_of(ci * c, c), c)

        def products(ci, carry):
            sl = rows(ci)
            vv, do32 = v_ref[sl, :], dob[sl, :].astype(F32)
            ktf = kt[ci].astype(F32)
            uf[ci] = _dot((ktf * fw.kd_row).astype(BF16), vv)
            ub[ci] = _dot((ktf * bw.kd_row).astype(BF16), vv)
            wf[ci] = _dot(qt[ci], (do32 * fw.qd).astype(BF16))
            wb[ci] = _dot(qt[ci], (do32 * bw.qd).astype(BF16))
            return carry

        lax.fori_loop(0, nc, products, 0, unroll=unroll)

        def scan(i, carry):
            sf, sb, gf, gb = carry
            j = nc - 1 - i
            sfa[i] = sf.astype(BF16)
            sba[j] = sb.astype(BF16)
            gfa[j] = gf.astype(BF16)
            gba[i] = gb.astype(BF16)
            return sf * fw.cd + uf[i], sb * bw.cd + ub[j], gf * fw.cd + wf[j], gb * bw.cd + wb[i]

        zero = jnp.zeros((hd, hd), F32)
        lax.fori_loop(0, nc, scan, (zero, zero, zero, zero))

        def one_dir(p, s_all, g_all, ci, qq, kk, vv, do, a, bm, at, bt):
            sb, gb = s_all[ci], g_all[ci]
            doq = (do.astype(F32) * p.qd).astype(BF16)
            dqc = _dot(doq, sb, NT)
            dq = _dot((bm * p.d).astype(BF16), kk) + dqc
            kkd = (kk.astype(F32) * p.kd_col).astype(BF16)
            dv = _dot((at * p.dt).astype(BF16), do) + _dot(kkd, gb)
            dk2 = _dot(vv, gb, NT) * p.kd_col
            dk = _dot((bt * p.dt).astype(BF16), qq) + dk2
            terms = (p.dist * p.d * a * bm + p.wq * qq.astype(F32) * dqc + p.wk * kk.astype(F32) * dk2
                     + (float(c) * p.cd) * gb.astype(F32) * sb.astype(F32))
            return dq, dk, dv, terms

        def chunk(ci, carry):
            af, ab = carry
            sl = rows(ci)
            qq, kk, vv, do = q_ref[sl, :], k_ref[sl, :], v_ref[sl, :], dob[sl, :]
            a, bm = _dot(qq, kk, NT), _dot(do, vv, NT)
            at, bt = _dot(kk, qq, NT), _dot(vv, do, NT)
            dqf, dkf, dvf, tf = one_dir(fw, sfa, gfa, ci, qq, kk, vv, do, a, bm, at, bt)
            dqb, dkb, dvb, tb = one_dir(bw, sba, gba, ci, qq, kk, vv, do, a, bm, at, bt)
            dq_ref[sl, :] = dqf + dqb
            dk_ref[sl, :] = dkf + dkb
            dv_ref[sl, :] = dvf + dvb
            return af + tf, ab + tb

        af, ab = lax.fori_loop(0, nc, chunk, (zero, zero), unroll=unroll)
        tot = lambda m: jnp.sum(jnp.sum(m, axis=0, keepdims=True), axis=1, keepdims=True)
        dlf_ref[...] = jnp.broadcast_to(tot(af).reshape(1, 1, 1), (1, 8, 128))
        dlb_ref[...] = jnp.broadcast_to(tot(ab).reshape(1, 1, 1), (1, 8, 128))

    smem = pl.BlockSpec(memory_space=pltpu.SMEM)
    head = pl.BlockSpec((t, 128), lambda h: (0, h))
    vec = pl.BlockSpec((1, 128), lambda h: (0, h))
    scal = pl.BlockSpec((1, 8, 128), lambda h: (h, 0, 0))
    mats = lambda dt: pltpu.VMEM((nc, hd, hd), dt)
    return pl.pallas_call(
        body, name="ret_bwd", grid=(RET_HEADS,),
        in_specs=[smem, smem, head, head, head, head, head, vec],
        out_specs=[head, head, head, vec, scal, scal],
        out_shape=[SDS((t, RET_WIDTH), F32)] * 3 + [SDS((1, RET_WIDTH), F32), SDS((RET_HEADS, 8, 128), F32),
                                                   SDS((RET_HEADS, 8, 128), F32)],
        scratch_shapes=[pltpu.VMEM((nc, hd, c), BF16), pltpu.VMEM((nc, hd, c), BF16), pltpu.VMEM((t, hd), BF16),
                        mats(F32), mats(F32), mats(F32), mats(F32), mats(BF16), mats(BF16), mats(BF16), mats(BF16)],
        compiler_params=_params(("parallel",)),
    )(lgf, lgb, qrot, krot, vb, orr, don, gnw)


def _ret_post_bwd(dq, dk, dv, cos, sin):
    t = dq.shape[0]
    tm = min(512, t)
    hd = RET_HEAD_DIM

    def body(dq_ref, dk_ref, dv_ref, c_ref, s_ref, oq_ref, ok_ref, ov_ref):
        cc = jnp.concatenate([c_ref[...]] * 4, axis=-1)
        ss = jnp.concatenate([s_ref[...]] * 4, axis=-1)
        oq_ref[...] = _rope_bwd(dq_ref[...], cc, ss, hd // 4).astype(BF16)
        ok_ref[...] = (_rope_bwd(dk_ref[...], cc, ss, hd // 4) * (hd ** -0.5)).astype(BF16)
        ov_ref[...] = dv_ref[...].astype(BF16)

    blk = pl.BlockSpec((tm, 512), lambda i: (i, 0))
    tab = pl.BlockSpec((tm, 128), lambda i: (i, 0))
    return pl.pallas_call(
        body, name="ret_post_bwd", grid=(t // tm,),
        in_specs=[blk, blk, blk, tab, tab], out_specs=[blk, blk, blk],
        out_shape=[SDS((t, 512), BF16)] * 3,
        compiler_params=_params(("parallel",)),
    )(dq, dk, dv, cos, sin)


def _attn_bwd(q, qt, k, v, doa, oa, lse, ex=None):
    t = q.shape[1]
    tq = min(256, t)
    nq = t // tq
    tk = min(ATTN_BWD_KEY_CHUNK, t)
    nk = t // tk
    hd = ATTN_HEAD_DIM
    scale = hd ** -0.5

    def body(q_ref, qt_ref, k_ref, v_ref, do_ref, o_ref, lse_ref, dq_ref, dkt_ref, dvt_ref):
        p, i = pl.program_id(0), pl.program_id(1)

        @pl.when(jnp.logical_and(p % 2 == 0, i == 0))
        def _():
            dkt_ref[...] = jnp.zeros_like(dkt_ref)
            dvt_ref[...] = jnp.zeros_like(dvt_ref)

        dov, ov = do_ref[...], o_ref[...]
        dovt = dov.T
        lanes = lambda col: jnp.concatenate([col] * (tk // 128), axis=1)
        outs = []
        for j in range(2):
            qq, qqt = q_ref[j], qt_ref[j]
            do32 = dov[:, j * hd:(j + 1) * hd]
            do, dot_ = do32.astype(BF16), dovt[j * hd:(j + 1) * hd, :].astype(BF16)
            dd = lanes(jnp.broadcast_to(jnp.sum(do32 * ov[:, j * hd:(j + 1) * hd], axis=1, keepdims=True), (tq, 128)))
            lse_j = lanes(jnp.broadcast_to(lse_ref[j], (128, tq)).T)
            dq = jnp.zeros((tq, hd), F32)
            for c in range(nk):
                sl = slice(c * tk, (c + 1) * tk)
                kc, vc = k_ref[0, sl, :], v_ref[0, sl, :]
                pr = jnp.exp(_dot(qq, kc, NT) - lse_j)
                ds = (pr * (_dot(do, vc, NT) - dd)).astype(BF16)
                dvt_ref[0, :, sl] += _dot(dot_, pr.astype(BF16))
                dkt_ref[0, :, sl] += _dot(qqt, ds)
                dq = dq + _dot(ds, kc)
            outs.append(dq * scale)
        dq_ref[...] = jnp.concatenate(outs, axis=-1)

    kv = pl.BlockSpec((1, t, hd), lambda p, i: (p // 2, 0, 0))
    kvt = pl.BlockSpec((1, hd, t), lambda p, i: (p // 2, 0, 0))
    pair = pl.BlockSpec((tq, 128), lambda p, i: (i, p))
    first = lambda: jnp.logical_and(pl.program_id(0) == 0, pl.program_id(1) == 0)
    last = lambda: jnp.logical_and(pl.program_id(0) == 3, pl.program_id(1) == nq - 1)
    xi, xo, xs, xscr, xargs = _ex_args(ex)
    return pl.pallas_call(
        _with_exchange(body, 7, 3, 0, ex, first, last), name="attn_bwd", grid=(4, nq),
        in_specs=[pl.BlockSpec((2, tq, hd), lambda p, i: (p, i, 0)), pl.BlockSpec((2, hd, tq), lambda p, i: (p, 0, i)),
                  kv, kv, pair, pair, pl.BlockSpec((2, 1, tq), lambda p, i: (p, 0, i))] + xi,
        out_specs=[pair, kvt, kvt] + xo,
        out_shape=[SDS((t, ATTN_WIDTH), F32), SDS((ATTN_KV_HEADS, hd, t), F32),
                   SDS((ATTN_KV_HEADS, hd, t), F32)] + xs,
        scratch_shapes=xscr,
        compiler_params=_params(("arbitrary", "arbitrary")),
    )(q, qt, k, v, doa, oa, lse, *xargs)


def _attn_post_bwd(dq, dk, dv, z, qn, kn, cos, sin, ones_bd):
    t = z.shape[0]
    tm = min(512, t)
    n = t // tm
    hd = ATTN_HEAD_DIM

    def body(dq_ref, dk_ref, dv_ref, zq_ref, zkv_ref, qn_ref, kn_ref, c_ref, s_ref, b_ref,
             dz_ref, dqn_ref, dkn_ref, acc_q, acc_k):
        i = pl.program_id(0)

        @pl.when(i == 0)
        def _():
            acc_q[...] = jnp.zeros_like(acc_q)
            acc_k[...] = jnp.zeros_like(acc_k)

        bd = b_ref[...]
        c2, s2 = c_ref[...], s_ref[...]

        def norm_bwd(dy, x, w, ones, cos_t, sin_t, acc):
            dyr = _rope_bwd(dy, cos_t, sin_t, hd // 4)
            r = lax.rsqrt(_group_mean(x * x, ones) + EPS)
            xh = x * r
            gy = dyr * w
            acc[...] += jnp.sum((dyr * xh).reshape(tm // 8, 8, x.shape[-1]), axis=0)
            return r * (gy - xh * _group_mean(gy * xh, ones))

        cq = jnp.concatenate([c2] * 4, axis=-1)
        sq = jnp.concatenate([s2] * 4, axis=-1)
        dz_ref[:, :512] = norm_bwd(dq_ref[...], zq_ref[...], qn_ref[...], bd, cq, sq, acc_q).astype(BF16)
        zkv = zkv_ref[...]
        dkk = jnp.concatenate([dk_ref[0], dk_ref[1]], axis=0).T
        dz_ref[:, 512:640] = norm_bwd(dkk, zkv[:, :128], kn_ref[...], bd[:128, :128], c2, s2, acc_k).astype(BF16)
        dz_ref[:, 640:768] = jnp.concatenate([dv_ref[0], dv_ref[1]], axis=0).T.astype(BF16)

        @pl.when(i == n - 1)
        def _():
            dqn_ref[...] = jnp.sum(acc_q[...], axis=0, keepdims=True)
            dkn_ref[...] = jnp.sum(acc_k[...], axis=0, keepdims=True)

    kv_blk = SEG["ka"][2] // 256
    kvs = pl.BlockSpec((ATTN_KV_HEADS, hd, tm), lambda i: (0, 0, i))
    const = lambda shape: pl.BlockSpec(shape, lambda i: (0, 0))
    return pl.pallas_call(
        body, name="attn_post_bwd", grid=(n,),
        in_specs=[pl.BlockSpec((tm, 512), lambda i: (i, 0)), kvs, kvs,
                  pl.BlockSpec((tm, 512), lambda i: (i, 0)), pl.BlockSpec((tm, 256), lambda i: (i, kv_blk)),
                  const((1, 512)), const((1, 128)),
                  pl.BlockSpec((tm, 128), lambda i: (i, 0)), pl.BlockSpec((tm, 128), lambda i: (i, 0)),
                  const((512, 512))],
        out_specs=[pl.BlockSpec((tm, 768), lambda i: (i, 0)), const((1, 512)), const((1, 128))],
        out_shape=[SDS((t, 768), BF16), SDS((1, 512), F32), SDS((1, 128), F32)],
        scratch_shapes=[pltpu.VMEM((8, 512), F32), pltpu.VMEM((8, 128), F32)],
        compiler_params=_params(("arbitrary",)),
    )(dq, dk, dv, z, z, qn, kn, cos, sin, ones_bd)


def _in_bwd(dxo, x, g, w_t, dz_a, dz_m, dqr, dkr, dvr, after=None):
    t, d = x.shape
    tm = min(256, t)
    n = t // tm
    parts = [(0, 0, 768, 0), (1, 0, 512, SEG["ga"][0]), (2, 0, 512, SEG["qr"][0]), (3, 0, 512, SEG["kr"][0]),
             (4, 0, 512, SEG["vr"][0]), (1, 512, 2560, SEG["gr"][0])]

    def body(dx_ref, x_ref, g_ref, w_ref, a_ref, m_ref, q_ref, k_ref, v_ref, o_ref, dg_ref, acc):
        i = pl.program_id(0)

        @pl.when(i == 0)
        def _():
            acc[...] = jnp.zeros_like(acc)

        pieces = [a_ref, m_ref, q_ref, k_ref, v_ref]
        dh = jnp.zeros((tm, d), F32)
        for pi, lo, w, row in parts:
            dh = dh + _dot(pieces[pi][:, lo:lo + w], w_ref[row:row + w, :])
        xv = x_ref[...]
        r = lax.rsqrt(jnp.mean(xv * xv, axis=-1, keepdims=True) + EPS)
        xh = xv * r
        gy = dh * g_ref[...]
        o_ref[...] = dx_ref[...] + r * (gy - xh * jnp.mean(gy * xh, axis=-1, keepdims=True))
        acc[...] += jnp.sum((dh * xh).reshape(tm // 8, 8, d), axis=0)

        @pl.when(i == n - 1)
        def _():
            dg_ref[...] = jnp.sum(acc[...], axis=0, keepdims=True)

    row = lambda w: pl.BlockSpec((tm, w), lambda i: (i, 0))
    const = lambda shape: pl.BlockSpec(shape, lambda i: (0, 0))
    extra = [] if after is None else [after]
    return pl.pallas_call(
        (lambda *refs: body(*refs[:9], *refs[9 + len(extra):])), name="in_bwd", grid=(n,),
        in_specs=[row(d), row(d), const((1, d)), const((D_IN, d)), row(768), row(3072), row(512), row(512),
                  row(512)] + [const(a.shape) for a in extra],
        out_specs=[row(d), const((1, d))],
        out_shape=[SDS((t, d), F32), SDS((1, d), F32)],
        scratch_shapes=[pltpu.VMEM((8, d), F32)],
        compiler_params=_params(("arbitrary",)),
    )(dxo, x, g, w_t, dz_a, dz_m, dqr, dkr, dvr, *extra)


def _dw_in(h_t, dz_a, dz_m, dqr, dkr, dvr):
    d, t = h_t.shape
    tn = 256
    parts = [(0, 0, 0, 3), (1, 0, SEG["ga"][0] // tn, 2), (2, 0, SEG["qr"][0] // tn, 2),
             (3, 0, SEG["kr"][0] // tn, 2), (4, 0, SEG["vr"][0] // tn, 2), (1, 2, SEG["gr"][0] // tn, 10)]
    pieces = [dz_a, dz_m, dqr, dkr, dvr]

    def col_block(pi):
        mine = [(c0, r0, n) for q, c0, r0, n in parts if q == pi]

        def index(j):
            c0, r0, n = mine[0]
            blk = c0 + jnp.clip(j - r0, 0, n - 1)
            for c0, r0, n in mine[1:]:
                blk = jnp.where(j >= r0, c0 + jnp.clip(j - r0, 0, n - 1), blk)
            return 0, blk

        return index

    def body(h_ref, *refs):
        o_ref = refs[-1]
        j = pl.program_id(0)
        for pi, _, r0, n in parts:
            @pl.when(jnp.logical_and(j >= r0, j < r0 + n))
            def _(p_ref=refs[pi]):
                o_ref[...] = _dot(h_ref[...], p_ref[...]).T.astype(BF16)

    return pl.pallas_call(
        body, name="dw_in", grid=(D_IN // tn,),
        in_specs=[pl.BlockSpec((d, t), lambda j: (0, 0))] + [pl.BlockSpec((t, tn), col_block(pi)) for pi in range(5)],
        out_specs=pl.BlockSpec((tn, d), lambda j: (j, 0)),
        out_shape=SDS((D_IN, d), BF16),
        compiler_params=_params(("arbitrary",)),
    )(h_t, *pieces)


def _adamw_math(w, g, m, v):
    mn = ADAM_B1 * m + (1.0 - ADAM_B1) * g
    vn = ADAM_B2 * v + (1.0 - ADAM_B2) * (g * g)
    m_hat = mn / (1.0 - ADAM_B1 ** ADAM_STEP)
    v_hat = vn / (1.0 - ADAM_B2 ** ADAM_STEP)
    return -ADAM_LR * (m_hat / (jnp.sqrt(v_hat) + ADAM_EPS) + ADAM_WD * w), mn, vn


def _sum_adamw(recvs, w, m, v, lane0, tn):
    depth, r, c = w.shape
    j0 = lane0 // tn

    def body(r0_ref, r1_ref, w_ref, m_ref, v_ref, g_ref, d_ref, mo_ref, vo_ref):
        def run(r_ref):
            g = r_ref[0].astype(F32)
            for s in range(1, N_DEV):
                g = g + r_ref[s].astype(F32)
            g_ref[0] = g
            d_ref[0], mo_ref[0], vo_ref[0] = _adamw_math(w_ref[0], g, m_ref[0], v_ref[0])

        for l, r_ref in enumerate((r0_ref, r1_ref)):
            pl.when(pl.program_id(0) == l)(functools.partial(run, r_ref))

    slots = pl.BlockSpec((N_DEV, r, tn), lambda l, j: (0, 0, j0 + j))
    blk = pl.BlockSpec((1, r, tn), lambda l, j: (l, 0, j))
    return pl.pallas_call(
        body, name="sum_adamw", grid=(depth, c // tn),
        in_specs=[slots, slots, blk, blk, blk], out_specs=[blk] * 4, out_shape=[SDS(w.shape, F32)] * 4,
        compiler_params=_params(("parallel", "parallel")),
    )(recvs[0], recvs[1], w, m, v)


def _adamw(w, g, m, v):
    rows, cols = w.shape
    tr = 256 if rows % 256 == 0 else rows

    def body(w_ref, g_ref, m_ref, v_ref, d_ref, mo_ref, vo_ref):
        d_ref[...], mo_ref[...], vo_ref[...] = _adamw_math(w_ref[...], g_ref[...], m_ref[...], v_ref[...])

    blk = pl.BlockSpec((tr, cols), lambda i: (i, 0))
    return pl.pallas_call(
        body, name="adamw", grid=(rows // tr,),
        in_specs=[blk] * 4, out_specs=[blk] * 3, out_shape=[SDS((rows, cols), F32)] * 3,
        compiler_params=_params(("parallel",)),
    )(w, g, m, v)


def _all_gather(shards):
    na = len(shards)
    chips = (4, 2, 6)

    def body(*refs):
        ins, outs = refs[:na], refs[na:2 * na]
        send_sems, recv_sems, local_sems = refs[2 * na:]
        _, mine = _flip(0)

        def rows(a, idx):
            r = shards[a].shape[0]
            return outs[a].at[pl.ds(pl.multiple_of(idx * r, 16), r), :]

        def copy(a, slot, block_idx, to, src=None):
            return pltpu.make_async_remote_copy(
                src_ref=rows(a, block_idx) if src is None else src, dst_ref=rows(a, block_idx),
                send_sem=send_sems.at[a, slot], recv_sem=recv_sems.at[a, slot],
                device_id=to, device_id_type=MESH_ID)

        sibling, sibling_idx = _flip(1)
        local, started = [], []
        for a in range(na):
            cp = pltpu.make_async_copy(ins[a], rows(a, mine), local_sems.at[a])
            cp.start()
            local.append(cp)
            first = [copy(a, 0, mine, sibling, src=ins[a])]
            first += [copy(a, 1 + j, mine, _flip(k)[0], src=ins[a]) for j, k in enumerate(chips)]
            for cp in first:
                cp.start()
            started += first
        for a in range(na):
            for j, k in enumerate(chips):
                _, theirs = _flip(k)
                copy(a, 1 + j, theirs, _flip(0)[0]).wait_recv()
                fwd = copy(a, 4 + j, theirs, sibling)
                fwd.start()
                started.append(fwd)
        for a in range(na):
            copy(a, 0, sibling_idx, _flip(0)[0]).wait_recv()
            for j, k in enumerate(chips):
                _, theirs = _flip(k | 1)
                copy(a, 4 + j, theirs, _flip(0)[0]).wait_recv()
        for cp in started:
            cp.wait_send()
        for cp in local:
            cp.wait()

    return pl.pallas_call(
        body, name="all_gather_weights",
        in_specs=[ANY] * na, out_specs=[ANY] * na,
        out_shape=[SDS((N_DEV * s.shape[0], s.shape[1]), s.dtype) for s in shards],
        scratch_shapes=[pltpu.SemaphoreType.DMA((na, 7)), pltpu.SemaphoreType.DMA((na, 7)),
                        pltpu.SemaphoreType.DMA((na,))],
        compiler_params=pltpu.CompilerParams(has_side_effects=True),
    )(*shards)


def _scatter_blocks_of(g_ref, rows, idx):
    return g_ref.at[pl.ds(pl.multiple_of(idx * rows, 16), rows), :]


def _scatter_start(g):
    rows = g.shape[0] // N_DEV
    land_shape = (N_DEV, rows, g.shape[1])

    def body(g_ref, land_ref, send_sems, recv_sems, g_thru, land_thru, token):
        _, mine = _flip(0)
        for k in range(1, N_DEV):
            peer, theirs = _flip(k)
            pltpu.make_async_remote_copy(
                src_ref=_scatter_blocks_of(g_ref, rows, theirs), dst_ref=land_ref.at[mine],
                send_sem=send_sems.at[k - 1], recv_sem=recv_sems.at[k - 1],
                device_id=peer, device_id_type=MESH_ID).start()
        token[...] = jnp.zeros_like(token)

    hbm, sem = pl.BlockSpec(memory_space=pltpu.HBM), pl.BlockSpec(memory_space=pltpu.SEMAPHORE)
    return pl.pallas_call(
        body, name="scatter_start",
        out_shape=(pltpu.SemaphoreType.DMA((N_DEV - 1,)), pltpu.SemaphoreType.DMA((N_DEV - 1,)),
                   pltpu.HBM(g.shape, g.dtype), pltpu.HBM(land_shape, g.dtype), SDS((8, 128), F32)),
        in_specs=(hbm, hbm), out_specs=(sem, sem, hbm, hbm, pl.BlockSpec(memory_space=pltpu.VMEM)),
        input_output_aliases={0: 2, 1: 3},
        compiler_params=pltpu.CompilerParams(has_side_effects=pltpu.SideEffectType.DATAFLOW_SIDE_EFFECTING),
    )(pltpu.with_memory_space_constraint(g, pltpu.HBM),
      pltpu.with_memory_space_constraint(lax.empty(land_shape, g.dtype), pltpu.HBM))


def _scatter_wait(send_sems, recv_sems, g_thru, land_thru, after):
    rows = g_thru.shape[0] // N_DEV

    def body(g_ref, land_ref, send_sems, recv_sems, after_ref, g_dead, got_ref):
        me, _ = _flip(0)
        for k in range(1, N_DEV):
            _, theirs = _flip(k)
            copy = pltpu.make_async_remote_copy(
                src_ref=_scatter_blocks_of(g_ref, rows, theirs), dst_ref=land_ref.at[theirs],
                send_sem=send_sems.at[k - 1], recv_sem=recv_sems.at[k - 1],
                device_id=me, device_id_type=MESH_ID)
            copy.wait_send()
            copy.wait_recv()

    hbm, sem = pl.BlockSpec(memory_space=pltpu.HBM), pl.BlockSpec(memory_space=pltpu.SEMAPHORE)
    return pl.pallas_call(
        body, name="scatter_wait",
        out_shape=(pltpu.HBM(g_thru.shape, g_thru.dtype), pltpu.HBM(land_thru.shape, land_thru.dtype)),
        in_specs=(hbm, hbm, sem, sem, ANY), out_specs=(hbm, hbm), input_output_aliases={0: 0, 1: 1},
        compiler_params=pltpu.CompilerParams(has_side_effects=pltpu.SideEffectType.DATAFLOW_SIDE_EFFECTING),
    )(g_thru, land_thru, send_sems, recv_sems, after)


def _place_own_block(g, land):
    rows = g.shape[0] // N_DEV

    def body(g_ref, land_ref, out_ref, sem):
        _, mine = _flip(0)
        cp = pltpu.make_async_copy(_scatter_blocks_of(g_ref, rows, mine), out_ref.at[mine], sem)
        cp.start()
        cp.wait()

    return pl.pallas_call(
        body, name="place_own_block", out_shape=SDS(land.shape, land.dtype),
        in_specs=[ANY, ANY], out_specs=ANY, input_output_aliases={1: 0},
        scratch_shapes=[pltpu.SemaphoreType.DMA],
        compiler_params=pltpu.CompilerParams(has_side_effects=True),
    )(g, land)


def _all_reduce_small(packed):
    shape = packed.shape

    def body(p_ref, o_ref, slots, send_sems, recv_sems):
        me, mine = _flip(0)
        slots[mine] = p_ref[...]
        sends = []
        for k in range(1, N_DEV):
            peer, _ = _flip(k)
            cp = pltpu.make_async_remote_copy(
                src_ref=p_ref, dst_ref=slots.at[mine], send_sem=send_sems.at[k - 1], recv_sem=recv_sems.at[k - 1],
                device_id=peer, device_id_type=MESH_ID)
            cp.start()
            sends.append(cp)
        for k in range(1, N_DEV):
            _, theirs = _flip(k)
            pltpu.make_async_remote_copy(
                src_ref=p_ref, dst_ref=slots.at[theirs], send_sem=send_sems.at[k - 1],
                recv_sem=recv_sems.at[k - 1], device_id=me, device_id_type=MESH_ID).wait_recv()
        for cp in sends:
            cp.wait_send()
        acc = slots[0]
        for s in range(1, N_DEV):
            acc = acc + slots[s]
        o_ref[...] = acc

    vm = pl.BlockSpec(memory_space=pltpu.VMEM)
    return pl.pallas_call(
        body, name="all_reduce_small", in_specs=[vm], out_specs=vm, out_shape=SDS(shape, F32),
        scratch_shapes=[pltpu.VMEM((N_DEV,) + shape, F32), pltpu.SemaphoreType.DMA((7,)),
                        pltpu.SemaphoreType.DMA((7,))],
        compiler_params=pltpu.CompilerParams(has_side_effects=True),
    )(packed)


def _layer_fwd(x, p, tabs, ex):
    z, h_t = _in_proj(x, p["norm_g"], p["w_in_t"])
    q, qt, k, v, vt = _attn_prep(z, p["qn"], p["kn"], tabs["ca"], tabs["sa"], tabs["ones"])
    oa, lse, *gathered = _attn_fwd(q, k, vt, ex)
    qrot, krot, vb, orr, on = _ret_fwd(z, p["lgf"], p["lgb"], p["gnw"], tabs["cr"], tabs["sr"])
    return z, h_t, q, qt, k, v, lse, oa, qrot, krot, vb, orr, on, gathered


def _layer_bwd(dxo, s, p, tabs, ex_attn, scatter_w_in):
    doa, don, dz_m, d_wout, d_wb_t = _merge_bwd(dxo, s["z"], s["oa"], s["on"], s["ya"], s["yb"], p["wb_t"], p["w_out"])
    dq_a, dk_a, dv_a, *recv_attn = _attn_bwd(s["q"], s["qt"], s["k"], s["v"], doa, s["oa"], s["lse"],
                                              ex_attn(d_wb_t, d_wout))
    dz_a, d_qn, d_kn = _attn_post_bwd(dq_a, dk_a, dv_a, s["z"], p["qn"], p["kn"], tabs["ca"], tabs["sa"],
                                      tabs["ones"])
    dq_r, dk_r, dv_r, d_gnw, d_lgf, d_lgb = _ret_bwd(s["qrot"], s["krot"], s["vb"], s["orr"], don, p["gnw"],
                                                     p["lgf"], p["lgb"])
    dqr, dkr, dvr = _ret_post_bwd(dq_r, dk_r, dv_r, tabs["cr"], tabs["sr"])
    buf = _dw_in(s["h_t"], dz_a, dz_m, dqr, dkr, dvr)
    recv_in, token = None, None
    if scatter_w_in:
        send_sems, recv_sems, buf_thru, land, token = _scatter_start(buf)
    dx, d_norm_g = _in_bwd(dxo, s["x"], p["norm_g"], p["w_in_t"], dz_a, dz_m, dqr, dkr, dvr, token)
    if scatter_w_in:
        recv_in = _place_own_block(*_scatter_wait(send_sems, recv_sems, buf_thru, land, dx))
    grads = dict(w_in_t=buf, wb_t=d_wb_t, w_out=d_wout, norm_g=d_norm_g, gnw=d_gnw,
                 qn=d_qn.reshape(ATTN_Q_HEADS, ATTN_HEAD_DIM).sum(axis=0),
                 kn=d_kn.reshape(ATTN_KV_HEADS, ATTN_HEAD_DIM).sum(axis=0),
                 lgf=d_lgf[:, 0, 0], lgb=d_lgb[:, 0, 0])
    return dx, grads, recv_attn, recv_in


def _adamw_nd(w, g, m, v):
    shape = w.shape
    two_d = (1, shape[0]) if w.ndim == 1 else (-1, shape[-1])
    out = _adamw(w.reshape(two_d), g.reshape(two_d), m.reshape(two_d), v.reshape(two_d))
    return tuple(o.reshape(shape) for o in out)


def kernel(x, norm_g, w_in, attn_q_norm, attn_k_norm, ret_decay_fwd, ret_decay_bwd, ret_gn_w, w_branch_attn, w_branch_ret, w_out, final_norm_g, loss_target, m_norm_g, m_w_in, m_attn_q_norm, m_attn_k_norm, m_ret_decay_fwd, m_ret_decay_bwd, m_ret_gn_w, m_w_branch_attn, m_w_branch_ret, m_w_out, m_final_norm_g, v_norm_g, v_w_in, v_attn_q_norm, v_attn_k_norm, v_ret_decay_fwd, v_ret_decay_bwd, v_ret_gn_w, v_w_branch_attn, v_w_branch_ret, v_w_out, v_final_norm_g):
    t, d = x.shape[1], x.shape[2]
    x2, target = x[0], loss_target[0]

    w_in_sh, wb_sh, wout_sh = [], [], []
    for l in range(DEPTH):
        w_in_sh.append(jnp.swapaxes(w_in[l], 0, 1).astype(BF16))
        wb_sh.append(jnp.concatenate([w_branch_attn[l].T, w_branch_ret[l].T], axis=1).astype(BF16))
        wout_sh.append(w_out[l].astype(BF16))

    ca, sa = _rope_tables(t, ATTN_HEAD_DIM)
    cr, sr = _rope_tables(t, RET_HEAD_DIM)
    grp = jnp.arange(ATTN_WIDTH) // ATTN_HEAD_DIM
    tabs = dict(ca=jnp.tile(ca, (1, 2)), sa=jnp.tile(sa, (1, 2)), cr=cr, sr=sr,
                ones=jnp.where(grp[:, None] == grp[None, :], 1.0 / ATTN_HEAD_DIM, 0.0).astype(BF16))
    layers = []
    for l in range(DEPTH):
        layers.append(dict(
            norm_g=norm_g[l][None], qn=jnp.tile(attn_q_norm[l], ATTN_Q_HEADS)[None],
            kn=jnp.tile(attn_k_norm[l], ATTN_KV_HEADS)[None], gnw=ret_gn_w[l][None],
            lgf=jax.nn.log_sigmoid(ret_decay_fwd[l]), lgb=jax.nn.log_sigmoid(ret_decay_bwd[l])))

    layers[0]["w_in_t"], = _all_gather([w_in_sh[0]])
    gathers = [_Exchange("gather", [wb_sh[0], wout_sh[0], w_in_sh[1]]), _Exchange("gather", [wb_sh[1], wout_sh[1]])]
    h = x2
    saved = []
    for l in range(DEPTH):
        p = layers[l]
        z, h_t, q, qt, k, v, lse, oa, qrot, krot, vb, orr, on, got = _layer_fwd(h, p, tabs, gathers[l])
        p["wb_t"], p["w_out"] = got[0], got[1]
        if l == 0:
            layers[1]["w_in_t"] = got[2]
        xn, ya, yb = _merge_fwd(h, z, oa, on, p["wb_t"], p["w_out"])
        saved.append(dict(x=h, z=z, h_t=h_t, q=q, qt=qt, k=k, v=v, lse=lse, oa=oa, qrot=qrot, krot=krot, vb=vb,
                          orr=orr, on=on, ya=ya, yb=yb))
        h = xn
    dx, d_final_g, loss_part = _final_loss(h, final_norm_g[None], target)

    grads = [None] * DEPTH
    dx, grads[1], _, _ = _layer_bwd(dx, saved[1], layers[1], tabs, lambda *a: None, False)
    g1 = grads[1]
    ex_attn = lambda d_wb_t, d_wout: _Exchange("scatter", [g1["w_in_t"], g1["wb_t"], g1["w_out"], d_wb_t, d_wout])
    dx, grads[0], recv_attn, recv_in = _layer_bwd(dx, saved[0], layers[0], tabs, ex_attn, True)
    recv = [recv_in, recv_attn[3], recv_attn[4], recv_attn[0], recv_attn[1], recv_attn[2]]
    tr = lambda a: jnp.swapaxes(a, 1, 2)
    sharded = {}
    sharded[id(w_in)] = [tr(o) for o in _sum_adamw([recv[0], recv[3]], tr(w_in), tr(m_w_in), tr(v_w_in), 0, 256)]
    sharded[id(w_branch_attn)] = [tr(o) for o in _sum_adamw(
        [recv[1], recv[4]], tr(w_branch_attn), tr(m_w_branch_attn), tr(v_w_branch_attn), 0, 512)]
    sharded[id(w_branch_ret)] = [tr(o) for o in _sum_adamw(
        [recv[1], recv[4]], tr(w_branch_ret), tr(m_w_branch_ret), tr(v_w_branch_ret), 512, 512)]
    sharded[id(w_out)] = _sum_adamw([recv[2], recv[5]], w_out, m_w_out, v_w_out, 0, 256)
    g_w_in, g_wba, g_wbr, g_wout = (sharded[id(w)][0] for w in (w_in, w_branch_attn, w_branch_ret, w_out))

    packed = jnp.zeros((8, 1024), F32)
    for l in range(DEPTH):
        gl = grads[l]
        packed = packed.at[l].set(gl["norm_g"][0])
        packed = packed.at[2, 512 * l:512 * (l + 1)].set(gl["gnw"][0])
        packed = packed.at[4, 128 * l:128 * l + 64].set(gl["qn"])
        packed = packed.at[4, 256 + 128 * l:256 + 128 * l + 64].set(gl["kn"])
        packed = packed.at[4, 512 + 128 * l:512 + 128 * l + 4].set(gl["lgf"])
        packed = packed.at[4, 768 + 128 * l:768 + 128 * l + 4].set(gl["lgb"])
    packed = packed.at[3].set(d_final_g[0])
    packed = packed.at[5, 0].set(loss_part[0, 0])
    red = _all_reduce_small(packed)
    loss = red[5, 0]
    g_norm_g = red[0:2]
    g_gnw = red[2].reshape(DEPTH, RET_WIDTH)
    g_final = red[3]
    g_qn = jnp.stack([red[4, 128 * l:128 * l + 64] for l in range(DEPTH)])
    g_kn = jnp.stack([red[4, 256 + 128 * l:256 + 128 * l + 64] for l in range(DEPTH)])
    g_lgf = jnp.stack([red[4, 512 + 128 * l:512 + 128 * l + 4] for l in range(DEPTH)])
    g_lgb = jnp.stack([red[4, 768 + 128 * l:768 + 128 * l + 4] for l in range(DEPTH)])
    g_df = g_lgf * jax.nn.sigmoid(-ret_decay_fwd)
    g_db = g_lgb * jax.nn.sigmoid(-ret_decay_bwd)

    grad_w = [g_norm_g, g_w_in, g_qn, g_kn, g_df, g_db, g_gnw, g_wba, g_wbr, g_wout, g_final]
    weights = [norm_g, w_in, attn_q_norm, attn_k_norm, ret_decay_fwd, ret_decay_bwd, ret_gn_w, w_branch_attn,
               w_branch_ret, w_out, final_norm_g]
    ms = [m_norm_g, m_w_in, m_attn_q_norm, m_attn_k_norm, m_ret_decay_fwd, m_ret_decay_bwd, m_ret_gn_w,
          m_w_branch_attn, m_w_branch_ret, m_w_out, m_final_norm_g]
    vs = [v_norm_g, v_w_in, v_attn_q_norm, v_attn_k_norm, v_ret_decay_fwd, v_ret_decay_bwd, v_ret_gn_w,
          v_w_branch_attn, v_w_branch_ret, v_w_out, v_final_norm_g]
    upd = [sharded[id(w)][1:] if id(w) in sharded else _adamw_nd(w, g, m, v)
           for w, g, m, v in zip(weights, grad_w, ms, vs)]
    return (loss, dx[None], *grad_w, *[u[0] for u in upd], *[u[1] for u in upd], *[u[2] for u in upd])
```

```python
import functools

import jax
import jax.numpy as jnp
from jax import lax
from jax.experimental import pallas as pl
from jax.experimental.pallas import tpu as pltpu

F32 = jnp.float32
BF16 = jnp.bfloat16
SDS = jax.ShapeDtypeStruct

D_MODEL = 1024
DEPTH = 2
GRID_W = 64
ATTN_Q_HEADS = 8
ATTN_KV_HEADS = 2
ATTN_HEAD_DIM = 64
ATTN_WIDTH = 512
ATTN_KV_WIDTH = 128
RET_HEADS = 4
RET_HEAD_DIM = 128
RET_WIDTH = 512
RET_CHUNK = 128
ATTN_KEY_CHUNK = 512
ATTN_BWD_KEY_CHUNK = 1024
QK_DOTS_PER_CHUNK = 4
EXP_LAG = 3
ROPE_THETA = 10000.0
EPS = 1e-6
D_IN = 5376
N_DEV = 8

ADAM_LR = 0.001
ADAM_B1 = 0.9
ADAM_B2 = 0.999
ADAM_EPS = 1e-08
ADAM_WD = 0.01
ADAM_STEP = 10

SEG = {
    "qa": (0, 512, 0),
    "ga": (768, 512, 512),
    "qr": (1280, 512, 1024),
    "kr": (1792, 512, 1536),
    "vr": (2304, 512, 2048),
    "gr": (2816, 512, 2560),
    "gm": (3328, 2048, 3072),
    "ka": (512, 128, 5120),
    "va": (640, 128, 5248),
}

VMEM_LIMIT = 60 * 1024 * 1024
NT = (((1,), (1,)), ((), ()))
TN = (((0,), (0,)), ((), ()))
MESH_ID = pl.DeviceIdType.MESH
ANY = pl.BlockSpec(memory_space=pl.ANY)


def _params(sem=None, vmem=VMEM_LIMIT):
    return pltpu.CompilerParams(dimension_semantics=sem, vmem_limit_bytes=vmem)


def _dot(a, b, dims=None):
    if dims is None:
        return jnp.dot(a, b, preferred_element_type=F32)
    return lax.dot_general(a, b, dims, preferred_element_type=F32)


def _sigmoid(x):
    return 1.0 / (1.0 + jnp.exp(-x))


def _swap_halves(x, q):
    n = x.shape[-1]
    axis = x.ndim - 1
    lane = lax.broadcasted_iota(jnp.int32, x.shape, axis)
    first = (lane % (2 * q)) < q
    return jnp.where(first, pltpu.roll(x, n - q, axis), pltpu.roll(x, q, axis))


def _rope(x, cos, sin_signed, q):
    return x * cos + _swap_halves(x, q) * sin_signed


def _rope_bwd(dy, cos, sin_signed, q):
    return dy * cos - _swap_halves(dy, q) * sin_signed


def _group_mean(v, ones_bd):
    hi = v.astype(BF16)
    r1 = v - hi.astype(F32)
    mid = r1.astype(BF16)
    lo = (r1 - mid.astype(F32)).astype(BF16)
    return _dot(hi, ones_bd) + _dot(mid, ones_bd) + _dot(lo, ones_bd)


def _rope_tables(t, head_dim):
    n_rows = t // GRID_W
    d_axis = head_dim // 2
    inv_freq = ROPE_THETA ** (-jnp.arange(0, d_axis, 2, dtype=F32) / d_axis)
    ar = jnp.arange(n_rows, dtype=F32)[:, None] * inv_freq
    ac = jnp.arange(GRID_W, dtype=F32)[:, None] * inv_freq
    by_row = lambda a: jnp.repeat(a, GRID_W, axis=0)
    by_col = lambda a: jnp.tile(a, (n_rows, 1))
    cr, sr, cc, sc = by_row(jnp.cos(ar)), by_row(jnp.sin(ar)), by_col(jnp.cos(ac)), by_col(jnp.sin(ac))
    return jnp.concatenate([cr, cr, cc, cc], axis=-1), jnp.concatenate([-sr, sr, -sc, sc], axis=-1)


def _me():
    return lax.axis_index("x"), lax.axis_index("y"), lax.axis_index("c")


def _flip(k):
    x, y, c = _me()
    px = 1 - x if k & 4 else x
    py = 1 - y if k & 2 else y
    pc = 1 - c if k & 1 else c
    return (px, py, pc), 4 * px + 2 * py + pc


class _Exchange:
    def __init__(self, kind, srcs):
        self.kind, self.srcs, self.n = kind, list(srcs), len(srcs)
        self.rows = [a.shape[0] if kind == "gather" else a.shape[0] // N_DEV for a in srcs]
        if kind == "gather":
            self.out_shape = [SDS((N_DEV * a.shape[0], a.shape[1]), a.dtype) for a in srcs]
        else:
            self.out_shape = [SDS((N_DEV, a.shape[0] // N_DEV, a.shape[1]), a.dtype) for a in srcs]
        self.scratch = [pltpu.SemaphoreType.DMA((self.n, N_DEV - 1)), pltpu.SemaphoreType.DMA((self.n, N_DEV - 1)),
                        pltpu.SemaphoreType.DMA((self.n,))]

    def _block(self, ref, a, idx):
        r = self.rows[a]
        return ref.at[pl.ds(pl.multiple_of(idx * r, 16), r), :]

    def _src(self, ins, a, idx):
        return ins[a] if self.kind == "gather" else self._block(ins[a], a, idx)

    def _dst(self, outs, a, idx):
        return self._block(outs[a], a, idx) if self.kind == "gather" else outs[a].at[idx]

    def _copies(self, ins, outs, sems):
        send_sems, recv_sems, local_sems = sems
        me, mine = _flip(0)
        local, sends, recvs = [], [], []
        for a in range(self.n):
            local.append(pltpu.make_async_copy(self._src(ins, a, mine), self._dst(outs, a, mine), local_sems.at[a]))
            for k in range(1, N_DEV):
                peer, theirs = _flip(k)
                sem = dict(send_sem=send_sems.at[a, k - 1], recv_sem=recv_sems.at[a, k - 1])
                sends.append(pltpu.make_async_remote_copy(
                    src_ref=self._src(ins, a, theirs), dst_ref=self._dst(outs, a, mine),
                    device_id=peer, device_id_type=MESH_ID, **sem))
                recvs.append(pltpu.make_async_remote_copy(
                    src_ref=self._dst(outs, a, theirs), dst_ref=self._dst(outs, a, theirs),
                    device_id=me, device_id_type=MESH_ID, **sem))
        return local, sends, recvs

    def start(self, ins, outs, sems):
        local, sends, _ = self._copies(ins, outs, sems)
        for cp in local + sends:
            cp.start()

    def wait(self, ins, outs, sems):
        local, sends, recvs = self._copies(ins, outs, sems)
        for cp in sends:
            cp.wait_send()
        for cp in recvs:
            cp.wait_recv()
        for cp in local:
            cp.wait()


def _with_exchange(body, n_in, n_out, n_scratch, ex, first, last):
    if ex is None:
        return body

    def wrapped(*refs):
        ins = refs[:n_in]
        ex_ins = refs[n_in:n_in + ex.n]
        outs = refs[n_in + ex.n:n_in + ex.n + n_out]
        ex_outs = refs[n_in + ex.n + n_out:n_in + 2 * ex.n + n_out]
        rest = refs[n_in + 2 * ex.n + n_out:]
        scratch, sems = rest[:n_scratch], rest[n_scratch:]

        @pl.when(first())
        def _():
            ex.start(ex_ins, ex_outs, sems)

        body(*ins, *outs, *scratch)

        @pl.when(last())
        def _():
            ex.wait(ex_ins, ex_outs, sems)

    return wrapped


def _ex_args(ex):
    if ex is None:
        return [], [], [], [], []
    return [ANY] * ex.n, [ANY] * ex.n, list(ex.out_shape), list(ex.scratch), list(ex.srcs)


def _in_proj(x, g, w_t):
    t, d = x.shape
    tm = min(256, t)

    def body(x_ref, g_ref, w_ref, z_ref, ht_ref):
        xv = x_ref[...]
        r = lax.rsqrt(jnp.mean(xv * xv, axis=-1, keepdims=True) + EPS)
        h = xv * r * g_ref[...]
        ht_ref[...] = h.T.astype(BF16)
        hb = h.astype(BF16)
        for nat, w, off in SEG.values():
            z_ref[:, off:off + w] = _dot(hb, w_ref[nat:nat + w, :], NT)

    return pl.pallas_call(
        body, name="in_proj", grid=(t // tm,),
        in_specs=[pl.BlockSpec((tm, d), lambda i: (i, 0)), pl.BlockSpec((1, d), lambda i: (0, 0)),
                  pl.BlockSpec((D_IN, d), lambda i: (0, 0))],
        out_specs=[pl.BlockSpec((tm, D_IN), lambda i: (i, 0)), pl.BlockSpec((d, tm), lambda i: (0, i))],
        out_shape=[SDS((t, D_IN), F32), SDS((d, t), BF16)],
        compiler_params=_params(("parallel",)),
    )(x, g, w_t)


def _attn_prep(z, qn, kn, cos, sin, ones_bd):
    t = z.shape[0]
    tm = min(ATTN_KEY_CHUNK, t)
    hd = ATTN_HEAD_DIM

    def body(zq_ref, zkv_ref, qn_ref, kn_ref, c_ref, s_ref, b_ref, q_out, qt_out, k_out, v_out, vt_out):
        bd = b_ref[...]
        c2, s2 = c_ref[...], s_ref[...]
        cq = jnp.concatenate([c2] * 4, axis=-1)
        sq = jnp.concatenate([s2] * 4, axis=-1)
        xq = zq_ref[...]
        yq = xq * lax.rsqrt(_group_mean(xq * xq, bd) + EPS) * qn_ref[...]
        yq = _rope(yq, cq, sq, hd // 4) * (hd ** -0.5)
        yqt = yq.T
        for h in range(ATTN_Q_HEADS):
            q_out[h] = yq[:, h * hd:(h + 1) * hd].astype(BF16)
            qt_out[h] = yqt[h * hd:(h + 1) * hd, :].astype(BF16)
        zkv = zkv_ref[...]
        xk, xv = zkv[:, :ATTN_KV_WIDTH], zkv[:, ATTN_KV_WIDTH:]
        yk = xk * lax.rsqrt(_group_mean(xk * xk, bd[:ATTN_KV_WIDTH, :ATTN_KV_WIDTH]) + EPS) * kn_ref[...]
        yk = _rope(yk, c2, s2, hd // 4)
        xvt = xv.T
        ones = jnp.ones((hd, tm), F32)
        for h in range(ATTN_KV_HEADS):
            k_out[h] = yk[:, h * hd:(h + 1) * hd].astype(BF16)
            v_out[h] = xv[:, h * hd:(h + 1) * hd].astype(BF16)
            vt_out[h, 0] = jnp.concatenate([xvt[h * hd:(h + 1) * hd, :], ones], axis=0).astype(BF16)

    kv_blk = SEG["ka"][2] // 256
    nk = t // tm
    return pl.pallas_call(
        body, name="attn_prep", grid=(nk,),
        in_specs=[pl.BlockSpec((tm, 512), lambda i: (i, 0)), pl.BlockSpec((tm, 256), lambda i: (i, kv_blk)),
                  pl.BlockSpec((1, 512), lambda i: (0, 0)), pl.BlockSpec((1, 128), lambda i: (0, 0)),
                  pl.BlockSpec((tm, 128), lambda i: (i, 0)), pl.BlockSpec((tm, 128), lambda i: (i, 0)),
                  pl.BlockSpec((512, 512), lambda i: (0, 0))],
        out_specs=[pl.BlockSpec((ATTN_Q_HEADS, tm, hd), lambda i: (0, i, 0)),
                   pl.BlockSpec((ATTN_Q_HEADS, hd, tm), lambda i: (0, 0, i)),
                   pl.BlockSpec((ATTN_KV_HEADS, tm, hd), lambda i: (0, i, 0)),
                   pl.BlockSpec((ATTN_KV_HEADS, tm, hd), lambda i: (0, i, 0)),
                   pl.BlockSpec((ATTN_KV_HEADS, 1, 2 * hd, tm), lambda i: (0, i, 0, 0))],
        out_shape=[SDS((ATTN_Q_HEADS, t, hd), BF16), SDS((ATTN_Q_HEADS, hd, t), BF16),
                   SDS((ATTN_KV_HEADS, t, hd), BF16), SDS((ATTN_KV_HEADS, t, hd), BF16),
                   SDS((ATTN_KV_HEADS, nk, 2 * hd, tm), BF16)],
        compiler_params=_params(("parallel",)),
    )(z, z, qn, kn, cos, sin, ones_bd)


def _attn_fwd(q, k, vt, ex=None):
    t = q.shape[1]
    tq = min(256, t)
    nk, tk = vt.shape[1], vt.shape[3]
    hd = ATTN_HEAD_DIM
    g = ATTN_Q_HEADS // ATTN_KV_HEADS

    def body(q_ref, k_ref, vt_ref, o_ref, lse_ref, s_scr):
        def pass_a(h, c, m8):
            part = tk // QK_DOTS_PER_CHUNK
            for lo in range(c * tk, (c + 1) * tk, part):
                st = _dot(k_ref[0, lo:lo + part, :], q_ref[h], NT)
                s_scr[h % 2, lo:lo + part, :] = st
                m8 = jnp.maximum(m8, jnp.max(st.reshape(part // 8, 8, tq), axis=0))
            return m8

        def pass_b(h, c, m, acc, after):
            e = jnp.exp(s_scr[h % 2, c * tk:(c + 1) * tk, :] - (m + after * 0.0)).astype(BF16)
            return acc + _dot(vt_ref[0, c], e)

        neg = jnp.full((8, tq), -jnp.inf, F32)
        m8 = neg
        for c in range(nk):
            m8 = pass_a(0, c, m8)
        outs = []
        for h in range(g):
            m = jnp.max(m8, axis=0, keepdims=True)
            acc = jnp.zeros((2 * hd, tq), F32)
            m8 = neg
            done = [m] * EXP_LAG
            for c in range(nk):
                if h + 1 < g:
                    m8 = pass_a(h + 1, c, m8)
                acc = pass_b(h, c, m, acc, done[-EXP_LAG])
                done.append(m8[0:1, :] if h + 1 < g else acc[hd:hd + 1, :])
            l = acc[hd:hd + 1, :]
            outs.append((acc[:hd, :] / l).T)
            lse_ref[h] = m + jnp.log(l)
        o_ref[...] = jnp.concatenate(outs, axis=-1)

    nq = t // tq
    first = lambda: jnp.logical_and(pl.program_id(0) == 0, pl.program_id(1) == 0)
    last = lambda: jnp.logical_and(pl.program_id(0) == ATTN_KV_HEADS - 1, pl.program_id(1) == nq - 1)
    xi, xo, xs, xscr, xargs = _ex_args(ex)
    return pl.pallas_call(
        _with_exchange(body, 3, 2, 1, ex, first, last), name="attn_fwd", grid=(ATTN_KV_HEADS, nq),
        in_specs=[pl.BlockSpec((g, tq, hd), lambda p, i: (p, i, 0)),
                  pl.BlockSpec((1, t, hd), lambda p, i: (p, 0, 0)),
                  pl.BlockSpec((1, nk, 2 * hd, tk), lambda p, i: (p, 0, 0, 0))] + xi,
        out_specs=[pl.BlockSpec((tq, g * hd), lambda p, i: (i, p)),
                   pl.BlockSpec((g, 1, tq), lambda p, i: (p, 0, i))] + xo,
        out_shape=[SDS((t, ATTN_WIDTH), F32), SDS((ATTN_Q_HEADS, 1, t), F32)] + xs,
        scratch_shapes=[pltpu.VMEM((2, t, tq), F32)] + xscr,
        compiler_params=_params(("arbitrary", "arbitrary")),
    )(q, k, vt, *xargs)


class _Dir:
    def __init__(self, lg, strict_future):
        c = RET_CHUNK
        ia = lax.broadcasted_iota(jnp.int32, (c, c), 0).astype(F32)
        ib = lax.broadcasted_iota(jnp.int32, (c, c), 1).astype(F32)
        col = lax.broadcasted_iota(jnp.int32, (c, 1), 0).astype(F32)
        row = lax.broadcasted_iota(jnp.int32, (1, c), 1).astype(F32)
        if strict_future:
            dist = ib - ia
            mask = dist > 0
            self.wq, self.wk, wk_row = c - col, col, row
        else:
            dist = ia - ib
            mask = dist >= 0
            self.wq, self.wk, wk_row = col + 1.0, c - 1.0 - col, c - 1.0 - row
        self.dist = jnp.maximum(dist, 0.0)
        self.d = jnp.where(mask, jnp.exp(self.dist * lg), 0.0)
        self.qd = jnp.exp(self.wq * lg)
        self.kd_col = jnp.exp(self.wk * lg)
        self.kd_row = jnp.exp(wk_row * lg)
        self.cd = jnp.exp(jnp.full((1, 1), float(c), F32) * lg)


def _ret_fwd(z, lgf, lgb, gnw, cos, sin):
    t = z.shape[0]
    c = RET_CHUNK
    nc = t // c
    hd = RET_HEAD_DIM
    unroll = 4 if nc % 4 == 0 else 1

    def body(lgf_ref, lgb_ref, q_ref, k_ref, v_ref, c_ref, s_ref, w_ref,
             qo_ref, ko_ref, vo_ref, orr_ref, on_ref, kt, uf, ub, sfa, sba):
        h = pl.program_id(0)
        fw = _Dir(lgf_ref[h], False)
        bw = _Dir(lgb_ref[h], True)
        cc, ss = c_ref[...], s_ref[...]
        qo_ref[...] = _rope(q_ref[...], cc, ss, hd // 4).astype(BF16)
        kr = _rope(k_ref[...], cc, ss, hd // 4) * (hd ** -0.5)
        ko_ref[...] = kr.astype(BF16)
        vo_ref[...] = v_ref[...].astype(BF16)
        for i in range(nc):
            kt[i] = kr[i * c:(i + 1) * c, :].T.astype(BF16)

        def rows(ci):
            return pl.ds(pl.multiple_of(ci * c, c), c)

        def kv_products(ci, carry):
            vv = vo_ref[rows(ci), :]
            ktf = kt[ci].astype(F32)
            uf[ci] = _dot((ktf * fw.kd_row).astype(BF16), vv)
            ub[ci] = _dot((ktf * bw.kd_row).astype(BF16), vv)
            return carry

        lax.fori_loop(0, nc, kv_products, 0, unroll=unroll)

        def scan(i, carry):
            sf, sb = carry
            j = nc - 1 - i
            sfa[i] = sf.astype(BF16)
            sba[j] = sb.astype(BF16)
            return sf * fw.cd + uf[i], sb * bw.cd + ub[j]

        zero = jnp.zeros((hd, hd), F32)
        lax.fori_loop(0, nc, scan, (zero, zero))
        gw = w_ref[...]

        def outputs(ci, carry):
            sl = rows(ci)
            qq, kk, vv = qo_ref[sl, :], ko_ref[sl, :], vo_ref[sl, :]
            a = _dot(qq, kk, NT)
            o = (_dot((a * fw.d).astype(BF16), vv) + _dot(qq, sfa[ci]) * fw.qd
                 + _dot((a * bw.d).astype(BF16), vv) + _dot(qq, sba[ci]) * bw.qd)
            orr_ref[sl, :] = o
            xc = o - jnp.mean(o, axis=-1, keepdims=True)
            var = jnp.mean(xc * xc, axis=-1, keepdims=True)
            on_ref[sl, :] = xc * lax.rsqrt(var + EPS) * gw
            return carry

        lax.fori_loop(0, nc, outputs, 0, unroll=unroll)

    smem = pl.BlockSpec(memory_space=pltpu.SMEM)
    col = lambda name: (lambda h: (0, SEG[name][2] // 128 + h))
    head = pl.BlockSpec((t, 128), lambda h: (0, h))
    full = pl.BlockSpec((t, 128), lambda h: (0, 0))
    return pl.pallas_call(
        body, name="ret_fwd", grid=(RET_HEADS,),
        in_specs=[smem, smem, pl.BlockSpec((t, 128), col("qr")), pl.BlockSpec((t, 128), col("kr")),
                  pl.BlockSpec((t, 128), col("vr")), full, full, pl.BlockSpec((1, 128), lambda h: (0, h))],
        out_specs=[head, head, head, head, head],
        out_shape=[SDS((t, RET_WIDTH), BF16)] * 3 + [SDS((t, RET_WIDTH), F32)] * 2,
        scratch_shapes=[pltpu.VMEM((nc, hd, c), BF16), pltpu.VMEM((nc, hd, hd), F32), pltpu.VMEM((nc, hd, hd), F32),
                        pltpu.VMEM((nc, hd, hd), BF16), pltpu.VMEM((nc, hd, hd), BF16)],
        compiler_params=_params(("parallel",)),
    )(lgf, lgb, z, z, z, cos, sin, gnw)


def _merge_fwd(x, z, oa, on, wb_t, wout):
    t, d = x.shape
    tm = min(256, t)

    def body(x_ref, ga_ref, gr_ref, gm0_ref, gm1_ref, oa_ref, on_ref, wb_ref, wo_ref, xn_ref, ya_ref, yb_ref):
        ga, gr = ga_ref[...], gr_ref[...]
        ua = ga * _sigmoid(ga) * oa_ref[...]
        ub = gr * _sigmoid(gr) * on_ref[...]
        ya = _dot(ua.astype(BF16), wb_ref[:, :512], NT)
        yb = _dot(ub.astype(BF16), wb_ref[:, 512:], NT)
        ya_ref[...] = ya
        yb_ref[...] = yb
        merged = _sigmoid(gm0_ref[...]) * ya + _sigmoid(gm1_ref[...]) * yb
        xn_ref[...] = x_ref[...] + _dot(merged.astype(BF16), wo_ref[...])

    row = lambda w, j: pl.BlockSpec((tm, w), lambda i: (i, j))
    const = lambda shape: pl.BlockSpec(shape, lambda i: (0, 0))
    return pl.pallas_call(
        body, name="merge_fwd", grid=(t // tm,),
        in_specs=[row(d, 0), row(512, SEG["ga"][2] // 512), row(512, SEG["gr"][2] // 512),
                  row(1024, SEG["gm"][2] // 1024), row(1024, SEG["gm"][2] // 1024 + 1),
                  row(512, 0), row(512, 0), const((d, 1024)), const((d, d))],
        out_specs=[row(d, 0), row(d, 0), row(d, 0)],
        out_shape=[SDS((t, d), F32)] * 3,
        compiler_params=_params(("parallel",)),
    )(x, z, z, z, z, oa, on, wb_t, wout)


def _final_loss(x, g, target):
    t, d = x.shape
    tm = min(512, t)
    n = t // tm

    def body(x_ref, g_ref, t_ref, dx_ref, dg_ref, loss_ref, acc_g, acc_l):
        i = pl.program_id(0)

        @pl.when(i == 0)
        def _():
            acc_g[...] = jnp.zeros_like(acc_g)
            acc_l[...] = jnp.zeros_like(acc_l)

        xv, gv = x_ref[...], g_ref[...]
        r = lax.rsqrt(jnp.mean(xv * xv, axis=-1, keepdims=True) + EPS)
        xh = xv * r
        err = xh * gv - t_ref[...]
        dy = err * (1.0 / d)
        gy = dy * gv
        dx_ref[...] = r * (gy - xh * jnp.mean(gy * xh, axis=-1, keepdims=True))
        acc_g[...] += jnp.sum((dy * xh).reshape(tm // 8, 8, d), axis=0)
        acc_l[...] += jnp.sum((err * err).reshape(tm // 8, 8, d), axis=0)

        @pl.when(i == n - 1)
        def _():
            dg_ref[...] = jnp.sum(acc_g[...], axis=0, keepdims=True)
            tot = jnp.sum(jnp.sum(acc_l[...], axis=0, keepdims=True), axis=1, keepdims=True)
            loss_ref[...] = jnp.broadcast_to(tot * (0.5 / d), (1, 128))

    return pl.pallas_call(
        body, name="final_loss", grid=(n,),
        in_specs=[pl.BlockSpec((tm, d), lambda i: (i, 0)), pl.BlockSpec((1, d), lambda i: (0, 0)),
                  pl.BlockSpec((tm, d), lambda i: (i, 0))],
        out_specs=[pl.BlockSpec((tm, d), lambda i: (i, 0)), pl.BlockSpec((1, d), lambda i: (0, 0)),
                   pl.BlockSpec((1, 128), lambda i: (0, 0))],
        out_shape=[SDS((t, d), F32), SDS((1, d), F32), SDS((1, 128), F32)],
        scratch_shapes=[pltpu.VMEM((8, d), F32), pltpu.VMEM((8, d), F32)],
        compiler_params=_params(("arbitrary",)),
    )(x, g, target)


def _merge_bwd(dxo, z, oa, on, ya, yb, wb_t, wout):
    t, d = dxo.shape
    tm = min(256, t)
    n = t // tm

    def body(dx_ref, ga_ref, gr_ref, gm0_ref, gm1_ref, oa_ref, on_ref, ya_ref, yb_ref, wb_ref, wo_ref,
             doa_ref, don_ref, dz_ref, dwo_ref, dwb_ref, acc_o, acc_b):
        i = pl.program_id(0)

        @pl.when(i == 0)
        def _():
            acc_o[...] = jnp.zeros_like(acc_o)
            acc_b[...] = jnp.zeros_like(acc_b)

        dxb = dx_ref[...].astype(BF16)
        ya, yb = ya_ref[...], yb_ref[...]
        g0, g1 = _sigmoid(gm0_ref[...]), _sigmoid(gm1_ref[...])
        mb = (g0 * ya + g1 * yb).astype(BF16)
        dm = _dot(dxb, wo_ref[...], NT)
        dya = (dm * g0).astype(BF16)
        dyb = (dm * g1).astype(BF16)
        dz_ref[:, 1024:2048] = (dm * ya * g0 * (1.0 - g0)).astype(BF16)
        dz_ref[:, 2048:3072] = (dm * yb * g1 * (1.0 - g1)).astype(BF16)

        def branch(g_ref, o_ref, dy, w, do_ref, lo):
            gv, ov = g_ref[...], o_ref[...]
            sg = _sigmoid(gv)
            silu = gv * sg
            du = _dot(dy, w)
            do_ref[...] = du * silu
            dz_ref[:, lo:lo + 512] = (du * ov * (sg * (1.0 + gv * (1.0 - sg)))).astype(BF16)
            acc_b[:, lo:lo + 512] += _dot(dy, (silu * ov).astype(BF16), TN)

        branch(ga_ref, oa_ref, dya, wb_ref[:, :512], doa_ref, 0)
        branch(gr_ref, on_ref, dyb, wb_ref[:, 512:], don_ref, 512)
        acc_o[...] += _dot(mb, dxb, TN)

        @pl.when(i == n - 1)
        def _():
            dwo_ref[...] = acc_o[...].astype(BF16)
            dwb_ref[...] = acc_b[...].astype(BF16)

    row = lambda w, j: pl.BlockSpec((tm, w), lambda i: (i, j))
    const = lambda shape: pl.BlockSpec(shape, lambda i: (0, 0))
    return pl.pallas_call(
        body, name="merge_bwd", grid=(n,),
        in_specs=[row(d, 0), row(512, SEG["ga"][2] // 512), row(512, SEG["gr"][2] // 512),
                  row(1024, SEG["gm"][2] // 1024), row(1024, SEG["gm"][2] // 1024 + 1),
                  row(512, 0), row(512, 0), row(d, 0), row(d, 0), const((d, 1024)), const((d, d))],
        out_specs=[row(512, 0), row(512, 0), row(3072, 0), const((d, d)), const((d, 1024))],
        out_shape=[SDS((t, 512), F32), SDS((t, 512), F32), SDS((t, 3072), BF16), SDS((d, d), BF16),
                   SDS((d, 1024), BF16)],
        scratch_shapes=[pltpu.VMEM((d, d), F32), pltpu.VMEM((d, 1024), F32)],
        compiler_params=_params(("arbitrary",)),
    )(dxo, z, z, z, z, oa, on, ya, yb, wb_t, wout)


def _ret_bwd(qrot, krot, vb, orr, don, gnw, lgf, lgb):
    t = qrot.shape[0]
    c = RET_CHUNK
    nc = t // c
    hd = RET_HEAD_DIM
    unroll = 2 if nc % 2 == 0 else 1

    def body(lgf_ref, lgb_ref, q_ref, k_ref, v_ref, o_ref, dn_ref, w_ref,
             dq_ref, dk_ref, dv_ref, dw_ref, dlf_ref, dlb_ref, qt, kt, dob, uf, ub, wf, wb, sfa, sba, gfa, gba):
        h = pl.program_id(0)
        fw = _Dir(lgf_ref[h], False)
        bw = _Dir(lgb_ref[h], True)
        fw.dt, bw.dt = fw.d.T, bw.d.T

        o = o_ref[...]
        xc = o - jnp.mean(o, axis=-1, keepdims=True)
        r = lax.rsqrt(jnp.mean(xc * xc, axis=-1, keepdims=True) + EPS)
        xh = xc * r
        dn = dn_ref[...]
        gy = dn * w_ref[...]
        d_o = r * (gy - jnp.mean(gy, axis=-1, keepdims=True) - xh * jnp.mean(gy * xh, axis=-1, keepdims=True))
        dw_ref[...] = jnp.sum(dn * xh, axis=0, keepdims=True)
        dob[...] = d_o.astype(BF16)
        for i in range(nc):
            qt[i] = q_ref[i * c:(i + 1) * c, :].astype(F32).T.astype(BF16)
            kt[i] = k_ref[i * c:(i + 1) * c, :].astype(F32).T.astype(BF16)

        def rows(ci):
            return pl.ds(pl.multiple_of(ci * c, c), c)

        def products(ci, carry):
            sl = rows(ci)
            vv, do32 = v_ref[sl, :], dob[sl, :].astype(F32)
            ktf = kt[ci].astype(F32)
            uf[ci] = _dot((ktf * fw.kd_row).astype(BF16), vv)
            ub[ci] = _dot((ktf * bw.kd_row).astype(BF16), vv)
            wf[ci] = _dot(qt[ci], (do32 * fw.qd).astype(BF16))
            wb[ci] = _dot(qt[ci], (do32 * bw.qd).astype(BF16))
            return carry

        lax.fori_loop(0, nc, products, 0, unroll=unroll)

        def scan(i, carry):
            sf, sb, gf, gb = carry
            j = nc - 1 - i
            sfa[i] = sf.astype(BF16)
            sba[j] = sb.astype(BF16)
            gfa[j] = gf.astype(BF16)
            gba[i] = gb.astype(BF16)
            return sf * fw.cd + uf[i], sb * bw.cd + ub[j], gf * fw.cd + wf[j], gb * bw.cd + wb[i]

        zero = jnp.zeros((hd, hd), F32)
        lax.fori_loop(0, nc, scan, (zero, zero, zero, zero))

        def one_dir(p, s_all, g_all, ci, qq, kk, vv, do, a, bm, at, bt):
            sb, gb = s_all[ci], g_all[ci]
            doq = (do.astype(F32) * p.qd).astype(BF16)
            dqc = _dot(doq, sb, NT)
            dq = _dot((bm * p.d).astype(BF16), kk) + dqc
            kkd = (kk.astype(F32) * p.kd_col).astype(BF16)
            dv = _dot((at * p.dt).astype(BF16), do) + _dot(kkd, gb)
            dk2 = _dot(vv, gb, NT) * p.kd_col
            dk = _dot((bt * p.dt).astype(BF16), qq) + dk2
            terms = (p.dist * p.d * a * bm + p.wq * qq.astype(F32) * dqc + p.wk * kk.astype(F32) * dk2
                     + (float(c) * p.cd) * gb.astype(F32) * sb.astype(F32))
            return dq, dk, dv, terms

        def chunk(ci, carry):
            af, ab = carry
            sl = rows(ci)
            qq, kk, vv, do = q_ref[sl, :], k_ref[sl, :], v_ref[sl, :], dob[sl, :]
            a, bm = _dot(qq, kk, NT), _dot(do, vv, NT)
            at, bt = _dot(kk, qq, NT), _dot(vv, do, NT)
            dqf, dkf, dvf, tf = one_dir(fw, sfa, gfa, ci, qq, kk, vv, do, a, bm, at, bt)
            dqb, dkb, dvb, tb = one_dir(bw, sba, gba, ci, qq, kk, vv, do, a, bm, at, bt)
            dq_ref[sl, :] = dqf + dqb
            dk_ref[sl, :] = dkf + dkb
            dv_ref[sl, :] = dvf + dvb
            return af + tf, ab + tb

        af, ab = lax.fori_loop(0, nc, chunk, (zero, zero), unroll=unroll)
        tot = lambda m: jnp.sum(jnp.sum(m, axis=0, keepdims=True), axis=1, keepdims=True)
        dlf_ref[...] = jnp.broadcast_to(tot(af).reshape(1, 1, 1), (1, 8, 128))
        dlb_ref[...] = jnp.broadcast_to(tot(ab).reshape(1, 1, 1), (1, 8, 128))

    smem = pl.BlockSpec(memory_space=pltpu.SMEM)
    head = pl.BlockSpec((t, 128), lambda h: (0, h))
    vec = pl.BlockSpec((1, 128), lambda h: (0, h))
    scal = pl.BlockSpec((1, 8, 128), lambda h: (h, 0, 0))
    mats = lambda dt: pltpu.VMEM((nc, hd, hd), dt)
    return pl.pallas_call(
        body, name="ret_bwd", grid=(RET_HEADS,),
        in_specs=[smem, smem, head, head, head, head, head, vec],
        out_specs=[head, head, head, vec, scal, scal],
        out_shape=[SDS((t, RET_WIDTH), F32)] * 3 + [SDS((1, RET_WIDTH), F32), SDS((RET_HEADS, 8, 128), F32),
                                                   SDS((RET_HEADS, 8, 128), F32)],
        scratch_shapes=[pltpu.VMEM((nc, hd, c), BF16), pltpu.VMEM((nc, hd, c), BF16), pltpu.VMEM((t, hd), BF16),
                        mats(F32), mats(F32), mats(F32), mats(F32), mats(BF16), mats(BF16), mats(BF16), mats(BF16)],
        compiler_params=_params(("parallel",)),
    )(lgf, lgb, qrot, krot, vb, orr, don, gnw)


def _ret_post_bwd(dq, dk, dv, cos, sin):
    t = dq.shape[0]
    tm = min(512, t)
    hd = RET_HEAD_DIM

    def body(dq_ref, dk_ref, dv_ref, c_ref, s_ref, oq_ref, ok_ref, ov_ref):
        cc = jnp.concatenate([c_ref[...]] * 4, axis=-1)
        ss = jnp.concatenate([s_ref[...]] * 4, axis=-1)
        oq_ref[...] = _rope_bwd(dq_ref[...], cc, ss, hd // 4).astype(BF16)
        ok_ref[...] = (_rope_bwd(dk_ref[...], cc, ss, hd // 4) * (hd ** -0.5)).astype(BF16)
        ov_ref[...] = dv_ref[...].astype(BF16)

    blk = pl.BlockSpec((tm, 512), lambda i: (i, 0))
    tab = pl.BlockSpec((tm, 128), lambda i: (i, 0))
    return pl.pallas_call(
        body, name="ret_post_bwd", grid=(t // tm,),
        in_specs=[blk, blk, blk, tab, tab], out_specs=[blk, blk, blk],
        out_shape=[SDS((t, 512), BF16)] * 3,
        compiler_params=_params(("parallel",)),
    )(dq, dk, dv, cos, sin)


def _attn_bwd(q, qt, k, v, doa, oa, lse, ex=None):
    t = q.shape[1]
    tq = min(256, t)
    nq = t // tq
    tk = min(ATTN_BWD_KEY_CHUNK, t)
    nk = t // tk
    hd = ATTN_HEAD_DIM
    scale = hd ** -0.5

    def body(q_ref, qt_ref, k_ref, v_ref, do_ref, o_ref, lse_ref, dq_ref, dkt_ref, dvt_ref):
        p, i = pl.program_id(0), pl.program_id(1)

        @pl.when(jnp.logical_and(p % 2 == 0, i == 0))
        def _():
            dkt_ref[...] = jnp.zeros_like(dkt_ref)
            dvt_ref[...] = jnp.zeros_like(dvt_ref)

        dov, ov = do_ref[...], o_ref[...]
        dovt = dov.T
        lanes = lambda col: jnp.concatenate([col] * (tk // 128), axis=1)
        outs = []
        for j in range(2):
            qq, qqt = q_ref[j], qt_ref[j]
            do32 = dov[:, j * hd:(j + 1) * hd]
            do, dot_ = do32.astype(BF16), dovt[j * hd:(j + 1) * hd, :].astype(BF16)
            dd = lanes(jnp.broadcast_to(jnp.sum(do32 * ov[:, j * hd:(j + 1) * hd], axis=1, keepdims=True), (tq, 128)))
            lse_j = lanes(jnp.broadcast_to(lse_ref[j], (128, tq)).T)
            dq = jnp.zeros((tq, hd), F32)
            for c in range(nk):
                sl = slice(c * tk, (c + 1) * tk)
                kc, vc = k_ref[0, sl, :], v_ref[0, sl, :]
                pr = jnp.exp(_dot(qq, kc, NT) - lse_j)
                ds = (pr * (_dot(do, vc, NT) - dd)).astype(BF16)
                dvt_ref[0, :, sl] += _dot(dot_, pr.astype(BF16))
                dkt_ref[0, :, sl] += _dot(qqt, ds)
                dq = dq + _dot(ds, kc)
            outs.append(dq * scale)
        dq_ref[...] = jnp.concatenate(outs, axis=-1)

    kv = pl.BlockSpec((1, t, hd), lambda p, i: (p // 2, 0, 0))
    kvt = pl.BlockSpec((1, hd, t), lambda p, i: (p // 2, 0, 0))
    pair = pl.BlockSpec((tq, 128), lambda p, i: (i, p))
    first = lambda: jnp.logical_and(pl.program_id(0) == 0, pl.program_id(1) == 0)
    last = lambda: jnp.logical_and(pl.program_id(0) == 3, pl.program_id(1) == nq - 1)
    xi, xo, xs, xscr, xargs = _ex_args(ex)
    return pl.pallas_call(
        _with_exchange(body, 7, 3, 0, ex, first, last), name="attn_bwd", grid=(4, nq),
        in_specs=[pl.BlockSpec((2, tq, hd), lambda p, i: (p, i, 0)), pl.BlockSpec((2, hd, tq), lambda p, i: (p, 0, i)),
                  kv, kv, pair, pair, pl.BlockSpec((2, 1, tq), lambda p, i: (p, 0, i))] + xi,
        out_specs=[pair, kvt, kvt] + xo,
        out_shape=[SDS((t, ATTN_WIDTH), F32), SDS((ATTN_KV_HEADS, hd, t), F32),
                   SDS((ATTN_KV_HEADS, hd, t), F32)] + xs,
        scratch_shapes=xscr,
        compiler_params=_params(("arbitrary", "arbitrary")),
    )(q, qt, k, v, doa, oa, lse, *xargs)


def _attn_post_bwd(dq, dk, dv, z, qn, kn, cos, sin, ones_bd):
    t = z.shape[0]
    tm = min(512, t)
    n = t // tm
    hd = ATTN_HEAD_DIM

    def body(dq_ref, dk_ref, dv_ref, zq_ref, zkv_ref, qn_ref, kn_ref, c_ref, s_ref, b_ref,
             dz_ref, dqn_ref, dkn_ref, acc_q, acc_k):
        i = pl.program_id(0)

        @pl.when(i == 0)
        def _():
            acc_q[...] = jnp.zeros_like(acc_q)
            acc_k[...] = jnp.zeros_like(acc_k)

        bd = b_ref[...]
        c2, s2 = c_ref[...], s_ref[...]

        def norm_bwd(dy, x, w, ones, cos_t, sin_t, acc):
            dyr = _rope_bwd(dy, cos_t, sin_t, hd // 4)
            r = lax.rsqrt(_group_mean(x * x, ones) + EPS)
            xh = x * r
            gy = dyr * w
            acc[...] += jnp.sum((dyr * xh).reshape(tm // 8, 8, x.shape[-1]), axis=0)
            return r * (gy - xh * _group_mean(gy * xh, ones))

        cq = jnp.concatenate([c2] * 4, axis=-1)
        sq = jnp.concatenate([s2] * 4, axis=-1)
        dz_ref[:, :512] = norm_bwd(dq_ref[...], zq_ref[...], qn_ref[...], bd, cq, sq, acc_q).astype(BF16)
        zkv = zkv_ref[...]
        dkk = jnp.concatenate([dk_ref[0], dk_ref[1]], axis=0).T
        dz_ref[:, 512:640] = norm_bwd(dkk, zkv[:, :128], kn_ref[...], bd[:128, :128], c2, s2, acc_k).astype(BF16)
        dz_ref[:, 640:768] = jnp.concatenate([dv_ref[0], dv_ref[1]], axis=0).T.astype(BF16)

        @pl.when(i == n - 1)
        def _():
            dqn_ref[...] = jnp.sum(acc_q[...], axis=0, keepdims=True)
            dkn_ref[...] = jnp.sum(acc_k[...], axis=0, keepdims=True)

    kv_blk = SEG["ka"][2] // 256
    kvs = pl.BlockSpec((ATTN_KV_HEADS, hd, tm), lambda i: (0, 0, i))
    const = lambda shape: pl.BlockSpec(shape, lambda i: (0, 0))
    return pl.pallas_call(
        body, name="attn_post_bwd", grid=(n,),
        in_specs=[pl.BlockSpec((tm, 512), lambda i: (i, 0)), kvs, kvs,
                  pl.BlockSpec((tm, 512), lambda i: (i, 0)), pl.BlockSpec((tm, 256), lambda i: (i, kv_blk)),
                  const((1, 512)), const((1, 128)),
                  pl.BlockSpec((tm, 128), lambda i: (i, 0)), pl.BlockSpec((tm, 128), lambda i: (i, 0)),
                  const((512, 512))],
        out_specs=[pl.BlockSpec((tm, 768), lambda i: (i, 0)), const((1, 512)), const((1, 128))],
        out_shape=[SDS((t, 768), BF16), SDS((1, 512), F32), SDS((1, 128), F32)],
        scratch_shapes=[pltpu.VMEM((8, 512), F32), pltpu.VMEM((8, 128), F32)],
        compiler_params=_params(("arbitrary",)),
    )(dq, dk, dv, z, z, qn, kn, cos, sin, ones_bd)


def _in_bwd(dxo, x, g, w_t, dz_a, dz_m, dqr, dkr, dvr, after=None):
    t, d = x.shape
    tm = min(256, t)
    n = t // tm
    parts = [(0, 0, 768, 0), (1, 0, 512, SEG["ga"][0]), (2, 0, 512, SEG["qr"][0]), (3, 0, 512, SEG["kr"][0]),
             (4, 0, 512, SEG["vr"][0]), (1, 512, 2560, SEG["gr"][0])]

    def body(dx_ref, x_ref, g_ref, w_ref, a_ref, m_ref, q_ref, k_ref, v_ref, o_ref, dg_ref, acc):
        i = pl.program_id(0)

        @pl.when(i == 0)
        def _():
            acc[...] = jnp.zeros_like(acc)

        pieces = [a_ref, m_ref, q_ref, k_ref, v_ref]
        dh = jnp.zeros((tm, d), F32)
        for pi, lo, w, row in parts:
            dh = dh + _dot(pieces[pi][:, lo:lo + w], w_ref[row:row + w, :])
        xv = x_ref[...]
        r = lax.rsqrt(jnp.mean(xv * xv, axis=-1, keepdims=True) + EPS)
        xh = xv * r
        gy = dh * g_ref[...]
        o_ref[...] = dx_ref[...] + r * (gy - xh * jnp.mean(gy * xh, axis=-1, keepdims=True))
        acc[...] += jnp.sum((dh * xh).reshape(tm // 8, 8, d), axis=0)

        @pl.when(i == n - 1)
        def _():
            dg_ref[...] = jnp.sum(acc[...], axis=0, keepdims=True)

    row = lambda w: pl.BlockSpec((tm, w), lambda i: (i, 0))
    const = lambda shape: pl.BlockSpec(shape, lambda i: (0, 0))
    extra = [] if after is None else [after]
    return pl.pallas_call(
        (lambda *refs: body(*refs[:9], *refs[9 + len(extra):])), name="in_bwd", grid=(n,),
        in_specs=[row(d), row(d), const((1, d)), const((D_IN, d)), row(768), row(3072), row(512), row(512),
                  row(512)] + [const(a.shape) for a in extra],
        out_specs=[row(d), const((1, d))],
        out_shape=[SDS((t, d), F32), SDS((1, d), F32)],
        scratch_shapes=[pltpu.VMEM((8, d), F32)],
        compiler_params=_params(("arbitrary",)),
    )(dxo, x, g, w_t, dz_a, dz_m, dqr, dkr, dvr, *extra)


def _dw_in(h_t, dz_a, dz_m, dqr, dkr, dvr):
    d, t = h_t.shape
    tn = 256
    parts = [(0, 0, 0, 3), (1, 0, SEG["ga"][0] // tn, 2), (2, 0, SEG["qr"][0] // tn, 2),
             (3, 0, SEG["kr"][0] // tn, 2), (4, 0, SEG["vr"][0] // tn, 2), (1, 2, SEG["gr"][0] // tn, 10)]
    pieces = [dz_a, dz_m, dqr, dkr, dvr]

    def col_block(pi):
        mine = [(c0, r0, n) for q, c0, r0, n in parts if q == pi]

        def index(j):
            c0, r0, n = mine[0]
            blk = c0 + jnp.clip(j - r0, 0, n - 1)
            for c0, r0, n in mine[1:]:
                blk = jnp.where(j >= r0, c0 + jnp.clip(j - r0, 0, n - 1), blk)
            return 0, blk

        return index

    def body(h_ref, *refs):
        o_ref = refs[-1]
        j = pl.program_id(0)
        for pi, _, r0, n in parts:
            @pl.when(jnp.logical_and(j >= r0, j < r0 + n))
            def _(p_ref=refs[pi]):
                o_ref[...] = _dot(h_ref[...], p_ref[...]).T.astype(BF16)

    return pl.pallas_call(
        body, name="dw_in", grid=(D_IN // tn,),
        in_specs=[pl.BlockSpec((d, t), lambda j: (0, 0))] + [pl.BlockSpec((t, tn), col_block(pi)) for pi in range(5)],
        out_specs=pl.BlockSpec((tn, d), lambda j: (j, 0)),
        out_shape=SDS((D_IN, d), BF16),
        compiler_params=_params(("arbitrary",)),
    )(h_t, *pieces)


def _adamw_math(w, g, m, v):
    mn = ADAM_B1 * m + (1.0 - ADAM_B1) * g
    vn = ADAM_B2 * v + (1.0 - ADAM_B2) * (g * g)
    m_hat = mn / (1.0 - ADAM_B1 ** ADAM_STEP)
    v_hat = vn / (1.0 - ADAM_B2 ** ADAM_STEP)
    return -ADAM_LR * (m_hat / (jnp.sqrt(v_hat) + ADAM_EPS) + ADAM_WD * w), mn, vn


def _sum_adamw(recvs, w, m, v, lane0, tn, layer0=0, prev=None):
    _, r, c = w.shape
    j0 = lane0 // tn
    n = len(recvs)

    def body(*refs):
        w_ref, m_ref, v_ref = refs[n:n + 3]
        g_ref, d_ref, mo_ref, vo_ref = refs[-4:]

        def run(r_ref):
            g = r_ref[0].astype(F32)
            for s in range(1, N_DEV):
                g = g + r_ref[s].astype(F32)
            g_ref[0] = g
            d_ref[0], mo_ref[0], vo_ref[0] = _adamw_math(w_ref[0], g, m_ref[0], v_ref[0])

        for i in range(n):
            pl.when(pl.program_id(0) == i)(functools.partial(run, refs[i]))

    slots = pl.BlockSpec((N_DEV, r, tn), lambda i, j: (0, 0, j0 + j))
    blk = pl.BlockSpec((1, r, tn), lambda i, j: (layer0 + i, 0, j))
    before = [] if prev is None else list(prev)
    return pl.pallas_call(
        body, name="sum_adamw", grid=(n, c // tn),
        in_specs=[slots] * n + [blk] * 3 + [ANY] * len(before), out_specs=[blk] * 4,
        out_shape=[SDS(w.shape, F32)] * 4,
        input_output_aliases={n + 3 + k: k for k in range(len(before))},
        compiler_params=_params(("parallel", "parallel")),
    )(*recvs, w, m, v, *before)


def _adamw(w, g, m, v):
    rows, cols = w.shape
    tr = 256 if rows % 256 == 0 else rows

    def body(w_ref, g_ref, m_ref, v_ref, d_ref, mo_ref, vo_ref):
        d_ref[...], mo_ref[...], vo_ref[...] = _adamw_math(w_ref[...], g_ref[...], m_ref[...], v_ref[...])

    blk = pl.BlockSpec((tr, cols), lambda i: (i, 0))
    return pl.pallas_call(
        body, name="adamw", grid=(rows // tr,),
        in_specs=[blk] * 4, out_specs=[blk] * 3, out_shape=[SDS((rows, cols), F32)] * 3,
        compiler_params=_params(("parallel",)),
    )(w, g, m, v)


def _all_gather(shards):
    na = len(shards)
    chips = (4, 2, 6)

    def body(*refs):
        ins, outs = refs[:na], refs[na:2 * na]
        send_sems, recv_sems, local_sems = refs[2 * na:]
        _, mine = _flip(0)

        def rows(a, idx):
            r = shards[a].shape[0]
            return outs[a].at[pl.ds(pl.multiple_of(idx * r, 16), r), :]

        def copy(a, slot, block_idx, to, src=None):
            return pltpu.make_async_remote_copy(
                src_ref=rows(a, block_idx) if src is None else src, dst_ref=rows(a, block_idx),
                send_sem=send_sems.at[a, slot], recv_sem=recv_sems.at[a, slot],
                device_id=to, device_id_type=MESH_ID)

        sibling, sibling_idx = _flip(1)
        local, started = [], []
        for a in range(na):
            cp = pltpu.make_async_copy(ins[a], rows(a, mine), local_sems.at[a])
            cp.start()
            local.append(cp)
            first = [copy(a, 0, mine, sibling, src=ins[a])]
            first += [copy(a, 1 + j, mine, _flip(k)[0], src=ins[a]) for j, k in enumerate(chips)]
            for cp in first:
                cp.start()
            started += first
        for a in range(na):
            for j, k in enumerate(chips):
                _, theirs = _flip(k)
                copy(a, 1 + j, theirs, _flip(0)[0]).wait_recv()
                fwd = copy(a, 4 + j, theirs, sibling)
                fwd.start()
                started.append(fwd)
        for a in range(na):
            copy(a, 0, sibling_idx, _flip(0)[0]).wait_recv()
            for j, k in enumerate(chips):
                _, theirs = _flip(k | 1)
                copy(a, 4 + j, theirs, _flip(0)[0]).wait_recv()
        for cp in started:
            cp.wait_send()
        for cp in local:
            cp.wait()

    return pl.pallas_call(
        body, name="all_gather_weights",
        in_specs=[ANY] * na, out_specs=[ANY] * na,
        out_shape=[SDS((N_DEV * s.shape[0], s.shape[1]), s.dtype) for s in shards],
        scratch_shapes=[pltpu.SemaphoreType.DMA((na, 7)), pltpu.SemaphoreType.DMA((na, 7)),
                        pltpu.SemaphoreType.DMA((na,))],
        compiler_params=pltpu.CompilerParams(has_side_effects=True),
    )(*shards)


def _scatter_blocks_of(g_ref, rows, idx):
    return g_ref.at[pl.ds(pl.multiple_of(idx * rows, 16), rows), :]


def _scatter_start(g):
    rows = g.shape[0] // N_DEV
    land_shape = (N_DEV, rows, g.shape[1])

    def body(g_ref, land_ref, send_sems, recv_sems, g_thru, land_thru, token):
        _, mine = _flip(0)
        for k in range(1, N_DEV):
            peer, theirs = _flip(k)
            pltpu.make_async_remote_copy(
                src_ref=_scatter_blocks_of(g_ref, rows, theirs), dst_ref=land_ref.at[mine],
                send_sem=send_sems.at[k - 1], recv_sem=recv_sems.at[k - 1],
                device_id=peer, device_id_type=MESH_ID).start()
        token[...] = jnp.zeros_like(token)

    hbm, sem = pl.BlockSpec(memory_space=pltpu.HBM), pl.BlockSpec(memory_space=pltpu.SEMAPHORE)
    return pl.pallas_call(
        body, name="scatter_start",
        out_shape=(pltpu.SemaphoreType.DMA((N_DEV - 1,)), pltpu.SemaphoreType.DMA((N_DEV - 1,)),
                   pltpu.HBM(g.shape, g.dtype), pltpu.HBM(land_shape, g.dtype), SDS((8, 128), F32)),
        in_specs=(hbm, hbm), out_specs=(sem, sem, hbm, hbm, pl.BlockSpec(memory_space=pltpu.VMEM)),
        input_output_aliases={0: 2, 1: 3},
        compiler_params=pltpu.CompilerParams(has_side_effects=pltpu.SideEffectType.DATAFLOW_SIDE_EFFECTING),
    )(pltpu.with_memory_space_constraint(g, pltpu.HBM),
      pltpu.with_memory_space_constraint(lax.empty(land_shape, g.dtype), pltpu.HBM))


def _scatter_wait(send_sems, recv_sems, g_thru, land_thru, after):
    rows = g_thru.shape[0] // N_DEV

    def body(g_ref, land_ref, send_sems, recv_sems, *rest):
        me, _ = _flip(0)
        for k in range(1, N_DEV):
            _, theirs = _flip(k)
            copy = pltpu.make_async_remote_copy(
                src_ref=_scatter_blocks_of(g_ref, rows, theirs), dst_ref=land_ref.at[theirs],
                send_sem=send_sems.at[k - 1], recv_sem=recv_sems.at[k - 1],
                device_id=me, device_id_type=MESH_ID)
            copy.wait_send()
            copy.wait_recv()

    hbm, sem = pl.BlockSpec(memory_space=pltpu.HBM), pl.BlockSpec(memory_space=pltpu.SEMAPHORE)
    return pl.pallas_call(
        body, name="scatter_wait",
        out_shape=(pltpu.HBM(g_thru.shape, g_thru.dtype), pltpu.HBM(land_thru.shape, land_thru.dtype)),
        in_specs=(hbm, hbm, sem, sem) + (ANY,) * len(after), out_specs=(hbm, hbm), input_output_aliases={0: 0, 1: 1},
        compiler_params=pltpu.CompilerParams(has_side_effects=pltpu.SideEffectType.DATAFLOW_SIDE_EFFECTING),
    )(g_thru, land_thru, send_sems, recv_sems, *after)


def _place_own_block(g, land):
    rows = g.shape[0] // N_DEV

    def body(g_ref, land_ref, out_ref, sem):
        _, mine = _flip(0)
        cp = pltpu.make_async_copy(_scatter_blocks_of(g_ref, rows, mine), out_ref.at[mine], sem)
        cp.start()
        cp.wait()

    return pl.pallas_call(
        body, name="place_own_block", out_shape=SDS(land.shape, land.dtype),
        in_specs=[ANY, ANY], out_specs=ANY, input_output_aliases={1: 0},
        scratch_shapes=[pltpu.SemaphoreType.DMA],
        compiler_params=pltpu.CompilerParams(has_side_effects=True),
    )(g, land)


def _all_reduce_small(packed):
    shape = packed.shape

    def body(p_ref, o_ref, slots, send_sems, recv_sems):
        me, mine = _flip(0)
        slots[mine] = p_ref[...]
        sends = []
        for k in range(1, N_DEV):
            peer, _ = _flip(k)
            cp = pltpu.make_async_remote_copy(
                src_ref=p_ref, dst_ref=slots.at[mine], send_sem=send_sems.at[k - 1], recv_sem=recv_sems.at[k - 1],
                device_id=peer, device_id_type=MESH_ID)
            cp.start()
            sends.append(cp)
        for k in range(1, N_DEV):
            _, theirs = _flip(k)
            pltpu.make_async_remote_copy(
                src_ref=p_ref, dst_ref=slots.at[theirs], send_sem=send_sems.at[k - 1],
                recv_sem=recv_sems.at[k - 1], device_id=me, device_id_type=MESH_ID).wait_recv()
        for cp in sends:
            cp.wait_send()
        acc = slots[0]
        for s in range(1, N_DEV):
            acc = acc + slots[s]
        o_ref[...] = acc

    vm = pl.BlockSpec(memory_space=pltpu.VMEM)
    return pl.pallas_call(
        body, name="all_reduce_small", in_specs=[vm], out_specs=vm, out_shape=SDS(shape, F32),
        scratch_shapes=[pltpu.VMEM((N_DEV,) + shape, F32), pltpu.SemaphoreType.DMA((7,)),
                        pltpu.SemaphoreType.DMA((7,))],
        compiler_params=pltpu.CompilerParams(has_side_effects=True),
    )(packed)


def _layer_fwd(x, p, tabs, ex):
    z, h_t = _in_proj(x, p["norm_g"], p["w_in_t"])
    q, qt, k, v, vt = _attn_prep(z, p["qn"], p["kn"], tabs["ca"], tabs["sa"], tabs["ones"])
    oa, lse, *gathered = _attn_fwd(q, k, vt, ex)
    qrot, krot, vb, orr, on = _ret_fwd(z, p["lgf"], p["lgb"], p["gnw"], tabs["cr"], tabs["sr"])
    return z, h_t, q, qt, k, v, lse, oa, qrot, krot, vb, orr, on, gathered


def _layer_bwd(dxo, s, p, tabs, ex_attn, scatter_w_in):
    doa, don, dz_m, d_wout, d_wb_t = _merge_bwd(dxo, s["z"], s["oa"], s["on"], s["ya"], s["yb"], p["wb_t"], p["w_out"])
    dq_a, dk_a, dv_a, *recv_attn = _attn_bwd(s["q"], s["qt"], s["k"], s["v"], doa, s["oa"], s["lse"],
                                              ex_attn(d_wb_t, d_wout))
    dz_a, d_qn, d_kn = _attn_post_bwd(dq_a, dk_a, dv_a, s["z"], p["qn"], p["kn"], tabs["ca"], tabs["sa"],
                                      tabs["ones"])
    dq_r, dk_r, dv_r, d_gnw, d_lgf, d_lgb = _ret_bwd(s["qrot"], s["krot"], s["vb"], s["orr"], don, p["gnw"],
                                                     p["lgf"], p["lgb"])
    dqr, dkr, dvr = _ret_post_bwd(dq_r, dk_r, dv_r, tabs["cr"], tabs["sr"])
    buf = _dw_in(s["h_t"], dz_a, dz_m, dqr, dkr, dvr)
    pending, token = None, None
    if scatter_w_in:
        *pending, token = _scatter_start(buf)
    dx, d_norm_g = _in_bwd(dxo, s["x"], p["norm_g"], p["w_in_t"], dz_a, dz_m, dqr, dkr, dvr, token)
    grads = dict(w_in_t=buf, wb_t=d_wb_t, w_out=d_wout, norm_g=d_norm_g, gnw=d_gnw,
                 qn=d_qn.reshape(ATTN_Q_HEADS, ATTN_HEAD_DIM).sum(axis=0),
                 kn=d_kn.reshape(ATTN_KV_HEADS, ATTN_HEAD_DIM).sum(axis=0),
                 lgf=d_lgf[:, 0, 0], lgb=d_lgb[:, 0, 0])
    return dx, grads, recv_attn, pending


def _adamw_nd(w, g, m, v):
    shape = w.shape
    two_d = (1, shape[0]) if w.ndim == 1 else (-1, shape[-1])
    out = _adamw(w.reshape(two_d), g.reshape(two_d), m.reshape(two_d), v.reshape(two_d))
    return tuple(o.reshape(shape) for o in out)


def kernel(x, norm_g, w_in, attn_q_norm, attn_k_norm, ret_decay_fwd, ret_decay_bwd, ret_gn_w, w_branch_attn, w_branch_ret, w_out, final_norm_g, loss_target, m_norm_g, m_w_in, m_attn_q_norm, m_attn_k_norm, m_ret_decay_fwd, m_ret_decay_bwd, m_ret_gn_w, m_w_branch_attn, m_w_branch_ret, m_w_out, m_final_norm_g, v_norm_g, v_w_in, v_attn_q_norm, v_attn_k_norm, v_ret_decay_fwd, v_ret_decay_bwd, v_ret_gn_w, v_w_branch_attn, v_w_branch_ret, v_w_out, v_final_norm_g):
    t, d = x.shape[1], x.shape[2]
    x2, target = x[0], loss_target[0]

    w_in_sh, wb_sh, wout_sh = [], [], []
    for l in range(DEPTH):
        w_in_sh.append(jnp.swapaxes(w_in[l], 0, 1).astype(BF16))
        wb_sh.append(jnp.concatenate([w_branch_attn[l].T, w_branch_ret[l].T], axis=1).astype(BF16))
        wout_sh.append(w_out[l].astype(BF16))

    ca, sa = _rope_tables(t, ATTN_HEAD_DIM)
    cr, sr = _rope_tables(t, RET_HEAD_DIM)
    grp = jnp.arange(ATTN_WIDTH) // ATTN_HEAD_DIM
    tabs = dict(ca=jnp.tile(ca, (1, 2)), sa=jnp.tile(sa, (1, 2)), cr=cr, sr=sr,
                ones=jnp.where(grp[:, None] == grp[None, :], 1.0 / ATTN_HEAD_DIM, 0.0).astype(BF16))
    layers = []
    for l in range(DEPTH):
        layers.append(dict(
            norm_g=norm_g[l][None], qn=jnp.tile(attn_q_norm[l], ATTN_Q_HEADS)[None],
            kn=jnp.tile(attn_k_norm[l], ATTN_KV_HEADS)[None], gnw=ret_gn_w[l][None],
            lgf=jax.nn.log_sigmoid(ret_decay_fwd[l]), lgb=jax.nn.log_sigmoid(ret_decay_bwd[l])))

    layers[0]["w_in_t"], = _all_gather([w_in_sh[0]])
    gathers = [_Exchange("gather", [wb_sh[0], wout_sh[0], w_in_sh[1]]), _Exchange("gather", [wb_sh[1], wout_sh[1]])]
    h = x2
    saved = []
    for l in range(DEPTH):
        p = layers[l]
        z, h_t, q, qt, k, v, lse, oa, qrot, krot, vb, orr, on, got = _layer_fwd(h, p, tabs, gathers[l])
        p["wb_t"], p["w_out"] = got[0], got[1]
        if l == 0:
            layers[1]["w_in_t"] = got[2]
        xn, ya, yb = _merge_fwd(h, z, oa, on, p["wb_t"], p["w_out"])
        saved.append(dict(x=h, z=z, h_t=h_t, q=q, qt=qt, k=k, v=v, lse=lse, oa=oa, qrot=qrot, krot=krot, vb=vb,
                          orr=orr, on=on, ya=ya, yb=yb))
        h = xn
    dx, d_final_g, loss_part = _final_loss(h, final_norm_g[None], target)

    grads = [None] * DEPTH
    dx, grads[1], _, _ = _layer_bwd(dx, saved[1], layers[1], tabs, lambda *a: None, False)
    g1 = grads[1]
    ex_attn = lambda d_wb_t, d_wout: _Exchange("scatter", [g1["w_in_t"], g1["wb_t"], g1["w_out"], d_wb_t, d_wout])
    dx, grads[0], recv_attn, pending = _layer_bwd(dx, saved[0], layers[0], tabs, ex_attn, True)
    recv = [None, recv_attn[3], recv_attn[4], recv_attn[0], recv_attn[1], recv_attn[2]]
    tr = lambda a: jnp.swapaxes(a, 1, 2)
    w_in_t = (tr(w_in), tr(m_w_in), tr(v_w_in))
    sharded = {}
    w_in_l1 = _sum_adamw([recv[3]], *w_in_t, 0, 256, layer0=1)
    sharded[id(w_branch_attn)] = [tr(o) for o in _sum_adamw(
        [recv[1], recv[4]], tr(w_branch_attn), tr(m_w_branch_attn), tr(v_w_branch_attn), 0, 512)]
    sharded[id(w_branch_ret)] = [tr(o) for o in _sum_adamw(
        [recv[1], recv[4]], tr(w_branch_ret), tr(m_w_branch_ret), tr(v_w_branch_ret), 512, 512)]
    sharded[id(w_out)] = _sum_adamw([recv[2], recv[5]], w_out, m_w_out, v_w_out, 0, 256)
    g_wba, g_wbr, g_wout = (sharded[id(w)][0] for w in (w_branch_attn, w_branch_ret, w_out))

    packed = jnp.zeros((8, 1024), F32)
    for l in range(DEPTH):
        gl = grads[l]
        packed = packed.at[l].set(gl["norm_g"][0])
        packed = packed.at[2, 512 * l:512 * (l + 1)].set(gl["gnw"][0])
        packed = packed.at[4, 128 * l:128 * l + 64].set(gl["qn"])
        packed = packed.at[4, 256 + 128 * l:256 + 128 * l + 64].set(gl["kn"])
        packed = packed.at[4, 512 + 128 * l:512 + 128 * l + 4].set(gl["lgf"])
        packed = packed.at[4, 768 + 128 * l:768 + 128 * l + 4].set(gl["lgb"])
    packed = packed.at[3].set(d_final_g[0])
    packed = packed.at[5, 0].set(loss_part[0, 0])
    red = _all_reduce_small(packed)
    loss = red[5, 0]
    g_norm_g = red[0:2]
    g_gnw = red[2].reshape(DEPTH, RET_WIDTH)
    g_final = red[3]
    g_qn = jnp.stack([red[4, 128 * l:128 * l + 64] for l in range(DEPTH)])
    g_kn = jnp.stack([red[4, 256 + 128 * l:256 + 128 * l + 64] for l in range(DEPTH)])
    g_lgf = jnp.stack([red[4, 512 + 128 * l:512 + 128 * l + 4] for l in range(DEPTH)])
    g_lgb = jnp.stack([red[4, 768 + 128 * l:768 + 128 * l + 4] for l in range(DEPTH)])
    g_df = g_lgf * jax.nn.sigmoid(-ret_decay_fwd)
    g_db = g_lgb * jax.nn.sigmoid(-ret_decay_bwd)

    grad_w = [g_norm_g, None, g_qn, g_kn, g_df, g_db, g_gnw, g_wba, g_wbr, g_wout, g_final]
    weights = [norm_g, w_in, attn_q_norm, attn_k_norm, ret_decay_fwd, ret_decay_bwd, ret_gn_w, w_branch_attn,
               w_branch_ret, w_out, final_norm_g]
    ms = [m_norm_g, m_w_in, m_attn_q_norm, m_attn_k_norm, m_ret_decay_fwd, m_ret_decay_bwd, m_ret_gn_w,
          m_w_branch_attn, m_w_branch_ret, m_w_out, m_final_norm_g]
    vs = [v_norm_g, v_w_in, v_attn_q_norm, v_attn_k_norm, v_ret_decay_fwd, v_ret_decay_bwd, v_ret_gn_w,
          v_w_branch_attn, v_w_branch_ret, v_w_out, v_final_norm_g]
    upd = [None if w is w_in else sharded[id(w)][1:] if id(w) in sharded else _adamw_nd(w, g, m, v)
           for w, g, m, v in zip(weights, grad_w, ms, vs)]

    done = [dx, w_in_l1[0], g_wout] + [u[0] for w, u in zip(weights, upd) if u is not None and id(w) not in sharded]
    recv[0] = _place_own_block(*_scatter_wait(*pending, done))
    w_in_upd = [tr(o) for o in _sum_adamw([recv[0]], *w_in_t, 0, 256, layer0=0, prev=w_in_l1)]
    grad_w[1], upd[1] = w_in_upd[0], w_in_upd[1:]
    return (loss, dx[None], *grad_w, *[u[0] for u in upd], *[u[1] for u in upd], *[u[2] for u in upd])
```

```python
import functools

import jax
import jax.numpy as jnp
from jax import lax
from jax.experimental import pallas as pl
from jax.experimental.pallas import tpu as pltpu

F32 = jnp.float32
BF16 = jnp.bfloat16
SDS = jax.ShapeDtypeStruct

D_MODEL = 1024
DEPTH = 2
GRID_W = 64
ATTN_Q_HEADS = 8
ATTN_KV_HEADS = 2
ATTN_HEAD_DIM = 64
ATTN_WIDTH = 512
ATTN_KV_WIDTH = 128
RET_HEADS = 4
RET_HEAD_DIM = 128
RET_WIDTH = 512
RET_CHUNK = 128
ATTN_KEY_CHUNK = 512
ATTN_BWD_KEY_CHUNK = 1024
QK_DOTS_PER_CHUNK = 4
EXP_LAG = 3
ROPE_THETA = 10000.0
EPS = 1e-6
D_IN = 5376
N_DEV = 8

ADAM_LR = 0.001
ADAM_B1 = 0.9
ADAM_B2 = 0.999
ADAM_EPS = 1e-08
ADAM_WD = 0.01
ADAM_STEP = 10

SEG = {
    "qa": (0, 512, 0),
    "ga": (768, 512, 512),
    "qr": (1280, 512, 1024),
    "kr": (1792, 512, 1536),
    "vr": (2304, 512, 2048),
    "gr": (2816, 512, 2560),
    "gm": (3328, 2048, 3072),
    "ka": (512, 128, 5120),
    "va": (640, 128, 5248),
}

VMEM_LIMIT = 60 * 1024 * 1024
NT = (((1,), (1,)), ((), ()))
TN = (((0,), (0,)), ((), ()))
MESH_ID = pl.DeviceIdType.MESH
ANY = pl.BlockSpec(memory_space=pl.ANY)


def _params(sem=None, vmem=VMEM_LIMIT):
    return pltpu.CompilerParams(dimension_semantics=sem, vmem_limit_bytes=vmem)


def _dot(a, b, dims=None):
    if dims is None:
        return jnp.dot(a, b, preferred_element_type=F32)
    return lax.dot_general(a, b, dims, preferred_element_type=F32)


def _sigmoid(x):
    return 1.0 / (1.0 + jnp.exp(-x))


def _swap_halves(x, q):
    n = x.shape[-1]
    axis = x.ndim - 1
    lane = lax.broadcasted_iota(jnp.int32, x.shape, axis)
    first = (lane % (2 * q)) < q
    return jnp.where(first, pltpu.roll(x, n - q, axis), pltpu.roll(x, q, axis))


def _rope(x, cos, sin_signed, q):
    return x * cos + _swap_halves(x, q) * sin_signed


def _rope_bwd(dy, cos, sin_signed, q):
    return dy * cos - _swap_halves(dy, q) * sin_signed


def _group_mean(v, ones_bd):
    hi = v.astype(BF16)
    r1 = v - hi.astype(F32)
    mid = r1.astype(BF16)
    lo = (r1 - mid.astype(F32)).astype(BF16)
    return _dot(hi, ones_bd) + _dot(mid, ones_bd) + _dot(lo, ones_bd)


def _rope_tables(t, head_dim):
    n_rows = t // GRID_W
    d_axis = head_dim // 2
    inv_freq = ROPE_THETA ** (-jnp.arange(0, d_axis, 2, dtype=F32) / d_axis)
    ar = jnp.arange(n_rows, dtype=F32)[:, None] * inv_freq
    ac = jnp.arange(GRID_W, dtype=F32)[:, None] * inv_freq
    by_row = lambda a: jnp.repeat(a, GRID_W, axis=0)
    by_col = lambda a: jnp.tile(a, (n_rows, 1))
    cr, sr, cc, sc = by_row(jnp.cos(ar)), by_row(jnp.sin(ar)), by_col(jnp.cos(ac)), by_col(jnp.sin(ac))
    return jnp.concatenate([cr, cr, cc, cc], axis=-1), jnp.concatenate([-sr, sr, -sc, sc], axis=-1)


def _me():
    return lax.axis_index("x"), lax.axis_index("y"), lax.axis_index("c")


def _flip(k):
    x, y, c = _me()
    px = 1 - x if k & 4 else x
    py = 1 - y if k & 2 else y
    pc = 1 - c if k & 1 else c
    return (px, py, pc), 4 * px + 2 * py + pc


class _Exchange:
    def __init__(self, kind, srcs):
        self.kind, self.srcs, self.n = kind, list(srcs), len(srcs)
        self.rows = [a.shape[0] if kind == "gather" else a.shape[0] // N_DEV for a in srcs]
        if kind == "gather":
            self.out_shape = [SDS((N_DEV * a.shape[0], a.shape[1]), a.dtype) for a in srcs]
        else:
            self.out_shape = [SDS((N_DEV, a.shape[0] // N_DEV, a.shape[1]), a.dtype) for a in srcs]
        self.scratch = [pltpu.SemaphoreType.DMA((self.n, N_DEV - 1)), pltpu.SemaphoreType.DMA((self.n, N_DEV - 1)),
                        pltpu.SemaphoreType.DMA((self.n,))]

    def _block(self, ref, a, idx):
        r = self.rows[a]
        return ref.at[pl.ds(pl.multiple_of(idx * r, 16), r), :]

    def _src(self, ins, a, idx):
        return ins[a] if self.kind == "gather" else self._block(ins[a], a, idx)

    def _dst(self, outs, a, idx):
        return self._block(outs[a], a, idx) if self.kind == "gather" else outs[a].at[idx]

    def _copies(self, ins, outs, sems):
        send_sems, recv_sems, local_sems = sems
        me, mine = _flip(0)
        local, sends, recvs = [], [], []
        for a in range(self.n):
            local.append(pltpu.make_async_copy(self._src(ins, a, mine), self._dst(outs, a, mine), local_sems.at[a]))
            for k in range(1, N_DEV):
                peer, theirs = _flip(k)
                sem = dict(send_sem=send_sems.at[a, k - 1], recv_sem=recv_sems.at[a, k - 1])
                sends.append(pltpu.make_async_remote_copy(
                    src_ref=self._src(ins, a, theirs), dst_ref=self._dst(outs, a, mine),
                    device_id=peer, device_id_type=MESH_ID, **sem))
                recvs.append(pltpu.make_async_remote_copy(
                    src_ref=self._dst(outs, a, theirs), dst_ref=self._dst(outs, a, theirs),
                    device_id=me, device_id_type=MESH_ID, **sem))
        return local, sends, recvs

    def start(self, ins, outs, sems):
        local, sends, _ = self._copies(ins, outs, sems)
        for cp in local + sends:
            cp.start()

    def wait(self, ins, outs, sems):
        local, sends, recvs = self._copies(ins, outs, sems)
        for cp in sends:
            cp.wait_send()
        for cp in recvs:
            cp.wait_recv()
        for cp in local:
            cp.wait()


def _with_exchange(body, n_in, n_out, n_scratch, ex, first, last):
    if ex is None:
        return body

    def wrapped(*refs):
        ins = refs[:n_in]
        ex_ins = refs[n_in:n_in + ex.n]
        outs = refs[n_in + ex.n:n_in + ex.n + n_out]
        ex_outs = refs[n_in + ex.n + n_out:n_in + 2 * ex.n + n_out]
        rest = refs[n_in + 2 * ex.n + n_out:]
        scratch, sems = rest[:n_scratch], rest[n_scratch:]

        @pl.when(first())
        def _():
            ex.start(ex_ins, ex_outs, sems)

        body(*ins, *outs, *scratch)

        @pl.when(last())
        def _():
            ex.wait(ex_ins, ex_outs, sems)

    return wrapped


def _ex_args(ex):
    if ex is None:
        return [], [], [], [], []
    return [ANY] * ex.n, [ANY] * ex.n, list(ex.out_shape), list(ex.scratch), list(ex.srcs)


def _in_proj(x, g, w_t):
    t, d = x.shape
    tm = min(256, t)

    def body(x_ref, g_ref, w_ref, z_ref, ht_ref):
        xv = x_ref[...]
        r = lax.rsqrt(jnp.mean(xv * xv, axis=-1, keepdims=True) + EPS)
        h = xv * r * g_ref[...]
        ht_ref[...] = h.T.astype(BF16)
        hb = h.astype(BF16)
        for nat, w, off in SEG.values():
            z_ref[:, off:off + w] = _dot(hb, w_ref[nat:nat + w, :], NT)

    return pl.pallas_call(
        body, name="in_proj", grid=(t // tm,),
        in_specs=[pl.BlockSpec((tm, d), lambda i: (i, 0)), pl.BlockSpec((1, d), lambda i: (0, 0)),
                  pl.BlockSpec((D_IN, d), lambda i: (0, 0))],
        out_specs=[pl.BlockSpec((tm, D_IN), lambda i: (i, 0)), pl.BlockSpec((d, tm), lambda i: (0, i))],
        out_shape=[SDS((t, D_IN), F32), SDS((d, t), BF16)],
        compiler_params=_params(("parallel",)),
    )(x, g, w_t)


def _attn_prep(z, qn, kn, cos, sin, ones_bd):
    t = z.shape[0]
    tm = min(ATTN_KEY_CHUNK, t)
    hd = ATTN_HEAD_DIM

    def body(zq_ref, zkv_ref, qn_ref, kn_ref, c_ref, s_ref, b_ref, q_out, qt_out, k_out, v_out, vt_out):
        bd = b_ref[...]
        c2, s2 = c_ref[...], s_ref[...]
        cq = jnp.concatenate([c2] * 4, axis=-1)
        sq = jnp.concatenate([s2] * 4, axis=-1)
        xq = zq_ref[...]
        yq = xq * lax.rsqrt(_group_mean(xq * xq, bd) + EPS) * qn_ref[...]
        yq = _rope(yq, cq, sq, hd // 4) * (hd ** -0.5)
        yqt = yq.T
        for h in range(ATTN_Q_HEADS):
            q_out[h] = yq[:, h * hd:(h + 1) * hd].astype(BF16)
            qt_out[h] = yqt[h * hd:(h + 1) * hd, :].astype(BF16)
        zkv = zkv_ref[...]
        xk, xv = zkv[:, :ATTN_KV_WIDTH], zkv[:, ATTN_KV_WIDTH:]
        yk = xk * lax.rsqrt(_group_mean(xk * xk, bd[:ATTN_KV_WIDTH, :ATTN_KV_WIDTH]) + EPS) * kn_ref[...]
        yk = _rope(yk, c2, s2, hd // 4)
        xvt = xv.T
        ones = jnp.ones((hd, tm), F32)
        for h in range(ATTN_KV_HEADS):
            k_out[h] = yk[:, h * hd:(h + 1) * hd].astype(BF16)
            v_out[h] = xv[:, h * hd:(h + 1) * hd].astype(BF16)
            vt_out[h, 0] = jnp.concatenate([xvt[h * hd:(h + 1) * hd, :], ones], axis=0).astype(BF16)

    kv_blk = SEG["ka"][2] // 256
    nk = t // tm
    return pl.pallas_call(
        body, name="attn_prep", grid=(nk,),
        in_specs=[pl.BlockSpec((tm, 512), lambda i: (i, 0)), pl.BlockSpec((tm, 256), lambda i: (i, kv_blk)),
                  pl.BlockSpec((1, 512), lambda i: (0, 0)), pl.BlockSpec((1, 128), lambda i: (0, 0)),
                  pl.BlockSpec((tm, 128), lambda i: (i, 0)), pl.BlockSpec((tm, 128), lambda i: (i, 0)),
                  pl.BlockSpec((512, 512), lambda i: (0, 0))],
        out_specs=[pl.BlockSpec((ATTN_Q_HEADS, tm, hd), lambda i: (0, i, 0)),
                   pl.BlockSpec((ATTN_Q_HEADS, hd, tm), lambda i: (0, 0, i)),
                   pl.BlockSpec((ATTN_KV_HEADS, tm, hd), lambda i: (0, i, 0)),
                   pl.BlockSpec((ATTN_KV_HEADS, tm, hd), lambda i: (0, i, 0)),
                   pl.BlockSpec((ATTN_KV_HEADS, 1, 2 * hd, tm), lambda i: (0, i, 0, 0))],
        out_shape=[SDS((ATTN_Q_HEADS, t, hd), BF16), SDS((ATTN_Q_HEADS, hd, t), BF16),
                   SDS((ATTN_KV_HEADS, t, hd), BF16), SDS((ATTN_KV_HEADS, t, hd), BF16),
                   SDS((ATTN_KV_HEADS, nk, 2 * hd, tm), BF16)],
        compiler_params=_params(("parallel",)),
    )(z, z, qn, kn, cos, sin, ones_bd)


def _attn_fwd(q, k, vt, ex=None):
    t = q.shape[1]
    tq = min(256, t)
    nk, tk = vt.shape[1], vt.shape[3]
    hd = ATTN_HEAD_DIM
    g = ATTN_Q_HEADS // ATTN_KV_HEADS

    def body(q_ref, k_ref, vt_ref, o_ref, lse_ref, s_scr):
        def pass_a(h, c, m8):
            part = tk // QK_DOTS_PER_CHUNK
            for lo in range(c * tk, (c + 1) * tk, part):
                st = _dot(k_ref[0, lo:lo + part, :], q_ref[h], NT)
                s_scr[h % 2, lo:lo + part, :] = st
                m8 = jnp.maximum(m8, jnp.max(st.reshape(part // 8, 8, tq), axis=0))
            return m8

        def pass_b(h, c, m, acc, after):
            e = jnp.exp(s_scr[h % 2, c * tk:(c + 1) * tk, :] - (m + after * 0.0)).astype(BF16)
            return acc + _dot(vt_ref[0, c], e)

        neg = jnp.full((8, tq), -jnp.inf, F32)
        m8 = neg
        for c in range(nk):
            m8 = pass_a(0, c, m8)
        outs = []
        for h in range(g):
            m = jnp.max(m8, axis=0, keepdims=True)
            acc = jnp.zeros((2 * hd, tq), F32)
            m8 = neg
            done = [m] * EXP_LAG
            for c in range(nk):
                if h + 1 < g:
                    m8 = pass_a(h + 1, c, m8)
                acc = pass_b(h, c, m, acc, done[-EXP_LAG])
                done.append(m8[0:1, :] if h + 1 < g else acc[hd:hd + 1, :])
            l = acc[hd:hd + 1, :]
            outs.append((acc[:hd, :] / l).T)
            lse_ref[h] = m + jnp.log(l)
        o_ref[...] = jnp.concatenate(outs, axis=-1)

    nq = t // tq
    first = lambda: jnp.logical_and(pl.program_id(0) == 0, pl.program_id(1) == 0)
    last = lambda: jnp.logical_and(pl.program_id(0) == ATTN_KV_HEADS - 1, pl.program_id(1) == nq - 1)
    xi, xo, xs, xscr, xargs = _ex_args(ex)
    return pl.pallas_call(
        _with_exchange(body, 3, 2, 1, ex, first, last), name="attn_fwd", grid=(ATTN_KV_HEADS, nq),
        in_specs=[pl.BlockSpec((g, tq, hd), lambda p, i: (p, i, 0)),
                  pl.BlockSpec((1, t, hd), lambda p, i: (p, 0, 0)),
                  pl.BlockSpec((1, nk, 2 * hd, tk), lambda p, i: (p, 0, 0, 0))] + xi,
        out_specs=[pl.BlockSpec((tq, g * hd), lambda p, i: (i, p)),
                   pl.BlockSpec((g, 1, tq), lambda p, i: (p, 0, i))] + xo,
        out_shape=[SDS((t, ATTN_WIDTH), F32), SDS((ATTN_Q_HEADS, 1, t), F32)] + xs,
        scratch_shapes=[pltpu.VMEM((2, t, tq), F32)] + xscr,
        compiler_params=_params(("arbitrary", "arbitrary")),
    )(q, k, vt, *xargs)


class _Dir:
    def __init__(self, lg, strict_future):
        c = RET_CHUNK
        ia = lax.broadcasted_iota(jnp.int32, (c, c), 0).astype(F32)
        ib = lax.broadcasted_iota(jnp.int32, (c, c), 1).astype(F32)
        col = lax.broadcasted_iota(jnp.int32, (c, 1), 0).astype(F32)
        row = lax.broadcasted_iota(jnp.int32, (1, c), 1).astype(F32)
        if strict_future:
            dist = ib - ia
            mask = dist > 0
            self.wq, self.wk, wk_row = c - col, col, row
        else:
            dist = ia - ib
            mask = dist >= 0
            self.wq, self.wk, wk_row = col + 1.0, c - 1.0 - col, c - 1.0 - row
        self.dist = jnp.maximum(dist, 0.0)
        self.d = jnp.where(mask, jnp.exp(self.dist * lg), 0.0)
        self.qd = jnp.exp(self.wq * lg)
        self.kd_col = jnp.exp(self.wk * lg)
        self.kd_row = jnp.exp(wk_row * lg)
        self.cd = jnp.exp(jnp.full((1, 1), float(c), F32) * lg)


def _ret_fwd(z, lgf, lgb, gnw, cos, sin):
    t = z.shape[0]
    c = RET_CHUNK
    nc = t // c
    hd = RET_HEAD_DIM
    unroll = 4 if nc % 4 == 0 else 1

    def body(lgf_ref, lgb_ref, q_ref, k_ref, v_ref, c_ref, s_ref, w_ref,
             qo_ref, ko_ref, vo_ref, orr_ref, on_ref, kt, uf, ub, sfa, sba):
        h = pl.program_id(0)
        fw = _Dir(lgf_ref[h], False)
        bw = _Dir(lgb_ref[h], True)
        cc, ss = c_ref[...], s_ref[...]
        qo_ref[...] = _rope(q_ref[...], cc, ss, hd // 4).astype(BF16)
        kr = _rope(k_ref[...], cc, ss, hd // 4) * (hd ** -0.5)
        ko_ref[...] = kr.astype(BF16)
        vo_ref[...] = v_ref[...].astype(BF16)
        for i in range(nc):
            kt[i] = kr[i * c:(i + 1) * c, :].T.astype(BF16)

        def rows(ci):
            return pl.ds(pl.multiple_of(ci * c, c), c)

        def kv_products(ci, carry):
            vv = vo_ref[rows(ci), :]
            ktf = kt[ci].astype(F32)
            uf[ci] = _dot((ktf * fw.kd_row).astype(BF16), vv)
            ub[ci] = _dot((ktf * bw.kd_row).astype(BF16), vv)
            return carry

        lax.fori_loop(0, nc, kv_products, 0, unroll=unroll)

        def scan(i, carry):
            sf, sb = carry
            j = nc - 1 - i
            sfa[i] = sf.astype(BF16)
            sba[j] = sb.astype(BF16)
            return sf * fw.cd + uf[i], sb * bw.cd + ub[j]

        zero = jnp.zeros((hd, hd), F32)
        lax.fori_loop(0, nc, scan, (zero, zero))
        gw = w_ref[...]

        def outputs(ci, carry):
            sl = rows(ci)
            qq, kk, vv = qo_ref[sl, :], ko_ref[sl, :], vo_ref[sl, :]
            a = _dot(qq, kk, NT)
            o = (_dot((a * fw.d).astype(BF16), vv) + _dot(qq, sfa[ci]) * fw.qd
                 + _dot((a * bw.d).astype(BF16), vv) + _dot(qq, sba[ci]) * bw.qd)
            orr_ref[sl, :] = o
            xc = o - jnp.mean(o, axis=-1, keepdims=True)
            var = jnp.mean(xc * xc, axis=-1, keepdims=True)
            on_ref[sl, :] = xc * lax.rsqrt(var + EPS) * gw
            return carry

        lax.fori_loop(0, nc, outputs, 0, unroll=unroll)

    smem = pl.BlockSpec(memory_space=pltpu.SMEM)
    col = lambda name: (lambda h: (0, SEG[name][2] // 128 + h))
    head = pl.BlockSpec((t, 128), lambda h: (0, h))
    full = pl.BlockSpec((t, 128), lambda h: (0, 0))
    return pl.pallas_call(
        body, name="ret_fwd", grid=(RET_HEADS,),
        in_specs=[smem, smem, pl.BlockSpec((t, 128), col("qr")), pl.BlockSpec((t, 128), col("kr")),
                  pl.BlockSpec((t, 128), col("vr")), full, full, pl.BlockSpec((1, 128), lambda h: (0, h))],
        out_specs=[head, head, head, head, head],
        out_shape=[SDS((t, RET_WIDTH), BF16)] * 3 + [SDS((t, RET_WIDTH), F32)] * 2,
        scratch_shapes=[pltpu.VMEM((nc, hd, c), BF16), pltpu.VMEM((nc, hd, hd), F32), pltpu.VMEM((nc, hd, hd), F32),
                        pltpu.VMEM((nc, hd, hd), BF16), pltpu.VMEM((nc, hd, hd), BF16)],
        compiler_params=_params(("parallel",)),
    )(lgf, lgb, z, z, z, cos, sin, gnw)


def _merge_fwd(x, z, oa, on, wb_t, wout):
    t, d = x.shape
    tm = min(256, t)

    def body(x_ref, ga_ref, gr_ref, gm0_ref, gm1_ref, oa_ref, on_ref, wb_ref, wo_ref, xn_ref, ya_ref, yb_ref):
        ga, gr = ga_ref[...], gr_ref[...]
        ua = ga * _sigmoid(ga) * oa_ref[...]
        ub = gr * _sigmoid(gr) * on_ref[...]
        ya = _dot(ua.astype(BF16), wb_ref[:, :512], NT)
        yb = _dot(ub.astype(BF16), wb_ref[:, 512:], NT)
        ya_ref[...] = ya
        yb_ref[...] = yb
        merged = _sigmoid(gm0_ref[...]) * ya + _sigmoid(gm1_ref[...]) * yb
        xn_ref[...] = x_ref[...] + _dot(merged.astype(BF16), wo_ref[...])

    row = lambda w, j: pl.BlockSpec((tm, w), lambda i: (i, j))
    const = lambda shape: pl.BlockSpec(shape, lambda i: (0, 0))
    return pl.pallas_call(
        body, name="merge_fwd", grid=(t // tm,),
        in_specs=[row(d, 0), row(512, SEG["ga"][2] // 512), row(512, SEG["gr"][2] // 512),
                  row(1024, SEG["gm"][2] // 1024), row(1024, SEG["gm"][2] // 1024 + 1),
                  row(512, 0), row(512, 0), const((d, 1024)), const((d, d))],
        out_specs=[row(d, 0), row(d, 0), row(d, 0)],
        out_shape=[SDS((t, d), F32)] * 3,
        compiler_params=_params(("parallel",)),
    )(x, z, z, z, z, oa, on, wb_t, wout)


def _final_loss(x, g, target):
    t, d = x.shape
    tm = min(512, t)
    n = t // tm

    def body(x_ref, g_ref, t_ref, dx_ref, dg_ref, loss_ref, acc_g, acc_l):
        i = pl.program_id(0)

        @pl.when(i == 0)
        def _():
            acc_g[...] = jnp.zeros_like(acc_g)
            acc_l[...] = jnp.zeros_like(acc_l)

        xv, gv = x_ref[...], g_ref[...]
        r = lax.rsqrt(jnp.mean(xv * xv, axis=-1, keepdims=True) + EPS)
        xh = xv * r
        err = xh * gv - t_ref[...]
        dy = err * (1.0 / d)
        gy = dy * gv
        dx_ref[...] = r * (gy - xh * jnp.mean(gy * xh, axis=-1, keepdims=True))
        acc_g[...] += jnp.sum((dy * xh).reshape(tm // 8, 8, d), axis=0)
        acc_l[...] += jnp.sum((err * err).reshape(tm // 8, 8, d), axis=0)

        @pl.when(i == n - 1)
        def _():
            dg_ref[...] = jnp.sum(acc_g[...], axis=0, keepdims=True)
            tot = jnp.sum(jnp.sum(acc_l[...], axis=0, keepdims=True), axis=1, keepdims=True)
            loss_ref[...] = jnp.broadcast_to(tot * (0.5 / d), (1, 128))

    return pl.pallas_call(
        body, name="final_loss", grid=(n,),
        in_specs=[pl.BlockSpec((tm, d), lambda i: (i, 0)), pl.BlockSpec((1, d), lambda i: (0, 0)),
                  pl.BlockSpec((tm, d), lambda i: (i, 0))],
        out_specs=[pl.BlockSpec((tm, d), lambda i: (i, 0)), pl.BlockSpec((1, d), lambda i: (0, 0)),
                   pl.BlockSpec((1, 128), lambda i: (0, 0))],
        out_shape=[SDS((t, d), F32), SDS((1, d), F32), SDS((1, 128), F32)],
        scratch_shapes=[pltpu.VMEM((8, d), F32), pltpu.VMEM((8, d), F32)],
        compiler_params=_params(("arbitrary",)),
    )(x, g, target)


def _merge_bwd(dxo, z, oa, on, ya, yb, wb_t, wout):
    t, d = dxo.shape
    tm = min(256, t)
    n = t // tm

    def body(dx_ref, ga_ref, gr_ref, gm0_ref, gm1_ref, oa_ref, on_ref, ya_ref, yb_ref, wb_ref, wo_ref,
             doa_ref, don_ref, dz_ref, dwo_ref, dwb_ref, acc_o, acc_b):
        i = pl.program_id(0)

        @pl.when(i == 0)
        def _():
            acc_o[...] = jnp.zeros_like(acc_o)
            acc_b[...] = jnp.zeros_like(acc_b)

        dxb = dx_ref[...].astype(BF16)
        ya, yb = ya_ref[...], yb_ref[...]
        g0, g1 = _sigmoid(gm0_ref[...]), _sigmoid(gm1_ref[...])
        mb = (g0 * ya + g1 * yb).astype(BF16)
        dm = _dot(dxb, wo_ref[...], NT)
        dya = (dm * g0).astype(BF16)
        dyb = (dm * g1).astype(BF16)
        dz_ref[:, 1024:2048] = (dm * ya * g0 * (1.0 - g0)).astype(BF16)
        dz_ref[:, 2048:3072] = (dm * yb * g1 * (1.0 - g1)).astype(BF16)

        def branch(g_ref, o_ref, dy, w, do_ref, lo):
            gv, ov = g_ref[...], o_ref[...]
            sg = _sigmoid(gv)
            silu = gv * sg
            du = _dot(dy, w)
            do_ref[...] = du * silu
            dz_ref[:, lo:lo + 512] = (du * ov * (sg * (1.0 + gv * (1.0 - sg)))).astype(BF16)
            acc_b[:, lo:lo + 512] += _dot(dy, (silu * ov).astype(BF16), TN)

        branch(ga_ref, oa_ref, dya, wb_ref[:, :512], doa_ref, 0)
        branch(gr_ref, on_ref, dyb, wb_ref[:, 512:], don_ref, 512)
        acc_o[...] += _dot(mb, dxb, TN)

        @pl.when(i == n - 1)
        def _():
            dwo_ref[...] = acc_o[...].astype(BF16)
            dwb_ref[...] = acc_b[...].astype(BF16)

    row = lambda w, j: pl.BlockSpec((tm, w), lambda i: (i, j))
    const = lambda shape: pl.BlockSpec(shape, lambda i: (0, 0))
    return pl.pallas_call(
        body, name="merge_bwd", grid=(n,),
        in_specs=[row(d, 0), row(512, SEG["ga"][2] // 512), row(512, SEG["gr"][2] // 512),
                  row(1024, SEG["gm"][2] // 1024), row(1024, SEG["gm"][2] // 1024 + 1),
                  row(512, 0), row(512, 0), row(d, 0), row(d, 0), const((d, 1024)), const((d, d))],
        out_specs=[row(512, 0), row(512, 0), row(3072, 0), const((d, d)), const((d, 1024))],
        out_shape=[SDS((t, 512), F32), SDS((t, 512), F32), SDS((t, 3072), BF16), SDS((d, d), BF16),
                   SDS((d, 1024), BF16)],
        scratch_shapes=[pltpu.VMEM((d, d), F32), pltpu.VMEM((d, 1024), F32)],
        compiler_params=_params(("arbitrary",)),
    )(dxo, z, z, z, z, oa, on, ya, yb, wb_t, wout)


def _ret_bwd(qrot, krot, vb, orr, don, gnw, lgf, lgb):
    t = qrot.shape[0]
    c = RET_CHUNK
    nc = t // c
    hd = RET_HEAD_DIM
    unroll = 2 if nc % 2 == 0 else 1

    def body(lgf_ref, lgb_ref, q_ref, k_ref, v_ref, o_ref, dn_ref, w_ref,
             dq_ref, dk_ref, dv_ref, dw_ref, dlf_ref, dlb_ref, qt, kt, dob, uf, ub, wf, wb, sfa, sba, gfa, gba):
        h = pl.program_id(0)
        fw = _Dir(lgf_ref[h], False)
        bw = _Dir(lgb_ref[h], True)
        fw.dt, bw.dt = fw.d.T, bw.d.T

        o = o_ref[...]
        xc = o - jnp.mean(o, axis=-1, keepdims=True)
        r = lax.rsqrt(jnp.mean(xc * xc, axis=-1, keepdims=True) + EPS)
        xh = xc * r
        dn = dn_ref[...]
        gy = dn * w_ref[...]
        d_o = r * (gy - jnp.mean(gy, axis=-1, keepdims=True) - xh * jnp.mean(gy * xh, axis=-1, keepdims=True))
        dw_ref[...] = jnp.sum(dn * xh, axis=0, keepdims=True)
        dob[...] = d_o.astype(BF16)
        for i in range(nc):
            qt[i] = q_ref[i * c:(i + 1) * c, :].astype(F32).T.astype(BF16)
            kt[i] = k_ref[i * c:(i + 1) * c, :].astype(F32).T.astype(BF16)

        def rows(ci):
            return pl.ds(pl.multiple_of(ci * c, c), c)

        def products(ci, carry):
            sl = rows(ci)
            vv, do32 = v_ref[sl, :], dob[sl, :].astype(F32)
            ktf = kt[ci].astype(F32)
            uf[ci] = _dot((ktf * fw.kd_row).astype(BF16), vv)
            ub[ci] = _dot((ktf * bw.kd_row).astype(BF16), vv)
            wf[ci] = _dot(qt[ci], (do32 * fw.qd).astype(BF16))
            wb[ci] = _dot(qt[ci], (do32 * bw.qd).astype(BF16))
            return carry

        lax.fori_loop(0, nc, products, 0, unroll=unroll)

        def scan(i, carry):
            sf, sb, gf, gb = carry
            j = nc - 1 - i
            sfa[i] = sf.astype(BF16)
            sba[j] = sb.astype(BF16)
            gfa[j] = gf.astype(BF16)
            gba[i] = gb.astype(BF16)
            return sf * fw.cd + uf[i], sb * bw.cd + ub[j], gf * fw.cd + wf[j], gb * bw.cd + wb[i]

        zero = jnp.zeros((hd, hd), F32)
        lax.fori_loop(0, nc, scan, (zero, zero, zero, zero))

        def one_dir(p, s_all, g_all, ci, qq, kk, vv, do, a, bm, at, bt):
            sb, gb = s_all[ci], g_all[ci]
            doq = (do.astype(F32) * p.qd).astype(BF16)
            dqc = _dot(doq, sb, NT)
            dq = _dot((bm * p.d).astype(BF16), kk) + dqc
            kkd = (kk.astype(F32) * p.kd_col).astype(BF16)
            dv = _dot((at * p.dt).astype(BF16), do) + _dot(kkd, gb)
            dk2 = _dot(vv, gb, NT) * p.kd_col
            dk = _dot((bt * p.dt).astype(BF16), qq) + dk2
            terms = (p.dist * p.d * a * bm + p.wq * qq.astype(F32) * dqc + p.wk * kk.astype(F32) * dk2
                     + (float(c) * p.cd) * gb.astype(F32) * sb.astype(F32))
            return dq, dk, dv, terms

        def chunk(ci, carry):
            af, ab = carry
            sl = rows(ci)
            qq, kk, vv, do = q_ref[sl, :], k_ref[sl, :], v_ref[sl, :], dob[sl, :]
            a, bm = _dot(qq, kk, NT), _dot(do, vv, NT)
            at, bt = _dot(kk, qq, NT), _dot(vv, do, NT)
            dqf, dkf, dvf, tf = one_dir(fw, sfa, gfa, ci, qq, kk, vv, do, a, bm, at, bt)
            dqb, dkb, dvb, tb = one_dir(bw, sba, gba, ci, qq, kk, vv, do, a, bm, at, bt)
            dq_ref[sl, :] = dqf + dqb
            dk_ref[sl, :] = dkf + dkb
            dv_ref[sl, :] = dvf + dvb
            return af + tf, ab + tb

        af, ab = lax.fori_loop(0, nc, chunk, (zero, zero), unroll=unroll)
        tot = lambda m: jnp.sum(jnp.sum(m, axis=0, keepdims=True), axis=1, keepdims=True)
        dlf_ref[...] = jnp.broadcast_to(tot(af).reshape(1, 1, 1), (1, 8, 128))
        dlb_ref[...] = jnp.broadcast_to(tot(ab).reshape(1, 1, 1), (1, 8, 128))

    smem = pl.BlockSpec(memory_space=pltpu.SMEM)
    head = pl.BlockSpec((t, 128), lambda h: (0, h))
    vec = pl.BlockSpec((1, 128), lambda h: (0, h))
    scal = pl.BlockSpec((1, 8, 128), lambda h: (h, 0, 0))
    mats = lambda dt: pltpu.VMEM((nc, hd, hd), dt)
    return pl.pallas_call(
        body, name="ret_bwd", grid=(RET_HEADS,),
        in_specs=[smem, smem, head, head, head, head, head, vec],
        out_specs=[head, head, head, vec, scal, scal],
        out_shape=[SDS((t, RET_WIDTH), F32)] * 3 + [SDS((1, RET_WIDTH), F32), SDS((RET_HEADS, 8, 128), F32),
                                                   SDS((RET_HEADS, 8, 128), F32)],
        scratch_shapes=[pltpu.VMEM((nc, hd, c), BF16), pltpu.VMEM((nc, hd, c), BF16), pltpu.VMEM((t, hd), BF16),
                        mats(F32), mats(F32), mats(F32), mats(F32), mats(BF16), mats(BF16), mats(BF16), mats(BF16)],
        compiler_params=_params(("parallel",)),
    )(lgf, lgb, qrot, krot, vb, orr, don, gnw)


def _ret_post_bwd(dq, dk, dv, cos, sin):
    t = dq.shape[0]
    tm = min(512, t)
    hd = RET_HEAD_DIM

    def body(dq_ref, dk_ref, dv_ref, c_ref, s_ref, oq_ref, ok_ref, ov_ref):
        cc = jnp.concatenate([c_ref[...]] * 4, axis=-1)
        ss = jnp.concatenate([s_ref[...]] * 4, axis=-1)
        oq_ref[...] = _rope_bwd(dq_ref[...], cc, ss, hd // 4).astype(BF16)
        ok_ref[...] = (_rope_bwd(dk_ref[...], cc, ss, hd // 4) * (hd ** -0.5)).astype(BF16)
        ov_ref[...] = dv_ref[...].astype(BF16)

    blk = pl.BlockSpec((tm, 512), lambda i: (i, 0))
    tab = pl.BlockSpec((tm, 128), lambda i: (i, 0))
    return pl.pallas_call(
        body, name="ret_post_bwd", grid=(t // tm,),
        in_specs=[blk, blk, blk, tab, tab], out_specs=[blk, blk, blk],
        out_shape=[SDS((t, 512), BF16)] * 3,
        compiler_params=_params(("parallel",)),
    )(dq, dk, dv, cos, sin)


def _attn_bwd(q, qt, k, v, doa, oa, lse, ex=None):
    t = q.shape[1]
    tq = min(256, t)
    nq = t // tq
    tk = min(ATTN_BWD_KEY_CHUNK, t)
    nk = t // tk
    hd = ATTN_HEAD_DIM
    scale = hd ** -0.5

    def body(q_ref, qt_ref, k_ref, v_ref, do_ref, o_ref, lse_ref, dq_ref, dkt_ref, dvt_ref):
        p, i = pl.program_id(0), pl.program_id(1)

        @pl.when(jnp.logical_and(p % 2 == 0, i == 0))
        def _():
            dkt_ref[...] = jnp.zeros_like(dkt_ref)
            dvt_ref[...] = jnp.zeros_like(dvt_ref)

        dov, ov = do_ref[...], o_ref[...]
        dovt = dov.T
        lanes = lambda col: jnp.concatenate([col] * (tk // 128), axis=1)
        outs = []
        for j in range(2):
            qq, qqt = q_ref[j], qt_ref[j]
            do32 = dov[:, j * hd:(j + 1) * hd]
            do, dot_ = do32.astype(BF16), dovt[j * hd:(j + 1) * hd, :].astype(BF16)
            dd = lanes(jnp.broadcast_to(jnp.sum(do32 * ov[:, j * hd:(j + 1) * hd], axis=1, keepdims=True), (tq, 128)))
            lse_j = lanes(jnp.broadcast_to(lse_ref[j], (128, tq)).T)
            dq = jnp.zeros((tq, hd), F32)
            for c in range(nk):
                sl = slice(c * tk, (c + 1) * tk)
                kc, vc = k_ref[0, sl, :], v_ref[0, sl, :]
                pr = jnp.exp(_dot(qq, kc, NT) - lse_j)
                ds = (pr * (_dot(do, vc, NT) - dd)).astype(BF16)
                dvt_ref[0, :, sl] += _dot(dot_, pr.astype(BF16))
                dkt_ref[0, :, sl] += _dot(qqt, ds)
                dq = dq + _dot(ds, kc)
            outs.append(dq * scale)
        dq_ref[...] = jnp.concatenate(outs, axis=-1)

    kv = pl.BlockSpec((1, t, hd), lambda p, i: (p // 2, 0, 0))
    kvt = pl.BlockSpec((1, hd, t), lambda p, i: (p // 2, 0, 0))
    pair = pl.BlockSpec((tq, 128), lambda p, i: (i, p))
    first = lambda: jnp.logical_and(pl.program_id(0) == 0, pl.program_id(1) == 0)
    last = lambda: jnp.logical_and(pl.program_id(0) == 3, pl.program_id(1) == nq - 1)
    xi, xo, xs, xscr, xargs = _ex_args(ex)
    return pl.pallas_call(
        _with_exchange(body, 7, 3, 0, ex, first, last), name="attn_bwd", grid=(4, nq),
        in_specs=[pl.BlockSpec((2, tq, hd), lambda p, i: (p, i, 0)), pl.BlockSpec((2, hd, tq), lambda p, i: (p, 0, i)),
                  kv, kv, pair, pair, pl.BlockSpec((2, 1, tq), lambda p, i: (p, 0, i))] + xi,
        out_specs=[pair, kvt, kvt] + xo,
        out_shape=[SDS((t, ATTN_WIDTH), F32), SDS((ATTN_KV_HEADS, hd, t), F32),
                   SDS((ATTN_KV_HEADS, hd, t), F32)] + xs,
        scratch_shapes=xscr,
        compiler_params=_params(("arbitrary", "arbitrary")),
    )(q, qt, k, v, doa, oa, lse, *xargs)


def _attn_post_bwd(dq, dk, dv, z, qn, kn, cos, sin, ones_bd):
    t = z.shape[0]
    tm = min(512, t)
    n = t // tm
    hd = ATTN_HEAD_DIM

    def body(dq_ref, dk_ref, dv_ref, zq_ref, zkv_ref, qn_ref, kn_ref, c_ref, s_ref, b_ref,
             dz_ref, dqn_ref, dkn_ref, acc_q, acc_k):
        i = pl.program_id(0)

        @pl.when(i == 0)
        def _():
            acc_q[...] = jnp.zeros_like(acc_q)
            acc_k[...] = jnp.zeros_like(acc_k)

        bd = b_ref[...]
        c2, s2 = c_ref[...], s_ref[...]

        def norm_bwd(dy, x, w, ones, cos_t, sin_t, acc):
            dyr = _rope_bwd(dy, cos_t, sin_t, hd // 4)
            r = lax.rsqrt(_group_mean(x * x, ones) + EPS)
            xh = x * r
            gy = dyr * w
            acc[...] += jnp.sum((dyr * xh).reshape(tm // 8, 8, x.shape[-1]), axis=0)
            return r * (gy - xh * _group_mean(gy * xh, ones))

        cq = jnp.concatenate([c2] * 4, axis=-1)
        sq = jnp.concatenate([s2] * 4, axis=-1)
        dz_ref[:, :512] = norm_bwd(dq_ref[...], zq_ref[...], qn_ref[...], bd, cq, sq, acc_q).astype(BF16)
        zkv = zkv_ref[...]
        dkk = jnp.concatenate([dk_ref[0], dk_ref[1]], axis=0).T
        dz_ref[:, 512:640] = norm_bwd(dkk, zkv[:, :128], kn_ref[...], bd[:128, :128], c2, s2, acc_k).astype(BF16)
        dz_ref[:, 640:768] = jnp.concatenate([dv_ref[0], dv_ref[1]], axis=0).T.astype(BF16)

        @pl.when(i == n - 1)
        def _():
            dqn_ref[...] = jnp.sum(acc_q[...], axis=0, keepdims=True)
            dkn_ref[...] = jnp.sum(acc_k[...], axis=0, keepdims=True)

    kv_blk = SEG["ka"][2] // 256
    kvs = pl.BlockSpec((ATTN_KV_HEADS, hd, tm), lambda i: (0, 0, i))
    const = lambda shape: pl.BlockSpec(shape, lambda i: (0, 0))
    return pl.pallas_call(
        body, name="attn_post_bwd", grid=(n,),
        in_specs=[pl.BlockSpec((tm, 512), lambda i: (i, 0)), kvs, kvs,
                  pl.BlockSpec((tm, 512), lambda i: (i, 0)), pl.BlockSpec((tm, 256), lambda i: (i, kv_blk)),
                  const((1, 512)), const((1, 128)),
                  pl.BlockSpec((tm, 128), lambda i: (i, 0)), pl.BlockSpec((tm, 128), lambda i: (i, 0)),
                  const((512, 512))],
        out_specs=[pl.BlockSpec((tm, 768), lambda i: (i, 0)), const((1, 512)), const((1, 128))],
        out_shape=[SDS((t, 768), BF16), SDS((1, 512), F32), SDS((1, 128), F32)],
        scratch_shapes=[pltpu.VMEM((8, 512), F32), pltpu.VMEM((8, 128), F32)],
        compiler_params=_params(("arbitrary",)),
    )(dq, dk, dv, z, z, qn, kn, cos, sin, ones_bd)


def _in_bwd(dxo, x, g, w_t, dz_a, dz_m, dqr, dkr, dvr, after=None):
    t, d = x.shape
    tm = min(256, t)
    n = t // tm
    parts = [(0, 0, 768, 0), (1, 0, 512, SEG["ga"][0]), (2, 0, 512, SEG["qr"][0]), (3, 0, 512, SEG["kr"][0]),
             (4, 0, 512, SEG["vr"][0]), (1, 512, 2560, SEG["gr"][0])]

    def body(dx_ref, x_ref, g_ref, w_ref, a_ref, m_ref, q_ref, k_ref, v_ref, o_ref, dg_ref, acc):
        i = pl.program_id(0)

        @pl.when(i == 0)
        def _():
            acc[...] = jnp.zeros_like(acc)

        pieces = [a_ref, m_ref, q_ref, k_ref, v_ref]
        dh = jnp.zeros((tm, d), F32)
        for pi, lo, w, row in parts:
            dh = dh + _dot(pieces[pi][:, lo:lo + w], w_ref[row:row + w, :])
        xv = x_ref[...]
        r = lax.rsqrt(jnp.mean(xv * xv, axis=-1, keepdims=True) + EPS)
        xh = xv * r
        gy = dh * g_ref[...]
        o_ref[...] = dx_ref[...] + r * (gy - xh * jnp.mean(gy * xh, axis=-1, keepdims=True))
        acc[...] += jnp.sum((dh * xh).reshape(tm // 8, 8, d), axis=0)

        @pl.when(i == n - 1)
        def _():
            dg_ref[...] = jnp.sum(acc[...], axis=0, keepdims=True)

    row = lambda w: pl.BlockSpec((tm, w), lambda i: (i, 0))
    const = lambda shape: pl.BlockSpec(shape, lambda i: (0, 0))
    extra = [] if after is None else [after]
    return pl.pallas_call(
        (lambda *refs: body(*refs[:9], *refs[9 + len(extra):])), name="in_bwd", grid=(n,),
        in_specs=[row(d), row(d), const((1, d)), const((D_IN, d)), row(768), row(3072), row(512), row(512),
                  row(512)] + [const(a.shape) for a in extra],
        out_specs=[row(d), const((1, d))],
        out_shape=[SDS((t, d), F32), SDS((1, d), F32)],
        scratch_shapes=[pltpu.VMEM((8, d), F32)],
        compiler_params=_params(("arbitrary",)),
    )(dxo, x, g, w_t, dz_a, dz_m, dqr, dkr, dvr, *extra)


def _dw_in(h_t, dz_a, dz_m, dqr, dkr, dvr):
    d, t = h_t.shape
    tn = 256
    parts = [(0, 0, 0, 3), (1, 0, SEG["ga"][0] // tn, 2), (2, 0, SEG["qr"][0] // tn, 2),
             (3, 0, SEG["kr"][0] // tn, 2), (4, 0, SEG["vr"][0] // tn, 2), (1, 2, SEG["gr"][0] // tn, 10)]
    pieces = [dz_a, dz_m, dqr, dkr, dvr]

    def col_block(pi):
        mine = [(c0, r0, n) for q, c0, r0, n in parts if q == pi]

        def index(j):
            c0, r0, n = mine[0]
            blk = c0 + jnp.clip(j - r0, 0, n - 1)
            for c0, r0, n in mine[1:]:
                blk = jnp.where(j >= r0, c0 + jnp.clip(j - r0, 0, n - 1), blk)
            return 0, blk

        return index

    def body(h_ref, *refs):
        o_ref = refs[-1]
        j = pl.program_id(0)
        for pi, _, r0, n in parts:
            @pl.when(jnp.logical_and(j >= r0, j < r0 + n))
            def _(p_ref=refs[pi]):
                o_ref[...] = _dot(h_ref[...], p_ref[...]).T.astype(BF16)

    return pl.pallas_call(
        body, name="dw_in", grid=(D_IN // tn,),
        in_specs=[pl.BlockSpec((d, t), lambda j: (0, 0))] + [pl.BlockSpec((t, tn), col_block(pi)) for pi in range(5)],
        out_specs=pl.BlockSpec((tn, d), lambda j: (j, 0)),
        out_shape=SDS((D_IN, d), BF16),
        compiler_params=_params(("arbitrary",)),
    )(h_t, *pieces)


def _adamw_math(w, g, m, v):
    mn = ADAM_B1 * m + (1.0 - ADAM_B1) * g
    vn = ADAM_B2 * v + (1.0 - ADAM_B2) * (g * g)
    m_hat = mn / (1.0 - ADAM_B1 ** ADAM_STEP)
    v_hat = vn / (1.0 - ADAM_B2 ** ADAM_STEP)
    return -ADAM_LR * (m_hat / (jnp.sqrt(v_hat) + ADAM_EPS) + ADAM_WD * w), mn, vn


def _sum_adamw(recvs, w, m, v, lane0, tn, layer0=0, prev=None, own=None):
    _, r, c = w.shape
    j0 = lane0 // tn
    n = len(recvs)
    has_own = own is not None

    def body(*refs):
        mine_ref, refs = (refs[0], refs[1:]) if has_own else (None, refs)
        w_ref, m_ref, v_ref = refs[n:n + 3]
        g_ref, d_ref, mo_ref, vo_ref = refs[-4:]

        def run(r_ref):
            def slot(s):
                if has_own:
                    return jnp.where(mine_ref[0] == s, refs[n + 3][...], r_ref[s]).astype(F32)
                return r_ref[s].astype(F32)

            g = slot(0)
            for s in range(1, N_DEV):
                g = g + slot(s)
            g_ref[0] = g
            d_ref[0], mo_ref[0], vo_ref[0] = _adamw_math(w_ref[0], g, m_ref[0], v_ref[0])

        for i in range(n):
            pl.when(pl.program_id(0) == i)(functools.partial(run, refs[i]))

    slots = pl.BlockSpec((N_DEV, r, tn), lambda i, j, *_: (0, 0, j0 + j))
    blk = pl.BlockSpec((1, r, tn), lambda i, j, *_: (layer0 + i, 0, j))
    before = [] if prev is None else list(prev)
    in_specs, args = [slots] * n + [blk] * 3, [*recvs, w, m, v]
    if has_own:
        assert n == 1
        in_specs.append(pl.BlockSpec((r, tn), lambda i, j, mine: (mine[0], j0 + j)))
        args.append(own[0])
    n_pre = len(args) + has_own
    return pl.pallas_call(
        body, name="sum_adamw",
        grid_spec=pltpu.PrefetchScalarGridSpec(
            num_scalar_prefetch=int(has_own), grid=(n, c // tn),
            in_specs=in_specs + [ANY] * len(before), out_specs=[blk] * 4),
        out_shape=[SDS(w.shape, F32)] * 4,
        input_output_aliases={n_pre + k: k for k in range(len(before))},
        compiler_params=_params(("parallel", "parallel")),
    )(*([own[1]] if has_own else []), *args, *before)


def _adamw(w, g, m, v):
    rows, cols = w.shape
    tr = 256 if rows % 256 == 0 else rows

    def body(w_ref, g_ref, m_ref, v_ref, d_ref, mo_ref, vo_ref):
        d_ref[...], mo_ref[...], vo_ref[...] = _adamw_math(w_ref[...], g_ref[...], m_ref[...], v_ref[...])

    blk = pl.BlockSpec((tr, cols), lambda i: (i, 0))
    return pl.pallas_call(
        body, name="adamw", grid=(rows // tr,),
        in_specs=[blk] * 4, out_specs=[blk] * 3, out_shape=[SDS((rows, cols), F32)] * 3,
        compiler_params=_params(("parallel",)),
    )(w, g, m, v)


def _all_gather(shards):
    na = len(shards)
    chips = (4, 2, 6)

    def body(*refs):
        ins, outs = refs[:na], refs[na:2 * na]
        send_sems, recv_sems, local_sems = refs[2 * na:]
        _, mine = _flip(0)

        def rows(a, idx):
            r = shards[a].shape[0]
            return outs[a].at[pl.ds(pl.multiple_of(idx * r, 16), r), :]

        def copy(a, slot, block_idx, to, src=None):
            return pltpu.make_async_remote_copy(
                src_ref=rows(a, block_idx) if src is None else src, dst_ref=rows(a, block_idx),
                send_sem=send_sems.at[a, slot], recv_sem=recv_sems.at[a, slot],
                device_id=to, device_id_type=MESH_ID)

        sibling, sibling_idx = _flip(1)
        local, started = [], []
        for a in range(na):
            cp = pltpu.make_async_copy(ins[a], rows(a, mine), local_sems.at[a])
            cp.start()
            local.append(cp)
            first = [copy(a, 0, mine, sibling, src=ins[a])]
            first += [copy(a, 1 + j, mine, _flip(k)[0], src=ins[a]) for j, k in enumerate(chips)]
            for cp in first:
                cp.start()
            started += first
        for a in range(na):
            for j, k in enumerate(chips):
                _, theirs = _flip(k)
                copy(a, 1 + j, theirs, _flip(0)[0]).wait_recv()
                fwd = copy(a, 4 + j, theirs, sibling)
                fwd.start()
                started.append(fwd)
        for a in range(na):
            copy(a, 0, sibling_idx, _flip(0)[0]).wait_recv()
            for j, k in enumerate(chips):
                _, theirs = _flip(k | 1)
                copy(a, 4 + j, theirs, _flip(0)[0]).wait_recv()
        for cp in started:
            cp.wait_send()
        for cp in local:
            cp.wait()

    return pl.pallas_call(
        body, name="all_gather_weights",
        in_specs=[ANY] * na, out_specs=[ANY] * na,
        out_shape=[SDS((N_DEV * s.shape[0], s.shape[1]), s.dtype) for s in shards],
        scratch_shapes=[pltpu.SemaphoreType.DMA((na, 7)), pltpu.SemaphoreType.DMA((na, 7)),
                        pltpu.SemaphoreType.DMA((na,))],
        compiler_params=pltpu.CompilerParams(has_side_effects=True),
    )(*shards)


def _scatter_blocks_of(g_ref, rows, idx):
    return g_ref.at[pl.ds(pl.multiple_of(idx * rows, 16), rows), :]


def _scatter_start(g):
    rows = g.shape[0] // N_DEV
    land_shape = (N_DEV, rows, g.shape[1])

    def body(g_ref, land_ref, send_sems, recv_sems, g_thru, land_thru, token):
        _, mine = _flip(0)
        for k in range(1, N_DEV):
            peer, theirs = _flip(k)
            pltpu.make_async_remote_copy(
                src_ref=_scatter_blocks_of(g_ref, rows, theirs), dst_ref=land_ref.at[mine],
                send_sem=send_sems.at[k - 1], recv_sem=recv_sems.at[k - 1],
                device_id=peer, device_id_type=MESH_ID).start()
        token[...] = jnp.zeros_like(token)

    hbm, sem = pl.BlockSpec(memory_space=pltpu.HBM), pl.BlockSpec(memory_space=pltpu.SEMAPHORE)
    return pl.pallas_call(
        body, name="scatter_start",
        out_shape=(pltpu.SemaphoreType.DMA((N_DEV - 1,)), pltpu.SemaphoreType.DMA((N_DEV - 1,)),
                   pltpu.HBM(g.shape, g.dtype), pltpu.HBM(land_shape, g.dtype), SDS((8, 128), F32)),
        in_specs=(hbm, hbm), out_specs=(sem, sem, hbm, hbm, pl.BlockSpec(memory_space=pltpu.VMEM)),
        input_output_aliases={0: 2, 1: 3},
        compiler_params=pltpu.CompilerParams(has_side_effects=pltpu.SideEffectType.DATAFLOW_SIDE_EFFECTING),
    )(pltpu.with_memory_space_constraint(g, pltpu.HBM),
      pltpu.with_memory_space_constraint(lax.empty(land_shape, g.dtype), pltpu.HBM))


def _scatter_wait(send_sems, recv_sems, g_thru, land_thru, after):
    rows = g_thru.shape[0] // N_DEV

    def body(g_ref, land_ref, send_sems, recv_sems, *rest):
        me, _ = _flip(0)
        for k in range(1, N_DEV):
            _, theirs = _flip(k)
            copy = pltpu.make_async_remote_copy(
                src_ref=_scatter_blocks_of(g_ref, rows, theirs), dst_ref=land_ref.at[theirs],
                send_sem=send_sems.at[k - 1], recv_sem=recv_sems.at[k - 1],
                device_id=me, device_id_type=MESH_ID)
            copy.wait_send()
            copy.wait_recv()

    hbm, sem = pl.BlockSpec(memory_space=pltpu.HBM), pl.BlockSpec(memory_space=pltpu.SEMAPHORE)
    return pl.pallas_call(
        body, name="scatter_wait",
        out_shape=(pltpu.HBM(g_thru.shape, g_thru.dtype), pltpu.HBM(land_thru.shape, land_thru.dtype)),
        in_specs=(hbm, hbm, sem, sem) + (ANY,) * len(after), out_specs=(hbm, hbm), input_output_aliases={0: 0, 1: 1},
        compiler_params=pltpu.CompilerParams(has_side_effects=pltpu.SideEffectType.DATAFLOW_SIDE_EFFECTING),
    )(g_thru, land_thru, send_sems, recv_sems, *after)


def _all_reduce_small(packed):
    shape = packed.shape

    def body(p_ref, o_ref, slots, send_sems, recv_sems):
        me, mine = _flip(0)
        slots[mine] = p_ref[...]
        sends = []
        for k in range(1, N_DEV):
            peer, _ = _flip(k)
            cp = pltpu.make_async_remote_copy(
                src_ref=p_ref, dst_ref=slots.at[mine], send_sem=send_sems.at[k - 1], recv_sem=recv_sems.at[k - 1],
                device_id=peer, device_id_type=MESH_ID)
            cp.start()
            sends.append(cp)
        for k in range(1, N_DEV):
            _, theirs = _flip(k)
            pltpu.make_async_remote_copy(
                src_ref=p_ref, dst_ref=slots.at[theirs], send_sem=send_sems.at[k - 1],
                recv_sem=recv_sems.at[k - 1], device_id=me, device_id_type=MESH_ID).wait_recv()
        for cp in sends:
            cp.wait_send()
        acc = slots[0]
        for s in range(1, N_DEV):
            acc = acc + slots[s]
        o_ref[...] = acc

    vm = pl.BlockSpec(memory_space=pltpu.VMEM)
    return pl.pallas_call(
        body, name="all_reduce_small", in_specs=[vm], out_specs=vm, out_shape=SDS(shape, F32),
        scratch_shapes=[pltpu.VMEM((N_DEV,) + shape, F32), pltpu.SemaphoreType.DMA((7,)),
                        pltpu.SemaphoreType.DMA((7,))],
        compiler_params=pltpu.CompilerParams(has_side_effects=True),
    )(packed)


def _layer_fwd(x, p, tabs, ex):
    z, h_t = _in_proj(x, p["norm_g"], p["w_in_t"])
    q, qt, k, v, vt = _attn_prep(z, p["qn"], p["kn"], tabs["ca"], tabs["sa"], tabs["ones"])
    oa, lse, *gathered = _attn_fwd(q, k, vt, ex)
    qrot, krot, vb, orr, on = _ret_fwd(z, p["lgf"], p["lgb"], p["gnw"], tabs["cr"], tabs["sr"])
    return z, h_t, q, qt, k, v, lse, oa, qrot, krot, vb, orr, on, gathered


def _layer_bwd(dxo, s, p, tabs, ex_attn, scatter_w_in):
    doa, don, dz_m, d_wout, d_wb_t = _merge_bwd(dxo, s["z"], s["oa"], s["on"], s["ya"], s["yb"], p["wb_t"], p["w_out"])
    dq_a, dk_a, dv_a, *recv_attn = _attn_bwd(s["q"], s["qt"], s["k"], s["v"], doa, s["oa"], s["lse"],
                                              ex_attn(d_wb_t, d_wout))
    dz_a, d_qn, d_kn = _attn_post_bwd(dq_a, dk_a, dv_a, s["z"], p["qn"], p["kn"], tabs["ca"], tabs["sa"],
                                      tabs["ones"])
    dq_r, dk_r, dv_r, d_gnw, d_lgf, d_lgb = _ret_bwd(s["qrot"], s["krot"], s["vb"], s["orr"], don, p["gnw"],
                                                     p["lgf"], p["lgb"])
    dqr, dkr, dvr = _ret_post_bwd(dq_r, dk_r, dv_r, tabs["cr"], tabs["sr"])
    buf = _dw_in(s["h_t"], dz_a, dz_m, dqr, dkr, dvr)
    pending, token = None, None
    if scatter_w_in:
        *pending, token = _scatter_start(buf)
    dx, d_norm_g = _in_bwd(dxo, s["x"], p["norm_g"], p["w_in_t"], dz_a, dz_m, dqr, dkr, dvr, token)
    grads = dict(w_in_t=buf, wb_t=d_wb_t, w_out=d_wout, norm_g=d_norm_g, gnw=d_gnw,
                 qn=d_qn.reshape(ATTN_Q_HEADS, ATTN_HEAD_DIM).sum(axis=0),
                 kn=d_kn.reshape(ATTN_KV_HEADS, ATTN_HEAD_DIM).sum(axis=0),
                 lgf=d_lgf[:, 0, 0], lgb=d_lgb[:, 0, 0])
    return dx, grads, recv_attn, pending


def _adamw_nd(w, g, m, v):
    shape = w.shape
    two_d = (1, shape[0]) if w.ndim == 1 else (-1, shape[-1])
    out = _adamw(w.reshape(two_d), g.reshape(two_d), m.reshape(two_d), v.reshape(two_d))
    return tuple(o.reshape(shape) for o in out)


def kernel(x, norm_g, w_in, attn_q_norm, attn_k_norm, ret_decay_fwd, ret_decay_bwd, ret_gn_w, w_branch_attn, w_branch_ret, w_out, final_norm_g, loss_target, m_norm_g, m_w_in, m_attn_q_norm, m_attn_k_norm, m_ret_decay_fwd, m_ret_decay_bwd, m_ret_gn_w, m_w_branch_attn, m_w_branch_ret, m_w_out, m_final_norm_g, v_norm_g, v_w_in, v_attn_q_norm, v_attn_k_norm, v_ret_decay_fwd, v_ret_decay_bwd, v_ret_gn_w, v_w_branch_attn, v_w_branch_ret, v_w_out, v_final_norm_g):
    t, d = x.shape[1], x.shape[2]
    x2, target = x[0], loss_target[0]

    w_in_sh, wb_sh, wout_sh = [], [], []
    for l in range(DEPTH):
        w_in_sh.append(jnp.swapaxes(w_in[l], 0, 1).astype(BF16))
        wb_sh.append(jnp.concatenate([w_branch_attn[l].T, w_branch_ret[l].T], axis=1).astype(BF16))
        wout_sh.append(w_out[l].astype(BF16))

    ca, sa = _rope_tables(t, ATTN_HEAD_DIM)
    cr, sr = _rope_tables(t, RET_HEAD_DIM)
    grp = jnp.arange(ATTN_WIDTH) // ATTN_HEAD_DIM
    tabs = dict(ca=jnp.tile(ca, (1, 2)), sa=jnp.tile(sa, (1, 2)), cr=cr, sr=sr,
                ones=jnp.where(grp[:, None] == grp[None, :], 1.0 / ATTN_HEAD_DIM, 0.0).astype(BF16))
    layers = []
    for l in range(DEPTH):
        layers.append(dict(
            norm_g=norm_g[l][None], qn=jnp.tile(attn_q_norm[l], ATTN_Q_HEADS)[None],
            kn=jnp.tile(attn_k_norm[l], ATTN_KV_HEADS)[None], gnw=ret_gn_w[l][None],
            lgf=jax.nn.log_sigmoid(ret_decay_fwd[l]), lgb=jax.nn.log_sigmoid(ret_decay_bwd[l])))

    layers[0]["w_in_t"], = _all_gather([w_in_sh[0]])
    gathers = [_Exchange("gather", [wb_sh[0], wout_sh[0], w_in_sh[1]]), _Exchange("gather", [wb_sh[1], wout_sh[1]])]
    h = x2
    saved = []
    for l in range(DEPTH):
        p = layers[l]
        z, h_t, q, qt, k, v, lse, oa, qrot, krot, vb, orr, on, got = _layer_fwd(h, p, tabs, gathers[l])
        p["wb_t"], p["w_out"] = got[0], got[1]
        if l == 0:
            layers[1]["w_in_t"] = got[2]
        xn, ya, yb = _merge_fwd(h, z, oa, on, p["wb_t"], p["w_out"])
        saved.append(dict(x=h, z=z, h_t=h_t, q=q, qt=qt, k=k, v=v, lse=lse, oa=oa, qrot=qrot, krot=krot, vb=vb,
                          orr=orr, on=on, ya=ya, yb=yb))
        h = xn
    dx, d_final_g, loss_part = _final_loss(h, final_norm_g[None], target)

    grads = [None] * DEPTH
    dx, grads[1], _, _ = _layer_bwd(dx, saved[1], layers[1], tabs, lambda *a: None, False)
    g1 = grads[1]
    ex_attn = lambda d_wb_t, d_wout: _Exchange("scatter", [g1["w_in_t"], g1["wb_t"], g1["w_out"], d_wb_t, d_wout])
    dx, grads[0], recv_attn, pending = _layer_bwd(dx, saved[0], layers[0], tabs, ex_attn, True)
    recv = [None, recv_attn[3], recv_attn[4], recv_attn[0], recv_attn[1], recv_attn[2]]
    tr = lambda a: jnp.swapaxes(a, 1, 2)
    w_in_t = (tr(w_in), tr(m_w_in), tr(v_w_in))
    sharded = {}
    w_in_l1 = _sum_adamw([recv[3]], *w_in_t, 0, 256, layer0=1)
    sharded[id(w_branch_attn)] = [tr(o) for o in _sum_adamw(
        [recv[1], recv[4]], tr(w_branch_attn), tr(m_w_branch_attn), tr(v_w_branch_attn), 0, 512)]
    sharded[id(w_branch_ret)] = [tr(o) for o in _sum_adamw(
        [recv[1], recv[4]], tr(w_branch_ret), tr(m_w_branch_ret), tr(v_w_branch_ret), 512, 512)]
    sharded[id(w_out)] = _sum_adamw([recv[2], recv[5]], w_out, m_w_out, v_w_out, 0, 256)
    g_wba, g_wbr, g_wout = (sharded[id(w)][0] for w in (w_branch_attn, w_branch_ret, w_out))

    packed = jnp.zeros((8, 1024), F32)
    for l in range(DEPTH):
        gl = grads[l]
        packed = packed.at[l].set(gl["norm_g"][0])
        packed = packed.at[2, 512 * l:512 * (l + 1)].set(gl["gnw"][0])
        packed = packed.at[4, 128 * l:128 * l + 64].set(gl["qn"])
        packed = packed.at[4, 256 + 128 * l:256 + 128 * l + 64].set(gl["kn"])
        packed = packed.at[4, 512 + 128 * l:512 + 128 * l + 4].set(gl["lgf"])
        packed = packed.at[4, 768 + 128 * l:768 + 128 * l + 4].set(gl["lgb"])
    packed = packed.at[3].set(d_final_g[0])
    packed = packed.at[5, 0].set(loss_part[0, 0])
    red = _all_reduce_small(packed)
    loss = red[5, 0]
    g_norm_g = red[0:2]
    g_gnw = red[2].reshape(DEPTH, RET_WIDTH)
    g_final = red[3]
    g_qn = jnp.stack([red[4, 128 * l:128 * l + 64] for l in range(DEPTH)])
    g_kn = jnp.stack([red[4, 256 + 128 * l:256 + 128 * l + 64] for l in range(DEPTH)])
    g_lgf = jnp.stack([red[4, 512 + 128 * l:512 + 128 * l + 4] for l in range(DEPTH)])
    g_lgb = jnp.stack([red[4, 768 + 128 * l:768 + 128 * l + 4] for l in range(DEPTH)])
    g_df = g_lgf * jax.nn.sigmoid(-ret_decay_fwd)
    g_db = g_lgb * jax.nn.sigmoid(-ret_decay_bwd)

    grad_w = [g_norm_g, None, g_qn, g_kn, g_df, g_db, g_gnw, g_wba, g_wbr, g_wout, g_final]
    weights = [norm_g, w_in, attn_q_norm, attn_k_norm, ret_decay_fwd, ret_decay_bwd, ret_gn_w, w_branch_attn,
               w_branch_ret, w_out, final_norm_g]
    ms = [m_norm_g, m_w_in, m_attn_q_norm, m_attn_k_norm, m_ret_decay_fwd, m_ret_decay_bwd, m_ret_gn_w,
          m_w_branch_attn, m_w_branch_ret, m_w_out, m_final_norm_g]
    vs = [v_norm_g, v_w_in, v_attn_q_norm, v_attn_k_norm, v_ret_decay_fwd, v_ret_decay_bwd, v_ret_gn_w,
          v_w_branch_attn, v_w_branch_ret, v_w_out, v_final_norm_g]
    upd = [None if w is w_in else sharded[id(w)][1:] if id(w) in sharded else _adamw_nd(w, g, m, v)
           for w, g, m, v in zip(weights, grad_w, ms, vs)]

    done = [dx, w_in_l1[0], g_wout] + [u[0] for w, u in zip(weights, upd) if u is not None and id(w) not in sharded]
    g_full, recv[0] = _scatter_wait(*pending, done)
    mine = (4 * lax.axis_index("x") + 2 * lax.axis_index("y") + lax.axis_index("c")).astype(jnp.int32)[None]
    w_in_upd = [tr(o) for o in _sum_adamw([recv[0]], *w_in_t, 0, 256, layer0=0, prev=w_in_l1, own=(g_full, mine))]
    grad_w[1], upd[1] = w_in_upd[0], w_in_upd[1:]
    return (loss, dx[None], *grad_w, *[u[0] for u in upd], *[u[1] for u in upd], *[u[2] for u in upd])
```

```python
import functools

import jax
import jax.numpy as jnp
from jax import lax
from jax.experimental import pallas as pl
from jax.experimental.pallas import tpu as pltpu

F32 = jnp.float32
BF16 = jnp.bfloat16
SDS = jax.ShapeDtypeStruct

D_MODEL = 1024
DEPTH = 2
GRID_W = 64
ATTN_Q_HEADS = 8
ATTN_KV_HEADS = 2
ATTN_HEAD_DIM = 64
ATTN_WIDTH = 512
ATTN_KV_WIDTH = 128
RET_HEADS = 4
RET_HEAD_DIM = 128
RET_WIDTH = 512
RET_CHUNK = 128
ATTN_KEY_CHUNK = 512
ATTN_BWD_KEY_CHUNK = 1024
QK_DOTS_PER_CHUNK = 4
EXP_LAG = 3
ROPE_THETA = 10000.0
EPS = 1e-6
D_IN = 5376
N_DEV = 8

ADAM_LR = 0.001
ADAM_B1 = 0.9
ADAM_B2 = 0.999
ADAM_EPS = 1e-08
ADAM_WD = 0.01
ADAM_STEP = 10

SEG = {
    "qa": (0, 512, 0),
    "ga": (768, 512, 512),
    "qr": (1280, 512, 1024),
    "kr": (1792, 512, 1536),
    "vr": (2304, 512, 2048),
    "gr": (2816, 512, 2560),
    "gm": (3328, 2048, 3072),
    "ka": (512, 128, 5120),
    "va": (640, 128, 5248),
}

VMEM_LIMIT = 60 * 1024 * 1024
NT = (((1,), (1,)), ((), ()))
TN = (((0,), (0,)), ((), ()))
MESH_ID = pl.DeviceIdType.MESH
ANY = pl.BlockSpec(memory_space=pl.ANY)


def _params(sem=None, vmem=VMEM_LIMIT):
    return pltpu.CompilerParams(dimension_semantics=sem, vmem_limit_bytes=vmem)


def _dot(a, b, dims=None):
    if dims is None:
        return jnp.dot(a, b, preferred_element_type=F32)
    return lax.dot_general(a, b, dims, preferred_element_type=F32)


def _sigmoid(x):
    return 1.0 / (1.0 + jnp.exp(-x))


def _swap_halves(x, q):
    n = x.shape[-1]
    axis = x.ndim - 1
    lane = lax.broadcasted_iota(jnp.int32, x.shape, axis)
    first = (lane % (2 * q)) < q
    return jnp.where(first, pltpu.roll(x, n - q, axis), pltpu.roll(x, q, axis))


def _rope(x, cos, sin_signed, q):
    return x * cos + _swap_halves(x, q) * sin_signed


def _rope_bwd(dy, cos, sin_signed, q):
    return dy * cos - _swap_halves(dy, q) * sin_signed


def _group_mean(v, ones_bd):
    hi = v.astype(BF16)
    r1 = v - hi.astype(F32)
    mid = r1.astype(BF16)
    lo = (r1 - mid.astype(F32)).astype(BF16)
    return _dot(hi, ones_bd) + _dot(mid, ones_bd) + _dot(lo, ones_bd)


def _rope_tables(t, head_dim):
    n_rows = t // GRID_W
    d_axis = head_dim // 2
    inv_freq = ROPE_THETA ** (-jnp.arange(0, d_axis, 2, dtype=F32) / d_axis)
    ar = jnp.arange(n_rows, dtype=F32)[:, None] * inv_freq
    ac = jnp.arange(GRID_W, dtype=F32)[:, None] * inv_freq
    by_row = lambda a: jnp.repeat(a, GRID_W, axis=0)
    by_col = lambda a: jnp.tile(a, (n_rows, 1))
    cr, sr, cc, sc = by_row(jnp.cos(ar)), by_row(jnp.sin(ar)), by_col(jnp.cos(ac)), by_col(jnp.sin(ac))
    return jnp.concatenate([cr, cr, cc, cc], axis=-1), jnp.concatenate([-sr, sr, -sc, sc], axis=-1)


def _me():
    return lax.axis_index("x"), lax.axis_index("y"), lax.axis_index("c")


def _flip(k):
    x, y, c = _me()
    px = 1 - x if k & 4 else x
    py = 1 - y if k & 2 else y
    pc = 1 - c if k & 1 else c
    return (px, py, pc), 4 * px + 2 * py + pc


class _Exchange:
    def __init__(self, kind, srcs):
        self.kind, self.srcs, self.n = kind, list(srcs), len(srcs)
        self.rows = [a.shape[0] if kind == "gather" else a.shape[0] // N_DEV for a in srcs]
        if kind == "gather":
            self.out_shape = [SDS((N_DEV * a.shape[0], a.shape[1]), a.dtype) for a in srcs]
        else:
            self.out_shape = [SDS((N_DEV, a.shape[0] // N_DEV, a.shape[1]), a.dtype) for a in srcs]
        self.scratch = [pltpu.SemaphoreType.DMA((self.n, N_DEV - 1)), pltpu.SemaphoreType.DMA((self.n, N_DEV - 1)),
                        pltpu.SemaphoreType.DMA((self.n,))]

    def _block(self, ref, a, idx):
        r = self.rows[a]
        return ref.at[pl.ds(pl.multiple_of(idx * r, 16), r), :]

    def _src(self, ins, a, idx):
        return ins[a] if self.kind == "gather" else self._block(ins[a], a, idx)

    def _dst(self, outs, a, idx):
        return self._block(outs[a], a, idx) if self.kind == "gather" else outs[a].at[idx]

    def _copies(self, ins, outs, sems):
        send_sems, recv_sems, local_sems = sems
        me, mine = _flip(0)
        local, sends, recvs = [], [], []
        for a in range(self.n):
            local.append(pltpu.make_async_copy(self._src(ins, a, mine), self._dst(outs, a, mine), local_sems.at[a]))
            for k in range(1, N_DEV):
                peer, theirs = _flip(k)
                sem = dict(send_sem=send_sems.at[a, k - 1], recv_sem=recv_sems.at[a, k - 1])
                sends.append(pltpu.make_async_remote_copy(
                    src_ref=self._src(ins, a, theirs), dst_ref=self._dst(outs, a, mine),
                    device_id=peer, device_id_type=MESH_ID, **sem))
                recvs.append(pltpu.make_async_remote_copy(
                    src_ref=self._dst(outs, a, theirs), dst_ref=self._dst(outs, a, theirs),
                    device_id=me, device_id_type=MESH_ID, **sem))
        return local, sends, recvs

    def start(self, ins, outs, sems):
        local, sends, _ = self._copies(ins, outs, sems)
        for cp in local + sends:
            cp.start()

    def wait(self, ins, outs, sems):
        local, sends, recvs = self._copies(ins, outs, sems)
        for cp in sends:
            cp.wait_send()
        for cp in recvs:
            cp.wait_recv()
        for cp in local:
            cp.wait()


def _with_exchange(body, n_in, n_out, n_scratch, ex, first, last):
    if ex is None:
        return body

    def wrapped(*refs):
        ins = refs[:n_in]
        ex_ins = refs[n_in:n_in + ex.n]
        outs = refs[n_in + ex.n:n_in + ex.n + n_out]
        ex_outs = refs[n_in + ex.n + n_out:n_in + 2 * ex.n + n_out]
        rest = refs[n_in + 2 * ex.n + n_out:]
        scratch, sems = rest[:n_scratch], rest[n_scratch:]

        @pl.when(first())
        def _():
            ex.start(ex_ins, ex_outs, sems)

        body(*ins, *outs, *scratch)

        @pl.when(last())
        def _():
            ex.wait(ex_ins, ex_outs, sems)

    return wrapped


def _ex_args(ex):
    if ex is None:
        return [], [], [], [], []
    return [ANY] * ex.n, [ANY] * ex.n, list(ex.out_shape), list(ex.scratch), list(ex.srcs)


def _in_proj(x, g, w_t):
    t, d = x.shape
    tm = min(256, t)

    def body(x_ref, g_ref, w_ref, z_ref, ht_ref):
        xv = x_ref[...]
        r = lax.rsqrt(jnp.mean(xv * xv, axis=-1, keepdims=True) + EPS)
        h = xv * r * g_ref[...]
        ht_ref[...] = h.T.astype(BF16)
        hb = h.astype(BF16)
        for nat, w, off in SEG.values():
            z_ref[:, off:off + w] = _dot(hb, w_ref[nat:nat + w, :], NT)

    return pl.pallas_call(
        body, name="in_proj", grid=(t // tm,),
        in_specs=[pl.BlockSpec((tm, d), lambda i: (i, 0)), pl.BlockSpec((1, d), lambda i: (0, 0)),
                  pl.BlockSpec((D_IN, d), lambda i: (0, 0))],
        out_specs=[pl.BlockSpec((tm, D_IN), lambda i: (i, 0)), pl.BlockSpec((d, tm), lambda i: (0, i))],
        out_shape=[SDS((t, D_IN), F32), SDS((d, t), BF16)],
        compiler_params=_params(("parallel",)),
    )(x, g, w_t)


def _attn_prep(z, qn, kn, cos, sin, ones_bd):
    t = z.shape[0]
    tm = min(ATTN_KEY_CHUNK, t)
    hd = ATTN_HEAD_DIM

    def body(zq_ref, zkv_ref, qn_ref, kn_ref, c_ref, s_ref, b_ref, q_out, qt_out, k_out, v_out, vt_out):
        bd = b_ref[...]
        c2, s2 = c_ref[...], s_ref[...]
        cq = jnp.concatenate([c2] * 4, axis=-1)
        sq = jnp.concatenate([s2] * 4, axis=-1)
        xq = zq_ref[...]
        yq = xq * lax.rsqrt(_group_mean(xq * xq, bd) + EPS) * qn_ref[...]
        yq = _rope(yq, cq, sq, hd // 4) * (hd ** -0.5)
        yqt = yq.T
        for h in range(ATTN_Q_HEADS):
            q_out[h] = yq[:, h * hd:(h + 1) * hd].astype(BF16)
            qt_out[h] = yqt[h * hd:(h + 1) * hd, :].astype(BF16)
        zkv = zkv_ref[...]
        xk, xv = zkv[:, :ATTN_KV_WIDTH], zkv[:, ATTN_KV_WIDTH:]
        yk = xk * lax.rsqrt(_group_mean(xk * xk, bd[:ATTN_KV_WIDTH, :ATTN_KV_WIDTH]) + EPS) * kn_ref[...]
        yk = _rope(yk, c2, s2, hd // 4)
        xvt = xv.T
        ones = jnp.ones((hd, tm), F32)
        for h in range(ATTN_KV_HEADS):
            k_out[h] = yk[:, h * hd:(h + 1) * hd].astype(BF16)
            v_out[h] = xv[:, h * hd:(h + 1) * hd].astype(BF16)
            vt_out[h, 0] = jnp.concatenate([xvt[h * hd:(h + 1) * hd, :], ones], axis=0).astype(BF16)

    kv_blk = SEG["ka"][2] // 256
    nk = t // tm
    return pl.pallas_call(
        body, name="attn_prep", grid=(nk,),
        in_specs=[pl.BlockSpec((tm, 512), lambda i: (i, 0)), pl.BlockSpec((tm, 256), lambda i: (i, kv_blk)),
                  pl.BlockSpec((1, 512), lambda i: (0, 0)), pl.BlockSpec((1, 128), lambda i: (0, 0)),
                  pl.BlockSpec((tm, 128), lambda i: (i, 0)), pl.BlockSpec((tm, 128), lambda i: (i, 0)),
                  pl.BlockSpec((512, 512), lambda i: (0, 0))],
        out_specs=[pl.BlockSpec((ATTN_Q_HEADS, tm, hd), lambda i: (0, i, 0)),
                   pl.BlockSpec((ATTN_Q_HEADS, hd, tm), lambda i: (0, 0, i)),
                   pl.BlockSpec((ATTN_KV_HEADS, tm, hd), lambda i: (0, i, 0)),
                   pl.BlockSpec((ATTN_KV_HEADS, tm, hd), lambda i: (0, i, 0)),
                   pl.BlockSpec((ATTN_KV_HEADS, 1, 2 * hd, tm), lambda i: (0, i, 0, 0))],
        out_shape=[SDS((ATTN_Q_HEADS, t, hd), BF16), SDS((ATTN_Q_HEADS, hd, t), BF16),
                   SDS((ATTN_KV_HEADS, t, hd), BF16), SDS((ATTN_KV_HEADS, t, hd), BF16),
                   SDS((ATTN_KV_HEADS, nk, 2 * hd, tm), BF16)],
        compiler_params=_params(("parallel",)),
    )(z, z, qn, kn, cos, sin, ones_bd)


def _attn_fwd(q, k, vt, ex=None):
    t = q.shape[1]
    tq = min(256, t)
    nk, tk = vt.shape[1], vt.shape[3]
    hd = ATTN_HEAD_DIM
    g = ATTN_Q_HEADS // ATTN_KV_HEADS

    def body(q_ref, k_ref, vt_ref, o_ref, lse_ref, s_scr):
        def pass_a(h, c, m8):
            part = tk // QK_DOTS_PER_CHUNK
            for lo in range(c * tk, (c + 1) * tk, part):
                st = _dot(k_ref[0, lo:lo + part, :], q_ref[h], NT)
                s_scr[h % 2, lo:lo + part, :] = st
                m8 = jnp.maximum(m8, jnp.max(st.reshape(part // 8, 8, tq), axis=0))
            return m8

        def pass_b(h, c, m, acc, after):
            e = jnp.exp(s_scr[h % 2, c * tk:(c + 1) * tk, :] - (m + after * 0.0)).astype(BF16)
            return acc + _dot(vt_ref[0, c], e)

        neg = jnp.full((8, tq), -jnp.inf, F32)
        m8 = neg
        for c in range(nk):
            m8 = pass_a(0, c, m8)
        outs = []
        for h in range(g):
            m = jnp.max(m8, axis=0, keepdims=True)
            acc = jnp.zeros((2 * hd, tq), F32)
            m8 = neg
            done = [m] * EXP_LAG
            for c in range(nk):
                if h + 1 < g:
                    m8 = pass_a(h + 1, c, m8)
                acc = pass_b(h, c, m, acc, done[-EXP_LAG])
                done.append(m8[0:1, :] if h + 1 < g else acc[hd:hd + 1, :])
            l = acc[hd:hd + 1, :]
            outs.append((acc[:hd, :] / l).T)
            lse_ref[h] = m + jnp.log(l)
        o_ref[...] = jnp.concatenate(outs, axis=-1)

    nq = t // tq
    first = lambda: jnp.logical_and(pl.program_id(0) == 0, pl.program_id(1) == 0)
    last = lambda: jnp.logical_and(pl.program_id(0) == ATTN_KV_HEADS - 1, pl.program_id(1) == nq - 1)
    xi, xo, xs, xscr, xargs = _ex_args(ex)
    return pl.pallas_call(
        _with_exchange(body, 3, 2, 1, ex, first, last), name="attn_fwd", grid=(ATTN_KV_HEADS, nq),
        in_specs=[pl.BlockSpec((g, tq, hd), lambda p, i: (p, i, 0)),
                  pl.BlockSpec((1, t, hd), lambda p, i: (p, 0, 0)),
                  pl.BlockSpec((1, nk, 2 * hd, tk), lambda p, i: (p, 0, 0, 0))] + xi,
        out_specs=[pl.BlockSpec((tq, g * hd), lambda p, i: (i, p)),
                   pl.BlockSpec((g, 1, tq), lambda p, i: (p, 0, i))] + xo,
        out_shape=[SDS((t, ATTN_WIDTH), F32), SDS((ATTN_Q_HEADS, 1, t), F32)] + xs,
        scratch_shapes=[pltpu.VMEM((2, t, tq), F32)] + xscr,
        compiler_params=_params(("arbitrary", "arbitrary")),
    )(q, k, vt, *xargs)


class _Dir:
    def __init__(self, lg, strict_future):
        c = RET_CHUNK
        ia = lax.broadcasted_iota(jnp.int32, (c, c), 0).astype(F32)
        ib = lax.broadcasted_iota(jnp.int32, (c, c), 1).astype(F32)
        col = lax.broadcasted_iota(jnp.int32, (c, 1), 0).astype(F32)
        row = lax.broadcasted_iota(jnp.int32, (1, c), 1).astype(F32)
        if strict_future:
            dist = ib - ia
            mask = dist > 0
            self.wq, self.wk, wk_row = c - col, col, row
        else:
            dist = ia - ib
            mask = dist >= 0
            self.wq, self.wk, wk_row = col + 1.0, c - 1.0 - col, c - 1.0 - row
        self.dist = jnp.maximum(dist, 0.0)
        self.d = jnp.where(mask, jnp.exp(self.dist * lg), 0.0)
        self.qd = jnp.exp(self.wq * lg)
        self.kd_col = jnp.exp(self.wk * lg)
        self.kd_row = jnp.exp(wk_row * lg)
        self.cd = jnp.exp(jnp.full((1, 1), float(c), F32) * lg)


def _ret_fwd(z, lgf, lgb, gnw, cos, sin):
    t = z.shape[0]
    c = RET_CHUNK
    nc = t // c
    hd = RET_HEAD_DIM
    unroll = 4 if nc % 4 == 0 else 1

    def body(lgf_ref, lgb_ref, q_ref, k_ref, v_ref, c_ref, s_ref, w_ref,
             qo_ref, ko_ref, vo_ref, orr_ref, on_ref, kt, uf, ub, sfa, sba):
        h = pl.program_id(0)
        fw = _Dir(lgf_ref[h], False)
        bw = _Dir(lgb_ref[h], True)
        cc, ss = c_ref[...], s_ref[...]
        qo_ref[...] = _rope(q_ref[...], cc, ss, hd // 4).astype(BF16)
        kr = _rope(k_ref[...], cc, ss, hd // 4) * (hd ** -0.5)
        ko_ref[...] = kr.astype(BF16)
        vo_ref[...] = v_ref[...].astype(BF16)
        for i in range(nc):
            kt[i] = kr[i * c:(i + 1) * c, :].T.astype(BF16)

        def rows(ci):
            return pl.ds(pl.multiple_of(ci * c, c), c)

        def kv_products(ci, carry):
            vv = vo_ref[rows(ci), :]
            ktf = kt[ci].astype(F32)
            uf[ci] = _dot((ktf * fw.kd_row).astype(BF16), vv)
            ub[ci] = _dot((ktf * bw.kd_row).astype(BF16), vv)
            return carry

        lax.fori_loop(0, nc, kv_products, 0, unroll=unroll)

        def scan(i, carry):
            sf, sb = carry
            j = nc - 1 - i
            sfa[i] = sf.astype(BF16)
            sba[j] = sb.astype(BF16)
            return sf * fw.cd + uf[i], sb * bw.cd + ub[j]

        zero = jnp.zeros((hd, hd), F32)
        lax.fori_loop(0, nc, scan, (zero, zero))
        gw = w_ref[...]

        def outputs(ci, carry):
            sl = rows(ci)
            qq, kk, vv = qo_ref[sl, :], ko_ref[sl, :], vo_ref[sl, :]
            a = _dot(qq, kk, NT)
            o = (_dot((a * fw.d).astype(BF16), vv) + _dot(qq, sfa[ci]) * fw.qd
                 + _dot((a * bw.d).astype(BF16), vv) + _dot(qq, sba[ci]) * bw.qd)
            orr_ref[sl, :] = o
            xc = o - jnp.mean(o, axis=-1, keepdims=True)
            var = jnp.mean(xc * xc, axis=-1, keepdims=True)
            on_ref[sl, :] = xc * lax.rsqrt(var + EPS) * gw
            return carry

        lax.fori_loop(0, nc, outputs, 0, unroll=unroll)

    smem = pl.BlockSpec(memory_space=pltpu.SMEM)
    col = lambda name: (lambda h: (0, SEG[name][2] // 128 + h))
    head = pl.BlockSpec((t, 128), lambda h: (0, h))
    full = pl.BlockSpec((t, 128), lambda h: (0, 0))
    return pl.pallas_call(
        body, name="ret_fwd", grid=(RET_HEADS,),
        in_specs=[smem, smem, pl.BlockSpec((t, 128), col("qr")), pl.BlockSpec((t, 128), col("kr")),
                  pl.BlockSpec((t, 128), col("vr")), full, full, pl.BlockSpec((1, 128), lambda h: (0, h))],
        out_specs=[head, head, head, head, head],
        out_shape=[SDS((t, RET_WIDTH), BF16)] * 3 + [SDS((t, RET_WIDTH), F32)] * 2,
        scratch_shapes=[pltpu.VMEM((nc, hd, c), BF16), pltpu.VMEM((nc, hd, hd), F32), pltpu.VMEM((nc, hd, hd), F32),
                        pltpu.VMEM((nc, hd, hd), BF16), pltpu.VMEM((nc, hd, hd), BF16)],
        compiler_params=_params(("parallel",)),
    )(lgf, lgb, z, z, z, cos, sin, gnw)


def _merge_fwd(x, z, oa, on, wb_t, wout):
    t, d = x.shape
    tm = min(256, t)

    def body(x_ref, ga_ref, gr_ref, gm0_ref, gm1_ref, oa_ref, on_ref, wb_ref, wo_ref, xn_ref, ya_ref, yb_ref):
        ga, gr = ga_ref[...], gr_ref[...]
        ua = ga * _sigmoid(ga) * oa_ref[...]
        ub = gr * _sigmoid(gr) * on_ref[...]
        ya = _dot(ua.astype(BF16), wb_ref[:, :512], NT)
        yb = _dot(ub.astype(BF16), wb_ref[:, 512:], NT)
        ya_ref[...] = ya
        yb_ref[...] = yb
        merged = _sigmoid(gm0_ref[...]) * ya + _sigmoid(gm1_ref[...]) * yb
        xn_ref[...] = x_ref[...] + _dot(merged.astype(BF16), wo_ref[...])

    row = lambda w, j: pl.BlockSpec((tm, w), lambda i: (i, j))
    const = lambda shape: pl.BlockSpec(shape, lambda i: (0, 0))
    return pl.pallas_call(
        body, name="merge_fwd", grid=(t // tm,),
        in_specs=[row(d, 0), row(512, SEG["ga"][2] // 512), row(512, SEG["gr"][2] // 512),
                  row(1024, SEG["gm"][2] // 1024), row(1024, SEG["gm"][2] // 1024 + 1),
                  row(512, 0), row(512, 0), const((d, 1024)), const((d, d))],
        out_specs=[row(d, 0), row(d, 0), row(d, 0)],
        out_shape=[SDS((t, d), F32)] * 3,
        compiler_params=_params(("parallel",)),
    )(x, z, z, z, z, oa, on, wb_t, wout)


def _final_loss(x, g, target):
    t, d = x.shape
    tm = min(512, t)
    n = t // tm

    def body(x_ref, g_ref, t_ref, dx_ref, dg_ref, loss_ref, acc_g, acc_l):
        i = pl.program_id(0)

        @pl.when(i == 0)
        def _():
            acc_g[...] = jnp.zeros_like(acc_g)
            acc_l[...] = jnp.zeros_like(acc_l)

        xv, gv = x_ref[...], g_ref[...]
        r = lax.rsqrt(jnp.mean(xv * xv, axis=-1, keepdims=True) + EPS)
        xh = xv * r
        err = xh * gv - t_ref[...]
        dy = err * (1.0 / d)
        gy = dy * gv
        dx_ref[...] = r * (gy - xh * jnp.mean(gy * xh, axis=-1, keepdims=True))
        acc_g[...] += jnp.sum((dy * xh).reshape(tm // 8, 8, d), axis=0)
        acc_l[...] += jnp.sum((err * err).reshape(tm // 8, 8, d), axis=0)

        @pl.when(i == n - 1)
        def _():
            dg_ref[...] = jnp.sum(acc_g[...], axis=0, keepdims=True)
            tot = jnp.sum(jnp.sum(acc_l[...], axis=0, keepdims=True), axis=1, keepdims=True)
            loss_ref[...] = jnp.broadcast_to(tot * (0.5 / d), (1, 128))

    return pl.pallas_call(
        body, name="final_loss", grid=(n,),
        in_specs=[pl.BlockSpec((tm, d), lambda i: (i, 0)), pl.BlockSpec((1, d), lambda i: (0, 0)),
                  pl.BlockSpec((tm, d), lambda i: (i, 0))],
        out_specs=[pl.BlockSpec((tm, d), lambda i: (i, 0)), pl.BlockSpec((1, d), lambda i: (0, 0)),
                   pl.BlockSpec((1, 128), lambda i: (0, 0))],
        out_shape=[SDS((t, d), F32), SDS((1, d), F32), SDS((1, 128), F32)],
        scratch_shapes=[pltpu.VMEM((8, d), F32), pltpu.VMEM((8, d), F32)],
        compiler_params=_params(("arbitrary",)),
    )(x, g, target)


def _merge_bwd(dxo, z, oa, on, ya, yb, wb_t, wout):
    t, d = dxo.shape
    tm = min(256, t)
    n = t // tm

    def body(dx_ref, ga_ref, gr_ref, gm0_ref, gm1_ref, oa_ref, on_ref, ya_ref, yb_ref, wb_ref, wo_ref,
             doa_ref, don_ref, dz_ref, dwo_ref, dwb_ref, acc_o, acc_b):
        i = pl.program_id(0)

        @pl.when(i == 0)
        def _():
            acc_o[...] = jnp.zeros_like(acc_o)
            acc_b[...] = jnp.zeros_like(acc_b)

        dxb = dx_ref[...].astype(BF16)
        ya, yb = ya_ref[...], yb_ref[...]
        g0, g1 = _sigmoid(gm0_ref[...]), _sigmoid(gm1_ref[...])
        mb = (g0 * ya + g1 * yb).astype(BF16)
        dm = _dot(dxb, wo_ref[...], NT)
        dya = (dm * g0).astype(BF16)
        dyb = (dm * g1).astype(BF16)
        dz_ref[:, 1024:2048] = (dm * ya * g0 * (1.0 - g0)).astype(BF16)
        dz_ref[:, 2048:3072] = (dm * yb * g1 * (1.0 - g1)).astype(BF16)

        def branch(g_ref, o_ref, dy, w, do_ref, lo):
            gv, ov = g_ref[...], o_ref[...]
            sg = _sigmoid(gv)
            silu = gv * sg
            du = _dot(dy, w)
            do_ref[...] = du * silu
            dz_ref[:, lo:lo + 512] = (du * ov * (sg * (1.0 + gv * (1.0 - sg)))).astype(BF16)
            acc_b[:, lo:lo + 512] += _dot(dy, (silu * ov).astype(BF16), TN)

        branch(ga_ref, oa_ref, dya, wb_ref[:, :512], doa_ref, 0)
        branch(gr_ref, on_ref, dyb, wb_ref[:, 512:], don_ref, 512)
        acc_o[...] += _dot(mb, dxb, TN)

        @pl.when(i == n - 1)
        def _():
            dwo_ref[...] = acc_o[...].astype(BF16)
            dwb_ref[...] = acc_b[...].astype(BF16)

    row = lambda w, j: pl.BlockSpec((tm, w), lambda i: (i, j))
    const = lambda shape: pl.BlockSpec(shape, lambda i: (0, 0))
    return pl.pallas_call(
        body, name="merge_bwd", grid=(n,),
        in_specs=[row(d, 0), row(512, SEG["ga"][2] // 512), row(512, SEG["gr"][2] // 512),
                  row(1024, SEG["gm"][2] // 1024), row(1024, SEG["gm"][2] // 1024 + 1),
                  row(512, 0), row(512, 0), row(d, 0), row(d, 0), const((d, 1024)), const((d, d))],
        out_specs=[row(512, 0), row(512, 0), row(3072, 0), const((d, d)), const((d, 1024))],
        out_shape=[SDS((t, 512), F32), SDS((t, 512), F32), SDS((t, 3072), BF16), SDS((d, d), BF16),
                   SDS((d, 1024), BF16)],
        scratch_shapes=[pltpu.VMEM((d, d), F32), pltpu.VMEM((d, 1024), F32)],
        compiler_params=_params(("arbitrary",)),
    )(dxo, z, z, z, z, oa, on, ya, yb, wb_t, wout)


def _ret_bwd(qrot, krot, vb, orr, don, gnw, lgf, lgb):
    t = qrot.shape[0]
    c = RET_CHUNK
    nc = t // c
    hd = RET_HEAD_DIM
    unroll = 2 if nc % 2 == 0 else 1

    def body(lgf_ref, lgb_ref, q_ref, k_ref, v_ref, o_ref, dn_ref, w_ref,
             dq_ref, dk_ref, dv_ref, dw_ref, dlf_ref, dlb_ref, qt, kt, dob, uf, ub, wf, wb, sfa, sba, gfa, gba):
        h = pl.program_id(0)
        fw = _Dir(lgf_ref[h], False)
        bw = _Dir(lgb_ref[h], True)
        fw.dt, bw.dt = fw.d.T, bw.d.T

        o = o_ref[...]
        xc = o - jnp.mean(o, axis=-1, keepdims=True)
        r = lax.rsqrt(jnp.mean(xc * xc, axis=-1, keepdims=True) + EPS)
        xh = xc * r
        dn = dn_ref[...]
        gy = dn * w_ref[...]
        d_o = r * (gy - jnp.mean(gy, axis=-1, keepdims=True) - xh * jnp.mean(gy * xh, axis=-1, keepdims=True))
        dw_ref[...] = jnp.sum(dn * xh, axis=0, keepdims=True)
        dob[...] = d_o.astype(BF16)
        for i in range(nc):
            qt[i] = q_ref[i * c:(i + 1) * c, :].astype(F32).T.astype(BF16)
            kt[i] = k_ref[i * c:(i + 1) * c, :].astype(F32).T.astype(BF16)

        def rows(ci):
            return pl.ds(pl.multiple_of(ci * c, c), c)

        def products(ci, carry):
            sl = rows(ci)
            vv, do32 = v_ref[sl, :], dob[sl, :].astype(F32)
            ktf = kt[ci].astype(F32)
            uf[ci] = _dot((ktf * fw.kd_row).astype(BF16), vv)
            ub[ci] = _dot((ktf * bw.kd_row).astype(BF16), vv)
            wf[ci] = _dot(qt[ci], (do32 * fw.qd).astype(BF16))
            wb[ci] = _dot(qt[ci], (do32 * bw.qd).astype(BF16))
            return carry

        lax.fori_loop(0, nc, products, 0, unroll=unroll)

        def scan(i, carry):
            sf, sb, gf, gb = carry
            j = nc - 1 - i
            sfa[i] = sf.astype(BF16)
            sba[j] = sb.astype(BF16)
            gfa[j] = gf.astype(BF16)
            gba[i] = gb.astype(BF16)
            return sf * fw.cd + uf[i], sb * bw.cd + ub[j], gf * fw.cd + wf[j], gb * bw.cd + wb[i]

        zero = jnp.zeros((hd, hd), F32)
        lax.fori_loop(0, nc, scan, (zero, zero, zero, zero))

        def one_dir(p, s_all, g_all, ci, qq, kk, vv, do, a, bm):
            sb, gb = s_all[ci], g_all[ci]
            doq = (do.astype(F32) * p.qd).astype(BF16)
            dqc = _dot(doq, sb, NT)
            kkd = (kk.astype(F32) * p.kd_col).astype(BF16)
            dk2 = _dot(vv, gb, NT) * p.kd_col
            terms = (p.dist * p.d * a * bm + p.wq * qq.astype(F32) * dqc + p.wk * kk.astype(F32) * dk2
                     + (float(c) * p.cd) * gb.astype(F32) * sb.astype(F32))
            return dqc, dk2, _dot(kkd, gb), terms

        d_both, dt_both = fw.d + bw.d, fw.dt + bw.dt

        def chunk(ci, carry):
            af, ab = carry
            sl = rows(ci)
            qq, kk, vv, do = q_ref[sl, :], k_ref[sl, :], v_ref[sl, :], dob[sl, :]
            a, bm = _dot(qq, kk, NT), _dot(do, vv, NT)
            at, bt = _dot(kk, qq, NT), _dot(vv, do, NT)
            dqf, dkf, dvf, tf = one_dir(fw, sfa, gfa, ci, qq, kk, vv, do, a, bm)
            dqb, dkb, dvb, tb = one_dir(bw, sba, gba, ci, qq, kk, vv, do, a, bm)
            dq_ref[sl, :] = _dot((bm * d_both).astype(BF16), kk) + dqf + dqb
            dk_ref[sl, :] = _dot((bt * dt_both).astype(BF16), qq) + dkf + dkb
            dv_ref[sl, :] = _dot((at * dt_both).astype(BF16), do) + dvf + dvb
            return af + tf, ab + tb

        af, ab = lax.fori_loop(0, nc, chunk, (zero, zero), unroll=unroll)
        tot = lambda m: jnp.sum(jnp.sum(m, axis=0, keepdims=True), axis=1, keepdims=True)
        dlf_ref[...] = jnp.broadcast_to(tot(af).reshape(1, 1, 1), (1, 8, 128))
        dlb_ref[...] = jnp.broadcast_to(tot(ab).reshape(1, 1, 1), (1, 8, 128))

    smem = pl.BlockSpec(memory_space=pltpu.SMEM)
    head = pl.BlockSpec((t, 128), lambda h: (0, h))
    vec = pl.BlockSpec((1, 128), lambda h: (0, h))
    scal = pl.BlockSpec((1, 8, 128), lambda h: (h, 0, 0))
    mats = lambda dt: pltpu.VMEM((nc, hd, hd), dt)
    return pl.pallas_call(
        body, name="ret_bwd", grid=(RET_HEADS,),
        in_specs=[smem, smem, head, head, head, head, head, vec],
        out_specs=[head, head, head, vec, scal, scal],
        out_shape=[SDS((t, RET_WIDTH), F32)] * 3 + [SDS((1, RET_WIDTH), F32), SDS((RET_HEADS, 8, 128), F32),
                                                   SDS((RET_HEADS, 8, 128), F32)],
        scratch_shapes=[pltpu.VMEM((nc, hd, c), BF16), pltpu.VMEM((nc, hd, c), BF16), pltpu.VMEM((t, hd), BF16),
                        mats(F32), mats(F32), mats(F32), mats(F32), mats(BF16), mats(BF16), mats(BF16), mats(BF16)],
        compiler_params=_params(("parallel",)),
    )(lgf, lgb, qrot, krot, vb, orr, don, gnw)


def _ret_post_bwd(dq, dk, dv, cos, sin):
    t = dq.shape[0]
    tm = min(512, t)
    hd = RET_HEAD_DIM

    def body(dq_ref, dk_ref, dv_ref, c_ref, s_ref, oq_ref, ok_ref, ov_ref):
        cc = jnp.concatenate([c_ref[...]] * 4, axis=-1)
        ss = jnp.concatenate([s_ref[...]] * 4, axis=-1)
        oq_ref[...] = _rope_bwd(dq_ref[...], cc, ss, hd // 4).astype(BF16)
        ok_ref[...] = (_rope_bwd(dk_ref[...], cc, ss, hd // 4) * (hd ** -0.5)).astype(BF16)
        ov_ref[...] = dv_ref[...].astype(BF16)

    blk = pl.BlockSpec((tm, 512), lambda i: (i, 0))
    tab = pl.BlockSpec((tm, 128), lambda i: (i, 0))
    return pl.pallas_call(
        body, name="ret_post_bwd", grid=(t // tm,),
        in_specs=[blk, blk, blk, tab, tab], out_specs=[blk, blk, blk],
        out_shape=[SDS((t, 512), BF16)] * 3,
        compiler_params=_params(("parallel",)),
    )(dq, dk, dv, cos, sin)


def _attn_bwd(q, qt, k, v, doa, oa, lse, ex=None):
    t = q.shape[1]
    tq = min(256, t)
    nq = t // tq
    tk = min(ATTN_BWD_KEY_CHUNK, t)
    nk = t // tk
    hd = ATTN_HEAD_DIM
    scale = hd ** -0.5

    def body(q_ref, qt_ref, k_ref, v_ref, do_ref, o_ref, lse_ref, dq_ref, dkt_ref, dvt_ref):
        p, i = pl.program_id(0), pl.program_id(1)

        @pl.when(jnp.logical_and(p % 2 == 0, i == 0))
        def _():
            dkt_ref[...] = jnp.zeros_like(dkt_ref)
            dvt_ref[...] = jnp.zeros_like(dvt_ref)

        dov, ov = do_ref[...], o_ref[...]
        dovt = dov.T
        lanes = lambda col: jnp.concatenate([col] * (tk // 128), axis=1)
        outs = []
        for j in range(2):
            qq, qqt = q_ref[j], qt_ref[j]
            do32 = dov[:, j * hd:(j + 1) * hd]
            do, dot_ = do32.astype(BF16), dovt[j * hd:(j + 1) * hd, :].astype(BF16)
            dd = lanes(jnp.broadcast_to(jnp.sum(do32 * ov[:, j * hd:(j + 1) * hd], axis=1, keepdims=True), (tq, 128)))
            lse_j = lanes(jnp.broadcast_to(lse_ref[j], (128, tq)).T)
            dq = jnp.zeros((tq, hd), F32)
            for c in range(nk):
                sl = slice(c * tk, (c + 1) * tk)
                kc, vc = k_ref[0, sl, :], v_ref[0, sl, :]
                pr = jnp.exp(_dot(qq, kc, NT) - lse_j)
                ds = (pr * (_dot(do, vc, NT) - dd)).astype(BF16)
                dvt_ref[0, :, sl] += _dot(dot_, pr.astype(BF16))
                dkt_ref[0, :, sl] += _dot(qqt, ds)
                dq = dq + _dot(ds, kc)
            outs.append(dq * scale)
        dq_ref[...] = jnp.concatenate(outs, axis=-1)

    kv = pl.BlockSpec((1, t, hd), lambda p, i: (p // 2, 0, 0))
    kvt = pl.BlockSpec((1, hd, t), lambda p, i: (p // 2, 0, 0))
    pair = pl.BlockSpec((tq, 128), lambda p, i: (i, p))
    first = lambda: jnp.logical_and(pl.program_id(0) == 0, pl.program_id(1) == 0)
    last = lambda: jnp.logical_and(pl.program_id(0) == 3, pl.program_id(1) == nq - 1)
    xi, xo, xs, xscr, xargs = _ex_args(ex)
    return pl.pallas_call(
        _with_exchange(body, 7, 3, 0, ex, first, last), name="attn_bwd", grid=(4, nq),
        in_specs=[pl.BlockSpec((2, tq, hd), lambda p, i: (p, i, 0)), pl.BlockSpec((2, hd, tq), lambda p, i: (p, 0, i)),
                  kv, kv, pair, pair, pl.BlockSpec((2, 1, tq), lambda p, i: (p, 0, i))] + xi,
        out_specs=[pair, kvt, kvt] + xo,
        out_shape=[SDS((t, ATTN_WIDTH), F32), SDS((ATTN_KV_HEADS, hd, t), F32),
                   SDS((ATTN_KV_HEADS, hd, t), F32)] + xs,
        scratch_shapes=xscr,
        compiler_params=_params(("arbitrary", "arbitrary")),
    )(q, qt, k, v, doa, oa, lse, *xargs)


def _attn_post_bwd(dq, dk, dv, z, qn, kn, cos, sin, ones_bd):
    t = z.shape[0]
    tm = min(512, t)
    n = t // tm
    hd = ATTN_HEAD_DIM

    def body(dq_ref, dk_ref, dv_ref, zq_ref, zkv_ref, qn_ref, kn_ref, c_ref, s_ref, b_ref,
             dz_ref, dqn_ref, dkn_ref, acc_q, acc_k):
        i = pl.program_id(0)

        @pl.when(i == 0)
        def _():
            acc_q[...] = jnp.zeros_like(acc_q)
            acc_k[...] = jnp.zeros_like(acc_k)

        bd = b_ref[...]
        c2, s2 = c_ref[...], s_ref[...]

        def norm_bwd(dy, x, w, ones, cos_t, sin_t, acc):
            dyr = _rope_bwd(dy, cos_t, sin_t, hd // 4)
            r = lax.rsqrt(_group_mean(x * x, ones) + EPS)
            xh = x * r
            gy = dyr * w
            acc[...] += jnp.sum((dyr * xh).reshape(tm // 8, 8, x.shape[-1]), axis=0)
            return r * (gy - xh * _group_mean(gy * xh, ones))

        cq = jnp.concatenate([c2] * 4, axis=-1)
        sq = jnp.concatenate([s2] * 4, axis=-1)
        dz_ref[:, :512] = norm_bwd(dq_ref[...], zq_ref[...], qn_ref[...], bd, cq, sq, acc_q).astype(BF16)
        zkv = zkv_ref[...]
        dkk = jnp.concatenate([dk_ref[0], dk_ref[1]], axis=0).T
        dz_ref[:, 512:640] = norm_bwd(dkk, zkv[:, :128], kn_ref[...], bd[:128, :128], c2, s2, acc_k).astype(BF16)
        dz_ref[:, 640:768] = jnp.concatenate([dv_ref[0], dv_ref[1]], axis=0).T.astype(BF16)

        @pl.when(i == n - 1)
        def _():
            dqn_ref[...] = jnp.sum(acc_q[...], axis=0, keepdims=True)
            dkn_ref[...] = jnp.sum(acc_k[...], axis=0, keepdims=True)

    kv_blk = SEG["ka"][2] // 256
    kvs = pl.BlockSpec((ATTN_KV_HEADS, hd, tm), lambda i: (0, 0, i))
    const = lambda shape: pl.BlockSpec(shape, lambda i: (0, 0))
    return pl.pallas_call(
        body, name="attn_post_bwd", grid=(n,),
        in_specs=[pl.BlockSpec((tm, 512), lambda i: (i, 0)), kvs, kvs,
                  pl.BlockSpec((tm, 512), lambda i: (i, 0)), pl.BlockSpec((tm, 256), lambda i: (i, kv_blk)),
                  const((1, 512)), const((1, 128)),
                  pl.BlockSpec((tm, 128), lambda i: (i, 0)), pl.BlockSpec((tm, 128), lambda i: (i, 0)),
                  const((512, 512))],
        out_specs=[pl.BlockSpec((tm, 768), lambda i: (i, 0)), const((1, 512)), const((1, 128))],
        out_shape=[SDS((t, 768), BF16), SDS((1, 512), F32), SDS((1, 128), F32)],
        scratch_shapes=[pltpu.VMEM((8, 512), F32), pltpu.VMEM((8, 128), F32)],
        compiler_params=_params(("arbitrary",)),
    )(dq, dk, dv, z, z, qn, kn, cos, sin, ones_bd)


def _in_bwd(dxo, x, g, w_t, dz_a, dz_m, dqr, dkr, dvr, after=None):
    t, d = x.shape
    tm = min(256, t)
    n = t // tm
    parts = [(0, 0, 768, 0), (1, 0, 512, SEG["ga"][0]), (2, 0, 512, SEG["qr"][0]), (3, 0, 512, SEG["kr"][0]),
             (4, 0, 512, SEG["vr"][0]), (1, 512, 2560, SEG["gr"][0])]

    def body(dx_ref, x_ref, g_ref, w_ref, a_ref, m_ref, q_ref, k_ref, v_ref, o_ref, dg_ref, acc):
        i = pl.program_id(0)

        @pl.when(i == 0)
        def _():
            acc[...] = jnp.zeros_like(acc)

        pieces = [a_ref, m_ref, q_ref, k_ref, v_ref]
        dh = jnp.zeros((tm, d), F32)
        for pi, lo, w, row in parts:
            dh = dh + _dot(pieces[pi][:, lo:lo + w], w_ref[row:row + w, :])
        xv = x_ref[...]
        r = lax.rsqrt(jnp.mean(xv * xv, axis=-1, keepdims=True) + EPS)
        xh = xv * r
        gy = dh * g_ref[...]
        o_ref[...] = dx_ref[...] + r * (gy - xh * jnp.mean(gy * xh, axis=-1, keepdims=True))
        acc[...] += jnp.sum((dh * xh).reshape(tm // 8, 8, d), axis=0)

        @pl.when(i == n - 1)
        def _():
            dg_ref[...] = jnp.sum(acc[...], axis=0, keepdims=True)

    row = lambda w: pl.BlockSpec((tm, w), lambda i: (i, 0))
    const = lambda shape: pl.BlockSpec(shape, lambda i: (0, 0))
    extra = [] if after is None else [after]
    return pl.pallas_call(
        (lambda *refs: body(*refs[:9], *refs[9 + len(extra):])), name="in_bwd", grid=(n,),
        in_specs=[row(d), row(d), const((1, d)), const((D_IN, d)), row(768), row(3072), row(512), row(512),
                  row(512)] + [const(a.shape) for a in extra],
        out_specs=[row(d), const((1, d))],
        out_shape=[SDS((t, d), F32), SDS((1, d), F32)],
        scratch_shapes=[pltpu.VMEM((8, d), F32)],
        compiler_params=_params(("arbitrary",)),
    )(dxo, x, g, w_t, dz_a, dz_m, dqr, dkr, dvr, *extra)


def _dw_in(h_t, dz_a, dz_m, dqr, dkr, dvr):
    d, t = h_t.shape
    tn = 256
    parts = [(0, 0, 0, 3), (1, 0, SEG["ga"][0] // tn, 2), (2, 0, SEG["qr"][0] // tn, 2),
             (3, 0, SEG["kr"][0] // tn, 2), (4, 0, SEG["vr"][0] // tn, 2), (1, 2, SEG["gr"][0] // tn, 10)]
    pieces = [dz_a, dz_m, dqr, dkr, dvr]

    def col_block(pi):
        mine = [(c0, r0, n) for q, c0, r0, n in parts if q == pi]

        def index(j):
            c0, r0, n = mine[0]
            blk = c0 + jnp.clip(j - r0, 0, n - 1)
            for c0, r0, n in mine[1:]:
                blk = jnp.where(j >= r0, c0 + jnp.clip(j - r0, 0, n - 1), blk)
            return 0, blk

        return index

    def body(h_ref, *refs):
        o_ref = refs[-1]
        j = pl.program_id(0)
        for pi, _, r0, n in parts:
            @pl.when(jnp.logical_and(j >= r0, j < r0 + n))
            def _(p_ref=refs[pi]):
                o_ref[...] = _dot(h_ref[...], p_ref[...]).T.astype(BF16)

    return pl.pallas_call(
        body, name="dw_in", grid=(D_IN // tn,),
        in_specs=[pl.BlockSpec((d, t), lambda j: (0, 0))] + [pl.BlockSpec((t, tn), col_block(pi)) for pi in range(5)],
        out_specs=pl.BlockSpec((tn, d), lambda j: (j, 0)),
        out_shape=SDS((D_IN, d), BF16),
        compiler_params=_params(("arbitrary",)),
    )(h_t, *pieces)


def _adamw_math(w, g, m, v):
    mn = ADAM_B1 * m + (1.0 - ADAM_B1) * g
    vn = ADAM_B2 * v + (1.0 - ADAM_B2) * (g * g)
    m_hat = mn / (1.0 - ADAM_B1 ** ADAM_STEP)
    v_hat = vn / (1.0 - ADAM_B2 ** ADAM_STEP)
    return -ADAM_LR * (m_hat / (jnp.sqrt(v_hat) + ADAM_EPS) + ADAM_WD * w), mn, vn


def _sum_adamw(recvs, w, m, v, lane0, tn, layer0=0, prev=None, own=None):
    _, r, c = w.shape
    j0 = lane0 // tn
    n = len(recvs)
    has_own = own is not None

    def body(*refs):
        mine_ref, refs = (refs[0], refs[1:]) if has_own else (None, refs)
        w_ref, m_ref, v_ref = refs[n:n + 3]
        g_ref, d_ref, mo_ref, vo_ref = refs[-4:]

        def run(r_ref):
            def slot(s):
                if has_own:
                    return jnp.where(mine_ref[0] == s, refs[n + 3][...], r_ref[s]).astype(F32)
                return r_ref[s].astype(F32)

            g = slot(0)
            for s in range(1, N_DEV):
                g = g + slot(s)
            g_ref[0] = g
            d_ref[0], mo_ref[0], vo_ref[0] = _adamw_math(w_ref[0], g, m_ref[0], v_ref[0])

        for i in range(n):
            pl.when(pl.program_id(0) == i)(functools.partial(run, refs[i]))

    slots = pl.BlockSpec((N_DEV, r, tn), lambda i, j, *_: (0, 0, j0 + j))
    blk = pl.BlockSpec((1, r, tn), lambda i, j, *_: (layer0 + i, 0, j))
    before = [] if prev is None else list(prev)
    in_specs, args = [slots] * n + [blk] * 3, [*recvs, w, m, v]
    if has_own:
        assert n == 1
        in_specs.append(pl.BlockSpec((r, tn), lambda i, j, mine: (mine[0], j0 + j)))
        args.append(own[0])
    n_pre = len(args) + has_own
    return pl.pallas_call(
        body, name="sum_adamw",
        grid_spec=pltpu.PrefetchScalarGridSpec(
            num_scalar_prefetch=int(has_own), grid=(n, c // tn),
            in_specs=in_specs + [ANY] * len(before), out_specs=[blk] * 4),
        out_shape=[SDS(w.shape, F32)] * 4,
        input_output_aliases={n_pre + k: k for k in range(len(before))},
        compiler_params=_params(("parallel", "parallel")),
    )(*([own[1]] if has_own else []), *args, *before)


def _adamw(w, g, m, v):
    rows, cols = w.shape
    tr = 256 if rows % 256 == 0 else rows

    def body(w_ref, g_ref, m_ref, v_ref, d_ref, mo_ref, vo_ref):
        d_ref[...], mo_ref[...], vo_ref[...] = _adamw_math(w_ref[...], g_ref[...], m_ref[...], v_ref[...])

    blk = pl.BlockSpec((tr, cols), lambda i: (i, 0))
    return pl.pallas_call(
        body, name="adamw", grid=(rows // tr,),
        in_specs=[blk] * 4, out_specs=[blk] * 3, out_shape=[SDS((rows, cols), F32)] * 3,
        compiler_params=_params(("parallel",)),
    )(w, g, m, v)


def _all_gather(shards):
    na = len(shards)
    chips = (4, 2, 6)

    def body(*refs):
        ins, outs = refs[:na], refs[na:2 * na]
        send_sems, recv_sems, local_sems = refs[2 * na:]
        _, mine = _flip(0)

        def rows(a, idx):
            r = shards[a].shape[0]
            return outs[a].at[pl.ds(pl.multiple_of(idx * r, 16), r), :]

        def copy(a, slot, block_idx, to, src=None):
            return pltpu.make_async_remote_copy(
                src_ref=rows(a, block_idx) if src is None else src, dst_ref=rows(a, block_idx),
                send_sem=send_sems.at[a, slot], recv_sem=recv_sems.at[a, slot],
                device_id=to, device_id_type=MESH_ID)

        sibling, sibling_idx = _flip(1)
        local, started = [], []
        for a in range(na):
            cp = pltpu.make_async_copy(ins[a], rows(a, mine), local_sems.at[a])
            cp.start()
            local.append(cp)
            first = [copy(a, 0, mine, sibling, src=ins[a])]
            first += [copy(a, 1 + j, mine, _flip(k)[0], src=ins[a]) for j, k in enumerate(chips)]
            for cp in first:
                cp.start()
            started += first
        for a in range(na):
            for j, k in enumerate(chips):
                _, theirs = _flip(k)
                copy(a, 1 + j, theirs, _flip(0)[0]).wait_recv()
                fwd = copy(a, 4 + j, theirs, sibling)
                fwd.start()
                started.append(fwd)
        for a in range(na):
            copy(a, 0, sibling_idx, _flip(0)[0]).wait_recv()
            for j, k in enumerate(chips):
                _, theirs = _flip(k | 1)
                copy(a, 4 + j, theirs, _flip(0)[0]).wait_recv()
        for cp in started:
            cp.wait_send()
        for cp in local:
            cp.wait()

    return pl.pallas_call(
        body, name="all_gather_weights",
        in_specs=[ANY] * na, out_specs=[ANY] * na,
        out_shape=[SDS((N_DEV * s.shape[0], s.shape[1]), s.dtype) for s in shards],
        scratch_shapes=[pltpu.SemaphoreType.DMA((na, 7)), pltpu.SemaphoreType.DMA((na, 7)),
                        pltpu.SemaphoreType.DMA((na,))],
        compiler_params=pltpu.CompilerParams(has_side_effects=True),
    )(*shards)


def _scatter_blocks_of(g_ref, rows, idx):
    return g_ref.at[pl.ds(pl.multiple_of(idx * rows, 16), rows), :]


def _scatter_start(g):
    rows = g.shape[0] // N_DEV
    land_shape = (N_DEV, rows, g.shape[1])

    def body(g_ref, land_ref, send_sems, recv_sems, g_thru, land_thru, token):
        _, mine = _flip(0)
        for k in range(1, N_DEV):
            peer, theirs = _flip(k)
            pltpu.make_async_remote_copy(
                src_ref=_scatter_blocks_of(g_ref, rows, theirs), dst_ref=land_ref.at[mine],
                send_sem=send_sems.at[k - 1], recv_sem=recv_sems.at[k - 1],
                device_id=peer, device_id_type=MESH_ID).start()
        token[...] = jnp.zeros_like(token)

    hbm, sem = pl.BlockSpec(memory_space=pltpu.HBM), pl.BlockSpec(memory_space=pltpu.SEMAPHORE)
    return pl.pallas_call(
        body, name="scatter_start",
        out_shape=(pltpu.SemaphoreType.DMA((N_DEV - 1,)), pltpu.SemaphoreType.DMA((N_DEV - 1,)),
                   pltpu.HBM(g.shape, g.dtype), pltpu.HBM(land_shape, g.dtype), SDS((8, 128), F32)),
        in_specs=(hbm, hbm), out_specs=(sem, sem, hbm, hbm, pl.BlockSpec(memory_space=pltpu.VMEM)),
        input_output_aliases={0: 2, 1: 3},
        compiler_params=pltpu.CompilerParams(has_side_effects=pltpu.SideEffectType.DATAFLOW_SIDE_EFFECTING),
    )(pltpu.with_memory_space_constraint(g, pltpu.HBM),
      pltpu.with_memory_space_constraint(lax.empty(land_shape, g.dtype), pltpu.HBM))


def _scatter_wait(send_sems, recv_sems, g_thru, land_thru, after):
    rows = g_thru.shape[0] // N_DEV

    def body(g_ref, land_ref, send_sems, recv_sems, *rest):
        me, _ = _flip(0)
        for k in range(1, N_DEV):
            _, theirs = _flip(k)
            copy = pltpu.make_async_remote_copy(
                src_ref=_scatter_blocks_of(g_ref, rows, theirs), dst_ref=land_ref.at[theirs],
                send_sem=send_sems.at[k - 1], recv_sem=recv_sems.at[k - 1],
                device_id=me, device_id_type=MESH_ID)
            copy.wait_send()
            copy.wait_recv()

    hbm, sem = pl.BlockSpec(memory_space=pltpu.HBM), pl.BlockSpec(memory_space=pltpu.SEMAPHORE)
    return pl.pallas_call(
        body, name="scatter_wait",
        out_shape=(pltpu.HBM(g_thru.shape, g_thru.dtype), pltpu.HBM(land_thru.shape, land_thru.dtype)),
        in_specs=(hbm, hbm, sem, sem) + (ANY,) * len(after), out_specs=(hbm, hbm), input_output_aliases={0: 0, 1: 1},
        compiler_params=pltpu.CompilerParams(has_side_effects=pltpu.SideEffectType.DATAFLOW_SIDE_EFFECTING),
    )(g_thru, land_thru, send_sems, recv_sems, *after)


def _all_reduce_small(packed):
    shape = packed.shape

    def body(p_ref, o_ref, slots, send_sems, recv_sems):
        me, mine = _flip(0)
        slots[mine] = p_ref[...]
        sends = []
        for k in range(1, N_DEV):
            peer, _ = _flip(k)
            cp = pltpu.make_async_remote_copy(
                src_ref=p_ref, dst_ref=slots.at[mine], send_sem=send_sems.at[k - 1], recv_sem=recv_sems.at[k - 1],
                device_id=peer, device_id_type=MESH_ID)
            cp.start()
            sends.append(cp)
        for k in range(1, N_DEV):
            _, theirs = _flip(k)
            pltpu.make_async_remote_copy(
                src_ref=p_ref, dst_ref=slots.at[theirs], send_sem=send_sems.at[k - 1],
                recv_sem=recv_sems.at[k - 1], device_id=me, device_id_type=MESH_ID).wait_recv()
        for cp in sends:
            cp.wait_send()
        acc = slots[0]
        for s in range(1, N_DEV):
            acc = acc + slots[s]
        o_ref[...] = acc

    vm = pl.BlockSpec(memory_space=pltpu.VMEM)
    return pl.pallas_call(
        body, name="all_reduce_small", in_specs=[vm], out_specs=vm, out_shape=SDS(shape, F32),
        scratch_shapes=[pltpu.VMEM((N_DEV,) + shape, F32), pltpu.SemaphoreType.DMA((7,)),
                        pltpu.SemaphoreType.DMA((7,))],
        compiler_params=pltpu.CompilerParams(has_side_effects=True),
    )(packed)


def _layer_fwd(x, p, tabs, ex):
    z, h_t = _in_proj(x, p["norm_g"], p["w_in_t"])
    q, qt, k, v, vt = _attn_prep(z, p["qn"], p["kn"], tabs["ca"], tabs["sa"], tabs["ones"])
    oa, lse, *gathered = _attn_fwd(q, k, vt, ex)
    qrot, krot, vb, orr, on = _ret_fwd(z, p["lgf"], p["lgb"], p["gnw"], tabs["cr"], tabs["sr"])
    return z, h_t, q, qt, k, v, lse, oa, qrot, krot, vb, orr, on, gathered


def _layer_bwd(dxo, s, p, tabs, ex_attn, scatter_w_in):
    doa, don, dz_m, d_wout, d_wb_t = _merge_bwd(dxo, s["z"], s["oa"], s["on"], s["ya"], s["yb"], p["wb_t"], p["w_out"])
    dq_a, dk_a, dv_a, *recv_attn = _attn_bwd(s["q"], s["qt"], s["k"], s["v"], doa, s["oa"], s["lse"],
                                              ex_attn(d_wb_t, d_wout))
    dz_a, d_qn, d_kn = _attn_post_bwd(dq_a, dk_a, dv_a, s["z"], p["qn"], p["kn"], tabs["ca"], tabs["sa"],
                                      tabs["ones"])
    dq_r, dk_r, dv_r, d_gnw, d_lgf, d_lgb = _ret_bwd(s["qrot"], s["krot"], s["vb"], s["orr"], don, p["gnw"],
                                                     p["lgf"], p["lgb"])
    dqr, dkr, dvr = _ret_post_bwd(dq_r, dk_r, dv_r, tabs["cr"], tabs["sr"])
    buf = _dw_in(s["h_t"], dz_a, dz_m, dqr, dkr, dvr)
    pending, token = None, None
    if scatter_w_in:
        *pending, token = _scatter_start(buf)
    dx, d_norm_g = _in_bwd(dxo, s["x"], p["norm_g"], p["w_in_t"], dz_a, dz_m, dqr, dkr, dvr, token)
    grads = dict(w_in_t=buf, wb_t=d_wb_t, w_out=d_wout, norm_g=d_norm_g, gnw=d_gnw,
                 qn=d_qn.reshape(ATTN_Q_HEADS, ATTN_HEAD_DIM).sum(axis=0),
                 kn=d_kn.reshape(ATTN_KV_HEADS, ATTN_HEAD_DIM).sum(axis=0),
                 lgf=d_lgf[:, 0, 0], lgb=d_lgb[:, 0, 0])
    return dx, grads, recv_attn, pending


def _adamw_nd(w, g, m, v):
    shape = w.shape
    two_d = (1, shape[0]) if w.ndim == 1 else (-1, shape[-1])
    out = _adamw(w.reshape(two_d), g.reshape(two_d), m.reshape(two_d), v.reshape(two_d))
    return tuple(o.reshape(shape) for o in out)


def kernel(x, norm_g, w_in, attn_q_norm, attn_k_norm, ret_decay_fwd, ret_decay_bwd, ret_gn_w, w_branch_attn, w_branch_ret, w_out, final_norm_g, loss_target, m_norm_g, m_w_in, m_attn_q_norm, m_attn_k_norm, m_ret_decay_fwd, m_ret_decay_bwd, m_ret_gn_w, m_w_branch_attn, m_w_branch_ret, m_w_out, m_final_norm_g, v_norm_g, v_w_in, v_attn_q_norm, v_attn_k_norm, v_ret_decay_fwd, v_ret_decay_bwd, v_ret_gn_w, v_w_branch_attn, v_w_branch_ret, v_w_out, v_final_norm_g):
    t, d = x.shape[1], x.shape[2]
    x2, target = x[0], loss_target[0]

    w_in_sh, wb_sh, wout_sh = [], [], []
    for l in range(DEPTH):
        w_in_sh.append(jnp.swapaxes(w_in[l], 0, 1).astype(BF16))
        wb_sh.append(jnp.concatenate([w_branch_attn[l].T, w_branch_ret[l].T], axis=1).astype(BF16))
        wout_sh.append(w_out[l].astype(BF16))

    ca, sa = _rope_tables(t, ATTN_HEAD_DIM)
    cr, sr = _rope_tables(t, RET_HEAD_DIM)
    grp = jnp.arange(ATTN_WIDTH) // ATTN_HEAD_DIM
    tabs = dict(ca=jnp.tile(ca, (1, 2)), sa=jnp.tile(sa, (1, 2)), cr=cr, sr=sr,
                ones=jnp.where(grp[:, None] == grp[None, :], 1.0 / ATTN_HEAD_DIM, 0.0).astype(BF16))
    layers = []
    for l in range(DEPTH):
        layers.append(dict(
            norm_g=norm_g[l][None], qn=jnp.tile(attn_q_norm[l], ATTN_Q_HEADS)[None],
            kn=jnp.tile(attn_k_norm[l], ATTN_KV_HEADS)[None], gnw=ret_gn_w[l][None],
            lgf=jax.nn.log_sigmoid(ret_decay_fwd[l]), lgb=jax.nn.log_sigmoid(ret_decay_bwd[l])))

    layers[0]["w_in_t"], = _all_gather([w_in_sh[0]])
    gathers = [_Exchange("gather", [wb_sh[0], wout_sh[0], w_in_sh[1]]), _Exchange("gather", [wb_sh[1], wout_sh[1]])]
    h = x2
    saved = []
    for l in range(DEPTH):
        p = layers[l]
        z, h_t, q, qt, k, v, lse, oa, qrot, krot, vb, orr, on, got = _layer_fwd(h, p, tabs, gathers[l])
        p["wb_t"], p["w_out"] = got[0], got[1]
        if l == 0:
            layers[1]["w_in_t"] = got[2]
        xn, ya, yb = _merge_fwd(h, z, oa, on, p["wb_t"], p["w_out"])
        saved.append(dict(x=h, z=z, h_t=h_t, q=q, qt=qt, k=k, v=v, lse=lse, oa=oa, qrot=qrot, krot=krot, vb=vb,
                          orr=orr, on=on, ya=ya, yb=yb))
        h = xn
    dx, d_final_g, loss_part = _final_loss(h, final_norm_g[None], target)

    grads = [None] * DEPTH
    dx, grads[1], _, _ = _layer_bwd(dx, saved[1], layers[1], tabs, lambda *a: None, False)
    g1 = grads[1]
    ex_attn = lambda d_wb_t, d_wout: _Exchange("scatter", [g1["w_in_t"], g1["wb_t"], g1["w_out"], d_wb_t, d_wout])
    dx, grads[0], recv_attn, pending = _layer_bwd(dx, saved[0], layers[0], tabs, ex_attn, True)
    recv = [None, recv_attn[3], recv_attn[4], recv_attn[0], recv_attn[1], recv_attn[2]]
    tr = lambda a: jnp.swapaxes(a, 1, 2)
    w_in_t = (tr(w_in), tr(m_w_in), tr(v_w_in))
    sharded = {}
    w_in_l1 = _sum_adamw([recv[3]], *w_in_t, 0, 256, layer0=1)
    sharded[id(w_branch_attn)] = [tr(o) for o in _sum_adamw(
        [recv[1], recv[4]], tr(w_branch_attn), tr(m_w_branch_attn), tr(v_w_branch_attn), 0, 512)]
    sharded[id(w_branch_ret)] = [tr(o) for o in _sum_adamw(
        [recv[1], recv[4]], tr(w_branch_ret), tr(m_w_branch_ret), tr(v_w_branch_ret), 512, 512)]
    sharded[id(w_out)] = _sum_adamw([recv[2], recv[5]], w_out, m_w_out, v_w_out, 0, 256)
    g_wba, g_wbr, g_wout = (sharded[id(w)][0] for w in (w_branch_attn, w_branch_ret, w_out))

    packed = jnp.zeros((8, 1024), F32)
    for l in range(DEPTH):
        gl = grads[l]
        packed = packed.at[l].set(gl["norm_g"][0])
        packed = packed.at[2, 512 * l:512 * (l + 1)].set(gl["gnw"][0])
        packed = packed.at[4, 128 * l:128 * l + 64].set(gl["qn"])
        packed = packed.at[4, 256 + 128 * l:256 + 128 * l + 64].set(gl["kn"])
        packed = packed.at[4, 512 + 128 * l:512 + 128 * l + 4].set(gl["lgf"])
        packed = packed.at[4, 768 + 128 * l:768 + 128 * l + 4].set(gl["lgb"])
    packed = packed.at[3].set(d_final_g[0])
    packed = packed.at[5, 0].set(loss_part[0, 0])
    red = _all_reduce_small(packed)
    loss = red[5, 0]
    g_norm_g = red[0:2]
    g_gnw = red[2].reshape(DEPTH, RET_WIDTH)
    g_final = red[3]
    g_qn = jnp.stack([red[4, 128 * l:128 * l + 64] for l in range(DEPTH)])
    g_kn = jnp.stack([red[4, 256 + 128 * l:256 + 128 * l + 64] for l in range(DEPTH)])
    g_lgf = jnp.stack([red[4, 512 + 128 * l:512 + 128 * l + 4] for l in range(DEPTH)])
    g_lgb = jnp.stack([red[4, 768 + 128 * l:768 + 128 * l + 4] for l in range(DEPTH)])
    g_df = g_lgf * jax.nn.sigmoid(-ret_decay_fwd)
    g_db = g_lgb * jax.nn.sigmoid(-ret_decay_bwd)

    grad_w = [g_norm_g, None, g_qn, g_kn, g_df, g_db, g_gnw, g_wba, g_wbr, g_wout, g_final]
    weights = [norm_g, w_in, attn_q_norm, attn_k_norm, ret_decay_fwd, ret_decay_bwd, ret_gn_w, w_branch_attn,
               w_branch_ret, w_out, final_norm_g]
    ms = [m_norm_g, m_w_in, m_attn_q_norm, m_attn_k_norm, m_ret_decay_fwd, m_ret_decay_bwd, m_ret_gn_w,
          m_w_branch_attn, m_w_branch_ret, m_w_out, m_final_norm_g]
    vs = [v_norm_g, v_w_in, v_attn_q_norm, v_attn_k_norm, v_ret_decay_fwd, v_ret_decay_bwd, v_ret_gn_w,
          v_w_branch_attn, v_w_branch_ret, v_w_out, v_final_norm_g]
    upd = [None if w is w_in else sharded[id(w)][1:] if id(w) in sharded else _adamw_nd(w, g, m, v)
           for w, g, m, v in zip(weights, grad_w, ms, vs)]

    done = [dx, w_in_l1[0], g_wout] + [u[0] for w, u in zip(weights, upd) if u is not None and id(w) not in sharded]
    g_full, recv[0] = _scatter_wait(*pending, done)
    mine = (4 * lax.axis_index("x") + 2 * lax.axis_index("y") + lax.axis_index("c")).astype(jnp.int32)[None]
    w_in_upd = [tr(o) for o in _sum_adamw([recv[0]], *w_in_t, 0, 256, layer0=0, prev=w_in_l1, own=(g_full, mine))]
    grad_w[1], upd[1] = w_in_upd[0], w_in_upd[1:]
    return (loss, dx[None], *grad_w, *[u[0] for u in upd], *[u[1] for u in upd], *[u[2] for u in upd])
```

```python
import functools

import jax
import jax.numpy as jnp
from jax import lax
from jax.experimental import pallas as pl
from jax.experimental.pallas import tpu as pltpu

F32 = jnp.float32
BF16 = jnp.bfloat16
SDS = jax.ShapeDtypeStruct

D_MODEL = 1024
DEPTH = 2
GRID_W = 64
ATTN_Q_HEADS = 8
ATTN_KV_HEADS = 2
ATTN_HEAD_DIM = 64
ATTN_WIDTH = 512
ATTN_KV_WIDTH = 128
RET_HEADS = 4
RET_HEAD_DIM = 128
RET_WIDTH = 512
RET_CHUNK = 128
ATTN_KEY_CHUNK = 512
ATTN_BWD_KEY_CHUNK = 1024
QK_DOTS_PER_CHUNK = 4
EXP_LAG = 3
ROPE_THETA = 10000.0
EPS = 1e-6
D_IN = 5376
N_DEV = 8

ADAM_LR = 0.001
ADAM_B1 = 0.9
ADAM_B2 = 0.999
ADAM_EPS = 1e-08
ADAM_WD = 0.01
ADAM_STEP = 10

SEG = {
    "qa": (0, 512, 0),
    "ga": (768, 512, 512),
    "qr": (1280, 512, 1024),
    "kr": (1792, 512, 1536),
    "vr": (2304, 512, 2048),
    "gr": (2816, 512, 2560),
    "gm": (3328, 2048, 3072),
    "ka": (512, 128, 5120),
    "va": (640, 128, 5248),
}

VMEM_LIMIT = 60 * 1024 * 1024
NT = (((1,), (1,)), ((), ()))
TN = (((0,), (0,)), ((), ()))
MESH_ID = pl.DeviceIdType.MESH
ANY = pl.BlockSpec(memory_space=pl.ANY)


def _params(sem=None, vmem=VMEM_LIMIT):
    return pltpu.CompilerParams(dimension_semantics=sem, vmem_limit_bytes=vmem)


def _dot(a, b, dims=None):
    if dims is None:
        return jnp.dot(a, b, preferred_element_type=F32)
    return lax.dot_general(a, b, dims, preferred_element_type=F32)


def _sigmoid(x):
    return 1.0 / (1.0 + jnp.exp(-x))


def _swap_halves(x, q):
    n = x.shape[-1]
    axis = x.ndim - 1
    lane = lax.broadcasted_iota(jnp.int32, x.shape, axis)
    first = (lane % (2 * q)) < q
    return jnp.where(first, pltpu.roll(x, n - q, axis), pltpu.roll(x, q, axis))


def _rope(x, cos, sin_signed, q):
    return x * cos + _swap_halves(x, q) * sin_signed


def _rope_bwd(dy, cos, sin_signed, q):
    return dy * cos - _swap_halves(dy, q) * sin_signed


def _group_mean(v, ones_bd):
    hi = v.astype(BF16)
    r1 = v - hi.astype(F32)
    mid = r1.astype(BF16)
    lo = (r1 - mid.astype(F32)).astype(BF16)
    return _dot(hi, ones_bd) + _dot(mid, ones_bd) + _dot(lo, ones_bd)


def _rope_tables(t, head_dim):
    n_rows = t // GRID_W
    d_axis = head_dim // 2
    inv_freq = ROPE_THETA ** (-jnp.arange(0, d_axis, 2, dtype=F32) / d_axis)
    ar = jnp.arange(n_rows, dtype=F32)[:, None] * inv_freq
    ac = jnp.arange(GRID_W, dtype=F32)[:, None] * inv_freq
    by_row = lambda a: jnp.repeat(a, GRID_W, axis=0)
    by_col = lambda a: jnp.tile(a, (n_rows, 1))
    cr, sr, cc, sc = by_row(jnp.cos(ar)), by_row(jnp.sin(ar)), by_col(jnp.cos(ac)), by_col(jnp.sin(ac))
    return jnp.concatenate([cr, cr, cc, cc], axis=-1), jnp.concatenate([-sr, sr, -sc, sc], axis=-1)


def _me():
    return lax.axis_index("x"), lax.axis_index("y"), lax.axis_index("c")


def _flip(k):
    x, y, c = _me()
    px = 1 - x if k & 4 else x
    py = 1 - y if k & 2 else y
    pc = 1 - c if k & 1 else c
    return (px, py, pc), 4 * px + 2 * py + pc


class _Exchange:
    def __init__(self, kind, srcs):
        self.kind, self.srcs, self.n = kind, list(srcs), len(srcs)
        self.rows = [a.shape[0] if kind == "gather" else a.shape[0] // N_DEV for a in srcs]
        if kind == "gather":
            self.out_shape = [SDS((N_DEV * a.shape[0], a.shape[1]), a.dtype) for a in srcs]
        else:
            self.out_shape = [SDS((N_DEV, a.shape[0] // N_DEV, a.shape[1]), a.dtype) for a in srcs]
        self.scratch = [pltpu.SemaphoreType.DMA((self.n, N_DEV - 1)), pltpu.SemaphoreType.DMA((self.n, N_DEV - 1)),
                        pltpu.SemaphoreType.DMA((self.n,))]

    def _block(self, ref, a, idx):
        r = self.rows[a]
        return ref.at[pl.ds(pl.multiple_of(idx * r, 16), r), :]

    def _src(self, ins, a, idx):
        return ins[a] if self.kind == "gather" else self._block(ins[a], a, idx)

    def _dst(self, outs, a, idx):
        return self._block(outs[a], a, idx) if self.kind == "gather" else outs[a].at[idx]

    def _copies(self, ins, outs, sems):
        send_sems, recv_sems, local_sems = sems
        me, mine = _flip(0)
        local, sends, recvs = [], [], []
        for a in range(self.n):
            local.append(pltpu.make_async_copy(self._src(ins, a, mine), self._dst(outs, a, mine), local_sems.at[a]))
            for k in range(1, N_DEV):
                peer, theirs = _flip(k)
                sem = dict(send_sem=send_sems.at[a, k - 1], recv_sem=recv_sems.at[a, k - 1])
                sends.append(pltpu.make_async_remote_copy(
                    src_ref=self._src(ins, a, theirs), dst_ref=self._dst(outs, a, mine),
                    device_id=peer, device_id_type=MESH_ID, **sem))
                recvs.append(pltpu.make_async_remote_copy(
                    src_ref=self._dst(outs, a, theirs), dst_ref=self._dst(outs, a, theirs),
                    device_id=me, device_id_type=MESH_ID, **sem))
        return local, sends, recvs

    def start(self, ins, outs, sems):
        local, sends, _ = self._copies(ins, outs, sems)
        for cp in local + sends:
            cp.start()

    def wait(self, ins, outs, sems):
        local, sends, recvs = self._copies(ins, outs, sems)
        for cp in sends:
            cp.wait_send()
        for cp in recvs:
            cp.wait_recv()
        for cp in local:
            cp.wait()


def _with_exchange(body, n_in, n_out, n_scratch, ex, first, last):
    if ex is None:
        return body

    def wrapped(*refs):
        ins = refs[:n_in]
        ex_ins = refs[n_in:n_in + ex.n]
        outs = refs[n_in + ex.n:n_in + ex.n + n_out]
        ex_outs = refs[n_in + ex.n + n_out:n_in + 2 * ex.n + n_out]
        rest = refs[n_in + 2 * ex.n + n_out:]
        scratch, sems = rest[:n_scratch], rest[n_scratch:]

        @pl.when(first())
        def _():
            ex.start(ex_ins, ex_outs, sems)

        body(*ins, *outs, *scratch)

        @pl.when(last())
        def _():
            ex.wait(ex_ins, ex_outs, sems)

    return wrapped


def _ex_args(ex):
    if ex is None:
        return [], [], [], [], []
    return [ANY] * ex.n, [ANY] * ex.n, list(ex.out_shape), list(ex.scratch), list(ex.srcs)


def _in_proj(x, g, w_t, qn, kn, cos, sin, ones_bd):
    t, d = x.shape
    tm = min(256, t)
    tk = min(ATTN_KEY_CHUNK, t)
    per_chunk = tk // tm
    hd = ATTN_HEAD_DIM

    def body(x_ref, g_ref, w_ref, qn_ref, kn_ref, c_ref, s_ref, b_ref,
             z_ref, ht_ref, q_out, qt_out, k_out, v_out, vt_out):
        xv = x_ref[...]
        r = lax.rsqrt(jnp.mean(xv * xv, axis=-1, keepdims=True) + EPS)
        h = xv * r * g_ref[...]
        ht_ref[...] = h.T.astype(BF16)
        hb = h.astype(BF16)
        seg = {}
        for name, (nat, w, off) in SEG.items():
            seg[name] = _dot(hb, w_ref[nat:nat + w, :], NT)
            z_ref[:, off:off + w] = seg[name]

        bd = b_ref[...]
        c2, s2 = c_ref[...], s_ref[...]
        cq = jnp.concatenate([c2] * 4, axis=-1)
        sq = jnp.concatenate([s2] * 4, axis=-1)
        xq, xk, xvv = seg["qa"], seg["ka"], seg["va"]
        yq = xq * lax.rsqrt(_group_mean(xq * xq, bd) + EPS) * qn_ref[...]
        yq = _rope(yq, cq, sq, hd // 4) * (hd ** -0.5)
        yqt = yq.T
        for hh in range(ATTN_Q_HEADS):
            q_out[hh] = yq[:, hh * hd:(hh + 1) * hd].astype(BF16)
            qt_out[hh] = yqt[hh * hd:(hh + 1) * hd, :].astype(BF16)
        yk = xk * lax.rsqrt(_group_mean(xk * xk, bd[:ATTN_KV_WIDTH, :ATTN_KV_WIDTH]) + EPS) * kn_ref[...]
        yk = _rope(yk, c2, s2, hd // 4)
        xvt = xvv.T
        ones = jnp.ones((hd, tm), F32)
        for hh in range(ATTN_KV_HEADS):
            k_out[hh] = yk[:, hh * hd:(hh + 1) * hd].astype(BF16)
            v_out[hh] = xvv[:, hh * hd:(hh + 1) * hd].astype(BF16)
            vt_out[hh, 0] = jnp.concatenate([xvt[hh * hd:(hh + 1) * hd, :], ones], axis=0).astype(BF16)

    const = lambda shape: pl.BlockSpec(shape, lambda i: (0,) * len(shape))
    rows = lambda w: pl.BlockSpec((tm, w), lambda i: (i, 0))
    return pl.pallas_call(
        body, name="in_proj", grid=(t // tm,),
        in_specs=[rows(d), const((1, d)), const((D_IN, d)), const((1, 512)), const((1, 128)), rows(128), rows(128),
                  const((512, 512))],
        out_specs=[rows(D_IN), pl.BlockSpec((d, tm), lambda i: (0, i)),
                   pl.BlockSpec((ATTN_Q_HEADS, tm, hd), lambda i: (0, i, 0)),
                   pl.BlockSpec((ATTN_Q_HEADS, hd, tm), lambda i: (0, 0, i)),
                   pl.BlockSpec((ATTN_KV_HEADS, tm, hd), lambda i: (0, i, 0)),
                   pl.BlockSpec((ATTN_KV_HEADS, tm, hd), lambda i: (0, i, 0)),
                   pl.BlockSpec((ATTN_KV_HEADS, 1, 2 * hd, tm), lambda i: (0, i // per_chunk, 0, i % per_chunk))],
        out_shape=[SDS((t, D_IN), F32), SDS((d, t), BF16),
                   SDS((ATTN_Q_HEADS, t, hd), BF16), SDS((ATTN_Q_HEADS, hd, t), BF16),
                   SDS((ATTN_KV_HEADS, t, hd), BF16), SDS((ATTN_KV_HEADS, t, hd), BF16),
                   SDS((ATTN_KV_HEADS, t // tk, 2 * hd, tk), BF16)],
        compiler_params=_params(("parallel",)),
    )(x, g, w_t, qn, kn, cos, sin, ones_bd)


def _attn_fwd(q, k, vt, ex=None):
    t = q.shape[1]
    tq = min(256, t)
    nk, tk = vt.shape[1], vt.shape[3]
    hd = ATTN_HEAD_DIM
    g = ATTN_Q_HEADS // ATTN_KV_HEADS

    def body(q_ref, k_ref, vt_ref, o_ref, lse_ref, s_scr):
        def pass_a(h, c, m8):
            part = tk // QK_DOTS_PER_CHUNK
            for lo in range(c * tk, (c + 1) * tk, part):
                st = _dot(k_ref[0, lo:lo + part, :], q_ref[h], NT)
                s_scr[h % 2, lo:lo + part, :] = st
                m8 = jnp.maximum(m8, jnp.max(st.reshape(part // 8, 8, tq), axis=0))
            return m8

        def pass_b(h, c, m, acc, after):
            e = jnp.exp(s_scr[h % 2, c * tk:(c + 1) * tk, :] - (m + after * 0.0)).astype(BF16)
            return acc + _dot(vt_ref[0, c], e)

        neg = jnp.full((8, tq), -jnp.inf, F32)
        m8 = neg
        for c in range(nk):
            m8 = pass_a(0, c, m8)
        outs = []
        for h in range(g):
            m = jnp.max(m8, axis=0, keepdims=True)
            acc = jnp.zeros((2 * hd, tq), F32)
            m8 = neg
            done = [m] * EXP_LAG
            for c in range(nk):
                if h + 1 < g:
                    m8 = pass_a(h + 1, c, m8)
                acc = pass_b(h, c, m, acc, done[-EXP_LAG])
                done.append(m8[0:1, :] if h + 1 < g else acc[hd:hd + 1, :])
            l = acc[hd:hd + 1, :]
            outs.append((acc[:hd, :] / l).T)
            lse_ref[h] = m + jnp.log(l)
        o_ref[...] = jnp.concatenate(outs, axis=-1)

    nq = t // tq
    first = lambda: jnp.logical_and(pl.program_id(0) == 0, pl.program_id(1) == 0)
    last = lambda: jnp.logical_and(pl.program_id(0) == ATTN_KV_HEADS - 1, pl.program_id(1) == nq - 1)
    xi, xo, xs, xscr, xargs = _ex_args(ex)
    return pl.pallas_call(
        _with_exchange(body, 3, 2, 1, ex, first, last), name="attn_fwd", grid=(ATTN_KV_HEADS, nq),
        in_specs=[pl.BlockSpec((g, tq, hd), lambda p, i: (p, i, 0)),
                  pl.BlockSpec((1, t, hd), lambda p, i: (p, 0, 0)),
                  pl.BlockSpec((1, nk, 2 * hd, tk), lambda p, i: (p, 0, 0, 0))] + xi,
        out_specs=[pl.BlockSpec((tq, g * hd), lambda p, i: (i, p)),
                   pl.BlockSpec((g, 1, tq), lambda p, i: (p, 0, i))] + xo,
        out_shape=[SDS((t, ATTN_WIDTH), F32), SDS((ATTN_Q_HEADS, 1, t), F32)] + xs,
        scratch_shapes=[pltpu.VMEM((2, t, tq), F32)] + xscr,
        compiler_params=_params(("arbitrary", "arbitrary")),
    )(q, k, vt, *xargs)


class _Dir:
    def __init__(self, lg, strict_future):
        c = RET_CHUNK
        ia = lax.broadcasted_iota(jnp.int32, (c, c), 0).astype(F32)
        ib = lax.broadcasted_iota(jnp.int32, (c, c), 1).astype(F32)
        col = lax.broadcasted_iota(jnp.int32, (c, 1), 0).astype(F32)
        row = lax.broadcasted_iota(jnp.int32, (1, c), 1).astype(F32)
        if strict_future:
            dist = ib - ia
            mask = dist > 0
            self.wq, self.wk, wk_row = c - col, col, row
        else:
            dist = ia - ib
            mask = dist >= 0
            self.wq, self.wk, wk_row = col + 1.0, c - 1.0 - col, c - 1.0 - row
        self.dist = jnp.maximum(dist, 0.0)
        self.d = jnp.where(mask, jnp.exp(self.dist * lg), 0.0)
        self.qd = jnp.exp(self.wq * lg)
        self.kd_col = jnp.exp(self.wk * lg)
        self.kd_row = jnp.exp(wk_row * lg)
        self.cd = jnp.exp(jnp.full((1, 1), float(c), F32) * lg)


def _ret_fwd(z, lgf, lgb, gnw, cos, sin):
    t = z.shape[0]
    c = RET_CHUNK
    nc = t // c
    hd = RET_HEAD_DIM
    unroll = 4 if nc % 4 == 0 else 1

    def body(lgf_ref, lgb_ref, q_ref, k_ref, v_ref, c_ref, s_ref, w_ref,
             qo_ref, ko_ref, vo_ref, orr_ref, on_ref, kt, uf, ub, sfa, sba):
        h = pl.program_id(0)
        fw = _Dir(lgf_ref[h], False)
        bw = _Dir(lgb_ref[h], True)
        cc, ss = c_ref[...], s_ref[...]
        qo_ref[...] = _rope(q_ref[...], cc, ss, hd // 4).astype(BF16)
        kr = _rope(k_ref[...], cc, ss, hd // 4) * (hd ** -0.5)
        ko_ref[...] = kr.astype(BF16)
        vo_ref[...] = v_ref[...].astype(BF16)
        for i in range(nc):
            kt[i] = kr[i * c:(i + 1) * c, :].T.astype(BF16)

        def rows(ci):
            return pl.ds(pl.multiple_of(ci * c, c), c)

        def kv_products(ci, carry):
            vv = vo_ref[rows(ci), :]
            ktf = kt[ci].astype(F32)
            uf[ci] = _dot((ktf * fw.kd_row).astype(BF16), vv)
            ub[ci] = _dot((ktf * bw.kd_row).astype(BF16), vv)
            return carry

        lax.fori_loop(0, nc, kv_products, 0, unroll=unroll)

        def scan(i, carry):
            sf, sb = carry
            j = nc - 1 - i
            sfa[i] = sf.astype(BF16)
            sba[j] = sb.astype(BF16)
            return sf * fw.cd + uf[i], sb * bw.cd + ub[j]

        zero = jnp.zeros((hd, hd), F32)
        lax.fori_loop(0, nc, scan, (zero, zero))
        gw = w_ref[...]

        def outputs(ci, carry):
            sl = rows(ci)
            qq, kk, vv = qo_ref[sl, :], ko_ref[sl, :], vo_ref[sl, :]
            a = _dot(qq, kk, NT)
            o = (_dot((a * fw.d).astype(BF16), vv) + _dot(qq, sfa[ci]) * fw.qd
                 + _dot((a * bw.d).astype(BF16), vv) + _dot(qq, sba[ci]) * bw.qd)
            orr_ref[sl, :] = o
            xc = o - jnp.mean(o, axis=-1, keepdims=True)
            var = jnp.mean(xc * xc, axis=-1, keepdims=True)
            on_ref[sl, :] = xc * lax.rsqrt(var + EPS) * gw
            return carry

        lax.fori_loop(0, nc, outputs, 0, unroll=unroll)

    smem = pl.BlockSpec(memory_space=pltpu.SMEM)
    col = lambda name: (lambda h: (0, SEG[name][2] // 128 + h))
    head = pl.BlockSpec((t, 128), lambda h: (0, h))
    full = pl.BlockSpec((t, 128), lambda h: (0, 0))
    return pl.pallas_call(
        body, name="ret_fwd", grid=(RET_HEADS,),
        in_specs=[smem, smem, pl.BlockSpec((t, 128), col("qr")), pl.BlockSpec((t, 128), col("kr")),
                  pl.BlockSpec((t, 128), col("vr")), full, full, pl.BlockSpec((1, 128), lambda h: (0, h))],
        out_specs=[head, head, head, head, head],
        out_shape=[SDS((t, RET_WIDTH), BF16)] * 3 + [SDS((t, RET_WIDTH), F32)] * 2,
        scratch_shapes=[pltpu.VMEM((nc, hd, c), BF16), pltpu.VMEM((nc, hd, hd), F32), pltpu.VMEM((nc, hd, hd), F32),
                        pltpu.VMEM((nc, hd, hd), BF16), pltpu.VMEM((nc, hd, hd), BF16)],
        compiler_params=_params(("parallel",)),
    )(lgf, lgb, z, z, z, cos, sin, gnw)


def _merge_fwd(x, z, oa, on, wb_t, wout):
    t, d = x.shape
    tm = min(256, t)

    def body(x_ref, ga_ref, gr_ref, gm0_ref, gm1_ref, oa_ref, on_ref, wb_ref, wo_ref, xn_ref, ya_ref, yb_ref):
        ga, gr = ga_ref[...], gr_ref[...]
        ua = ga * _sigmoid(ga) * oa_ref[...]
        ub = gr * _sigmoid(gr) * on_ref[...]
        ya = _dot(ua.astype(BF16), wb_ref[:, :512], NT)
        yb = _dot(ub.astype(BF16), wb_ref[:, 512:], NT)
        ya_ref[...] = ya
        yb_ref[...] = yb
        merged = _sigmoid(gm0_ref[...]) * ya + _sigmoid(gm1_ref[...]) * yb
        xn_ref[...] = x_ref[...] + _dot(merged.astype(BF16), wo_ref[...])

    row = lambda w, j: pl.BlockSpec((tm, w), lambda i: (i, j))
    const = lambda shape: pl.BlockSpec(shape, lambda i: (0, 0))
    return pl.pallas_call(
        body, name="merge_fwd", grid=(t // tm,),
        in_specs=[row(d, 0), row(512, SEG["ga"][2] // 512), row(512, SEG["gr"][2] // 512),
                  row(1024, SEG["gm"][2] // 1024), row(1024, SEG["gm"][2] // 1024 + 1),
                  row(512, 0), row(512, 0), const((d, 1024)), const((d, d))],
        out_specs=[row(d, 0), row(d, 0), row(d, 0)],
        out_shape=[SDS((t, d), F32)] * 3,
        compiler_params=_params(("parallel",)),
    )(x, z, z, z, z, oa, on, wb_t, wout)


def _final_loss(x, g, target):
    t, d = x.shape
    tm = min(512, t)
    n = t // tm

    def body(x_ref, g_ref, t_ref, dx_ref, dg_ref, loss_ref, acc_g, acc_l):
        i = pl.program_id(0)

        @pl.when(i == 0)
        def _():
            acc_g[...] = jnp.zeros_like(acc_g)
            acc_l[...] = jnp.zeros_like(acc_l)

        xv, gv = x_ref[...], g_ref[...]
        r = lax.rsqrt(jnp.mean(xv * xv, axis=-1, keepdims=True) + EPS)
        xh = xv * r
        err = xh * gv - t_ref[...]
        dy = err * (1.0 / d)
        gy = dy * gv
        dx_ref[...] = r * (gy - xh * jnp.mean(gy * xh, axis=-1, keepdims=True))
        acc_g[...] += jnp.sum((dy * xh).reshape(tm // 8, 8, d), axis=0)
        acc_l[...] += jnp.sum((err * err).reshape(tm // 8, 8, d), axis=0)

        @pl.when(i == n - 1)
        def _():
            dg_ref[...] = jnp.sum(acc_g[...], axis=0, keepdims=True)
            tot = jnp.sum(jnp.sum(acc_l[...], axis=0, keepdims=True), axis=1, keepdims=True)
            loss_ref[...] = jnp.broadcast_to(tot * (0.5 / d), (1, 128))

    return pl.pallas_call(
        body, name="final_loss", grid=(n,),
        in_specs=[pl.BlockSpec((tm, d), lambda i: (i, 0)), pl.BlockSpec((1, d), lambda i: (0, 0)),
                  pl.BlockSpec((tm, d), lambda i: (i, 0))],
        out_specs=[pl.BlockSpec((tm, d), lambda i: (i, 0)), pl.BlockSpec((1, d), lambda i: (0, 0)),
                   pl.BlockSpec((1, 128), lambda i: (0, 0))],
        out_shape=[SDS((t, d), F32), SDS((1, d), F32), SDS((1, 128), F32)],
        scratch_shapes=[pltpu.VMEM((8, d), F32), pltpu.VMEM((8, d), F32)],
        compiler_params=_params(("arbitrary",)),
    )(x, g, target)


def _merge_bwd_out(dxo, z, ya, yb, wout):
    t, d = dxo.shape
    tm = min(512, t)
    n = t // tm

    def body(dx_ref, gm0_ref, gm1_ref, ya_ref, yb_ref, wo_ref, dya_ref, dyb_ref, dz_ref, dwo_ref, acc):
        i = pl.program_id(0)

        @pl.when(i == 0)
        def _():
            acc[...] = jnp.zeros_like(acc)

        dxb = dx_ref[...].astype(BF16)
        ya, yb = ya_ref[...], yb_ref[...]
        g0, g1 = _sigmoid(gm0_ref[...]), _sigmoid(gm1_ref[...])
        mb = (g0 * ya + g1 * yb).astype(BF16)
        dm = _dot(dxb, wo_ref[...], NT)
        dya_ref[...] = (dm * g0).astype(BF16)
        dyb_ref[...] = (dm * g1).astype(BF16)
        dz_ref[:, 1024:2048] = (dm * ya * g0 * (1.0 - g0)).astype(BF16)
        dz_ref[:, 2048:3072] = (dm * yb * g1 * (1.0 - g1)).astype(BF16)
        acc[...] += _dot(mb, dxb, TN)

        @pl.when(i == n - 1)
        def _():
            dwo_ref[...] = acc[...].astype(BF16)

    row = lambda w, j: pl.BlockSpec((tm, w), lambda i: (i, j))
    const = lambda shape: pl.BlockSpec(shape, lambda i: (0, 0))
    return pl.pallas_call(
        body, name="merge_bwd_out", grid=(n,),
        in_specs=[row(d, 0), row(1024, SEG["gm"][2] // 1024), row(1024, SEG["gm"][2] // 1024 + 1), row(d, 0),
                  row(d, 0), const((d, d))],
        out_specs=[row(d, 0), row(d, 0), row(3072, 0), const((d, d))],
        out_shape=[SDS((t, d), BF16), SDS((t, d), BF16), SDS((t, 3072), BF16), SDS((d, d), BF16)],
        scratch_shapes=[pltpu.VMEM((d, d), F32)],
        compiler_params=_params(("arbitrary",)),
    )(dxo, z, z, ya, yb, wout)


def _merge_bwd_branch(dya, dyb, z, oa, on, wb_t, dz):
    t, d = dya.shape
    tm = min(512, t)
    n = t // tm

    def body(dya_ref, dyb_ref, ga_ref, gr_ref, oa_ref, on_ref, wb_ref, dz_in, doa_ref, don_ref, dz_ref, dwb_ref, acc):
        i = pl.program_id(0)

        @pl.when(i == 0)
        def _():
            acc[...] = jnp.zeros_like(acc)

        def branch(g_ref, o_ref, dy, w, do_ref, lo):
            gv, ov = g_ref[...], o_ref[...]
            sg = _sigmoid(gv)
            silu = gv * sg
            du = _dot(dy, w)
            do_ref[...] = du * silu
            dz_ref[:, lo:lo + 512] = (du * ov * (sg * (1.0 + gv * (1.0 - sg)))).astype(BF16)
            acc[:, lo:lo + 512] += _dot(dy, (silu * ov).astype(BF16), TN)

        branch(ga_ref, oa_ref, dya_ref[...], wb_ref[:, :512], doa_ref, 0)
        branch(gr_ref, on_ref, dyb_ref[...], wb_ref[:, 512:], don_ref, 512)

        @pl.when(i == n - 1)
        def _():
            dwb_ref[...] = acc[...].astype(BF16)

    row = lambda w, j: pl.BlockSpec((tm, w), lambda i: (i, j))
    const = lambda shape: pl.BlockSpec(shape, lambda i: (0, 0))
    return pl.pallas_call(
        body, name="merge_bwd_branch", grid=(n,),
        in_specs=[row(d, 0), row(d, 0), row(512, SEG["ga"][2] // 512), row(512, SEG["gr"][2] // 512),
                  row(512, 0), row(512, 0), const((d, 1024)), ANY],
        out_specs=[row(512, 0), row(512, 0), row(1024, 0), const((d, 1024))],
        out_shape=[SDS((t, 512), F32), SDS((t, 512), F32), SDS(dz.shape, BF16), SDS((d, 1024), BF16)],
        input_output_aliases={7: 2},
        scratch_shapes=[pltpu.VMEM((d, 1024), F32)],
        compiler_params=_params(("arbitrary",)),
    )(dya, dyb, z, z, oa, on, wb_t, dz)


def _ret_bwd(qrot, krot, vb, orr, don, gnw, lgf, lgb):
    t = qrot.shape[0]
    c = RET_CHUNK
    nc = t // c
    hd = RET_HEAD_DIM
    unroll = 2 if nc % 2 == 0 else 1

    def body(lgf_ref, lgb_ref, q_ref, k_ref, v_ref, o_ref, dn_ref, w_ref,
             dq_ref, dk_ref, dv_ref, dw_ref, dlf_ref, dlb_ref, qt, kt, dob, uf, ub, wf, wb, sfa, sba, gfa, gba):
        h = pl.program_id(0)
        fw = _Dir(lgf_ref[h], False)
        bw = _Dir(lgb_ref[h], True)
        fw.dt, bw.dt = fw.d.T, bw.d.T

        o = o_ref[...]
        xc = o - jnp.mean(o, axis=-1, keepdims=True)
        r = lax.rsqrt(jnp.mean(xc * xc, axis=-1, keepdims=True) + EPS)
        xh = xc * r
        dn = dn_ref[...]
        gy = dn * w_ref[...]
        d_o = r * (gy - jnp.mean(gy, axis=-1, keepdims=True) - xh * jnp.mean(gy * xh, axis=-1, keepdims=True))
        dw_ref[...] = jnp.sum(dn * xh, axis=0, keepdims=True)
        dob[...] = d_o.astype(BF16)
        for i in range(nc):
            qt[i] = q_ref[i * c:(i + 1) * c, :].astype(F32).T.astype(BF16)
            kt[i] = k_ref[i * c:(i + 1) * c, :].astype(F32).T.astype(BF16)

        def rows(ci):
            return pl.ds(pl.multiple_of(ci * c, c), c)

        def products(ci, carry):
            sl = rows(ci)
            vv, do32 = v_ref[sl, :], dob[sl, :].astype(F32)
            ktf = kt[ci].astype(F32)
            uf[ci] = _dot((ktf * fw.kd_row).astype(BF16), vv)
            ub[ci] = _dot((ktf * bw.kd_row).astype(BF16), vv)
            wf[ci] = _dot(qt[ci], (do32 * fw.qd).astype(BF16))
            wb[ci] = _dot(qt[ci], (do32 * bw.qd).astype(BF16))
            return carry

        lax.fori_loop(0, nc, products, 0, unroll=unroll)

        def scan(i, carry):
            sf, sb, gf, gb = carry
            j = nc - 1 - i
            sfa[i] = sf.astype(BF16)
            sba[j] = sb.astype(BF16)
            gfa[j] = gf.astype(BF16)
            gba[i] = gb.astype(BF16)
            return sf * fw.cd + uf[i], sb * bw.cd + ub[j], gf * fw.cd + wf[j], gb * bw.cd + wb[i]

        zero = jnp.zeros((hd, hd), F32)
        lax.fori_loop(0, nc, scan, (zero, zero, zero, zero))

        def one_dir(p, s_all, g_all, ci, qq, kk, vv, do, a, bm):
            sb, gb = s_all[ci], g_all[ci]
            doq = (do.astype(F32) * p.qd).astype(BF16)
            dqc = _dot(doq, sb, NT)
            kkd = (kk.astype(F32) * p.kd_col).astype(BF16)
            dk2 = _dot(vv, gb, NT) * p.kd_col
            terms = (p.dist * p.d * a * bm + p.wq * qq.astype(F32) * dqc + p.wk * kk.astype(F32) * dk2
                     + (float(c) * p.cd) * gb.astype(F32) * sb.astype(F32))
            return dqc, dk2, _dot(kkd, gb), terms

        d_both, dt_both = fw.d + bw.d, fw.dt + bw.dt

        def chunk(ci, carry):
            af, ab = carry
            sl = rows(ci)
            qq, kk, vv, do = q_ref[sl, :], k_ref[sl, :], v_ref[sl, :], dob[sl, :]
            a, bm = _dot(qq, kk, NT), _dot(do, vv, NT)
            at, bt = _dot(kk, qq, NT), _dot(vv, do, NT)
            dqf, dkf, dvf, tf = one_dir(fw, sfa, gfa, ci, qq, kk, vv, do, a, bm)
            dqb, dkb, dvb, tb = one_dir(bw, sba, gba, ci, qq, kk, vv, do, a, bm)
            dq_ref[sl, :] = _dot((bm * d_both).astype(BF16), kk) + dqf + dqb
            dk_ref[sl, :] = _dot((bt * dt_both).astype(BF16), qq) + dkf + dkb
            dv_ref[sl, :] = _dot((at * dt_both).astype(BF16), do) + dvf + dvb
            return af + tf, ab + tb

        af, ab = lax.fori_loop(0, nc, chunk, (zero, zero), unroll=unroll)
        tot = lambda m: jnp.sum(jnp.sum(m, axis=0, keepdims=True), axis=1, keepdims=True)
        dlf_ref[...] = jnp.broadcast_to(tot(af).reshape(1, 1, 1), (1, 8, 128))
        dlb_ref[...] = jnp.broadcast_to(tot(ab).reshape(1, 1, 1), (1, 8, 128))

    smem = pl.BlockSpec(memory_space=pltpu.SMEM)
    head = pl.BlockSpec((t, 128), lambda h: (0, h))
    vec = pl.BlockSpec((1, 128), lambda h: (0, h))
    scal = pl.BlockSpec((1, 8, 128), lambda h: (h, 0, 0))
    mats = lambda dt: pltpu.VMEM((nc, hd, hd), dt)
    return pl.pallas_call(
        body, name="ret_bwd", grid=(RET_HEADS,),
        in_specs=[smem, smem, head, head, head, head, head, vec],
        out_specs=[head, head, head, vec, scal, scal],
        out_shape=[SDS((t, RET_WIDTH), F32)] * 3 + [SDS((1, RET_WIDTH), F32), SDS((RET_HEADS, 8, 128), F32),
                                                   SDS((RET_HEADS, 8, 128), F32)],
        scratch_shapes=[pltpu.VMEM((nc, hd, c), BF16), pltpu.VMEM((nc, hd, c), BF16), pltpu.VMEM((t, hd), BF16),
                        mats(F32), mats(F32), mats(F32), mats(F32), mats(BF16), mats(BF16), mats(BF16), mats(BF16)],
        compiler_params=_params(("parallel",)),
    )(lgf, lgb, qrot, krot, vb, orr, don, gnw)


def _ret_post_bwd(dq, dk, dv, cos, sin):
    t = dq.shape[0]
    tm = min(512, t)
    hd = RET_HEAD_DIM

    def body(dq_ref, dk_ref, dv_ref, c_ref, s_ref, oq_ref, ok_ref, ov_ref):
        cc = jnp.concatenate([c_ref[...]] * 4, axis=-1)
        ss = jnp.concatenate([s_ref[...]] * 4, axis=-1)
        oq_ref[...] = _rope_bwd(dq_ref[...], cc, ss, hd // 4).astype(BF16)
        ok_ref[...] = (_rope_bwd(dk_ref[...], cc, ss, hd // 4) * (hd ** -0.5)).astype(BF16)
        ov_ref[...] = dv_ref[...].astype(BF16)

    blk = pl.BlockSpec((tm, 512), lambda i: (i, 0))
    tab = pl.BlockSpec((tm, 128), lambda i: (i, 0))
    return pl.pallas_call(
        body, name="ret_post_bwd", grid=(t // tm,),
        in_specs=[blk, blk, blk, tab, tab], out_specs=[blk, blk, blk],
        out_shape=[SDS((t, 512), BF16)] * 3,
        compiler_params=_params(("parallel",)),
    )(dq, dk, dv, cos, sin)


def _attn_bwd(q, qt, k, v, doa, oa, lse, ex=None):
    t = q.shape[1]
    tq = min(256, t)
    nq = t // tq
    tk = min(ATTN_BWD_KEY_CHUNK, t)
    nk = t // tk
    hd = ATTN_HEAD_DIM
    scale = hd ** -0.5

    def body(q_ref, qt_ref, k_ref, v_ref, do_ref, o_ref, lse_ref, dq_ref, dkt_ref, dvt_ref):
        p, i = pl.program_id(0), pl.program_id(1)

        @pl.when(jnp.logical_and(p % 2 == 0, i == 0))
        def _():
            dkt_ref[...] = jnp.zeros_like(dkt_ref)
            dvt_ref[...] = jnp.zeros_like(dvt_ref)

        dov, ov = do_ref[...], o_ref[...]
        dovt = dov.T
        lanes = lambda col: jnp.concatenate([col] * (tk // 128), axis=1)
        outs = []
        for j in range(2):
            qq, qqt = q_ref[j], qt_ref[j]
            do32 = dov[:, j * hd:(j + 1) * hd]
            do, dot_ = do32.astype(BF16), dovt[j * hd:(j + 1) * hd, :].astype(BF16)
            dd = lanes(jnp.broadcast_to(jnp.sum(do32 * ov[:, j * hd:(j + 1) * hd], axis=1, keepdims=True), (tq, 128)))
            lse_j = lanes(jnp.broadcast_to(lse_ref[j], (128, tq)).T)
            dq = jnp.zeros((tq, hd), F32)
            for c in range(nk):
                sl = slice(c * tk, (c + 1) * tk)
                kc, vc = k_ref[0, sl, :], v_ref[0, sl, :]
                pr = jnp.exp(_dot(qq, kc, NT) - lse_j)
                ds = (pr * (_dot(do, vc, NT) - dd)).astype(BF16)
                dvt_ref[0, :, sl] += _dot(dot_, pr.astype(BF16))
                dkt_ref[0, :, sl] += _dot(qqt, ds)
                dq = dq + _dot(ds, kc)
            outs.append(dq * scale)
        dq_ref[...] = jnp.concatenate(outs, axis=-1)

    kv = pl.BlockSpec((1, t, hd), lambda p, i: (p // 2, 0, 0))
    kvt = pl.BlockSpec((1, hd, t), lambda p, i: (p // 2, 0, 0))
    pair = pl.BlockSpec((tq, 128), lambda p, i: (i, p))
    first = lambda: jnp.logical_and(pl.program_id(0) == 0, pl.program_id(1) == 0)
    last = lambda: jnp.logical_and(pl.program_id(0) == 3, pl.program_id(1) == nq - 1)
    xi, xo, xs, xscr, xargs = _ex_args(ex)
    return pl.pallas_call(
        _with_exchange(body, 7, 3, 0, ex, first, last), name="attn_bwd", grid=(4, nq),
        in_specs=[pl.BlockSpec((2, tq, hd), lambda p, i: (p, i, 0)), pl.BlockSpec((2, hd, tq), lambda p, i: (p, 0, i)),
                  kv, kv, pair, pair, pl.BlockSpec((2, 1, tq), lambda p, i: (p, 0, i))] + xi,
        out_specs=[pair, kvt, kvt] + xo,
        out_shape=[SDS((t, ATTN_WIDTH), F32), SDS((ATTN_KV_HEADS, hd, t), F32),
                   SDS((ATTN_KV_HEADS, hd, t), F32)] + xs,
        scratch_shapes=xscr,
        compiler_params=_params(("arbitrary", "arbitrary")),
    )(q, qt, k, v, doa, oa, lse, *xargs)


def _attn_post_bwd(dq, dk, dv, z, qn, kn, cos, sin, ones_bd):
    t = z.shape[0]
    tm = min(512, t)
    n = t // tm
    hd = ATTN_HEAD_DIM

    def body(dq_ref, dk_ref, dv_ref, zq_ref, zkv_ref, qn_ref, kn_ref, c_ref, s_ref, b_ref,
             dz_ref, dqn_ref, dkn_ref, acc_q, acc_k):
        i = pl.program_id(0)

        @pl.when(i == 0)
        def _():
            acc_q[...] = jnp.zeros_like(acc_q)
            acc_k[...] = jnp.zeros_like(acc_k)

        bd = b_ref[...]
        c2, s2 = c_ref[...], s_ref[...]

        def norm_bwd(dy, x, w, ones, cos_t, sin_t, acc):
            dyr = _rope_bwd(dy, cos_t, sin_t, hd // 4)
            r = lax.rsqrt(_group_mean(x * x, ones) + EPS)
            xh = x * r
            gy = dyr * w
            acc[...] += jnp.sum((dyr * xh).reshape(tm // 8, 8, x.shape[-1]), axis=0)
            return r * (gy - xh * _group_mean(gy * xh, ones))

        cq = jnp.concatenate([c2] * 4, axis=-1)
        sq = jnp.concatenate([s2] * 4, axis=-1)
        dz_ref[:, :512] = norm_bwd(dq_ref[...], zq_ref[...], qn_ref[...], bd, cq, sq, acc_q).astype(BF16)
        zkv = zkv_ref[...]
        dkk = jnp.concatenate([dk_ref[0], dk_ref[1]], axis=0).T
        dz_ref[:, 512:640] = norm_bwd(dkk, zkv[:, :128], kn_ref[...], bd[:128, :128], c2, s2, acc_k).astype(BF16)
        dz_ref[:, 640:768] = jnp.concatenate([dv_ref[0], dv_ref[1]], axis=0).T.astype(BF16)

        @pl.when(i == n - 1)
        def _():
            dqn_ref[...] = jnp.sum(acc_q[...], axis=0, keepdims=True)
            dkn_ref[...] = jnp.sum(acc_k[...], axis=0, keepdims=True)

    kv_blk = SEG["ka"][2] // 256
    kvs = pl.BlockSpec((ATTN_KV_HEADS, hd, tm), lambda i: (0, 0, i))
    const = lambda shape: pl.BlockSpec(shape, lambda i: (0, 0))
    return pl.pallas_call(
        body, name="attn_post_bwd", grid=(n,),
        in_specs=[pl.BlockSpec((tm, 512), lambda i: (i, 0)), kvs, kvs,
                  pl.BlockSpec((tm, 512), lambda i: (i, 0)), pl.BlockSpec((tm, 256), lambda i: (i, kv_blk)),
                  const((1, 512)), const((1, 128)),
                  pl.BlockSpec((tm, 128), lambda i: (i, 0)), pl.BlockSpec((tm, 128), lambda i: (i, 0)),
                  const((512, 512))],
        out_specs=[pl.BlockSpec((tm, 768), lambda i: (i, 0)), const((1, 512)), const((1, 128))],
        out_shape=[SDS((t, 768), BF16), SDS((1, 512), F32), SDS((1, 128), F32)],
        scratch_shapes=[pltpu.VMEM((8, 512), F32), pltpu.VMEM((8, 128), F32)],
        compiler_params=_params(("arbitrary",)),
    )(dq, dk, dv, z, z, qn, kn, cos, sin, ones_bd)


def _in_bwd(dxo, x, g, w_t, dz_a, dz_m, dqr, dkr, dvr, after=None):
    t, d = x.shape
    tm = min(256, t)
    n = t // tm
    parts = [(0, 0, 768, 0), (1, 0, 512, SEG["ga"][0]), (2, 0, 512, SEG["qr"][0]), (3, 0, 512, SEG["kr"][0]),
             (4, 0, 512, SEG["vr"][0]), (1, 512, 2560, SEG["gr"][0])]

    def body(dx_ref, x_ref, g_ref, w_ref, a_ref, m_ref, q_ref, k_ref, v_ref, o_ref, dg_ref, acc):
        i = pl.program_id(0)

        @pl.when(i == 0)
        def _():
            acc[...] = jnp.zeros_like(acc)

        pieces = [a_ref, m_ref, q_ref, k_ref, v_ref]
        dh = jnp.zeros((tm, d), F32)
        for pi, lo, w, row in parts:
            dh = dh + _dot(pieces[pi][:, lo:lo + w], w_ref[row:row + w, :])
        xv = x_ref[...]
        r = lax.rsqrt(jnp.mean(xv * xv, axis=-1, keepdims=True) + EPS)
        xh = xv * r
        gy = dh * g_ref[...]
        o_ref[...] = dx_ref[...] + r * (gy - xh * jnp.mean(gy * xh, axis=-1, keepdims=True))
        acc[...] += jnp.sum((dh * xh).reshape(tm // 8, 8, d), axis=0)

        @pl.when(i == n - 1)
        def _():
            dg_ref[...] = jnp.sum(acc[...], axis=0, keepdims=True)

    row = lambda w: pl.BlockSpec((tm, w), lambda i: (i, 0))
    const = lambda shape: pl.BlockSpec(shape, lambda i: (0, 0))
    extra = [] if after is None else [after]
    return pl.pallas_call(
        (lambda *refs: body(*refs[:9], *refs[9 + len(extra):])), name="in_bwd", grid=(n,),
        in_specs=[row(d), row(d), const((1, d)), const((D_IN, d)), row(768), row(3072), row(512), row(512),
                  row(512)] + [const(a.shape) for a in extra],
        out_specs=[row(d), const((1, d))],
        out_shape=[SDS((t, d), F32), SDS((1, d), F32)],
        scratch_shapes=[pltpu.VMEM((8, d), F32)],
        compiler_params=_params(("arbitrary",)),
    )(dxo, x, g, w_t, dz_a, dz_m, dqr, dkr, dvr, *extra)


def _dw_in(h_t, dz_a, dz_m, dqr, dkr, dvr):
    d, t = h_t.shape
    tn = 256
    parts = [(0, 0, 0, 3), (1, 0, SEG["ga"][0] // tn, 2), (2, 0, SEG["qr"][0] // tn, 2),
             (3, 0, SEG["kr"][0] // tn, 2), (4, 0, SEG["vr"][0] // tn, 2), (1, 2, SEG["gr"][0] // tn, 10)]
    pieces = [dz_a, dz_m, dqr, dkr, dvr]

    def col_block(pi):
        mine = [(c0, r0, n) for q, c0, r0, n in parts if q == pi]

        def index(j):
            c0, r0, n = mine[0]
            blk = c0 + jnp.clip(j - r0, 0, n - 1)
            for c0, r0, n in mine[1:]:
                blk = jnp.where(j >= r0, c0 + jnp.clip(j - r0, 0, n - 1), blk)
            return 0, blk

        return index

    def body(h_ref, *refs):
        o_ref = refs[-1]
        j = pl.program_id(0)
        for pi, _, r0, n in parts:
            @pl.when(jnp.logical_and(j >= r0, j < r0 + n))
            def _(p_ref=refs[pi]):
                o_ref[...] = _dot(h_ref[...], p_ref[...]).T.astype(BF16)

    return pl.pallas_call(
        body, name="dw_in", grid=(D_IN // tn,),
        in_specs=[pl.BlockSpec((d, t), lambda j: (0, 0))] + [pl.BlockSpec((t, tn), col_block(pi)) for pi in range(5)],
        out_specs=pl.BlockSpec((tn, d), lambda j: (j, 0)),
        out_shape=SDS((D_IN, d), BF16),
        compiler_params=_params(("arbitrary",)),
    )(h_t, *pieces)


def _adamw_math(w, g, m, v):
    mn = ADAM_B1 * m + (1.0 - ADAM_B1) * g
    vn = ADAM_B2 * v + (1.0 - ADAM_B2) * (g * g)
    m_hat = mn / (1.0 - ADAM_B1 ** ADAM_STEP)
    v_hat = vn / (1.0 - ADAM_B2 ** ADAM_STEP)
    return -ADAM_LR * (m_hat / (jnp.sqrt(v_hat) + ADAM_EPS) + ADAM_WD * w), mn, vn


def _sum_adamw(recvs, w, m, v, lane0, tn, layer0=0, prev=None, own=None):
    _, r, c = w.shape
    j0 = lane0 // tn
    n = len(recvs)
    has_own = own is not None

    def body(*refs):
        mine_ref, refs = (refs[0], refs[1:]) if has_own else (None, refs)
        w_ref, m_ref, v_ref = refs[n:n + 3]
        g_ref, d_ref, mo_ref, vo_ref = refs[-4:]

        def run(r_ref):
            def slot(s):
                if has_own:
                    return jnp.where(mine_ref[0] == s, refs[n + 3][...], r_ref[s]).astype(F32)
                return r_ref[s].astype(F32)

            g = slot(0)
            for s in range(1, N_DEV):
                g = g + slot(s)
            g_ref[0] = g
            d_ref[0], mo_ref[0], vo_ref[0] = _adamw_math(w_ref[0], g, m_ref[0], v_ref[0])

        for i in range(n):
            pl.when(pl.program_id(0) == i)(functools.partial(run, refs[i]))

    slots = pl.BlockSpec((N_DEV, r, tn), lambda i, j, *_: (0, 0, j0 + j))
    blk = pl.BlockSpec((1, r, tn), lambda i, j, *_: (layer0 + i, 0, j))
    before = [] if prev is None else list(prev)
    in_specs, args = [slots] * n + [blk] * 3, [*recvs, w, m, v]
    if has_own:
        assert n == 1
        in_specs.append(pl.BlockSpec((r, tn), lambda i, j, mine: (mine[0], j0 + j)))
        args.append(own[0])
    n_pre = len(args) + has_own
    return pl.pallas_call(
        body, name="sum_adamw",
        grid_spec=pltpu.PrefetchScalarGridSpec(
            num_scalar_prefetch=int(has_own), grid=(n, c // tn),
            in_specs=in_specs + [ANY] * len(before), out_specs=[blk] * 4),
        out_shape=[SDS(w.shape, F32)] * 4,
        input_output_aliases={n_pre + k: k for k in range(len(before))},
        compiler_params=_params(("parallel", "parallel")),
    )(*([own[1]] if has_own else []), *args, *before)


def _adamw(w, g, m, v):
    rows, cols = w.shape
    tr = 256 if rows % 256 == 0 else rows

    def body(w_ref, g_ref, m_ref, v_ref, d_ref, mo_ref, vo_ref):
        d_ref[...], mo_ref[...], vo_ref[...] = _adamw_math(w_ref[...], g_ref[...], m_ref[...], v_ref[...])

    blk = pl.BlockSpec((tr, cols), lambda i: (i, 0))
    return pl.pallas_call(
        body, name="adamw", grid=(rows // tr,),
        in_specs=[blk] * 4, out_specs=[blk] * 3, out_shape=[SDS((rows, cols), F32)] * 3,
        compiler_params=_params(("parallel",)),
    )(w, g, m, v)


def _all_gather(shards):
    na = len(shards)
    chips = (4, 2, 6)

    def body(*refs):
        ins, outs = refs[:na], refs[na:2 * na]
        send_sems, recv_sems, local_sems = refs[2 * na:]
        _, mine = _flip(0)

        def rows(a, idx):
            r = shards[a].shape[0]
            return outs[a].at[pl.ds(pl.multiple_of(idx * r, 16), r), :]

        def copy(a, slot, block_idx, to, src=None):
            return pltpu.make_async_remote_copy(
                src_ref=rows(a, block_idx) if src is None else src, dst_ref=rows(a, block_idx),
                send_sem=send_sems.at[a, slot], recv_sem=recv_sems.at[a, slot],
                device_id=to, device_id_type=MESH_ID)

        sibling, sibling_idx = _flip(1)
        local, started = [], []
        for a in range(na):
            cp = pltpu.make_async_copy(ins[a], rows(a, mine), local_sems.at[a])
            cp.start()
            local.append(cp)
            first = [copy(a, 0, mine, sibling, src=ins[a])]
            first += [copy(a, 1 + j, mine, _flip(k)[0], src=ins[a]) for j, k in enumerate(chips)]
            for cp in first:
                cp.start()
            started += first
        for a in range(na):
            for j, k in enumerate(chips):
                _, theirs = _flip(k)
                copy(a, 1 + j, theirs, _flip(0)[0]).wait_recv()
                fwd = copy(a, 4 + j, theirs, sibling)
                fwd.start()
                started.append(fwd)
        for a in range(na):
            copy(a, 0, sibling_idx, _flip(0)[0]).wait_recv()
            for j, k in enumerate(chips):
                _, theirs = _flip(k | 1)
                copy(a, 4 + j, theirs, _flip(0)[0]).wait_recv()
        for cp in started:
            cp.wait_send()
        for cp in local:
            cp.wait()

    return pl.pallas_call(
        body, name="all_gather_weights",
        in_specs=[ANY] * na, out_specs=[ANY] * na,
        out_shape=[SDS((N_DEV * s.shape[0], s.shape[1]), s.dtype) for s in shards],
        scratch_shapes=[pltpu.SemaphoreType.DMA((na, 7)), pltpu.SemaphoreType.DMA((na, 7)),
                        pltpu.SemaphoreType.DMA((na,))],
        compiler_params=pltpu.CompilerParams(has_side_effects=True),
    )(*shards)


def _scatter_blocks_of(g_ref, rows, idx):
    return g_ref.at[pl.ds(pl.multiple_of(idx * rows, 16), rows), :]


def _scatter_start(g):
    rows = g.shape[0] // N_DEV
    land_shape = (N_DEV, rows, g.shape[1])

    def body(g_ref, land_ref, send_sems, recv_sems, g_thru, land_thru, token):
        _, mine = _flip(0)
        for k in range(1, N_DEV):
            peer, theirs = _flip(k)
            pltpu.make_async_remote_copy(
                src_ref=_scatter_blocks_of(g_ref, rows, theirs), dst_ref=land_ref.at[mine],
                send_sem=send_sems.at[k - 1], recv_sem=recv_sems.at[k - 1],
                device_id=peer, device_id_type=MESH_ID).start()
        token[...] = jnp.zeros_like(token)

    hbm, sem = pl.BlockSpec(memory_space=pltpu.HBM), pl.BlockSpec(memory_space=pltpu.SEMAPHORE)
    return pl.pallas_call(
        body, name="scatter_start",
        out_shape=(pltpu.SemaphoreType.DMA((N_DEV - 1,)), pltpu.SemaphoreType.DMA((N_DEV - 1,)),
                   pltpu.HBM(g.shape, g.dtype), pltpu.HBM(land_shape, g.dtype), SDS((8, 128), F32)),
        in_specs=(hbm, hbm), out_specs=(sem, sem, hbm, hbm, pl.BlockSpec(memory_space=pltpu.VMEM)),
        input_output_aliases={0: 2, 1: 3},
        compiler_params=pltpu.CompilerParams(has_side_effects=pltpu.SideEffectType.DATAFLOW_SIDE_EFFECTING),
    )(pltpu.with_memory_space_constraint(g, pltpu.HBM),
      pltpu.with_memory_space_constraint(lax.empty(land_shape, g.dtype), pltpu.HBM))


def _scatter_wait(send_sems, recv_sems, g_thru, land_thru, after):
    rows = g_thru.shape[0] // N_DEV

    def body(g_ref, land_ref, send_sems, recv_sems, *rest):
        me, _ = _flip(0)
        for k in range(1, N_DEV):
            _, theirs = _flip(k)
            copy = pltpu.make_async_remote_copy(
                src_ref=_scatter_blocks_of(g_ref, rows, theirs), dst_ref=land_ref.at[theirs],
                send_sem=send_sems.at[k - 1], recv_sem=recv_sems.at[k - 1],
                device_id=me, device_id_type=MESH_ID)
            copy.wait_send()
            copy.wait_recv()

    hbm, sem = pl.BlockSpec(memory_space=pltpu.HBM), pl.BlockSpec(memory_space=pltpu.SEMAPHORE)
    return pl.pallas_call(
        body, name="scatter_wait",
        out_shape=(pltpu.HBM(g_thru.shape, g_thru.dtype), pltpu.HBM(land_thru.shape, land_thru.dtype)),
        in_specs=(hbm, hbm, sem, sem) + (ANY,) * len(after), out_specs=(hbm, hbm), input_output_aliases={0: 0, 1: 1},
        compiler_params=pltpu.CompilerParams(has_side_effects=pltpu.SideEffectType.DATAFLOW_SIDE_EFFECTING),
    )(g_thru, land_thru, send_sems, recv_sems, *after)


def _all_reduce_small(packed):
    shape = packed.shape

    def body(p_ref, o_ref, slots, send_sems, recv_sems):
        me, mine = _flip(0)
        slots[mine] = p_ref[...]
        sends = []
        for k in range(1, N_DEV):
            peer, _ = _flip(k)
            cp = pltpu.make_async_remote_copy(
                src_ref=p_ref, dst_ref=slots.at[mine], send_sem=send_sems.at[k - 1], recv_sem=recv_sems.at[k - 1],
                device_id=peer, device_id_type=MESH_ID)
            cp.start()
            sends.append(cp)
        for k in range(1, N_DEV):
            _, theirs = _flip(k)
            pltpu.make_async_remote_copy(
                src_ref=p_ref, dst_ref=slots.at[theirs], send_sem=send_sems.at[k - 1],
                recv_sem=recv_sems.at[k - 1], device_id=me, device_id_type=MESH_ID).wait_recv()
        for cp in sends:
            cp.wait_send()
        acc = slots[0]
        for s in range(1, N_DEV):
            acc = acc + slots[s]
        o_ref[...] = acc

    vm = pl.BlockSpec(memory_space=pltpu.VMEM)
    return pl.pallas_call(
        body, name="all_reduce_small", in_specs=[vm], out_specs=vm, out_shape=SDS(shape, F32),
        scratch_shapes=[pltpu.VMEM((N_DEV,) + shape, F32), pltpu.SemaphoreType.DMA((7,)),
                        pltpu.SemaphoreType.DMA((7,))],
        compiler_params=pltpu.CompilerParams(has_side_effects=True),
    )(packed)


def _layer_fwd(x, p, tabs, ex):
    z, h_t, q, qt, k, v, vt = _in_proj(x, p["norm_g"], p["w_in_t"], p["qn"], p["kn"], tabs["ca"], tabs["sa"],
                                       tabs["ones"])
    oa, lse, *gathered = _attn_fwd(q, k, vt, ex)
    qrot, krot, vb, orr, on = _ret_fwd(z, p["lgf"], p["lgb"], p["gnw"], tabs["cr"], tabs["sr"])
    return z, h_t, q, qt, k, v, lse, oa, qrot, krot, vb, orr, on, gathered


def _layer_bwd(dxo, s, p, tabs, ex_attn, scatter_w_in):
    dya, dyb, dz_m, d_wout = _merge_bwd_out(dxo, s["z"], s["ya"], s["yb"], p["w_out"])
    doa, don, dz_m, d_wb_t = _merge_bwd_branch(dya, dyb, s["z"], s["oa"], s["on"], p["wb_t"], dz_m)
    dq_a, dk_a, dv_a, *recv_attn = _attn_bwd(s["q"], s["qt"], s["k"], s["v"], doa, s["oa"], s["lse"],
                                              ex_attn(d_wb_t, d_wout))
    dz_a, d_qn, d_kn = _attn_post_bwd(dq_a, dk_a, dv_a, s["z"], p["qn"], p["kn"], tabs["ca"], tabs["sa"],
                                      tabs["ones"])
    dq_r, dk_r, dv_r, d_gnw, d_lgf, d_lgb = _ret_bwd(s["qrot"], s["krot"], s["vb"], s["orr"], don, p["gnw"],
                                                     p["lgf"], p["lgb"])
    dqr, dkr, dvr = _ret_post_bwd(dq_r, dk_r, dv_r, tabs["cr"], tabs["sr"])
    buf = _dw_in(s["h_t"], dz_a, dz_m, dqr, dkr, dvr)
    pending, token = None, None
    if scatter_w_in:
        *pending, token = _scatter_start(buf)
    dx, d_norm_g = _in_bwd(dxo, s["x"], p["norm_g"], p["w_in_t"], dz_a, dz_m, dqr, dkr, dvr, token)
    grads = dict(w_in_t=buf, wb_t=d_wb_t, w_out=d_wout, norm_g=d_norm_g, gnw=d_gnw,
                 qn=d_qn.reshape(ATTN_Q_HEADS, ATTN_HEAD_DIM).sum(axis=0),
                 kn=d_kn.reshape(ATTN_KV_HEADS, ATTN_HEAD_DIM).sum(axis=0),
                 lgf=d_lgf[:, 0, 0], lgb=d_lgb[:, 0, 0])
    return dx, grads, recv_attn, pending


def _adamw_nd(w, g, m, v):
    shape = w.shape
    two_d = (1, shape[0]) if w.ndim == 1 else (-1, shape[-1])
    out = _adamw(w.reshape(two_d), g.reshape(two_d), m.reshape(two_d), v.reshape(two_d))
    return tuple(o.reshape(shape) for o in out)


def kernel(x, norm_g, w_in, attn_q_norm, attn_k_norm, ret_decay_fwd, ret_decay_bwd, ret_gn_w, w_branch_attn, w_branch_ret, w_out, final_norm_g, loss_target, m_norm_g, m_w_in, m_attn_q_norm, m_attn_k_norm, m_ret_decay_fwd, m_ret_decay_bwd, m_ret_gn_w, m_w_branch_attn, m_w_branch_ret, m_w_out, m_final_norm_g, v_norm_g, v_w_in, v_attn_q_norm, v_attn_k_norm, v_ret_decay_fwd, v_ret_decay_bwd, v_ret_gn_w, v_w_branch_attn, v_w_branch_ret, v_w_out, v_final_norm_g):
    t, d = x.shape[1], x.shape[2]
    x2, target = x[0], loss_target[0]

    w_in_sh, wb_sh, wout_sh = [], [], []
    for l in range(DEPTH):
        w_in_sh.append(jnp.swapaxes(w_in[l], 0, 1).astype(BF16))
        wb_sh.append(jnp.concatenate([w_branch_attn[l].T, w_branch_ret[l].T], axis=1).astype(BF16))
        wout_sh.append(w_out[l].astype(BF16))

    ca, sa = _rope_tables(t, ATTN_HEAD_DIM)
    cr, sr = _rope_tables(t, RET_HEAD_DIM)
    grp = jnp.arange(ATTN_WIDTH) // ATTN_HEAD_DIM
    tabs = dict(ca=jnp.tile(ca, (1, 2)), sa=jnp.tile(sa, (1, 2)), cr=cr, sr=sr,
                ones=jnp.where(grp[:, None] == grp[None, :], 1.0 / ATTN_HEAD_DIM, 0.0).astype(BF16))
    layers = []
    for l in range(DEPTH):
        layers.append(dict(
            norm_g=norm_g[l][None], qn=jnp.tile(attn_q_norm[l], ATTN_Q_HEADS)[None],
            kn=jnp.tile(attn_k_norm[l], ATTN_KV_HEADS)[None], gnw=ret_gn_w[l][None],
            lgf=jax.nn.log_sigmoid(ret_decay_fwd[l]), lgb=jax.nn.log_sigmoid(ret_decay_bwd[l])))

    layers[0]["w_in_t"], = _all_gather([w_in_sh[0]])
    gathers = [_Exchange("gather", [wb_sh[0], wout_sh[0], w_in_sh[1]]), _Exchange("gather", [wb_sh[1], wout_sh[1]])]
    h = x2
    saved = []
    for l in range(DEPTH):
        p = layers[l]
        z, h_t, q, qt, k, v, lse, oa, qrot, krot, vb, orr, on, got = _layer_fwd(h, p, tabs, gathers[l])
        p["wb_t"], p["w_out"] = got[0], got[1]
        if l == 0:
            layers[1]["w_in_t"] = got[2]
        xn, ya, yb = _merge_fwd(h, z, oa, on, p["wb_t"], p["w_out"])
        saved.append(dict(x=h, z=z, h_t=h_t, q=q, qt=qt, k=k, v=v, lse=lse, oa=oa, qrot=qrot, krot=krot, vb=vb,
                          orr=orr, on=on, ya=ya, yb=yb))
        h = xn
    dx, d_final_g, loss_part = _final_loss(h, final_norm_g[None], target)

    grads = [None] * DEPTH
    dx, grads[1], _, _ = _layer_bwd(dx, saved[1], layers[1], tabs, lambda *a: None, False)
    g1 = grads[1]
    ex_attn = lambda d_wb_t, d_wout: _Exchange("scatter", [g1["w_in_t"], g1["wb_t"], g1["w_out"], d_wb_t, d_wout])
    dx, grads[0], recv_attn, pending = _layer_bwd(dx, saved[0], layers[0], tabs, ex_attn, True)
    recv = [None, recv_attn[3], recv_attn[4], recv_attn[0], recv_attn[1], recv_attn[2]]
    tr = lambda a: jnp.swapaxes(a, 1, 2)
    w_in_t = (tr(w_in), tr(m_w_in), tr(v_w_in))
    sharded = {}
    w_in_l1 = _sum_adamw([recv[3]], *w_in_t, 0, 256, layer0=1)
    sharded[id(w_branch_attn)] = [tr(o) for o in _sum_adamw(
        [recv[1], recv[4]], tr(w_branch_attn), tr(m_w_branch_attn), tr(v_w_branch_attn), 0, 512)]
    sharded[id(w_branch_ret)] = [tr(o) for o in _sum_adamw(
        [recv[1], recv[4]], tr(w_branch_ret), tr(m_w_branch_ret), tr(v_w_branch_ret), 512, 512)]
    sharded[id(w_out)] = _sum_adamw([recv[2], recv[5]], w_out, m_w_out, v_w_out, 0, 256)
    g_wba, g_wbr, g_wout = (sharded[id(w)][0] for w in (w_branch_attn, w_branch_ret, w_out))

    packed = jnp.zeros((8, 1024), F32)
    for l in range(DEPTH):
        gl = grads[l]
        packed = packed.at[l].set(gl["norm_g"][0])
        packed = packed.at[2, 512 * l:512 * (l + 1)].set(gl["gnw"][0])
        packed = packed.at[4, 128 * l:128 * l + 64].set(gl["qn"])
        packed = packed.at[4, 256 + 128 * l:256 + 128 * l + 64].set(gl["kn"])
        packed = packed.at[4, 512 + 128 * l:512 + 128 * l + 4].set(gl["lgf"])
        packed = packed.at[4, 768 + 128 * l:768 + 128 * l + 4].set(gl["lgb"])
    packed = packed.at[3].set(d_final_g[0])
    packed = packed.at[5, 0].set(loss_part[0, 0])
    red = _all_reduce_small(packed)
    loss = red[5, 0]
    g_norm_g = red[0:2]
    g_gnw = red[2].reshape(DEPTH, RET_WIDTH)
    g_final = red[3]
    g_qn = jnp.stack([red[4, 128 * l:128 * l + 64] for l in range(DEPTH)])
    g_kn = jnp.stack([red[4, 256 + 128 * l:256 + 128 * l + 64] for l in range(DEPTH)])
    g_lgf = jnp.stack([red[4, 512 + 128 * l:512 + 128 * l + 4] for l in range(DEPTH)])
    g_lgb = jnp.stack([red[4, 768 + 128 * l:768 + 128 * l + 4] for l in range(DEPTH)])
    g_df = g_lgf * jax.nn.sigmoid(-ret_decay_fwd)
    g_db = g_lgb * jax.nn.sigmoid(-ret_decay_bwd)

    grad_w = [g_norm_g, None, g_qn, g_kn, g_df, g_db, g_gnw, g_wba, g_wbr, g_wout, g_final]
    weights = [norm_g, w_in, attn_q_norm, attn_k_norm, ret_decay_fwd, ret_decay_bwd, ret_gn_w, w_branch_attn,
               w_branch_ret, w_out, final_norm_g]
    ms = [m_norm_g, m_w_in, m_attn_q_norm, m_attn_k_norm, m_ret_decay_fwd, m_ret_decay_bwd, m_ret_gn_w,
          m_w_branch_attn, m_w_branch_ret, m_w_out, m_final_norm_g]
    vs = [v_norm_g, v_w_in, v_attn_q_norm, v_attn_k_norm, v_ret_decay_fwd, v_ret_decay_bwd, v_ret_gn_w,
          v_w_branch_attn, v_w_branch_ret, v_w_out, v_final_norm_g]
    upd = [None if w is w_in else sharded[id(w)][1:] if id(w) in sharded else _adamw_nd(w, g, m, v)
           for w, g, m, v in zip(weights, grad_w, ms, vs)]

    done = [dx, w_in_l1[0], g_wout] + [u[0] for w, u in zip(weights, upd) if u is not None and id(w) not in sharded]
    g_full, recv[0] = _scatter_wait(*pending, done)
    mine = (4 * lax.axis_index("x") + 2 * lax.axis_index("y") + lax.axis_index("c")).astype(jnp.int32)[None]
    w_in_upd = [tr(o) for o in _sum_adamw([recv[0]], *w_in_t, 0, 256, layer0=0, prev=w_in_l1, own=(g_full, mine))]
    grad_w[1], upd[1] = w_in_upd[0], w_in_upd[1:]
    return (loss, dx[None], *grad_w, *[u[0] for u in upd], *[u[1] for u in upd], *[u[2] for u in upd])
```

```python
import functools

import jax
import jax.numpy as jnp
from jax import lax
from jax.experimental import pallas as pl
from jax.experimental.pallas import tpu as pltpu

F32 = jnp.float32
BF16 = jnp.bfloat16
SDS = jax.ShapeDtypeStruct

D_MODEL = 1024
DEPTH = 2
GRID_W = 64
ATTN_Q_HEADS = 8
ATTN_KV_HEADS = 2
ATTN_HEAD_DIM = 64
ATTN_WIDTH = 512
ATTN_KV_WIDTH = 128
RET_HEADS = 4
RET_HEAD_DIM = 128
RET_WIDTH = 512
RET_CHUNK = 128
ATTN_KEY_CHUNK = 512
ATTN_BWD_KEY_CHUNK = 1024
QK_DOTS_PER_CHUNK = 4
EXP_LAG = 3
ROPE_THETA = 10000.0
EPS = 1e-6
D_IN = 5376
N_DEV = 8

ADAM_LR = 0.001
ADAM_B1 = 0.9
ADAM_B2 = 0.999
ADAM_EPS = 1e-08
ADAM_WD = 0.01
ADAM_STEP = 10

SEG = {
    "qa": (0, 512, 0),
    "ga": (768, 512, 512),
    "qr": (1280, 512, 1024),
    "kr": (1792, 512, 1536),
    "vr": (2304, 512, 2048),
    "gr": (2816, 512, 2560),
    "gm": (3328, 2048, 3072),
    "ka": (512, 128, 5120),
    "va": (640, 128, 5248),
}

VMEM_LIMIT = 60 * 1024 * 1024
NT = (((1,), (1,)), ((), ()))
TN = (((0,), (0,)), ((), ()))
MESH_ID = pl.DeviceIdType.MESH
ANY = pl.BlockSpec(memory_space=pl.ANY)


def _params(sem=None, vmem=VMEM_LIMIT):
    return pltpu.CompilerParams(dimension_semantics=sem, vmem_limit_bytes=vmem)


def _dot(a, b, dims=None):
    if dims is None:
        return jnp.dot(a, b, preferred_element_type=F32)
    return lax.dot_general(a, b, dims, preferred_element_type=F32)


def _sigmoid(x):
    return 1.0 / (1.0 + jnp.exp(-x))


def _swap_halves(x, q):
    n = x.shape[-1]
    axis = x.ndim - 1
    lane = lax.broadcasted_iota(jnp.int32, x.shape, axis)
    first = (lane % (2 * q)) < q
    return jnp.where(first, pltpu.roll(x, n - q, axis), pltpu.roll(x, q, axis))


def _rope(x, cos, sin_signed, q):
    return x * cos + _swap_halves(x, q) * sin_signed


def _rope_bwd(dy, cos, sin_signed, q):
    return dy * cos - _swap_halves(dy, q) * sin_signed


def _group_mean(v, ones_bd):
    hi = v.astype(BF16)
    lo = (v - hi.astype(F32)).astype(BF16)
    return _dot(hi, ones_bd) + _dot(lo, ones_bd)


def _rope_tables(t, head_dim):
    n_rows = t // GRID_W
    d_axis = head_dim // 2
    inv_freq = ROPE_THETA ** (-jnp.arange(0, d_axis, 2, dtype=F32) / d_axis)
    ar = jnp.arange(n_rows, dtype=F32)[:, None] * inv_freq
    ac = jnp.arange(GRID_W, dtype=F32)[:, None] * inv_freq
    by_row = lambda a: jnp.repeat(a, GRID_W, axis=0)
    by_col = lambda a: jnp.tile(a, (n_rows, 1))
    cr, sr, cc, sc = by_row(jnp.cos(ar)), by_row(jnp.sin(ar)), by_col(jnp.cos(ac)), by_col(jnp.sin(ac))
    return jnp.concatenate([cr, cr, cc, cc], axis=-1), jnp.concatenate([-sr, sr, -sc, sc], axis=-1)


def _me():
    return lax.axis_index("x"), lax.axis_index("y"), lax.axis_index("c")


def _flip(k):
    x, y, c = _me()
    px = 1 - x if k & 4 else x
    py = 1 - y if k & 2 else y
    pc = 1 - c if k & 1 else c
    return (px, py, pc), 4 * px + 2 * py + pc


class _Exchange:
    def __init__(self, kind, srcs):
        self.kind, self.srcs, self.n = kind, list(srcs), len(srcs)
        self.rows = [a.shape[0] if kind == "gather" else a.shape[0] // N_DEV for a in srcs]
        if kind == "gather":
            self.out_shape = [SDS((N_DEV * a.shape[0], a.shape[1]), a.dtype) for a in srcs]
        else:
            self.out_shape = [SDS((N_DEV, a.shape[0] // N_DEV, a.shape[1]), a.dtype) for a in srcs]
        self.scratch = [pltpu.SemaphoreType.DMA((self.n, N_DEV - 1)), pltpu.SemaphoreType.DMA((self.n, N_DEV - 1)),
                        pltpu.SemaphoreType.DMA((self.n,))]

    def _block(self, ref, a, idx):
        r = self.rows[a]
        return ref.at[pl.ds(pl.multiple_of(idx * r, 16), r), :]

    def _src(self, ins, a, idx):
        return ins[a] if self.kind == "gather" else self._block(ins[a], a, idx)

    def _dst(self, outs, a, idx):
        return self._block(outs[a], a, idx) if self.kind == "gather" else outs[a].at[idx]

    def _copies(self, ins, outs, sems):
        send_sems, recv_sems, local_sems = sems
        me, mine = _flip(0)
        local, sends, recvs = [], [], []
        for a in range(self.n):
            local.append(pltpu.make_async_copy(self._src(ins, a, mine), self._dst(outs, a, mine), local_sems.at[a]))
            for k in range(1, N_DEV):
                peer, theirs = _flip(k)
                sem = dict(send_sem=send_sems.at[a, k - 1], recv_sem=recv_sems.at[a, k - 1])
                sends.append(pltpu.make_async_remote_copy(
                    src_ref=self._src(ins, a, theirs), dst_ref=self._dst(outs, a, mine),
                    device_id=peer, device_id_type=MESH_ID, **sem))
                recvs.append(pltpu.make_async_remote_copy(
                    src_ref=self._dst(outs, a, theirs), dst_ref=self._dst(outs, a, theirs),
                    device_id=me, device_id_type=MESH_ID, **sem))
        return local, sends, recvs

    def start(self, ins, outs, sems):
        local, sends, _ = self._copies(ins, outs, sems)
        for cp in local + sends:
            cp.start()

    def wait(self, ins, outs, sems):
        local, sends, recvs = self._copies(ins, outs, sems)
        for cp in sends:
            cp.wait_send()
        for cp in recvs:
            cp.wait_recv()
        for cp in local:
            cp.wait()


def _with_exchange(body, n_in, n_out, n_scratch, ex, first, last):
    if ex is None:
        return body

    def wrapped(*refs):
        ins = refs[:n_in]
        ex_ins = refs[n_in:n_in + ex.n]
        outs = refs[n_in + ex.n:n_in + ex.n + n_out]
        ex_outs = refs[n_in + ex.n + n_out:n_in + 2 * ex.n + n_out]
        rest = refs[n_in + 2 * ex.n + n_out:]
        scratch, sems = rest[:n_scratch], rest[n_scratch:]

        @pl.when(first())
        def _():
            ex.start(ex_ins, ex_outs, sems)

        body(*ins, *outs, *scratch)

        @pl.when(last())
        def _():
            ex.wait(ex_ins, ex_outs, sems)

    return wrapped


def _ex_args(ex):
    if ex is None:
        return [], [], [], [], []
    return [ANY] * ex.n, [ANY] * ex.n, list(ex.out_shape), list(ex.scratch), list(ex.srcs)


def _in_proj(x, g, w_t, qn, kn, cos, sin, ones_bd):
    t, d = x.shape
    tm = min(256, t)
    tk = min(ATTN_KEY_CHUNK, t)
    per_chunk = tk // tm
    hd = ATTN_HEAD_DIM

    def body(x_ref, g_ref, w_ref, qn_ref, kn_ref, c_ref, s_ref, b_ref,
             z_ref, ht_ref, q_out, qt_out, k_out, v_out, vt_out):
        xv = x_ref[...]
        r = lax.rsqrt(jnp.mean(xv * xv, axis=-1, keepdims=True) + EPS)
        h = xv * r * g_ref[...]
        ht_ref[...] = h.T.astype(BF16)
        hb = h.astype(BF16)
        seg = {}
        for name, (nat, w, off) in SEG.items():
            seg[name] = _dot(hb, w_ref[nat:nat + w, :], NT)
            z_ref[:, off:off + w] = seg[name]

        bd = b_ref[...]
        c2, s2 = c_ref[...], s_ref[...]
        cq = jnp.concatenate([c2] * 4, axis=-1)
        sq = jnp.concatenate([s2] * 4, axis=-1)
        xq, xk, xvv = seg["qa"], seg["ka"], seg["va"]
        yq = xq * lax.rsqrt(_group_mean(xq * xq, bd) + EPS) * qn_ref[...]
        yq = _rope(yq, cq, sq, hd // 4) * (hd ** -0.5)
        yqt = yq.T
        for hh in range(ATTN_Q_HEADS):
            q_out[hh] = yq[:, hh * hd:(hh + 1) * hd].astype(BF16)
            qt_out[hh] = yqt[hh * hd:(hh + 1) * hd, :].astype(BF16)
        yk = xk * lax.rsqrt(_group_mean(xk * xk, bd[:ATTN_KV_WIDTH, :ATTN_KV_WIDTH]) + EPS) * kn_ref[...]
        yk = _rope(yk, c2, s2, hd // 4)
        xvt = xvv.T
        ones = jnp.ones((hd, tm), F32)
        for hh in range(ATTN_KV_HEADS):
            k_out[hh] = yk[:, hh * hd:(hh + 1) * hd].astype(BF16)
            v_out[hh] = xvv[:, hh * hd:(hh + 1) * hd].astype(BF16)
            vt_out[hh, 0] = jnp.concatenate([xvt[hh * hd:(hh + 1) * hd, :], ones], axis=0).astype(BF16)

    const = lambda shape: pl.BlockSpec(shape, lambda i: (0,) * len(shape))
    rows = lambda w: pl.BlockSpec((tm, w), lambda i: (i, 0))
    return pl.pallas_call(
        body, name="in_proj", grid=(t // tm,),
        in_specs=[rows(d), const((1, d)), const((D_IN, d)), const((1, 512)), const((1, 128)), rows(128), rows(128),
                  const((512, 512))],
        out_specs=[rows(D_IN), pl.BlockSpec((d, tm), lambda i: (0, i)),
                   pl.BlockSpec((ATTN_Q_HEADS, tm, hd), lambda i: (0, i, 0)),
                   pl.BlockSpec((ATTN_Q_HEADS, hd, tm), lambda i: (0, 0, i)),
                   pl.BlockSpec((ATTN_KV_HEADS, tm, hd), lambda i: (0, i, 0)),
                   pl.BlockSpec((ATTN_KV_HEADS, tm, hd), lambda i: (0, i, 0)),
                   pl.BlockSpec((ATTN_KV_HEADS, 1, 2 * hd, tm), lambda i: (0, i // per_chunk, 0, i % per_chunk))],
        out_shape=[SDS((t, D_IN), F32), SDS((d, t), BF16),
                   SDS((ATTN_Q_HEADS, t, hd), BF16), SDS((ATTN_Q_HEADS, hd, t), BF16),
                   SDS((ATTN_KV_HEADS, t, hd), BF16), SDS((ATTN_KV_HEADS, t, hd), BF16),
                   SDS((ATTN_KV_HEADS, t // tk, 2 * hd, tk), BF16)],
        compiler_params=_params(("parallel",)),
    )(x, g, w_t, qn, kn, cos, sin, ones_bd)


def _attn_fwd(q, k, vt, ex=None):
    t = q.shape[1]
    tq = min(256, t)
    nk, tk = vt.shape[1], vt.shape[3]
    hd = ATTN_HEAD_DIM
    g = ATTN_Q_HEADS // ATTN_KV_HEADS

    def body(q_ref, k_ref, vt_ref, o_ref, lse_ref, s_scr):
        def pass_a(h, c, m8):
            part = tk // QK_DOTS_PER_CHUNK
            for lo in range(c * tk, (c + 1) * tk, part):
                st = _dot(k_ref[0, lo:lo + part, :], q_ref[h], NT)
                s_scr[h % 2, lo:lo + part, :] = st
                m8 = jnp.maximum(m8, jnp.max(st.reshape(part // 8, 8, tq), axis=0))
            return m8

        def pass_b(h, c, m, acc, after):
            e = jnp.exp(s_scr[h % 2, c * tk:(c + 1) * tk, :] - (m + after * 0.0)).astype(BF16)
            return acc + _dot(vt_ref[0, c], e)

        neg = jnp.full((8, tq), -jnp.inf, F32)
        m8 = neg
        for c in range(nk):
            m8 = pass_a(0, c, m8)
        outs = []
        for h in range(g):
            m = jnp.max(m8, axis=0, keepdims=True)
            acc = jnp.zeros((2 * hd, tq), F32)
            m8 = neg
            done = [m] * EXP_LAG
            for c in range(nk):
                if h + 1 < g:
                    m8 = pass_a(h + 1, c, m8)
                acc = pass_b(h, c, m, acc, done[-EXP_LAG])
                done.append(m8[0:1, :] if h + 1 < g else acc[hd:hd + 1, :])
            l = acc[hd:hd + 1, :]
            outs.append((acc[:hd, :] / l).T)
            lse_ref[h] = m + jnp.log(l)
        o_ref[...] = jnp.concatenate(outs, axis=-1)

    nq = t // tq
    first = lambda: jnp.logical_and(pl.program_id(0) == 0, pl.program_id(1) == 0)
    last = lambda: jnp.logical_and(pl.program_id(0) == ATTN_KV_HEADS - 1, pl.program_id(1) == nq - 1)
    xi, xo, xs, xscr, xargs = _ex_args(ex)
    return pl.pallas_call(
        _with_exchange(body, 3, 2, 1, ex, first, last), name="attn_fwd", grid=(ATTN_KV_HEADS, nq),
        in_specs=[pl.BlockSpec((g, tq, hd), lambda p, i: (p, i, 0)),
                  pl.BlockSpec((1, t, hd), lambda p, i: (p, 0, 0)),
                  pl.BlockSpec((1, nk, 2 * hd, tk), lambda p, i: (p, 0, 0, 0))] + xi,
        out_specs=[pl.BlockSpec((tq, g * hd), lambda p, i: (i, p)),
                   pl.BlockSpec((g, 1, tq), lambda p, i: (p, 0, i))] + xo,
        out_shape=[SDS((t, ATTN_WIDTH), F32), SDS((ATTN_Q_HEADS, 1, t), F32)] + xs,
        scratch_shapes=[pltpu.VMEM((2, t, tq), F32)] + xscr,
        compiler_params=_params(("arbitrary", "arbitrary")),
    )(q, k, vt, *xargs)


class _Dir:
    def __init__(self, lg, strict_future):
        c = RET_CHUNK
        ia = lax.broadcasted_iota(jnp.int32, (c, c), 0).astype(F32)
        ib = lax.broadcasted_iota(jnp.int32, (c, c), 1).astype(F32)
        col = lax.broadcasted_iota(jnp.int32, (c, 1), 0).astype(F32)
        row = lax.broadcasted_iota(jnp.int32, (1, c), 1).astype(F32)
        if strict_future:
            dist = ib - ia
            mask = dist > 0
            self.wq, self.wk, wk_row = c - col, col, row
        else:
            dist = ia - ib
            mask = dist >= 0
            self.wq, self.wk, wk_row = col + 1.0, c - 1.0 - col, c - 1.0 - row
        self.dist = jnp.maximum(dist, 0.0)
        self.d = jnp.where(mask, jnp.exp(self.dist * lg), 0.0)
        self.qd = jnp.exp(self.wq * lg)
        self.kd_col = jnp.exp(self.wk * lg)
        self.kd_row = jnp.exp(wk_row * lg)
        self.cd = jnp.exp(jnp.full((1, 1), float(c), F32) * lg)


def _ret_fwd(z, lgf, lgb, gnw, cos, sin):
    t = z.shape[0]
    c = RET_CHUNK
    nc = t // c
    hd = RET_HEAD_DIM
    unroll = 4 if nc % 4 == 0 else 1

    def body(lgf_ref, lgb_ref, q_ref, k_ref, v_ref, c_ref, s_ref, w_ref,
             qo_ref, ko_ref, vo_ref, orr_ref, on_ref, kt, uf, ub, sfa, sba):
        h = pl.program_id(0)
        fw = _Dir(lgf_ref[h], False)
        bw = _Dir(lgb_ref[h], True)
        cc, ss = c_ref[...], s_ref[...]
        qo_ref[...] = _rope(q_ref[...], cc, ss, hd // 4).astype(BF16)
        kr = _rope(k_ref[...], cc, ss, hd // 4) * (hd ** -0.5)
        ko_ref[...] = kr.astype(BF16)
        vo_ref[...] = v_ref[...].astype(BF16)
        for i in range(nc):
            kt[i] = kr[i * c:(i + 1) * c, :].T.astype(BF16)

        def rows(ci):
            return pl.ds(pl.multiple_of(ci * c, c), c)

        def kv_products(ci, carry):
            vv = vo_ref[rows(ci), :]
            ktf = kt[ci].astype(F32)
            uf[ci] = _dot((ktf * fw.kd_row).astype(BF16), vv)
            ub[ci] = _dot((ktf * bw.kd_row).astype(BF16), vv)
            return carry

        lax.fori_loop(0, nc, kv_products, 0, unroll=unroll)

        def scan(i, carry):
            sf, sb = carry
            j = nc - 1 - i
            sfa[i] = sf.astype(BF16)
            sba[j] = sb.astype(BF16)
            return sf * fw.cd + uf[i], sb * bw.cd + ub[j]

        zero = jnp.zeros((hd, hd), F32)
        lax.fori_loop(0, nc, scan, (zero, zero))
        gw = w_ref[...]

        def outputs(ci, carry):
            sl = rows(ci)
            qq, kk, vv = qo_ref[sl, :], ko_ref[sl, :], vo_ref[sl, :]
            a = _dot(qq, kk, NT)
            o = (_dot((a * fw.d).astype(BF16), vv) + _dot(qq, sfa[ci]) * fw.qd
                 + _dot((a * bw.d).astype(BF16), vv) + _dot(qq, sba[ci]) * bw.qd)
            orr_ref[sl, :] = o
            xc = o - jnp.mean(o, axis=-1, keepdims=True)
            var = jnp.mean(xc * xc, axis=-1, keepdims=True)
            on_ref[sl, :] = xc * lax.rsqrt(var + EPS) * gw
            return carry

        lax.fori_loop(0, nc, outputs, 0, unroll=unroll)

    smem = pl.BlockSpec(memory_space=pltpu.SMEM)
    col = lambda name: (lambda h: (0, SEG[name][2] // 128 + h))
    head = pl.BlockSpec((t, 128), lambda h: (0, h))
    full = pl.BlockSpec((t, 128), lambda h: (0, 0))
    return pl.pallas_call(
        body, name="ret_fwd", grid=(RET_HEADS,),
        in_specs=[smem, smem, pl.BlockSpec((t, 128), col("qr")), pl.BlockSpec((t, 128), col("kr")),
                  pl.BlockSpec((t, 128), col("vr")), full, full, pl.BlockSpec((1, 128), lambda h: (0, h))],
        out_specs=[head, head, head, head, head],
        out_shape=[SDS((t, RET_WIDTH), BF16)] * 3 + [SDS((t, RET_WIDTH), F32)] * 2,
        scratch_shapes=[pltpu.VMEM((nc, hd, c), BF16), pltpu.VMEM((nc, hd, hd), F32), pltpu.VMEM((nc, hd, hd), F32),
                        pltpu.VMEM((nc, hd, hd), BF16), pltpu.VMEM((nc, hd, hd), BF16)],
        compiler_params=_params(("parallel",)),
    )(lgf, lgb, z, z, z, cos, sin, gnw)


def _merge_fwd(x, z, oa, on, wb_t, wout):
    t, d = x.shape
    tm = min(256, t)

    def body(x_ref, ga_ref, gr_ref, gm0_ref, gm1_ref, oa_ref, on_ref, wb_ref, wo_ref, xn_ref, ya_ref, yb_ref):
        ga, gr = ga_ref[...], gr_ref[...]
        ua = ga * _sigmoid(ga) * oa_ref[...]
        ub = gr * _sigmoid(gr) * on_ref[...]
        ya = _dot(ua.astype(BF16), wb_ref[:, :512], NT)
        yb = _dot(ub.astype(BF16), wb_ref[:, 512:], NT)
        ya_ref[...] = ya
        yb_ref[...] = yb
        merged = _sigmoid(gm0_ref[...]) * ya + _sigmoid(gm1_ref[...]) * yb
        xn_ref[...] = x_ref[...] + _dot(merged.astype(BF16), wo_ref[...])

    row = lambda w, j: pl.BlockSpec((tm, w), lambda i: (i, j))
    const = lambda shape: pl.BlockSpec(shape, lambda i: (0, 0))
    return pl.pallas_call(
        body, name="merge_fwd", grid=(t // tm,),
        in_specs=[row(d, 0), row(512, SEG["ga"][2] // 512), row(512, SEG["gr"][2] // 512),
                  row(1024, SEG["gm"][2] // 1024), row(1024, SEG["gm"][2] // 1024 + 1),
                  row(512, 0), row(512, 0), const((d, 1024)), const((d, d))],
        out_specs=[row(d, 0), row(d, 0), row(d, 0)],
        out_shape=[SDS((t, d), F32)] * 3,
        compiler_params=_params(("parallel",)),
    )(x, z, z, z, z, oa, on, wb_t, wout)


def _final_loss(x, g, target):
    t, d = x.shape
    tm = min(512, t)
    n = t // tm

    def body(x_ref, g_ref, t_ref, dx_ref, dg_ref, loss_ref, acc_g, acc_l):
        i = pl.program_id(0)

        @pl.when(i == 0)
        def _():
            acc_g[...] = jnp.zeros_like(acc_g)
            acc_l[...] = jnp.zeros_like(acc_l)

        xv, gv = x_ref[...], g_ref[...]
        r = lax.rsqrt(jnp.mean(xv * xv, axis=-1, keepdims=True) + EPS)
        xh = xv * r
        err = xh * gv - t_ref[...]
        dy = err * (1.0 / d)
        gy = dy * gv
        dx_ref[...] = r * (gy - xh * jnp.mean(gy * xh, axis=-1, keepdims=True))
        acc_g[...] += jnp.sum((dy * xh).reshape(tm // 8, 8, d), axis=0)
        acc_l[...] += jnp.sum((err * err).reshape(tm // 8, 8, d), axis=0)

        @pl.when(i == n - 1)
        def _():
            dg_ref[...] = jnp.sum(acc_g[...], axis=0, keepdims=True)
            tot = jnp.sum(jnp.sum(acc_l[...], axis=0, keepdims=True), axis=1, keepdims=True)
            loss_ref[...] = jnp.broadcast_to(tot * (0.5 / d), (1, 128))

    return pl.pallas_call(
        body, name="final_loss", grid=(n,),
        in_specs=[pl.BlockSpec((tm, d), lambda i: (i, 0)), pl.BlockSpec((1, d), lambda i: (0, 0)),
                  pl.BlockSpec((tm, d), lambda i: (i, 0))],
        out_specs=[pl.BlockSpec((tm, d), lambda i: (i, 0)), pl.BlockSpec((1, d), lambda i: (0, 0)),
                   pl.BlockSpec((1, 128), lambda i: (0, 0))],
        out_shape=[SDS((t, d), F32), SDS((1, d), F32), SDS((1, 128), F32)],
        scratch_shapes=[pltpu.VMEM((8, d), F32), pltpu.VMEM((8, d), F32)],
        compiler_params=_params(("arbitrary",)),
    )(x, g, target)


def _merge_bwd(dxo, z, oa, on, ya, yb, wb_t, wout):
    t, d = dxo.shape
    tm = min(256, t)
    n = t // tm

    def body(dx_ref, ga_ref, gr_ref, gm0_ref, gm1_ref, oa_ref, on_ref, ya_ref, yb_ref, wb_ref, wo_ref,
             doa_ref, don_ref, dz_ref, dwo_ref, dwb_ref, acc_o, acc_b):
        i = pl.program_id(0)

        @pl.when(i == 0)
        def _():
            acc_o[...] = jnp.zeros_like(acc_o)
            acc_b[...] = jnp.zeros_like(acc_b)

        dxb = dx_ref[...].astype(BF16)
        ya, yb = ya_ref[...], yb_ref[...]
        g0, g1 = _sigmoid(gm0_ref[...]), _sigmoid(gm1_ref[...])
        mb = (g0 * ya + g1 * yb).astype(BF16)
        dm = _dot(dxb, wo_ref[...], NT)
        dya = (dm * g0).astype(BF16)
        dyb = (dm * g1).astype(BF16)
        dz_ref[:, 1024:2048] = (dm * ya * g0 * (1.0 - g0)).astype(BF16)
        dz_ref[:, 2048:3072] = (dm * yb * g1 * (1.0 - g1)).astype(BF16)

        def branch(g_ref, o_ref, dy, w, do_ref, lo):
            gv, ov = g_ref[...], o_ref[...]
            sg = _sigmoid(gv)
            silu = gv * sg
            du = _dot(dy, w)
            do_ref[...] = du * silu
            dz_ref[:, lo:lo + 512] = (du * ov * (sg * (1.0 + gv * (1.0 - sg)))).astype(BF16)
            acc_b[:, lo:lo + 512] += _dot(dy, (silu * ov).astype(BF16), TN)

        branch(ga_ref, oa_ref, dya, wb_ref[:, :512], doa_ref, 0)
        branch(gr_ref, on_ref, dyb, wb_ref[:, 512:], don_ref, 512)
        acc_o[...] += _dot(mb, dxb, TN)

        @pl.when(i == n - 1)
        def _():
            dwo_ref[...] = acc_o[...].astype(BF16)
            dwb_ref[...] = acc_b[...].astype(BF16)

    row = lambda w, j: pl.BlockSpec((tm, w), lambda i: (i, j))
    const = lambda shape: pl.BlockSpec(shape, lambda i: (0, 0))
    return pl.pallas_call(
        body, name="merge_bwd", grid=(n,),
        in_specs=[row(d, 0), row(512, SEG["ga"][2] // 512), row(512, SEG["gr"][2] // 512),
                  row(1024, SEG["gm"][2] // 1024), row(1024, SEG["gm"][2] // 1024 + 1),
                  row(512, 0), row(512, 0), row(d, 0), row(d, 0), const((d, 1024)), const((d, d))],
        out_specs=[row(512, 0), row(512, 0), row(3072, 0), const((d, d)), const((d, 1024))],
        out_shape=[SDS((t, 512), F32), SDS((t, 512), F32), SDS((t, 3072), BF16), SDS((d, d), BF16),
                   SDS((d, 1024), BF16)],
        scratch_shapes=[pltpu.VMEM((d, d), F32), pltpu.VMEM((d, 1024), F32)],
        compiler_params=_params(("arbitrary",)),
    )(dxo, z, z, z, z, oa, on, ya, yb, wb_t, wout)


def _ret_bwd(qrot, krot, vb, orr, don, gnw, lgf, lgb):
    t = qrot.shape[0]
    c = RET_CHUNK
    nc = t // c
    hd = RET_HEAD_DIM
    unroll = 2 if nc % 2 == 0 else 1

    def body(lgf_ref, lgb_ref, q_ref, k_ref, v_ref, o_ref, dn_ref, w_ref,
             dq_ref, dk_ref, dv_ref, dw_ref, dlf_ref, dlb_ref, qt, kt, dob, uf, ub, wf, wb, sfa, sba, gfa, gba):
        h = pl.program_id(0)
        fw = _Dir(lgf_ref[h], False)
        bw = _Dir(lgb_ref[h], True)
        fw.dt, bw.dt = fw.d.T, bw.d.T

        o = o_ref[...]
        xc = o - jnp.mean(o, axis=-1, keepdims=True)
        r = lax.rsqrt(jnp.mean(xc * xc, axis=-1, keepdims=True) + EPS)
        xh = xc * r
        dn = dn_ref[...]
        gy = dn * w_ref[...]
        d_o = r * (gy - jnp.mean(gy, axis=-1, keepdims=True) - xh * jnp.mean(gy * xh, axis=-1, keepdims=True))
        dw_ref[...] = jnp.sum(dn * xh, axis=0, keepdims=True)
        dob[...] = d_o.astype(BF16)
        for i in range(nc):
            qt[i] = q_ref[i * c:(i + 1) * c, :].astype(F32).T.astype(BF16)
            kt[i] = k_ref[i * c:(i + 1) * c, :].astype(F32).T.astype(BF16)

        def rows(ci):
            return pl.ds(pl.multiple_of(ci * c, c), c)

        def products(ci, carry):
            sl = rows(ci)
            vv, do32 = v_ref[sl, :], dob[sl, :].astype(F32)
            ktf = kt[ci].astype(F32)
            uf[ci] = _dot((ktf * fw.kd_row).astype(BF16), vv)
            ub[ci] = _dot((ktf * bw.kd_row).astype(BF16), vv)
            wf[ci] = _dot(qt[ci], (do32 * fw.qd).astype(BF16))
            wb[ci] = _dot(qt[ci], (do32 * bw.qd).astype(BF16))
            return carry

        lax.fori_loop(0, nc, products, 0, unroll=unroll)

        def scan(i, carry):
            sf, sb, gf, gb = carry
            j = nc - 1 - i
            sfa[i] = sf.astype(BF16)
            sba[j] = sb.astype(BF16)
            gfa[j] = gf.astype(BF16)
            gba[i] = gb.astype(BF16)
            return sf * fw.cd + uf[i], sb * bw.cd + ub[j], gf * fw.cd + wf[j], gb * bw.cd + wb[i]

        zero = jnp.zeros((hd, hd), F32)
        lax.fori_loop(0, nc, scan, (zero, zero, zero, zero))

        def one_dir(p, s_all, g_all, ci, qq, kk, vv, do, a, bm):
            sb, gb = s_all[ci], g_all[ci]
            doq = (do.astype(F32) * p.qd).astype(BF16)
            dqc = _dot(doq, sb, NT)
            kkd = (kk.astype(F32) * p.kd_col).astype(BF16)
            dk2 = _dot(vv, gb, NT) * p.kd_col
            terms = (p.dist * p.d * a * bm + p.wq * qq.astype(F32) * dqc + p.wk * kk.astype(F32) * dk2
                     + (float(c) * p.cd) * gb.astype(F32) * sb.astype(F32))
            return dqc, dk2, _dot(kkd, gb), terms

        d_both, dt_both = fw.d + bw.d, fw.dt + bw.dt

        def chunk(ci, carry):
            af, ab = carry
            sl = rows(ci)
            qq, kk, vv, do = q_ref[sl, :], k_ref[sl, :], v_ref[sl, :], dob[sl, :]
            a, bm = _dot(qq, kk, NT), _dot(do, vv, NT)
            at, bt = _dot(kk, qq, NT), _dot(vv, do, NT)
            dqf, dkf, dvf, tf = one_dir(fw, sfa, gfa, ci, qq, kk, vv, do, a, bm)
            dqb, dkb, dvb, tb = one_dir(bw, sba, gba, ci, qq, kk, vv, do, a, bm)
            dq_ref[sl, :] = _dot((bm * d_both).astype(BF16), kk) + dqf + dqb
            dk_ref[sl, :] = _dot((bt * dt_both).astype(BF16), qq) + dkf + dkb
            dv_ref[sl, :] = _dot((at * dt_both).astype(BF16), do) + dvf + dvb
            return af + tf, ab + tb

        af, ab = lax.fori_loop(0, nc, chunk, (zero, zero), unroll=unroll)
        tot = lambda m: jnp.sum(jnp.sum(m, axis=0, keepdims=True), axis=1, keepdims=True)
        dlf_ref[...] = jnp.broadcast_to(tot(af).reshape(1, 1, 1), (1, 8, 128))
        dlb_ref[...] = jnp.broadcast_to(tot(ab).reshape(1, 1, 1), (1, 8, 128))

    smem = pl.BlockSpec(memory_space=pltpu.SMEM)
    head = pl.BlockSpec((t, 128), lambda h: (0, h))
    vec = pl.BlockSpec((1, 128), lambda h: (0, h))
    scal = pl.BlockSpec((1, 8, 128), lambda h: (h, 0, 0))
    mats = lambda dt: pltpu.VMEM((nc, hd, hd), dt)
    return pl.pallas_call(
        body, name="ret_bwd", grid=(RET_HEADS,),
        in_specs=[smem, smem, head, head, head, head, head, vec],
        out_specs=[head, head, head, vec, scal, scal],
        out_shape=[SDS((t, RET_WIDTH), F32)] * 3 + [SDS((1, RET_WIDTH), F32), SDS((RET_HEADS, 8, 128), F32),
                                                   SDS((RET_HEADS, 8, 128), F32)],
        scratch_shapes=[pltpu.VMEM((nc, hd, c), BF16), pltpu.VMEM((nc, hd, c), BF16), pltpu.VMEM((t, hd), BF16),
                        mats(F32), mats(F32), mats(F32), mats(F32), mats(BF16), mats(BF16), mats(BF16), mats(BF16)],
        compiler_params=_params(("parallel",)),
    )(lgf, lgb, qrot, krot, vb, orr, don, gnw)


def _ret_post_bwd(dq, dk, dv, cos, sin):
    t = dq.shape[0]
    tm = min(512, t)
    hd = RET_HEAD_DIM

    def body(dq_ref, dk_ref, dv_ref, c_ref, s_ref, oq_ref, ok_ref, ov_ref):
        cc = jnp.concatenate([c_ref[...]] * 4, axis=-1)
        ss = jnp.concatenate([s_ref[...]] * 4, axis=-1)
        oq_ref[...] = _rope_bwd(dq_ref[...], cc, ss, hd // 4).astype(BF16)
        ok_ref[...] = (_rope_bwd(dk_ref[...], cc, ss, hd // 4) * (hd ** -0.5)).astype(BF16)
        ov_ref[...] = dv_ref[...].astype(BF16)

    blk = pl.BlockSpec((tm, 512), lambda i: (i, 0))
    tab = pl.BlockSpec((tm, 128), lambda i: (i, 0))
    return pl.pallas_call(
        body, name="ret_post_bwd", grid=(t // tm,),
        in_specs=[blk, blk, blk, tab, tab], out_specs=[blk, blk, blk],
        out_shape=[SDS((t, 512), BF16)] * 3,
        compiler_params=_params(("parallel",)),
    )(dq, dk, dv, cos, sin)


def _attn_bwd(q, qt, k, v, doa, oa, lse, ex=None):
    t = q.shape[1]
    tq = min(256, t)
    nq = t // tq
    tk = min(ATTN_BWD_KEY_CHUNK, t)
    nk = t // tk
    hd = ATTN_HEAD_DIM
    scale = hd ** -0.5

    def body(q_ref, qt_ref, k_ref, v_ref, do_ref, o_ref, lse_ref, dq_ref, dkt_ref, dvt_ref):
        p, i = pl.program_id(0), pl.program_id(1)

        @pl.when(jnp.logical_and(p % 2 == 0, i == 0))
        def _():
            dkt_ref[...] = jnp.zeros_like(dkt_ref)
            dvt_ref[...] = jnp.zeros_like(dvt_ref)

        dov, ov = do_ref[...], o_ref[...]
        dovt = dov.T
        lanes = lambda col: jnp.concatenate([col] * (tk // 128), axis=1)
        outs = []
        for j in range(2):
            qq, qqt = q_ref[j], qt_ref[j]
            do32 = dov[:, j * hd:(j + 1) * hd]
            do, dot_ = do32.astype(BF16), dovt[j * hd:(j + 1) * hd, :].astype(BF16)
            dd = lanes(jnp.broadcast_to(jnp.sum(do32 * ov[:, j * hd:(j + 1) * hd], axis=1, keepdims=True), (tq, 128)))
            lse_j = lanes(jnp.broadcast_to(lse_ref[j], (128, tq)).T)
            dq = jnp.zeros((tq, hd), F32)
            for c in range(nk):
                sl = slice(c * tk, (c + 1) * tk)
                kc, vc = k_ref[0, sl, :], v_ref[0, sl, :]
                pr = jnp.exp(_dot(qq, kc, NT) - lse_j)
                ds = (pr * (_dot(do, vc, NT) - dd)).astype(BF16)
                dvt_ref[0, :, sl] += _dot(dot_, pr.astype(BF16))
                dkt_ref[0, :, sl] += _dot(qqt, ds)
                dq = dq + _dot(ds, kc)
            outs.append(dq * scale)
        dq_ref[...] = jnp.concatenate(outs, axis=-1)

    kv = pl.BlockSpec((1, t, hd), lambda p, i: (p // 2, 0, 0))
    kvt = pl.BlockSpec((1, hd, t), lambda p, i: (p // 2, 0, 0))
    pair = pl.BlockSpec((tq, 128), lambda p, i: (i, p))
    first = lambda: jnp.logical_and(pl.program_id(0) == 0, pl.program_id(1) == 0)
    last = lambda: jnp.logical_and(pl.program_id(0) == 3, pl.program_id(1) == nq - 1)
    xi, xo, xs, xscr, xargs = _ex_args(ex)
    return pl.pallas_call(
        _with_exchange(body, 7, 3, 0, ex, first, last), name="attn_bwd", grid=(4, nq),
        in_specs=[pl.BlockSpec((2, tq, hd), lambda p, i: (p, i, 0)), pl.BlockSpec((2, hd, tq), lambda p, i: (p, 0, i)),
                  kv, kv, pair, pair, pl.BlockSpec((2, 1, tq), lambda p, i: (p, 0, i))] + xi,
        out_specs=[pair, kvt, kvt] + xo,
        out_shape=[SDS((t, ATTN_WIDTH), F32), SDS((ATTN_KV_HEADS, hd, t), F32),
                   SDS((ATTN_KV_HEADS, hd, t), F32)] + xs,
        scratch_shapes=xscr,
        compiler_params=_params(("arbitrary", "arbitrary")),
    )(q, qt, k, v, doa, oa, lse, *xargs)


def _attn_post_bwd(dq, dk, dv, z, qn, kn, cos, sin, ones_bd):
    t = z.shape[0]
    tm = min(512, t)
    n = t // tm
    hd = ATTN_HEAD_DIM

    def body(dq_ref, dk_ref, dv_ref, zq_ref, zkv_ref, qn_ref, kn_ref, c_ref, s_ref, b_ref,
             dz_ref, dqn_ref, dkn_ref, acc_q, acc_k):
        i = pl.program_id(0)

        @pl.when(i == 0)
        def _():
            acc_q[...] = jnp.zeros_like(acc_q)
            acc_k[...] = jnp.zeros_like(acc_k)

        bd = b_ref[...]
        c2, s2 = c_ref[...], s_ref[...]

        def norm_bwd(dy, x, w, ones, cos_t, sin_t, acc):
            dyr = _rope_bwd(dy, cos_t, sin_t, hd // 4)
            r = lax.rsqrt(_group_mean(x * x, ones) + EPS)
            xh = x * r
            gy = dyr * w
            acc[...] += jnp.sum((dyr * xh).reshape(tm // 8, 8, x.shape[-1]), axis=0)
            return r * (gy - xh * _group_mean(gy * xh, ones))

        cq = jnp.concatenate([c2] * 4, axis=-1)
        sq = jnp.concatenate([s2] * 4, axis=-1)
        dz_ref[:, :512] = norm_bwd(dq_ref[...], zq_ref[...], qn_ref[...], bd, cq, sq, acc_q).astype(BF16)
        zkv = zkv_ref[...]
        dkk = jnp.concatenate([dk_ref[0], dk_ref[1]], axis=0).T
        dz_ref[:, 512:640] = norm_bwd(dkk, zkv[:, :128], kn_ref[...], bd[:128, :128], c2, s2, acc_k).astype(BF16)
        dz_ref[:, 640:768] = jnp.concatenate([dv_ref[0], dv_ref[1]], axis=0).T.astype(BF16)

        @pl.when(i == n - 1)
        def _():
            dqn_ref[...] = jnp.sum(acc_q[...], axis=0, keepdims=True)
            dkn_ref[...] = jnp.sum(acc_k[...], axis=0, keepdims=True)

    kv_blk = SEG["ka"][2] // 256
    kvs = pl.BlockSpec((ATTN_KV_HEADS, hd, tm), lambda i: (0, 0, i))
    const = lambda shape: pl.BlockSpec(shape, lambda i: (0, 0))
    return pl.pallas_call(
        body, name="attn_post_bwd", grid=(n,),
        in_specs=[pl.BlockSpec((tm, 512), lambda i: (i, 0)), kvs, kvs,
                  pl.BlockSpec((tm, 512), lambda i: (i, 0)), pl.BlockSpec((tm, 256), lambda i: (i, kv_blk)),
                  const((1, 512)), const((1, 128)),
                  pl.BlockSpec((tm, 128), lambda i: (i, 0)), pl.BlockSpec((tm, 128), lambda i: (i, 0)),
                  const((512, 512))],
        out_specs=[pl.BlockSpec((tm, 768), lambda i: (i, 0)), const((1, 512)), const((1, 128))],
        out_shape=[SDS((t, 768), BF16), SDS((1, 512), F32), SDS((1, 128), F32)],
        scratch_shapes=[pltpu.VMEM((8, 512), F32), pltpu.VMEM((8, 128), F32)],
        compiler_params=_params(("arbitrary",)),
    )(dq, dk, dv, z, z, qn, kn, cos, sin, ones_bd)


def _in_bwd(dxo, x, g, w_t, dz_a, dz_m, dqr, dkr, dvr, after=None):
    t, d = x.shape
    tm = min(256, t)
    n = t // tm
    parts = [(0, 0, 768, 0), (1, 0, 512, SEG["ga"][0]), (2, 0, 512, SEG["qr"][0]), (3, 0, 512, SEG["kr"][0]),
             (4, 0, 512, SEG["vr"][0]), (1, 512, 2560, SEG["gr"][0])]

    def body(dx_ref, x_ref, g_ref, w_ref, a_ref, m_ref, q_ref, k_ref, v_ref, o_ref, dg_ref, acc):
        i = pl.program_id(0)

        @pl.when(i == 0)
        def _():
            acc[...] = jnp.zeros_like(acc)

        pieces = [a_ref, m_ref, q_ref, k_ref, v_ref]
        dh = jnp.zeros((tm, d), F32)
        for pi, lo, w, row in parts:
            dh = dh + _dot(pieces[pi][:, lo:lo + w], w_ref[row:row + w, :])
        xv = x_ref[...]
        r = lax.rsqrt(jnp.mean(xv * xv, axis=-1, keepdims=True) + EPS)
        xh = xv * r
        gy = dh * g_ref[...]
        o_ref[...] = dx_ref[...] + r * (gy - xh * jnp.mean(gy * xh, axis=-1, keepdims=True))
        acc[...] += jnp.sum((dh * xh).reshape(tm // 8, 8, d), axis=0)

        @pl.when(i == n - 1)
        def _():
            dg_ref[...] = jnp.sum(acc[...], axis=0, keepdims=True)

    row = lambda w: pl.BlockSpec((tm, w), lambda i: (i, 0))
    const = lambda shape: pl.BlockSpec(shape, lambda i: (0, 0))
    extra = [] if after is None else [after]
    return pl.pallas_call(
        (lambda *refs: body(*refs[:9], *refs[9 + len(extra):])), name="in_bwd", grid=(n,),
        in_specs=[row(d), row(d), const((1, d)), const((D_IN, d)), row(768), row(3072), row(512), row(512),
                  row(512)] + [const(a.shape) for a in extra],
        out_specs=[row(d), const((1, d))],
        out_shape=[SDS((t, d), F32), SDS((1, d), F32)],
        scratch_shapes=[pltpu.VMEM((8, d), F32)],
        compiler_params=_params(("arbitrary",)),
    )(dxo, x, g, w_t, dz_a, dz_m, dqr, dkr, dvr, *extra)


def _dw_in(h_t, dz_a, dz_m, dqr, dkr, dvr):
    d, t = h_t.shape
    tn = 256
    parts = [(0, 0, 0, 3), (1, 0, SEG["ga"][0] // tn, 2), (2, 0, SEG["qr"][0] // tn, 2),
             (3, 0, SEG["kr"][0] // tn, 2), (4, 0, SEG["vr"][0] // tn, 2), (1, 2, SEG["gr"][0] // tn, 10)]
    pieces = [dz_a, dz_m, dqr, dkr, dvr]

    def col_block(pi):
        mine = [(c0, r0, n) for q, c0, r0, n in parts if q == pi]

        def index(j):
            c0, r0, n = mine[0]
            blk = c0 + jnp.clip(j - r0, 0, n - 1)
            for c0, r0, n in mine[1:]:
                blk = jnp.where(j >= r0, c0 + jnp.clip(j - r0, 0, n - 1), blk)
            return 0, blk

        return index

    def body(h_ref, *refs):
        o_ref = refs[-1]
        j = pl.program_id(0)
        for pi, _, r0, n in parts:
            @pl.when(jnp.logical_and(j >= r0, j < r0 + n))
            def _(p_ref=refs[pi]):
                o_ref[...] = _dot(h_ref[...], p_ref[...]).T.astype(BF16)

    return pl.pallas_call(
        body, name="dw_in", grid=(D_IN // tn,),
        in_specs=[pl.BlockSpec((d, t), lambda j: (0, 0))] + [pl.BlockSpec((t, tn), col_block(pi)) for pi in range(5)],
        out_specs=pl.BlockSpec((tn, d), lambda j: (j, 0)),
        out_shape=SDS((D_IN, d), BF16),
        compiler_params=_params(("arbitrary",)),
    )(h_t, *pieces)


def _adamw_math(w, g, m, v):
    mn = ADAM_B1 * m + (1.0 - ADAM_B1) * g
    vn = ADAM_B2 * v + (1.0 - ADAM_B2) * (g * g)
    m_hat = mn / (1.0 - ADAM_B1 ** ADAM_STEP)
    v_hat = vn / (1.0 - ADAM_B2 ** ADAM_STEP)
    return -ADAM_LR * (m_hat / (jnp.sqrt(v_hat) + ADAM_EPS) + ADAM_WD * w), mn, vn


def _sum_adamw(recvs, w, m, v, lane0, tn, layer0=0, prev=None, own=None):
    _, r, c = w.shape
    j0 = lane0 // tn
    n = len(recvs)
    has_own = own is not None

    def body(*refs):
        mine_ref, refs = (refs[0], refs[1:]) if has_own else (None, refs)
        w_ref, m_ref, v_ref = refs[n:n + 3]
        g_ref, d_ref, mo_ref, vo_ref = refs[-4:]

        def run(r_ref):
            def slot(s):
                if has_own:
                    return jnp.where(mine_ref[0] == s, refs[n + 3][...], r_ref[s]).astype(F32)
                return r_ref[s].astype(F32)

            g = slot(0)
            for s in range(1, N_DEV):
                g = g + slot(s)
            g_ref[0] = g
            d_ref[0], mo_ref[0], vo_ref[0] = _adamw_math(w_ref[0], g, m_ref[0], v_ref[0])

        for i in range(n):
            pl.when(pl.program_id(0) == i)(functools.partial(run, refs[i]))

    slots = pl.BlockSpec((N_DEV, r, tn), lambda i, j, *_: (0, 0, j0 + j))
    blk = pl.BlockSpec((1, r, tn), lambda i, j, *_: (layer0 + i, 0, j))
    before = [] if prev is None else list(prev)
    in_specs, args = [slots] * n + [blk] * 3, [*recvs, w, m, v]
    if has_own:
        assert n == 1
        in_specs.append(pl.BlockSpec((r, tn), lambda i, j, mine: (mine[0], j0 + j)))
        args.append(own[0])
    n_pre = len(args) + has_own
    return pl.pallas_call(
        body, name="sum_adamw",
        grid_spec=pltpu.PrefetchScalarGridSpec(
            num_scalar_prefetch=int(has_own), grid=(n, c // tn),
            in_specs=in_specs + [ANY] * len(before), out_specs=[blk] * 4),
        out_shape=[SDS(w.shape, F32)] * 4,
        input_output_aliases={n_pre + k: k for k in range(len(before))},
        compiler_params=_params(("parallel", "parallel")),
    )(*([own[1]] if has_own else []), *args, *before)


def _adamw(w, g, m, v):
    rows, cols = w.shape
    tr = 256 if rows % 256 == 0 else rows

    def body(w_ref, g_ref, m_ref, v_ref, d_ref, mo_ref, vo_ref):
        d_ref[...], mo_ref[...], vo_ref[...] = _adamw_math(w_ref[...], g_ref[...], m_ref[...], v_ref[...])

    blk = pl.BlockSpec((tr, cols), lambda i: (i, 0))
    return pl.pallas_call(
        body, name="adamw", grid=(rows // tr,),
        in_specs=[blk] * 4, out_specs=[blk] * 3, out_shape=[SDS((rows, cols), F32)] * 3,
        compiler_params=_params(("parallel",)),
    )(w, g, m, v)


def _all_gather(shards):
    na = len(shards)
    chips = (4, 2, 6)

    def body(*refs):
        ins, outs = refs[:na], refs[na:2 * na]
        send_sems, recv_sems, local_sems = refs[2 * na:]
        _, mine = _flip(0)

        def rows(a, idx):
            r = shards[a].shape[0]
            return outs[a].at[pl.ds(pl.multiple_of(idx * r, 16), r), :]

        def copy(a, slot, block_idx, to, src=None):
            return pltpu.make_async_remote_copy(
                src_ref=rows(a, block_idx) if src is None else src, dst_ref=rows(a, block_idx),
                send_sem=send_sems.at[a, slot], recv_sem=recv_sems.at[a, slot],
                device_id=to, device_id_type=MESH_ID)

        sibling, sibling_idx = _flip(1)
        local, started = [], []
        for a in range(na):
            cp = pltpu.make_async_copy(ins[a], rows(a, mine), local_sems.at[a])
            cp.start()
            local.append(cp)
            first = [copy(a, 0, mine, sibling, src=ins[a])]
            first += [copy(a, 1 + j, mine, _flip(k)[0], src=ins[a]) for j, k in enumerate(chips)]
            for cp in first:
                cp.start()
            started += first
        for a in range(na):
            for j, k in enumerate(chips):
                _, theirs = _flip(k)
                copy(a, 1 + j, theirs, _flip(0)[0]).wait_recv()
                fwd = copy(a, 4 + j, theirs, sibling)
                fwd.start()
                started.append(fwd)
        for a in range(na):
            copy(a, 0, sibling_idx, _flip(0)[0]).wait_recv()
            for j, k in enumerate(chips):
                _, theirs = _flip(k | 1)
                copy(a, 4 + j, theirs, _flip(0)[0]).wait_recv()
        for cp in started:
            cp.wait_send()
        for cp in local:
            cp.wait()

    return pl.pallas_call(
        body, name="all_gather_weights",
        in_specs=[ANY] * na, out_specs=[ANY] * na,
        out_shape=[SDS((N_DEV * s.shape[0], s.shape[1]), s.dtype) for s in shards],
        scratch_shapes=[pltpu.SemaphoreType.DMA((na, 7)), pltpu.SemaphoreType.DMA((na, 7)),
                        pltpu.SemaphoreType.DMA((na,))],
        compiler_params=pltpu.CompilerParams(has_side_effects=True),
    )(*shards)


def _scatter_blocks_of(g_ref, rows, idx):
    return g_ref.at[pl.ds(pl.multiple_of(idx * rows, 16), rows), :]


def _scatter_start(g):
    rows = g.shape[0] // N_DEV
    land_shape = (N_DEV, rows, g.shape[1])

    def body(g_ref, land_ref, send_sems, recv_sems, g_thru, land_thru, token):
        _, mine = _flip(0)
        for k in range(1, N_DEV):
            peer, theirs = _flip(k)
            pltpu.make_async_remote_copy(
                src_ref=_scatter_blocks_of(g_ref, rows, theirs), dst_ref=land_ref.at[mine],
                send_sem=send_sems.at[k - 1], recv_sem=recv_sems.at[k - 1],
                device_id=peer, device_id_type=MESH_ID).start()
        token[...] = jnp.zeros_like(token)

    hbm, sem = pl.BlockSpec(memory_space=pltpu.HBM), pl.BlockSpec(memory_space=pltpu.SEMAPHORE)
    return pl.pallas_call(
        body, name="scatter_start",
        out_shape=(pltpu.SemaphoreType.DMA((N_DEV - 1,)), pltpu.SemaphoreType.DMA((N_DEV - 1,)),
                   pltpu.HBM(g.shape, g.dtype), pltpu.HBM(land_shape, g.dtype), SDS((8, 128), F32)),
        in_specs=(hbm, hbm), out_specs=(sem, sem, hbm, hbm, pl.BlockSpec(memory_space=pltpu.VMEM)),
        input_output_aliases={0: 2, 1: 3},
        compiler_params=pltpu.CompilerParams(has_side_effects=pltpu.SideEffectType.DATAFLOW_SIDE_EFFECTING),
    )(pltpu.with_memory_space_constraint(g, pltpu.HBM),
      pltpu.with_memory_space_constraint(lax.empty(land_shape, g.dtype), pltpu.HBM))


def _scatter_wait(send_sems, recv_sems, g_thru, land_thru, after):
    rows = g_thru.shape[0] // N_DEV

    def body(g_ref, land_ref, send_sems, recv_sems, *rest):
        me, _ = _flip(0)
        for k in range(1, N_DEV):
            _, theirs = _flip(k)
            copy = pltpu.make_async_remote_copy(
                src_ref=_scatter_blocks_of(g_ref, rows, theirs), dst_ref=land_ref.at[theirs],
                send_sem=send_sems.at[k - 1], recv_sem=recv_sems.at[k - 1],
                device_id=me, device_id_type=MESH_ID)
            copy.wait_send()
            copy.wait_recv()

    hbm, sem = pl.BlockSpec(memory_space=pltpu.HBM), pl.BlockSpec(memory_space=pltpu.SEMAPHORE)
    return pl.pallas_call(
        body, name="scatter_wait",
        out_shape=(pltpu.HBM(g_thru.shape, g_thru.dtype), pltpu.HBM(land_thru.shape, land_thru.dtype)),
        in_specs=(hbm, hbm, sem, sem) + (ANY,) * len(after), out_specs=(hbm, hbm), input_output_aliases={0: 0, 1: 1},
        compiler_params=pltpu.CompilerParams(has_side_effects=pltpu.SideEffectType.DATAFLOW_SIDE_EFFECTING),
    )(g_thru, land_thru, send_sems, recv_sems, *after)


def _all_reduce_small(packed):
    shape = packed.shape

    def body(p_ref, o_ref, slots, send_sems, recv_sems):
        me, mine = _flip(0)
        slots[mine] = p_ref[...]
        sends = []
        for k in range(1, N_DEV):
            peer, _ = _flip(k)
            cp = pltpu.make_async_remote_copy(
                src_ref=p_ref, dst_ref=slots.at[mine], send_sem=send_sems.at[k - 1], recv_sem=recv_sems.at[k - 1],
                device_id=peer, device_id_type=MESH_ID)
            cp.start()
            sends.append(cp)
        for k in range(1, N_DEV):
            _, theirs = _flip(k)
            pltpu.make_async_remote_copy(
                src_ref=p_ref, dst_ref=slots.at[theirs], send_sem=send_sems.at[k - 1],
                recv_sem=recv_sems.at[k - 1], device_id=me, device_id_type=MESH_ID).wait_recv()
        for cp in sends:
            cp.wait_send()
        acc = slots[0]
        for s in range(1, N_DEV):
            acc = acc + slots[s]
        o_ref[...] = acc

    vm = pl.BlockSpec(memory_space=pltpu.VMEM)
    return pl.pallas_call(
        body, name="all_reduce_small", in_specs=[vm], out_specs=vm, out_shape=SDS(shape, F32),
        scratch_shapes=[pltpu.VMEM((N_DEV,) + shape, F32), pltpu.SemaphoreType.DMA((7,)),
                        pltpu.SemaphoreType.DMA((7,))],
        compiler_params=pltpu.CompilerParams(has_side_effects=True),
    )(packed)


def _layer_fwd(x, p, tabs, ex):
    z, h_t, q, qt, k, v, vt = _in_proj(x, p["norm_g"], p["w_in_t"], p["qn"], p["kn"], tabs["ca"], tabs["sa"],
                                       tabs["ones"])
    oa, lse, *gathered = _attn_fwd(q, k, vt, ex)
    qrot, krot, vb, orr, on = _ret_fwd(z, p["lgf"], p["lgb"], p["gnw"], tabs["cr"], tabs["sr"])
    return z, h_t, q, qt, k, v, lse, oa, qrot, krot, vb, orr, on, gathered


def _layer_bwd(dxo, s, p, tabs, ex_attn, scatter_w_in):
    doa, don, dz_m, d_wout, d_wb_t = _merge_bwd(dxo, s["z"], s["oa"], s["on"], s["ya"], s["yb"], p["wb_t"], p["w_out"])
    dq_a, dk_a, dv_a, *recv_attn = _attn_bwd(s["q"], s["qt"], s["k"], s["v"], doa, s["oa"], s["lse"],
                                              ex_attn(d_wb_t, d_wout))
    dz_a, d_qn, d_kn = _attn_post_bwd(dq_a, dk_a, dv_a, s["z"], p["qn"], p["kn"], tabs["ca"], tabs["sa"],
                                      tabs["ones"])
    dq_r, dk_r, dv_r, d_gnw, d_lgf, d_lgb = _ret_bwd(s["qrot"], s["krot"], s["vb"], s["orr"], don, p["gnw"],
                                                     p["lgf"], p["lgb"])
    dqr, dkr, dvr = _ret_post_bwd(dq_r, dk_r, dv_r, tabs["cr"], tabs["sr"])
    buf = _dw_in(s["h_t"], dz_a, dz_m, dqr, dkr, dvr)
    pending, token = None, None
    if scatter_w_in:
        *pending, token = _scatter_start(buf)
    dx, d_norm_g = _in_bwd(dxo, s["x"], p["norm_g"], p["w_in_t"], dz_a, dz_m, dqr, dkr, dvr, token)
    grads = dict(w_in_t=buf, wb_t=d_wb_t, w_out=d_wout, norm_g=d_norm_g, gnw=d_gnw,
                 qn=d_qn.reshape(ATTN_Q_HEADS, ATTN_HEAD_DIM).sum(axis=0),
                 kn=d_kn.reshape(ATTN_KV_HEADS, ATTN_HEAD_DIM).sum(axis=0),
                 lgf=d_lgf[:, 0, 0], lgb=d_lgb[:, 0, 0])
    return dx, grads, recv_attn, pending


def _adamw_nd(w, g, m, v):
    shape = w.shape
    two_d = (1, shape[0]) if w.ndim == 1 else (-1, shape[-1])
    out = _adamw(w.reshape(two_d), g.reshape(two_d), m.reshape(two_d), v.reshape(two_d))
    return tuple(o.reshape(shape) for o in out)


def kernel(x, norm_g, w_in, attn_q_norm, attn_k_norm, ret_decay_fwd, ret_decay_bwd, ret_gn_w, w_branch_attn, w_branch_ret, w_out, final_norm_g, loss_target, m_norm_g, m_w_in, m_attn_q_norm, m_attn_k_norm, m_ret_decay_fwd, m_ret_decay_bwd, m_ret_gn_w, m_w_branch_attn, m_w_branch_ret, m_w_out, m_final_norm_g, v_norm_g, v_w_in, v_attn_q_norm, v_attn_k_norm, v_ret_decay_fwd, v_ret_decay_bwd, v_ret_gn_w, v_w_branch_attn, v_w_branch_ret, v_w_out, v_final_norm_g):
    t, d = x.shape[1], x.shape[2]
    x2, target = x[0], loss_target[0]

    w_in_sh, wb_sh, wout_sh = [], [], []
    for l in range(DEPTH):
        w_in_sh.append(jnp.swapaxes(w_in[l], 0, 1).astype(BF16))
        wb_sh.append(jnp.concatenate([w_branch_attn[l].T, w_branch_ret[l].T], axis=1).astype(BF16))
        wout_sh.append(w_out[l].astype(BF16))

    ca, sa = _rope_tables(t, ATTN_HEAD_DIM)
    cr, sr = _rope_tables(t, RET_HEAD_DIM)
    grp = jnp.arange(ATTN_WIDTH) // ATTN_HEAD_DIM
    tabs = dict(ca=jnp.tile(ca, (1, 2)), sa=jnp.tile(sa, (1, 2)), cr=cr, sr=sr,
                ones=jnp.where(grp[:, None] == grp[None, :], 1.0 / ATTN_HEAD_DIM, 0.0).astype(BF16))
    layers = []
    for l in range(DEPTH):
        layers.append(dict(
            norm_g=norm_g[l][None], qn=jnp.tile(attn_q_norm[l], ATTN_Q_HEADS)[None],
            kn=jnp.tile(attn_k_norm[l], ATTN_KV_HEADS)[None], gnw=ret_gn_w[l][None],
            lgf=jax.nn.log_sigmoid(ret_decay_fwd[l]), lgb=jax.nn.log_sigmoid(ret_decay_bwd[l])))

    layers[0]["w_in_t"], = _all_gather([w_in_sh[0]])
    gathers = [_Exchange("gather", [wb_sh[0], wout_sh[0], w_in_sh[1]]), _Exchange("gather", [wb_sh[1], wout_sh[1]])]
    h = x2
    saved = []
    for l in range(DEPTH):
        p = layers[l]
        z, h_t, q, qt, k, v, lse, oa, qrot, krot, vb, orr, on, got = _layer_fwd(h, p, tabs, gathers[l])
        p["wb_t"], p["w_out"] = got[0], got[1]
        if l == 0:
            layers[1]["w_in_t"] = got[2]
        xn, ya, yb = _merge_fwd(h, z, oa, on, p["wb_t"], p["w_out"])
        saved.append(dict(x=h, z=z, h_t=h_t, q=q, qt=qt, k=k, v=v, lse=lse, oa=oa, qrot=qrot, krot=krot, vb=vb,
                          orr=orr, on=on, ya=ya, yb=yb))
        h = xn
    dx, d_final_g, loss_part = _final_loss(h, final_norm_g[None], target)

    grads = [None] * DEPTH
    dx, grads[1], _, _ = _layer_bwd(dx, saved[1], layers[1], tabs, lambda *a: None, False)
    g1 = grads[1]
    ex_attn = lambda d_wb_t, d_wout: _Exchange("scatter", [g1["w_in_t"], g1["wb_t"], g1["w_out"], d_wb_t, d_wout])
    dx, grads[0], recv_attn, pending = _layer_bwd(dx, saved[0], layers[0], tabs, ex_attn, True)
    recv = [None, recv_attn[3], recv_attn[4], recv_attn[0], recv_attn[1], recv_attn[2]]
    tr = lambda a: jnp.swapaxes(a, 1, 2)
    w_in_t = (tr(w_in), tr(m_w_in), tr(v_w_in))
    sharded = {}
    w_in_l1 = _sum_adamw([recv[3]], *w_in_t, 0, 256, layer0=1)
    sharded[id(w_branch_attn)] = [tr(o) for o in _sum_adamw(
        [recv[1], recv[4]], tr(w_branch_attn), tr(m_w_branch_attn), tr(v_w_branch_attn), 0, 512)]
    sharded[id(w_branch_ret)] = [tr(o) for o in _sum_adamw(
        [recv[1], recv[4]], tr(w_branch_ret), tr(m_w_branch_ret), tr(v_w_branch_ret), 512, 512)]
    sharded[id(w_out)] = _sum_adamw([recv[2], recv[5]], w_out, m_w_out, v_w_out, 0, 256)
    g_wba, g_wbr, g_wout = (sharded[id(w)][0] for w in (w_branch_attn, w_branch_ret, w_out))

    packed = jnp.zeros((8, 1024), F32)
    for l in range(DEPTH):
        gl = grads[l]
        packed = packed.at[l].set(gl["norm_g"][0])
        packed = packed.at[2, 512 * l:512 * (l + 1)].set(gl["gnw"][0])
        packed = packed.at[4, 128 * l:128 * l + 64].set(gl["qn"])
        packed = packed.at[4, 256 + 128 * l:256 + 128 * l + 64].set(gl["kn"])
        packed = packed.at[4, 512 + 128 * l:512 + 128 * l + 4].set(gl["lgf"])
        packed = packed.at[4, 768 + 128 * l:768 + 128 * l + 4].set(gl["lgb"])
    packed = packed.at[3].set(d_final_g[0])
    packed = packed.at[5, 0].set(loss_part[0, 0])
    red = _all_reduce_small(packed)
    loss = red[5, 0]
    g_norm_g = red[0:2]
    g_gnw = red[2].reshape(DEPTH, RET_WIDTH)
    g_final = red[3]
    g_qn = jnp.stack([red[4, 128 * l:128 * l + 64] for l in range(DEPTH)])
    g_kn = jnp.stack([red[4, 256 + 128 * l:256 + 128 * l + 64] for l in range(DEPTH)])
    g_lgf = jnp.stack([red[4, 512 + 128 * l:512 + 128 * l + 4] for l in range(DEPTH)])
    g_lgb = jnp.stack([red[4, 768 + 128 * l:768 + 128 * l + 4] for l in range(DEPTH)])
    g_df = g_lgf * jax.nn.sigmoid(-ret_decay_fwd)
    g_db = g_lgb * jax.nn.sigmoid(-ret_decay_bwd)

    grad_w = [g_norm_g, None, g_qn, g_kn, g_df, g_db, g_gnw, g_wba, g_wbr, g_wout, g_final]
    weights = [norm_g, w_in, attn_q_norm, attn_k_norm, ret_decay_fwd, ret_decay_bwd, ret_gn_w, w_branch_attn,
               w_branch_ret, w_out, final_norm_g]
    ms = [m_norm_g, m_w_in, m_attn_q_norm, m_attn_k_norm, m_ret_decay_fwd, m_ret_decay_bwd, m_ret_gn_w,
          m_w_branch_attn, m_w_branch_ret, m_w_out, m_final_norm_g]
    vs = [v_norm_g, v_w_in, v_attn_q_norm, v_attn_k_norm, v_ret_decay_fwd, v_ret_decay_bwd, v_ret_gn_w,
          v_w_branch_attn, v_w_branch_ret, v_w_out, v_final_norm_g]
    upd = [None if w is w_in else sharded[id(w)][1:] if id(w) in sharded else _adamw_nd(w, g, m, v)
           for w, g, m, v in zip(weights, grad_w, ms, vs)]

    done = [dx, w_in_l1[0], g_wout] + [u[0] for w, u in zip(weights, upd) if u is not None and id(w) not in sharded]
    g_full, recv[0] = _scatter_wait(*pending, done)
    mine = (4 * lax.axis_index("x") + 2 * lax.axis_index("y") + lax.axis_index("c")).astype(jnp.int32)[None]
    w_in_upd = [tr(o) for o in _sum_adamw([recv[0]], *w_in_t, 0, 256, layer0=0, prev=w_in_l1, own=(g_full, mine))]
    grad_w[1], upd[1] = w_in_upd[0], w_in_upd[1:]
    return (loss, dx[None], *grad_w, *[u[0] for u in upd], *[u[1] for u in upd], *[u[2] for u in upd])
```

```python
import functools

import jax
import jax.numpy as jnp
from jax import lax
from jax.experimental import pallas as pl
from jax.experimental.pallas import tpu as pltpu

F32 = jnp.float32
BF16 = jnp.bfloat16
SDS = jax.ShapeDtypeStruct

D_MODEL = 1024
DEPTH = 2
GRID_W = 64
ATTN_Q_HEADS = 8
ATTN_KV_HEADS = 2
ATTN_HEAD_DIM = 64
ATTN_WIDTH = 512
ATTN_KV_WIDTH = 128
RET_HEADS = 4
RET_HEAD_DIM = 128
RET_WIDTH = 512
RET_CHUNK = 128
ATTN_KEY_CHUNK = 512
ATTN_BWD_KEY_CHUNK = 1024
ATTN_BWD_QUERY_TILE = 512
ATTN_FWD_QUERY_TILE = 512
QK_DOTS_PER_CHUNK = 4
EXP_LAG = 3
ROPE_THETA = 10000.0
EPS = 1e-6
D_IN = 5376
N_DEV = 8

ADAM_LR = 0.001
ADAM_B1 = 0.9
ADAM_B2 = 0.999
ADAM_EPS = 1e-08
ADAM_WD = 0.01
ADAM_STEP = 10

SEG = {
    "qa": (0, 512, 0),
    "ga": (768, 512, 512),
    "qr": (1280, 512, 1024),
    "kr": (1792, 512, 1536),
    "vr": (2304, 512, 2048),
    "gr": (2816, 512, 2560),
    "gm": (3328, 2048, 3072),
    "ka": (512, 128, 5120),
    "va": (640, 128, 5248),
}

VMEM_LIMIT = 60 * 1024 * 1024
NT = (((1,), (1,)), ((), ()))
TN = (((0,), (0,)), ((), ()))
MESH_ID = pl.DeviceIdType.MESH
ANY = pl.BlockSpec(memory_space=pl.ANY)


def _params(sem=None, vmem=VMEM_LIMIT):
    return pltpu.CompilerParams(dimension_semantics=sem, vmem_limit_bytes=vmem)


def _dot(a, b, dims=None):
    if dims is None:
        return jnp.dot(a, b, preferred_element_type=F32)
    return lax.dot_general(a, b, dims, preferred_element_type=F32)


def _sigmoid(x):
    return 1.0 / (1.0 + jnp.exp(-x))


def _swap_halves(x, q):
    n = x.shape[-1]
    axis = x.ndim - 1
    lane = lax.broadcasted_iota(jnp.int32, x.shape, axis)
    first = (lane % (2 * q)) < q
    return jnp.where(first, pltpu.roll(x, n - q, axis), pltpu.roll(x, q, axis))


def _rope(x, cos, sin_signed, q):
    return x * cos + _swap_halves(x, q) * sin_signed


def _rope_bwd(dy, cos, sin_signed, q):
    return dy * cos - _swap_halves(dy, q) * sin_signed


def _group_mean(v, ones_bd):
    hi = v.astype(BF16)
    lo = (v - hi.astype(F32)).astype(BF16)
    return _dot(hi, ones_bd) + _dot(lo, ones_bd)


def _rope_tables(t, head_dim):
    n_rows = t // GRID_W
    d_axis = head_dim // 2
    inv_freq = ROPE_THETA ** (-jnp.arange(0, d_axis, 2, dtype=F32) / d_axis)
    ar = jnp.arange(n_rows, dtype=F32)[:, None] * inv_freq
    ac = jnp.arange(GRID_W, dtype=F32)[:, None] * inv_freq
    by_row = lambda a: jnp.repeat(a, GRID_W, axis=0)
    by_col = lambda a: jnp.tile(a, (n_rows, 1))
    cr, sr, cc, sc = by_row(jnp.cos(ar)), by_row(jnp.sin(ar)), by_col(jnp.cos(ac)), by_col(jnp.sin(ac))
    return jnp.concatenate([cr, cr, cc, cc], axis=-1), jnp.concatenate([-sr, sr, -sc, sc], axis=-1)


def _me():
    return lax.axis_index("x"), lax.axis_index("y"), lax.axis_index("c")


def _flip(k):
    x, y, c = _me()
    px = 1 - x if k & 4 else x
    py = 1 - y if k & 2 else y
    pc = 1 - c if k & 1 else c
    return (px, py, pc), 4 * px + 2 * py + pc


class _Exchange:
    def __init__(self, kind, srcs):
        self.kind, self.srcs, self.n = kind, list(srcs), len(srcs)
        self.rows = [a.shape[0] if kind == "gather" else a.shape[0] // N_DEV for a in srcs]
        if kind == "gather":
            self.out_shape = [SDS((N_DEV * a.shape[0], a.shape[1]), a.dtype) for a in srcs]
        else:
            self.out_shape = [SDS((N_DEV, a.shape[0] // N_DEV, a.shape[1]), a.dtype) for a in srcs]
        self.scratch = [pltpu.SemaphoreType.DMA((self.n, N_DEV - 1)), pltpu.SemaphoreType.DMA((self.n, N_DEV - 1)),
                        pltpu.SemaphoreType.DMA((self.n,))]

    def _block(self, ref, a, idx):
        r = self.rows[a]
        return ref.at[pl.ds(pl.multiple_of(idx * r, 16), r), :]

    def _src(self, ins, a, idx):
        return ins[a] if self.kind == "gather" else self._block(ins[a], a, idx)

    def _dst(self, outs, a, idx):
        return self._block(outs[a], a, idx) if self.kind == "gather" else outs[a].at[idx]

    def _copies(self, ins, outs, sems):
        send_sems, recv_sems, local_sems = sems
        me, mine = _flip(0)
        local, sends, recvs = [], [], []
        for a in range(self.n):
            local.append(pltpu.make_async_copy(self._src(ins, a, mine), self._dst(outs, a, mine), local_sems.at[a]))
            for k in range(1, N_DEV):
                peer, theirs = _flip(k)
                sem = dict(send_sem=send_sems.at[a, k - 1], recv_sem=recv_sems.at[a, k - 1])
                sends.append(pltpu.make_async_remote_copy(
                    src_ref=self._src(ins, a, theirs), dst_ref=self._dst(outs, a, mine),
                    device_id=peer, device_id_type=MESH_ID, **sem))
                recvs.append(pltpu.make_async_remote_copy(
                    src_ref=self._dst(outs, a, theirs), dst_ref=self._dst(outs, a, theirs),
                    device_id=me, device_id_type=MESH_ID, **sem))
        return local, sends, recvs

    def start(self, ins, outs, sems):
        local, sends, _ = self._copies(ins, outs, sems)
        for cp in local + sends:
            cp.start()

    def wait(self, ins, outs, sems):
        local, sends, recvs = self._copies(ins, outs, sems)
        for cp in sends:
            cp.wait_send()
        for cp in recvs:
            cp.wait_recv()
        for cp in local:
            cp.wait()


def _with_exchange(body, n_in, n_out, n_scratch, ex, first, last):
    if ex is None:
        return body

    def wrapped(*refs):
        ins = refs[:n_in]
        ex_ins = refs[n_in:n_in + ex.n]
        outs = refs[n_in + ex.n:n_in + ex.n + n_out]
        ex_outs = refs[n_in + ex.n + n_out:n_in + 2 * ex.n + n_out]
        rest = refs[n_in + 2 * ex.n + n_out:]
        scratch, sems = rest[:n_scratch], rest[n_scratch:]

        @pl.when(first())
        def _():
            ex.start(ex_ins, ex_outs, sems)

        body(*ins, *outs, *scratch)

        @pl.when(last())
        def _():
            ex.wait(ex_ins, ex_outs, sems)

    return wrapped


def _ex_args(ex):
    if ex is None:
        return [], [], [], [], []
    return [ANY] * ex.n, [ANY] * ex.n, list(ex.out_shape), list(ex.scratch), list(ex.srcs)


def _in_proj(x, g, w_t, qn, kn, cos, sin, ones_bd):
    t, d = x.shape
    tm = min(256, t)
    tk = min(ATTN_KEY_CHUNK, t)
    per_chunk = tk // tm
    hd = ATTN_HEAD_DIM

    def body(x_ref, g_ref, w_ref, qn_ref, kn_ref, c_ref, s_ref, b_ref,
             z_ref, ht_ref, q_out, qt_out, k_out, v_out, vt_out):
        xv = x_ref[...]
        r = lax.rsqrt(jnp.mean(xv * xv, axis=-1, keepdims=True) + EPS)
        h = xv * r * g_ref[...]
        ht_ref[...] = h.T.astype(BF16)
        hb = h.astype(BF16)
        seg = {}
        for name, (nat, w, off) in SEG.items():
            seg[name] = _dot(hb, w_ref[nat:nat + w, :], NT)
            z_ref[:, off:off + w] = seg[name]

        bd = b_ref[...]
        c2, s2 = c_ref[...], s_ref[...]
        cq = jnp.concatenate([c2] * 4, axis=-1)
        sq = jnp.concatenate([s2] * 4, axis=-1)
        xq, xk, xvv = seg["qa"], seg["ka"], seg["va"]
        yq = xq * lax.rsqrt(_group_mean(xq * xq, bd) + EPS) * qn_ref[...]
        yq = _rope(yq, cq, sq, hd // 4) * (hd ** -0.5)
        yqt = yq.T
        for hh in range(ATTN_Q_HEADS):
            q_out[hh] = yq[:, hh * hd:(hh + 1) * hd].astype(BF16)
            qt_out[hh] = yqt[hh * hd:(hh + 1) * hd, :].astype(BF16)
        yk = xk * lax.rsqrt(_group_mean(xk * xk, bd[:ATTN_KV_WIDTH, :ATTN_KV_WIDTH]) + EPS) * kn_ref[...]
        yk = _rope(yk, c2, s2, hd // 4)
        xvt = xvv.T
        ones = jnp.ones((hd, tm), F32)
        for hh in range(ATTN_KV_HEADS):
            k_out[hh] = yk[:, hh * hd:(hh + 1) * hd].astype(BF16)
            v_out[hh] = xvv[:, hh * hd:(hh + 1) * hd].astype(BF16)
            vt_out[hh, 0] = jnp.concatenate([xvt[hh * hd:(hh + 1) * hd, :], ones], axis=0).astype(BF16)

    const = lambda shape: pl.BlockSpec(shape, lambda i: (0,) * len(shape))
    rows = lambda w: pl.BlockSpec((tm, w), lambda i: (i, 0))
    return pl.pallas_call(
        body, name="in_proj", grid=(t // tm,),
        in_specs=[rows(d), const((1, d)), const((D_IN, d)), const((1, 512)), const((1, 128)), rows(128), rows(128),
                  const((512, 512))],
        out_specs=[rows(D_IN), pl.BlockSpec((d, tm), lambda i: (0, i)),
                   pl.BlockSpec((ATTN_Q_HEADS, tm, hd), lambda i: (0, i, 0)),
                   pl.BlockSpec((ATTN_Q_HEADS, hd, tm), lambda i: (0, 0, i)),
                   pl.BlockSpec((ATTN_KV_HEADS, tm, hd), lambda i: (0, i, 0)),
                   pl.BlockSpec((ATTN_KV_HEADS, tm, hd), lambda i: (0, i, 0)),
                   pl.BlockSpec((ATTN_KV_HEADS, 1, 2 * hd, tm), lambda i: (0, i // per_chunk, 0, i % per_chunk))],
        out_shape=[SDS((t, D_IN), F32), SDS((d, t), BF16),
                   SDS((ATTN_Q_HEADS, t, hd), BF16), SDS((ATTN_Q_HEADS, hd, t), BF16),
                   SDS((ATTN_KV_HEADS, t, hd), BF16), SDS((ATTN_KV_HEADS, t, hd), BF16),
                   SDS((ATTN_KV_HEADS, t // tk, 2 * hd, tk), BF16)],
        compiler_params=_params(("parallel",)),
    )(x, g, w_t, qn, kn, cos, sin, ones_bd)


def _attn_fwd(q, k, vt, ex=None):
    t = q.shape[1]
    tq = min(ATTN_FWD_QUERY_TILE, t)
    nk, tk = vt.shape[1], vt.shape[3]
    hd = ATTN_HEAD_DIM
    g = ATTN_Q_HEADS // ATTN_KV_HEADS

    def body(q_ref, k_ref, vt_ref, o_ref, lse_ref, s_scr):
        def pass_a(h, c, m8):
            part = tk // QK_DOTS_PER_CHUNK
            for lo in range(c * tk, (c + 1) * tk, part):
                st = _dot(k_ref[0, lo:lo + part, :], q_ref[h], NT)
                s_scr[h % 2, lo:lo + part, :] = st
                m8 = jnp.maximum(m8, jnp.max(st.reshape(part // 8, 8, tq), axis=0))
            return m8

        def pass_b(h, c, m, acc, after):
            e = jnp.exp(s_scr[h % 2, c * tk:(c + 1) * tk, :] - (m + after * 0.0)).astype(BF16)
            return acc + _dot(vt_ref[0, c], e)

        neg = jnp.full((8, tq), -jnp.inf, F32)
        m8 = neg
        for c in range(nk):
            m8 = pass_a(0, c, m8)
        outs = []
        for h in range(g):
            m = jnp.max(m8, axis=0, keepdims=True)
            acc = jnp.zeros((2 * hd, tq), F32)
            m8 = neg
            done = [m] * EXP_LAG
            for c in range(nk):
                if h + 1 < g:
                    m8 = pass_a(h + 1, c, m8)
                acc = pass_b(h, c, m, acc, done[-EXP_LAG])
                done.append(m8[0:1, :] if h + 1 < g else acc[hd:hd + 1, :])
            l = acc[hd:hd + 1, :]
            outs.append((acc[:hd, :] / l).T)
            lse_ref[h] = m + jnp.log(l)
        o_ref[...] = jnp.concatenate(outs, axis=-1)

    nq = t // tq
    first = lambda: jnp.logical_and(pl.program_id(0) == 0, pl.program_id(1) == 0)
    last = lambda: jnp.logical_and(pl.program_id(0) == ATTN_KV_HEADS - 1, pl.program_id(1) == nq - 1)
    xi, xo, xs, xscr, xargs = _ex_args(ex)
    return pl.pallas_call(
        _with_exchange(body, 3, 2, 1, ex, first, last), name="attn_fwd", grid=(ATTN_KV_HEADS, nq),
        in_specs=[pl.BlockSpec((g, tq, hd), lambda p, i: (p, i, 0)),
                  pl.BlockSpec((1, t, hd), lambda p, i: (p, 0, 0)),
                  pl.BlockSpec((1, nk, 2 * hd, tk), lambda p, i: (p, 0, 0, 0))] + xi,
        out_specs=[pl.BlockSpec((tq, g * hd), lambda p, i: (i, p)),
                   pl.BlockSpec((g, 1, tq), lambda p, i: (p, 0, i))] + xo,
        out_shape=[SDS((t, ATTN_WIDTH), F32), SDS((ATTN_Q_HEADS, 1, t), F32)] + xs,
        scratch_shapes=[pltpu.VMEM((2, t, tq), F32)] + xscr,
        compiler_params=_params(("arbitrary", "arbitrary")),
    )(q, k, vt, *xargs)


class _Dir:
    def __init__(self, lg, strict_future):
        c = RET_CHUNK
        ia = lax.broadcasted_iota(jnp.int32, (c, c), 0).astype(F32)
        ib = lax.broadcasted_iota(jnp.int32, (c, c), 1).astype(F32)
        col = lax.broadcasted_iota(jnp.int32, (c, 1), 0).astype(F32)
        row = lax.broadcasted_iota(jnp.int32, (1, c), 1).astype(F32)
        if strict_future:
            dist = ib - ia
            mask = dist > 0
            self.wq, self.wk, wk_row = c - col, col, row
        else:
            dist = ia - ib
            mask = dist >= 0
            self.wq, self.wk, wk_row = col + 1.0, c - 1.0 - col, c - 1.0 - row
        self.dist = jnp.maximum(dist, 0.0)
        self.d = jnp.where(mask, jnp.exp(self.dist * lg), 0.0)
        self.qd = jnp.exp(self.wq * lg)
        self.kd_col = jnp.exp(self.wk * lg)
        self.kd_row = jnp.exp(wk_row * lg)
        self.cd = jnp.exp(jnp.full((1, 1), float(c), F32) * lg)


def _ret_fwd(z, lgf, lgb, gnw, cos, sin):
    t = z.shape[0]
    c = RET_CHUNK
    nc = t // c
    hd = RET_HEAD_DIM
    unroll = 4 if nc % 4 == 0 else 1

    def body(lgf_ref, lgb_ref, q_ref, k_ref, v_ref, c_ref, s_ref, w_ref,
             qo_ref, ko_ref, vo_ref, orr_ref, on_ref, kt, uf, ub, sfa, sba):
        h = pl.program_id(0)
        fw = _Dir(lgf_ref[h], False)
        bw = _Dir(lgb_ref[h], True)
        cc, ss = c_ref[...], s_ref[...]
        qo_ref[...] = _rope(q_ref[...], cc, ss, hd // 4).astype(BF16)
        kr = _rope(k_ref[...], cc, ss, hd // 4) * (hd ** -0.5)
        ko_ref[...] = kr.astype(BF16)
        vo_ref[...] = v_ref[...].astype(BF16)
        for i in range(nc):
            kt[i] = kr[i * c:(i + 1) * c, :].T.astype(BF16)

        def rows(ci):
            return pl.ds(pl.multiple_of(ci * c, c), c)

        def kv_products(ci, carry):
            vv = vo_ref[rows(ci), :]
            ktf = kt[ci].astype(F32)
            uf[ci] = _dot((ktf * fw.kd_row).astype(BF16), vv)
            ub[ci] = _dot((ktf * bw.kd_row).astype(BF16), vv)
            return carry

        lax.fori_loop(0, nc, kv_products, 0, unroll=unroll)

        def scan(i, carry):
            sf, sb = carry
            j = nc - 1 - i
            sfa[i] = sf.astype(BF16)
            sba[j] = sb.astype(BF16)
            return sf * fw.cd + uf[i], sb * bw.cd + ub[j]

        zero = jnp.zeros((hd, hd), F32)
        lax.fori_loop(0, nc, scan, (zero, zero))
        gw = w_ref[...]

        def outputs(ci, carry):
            sl = rows(ci)
            qq, kk, vv = qo_ref[sl, :], ko_ref[sl, :], vo_ref[sl, :]
            a = _dot(qq, kk, NT)
            o = (_dot((a * fw.d).astype(BF16), vv) + _dot(qq, sfa[ci]) * fw.qd
                 + _dot((a * bw.d).astype(BF16), vv) + _dot(qq, sba[ci]) * bw.qd)
            orr_ref[sl, :] = o
            xc = o - jnp.mean(o, axis=-1, keepdims=True)
            var = jnp.mean(xc * xc, axis=-1, keepdims=True)
            on_ref[sl, :] = xc * lax.rsqrt(var + EPS) * gw
            return carry

        lax.fori_loop(0, nc, outputs, 0, unroll=unroll)

    smem = pl.BlockSpec(memory_space=pltpu.SMEM)
    col = lambda name: (lambda h: (0, SEG[name][2] // 128 + h))
    head = pl.BlockSpec((t, 128), lambda h: (0, h))
    full = pl.BlockSpec((t, 128), lambda h: (0, 0))
    return pl.pallas_call(
        body, name="ret_fwd", grid=(RET_HEADS,),
        in_specs=[smem, smem, pl.BlockSpec((t, 128), col("qr")), pl.BlockSpec((t, 128), col("kr")),
                  pl.BlockSpec((t, 128), col("vr")), full, full, pl.BlockSpec((1, 128), lambda h: (0, h))],
        out_specs=[head, head, head, head, head],
        out_shape=[SDS((t, RET_WIDTH), BF16)] * 3 + [SDS((t, RET_WIDTH), F32)] * 2,
        scratch_shapes=[pltpu.VMEM((nc, hd, c), BF16), pltpu.VMEM((nc, hd, hd), F32), pltpu.VMEM((nc, hd, hd), F32),
                        pltpu.VMEM((nc, hd, hd), BF16), pltpu.VMEM((nc, hd, hd), BF16)],
        compiler_params=_params(("parallel",)),
    )(lgf, lgb, z, z, z, cos, sin, gnw)


def _merge_fwd(x, z, oa, on, wb_t, wout):
    t, d = x.shape
    tm = min(256, t)

    def body(x_ref, ga_ref, gr_ref, gm0_ref, gm1_ref, oa_ref, on_ref, wb_ref, wo_ref, xn_ref, ya_ref, yb_ref):
        ga, gr = ga_ref[...], gr_ref[...]
        ua = ga * _sigmoid(ga) * oa_ref[...]
        ub = gr * _sigmoid(gr) * on_ref[...]
        ya = _dot(ua.astype(BF16), wb_ref[:, :512], NT)
        yb = _dot(ub.astype(BF16), wb_ref[:, 512:], NT)
        ya_ref[...] = ya
        yb_ref[...] = yb
        merged = _sigmoid(gm0_ref[...]) * ya + _sigmoid(gm1_ref[...]) * yb
        xn_ref[...] = x_ref[...] + _dot(merged.astype(BF16), wo_ref[...])

    row = lambda w, j: pl.BlockSpec((tm, w), lambda i: (i, j))
    const = lambda shape: pl.BlockSpec(shape, lambda i: (0, 0))
    return pl.pallas_call(
        body, name="merge_fwd", grid=(t // tm,),
        in_specs=[row(d, 0), row(512, SEG["ga"][2] // 512), row(512, SEG["gr"][2] // 512),
                  row(1024, SEG["gm"][2] // 1024), row(1024, SEG["gm"][2] // 1024 + 1),
                  row(512, 0), row(512, 0), const((d, 1024)), const((d, d))],
        out_specs=[row(d, 0), row(d, 0), row(d, 0)],
        out_shape=[SDS((t, d), F32)] * 3,
        compiler_params=_params(("parallel",)),
    )(x, z, z, z, z, oa, on, wb_t, wout)


def _final_loss(x, g, target):
    t, d = x.shape
    tm = min(512, t)
    n = t // tm

    def body(x_ref, g_ref, t_ref, dx_ref, dg_ref, loss_ref, acc_g, acc_l):
        i = pl.program_id(0)

        @pl.when(i == 0)
        def _():
            acc_g[...] = jnp.zeros_like(acc_g)
            acc_l[...] = jnp.zeros_like(acc_l)

        xv, gv = x_ref[...], g_ref[...]
        r = lax.rsqrt(jnp.mean(xv * xv, axis=-1, keepdims=True) + EPS)
        xh = xv * r
        err = xh * gv - t_ref[...]
        dy = err * (1.0 / d)
        gy = dy * gv
        dx_ref[...] = r * (gy - xh * jnp.mean(gy * xh, axis=-1, keepdims=True))
        acc_g[...] += jnp.sum((dy * xh).reshape(tm // 8, 8, d), axis=0)
        acc_l[...] += jnp.sum((err * err).reshape(tm // 8, 8, d), axis=0)

        @pl.when(i == n - 1)
        def _():
            dg_ref[...] = jnp.sum(acc_g[...], axis=0, keepdims=True)
            tot = jnp.sum(jnp.sum(acc_l[...], axis=0, keepdims=True), axis=1, keepdims=True)
            loss_ref[...] = jnp.broadcast_to(tot * (0.5 / d), (1, 128))

    return pl.pallas_call(
        body, name="final_loss", grid=(n,),
        in_specs=[pl.BlockSpec((tm, d), lambda i: (i, 0)), pl.BlockSpec((1, d), lambda i: (0, 0)),
                  pl.BlockSpec((tm, d), lambda i: (i, 0))],
        out_specs=[pl.BlockSpec((tm, d), lambda i: (i, 0)), pl.BlockSpec((1, d), lambda i: (0, 0)),
                   pl.BlockSpec((1, 128), lambda i: (0, 0))],
        out_shape=[SDS((t, d), F32), SDS((1, d), F32), SDS((1, 128), F32)],
        scratch_shapes=[pltpu.VMEM((8, d), F32), pltpu.VMEM((8, d), F32)],
        compiler_params=_params(("arbitrary",)),
    )(x, g, target)


def _merge_bwd(dxo, z, oa, on, ya, yb, wb_t, wout):
    t, d = dxo.shape
    tm = min(256, t)
    n = t // tm

    def body(dx_ref, ga_ref, gr_ref, gm0_ref, gm1_ref, oa_ref, on_ref, ya_ref, yb_ref, wb_ref, wo_ref,
             doa_ref, don_ref, dz_ref, dwo_ref, dwb_ref, acc_o, acc_b):
        i = pl.program_id(0)

        @pl.when(i == 0)
        def _():
            acc_o[...] = jnp.zeros_like(acc_o)
            acc_b[...] = jnp.zeros_like(acc_b)

        dxb = dx_ref[...].astype(BF16)
        ya, yb = ya_ref[...], yb_ref[...]
        g0, g1 = _sigmoid(gm0_ref[...]), _sigmoid(gm1_ref[...])
        mb = (g0 * ya + g1 * yb).astype(BF16)
        dm = _dot(dxb, wo_ref[...], NT)
        dya = (dm * g0).astype(BF16)
        dyb = (dm * g1).astype(BF16)
        dz_ref[:, 1024:2048] = (dm * ya * g0 * (1.0 - g0)).astype(BF16)
        dz_ref[:, 2048:3072] = (dm * yb * g1 * (1.0 - g1)).astype(BF16)

        def branch(g_ref, o_ref, dy, w, do_ref, lo):
            gv, ov = g_ref[...], o_ref[...]
            sg = _sigmoid(gv)
            silu = gv * sg
            du = _dot(dy, w)
            do_ref[...] = du * silu
            dz_ref[:, lo:lo + 512] = (du * ov * (sg * (1.0 + gv * (1.0 - sg)))).astype(BF16)
            acc_b[:, lo:lo + 512] += _dot(dy, (silu * ov).astype(BF16), TN)

        branch(ga_ref, oa_ref, dya, wb_ref[:, :512], doa_ref, 0)
        branch(gr_ref, on_ref, dyb, wb_ref[:, 512:], don_ref, 512)
        acc_o[...] += _dot(mb, dxb, TN)

        @pl.when(i == n - 1)
        def _():
            dwo_ref[...] = acc_o[...].astype(BF16)
            dwb_ref[...] = acc_b[...].astype(BF16)

    row = lambda w, j: pl.BlockSpec((tm, w), lambda i: (i, j))
    const = lambda shape: pl.BlockSpec(shape, lambda i: (0, 0))
    return pl.pallas_call(
        body, name="merge_bwd", grid=(n,),
        in_specs=[row(d, 0), row(512, SEG["ga"][2] // 512), row(512, SEG["gr"][2] // 512),
                  row(1024, SEG["gm"][2] // 1024), row(1024, SEG["gm"][2] // 1024 + 1),
                  row(512, 0), row(512, 0), row(d, 0), row(d, 0), const((d, 1024)), const((d, d))],
        out_specs=[row(512, 0), row(512, 0), row(3072, 0), const((d, d)), const((d, 1024))],
        out_shape=[SDS((t, 512), F32), SDS((t, 512), F32), SDS((t, 3072), BF16), SDS((d, d), BF16),
                   SDS((d, 1024), BF16)],
        scratch_shapes=[pltpu.VMEM((d, d), F32), pltpu.VMEM((d, 1024), F32)],
        compiler_params=_params(("arbitrary",)),
    )(dxo, z, z, z, z, oa, on, ya, yb, wb_t, wout)


def _ret_bwd(qrot, krot, vb, orr, don, gnw, lgf, lgb):
    t = qrot.shape[0]
    c = RET_CHUNK
    nc = t // c
    hd = RET_HEAD_DIM
    unroll = 2 if nc % 2 == 0 else 1

    def body(lgf_ref, lgb_ref, q_ref, k_ref, v_ref, o_ref, dn_ref, w_ref,
             dq_ref, dk_ref, dv_ref, dw_ref, dlf_ref, dlb_ref, qt, kt, dob, uf, ub, wf, wb, sfa, sba, gfa, gba):
        h = pl.program_id(0)
        fw = _Dir(lgf_ref[h], False)
        bw = _Dir(lgb_ref[h], True)
        fw.dt, bw.dt = fw.d.T, bw.d.T

        o = o_ref[...]
        xc = o - jnp.mean(o, axis=-1, keepdims=True)
        r = lax.rsqrt(jnp.mean(xc * xc, axis=-1, keepdims=True) + EPS)
        xh = xc * r
        dn = dn_ref[...]
        gy = dn * w_ref[...]
        d_o = r * (gy - jnp.mean(gy, axis=-1, keepdims=True) - xh * jnp.mean(gy * xh, axis=-1, keepdims=True))
        dw_ref[...] = jnp.sum(dn * xh, axis=0, keepdims=True)
        dob[...] = d_o.astype(BF16)
        for i in range(nc):
            qt[i] = q_ref[i * c:(i + 1) * c, :].astype(F32).T.astype(BF16)
            kt[i] = k_ref[i * c:(i + 1) * c, :].astype(F32).T.astype(BF16)

        def rows(ci):
            return pl.ds(pl.multiple_of(ci * c, c), c)

        def products(ci, carry):
            sl = rows(ci)
            vv, do32 = v_ref[sl, :], dob[sl, :].astype(F32)
            ktf = kt[ci].astype(F32)
            uf[ci] = _dot((ktf * fw.kd_row).astype(BF16), vv)
            ub[ci] = _dot((ktf * bw.kd_row).astype(BF16), vv)
            wf[ci] = _dot(qt[ci], (do32 * fw.qd).astype(BF16))
            wb[ci] = _dot(qt[ci], (do32 * bw.qd).astype(BF16))
            return carry

        lax.fori_loop(0, nc, products, 0, unroll=unroll)

        def scan(i, carry):
            sf, sb, gf, gb = carry
            j = nc - 1 - i
            sfa[i] = sf.astype(BF16)
            sba[j] = sb.astype(BF16)
            gfa[j] = gf.astype(BF16)
            gba[i] = gb.astype(BF16)
            return sf * fw.cd + uf[i], sb * bw.cd + ub[j], gf * fw.cd + wf[j], gb * bw.cd + wb[i]

        zero = jnp.zeros((hd, hd), F32)
        lax.fori_loop(0, nc, scan, (zero, zero, zero, zero))

        def one_dir(p, s_all, g_all, ci, qq, kk, vv, do, a, bm):
            sb, gb = s_all[ci], g_all[ci]
            doq = (do.astype(F32) * p.qd).astype(BF16)
            dqc = _dot(doq, sb, NT)
            kkd = (kk.astype(F32) * p.kd_col).astype(BF16)
            dk2 = _dot(vv, gb, NT) * p.kd_col
            terms = (p.dist * p.d * a * bm + p.wq * qq.astype(F32) * dqc + p.wk * kk.astype(F32) * dk2
                     + (float(c) * p.cd) * gb.astype(F32) * sb.astype(F32))
            return dqc, dk2, _dot(kkd, gb), terms

        d_both, dt_both = fw.d + bw.d, fw.dt + bw.dt

        def chunk(ci, carry):
            af, ab = carry
            sl = rows(ci)
            qq, kk, vv, do = q_ref[sl, :], k_ref[sl, :], v_ref[sl, :], dob[sl, :]
            a, bm = _dot(qq, kk, NT), _dot(do, vv, NT)
            at, bt = _dot(kk, qq, NT), _dot(vv, do, NT)
            dqf, dkf, dvf, tf = one_dir(fw, sfa, gfa, ci, qq, kk, vv, do, a, bm)
            dqb, dkb, dvb, tb = one_dir(bw, sba, gba, ci, qq, kk, vv, do, a, bm)
            dq_ref[sl, :] = _dot((bm * d_both).astype(BF16), kk) + dqf + dqb
            dk_ref[sl, :] = _dot((bt * dt_both).astype(BF16), qq) + dkf + dkb
            dv_ref[sl, :] = _dot((at * dt_both).astype(BF16), do) + dvf + dvb
            return af + tf, ab + tb

        af, ab = lax.fori_loop(0, nc, chunk, (zero, zero), unroll=unroll)
        tot = lambda m: jnp.sum(jnp.sum(m, axis=0, keepdims=True), axis=1, keepdims=True)
        dlf_ref[...] = jnp.broadcast_to(tot(af).reshape(1, 1, 1), (1, 8, 128))
        dlb_ref[...] = jnp.broadcast_to(tot(ab).reshape(1, 1, 1), (1, 8, 128))

    smem = pl.BlockSpec(memory_space=pltpu.SMEM)
    head = pl.BlockSpec((t, 128), lambda h: (0, h))
    vec = pl.BlockSpec((1, 128), lambda h: (0, h))
    scal = pl.BlockSpec((1, 8, 128), lambda h: (h, 0, 0))
    mats = lambda dt: pltpu.VMEM((nc, hd, hd), dt)
    return pl.pallas_call(
        body, name="ret_bwd", grid=(RET_HEADS,),
        in_specs=[smem, smem, head, head, head, head, head, vec],
        out_specs=[head, head, head, vec, scal, scal],
        out_shape=[SDS((t, RET_WIDTH), F32)] * 3 + [SDS((1, RET_WIDTH), F32), SDS((RET_HEADS, 8, 128), F32),
                                                   SDS((RET_HEADS, 8, 128), F32)],
        scratch_shapes=[pltpu.VMEM((nc, hd, c), BF16), pltpu.VMEM((nc, hd, c), BF16), pltpu.VMEM((t, hd), BF16),
                        mats(F32), mats(F32), mats(F32), mats(F32), mats(BF16), mats(BF16), mats(BF16), mats(BF16)],
        compiler_params=_params(("parallel",)),
    )(lgf, lgb, qrot, krot, vb, orr, don, gnw)


def _ret_post_bwd(dq, dk, dv, cos, sin):
    t = dq.shape[0]
    tm = min(512, t)
    hd = RET_HEAD_DIM

    def body(dq_ref, dk_ref, dv_ref, c_ref, s_ref, oq_ref, ok_ref, ov_ref):
        cc = jnp.concatenate([c_ref[...]] * 4, axis=-1)
        ss = jnp.concatenate([s_ref[...]] * 4, axis=-1)
        oq_ref[...] = _rope_bwd(dq_ref[...], cc, ss, hd // 4).astype(BF16)
        ok_ref[...] = (_rope_bwd(dk_ref[...], cc, ss, hd // 4) * (hd ** -0.5)).astype(BF16)
        ov_ref[...] = dv_ref[...].astype(BF16)

    blk = pl.BlockSpec((tm, 512), lambda i: (i, 0))
    tab = pl.BlockSpec((tm, 128), lambda i: (i, 0))
    return pl.pallas_call(
        body, name="ret_post_bwd", grid=(t // tm,),
        in_specs=[blk, blk, blk, tab, tab], out_specs=[blk, blk, blk],
        out_shape=[SDS((t, 512), BF16)] * 3,
        compiler_params=_params(("parallel",)),
    )(dq, dk, dv, cos, sin)


def _attn_bwd(q, qt, k, v, doa, oa, lse, ex=None):
    t = q.shape[1]
    tq = min(ATTN_BWD_QUERY_TILE, t)
    nq = t // tq
    tk = min(ATTN_BWD_KEY_CHUNK, t)
    nk = t // tk
    hd = ATTN_HEAD_DIM
    scale = hd ** -0.5

    def body(q_ref, qt_ref, k_ref, v_ref, do_ref, o_ref, lse_ref, dq_ref, dkt_ref, dvt_ref):
        p, i = pl.program_id(0), pl.program_id(1)

        @pl.when(jnp.logical_and(p % 2 == 0, i == 0))
        def _():
            dkt_ref[...] = jnp.zeros_like(dkt_ref)
            dvt_ref[...] = jnp.zeros_like(dvt_ref)

        dov, ov = do_ref[...], o_ref[...]
        dovt = dov.T
        lanes = lambda col: jnp.concatenate([col] * (tk // 128), axis=1)
        outs = []
        for j in range(2):
            qq, qqt = q_ref[j], qt_ref[j]
            do32 = dov[:, j * hd:(j + 1) * hd]
            do, dot_ = do32.astype(BF16), dovt[j * hd:(j + 1) * hd, :].astype(BF16)
            dd = lanes(jnp.broadcast_to(jnp.sum(do32 * ov[:, j * hd:(j + 1) * hd], axis=1, keepdims=True), (tq, 128)))
            lse_j = lanes(jnp.broadcast_to(lse_ref[j], (128, tq)).T)
            dq = jnp.zeros((tq, hd), F32)
            for c in range(nk):
                sl = slice(c * tk, (c + 1) * tk)
                kc, vc = k_ref[0, sl, :], v_ref[0, sl, :]
                pr = jnp.exp(_dot(qq, kc, NT) - lse_j)
                ds = (pr * (_dot(do, vc, NT) - dd)).astype(BF16)
                dvt_ref[0, :, sl] += _dot(dot_, pr.astype(BF16))
                dkt_ref[0, :, sl] += _dot(qqt, ds)
                dq = dq + _dot(ds, kc)
            outs.append(dq * scale)
        dq_ref[...] = jnp.concatenate(outs, axis=-1)

    kv = pl.BlockSpec((1, t, hd), lambda p, i: (p // 2, 0, 0))
    kvt = pl.BlockSpec((1, hd, t), lambda p, i: (p // 2, 0, 0))
    pair = pl.BlockSpec((tq, 128), lambda p, i: (i, p))
    first = lambda: jnp.logical_and(pl.program_id(0) == 0, pl.program_id(1) == 0)
    last = lambda: jnp.logical_and(pl.program_id(0) == 3, pl.program_id(1) == nq - 1)
    xi, xo, xs, xscr, xargs = _ex_args(ex)
    return pl.pallas_call(
        _with_exchange(body, 7, 3, 0, ex, first, last), name="attn_bwd", grid=(4, nq),
        in_specs=[pl.BlockSpec((2, tq, hd), lambda p, i: (p, i, 0)), pl.BlockSpec((2, hd, tq), lambda p, i: (p, 0, i)),
                  kv, kv, pair, pair, pl.BlockSpec((2, 1, tq), lambda p, i: (p, 0, i))] + xi,
        out_specs=[pair, kvt, kvt] + xo,
        out_shape=[SDS((t, ATTN_WIDTH), F32), SDS((ATTN_KV_HEADS, hd, t), F32),
                   SDS((ATTN_KV_HEADS, hd, t), F32)] + xs,
        scratch_shapes=xscr,
        compiler_params=_params(("arbitrary", "arbitrary")),
    )(q, qt, k, v, doa, oa, lse, *xargs)


def _attn_post_bwd(dq, dk, dv, z, qn, kn, cos, sin, ones_bd):
    t = z.shape[0]
    tm = min(512, t)
    n = t // tm
    hd = ATTN_HEAD_DIM

    def body(dq_ref, dk_ref, dv_ref, zq_ref, zkv_ref, qn_ref, kn_ref, c_ref, s_ref, b_ref,
             dz_ref, dqn_ref, dkn_ref, acc_q, acc_k):
        i = pl.program_id(0)

        @pl.when(i == 0)
        def _():
            acc_q[...] = jnp.zeros_like(acc_q)
            acc_k[...] = jnp.zeros_like(acc_k)

        bd = b_ref[...]
        c2, s2 = c_ref[...], s_ref[...]

        def norm_bwd(dy, x, w, ones, cos_t, sin_t, acc):
            dyr = _rope_bwd(dy, cos_t, sin_t, hd // 4)
            r = lax.rsqrt(_group_mean(x * x, ones) + EPS)
            xh = x * r
            gy = dyr * w
            acc[...] += jnp.sum((dyr * xh).reshape(tm // 8, 8, x.shape[-1]), axis=0)
            return r * (gy - xh * _group_mean(gy * xh, ones))

        cq = jnp.concatenate([c2] * 4, axis=-1)
        sq = jnp.concatenate([s2] * 4, axis=-1)
        dz_ref[:, :512] = norm_bwd(dq_ref[...], zq_ref[...], qn_ref[...], bd, cq, sq, acc_q).astype(BF16)
        zkv = zkv_ref[...]
        dkk = jnp.concatenate([dk_ref[0], dk_ref[1]], axis=0).T
        dz_ref[:, 512:640] = norm_bwd(dkk, zkv[:, :128], kn_ref[...], bd[:128, :128], c2, s2, acc_k).astype(BF16)
        dz_ref[:, 640:768] = jnp.concatenate([dv_ref[0], dv_ref[1]], axis=0).T.astype(BF16)

        @pl.when(i == n - 1)
        def _():
            dqn_ref[...] = jnp.sum(acc_q[...], axis=0, keepdims=True)
            dkn_ref[...] = jnp.sum(acc_k[...], axis=0, keepdims=True)

    kv_blk = SEG["ka"][2] // 256
    kvs = pl.BlockSpec((ATTN_KV_HEADS, hd, tm), lambda i: (0, 0, i))
    const = lambda shape: pl.BlockSpec(shape, lambda i: (0, 0))
    return pl.pallas_call(
        body, name="attn_post_bwd", grid=(n,),
        in_specs=[pl.BlockSpec((tm, 512), lambda i: (i, 0)), kvs, kvs,
                  pl.BlockSpec((tm, 512), lambda i: (i, 0)), pl.BlockSpec((tm, 256), lambda i: (i, kv_blk)),
                  const((1, 512)), const((1, 128)),
                  pl.BlockSpec((tm, 128), lambda i: (i, 0)), pl.BlockSpec((tm, 128), lambda i: (i, 0)),
                  const((512, 512))],
        out_specs=[pl.BlockSpec((tm, 768), lambda i: (i, 0)), const((1, 512)), const((1, 128))],
        out_shape=[SDS((t, 768), BF16), SDS((1, 512), F32), SDS((1, 128), F32)],
        scratch_shapes=[pltpu.VMEM((8, 512), F32), pltpu.VMEM((8, 128), F32)],
        compiler_params=_params(("arbitrary",)),
    )(dq, dk, dv, z, z, qn, kn, cos, sin, ones_bd)


def _in_bwd(dxo, x, g, w_t, dz_a, dz_m, dqr, dkr, dvr, after=None):
    t, d = x.shape
    tm = min(256, t)
    n = t // tm
    parts = [(0, 0, 768, 0), (1, 0, 512, SEG["ga"][0]), (2, 0, 512, SEG["qr"][0]), (3, 0, 512, SEG["kr"][0]),
             (4, 0, 512, SEG["vr"][0]), (1, 512, 2560, SEG["gr"][0])]

    def body(dx_ref, x_ref, g_ref, w_ref, a_ref, m_ref, q_ref, k_ref, v_ref, o_ref, dg_ref, acc):
        i = pl.program_id(0)

        @pl.when(i == 0)
        def _():
            acc[...] = jnp.zeros_like(acc)

        pieces = [a_ref, m_ref, q_ref, k_ref, v_ref]
        dh = jnp.zeros((tm, d), F32)
        for pi, lo, w, row in parts:
            dh = dh + _dot(pieces[pi][:, lo:lo + w], w_ref[row:row + w, :])
        xv = x_ref[...]
        r = lax.rsqrt(jnp.mean(xv * xv, axis=-1, keepdims=True) + EPS)
        xh = xv * r
        gy = dh * g_ref[...]
        o_ref[...] = dx_ref[...] + r * (gy - xh * jnp.mean(gy * xh, axis=-1, keepdims=True))
        acc[...] += jnp.sum((dh * xh).reshape(tm // 8, 8, d), axis=0)

        @pl.when(i == n - 1)
        def _():
            dg_ref[...] = jnp.sum(acc[...], axis=0, keepdims=True)

    row = lambda w: pl.BlockSpec((tm, w), lambda i: (i, 0))
    const = lambda shape: pl.BlockSpec(shape, lambda i: (0, 0))
    extra = [] if after is None else [after]
    return pl.pallas_call(
        (lambda *refs: body(*refs[:9], *refs[9 + len(extra):])), name="in_bwd", grid=(n,),
        in_specs=[row(d), row(d), const((1, d)), const((D_IN, d)), row(768), row(3072), row(512), row(512),
                  row(512)] + [const(a.shape) for a in extra],
        out_specs=[row(d), const((1, d))],
        out_shape=[SDS((t, d), F32), SDS((1, d), F32)],
        scratch_shapes=[pltpu.VMEM((8, d), F32)],
        compiler_params=_params(("arbitrary",)),
    )(dxo, x, g, w_t, dz_a, dz_m, dqr, dkr, dvr, *extra)


def _dw_in(h_t, dz_a, dz_m, dqr, dkr, dvr):
    d, t = h_t.shape
    tn = 256
    parts = [(0, 0, 0, 3), (1, 0, SEG["ga"][0] // tn, 2), (2, 0, SEG["qr"][0] // tn, 2),
             (3, 0, SEG["kr"][0] // tn, 2), (4, 0, SEG["vr"][0] // tn, 2), (1, 2, SEG["gr"][0] // tn, 10)]
    pieces = [dz_a, dz_m, dqr, dkr, dvr]

    def col_block(pi):
        mine = [(c0, r0, n) for q, c0, r0, n in parts if q == pi]

        def index(j):
            c0, r0, n = mine[0]
            blk = c0 + jnp.clip(j - r0, 0, n - 1)
            for c0, r0, n in mine[1:]:
                blk = jnp.where(j >= r0, c0 + jnp.clip(j - r0, 0, n - 1), blk)
            return 0, blk

        return index

    def body(h_ref, *refs):
        o_ref = refs[-1]
        j = pl.program_id(0)
        for pi, _, r0, n in parts:
            @pl.when(jnp.logical_and(j >= r0, j < r0 + n))
            def _(p_ref=refs[pi]):
                o_ref[...] = _dot(h_ref[...], p_ref[...]).T.astype(BF16)

    return pl.pallas_call(
        body, name="dw_in", grid=(D_IN // tn,),
        in_specs=[pl.BlockSpec((d, t), lambda j: (0, 0))] + [pl.BlockSpec((t, tn), col_block(pi)) for pi in range(5)],
        out_specs=pl.BlockSpec((tn, d), lambda j: (j, 0)),
        out_shape=SDS((D_IN, d), BF16),
        compiler_params=_params(("arbitrary",)),
    )(h_t, *pieces)


def _adamw_math(w, g, m, v):
    mn = ADAM_B1 * m + (1.0 - ADAM_B1) * g
    vn = ADAM_B2 * v + (1.0 - ADAM_B2) * (g * g)
    m_hat = mn / (1.0 - ADAM_B1 ** ADAM_STEP)
    v_hat = vn / (1.0 - ADAM_B2 ** ADAM_STEP)
    return -ADAM_LR * (m_hat / (jnp.sqrt(v_hat) + ADAM_EPS) + ADAM_WD * w), mn, vn


def _sum_adamw(recvs, w, m, v, lane0, tn, layer0=0, prev=None, own=None):
    _, r, c = w.shape
    j0 = lane0 // tn
    n = len(recvs)
    has_own = own is not None

    def body(*refs):
        mine_ref, refs = (refs[0], refs[1:]) if has_own else (None, refs)
        w_ref, m_ref, v_ref = refs[n:n + 3]
        g_ref, d_ref, mo_ref, vo_ref = refs[-4:]

        def run(r_ref):
            def slot(s):
                if has_own:
                    return jnp.where(mine_ref[0] == s, refs[n + 3][...], r_ref[s]).astype(F32)
                return r_ref[s].astype(F32)

            g = slot(0)
            for s in range(1, N_DEV):
                g = g + slot(s)
            g_ref[0] = g
            d_ref[0], mo_ref[0], vo_ref[0] = _adamw_math(w_ref[0], g, m_ref[0], v_ref[0])

        for i in range(n):
            pl.when(pl.program_id(0) == i)(functools.partial(run, refs[i]))

    slots = pl.BlockSpec((N_DEV, r, tn), lambda i, j, *_: (0, 0, j0 + j))
    blk = pl.BlockSpec((1, r, tn), lambda i, j, *_: (layer0 + i, 0, j))
    before = [] if prev is None else list(prev)
    in_specs, args = [slots] * n + [blk] * 3, [*recvs, w, m, v]
    if has_own:
        assert n == 1
        in_specs.append(pl.BlockSpec((r, tn), lambda i, j, mine: (mine[0], j0 + j)))
        args.append(own[0])
    n_pre = len(args) + has_own
    return pl.pallas_call(
        body, name="sum_adamw",
        grid_spec=pltpu.PrefetchScalarGridSpec(
            num_scalar_prefetch=int(has_own), grid=(n, c // tn),
            in_specs=in_specs + [ANY] * len(before), out_specs=[blk] * 4),
        out_shape=[SDS(w.shape, F32)] * 4,
        input_output_aliases={n_pre + k: k for k in range(len(before))},
        compiler_params=_params(("parallel", "parallel")),
    )(*([own[1]] if has_own else []), *args, *before)


def _adamw(w, g, m, v):
    rows, cols = w.shape
    tr = 256 if rows % 256 == 0 else rows

    def body(w_ref, g_ref, m_ref, v_ref, d_ref, mo_ref, vo_ref):
        d_ref[...], mo_ref[...], vo_ref[...] = _adamw_math(w_ref[...], g_ref[...], m_ref[...], v_ref[...])

    blk = pl.BlockSpec((tr, cols), lambda i: (i, 0))
    return pl.pallas_call(
        body, name="adamw", grid=(rows // tr,),
        in_specs=[blk] * 4, out_specs=[blk] * 3, out_shape=[SDS((rows, cols), F32)] * 3,
        compiler_params=_params(("parallel",)),
    )(w, g, m, v)


def _all_gather(shards):
    na = len(shards)
    chips = (4, 2, 6)

    def body(*refs):
        ins, outs = refs[:na], refs[na:2 * na]
        send_sems, recv_sems, local_sems = refs[2 * na:]
        _, mine = _flip(0)

        def rows(a, idx):
            r = shards[a].shape[0]
            return outs[a].at[pl.ds(pl.multiple_of(idx * r, 16), r), :]

        def copy(a, slot, block_idx, to, src=None):
            return pltpu.make_async_remote_copy(
                src_ref=rows(a, block_idx) if src is None else src, dst_ref=rows(a, block_idx),
                send_sem=send_sems.at[a, slot], recv_sem=recv_sems.at[a, slot],
                device_id=to, device_id_type=MESH_ID)

        sibling, sibling_idx = _flip(1)
        local, started = [], []
        for a in range(na):
            cp = pltpu.make_async_copy(ins[a], rows(a, mine), local_sems.at[a])
            cp.start()
            local.append(cp)
            first = [copy(a, 0, mine, sibling, src=ins[a])]
            first += [copy(a, 1 + j, mine, _flip(k)[0], src=ins[a]) for j, k in enumerate(chips)]
            for cp in first:
                cp.start()
            started += first
        for a in range(na):
            for j, k in enumerate(chips):
                _, theirs = _flip(k)
                copy(a, 1 + j, theirs, _flip(0)[0]).wait_recv()
                fwd = copy(a, 4 + j, theirs, sibling)
                fwd.start()
                started.append(fwd)
        for a in range(na):
            copy(a, 0, sibling_idx, _flip(0)[0]).wait_recv()
            for j, k in enumerate(chips):
                _, theirs = _flip(k | 1)
                copy(a, 4 + j, theirs, _flip(0)[0]).wait_recv()
        for cp in started:
            cp.wait_send()
        for cp in local:
            cp.wait()

    return pl.pallas_call(
        body, name="all_gather_weights",
        in_specs=[ANY] * na, out_specs=[ANY] * na,
        out_shape=[SDS((N_DEV * s.shape[0], s.shape[1]), s.dtype) for s in shards],
        scratch_shapes=[pltpu.SemaphoreType.DMA((na, 7)), pltpu.SemaphoreType.DMA((na, 7)),
                        pltpu.SemaphoreType.DMA((na,))],
        compiler_params=pltpu.CompilerParams(has_side_effects=True),
    )(*shards)


def _scatter_blocks_of(g_ref, rows, idx):
    return g_ref.at[pl.ds(pl.multiple_of(idx * rows, 16), rows), :]


def _scatter_start(g):
    rows = g.shape[0] // N_DEV
    land_shape = (N_DEV, rows, g.shape[1])

    def body(g_ref, land_ref, send_sems, recv_sems, g_thru, land_thru, token):
        _, mine = _flip(0)
        for k in range(1, N_DEV):
            peer, theirs = _flip(k)
            pltpu.make_async_remote_copy(
                src_ref=_scatter_blocks_of(g_ref, rows, theirs), dst_ref=land_ref.at[mine],
                send_sem=send_sems.at[k - 1], recv_sem=recv_sems.at[k - 1],
                device_id=peer, device_id_type=MESH_ID).start()
        token[...] = jnp.zeros_like(token)

    hbm, sem = pl.BlockSpec(memory_space=pltpu.HBM), pl.BlockSpec(memory_space=pltpu.SEMAPHORE)
    return pl.pallas_call(
        body, name="scatter_start",
        out_shape=(pltpu.SemaphoreType.DMA((N_DEV - 1,)), pltpu.SemaphoreType.DMA((N_DEV - 1,)),
                   pltpu.HBM(g.shape, g.dtype), pltpu.HBM(land_shape, g.dtype), SDS((8, 128), F32)),
        in_specs=(hbm, hbm), out_specs=(sem, sem, hbm, hbm, pl.BlockSpec(memory_space=pltpu.VMEM)),
        input_output_aliases={0: 2, 1: 3},
        compiler_params=pltpu.CompilerParams(has_side_effects=pltpu.SideEffectType.DATAFLOW_SIDE_EFFECTING),
    )(pltpu.with_memory_space_constraint(g, pltpu.HBM),
      pltpu.with_memory_space_constraint(lax.empty(land_shape, g.dtype), pltpu.HBM))


def _scatter_wait(send_sems, recv_sems, g_thru, land_thru, after):
    rows = g_thru.shape[0] // N_DEV

    def body(g_ref, land_ref, send_sems, recv_sems, *rest):
        me, _ = _flip(0)
        for k in range(1, N_DEV):
            _, theirs = _flip(k)
            copy = pltpu.make_async_remote_copy(
                src_ref=_scatter_blocks_of(g_ref, rows, theirs), dst_ref=land_ref.at[theirs],
                send_sem=send_sems.at[k - 1], recv_sem=recv_sems.at[k - 1],
                device_id=me, device_id_type=MESH_ID)
            copy.wait_send()
            copy.wait_recv()

    hbm, sem = pl.BlockSpec(memory_space=pltpu.HBM), pl.BlockSpec(memory_space=pltpu.SEMAPHORE)
    return pl.pallas_call(
        body, name="scatter_wait",
        out_shape=(pltpu.HBM(g_thru.shape, g_thru.dtype), pltpu.HBM(land_thru.shape, land_thru.dtype)),
        in_specs=(hbm, hbm, sem, sem) + (ANY,) * len(after), out_specs=(hbm, hbm), input_output_aliases={0: 0, 1: 1},
        compiler_params=pltpu.CompilerParams(has_side_effects=pltpu.SideEffectType.DATAFLOW_SIDE_EFFECTING),
    )(g_thru, land_thru, send_sems, recv_sems, *after)


def _all_reduce_small(packed):
    shape = packed.shape

    def body(p_ref, o_ref, slots, send_sems, recv_sems):
        me, mine = _flip(0)
        slots[mine] = p_ref[...]
        sends = []
        for k in range(1, N_DEV):
            peer, _ = _flip(k)
            cp = pltpu.make_async_remote_copy(
                src_ref=p_ref, dst_ref=slots.at[mine], send_sem=send_sems.at[k - 1], recv_sem=recv_sems.at[k - 1],
                device_id=peer, device_id_type=MESH_ID)
            cp.start()
            sends.append(cp)
        for k in range(1, N_DEV):
            _, theirs = _flip(k)
            pltpu.make_async_remote_copy(
                src_ref=p_ref, dst_ref=slots.at[theirs], send_sem=send_sems.at[k - 1],
                recv_sem=recv_sems.at[k - 1], device_id=me, device_id_type=MESH_ID).wait_recv()
        for cp in sends:
            cp.wait_send()
        acc = slots[0]
        for s in range(1, N_DEV):
            acc = acc + slots[s]
        o_ref[...] = acc

    vm = pl.BlockSpec(memory_space=pltpu.VMEM)
    return pl.pallas_call(
        body, name="all_reduce_small", in_specs=[vm], out_specs=vm, out_shape=SDS(shape, F32),
        scratch_shapes=[pltpu.VMEM((N_DEV,) + shape, F32), pltpu.SemaphoreType.DMA((7,)),
                        pltpu.SemaphoreType.DMA((7,))],
        compiler_params=pltpu.CompilerParams(has_side_effects=True),
    )(packed)


def _layer_fwd(x, p, tabs, ex):
    z, h_t, q, qt, k, v, vt = _in_proj(x, p["norm_g"], p["w_in_t"], p["qn"], p["kn"], tabs["ca"], tabs["sa"],
                                       tabs["ones"])
    oa, lse, *gathered = _attn_fwd(q, k, vt, ex)
    qrot, krot, vb, orr, on = _ret_fwd(z, p["lgf"], p["lgb"], p["gnw"], tabs["cr"], tabs["sr"])
    return z, h_t, q, qt, k, v, lse, oa, qrot, krot, vb, orr, on, gathered


def _layer_bwd(dxo, s, p, tabs, ex_attn, scatter_w_in):
    doa, don, dz_m, d_wout, d_wb_t = _merge_bwd(dxo, s["z"], s["oa"], s["on"], s["ya"], s["yb"], p["wb_t"], p["w_out"])
    dq_a, dk_a, dv_a, *recv_attn = _attn_bwd(s["q"], s["qt"], s["k"], s["v"], doa, s["oa"], s["lse"],
                                              ex_attn(d_wb_t, d_wout))
    dz_a, d_qn, d_kn = _attn_post_bwd(dq_a, dk_a, dv_a, s["z"], p["qn"], p["kn"], tabs["ca"], tabs["sa"],
                                      tabs["ones"])
    dq_r, dk_r, dv_r, d_gnw, d_lgf, d_lgb = _ret_bwd(s["qrot"], s["krot"], s["vb"], s["orr"], don, p["gnw"],
                                                     p["lgf"], p["lgb"])
    dqr, dkr, dvr = _ret_post_bwd(dq_r, dk_r, dv_r, tabs["cr"], tabs["sr"])
    buf = _dw_in(s["h_t"], dz_a, dz_m, dqr, dkr, dvr)
    pending, token = None, None
    if scatter_w_in:
        *pending, token = _scatter_start(buf)
    dx, d_norm_g = _in_bwd(dxo, s["x"], p["norm_g"], p["w_in_t"], dz_a, dz_m, dqr, dkr, dvr, token)
    grads = dict(w_in_t=buf, wb_t=d_wb_t, w_out=d_wout, norm_g=d_norm_g, gnw=d_gnw,
                 qn=d_qn.reshape(ATTN_Q_HEADS, ATTN_HEAD_DIM).sum(axis=0),
                 kn=d_kn.reshape(ATTN_KV_HEADS, ATTN_HEAD_DIM).sum(axis=0),
                 lgf=d_lgf[:, 0, 0], lgb=d_lgb[:, 0, 0])
    return dx, grads, recv_attn, pending


def _adamw_nd(w, g, m, v):
    shape = w.shape
    two_d = (1, shape[0]) if w.ndim == 1 else (-1, shape[-1])
    out = _adamw(w.reshape(two_d), g.reshape(two_d), m.reshape(two_d), v.reshape(two_d))
    return tuple(o.reshape(shape) for o in out)


def kernel(x, norm_g, w_in, attn_q_norm, attn_k_norm, ret_decay_fwd, ret_decay_bwd, ret_gn_w, w_branch_attn, w_branch_ret, w_out, final_norm_g, loss_target, m_norm_g, m_w_in, m_attn_q_norm, m_attn_k_norm, m_ret_decay_fwd, m_ret_decay_bwd, m_ret_gn_w, m_w_branch_attn, m_w_branch_ret, m_w_out, m_final_norm_g, v_norm_g, v_w_in, v_attn_q_norm, v_attn_k_norm, v_ret_decay_fwd, v_ret_decay_bwd, v_ret_gn_w, v_w_branch_attn, v_w_branch_ret, v_w_out, v_final_norm_g):
    t, d = x.shape[1], x.shape[2]
    x2, target = x[0], loss_target[0]

    w_in_sh, wb_sh, wout_sh = [], [], []
    for l in range(DEPTH):
        w_in_sh.append(jnp.swapaxes(w_in[l], 0, 1).astype(BF16))
        wb_sh.append(jnp.concatenate([w_branch_attn[l].T, w_branch_ret[l].T], axis=1).astype(BF16))
        wout_sh.append(w_out[l].astype(BF16))

    ca, sa = _rope_tables(t, ATTN_HEAD_DIM)
    cr, sr = _rope_tables(t, RET_HEAD_DIM)
    grp = jnp.arange(ATTN_WIDTH) // ATTN_HEAD_DIM
    tabs = dict(ca=jnp.tile(ca, (1, 2)), sa=jnp.tile(sa, (1, 2)), cr=cr, sr=sr,
                ones=jnp.where(grp[:, None] == grp[None, :], 1.0 / ATTN_HEAD_DIM, 0.0).astype(BF16))
    layers = []
    for l in range(DEPTH):
        layers.append(dict(
            norm_g=norm_g[l][None], qn=jnp.tile(attn_q_norm[l], ATTN_Q_HEADS)[None],
            kn=jnp.tile(attn_k_norm[l], ATTN_KV_HEADS)[None], gnw=ret_gn_w[l][None],
            lgf=jax.nn.log_sigmoid(ret_decay_fwd[l]), lgb=jax.nn.log_sigmoid(ret_decay_bwd[l])))

    layers[0]["w_in_t"], = _all_gather([w_in_sh[0]])
    gathers = [_Exchange("gather", [wb_sh[0], wout_sh[0], w_in_sh[1]]), _Exchange("gather", [wb_sh[1], wout_sh[1]])]
    h = x2
    saved = []
    for l in range(DEPTH):
        p = layers[l]
        z, h_t, q, qt, k, v, lse, oa, qrot, krot, vb, orr, on, got = _layer_fwd(h, p, tabs, gathers[l])
        p["wb_t"], p["w_out"] = got[0], got[1]
        if l == 0:
            layers[1]["w_in_t"] = got[2]
        xn, ya, yb = _merge_fwd(h, z, oa, on, p["wb_t"], p["w_out"])
        saved.append(dict(x=h, z=z, h_t=h_t, q=q, qt=qt, k=k, v=v, lse=lse, oa=oa, qrot=qrot, krot=krot, vb=vb,
                          orr=orr, on=on, ya=ya, yb=yb))
        h = xn
    dx, d_final_g, loss_part = _final_loss(h, final_norm_g[None], target)

    grads = [None] * DEPTH
    dx, grads[1], _, _ = _layer_bwd(dx, saved[1], layers[1], tabs, lambda *a: None, False)
    g1 = grads[1]
    ex_attn = lambda d_wb_t, d_wout: _Exchange("scatter", [g1["w_in_t"], g1["wb_t"], g1["w_out"], d_wb_t, d_wout])
    dx, grads[0], recv_attn, pending = _layer_bwd(dx, saved[0], layers[0], tabs, ex_attn, True)
    recv = [None, recv_attn[3], recv_attn[4], recv_attn[0], recv_attn[1], recv_attn[2]]
    tr = lambda a: jnp.swapaxes(a, 1, 2)
    w_in_t = (tr(w_in), tr(m_w_in), tr(v_w_in))
    sharded = {}
    w_in_l1 = _sum_adamw([recv[3]], *w_in_t, 0, 256, layer0=1)
    sharded[id(w_branch_attn)] = [tr(o) for o in _sum_adamw(
        [recv[1], recv[4]], tr(w_branch_attn), tr(m_w_branch_attn), tr(v_w_branch_attn), 0, 512)]
    sharded[id(w_branch_ret)] = [tr(o) for o in _sum_adamw(
        [recv[1], recv[4]], tr(w_branch_ret), tr(m_w_branch_ret), tr(v_w_branch_ret), 512, 512)]
    sharded[id(w_out)] = _sum_adamw([recv[2], recv[5]], w_out, m_w_out, v_w_out, 0, 256)
    g_wba, g_wbr, g_wout = (sharded[id(w)][0] for w in (w_branch_attn, w_branch_ret, w_out))

    packed = jnp.zeros((8, 1024), F32)
    for l in range(DEPTH):
        gl = grads[l]
        packed = packed.at[l].set(gl["norm_g"][0])
        packed = packed.at[2, 512 * l:512 * (l + 1)].set(gl["gnw"][0])
        packed = packed.at[4, 128 * l:128 * l + 64].set(gl["qn"])
        packed = packed.at[4, 256 + 128 * l:256 + 128 * l + 64].set(gl["kn"])
        packed = packed.at[4, 512 + 128 * l:512 + 128 * l + 4].set(gl["lgf"])
        packed = packed.at[4, 768 + 128 * l:768 + 128 * l + 4].set(gl["lgb"])
    packed = packed.at[3].set(d_final_g[0])
    packed = packed.at[5, 0].set(loss_part[0, 0])
    red = _all_reduce_small(packed)
    loss = red[5, 0]
    g_norm_g = red[0:2]
    g_gnw = red[2].reshape(DEPTH, RET_WIDTH)
    g_final = red[3]
    g_qn = jnp.stack([red[4, 128 * l:128 * l + 64] for l in range(DEPTH)])
    g_kn = jnp.stack([red[4, 256 + 128 * l:256 + 128 * l + 64] for l in range(DEPTH)])
    g_lgf = jnp.stack([red[4, 512 + 128 * l:512 + 128 * l + 4] for l in range(DEPTH)])
    g_lgb = jnp.stack([red[4, 768 + 128 * l:768 + 128 * l + 4] for l in range(DEPTH)])
    g_df = g_lgf * jax.nn.sigmoid(-ret_decay_fwd)
    g_db = g_lgb * jax.nn.sigmoid(-ret_decay_bwd)

    grad_w = [g_norm_g, None, g_qn, g_kn, g_df, g_db, g_gnw, g_wba, g_wbr, g_wout, g_final]
    weights = [norm_g, w_in, attn_q_norm, attn_k_norm, ret_decay_fwd, ret_decay_bwd, ret_gn_w, w_branch_attn,
               w_branch_ret, w_out, final_norm_g]
    ms = [m_norm_g, m_w_in, m_attn_q_norm, m_attn_k_norm, m_ret_decay_fwd, m_ret_decay_bwd, m_ret_gn_w,
          m_w_branch_attn, m_w_branch_ret, m_w_out, m_final_norm_g]
    vs = [v_norm_g, v_w_in, v_attn_q_norm, v_attn_k_norm, v_ret_decay_fwd, v_ret_decay_bwd, v_ret_gn_w,
          v_w_branch_attn, v_w_branch_ret, v_w_out, v_final_norm_g]
    upd = [None if w is w_in else sharded[id(w)][1:] if id(w) in sharded else _adamw_nd(w, g, m, v)
           for w, g, m, v in zip(weights, grad_w, ms, vs)]

    done = [dx, w_in_l1[0], g_wout] + [u[0] for w, u in zip(weights, upd) if u is not None and id(w) not in sharded]
    g_full, recv[0] = _scatter_wait(*pending, done)
    mine = (4 * lax.axis_index("x") + 2 * lax.axis_index("y") + lax.axis_index("c")).astype(jnp.int32)[None]
    w_in_upd = [tr(o) for o in _sum_adamw([recv[0]], *w_in_t, 0, 256, layer0=0, prev=w_in_l1, own=(g_full, mine))]
    grad_w[1], upd[1] = w_in_upd[0], w_in_upd[1:]
    return (loss, dx[None], *grad_w, *[u[0] for u in upd], *[u[1] for u in upd], *[u[2] for u in upd])
```

```python
import functools

import jax
import jax.numpy as jnp
from jax import lax
from jax.experimental import pallas as pl
from jax.experimental.pallas import tpu as pltpu

F32 = jnp.float32
BF16 = jnp.bfloat16
SDS = jax.ShapeDtypeStruct

D_MODEL = 1024
DEPTH = 2
GRID_W = 64
ATTN_Q_HEADS = 8
ATTN_KV_HEADS = 2
ATTN_HEAD_DIM = 64
ATTN_WIDTH = 512
ATTN_KV_WIDTH = 128
RET_HEADS = 4
RET_HEAD_DIM = 128
RET_WIDTH = 512
RET_CHUNK = 128
ATTN_KEY_CHUNK = 512
ATTN_BWD_KEY_CHUNK = 1024
ATTN_BWD_QUERY_TILE = 512
ATTN_FWD_QUERY_TILE = 512
QK_DOTS_PER_CHUNK = 4
EXP_LAG = 3
ROPE_THETA = 10000.0
EPS = 1e-6
D_IN = 5376
N_DEV = 8

ADAM_LR = 0.001
ADAM_B1 = 0.9
ADAM_B2 = 0.999
ADAM_EPS = 1e-08
ADAM_WD = 0.01
ADAM_STEP = 10

SEG = {
    "qa": (0, 512, 0),
    "ga": (768, 512, 512),
    "qr": (1280, 512, 1024),
    "kr": (1792, 512, 1536),
    "vr": (2304, 512, 2048),
    "gr": (2816, 512, 2560),
    "gm": (3328, 2048, 3072),
    "ka": (512, 128, 5120),
    "va": (640, 128, 5248),
}

VMEM_LIMIT = 60 * 1024 * 1024
NT = (((1,), (1,)), ((), ()))
TN = (((0,), (0,)), ((), ()))
MESH_ID = pl.DeviceIdType.MESH
ANY = pl.BlockSpec(memory_space=pl.ANY)


def _params(sem=None, vmem=VMEM_LIMIT):
    return pltpu.CompilerParams(dimension_semantics=sem, vmem_limit_bytes=vmem)


def _dot(a, b, dims=None):
    if dims is None:
        return jnp.dot(a, b, preferred_element_type=F32)
    return lax.dot_general(a, b, dims, preferred_element_type=F32)


def _sigmoid(x):
    return 1.0 / (1.0 + jnp.exp(-x))


def _swap_halves(x, q):
    n = x.shape[-1]
    axis = x.ndim - 1
    lane = lax.broadcasted_iota(jnp.int32, x.shape, axis)
    first = (lane % (2 * q)) < q
    return jnp.where(first, pltpu.roll(x, n - q, axis), pltpu.roll(x, q, axis))


def _rope(x, cos, sin_signed, q):
    return x * cos + _swap_halves(x, q) * sin_signed


def _rope_bwd(dy, cos, sin_signed, q):
    return dy * cos - _swap_halves(dy, q) * sin_signed


def _group_mean(v, ones_bd):
    hi = v.astype(BF16)
    lo = (v - hi.astype(F32)).astype(BF16)
    return _dot(hi, ones_bd) + _dot(lo, ones_bd)


def _rope_tables(t, head_dim):
    n_rows = t // GRID_W
    d_axis = head_dim // 2
    inv_freq = ROPE_THETA ** (-jnp.arange(0, d_axis, 2, dtype=F32) / d_axis)
    ar = jnp.arange(n_rows, dtype=F32)[:, None] * inv_freq
    ac = jnp.arange(GRID_W, dtype=F32)[:, None] * inv_freq
    by_row = lambda a: jnp.repeat(a, GRID_W, axis=0)
    by_col = lambda a: jnp.tile(a, (n_rows, 1))
    cr, sr, cc, sc = by_row(jnp.cos(ar)), by_row(jnp.sin(ar)), by_col(jnp.cos(ac)), by_col(jnp.sin(ac))
    return jnp.concatenate([cr, cr, cc, cc], axis=-1), jnp.concatenate([-sr, sr, -sc, sc], axis=-1)


def _me():
    return lax.axis_index("x"), lax.axis_index("y"), lax.axis_index("c")


def _flip(k):
    x, y, c = _me()
    px = 1 - x if k & 4 else x
    py = 1 - y if k & 2 else y
    pc = 1 - c if k & 1 else c
    return (px, py, pc), 4 * px + 2 * py + pc


class _Exchange:
    def __init__(self, kind, srcs):
        self.kind, self.srcs, self.n = kind, list(srcs), len(srcs)
        self.rows = [a.shape[0] if kind == "gather" else a.shape[0] // N_DEV for a in srcs]
        if kind == "gather":
            self.out_shape = [SDS((N_DEV * a.shape[0], a.shape[1]), a.dtype) for a in srcs]
        else:
            self.out_shape = [SDS((N_DEV, a.shape[0] // N_DEV, a.shape[1]), a.dtype) for a in srcs]
        self.scratch = [pltpu.SemaphoreType.DMA((self.n, N_DEV - 1)), pltpu.SemaphoreType.DMA((self.n, N_DEV - 1)),
                        pltpu.SemaphoreType.DMA((self.n,))]

    def _block(self, ref, a, idx):
        r = self.rows[a]
        return ref.at[pl.ds(pl.multiple_of(idx * r, 16), r), :]

    def _src(self, ins, a, idx):
        return ins[a] if self.kind == "gather" else self._block(ins[a], a, idx)

    def _dst(self, outs, a, idx):
        return self._block(outs[a], a, idx) if self.kind == "gather" else outs[a].at[idx]

    def _copies(self, ins, outs, sems):
        send_sems, recv_sems, local_sems = sems
        me, mine = _flip(0)
        local, sends, recvs = [], [], []
        for a in range(self.n):
            local.append(pltpu.make_async_copy(self._src(ins, a, mine), self._dst(outs, a, mine), local_sems.at[a]))
            for k in range(1, N_DEV):
                peer, theirs = _flip(k)
                sem = dict(send_sem=send_sems.at[a, k - 1], recv_sem=recv_sems.at[a, k - 1])
                sends.append(pltpu.make_async_remote_copy(
                    src_ref=self._src(ins, a, theirs), dst_ref=self._dst(outs, a, mine),
                    device_id=peer, device_id_type=MESH_ID, **sem))
                recvs.append(pltpu.make_async_remote_copy(
                    src_ref=self._dst(outs, a, theirs), dst_ref=self._dst(outs, a, theirs),
                    device_id=me, device_id_type=MESH_ID, **sem))
        return local, sends, recvs

    def start(self, ins, outs, sems):
        local, sends, _ = self._copies(ins, outs, sems)
        for cp in local + sends:
            cp.start()

    def wait(self, ins, outs, sems):
        local, sends, recvs = self._copies(ins, outs, sems)
        for cp in sends:
            cp.wait_send()
        for cp in recvs:
            cp.wait_recv()
        for cp in local:
            cp.wait()


def _with_exchange(body, n_in, n_out, n_scratch, ex, first, last):
    if ex is None:
        return body

    def wrapped(*refs):
        ins = refs[:n_in]
        ex_ins = refs[n_in:n_in + ex.n]
        outs = refs[n_in + ex.n:n_in + ex.n + n_out]
        ex_outs = refs[n_in + ex.n + n_out:n_in + 2 * ex.n + n_out]
        rest = refs[n_in + 2 * ex.n + n_out:]
        scratch, sems = rest[:n_scratch], rest[n_scratch:]

        @pl.when(first())
        def _():
            ex.start(ex_ins, ex_outs, sems)

        body(*ins, *outs, *scratch)

        @pl.when(last())
        def _():
            ex.wait(ex_ins, ex_outs, sems)

    return wrapped


def _ex_args(ex):
    if ex is None:
        return [], [], [], [], []
    return [ANY] * ex.n, [ANY] * ex.n, list(ex.out_shape), list(ex.scratch), list(ex.srcs)


def _in_proj(x, g, w_t, qn, kn, cos, sin, ones_bd):
    t, d = x.shape
    tm = min(256, t)
    tk = min(ATTN_KEY_CHUNK, t)
    per_chunk = tk // tm
    hd = ATTN_HEAD_DIM

    def body(x_ref, g_ref, w_ref, qn_ref, kn_ref, c_ref, s_ref, b_ref,
             z_ref, ht_ref, q_out, qt_out, k_out, v_out, vt_out):
        xv = x_ref[...]
        r = lax.rsqrt(jnp.mean(xv * xv, axis=-1, keepdims=True) + EPS)
        h = xv * r * g_ref[...]
        ht_ref[...] = h.T.astype(BF16)
        hb = h.astype(BF16)
        seg = {}
        for name, (nat, w, off) in SEG.items():
            seg[name] = _dot(hb, w_ref[nat:nat + w, :], NT)
            z_ref[:, off:off + w] = seg[name]

        bd = b_ref[...]
        c2, s2 = c_ref[...], s_ref[...]
        cq = jnp.concatenate([c2] * 4, axis=-1)
        sq = jnp.concatenate([s2] * 4, axis=-1)
        xq, xk, xvv = seg["qa"], seg["ka"], seg["va"]
        yq = xq * lax.rsqrt(_group_mean(xq * xq, bd) + EPS) * qn_ref[...]
        yq = _rope(yq, cq, sq, hd // 4) * (hd ** -0.5)
        yqt = yq.T
        for hh in range(ATTN_Q_HEADS):
            q_out[hh] = yq[:, hh * hd:(hh + 1) * hd].astype(BF16)
            qt_out[hh] = yqt[hh * hd:(hh + 1) * hd, :].astype(BF16)
        yk = xk * lax.rsqrt(_group_mean(xk * xk, bd[:ATTN_KV_WIDTH, :ATTN_KV_WIDTH]) + EPS) * kn_ref[...]
        yk = _rope(yk, c2, s2, hd // 4)
        xvt = xvv.T
        ones = jnp.ones((hd, tm), F32)
        for hh in range(ATTN_KV_HEADS):
            k_out[hh] = yk[:, hh * hd:(hh + 1) * hd].astype(BF16)
            v_out[hh] = xvv[:, hh * hd:(hh + 1) * hd].astype(BF16)
            vt_out[hh, 0] = jnp.concatenate([xvt[hh * hd:(hh + 1) * hd, :], ones], axis=0).astype(BF16)

    const = lambda shape: pl.BlockSpec(shape, lambda i: (0,) * len(shape))
    rows = lambda w: pl.BlockSpec((tm, w), lambda i: (i, 0))
    return pl.pallas_call(
        body, name="in_proj", grid=(t // tm,),
        in_specs=[rows(d), const((1, d)), const((D_IN, d)), const((1, 512)), const((1, 128)), rows(128), rows(128),
                  const((512, 512))],
        out_specs=[rows(D_IN), pl.BlockSpec((d, tm), lambda i: (0, i)),
                   pl.BlockSpec((ATTN_Q_HEADS, tm, hd), lambda i: (0, i, 0)),
                   pl.BlockSpec((ATTN_Q_HEADS, hd, tm), lambda i: (0, 0, i)),
                   pl.BlockSpec((ATTN_KV_HEADS, tm, hd), lambda i: (0, i, 0)),
                   pl.BlockSpec((ATTN_KV_HEADS, tm, hd), lambda i: (0, i, 0)),
                   pl.BlockSpec((ATTN_KV_HEADS, 1, 2 * hd, tm), lambda i: (0, i // per_chunk, 0, i % per_chunk))],
        out_shape=[SDS((t, D_IN), F32), SDS((d, t), BF16),
                   SDS((ATTN_Q_HEADS, t, hd), BF16), SDS((ATTN_Q_HEADS, hd, t), BF16),
                   SDS((ATTN_KV_HEADS, t, hd), BF16), SDS((ATTN_KV_HEADS, t, hd), BF16),
                   SDS((ATTN_KV_HEADS, t // tk, 2 * hd, tk), BF16)],
        compiler_params=_params(("parallel",)),
    )(x, g, w_t, qn, kn, cos, sin, ones_bd)


def _attn_fwd(q, k, vt, ex=None):
    t = q.shape[1]
    tq = min(ATTN_FWD_QUERY_TILE, t)
    nk, tk = vt.shape[1], vt.shape[3]
    hd = ATTN_HEAD_DIM
    g = ATTN_Q_HEADS // ATTN_KV_HEADS

    def body(q_ref, k_ref, vt_ref, o_ref, lse_ref, s_scr):
        def pass_a(h, c, m8):
            part = tk // QK_DOTS_PER_CHUNK
            for lo in range(c * tk, (c + 1) * tk, part):
                st = _dot(k_ref[0, lo:lo + part, :], q_ref[h], NT)
                s_scr[h % 2, lo:lo + part, :] = st
                m8 = jnp.maximum(m8, jnp.max(st.reshape(part // 8, 8, tq), axis=0))
            return m8

        def pass_b(h, c, m, acc, after):
            e = jnp.exp(s_scr[h % 2, c * tk:(c + 1) * tk, :] - (m + after * 0.0)).astype(BF16)
            return acc + _dot(vt_ref[0, c], e)

        neg = jnp.full((8, tq), -jnp.inf, F32)
        m8 = neg
        for c in range(nk):
            m8 = pass_a(0, c, m8)
        outs = []
        for h in range(g):
            m = jnp.max(m8, axis=0, keepdims=True)
            acc = jnp.zeros((2 * hd, tq), F32)
            m8 = neg
            done = [m] * EXP_LAG
            for c in range(nk):
                if h + 1 < g:
                    m8 = pass_a(h + 1, c, m8)
                acc = pass_b(h, c, m, acc, done[-EXP_LAG])
                done.append(m8[0:1, :] if h + 1 < g else acc[hd:hd + 1, :])
            l = acc[hd:hd + 1, :]
            outs.append((acc[:hd, :] / l).T)
            lse_ref[h] = m + jnp.log(l)
        o_ref[...] = jnp.concatenate(outs, axis=-1)

    nq = t // tq
    first = lambda: jnp.logical_and(pl.program_id(0) == 0, pl.program_id(1) == 0)
    last = lambda: jnp.logical_and(pl.program_id(0) == ATTN_KV_HEADS - 1, pl.program_id(1) == nq - 1)
    xi, xo, xs, xscr, xargs = _ex_args(ex)
    return pl.pallas_call(
        _with_exchange(body, 3, 2, 1, ex, first, last), name="attn_fwd", grid=(ATTN_KV_HEADS, nq),
        in_specs=[pl.BlockSpec((g, tq, hd), lambda p, i: (p, i, 0)),
                  pl.BlockSpec((1, t, hd), lambda p, i: (p, 0, 0)),
                  pl.BlockSpec((1, nk, 2 * hd, tk), lambda p, i: (p, 0, 0, 0))] + xi,
        out_specs=[pl.BlockSpec((tq, g * hd), lambda p, i: (i, p)),
                   pl.BlockSpec((g, 1, tq), lambda p, i: (p, 0, i))] + xo,
        out_shape=[SDS((t, ATTN_WIDTH), F32), SDS((ATTN_Q_HEADS, 1, t), F32)] + xs,
        scratch_shapes=[pltpu.VMEM((2, t, tq), F32)] + xscr,
        compiler_params=_params(("arbitrary", "arbitrary")),
    )(q, k, vt, *xargs)


class _Dir:
    def __init__(self, lg, strict_future):
        c = RET_CHUNK
        ia = lax.broadcasted_iota(jnp.int32, (c, c), 0).astype(F32)
        ib = lax.broadcasted_iota(jnp.int32, (c, c), 1).astype(F32)
        col = lax.broadcasted_iota(jnp.int32, (c, 1), 0).astype(F32)
        row = lax.broadcasted_iota(jnp.int32, (1, c), 1).astype(F32)
        if strict_future:
            dist = ib - ia
            mask = dist > 0
            self.wq, self.wk, wk_row = c - col, col, row
        else:
            dist = ia - ib
            mask = dist >= 0
            self.wq, self.wk, wk_row = col + 1.0, c - 1.0 - col, c - 1.0 - row
        self.dist = jnp.maximum(dist, 0.0)
        self.d = jnp.where(mask, jnp.exp(self.dist * lg), 0.0)
        self.qd = jnp.exp(self.wq * lg)
        self.kd_col = jnp.exp(self.wk * lg)
        self.kd_row = jnp.exp(wk_row * lg)
        self.cd = jnp.exp(jnp.full((1, 1), float(c), F32) * lg)


def _ret_fwd(z, lgf, lgb, gnw, cos, sin):
    t = z.shape[0]
    c = RET_CHUNK
    nc = t // c
    hd = RET_HEAD_DIM
    unroll = 4 if nc % 4 == 0 else 1

    def body(lgf_ref, lgb_ref, q_ref, k_ref, v_ref, c_ref, s_ref, w_ref,
             qo_ref, ko_ref, vo_ref, orr_ref, on_ref, kt, uf, ub, sfa, sba):
        h = pl.program_id(0)
        fw = _Dir(lgf_ref[h], False)
        bw = _Dir(lgb_ref[h], True)
        cc, ss = c_ref[...], s_ref[...]
        qo_ref[...] = _rope(q_ref[...], cc, ss, hd // 4).astype(BF16)
        kr = _rope(k_ref[...], cc, ss, hd // 4) * (hd ** -0.5)
        ko_ref[...] = kr.astype(BF16)
        vo_ref[...] = v_ref[...].astype(BF16)
        for i in range(nc):
            kt[i] = kr[i * c:(i + 1) * c, :].T.astype(BF16)

        def rows(ci):
            return pl.ds(pl.multiple_of(ci * c, c), c)

        def kv_products(ci, carry):
            vv = vo_ref[rows(ci), :]
            ktf = kt[ci].astype(F32)
            uf[ci] = _dot((ktf * fw.kd_row).astype(BF16), vv)
            ub[ci] = _dot((ktf * bw.kd_row).astype(BF16), vv)
            return carry

        lax.fori_loop(0, nc, kv_products, 0, unroll=unroll)

        def scan(i, carry):
            sf, sb = carry
            j = nc - 1 - i
            sfa[i] = sf.astype(BF16)
            sba[j] = sb.astype(BF16)
            return sf * fw.cd + uf[i], sb * bw.cd + ub[j]

        zero = jnp.zeros((hd, hd), F32)
        lax.fori_loop(0, nc, scan, (zero, zero))
        gw = w_ref[...]

        def outputs(ci, carry):
            sl = rows(ci)
            qq, kk, vv = qo_ref[sl, :], ko_ref[sl, :], vo_ref[sl, :]
            a = _dot(qq, kk, NT)
            o = (_dot((a * fw.d).astype(BF16), vv) + _dot(qq, sfa[ci]) * fw.qd
                 + _dot((a * bw.d).astype(BF16), vv) + _dot(qq, sba[ci]) * bw.qd)
            orr_ref[sl, :] = o
            xc = o - jnp.mean(o, axis=-1, keepdims=True)
            var = jnp.mean(xc * xc, axis=-1, keepdims=True)
            on_ref[sl, :] = xc * lax.rsqrt(var + EPS) * gw
            return carry

        lax.fori_loop(0, nc, outputs, 0, unroll=unroll)

    smem = pl.BlockSpec(memory_space=pltpu.SMEM)
    col = lambda name: (lambda h: (0, SEG[name][2] // 128 + h))
    head = pl.BlockSpec((t, 128), lambda h: (0, h))
    full = pl.BlockSpec((t, 128), lambda h: (0, 0))
    return pl.pallas_call(
        body, name="ret_fwd", grid=(RET_HEADS,),
        in_specs=[smem, smem, pl.BlockSpec((t, 128), col("qr")), pl.BlockSpec((t, 128), col("kr")),
                  pl.BlockSpec((t, 128), col("vr")), full, full, pl.BlockSpec((1, 128), lambda h: (0, h))],
        out_specs=[head, head, head, head, head],
        out_shape=[SDS((t, RET_WIDTH), BF16)] * 3 + [SDS((t, RET_WIDTH), F32)] * 2,
        scratch_shapes=[pltpu.VMEM((nc, hd, c), BF16), pltpu.VMEM((nc, hd, hd), F32), pltpu.VMEM((nc, hd, hd), F32),
                        pltpu.VMEM((nc, hd, hd), BF16), pltpu.VMEM((nc, hd, hd), BF16)],
        compiler_params=_params(("parallel",)),
    )(lgf, lgb, z, z, z, cos, sin, gnw)


def _merge_fwd(x, z, oa, on, wb_t, wout):
    t, d = x.shape
    tm = min(256, t)

    def body(x_ref, ga_ref, gr_ref, gm0_ref, gm1_ref, oa_ref, on_ref, wb_ref, wo_ref, xn_ref, ya_ref, yb_ref):
        ga, gr = ga_ref[...], gr_ref[...]
        ua = ga * _sigmoid(ga) * oa_ref[...]
        ub = gr * _sigmoid(gr) * on_ref[...]
        ya = _dot(ua.astype(BF16), wb_ref[:, :512], NT)
        yb = _dot(ub.astype(BF16), wb_ref[:, 512:], NT)
        ya_ref[...] = ya
        yb_ref[...] = yb
        merged = _sigmoid(gm0_ref[...]) * ya + _sigmoid(gm1_ref[...]) * yb
        xn_ref[...] = x_ref[...] + _dot(merged.astype(BF16), wo_ref[...])

    row = lambda w, j: pl.BlockSpec((tm, w), lambda i: (i, j))
    const = lambda shape: pl.BlockSpec(shape, lambda i: (0, 0))
    return pl.pallas_call(
        body, name="merge_fwd", grid=(t // tm,),
        in_specs=[row(d, 0), row(512, SEG["ga"][2] // 512), row(512, SEG["gr"][2] // 512),
                  row(1024, SEG["gm"][2] // 1024), row(1024, SEG["gm"][2] // 1024 + 1),
                  row(512, 0), row(512, 0), const((d, 1024)), const((d, d))],
        out_specs=[row(d, 0), row(d, 0), row(d, 0)],
        out_shape=[SDS((t, d), F32)] * 3,
        compiler_params=_params(("parallel",)),
    )(x, z, z, z, z, oa, on, wb_t, wout)


def _final_loss(x, g, target):
    t, d = x.shape
    tm = min(512, t)
    n = t // tm

    def body(x_ref, g_ref, t_ref, dx_ref, dg_ref, loss_ref, acc_g, acc_l):
        i = pl.program_id(0)

        @pl.when(i == 0)
        def _():
            acc_g[...] = jnp.zeros_like(acc_g)
            acc_l[...] = jnp.zeros_like(acc_l)

        xv, gv = x_ref[...], g_ref[...]
        r = lax.rsqrt(jnp.mean(xv * xv, axis=-1, keepdims=True) + EPS)
        xh = xv * r
        err = xh * gv - t_ref[...]
        dy = err * (1.0 / d)
        gy = dy * gv
        dx_ref[...] = r * (gy - xh * jnp.mean(gy * xh, axis=-1, keepdims=True))
        acc_g[...] += jnp.sum((dy * xh).reshape(tm // 8, 8, d), axis=0)
        acc_l[...] += jnp.sum((err * err).reshape(tm // 8, 8, d), axis=0)

        @pl.when(i == n - 1)
        def _():
            dg_ref[...] = jnp.sum(acc_g[...], axis=0, keepdims=True)
            tot = jnp.sum(jnp.sum(acc_l[...], axis=0, keepdims=True), axis=1, keepdims=True)
            loss_ref[...] = jnp.broadcast_to(tot * (0.5 / d), (1, 128))

    return pl.pallas_call(
        body, name="final_loss", grid=(n,),
        in_specs=[pl.BlockSpec((tm, d), lambda i: (i, 0)), pl.BlockSpec((1, d), lambda i: (0, 0)),
                  pl.BlockSpec((tm, d), lambda i: (i, 0))],
        out_specs=[pl.BlockSpec((tm, d), lambda i: (i, 0)), pl.BlockSpec((1, d), lambda i: (0, 0)),
                   pl.BlockSpec((1, 128), lambda i: (0, 0))],
        out_shape=[SDS((t, d), F32), SDS((1, d), F32), SDS((1, 128), F32)],
        scratch_shapes=[pltpu.VMEM((8, d), F32), pltpu.VMEM((8, d), F32)],
        compiler_params=_params(("arbitrary",)),
    )(x, g, target)


def _merge_bwd(dxo, z, oa, on, ya, yb, wb_t, wout):
    t, d = dxo.shape
    tm = min(256, t)
    n = t // tm

    def body(dx_ref, ga_ref, gr_ref, gm0_ref, gm1_ref, oa_ref, on_ref, ya_ref, yb_ref, wb_ref, wo_ref,
             doa_ref, don_ref, dz_ref, dwo_ref, dwb_ref, acc_o, acc_b):
        i = pl.program_id(0)

        @pl.when(i == 0)
        def _():
            acc_o[...] = jnp.zeros_like(acc_o)
            acc_b[...] = jnp.zeros_like(acc_b)

        dxb = dx_ref[...].astype(BF16)
        ya, yb = ya_ref[...], yb_ref[...]
        g0, g1 = _sigmoid(gm0_ref[...]), _sigmoid(gm1_ref[...])
        mb = (g0 * ya + g1 * yb).astype(BF16)
        dm = _dot(dxb, wo_ref[...], NT)
        dya = (dm * g0).astype(BF16)
        dyb = (dm * g1).astype(BF16)
        dz_ref[:, 1024:2048] = (dm * ya * g0 * (1.0 - g0)).astype(BF16)
        dz_ref[:, 2048:3072] = (dm * yb * g1 * (1.0 - g1)).astype(BF16)

        def branch(g_ref, o_ref, dy, w, do_ref, lo):
            gv, ov = g_ref[...], o_ref[...]
            sg = _sigmoid(gv)
            silu = gv * sg
            du = _dot(dy, w)
            do_ref[...] = du * silu
            dz_ref[:, lo:lo + 512] = (du * ov * (sg * (1.0 + gv * (1.0 - sg)))).astype(BF16)
            acc_b[:, lo:lo + 512] += _dot(dy, (silu * ov).astype(BF16), TN)

        branch(ga_ref, oa_ref, dya, wb_ref[:, :512], doa_ref, 0)
        branch(gr_ref, on_ref, dyb, wb_ref[:, 512:], don_ref, 512)
        acc_o[...] += _dot(mb, dxb, TN)

        @pl.when(i == n - 1)
        def _():
            dwo_ref[...] = acc_o[...].astype(BF16)
            dwb_ref[...] = acc_b[...].astype(BF16)

    row = lambda w, j: pl.BlockSpec((tm, w), lambda i: (i, j))
    const = lambda shape: pl.BlockSpec(shape, lambda i: (0, 0))
    return pl.pallas_call(
        body, name="merge_bwd", grid=(n,),
        in_specs=[row(d, 0), row(512, SEG["ga"][2] // 512), row(512, SEG["gr"][2] // 512),
                  row(1024, SEG["gm"][2] // 1024), row(1024, SEG["gm"][2] // 1024 + 1),
                  row(512, 0), row(512, 0), row(d, 0), row(d, 0), const((d, 1024)), const((d, d))],
        out_specs=[row(512, 0), row(512, 0), row(3072, 0), const((d, d)), const((d, 1024))],
        out_shape=[SDS((t, 512), F32), SDS((t, 512), F32), SDS((t, 3072), BF16), SDS((d, d), BF16),
                   SDS((d, 1024), BF16)],
        scratch_shapes=[pltpu.VMEM((d, d), F32), pltpu.VMEM((d, 1024), F32)],
        compiler_params=_params(("arbitrary",)),
    )(dxo, z, z, z, z, oa, on, ya, yb, wb_t, wout)


def _ret_bwd(qrot, krot, vb, orr, don, gnw, lgf, lgb):
    t = qrot.shape[0]
    c = RET_CHUNK
    nc = t // c
    hd = RET_HEAD_DIM
    unroll = 2 if nc % 2 == 0 else 1

    def body(lgf_ref, lgb_ref, q_ref, k_ref, v_ref, o_ref, dn_ref, w_ref,
             dq_ref, dk_ref, dv_ref, dw_ref, dlf_ref, dlb_ref, qt, kt, dob, uf, ub, wf, wb, sfa, sba, gfa, gba):
        h = pl.program_id(0)
        fw = _Dir(lgf_ref[h], False)
        bw = _Dir(lgb_ref[h], True)
        fw.dt, bw.dt = fw.d.T, bw.d.T

        o = o_ref[...]
        xc = o - jnp.mean(o, axis=-1, keepdims=True)
        r = lax.rsqrt(jnp.mean(xc * xc, axis=-1, keepdims=True) + EPS)
        xh = xc * r
        dn = dn_ref[...]
        gy = dn * w_ref[...]
        d_o = r * (gy - jnp.mean(gy, axis=-1, keepdims=True) - xh * jnp.mean(gy * xh, axis=-1, keepdims=True))
        dw_ref[...] = jnp.sum(dn * xh, axis=0, keepdims=True)
        dob[...] = d_o.astype(BF16)
        for i in range(nc):
            qt[i] = q_ref[i * c:(i + 1) * c, :].astype(F32).T.astype(BF16)
            kt[i] = k_ref[i * c:(i + 1) * c, :].astype(F32).T.astype(BF16)

        def rows(ci):
            return pl.ds(pl.multiple_of(ci * c, c), c)

        def products(ci, carry):
            sl = rows(ci)
            vv, do32 = v_ref[sl, :], dob[sl, :].astype(F32)
            ktf = kt[ci].astype(F32)
            uf[ci] = _dot((ktf * fw.kd_row).astype(BF16), vv)
            ub[ci] = _dot((ktf * bw.kd_row).astype(BF16), vv)
            wf[ci] = _dot(qt[ci], (do32 * fw.qd).astype(BF16))
            wb[ci] = _dot(qt[ci], (do32 * bw.qd).astype(BF16))
            return carry

        lax.fori_loop(0, nc, products, 0, unroll=unroll)

        def scan(i, carry):
            sf, sb, gf, gb = carry
            j = nc - 1 - i
            sfa[i] = sf.astype(BF16)
            sba[j] = sb.astype(BF16)
            gfa[j] = gf.astype(BF16)
            gba[i] = gb.astype(BF16)
            return sf * fw.cd + uf[i], sb * bw.cd + ub[j], gf * fw.cd + wf[j], gb * bw.cd + wb[i]

        zero = jnp.zeros((hd, hd), F32)
        lax.fori_loop(0, nc, scan, (zero, zero, zero, zero))

        def one_dir(p, s_all, g_all, ci, qq, kk, vv, do, a, bm):
            sb, gb = s_all[ci], g_all[ci]
            doq = (do.astype(F32) * p.qd).astype(BF16)
            dqc = _dot(doq, sb, NT)
            kkd = (kk.astype(F32) * p.kd_col).astype(BF16)
            dk2 = _dot(vv, gb, NT) * p.kd_col
            terms = (p.dist * p.d * a * bm + p.wq * qq.astype(F32) * dqc + p.wk * kk.astype(F32) * dk2
                     + (float(c) * p.cd) * gb.astype(F32) * sb.astype(F32))
            return dqc, dk2, _dot(kkd, gb), terms

        d_both, dt_both = fw.d + bw.d, fw.dt + bw.dt

        def chunk(ci, carry):
            af, ab = carry
            sl = rows(ci)
            qq, kk, vv, do = q_ref[sl, :], k_ref[sl, :], v_ref[sl, :], dob[sl, :]
            a, bm = _dot(qq, kk, NT), _dot(do, vv, NT)
            at, bt = _dot(kk, qq, NT), _dot(vv, do, NT)
            dqf, dkf, dvf, tf = one_dir(fw, sfa, gfa, ci, qq, kk, vv, do, a, bm)
            dqb, dkb, dvb, tb = one_dir(bw, sba, gba, ci, qq, kk, vv, do, a, bm)
            dq_ref[sl, :] = _dot((bm * d_both).astype(BF16), kk) + dqf + dqb
            dk_ref[sl, :] = _dot((bt * dt_both).astype(BF16), qq) + dkf + dkb
            dv_ref[sl, :] = _dot((at * dt_both).astype(BF16), do) + dvf + dvb
            return af + tf, ab + tb

        af, ab = lax.fori_loop(0, nc, chunk, (zero, zero), unroll=unroll)
        tot = lambda m: jnp.sum(jnp.sum(m, axis=0, keepdims=True), axis=1, keepdims=True)
        dlf_ref[...] = jnp.broadcast_to(tot(af).reshape(1, 1, 1), (1, 8, 128))
        dlb_ref[...] = jnp.broadcast_to(tot(ab).reshape(1, 1, 1), (1, 8, 128))

    smem = pl.BlockSpec(memory_space=pltpu.SMEM)
    head = pl.BlockSpec((t, 128), lambda h: (0, h))
    vec = pl.BlockSpec((1, 128), lambda h: (0, h))
    scal = pl.BlockSpec((1, 8, 128), lambda h: (h, 0, 0))
    mats = lambda dt: pltpu.VMEM((nc, hd, hd), dt)
    return pl.pallas_call(
        body, name="ret_bwd", grid=(RET_HEADS,),
        in_specs=[smem, smem, head, head, head, head, head, vec],
        out_specs=[head, head, head, vec, scal, scal],
        out_shape=[SDS((t, RET_WIDTH), F32)] * 3 + [SDS((1, RET_WIDTH), F32), SDS((RET_HEADS, 8, 128), F32),
                                                   SDS((RET_HEADS, 8, 128), F32)],
        scratch_shapes=[pltpu.VMEM((nc, hd, c), BF16), pltpu.VMEM((nc, hd, c), BF16), pltpu.VMEM((t, hd), BF16),
                        mats(F32), mats(F32), mats(F32), mats(F32), mats(BF16), mats(BF16), mats(BF16), mats(BF16)],
        compiler_params=_params(("parallel",)),
    )(lgf, lgb, qrot, krot, vb, orr, don, gnw)


def _ret_post_bwd(dq, dk, dv, cos, sin):
    t = dq.shape[0]
    tm = min(512, t)
    hd = RET_HEAD_DIM

    def body(dq_ref, dk_ref, dv_ref, c_ref, s_ref, oq_ref, ok_ref, ov_ref):
        cc = jnp.concatenate([c_ref[...]] * 4, axis=-1)
        ss = jnp.concatenate([s_ref[...]] * 4, axis=-1)
        oq_ref[...] = _rope_bwd(dq_ref[...], cc, ss, hd // 4).astype(BF16)
        ok_ref[...] = (_rope_bwd(dk_ref[...], cc, ss, hd // 4) * (hd ** -0.5)).astype(BF16)
        ov_ref[...] = dv_ref[...].astype(BF16)

    blk = pl.BlockSpec((tm, 512), lambda i: (i, 0))
    tab = pl.BlockSpec((tm, 128), lambda i: (i, 0))
    return pl.pallas_call(
        body, name="ret_post_bwd", grid=(t // tm,),
        in_specs=[blk, blk, blk, tab, tab], out_specs=[blk, blk, blk],
        out_shape=[SDS((t, 512), BF16)] * 3,
        compiler_params=_params(("parallel",)),
    )(dq, dk, dv, cos, sin)


def _attn_bwd(q, qt, k, v, doa, oa, lse, ex=None):
    t = q.shape[1]
    tq = min(ATTN_BWD_QUERY_TILE, t)
    nq = t // tq
    tk = min(ATTN_BWD_KEY_CHUNK, t)
    nk = t // tk
    hd = ATTN_HEAD_DIM
    scale = hd ** -0.5

    def body(q_ref, qt_ref, k_ref, v_ref, do_ref, o_ref, lse_ref, dq_ref, dkt_ref, dvt_ref):
        p, i = pl.program_id(0), pl.program_id(1)

        @pl.when(jnp.logical_and(p % 2 == 0, i == 0))
        def _():
            dkt_ref[...] = jnp.zeros_like(dkt_ref)
            dvt_ref[...] = jnp.zeros_like(dvt_ref)

        dov, ov = do_ref[...], o_ref[...]
        dovt = dov.T
        lanes = lambda col: jnp.concatenate([col] * (tk // 128), axis=1)
        outs = []
        for j in range(2):
            qq, qqt = q_ref[j], qt_ref[j]
            do32 = dov[:, j * hd:(j + 1) * hd]
            do, dot_ = do32.astype(BF16), dovt[j * hd:(j + 1) * hd, :].astype(BF16)
            dd = lanes(jnp.broadcast_to(jnp.sum(do32 * ov[:, j * hd:(j + 1) * hd], axis=1, keepdims=True), (tq, 128)))
            lse_j = lanes(jnp.broadcast_to(lse_ref[j], (128, tq)).T)
            dq = jnp.zeros((tq, hd), F32)
            for c in range(nk):
                sl = slice(c * tk, (c + 1) * tk)
                kc, vc = k_ref[0, sl, :], v_ref[0, sl, :]
                pr = jnp.exp(_dot(qq, kc, NT) - lse_j)
                ds = (pr * (_dot(do, vc, NT) - dd)).astype(BF16)
                dvt_ref[0, :, sl] += _dot(dot_, pr.astype(BF16))
                dkt_ref[0, :, sl] += _dot(qqt, ds)
                dq = dq + _dot(ds, kc)
            outs.append(dq * scale)
        dq_ref[...] = jnp.concatenate(outs, axis=-1)

    kv = pl.BlockSpec((1, t, hd), lambda p, i: (p // 2, 0, 0))
    kvt = pl.BlockSpec((1, hd, t), lambda p, i: (p // 2, 0, 0))
    pair = pl.BlockSpec((tq, 128), lambda p, i: (i, p))
    first = lambda: jnp.logical_and(pl.program_id(0) == 0, pl.program_id(1) == 0)
    last = lambda: jnp.logical_and(pl.program_id(0) == 3, pl.program_id(1) == nq - 1)
    xi, xo, xs, xscr, xargs = _ex_args(ex)
    return pl.pallas_call(
        _with_exchange(body, 7, 3, 0, ex, first, last), name="attn_bwd", grid=(4, nq),
        in_specs=[pl.BlockSpec((2, tq, hd), lambda p, i: (p, i, 0)), pl.BlockSpec((2, hd, tq), lambda p, i: (p, 0, i)),
                  kv, kv, pair, pair, pl.BlockSpec((2, 1, tq), lambda p, i: (p, 0, i))] + xi,
        out_specs=[pair, kvt, kvt] + xo,
        out_shape=[SDS((t, ATTN_WIDTH), F32), SDS((ATTN_KV_HEADS, hd, t), F32),
                   SDS((ATTN_KV_HEADS, hd, t), F32)] + xs,
        scratch_shapes=xscr,
        compiler_params=_params(("arbitrary", "arbitrary")),
    )(q, qt, k, v, doa, oa, lse, *xargs)


def _attn_post_bwd(dq, dk, dv, z, qn, kn, cos, sin, ones_bd):
    t = z.shape[0]
    tm = min(512, t)
    n = t // tm
    hd = ATTN_HEAD_DIM

    def body(dq_ref, dk_ref, dv_ref, zq_ref, zkv_ref, qn_ref, kn_ref, c_ref, s_ref, b_ref,
             dz_ref, dqn_ref, dkn_ref, acc_q, acc_k):
        i = pl.program_id(0)

        @pl.when(i == 0)
        def _():
            acc_q[...] = jnp.zeros_like(acc_q)
            acc_k[...] = jnp.zeros_like(acc_k)

        bd = b_ref[...]
        c2, s2 = c_ref[...], s_ref[...]

        def norm_bwd(dy, x, w, ones, cos_t, sin_t, acc):
            dyr = _rope_bwd(dy, cos_t, sin_t, hd // 4)
            r = lax.rsqrt(_group_mean(x * x, ones) + EPS)
            xh = x * r
            gy = dyr * w
            acc[...] += jnp.sum((dyr * xh).reshape(tm // 8, 8, x.shape[-1]), axis=0)
            return r * (gy - xh * _group_mean(gy * xh, ones))

        cq = jnp.concatenate([c2] * 4, axis=-1)
        sq = jnp.concatenate([s2] * 4, axis=-1)
        dz_ref[:, :512] = norm_bwd(dq_ref[...], zq_ref[...], qn_ref[...], bd, cq, sq, acc_q).astype(BF16)
        zkv = zkv_ref[...]
        dkk = jnp.concatenate([dk_ref[0], dk_ref[1]], axis=0).T
        dz_ref[:, 512:640] = norm_bwd(dkk, zkv[:, :128], kn_ref[...], bd[:128, :128], c2, s2, acc_k).astype(BF16)
        dz_ref[:, 640:768] = jnp.concatenate([dv_ref[0], dv_ref[1]], axis=0).T.astype(BF16)

        @pl.when(i == n - 1)
        def _():
            dqn_ref[...] = jnp.sum(acc_q[...], axis=0, keepdims=True)
            dkn_ref[...] = jnp.sum(acc_k[...], axis=0, keepdims=True)

    kv_blk = SEG["ka"][2] // 256
    kvs = pl.BlockSpec((ATTN_KV_HEADS, hd, tm), lambda i: (0, 0, i))
    const = lambda shape: pl.BlockSpec(shape, lambda i: (0, 0))
    return pl.pallas_call(
        body, name="attn_post_bwd", grid=(n,),
        in_specs=[pl.BlockSpec((tm, 512), lambda i: (i, 0)), kvs, kvs,
                  pl.BlockSpec((tm, 512), lambda i: (i, 0)), pl.BlockSpec((tm, 256), lambda i: (i, kv_blk)),
                  const((1, 512)), const((1, 128)),
                  pl.BlockSpec((tm, 128), lambda i: (i, 0)), pl.BlockSpec((tm, 128), lambda i: (i, 0)),
                  const((512, 512))],
        out_specs=[pl.BlockSpec((tm, 768), lambda i: (i, 0)), const((1, 512)), const((1, 128))],
        out_shape=[SDS((t, 768), BF16), SDS((1, 512), F32), SDS((1, 128), F32)],
        scratch_shapes=[pltpu.VMEM((8, 512), F32), pltpu.VMEM((8, 128), F32)],
        compiler_params=_params(("arbitrary",)),
    )(dq, dk, dv, z, z, qn, kn, cos, sin, ones_bd)


def _in_bwd(dxo, x, g, w_t, dz_a, dz_m, dqr, dkr, dvr, after=None):
    t, d = x.shape
    tm = min(256, t)
    n = t // tm
    parts = [(0, 0, 768, 0), (1, 0, 512, SEG["ga"][0]), (2, 0, 512, SEG["qr"][0]), (3, 0, 512, SEG["kr"][0]),
             (4, 0, 512, SEG["vr"][0]), (1, 512, 2560, SEG["gr"][0])]

    def body(dx_ref, x_ref, g_ref, w_ref, a_ref, m_ref, q_ref, k_ref, v_ref, o_ref, dg_ref, acc):
        i = pl.program_id(0)

        @pl.when(i == 0)
        def _():
            acc[...] = jnp.zeros_like(acc)

        pieces = [a_ref, m_ref, q_ref, k_ref, v_ref]
        dh = jnp.zeros((tm, d), F32)
        for pi, lo, w, row in parts:
            dh = dh + _dot(pieces[pi][:, lo:lo + w], w_ref[row:row + w, :])
        xv = x_ref[...]
        r = lax.rsqrt(jnp.mean(xv * xv, axis=-1, keepdims=True) + EPS)
        xh = xv * r
        gy = dh * g_ref[...]
        o_ref[...] = dx_ref[...] + r * (gy - xh * jnp.mean(gy * xh, axis=-1, keepdims=True))
        acc[...] += jnp.sum((dh * xh).reshape(tm // 8, 8, d), axis=0)

        @pl.when(i == n - 1)
        def _():
            dg_ref[...] = jnp.sum(acc[...], axis=0, keepdims=True)

    row = lambda w: pl.BlockSpec((tm, w), lambda i: (i, 0))
    const = lambda shape: pl.BlockSpec(shape, lambda i: (0, 0))
    extra = [] if after is None else [after]
    return pl.pallas_call(
        (lambda *refs: body(*refs[:9], *refs[9 + len(extra):])), name="in_bwd", grid=(n,),
        in_specs=[row(d), row(d), const((1, d)), const((D_IN, d)), row(768), row(3072), row(512), row(512),
                  row(512)] + [const(a.shape) for a in extra],
        out_specs=[row(d), const((1, d))],
        out_shape=[SDS((t, d), F32), SDS((1, d), F32)],
        scratch_shapes=[pltpu.VMEM((8, d), F32)],
        compiler_params=_params(("arbitrary",)),
    )(dxo, x, g, w_t, dz_a, dz_m, dqr, dkr, dvr, *extra)


def _dw_in(h_t, dz_a, dz_m, dqr, dkr, dvr):
    d, t = h_t.shape
    tn = 256
    parts = [(0, 0, 0, 3), (1, 0, SEG["ga"][0] // tn, 2), (2, 0, SEG["qr"][0] // tn, 2),
             (3, 0, SEG["kr"][0] // tn, 2), (4, 0, SEG["vr"][0] // tn, 2), (1, 2, SEG["gr"][0] // tn, 10)]
    pieces = [dz_a, dz_m, dqr, dkr, dvr]

    def col_block(pi):
        mine = [(c0, r0, n) for q, c0, r0, n in parts if q == pi]

        def index(j):
            c0, r0, n = mine[0]
            blk = c0 + jnp.clip(j - r0, 0, n - 1)
            for c0, r0, n in mine[1:]:
                blk = jnp.where(j >= r0, c0 + jnp.clip(j - r0, 0, n - 1), blk)
            return 0, blk

        return index

    def body(h_ref, *refs):
        o_ref = refs[-1]
        j = pl.program_id(0)
        for pi, _, r0, n in parts:
            @pl.when(jnp.logical_and(j >= r0, j < r0 + n))
            def _(p_ref=refs[pi]):
                o_ref[...] = _dot(h_ref[...], p_ref[...]).T.astype(BF16)

    return pl.pallas_call(
        body, name="dw_in", grid=(D_IN // tn,),
        in_specs=[pl.BlockSpec((d, t), lambda j: (0, 0))] + [pl.BlockSpec((t, tn), col_block(pi)) for pi in range(5)],
        out_specs=pl.BlockSpec((tn, d), lambda j: (j, 0)),
        out_shape=SDS((D_IN, d), BF16),
        compiler_params=_params(("arbitrary",)),
    )(h_t, *pieces)


def _adamw_math(w, g, m, v):
    mn = ADAM_B1 * m + (1.0 - ADAM_B1) * g
    vn = ADAM_B2 * v + (1.0 - ADAM_B2) * (g * g)
    m_hat = mn / (1.0 - ADAM_B1 ** ADAM_STEP)
    v_hat = vn / (1.0 - ADAM_B2 ** ADAM_STEP)
    return -ADAM_LR * (m_hat / (jnp.sqrt(v_hat) + ADAM_EPS) + ADAM_WD * w), mn, vn


def _sum_adamw(recvs, w, m, v, lane0, tn, layer0=0, prev=None, own=None):
    _, r, c = w.shape
    j0 = lane0 // tn
    n = len(recvs)
    has_own = own is not None

    def body(*refs):
        mine_ref, refs = (refs[0], refs[1:]) if has_own else (None, refs)
        w_ref, m_ref, v_ref = refs[n:n + 3]
        g_ref, d_ref, mo_ref, vo_ref = refs[-4:]

        def run(r_ref):
            def slot(s):
                if has_own:
                    return jnp.where(mine_ref[0] == s, refs[n + 3][...], r_ref[s]).astype(F32)
                return r_ref[s].astype(F32)

            g = slot(0)
            for s in range(1, N_DEV):
                g = g + slot(s)
            g_ref[0] = g
            d_ref[0], mo_ref[0], vo_ref[0] = _adamw_math(w_ref[0], g, m_ref[0], v_ref[0])

        for i in range(n):
            pl.when(pl.program_id(0) == i)(functools.partial(run, refs[i]))

    slots = pl.BlockSpec((N_DEV, r, tn), lambda i, j, *_: (0, 0, j0 + j))
    blk = pl.BlockSpec((1, r, tn), lambda i, j, *_: (layer0 + i, 0, j))
    before = [] if prev is None else list(prev)
    in_specs, args = [slots] * n + [blk] * 3, [*recvs, w, m, v]
    if has_own:
        assert n == 1
        in_specs.append(pl.BlockSpec((r, tn), lambda i, j, mine: (mine[0], j0 + j)))
        args.append(own[0])
    n_pre = len(args) + has_own
    return pl.pallas_call(
        body, name="sum_adamw",
        grid_spec=pltpu.PrefetchScalarGridSpec(
            num_scalar_prefetch=int(has_own), grid=(n, c // tn),
            in_specs=in_specs + [ANY] * len(before), out_specs=[blk] * 4),
        out_shape=[SDS(w.shape, F32)] * 4,
        input_output_aliases={n_pre + k: k for k in range(len(before))},
        compiler_params=_params(("parallel", "parallel")),
    )(*([own[1]] if has_own else []), *args, *before)


def _adamw(w, g, m, v):
    rows, cols = w.shape
    tr = 256 if rows % 256 == 0 else rows

    def body(w_ref, g_ref, m_ref, v_ref, d_ref, mo_ref, vo_ref):
        d_ref[...], mo_ref[...], vo_ref[...] = _adamw_math(w_ref[...], g_ref[...], m_ref[...], v_ref[...])

    blk = pl.BlockSpec((tr, cols), lambda i: (i, 0))
    return pl.pallas_call(
        body, name="adamw", grid=(rows // tr,),
        in_specs=[blk] * 4, out_specs=[blk] * 3, out_shape=[SDS((rows, cols), F32)] * 3,
        compiler_params=_params(("parallel",)),
    )(w, g, m, v)


def _all_gather(shards):
    na = len(shards)
    chips = (4, 2, 6)

    def body(*refs):
        ins, outs = refs[:na], refs[na:2 * na]
        send_sems, recv_sems, local_sems = refs[2 * na:]
        _, mine = _flip(0)

        def rows(a, idx):
            r = shards[a].shape[0]
            return outs[a].at[pl.ds(pl.multiple_of(idx * r, 16), r), :]

        def copy(a, slot, block_idx, to, src=None):
            return pltpu.make_async_remote_copy(
                src_ref=rows(a, block_idx) if src is None else src, dst_ref=rows(a, block_idx),
                send_sem=send_sems.at[a, slot], recv_sem=recv_sems.at[a, slot],
                device_id=to, device_id_type=MESH_ID)

        sibling, sibling_idx = _flip(1)
        local, started = [], []
        for a in range(na):
            cp = pltpu.make_async_copy(ins[a], rows(a, mine), local_sems.at[a])
            cp.start()
            local.append(cp)
            first = [copy(a, 0, mine, sibling, src=ins[a])]
            first += [copy(a, 1 + j, mine, _flip(k)[0], src=ins[a]) for j, k in enumerate(chips)]
            for cp in first:
                cp.start()
            started += first
        for a in range(na):
            for j, k in enumerate(chips):
                _, theirs = _flip(k)
                copy(a, 1 + j, theirs, _flip(0)[0]).wait_recv()
                fwd = copy(a, 4 + j, theirs, sibling)
                fwd.start()
                started.append(fwd)
        for a in range(na):
            copy(a, 0, sibling_idx, _flip(0)[0]).wait_recv()
            for j, k in enumerate(chips):
                _, theirs = _flip(k | 1)
                copy(a, 4 + j, theirs, _flip(0)[0]).wait_recv()
        for cp in started:
            cp.wait_send()
        for cp in local:
            cp.wait()

    return pl.pallas_call(
        body, name="all_gather_weights",
        in_specs=[ANY] * na, out_specs=[ANY] * na,
        out_shape=[SDS((N_DEV * s.shape[0], s.shape[1]), s.dtype) for s in shards],
        scratch_shapes=[pltpu.SemaphoreType.DMA((na, 7)), pltpu.SemaphoreType.DMA((na, 7)),
                        pltpu.SemaphoreType.DMA((na,))],
        compiler_params=pltpu.CompilerParams(has_side_effects=True),
    )(*shards)


def _scatter_blocks_of(g_ref, rows, idx):
    return g_ref.at[pl.ds(pl.multiple_of(idx * rows, 16), rows), :]


def _scatter_start(g):
    rows = g.shape[0] // N_DEV
    land_shape = (N_DEV, rows, g.shape[1])

    def body(g_ref, land_ref, send_sems, recv_sems, g_thru, land_thru, token):
        _, mine = _flip(0)
        for k in range(1, N_DEV):
            peer, theirs = _flip(k)
            pltpu.make_async_remote_copy(
                src_ref=_scatter_blocks_of(g_ref, rows, theirs), dst_ref=land_ref.at[mine],
                send_sem=send_sems.at[k - 1], recv_sem=recv_sems.at[k - 1],
                device_id=peer, device_id_type=MESH_ID).start()
        token[...] = jnp.zeros_like(token)

    hbm, sem = pl.BlockSpec(memory_space=pltpu.HBM), pl.BlockSpec(memory_space=pltpu.SEMAPHORE)
    return pl.pallas_call(
        body, name="scatter_start",
        out_shape=(pltpu.SemaphoreType.DMA((N_DEV - 1,)), pltpu.SemaphoreType.DMA((N_DEV - 1,)),
                   pltpu.HBM(g.shape, g.dtype), pltpu.HBM(land_shape, g.dtype), SDS((8, 128), F32)),
        in_specs=(hbm, hbm), out_specs=(sem, sem, hbm, hbm, pl.BlockSpec(memory_space=pltpu.VMEM)),
        input_output_aliases={0: 2, 1: 3},
        compiler_params=pltpu.CompilerParams(has_side_effects=pltpu.SideEffectType.DATAFLOW_SIDE_EFFECTING),
    )(pltpu.with_memory_space_constraint(g, pltpu.HBM),
      pltpu.with_memory_space_constraint(lax.empty(land_shape, g.dtype), pltpu.HBM))


def _scatter_wait(send_sems, recv_sems, g_thru, land_thru, after):
    rows = g_thru.shape[0] // N_DEV

    def body(g_ref, land_ref, send_sems, recv_sems, *rest):
        me, _ = _flip(0)
        for k in range(1, N_DEV):
            _, theirs = _flip(k)
            copy = pltpu.make_async_remote_copy(
                src_ref=_scatter_blocks_of(g_ref, rows, theirs), dst_ref=land_ref.at[theirs],
                send_sem=send_sems.at[k - 1], recv_sem=recv_sems.at[k - 1],
                device_id=me, device_id_type=MESH_ID)
            copy.wait_send()
            copy.wait_recv()

    hbm, sem = pl.BlockSpec(memory_space=pltpu.HBM), pl.BlockSpec(memory_space=pltpu.SEMAPHORE)
    return pl.pallas_call(
        body, name="scatter_wait",
        out_shape=(pltpu.HBM(g_thru.shape, g_thru.dtype), pltpu.HBM(land_thru.shape, land_thru.dtype)),
        in_specs=(hbm, hbm, sem, sem) + (ANY,) * len(after), out_specs=(hbm, hbm), input_output_aliases={0: 0, 1: 1},
        compiler_params=pltpu.CompilerParams(has_side_effects=pltpu.SideEffectType.DATAFLOW_SIDE_EFFECTING),
    )(g_thru, land_thru, send_sems, recv_sems, *after)


def _all_reduce_small(packed, after):
    shape = packed.shape

    def body(p_ref, after_ref, o_ref, slots, send_sems, recv_sems):
        me, mine = _flip(0)
        slots[mine] = p_ref[...]
        sends = []
        for k in range(1, N_DEV):
            peer, _ = _flip(k)
            cp = pltpu.make_async_remote_copy(
                src_ref=p_ref, dst_ref=slots.at[mine], send_sem=send_sems.at[k - 1], recv_sem=recv_sems.at[k - 1],
                device_id=peer, device_id_type=MESH_ID)
            cp.start()
            sends.append(cp)
        for k in range(1, N_DEV):
            _, theirs = _flip(k)
            pltpu.make_async_remote_copy(
                src_ref=p_ref, dst_ref=slots.at[theirs], send_sem=send_sems.at[k - 1],
                recv_sem=recv_sems.at[k - 1], device_id=me, device_id_type=MESH_ID).wait_recv()
        for cp in sends:
            cp.wait_send()
        acc = slots[0]
        for s in range(1, N_DEV):
            acc = acc + slots[s]
        o_ref[...] = acc

    vm = pl.BlockSpec(memory_space=pltpu.VMEM)
    return pl.pallas_call(
        body, name="all_reduce_small", in_specs=[vm, ANY], out_specs=vm, out_shape=SDS(shape, F32),
        scratch_shapes=[pltpu.VMEM((N_DEV,) + shape, F32), pltpu.SemaphoreType.DMA((7,)),
                        pltpu.SemaphoreType.DMA((7,))],
        compiler_params=pltpu.CompilerParams(has_side_effects=True),
    )(packed, after)


def _layer_fwd(x, p, tabs, ex):
    z, h_t, q, qt, k, v, vt = _in_proj(x, p["norm_g"], p["w_in_t"], p["qn"], p["kn"], tabs["ca"], tabs["sa"],
                                       tabs["ones"])
    oa, lse, *gathered = _attn_fwd(q, k, vt, ex)
    qrot, krot, vb, orr, on = _ret_fwd(z, p["lgf"], p["lgb"], p["gnw"], tabs["cr"], tabs["sr"])
    return z, h_t, q, qt, k, v, lse, oa, qrot, krot, vb, orr, on, gathered


def _layer_bwd(dxo, s, p, tabs, ex_attn, scatter_w_in):
    doa, don, dz_m, d_wout, d_wb_t = _merge_bwd(dxo, s["z"], s["oa"], s["on"], s["ya"], s["yb"], p["wb_t"], p["w_out"])
    dq_a, dk_a, dv_a, *recv_attn = _attn_bwd(s["q"], s["qt"], s["k"], s["v"], doa, s["oa"], s["lse"],
                                              ex_attn(d_wb_t, d_wout))
    dz_a, d_qn, d_kn = _attn_post_bwd(dq_a, dk_a, dv_a, s["z"], p["qn"], p["kn"], tabs["ca"], tabs["sa"],
                                      tabs["ones"])
    dq_r, dk_r, dv_r, d_gnw, d_lgf, d_lgb = _ret_bwd(s["qrot"], s["krot"], s["vb"], s["orr"], don, p["gnw"],
                                                     p["lgf"], p["lgb"])
    dqr, dkr, dvr = _ret_post_bwd(dq_r, dk_r, dv_r, tabs["cr"], tabs["sr"])
    buf = _dw_in(s["h_t"], dz_a, dz_m, dqr, dkr, dvr)
    pending, token = None, None
    if scatter_w_in:
        *pending, token = _scatter_start(buf)
    dx, d_norm_g = _in_bwd(dxo, s["x"], p["norm_g"], p["w_in_t"], dz_a, dz_m, dqr, dkr, dvr, token)
    grads = dict(w_in_t=buf, wb_t=d_wb_t, w_out=d_wout, norm_g=d_norm_g, gnw=d_gnw,
                 qn=d_qn.reshape(ATTN_Q_HEADS, ATTN_HEAD_DIM).sum(axis=0),
                 kn=d_kn.reshape(ATTN_KV_HEADS, ATTN_HEAD_DIM).sum(axis=0),
                 lgf=d_lgf[:, 0, 0], lgb=d_lgb[:, 0, 0])
    return dx, grads, recv_attn, pending


def _adamw_nd(w, g, m, v):
    shape = w.shape
    two_d = (1, shape[0]) if w.ndim == 1 else (-1, shape[-1])
    out = _adamw(w.reshape(two_d), g.reshape(two_d), m.reshape(two_d), v.reshape(two_d))
    return tuple(o.reshape(shape) for o in out)


def kernel(x, norm_g, w_in, attn_q_norm, attn_k_norm, ret_decay_fwd, ret_decay_bwd, ret_gn_w, w_branch_attn, w_branch_ret, w_out, final_norm_g, loss_target, m_norm_g, m_w_in, m_attn_q_norm, m_attn_k_norm, m_ret_decay_fwd, m_ret_decay_bwd, m_ret_gn_w, m_w_branch_attn, m_w_branch_ret, m_w_out, m_final_norm_g, v_norm_g, v_w_in, v_attn_q_norm, v_attn_k_norm, v_ret_decay_fwd, v_ret_decay_bwd, v_ret_gn_w, v_w_branch_attn, v_w_branch_ret, v_w_out, v_final_norm_g):
    t, d = x.shape[1], x.shape[2]
    x2, target = x[0], loss_target[0]

    w_in_sh, wb_sh, wout_sh = [], [], []
    for l in range(DEPTH):
        w_in_sh.append(jnp.swapaxes(w_in[l], 0, 1).astype(BF16))
        wb_sh.append(jnp.concatenate([w_branch_attn[l].T, w_branch_ret[l].T], axis=1).astype(BF16))
        wout_sh.append(w_out[l].astype(BF16))

    ca, sa = _rope_tables(t, ATTN_HEAD_DIM)
    cr, sr = _rope_tables(t, RET_HEAD_DIM)
    grp = jnp.arange(ATTN_WIDTH) // ATTN_HEAD_DIM
    tabs = dict(ca=jnp.tile(ca, (1, 2)), sa=jnp.tile(sa, (1, 2)), cr=cr, sr=sr,
                ones=jnp.where(grp[:, None] == grp[None, :], 1.0 / ATTN_HEAD_DIM, 0.0).astype(BF16))
    layers = []
    for l in range(DEPTH):
        layers.append(dict(
            norm_g=norm_g[l][None], qn=jnp.tile(attn_q_norm[l], ATTN_Q_HEADS)[None],
            kn=jnp.tile(attn_k_norm[l], ATTN_KV_HEADS)[None], gnw=ret_gn_w[l][None],
            lgf=jax.nn.log_sigmoid(ret_decay_fwd[l]), lgb=jax.nn.log_sigmoid(ret_decay_bwd[l])))

    layers[0]["w_in_t"], = _all_gather([w_in_sh[0]])
    gathers = [_Exchange("gather", [wb_sh[0], wout_sh[0], w_in_sh[1]]), _Exchange("gather", [wb_sh[1], wout_sh[1]])]
    h = x2
    saved = []
    for l in range(DEPTH):
        p = layers[l]
        z, h_t, q, qt, k, v, lse, oa, qrot, krot, vb, orr, on, got = _layer_fwd(h, p, tabs, gathers[l])
        p["wb_t"], p["w_out"] = got[0], got[1]
        if l == 0:
            layers[1]["w_in_t"] = got[2]
        xn, ya, yb = _merge_fwd(h, z, oa, on, p["wb_t"], p["w_out"])
        saved.append(dict(x=h, z=z, h_t=h_t, q=q, qt=qt, k=k, v=v, lse=lse, oa=oa, qrot=qrot, krot=krot, vb=vb,
                          orr=orr, on=on, ya=ya, yb=yb))
        h = xn
    dx, d_final_g, loss_part = _final_loss(h, final_norm_g[None], target)

    grads = [None] * DEPTH
    dx, grads[1], _, _ = _layer_bwd(dx, saved[1], layers[1], tabs, lambda *a: None, False)
    g1 = grads[1]
    ex_attn = lambda d_wb_t, d_wout: _Exchange("scatter", [g1["w_in_t"], g1["wb_t"], g1["w_out"], d_wb_t, d_wout])
    dx, grads[0], recv_attn, pending = _layer_bwd(dx, saved[0], layers[0], tabs, ex_attn, True)
    recv = [None, recv_attn[3], recv_attn[4], recv_attn[0], recv_attn[1], recv_attn[2]]
    tr = lambda a: jnp.swapaxes(a, 1, 2)
    w_in_t = (tr(w_in), tr(m_w_in), tr(v_w_in))
    sharded = {}
    w_in_l1 = _sum_adamw([recv[3]], *w_in_t, 0, 256, layer0=1)
    sharded[id(w_branch_attn)] = [tr(o) for o in _sum_adamw(
        [recv[1], recv[4]], tr(w_branch_attn), tr(m_w_branch_attn), tr(v_w_branch_attn), 0, 512)]
    sharded[id(w_branch_ret)] = [tr(o) for o in _sum_adamw(
        [recv[1], recv[4]], tr(w_branch_ret), tr(m_w_branch_ret), tr(v_w_branch_ret), 512, 512)]
    sharded[id(w_out)] = _sum_adamw([recv[2], recv[5]], w_out, m_w_out, v_w_out, 0, 256)
    g_wba, g_wbr, g_wout = (sharded[id(w)][0] for w in (w_branch_attn, w_branch_ret, w_out))
    g_full, recv[0] = _scatter_wait(*pending, [dx, w_in_l1[0], g_wout])
    mine = (4 * lax.axis_index("x") + 2 * lax.axis_index("y") + lax.axis_index("c")).astype(jnp.int32)[None]
    w_in_raw = _sum_adamw([recv[0]], *w_in_t, 0, 256, layer0=0, prev=w_in_l1, own=(g_full, mine))
    sharded[id(w_in)] = [tr(o) for o in w_in_raw]

    packed = jnp.zeros((8, 1024), F32)
    for l in range(DEPTH):
        gl = grads[l]
        packed = packed.at[l].set(gl["norm_g"][0])
        packed = packed.at[2, 512 * l:512 * (l + 1)].set(gl["gnw"][0])
        packed = packed.at[4, 128 * l:128 * l + 64].set(gl["qn"])
        packed = packed.at[4, 256 + 128 * l:256 + 128 * l + 64].set(gl["kn"])
        packed = packed.at[4, 512 + 128 * l:512 + 128 * l + 4].set(gl["lgf"])
        packed = packed.at[4, 768 + 128 * l:768 + 128 * l + 4].set(gl["lgb"])
    packed = packed.at[3].set(d_final_g[0])
    packed = packed.at[5, 0].set(loss_part[0, 0])
    red = _all_reduce_small(packed, w_in_raw[0])
    loss = red[5, 0]
    g_norm_g = red[0:2]
    g_gnw = red[2].reshape(DEPTH, RET_WIDTH)
    g_final = red[3]
    g_qn = jnp.stack([red[4, 128 * l:128 * l + 64] for l in range(DEPTH)])
    g_kn = jnp.stack([red[4, 256 + 128 * l:256 + 128 * l + 64] for l in range(DEPTH)])
    g_lgf = jnp.stack([red[4, 512 + 128 * l:512 + 128 * l + 4] for l in range(DEPTH)])
    g_lgb = jnp.stack([red[4, 768 + 128 * l:768 + 128 * l + 4] for l in range(DEPTH)])
    g_df = g_lgf * jax.nn.sigmoid(-ret_decay_fwd)
    g_db = g_lgb * jax.nn.sigmoid(-ret_decay_bwd)

    grad_w = [g_norm_g, sharded[id(w_in)][0], g_qn, g_kn, g_df, g_db, g_gnw, g_wba, g_wbr, g_wout, g_final]
    weights = [norm_g, w_in, attn_q_norm, attn_k_norm, ret_decay_fwd, ret_decay_bwd, ret_gn_w, w_branch_attn,
               w_branch_ret, w_out, final_norm_g]
    ms = [m_norm_g, m_w_in, m_attn_q_norm, m_attn_k_norm, m_ret_decay_fwd, m_ret_decay_bwd, m_ret_gn_w,
          m_w_branch_attn, m_w_branch_ret, m_w_out, m_final_norm_g]
    vs = [v_norm_g, v_w_in, v_attn_q_norm, v_attn_k_norm, v_ret_decay_fwd, v_ret_decay_bwd, v_ret_gn_w,
          v_w_branch_attn, v_w_branch_ret, v_w_out, v_final_norm_g]
    upd = [sharded[id(w)][1:] if id(w) in sharded else _adamw_nd(w, g, m, v)
           for w, g, m, v in zip(weights, grad_w, ms, vs)]
    return (loss, dx[None], *grad_w, *[u[0] for u in upd], *[u[1] for u in upd], *[u[2] for u in upd])
```

```python
import functools

import jax
import jax.numpy as jnp
from jax import lax
from jax.experimental import pallas as pl
from jax.experimental.pallas import tpu as pltpu

F32 = jnp.float32
BF16 = jnp.bfloat16
SDS = jax.ShapeDtypeStruct

D_MODEL = 1024
DEPTH = 2
GRID_W = 64
ATTN_Q_HEADS = 8
ATTN_KV_HEADS = 2
ATTN_HEAD_DIM = 64
ATTN_WIDTH = 512
ATTN_KV_WIDTH = 128
RET_HEADS = 4
RET_HEAD_DIM = 128
RET_WIDTH = 512
RET_CHUNK = 128
ATTN_KEY_CHUNK = 512
ATTN_BWD_KEY_CHUNK = 1024
ATTN_BWD_QUERY_TILE = 512
ATTN_FWD_QUERY_TILE = 512
QK_DOTS_PER_CHUNK = 4
EXP_LAG = 3
ROPE_THETA = 10000.0
EPS = 1e-6
D_IN = 5376
N_DEV = 8

ADAM_LR = 0.001
ADAM_B1 = 0.9
ADAM_B2 = 0.999
ADAM_EPS = 1e-08
ADAM_WD = 0.01
ADAM_STEP = 10

SEG = {
    "qa": (0, 512, 0),
    "ga": (768, 512, 512),
    "qr": (1280, 512, 1024),
    "kr": (1792, 512, 1536),
    "vr": (2304, 512, 2048),
    "gr": (2816, 512, 2560),
    "gm": (3328, 2048, 3072),
    "ka": (512, 128, 5120),
    "va": (640, 128, 5248),
}

VMEM_LIMIT = 60 * 1024 * 1024
NT = (((1,), (1,)), ((), ()))
TN = (((0,), (0,)), ((), ()))
MESH_ID = pl.DeviceIdType.MESH
ANY = pl.BlockSpec(memory_space=pl.ANY)


def _params(sem=None, vmem=VMEM_LIMIT):
    return pltpu.CompilerParams(dimension_semantics=sem, vmem_limit_bytes=vmem)


def _dot(a, b, dims=None):
    if dims is None:
        return jnp.dot(a, b, preferred_element_type=F32)
    return lax.dot_general(a, b, dims, preferred_element_type=F32)


def _sigmoid(x):
    return 1.0 / (1.0 + jnp.exp(-x))


def _swap_halves(x, q):
    n = x.shape[-1]
    axis = x.ndim - 1
    lane = lax.broadcasted_iota(jnp.int32, x.shape, axis)
    first = (lane % (2 * q)) < q
    return jnp.where(first, pltpu.roll(x, n - q, axis), pltpu.roll(x, q, axis))


def _rope(x, cos, sin_signed, q):
    return x * cos + _swap_halves(x, q) * sin_signed


def _rope_bwd(dy, cos, sin_signed, q):
    return dy * cos - _swap_halves(dy, q) * sin_signed


def _group_mean(v, ones_bd):
    hi = v.astype(BF16)
    lo = (v - hi.astype(F32)).astype(BF16)
    return _dot(hi, ones_bd) + _dot(lo, ones_bd)


def _rope_tables(t, head_dim):
    n_rows = t // GRID_W
    d_axis = head_dim // 2
    inv_freq = ROPE_THETA ** (-jnp.arange(0, d_axis, 2, dtype=F32) / d_axis)
    ar = jnp.arange(n_rows, dtype=F32)[:, None] * inv_freq
    ac = jnp.arange(GRID_W, dtype=F32)[:, None] * inv_freq
    by_row = lambda a: jnp.repeat(a, GRID_W, axis=0)
    by_col = lambda a: jnp.tile(a, (n_rows, 1))
    cr, sr, cc, sc = by_row(jnp.cos(ar)), by_row(jnp.sin(ar)), by_col(jnp.cos(ac)), by_col(jnp.sin(ac))
    return jnp.concatenate([cr, cr, cc, cc], axis=-1), jnp.concatenate([-sr, sr, -sc, sc], axis=-1)


def _me():
    return lax.axis_index("x"), lax.axis_index("y"), lax.axis_index("c")


def _flip(k):
    x, y, c = _me()
    px = 1 - x if k & 4 else x
    py = 1 - y if k & 2 else y
    pc = 1 - c if k & 1 else c
    return (px, py, pc), 4 * px + 2 * py + pc


class _Exchange:
    def __init__(self, kind, srcs):
        self.kind, self.srcs, self.n = kind, list(srcs), len(srcs)
        self.rows = [a.shape[0] if kind == "gather" else a.shape[0] // N_DEV for a in srcs]
        if kind == "gather":
            self.out_shape = [SDS((N_DEV * a.shape[0], a.shape[1]), a.dtype) for a in srcs]
        else:
            self.out_shape = [SDS((N_DEV, a.shape[0] // N_DEV, a.shape[1]), a.dtype) for a in srcs]
        self.scratch = [pltpu.SemaphoreType.DMA((self.n, N_DEV - 1)), pltpu.SemaphoreType.DMA((self.n, N_DEV - 1)),
                        pltpu.SemaphoreType.DMA((self.n,))]

    def _block(self, ref, a, idx):
        r = self.rows[a]
        return ref.at[pl.ds(pl.multiple_of(idx * r, 16), r), :]

    def _src(self, ins, a, idx):
        return ins[a] if self.kind == "gather" else self._block(ins[a], a, idx)

    def _dst(self, outs, a, idx):
        return self._block(outs[a], a, idx) if self.kind == "gather" else outs[a].at[idx]

    def _copies(self, ins, outs, sems):
        send_sems, recv_sems, local_sems = sems
        me, mine = _flip(0)
        local, sends, recvs = [], [], []
        for a in range(self.n):
            local.append(pltpu.make_async_copy(self._src(ins, a, mine), self._dst(outs, a, mine), local_sems.at[a]))
            for k in range(1, N_DEV):
                peer, theirs = _flip(k)
                sem = dict(send_sem=send_sems.at[a, k - 1], recv_sem=recv_sems.at[a, k - 1])
                sends.append(pltpu.make_async_remote_copy(
                    src_ref=self._src(ins, a, theirs), dst_ref=self._dst(outs, a, mine),
                    device_id=peer, device_id_type=MESH_ID, **sem))
                recvs.append(pltpu.make_async_remote_copy(
                    src_ref=self._dst(outs, a, theirs), dst_ref=self._dst(outs, a, theirs),
                    device_id=me, device_id_type=MESH_ID, **sem))
        return local, sends, recvs

    def start(self, ins, outs, sems):
        local, sends, _ = self._copies(ins, outs, sems)
        for cp in local + sends:
            cp.start()

    def wait(self, ins, outs, sems):
        local, sends, recvs = self._copies(ins, outs, sems)
        for cp in sends:
            cp.wait_send()
        for cp in recvs:
            cp.wait_recv()
        for cp in local:
            cp.wait()


def _with_exchange(body, n_in, n_out, n_scratch, ex, first, last):
    if ex is None:
        return body

    def wrapped(*refs):
        ins = refs[:n_in]
        ex_ins = refs[n_in:n_in + ex.n]
        outs = refs[n_in + ex.n:n_in + ex.n + n_out]
        ex_outs = refs[n_in + ex.n + n_out:n_in + 2 * ex.n + n_out]
        rest = refs[n_in + 2 * ex.n + n_out:]
        scratch, sems = rest[:n_scratch], rest[n_scratch:]

        @pl.when(first())
        def _():
            ex.start(ex_ins, ex_outs, sems)

        body(*ins, *outs, *scratch)

        @pl.when(last())
        def _():
            ex.wait(ex_ins, ex_outs, sems)

    return wrapped


def _ex_args(ex):
    if ex is None:
        return [], [], [], [], []
    return [ANY] * ex.n, [ANY] * ex.n, list(ex.out_shape), list(ex.scratch), list(ex.srcs)


def _in_proj(x, g, w_t, qn, kn, cos, sin, ones_bd):
    t, d = x.shape
    tm = min(256, t)
    tk = min(ATTN_KEY_CHUNK, t)
    per_chunk = tk // tm
    hd = ATTN_HEAD_DIM

    def body(x_ref, g_ref, w_ref, qn_ref, kn_ref, c_ref, s_ref, b_ref,
             z_ref, ht_ref, q_out, qt_out, k_out, v_out, vt_out):
        xv = x_ref[...]
        r = lax.rsqrt(jnp.mean(xv * xv, axis=-1, keepdims=True) + EPS)
        h = xv * r * g_ref[...]
        ht_ref[...] = h.T.astype(BF16)
        hb = h.astype(BF16)
        seg = {}
        for name, (nat, w, off) in SEG.items():
            seg[name] = _dot(hb, w_ref[nat:nat + w, :], NT)
            z_ref[:, off:off + w] = seg[name]

        bd = b_ref[...]
        c2, s2 = c_ref[...], s_ref[...]
        cq = jnp.concatenate([c2] * 4, axis=-1)
        sq = jnp.concatenate([s2] * 4, axis=-1)
        xq, xk, xvv = seg["qa"], seg["ka"], seg["va"]
        yq = xq * lax.rsqrt(_group_mean(xq * xq, bd) + EPS) * qn_ref[...]
        yq = _rope(yq, cq, sq, hd // 4) * (hd ** -0.5)
        yqt = yq.T
        for hh in range(ATTN_Q_HEADS):
            q_out[hh] = yq[:, hh * hd:(hh + 1) * hd].astype(BF16)
            qt_out[hh] = yqt[hh * hd:(hh + 1) * hd, :].astype(BF16)
        yk = xk * lax.rsqrt(_group_mean(xk * xk, bd[:ATTN_KV_WIDTH, :ATTN_KV_WIDTH]) + EPS) * kn_ref[...]
        yk = _rope(yk, c2, s2, hd // 4)
        xvt = xvv.T
        ones = jnp.ones((hd, tm), F32)
        for hh in range(ATTN_KV_HEADS):
            k_out[hh] = yk[:, hh * hd:(hh + 1) * hd].astype(BF16)
            v_out[hh] = xvv[:, hh * hd:(hh + 1) * hd].astype(BF16)
            vt_out[hh, 0] = jnp.concatenate([xvt[hh * hd:(hh + 1) * hd, :], ones], axis=0).astype(BF16)

    const = lambda shape: pl.BlockSpec(shape, lambda i: (0,) * len(shape))
    rows = lambda w: pl.BlockSpec((tm, w), lambda i: (i, 0))
    return pl.pallas_call(
        body, name="in_proj", grid=(t // tm,),
        in_specs=[rows(d), const((1, d)), const((D_IN, d)), const((1, 512)), const((1, 128)), rows(128), rows(128),
                  const((512, 512))],
        out_specs=[rows(D_IN), pl.BlockSpec((d, tm), lambda i: (0, i)),
                   pl.BlockSpec((ATTN_Q_HEADS, tm, hd), lambda i: (0, i, 0)),
                   pl.BlockSpec((ATTN_Q_HEADS, hd, tm), lambda i: (0, 0, i)),
                   pl.BlockSpec((ATTN_KV_HEADS, tm, hd), lambda i: (0, i, 0)),
                   pl.BlockSpec((ATTN_KV_HEADS, tm, hd), lambda i: (0, i, 0)),
                   pl.BlockSpec((ATTN_KV_HEADS, 1, 2 * hd, tm), lambda i: (0, i // per_chunk, 0, i % per_chunk))],
        out_shape=[SDS((t, D_IN), F32), SDS((d, t), BF16),
                   SDS((ATTN_Q_HEADS, t, hd), BF16), SDS((ATTN_Q_HEADS, hd, t), BF16),
                   SDS((ATTN_KV_HEADS, t, hd), BF16), SDS((ATTN_KV_HEADS, t, hd), BF16),
                   SDS((ATTN_KV_HEADS, t // tk, 2 * hd, tk), BF16)],
        compiler_params=_params(("parallel",)),
    )(x, g, w_t, qn, kn, cos, sin, ones_bd)


def _attn_fwd(q, k, vt, ex=None):
    t = q.shape[1]
    tq = min(ATTN_FWD_QUERY_TILE, t)
    nk, tk = vt.shape[1], vt.shape[3]
    hd = ATTN_HEAD_DIM
    g = ATTN_Q_HEADS // ATTN_KV_HEADS

    def body(q_ref, k_ref, vt_ref, o_ref, lse_ref, s_scr):
        def pass_a(h, c, m8):
            part = tk // QK_DOTS_PER_CHUNK
            for lo in range(c * tk, (c + 1) * tk, part):
                st = _dot(k_ref[0, lo:lo + part, :], q_ref[h], NT)
                s_scr[h % 2, lo:lo + part, :] = st
                m8 = jnp.maximum(m8, jnp.max(st.reshape(part // 8, 8, tq), axis=0))
            return m8

        def pass_b(h, c, m, acc, after):
            e = jnp.exp(s_scr[h % 2, c * tk:(c + 1) * tk, :] - (m + after * 0.0)).astype(BF16)
            return acc + _dot(vt_ref[0, c], e)

        neg = jnp.full((8, tq), -jnp.inf, F32)
        m8 = neg
        for c in range(nk):
            m8 = pass_a(0, c, m8)
        outs = []
        for h in range(g):
            m = jnp.max(m8, axis=0, keepdims=True)
            acc = jnp.zeros((2 * hd, tq), F32)
            m8 = neg
            done = [m] * EXP_LAG
            for c in range(nk):
                if h + 1 < g:
                    m8 = pass_a(h + 1, c, m8)
                acc = pass_b(h, c, m, acc, done[-EXP_LAG])
                done.append(m8[0:1, :] if h + 1 < g else acc[hd:hd + 1, :])
            l = acc[hd:hd + 1, :]
            outs.append((acc[:hd, :] / l).T)
            lse_ref[h] = m + jnp.log(l)
        o_ref[...] = jnp.concatenate(outs, axis=-1)

    nq = t // tq
    first = lambda: jnp.logical_and(pl.program_id(0) == 0, pl.program_id(1) == 0)
    last = lambda: jnp.logical_and(pl.program_id(0) == ATTN_KV_HEADS - 1, pl.program_id(1) == nq - 1)
    xi, xo, xs, xscr, xargs = _ex_args(ex)
    return pl.pallas_call(
        _with_exchange(body, 3, 2, 1, ex, first, last), name="attn_fwd", grid=(ATTN_KV_HEADS, nq),
        in_specs=[pl.BlockSpec((g, tq, hd), lambda p, i: (p, i, 0)),
                  pl.BlockSpec((1, t, hd), lambda p, i: (p, 0, 0)),
                  pl.BlockSpec((1, nk, 2 * hd, tk), lambda p, i: (p, 0, 0, 0))] + xi,
        out_specs=[pl.BlockSpec((tq, g * hd), lambda p, i: (i, p)),
                   pl.BlockSpec((g, 1, tq), lambda p, i: (p, 0, i))] + xo,
        out_shape=[SDS((t, ATTN_WIDTH), F32), SDS((ATTN_Q_HEADS, 1, t), F32)] + xs,
        scratch_shapes=[pltpu.VMEM((2, t, tq), F32)] + xscr,
        compiler_params=_params(("arbitrary", "arbitrary")),
    )(q, k, vt, *xargs)


class _Dir:
    def __init__(self, lg, strict_future):
        c = RET_CHUNK
        ia = lax.broadcasted_iota(jnp.int32, (c, c), 0).astype(F32)
        ib = lax.broadcasted_iota(jnp.int32, (c, c), 1).astype(F32)
        col = lax.broadcasted_iota(jnp.int32, (c, 1), 0).astype(F32)
        row = lax.broadcasted_iota(jnp.int32, (1, c), 1).astype(F32)
        if strict_future:
            dist = ib - ia
            mask = dist > 0
            self.wq, self.wk, wk_row = c - col, col, row
        else:
            dist = ia - ib
            mask = dist >= 0
            self.wq, self.wk, wk_row = col + 1.0, c - 1.0 - col, c - 1.0 - row
        self.dist = jnp.maximum(dist, 0.0)
        self.d = jnp.where(mask, jnp.exp(self.dist * lg), 0.0)
        self.qd = jnp.exp(self.wq * lg)
        self.kd_col = jnp.exp(self.wk * lg)
        self.kd_row = jnp.exp(wk_row * lg)
        self.cd = jnp.exp(jnp.full((1, 1), float(c), F32) * lg)


def _ret_fwd(z, lgf, lgb, gnw, cos, sin):
    t = z.shape[0]
    c = RET_CHUNK
    nc = t // c
    hd = RET_HEAD_DIM
    unroll = 4 if nc % 4 == 0 else 1

    def body(lgf_ref, lgb_ref, q_ref, k_ref, v_ref, c_ref, s_ref, w_ref,
             qo_ref, ko_ref, vo_ref, orr_ref, on_ref, kt, uf, ub, sfa, sba):
        h = pl.program_id(0)
        fw = _Dir(lgf_ref[h], False)
        bw = _Dir(lgb_ref[h], True)
        cc, ss = c_ref[...], s_ref[...]
        qo_ref[...] = _rope(q_ref[...], cc, ss, hd // 4).astype(BF16)
        kr = _rope(k_ref[...], cc, ss, hd // 4) * (hd ** -0.5)
        ko_ref[...] = kr.astype(BF16)
        vo_ref[...] = v_ref[...].astype(BF16)
        for i in range(nc):
            kt[i] = kr[i * c:(i + 1) * c, :].T.astype(BF16)

        def rows(ci):
            return pl.ds(pl.multiple_of(ci * c, c), c)

        def kv_products(ci, carry):
            vv = vo_ref[rows(ci), :]
            ktf = kt[ci].astype(F32)
            uf[ci] = _dot((ktf * fw.kd_row).astype(BF16), vv)
            ub[ci] = _dot((ktf * bw.kd_row).astype(BF16), vv)
            return carry

        lax.fori_loop(0, nc, kv_products, 0, unroll=unroll)

        def scan(i, carry):
            sf, sb = carry
            j = nc - 1 - i
            sfa[i] = sf.astype(BF16)
            sba[j] = sb.astype(BF16)
            return sf * fw.cd + uf[i], sb * bw.cd + ub[j]

        zero = jnp.zeros((hd, hd), F32)
        lax.fori_loop(0, nc, scan, (zero, zero))
        gw = w_ref[...]

        def outputs(ci, carry):
            sl = rows(ci)
            qq, kk, vv = qo_ref[sl, :], ko_ref[sl, :], vo_ref[sl, :]
            a = _dot(qq, kk, NT)
            o = (_dot((a * fw.d).astype(BF16), vv) + _dot(qq, sfa[ci]) * fw.qd
                 + _dot((a * bw.d).astype(BF16), vv) + _dot(qq, sba[ci]) * bw.qd)
            orr_ref[sl, :] = o
            xc = o - jnp.mean(o, axis=-1, keepdims=True)
            var = jnp.mean(xc * xc, axis=-1, keepdims=True)
            on_ref[sl, :] = xc * lax.rsqrt(var + EPS) * gw
            return carry

        lax.fori_loop(0, nc, outputs, 0, unroll=unroll)

    smem = pl.BlockSpec(memory_space=pltpu.SMEM)
    col = lambda name: (lambda h: (0, SEG[name][2] // 128 + h))
    head = pl.BlockSpec((t, 128), lambda h: (0, h))
    full = pl.BlockSpec((t, 128), lambda h: (0, 0))
    return pl.pallas_call(
        body, name="ret_fwd", grid=(RET_HEADS,),
        in_specs=[smem, smem, pl.BlockSpec((t, 128), col("qr")), pl.BlockSpec((t, 128), col("kr")),
                  pl.BlockSpec((t, 128), col("vr")), full, full, pl.BlockSpec((1, 128), lambda h: (0, h))],
        out_specs=[head, head, head, head, head],
        out_shape=[SDS((t, RET_WIDTH), BF16)] * 3 + [SDS((t, RET_WIDTH), F32)] * 2,
        scratch_shapes=[pltpu.VMEM((nc, hd, c), BF16), pltpu.VMEM((nc, hd, hd), F32), pltpu.VMEM((nc, hd, hd), F32),
                        pltpu.VMEM((nc, hd, hd), BF16), pltpu.VMEM((nc, hd, hd), BF16)],
        compiler_params=_params(("parallel",)),
    )(lgf, lgb, z, z, z, cos, sin, gnw)


def _merge_fwd(x, z, oa, on, wb_t, wout):
    t, d = x.shape
    tm = min(256, t)

    def body(x_ref, ga_ref, gr_ref, gm0_ref, gm1_ref, oa_ref, on_ref, wb_ref, wo_ref, xn_ref, ya_ref, yb_ref):
        ga, gr = ga_ref[...], gr_ref[...]
        ua = ga * _sigmoid(ga) * oa_ref[...]
        ub = gr * _sigmoid(gr) * on_ref[...]
        ya = _dot(ua.astype(BF16), wb_ref[:, :512], NT)
        yb = _dot(ub.astype(BF16), wb_ref[:, 512:], NT)
        ya_ref[...] = ya
        yb_ref[...] = yb
        merged = _sigmoid(gm0_ref[...]) * ya + _sigmoid(gm1_ref[...]) * yb
        xn_ref[...] = x_ref[...] + _dot(merged.astype(BF16), wo_ref[...])

    row = lambda w, j: pl.BlockSpec((tm, w), lambda i: (i, j))
    const = lambda shape: pl.BlockSpec(shape, lambda i: (0, 0))
    return pl.pallas_call(
        body, name="merge_fwd", grid=(t // tm,),
        in_specs=[row(d, 0), row(512, SEG["ga"][2] // 512), row(512, SEG["gr"][2] // 512),
                  row(1024, SEG["gm"][2] // 1024), row(1024, SEG["gm"][2] // 1024 + 1),
                  row(512, 0), row(512, 0), const((d, 1024)), const((d, d))],
        out_specs=[row(d, 0), row(d, 0), row(d, 0)],
        out_shape=[SDS((t, d), F32)] * 3,
        compiler_params=_params(("parallel",)),
    )(x, z, z, z, z, oa, on, wb_t, wout)


def _final_loss(x, g, target):
    t, d = x.shape
    tm = min(512, t)
    n = t // tm

    def body(x_ref, g_ref, t_ref, dx_ref, dg_ref, loss_ref, acc_g, acc_l):
        i = pl.program_id(0)

        @pl.when(i == 0)
        def _():
            acc_g[...] = jnp.zeros_like(acc_g)
            acc_l[...] = jnp.zeros_like(acc_l)

        xv, gv = x_ref[...], g_ref[...]
        r = lax.rsqrt(jnp.mean(xv * xv, axis=-1, keepdims=True) + EPS)
        xh = xv * r
        err = xh * gv - t_ref[...]
        dy = err * (1.0 / d)
        gy = dy * gv
        dx_ref[...] = r * (gy - xh * jnp.mean(gy * xh, axis=-1, keepdims=True))
        acc_g[...] += jnp.sum((dy * xh).reshape(tm // 8, 8, d), axis=0)
        acc_l[...] += jnp.sum((err * err).reshape(tm // 8, 8, d), axis=0)

        @pl.when(i == n - 1)
        def _():
            dg_ref[...] = jnp.sum(acc_g[...], axis=0, keepdims=True)
            tot = jnp.sum(jnp.sum(acc_l[...], axis=0, keepdims=True), axis=1, keepdims=True)
            loss_ref[...] = jnp.broadcast_to(tot * (0.5 / d), (1, 128))

    return pl.pallas_call(
        body, name="final_loss", grid=(n,),
        in_specs=[pl.BlockSpec((tm, d), lambda i: (i, 0)), pl.BlockSpec((1, d), lambda i: (0, 0)),
                  pl.BlockSpec((tm, d), lambda i: (i, 0))],
        out_specs=[pl.BlockSpec((tm, d), lambda i: (i, 0)), pl.BlockSpec((1, d), lambda i: (0, 0)),
                   pl.BlockSpec((1, 128), lambda i: (0, 0))],
        out_shape=[SDS((t, d), F32), SDS((1, d), F32), SDS((1, 128), F32)],
        scratch_shapes=[pltpu.VMEM((8, d), F32), pltpu.VMEM((8, d), F32)],
        compiler_params=_params(("arbitrary",)),
    )(x, g, target)


def _merge_bwd(dxo, z, oa, on, ya, yb, wb_t, wout):
    t, d = dxo.shape
    tm = min(256, t)
    n = t // tm

    def body(dx_ref, ga_ref, gr_ref, gm0_ref, gm1_ref, oa_ref, on_ref, ya_ref, yb_ref, wb_ref, wo_ref,
             doa_ref, don_ref, dz_ref, dwo_ref, dwb_ref, acc_o, acc_b):
        i = pl.program_id(0)

        @pl.when(i == 0)
        def _():
            acc_o[...] = jnp.zeros_like(acc_o)
            acc_b[...] = jnp.zeros_like(acc_b)

        dxb = dx_ref[...].astype(BF16)
        ya, yb = ya_ref[...], yb_ref[...]
        g0, g1 = _sigmoid(gm0_ref[...]), _sigmoid(gm1_ref[...])
        mb = (g0 * ya + g1 * yb).astype(BF16)
        dm = _dot(dxb, wo_ref[...], NT)
        dya = (dm * g0).astype(BF16)
        dyb = (dm * g1).astype(BF16)
        dz_ref[:, 1024:2048] = (dm * ya * g0 * (1.0 - g0)).astype(BF16)
        dz_ref[:, 2048:3072] = (dm * yb * g1 * (1.0 - g1)).astype(BF16)

        def branch(g_ref, o_ref, dy, w, do_ref, lo):
            gv, ov = g_ref[...], o_ref[...]
            sg = _sigmoid(gv)
            silu = gv * sg
            du = _dot(dy, w)
            do_ref[...] = du * silu
            dz_ref[:, lo:lo + 512] = (du * ov * (sg * (1.0 + gv * (1.0 - sg)))).astype(BF16)
            acc_b[:, lo:lo + 512] += _dot(dy, (silu * ov).astype(BF16), TN)

        branch(ga_ref, oa_ref, dya, wb_ref[:, :512], doa_ref, 0)
        branch(gr_ref, on_ref, dyb, wb_ref[:, 512:], don_ref, 512)
        acc_o[...] += _dot(mb, dxb, TN)

        @pl.when(i == n - 1)
        def _():
            dwo_ref[...] = acc_o[...].astype(BF16)
            dwb_ref[...] = acc_b[...].astype(BF16)

    row = lambda w, j: pl.BlockSpec((tm, w), lambda i: (i, j))
    const = lambda shape: pl.BlockSpec(shape, lambda i: (0, 0))
    return pl.pallas_call(
        body, name="merge_bwd", grid=(n,),
        in_specs=[row(d, 0), row(512, SEG["ga"][2] // 512), row(512, SEG["gr"][2] // 512),
                  row(1024, SEG["gm"][2] // 1024), row(1024, SEG["gm"][2] // 1024 + 1),
                  row(512, 0), row(512, 0), row(d, 0), row(d, 0), const((d, 1024)), const((d, d))],
        out_specs=[row(512, 0), row(512, 0), row(3072, 0), const((d, d)), const((d, 1024))],
        out_shape=[SDS((t, 512), F32), SDS((t, 512), F32), SDS((t, 3072), BF16), SDS((d, d), BF16),
                   SDS((d, 1024), BF16)],
        scratch_shapes=[pltpu.VMEM((d, d), F32), pltpu.VMEM((d, 1024), F32)],
        compiler_params=_params(("arbitrary",)),
    )(dxo, z, z, z, z, oa, on, ya, yb, wb_t, wout)


def _ret_bwd(qrot, krot, vb, orr, don, gnw, lgf, lgb):
    t = qrot.shape[0]
    c = RET_CHUNK
    nc = t // c
    hd = RET_HEAD_DIM
    unroll = 4 if nc % 4 == 0 else 1

    def body(lgf_ref, lgb_ref, q_ref, k_ref, v_ref, o_ref, dn_ref, w_ref,
             dq_ref, dk_ref, dv_ref, dw_ref, dlf_ref, dlb_ref, qt, kt, dob, uf, ub, wf, wb, sfa, sba, gfa, gba):
        h = pl.program_id(0)
        fw = _Dir(lgf_ref[h], False)
        bw = _Dir(lgb_ref[h], True)
        fw.dt, bw.dt = fw.d.T, bw.d.T

        o = o_ref[...]
        xc = o - jnp.mean(o, axis=-1, keepdims=True)
        r = lax.rsqrt(jnp.mean(xc * xc, axis=-1, keepdims=True) + EPS)
        xh = xc * r
        dn = dn_ref[...]
        gy = dn * w_ref[...]
        d_o = r * (gy - jnp.mean(gy, axis=-1, keepdims=True) - xh * jnp.mean(gy * xh, axis=-1, keepdims=True))
        dw_ref[...] = jnp.sum(dn * xh, axis=0, keepdims=True)
        dob[...] = d_o.astype(BF16)
        for i in range(nc):
            qt[i] = q_ref[i * c:(i + 1) * c, :].astype(F32).T.astype(BF16)
            kt[i] = k_ref[i * c:(i + 1) * c, :].astype(F32).T.astype(BF16)

        def rows(ci):
            return pl.ds(pl.multiple_of(ci * c, c), c)

        def products(ci, carry):
            sl = rows(ci)
            vv, do32 = v_ref[sl, :], dob[sl, :].astype(F32)
            ktf = kt[ci].astype(F32)
            uf[ci] = _dot((ktf * fw.kd_row).astype(BF16), vv)
            ub[ci] = _dot((ktf * bw.kd_row).astype(BF16), vv)
            wf[ci] = _dot(qt[ci], (do32 * fw.qd).astype(BF16))
            wb[ci] = _dot(qt[ci], (do32 * bw.qd).astype(BF16))
            return carry

        lax.fori_loop(0, nc, products, 0, unroll=unroll)

        def scan(i, carry):
            sf, sb, gf, gb = carry
            j = nc - 1 - i
            sfa[i] = sf.astype(BF16)
            sba[j] = sb.astype(BF16)
            gfa[j] = gf.astype(BF16)
            gba[i] = gb.astype(BF16)
            return sf * fw.cd + uf[i], sb * bw.cd + ub[j], gf * fw.cd + wf[j], gb * bw.cd + wb[i]

        zero = jnp.zeros((hd, hd), F32)
        lax.fori_loop(0, nc, scan, (zero, zero, zero, zero))

        def one_dir(p, s_all, g_all, ci, qq, kk, vv, do, a, bm):
            sb, gb = s_all[ci], g_all[ci]
            doq = (do.astype(F32) * p.qd).astype(BF16)
            dqc = _dot(doq, sb, NT)
            kkd = (kk.astype(F32) * p.kd_col).astype(BF16)
            dk2 = _dot(vv, gb, NT) * p.kd_col
            terms = (p.dist * p.d * a * bm + p.wq * qq.astype(F32) * dqc + p.wk * kk.astype(F32) * dk2
                     + (float(c) * p.cd) * gb.astype(F32) * sb.astype(F32))
            return dqc, dk2, _dot(kkd, gb), terms

        d_both, dt_both = fw.d + bw.d, fw.dt + bw.dt

        def chunk(ci, carry):
            af, ab = carry
            sl = rows(ci)
            qq, kk, vv, do = q_ref[sl, :], k_ref[sl, :], v_ref[sl, :], dob[sl, :]
            a, bm = _dot(qq, kk, NT), _dot(do, vv, NT)
            at, bt = _dot(kk, qq, NT), _dot(vv, do, NT)
            dqf, dkf, dvf, tf = one_dir(fw, sfa, gfa, ci, qq, kk, vv, do, a, bm)
            dqb, dkb, dvb, tb = one_dir(bw, sba, gba, ci, qq, kk, vv, do, a, bm)
            dq_ref[sl, :] = _dot((bm * d_both).astype(BF16), kk) + dqf + dqb
            dk_ref[sl, :] = _dot((bt * dt_both).astype(BF16), qq) + dkf + dkb
            dv_ref[sl, :] = _dot((at * dt_both).astype(BF16), do) + dvf + dvb
            return af + tf, ab + tb

        pair = 4 if nc % 4 == 0 else 1

        def chunks(i, carry):
            for j in range(pair):
                carry = chunk(i * pair + j, carry)
            return carry

        af, ab = lax.fori_loop(0, nc // pair, chunks, (zero, zero))
        tot = lambda m: jnp.sum(jnp.sum(m, axis=0, keepdims=True), axis=1, keepdims=True)
        dlf_ref[...] = jnp.broadcast_to(tot(af).reshape(1, 1, 1), (1, 8, 128))
        dlb_ref[...] = jnp.broadcast_to(tot(ab).reshape(1, 1, 1), (1, 8, 128))

    smem = pl.BlockSpec(memory_space=pltpu.SMEM)
    head = pl.BlockSpec((t, 128), lambda h: (0, h))
    vec = pl.BlockSpec((1, 128), lambda h: (0, h))
    scal = pl.BlockSpec((1, 8, 128), lambda h: (h, 0, 0))
    mats = lambda dt: pltpu.VMEM((nc, hd, hd), dt)
    return pl.pallas_call(
        body, name="ret_bwd", grid=(RET_HEADS,),
        in_specs=[smem, smem, head, head, head, head, head, vec],
        out_specs=[head, head, head, vec, scal, scal],
        out_shape=[SDS((t, RET_WIDTH), F32)] * 3 + [SDS((1, RET_WIDTH), F32), SDS((RET_HEADS, 8, 128), F32),
                                                   SDS((RET_HEADS, 8, 128), F32)],
        scratch_shapes=[pltpu.VMEM((nc, hd, c), BF16), pltpu.VMEM((nc, hd, c), BF16), pltpu.VMEM((t, hd), BF16),
                        mats(F32), mats(F32), mats(F32), mats(F32), mats(BF16), mats(BF16), mats(BF16), mats(BF16)],
        compiler_params=_params(("parallel",)),
    )(lgf, lgb, qrot, krot, vb, orr, don, gnw)


def _ret_post_bwd(dq, dk, dv, cos, sin):
    t = dq.shape[0]
    tm = min(512, t)
    hd = RET_HEAD_DIM

    def body(dq_ref, dk_ref, dv_ref, c_ref, s_ref, oq_ref, ok_ref, ov_ref):
        cc = jnp.concatenate([c_ref[...]] * 4, axis=-1)
        ss = jnp.concatenate([s_ref[...]] * 4, axis=-1)
        oq_ref[...] = _rope_bwd(dq_ref[...], cc, ss, hd // 4).astype(BF16)
        ok_ref[...] = (_rope_bwd(dk_ref[...], cc, ss, hd // 4) * (hd ** -0.5)).astype(BF16)
        ov_ref[...] = dv_ref[...].astype(BF16)

    blk = pl.BlockSpec((tm, 512), lambda i: (i, 0))
    tab = pl.BlockSpec((tm, 128), lambda i: (i, 0))
    return pl.pallas_call(
        body, name="ret_post_bwd", grid=(t // tm,),
        in_specs=[blk, blk, blk, tab, tab], out_specs=[blk, blk, blk],
        out_shape=[SDS((t, 512), BF16)] * 3,
        compiler_params=_params(("parallel",)),
    )(dq, dk, dv, cos, sin)


def _attn_bwd(q, qt, k, v, doa, oa, lse, ex=None):
    t = q.shape[1]
    tq = min(ATTN_BWD_QUERY_TILE, t)
    nq = t // tq
    tk = min(ATTN_BWD_KEY_CHUNK, t)
    nk = t // tk
    hd = ATTN_HEAD_DIM
    scale = hd ** -0.5

    def body(q_ref, qt_ref, k_ref, v_ref, do_ref, o_ref, lse_ref, dq_ref, dkt_ref, dvt_ref):
        p, i = pl.program_id(0), pl.program_id(1)

        @pl.when(jnp.logical_and(p % 2 == 0, i == 0))
        def _():
            dkt_ref[...] = jnp.zeros_like(dkt_ref)
            dvt_ref[...] = jnp.zeros_like(dvt_ref)

        dov, ov = do_ref[...], o_ref[...]
        dovt = dov.T
        lanes = lambda col: jnp.concatenate([col] * (tk // 128), axis=1)
        outs = []
        for j in range(2):
            qq, qqt = q_ref[j], qt_ref[j]
            do32 = dov[:, j * hd:(j + 1) * hd]
            do, dot_ = do32.astype(BF16), dovt[j * hd:(j + 1) * hd, :].astype(BF16)
            dd = lanes(jnp.broadcast_to(jnp.sum(do32 * ov[:, j * hd:(j + 1) * hd], axis=1, keepdims=True), (tq, 128)))
            lse_j = lanes(jnp.broadcast_to(lse_ref[j], (128, tq)).T)
            dq = jnp.zeros((tq, hd), F32)
            for c in range(nk):
                sl = slice(c * tk, (c + 1) * tk)
                kc, vc = k_ref[0, sl, :], v_ref[0, sl, :]
                pr = jnp.exp(_dot(qq, kc, NT) - lse_j)
                ds = (pr * (_dot(do, vc, NT) - dd)).astype(BF16)
                dvt_ref[0, :, sl] += _dot(dot_, pr.astype(BF16))
                dkt_ref[0, :, sl] += _dot(qqt, ds)
                dq = dq + _dot(ds, kc)
            outs.append(dq * scale)
        dq_ref[...] = jnp.concatenate(outs, axis=-1)

    kv = pl.BlockSpec((1, t, hd), lambda p, i: (p // 2, 0, 0))
    kvt = pl.BlockSpec((1, hd, t), lambda p, i: (p // 2, 0, 0))
    pair = pl.BlockSpec((tq, 128), lambda p, i: (i, p))
    first = lambda: jnp.logical_and(pl.program_id(0) == 0, pl.program_id(1) == 0)
    last = lambda: jnp.logical_and(pl.program_id(0) == 3, pl.program_id(1) == nq - 1)
    xi, xo, xs, xscr, xargs = _ex_args(ex)
    return pl.pallas_call(
        _with_exchange(body, 7, 3, 0, ex, first, last), name="attn_bwd", grid=(4, nq),
        in_specs=[pl.BlockSpec((2, tq, hd), lambda p, i: (p, i, 0)), pl.BlockSpec((2, hd, tq), lambda p, i: (p, 0, i)),
                  kv, kv, pair, pair, pl.BlockSpec((2, 1, tq), lambda p, i: (p, 0, i))] + xi,
        out_specs=[pair, kvt, kvt] + xo,
        out_shape=[SDS((t, ATTN_WIDTH), F32), SDS((ATTN_KV_HEADS, hd, t), F32),
                   SDS((ATTN_KV_HEADS, hd, t), F32)] + xs,
        scratch_shapes=xscr,
        compiler_params=_params(("arbitrary", "arbitrary")),
    )(q, qt, k, v, doa, oa, lse, *xargs)


def _attn_post_bwd(dq, dk, dv, z, qn, kn, cos, sin, ones_bd):
    t = z.shape[0]
    tm = min(512, t)
    n = t // tm
    hd = ATTN_HEAD_DIM

    def body(dq_ref, dk_ref, dv_ref, zq_ref, zkv_ref, qn_ref, kn_ref, c_ref, s_ref, b_ref,
             dz_ref, dqn_ref, dkn_ref, acc_q, acc_k):
        i = pl.program_id(0)

        @pl.when(i == 0)
        def _():
            acc_q[...] = jnp.zeros_like(acc_q)
            acc_k[...] = jnp.zeros_like(acc_k)

        bd = b_ref[...]
        c2, s2 = c_ref[...], s_ref[...]

        def norm_bwd(dy, x, w, ones, cos_t, sin_t, acc):
            dyr = _rope_bwd(dy, cos_t, sin_t, hd // 4)
            r = lax.rsqrt(_group_mean(x * x, ones) + EPS)
            xh = x * r
            gy = dyr * w
            acc[...] += jnp.sum((dyr * xh).reshape(tm // 8, 8, x.shape[-1]), axis=0)
            return r * (gy - xh * _group_mean(gy * xh, ones))

        cq = jnp.concatenate([c2] * 4, axis=-1)
        sq = jnp.concatenate([s2] * 4, axis=-1)
        dz_ref[:, :512] = norm_bwd(dq_ref[...], zq_ref[...], qn_ref[...], bd, cq, sq, acc_q).astype(BF16)
        zkv = zkv_ref[...]
        dkk = jnp.concatenate([dk_ref[0], dk_ref[1]], axis=0).T
        dz_ref[:, 512:640] = norm_bwd(dkk, zkv[:, :128], kn_ref[...], bd[:128, :128], c2, s2, acc_k).astype(BF16)
        dz_ref[:, 640:768] = jnp.concatenate([dv_ref[0], dv_ref[1]], axis=0).T.astype(BF16)

        @pl.when(i == n - 1)
        def _():
            dqn_ref[...] = jnp.sum(acc_q[...], axis=0, keepdims=True)
            dkn_ref[...] = jnp.sum(acc_k[...], axis=0, keepdims=True)

    kv_blk = SEG["ka"][2] // 256
    kvs = pl.BlockSpec((ATTN_KV_HEADS, hd, tm), lambda i: (0, 0, i))
    const = lambda shape: pl.BlockSpec(shape, lambda i: (0, 0))
    return pl.pallas_call(
        body, name="attn_post_bwd", grid=(n,),
        in_specs=[pl.BlockSpec((tm, 512), lambda i: (i, 0)), kvs, kvs,
                  pl.BlockSpec((tm, 512), lambda i: (i, 0)), pl.BlockSpec((tm, 256), lambda i: (i, kv_blk)),
                  const((1, 512)), const((1, 128)),
                  pl.BlockSpec((tm, 128), lambda i: (i, 0)), pl.BlockSpec((tm, 128), lambda i: (i, 0)),
                  const((512, 512))],
        out_specs=[pl.BlockSpec((tm, 768), lambda i: (i, 0)), const((1, 512)), const((1, 128))],
        out_shape=[SDS((t, 768), BF16), SDS((1, 512), F32), SDS((1, 128), F32)],
        scratch_shapes=[pltpu.VMEM((8, 512), F32), pltpu.VMEM((8, 128), F32)],
        compiler_params=_params(("arbitrary",)),
    )(dq, dk, dv, z, z, qn, kn, cos, sin, ones_bd)


def _in_bwd(dxo, x, g, w_t, dz_a, dz_m, dqr, dkr, dvr, after=None):
    t, d = x.shape
    tm = min(256, t)
    n = t // tm
    parts = [(0, 0, 768, 0), (1, 0, 512, SEG["ga"][0]), (2, 0, 512, SEG["qr"][0]), (3, 0, 512, SEG["kr"][0]),
             (4, 0, 512, SEG["vr"][0]), (1, 512, 2560, SEG["gr"][0])]

    def body(dx_ref, x_ref, g_ref, w_ref, a_ref, m_ref, q_ref, k_ref, v_ref, o_ref, dg_ref, acc):
        i = pl.program_id(0)

        @pl.when(i == 0)
        def _():
            acc[...] = jnp.zeros_like(acc)

        pieces = [a_ref, m_ref, q_ref, k_ref, v_ref]
        dh = jnp.zeros((tm, d), F32)
        for pi, lo, w, row in parts:
            dh = dh + _dot(pieces[pi][:, lo:lo + w], w_ref[row:row + w, :])
        xv = x_ref[...]
        r = lax.rsqrt(jnp.mean(xv * xv, axis=-1, keepdims=True) + EPS)
        xh = xv * r
        gy = dh * g_ref[...]
        o_ref[...] = dx_ref[...] + r * (gy - xh * jnp.mean(gy * xh, axis=-1, keepdims=True))
        acc[...] += jnp.sum((dh * xh).reshape(tm // 8, 8, d), axis=0)

        @pl.when(i == n - 1)
        def _():
            dg_ref[...] = jnp.sum(acc[...], axis=0, keepdims=True)

    row = lambda w: pl.BlockSpec((tm, w), lambda i: (i, 0))
    const = lambda shape: pl.BlockSpec(shape, lambda i: (0, 0))
    extra = [] if after is None else [after]
    return pl.pallas_call(
        (lambda *refs: body(*refs[:9], *refs[9 + len(extra):])), name="in_bwd", grid=(n,),
        in_specs=[row(d), row(d), const((1, d)), const((D_IN, d)), row(768), row(3072), row(512), row(512),
                  row(512)] + [const(a.shape) for a in extra],
        out_specs=[row(d), const((1, d))],
        out_shape=[SDS((t, d), F32), SDS((1, d), F32)],
        scratch_shapes=[pltpu.VMEM((8, d), F32)],
        compiler_params=_params(("arbitrary",)),
    )(dxo, x, g, w_t, dz_a, dz_m, dqr, dkr, dvr, *extra)


def _dw_in(h_t, dz_a, dz_m, dqr, dkr, dvr):
    d, t = h_t.shape
    tn = 256
    parts = [(0, 0, 0, 3), (1, 0, SEG["ga"][0] // tn, 2), (2, 0, SEG["qr"][0] // tn, 2),
             (3, 0, SEG["kr"][0] // tn, 2), (4, 0, SEG["vr"][0] // tn, 2), (1, 2, SEG["gr"][0] // tn, 10)]
    pieces = [dz_a, dz_m, dqr, dkr, dvr]

    def col_block(pi):
        mine = [(c0, r0, n) for q, c0, r0, n in parts if q == pi]

        def index(j):
            c0, r0, n = mine[0]
            blk = c0 + jnp.clip(j - r0, 0, n - 1)
            for c0, r0, n in mine[1:]:
                blk = jnp.where(j >= r0, c0 + jnp.clip(j - r0, 0, n - 1), blk)
            return 0, blk

        return index

    def body(h_ref, *refs):
        o_ref = refs[-1]
        j = pl.program_id(0)
        for pi, _, r0, n in parts:
            @pl.when(jnp.logical_and(j >= r0, j < r0 + n))
            def _(p_ref=refs[pi]):
                o_ref[...] = _dot(h_ref[...], p_ref[...]).T.astype(BF16)

    return pl.pallas_call(
        body, name="dw_in", grid=(D_IN // tn,),
        in_specs=[pl.BlockSpec((d, t), lambda j: (0, 0))] + [pl.BlockSpec((t, tn), col_block(pi)) for pi in range(5)],
        out_specs=pl.BlockSpec((tn, d), lambda j: (j, 0)),
        out_shape=SDS((D_IN, d), BF16),
        compiler_params=_params(("arbitrary",)),
    )(h_t, *pieces)


def _adamw_math(w, g, m, v):
    mn = ADAM_B1 * m + (1.0 - ADAM_B1) * g
    vn = ADAM_B2 * v + (1.0 - ADAM_B2) * (g * g)
    m_hat = mn / (1.0 - ADAM_B1 ** ADAM_STEP)
    v_hat = vn / (1.0 - ADAM_B2 ** ADAM_STEP)
    return -ADAM_LR * (m_hat / (jnp.sqrt(v_hat) + ADAM_EPS) + ADAM_WD * w), mn, vn


def _sum_adamw(recvs, w, m, v, lane0, tn, layer0=0, prev=None, own=None):
    _, r, c = w.shape
    j0 = lane0 // tn
    n = len(recvs)
    has_own = own is not None

    def body(*refs):
        mine_ref, refs = (refs[0], refs[1:]) if has_own else (None, refs)
        w_ref, m_ref, v_ref = refs[n:n + 3]
        g_ref, d_ref, mo_ref, vo_ref = refs[-4:]

        def run(r_ref):
            def slot(s):
                if has_own:
                    return jnp.where(mine_ref[0] == s, refs[n + 3][...], r_ref[s]).astype(F32)
                return r_ref[s].astype(F32)

            g = slot(0)
            for s in range(1, N_DEV):
                g = g + slot(s)
            g_ref[0] = g
            d_ref[0], mo_ref[0], vo_ref[0] = _adamw_math(w_ref[0], g, m_ref[0], v_ref[0])

        for i in range(n):
            pl.when(pl.program_id(0) == i)(functools.partial(run, refs[i]))

    slots = pl.BlockSpec((N_DEV, r, tn), lambda i, j, *_: (0, 0, j0 + j))
    blk = pl.BlockSpec((1, r, tn), lambda i, j, *_: (layer0 + i, 0, j))
    before = [] if prev is None else list(prev)
    in_specs, args = [slots] * n + [blk] * 3, [*recvs, w, m, v]
    if has_own:
        assert n == 1
        in_specs.append(pl.BlockSpec((r, tn), lambda i, j, mine: (mine[0], j0 + j)))
        args.append(own[0])
    n_pre = len(args) + has_own
    return pl.pallas_call(
        body, name="sum_adamw",
        grid_spec=pltpu.PrefetchScalarGridSpec(
            num_scalar_prefetch=int(has_own), grid=(n, c // tn),
            in_specs=in_specs + [ANY] * len(before), out_specs=[blk] * 4),
        out_shape=[SDS(w.shape, F32)] * 4,
        input_output_aliases={n_pre + k: k for k in range(len(before))},
        compiler_params=_params(("parallel", "parallel")),
    )(*([own[1]] if has_own else []), *args, *before)


def _adamw(w, g, m, v):
    rows, cols = w.shape
    tr = 256 if rows % 256 == 0 else rows

    def body(w_ref, g_ref, m_ref, v_ref, d_ref, mo_ref, vo_ref):
        d_ref[...], mo_ref[...], vo_ref[...] = _adamw_math(w_ref[...], g_ref[...], m_ref[...], v_ref[...])

    blk = pl.BlockSpec((tr, cols), lambda i: (i, 0))
    return pl.pallas_call(
        body, name="adamw", grid=(rows // tr,),
        in_specs=[blk] * 4, out_specs=[blk] * 3, out_shape=[SDS((rows, cols), F32)] * 3,
        compiler_params=_params(("parallel",)),
    )(w, g, m, v)


def _all_gather(shards):
    na = len(shards)
    chips = (4, 2, 6)

    def body(*refs):
        ins, outs = refs[:na], refs[na:2 * na]
        send_sems, recv_sems, local_sems = refs[2 * na:]
        _, mine = _flip(0)

        def rows(a, idx):
            r = shards[a].shape[0]
            return outs[a].at[pl.ds(pl.multiple_of(idx * r, 16), r), :]

        def copy(a, slot, block_idx, to, src=None):
            return pltpu.make_async_remote_copy(
                src_ref=rows(a, block_idx) if src is None else src, dst_ref=rows(a, block_idx),
                send_sem=send_sems.at[a, slot], recv_sem=recv_sems.at[a, slot],
                device_id=to, device_id_type=MESH_ID)

        sibling, sibling_idx = _flip(1)
        local, started = [], []
        for a in range(na):
            cp = pltpu.make_async_copy(ins[a], rows(a, mine), local_sems.at[a])
            cp.start()
            local.append(cp)
            first = [copy(a, 0, mine, sibling, src=ins[a])]
            first += [copy(a, 1 + j, mine, _flip(k)[0], src=ins[a]) for j, k in enumerate(chips)]
            for cp in first:
                cp.start()
            started += first
        for a in range(na):
            for j, k in enumerate(chips):
                _, theirs = _flip(k)
                copy(a, 1 + j, theirs, _flip(0)[0]).wait_recv()
                fwd = copy(a, 4 + j, theirs, sibling)
                fwd.start()
                started.append(fwd)
        for a in range(na):
            copy(a, 0, sibling_idx, _flip(0)[0]).wait_recv()
            for j, k in enumerate(chips):
                _, theirs = _flip(k | 1)
                copy(a, 4 + j, theirs, _flip(0)[0]).wait_recv()
        for cp in started:
            cp.wait_send()
        for cp in local:
            cp.wait()

    return pl.pallas_call(
        body, name="all_gather_weights",
        in_specs=[ANY] * na, out_specs=[ANY] * na,
        out_shape=[SDS((N_DEV * s.shape[0], s.shape[1]), s.dtype) for s in shards],
        scratch_shapes=[pltpu.SemaphoreType.DMA((na, 7)), pltpu.SemaphoreType.DMA((na, 7)),
                        pltpu.SemaphoreType.DMA((na,))],
        compiler_params=pltpu.CompilerParams(has_side_effects=True),
    )(*shards)


def _scatter_blocks_of(g_ref, rows, idx):
    return g_ref.at[pl.ds(pl.multiple_of(idx * rows, 16), rows), :]


def _scatter_start(g):
    rows = g.shape[0] // N_DEV
    land_shape = (N_DEV, rows, g.shape[1])

    def body(g_ref, land_ref, send_sems, recv_sems, g_thru, land_thru, token):
        _, mine = _flip(0)
        for k in range(1, N_DEV):
            peer, theirs = _flip(k)
            pltpu.make_async_remote_copy(
                src_ref=_scatter_blocks_of(g_ref, rows, theirs), dst_ref=land_ref.at[mine],
                send_sem=send_sems.at[k - 1], recv_sem=recv_sems.at[k - 1],
                device_id=peer, device_id_type=MESH_ID).start()
        token[...] = jnp.zeros_like(token)

    hbm, sem = pl.BlockSpec(memory_space=pltpu.HBM), pl.BlockSpec(memory_space=pltpu.SEMAPHORE)
    return pl.pallas_call(
        body, name="scatter_start",
        out_shape=(pltpu.SemaphoreType.DMA((N_DEV - 1,)), pltpu.SemaphoreType.DMA((N_DEV - 1,)),
                   pltpu.HBM(g.shape, g.dtype), pltpu.HBM(land_shape, g.dtype), SDS((8, 128), F32)),
        in_specs=(hbm, hbm), out_specs=(sem, sem, hbm, hbm, pl.BlockSpec(memory_space=pltpu.VMEM)),
        input_output_aliases={0: 2, 1: 3},
        compiler_params=pltpu.CompilerParams(has_side_effects=pltpu.SideEffectType.DATAFLOW_SIDE_EFFECTING),
    )(pltpu.with_memory_space_constraint(g, pltpu.HBM),
      pltpu.with_memory_space_constraint(lax.empty(land_shape, g.dtype), pltpu.HBM))


def _scatter_wait(send_sems, recv_sems, g_thru, land_thru, after):
    rows = g_thru.shape[0] // N_DEV

    def body(g_ref, land_ref, send_sems, recv_sems, *rest):
        me, _ = _flip(0)
        for k in range(1, N_DEV):
            _, theirs = _flip(k)
            copy = pltpu.make_async_remote_copy(
                src_ref=_scatter_blocks_of(g_ref, rows, theirs), dst_ref=land_ref.at[theirs],
                send_sem=send_sems.at[k - 1], recv_sem=recv_sems.at[k - 1],
                device_id=me, device_id_type=MESH_ID)
            copy.wait_send()
            copy.wait_recv()

    hbm, sem = pl.BlockSpec(memory_space=pltpu.HBM), pl.BlockSpec(memory_space=pltpu.SEMAPHORE)
    return pl.pallas_call(
        body, name="scatter_wait",
        out_shape=(pltpu.HBM(g_thru.shape, g_thru.dtype), pltpu.HBM(land_thru.shape, land_thru.dtype)),
        in_specs=(hbm, hbm, sem, sem) + (ANY,) * len(after), out_specs=(hbm, hbm), input_output_aliases={0: 0, 1: 1},
        compiler_params=pltpu.CompilerParams(has_side_effects=pltpu.SideEffectType.DATAFLOW_SIDE_EFFECTING),
    )(g_thru, land_thru, send_sems, recv_sems, *after)


def _all_reduce_small(packed):
    shape = packed.shape

    def body(p_ref, o_ref, slots, send_sems, recv_sems):
        me, mine = _flip(0)
        slots[mine] = p_ref[...]
        sends = []
        for k in range(1, N_DEV):
            peer, _ = _flip(k)
            cp = pltpu.make_async_remote_copy(
                src_ref=p_ref, dst_ref=slots.at[mine], send_sem=send_sems.at[k - 1], recv_sem=recv_sems.at[k - 1],
                device_id=peer, device_id_type=MESH_ID)
            cp.start()
            sends.append(cp)
        for k in range(1, N_DEV):
            _, theirs = _flip(k)
            pltpu.make_async_remote_copy(
                src_ref=p_ref, dst_ref=slots.at[theirs], send_sem=send_sems.at[k - 1],
                recv_sem=recv_sems.at[k - 1], device_id=me, device_id_type=MESH_ID).wait_recv()
        for cp in sends:
            cp.wait_send()
        acc = slots[0]
        for s in range(1, N_DEV):
            acc = acc + slots[s]
        o_ref[...] = acc

    vm = pl.BlockSpec(memory_space=pltpu.VMEM)
    return pl.pallas_call(
        body, name="all_reduce_small", in_specs=[vm], out_specs=vm, out_shape=SDS(shape, F32),
        scratch_shapes=[pltpu.VMEM((N_DEV,) + shape, F32), pltpu.SemaphoreType.DMA((7,)),
                        pltpu.SemaphoreType.DMA((7,))],
        compiler_params=pltpu.CompilerParams(has_side_effects=True),
    )(packed)


def _layer_fwd(x, p, tabs, ex):
    z, h_t, q, qt, k, v, vt = _in_proj(x, p["norm_g"], p["w_in_t"], p["qn"], p["kn"], tabs["ca"], tabs["sa"],
                                       tabs["ones"])
    oa, lse, *gathered = _attn_fwd(q, k, vt, ex)
    qrot, krot, vb, orr, on = _ret_fwd(z, p["lgf"], p["lgb"], p["gnw"], tabs["cr"], tabs["sr"])
    return z, h_t, q, qt, k, v, lse, oa, qrot, krot, vb, orr, on, gathered


def _layer_bwd(dxo, s, p, tabs, ex_attn, scatter_w_in):
    doa, don, dz_m, d_wout, d_wb_t = _merge_bwd(dxo, s["z"], s["oa"], s["on"], s["ya"], s["yb"], p["wb_t"], p["w_out"])
    dq_a, dk_a, dv_a, *recv_attn = _attn_bwd(s["q"], s["qt"], s["k"], s["v"], doa, s["oa"], s["lse"],
                                              ex_attn(d_wb_t, d_wout))
    dz_a, d_qn, d_kn = _attn_post_bwd(dq_a, dk_a, dv_a, s["z"], p["qn"], p["kn"], tabs["ca"], tabs["sa"],
                                      tabs["ones"])
    dq_r, dk_r, dv_r, d_gnw, d_lgf, d_lgb = _ret_bwd(s["qrot"], s["krot"], s["vb"], s["orr"], don, p["gnw"],
                                                     p["lgf"], p["lgb"])
    dqr, dkr, dvr = _ret_post_bwd(dq_r, dk_r, dv_r, tabs["cr"], tabs["sr"])
    buf = _dw_in(s["h_t"], dz_a, dz_m, dqr, dkr, dvr)
    pending, token = None, None
    if scatter_w_in:
        *pending, token = _scatter_start(buf)
    dx, d_norm_g = _in_bwd(dxo, s["x"], p["norm_g"], p["w_in_t"], dz_a, dz_m, dqr, dkr, dvr, token)
    grads = dict(w_in_t=buf, wb_t=d_wb_t, w_out=d_wout, norm_g=d_norm_g, gnw=d_gnw,
                 qn=d_qn.reshape(ATTN_Q_HEADS, ATTN_HEAD_DIM).sum(axis=0),
                 kn=d_kn.reshape(ATTN_KV_HEADS, ATTN_HEAD_DIM).sum(axis=0),
                 lgf=d_lgf[:, 0, 0], lgb=d_lgb[:, 0, 0])
    return dx, grads, recv_attn, pending


def _adamw_nd(w, g, m, v):
    shape = w.shape
    two_d = (1, shape[0]) if w.ndim == 1 else (-1, shape[-1])
    out = _adamw(w.reshape(two_d), g.reshape(two_d), m.reshape(two_d), v.reshape(two_d))
    return tuple(o.reshape(shape) for o in out)


def kernel(x, norm_g, w_in, attn_q_norm, attn_k_norm, ret_decay_fwd, ret_decay_bwd, ret_gn_w, w_branch_attn, w_branch_ret, w_out, final_norm_g, loss_target, m_norm_g, m_w_in, m_attn_q_norm, m_attn_k_norm, m_ret_decay_fwd, m_ret_decay_bwd, m_ret_gn_w, m_w_branch_attn, m_w_branch_ret, m_w_out, m_final_norm_g, v_norm_g, v_w_in, v_attn_q_norm, v_attn_k_norm, v_ret_decay_fwd, v_ret_decay_bwd, v_ret_gn_w, v_w_branch_attn, v_w_branch_ret, v_w_out, v_final_norm_g):
    t, d = x.shape[1], x.shape[2]
    x2, target = x[0], loss_target[0]

    w_in_sh, wb_sh, wout_sh = [], [], []
    for l in range(DEPTH):
        w_in_sh.append(jnp.swapaxes(w_in[l], 0, 1).astype(BF16))
        wb_sh.append(jnp.concatenate([w_branch_attn[l].T, w_branch_ret[l].T], axis=1).astype(BF16))
        wout_sh.append(w_out[l].astype(BF16))

    ca, sa = _rope_tables(t, ATTN_HEAD_DIM)
    cr, sr = _rope_tables(t, RET_HEAD_DIM)
    grp = jnp.arange(ATTN_WIDTH) // ATTN_HEAD_DIM
    tabs = dict(ca=jnp.tile(ca, (1, 2)), sa=jnp.tile(sa, (1, 2)), cr=cr, sr=sr,
                ones=jnp.where(grp[:, None] == grp[None, :], 1.0 / ATTN_HEAD_DIM, 0.0).astype(BF16))
    layers = []
    for l in range(DEPTH):
        layers.append(dict(
            norm_g=norm_g[l][None], qn=jnp.tile(attn_q_norm[l], ATTN_Q_HEADS)[None],
            kn=jnp.tile(attn_k_norm[l], ATTN_KV_HEADS)[None], gnw=ret_gn_w[l][None],
            lgf=jax.nn.log_sigmoid(ret_decay_fwd[l]), lgb=jax.nn.log_sigmoid(ret_decay_bwd[l])))

    layers[0]["w_in_t"], = _all_gather([w_in_sh[0]])
    gathers = [_Exchange("gather", [wb_sh[0], wout_sh[0], w_in_sh[1]]), _Exchange("gather", [wb_sh[1], wout_sh[1]])]
    h = x2
    saved = []
    for l in range(DEPTH):
        p = layers[l]
        z, h_t, q, qt, k, v, lse, oa, qrot, krot, vb, orr, on, got = _layer_fwd(h, p, tabs, gathers[l])
        p["wb_t"], p["w_out"] = got[0], got[1]
        if l == 0:
            layers[1]["w_in_t"] = got[2]
        xn, ya, yb = _merge_fwd(h, z, oa, on, p["wb_t"], p["w_out"])
        saved.append(dict(x=h, z=z, h_t=h_t, q=q, qt=qt, k=k, v=v, lse=lse, oa=oa, qrot=qrot, krot=krot, vb=vb,
                          orr=orr, on=on, ya=ya, yb=yb))
        h = xn
    dx, d_final_g, loss_part = _final_loss(h, final_norm_g[None], target)

    grads = [None] * DEPTH
    dx, grads[1], _, _ = _layer_bwd(dx, saved[1], layers[1], tabs, lambda *a: None, False)
    g1 = grads[1]
    ex_attn = lambda d_wb_t, d_wout: _Exchange("scatter", [g1["w_in_t"], g1["wb_t"], g1["w_out"], d_wb_t, d_wout])
    dx, grads[0], recv_attn, pending = _layer_bwd(dx, saved[0], layers[0], tabs, ex_attn, True)
    recv = [None, recv_attn[3], recv_attn[4], recv_attn[0], recv_attn[1], recv_attn[2]]
    tr = lambda a: jnp.swapaxes(a, 1, 2)
    w_in_t = (tr(w_in), tr(m_w_in), tr(v_w_in))
    sharded = {}
    w_in_l1 = _sum_adamw([recv[3]], *w_in_t, 0, 256, layer0=1)
    sharded[id(w_branch_attn)] = [tr(o) for o in _sum_adamw(
        [recv[1], recv[4]], tr(w_branch_attn), tr(m_w_branch_attn), tr(v_w_branch_attn), 0, 512)]
    sharded[id(w_branch_ret)] = [tr(o) for o in _sum_adamw(
        [recv[1], recv[4]], tr(w_branch_ret), tr(m_w_branch_ret), tr(v_w_branch_ret), 512, 512)]
    sharded[id(w_out)] = _sum_adamw([recv[2], recv[5]], w_out, m_w_out, v_w_out, 0, 256)
    g_wba, g_wbr, g_wout = (sharded[id(w)][0] for w in (w_branch_attn, w_branch_ret, w_out))

    packed = jnp.zeros((8, 1024), F32)
    for l in range(DEPTH):
        gl = grads[l]
        packed = packed.at[l].set(gl["norm_g"][0])
        packed = packed.at[2, 512 * l:512 * (l + 1)].set(gl["gnw"][0])
        packed = packed.at[4, 128 * l:128 * l + 64].set(gl["qn"])
        packed = packed.at[4, 256 + 128 * l:256 + 128 * l + 64].set(gl["kn"])
        packed = packed.at[4, 512 + 128 * l:512 + 128 * l + 4].set(gl["lgf"])
        packed = packed.at[4, 768 + 128 * l:768 + 128 * l + 4].set(gl["lgb"])
    packed = packed.at[3].set(d_final_g[0])
    packed = packed.at[5, 0].set(loss_part[0, 0])
    red = _all_reduce_small(packed)
    loss = red[5, 0]
    g_norm_g = red[0:2]
    g_gnw = red[2].reshape(DEPTH, RET_WIDTH)
    g_final = red[3]
    g_qn = jnp.stack([red[4, 128 * l:128 * l + 64] for l in range(DEPTH)])
    g_kn = jnp.stack([red[4, 256 + 128 * l:256 + 128 * l + 64] for l in range(DEPTH)])
    g_lgf = jnp.stack([red[4, 512 + 128 * l:512 + 128 * l + 4] for l in range(DEPTH)])
    g_lgb = jnp.stack([red[4, 768 + 128 * l:768 + 128 * l + 4] for l in range(DEPTH)])
    g_df = g_lgf * jax.nn.sigmoid(-ret_decay_fwd)
    g_db = g_lgb * jax.nn.sigmoid(-ret_decay_bwd)

    grad_w = [g_norm_g, None, g_qn, g_kn, g_df, g_db, g_gnw, g_wba, g_wbr, g_wout, g_final]
    weights = [norm_g, w_in, attn_q_norm, attn_k_norm, ret_decay_fwd, ret_decay_bwd, ret_gn_w, w_branch_attn,
               w_branch_ret, w_out, final_norm_g]
    ms = [m_norm_g, m_w_in, m_attn_q_norm, m_attn_k_norm, m_ret_decay_fwd, m_ret_decay_bwd, m_ret_gn_w,
          m_w_branch_attn, m_w_branch_ret, m_w_out, m_final_norm_g]
    vs = [v_norm_g, v_w_in, v_attn_q_norm, v_attn_k_norm, v_ret_decay_fwd, v_ret_decay_bwd, v_ret_gn_w,
          v_w_branch_attn, v_w_branch_ret, v_w_out, v_final_norm_g]
    upd = [None if w is w_in else sharded[id(w)][1:] if id(w) in sharded else _adamw_nd(w, g, m, v)
           for w, g, m, v in zip(weights, grad_w, ms, vs)]

    done = [dx, w_in_l1[0], g_wout] + [u[0] for w, u in zip(weights, upd) if u is not None and id(w) not in sharded]
    g_full, recv[0] = _scatter_wait(*pending, done)
    mine = (4 * lax.axis_index("x") + 2 * lax.axis_index("y") + lax.axis_index("c")).astype(jnp.int32)[None]
    w_in_upd = [tr(o) for o in _sum_adamw([recv[0]], *w_in_t, 0, 256, layer0=0, prev=w_in_l1, own=(g_full, mine))]
    grad_w[1], upd[1] = w_in_upd[0], w_in_upd[1:]
    return (loss, dx[None], *grad_w, *[u[0] for u in upd], *[u[1] for u in upd], *[u[2] for u in upd])
```

```python
import functools

import jax
import jax.numpy as jnp
from jax import lax
from jax.experimental import pallas as pl
from jax.experimental.pallas import tpu as pltpu

F32 = jnp.float32
BF16 = jnp.bfloat16
SDS = jax.ShapeDtypeStruct

D_MODEL = 1024
DEPTH = 2
GRID_W = 64
ATTN_Q_HEADS = 8
ATTN_KV_HEADS = 2
ATTN_HEAD_DIM = 64
ATTN_WIDTH = 512
ATTN_KV_WIDTH = 128
RET_HEADS = 4
RET_HEAD_DIM = 128
RET_WIDTH = 512
RET_CHUNK = 128
ATTN_KEY_CHUNK = 512
ATTN_BWD_KEY_CHUNK = 1024
ATTN_BWD_QUERY_TILE = 512
ATTN_FWD_QUERY_TILE = 512
QK_DOTS_PER_CHUNK = 4
EXP_LAG = 3
ROPE_THETA = 10000.0
EPS = 1e-6
D_IN = 5376
N_DEV = 8

ADAM_LR = 0.001
ADAM_B1 = 0.9
ADAM_B2 = 0.999
ADAM_EPS = 1e-08
ADAM_WD = 0.01
ADAM_STEP = 10

SEG = {
    "qa": (0, 512, 0),
    "ga": (768, 512, 512),
    "qr": (1280, 512, 1024),
    "kr": (1792, 512, 1536),
    "vr": (2304, 512, 2048),
    "gr": (2816, 512, 2560),
    "gm": (3328, 2048, 3072),
    "ka": (512, 128, 5120),
    "va": (640, 128, 5248),
}

VMEM_LIMIT = 60 * 1024 * 1024
NT = (((1,), (1,)), ((), ()))
TN = (((0,), (0,)), ((), ()))
MESH_ID = pl.DeviceIdType.MESH
ANY = pl.BlockSpec(memory_space=pl.ANY)


def _params(sem=None, vmem=VMEM_LIMIT):
    return pltpu.CompilerParams(dimension_semantics=sem, vmem_limit_bytes=vmem)


def _dot(a, b, dims=None):
    if dims is None:
        return jnp.dot(a, b, preferred_element_type=F32)
    return lax.dot_general(a, b, dims, preferred_element_type=F32)


def _sigmoid(x):
    return 1.0 / (1.0 + jnp.exp(-x))


def _swap_halves(x, q):
    n = x.shape[-1]
    axis = x.ndim - 1
    lane = lax.broadcasted_iota(jnp.int32, x.shape, axis)
    first = (lane % (2 * q)) < q
    return jnp.where(first, pltpu.roll(x, n - q, axis), pltpu.roll(x, q, axis))


def _rope(x, cos, sin_signed, q):
    return x * cos + _swap_halves(x, q) * sin_signed


def _rope_bwd(dy, cos, sin_signed, q):
    return dy * cos - _swap_halves(dy, q) * sin_signed


def _group_mean(v, ones_bd):
    hi = v.astype(BF16)
    lo = (v - hi.astype(F32)).astype(BF16)
    return _dot(hi, ones_bd) + _dot(lo, ones_bd)


def _rope_tables(t, head_dim):
    n_rows = t // GRID_W
    d_axis = head_dim // 2
    inv_freq = ROPE_THETA ** (-jnp.arange(0, d_axis, 2, dtype=F32) / d_axis)
    ar = jnp.arange(n_rows, dtype=F32)[:, None] * inv_freq
    ac = jnp.arange(GRID_W, dtype=F32)[:, None] * inv_freq
    by_row = lambda a: jnp.repeat(a, GRID_W, axis=0)
    by_col = lambda a: jnp.tile(a, (n_rows, 1))
    cr, sr, cc, sc = by_row(jnp.cos(ar)), by_row(jnp.sin(ar)), by_col(jnp.cos(ac)), by_col(jnp.sin(ac))
    return jnp.concatenate([cr, cr, cc, cc], axis=-1), jnp.concatenate([-sr, sr, -sc, sc], axis=-1)


def _me():
    return lax.axis_index("x"), lax.axis_index("y"), lax.axis_index("c")


def _flip(k):
    x, y, c = _me()
    px = 1 - x if k & 4 else x
    py = 1 - y if k & 2 else y
    pc = 1 - c if k & 1 else c
    return (px, py, pc), 4 * px + 2 * py + pc


class _Exchange:
    def __init__(self, kind, srcs):
        self.kind, self.srcs, self.n = kind, list(srcs), len(srcs)
        self.rows = [a.shape[0] if kind == "gather" else a.shape[0] // N_DEV for a in srcs]
        if kind == "gather":
            self.out_shape = [SDS((N_DEV * a.shape[0], a.shape[1]), a.dtype) for a in srcs]
        else:
            self.out_shape = [SDS((N_DEV, a.shape[0] // N_DEV, a.shape[1]), a.dtype) for a in srcs]
        self.scratch = [pltpu.SemaphoreType.DMA((self.n, N_DEV - 1)), pltpu.SemaphoreType.DMA((self.n, N_DEV - 1)),
                        pltpu.SemaphoreType.DMA((self.n,))]

    def _block(self, ref, a, idx):
        r = self.rows[a]
        return ref.at[pl.ds(pl.multiple_of(idx * r, 16), r), :]

    def _src(self, ins, a, idx):
        return ins[a] if self.kind == "gather" else self._block(ins[a], a, idx)

    def _dst(self, outs, a, idx):
        return self._block(outs[a], a, idx) if self.kind == "gather" else outs[a].at[idx]

    def _copies(self, ins, outs, sems):
        send_sems, recv_sems, local_sems = sems
        me, mine = _flip(0)
        local, sends, recvs = [], [], []
        for a in range(self.n):
            local.append(pltpu.make_async_copy(self._src(ins, a, mine), self._dst(outs, a, mine), local_sems.at[a]))
            for k in range(1, N_DEV):
                peer, theirs = _flip(k)
                sem = dict(send_sem=send_sems.at[a, k - 1], recv_sem=recv_sems.at[a, k - 1])
                sends.append(pltpu.make_async_remote_copy(
                    src_ref=self._src(ins, a, theirs), dst_ref=self._dst(outs, a, mine),
                    device_id=peer, device_id_type=MESH_ID, **sem))
                recvs.append(pltpu.make_async_remote_copy(
                    src_ref=self._dst(outs, a, theirs), dst_ref=self._dst(outs, a, theirs),
                    device_id=me, device_id_type=MESH_ID, **sem))
        return local, sends, recvs

    def start(self, ins, outs, sems):
        local, sends, _ = self._copies(ins, outs, sems)
        for cp in local + sends:
            cp.start()

    def wait(self, ins, outs, sems):
        local, sends, recvs = self._copies(ins, outs, sems)
        for cp in sends:
            cp.wait_send()
        for cp in recvs:
            cp.wait_recv()
        for cp in local:
            cp.wait()


def _with_exchange(body, n_in, n_out, n_scratch, ex, first, last):
    if ex is None:
        return body

    def wrapped(*refs):
        ins = refs[:n_in]
        ex_ins = refs[n_in:n_in + ex.n]
        outs = refs[n_in + ex.n:n_in + ex.n + n_out]
        ex_outs = refs[n_in + ex.n + n_out:n_in + 2 * ex.n + n_out]
        rest = refs[n_in + 2 * ex.n + n_out:]
        scratch, sems = rest[:n_scratch], rest[n_scratch:]

        @pl.when(first())
        def _():
            ex.start(ex_ins, ex_outs, sems)

        body(*ins, *outs, *scratch)

        @pl.when(last())
        def _():
            ex.wait(ex_ins, ex_outs, sems)

    return wrapped


def _ex_args(ex):
    if ex is None:
        return [], [], [], [], []
    return [ANY] * ex.n, [ANY] * ex.n, list(ex.out_shape), list(ex.scratch), list(ex.srcs)


def _in_proj(x, g, w_t, qn, kn, cos, sin, ones_bd):
    t, d = x.shape
    tm = min(256, t)
    tk = min(ATTN_KEY_CHUNK, t)
    per_chunk = tk // tm
    hd = ATTN_HEAD_DIM

    def body(x_ref, g_ref, w_ref, qn_ref, kn_ref, c_ref, s_ref, b_ref,
             z_ref, ht_ref, q_out, qt_out, k_out, v_out, vt_out):
        xv = x_ref[...]
        r = lax.rsqrt(jnp.mean(xv * xv, axis=-1, keepdims=True) + EPS)
        h = xv * r * g_ref[...]
        ht_ref[...] = h.T.astype(BF16)
        hb = h.astype(BF16)
        seg = {}
        for name, (nat, w, off) in SEG.items():
            seg[name] = _dot(hb, w_ref[nat:nat + w, :], NT)
            z_ref[:, off:off + w] = seg[name]

        bd = b_ref[...]
        c2, s2 = c_ref[...], s_ref[...]
        cq = jnp.concatenate([c2] * 4, axis=-1)
        sq = jnp.concatenate([s2] * 4, axis=-1)
        xq, xk, xvv = seg["qa"], seg["ka"], seg["va"]
        yq = xq * lax.rsqrt(_group_mean(xq * xq, bd) + EPS) * qn_ref[...]
        yq = _rope(yq, cq, sq, hd // 4) * (hd ** -0.5)
        yqt = yq.T
        for hh in range(ATTN_Q_HEADS):
            q_out[hh] = yq[:, hh * hd:(hh + 1) * hd].astype(BF16)
            qt_out[hh] = yqt[hh * hd:(hh + 1) * hd, :].astype(BF16)
        yk = xk * lax.rsqrt(_group_mean(xk * xk, bd[:ATTN_KV_WIDTH, :ATTN_KV_WIDTH]) + EPS) * kn_ref[...]
        yk = _rope(yk, c2, s2, hd // 4)
        xvt = xvv.T
        ones = jnp.ones((hd, tm), F32)
        for hh in range(ATTN_KV_HEADS):
            k_out[hh] = yk[:, hh * hd:(hh + 1) * hd].astype(BF16)
            v_out[hh] = xvv[:, hh * hd:(hh + 1) * hd].astype(BF16)
            vt_out[hh, 0] = jnp.concatenate([xvt[hh * hd:(hh + 1) * hd, :], ones], axis=0).astype(BF16)

    const = lambda shape: pl.BlockSpec(shape, lambda i: (0,) * len(shape))
    rows = lambda w: pl.BlockSpec((tm, w), lambda i: (i, 0))
    return pl.pallas_call(
        body, name="in_proj", grid=(t // tm,),
        in_specs=[rows(d), const((1, d)), const((D_IN, d)), const((1, 512)), const((1, 128)), rows(128), rows(128),
                  const((512, 512))],
        out_specs=[rows(D_IN), pl.BlockSpec((d, tm), lambda i: (0, i)),
                   pl.BlockSpec((ATTN_Q_HEADS, tm, hd), lambda i: (0, i, 0)),
                   pl.BlockSpec((ATTN_Q_HEADS, hd, tm), lambda i: (0, 0, i)),
                   pl.BlockSpec((ATTN_KV_HEADS, tm, hd), lambda i: (0, i, 0)),
                   pl.BlockSpec((ATTN_KV_HEADS, tm, hd), lambda i: (0, i, 0)),
                   pl.BlockSpec((ATTN_KV_HEADS, 1, 2 * hd, tm), lambda i: (0, i // per_chunk, 0, i % per_chunk))],
        out_shape=[SDS((t, D_IN), F32), SDS((d, t), BF16),
                   SDS((ATTN_Q_HEADS, t, hd), BF16), SDS((ATTN_Q_HEADS, hd, t), BF16),
                   SDS((ATTN_KV_HEADS, t, hd), BF16), SDS((ATTN_KV_HEADS, t, hd), BF16),
                   SDS((ATTN_KV_HEADS, t // tk, 2 * hd, tk), BF16)],
        compiler_params=_params(("parallel",)),
    )(x, g, w_t, qn, kn, cos, sin, ones_bd)


def _attn_fwd(q, k, vt, ex=None):
    t = q.shape[1]
    tq = min(ATTN_FWD_QUERY_TILE, t)
    nk, tk = vt.shape[1], vt.shape[3]
    hd = ATTN_HEAD_DIM
    g = ATTN_Q_HEADS // ATTN_KV_HEADS

    def body(q_ref, k_ref, vt_ref, o_ref, lse_ref, s_scr):
        def pass_a(h, c, m8):
            part = tk // QK_DOTS_PER_CHUNK
            for lo in range(c * tk, (c + 1) * tk, part):
                st = _dot(k_ref[0, lo:lo + part, :], q_ref[h], NT)
                s_scr[h % 2, lo:lo + part, :] = st
                m8 = jnp.maximum(m8, jnp.max(st.reshape(part // 8, 8, tq), axis=0))
            return m8

        def pass_b(h, c, m, acc, after):
            e = jnp.exp(s_scr[h % 2, c * tk:(c + 1) * tk, :] - (m + after * 0.0)).astype(BF16)
            return acc + _dot(vt_ref[0, c], e)

        neg = jnp.full((8, tq), -jnp.inf, F32)
        m8 = neg
        for c in range(nk):
            m8 = pass_a(0, c, m8)
        outs = []
        for h in range(g):
            m = jnp.max(m8, axis=0, keepdims=True)
            acc = jnp.zeros((2 * hd, tq), F32)
            m8 = neg
            done = [m] * EXP_LAG
            for c in range(nk):
                if h + 1 < g:
                    m8 = pass_a(h + 1, c, m8)
                acc = pass_b(h, c, m, acc, done[-EXP_LAG])
                done.append(m8[0:1, :] if h + 1 < g else acc[hd:hd + 1, :])
            l = acc[hd:hd + 1, :]
            outs.append((acc[:hd, :] / l).T)
            lse_ref[h] = m + jnp.log(l)
        o_ref[...] = jnp.concatenate(outs, axis=-1)

    nq = t // tq
    first = lambda: jnp.logical_and(pl.program_id(0) == 0, pl.program_id(1) == 0)
    last = lambda: jnp.logical_and(pl.program_id(0) == ATTN_KV_HEADS - 1, pl.program_id(1) == nq - 1)
    xi, xo, xs, xscr, xargs = _ex_args(ex)
    return pl.pallas_call(
        _with_exchange(body, 3, 2, 1, ex, first, last), name="attn_fwd", grid=(ATTN_KV_HEADS, nq),
        in_specs=[pl.BlockSpec((g, tq, hd), lambda p, i: (p, i, 0)),
                  pl.BlockSpec((1, t, hd), lambda p, i: (p, 0, 0)),
                  pl.BlockSpec((1, nk, 2 * hd, tk), lambda p, i: (p, 0, 0, 0))] + xi,
        out_specs=[pl.BlockSpec((tq, g * hd), lambda p, i: (i, p)),
                   pl.BlockSpec((g, 1, tq), lambda p, i: (p, 0, i))] + xo,
        out_shape=[SDS((t, ATTN_WIDTH), F32), SDS((ATTN_Q_HEADS, 1, t), F32)] + xs,
        scratch_shapes=[pltpu.VMEM((2, t, tq), F32)] + xscr,
        compiler_params=_params(("arbitrary", "arbitrary")),
    )(q, k, vt, *xargs)


class _Dir:
    def __init__(self, lg, strict_future):
        c = RET_CHUNK
        ia = lax.broadcasted_iota(jnp.int32, (c, c), 0).astype(F32)
        ib = lax.broadcasted_iota(jnp.int32, (c, c), 1).astype(F32)
        col = lax.broadcasted_iota(jnp.int32, (c, 1), 0).astype(F32)
        row = lax.broadcasted_iota(jnp.int32, (1, c), 1).astype(F32)
        if strict_future:
            dist = ib - ia
            mask = dist > 0
            self.wq, self.wk, wk_row = c - col, col, row
        else:
            dist = ia - ib
            mask = dist >= 0
            self.wq, self.wk, wk_row = col + 1.0, c - 1.0 - col, c - 1.0 - row
        self.dist = jnp.maximum(dist, 0.0)
        self.d = jnp.where(mask, jnp.exp(self.dist * lg), 0.0)
        self.qd = jnp.exp(self.wq * lg)
        self.kd_col = jnp.exp(self.wk * lg)
        self.kd_row = jnp.exp(wk_row * lg)
        self.cd = jnp.exp(jnp.full((1, 1), float(c), F32) * lg)


def _ret_fwd(z, lgf, lgb, gnw, cos, sin):
    t = z.shape[0]
    c = RET_CHUNK
    nc = t // c
    hd = RET_HEAD_DIM
    unroll = 4 if nc % 4 == 0 else 1

    def body(lgf_ref, lgb_ref, q_ref, k_ref, v_ref, c_ref, s_ref, w_ref,
             qo_ref, ko_ref, vo_ref, orr_ref, on_ref, kt, uf, ub, sfa, sba):
        h = pl.program_id(0)
        fw = _Dir(lgf_ref[h], False)
        bw = _Dir(lgb_ref[h], True)
        cc, ss = c_ref[...], s_ref[...]
        qo_ref[...] = _rope(q_ref[...], cc, ss, hd // 4).astype(BF16)
        kr = _rope(k_ref[...], cc, ss, hd // 4) * (hd ** -0.5)
        ko_ref[...] = kr.astype(BF16)
        vo_ref[...] = v_ref[...].astype(BF16)
        for i in range(nc):
            kt[i] = kr[i * c:(i + 1) * c, :].T.astype(BF16)

        def rows(ci):
            return pl.ds(pl.multiple_of(ci * c, c), c)

        def kv_products(ci, carry):
            vv = vo_ref[rows(ci), :]
            ktf = kt[ci].astype(F32)
            uf[ci] = _dot((ktf * fw.kd_row).astype(BF16), vv)
            ub[ci] = _dot((ktf * bw.kd_row).astype(BF16), vv)
            return carry

        lax.fori_loop(0, nc, kv_products, 0, unroll=unroll)

        def scan(i, carry):
            sf, sb = carry
            j = nc - 1 - i
            sfa[i] = sf.astype(BF16)
            sba[j] = sb.astype(BF16)
            return sf * fw.cd + uf[i], sb * bw.cd + ub[j]

        zero = jnp.zeros((hd, hd), F32)
        lax.fori_loop(0, nc, scan, (zero, zero))
        gw = w_ref[...]

        def outputs(ci, carry):
            sl = rows(ci)
            qq, kk, vv = qo_ref[sl, :], ko_ref[sl, :], vo_ref[sl, :]
            a = _dot(qq, kk, NT)
            o = (_dot((a * fw.d).astype(BF16), vv) + _dot(qq, sfa[ci]) * fw.qd
                 + _dot((a * bw.d).astype(BF16), vv) + _dot(qq, sba[ci]) * bw.qd)
            orr_ref[sl, :] = o
            xc = o - jnp.mean(o, axis=-1, keepdims=True)
            var = jnp.mean(xc * xc, axis=-1, keepdims=True)
            on_ref[sl, :] = xc * lax.rsqrt(var + EPS) * gw
            return carry

        group = 32 if nc % 32 == 0 else 1

        def output_group(i, carry):
            for j in range(group):
                outputs(i * group + j, carry)
            return carry

        lax.fori_loop(0, nc // group, output_group, 0)

    smem = pl.BlockSpec(memory_space=pltpu.SMEM)
    col = lambda name: (lambda h: (0, SEG[name][2] // 128 + h))
    head = pl.BlockSpec((t, 128), lambda h: (0, h))
    full = pl.BlockSpec((t, 128), lambda h: (0, 0))
    return pl.pallas_call(
        body, name="ret_fwd", grid=(RET_HEADS,),
        in_specs=[smem, smem, pl.BlockSpec((t, 128), col("qr")), pl.BlockSpec((t, 128), col("kr")),
                  pl.BlockSpec((t, 128), col("vr")), full, full, pl.BlockSpec((1, 128), lambda h: (0, h))],
        out_specs=[head, head, head, head, head],
        out_shape=[SDS((t, RET_WIDTH), BF16)] * 3 + [SDS((t, RET_WIDTH), F32)] * 2,
        scratch_shapes=[pltpu.VMEM((nc, hd, c), BF16), pltpu.VMEM((nc, hd, hd), F32), pltpu.VMEM((nc, hd, hd), F32),
                        pltpu.VMEM((nc, hd, hd), BF16), pltpu.VMEM((nc, hd, hd), BF16)],
        compiler_params=_params(("parallel",)),
    )(lgf, lgb, z, z, z, cos, sin, gnw)


def _merge_fwd(x, z, oa, on, wb_t, wout):
    t, d = x.shape
    tm = min(256, t)

    def body(x_ref, ga_ref, gr_ref, gm0_ref, gm1_ref, oa_ref, on_ref, wb_ref, wo_ref, xn_ref, ya_ref, yb_ref):
        ga, gr = ga_ref[...], gr_ref[...]
        ua = ga * _sigmoid(ga) * oa_ref[...]
        ub = gr * _sigmoid(gr) * on_ref[...]
        ya = _dot(ua.astype(BF16), wb_ref[:, :512], NT)
        yb = _dot(ub.astype(BF16), wb_ref[:, 512:], NT)
        ya_ref[...] = ya
        yb_ref[...] = yb
        merged = _sigmoid(gm0_ref[...]) * ya + _sigmoid(gm1_ref[...]) * yb
        xn_ref[...] = x_ref[...] + _dot(merged.astype(BF16), wo_ref[...])

    row = lambda w, j: pl.BlockSpec((tm, w), lambda i: (i, j))
    const = lambda shape: pl.BlockSpec(shape, lambda i: (0, 0))
    return pl.pallas_call(
        body, name="merge_fwd", grid=(t // tm,),
        in_specs=[row(d, 0), row(512, SEG["ga"][2] // 512), row(512, SEG["gr"][2] // 512),
                  row(1024, SEG["gm"][2] // 1024), row(1024, SEG["gm"][2] // 1024 + 1),
                  row(512, 0), row(512, 0), const((d, 1024)), const((d, d))],
        out_specs=[row(d, 0), row(d, 0), row(d, 0)],
        out_shape=[SDS((t, d), F32)] * 3,
        compiler_params=_params(("parallel",)),
    )(x, z, z, z, z, oa, on, wb_t, wout)


def _final_loss(x, g, target):
    t, d = x.shape
    tm = min(512, t)
    n = t // tm

    def body(x_ref, g_ref, t_ref, dx_ref, dg_ref, loss_ref, acc_g, acc_l):
        i = pl.program_id(0)

        @pl.when(i == 0)
        def _():
            acc_g[...] = jnp.zeros_like(acc_g)
            acc_l[...] = jnp.zeros_like(acc_l)

        xv, gv = x_ref[...], g_ref[...]
        r = lax.rsqrt(jnp.mean(xv * xv, axis=-1, keepdims=True) + EPS)
        xh = xv * r
        err = xh * gv - t_ref[...]
        dy = err * (1.0 / d)
        gy = dy * gv
        dx_ref[...] = r * (gy - xh * jnp.mean(gy * xh, axis=-1, keepdims=True))
        acc_g[...] += jnp.sum((dy * xh).reshape(tm // 8, 8, d), axis=0)
        acc_l[...] += jnp.sum((err * err).reshape(tm // 8, 8, d), axis=0)

        @pl.when(i == n - 1)
        def _():
            dg_ref[...] = jnp.sum(acc_g[...], axis=0, keepdims=True)
            tot = jnp.sum(jnp.sum(acc_l[...], axis=0, keepdims=True), axis=1, keepdims=True)
            loss_ref[...] = jnp.broadcast_to(tot * (0.5 / d), (1, 128))

    return pl.pallas_call(
        body, name="final_loss", grid=(n,),
        in_specs=[pl.BlockSpec((tm, d), lambda i: (i, 0)), pl.BlockSpec((1, d), lambda i: (0, 0)),
                  pl.BlockSpec((tm, d), lambda i: (i, 0))],
        out_specs=[pl.BlockSpec((tm, d), lambda i: (i, 0)), pl.BlockSpec((1, d), lambda i: (0, 0)),
                   pl.BlockSpec((1, 128), lambda i: (0, 0))],
        out_shape=[SDS((t, d), F32), SDS((1, d), F32), SDS((1, 128), F32)],
        scratch_shapes=[pltpu.VMEM((8, d), F32), pltpu.VMEM((8, d), F32)],
        compiler_params=_params(("arbitrary",)),
    )(x, g, target)


def _merge_bwd(dxo, z, oa, on, ya, yb, wb_t, wout):
    t, d = dxo.shape
    tm = min(256, t)
    n = t // tm

    def body(dx_ref, ga_ref, gr_ref, gm0_ref, gm1_ref, oa_ref, on_ref, ya_ref, yb_ref, wb_ref, wo_ref,
             doa_ref, don_ref, dz_ref, dwo_ref, dwb_ref, acc_o, acc_b):
        i = pl.program_id(0)

        @pl.when(i == 0)
        def _():
            acc_o[...] = jnp.zeros_like(acc_o)
            acc_b[...] = jnp.zeros_like(acc_b)

        dxb = dx_ref[...].astype(BF16)
        ya, yb = ya_ref[...], yb_ref[...]
        g0, g1 = _sigmoid(gm0_ref[...]), _sigmoid(gm1_ref[...])
        mb = (g0 * ya + g1 * yb).astype(BF16)
        dm = _dot(dxb, wo_ref[...], NT)
        dya = (dm * g0).astype(BF16)
        dyb = (dm * g1).astype(BF16)
        dz_ref[:, 1024:2048] = (dm * ya * g0 * (1.0 - g0)).astype(BF16)
        dz_ref[:, 2048:3072] = (dm * yb * g1 * (1.0 - g1)).astype(BF16)

        def branch(g_ref, o_ref, dy, w, do_ref, lo):
            gv, ov = g_ref[...], o_ref[...]
            sg = _sigmoid(gv)
            silu = gv * sg
            du = _dot(dy, w)
            do_ref[...] = du * silu
            dz_ref[:, lo:lo + 512] = (du * ov * (sg * (1.0 + gv * (1.0 - sg)))).astype(BF16)
            acc_b[:, lo:lo + 512] += _dot(dy, (silu * ov).astype(BF16), TN)

        branch(ga_ref, oa_ref, dya, wb_ref[:, :512], doa_ref, 0)
        branch(gr_ref, on_ref, dyb, wb_ref[:, 512:], don_ref, 512)
        acc_o[...] += _dot(mb, dxb, TN)

        @pl.when(i == n - 1)
        def _():
            dwo_ref[...] = acc_o[...].astype(BF16)
            dwb_ref[...] = acc_b[...].astype(BF16)

    row = lambda w, j: pl.BlockSpec((tm, w), lambda i: (i, j))
    const = lambda shape: pl.BlockSpec(shape, lambda i: (0, 0))
    return pl.pallas_call(
        body, name="merge_bwd", grid=(n,),
        in_specs=[row(d, 0), row(512, SEG["ga"][2] // 512), row(512, SEG["gr"][2] // 512),
                  row(1024, SEG["gm"][2] // 1024), row(1024, SEG["gm"][2] // 1024 + 1),
                  row(512, 0), row(512, 0), row(d, 0), row(d, 0), const((d, 1024)), const((d, d))],
        out_specs=[row(512, 0), row(512, 0), row(3072, 0), const((d, d)), const((d, 1024))],
        out_shape=[SDS((t, 512), F32), SDS((t, 512), F32), SDS((t, 3072), BF16), SDS((d, d), BF16),
                   SDS((d, 1024), BF16)],
        scratch_shapes=[pltpu.VMEM((d, d), F32), pltpu.VMEM((d, 1024), F32)],
        compiler_params=_params(("arbitrary",)),
    )(dxo, z, z, z, z, oa, on, ya, yb, wb_t, wout)


def _ret_bwd(qrot, krot, vb, orr, don, gnw, lgf, lgb):
    t = qrot.shape[0]
    c = RET_CHUNK
    nc = t // c
    hd = RET_HEAD_DIM
    unroll = 4 if nc % 4 == 0 else 1

    def body(lgf_ref, lgb_ref, q_ref, k_ref, v_ref, o_ref, dn_ref, w_ref,
             dq_ref, dk_ref, dv_ref, dw_ref, dlf_ref, dlb_ref, qt, kt, dob, uf, ub, wf, wb, sfa, sba, gfa, gba):
        h = pl.program_id(0)
        fw = _Dir(lgf_ref[h], False)
        bw = _Dir(lgb_ref[h], True)
        fw.dt, bw.dt = fw.d.T, bw.d.T

        o = o_ref[...]
        xc = o - jnp.mean(o, axis=-1, keepdims=True)
        r = lax.rsqrt(jnp.mean(xc * xc, axis=-1, keepdims=True) + EPS)
        xh = xc * r
        dn = dn_ref[...]
        gy = dn * w_ref[...]
        d_o = r * (gy - jnp.mean(gy, axis=-1, keepdims=True) - xh * jnp.mean(gy * xh, axis=-1, keepdims=True))
        dw_ref[...] = jnp.sum(dn * xh, axis=0, keepdims=True)
        dob[...] = d_o.astype(BF16)
        for i in range(nc):
            qt[i] = q_ref[i * c:(i + 1) * c, :].astype(F32).T.astype(BF16)
            kt[i] = k_ref[i * c:(i + 1) * c, :].astype(F32).T.astype(BF16)

        def rows(ci):
            return pl.ds(pl.multiple_of(ci * c, c), c)

        def products(ci, carry):
            sl = rows(ci)
            vv, do32 = v_ref[sl, :], dob[sl, :].astype(F32)
            ktf = kt[ci].astype(F32)
            uf[ci] = _dot((ktf * fw.kd_row).astype(BF16), vv)
            ub[ci] = _dot((ktf * bw.kd_row).astype(BF16), vv)
            wf[ci] = _dot(qt[ci], (do32 * fw.qd).astype(BF16))
            wb[ci] = _dot(qt[ci], (do32 * bw.qd).astype(BF16))
            return carry

        lax.fori_loop(0, nc, products, 0, unroll=unroll)

        def scan(i, carry):
            sf, sb, gf, gb = carry
            j = nc - 1 - i
            sfa[i] = sf.astype(BF16)
            sba[j] = sb.astype(BF16)
            gfa[j] = gf.astype(BF16)
            gba[i] = gb.astype(BF16)
            return sf * fw.cd + uf[i], sb * bw.cd + ub[j], gf * fw.cd + wf[j], gb * bw.cd + wb[i]

        zero = jnp.zeros((hd, hd), F32)
        lax.fori_loop(0, nc, scan, (zero, zero, zero, zero))

        def one_dir(p, s_all, g_all, ci, qq, kk, vv, do, a, bm):
            sb, gb = s_all[ci], g_all[ci]
            doq = (do.astype(F32) * p.qd).astype(BF16)
            dqc = _dot(doq, sb, NT)
            kkd = (kk.astype(F32) * p.kd_col).astype(BF16)
            dk2 = _dot(vv, gb, NT) * p.kd_col
            terms = (p.dist * p.d * a * bm + p.wq * qq.astype(F32) * dqc + p.wk * kk.astype(F32) * dk2
                     + (float(c) * p.cd) * gb.astype(F32) * sb.astype(F32))
            return dqc, dk2, _dot(kkd, gb), terms

        d_both, dt_both = fw.d + bw.d, fw.dt + bw.dt

        def chunk(ci, carry):
            af, ab = carry
            sl = rows(ci)
            qq, kk, vv, do = q_ref[sl, :], k_ref[sl, :], v_ref[sl, :], dob[sl, :]
            a, bm = _dot(qq, kk, NT), _dot(do, vv, NT)
            at, bt = _dot(kk, qq, NT), _dot(vv, do, NT)
            dqf, dkf, dvf, tf = one_dir(fw, sfa, gfa, ci, qq, kk, vv, do, a, bm)
            dqb, dkb, dvb, tb = one_dir(bw, sba, gba, ci, qq, kk, vv, do, a, bm)
            dq_ref[sl, :] = _dot((bm * d_both).astype(BF16), kk) + dqf + dqb
            dk_ref[sl, :] = _dot((bt * dt_both).astype(BF16), qq) + dkf + dkb
            dv_ref[sl, :] = _dot((at * dt_both).astype(BF16), do) + dvf + dvb
            return af + tf, ab + tb

        pair = 8 if nc % 8 == 0 else 1

        def chunks(i, carry):
            for j in range(pair):
                carry = chunk(i * pair + j, carry)
            return carry

        af, ab = lax.fori_loop(0, nc // pair, chunks, (zero, zero))
        tot = lambda m: jnp.sum(jnp.sum(m, axis=0, keepdims=True), axis=1, keepdims=True)
        dlf_ref[...] = jnp.broadcast_to(tot(af).reshape(1, 1, 1), (1, 8, 128))
        dlb_ref[...] = jnp.broadcast_to(tot(ab).reshape(1, 1, 1), (1, 8, 128))

    smem = pl.BlockSpec(memory_space=pltpu.SMEM)
    head = pl.BlockSpec((t, 128), lambda h: (0, h))
    vec = pl.BlockSpec((1, 128), lambda h: (0, h))
    scal = pl.BlockSpec((1, 8, 128), lambda h: (h, 0, 0))
    mats = lambda dt: pltpu.VMEM((nc, hd, hd), dt)
    return pl.pallas_call(
        body, name="ret_bwd", grid=(RET_HEADS,),
        in_specs=[smem, smem, head, head, head, head, head, vec],
        out_specs=[head, head, head, vec, scal, scal],
        out_shape=[SDS((t, RET_WIDTH), F32)] * 3 + [SDS((1, RET_WIDTH), F32), SDS((RET_HEADS, 8, 128), F32),
                                                   SDS((RET_HEADS, 8, 128), F32)],
        scratch_shapes=[pltpu.VMEM((nc, hd, c), BF16), pltpu.VMEM((nc, hd, c), BF16), pltpu.VMEM((t, hd), BF16),
                        mats(F32), mats(F32), mats(F32), mats(F32), mats(BF16), mats(BF16), mats(BF16), mats(BF16)],
        compiler_params=_params(("parallel",)),
    )(lgf, lgb, qrot, krot, vb, orr, don, gnw)


def _ret_post_bwd(dq, dk, dv, cos, sin):
    t = dq.shape[0]
    tm = min(512, t)
    hd = RET_HEAD_DIM

    def body(dq_ref, dk_ref, dv_ref, c_ref, s_ref, oq_ref, ok_ref, ov_ref):
        cc = jnp.concatenate([c_ref[...]] * 4, axis=-1)
        ss = jnp.concatenate([s_ref[...]] * 4, axis=-1)
        oq_ref[...] = _rope_bwd(dq_ref[...], cc, ss, hd // 4).astype(BF16)
        ok_ref[...] = (_rope_bwd(dk_ref[...], cc, ss, hd // 4) * (hd ** -0.5)).astype(BF16)
        ov_ref[...] = dv_ref[...].astype(BF16)

    blk = pl.BlockSpec((tm, 512), lambda i: (i, 0))
    tab = pl.BlockSpec((tm, 128), lambda i: (i, 0))
    return pl.pallas_call(
        body, name="ret_post_bwd", grid=(t // tm,),
        in_specs=[blk, blk, blk, tab, tab], out_specs=[blk, blk, blk],
        out_shape=[SDS((t, 512), BF16)] * 3,
        compiler_params=_params(("parallel",)),
    )(dq, dk, dv, cos, sin)


def _attn_bwd(q, qt, k, v, doa, oa, lse, ex=None):
    t = q.shape[1]
    tq = min(ATTN_BWD_QUERY_TILE, t)
    nq = t // tq
    tk = min(ATTN_BWD_KEY_CHUNK, t)
    nk = t // tk
    hd = ATTN_HEAD_DIM
    scale = hd ** -0.5

    def body(q_ref, qt_ref, k_ref, v_ref, do_ref, o_ref, lse_ref, dq_ref, dkt_ref, dvt_ref):
        p, i = pl.program_id(0), pl.program_id(1)

        @pl.when(jnp.logical_and(p % 2 == 0, i == 0))
        def _():
            dkt_ref[...] = jnp.zeros_like(dkt_ref)
            dvt_ref[...] = jnp.zeros_like(dvt_ref)

        dov, ov = do_ref[...], o_ref[...]
        dovt = dov.T
        lanes = lambda col: jnp.concatenate([col] * (tk // 128), axis=1)
        outs = []
        for j in range(2):
            qq, qqt = q_ref[j], qt_ref[j]
            do32 = dov[:, j * hd:(j + 1) * hd]
            do, dot_ = do32.astype(BF16), dovt[j * hd:(j + 1) * hd, :].astype(BF16)
            dd = lanes(jnp.broadcast_to(jnp.sum(do32 * ov[:, j * hd:(j + 1) * hd], axis=1, keepdims=True), (tq, 128)))
            lse_j = lanes(jnp.broadcast_to(lse_ref[j], (128, tq)).T)
            dq = jnp.zeros((tq, hd), F32)
            for c in range(nk):
                sl = slice(c * tk, (c + 1) * tk)
                kc, vc = k_ref[0, sl, :], v_ref[0, sl, :]
                pr = jnp.exp(_dot(qq, kc, NT) - lse_j)
                ds = (pr * (_dot(do, vc, NT) - dd)).astype(BF16)
                dvt_ref[0, :, sl] += _dot(dot_, pr.astype(BF16))
                dkt_ref[0, :, sl] += _dot(qqt, ds)
                dq = dq + _dot(ds, kc)
            outs.append(dq * scale)
        dq_ref[...] = jnp.concatenate(outs, axis=-1)

    kv = pl.BlockSpec((1, t, hd), lambda p, i: (p // 2, 0, 0))
    kvt = pl.BlockSpec((1, hd, t), lambda p, i: (p // 2, 0, 0))
    pair = pl.BlockSpec((tq, 128), lambda p, i: (i, p))
    first = lambda: jnp.logical_and(pl.program_id(0) == 0, pl.program_id(1) == 0)
    last = lambda: jnp.logical_and(pl.program_id(0) == 3, pl.program_id(1) == nq - 1)
    xi, xo, xs, xscr, xargs = _ex_args(ex)
    return pl.pallas_call(
        _with_exchange(body, 7, 3, 0, ex, first, last), name="attn_bwd", grid=(4, nq),
        in_specs=[pl.BlockSpec((2, tq, hd), lambda p, i: (p, i, 0)), pl.BlockSpec((2, hd, tq), lambda p, i: (p, 0, i)),
                  kv, kv, pair, pair, pl.BlockSpec((2, 1, tq), lambda p, i: (p, 0, i))] + xi,
        out_specs=[pair, kvt, kvt] + xo,
        out_shape=[SDS((t, ATTN_WIDTH), F32), SDS((ATTN_KV_HEADS, hd, t), F32),
                   SDS((ATTN_KV_HEADS, hd, t), F32)] + xs,
        scratch_shapes=xscr,
        compiler_params=_params(("arbitrary", "arbitrary")),
    )(q, qt, k, v, doa, oa, lse, *xargs)


def _attn_post_bwd(dq, dk, dv, z, qn, kn, cos, sin, ones_bd):
    t = z.shape[0]
    tm = min(512, t)
    n = t // tm
    hd = ATTN_HEAD_DIM

    def body(dq_ref, dk_ref, dv_ref, zq_ref, zkv_ref, qn_ref, kn_ref, c_ref, s_ref, b_ref,
             dz_ref, dqn_ref, dkn_ref, acc_q, acc_k):
        i = pl.program_id(0)

        @pl.when(i == 0)
        def _():
            acc_q[...] = jnp.zeros_like(acc_q)
            acc_k[...] = jnp.zeros_like(acc_k)

        bd = b_ref[...]
        c2, s2 = c_ref[...], s_ref[...]

        def norm_bwd(dy, x, w, ones, cos_t, sin_t, acc):
            dyr = _rope_bwd(dy, cos_t, sin_t, hd // 4)
            r = lax.rsqrt(_group_mean(x * x, ones) + EPS)
            xh = x * r
            gy = dyr * w
            acc[...] += jnp.sum((dyr * xh).reshape(tm // 8, 8, x.shape[-1]), axis=0)
            return r * (gy - xh * _group_mean(gy * xh, ones))

        cq = jnp.concatenate([c2] * 4, axis=-1)
        sq = jnp.concatenate([s2] * 4, axis=-1)
        dz_ref[:, :512] = norm_bwd(dq_ref[...], zq_ref[...], qn_ref[...], bd, cq, sq, acc_q).astype(BF16)
        zkv = zkv_ref[...]
        dkk = jnp.concatenate([dk_ref[0], dk_ref[1]], axis=0).T
        dz_ref[:, 512:640] = norm_bwd(dkk, zkv[:, :128], kn_ref[...], bd[:128, :128], c2, s2, acc_k).astype(BF16)
        dz_ref[:, 640:768] = jnp.concatenate([dv_ref[0], dv_ref[1]], axis=0).T.astype(BF16)

        @pl.when(i == n - 1)
        def _():
            dqn_ref[...] = jnp.sum(acc_q[...], axis=0, keepdims=True)
            dkn_ref[...] = jnp.sum(acc_k[...], axis=0, keepdims=True)

    kv_blk = SEG["ka"][2] // 256
    kvs = pl.BlockSpec((ATTN_KV_HEADS, hd, tm), lambda i: (0, 0, i))
    const = lambda shape: pl.BlockSpec(shape, lambda i: (0, 0))
    return pl.pallas_call(
        body, name="attn_post_bwd", grid=(n,),
        in_specs=[pl.BlockSpec((tm, 512), lambda i: (i, 0)), kvs, kvs,
                  pl.BlockSpec((tm, 512), lambda i: (i, 0)), pl.BlockSpec((tm, 256), lambda i: (i, kv_blk)),
                  const((1, 512)), const((1, 128)),
                  pl.BlockSpec((tm, 128), lambda i: (i, 0)), pl.BlockSpec((tm, 128), lambda i: (i, 0)),
                  const((512, 512))],
        out_specs=[pl.BlockSpec((tm, 768), lambda i: (i, 0)), const((1, 512)), const((1, 128))],
        out_shape=[SDS((t, 768), BF16), SDS((1, 512), F32), SDS((1, 128), F32)],
        scratch_shapes=[pltpu.VMEM((8, 512), F32), pltpu.VMEM((8, 128), F32)],
        compiler_params=_params(("arbitrary",)),
    )(dq, dk, dv, z, z, qn, kn, cos, sin, ones_bd)


def _in_bwd(dxo, x, g, w_t, dz_a, dz_m, dqr, dkr, dvr, after=None):
    t, d = x.shape
    tm = min(256, t)
    n = t // tm
    parts = [(0, 0, 768, 0), (1, 0, 512, SEG["ga"][0]), (2, 0, 512, SEG["qr"][0]), (3, 0, 512, SEG["kr"][0]),
             (4, 0, 512, SEG["vr"][0]), (1, 512, 2560, SEG["gr"][0])]

    def body(dx_ref, x_ref, g_ref, w_ref, a_ref, m_ref, q_ref, k_ref, v_ref, o_ref, dg_ref, acc):
        i = pl.program_id(0)

        @pl.when(i == 0)
        def _():
            acc[...] = jnp.zeros_like(acc)

        pieces = [a_ref, m_ref, q_ref, k_ref, v_ref]
        dh = jnp.zeros((tm, d), F32)
        for pi, lo, w, row in parts:
            dh = dh + _dot(pieces[pi][:, lo:lo + w], w_ref[row:row + w, :])
        xv = x_ref[...]
        r = lax.rsqrt(jnp.mean(xv * xv, axis=-1, keepdims=True) + EPS)
        xh = xv * r
        gy = dh * g_ref[...]
        o_ref[...] = dx_ref[...] + r * (gy - xh * jnp.mean(gy * xh, axis=-1, keepdims=True))
        acc[...] += jnp.sum((dh * xh).reshape(tm // 8, 8, d), axis=0)

        @pl.when(i == n - 1)
        def _():
            dg_ref[...] = jnp.sum(acc[...], axis=0, keepdims=True)

    row = lambda w: pl.BlockSpec((tm, w), lambda i: (i, 0))
    const = lambda shape: pl.BlockSpec(shape, lambda i: (0, 0))
    extra = [] if after is None else [after]
    return pl.pallas_call(
        (lambda *refs: body(*refs[:9], *refs[9 + len(extra):])), name="in_bwd", grid=(n,),
        in_specs=[row(d), row(d), const((1, d)), const((D_IN, d)), row(768), row(3072), row(512), row(512),
                  row(512)] + [const(a.shape) for a in extra],
        out_specs=[row(d), const((1, d))],
        out_shape=[SDS((t, d), F32), SDS((1, d), F32)],
        scratch_shapes=[pltpu.VMEM((8, d), F32)],
        compiler_params=_params(("arbitrary",)),
    )(dxo, x, g, w_t, dz_a, dz_m, dqr, dkr, dvr, *extra)


def _dw_in(h_t, dz_a, dz_m, dqr, dkr, dvr):
    d, t = h_t.shape
    tn = 256
    parts = [(0, 0, 0, 3), (1, 0, SEG["ga"][0] // tn, 2), (2, 0, SEG["qr"][0] // tn, 2),
             (3, 0, SEG["kr"][0] // tn, 2), (4, 0, SEG["vr"][0] // tn, 2), (1, 2, SEG["gr"][0] // tn, 10)]
    pieces = [dz_a, dz_m, dqr, dkr, dvr]

    def col_block(pi):
        mine = [(c0, r0, n) for q, c0, r0, n in parts if q == pi]

        def index(j):
            c0, r0, n = mine[0]
            blk = c0 + jnp.clip(j - r0, 0, n - 1)
            for c0, r0, n in mine[1:]:
                blk = jnp.where(j >= r0, c0 + jnp.clip(j - r0, 0, n - 1), blk)
            return 0, blk

        return index

    def body(h_ref, *refs):
        o_ref = refs[-1]
        j = pl.program_id(0)
        for pi, _, r0, n in parts:
            @pl.when(jnp.logical_and(j >= r0, j < r0 + n))
            def _(p_ref=refs[pi]):
                o_ref[...] = _dot(h_ref[...], p_ref[...]).T.astype(BF16)

    return pl.pallas_call(
        body, name="dw_in", grid=(D_IN // tn,),
        in_specs=[pl.BlockSpec((d, t), lambda j: (0, 0))] + [pl.BlockSpec((t, tn), col_block(pi)) for pi in range(5)],
        out_specs=pl.BlockSpec((tn, d), lambda j: (j, 0)),
        out_shape=SDS((D_IN, d), BF16),
        compiler_params=_params(("arbitrary",)),
    )(h_t, *pieces)


def _adamw_math(w, g, m, v):
    mn = ADAM_B1 * m + (1.0 - ADAM_B1) * g
    vn = ADAM_B2 * v + (1.0 - ADAM_B2) * (g * g)
    m_hat = mn / (1.0 - ADAM_B1 ** ADAM_STEP)
    v_hat = vn / (1.0 - ADAM_B2 ** ADAM_STEP)
    return -ADAM_LR * (m_hat / (jnp.sqrt(v_hat) + ADAM_EPS) + ADAM_WD * w), mn, vn


def _sum_adamw(recvs, w, m, v, lane0, tn, layer0=0, prev=None, own=None):
    _, r, c = w.shape
    j0 = lane0 // tn
    n = len(recvs)
    has_own = own is not None

    def body(*refs):
        mine_ref, refs = (refs[0], refs[1:]) if has_own else (None, refs)
        w_ref, m_ref, v_ref = refs[n:n + 3]
        g_ref, d_ref, mo_ref, vo_ref = refs[-4:]

        def run(r_ref):
            def slot(s):
                if has_own:
                    return jnp.where(mine_ref[0] == s, refs[n + 3][...], r_ref[s]).astype(F32)
                return r_ref[s].astype(F32)

            g = slot(0)
            for s in range(1, N_DEV):
                g = g + slot(s)
            g_ref[0] = g
            d_ref[0], mo_ref[0], vo_ref[0] = _adamw_math(w_ref[0], g, m_ref[0], v_ref[0])

        for i in range(n):
            pl.when(pl.program_id(0) == i)(functools.partial(run, refs[i]))

    slots = pl.BlockSpec((N_DEV, r, tn), lambda i, j, *_: (0, 0, j0 + j))
    blk = pl.BlockSpec((1, r, tn), lambda i, j, *_: (layer0 + i, 0, j))
    before = [] if prev is None else list(prev)
    in_specs, args = [slots] * n + [blk] * 3, [*recvs, w, m, v]
    if has_own:
        assert n == 1
        in_specs.append(pl.BlockSpec((r, tn), lambda i, j, mine: (mine[0], j0 + j)))
        args.append(own[0])
    n_pre = len(args) + has_own
    return pl.pallas_call(
        body, name="sum_adamw",
        grid_spec=pltpu.PrefetchScalarGridSpec(
            num_scalar_prefetch=int(has_own), grid=(n, c // tn),
            in_specs=in_specs + [ANY] * len(before), out_specs=[blk] * 4),
        out_shape=[SDS(w.shape, F32)] * 4,
        input_output_aliases={n_pre + k: k for k in range(len(before))},
        compiler_params=_params(("parallel", "parallel")),
    )(*([own[1]] if has_own else []), *args, *before)


def _adamw(w, g, m, v):
    rows, cols = w.shape
    tr = 256 if rows % 256 == 0 else rows

    def body(w_ref, g_ref, m_ref, v_ref, d_ref, mo_ref, vo_ref):
        d_ref[...], mo_ref[...], vo_ref[...] = _adamw_math(w_ref[...], g_ref[...], m_ref[...], v_ref[...])

    blk = pl.BlockSpec((tr, cols), lambda i: (i, 0))
    return pl.pallas_call(
        body, name="adamw", grid=(rows // tr,),
        in_specs=[blk] * 4, out_specs=[blk] * 3, out_shape=[SDS((rows, cols), F32)] * 3,
        compiler_params=_params(("parallel",)),
    )(w, g, m, v)


def _all_gather(shards):
    na = len(shards)
    chips = (4, 2, 6)

    def body(*refs):
        ins, outs = refs[:na], refs[na:2 * na]
        send_sems, recv_sems, local_sems = refs[2 * na:]
        _, mine = _flip(0)

        def rows(a, idx):
            r = shards[a].shape[0]
            return outs[a].at[pl.ds(pl.multiple_of(idx * r, 16), r), :]

        def copy(a, slot, block_idx, to, src=None):
            return pltpu.make_async_remote_copy(
                src_ref=rows(a, block_idx) if src is None else src, dst_ref=rows(a, block_idx),
                send_sem=send_sems.at[a, slot], recv_sem=recv_sems.at[a, slot],
                device_id=to, device_id_type=MESH_ID)

        sibling, sibling_idx = _flip(1)
        local, started = [], []
        for a in range(na):
            cp = pltpu.make_async_copy(ins[a], rows(a, mine), local_sems.at[a])
            cp.start()
            local.append(cp)
            first = [copy(a, 0, mine, sibling, src=ins[a])]
            first += [copy(a, 1 + j, mine, _flip(k)[0], src=ins[a]) for j, k in enumerate(chips)]
            for cp in first:
                cp.start()
            started += first
        for a in range(na):
            for j, k in enumerate(chips):
                _, theirs = _flip(k)
                copy(a, 1 + j, theirs, _flip(0)[0]).wait_recv()
                fwd = copy(a, 4 + j, theirs, sibling)
                fwd.start()
                started.append(fwd)
        for a in range(na):
            copy(a, 0, sibling_idx, _flip(0)[0]).wait_recv()
            for j, k in enumerate(chips):
                _, theirs = _flip(k | 1)
                copy(a, 4 + j, theirs, _flip(0)[0]).wait_recv()
        for cp in started:
            cp.wait_send()
        for cp in local:
            cp.wait()

    return pl.pallas_call(
        body, name="all_gather_weights",
        in_specs=[ANY] * na, out_specs=[ANY] * na,
        out_shape=[SDS((N_DEV * s.shape[0], s.shape[1]), s.dtype) for s in shards],
        scratch_shapes=[pltpu.SemaphoreType.DMA((na, 7)), pltpu.SemaphoreType.DMA((na, 7)),
                        pltpu.SemaphoreType.DMA((na,))],
        compiler_params=pltpu.CompilerParams(has_side_effects=True),
    )(*shards)


def _scatter_blocks_of(g_ref, rows, idx):
    return g_ref.at[pl.ds(pl.multiple_of(idx * rows, 16), rows), :]


def _scatter_start(g):
    rows = g.shape[0] // N_DEV
    land_shape = (N_DEV, rows, g.shape[1])

    def body(g_ref, land_ref, send_sems, recv_sems, g_thru, land_thru, token):
        _, mine = _flip(0)
        for k in range(1, N_DEV):
            peer, theirs = _flip(k)
            pltpu.make_async_remote_copy(
                src_ref=_scatter_blocks_of(g_ref, rows, theirs), dst_ref=land_ref.at[mine],
                send_sem=send_sems.at[k - 1], recv_sem=recv_sems.at[k - 1],
                device_id=peer, device_id_type=MESH_ID).start()
        token[...] = jnp.zeros_like(token)

    hbm, sem = pl.BlockSpec(memory_space=pltpu.HBM), pl.BlockSpec(memory_space=pltpu.SEMAPHORE)
    return pl.pallas_call(
        body, name="scatter_start",
        out_shape=(pltpu.SemaphoreType.DMA((N_DEV - 1,)), pltpu.SemaphoreType.DMA((N_DEV - 1,)),
                   pltpu.HBM(g.shape, g.dtype), pltpu.HBM(land_shape, g.dtype), SDS((8, 128), F32)),
        in_specs=(hbm, hbm), out_specs=(sem, sem, hbm, hbm, pl.BlockSpec(memory_space=pltpu.VMEM)),
        input_output_aliases={0: 2, 1: 3},
        compiler_params=pltpu.CompilerParams(has_side_effects=pltpu.SideEffectType.DATAFLOW_SIDE_EFFECTING),
    )(pltpu.with_memory_space_constraint(g, pltpu.HBM),
      pltpu.with_memory_space_constraint(lax.empty(land_shape, g.dtype), pltpu.HBM))


def _scatter_wait(send_sems, recv_sems, g_thru, land_thru, after):
    rows = g_thru.shape[0] // N_DEV

    def body(g_ref, land_ref, send_sems, recv_sems, *rest):
        me, _ = _flip(0)
        for k in range(1, N_DEV):
            _, theirs = _flip(k)
            copy = pltpu.make_async_remote_copy(
                src_ref=_scatter_blocks_of(g_ref, rows, theirs), dst_ref=land_ref.at[theirs],
                send_sem=send_sems.at[k - 1], recv_sem=recv_sems.at[k - 1],
                device_id=me, device_id_type=MESH_ID)
            copy.wait_send()
            copy.wait_recv()

    hbm, sem = pl.BlockSpec(memory_space=pltpu.HBM), pl.BlockSpec(memory_space=pltpu.SEMAPHORE)
    return pl.pallas_call(
        body, name="scatter_wait",
        out_shape=(pltpu.HBM(g_thru.shape, g_thru.dtype), pltpu.HBM(land_thru.shape, land_thru.dtype)),
        in_specs=(hbm, hbm, sem, sem) + (ANY,) * len(after), out_specs=(hbm, hbm), input_output_aliases={0: 0, 1: 1},
        compiler_params=pltpu.CompilerParams(has_side_effects=pltpu.SideEffectType.DATAFLOW_SIDE_EFFECTING),
    )(g_thru, land_thru, send_sems, recv_sems, *after)


def _all_reduce_small(packed):
    shape = packed.shape

    def body(p_ref, o_ref, slots, send_sems, recv_sems):
        me, mine = _flip(0)
        slots[mine] = p_ref[...]
        sends = []
        for k in range(1, N_DEV):
            peer, _ = _flip(k)
            cp = pltpu.make_async_remote_copy(
                src_ref=p_ref, dst_ref=slots.at[mine], send_sem=send_sems.at[k - 1], recv_sem=recv_sems.at[k - 1],
                device_id=peer, device_id_type=MESH_ID)
            cp.start()
            sends.append(cp)
        for k in range(1, N_DEV):
            _, theirs = _flip(k)
            pltpu.make_async_remote_copy(
                src_ref=p_ref, dst_ref=slots.at[theirs], send_sem=send_sems.at[k - 1],
                recv_sem=recv_sems.at[k - 1], device_id=me, device_id_type=MESH_ID).wait_recv()
        for cp in sends:
            cp.wait_send()
        acc = slots[0]
        for s in range(1, N_DEV):
            acc = acc + slots[s]
        o_ref[...] = acc

    vm = pl.BlockSpec(memory_space=pltpu.VMEM)
    return pl.pallas_call(
        body, name="all_reduce_small", in_specs=[vm], out_specs=vm, out_shape=SDS(shape, F32),
        scratch_shapes=[pltpu.VMEM((N_DEV,) + shape, F32), pltpu.SemaphoreType.DMA((7,)),
                        pltpu.SemaphoreType.DMA((7,))],
        compiler_params=pltpu.CompilerParams(has_side_effects=True),
    )(packed)


def _layer_fwd(x, p, tabs, ex):
    z, h_t, q, qt, k, v, vt = _in_proj(x, p["norm_g"], p["w_in_t"], p["qn"], p["kn"], tabs["ca"], tabs["sa"],
                                       tabs["ones"])
    oa, lse, *gathered = _attn_fwd(q, k, vt, ex)
    qrot, krot, vb, orr, on = _ret_fwd(z, p["lgf"], p["lgb"], p["gnw"], tabs["cr"], tabs["sr"])
    return z, h_t, q, qt, k, v, lse, oa, qrot, krot, vb, orr, on, gathered


def _layer_bwd(dxo, s, p, tabs, ex_attn, scatter_w_in):
    doa, don, dz_m, d_wout, d_wb_t = _merge_bwd(dxo, s["z"], s["oa"], s["on"], s["ya"], s["yb"], p["wb_t"], p["w_out"])
    dq_a, dk_a, dv_a, *recv_attn = _attn_bwd(s["q"], s["qt"], s["k"], s["v"], doa, s["oa"], s["lse"],
                                              ex_attn(d_wb_t, d_wout))
    dz_a, d_qn, d_kn = _attn_post_bwd(dq_a, dk_a, dv_a, s["z"], p["qn"], p["kn"], tabs["ca"], tabs["sa"],
                                      tabs["ones"])
    dq_r, dk_r, dv_r, d_gnw, d_lgf, d_lgb = _ret_bwd(s["qrot"], s["krot"], s["vb"], s["orr"], don, p["gnw"],
                                                     p["lgf"], p["lgb"])
    dqr, dkr, dvr = _ret_post_bwd(dq_r, dk_r, dv_r, tabs["cr"], tabs["sr"])
    buf = _dw_in(s["h_t"], dz_a, dz_m, dqr, dkr, dvr)
    pending, token = None, None
    if scatter_w_in:
        *pending, token = _scatter_start(buf)
    dx, d_norm_g = _in_bwd(dxo, s["x"], p["norm_g"], p["w_in_t"], dz_a, dz_m, dqr, dkr, dvr, token)
    grads = dict(w_in_t=buf, wb_t=d_wb_t, w_out=d_wout, norm_g=d_norm_g, gnw=d_gnw,
                 qn=d_qn.reshape(ATTN_Q_HEADS, ATTN_HEAD_DIM).sum(axis=0),
                 kn=d_kn.reshape(ATTN_KV_HEADS, ATTN_HEAD_DIM).sum(axis=0),
                 lgf=d_lgf[:, 0, 0], lgb=d_lgb[:, 0, 0])
    return dx, grads, recv_attn, pending


def _adamw_nd(w, g, m, v):
    shape = w.shape
    two_d = (1, shape[0]) if w.ndim == 1 else (-1, shape[-1])
    out = _adamw(w.reshape(two_d), g.reshape(two_d), m.reshape(two_d), v.reshape(two_d))
    return tuple(o.reshape(shape) for o in out)


def kernel(x, norm_g, w_in, attn_q_norm, attn_k_norm, ret_decay_fwd, ret_decay_bwd, ret_gn_w, w_branch_attn, w_branch_ret, w_out, final_norm_g, loss_target, m_norm_g, m_w_in, m_attn_q_norm, m_attn_k_norm, m_ret_decay_fwd, m_ret_decay_bwd, m_ret_gn_w, m_w_branch_attn, m_w_branch_ret, m_w_out, m_final_norm_g, v_norm_g, v_w_in, v_attn_q_norm, v_attn_k_norm, v_ret_decay_fwd, v_ret_decay_bwd, v_ret_gn_w, v_w_branch_attn, v_w_branch_ret, v_w_out, v_final_norm_g):
    t, d = x.shape[1], x.shape[2]
    x2, target = x[0], loss_target[0]

    w_in_sh, wb_sh, wout_sh = [], [], []
    for l in range(DEPTH):
        w_in_sh.append(jnp.swapaxes(w_in[l], 0, 1).astype(BF16))
        wb_sh.append(jnp.concatenate([w_branch_attn[l].T, w_branch_ret[l].T], axis=1).astype(BF16))
        wout_sh.append(w_out[l].astype(BF16))

    ca, sa = _rope_tables(t, ATTN_HEAD_DIM)
    cr, sr = _rope_tables(t, RET_HEAD_DIM)
    grp = jnp.arange(ATTN_WIDTH) // ATTN_HEAD_DIM
    tabs = dict(ca=jnp.tile(ca, (1, 2)), sa=jnp.tile(sa, (1, 2)), cr=cr, sr=sr,
                ones=jnp.where(grp[:, None] == grp[None, :], 1.0 / ATTN_HEAD_DIM, 0.0).astype(BF16))
    layers = []
    for l in range(DEPTH):
        layers.append(dict(
            norm_g=norm_g[l][None], qn=jnp.tile(attn_q_norm[l], ATTN_Q_HEADS)[None],
            kn=jnp.tile(attn_k_norm[l], ATTN_KV_HEADS)[None], gnw=ret_gn_w[l][None],
            lgf=jax.nn.log_sigmoid(ret_decay_fwd[l]), lgb=jax.nn.log_sigmoid(ret_decay_bwd[l])))

    layers[0]["w_in_t"], = _all_gather([w_in_sh[0]])
    gathers = [_Exchange("gather", [wb_sh[0], wout_sh[0], w_in_sh[1]]), _Exchange("gather", [wb_sh[1], wout_sh[1]])]
    h = x2
    saved = []
    for l in range(DEPTH):
        p = layers[l]
        z, h_t, q, qt, k, v, lse, oa, qrot, krot, vb, orr, on, got = _layer_fwd(h, p, tabs, gathers[l])
        p["wb_t"], p["w_out"] = got[0], got[1]
        if l == 0:
            layers[1]["w_in_t"] = got[2]
        xn, ya, yb = _merge_fwd(h, z, oa, on, p["wb_t"], p["w_out"])
        saved.append(dict(x=h, z=z, h_t=h_t, q=q, qt=qt, k=k, v=v, lse=lse, oa=oa, qrot=qrot, krot=krot, vb=vb,
                          orr=orr, on=on, ya=ya, yb=yb))
        h = xn
    dx, d_final_g, loss_part = _final_loss(h, final_norm_g[None], target)

    grads = [None] * DEPTH
    dx, grads[1], _, _ = _layer_bwd(dx, saved[1], layers[1], tabs, lambda *a: None, False)
    g1 = grads[1]
    ex_attn = lambda d_wb_t, d_wout: _Exchange("scatter", [g1["w_in_t"], g1["wb_t"], g1["w_out"], d_wb_t, d_wout])
    dx, grads[0], recv_attn, pending = _layer_bwd(dx, saved[0], layers[0], tabs, ex_attn, True)
    recv = [None, recv_attn[3], recv_attn[4], recv_attn[0], recv_attn[1], recv_attn[2]]
    tr = lambda a: jnp.swapaxes(a, 1, 2)
    w_in_t = (tr(w_in), tr(m_w_in), tr(v_w_in))
    sharded = {}
    w_in_l1 = _sum_adamw([recv[3]], *w_in_t, 0, 256, layer0=1)
    sharded[id(w_branch_attn)] = [tr(o) for o in _sum_adamw(
        [recv[1], recv[4]], tr(w_branch_attn), tr(m_w_branch_attn), tr(v_w_branch_attn), 0, 512)]
    sharded[id(w_branch_ret)] = [tr(o) for o in _sum_adamw(
        [recv[1], recv[4]], tr(w_branch_ret), tr(m_w_branch_ret), tr(v_w_branch_ret), 512, 512)]
    sharded[id(w_out)] = _sum_adamw([recv[2], recv[5]], w_out, m_w_out, v_w_out, 0, 256)
    g_wba, g_wbr, g_wout = (sharded[id(w)][0] for w in (w_branch_attn, w_branch_ret, w_out))

    packed = jnp.zeros((8, 1024), F32)
    for l in range(DEPTH):
        gl = grads[l]
        packed = packed.at[l].set(gl["norm_g"][0])
        packed = packed.at[2, 512 * l:512 * (l + 1)].set(gl["gnw"][0])
        packed = packed.at[4, 128 * l:128 * l + 64].set(gl["qn"])
        packed = packed.at[4, 256 + 128 * l:256 + 128 * l + 64].set(gl["kn"])
        packed = packed.at[4, 512 + 128 * l:512 + 128 * l + 4].set(gl["lgf"])
        packed = packed.at[4, 768 + 128 * l:768 + 128 * l + 4].set(gl["lgb"])
    packed = packed.at[3].set(d_final_g[0])
    packed = packed.at[5, 0].set(loss_part[0, 0])
    red = _all_reduce_small(packed)
    loss = red[5, 0]
    g_norm_g = red[0:2]
    g_gnw = red[2].reshape(DEPTH, RET_WIDTH)
    g_final = red[3]
    g_qn = jnp.stack([red[4, 128 * l:128 * l + 64] for l in range(DEPTH)])
    g_kn = jnp.stack([red[4, 256 + 128 * l:256 + 128 * l + 64] for l in range(DEPTH)])
    g_lgf = jnp.stack([red[4, 512 + 128 * l:512 + 128 * l + 4] for l in range(DEPTH)])
    g_lgb = jnp.stack([red[4, 768 + 128 * l:768 + 128 * l + 4] for l in range(DEPTH)])
    g_df = g_lgf * jax.nn.sigmoid(-ret_decay_fwd)
    g_db = g_lgb * jax.nn.sigmoid(-ret_decay_bwd)

    grad_w = [g_norm_g, None, g_qn, g_kn, g_df, g_db, g_gnw, g_wba, g_wbr, g_wout, g_final]
    weights = [norm_g, w_in, attn_q_norm, attn_k_norm, ret_decay_fwd, ret_decay_bwd, ret_gn_w, w_branch_attn,
               w_branch_ret, w_out, final_norm_g]
    ms = [m_norm_g, m_w_in, m_attn_q_norm, m_attn_k_norm, m_ret_decay_fwd, m_ret_decay_bwd, m_ret_gn_w,
          m_w_branch_attn, m_w_branch_ret, m_w_out, m_final_norm_g]
    vs = [v_norm_g, v_w_in, v_attn_q_norm, v_attn_k_norm, v_ret_decay_fwd, v_ret_decay_bwd, v_ret_gn_w,
          v_w_branch_attn, v_w_branch_ret, v_w_out, v_final_norm_g]
    upd = [None if w is w_in else sharded[id(w)][1:] if id(w) in sharded else _adamw_nd(w, g, m, v)
           for w, g, m, v in zip(weights, grad_w, ms, vs)]

    done = [dx, w_in_l1[0], g_wout] + [u[0] for w, u in zip(weights, upd) if u is not None and id(w) not in sharded]
    g_full, recv[0] = _scatter_wait(*pending, done)
    mine = (4 * lax.axis_index("x") + 2 * lax.axis_index("y") + lax.axis_index("c")).astype(jnp.int32)[None]
    w_in_upd = [tr(o) for o in _sum_adamw([recv[0]], *w_in_t, 0, 256, layer0=0, prev=w_in_l1, own=(g_full, mine))]
    grad_w[1], upd[1] = w_in_upd[0], w_in_upd[1:]
    return (loss, dx[None], *grad_w, *[u[0] for u in upd], *[u[1] for u in upd], *[u[2] for u in upd])
```

```python
import functools

import jax
import jax.numpy as jnp
from jax import lax
from jax.experimental import pallas as pl
from jax.experimental.pallas import tpu as pltpu

F32 = jnp.float32
BF16 = jnp.bfloat16
SDS = jax.ShapeDtypeStruct

D_MODEL = 1024
DEPTH = 2
GRID_W = 64
ATTN_Q_HEADS = 8
ATTN_KV_HEADS = 2
ATTN_HEAD_DIM = 64
ATTN_WIDTH = 512
ATTN_KV_WIDTH = 128
RET_HEADS = 4
RET_HEAD_DIM = 128
RET_WIDTH = 512
RET_CHUNK = 128
ATTN_KEY_CHUNK = 512
ATTN_BWD_KEY_CHUNK = 1024
ATTN_BWD_QUERY_TILE = 512
ATTN_FWD_QUERY_TILE = 512
QK_DOTS_PER_CHUNK = 4
EXP_LAG = 3
ROPE_THETA = 10000.0
EPS = 1e-6
D_IN = 5376
N_DEV = 8

ADAM_LR = 0.001
ADAM_B1 = 0.9
ADAM_B2 = 0.999
ADAM_EPS = 1e-08
ADAM_WD = 0.01
ADAM_STEP = 10

SEG = {
    "qa": (0, 512, 0),
    "ga": (768, 512, 512),
    "qr": (1280, 512, 1024),
    "kr": (1792, 512, 1536),
    "vr": (2304, 512, 2048),
    "gr": (2816, 512, 2560),
    "gm": (3328, 2048, 3072),
    "ka": (512, 128, 5120),
    "va": (640, 128, 5248),
}

VMEM_LIMIT = 60 * 1024 * 1024
NT = (((1,), (1,)), ((), ()))
TN = (((0,), (0,)), ((), ()))
MESH_ID = pl.DeviceIdType.MESH
ANY = pl.BlockSpec(memory_space=pl.ANY)


def _params(sem=None, vmem=VMEM_LIMIT):
    return pltpu.CompilerParams(dimension_semantics=sem, vmem_limit_bytes=vmem)


def _dot(a, b, dims=None):
    if dims is None:
        return jnp.dot(a, b, preferred_element_type=F32)
    return lax.dot_general(a, b, dims, preferred_element_type=F32)


def _sigmoid(x):
    return 1.0 / (1.0 + jnp.exp(-x))


def _swap_halves(x, q):
    n = x.shape[-1]
    axis = x.ndim - 1
    lane = lax.broadcasted_iota(jnp.int32, x.shape, axis)
    first = (lane % (2 * q)) < q
    return jnp.where(first, pltpu.roll(x, n - q, axis), pltpu.roll(x, q, axis))


def _rope(x, cos, sin_signed, q):
    return x * cos + _swap_halves(x, q) * sin_signed


def _rope_bwd(dy, cos, sin_signed, q):
    return dy * cos - _swap_halves(dy, q) * sin_signed


def _group_mean(v, ones_bd):
    hi = v.astype(BF16)
    lo = (v - hi.astype(F32)).astype(BF16)
    return _dot(hi, ones_bd) + _dot(lo, ones_bd)


def _rope_tables(t, head_dim):
    n_rows = t // GRID_W
    d_axis = head_dim // 2
    inv_freq = ROPE_THETA ** (-jnp.arange(0, d_axis, 2, dtype=F32) / d_axis)
    ar = jnp.arange(n_rows, dtype=F32)[:, None] * inv_freq
    ac = jnp.arange(GRID_W, dtype=F32)[:, None] * inv_freq
    by_row = lambda a: jnp.repeat(a, GRID_W, axis=0)
    by_col = lambda a: jnp.tile(a, (n_rows, 1))
    cr, sr, cc, sc = by_row(jnp.cos(ar)), by_row(jnp.sin(ar)), by_col(jnp.cos(ac)), by_col(jnp.sin(ac))
    return jnp.concatenate([cr, cr, cc, cc], axis=-1), jnp.concatenate([-sr, sr, -sc, sc], axis=-1)


def _me():
    return lax.axis_index("x"), lax.axis_index("y"), lax.axis_index("c")


def _flip(k):
    x, y, c = _me()
    px = 1 - x if k & 4 else x
    py = 1 - y if k & 2 else y
    pc = 1 - c if k & 1 else c
    return (px, py, pc), 4 * px + 2 * py + pc


class _Exchange:
    def __init__(self, kind, srcs):
        self.kind, self.srcs, self.n = kind, list(srcs), len(srcs)
        self.rows = [a.shape[0] if kind == "gather" else a.shape[0] // N_DEV for a in srcs]
        if kind == "gather":
            self.out_shape = [SDS((N_DEV * a.shape[0], a.shape[1]), a.dtype) for a in srcs]
        else:
            self.out_shape = [SDS((N_DEV, a.shape[0] // N_DEV, a.shape[1]), a.dtype) for a in srcs]
        self.scratch = [pltpu.SemaphoreType.DMA((self.n, N_DEV - 1)), pltpu.SemaphoreType.DMA((self.n, N_DEV - 1)),
                        pltpu.SemaphoreType.DMA((self.n,))]

    def _block(self, ref, a, idx):
        r = self.rows[a]
        return ref.at[pl.ds(pl.multiple_of(idx * r, 16), r), :]

    def _src(self, ins, a, idx):
        return ins[a] if self.kind == "gather" else self._block(ins[a], a, idx)

    def _dst(self, outs, a, idx):
        return self._block(outs[a], a, idx) if self.kind == "gather" else outs[a].at[idx]

    def _copies(self, ins, outs, sems):
        send_sems, recv_sems, local_sems = sems
        me, mine = _flip(0)
        local, sends, recvs = [], [], []
        for a in range(self.n):
            local.append(pltpu.make_async_copy(self._src(ins, a, mine), self._dst(outs, a, mine), local_sems.at[a]))
            for k in range(1, N_DEV):
                peer, theirs = _flip(k)
                sem = dict(send_sem=send_sems.at[a, k - 1], recv_sem=recv_sems.at[a, k - 1])
                sends.append(pltpu.make_async_remote_copy(
                    src_ref=self._src(ins, a, theirs), dst_ref=self._dst(outs, a, mine),
                    device_id=peer, device_id_type=MESH_ID, **sem))
                recvs.append(pltpu.make_async_remote_copy(
                    src_ref=self._dst(outs, a, theirs), dst_ref=self._dst(outs, a, theirs),
                    device_id=me, device_id_type=MESH_ID, **sem))
        return local, sends, recvs

    def start(self, ins, outs, sems):
        local, sends, _ = self._copies(ins, outs, sems)
        for cp in local + sends:
            cp.start()

    def wait(self, ins, outs, sems):
        local, sends, recvs = self._copies(ins, outs, sems)
        for cp in sends:
            cp.wait_send()
        for cp in recvs:
            cp.wait_recv()
        for cp in local:
            cp.wait()


def _with_exchange(body, n_in, n_out, n_scratch, ex, first, last):
    if ex is None:
        return body

    def wrapped(*refs):
        ins = refs[:n_in]
        ex_ins = refs[n_in:n_in + ex.n]
        outs = refs[n_in + ex.n:n_in + ex.n + n_out]
        ex_outs = refs[n_in + ex.n + n_out:n_in + 2 * ex.n + n_out]
        rest = refs[n_in + 2 * ex.n + n_out:]
        scratch, sems = rest[:n_scratch], rest[n_scratch:]

        @pl.when(first())
        def _():
            ex.start(ex_ins, ex_outs, sems)

        body(*ins, *outs, *scratch)

        @pl.when(last())
        def _():
            ex.wait(ex_ins, ex_outs, sems)

    return wrapped


def _ex_args(ex):
    if ex is None:
        return [], [], [], [], []
    return [ANY] * ex.n, [ANY] * ex.n, list(ex.out_shape), list(ex.scratch), list(ex.srcs)


def _in_proj(x, g, w_t, qn, kn, cos, sin, ones_bd, cos_r, sin_r):
    t, d = x.shape
    tm = min(256, t)
    tk = min(ATTN_KEY_CHUNK, t)
    per_chunk = tk // tm
    hd = ATTN_HEAD_DIM

    def body(x_ref, g_ref, w_ref, qn_ref, kn_ref, c_ref, s_ref, b_ref, cr_ref, sr_ref,
             z_ref, ht_ref, q_out, qt_out, k_out, v_out, vt_out, qr_out, kr_out, vr_out):
        xv = x_ref[...]
        r = lax.rsqrt(jnp.mean(xv * xv, axis=-1, keepdims=True) + EPS)
        h = xv * r * g_ref[...]
        ht_ref[...] = h.T.astype(BF16)
        hb = h.astype(BF16)
        def project(name):
            nat, w, off = SEG[name]
            zs = _dot(hb, w_ref[nat:nat + w, :], NT)
            z_ref[:, off:off + w] = zs
            return zs

        seg = {name: project(name) for name in ("qa", "ka", "va")}
        bd = b_ref[...]
        c2, s2 = c_ref[...], s_ref[...]
        cq = jnp.concatenate([c2] * 4, axis=-1)
        sq = jnp.concatenate([s2] * 4, axis=-1)
        xq, xk, xvv = seg["qa"], seg["ka"], seg["va"]
        yq = xq * lax.rsqrt(_group_mean(xq * xq, bd) + EPS) * qn_ref[...]
        yq = _rope(yq, cq, sq, hd // 4) * (hd ** -0.5)
        yqt = yq.T
        for hh in range(ATTN_Q_HEADS):
            q_out[hh] = yq[:, hh * hd:(hh + 1) * hd].astype(BF16)
            qt_out[hh] = yqt[hh * hd:(hh + 1) * hd, :].astype(BF16)
        yk = xk * lax.rsqrt(_group_mean(xk * xk, bd[:ATTN_KV_WIDTH, :ATTN_KV_WIDTH]) + EPS) * kn_ref[...]
        yk = _rope(yk, c2, s2, hd // 4)
        xvt = xvv.T
        ones = jnp.ones((hd, tm), F32)
        for hh in range(ATTN_KV_HEADS):
            k_out[hh] = yk[:, hh * hd:(hh + 1) * hd].astype(BF16)
            v_out[hh] = xvv[:, hh * hd:(hh + 1) * hd].astype(BF16)
            vt_out[hh, 0] = jnp.concatenate([xvt[hh * hd:(hh + 1) * hd, :], ones], axis=0).astype(BF16)
        rd = RET_HEAD_DIM
        cr = jnp.concatenate([cr_ref[...]] * RET_HEADS, axis=-1)
        sr = jnp.concatenate([sr_ref[...]] * RET_HEADS, axis=-1)
        qr_out[...] = _rope(project("qr"), cr, sr, rd // 4).astype(BF16)
        kr_out[...] = (_rope(project("kr"), cr, sr, rd // 4) * (rd ** -0.5)).astype(BF16)
        vr_out[...] = project("vr").astype(BF16)
        for name in ("ga", "gr", "gm"):
            project(name)

    const = lambda shape: pl.BlockSpec(shape, lambda i: (0,) * len(shape))
    rows = lambda w: pl.BlockSpec((tm, w), lambda i: (i, 0))
    return pl.pallas_call(
        body, name="in_proj", grid=(t // tm,),
        in_specs=[rows(d), const((1, d)), const((D_IN, d)), const((1, 512)), const((1, 128)), rows(128), rows(128),
                  const((512, 512)), rows(128), rows(128)],
        out_specs=[rows(D_IN), pl.BlockSpec((d, tm), lambda i: (0, i)),
                   pl.BlockSpec((ATTN_Q_HEADS, tm, hd), lambda i: (0, i, 0)),
                   pl.BlockSpec((ATTN_Q_HEADS, hd, tm), lambda i: (0, 0, i)),
                   pl.BlockSpec((ATTN_KV_HEADS, tm, hd), lambda i: (0, i, 0)),
                   pl.BlockSpec((ATTN_KV_HEADS, tm, hd), lambda i: (0, i, 0)),
                   pl.BlockSpec((ATTN_KV_HEADS, 1, 2 * hd, tm), lambda i: (0, i // per_chunk, 0, i % per_chunk)),
                   rows(RET_WIDTH), rows(RET_WIDTH), rows(RET_WIDTH)],
        out_shape=[SDS((t, D_IN), F32), SDS((d, t), BF16),
                   SDS((ATTN_Q_HEADS, t, hd), BF16), SDS((ATTN_Q_HEADS, hd, t), BF16),
                   SDS((ATTN_KV_HEADS, t, hd), BF16), SDS((ATTN_KV_HEADS, t, hd), BF16),
                   SDS((ATTN_KV_HEADS, t // tk, 2 * hd, tk), BF16)] + [SDS((t, RET_WIDTH), BF16)] * 3,
        compiler_params=_params(("parallel",)),
    )(x, g, w_t, qn, kn, cos, sin, ones_bd, cos_r, sin_r)


def _attn_fwd(q, k, vt, ex=None):
    t = q.shape[1]
    tq = min(ATTN_FWD_QUERY_TILE, t)
    nk, tk = vt.shape[1], vt.shape[3]
    hd = ATTN_HEAD_DIM
    g = ATTN_Q_HEADS // ATTN_KV_HEADS

    def body(q_ref, k_ref, vt_ref, o_ref, lse_ref, s_scr):
        def pass_a(h, c, m8):
            part = tk // QK_DOTS_PER_CHUNK
            for lo in range(c * tk, (c + 1) * tk, part):
                st = _dot(k_ref[0, lo:lo + part, :], q_ref[h], NT)
                s_scr[h % 2, lo:lo + part, :] = st
                m8 = jnp.maximum(m8, jnp.max(st.reshape(part // 8, 8, tq), axis=0))
            return m8

        def pass_b(h, c, m, acc, after):
            e = jnp.exp(s_scr[h % 2, c * tk:(c + 1) * tk, :] - (m + after * 0.0)).astype(BF16)
            return acc + _dot(vt_ref[0, c], e)

        neg = jnp.full((8, tq), -jnp.inf, F32)
        m8 = neg
        for c in range(nk):
            m8 = pass_a(0, c, m8)
        outs = []
        for h in range(g):
            m = jnp.max(m8, axis=0, keepdims=True)
            acc = jnp.zeros((2 * hd, tq), F32)
            m8 = neg
            done = [m] * EXP_LAG
            for c in range(nk):
                if h + 1 < g:
                    m8 = pass_a(h + 1, c, m8)
                acc = pass_b(h, c, m, acc, done[-EXP_LAG])
                done.append(m8[0:1, :] if h + 1 < g else acc[hd:hd + 1, :])
            l = acc[hd:hd + 1, :]
            outs.append((acc[:hd, :] / l).T)
            lse_ref[h] = m + jnp.log(l)
        o_ref[...] = jnp.concatenate(outs, axis=-1)

    nq = t // tq
    first = lambda: jnp.logical_and(pl.program_id(0) == 0, pl.program_id(1) == 0)
    last = lambda: jnp.logical_and(pl.program_id(0) == ATTN_KV_HEADS - 1, pl.program_id(1) == nq - 1)
    xi, xo, xs, xscr, xargs = _ex_args(ex)
    return pl.pallas_call(
        _with_exchange(body, 3, 2, 1, ex, first, last), name="attn_fwd", grid=(ATTN_KV_HEADS, nq),
        in_specs=[pl.BlockSpec((g, tq, hd), lambda p, i: (p, i, 0)),
                  pl.BlockSpec((1, t, hd), lambda p, i: (p, 0, 0)),
                  pl.BlockSpec((1, nk, 2 * hd, tk), lambda p, i: (p, 0, 0, 0))] + xi,
        out_specs=[pl.BlockSpec((tq, g * hd), lambda p, i: (i, p)),
                   pl.BlockSpec((g, 1, tq), lambda p, i: (p, 0, i))] + xo,
        out_shape=[SDS((t, ATTN_WIDTH), F32), SDS((ATTN_Q_HEADS, 1, t), F32)] + xs,
        scratch_shapes=[pltpu.VMEM((2, t, tq), F32)] + xscr,
        compiler_params=_params(("arbitrary", "arbitrary")),
    )(q, k, vt, *xargs)


class _Dir:
    def __init__(self, lg, strict_future):
        c = RET_CHUNK
        ia = lax.broadcasted_iota(jnp.int32, (c, c), 0).astype(F32)
        ib = lax.broadcasted_iota(jnp.int32, (c, c), 1).astype(F32)
        col = lax.broadcasted_iota(jnp.int32, (c, 1), 0).astype(F32)
        row = lax.broadcasted_iota(jnp.int32, (1, c), 1).astype(F32)
        if strict_future:
            dist = ib - ia
            mask = dist > 0
            self.wq, self.wk, wk_row = c - col, col, row
        else:
            dist = ia - ib
            mask = dist >= 0
            self.wq, self.wk, wk_row = col + 1.0, c - 1.0 - col, c - 1.0 - row
        self.dist = jnp.maximum(dist, 0.0)
        self.d = jnp.where(mask, jnp.exp(self.dist * lg), 0.0)
        self.qd = jnp.exp(self.wq * lg)
        self.kd_col = jnp.exp(self.wk * lg)
        self.kd_row = jnp.exp(wk_row * lg)
        self.cd = jnp.exp(jnp.full((1, 1), float(c), F32) * lg)


def _ret_fwd(qrot, krot, vb, lgf, lgb, gnw):
    t = qrot.shape[0]
    c = RET_CHUNK
    nc = t // c
    hd = RET_HEAD_DIM
    unroll = 4 if nc % 4 == 0 else 1

    def body(lgf_ref, lgb_ref, qo_ref, ko_ref, vo_ref, w_ref, orr_ref, on_ref, kt, uf, ub, sfa, sba):
        h = pl.program_id(0)
        fw = _Dir(lgf_ref[h], False)
        bw = _Dir(lgb_ref[h], True)
        for i in range(nc):
            kt[i] = ko_ref[i * c:(i + 1) * c, :].astype(F32).T.astype(BF16)

        def rows(ci):
            return pl.ds(pl.multiple_of(ci * c, c), c)

        def kv_products(ci, carry):
            vv = vo_ref[rows(ci), :]
            ktf = kt[ci].astype(F32)
            uf[ci] = _dot((ktf * fw.kd_row).astype(BF16), vv)
            ub[ci] = _dot((ktf * bw.kd_row).astype(BF16), vv)
            return carry

        lax.fori_loop(0, nc, kv_products, 0, unroll=unroll)

        def scan(i, carry):
            sf, sb = carry
            j = nc - 1 - i
            sfa[i] = sf.astype(BF16)
            sba[j] = sb.astype(BF16)
            return sf * fw.cd + uf[i], sb * bw.cd + ub[j]

        zero = jnp.zeros((hd, hd), F32)
        lax.fori_loop(0, nc, scan, (zero, zero))
        gw = w_ref[...]

        def outputs(ci, carry):
            sl = rows(ci)
            qq, kk, vv = qo_ref[sl, :], ko_ref[sl, :], vo_ref[sl, :]
            a = _dot(qq, kk, NT)
            o = (_dot((a * fw.d).astype(BF16), vv) + _dot(qq, sfa[ci]) * fw.qd
                 + _dot((a * bw.d).astype(BF16), vv) + _dot(qq, sba[ci]) * bw.qd)
            orr_ref[sl, :] = o
            xc = o - jnp.mean(o, axis=-1, keepdims=True)
            var = jnp.mean(xc * xc, axis=-1, keepdims=True)
            on_ref[sl, :] = xc * lax.rsqrt(var + EPS) * gw
            return carry

        group = 32 if nc % 32 == 0 else 1

        def output_group(i, carry):
            for j in range(group):
                outputs(i * group + j, carry)
            return carry

        lax.fori_loop(0, nc // group, output_group, 0)

    smem = pl.BlockSpec(memory_space=pltpu.SMEM)
    head = pl.BlockSpec((t, 128), lambda h: (0, h))
    return pl.pallas_call(
        body, name="ret_fwd", grid=(RET_HEADS,),
        in_specs=[smem, smem, head, head, head, pl.BlockSpec((1, 128), lambda h: (0, h))],
        out_specs=[head, head],
        out_shape=[SDS((t, RET_WIDTH), F32)] * 2,
        scratch_shapes=[pltpu.VMEM((nc, hd, c), BF16), pltpu.VMEM((nc, hd, hd), F32), pltpu.VMEM((nc, hd, hd), F32),
                        pltpu.VMEM((nc, hd, hd), BF16), pltpu.VMEM((nc, hd, hd), BF16)],
        compiler_params=_params(("parallel",)),
    )(lgf, lgb, qrot, krot, vb, gnw)


def _merge_fwd(x, z, oa, on, wb_t, wout):
    t, d = x.shape
    tm = min(256, t)

    def body(x_ref, ga_ref, gr_ref, gm0_ref, gm1_ref, oa_ref, on_ref, wb_ref, wo_ref, xn_ref, ya_ref, yb_ref):
        ga, gr = ga_ref[...], gr_ref[...]
        ua = ga * _sigmoid(ga) * oa_ref[...]
        ub = gr * _sigmoid(gr) * on_ref[...]
        ya = _dot(ua.astype(BF16), wb_ref[:, :512], NT)
        yb = _dot(ub.astype(BF16), wb_ref[:, 512:], NT)
        ya_ref[...] = ya
        yb_ref[...] = yb
        merged = _sigmoid(gm0_ref[...]) * ya + _sigmoid(gm1_ref[...]) * yb
        xn_ref[...] = x_ref[...] + _dot(merged.astype(BF16), wo_ref[...])

    row = lambda w, j: pl.BlockSpec((tm, w), lambda i: (i, j))
    const = lambda shape: pl.BlockSpec(shape, lambda i: (0, 0))
    return pl.pallas_call(
        body, name="merge_fwd", grid=(t // tm,),
        in_specs=[row(d, 0), row(512, SEG["ga"][2] // 512), row(512, SEG["gr"][2] // 512),
                  row(1024, SEG["gm"][2] // 1024), row(1024, SEG["gm"][2] // 1024 + 1),
                  row(512, 0), row(512, 0), const((d, 1024)), const((d, d))],
        out_specs=[row(d, 0), row(d, 0), row(d, 0)],
        out_shape=[SDS((t, d), F32)] * 3,
        compiler_params=_params(("parallel",)),
    )(x, z, z, z, z, oa, on, wb_t, wout)


def _final_loss(x, g, target):
    t, d = x.shape
    tm = min(512, t)
    n = t // tm

    def body(x_ref, g_ref, t_ref, dx_ref, dg_ref, loss_ref, acc_g, acc_l):
        i = pl.program_id(0)

        @pl.when(i == 0)
        def _():
            acc_g[...] = jnp.zeros_like(acc_g)
            acc_l[...] = jnp.zeros_like(acc_l)

        xv, gv = x_ref[...], g_ref[...]
        r = lax.rsqrt(jnp.mean(xv * xv, axis=-1, keepdims=True) + EPS)
        xh = xv * r
        err = xh * gv - t_ref[...]
        dy = err * (1.0 / d)
        gy = dy * gv
        dx_ref[...] = r * (gy - xh * jnp.mean(gy * xh, axis=-1, keepdims=True))
        acc_g[...] += jnp.sum((dy * xh).reshape(tm // 8, 8, d), axis=0)
        acc_l[...] += jnp.sum((err * err).reshape(tm // 8, 8, d), axis=0)

        @pl.when(i == n - 1)
        def _():
            dg_ref[...] = jnp.sum(acc_g[...], axis=0, keepdims=True)
            tot = jnp.sum(jnp.sum(acc_l[...], axis=0, keepdims=True), axis=1, keepdims=True)
            loss_ref[...] = jnp.broadcast_to(tot * (0.5 / d), (1, 128))

    return pl.pallas_call(
        body, name="final_loss", grid=(n,),
        in_specs=[pl.BlockSpec((tm, d), lambda i: (i, 0)), pl.BlockSpec((1, d), lambda i: (0, 0)),
                  pl.BlockSpec((tm, d), lambda i: (i, 0))],
        out_specs=[pl.BlockSpec((tm, d), lambda i: (i, 0)), pl.BlockSpec((1, d), lambda i: (0, 0)),
                   pl.BlockSpec((1, 128), lambda i: (0, 0))],
        out_shape=[SDS((t, d), F32), SDS((1, d), F32), SDS((1, 128), F32)],
        scratch_shapes=[pltpu.VMEM((8, d), F32), pltpu.VMEM((8, d), F32)],
        compiler_params=_params(("arbitrary",)),
    )(x, g, target)


def _merge_bwd(dxo, z, oa, on, ya, yb, wb_t, wout):
    t, d = dxo.shape
    tm = min(256, t)
    n = t // tm

    def body(dx_ref, ga_ref, gr_ref, gm0_ref, gm1_ref, oa_ref, on_ref, ya_ref, yb_ref, wb_ref, wo_ref,
             doa_ref, don_ref, dz_ref, dwo_ref, dwb_ref, acc_o, acc_b):
        i = pl.program_id(0)

        @pl.when(i == 0)
        def _():
            acc_o[...] = jnp.zeros_like(acc_o)
            acc_b[...] = jnp.zeros_like(acc_b)

        dxb = dx_ref[...].astype(BF16)
        ya, yb = ya_ref[...], yb_ref[...]
        g0, g1 = _sigmoid(gm0_ref[...]), _sigmoid(gm1_ref[...])
        mb = (g0 * ya + g1 * yb).astype(BF16)
        dm = _dot(dxb, wo_ref[...], NT)
        dya = (dm * g0).astype(BF16)
        dyb = (dm * g1).astype(BF16)
        dz_ref[:, 1024:2048] = (dm * ya * g0 * (1.0 - g0)).astype(BF16)
        dz_ref[:, 2048:3072] = (dm * yb * g1 * (1.0 - g1)).astype(BF16)

        def branch(g_ref, o_ref, dy, w, do_ref, lo):
            gv, ov = g_ref[...], o_ref[...]
            sg = _sigmoid(gv)
            silu = gv * sg
            du = _dot(dy, w)
            do_ref[...] = du * silu
            dz_ref[:, lo:lo + 512] = (du * ov * (sg * (1.0 + gv * (1.0 - sg)))).astype(BF16)
            acc_b[:, lo:lo + 512] += _dot(dy, (silu * ov).astype(BF16), TN)

        branch(ga_ref, oa_ref, dya, wb_ref[:, :512], doa_ref, 0)
        branch(gr_ref, on_ref, dyb, wb_ref[:, 512:], don_ref, 512)
        acc_o[...] += _dot(mb, dxb, TN)

        @pl.when(i == n - 1)
        def _():
            dwo_ref[...] = acc_o[...].astype(BF16)
            dwb_ref[...] = acc_b[...].astype(BF16)

    row = lambda w, j: pl.BlockSpec((tm, w), lambda i: (i, j))
    const = lambda shape: pl.BlockSpec(shape, lambda i: (0, 0))
    return pl.pallas_call(
        body, name="merge_bwd", grid=(n,),
        in_specs=[row(d, 0), row(512, SEG["ga"][2] // 512), row(512, SEG["gr"][2] // 512),
                  row(1024, SEG["gm"][2] // 1024), row(1024, SEG["gm"][2] // 1024 + 1),
                  row(512, 0), row(512, 0), row(d, 0), row(d, 0), const((d, 1024)), const((d, d))],
        out_specs=[row(512, 0), row(512, 0), row(3072, 0), const((d, d)), const((d, 1024))],
        out_shape=[SDS((t, 512), F32), SDS((t, 512), F32), SDS((t, 3072), BF16), SDS((d, d), BF16),
                   SDS((d, 1024), BF16)],
        scratch_shapes=[pltpu.VMEM((d, d), F32), pltpu.VMEM((d, 1024), F32)],
        compiler_params=_params(("arbitrary",)),
    )(dxo, z, z, z, z, oa, on, ya, yb, wb_t, wout)


def _ret_bwd(qrot, krot, vb, orr, don, gnw, lgf, lgb):
    t = qrot.shape[0]
    c = RET_CHUNK
    nc = t // c
    hd = RET_HEAD_DIM
    unroll = 4 if nc % 4 == 0 else 1

    def body(lgf_ref, lgb_ref, q_ref, k_ref, v_ref, o_ref, dn_ref, w_ref,
             dq_ref, dk_ref, dv_ref, dw_ref, dlf_ref, dlb_ref, qt, kt, dob, uf, ub, wf, wb, sfa, sba, gfa, gba):
        h = pl.program_id(0)
        fw = _Dir(lgf_ref[h], False)
        bw = _Dir(lgb_ref[h], True)
        fw.dt, bw.dt = fw.d.T, bw.d.T

        o = o_ref[...]
        xc = o - jnp.mean(o, axis=-1, keepdims=True)
        r = lax.rsqrt(jnp.mean(xc * xc, axis=-1, keepdims=True) + EPS)
        xh = xc * r
        dn = dn_ref[...]
        gy = dn * w_ref[...]
        d_o = r * (gy - jnp.mean(gy, axis=-1, keepdims=True) - xh * jnp.mean(gy * xh, axis=-1, keepdims=True))
        dw_ref[...] = jnp.sum(dn * xh, axis=0, keepdims=True)
        dob[...] = d_o.astype(BF16)
        for i in range(nc):
            qt[i] = q_ref[i * c:(i + 1) * c, :].astype(F32).T.astype(BF16)
            kt[i] = k_ref[i * c:(i + 1) * c, :].astype(F32).T.astype(BF16)

        def rows(ci):
            return pl.ds(pl.multiple_of(ci * c, c), c)

        def products(ci, carry):
            sl = rows(ci)
            vv, do32 = v_ref[sl, :], dob[sl, :].astype(F32)
            ktf = kt[ci].astype(F32)
            uf[ci] = _dot((ktf * fw.kd_row).astype(BF16), vv)
            ub[ci] = _dot((ktf * bw.kd_row).astype(BF16), vv)
            wf[ci] = _dot(qt[ci], (do32 * fw.qd).astype(BF16))
            wb[ci] = _dot(qt[ci], (do32 * bw.qd).astype(BF16))
            return carry

        lax.fori_loop(0, nc, products, 0, unroll=unroll)

        def scan(i, carry):
            sf, sb, gf, gb = carry
            j = nc - 1 - i
            sfa[i] = sf.astype(BF16)
            sba[j] = sb.astype(BF16)
            gfa[j] = gf.astype(BF16)
            gba[i] = gb.astype(BF16)
            return sf * fw.cd + uf[i], sb * bw.cd + ub[j], gf * fw.cd + wf[j], gb * bw.cd + wb[i]

        zero = jnp.zeros((hd, hd), F32)
        lax.fori_loop(0, nc, scan, (zero, zero, zero, zero))

        def one_dir(p, s_all, g_all, ci, qq, kk, vv, do, a, bm):
            sb, gb = s_all[ci], g_all[ci]
            doq = (do.astype(F32) * p.qd).astype(BF16)
            dqc = _dot(doq, sb, NT)
            kkd = (kk.astype(F32) * p.kd_col).astype(BF16)
            dk2 = _dot(vv, gb, NT) * p.kd_col
            terms = (p.dist * p.d * a * bm + p.wq * qq.astype(F32) * dqc + p.wk * kk.astype(F32) * dk2
                     + (float(c) * p.cd) * gb.astype(F32) * sb.astype(F32))
            return dqc, dk2, _dot(kkd, gb), terms

        d_both, dt_both = fw.d + bw.d, fw.dt + bw.dt

        def chunk(ci, carry):
            af, ab = carry
            sl = rows(ci)
            qq, kk, vv, do = q_ref[sl, :], k_ref[sl, :], v_ref[sl, :], dob[sl, :]
            a, bm = _dot(qq, kk, NT), _dot(do, vv, NT)
            at, bt = _dot(kk, qq, NT), _dot(vv, do, NT)
            dqf, dkf, dvf, tf = one_dir(fw, sfa, gfa, ci, qq, kk, vv, do, a, bm)
            dqb, dkb, dvb, tb = one_dir(bw, sba, gba, ci, qq, kk, vv, do, a, bm)
            dq_ref[sl, :] = _dot((bm * d_both).astype(BF16), kk) + dqf + dqb
            dk_ref[sl, :] = _dot((bt * dt_both).astype(BF16), qq) + dkf + dkb
            dv_ref[sl, :] = _dot((at * dt_both).astype(BF16), do) + dvf + dvb
            return af + tf, ab + tb

        pair = 8 if nc % 8 == 0 else 1

        def chunks(i, carry):
            for j in range(pair):
                carry = chunk(i * pair + j, carry)
            return carry

        af, ab = lax.fori_loop(0, nc // pair, chunks, (zero, zero))
        tot = lambda m: jnp.sum(jnp.sum(m, axis=0, keepdims=True), axis=1, keepdims=True)
        dlf_ref[...] = jnp.broadcast_to(tot(af).reshape(1, 1, 1), (1, 8, 128))
        dlb_ref[...] = jnp.broadcast_to(tot(ab).reshape(1, 1, 1), (1, 8, 128))

    smem = pl.BlockSpec(memory_space=pltpu.SMEM)
    head = pl.BlockSpec((t, 128), lambda h: (0, h))
    vec = pl.BlockSpec((1, 128), lambda h: (0, h))
    scal = pl.BlockSpec((1, 8, 128), lambda h: (h, 0, 0))
    mats = lambda dt: pltpu.VMEM((nc, hd, hd), dt)
    return pl.pallas_call(
        body, name="ret_bwd", grid=(RET_HEADS,),
        in_specs=[smem, smem, head, head, head, head, head, vec],
        out_specs=[head, head, head, vec, scal, scal],
        out_shape=[SDS((t, RET_WIDTH), F32)] * 3 + [SDS((1, RET_WIDTH), F32), SDS((RET_HEADS, 8, 128), F32),
                                                   SDS((RET_HEADS, 8, 128), F32)],
        scratch_shapes=[pltpu.VMEM((nc, hd, c), BF16), pltpu.VMEM((nc, hd, c), BF16), pltpu.VMEM((t, hd), BF16),
                        mats(F32), mats(F32), mats(F32), mats(F32), mats(BF16), mats(BF16), mats(BF16), mats(BF16)],
        compiler_params=_params(("parallel",)),
    )(lgf, lgb, qrot, krot, vb, orr, don, gnw)


def _ret_post_bwd(dq, dk, dv, cos, sin):
    t = dq.shape[0]
    tm = min(512, t)
    hd = RET_HEAD_DIM

    def body(dq_ref, dk_ref, dv_ref, c_ref, s_ref, oq_ref, ok_ref, ov_ref):
        cc = jnp.concatenate([c_ref[...]] * 4, axis=-1)
        ss = jnp.concatenate([s_ref[...]] * 4, axis=-1)
        oq_ref[...] = _rope_bwd(dq_ref[...], cc, ss, hd // 4).astype(BF16)
        ok_ref[...] = (_rope_bwd(dk_ref[...], cc, ss, hd // 4) * (hd ** -0.5)).astype(BF16)
        ov_ref[...] = dv_ref[...].astype(BF16)

    blk = pl.BlockSpec((tm, 512), lambda i: (i, 0))
    tab = pl.BlockSpec((tm, 128), lambda i: (i, 0))
    return pl.pallas_call(
        body, name="ret_post_bwd", grid=(t // tm,),
        in_specs=[blk, blk, blk, tab, tab], out_specs=[blk, blk, blk],
        out_shape=[SDS((t, 512), BF16)] * 3,
        compiler_params=_params(("parallel",)),
    )(dq, dk, dv, cos, sin)


def _attn_bwd(q, qt, k, v, doa, oa, lse, ex=None):
    t = q.shape[1]
    tq = min(ATTN_BWD_QUERY_TILE, t)
    nq = t // tq
    tk = min(ATTN_BWD_KEY_CHUNK, t)
    nk = t // tk
    hd = ATTN_HEAD_DIM
    scale = hd ** -0.5

    def body(q_ref, qt_ref, k_ref, v_ref, do_ref, o_ref, lse_ref, dq_ref, dkt_ref, dvt_ref):
        p, i = pl.program_id(0), pl.program_id(1)

        @pl.when(jnp.logical_and(p % 2 == 0, i == 0))
        def _():
            dkt_ref[...] = jnp.zeros_like(dkt_ref)
            dvt_ref[...] = jnp.zeros_like(dvt_ref)

        dov, ov = do_ref[...], o_ref[...]
        dovt = dov.T
        lanes = lambda col: jnp.concatenate([col] * (tk // 128), axis=1)
        outs = []
        for j in range(2):
            qq, qqt = q_ref[j], qt_ref[j]
            do32 = dov[:, j * hd:(j + 1) * hd]
            do, dot_ = do32.astype(BF16), dovt[j * hd:(j + 1) * hd, :].astype(BF16)
            dd = lanes(jnp.broadcast_to(jnp.sum(do32 * ov[:, j * hd:(j + 1) * hd], axis=1, keepdims=True), (tq, 128)))
            lse_j = lanes(jnp.broadcast_to(lse_ref[j], (128, tq)).T)
            dq = jnp.zeros((tq, hd), F32)
            for c in range(nk):
                sl = slice(c * tk, (c + 1) * tk)
                kc, vc = k_ref[0, sl, :], v_ref[0, sl, :]
                pr = jnp.exp(_dot(qq, kc, NT) - lse_j)
                ds = (pr * (_dot(do, vc, NT) - dd)).astype(BF16)
                dvt_ref[0, :, sl] += _dot(dot_, pr.astype(BF16))
                dkt_ref[0, :, sl] += _dot(qqt, ds)
                dq = dq + _dot(ds, kc)
            outs.append(dq * scale)
        dq_ref[...] = jnp.concatenate(outs, axis=-1)

    kv = pl.BlockSpec((1, t, hd), lambda p, i: (p // 2, 0, 0))
    kvt = pl.BlockSpec((1, hd, t), lambda p, i: (p // 2, 0, 0))
    pair = pl.BlockSpec((tq, 128), lambda p, i: (i, p))
    first = lambda: jnp.logical_and(pl.program_id(0) == 0, pl.program_id(1) == 0)
    last = lambda: jnp.logical_and(pl.program_id(0) == 3, pl.program_id(1) == nq - 1)
    xi, xo, xs, xscr, xargs = _ex_args(ex)
    return pl.pallas_call(
        _with_exchange(body, 7, 3, 0, ex, first, last), name="attn_bwd", grid=(4, nq),
        in_specs=[pl.BlockSpec((2, tq, hd), lambda p, i: (p, i, 0)), pl.BlockSpec((2, hd, tq), lambda p, i: (p, 0, i)),
                  kv, kv, pair, pair, pl.BlockSpec((2, 1, tq), lambda p, i: (p, 0, i))] + xi,
        out_specs=[pair, kvt, kvt] + xo,
        out_shape=[SDS((t, ATTN_WIDTH), F32), SDS((ATTN_KV_HEADS, hd, t), F32),
                   SDS((ATTN_KV_HEADS, hd, t), F32)] + xs,
        scratch_shapes=xscr,
        compiler_params=_params(("arbitrary", "arbitrary")),
    )(q, qt, k, v, doa, oa, lse, *xargs)


def _attn_post_bwd(dq, dk, dv, z, qn, kn, cos, sin, ones_bd):
    t = z.shape[0]
    tm = min(512, t)
    n = t // tm
    hd = ATTN_HEAD_DIM

    def body(dq_ref, dk_ref, dv_ref, zq_ref, zkv_ref, qn_ref, kn_ref, c_ref, s_ref, b_ref,
             dz_ref, dqn_ref, dkn_ref, acc_q, acc_k):
        i = pl.program_id(0)

        @pl.when(i == 0)
        def _():
            acc_q[...] = jnp.zeros_like(acc_q)
            acc_k[...] = jnp.zeros_like(acc_k)

        bd = b_ref[...]
        c2, s2 = c_ref[...], s_ref[...]

        def norm_bwd(dy, x, w, ones, cos_t, sin_t, acc):
            dyr = _rope_bwd(dy, cos_t, sin_t, hd // 4)
            r = lax.rsqrt(_group_mean(x * x, ones) + EPS)
            xh = x * r
            gy = dyr * w
            acc[...] += jnp.sum((dyr * xh).reshape(tm // 8, 8, x.shape[-1]), axis=0)
            return r * (gy - xh * _group_mean(gy * xh, ones))

        cq = jnp.concatenate([c2] * 4, axis=-1)
        sq = jnp.concatenate([s2] * 4, axis=-1)
        dz_ref[:, :512] = norm_bwd(dq_ref[...], zq_ref[...], qn_ref[...], bd, cq, sq, acc_q).astype(BF16)
        zkv = zkv_ref[...]
        dkk = jnp.concatenate([dk_ref[0], dk_ref[1]], axis=0).T
        dz_ref[:, 512:640] = norm_bwd(dkk, zkv[:, :128], kn_ref[...], bd[:128, :128], c2, s2, acc_k).astype(BF16)
        dz_ref[:, 640:768] = jnp.concatenate([dv_ref[0], dv_ref[1]], axis=0).T.astype(BF16)

        @pl.when(i == n - 1)
        def _():
            dqn_ref[...] = jnp.sum(acc_q[...], axis=0, keepdims=True)
            dkn_ref[...] = jnp.sum(acc_k[...], axis=0, keepdims=True)

    kv_blk = SEG["ka"][2] // 256
    kvs = pl.BlockSpec((ATTN_KV_HEADS, hd, tm), lambda i: (0, 0, i))
    const = lambda shape: pl.BlockSpec(shape, lambda i: (0, 0))
    return pl.pallas_call(
        body, name="attn_post_bwd", grid=(n,),
        in_specs=[pl.BlockSpec((tm, 512), lambda i: (i, 0)), kvs, kvs,
                  pl.BlockSpec((tm, 512), lambda i: (i, 0)), pl.BlockSpec((tm, 256), lambda i: (i, kv_blk)),
                  const((1, 512)), const((1, 128)),
                  pl.BlockSpec((tm, 128), lambda i: (i, 0)), pl.BlockSpec((tm, 128), lambda i: (i, 0)),
                  const((512, 512))],
        out_specs=[pl.BlockSpec((tm, 768), lambda i: (i, 0)), const((1, 512)), const((1, 128))],
        out_shape=[SDS((t, 768), BF16), SDS((1, 512), F32), SDS((1, 128), F32)],
        scratch_shapes=[pltpu.VMEM((8, 512), F32), pltpu.VMEM((8, 128), F32)],
        compiler_params=_params(("arbitrary",)),
    )(dq, dk, dv, z, z, qn, kn, cos, sin, ones_bd)


def _in_bwd(dxo, x, g, w_t, dz_a, dz_m, dqr, dkr, dvr, after=None):
    t, d = x.shape
    tm = min(256, t)
    n = t // tm
    parts = [(0, 0, 768, 0), (1, 0, 512, SEG["ga"][0]), (2, 0, 512, SEG["qr"][0]), (3, 0, 512, SEG["kr"][0]),
             (4, 0, 512, SEG["vr"][0]), (1, 512, 2560, SEG["gr"][0])]

    def body(dx_ref, x_ref, g_ref, w_ref, a_ref, m_ref, q_ref, k_ref, v_ref, o_ref, dg_ref, acc):
        i = pl.program_id(0)

        @pl.when(i == 0)
        def _():
            acc[...] = jnp.zeros_like(acc)

        pieces = [a_ref, m_ref, q_ref, k_ref, v_ref]
        dh = jnp.zeros((tm, d), F32)
        for pi, lo, w, row in parts:
            dh = dh + _dot(pieces[pi][:, lo:lo + w], w_ref[row:row + w, :])
        xv = x_ref[...]
        r = lax.rsqrt(jnp.mean(xv * xv, axis=-1, keepdims=True) + EPS)
        xh = xv * r
        gy = dh * g_ref[...]
        o_ref[...] = dx_ref[...] + r * (gy - xh * jnp.mean(gy * xh, axis=-1, keepdims=True))
        acc[...] += jnp.sum((dh * xh).reshape(tm // 8, 8, d), axis=0)

        @pl.when(i == n - 1)
        def _():
            dg_ref[...] = jnp.sum(acc[...], axis=0, keepdims=True)

    row = lambda w: pl.BlockSpec((tm, w), lambda i: (i, 0))
    const = lambda shape: pl.BlockSpec(shape, lambda i: (0, 0))
    extra = [] if after is None else [after]
    return pl.pallas_call(
        (lambda *refs: body(*refs[:9], *refs[9 + len(extra):])), name="in_bwd", grid=(n,),
        in_specs=[row(d), row(d), const((1, d)), const((D_IN, d)), row(768), row(3072), row(512), row(512),
                  row(512)] + [const(a.shape) for a in extra],
        out_specs=[row(d), const((1, d))],
        out_shape=[SDS((t, d), F32), SDS((1, d), F32)],
        scratch_shapes=[pltpu.VMEM((8, d), F32)],
        compiler_params=_params(("arbitrary",)),
    )(dxo, x, g, w_t, dz_a, dz_m, dqr, dkr, dvr, *extra)


def _dw_in(h_t, dz_a, dz_m, dqr, dkr, dvr):
    d, t = h_t.shape
    tn = 256
    parts = [(0, 0, 0, 3), (1, 0, SEG["ga"][0] // tn, 2), (2, 0, SEG["qr"][0] // tn, 2),
             (3, 0, SEG["kr"][0] // tn, 2), (4, 0, SEG["vr"][0] // tn, 2), (1, 2, SEG["gr"][0] // tn, 10)]
    pieces = [dz_a, dz_m, dqr, dkr, dvr]

    def col_block(pi):
        mine = [(c0, r0, n) for q, c0, r0, n in parts if q == pi]

        def index(j):
            c0, r0, n = mine[0]
            blk = c0 + jnp.clip(j - r0, 0, n - 1)
            for c0, r0, n in mine[1:]:
                blk = jnp.where(j >= r0, c0 + jnp.clip(j - r0, 0, n - 1), blk)
            return 0, blk

        return index

    def body(h_ref, *refs):
        o_ref = refs[-1]
        j = pl.program_id(0)
        for pi, _, r0, n in parts:
            @pl.when(jnp.logical_and(j >= r0, j < r0 + n))
            def _(p_ref=refs[pi]):
                o_ref[...] = _dot(h_ref[...], p_ref[...]).T.astype(BF16)

    return pl.pallas_call(
        body, name="dw_in", grid=(D_IN // tn,),
        in_specs=[pl.BlockSpec((d, t), lambda j: (0, 0))] + [pl.BlockSpec((t, tn), col_block(pi)) for pi in range(5)],
        out_specs=pl.BlockSpec((tn, d), lambda j: (j, 0)),
        out_shape=SDS((D_IN, d), BF16),
        compiler_params=_params(("arbitrary",)),
    )(h_t, *pieces)


def _adamw_math(w, g, m, v):
    mn = ADAM_B1 * m + (1.0 - ADAM_B1) * g
    vn = ADAM_B2 * v + (1.0 - ADAM_B2) * (g * g)
    m_hat = mn / (1.0 - ADAM_B1 ** ADAM_STEP)
    v_hat = vn / (1.0 - ADAM_B2 ** ADAM_STEP)
    return -ADAM_LR * (m_hat / (jnp.sqrt(v_hat) + ADAM_EPS) + ADAM_WD * w), mn, vn


def _sum_adamw(recvs, w, m, v, lane0, tn, layer0=0, prev=None, own=None):
    _, r, c = w.shape
    j0 = lane0 // tn
    n = len(recvs)
    has_own = own is not None

    def body(*refs):
        mine_ref, refs = (refs[0], refs[1:]) if has_own else (None, refs)
        w_ref, m_ref, v_ref = refs[n:n + 3]
        g_ref, d_ref, mo_ref, vo_ref = refs[-4:]

        def run(r_ref):
            def slot(s):
                if has_own:
                    return jnp.where(mine_ref[0] == s, refs[n + 3][...], r_ref[s]).astype(F32)
                return r_ref[s].astype(F32)

            g = slot(0)
            for s in range(1, N_DEV):
                g = g + slot(s)
            g_ref[0] = g
            d_ref[0], mo_ref[0], vo_ref[0] = _adamw_math(w_ref[0], g, m_ref[0], v_ref[0])

        for i in range(n):
            pl.when(pl.program_id(0) == i)(functools.partial(run, refs[i]))

    slots = pl.BlockSpec((N_DEV, r, tn), lambda i, j, *_: (0, 0, j0 + j))
    blk = pl.BlockSpec((1, r, tn), lambda i, j, *_: (layer0 + i, 0, j))
    before = [] if prev is None else list(prev)
    in_specs, args = [slots] * n + [blk] * 3, [*recvs, w, m, v]
    if has_own:
        assert n == 1
        in_specs.append(pl.BlockSpec((r, tn), lambda i, j, mine: (mine[0], j0 + j)))
        args.append(own[0])
    n_pre = len(args) + has_own
    return pl.pallas_call(
        body, name="sum_adamw",
        grid_spec=pltpu.PrefetchScalarGridSpec(
            num_scalar_prefetch=int(has_own), grid=(n, c // tn),
            in_specs=in_specs + [ANY] * len(before), out_specs=[blk] * 4),
        out_shape=[SDS(w.shape, F32)] * 4,
        input_output_aliases={n_pre + k: k for k in range(len(before))},
        compiler_params=_params(("parallel", "parallel")),
    )(*([own[1]] if has_own else []), *args, *before)


def _adamw(w, g, m, v):
    rows, cols = w.shape
    tr = 256 if rows % 256 == 0 else rows

    def body(w_ref, g_ref, m_ref, v_ref, d_ref, mo_ref, vo_ref):
        d_ref[...], mo_ref[...], vo_ref[...] = _adamw_math(w_ref[...], g_ref[...], m_ref[...], v_ref[...])

    blk = pl.BlockSpec((tr, cols), lambda i: (i, 0))
    return pl.pallas_call(
        body, name="adamw", grid=(rows // tr,),
        in_specs=[blk] * 4, out_specs=[blk] * 3, out_shape=[SDS((rows, cols), F32)] * 3,
        compiler_params=_params(("parallel",)),
    )(w, g, m, v)


def _all_gather(shards):
    na = len(shards)
    chips = (4, 2, 6)

    def body(*refs):
        ins, outs = refs[:na], refs[na:2 * na]
        send_sems, recv_sems, local_sems = refs[2 * na:]
        _, mine = _flip(0)

        def rows(a, idx):
            r = shards[a].shape[0]
            return outs[a].at[pl.ds(pl.multiple_of(idx * r, 16), r), :]

        def copy(a, slot, block_idx, to, src=None):
            return pltpu.make_async_remote_copy(
                src_ref=rows(a, block_idx) if src is None else src, dst_ref=rows(a, block_idx),
                send_sem=send_sems.at[a, slot], recv_sem=recv_sems.at[a, slot],
                device_id=to, device_id_type=MESH_ID)

        sibling, sibling_idx = _flip(1)
        local, started = [], []
        for a in range(na):
            cp = pltpu.make_async_copy(ins[a], rows(a, mine), local_sems.at[a])
            cp.start()
            local.append(cp)
            first = [copy(a, 0, mine, sibling, src=ins[a])]
            first += [copy(a, 1 + j, mine, _flip(k)[0], src=ins[a]) for j, k in enumerate(chips)]
            for cp in first:
                cp.start()
            started += first
        for a in range(na):
            for j, k in enumerate(chips):
                _, theirs = _flip(k)
                copy(a, 1 + j, theirs, _flip(0)[0]).wait_recv()
                fwd = copy(a, 4 + j, theirs, sibling)
                fwd.start()
                started.append(fwd)
        for a in range(na):
            copy(a, 0, sibling_idx, _flip(0)[0]).wait_recv()
            for j, k in enumerate(chips):
                _, theirs = _flip(k | 1)
                copy(a, 4 + j, theirs, _flip(0)[0]).wait_recv()
        for cp in started:
            cp.wait_send()
        for cp in local:
            cp.wait()

    return pl.pallas_call(
        body, name="all_gather_weights",
        in_specs=[ANY] * na, out_specs=[ANY] * na,
        out_shape=[SDS((N_DEV * s.shape[0], s.shape[1]), s.dtype) for s in shards],
        scratch_shapes=[pltpu.SemaphoreType.DMA((na, 7)), pltpu.SemaphoreType.DMA((na, 7)),
                        pltpu.SemaphoreType.DMA((na,))],
        compiler_params=pltpu.CompilerParams(has_side_effects=True),
    )(*shards)


def _scatter_blocks_of(g_ref, rows, idx):
    return g_ref.at[pl.ds(pl.multiple_of(idx * rows, 16), rows), :]


def _scatter_start(g):
    rows = g.shape[0] // N_DEV
    land_shape = (N_DEV, rows, g.shape[1])

    def body(g_ref, land_ref, send_sems, recv_sems, g_thru, land_thru, token):
        _, mine = _flip(0)
        for k in range(1, N_DEV):
            peer, theirs = _flip(k)
            pltpu.make_async_remote_copy(
                src_ref=_scatter_blocks_of(g_ref, rows, theirs), dst_ref=land_ref.at[mine],
                send_sem=send_sems.at[k - 1], recv_sem=recv_sems.at[k - 1],
                device_id=peer, device_id_type=MESH_ID).start()
        token[...] = jnp.zeros_like(token)

    hbm, sem = pl.BlockSpec(memory_space=pltpu.HBM), pl.BlockSpec(memory_space=pltpu.SEMAPHORE)
    return pl.pallas_call(
        body, name="scatter_start",
        out_shape=(pltpu.SemaphoreType.DMA((N_DEV - 1,)), pltpu.SemaphoreType.DMA((N_DEV - 1,)),
                   pltpu.HBM(g.shape, g.dtype), pltpu.HBM(land_shape, g.dtype), SDS((8, 128), F32)),
        in_specs=(hbm, hbm), out_specs=(sem, sem, hbm, hbm, pl.BlockSpec(memory_space=pltpu.VMEM)),
        input_output_aliases={0: 2, 1: 3},
        compiler_params=pltpu.CompilerParams(has_side_effects=pltpu.SideEffectType.DATAFLOW_SIDE_EFFECTING),
    )(pltpu.with_memory_space_constraint(g, pltpu.HBM),
      pltpu.with_memory_space_constraint(lax.empty(land_shape, g.dtype), pltpu.HBM))


def _scatter_wait(send_sems, recv_sems, g_thru, land_thru, after):
    rows = g_thru.shape[0] // N_DEV

    def body(g_ref, land_ref, send_sems, recv_sems, *rest):
        me, _ = _flip(0)
        for k in range(1, N_DEV):
            _, theirs = _flip(k)
            copy = pltpu.make_async_remote_copy(
                src_ref=_scatter_blocks_of(g_ref, rows, theirs), dst_ref=land_ref.at[theirs],
                send_sem=send_sems.at[k - 1], recv_sem=recv_sems.at[k - 1],
                device_id=me, device_id_type=MESH_ID)
            copy.wait_send()
            copy.wait_recv()

    hbm, sem = pl.BlockSpec(memory_space=pltpu.HBM), pl.BlockSpec(memory_space=pltpu.SEMAPHORE)
    return pl.pallas_call(
        body, name="scatter_wait",
        out_shape=(pltpu.HBM(g_thru.shape, g_thru.dtype), pltpu.HBM(land_thru.shape, land_thru.dtype)),
        in_specs=(hbm, hbm, sem, sem) + (ANY,) * len(after), out_specs=(hbm, hbm), input_output_aliases={0: 0, 1: 1},
        compiler_params=pltpu.CompilerParams(has_side_effects=pltpu.SideEffectType.DATAFLOW_SIDE_EFFECTING),
    )(g_thru, land_thru, send_sems, recv_sems, *after)


def _all_reduce_small(packed):
    shape = packed.shape

    def body(p_ref, o_ref, slots, send_sems, recv_sems):
        me, mine = _flip(0)
        slots[mine] = p_ref[...]
        sends = []
        for k in range(1, N_DEV):
            peer, _ = _flip(k)
            cp = pltpu.make_async_remote_copy(
                src_ref=p_ref, dst_ref=slots.at[mine], send_sem=send_sems.at[k - 1], recv_sem=recv_sems.at[k - 1],
                device_id=peer, device_id_type=MESH_ID)
            cp.start()
            sends.append(cp)
        for k in range(1, N_DEV):
            _, theirs = _flip(k)
            pltpu.make_async_remote_copy(
                src_ref=p_ref, dst_ref=slots.at[theirs], send_sem=send_sems.at[k - 1],
                recv_sem=recv_sems.at[k - 1], device_id=me, device_id_type=MESH_ID).wait_recv()
        for cp in sends:
            cp.wait_send()
        acc = slots[0]
        for s in range(1, N_DEV):
            acc = acc + slots[s]
        o_ref[...] = acc

    vm = pl.BlockSpec(memory_space=pltpu.VMEM)
    return pl.pallas_call(
        body, name="all_reduce_small", in_specs=[vm], out_specs=vm, out_shape=SDS(shape, F32),
        scratch_shapes=[pltpu.VMEM((N_DEV,) + shape, F32), pltpu.SemaphoreType.DMA((7,)),
                        pltpu.SemaphoreType.DMA((7,))],
        compiler_params=pltpu.CompilerParams(has_side_effects=True),
    )(packed)


def _layer_fwd(x, p, tabs, ex):
    z, h_t, q, qt, k, v, vt, qrot, krot, vb = _in_proj(x, p["norm_g"], p["w_in_t"], p["qn"], p["kn"], tabs["ca"],
                                                       tabs["sa"], tabs["ones"], tabs["cr"], tabs["sr"])
    oa, lse, *gathered = _attn_fwd(q, k, vt, ex)
    orr, on = _ret_fwd(qrot, krot, vb, p["lgf"], p["lgb"], p["gnw"])
    return z, h_t, q, qt, k, v, lse, oa, qrot, krot, vb, orr, on, gathered


def _layer_bwd(dxo, s, p, tabs, ex_attn, scatter_w_in):
    doa, don, dz_m, d_wout, d_wb_t = _merge_bwd(dxo, s["z"], s["oa"], s["on"], s["ya"], s["yb"], p["wb_t"], p["w_out"])
    dq_a, dk_a, dv_a, *recv_attn = _attn_bwd(s["q"], s["qt"], s["k"], s["v"], doa, s["oa"], s["lse"],
                                              ex_attn(d_wb_t, d_wout))
    dz_a, d_qn, d_kn = _attn_post_bwd(dq_a, dk_a, dv_a, s["z"], p["qn"], p["kn"], tabs["ca"], tabs["sa"],
                                      tabs["ones"])
    dq_r, dk_r, dv_r, d_gnw, d_lgf, d_lgb = _ret_bwd(s["qrot"], s["krot"], s["vb"], s["orr"], don, p["gnw"],
                                                     p["lgf"], p["lgb"])
    dqr, dkr, dvr = _ret_post_bwd(dq_r, dk_r, dv_r, tabs["cr"], tabs["sr"])
    buf = _dw_in(s["h_t"], dz_a, dz_m, dqr, dkr, dvr)
    pending, token = None, None
    if scatter_w_in:
        *pending, token = _scatter_start(buf)
    dx, d_norm_g = _in_bwd(dxo, s["x"], p["norm_g"], p["w_in_t"], dz_a, dz_m, dqr, dkr, dvr, token)
    grads = dict(w_in_t=buf, wb_t=d_wb_t, w_out=d_wout, norm_g=d_norm_g, gnw=d_gnw,
                 qn=d_qn.reshape(ATTN_Q_HEADS, ATTN_HEAD_DIM).sum(axis=0),
                 kn=d_kn.reshape(ATTN_KV_HEADS, ATTN_HEAD_DIM).sum(axis=0),
                 lgf=d_lgf[:, 0, 0], lgb=d_lgb[:, 0, 0])
    return dx, grads, recv_attn, pending


def _adamw_nd(w, g, m, v):
    shape = w.shape
    two_d = (1, shape[0]) if w.ndim == 1 else (-1, shape[-1])
    out = _adamw(w.reshape(two_d), g.reshape(two_d), m.reshape(two_d), v.reshape(two_d))
    return tuple(o.reshape(shape) for o in out)


def kernel(x, norm_g, w_in, attn_q_norm, attn_k_norm, ret_decay_fwd, ret_decay_bwd, ret_gn_w, w_branch_attn, w_branch_ret, w_out, final_norm_g, loss_target, m_norm_g, m_w_in, m_attn_q_norm, m_attn_k_norm, m_ret_decay_fwd, m_ret_decay_bwd, m_ret_gn_w, m_w_branch_attn, m_w_branch_ret, m_w_out, m_final_norm_g, v_norm_g, v_w_in, v_attn_q_norm, v_attn_k_norm, v_ret_decay_fwd, v_ret_decay_bwd, v_ret_gn_w, v_w_branch_attn, v_w_branch_ret, v_w_out, v_final_norm_g):
    t, d = x.shape[1], x.shape[2]
    x2, target = x[0], loss_target[0]

    w_in_sh, wb_sh, wout_sh = [], [], []
    for l in range(DEPTH):
        w_in_sh.append(jnp.swapaxes(w_in[l], 0, 1).astype(BF16))
        wb_sh.append(jnp.concatenate([w_branch_attn[l].T, w_branch_ret[l].T], axis=1).astype(BF16))
        wout_sh.append(w_out[l].astype(BF16))

    ca, sa = _rope_tables(t, ATTN_HEAD_DIM)
    cr, sr = _rope_tables(t, RET_HEAD_DIM)
    grp = jnp.arange(ATTN_WIDTH) // ATTN_HEAD_DIM
    tabs = dict(ca=jnp.tile(ca, (1, 2)), sa=jnp.tile(sa, (1, 2)), cr=cr, sr=sr,
                ones=jnp.where(grp[:, None] == grp[None, :], 1.0 / ATTN_HEAD_DIM, 0.0).astype(BF16))
    layers = []
    for l in range(DEPTH):
        layers.append(dict(
            norm_g=norm_g[l][None], qn=jnp.tile(attn_q_norm[l], ATTN_Q_HEADS)[None],
            kn=jnp.tile(attn_k_norm[l], ATTN_KV_HEADS)[None], gnw=ret_gn_w[l][None],
            lgf=jax.nn.log_sigmoid(ret_decay_fwd[l]), lgb=jax.nn.log_sigmoid(ret_decay_bwd[l])))

    layers[0]["w_in_t"], = _all_gather([w_in_sh[0]])
    gathers = [_Exchange("gather", [wb_sh[0], wout_sh[0], w_in_sh[1]]), _Exchange("gather", [wb_sh[1], wout_sh[1]])]
    h = x2
    saved = []
    for l in range(DEPTH):
        p = layers[l]
        z, h_t, q, qt, k, v, lse, oa, qrot, krot, vb, orr, on, got = _layer_fwd(h, p, tabs, gathers[l])
        p["wb_t"], p["w_out"] = got[0], got[1]
        if l == 0:
            layers[1]["w_in_t"] = got[2]
        xn, ya, yb = _merge_fwd(h, z, oa, on, p["wb_t"], p["w_out"])
        saved.append(dict(x=h, z=z, h_t=h_t, q=q, qt=qt, k=k, v=v, lse=lse, oa=oa, qrot=qrot, krot=krot, vb=vb,
                          orr=orr, on=on, ya=ya, yb=yb))
        h = xn
    dx, d_final_g, loss_part = _final_loss(h, final_norm_g[None], target)

    grads = [None] * DEPTH
    dx, grads[1], _, _ = _layer_bwd(dx, saved[1], layers[1], tabs, lambda *a: None, False)
    g1 = grads[1]
    ex_attn = lambda d_wb_t, d_wout: _Exchange("scatter", [g1["w_in_t"], g1["wb_t"], g1["w_out"], d_wb_t, d_wout])
    dx, grads[0], recv_attn, pending = _layer_bwd(dx, saved[0], layers[0], tabs, ex_attn, True)
    recv = [None, recv_attn[3], recv_attn[4], recv_attn[0], recv_attn[1], recv_attn[2]]
    tr = lambda a: jnp.swapaxes(a, 1, 2)
    w_in_t = (tr(w_in), tr(m_w_in), tr(v_w_in))
    sharded = {}
    w_in_l1 = _sum_adamw([recv[3]], *w_in_t, 0, 256, layer0=1)
    sharded[id(w_branch_attn)] = [tr(o) for o in _sum_adamw(
        [recv[1], recv[4]], tr(w_branch_attn), tr(m_w_branch_attn), tr(v_w_branch_attn), 0, 512)]
    sharded[id(w_branch_ret)] = [tr(o) for o in _sum_adamw(
        [recv[1], recv[4]], tr(w_branch_ret), tr(m_w_branch_ret), tr(v_w_branch_ret), 512, 512)]
    sharded[id(w_out)] = _sum_adamw([recv[2], recv[5]], w_out, m_w_out, v_w_out, 0, 256)
    g_wba, g_wbr, g_wout = (sharded[id(w)][0] for w in (w_branch_attn, w_branch_ret, w_out))

    packed = jnp.zeros((8, 1024), F32)
    for l in range(DEPTH):
        gl = grads[l]
        packed = packed.at[l].set(gl["norm_g"][0])
        packed = packed.at[2, 512 * l:512 * (l + 1)].set(gl["gnw"][0])
        packed = packed.at[4, 128 * l:128 * l + 64].set(gl["qn"])
        packed = packed.at[4, 256 + 128 * l:256 + 128 * l + 64].set(gl["kn"])
        packed = packed.at[4, 512 + 128 * l:512 + 128 * l + 4].set(gl["lgf"])
        packed = packed.at[4, 768 + 128 * l:768 + 128 * l + 4].set(gl["lgb"])
    packed = packed.at[3].set(d_final_g[0])
    packed = packed.at[5, 0].set(loss_part[0, 0])
    red = _all_reduce_small(packed)
    loss = red[5, 0]
    g_norm_g = red[0:2]
    g_gnw = red[2].reshape(DEPTH, RET_WIDTH)
    g_final = red[3]
    g_qn = jnp.stack([red[4, 128 * l:128 * l + 64] for l in range(DEPTH)])
    g_kn = jnp.stack([red[4, 256 + 128 * l:256 + 128 * l + 64] for l in range(DEPTH)])
    g_lgf = jnp.stack([red[4, 512 + 128 * l:512 + 128 * l + 4] for l in range(DEPTH)])
    g_lgb = jnp.stack([red[4, 768 + 128 * l:768 + 128 * l + 4] for l in range(DEPTH)])
    g_df = g_lgf * jax.nn.sigmoid(-ret_decay_fwd)
    g_db = g_lgb * jax.nn.sigmoid(-ret_decay_bwd)

    grad_w = [g_norm_g, None, g_qn, g_kn, g_df, g_db, g_gnw, g_wba, g_wbr, g_wout, g_final]
    weights = [norm_g, w_in, attn_q_norm, attn_k_norm, ret_decay_fwd, ret_decay_bwd, ret_gn_w, w_branch_attn,
               w_branch_ret, w_out, final_norm_g]
    ms = [m_norm_g, m_w_in, m_attn_q_norm, m_attn_k_norm, m_ret_decay_fwd, m_ret_decay_bwd, m_ret_gn_w,
          m_w_branch_attn, m_w_branch_ret, m_w_out, m_final_norm_g]
    vs = [v_norm_g, v_w_in, v_attn_q_norm, v_attn_k_norm, v_ret_decay_fwd, v_ret_decay_bwd, v_ret_gn_w,
          v_w_branch_attn, v_w_branch_ret, v_w_out, v_final_norm_g]
    upd = [None if w is w_in else sharded[id(w)][1:] if id(w) in sharded else _adamw_nd(w, g, m, v)
           for w, g, m, v in zip(weights, grad_w, ms, vs)]

    done = [dx, w_in_l1[0], g_wout] + [u[0] for w, u in zip(weights, upd) if u is not None and id(w) not in sharded]
    g_full, recv[0] = _scatter_wait(*pending, done)
    mine = (4 * lax.axis_index("x") + 2 * lax.axis_index("y") + lax.axis_index("c")).astype(jnp.int32)[None]
    w_in_upd = [tr(o) for o in _sum_adamw([recv[0]], *w_in_t, 0, 256, layer0=0, prev=w_in_l1, own=(g_full, mine))]
    grad_w[1], upd[1] = w_in_upd[0], w_in_upd[1:]
    return (loss, dx[None], *grad_w, *[u[0] for u in upd], *[u[1] for u in upd], *[u[2] for u in upd])
```

```python
import functools

import jax
import jax.numpy as jnp
from jax import lax
from jax.experimental import pallas as pl
from jax.experimental.pallas import tpu as pltpu

F32 = jnp.float32
BF16 = jnp.bfloat16
SDS = jax.ShapeDtypeStruct

D_MODEL = 1024
DEPTH = 2
GRID_W = 64
ATTN_Q_HEADS = 8
ATTN_KV_HEADS = 2
ATTN_HEAD_DIM = 64
ATTN_WIDTH = 512
ATTN_KV_WIDTH = 128
RET_HEADS = 4
RET_HEAD_DIM = 128
RET_WIDTH = 512
RET_CHUNK = 128
ATTN_KEY_CHUNK = 512
ATTN_BWD_KEY_CHUNK = 1024
ATTN_BWD_QUERY_TILE = 512
ATTN_FWD_QUERY_TILE = 512
QK_DOTS_PER_CHUNK = 4
EXP_LAG = 3
ROPE_THETA = 10000.0
EPS = 1e-6
D_IN = 5376
N_DEV = 8

ADAM_LR = 0.001
ADAM_B1 = 0.9
ADAM_B2 = 0.999
ADAM_EPS = 1e-08
ADAM_WD = 0.01
ADAM_STEP = 10

SEG = {
    "qa": (0, 512, 0),
    "ga": (768, 512, 512),
    "qr": (1280, 512, 1024),
    "kr": (1792, 512, 1536),
    "vr": (2304, 512, 2048),
    "gr": (2816, 512, 2560),
    "gm": (3328, 2048, 3072),
    "ka": (512, 128, 5120),
    "va": (640, 128, 5248),
}

VMEM_LIMIT = 60 * 1024 * 1024
NT = (((1,), (1,)), ((), ()))
TN = (((0,), (0,)), ((), ()))
MESH_ID = pl.DeviceIdType.MESH
ANY = pl.BlockSpec(memory_space=pl.ANY)


def _params(sem=None, vmem=VMEM_LIMIT):
    return pltpu.CompilerParams(dimension_semantics=sem, vmem_limit_bytes=vmem)


def _dot(a, b, dims=None):
    if dims is None:
        return jnp.dot(a, b, preferred_element_type=F32)
    return lax.dot_general(a, b, dims, preferred_element_type=F32)


def _sigmoid(x):
    return 1.0 / (1.0 + jnp.exp(-x))


def _swap_halves(x, q):
    n = x.shape[-1]
    axis = x.ndim - 1
    lane = lax.broadcasted_iota(jnp.int32, x.shape, axis)
    first = (lane % (2 * q)) < q
    return jnp.where(first, pltpu.roll(x, n - q, axis), pltpu.roll(x, q, axis))


def _rope(x, cos, sin_signed, q):
    return x * cos + _swap_halves(x, q) * sin_signed


def _rope_bwd(dy, cos, sin_signed, q):
    return dy * cos - _swap_halves(dy, q) * sin_signed


def _group_mean(v, ones_bd):
    hi = v.astype(BF16)
    lo = (v - hi.astype(F32)).astype(BF16)
    return _dot(hi, ones_bd) + _dot(lo, ones_bd)


def _rope_tables(t, head_dim):
    n_rows = t // GRID_W
    d_axis = head_dim // 2
    inv_freq = ROPE_THETA ** (-jnp.arange(0, d_axis, 2, dtype=F32) / d_axis)
    ar = jnp.arange(n_rows, dtype=F32)[:, None] * inv_freq
    ac = jnp.arange(GRID_W, dtype=F32)[:, None] * inv_freq
    by_row = lambda a: jnp.repeat(a, GRID_W, axis=0)
    by_col = lambda a: jnp.tile(a, (n_rows, 1))
    cr, sr, cc, sc = by_row(jnp.cos(ar)), by_row(jnp.sin(ar)), by_col(jnp.cos(ac)), by_col(jnp.sin(ac))
    return jnp.concatenate([cr, cr, cc, cc], axis=-1), jnp.concatenate([-sr, sr, -sc, sc], axis=-1)


def _me():
    return lax.axis_index("x"), lax.axis_index("y"), lax.axis_index("c")


def _flip(k):
    x, y, c = _me()
    px = 1 - x if k & 4 else x
    py = 1 - y if k & 2 else y
    pc = 1 - c if k & 1 else c
    return (px, py, pc), 4 * px + 2 * py + pc


class _Exchange:
    def __init__(self, kind, srcs):
        self.kind, self.srcs, self.n = kind, list(srcs), len(srcs)
        self.rows = [a.shape[0] if kind == "gather" else a.shape[0] // N_DEV for a in srcs]
        if kind == "gather":
            self.out_shape = [SDS((N_DEV * a.shape[0], a.shape[1]), a.dtype) for a in srcs]
        else:
            self.out_shape = [SDS((N_DEV, a.shape[0] // N_DEV, a.shape[1]), a.dtype) for a in srcs]
        self.scratch = [pltpu.SemaphoreType.DMA((self.n, N_DEV - 1)), pltpu.SemaphoreType.DMA((self.n, N_DEV - 1)),
                        pltpu.SemaphoreType.DMA((self.n,))]

    def _block(self, ref, a, idx):
        r = self.rows[a]
        return ref.at[pl.ds(pl.multiple_of(idx * r, 16), r), :]

    def _src(self, ins, a, idx):
        return ins[a] if self.kind == "gather" else self._block(ins[a], a, idx)

    def _dst(self, outs, a, idx):
        return self._block(outs[a], a, idx) if self.kind == "gather" else outs[a].at[idx]

    def _copies(self, ins, outs, sems):
        send_sems, recv_sems, local_sems = sems
        me, mine = _flip(0)
        local, sends, recvs = [], [], []
        for a in range(self.n):
            local.append(pltpu.make_async_copy(self._src(ins, a, mine), self._dst(outs, a, mine), local_sems.at[a]))
            for k in range(1, N_DEV):
                peer, theirs = _flip(k)
                sem = dict(send_sem=send_sems.at[a, k - 1], recv_sem=recv_sems.at[a, k - 1])
                sends.append(pltpu.make_async_remote_copy(
                    src_ref=self._src(ins, a, theirs), dst_ref=self._dst(outs, a, mine),
                    device_id=peer, device_id_type=MESH_ID, **sem))
                recvs.append(pltpu.make_async_remote_copy(
                    src_ref=self._dst(outs, a, theirs), dst_ref=self._dst(outs, a, theirs),
                    device_id=me, device_id_type=MESH_ID, **sem))
        return local, sends, recvs

    def start(self, ins, outs, sems):
        local, sends, _ = self._copies(ins, outs, sems)
        for cp in local + sends:
            cp.start()

    def wait(self, ins, outs, sems):
        local, sends, recvs = self._copies(ins, outs, sems)
        for cp in sends:
            cp.wait_send()
        for cp in recvs:
            cp.wait_recv()
        for cp in local:
            cp.wait()


def _with_exchange(body, n_in, n_out, n_scratch, ex, first, last):
    if ex is None:
        return body

    def wrapped(*refs):
        ins = refs[:n_in]
        ex_ins = refs[n_in:n_in + ex.n]
        outs = refs[n_in + ex.n:n_in + ex.n + n_out]
        ex_outs = refs[n_in + ex.n + n_out:n_in + 2 * ex.n + n_out]
        rest = refs[n_in + 2 * ex.n + n_out:]
        scratch, sems = rest[:n_scratch], rest[n_scratch:]

        @pl.when(first())
        def _():
            ex.start(ex_ins, ex_outs, sems)

        body(*ins, *outs, *scratch)

        @pl.when(last())
        def _():
            ex.wait(ex_ins, ex_outs, sems)

    return wrapped


def _ex_args(ex):
    if ex is None:
        return [], [], [], [], []
    return [ANY] * ex.n, [ANY] * ex.n, list(ex.out_shape), list(ex.scratch), list(ex.srcs)


def _in_proj(x, g, w_t, qn, kn, cos, sin, ones_bd, cos_r, sin_r):
    t, d = x.shape
    tm = min(256, t)
    tk = min(ATTN_KEY_CHUNK, t)
    per_chunk = tk // tm
    hd = ATTN_HEAD_DIM

    def body(x_ref, g_ref, w_ref, qn_ref, kn_ref, c_ref, s_ref, b_ref, cr_ref, sr_ref,
             z_ref, ht_ref, q_out, qt_out, k_out, v_out, vt_out, qr_out, kr_out, vr_out):
        xv = x_ref[...]
        r = lax.rsqrt(jnp.mean(xv * xv, axis=-1, keepdims=True) + EPS)
        h = xv * r * g_ref[...]
        ht_ref[...] = h.T.astype(BF16)
        hb = h.astype(BF16)
        def project(name):
            nat, w, off = SEG[name]
            zs = _dot(hb, w_ref[nat:nat + w, :], NT)
            z_ref[:, off:off + w] = zs
            return zs

        seg = {name: project(name) for name in ("qa", "ka", "va")}
        bd = b_ref[...]
        c2, s2 = c_ref[...], s_ref[...]
        cq = jnp.concatenate([c2] * 4, axis=-1)
        sq = jnp.concatenate([s2] * 4, axis=-1)
        xq, xk, xvv = seg["qa"], seg["ka"], seg["va"]
        yq = xq * lax.rsqrt(_group_mean(xq * xq, bd) + EPS) * qn_ref[...]
        yq = _rope(yq, cq, sq, hd // 4) * (hd ** -0.5)
        yqt = yq.T
        for hh in range(ATTN_Q_HEADS):
            q_out[hh] = yq[:, hh * hd:(hh + 1) * hd].astype(BF16)
            qt_out[hh] = yqt[hh * hd:(hh + 1) * hd, :].astype(BF16)
        yk = xk * lax.rsqrt(_group_mean(xk * xk, bd[:ATTN_KV_WIDTH, :ATTN_KV_WIDTH]) + EPS) * kn_ref[...]
        yk = _rope(yk, c2, s2, hd // 4)
        xvt = xvv.T
        ones = jnp.ones((hd, tm), F32)
        for hh in range(ATTN_KV_HEADS):
            k_out[hh] = yk[:, hh * hd:(hh + 1) * hd].astype(BF16)
            v_out[hh] = xvv[:, hh * hd:(hh + 1) * hd].astype(BF16)
            vt_out[hh, 0] = jnp.concatenate([xvt[hh * hd:(hh + 1) * hd, :], ones], axis=0).astype(BF16)
        rd = RET_HEAD_DIM
        cr = jnp.concatenate([cr_ref[...]] * RET_HEADS, axis=-1)
        sr = jnp.concatenate([sr_ref[...]] * RET_HEADS, axis=-1)
        qr_out[...] = _rope(project("qr"), cr, sr, rd // 4).astype(BF16)
        kr_out[...] = (_rope(project("kr"), cr, sr, rd // 4) * (rd ** -0.5)).astype(BF16)
        vr_out[...] = project("vr").astype(BF16)
        for name in ("ga", "gr", "gm"):
            project(name)

    const = lambda shape: pl.BlockSpec(shape, lambda i: (0,) * len(shape))
    rows = lambda w: pl.BlockSpec((tm, w), lambda i: (i, 0))
    return pl.pallas_call(
        body, name="in_proj", grid=(t // tm,),
        in_specs=[rows(d), const((1, d)), const((D_IN, d)), const((1, 512)), const((1, 128)), rows(128), rows(128),
                  const((512, 512)), rows(128), rows(128)],
        out_specs=[rows(D_IN), pl.BlockSpec((d, tm), lambda i: (0, i)),
                   pl.BlockSpec((ATTN_Q_HEADS, tm, hd), lambda i: (0, i, 0)),
                   pl.BlockSpec((ATTN_Q_HEADS, hd, tm), lambda i: (0, 0, i)),
                   pl.BlockSpec((ATTN_KV_HEADS, tm, hd), lambda i: (0, i, 0)),
                   pl.BlockSpec((ATTN_KV_HEADS, tm, hd), lambda i: (0, i, 0)),
                   pl.BlockSpec((ATTN_KV_HEADS, 1, 2 * hd, tm), lambda i: (0, i // per_chunk, 0, i % per_chunk)),
                   rows(RET_WIDTH), rows(RET_WIDTH), rows(RET_WIDTH)],
        out_shape=[SDS((t, D_IN), F32), SDS((d, t), BF16),
                   SDS((ATTN_Q_HEADS, t, hd), BF16), SDS((ATTN_Q_HEADS, hd, t), BF16),
                   SDS((ATTN_KV_HEADS, t, hd), BF16), SDS((ATTN_KV_HEADS, t, hd), BF16),
                   SDS((ATTN_KV_HEADS, t // tk, 2 * hd, tk), BF16)] + [SDS((t, RET_WIDTH), BF16)] * 3,
        compiler_params=_params(("parallel",)),
    )(x, g, w_t, qn, kn, cos, sin, ones_bd, cos_r, sin_r)


def _attn_fwd(q, k, vt, ex=None):
    t = q.shape[1]
    tq = min(ATTN_FWD_QUERY_TILE, t)
    nk, tk = vt.shape[1], vt.shape[3]
    hd = ATTN_HEAD_DIM
    g = ATTN_Q_HEADS // ATTN_KV_HEADS

    def body(q_ref, k_ref, vt_ref, o_ref, lse_ref, s_scr):
        def pass_a(h, c, m8):
            part = tk // QK_DOTS_PER_CHUNK
            for lo in range(c * tk, (c + 1) * tk, part):
                st = _dot(k_ref[0, lo:lo + part, :], q_ref[h], NT)
                s_scr[h % 2, lo:lo + part, :] = st
                m8 = jnp.maximum(m8, jnp.max(st.reshape(part // 8, 8, tq), axis=0))
            return m8

        def pass_b(h, c, m, acc, after):
            e = jnp.exp(s_scr[h % 2, c * tk:(c + 1) * tk, :] - (m + after * 0.0)).astype(BF16)
            return acc + _dot(vt_ref[0, c], e)

        neg = jnp.full((8, tq), -jnp.inf, F32)
        m8 = neg
        for c in range(nk):
            m8 = pass_a(0, c, m8)
        outs = []
        for h in range(g):
            m = jnp.max(m8, axis=0, keepdims=True)
            acc = jnp.zeros((2 * hd, tq), F32)
            m8 = neg
            done = [m] * EXP_LAG
            for c in range(nk):
                if h + 1 < g:
                    m8 = pass_a(h + 1, c, m8)
                acc = pass_b(h, c, m, acc, done[-EXP_LAG])
                done.append(m8[0:1, :] if h + 1 < g else acc[hd:hd + 1, :])
            l = acc[hd:hd + 1, :]
            outs.append((acc[:hd, :] / l).T)
            lse_ref[h] = m + jnp.log(l)
        o_ref[...] = jnp.concatenate(outs, axis=-1)

    nq = t // tq
    first = lambda: jnp.logical_and(pl.program_id(0) == 0, pl.program_id(1) == 0)
    last = lambda: jnp.logical_and(pl.program_id(0) == ATTN_KV_HEADS - 1, pl.program_id(1) == nq - 1)
    xi, xo, xs, xscr, xargs = _ex_args(ex)
    return pl.pallas_call(
        _with_exchange(body, 3, 2, 1, ex, first, last), name="attn_fwd", grid=(ATTN_KV_HEADS, nq),
        in_specs=[pl.BlockSpec((g, tq, hd), lambda p, i: (p, i, 0)),
                  pl.BlockSpec((1, t, hd), lambda p, i: (p, 0, 0)),
                  pl.BlockSpec((1, nk, 2 * hd, tk), lambda p, i: (p, 0, 0, 0))] + xi,
        out_specs=[pl.BlockSpec((tq, g * hd), lambda p, i: (i, p)),
                   pl.BlockSpec((g, 1, tq), lambda p, i: (p, 0, i))] + xo,
        out_shape=[SDS((t, ATTN_WIDTH), F32), SDS((ATTN_Q_HEADS, 1, t), F32)] + xs,
        scratch_shapes=[pltpu.VMEM((2, t, tq), F32)] + xscr,
        compiler_params=_params(("arbitrary", "arbitrary")),
    )(q, k, vt, *xargs)


class _Dir:
    def __init__(self, lg, strict_future):
        c = RET_CHUNK
        ia = lax.broadcasted_iota(jnp.int32, (c, c), 0).astype(F32)
        ib = lax.broadcasted_iota(jnp.int32, (c, c), 1).astype(F32)
        col = lax.broadcasted_iota(jnp.int32, (c, 1), 0).astype(F32)
        row = lax.broadcasted_iota(jnp.int32, (1, c), 1).astype(F32)
        if strict_future:
            dist = ib - ia
            mask = dist > 0
            self.wq, self.wk, wk_row = c - col, col, row
        else:
            dist = ia - ib
            mask = dist >= 0
            self.wq, self.wk, wk_row = col + 1.0, c - 1.0 - col, c - 1.0 - row
        self.dist = jnp.maximum(dist, 0.0)
        self.d = jnp.where(mask, jnp.exp(self.dist * lg), 0.0)
        self.qd = jnp.exp(self.wq * lg)
        self.kd_col = jnp.exp(self.wk * lg)
        self.kd_row = jnp.exp(wk_row * lg)
        self.cd = jnp.exp(jnp.full((1, 1), float(c), F32) * lg)


def _ret_fwd(qrot, krot, vb, lgf, lgb, gnw):
    t = qrot.shape[0]
    c = RET_CHUNK
    nc = t // c
    hd = RET_HEAD_DIM
    unroll = 4 if nc % 4 == 0 else 1

    def body(lgf_ref, lgb_ref, qo_ref, ko_ref, vo_ref, w_ref, orr_ref, on_ref, kt, uf, ub, sfa, sba):
        h = pl.program_id(0)
        fw = _Dir(lgf_ref[h], False)
        bw = _Dir(lgb_ref[h], True)
        for i in range(nc):
            kt[i] = ko_ref[i * c:(i + 1) * c, :].astype(F32).T.astype(BF16)

        def rows(ci):
            return pl.ds(pl.multiple_of(ci * c, c), c)

        def kv_products(ci, carry):
            vv = vo_ref[rows(ci), :]
            ktf = kt[ci].astype(F32)
            uf[ci] = _dot((ktf * fw.kd_row).astype(BF16), vv)
            ub[ci] = _dot((ktf * bw.kd_row).astype(BF16), vv)
            return carry

        lax.fori_loop(0, nc, kv_products, 0, unroll=unroll)

        def scan(i, carry):
            sf, sb = carry
            j = nc - 1 - i
            sfa[i] = sf.astype(BF16)
            sba[j] = sb.astype(BF16)
            return sf * fw.cd + uf[i], sb * bw.cd + ub[j]

        zero = jnp.zeros((hd, hd), F32)
        lax.fori_loop(0, nc, scan, (zero, zero))
        gw = w_ref[...]

        def outputs(ci, carry):
            sl = rows(ci)
            qq, kk, vv = qo_ref[sl, :], ko_ref[sl, :], vo_ref[sl, :]
            a = _dot(qq, kk, NT)
            o = (_dot((a * fw.d).astype(BF16), vv) + _dot(qq, sfa[ci]) * fw.qd
                 + _dot((a * bw.d).astype(BF16), vv) + _dot(qq, sba[ci]) * bw.qd)
            orr_ref[sl, :] = o
            xc = o - jnp.mean(o, axis=-1, keepdims=True)
            var = jnp.mean(xc * xc, axis=-1, keepdims=True)
            on_ref[sl, :] = xc * lax.rsqrt(var + EPS) * gw
            return carry

        group = 32 if nc % 32 == 0 else 1

        def output_group(i, carry):
            for j in range(group):
                outputs(i * group + j, carry)
            return carry

        lax.fori_loop(0, nc // group, output_group, 0)

    smem = pl.BlockSpec(memory_space=pltpu.SMEM)
    head = pl.BlockSpec((t, 128), lambda h: (0, h))
    return pl.pallas_call(
        body, name="ret_fwd", grid=(RET_HEADS,),
        in_specs=[smem, smem, head, head, head, pl.BlockSpec((1, 128), lambda h: (0, h))],
        out_specs=[head, head],
        out_shape=[SDS((t, RET_WIDTH), F32)] * 2,
        scratch_shapes=[pltpu.VMEM((nc, hd, c), BF16), pltpu.VMEM((nc, hd, hd), F32), pltpu.VMEM((nc, hd, hd), F32),
                        pltpu.VMEM((nc, hd, hd), BF16), pltpu.VMEM((nc, hd, hd), BF16)],
        compiler_params=_params(("parallel",)),
    )(lgf, lgb, qrot, krot, vb, gnw)


def _merge_fwd(x, z, oa, on, wb_t, wout):
    t, d = x.shape
    tm = min(256, t)

    def body(x_ref, ga_ref, gr_ref, gm0_ref, gm1_ref, oa_ref, on_ref, wb_ref, wo_ref, xn_ref, ya_ref, yb_ref):
        ga, gr = ga_ref[...], gr_ref[...]
        ua = ga * _sigmoid(ga) * oa_ref[...]
        ub = gr * _sigmoid(gr) * on_ref[...]
        ya = _dot(ua.astype(BF16), wb_ref[:, :512], NT)
        yb = _dot(ub.astype(BF16), wb_ref[:, 512:], NT)
        ya_ref[...] = ya
        yb_ref[...] = yb
        merged = _sigmoid(gm0_ref[...]) * ya + _sigmoid(gm1_ref[...]) * yb
        xn_ref[...] = x_ref[...] + _dot(merged.astype(BF16), wo_ref[...])

    row = lambda w, j: pl.BlockSpec((tm, w), lambda i: (i, j))
    const = lambda shape: pl.BlockSpec(shape, lambda i: (0, 0))
    return pl.pallas_call(
        body, name="merge_fwd", grid=(t // tm,),
        in_specs=[row(d, 0), row(512, SEG["ga"][2] // 512), row(512, SEG["gr"][2] // 512),
                  row(1024, SEG["gm"][2] // 1024), row(1024, SEG["gm"][2] // 1024 + 1),
                  row(512, 0), row(512, 0), const((d, 1024)), const((d, d))],
        out_specs=[row(d, 0), row(d, 0), row(d, 0)],
        out_shape=[SDS((t, d), F32)] * 3,
        compiler_params=_params(("parallel",)),
    )(x, z, z, z, z, oa, on, wb_t, wout)


def _final_loss(x, g, target):
    t, d = x.shape
    tm = min(512, t)
    n = t // tm

    def body(x_ref, g_ref, t_ref, dx_ref, dg_ref, loss_ref, acc_g, acc_l):
        i = pl.program_id(0)

        @pl.when(i == 0)
        def _():
            acc_g[...] = jnp.zeros_like(acc_g)
            acc_l[...] = jnp.zeros_like(acc_l)

        xv, gv = x_ref[...], g_ref[...]
        r = lax.rsqrt(jnp.mean(xv * xv, axis=-1, keepdims=True) + EPS)
        xh = xv * r
        err = xh * gv - t_ref[...]
        dy = err * (1.0 / d)
        gy = dy * gv
        dx_ref[...] = r * (gy - xh * jnp.mean(gy * xh, axis=-1, keepdims=True))
        acc_g[...] += jnp.sum((dy * xh).reshape(tm // 8, 8, d), axis=0)
        acc_l[...] += jnp.sum((err * err).reshape(tm // 8, 8, d), axis=0)

        @pl.when(i == n - 1)
        def _():
            dg_ref[...] = jnp.sum(acc_g[...], axis=0, keepdims=True)
            tot = jnp.sum(jnp.sum(acc_l[...], axis=0, keepdims=True), axis=1, keepdims=True)
            loss_ref[...] = jnp.broadcast_to(tot * (0.5 / d), (1, 128))

    return pl.pallas_call(
        body, name="final_loss", grid=(n,),
        in_specs=[pl.BlockSpec((tm, d), lambda i: (i, 0)), pl.BlockSpec((1, d), lambda i: (0, 0)),
                  pl.BlockSpec((tm, d), lambda i: (i, 0))],
        out_specs=[pl.BlockSpec((tm, d), lambda i: (i, 0)), pl.BlockSpec((1, d), lambda i: (0, 0)),
                   pl.BlockSpec((1, 128), lambda i: (0, 0))],
        out_shape=[SDS((t, d), F32), SDS((1, d), F32), SDS((1, 128), F32)],
        scratch_shapes=[pltpu.VMEM((8, d), F32), pltpu.VMEM((8, d), F32)],
        compiler_params=_params(("arbitrary",)),
    )(x, g, target)


def _merge_bwd(dxo, z, oa, on, ya, yb, wb_t, wout):
    t, d = dxo.shape
    tm = min(256, t)
    n = t // tm

    def body(dx_ref, ga_ref, gr_ref, gm0_ref, gm1_ref, oa_ref, on_ref, ya_ref, yb_ref, wb_ref, wo_ref,
             doa_ref, don_ref, dz_ref, dwo_ref, dwb_ref, acc_o, acc_b):
        i = pl.program_id(0)

        @pl.when(i == 0)
        def _():
            acc_o[...] = jnp.zeros_like(acc_o)
            acc_b[...] = jnp.zeros_like(acc_b)

        dxb = dx_ref[...].astype(BF16)
        ya, yb = ya_ref[...], yb_ref[...]
        g0, g1 = _sigmoid(gm0_ref[...]), _sigmoid(gm1_ref[...])
        mb = (g0 * ya + g1 * yb).astype(BF16)
        dm = _dot(dxb, wo_ref[...], NT)
        dya = (dm * g0).astype(BF16)
        dyb = (dm * g1).astype(BF16)
        dz_ref[:, 1024:2048] = (dm * ya * g0 * (1.0 - g0)).astype(BF16)
        dz_ref[:, 2048:3072] = (dm * yb * g1 * (1.0 - g1)).astype(BF16)

        def branch(g_ref, o_ref, dy, w, do_ref, lo):
            gv, ov = g_ref[...], o_ref[...]
            sg = _sigmoid(gv)
            silu = gv * sg
            du = _dot(dy, w)
            do_ref[...] = du * silu
            dz_ref[:, lo:lo + 512] = (du * ov * (sg * (1.0 + gv * (1.0 - sg)))).astype(BF16)
            acc_b[:, lo:lo + 512] += _dot(dy, (silu * ov).astype(BF16), TN)

        branch(ga_ref, oa_ref, dya, wb_ref[:, :512], doa_ref, 0)
        branch(gr_ref, on_ref, dyb, wb_ref[:, 512:], don_ref, 512)
        acc_o[...] += _dot(mb, dxb, TN)

        @pl.when(i == n - 1)
        def _():
            dwo_ref[...] = acc_o[...].astype(BF16)
            dwb_ref[...] = acc_b[...].astype(BF16)

    row = lambda w, j: pl.BlockSpec((tm, w), lambda i: (i, j))
    const = lambda shape: pl.BlockSpec(shape, lambda i: (0, 0))
    return pl.pallas_call(
        body, name="merge_bwd", grid=(n,),
        in_specs=[row(d, 0), row(512, SEG["ga"][2] // 512), row(512, SEG["gr"][2] // 512),
                  row(1024, SEG["gm"][2] // 1024), row(1024, SEG["gm"][2] // 1024 + 1),
                  row(512, 0), row(512, 0), row(d, 0), row(d, 0), const((d, 1024)), const((d, d))],
        out_specs=[row(512, 0), row(512, 0), row(3072, 0), const((d, d)), const((d, 1024))],
        out_shape=[SDS((t, 512), F32), SDS((t, 512), F32), SDS((t, 3072), BF16), SDS((d, d), BF16),
                   SDS((d, 1024), BF16)],
        scratch_shapes=[pltpu.VMEM((d, d), F32), pltpu.VMEM((d, 1024), F32)],
        compiler_params=_params(("arbitrary",)),
    )(dxo, z, z, z, z, oa, on, ya, yb, wb_t, wout)


def _ret_bwd(qrot, krot, vb, orr, don, gnw, lgf, lgb, cos, sin):
    t = qrot.shape[0]
    c = RET_CHUNK
    nc = t // c
    hd = RET_HEAD_DIM
    unroll = 4 if nc % 4 == 0 else 1

    def body(lgf_ref, lgb_ref, q_ref, k_ref, v_ref, o_ref, dn_ref, w_ref, c_ref, s_ref,
             dq_ref, dk_ref, dv_ref, dw_ref, dlf_ref, dlb_ref, qt, kt, dob, uf, ub, wf, wb, sfa, sba, gfa, gba):
        h = pl.program_id(0)
        fw = _Dir(lgf_ref[h], False)
        bw = _Dir(lgb_ref[h], True)
        fw.dt, bw.dt = fw.d.T, bw.d.T

        o = o_ref[...]
        xc = o - jnp.mean(o, axis=-1, keepdims=True)
        r = lax.rsqrt(jnp.mean(xc * xc, axis=-1, keepdims=True) + EPS)
        xh = xc * r
        dn = dn_ref[...]
        gy = dn * w_ref[...]
        d_o = r * (gy - jnp.mean(gy, axis=-1, keepdims=True) - xh * jnp.mean(gy * xh, axis=-1, keepdims=True))
        dw_ref[...] = jnp.sum(dn * xh, axis=0, keepdims=True)
        dob[...] = d_o.astype(BF16)
        for i in range(nc):
            qt[i] = q_ref[i * c:(i + 1) * c, :].astype(F32).T.astype(BF16)
            kt[i] = k_ref[i * c:(i + 1) * c, :].astype(F32).T.astype(BF16)

        def rows(ci):
            return pl.ds(pl.multiple_of(ci * c, c), c)

        def products(ci, carry):
            sl = rows(ci)
            vv, do32 = v_ref[sl, :], dob[sl, :].astype(F32)
            ktf = kt[ci].astype(F32)
            uf[ci] = _dot((ktf * fw.kd_row).astype(BF16), vv)
            ub[ci] = _dot((ktf * bw.kd_row).astype(BF16), vv)
            wf[ci] = _dot(qt[ci], (do32 * fw.qd).astype(BF16))
            wb[ci] = _dot(qt[ci], (do32 * bw.qd).astype(BF16))
            return carry

        lax.fori_loop(0, nc, products, 0, unroll=unroll)

        def scan(i, carry):
            sf, sb, gf, gb = carry
            j = nc - 1 - i
            sfa[i] = sf.astype(BF16)
            sba[j] = sb.astype(BF16)
            gfa[j] = gf.astype(BF16)
            gba[i] = gb.astype(BF16)
            return sf * fw.cd + uf[i], sb * bw.cd + ub[j], gf * fw.cd + wf[j], gb * bw.cd + wb[i]

        zero = jnp.zeros((hd, hd), F32)
        lax.fori_loop(0, nc, scan, (zero, zero, zero, zero))

        def one_dir(p, s_all, g_all, ci, qq, kk, vv, do, a, bm):
            sb, gb = s_all[ci], g_all[ci]
            doq = (do.astype(F32) * p.qd).astype(BF16)
            dqc = _dot(doq, sb, NT)
            kkd = (kk.astype(F32) * p.kd_col).astype(BF16)
            dk2 = _dot(vv, gb, NT) * p.kd_col
            terms = (p.dist * p.d * a * bm + p.wq * qq.astype(F32) * dqc + p.wk * kk.astype(F32) * dk2
                     + (float(c) * p.cd) * gb.astype(F32) * sb.astype(F32))
            return dqc, dk2, _dot(kkd, gb), terms

        d_both, dt_both = fw.d + bw.d, fw.dt + bw.dt

        def chunk(ci, carry):
            af, ab = carry
            sl = rows(ci)
            qq, kk, vv, do = q_ref[sl, :], k_ref[sl, :], v_ref[sl, :], dob[sl, :]
            a, bm = _dot(qq, kk, NT), _dot(do, vv, NT)
            at, bt = _dot(kk, qq, NT), _dot(vv, do, NT)
            dqf, dkf, dvf, tf = one_dir(fw, sfa, gfa, ci, qq, kk, vv, do, a, bm)
            dqb, dkb, dvb, tb = one_dir(bw, sba, gba, ci, qq, kk, vv, do, a, bm)
            cc, ss = c_ref[sl, :], s_ref[sl, :]
            dq = _dot((bm * d_both).astype(BF16), kk) + dqf + dqb
            dk = _dot((bt * dt_both).astype(BF16), qq) + dkf + dkb
            dq_ref[sl, :] = _rope_bwd(dq, cc, ss, hd // 4).astype(BF16)
            dk_ref[sl, :] = (_rope_bwd(dk, cc, ss, hd // 4) * (hd ** -0.5)).astype(BF16)
            dv_ref[sl, :] = (_dot((at * dt_both).astype(BF16), do) + dvf + dvb).astype(BF16)
            return af + tf, ab + tb

        pair = 8 if nc % 8 == 0 else 1

        def chunks(i, carry):
            for j in range(pair):
                carry = chunk(i * pair + j, carry)
            return carry

        af, ab = lax.fori_loop(0, nc // pair, chunks, (zero, zero))
        tot = lambda m: jnp.sum(jnp.sum(m, axis=0, keepdims=True), axis=1, keepdims=True)
        dlf_ref[...] = jnp.broadcast_to(tot(af).reshape(1, 1, 1), (1, 8, 128))
        dlb_ref[...] = jnp.broadcast_to(tot(ab).reshape(1, 1, 1), (1, 8, 128))

    smem = pl.BlockSpec(memory_space=pltpu.SMEM)
    head = pl.BlockSpec((t, 128), lambda h: (0, h))
    vec = pl.BlockSpec((1, 128), lambda h: (0, h))
    scal = pl.BlockSpec((1, 8, 128), lambda h: (h, 0, 0))
    table = pl.BlockSpec((t, 128), lambda h: (0, 0))
    mats = lambda dt: pltpu.VMEM((nc, hd, hd), dt)
    return pl.pallas_call(
        body, name="ret_bwd", grid=(RET_HEADS,),
        in_specs=[smem, smem, head, head, head, head, head, vec, table, table],
        out_specs=[head, head, head, vec, scal, scal],
        out_shape=[SDS((t, RET_WIDTH), BF16)] * 3 + [SDS((1, RET_WIDTH), F32), SDS((RET_HEADS, 8, 128), F32),
                                                    SDS((RET_HEADS, 8, 128), F32)],
        scratch_shapes=[pltpu.VMEM((nc, hd, c), BF16), pltpu.VMEM((nc, hd, c), BF16), pltpu.VMEM((t, hd), BF16),
                        mats(F32), mats(F32), mats(F32), mats(F32), mats(BF16), mats(BF16), mats(BF16), mats(BF16)],
        compiler_params=_params(("parallel",)),
    )(lgf, lgb, qrot, krot, vb, orr, don, gnw, cos, sin)


def _attn_bwd(q, qt, k, v, doa, oa, lse, ex=None):
    t = q.shape[1]
    tq = min(ATTN_BWD_QUERY_TILE, t)
    nq = t // tq
    tk = min(ATTN_BWD_KEY_CHUNK, t)
    nk = t // tk
    hd = ATTN_HEAD_DIM
    scale = hd ** -0.5

    def body(q_ref, qt_ref, k_ref, v_ref, do_ref, o_ref, lse_ref, dq_ref, dkt_ref, dvt_ref):
        p, i = pl.program_id(0), pl.program_id(1)

        @pl.when(jnp.logical_and(p % 2 == 0, i == 0))
        def _():
            dkt_ref[...] = jnp.zeros_like(dkt_ref)
            dvt_ref[...] = jnp.zeros_like(dvt_ref)

        dov, ov = do_ref[...], o_ref[...]
        dovt = dov.T
        lanes = lambda col: jnp.concatenate([col] * (tk // 128), axis=1)
        outs = []
        for j in range(2):
            qq, qqt = q_ref[j], qt_ref[j]
            do32 = dov[:, j * hd:(j + 1) * hd]
            do, dot_ = do32.astype(BF16), dovt[j * hd:(j + 1) * hd, :].astype(BF16)
            dd = lanes(jnp.broadcast_to(jnp.sum(do32 * ov[:, j * hd:(j + 1) * hd], axis=1, keepdims=True), (tq, 128)))
            lse_j = lanes(jnp.broadcast_to(lse_ref[j], (128, tq)).T)
            dq = jnp.zeros((tq, hd), F32)
            for c in range(nk):
                sl = slice(c * tk, (c + 1) * tk)
                kc, vc = k_ref[0, sl, :], v_ref[0, sl, :]
                pr = jnp.exp(_dot(qq, kc, NT) - lse_j)
                ds = (pr * (_dot(do, vc, NT) - dd)).astype(BF16)
                dvt_ref[0, :, sl] += _dot(dot_, pr.astype(BF16))
                dkt_ref[0, :, sl] += _dot(qqt, ds)
                dq = dq + _dot(ds, kc)
            outs.append(dq * scale)
        dq_ref[...] = jnp.concatenate(outs, axis=-1)

    kv = pl.BlockSpec((1, t, hd), lambda p, i: (p // 2, 0, 0))
    kvt = pl.BlockSpec((1, hd, t), lambda p, i: (p // 2, 0, 0))
    pair = pl.BlockSpec((tq, 128), lambda p, i: (i, p))
    first = lambda: jnp.logical_and(pl.program_id(0) == 0, pl.program_id(1) == 0)
    last = lambda: jnp.logical_and(pl.program_id(0) == 3, pl.program_id(1) == nq - 1)
    xi, xo, xs, xscr, xargs = _ex_args(ex)
    return pl.pallas_call(
        _with_exchange(body, 7, 3, 0, ex, first, last), name="attn_bwd", grid=(4, nq),
        in_specs=[pl.BlockSpec((2, tq, hd), lambda p, i: (p, i, 0)), pl.BlockSpec((2, hd, tq), lambda p, i: (p, 0, i)),
                  kv, kv, pair, pair, pl.BlockSpec((2, 1, tq), lambda p, i: (p, 0, i))] + xi,
        out_specs=[pair, kvt, kvt] + xo,
        out_shape=[SDS((t, ATTN_WIDTH), F32), SDS((ATTN_KV_HEADS, hd, t), F32),
                   SDS((ATTN_KV_HEADS, hd, t), F32)] + xs,
        scratch_shapes=xscr,
        compiler_params=_params(("arbitrary", "arbitrary")),
    )(q, qt, k, v, doa, oa, lse, *xargs)


def _attn_post_bwd(dq, dk, dv, z, qn, kn, cos, sin, ones_bd):
    t = z.shape[0]
    tm = min(512, t)
    n = t // tm
    hd = ATTN_HEAD_DIM

    def body(dq_ref, dk_ref, dv_ref, zq_ref, zkv_ref, qn_ref, kn_ref, c_ref, s_ref, b_ref,
             dz_ref, dqn_ref, dkn_ref, acc_q, acc_k):
        i = pl.program_id(0)

        @pl.when(i == 0)
        def _():
            acc_q[...] = jnp.zeros_like(acc_q)
            acc_k[...] = jnp.zeros_like(acc_k)

        bd = b_ref[...]
        c2, s2 = c_ref[...], s_ref[...]

        def norm_bwd(dy, x, w, ones, cos_t, sin_t, acc):
            dyr = _rope_bwd(dy, cos_t, sin_t, hd // 4)
            r = lax.rsqrt(_group_mean(x * x, ones) + EPS)
            xh = x * r
            gy = dyr * w
            acc[...] += jnp.sum((dyr * xh).reshape(tm // 8, 8, x.shape[-1]), axis=0)
            return r * (gy - xh * _group_mean(gy * xh, ones))

        cq = jnp.concatenate([c2] * 4, axis=-1)
        sq = jnp.concatenate([s2] * 4, axis=-1)
        dz_ref[:, :512] = norm_bwd(dq_ref[...], zq_ref[...], qn_ref[...], bd, cq, sq, acc_q).astype(BF16)
        zkv = zkv_ref[...]
        dkk = jnp.concatenate([dk_ref[0], dk_ref[1]], axis=0).T
        dz_ref[:, 512:640] = norm_bwd(dkk, zkv[:, :128], kn_ref[...], bd[:128, :128], c2, s2, acc_k).astype(BF16)
        dz_ref[:, 640:768] = jnp.concatenate([dv_ref[0], dv_ref[1]], axis=0).T.astype(BF16)

        @pl.when(i == n - 1)
        def _():
            dqn_ref[...] = jnp.sum(acc_q[...], axis=0, keepdims=True)
            dkn_ref[...] = jnp.sum(acc_k[...], axis=0, keepdims=True)

    kv_blk = SEG["ka"][2] // 256
    kvs = pl.BlockSpec((ATTN_KV_HEADS, hd, tm), lambda i: (0, 0, i))
    const = lambda shape: pl.BlockSpec(shape, lambda i: (0, 0))
    return pl.pallas_call(
        body, name="attn_post_bwd", grid=(n,),
        in_specs=[pl.BlockSpec((tm, 512), lambda i: (i, 0)), kvs, kvs,
                  pl.BlockSpec((tm, 512), lambda i: (i, 0)), pl.BlockSpec((tm, 256), lambda i: (i, kv_blk)),
                  const((1, 512)), const((1, 128)),
                  pl.BlockSpec((tm, 128), lambda i: (i, 0)), pl.BlockSpec((tm, 128), lambda i: (i, 0)),
                  const((512, 512))],
        out_specs=[pl.BlockSpec((tm, 768), lambda i: (i, 0)), const((1, 512)), const((1, 128))],
        out_shape=[SDS((t, 768), BF16), SDS((1, 512), F32), SDS((1, 128), F32)],
        scratch_shapes=[pltpu.VMEM((8, 512), F32), pltpu.VMEM((8, 128), F32)],
        compiler_params=_params(("arbitrary",)),
    )(dq, dk, dv, z, z, qn, kn, cos, sin, ones_bd)


def _in_bwd(dxo, x, g, w_t, dz_a, dz_m, dqr, dkr, dvr, after=None):
    t, d = x.shape
    tm = min(256, t)
    n = t // tm
    parts = [(0, 0, 768, 0), (1, 0, 512, SEG["ga"][0]), (2, 0, 512, SEG["qr"][0]), (3, 0, 512, SEG["kr"][0]),
             (4, 0, 512, SEG["vr"][0]), (1, 512, 2560, SEG["gr"][0])]

    def body(dx_ref, x_ref, g_ref, w_ref, a_ref, m_ref, q_ref, k_ref, v_ref, o_ref, dg_ref, acc):
        i = pl.program_id(0)

        @pl.when(i == 0)
        def _():
            acc[...] = jnp.zeros_like(acc)

        pieces = [a_ref, m_ref, q_ref, k_ref, v_ref]
        dh = jnp.zeros((tm, d), F32)
        for pi, lo, w, row in parts:
            dh = dh + _dot(pieces[pi][:, lo:lo + w], w_ref[row:row + w, :])
        xv = x_ref[...]
        r = lax.rsqrt(jnp.mean(xv * xv, axis=-1, keepdims=True) + EPS)
        xh = xv * r
        gy = dh * g_ref[...]
        o_ref[...] = dx_ref[...] + r * (gy - xh * jnp.mean(gy * xh, axis=-1, keepdims=True))
        acc[...] += jnp.sum((dh * xh).reshape(tm // 8, 8, d), axis=0)

        @pl.when(i == n - 1)
        def _():
            dg_ref[...] = jnp.sum(acc[...], axis=0, keepdims=True)

    row = lambda w: pl.BlockSpec((tm, w), lambda i: (i, 0))
    const = lambda shape: pl.BlockSpec(shape, lambda i: (0, 0))
    extra = [] if after is None else [after]
    return pl.pallas_call(
        (lambda *refs: body(*refs[:9], *refs[9 + len(extra):])), name="in_bwd", grid=(n,),
        in_specs=[row(d), row(d), const((1, d)), const((D_IN, d)), row(768), row(3072), row(512), row(512),
                  row(512)] + [const(a.shape) for a in extra],
        out_specs=[row(d), const((1, d))],
        out_shape=[SDS((t, d), F32), SDS((1, d), F32)],
        scratch_shapes=[pltpu.VMEM((8, d), F32)],
        compiler_params=_params(("arbitrary",)),
    )(dxo, x, g, w_t, dz_a, dz_m, dqr, dkr, dvr, *extra)


def _dw_in(h_t, dz_a, dz_m, dqr, dkr, dvr):
    d, t = h_t.shape
    tn = 256
    parts = [(0, 0, 0, 3), (1, 0, SEG["ga"][0] // tn, 2), (2, 0, SEG["qr"][0] // tn, 2),
             (3, 0, SEG["kr"][0] // tn, 2), (4, 0, SEG["vr"][0] // tn, 2), (1, 2, SEG["gr"][0] // tn, 10)]
    pieces = [dz_a, dz_m, dqr, dkr, dvr]

    def col_block(pi):
        mine = [(c0, r0, n) for q, c0, r0, n in parts if q == pi]

        def index(j):
            c0, r0, n = mine[0]
            blk = c0 + jnp.clip(j - r0, 0, n - 1)
            for c0, r0, n in mine[1:]:
                blk = jnp.where(j >= r0, c0 + jnp.clip(j - r0, 0, n - 1), blk)
            return 0, blk

        return index

    def body(h_ref, *refs):
        o_ref = refs[-1]
        j = pl.program_id(0)
        for pi, _, r0, n in parts:
            @pl.when(jnp.logical_and(j >= r0, j < r0 + n))
            def _(p_ref=refs[pi]):
                o_ref[...] = _dot(h_ref[...], p_ref[...]).T.astype(BF16)

    return pl.pallas_call(
        body, name="dw_in", grid=(D_IN // tn,),
        in_specs=[pl.BlockSpec((d, t), lambda j: (0, 0))] + [pl.BlockSpec((t, tn), col_block(pi)) for pi in range(5)],
        out_specs=pl.BlockSpec((tn, d), lambda j: (j, 0)),
        out_shape=SDS((D_IN, d), BF16),
        compiler_params=_params(("arbitrary",)),
    )(h_t, *pieces)


def _adamw_math(w, g, m, v):
    mn = ADAM_B1 * m + (1.0 - ADAM_B1) * g
    vn = ADAM_B2 * v + (1.0 - ADAM_B2) * (g * g)
    m_hat = mn / (1.0 - ADAM_B1 ** ADAM_STEP)
    v_hat = vn / (1.0 - ADAM_B2 ** ADAM_STEP)
    return -ADAM_LR * (m_hat / (jnp.sqrt(v_hat) + ADAM_EPS) + ADAM_WD * w), mn, vn


def _sum_adamw(recvs, w, m, v, lane0, tn, layer0=0, prev=None, own=None):
    _, r, c = w.shape
    j0 = lane0 // tn
    n = len(recvs)
    has_own = own is not None

    def body(*refs):
        mine_ref, refs = (refs[0], refs[1:]) if has_own else (None, refs)
        w_ref, m_ref, v_ref = refs[n:n + 3]
        g_ref, d_ref, mo_ref, vo_ref = refs[-4:]

        def run(r_ref):
            def slot(s):
                if has_own:
                    return jnp.where(mine_ref[0] == s, refs[n + 3][...], r_ref[s]).astype(F32)
                return r_ref[s].astype(F32)

            g = slot(0)
            for s in range(1, N_DEV):
                g = g + slot(s)
            g_ref[0] = g
            d_ref[0], mo_ref[0], vo_ref[0] = _adamw_math(w_ref[0], g, m_ref[0], v_ref[0])

        for i in range(n):
            pl.when(pl.program_id(0) == i)(functools.partial(run, refs[i]))

    slots = pl.BlockSpec((N_DEV, r, tn), lambda i, j, *_: (0, 0, j0 + j))
    blk = pl.BlockSpec((1, r, tn), lambda i, j, *_: (layer0 + i, 0, j))
    before = [] if prev is None else list(prev)
    in_specs, args = [slots] * n + [blk] * 3, [*recvs, w, m, v]
    if has_own:
        assert n == 1
        in_specs.append(pl.BlockSpec((r, tn), lambda i, j, mine: (mine[0], j0 + j)))
        args.append(own[0])
    n_pre = len(args) + has_own
    return pl.pallas_call(
        body, name="sum_adamw",
        grid_spec=pltpu.PrefetchScalarGridSpec(
            num_scalar_prefetch=int(has_own), grid=(n, c // tn),
            in_specs=in_specs + [ANY] * len(before), out_specs=[blk] * 4),
        out_shape=[SDS(w.shape, F32)] * 4,
        input_output_aliases={n_pre + k: k for k in range(len(before))},
        compiler_params=_params(("parallel", "parallel")),
    )(*([own[1]] if has_own else []), *args, *before)


def _adamw(w, g, m, v):
    rows, cols = w.shape
    tr = 256 if rows % 256 == 0 else rows

    def body(w_ref, g_ref, m_ref, v_ref, d_ref, mo_ref, vo_ref):
        d_ref[...], mo_ref[...], vo_ref[...] = _adamw_math(w_ref[...], g_ref[...], m_ref[...], v_ref[...])

    blk = pl.BlockSpec((tr, cols), lambda i: (i, 0))
    return pl.pallas_call(
        body, name="adamw", grid=(rows // tr,),
        in_specs=[blk] * 4, out_specs=[blk] * 3, out_shape=[SDS((rows, cols), F32)] * 3,
        compiler_params=_params(("parallel",)),
    )(w, g, m, v)


def _all_gather(shards):
    na = len(shards)
    chips = (4, 2, 6)

    def body(*refs):
        ins, outs = refs[:na], refs[na:2 * na]
        send_sems, recv_sems, local_sems = refs[2 * na:]
        _, mine = _flip(0)

        def rows(a, idx):
            r = shards[a].shape[0]
            return outs[a].at[pl.ds(pl.multiple_of(idx * r, 16), r), :]

        def copy(a, slot, block_idx, to, src=None):
            return pltpu.make_async_remote_copy(
                src_ref=rows(a, block_idx) if src is None else src, dst_ref=rows(a, block_idx),
                send_sem=send_sems.at[a, slot], recv_sem=recv_sems.at[a, slot],
                device_id=to, device_id_type=MESH_ID)

        sibling, sibling_idx = _flip(1)
        local, started = [], []
        for a in range(na):
            cp = pltpu.make_async_copy(ins[a], rows(a, mine), local_sems.at[a])
            cp.start()
            local.append(cp)
            first = [copy(a, 0, mine, sibling, src=ins[a])]
            first += [copy(a, 1 + j, mine, _flip(k)[0], src=ins[a]) for j, k in enumerate(chips)]
            for cp in first:
                cp.start()
            started += first
        for a in range(na):
            for j, k in enumerate(chips):
                _, theirs = _flip(k)
                copy(a, 1 + j, theirs, _flip(0)[0]).wait_recv()
                fwd = copy(a, 4 + j, theirs, sibling)
                fwd.start()
                started.append(fwd)
        for a in range(na):
            copy(a, 0, sibling_idx, _flip(0)[0]).wait_recv()
            for j, k in enumerate(chips):
                _, theirs = _flip(k | 1)
                copy(a, 4 + j, theirs, _flip(0)[0]).wait_recv()
        for cp in started:
            cp.wait_send()
        for cp in local:
            cp.wait()

    return pl.pallas_call(
        body, name="all_gather_weights",
        in_specs=[ANY] * na, out_specs=[ANY] * na,
        out_shape=[SDS((N_DEV * s.shape[0], s.shape[1]), s.dtype) for s in shards],
        scratch_shapes=[pltpu.SemaphoreType.DMA((na, 7)), pltpu.SemaphoreType.DMA((na, 7)),
                        pltpu.SemaphoreType.DMA((na,))],
        compiler_params=pltpu.CompilerParams(has_side_effects=True),
    )(*shards)


def _scatter_blocks_of(g_ref, rows, idx):
    return g_ref.at[pl.ds(pl.multiple_of(idx * rows, 16), rows), :]


def _scatter_start(g):
    rows = g.shape[0] // N_DEV
    land_shape = (N_DEV, rows, g.shape[1])

    def body(g_ref, land_ref, send_sems, recv_sems, g_thru, land_thru, token):
        _, mine = _flip(0)
        for k in range(1, N_DEV):
            peer, theirs = _flip(k)
            pltpu.make_async_remote_copy(
                src_ref=_scatter_blocks_of(g_ref, rows, theirs), dst_ref=land_ref.at[mine],
                send_sem=send_sems.at[k - 1], recv_sem=recv_sems.at[k - 1],
                device_id=peer, device_id_type=MESH_ID).start()
        token[...] = jnp.zeros_like(token)

    hbm, sem = pl.BlockSpec(memory_space=pltpu.HBM), pl.BlockSpec(memory_space=pltpu.SEMAPHORE)
    return pl.pallas_call(
        body, name="scatter_start",
        out_shape=(pltpu.SemaphoreType.DMA((N_DEV - 1,)), pltpu.SemaphoreType.DMA((N_DEV - 1,)),
                   pltpu.HBM(g.shape, g.dtype), pltpu.HBM(land_shape, g.dtype), SDS((8, 128), F32)),
        in_specs=(hbm, hbm), out_specs=(sem, sem, hbm, hbm, pl.BlockSpec(memory_space=pltpu.VMEM)),
        input_output_aliases={0: 2, 1: 3},
        compiler_params=pltpu.CompilerParams(has_side_effects=pltpu.SideEffectType.DATAFLOW_SIDE_EFFECTING),
    )(pltpu.with_memory_space_constraint(g, pltpu.HBM),
      pltpu.with_memory_space_constraint(lax.empty(land_shape, g.dtype), pltpu.HBM))


def _scatter_wait(send_sems, recv_sems, g_thru, land_thru, after):
    rows = g_thru.shape[0] // N_DEV

    def body(g_ref, land_ref, send_sems, recv_sems, *rest):
        me, _ = _flip(0)
        for k in range(1, N_DEV):
            _, theirs = _flip(k)
            copy = pltpu.make_async_remote_copy(
                src_ref=_scatter_blocks_of(g_ref, rows, theirs), dst_ref=land_ref.at[theirs],
                send_sem=send_sems.at[k - 1], recv_sem=recv_sems.at[k - 1],
                device_id=me, device_id_type=MESH_ID)
            copy.wait_send()
            copy.wait_recv()

    hbm, sem = pl.BlockSpec(memory_space=pltpu.HBM), pl.BlockSpec(memory_space=pltpu.SEMAPHORE)
    return pl.pallas_call(
        body, name="scatter_wait",
        out_shape=(pltpu.HBM(g_thru.shape, g_thru.dtype), pltpu.HBM(land_thru.shape, land_thru.dtype)),
        in_specs=(hbm, hbm, sem, sem) + (ANY,) * len(after), out_specs=(hbm, hbm), input_output_aliases={0: 0, 1: 1},
        compiler_params=pltpu.CompilerParams(has_side_effects=pltpu.SideEffectType.DATAFLOW_SIDE_EFFECTING),
    )(g_thru, land_thru, send_sems, recv_sems, *after)


def _all_reduce_small(packed):
    shape = packed.shape

    def body(p_ref, o_ref, slots, send_sems, recv_sems):
        me, mine = _flip(0)
        slots[mine] = p_ref[...]
        sends = []
        for k in range(1, N_DEV):
            peer, _ = _flip(k)
            cp = pltpu.make_async_remote_copy(
                src_ref=p_ref, dst_ref=slots.at[mine], send_sem=send_sems.at[k - 1], recv_sem=recv_sems.at[k - 1],
                device_id=peer, device_id_type=MESH_ID)
            cp.start()
            sends.append(cp)
        for k in range(1, N_DEV):
            _, theirs = _flip(k)
            pltpu.make_async_remote_copy(
                src_ref=p_ref, dst_ref=slots.at[theirs], send_sem=send_sems.at[k - 1],
                recv_sem=recv_sems.at[k - 1], device_id=me, device_id_type=MESH_ID).wait_recv()
        for cp in sends:
            cp.wait_send()
        acc = slots[0]
        for s in range(1, N_DEV):
            acc = acc + slots[s]
        o_ref[...] = acc

    vm = pl.BlockSpec(memory_space=pltpu.VMEM)
    return pl.pallas_call(
        body, name="all_reduce_small", in_specs=[vm], out_specs=vm, out_shape=SDS(shape, F32),
        scratch_shapes=[pltpu.VMEM((N_DEV,) + shape, F32), pltpu.SemaphoreType.DMA((7,)),
                        pltpu.SemaphoreType.DMA((7,))],
        compiler_params=pltpu.CompilerParams(has_side_effects=True),
    )(packed)


def _layer_fwd(x, p, tabs, ex):
    z, h_t, q, qt, k, v, vt, qrot, krot, vb = _in_proj(x, p["norm_g"], p["w_in_t"], p["qn"], p["kn"], tabs["ca"],
                                                       tabs["sa"], tabs["ones"], tabs["cr"], tabs["sr"])
    oa, lse, *gathered = _attn_fwd(q, k, vt, ex)
    orr, on = _ret_fwd(qrot, krot, vb, p["lgf"], p["lgb"], p["gnw"])
    return z, h_t, q, qt, k, v, lse, oa, qrot, krot, vb, orr, on, gathered


def _layer_bwd(dxo, s, p, tabs, ex_attn, scatter_w_in):
    doa, don, dz_m, d_wout, d_wb_t = _merge_bwd(dxo, s["z"], s["oa"], s["on"], s["ya"], s["yb"], p["wb_t"], p["w_out"])
    dq_a, dk_a, dv_a, *recv_attn = _attn_bwd(s["q"], s["qt"], s["k"], s["v"], doa, s["oa"], s["lse"],
                                              ex_attn(d_wb_t, d_wout))
    dz_a, d_qn, d_kn = _attn_post_bwd(dq_a, dk_a, dv_a, s["z"], p["qn"], p["kn"], tabs["ca"], tabs["sa"],
                                      tabs["ones"])
    dqr, dkr, dvr, d_gnw, d_lgf, d_lgb = _ret_bwd(s["qrot"], s["krot"], s["vb"], s["orr"], don, p["gnw"],
                                                  p["lgf"], p["lgb"], tabs["cr"], tabs["sr"])
    buf = _dw_in(s["h_t"], dz_a, dz_m, dqr, dkr, dvr)
    pending, token = None, None
    if scatter_w_in:
        *pending, token = _scatter_start(buf)
    dx, d_norm_g = _in_bwd(dxo, s["x"], p["norm_g"], p["w_in_t"], dz_a, dz_m, dqr, dkr, dvr, token)
    grads = dict(w_in_t=buf, wb_t=d_wb_t, w_out=d_wout, norm_g=d_norm_g, gnw=d_gnw,
                 qn=d_qn.reshape(ATTN_Q_HEADS, ATTN_HEAD_DIM).sum(axis=0),
                 kn=d_kn.reshape(ATTN_KV_HEADS, ATTN_HEAD_DIM).sum(axis=0),
                 lgf=d_lgf[:, 0, 0], lgb=d_lgb[:, 0, 0])
    return dx, grads, recv_attn, pending


def _adamw_nd(w, g, m, v):
    shape = w.shape
    two_d = (1, shape[0]) if w.ndim == 1 else (-1, shape[-1])
    out = _adamw(w.reshape(two_d), g.reshape(two_d), m.reshape(two_d), v.reshape(two_d))
    return tuple(o.reshape(shape) for o in out)


def kernel(x, norm_g, w_in, attn_q_norm, attn_k_norm, ret_decay_fwd, ret_decay_bwd, ret_gn_w, w_branch_attn, w_branch_ret, w_out, final_norm_g, loss_target, m_norm_g, m_w_in, m_attn_q_norm, m_attn_k_norm, m_ret_decay_fwd, m_ret_decay_bwd, m_ret_gn_w, m_w_branch_attn, m_w_branch_ret, m_w_out, m_final_norm_g, v_norm_g, v_w_in, v_attn_q_norm, v_attn_k_norm, v_ret_decay_fwd, v_ret_decay_bwd, v_ret_gn_w, v_w_branch_attn, v_w_branch_ret, v_w_out, v_final_norm_g):
    t, d = x.shape[1], x.shape[2]
    x2, target = x[0], loss_target[0]

    w_in_sh, wb_sh, wout_sh = [], [], []
    for l in range(DEPTH):
        w_in_sh.append(jnp.swapaxes(w_in[l], 0, 1).astype(BF16))
        wb_sh.append(jnp.concatenate([w_branch_attn[l].T, w_branch_ret[l].T], axis=1).astype(BF16))
        wout_sh.append(w_out[l].astype(BF16))

    ca, sa = _rope_tables(t, ATTN_HEAD_DIM)
    cr, sr = _rope_tables(t, RET_HEAD_DIM)
    grp = jnp.arange(ATTN_WIDTH) // ATTN_HEAD_DIM
    tabs = dict(ca=jnp.tile(ca, (1, 2)), sa=jnp.tile(sa, (1, 2)), cr=cr, sr=sr,
                ones=jnp.where(grp[:, None] == grp[None, :], 1.0 / ATTN_HEAD_DIM, 0.0).astype(BF16))
    layers = []
    for l in range(DEPTH):
        layers.append(dict(
            norm_g=norm_g[l][None], qn=jnp.tile(attn_q_norm[l], ATTN_Q_HEADS)[None],
            kn=jnp.tile(attn_k_norm[l], ATTN_KV_HEADS)[None], gnw=ret_gn_w[l][None],
            lgf=jax.nn.log_sigmoid(ret_decay_fwd[l]), lgb=jax.nn.log_sigmoid(ret_decay_bwd[l])))

    layers[0]["w_in_t"], = _all_gather([w_in_sh[0]])
    gathers = [_Exchange("gather", [wb_sh[0], wout_sh[0], w_in_sh[1]]), _Exchange("gather", [wb_sh[1], wout_sh[1]])]
    h = x2
    saved = []
    for l in range(DEPTH):
        p = layers[l]
        z, h_t, q, qt, k, v, lse, oa, qrot, krot, vb, orr, on, got = _layer_fwd(h, p, tabs, gathers[l])
        p["wb_t"], p["w_out"] = got[0], got[1]
        if l == 0:
            layers[1]["w_in_t"] = got[2]
        xn, ya, yb = _merge_fwd(h, z, oa, on, p["wb_t"], p["w_out"])
        saved.append(dict(x=h, z=z, h_t=h_t, q=q, qt=qt, k=k, v=v, lse=lse, oa=oa, qrot=qrot, krot=krot, vb=vb,
                          orr=orr, on=on, ya=ya, yb=yb))
        h = xn
    dx, d_final_g, loss_part = _final_loss(h, final_norm_g[None], target)

    grads = [None] * DEPTH
    dx, grads[1], _, _ = _layer_bwd(dx, saved[1], layers[1], tabs, lambda *a: None, False)
    g1 = grads[1]
    ex_attn = lambda d_wb_t, d_wout: _Exchange("scatter", [g1["w_in_t"], g1["wb_t"], g1["w_out"], d_wb_t, d_wout])
    dx, grads[0], recv_attn, pending = _layer_bwd(dx, saved[0], layers[0], tabs, ex_attn, True)
    recv = [None, recv_attn[3], recv_attn[4], recv_attn[0], recv_attn[1], recv_attn[2]]
    tr = lambda a: jnp.swapaxes(a, 1, 2)
    w_in_t = (tr(w_in), tr(m_w_in), tr(v_w_in))
    sharded = {}
    w_in_l1 = _sum_adamw([recv[3]], *w_in_t, 0, 256, layer0=1)
    sharded[id(w_branch_attn)] = [tr(o) for o in _sum_adamw(
        [recv[1], recv[4]], tr(w_branch_attn), tr(m_w_branch_attn), tr(v_w_branch_attn), 0, 512)]
    sharded[id(w_branch_ret)] = [tr(o) for o in _sum_adamw(
        [recv[1], recv[4]], tr(w_branch_ret), tr(m_w_branch_ret), tr(v_w_branch_ret), 512, 512)]
    sharded[id(w_out)] = _sum_adamw([recv[2], recv[5]], w_out, m_w_out, v_w_out, 0, 256)
    g_wba, g_wbr, g_wout = (sharded[id(w)][0] for w in (w_branch_attn, w_branch_ret, w_out))

    packed = jnp.zeros((8, 1024), F32)
    for l in range(DEPTH):
        gl = grads[l]
        packed = packed.at[l].set(gl["norm_g"][0])
        packed = packed.at[2, 512 * l:512 * (l + 1)].set(gl["gnw"][0])
        packed = packed.at[4, 128 * l:128 * l + 64].set(gl["qn"])
        packed = packed.at[4, 256 + 128 * l:256 + 128 * l + 64].set(gl["kn"])
        packed = packed.at[4, 512 + 128 * l:512 + 128 * l + 4].set(gl["lgf"])
        packed = packed.at[4, 768 + 128 * l:768 + 128 * l + 4].set(gl["lgb"])
    packed = packed.at[3].set(d_final_g[0])
    packed = packed.at[5, 0].set(loss_part[0, 0])
    red = _all_reduce_small(packed)
    loss = red[5, 0]
    g_norm_g = red[0:2]
    g_gnw = red[2].reshape(DEPTH, RET_WIDTH)
    g_final = red[3]
    g_qn = jnp.stack([red[4, 128 * l:128 * l + 64] for l in range(DEPTH)])
    g_kn = jnp.stack([red[4, 256 + 128 * l:256 + 128 * l + 64] for l in range(DEPTH)])
    g_lgf = jnp.stack([red[4, 512 + 128 * l:512 + 128 * l + 4] for l in range(DEPTH)])
    g_lgb = jnp.stack([red[4, 768 + 128 * l:768 + 128 * l + 4] for l in range(DEPTH)])
    g_df = g_lgf * jax.nn.sigmoid(-ret_decay_fwd)
    g_db = g_lgb * jax.nn.sigmoid(-ret_decay_bwd)

    grad_w = [g_norm_g, None, g_qn, g_kn, g_df, g_db, g_gnw, g_wba, g_wbr, g_wout, g_final]
    weights = [norm_g, w_in, attn_q_norm, attn_k_norm, ret_decay_fwd, ret_decay_bwd, ret_gn_w, w_branch_attn,
               w_branch_ret, w_out, final_norm_g]
    ms = [m_norm_g, m_w_in, m_attn_q_norm, m_attn_k_norm, m_ret_decay_fwd, m_ret_decay_bwd, m_ret_gn_w,
          m_w_branch_attn, m_w_branch_ret, m_w_out, m_final_norm_g]
    vs = [v_norm_g, v_w_in, v_attn_q_norm, v_attn_k_norm, v_ret_decay_fwd, v_ret_decay_bwd, v_ret_gn_w,
          v_w_branch_attn, v_w_branch_ret, v_w_out, v_final_norm_g]
    upd = [None if w is w_in else sharded[id(w)][1:] if id(w) in sharded else _adamw_nd(w, g, m, v)
           for w, g, m, v in zip(weights, grad_w, ms, vs)]

    done = [dx, w_in_l1[0], g_wout] + [u[0] for w, u in zip(weights, upd) if u is not None and id(w) not in sharded]
    g_full, recv[0] = _scatter_wait(*pending, done)
    mine = (4 * lax.axis_index("x") + 2 * lax.axis_index("y") + lax.axis_index("c")).astype(jnp.int32)[None]
    w_in_upd = [tr(o) for o in _sum_adamw([recv[0]], *w_in_t, 0, 256, layer0=0, prev=w_in_l1, own=(g_full, mine))]
    grad_w[1], upd[1] = w_in_upd[0], w_in_upd[1:]
    return (loss, dx[None], *grad_w, *[u[0] for u in upd], *[u[1] for u in upd], *[u[2] for u in upd])
```

```python
import functools

import jax
import jax.numpy as jnp
from jax import lax
from jax.experimental import pallas as pl
from jax.experimental.pallas import tpu as pltpu

F32 = jnp.float32
BF16 = jnp.bfloat16
SDS = jax.ShapeDtypeStruct

D_MODEL = 1024
DEPTH = 2
GRID_W = 64
ATTN_Q_HEADS = 8
ATTN_KV_HEADS = 2
ATTN_HEAD_DIM = 64
ATTN_WIDTH = 512
ATTN_KV_WIDTH = 128
RET_HEADS = 4
RET_HEAD_DIM = 128
RET_WIDTH = 512
RET_CHUNK = 128
ATTN_KEY_CHUNK = 512
ATTN_BWD_KEY_CHUNK = 1024
ATTN_BWD_QUERY_TILE = 512
ATTN_FWD_QUERY_TILE = 512
QK_DOTS_PER_CHUNK = 4
EXP_LAG = 3
ROPE_THETA = 10000.0
EPS = 1e-6
D_IN = 5376
N_DEV = 8

ADAM_LR = 0.001
ADAM_B1 = 0.9
ADAM_B2 = 0.999
ADAM_EPS = 1e-08
ADAM_WD = 0.01
ADAM_STEP = 10

SEG = {
    "qa": (0, 512, 0),
    "ga": (768, 512, 512),
    "qr": (1280, 512, 1024),
    "kr": (1792, 512, 1536),
    "vr": (2304, 512, 2048),
    "gr": (2816, 512, 2560),
    "gm": (3328, 2048, 3072),
    "ka": (512, 128, 5120),
    "va": (640, 128, 5248),
}

VMEM_LIMIT = 60 * 1024 * 1024
NT = (((1,), (1,)), ((), ()))
TN = (((0,), (0,)), ((), ()))
MESH_ID = pl.DeviceIdType.MESH
ANY = pl.BlockSpec(memory_space=pl.ANY)


def _params(sem=None, vmem=VMEM_LIMIT):
    return pltpu.CompilerParams(dimension_semantics=sem, vmem_limit_bytes=vmem)


def _dot(a, b, dims=None):
    if dims is None:
        return jnp.dot(a, b, preferred_element_type=F32)
    return lax.dot_general(a, b, dims, preferred_element_type=F32)


def _sigmoid(x):
    return 1.0 / (1.0 + jnp.exp(-x))


def _swap_halves(x, q):
    n = x.shape[-1]
    axis = x.ndim - 1
    lane = lax.broadcasted_iota(jnp.int32, x.shape, axis)
    first = (lane % (2 * q)) < q
    return jnp.where(first, pltpu.roll(x, n - q, axis), pltpu.roll(x, q, axis))


def _rope(x, cos, sin_signed, q):
    return x * cos + _swap_halves(x, q) * sin_signed


def _rope_bwd(dy, cos, sin_signed, q):
    return dy * cos - _swap_halves(dy, q) * sin_signed


def _group_mean(v, ones_bd):
    hi = v.astype(BF16)
    lo = (v - hi.astype(F32)).astype(BF16)
    return _dot(hi, ones_bd) + _dot(lo, ones_bd)


def _rope_tables(t, head_dim):
    n_rows = t // GRID_W
    d_axis = head_dim // 2
    inv_freq = ROPE_THETA ** (-jnp.arange(0, d_axis, 2, dtype=F32) / d_axis)
    ar = jnp.arange(n_rows, dtype=F32)[:, None] * inv_freq
    ac = jnp.arange(GRID_W, dtype=F32)[:, None] * inv_freq
    by_row = lambda a: jnp.repeat(a, GRID_W, axis=0)
    by_col = lambda a: jnp.tile(a, (n_rows, 1))
    cr, sr, cc, sc = by_row(jnp.cos(ar)), by_row(jnp.sin(ar)), by_col(jnp.cos(ac)), by_col(jnp.sin(ac))
    return jnp.concatenate([cr, cr, cc, cc], axis=-1), jnp.concatenate([-sr, sr, -sc, sc], axis=-1)


def _me():
    return lax.axis_index("x"), lax.axis_index("y"), lax.axis_index("c")


def _flip(k):
    x, y, c = _me()
    px = 1 - x if k & 4 else x
    py = 1 - y if k & 2 else y
    pc = 1 - c if k & 1 else c
    return (px, py, pc), 4 * px + 2 * py + pc


class _Exchange:
    def __init__(self, kind, srcs):
        self.kind, self.srcs, self.n = kind, list(srcs), len(srcs)
        self.rows = [a.shape[0] if kind == "gather" else a.shape[0] // N_DEV for a in srcs]
        if kind == "gather":
            self.out_shape = [SDS((N_DEV * a.shape[0], a.shape[1]), a.dtype) for a in srcs]
        else:
            self.out_shape = [SDS((N_DEV, a.shape[0] // N_DEV, a.shape[1]), a.dtype) for a in srcs]
        self.scratch = [pltpu.SemaphoreType.DMA((self.n, N_DEV - 1)), pltpu.SemaphoreType.DMA((self.n, N_DEV - 1)),
                        pltpu.SemaphoreType.DMA((self.n,))]

    def _block(self, ref, a, idx):
        r = self.rows[a]
        return ref.at[pl.ds(pl.multiple_of(idx * r, 16), r), :]

    def _src(self, ins, a, idx):
        return ins[a] if self.kind == "gather" else self._block(ins[a], a, idx)

    def _dst(self, outs, a, idx):
        return self._block(outs[a], a, idx) if self.kind == "gather" else outs[a].at[idx]

    def _copies(self, ins, outs, sems):
        send_sems, recv_sems, local_sems = sems
        me, mine = _flip(0)
        local, sends, recvs = [], [], []
        for a in range(self.n):
            local.append(pltpu.make_async_copy(self._src(ins, a, mine), self._dst(outs, a, mine), local_sems.at[a]))
            for k in range(1, N_DEV):
                peer, theirs = _flip(k)
                sem = dict(send_sem=send_sems.at[a, k - 1], recv_sem=recv_sems.at[a, k - 1])
                sends.append(pltpu.make_async_remote_copy(
                    src_ref=self._src(ins, a, theirs), dst_ref=self._dst(outs, a, mine),
                    device_id=peer, device_id_type=MESH_ID, **sem))
                recvs.append(pltpu.make_async_remote_copy(
                    src_ref=self._dst(outs, a, theirs), dst_ref=self._dst(outs, a, theirs),
                    device_id=me, device_id_type=MESH_ID, **sem))
        return local, sends, recvs

    def start(self, ins, outs, sems):
        local, sends, _ = self._copies(ins, outs, sems)
        for cp in local + sends:
            cp.start()

    def wait(self, ins, outs, sems):
        local, sends, recvs = self._copies(ins, outs, sems)
        for cp in sends:
            cp.wait_send()
        for cp in recvs:
            cp.wait_recv()
        for cp in local:
            cp.wait()


def _with_exchange(body, n_in, n_out, n_scratch, ex, first, last):
    if ex is None:
        return body

    def wrapped(*refs):
        ins = refs[:n_in]
        ex_ins = refs[n_in:n_in + ex.n]
        outs = refs[n_in + ex.n:n_in + ex.n + n_out]
        ex_outs = refs[n_in + ex.n + n_out:n_in + 2 * ex.n + n_out]
        rest = refs[n_in + 2 * ex.n + n_out:]
        scratch, sems = rest[:n_scratch], rest[n_scratch:]

        @pl.when(first())
        def _():
            ex.start(ex_ins, ex_outs, sems)

        body(*ins, *outs, *scratch)

        @pl.when(last())
        def _():
            ex.wait(ex_ins, ex_outs, sems)

    return wrapped


def _ex_args(ex):
    if ex is None:
        return [], [], [], [], []
    return [ANY] * ex.n, [ANY] * ex.n, list(ex.out_shape), list(ex.scratch), list(ex.srcs)


def _in_proj(x, g, w_t, qn, kn, cos, sin, ones_bd, cos_r, sin_r):
    t, d = x.shape
    tm = min(256, t)
    tk = min(ATTN_KEY_CHUNK, t)
    per_chunk = tk // tm
    hd = ATTN_HEAD_DIM

    def body(x_ref, g_ref, w_ref, qn_ref, kn_ref, c_ref, s_ref, b_ref, cr_ref, sr_ref,
             z_ref, ht_ref, q_out, qt_out, k_out, v_out, vt_out, qr_out, kr_out, vr_out):
        xv = x_ref[...]
        r = lax.rsqrt(jnp.mean(xv * xv, axis=-1, keepdims=True) + EPS)
        h = xv * r * g_ref[...]
        ht_ref[...] = h.T.astype(BF16)
        hb = h.astype(BF16)
        def project(name):
            nat, w, off = SEG[name]
            zs = _dot(hb, w_ref[nat:nat + w, :], NT)
            z_ref[:, off:off + w] = zs
            return zs

        seg = {name: project(name) for name in ("qa", "ka", "va")}
        bd = b_ref[...]
        c2, s2 = c_ref[...], s_ref[...]
        cq = jnp.concatenate([c2] * 4, axis=-1)
        sq = jnp.concatenate([s2] * 4, axis=-1)
        xq, xk, xvv = seg["qa"], seg["ka"], seg["va"]
        yq = xq * lax.rsqrt(_group_mean(xq * xq, bd) + EPS) * qn_ref[...]
        yq = _rope(yq, cq, sq, hd // 4) * (hd ** -0.5)
        yqt = yq.T
        for hh in range(ATTN_Q_HEADS):
            q_out[hh] = yq[:, hh * hd:(hh + 1) * hd].astype(BF16)
            qt_out[hh] = yqt[hh * hd:(hh + 1) * hd, :].astype(BF16)
        yk = xk * lax.rsqrt(_group_mean(xk * xk, bd[:ATTN_KV_WIDTH, :ATTN_KV_WIDTH]) + EPS) * kn_ref[...]
        yk = _rope(yk, c2, s2, hd // 4)
        xvt = xvv.T
        ones = jnp.ones((hd, tm), F32)
        for hh in range(ATTN_KV_HEADS):
            k_out[hh] = yk[:, hh * hd:(hh + 1) * hd].astype(BF16)
            v_out[hh] = xvv[:, hh * hd:(hh + 1) * hd].astype(BF16)
            vt_out[hh, 0] = jnp.concatenate([xvt[hh * hd:(hh + 1) * hd, :], ones], axis=0).astype(BF16)
        rd = RET_HEAD_DIM
        cr = jnp.concatenate([cr_ref[...]] * RET_HEADS, axis=-1)
        sr = jnp.concatenate([sr_ref[...]] * RET_HEADS, axis=-1)
        qr_out[...] = _rope(project("qr"), cr, sr, rd // 4).astype(BF16)
        kr_out[...] = (_rope(project("kr"), cr, sr, rd // 4) * (rd ** -0.5)).astype(BF16)
        vr_out[...] = project("vr").astype(BF16)
        for name in ("ga", "gr", "gm"):
            project(name)

    const = lambda shape: pl.BlockSpec(shape, lambda i: (0,) * len(shape))
    rows = lambda w: pl.BlockSpec((tm, w), lambda i: (i, 0))
    return pl.pallas_call(
        body, name="in_proj", grid=(t // tm,),
        in_specs=[rows(d), const((1, d)), const((D_IN, d)), const((1, 512)), const((1, 128)), rows(128), rows(128),
                  const((512, 512)), rows(128), rows(128)],
        out_specs=[rows(D_IN), pl.BlockSpec((d, tm), lambda i: (0, i)),
                   pl.BlockSpec((ATTN_Q_HEADS, tm, hd), lambda i: (0, i, 0)),
                   pl.BlockSpec((ATTN_Q_HEADS, hd, tm), lambda i: (0, 0, i)),
                   pl.BlockSpec((ATTN_KV_HEADS, tm, hd), lambda i: (0, i, 0)),
                   pl.BlockSpec((ATTN_KV_HEADS, tm, hd), lambda i: (0, i, 0)),
                   pl.BlockSpec((ATTN_KV_HEADS, 1, 2 * hd, tm), lambda i: (0, i // per_chunk, 0, i % per_chunk)),
                   rows(RET_WIDTH), rows(RET_WIDTH), rows(RET_WIDTH)],
        out_shape=[SDS((t, D_IN), F32), SDS((d, t), BF16),
                   SDS((ATTN_Q_HEADS, t, hd), BF16), SDS((ATTN_Q_HEADS, hd, t), BF16),
                   SDS((ATTN_KV_HEADS, t, hd), BF16), SDS((ATTN_KV_HEADS, t, hd), BF16),
                   SDS((ATTN_KV_HEADS, t // tk, 2 * hd, tk), BF16)] + [SDS((t, RET_WIDTH), BF16)] * 3,
        compiler_params=_params(("parallel",)),
    )(x, g, w_t, qn, kn, cos, sin, ones_bd, cos_r, sin_r)


def _attn_fwd(q, k, vt, ex=None):
    t = q.shape[1]
    tq = min(ATTN_FWD_QUERY_TILE, t)
    nk, tk = vt.shape[1], vt.shape[3]
    hd = ATTN_HEAD_DIM
    g = ATTN_Q_HEADS // ATTN_KV_HEADS

    def body(q_ref, k_ref, vt_ref, o_ref, lse_ref, s_scr):
        def pass_a(h, c, m8):
            part = tk // QK_DOTS_PER_CHUNK
            for lo in range(c * tk, (c + 1) * tk, part):
                st = _dot(k_ref[0, lo:lo + part, :], q_ref[h], NT)
                s_scr[h % 2, lo:lo + part, :] = st
                m8 = jnp.maximum(m8, jnp.max(st.reshape(part // 8, 8, tq), axis=0))
            return m8

        def pass_b(h, c, m, acc, after):
            e = jnp.exp(s_scr[h % 2, c * tk:(c + 1) * tk, :] - (m + after * 0.0)).astype(BF16)
            return acc + _dot(vt_ref[0, c], e)

        neg = jnp.full((8, tq), -jnp.inf, F32)
        m8 = neg
        for c in range(nk):
            m8 = pass_a(0, c, m8)
        outs = []
        for h in range(g):
            m = jnp.max(m8, axis=0, keepdims=True)
            acc = jnp.zeros((2 * hd, tq), F32)
            m8 = neg
            done = [m] * EXP_LAG
            for c in range(nk):
                if h + 1 < g:
                    m8 = pass_a(h + 1, c, m8)
                acc = pass_b(h, c, m, acc, done[-EXP_LAG])
                done.append(m8[0:1, :] if h + 1 < g else acc[hd:hd + 1, :])
            l = acc[hd:hd + 1, :]
            outs.append((acc[:hd, :] / l).T)
            lse_ref[h] = m + jnp.log(l)
        o_ref[...] = jnp.concatenate(outs, axis=-1)

    nq = t // tq
    first = lambda: jnp.logical_and(pl.program_id(0) == 0, pl.program_id(1) == 0)
    last = lambda: jnp.logical_and(pl.program_id(0) == ATTN_KV_HEADS - 1, pl.program_id(1) == nq - 1)
    xi, xo, xs, xscr, xargs = _ex_args(ex)
    return pl.pallas_call(
        _with_exchange(body, 3, 2, 1, ex, first, last), name="attn_fwd", grid=(ATTN_KV_HEADS, nq),
        in_specs=[pl.BlockSpec((g, tq, hd), lambda p, i: (p, i, 0)),
                  pl.BlockSpec((1, t, hd), lambda p, i: (p, 0, 0)),
                  pl.BlockSpec((1, nk, 2 * hd, tk), lambda p, i: (p, 0, 0, 0))] + xi,
        out_specs=[pl.BlockSpec((tq, g * hd), lambda p, i: (i, p)),
                   pl.BlockSpec((g, 1, tq), lambda p, i: (p, 0, i))] + xo,
        out_shape=[SDS((t, ATTN_WIDTH), F32), SDS((ATTN_Q_HEADS, 1, t), F32)] + xs,
        scratch_shapes=[pltpu.VMEM((2, t, tq), F32)] + xscr,
        compiler_params=_params(("arbitrary", "arbitrary")),
    )(q, k, vt, *xargs)


class _Dir:
    def __init__(self, lg, strict_future):
        c = RET_CHUNK
        ia = lax.broadcasted_iota(jnp.int32, (c, c), 0).astype(F32)
        ib = lax.broadcasted_iota(jnp.int32, (c, c), 1).astype(F32)
        col = lax.broadcasted_iota(jnp.int32, (c, 1), 0).astype(F32)
        row = lax.broadcasted_iota(jnp.int32, (1, c), 1).astype(F32)
        if strict_future:
            dist = ib - ia
            mask = dist > 0
            self.wq, self.wk, wk_row = c - col, col, row
        else:
            dist = ia - ib
            mask = dist >= 0
            self.wq, self.wk, wk_row = col + 1.0, c - 1.0 - col, c - 1.0 - row
        self.dist = jnp.maximum(dist, 0.0)
        self.d = jnp.where(mask, jnp.exp(self.dist * lg), 0.0)
        self.qd = jnp.exp(self.wq * lg)
        self.kd_col = jnp.exp(self.wk * lg)
        self.kd_row = jnp.exp(wk_row * lg)
        self.cd = jnp.exp(jnp.full((1, 1), float(c), F32) * lg)


def _ret_fwd(qrot, krot, vb, lgf, lgb, gnw):
    t = qrot.shape[0]
    c = RET_CHUNK
    nc = t // c
    hd = RET_HEAD_DIM
    unroll = 4 if nc % 4 == 0 else 1

    def body(lgf_ref, lgb_ref, qo_ref, ko_ref, vo_ref, w_ref, orr_ref, on_ref, kt, uf, ub, sfa, sba):
        h = pl.program_id(0)
        fw = _Dir(lgf_ref[h], False)
        bw = _Dir(lgb_ref[h], True)
        for i in range(nc):
            kt[i] = ko_ref[i * c:(i + 1) * c, :].astype(F32).T.astype(BF16)

        def rows(ci):
            return pl.ds(pl.multiple_of(ci * c, c), c)

        def kv_products(ci, carry):
            vv = vo_ref[rows(ci), :]
            ktf = kt[ci].astype(F32)
            uf[ci] = _dot((ktf * fw.kd_row).astype(BF16), vv)
            ub[ci] = _dot((ktf * bw.kd_row).astype(BF16), vv)
            return carry

        lax.fori_loop(0, nc, kv_products, 0, unroll=unroll)

        def scan(i, carry):
            sf, sb = carry
            j = nc - 1 - i
            sfa[i] = sf.astype(BF16)
            sba[j] = sb.astype(BF16)
            return sf * fw.cd + uf[i], sb * bw.cd + ub[j]

        zero = jnp.zeros((hd, hd), F32)
        lax.fori_loop(0, nc, scan, (zero, zero))
        gw = w_ref[...]

        def outputs(ci, carry):
            sl = rows(ci)
            qq, kk, vv = qo_ref[sl, :], ko_ref[sl, :], vo_ref[sl, :]
            a = _dot(qq, kk, NT)
            o = (_dot((a * fw.d).astype(BF16), vv) + _dot(qq, sfa[ci]) * fw.qd
                 + _dot((a * bw.d).astype(BF16), vv) + _dot(qq, sba[ci]) * bw.qd)
            orr_ref[sl, :] = o
            xc = o - jnp.mean(o, axis=-1, keepdims=True)
            var = jnp.mean(xc * xc, axis=-1, keepdims=True)
            on_ref[sl, :] = xc * lax.rsqrt(var + EPS) * gw
            return carry

        group = 32 if nc % 32 == 0 else 1

        def output_group(i, carry):
            for j in range(group):
                outputs(i * group + j, carry)
            return carry

        lax.fori_loop(0, nc // group, output_group, 0)

    smem = pl.BlockSpec(memory_space=pltpu.SMEM)
    head = pl.BlockSpec((t, 128), lambda h: (0, h))
    return pl.pallas_call(
        body, name="ret_fwd", grid=(RET_HEADS,),
        in_specs=[smem, smem, head, head, head, pl.BlockSpec((1, 128), lambda h: (0, h))],
        out_specs=[head, head],
        out_shape=[SDS((t, RET_WIDTH), F32)] * 2,
        scratch_shapes=[pltpu.VMEM((nc, hd, c), BF16), pltpu.VMEM((nc, hd, hd), F32), pltpu.VMEM((nc, hd, hd), F32),
                        pltpu.VMEM((nc, hd, hd), BF16), pltpu.VMEM((nc, hd, hd), BF16)],
        compiler_params=_params(("parallel",)),
    )(lgf, lgb, qrot, krot, vb, gnw)


def _merge_fwd(x, z, oa, on, wb_t, wout, head=None):
    t, d = x.shape
    tm = min(256, t)
    n = t // tm

    def body(x_ref, ga_ref, gr_ref, gm0_ref, gm1_ref, oa_ref, on_ref, wb_ref, wo_ref, *rest):
        ga, gr = ga_ref[...], gr_ref[...]
        ua = ga * _sigmoid(ga) * oa_ref[...]
        ub = gr * _sigmoid(gr) * on_ref[...]
        ya = _dot(ua.astype(BF16), wb_ref[:, :512], NT)
        yb = _dot(ub.astype(BF16), wb_ref[:, 512:], NT)
        merged = _sigmoid(gm0_ref[...]) * ya + _sigmoid(gm1_ref[...]) * yb
        xn = x_ref[...] + _dot(merged.astype(BF16), wo_ref[...])
        if head is None:
            xn_ref, ya_ref, yb_ref = rest
            xn_ref[...] = xn
        else:
            g_ref, t_ref, dx_ref, ya_ref, yb_ref, dg_ref, loss_ref, acc_g, acc_l = rest
            i = pl.program_id(0)

            @pl.when(i == 0)
            def _():
                acc_g[...] = jnp.zeros_like(acc_g)
                acc_l[...] = jnp.zeros_like(acc_l)

            gv = g_ref[...]
            r = lax.rsqrt(jnp.mean(xn * xn, axis=-1, keepdims=True) + EPS)
            xh = xn * r
            err = xh * gv - t_ref[...]
            dy = err * (1.0 / d)
            gy = dy * gv
            dx_ref[...] = r * (gy - xh * jnp.mean(gy * xh, axis=-1, keepdims=True))
            acc_g[...] += jnp.sum((dy * xh).reshape(tm // 8, 8, d), axis=0)
            acc_l[...] += jnp.sum((err * err).reshape(tm // 8, 8, d), axis=0)

            @pl.when(i == n - 1)
            def _():
                dg_ref[...] = jnp.sum(acc_g[...], axis=0, keepdims=True)
                tot = jnp.sum(jnp.sum(acc_l[...], axis=0, keepdims=True), axis=1, keepdims=True)
                loss_ref[...] = jnp.broadcast_to(tot * (0.5 / d), (1, 128))
        ya_ref[...] = ya.astype(BF16)
        yb_ref[...] = yb.astype(BF16)

    row = lambda w, j: pl.BlockSpec((tm, w), lambda i: (i, j))
    const = lambda shape: pl.BlockSpec(shape, lambda i: (0, 0))
    in_specs = [row(d, 0), row(512, SEG["ga"][2] // 512), row(512, SEG["gr"][2] // 512),
                row(1024, SEG["gm"][2] // 1024), row(1024, SEG["gm"][2] // 1024 + 1),
                row(512, 0), row(512, 0), const((d, 1024)), const((d, d))]
    out_specs = [row(d, 0), row(d, 0), row(d, 0)]
    out_shape = [SDS((t, d), F32), SDS((t, d), BF16), SDS((t, d), BF16)]
    args, scratch = [x, z, z, z, z, oa, on, wb_t, wout], []
    if head is not None:
        in_specs += [const((1, d)), row(d, 0)]
        out_specs += [const((1, d)), const((1, 128))]
        out_shape += [SDS((1, d), F32), SDS((1, 128), F32)]
        args += list(head)
        scratch = [pltpu.VMEM((8, d), F32), pltpu.VMEM((8, d), F32)]
    return pl.pallas_call(
        body, name="merge_fwd", grid=(n,), in_specs=in_specs, out_specs=out_specs, out_shape=out_shape,
        scratch_shapes=scratch,
        compiler_params=_params(("arbitrary",) if head is not None else ("parallel",)),
    )(*args)


def _merge_bwd(dxo, z, oa, on, ya, yb, wb_t, wout):
    t, d = dxo.shape
    tm = min(256, t)
    n = t // tm

    def body(dx_ref, ga_ref, gr_ref, gm0_ref, gm1_ref, oa_ref, on_ref, ya_ref, yb_ref, wb_ref, wo_ref,
             doa_ref, don_ref, dz_ref, dwo_ref, dwb_ref, acc_o, acc_b):
        i = pl.program_id(0)

        @pl.when(i == 0)
        def _():
            acc_o[...] = jnp.zeros_like(acc_o)
            acc_b[...] = jnp.zeros_like(acc_b)

        dxb = dx_ref[...].astype(BF16)
        ya, yb = ya_ref[...].astype(F32), yb_ref[...].astype(F32)
        g0, g1 = _sigmoid(gm0_ref[...]), _sigmoid(gm1_ref[...])
        mb = (g0 * ya + g1 * yb).astype(BF16)
        dm = _dot(dxb, wo_ref[...], NT)
        dya = (dm * g0).astype(BF16)
        dyb = (dm * g1).astype(BF16)
        dz_ref[:, 1024:2048] = (dm * ya * g0 * (1.0 - g0)).astype(BF16)
        dz_ref[:, 2048:3072] = (dm * yb * g1 * (1.0 - g1)).astype(BF16)

        def branch(g_ref, o_ref, dy, w, do_ref, lo):
            gv, ov = g_ref[...], o_ref[...]
            sg = _sigmoid(gv)
            silu = gv * sg
            du = _dot(dy, w)
            do_ref[...] = du * silu
            dz_ref[:, lo:lo + 512] = (du * ov * (sg * (1.0 + gv * (1.0 - sg)))).astype(BF16)
            acc_b[:, lo:lo + 512] += _dot(dy, (silu * ov).astype(BF16), TN)

        branch(ga_ref, oa_ref, dya, wb_ref[:, :512], doa_ref, 0)
        branch(gr_ref, on_ref, dyb, wb_ref[:, 512:], don_ref, 512)
        acc_o[...] += _dot(mb, dxb, TN)

        @pl.when(i == n - 1)
        def _():
            dwo_ref[...] = acc_o[...].astype(BF16)
            dwb_ref[...] = acc_b[...].astype(BF16)

    row = lambda w, j: pl.BlockSpec((tm, w), lambda i: (i, j))
    const = lambda shape: pl.BlockSpec(shape, lambda i: (0, 0))
    return pl.pallas_call(
        body, name="merge_bwd", grid=(n,),
        in_specs=[row(d, 0), row(512, SEG["ga"][2] // 512), row(512, SEG["gr"][2] // 512),
                  row(1024, SEG["gm"][2] // 1024), row(1024, SEG["gm"][2] // 1024 + 1),
                  row(512, 0), row(512, 0), row(d, 0), row(d, 0), const((d, 1024)), const((d, d))],
        out_specs=[row(512, 0), row(512, 0), row(3072, 0), const((d, d)), const((d, 1024))],
        out_shape=[SDS((t, 512), F32), SDS((t, 512), F32), SDS((t, 3072), BF16), SDS((d, d), BF16),
                   SDS((d, 1024), BF16)],
        scratch_shapes=[pltpu.VMEM((d, d), F32), pltpu.VMEM((d, 1024), F32)],
        compiler_params=_params(("arbitrary",)),
    )(dxo, z, z, z, z, oa, on, ya, yb, wb_t, wout)


def _ret_bwd(qrot, krot, vb, orr, don, gnw, lgf, lgb, cos, sin):
    t = qrot.shape[0]
    c = RET_CHUNK
    nc = t // c
    hd = RET_HEAD_DIM
    unroll = 4 if nc % 4 == 0 else 1

    def body(lgf_ref, lgb_ref, q_ref, k_ref, v_ref, o_ref, dn_ref, w_ref, c_ref, s_ref,
             dq_ref, dk_ref, dv_ref, dw_ref, dlf_ref, dlb_ref, qt, kt, dob, uf, ub, wf, wb, sfa, sba, gfa, gba):
        h = pl.program_id(0)
        fw = _Dir(lgf_ref[h], False)
        bw = _Dir(lgb_ref[h], True)
        fw.dt, bw.dt = fw.d.T, bw.d.T

        o = o_ref[...]
        xc = o - jnp.mean(o, axis=-1, keepdims=True)
        r = lax.rsqrt(jnp.mean(xc * xc, axis=-1, keepdims=True) + EPS)
        xh = xc * r
        dn = dn_ref[...]
        gy = dn * w_ref[...]
        d_o = r * (gy - jnp.mean(gy, axis=-1, keepdims=True) - xh * jnp.mean(gy * xh, axis=-1, keepdims=True))
        dw_ref[...] = jnp.sum(dn * xh, axis=0, keepdims=True)
        dob[...] = d_o.astype(BF16)
        for i in range(nc):
            qt[i] = q_ref[i * c:(i + 1) * c, :].astype(F32).T.astype(BF16)
            kt[i] = k_ref[i * c:(i + 1) * c, :].astype(F32).T.astype(BF16)

        def rows(ci):
            return pl.ds(pl.multiple_of(ci * c, c), c)

        def products(ci, carry):
            sl = rows(ci)
            vv, do32 = v_ref[sl, :], dob[sl, :].astype(F32)
            ktf = kt[ci].astype(F32)
            uf[ci] = _dot((ktf * fw.kd_row).astype(BF16), vv)
            ub[ci] = _dot((ktf * bw.kd_row).astype(BF16), vv)
            wf[ci] = _dot(qt[ci], (do32 * fw.qd).astype(BF16))
            wb[ci] = _dot(qt[ci], (do32 * bw.qd).astype(BF16))
            return carry

        lax.fori_loop(0, nc, products, 0, unroll=unroll)

        def scan(i, carry):
            sf, sb, gf, gb = carry
            j = nc - 1 - i
            sfa[i] = sf.astype(BF16)
            sba[j] = sb.astype(BF16)
            gfa[j] = gf.astype(BF16)
            gba[i] = gb.astype(BF16)
            return sf * fw.cd + uf[i], sb * bw.cd + ub[j], gf * fw.cd + wf[j], gb * bw.cd + wb[i]

        zero = jnp.zeros((hd, hd), F32)
        lax.fori_loop(0, nc, scan, (zero, zero, zero, zero))

        def one_dir(p, s_all, g_all, ci, qq, kk, vv, do, a, bm):
            sb, gb = s_all[ci], g_all[ci]
            doq = (do.astype(F32) * p.qd).astype(BF16)
            dqc = _dot(doq, sb, NT)
            kkd = (kk.astype(F32) * p.kd_col).astype(BF16)
            dk2 = _dot(vv, gb, NT) * p.kd_col
            terms = (p.dist * p.d * a * bm + p.wq * qq.astype(F32) * dqc + p.wk * kk.astype(F32) * dk2
                     + (float(c) * p.cd) * gb.astype(F32) * sb.astype(F32))
            return dqc, dk2, _dot(kkd, gb), terms

        d_both, dt_both = fw.d + bw.d, fw.dt + bw.dt

        def chunk(ci, carry):
            af, ab = carry
            sl = rows(ci)
            qq, kk, vv, do = q_ref[sl, :], k_ref[sl, :], v_ref[sl, :], dob[sl, :]
            a, bm = _dot(qq, kk, NT), _dot(do, vv, NT)
            at, bt = _dot(kk, qq, NT), _dot(vv, do, NT)
            dqf, dkf, dvf, tf = one_dir(fw, sfa, gfa, ci, qq, kk, vv, do, a, bm)
            dqb, dkb, dvb, tb = one_dir(bw, sba, gba, ci, qq, kk, vv, do, a, bm)
            cc, ss = c_ref[sl, :], s_ref[sl, :]
            dq = _dot((bm * d_both).astype(BF16), kk) + dqf + dqb
            dk = _dot((bt * dt_both).astype(BF16), qq) + dkf + dkb
            dq_ref[sl, :] = _rope_bwd(dq, cc, ss, hd // 4).astype(BF16)
            dk_ref[sl, :] = (_rope_bwd(dk, cc, ss, hd // 4) * (hd ** -0.5)).astype(BF16)
            dv_ref[sl, :] = (_dot((at * dt_both).astype(BF16), do) + dvf + dvb).astype(BF16)
            return af + tf, ab + tb

        pair = 8 if nc % 8 == 0 else 1

        def chunks(i, carry):
            for j in range(pair):
                carry = chunk(i * pair + j, carry)
            return carry

        af, ab = lax.fori_loop(0, nc // pair, chunks, (zero, zero))
        tot = lambda m: jnp.sum(jnp.sum(m, axis=0, keepdims=True), axis=1, keepdims=True)
        dlf_ref[...] = jnp.broadcast_to(tot(af).reshape(1, 1, 1), (1, 8, 128))
        dlb_ref[...] = jnp.broadcast_to(tot(ab).reshape(1, 1, 1), (1, 8, 128))

    smem = pl.BlockSpec(memory_space=pltpu.SMEM)
    head = pl.BlockSpec((t, 128), lambda h: (0, h))
    vec = pl.BlockSpec((1, 128), lambda h: (0, h))
    scal = pl.BlockSpec((1, 8, 128), lambda h: (h, 0, 0))
    table = pl.BlockSpec((t, 128), lambda h: (0, 0))
    mats = lambda dt: pltpu.VMEM((nc, hd, hd), dt)
    return pl.pallas_call(
        body, name="ret_bwd", grid=(RET_HEADS,),
        in_specs=[smem, smem, head, head, head, head, head, vec, table, table],
        out_specs=[head, head, head, vec, scal, scal],
        out_shape=[SDS((t, RET_WIDTH), BF16)] * 3 + [SDS((1, RET_WIDTH), F32), SDS((RET_HEADS, 8, 128), F32),
                                                    SDS((RET_HEADS, 8, 128), F32)],
        scratch_shapes=[pltpu.VMEM((nc, hd, c), BF16), pltpu.VMEM((nc, hd, c), BF16), pltpu.VMEM((t, hd), BF16),
                        mats(F32), mats(F32), mats(F32), mats(F32), mats(BF16), mats(BF16), mats(BF16), mats(BF16)],
        compiler_params=_params(("parallel",)),
    )(lgf, lgb, qrot, krot, vb, orr, don, gnw, cos, sin)


def _attn_bwd(q, qt, k, v, doa, oa, lse, ex=None):
    t = q.shape[1]
    tq = min(ATTN_BWD_QUERY_TILE, t)
    nq = t // tq
    tk = min(ATTN_BWD_KEY_CHUNK, t)
    nk = t // tk
    hd = ATTN_HEAD_DIM
    scale = hd ** -0.5

    def body(q_ref, qt_ref, k_ref, v_ref, do_ref, o_ref, lse_ref, dq_ref, dkt_ref, dvt_ref):
        p, i = pl.program_id(0), pl.program_id(1)

        @pl.when(jnp.logical_and(p % 2 == 0, i == 0))
        def _():
            dkt_ref[...] = jnp.zeros_like(dkt_ref)
            dvt_ref[...] = jnp.zeros_like(dvt_ref)

        dov, ov = do_ref[...], o_ref[...]
        dovt = dov.T
        lanes = lambda col: jnp.concatenate([col] * (tk // 128), axis=1)
        outs = []
        for j in range(2):
            qq, qqt = q_ref[j], qt_ref[j]
            do32 = dov[:, j * hd:(j + 1) * hd]
            do, dot_ = do32.astype(BF16), dovt[j * hd:(j + 1) * hd, :].astype(BF16)
            dd = lanes(jnp.broadcast_to(jnp.sum(do32 * ov[:, j * hd:(j + 1) * hd], axis=1, keepdims=True), (tq, 128)))
            lse_j = lanes(jnp.broadcast_to(lse_ref[j], (128, tq)).T)
            dq = jnp.zeros((tq, hd), F32)
            for c in range(nk):
                sl = slice(c * tk, (c + 1) * tk)
                kc, vc = k_ref[0, sl, :], v_ref[0, sl, :]
                pr = jnp.exp(_dot(qq, kc, NT) - lse_j)
                ds = (pr * (_dot(do, vc, NT) - dd)).astype(BF16)
                dvt_ref[0, :, sl] += _dot(dot_, pr.astype(BF16))
                dkt_ref[0, :, sl] += _dot(qqt, ds)
                dq = dq + _dot(ds, kc)
            outs.append(dq * scale)
        dq_ref[...] = jnp.concatenate(outs, axis=-1)

    kv = pl.BlockSpec((1, t, hd), lambda p, i: (p // 2, 0, 0))
    kvt = pl.BlockSpec((1, hd, t), lambda p, i: (p // 2, 0, 0))
    pair = pl.BlockSpec((tq, 128), lambda p, i: (i, p))
    first = lambda: jnp.logical_and(pl.program_id(0) == 0, pl.program_id(1) == 0)
    last = lambda: jnp.logical_and(pl.program_id(0) == 3, pl.program_id(1) == nq - 1)
    xi, xo, xs, xscr, xargs = _ex_args(ex)
    return pl.pallas_call(
        _with_exchange(body, 7, 3, 0, ex, first, last), name="attn_bwd", grid=(4, nq),
        in_specs=[pl.BlockSpec((2, tq, hd), lambda p, i: (p, i, 0)), pl.BlockSpec((2, hd, tq), lambda p, i: (p, 0, i)),
                  kv, kv, pair, pair, pl.BlockSpec((2, 1, tq), lambda p, i: (p, 0, i))] + xi,
        out_specs=[pair, kvt, kvt] + xo,
        out_shape=[SDS((t, ATTN_WIDTH), F32), SDS((ATTN_KV_HEADS, hd, t), F32),
                   SDS((ATTN_KV_HEADS, hd, t), F32)] + xs,
        scratch_shapes=xscr,
        compiler_params=_params(("arbitrary", "arbitrary")),
    )(q, qt, k, v, doa, oa, lse, *xargs)


def _attn_post_bwd(dq, dk, dv, z, qn, kn, cos, sin, ones_bd):
    t = z.shape[0]
    tm = min(512, t)
    n = t // tm
    hd = ATTN_HEAD_DIM

    def body(dq_ref, dk_ref, dv_ref, zq_ref, zkv_ref, qn_ref, kn_ref, c_ref, s_ref, b_ref,
             dz_ref, dqn_ref, dkn_ref, acc_q, acc_k):
        i = pl.program_id(0)

        @pl.when(i == 0)
        def _():
            acc_q[...] = jnp.zeros_like(acc_q)
            acc_k[...] = jnp.zeros_like(acc_k)

        bd = b_ref[...]
        c2, s2 = c_ref[...], s_ref[...]

        def norm_bwd(dy, x, w, ones, cos_t, sin_t, acc):
            dyr = _rope_bwd(dy, cos_t, sin_t, hd // 4)
            r = lax.rsqrt(_group_mean(x * x, ones) + EPS)
            xh = x * r
            gy = dyr * w
            acc[...] += jnp.sum((dyr * xh).reshape(tm // 8, 8, x.shape[-1]), axis=0)
            return r * (gy - xh * _group_mean(gy * xh, ones))

        cq = jnp.concatenate([c2] * 4, axis=-1)
        sq = jnp.concatenate([s2] * 4, axis=-1)
        dz_ref[:, :512] = norm_bwd(dq_ref[...], zq_ref[...], qn_ref[...], bd, cq, sq, acc_q).astype(BF16)
        zkv = zkv_ref[...]
        dkk = jnp.concatenate([dk_ref[0], dk_ref[1]], axis=0).T
        dz_ref[:, 512:640] = norm_bwd(dkk, zkv[:, :128], kn_ref[...], bd[:128, :128], c2, s2, acc_k).astype(BF16)
        dz_ref[:, 640:768] = jnp.concatenate([dv_ref[0], dv_ref[1]], axis=0).T.astype(BF16)

        @pl.when(i == n - 1)
        def _():
            dqn_ref[...] = jnp.sum(acc_q[...], axis=0, keepdims=True)
            dkn_ref[...] = jnp.sum(acc_k[...], axis=0, keepdims=True)

    kv_blk = SEG["ka"][2] // 256
    kvs = pl.BlockSpec((ATTN_KV_HEADS, hd, tm), lambda i: (0, 0, i))
    const = lambda shape: pl.BlockSpec(shape, lambda i: (0, 0))
    return pl.pallas_call(
        body, name="attn_post_bwd", grid=(n,),
        in_specs=[pl.BlockSpec((tm, 512), lambda i: (i, 0)), kvs, kvs,
                  pl.BlockSpec((tm, 512), lambda i: (i, 0)), pl.BlockSpec((tm, 256), lambda i: (i, kv_blk)),
                  const((1, 512)), const((1, 128)),
                  pl.BlockSpec((tm, 128), lambda i: (i, 0)), pl.BlockSpec((tm, 128), lambda i: (i, 0)),
                  const((512, 512))],
        out_specs=[pl.BlockSpec((tm, 768), lambda i: (i, 0)), const((1, 512)), const((1, 128))],
        out_shape=[SDS((t, 768), BF16), SDS((1, 512), F32), SDS((1, 128), F32)],
        scratch_shapes=[pltpu.VMEM((8, 512), F32), pltpu.VMEM((8, 128), F32)],
        compiler_params=_params(("arbitrary",)),
    )(dq, dk, dv, z, z, qn, kn, cos, sin, ones_bd)


def _in_bwd(dxo, x, g, w_t, dz_a, dz_m, dqr, dkr, dvr, after=None):
    t, d = x.shape
    tm = min(256, t)
    n = t // tm
    parts = [(0, 0, 768, 0), (1, 0, 512, SEG["ga"][0]), (2, 0, 512, SEG["qr"][0]), (3, 0, 512, SEG["kr"][0]),
             (4, 0, 512, SEG["vr"][0]), (1, 512, 2560, SEG["gr"][0])]

    def body(dx_ref, x_ref, g_ref, w_ref, a_ref, m_ref, q_ref, k_ref, v_ref, o_ref, dg_ref, acc):
        i = pl.program_id(0)

        @pl.when(i == 0)
        def _():
            acc[...] = jnp.zeros_like(acc)

        pieces = [a_ref, m_ref, q_ref, k_ref, v_ref]
        dh = jnp.zeros((tm, d), F32)
        for pi, lo, w, row in parts:
            dh = dh + _dot(pieces[pi][:, lo:lo + w], w_ref[row:row + w, :])
        xv = x_ref[...]
        r = lax.rsqrt(jnp.mean(xv * xv, axis=-1, keepdims=True) + EPS)
        xh = xv * r
        gy = dh * g_ref[...]
        o_ref[...] = dx_ref[...] + r * (gy - xh * jnp.mean(gy * xh, axis=-1, keepdims=True))
        acc[...] += jnp.sum((dh * xh).reshape(tm // 8, 8, d), axis=0)

        @pl.when(i == n - 1)
        def _():
            dg_ref[...] = jnp.sum(acc[...], axis=0, keepdims=True)

    row = lambda w: pl.BlockSpec((tm, w), lambda i: (i, 0))
    const = lambda shape: pl.BlockSpec(shape, lambda i: (0, 0))
    extra = [] if after is None else [after]
    return pl.pallas_call(
        (lambda *refs: body(*refs[:9], *refs[9 + len(extra):])), name="in_bwd", grid=(n,),
        in_specs=[row(d), row(d), const((1, d)), const((D_IN, d)), row(768), row(3072), row(512), row(512),
                  row(512)] + [const(a.shape) for a in extra],
        out_specs=[row(d), const((1, d))],
        out_shape=[SDS((t, d), F32), SDS((1, d), F32)],
        scratch_shapes=[pltpu.VMEM((8, d), F32)],
        compiler_params=_params(("arbitrary",)),
    )(dxo, x, g, w_t, dz_a, dz_m, dqr, dkr, dvr, *extra)


def _dw_in(h_t, dz_a, dz_m, dqr, dkr, dvr):
    d, t = h_t.shape
    tn = 256
    parts = [(0, 0, 0, 3), (1, 0, SEG["ga"][0] // tn, 2), (2, 0, SEG["qr"][0] // tn, 2),
             (3, 0, SEG["kr"][0] // tn, 2), (4, 0, SEG["vr"][0] // tn, 2), (1, 2, SEG["gr"][0] // tn, 10)]
    pieces = [dz_a, dz_m, dqr, dkr, dvr]

    def col_block(pi):
        mine = [(c0, r0, n) for q, c0, r0, n in parts if q == pi]

        def index(j):
            c0, r0, n = mine[0]
            blk = c0 + jnp.clip(j - r0, 0, n - 1)
            for c0, r0, n in mine[1:]:
                blk = jnp.where(j >= r0, c0 + jnp.clip(j - r0, 0, n - 1), blk)
            return 0, blk

        return index

    def body(h_ref, *refs):
        o_ref = refs[-1]
        j = pl.program_id(0)
        for pi, _, r0, n in parts:
            @pl.when(jnp.logical_and(j >= r0, j < r0 + n))
            def _(p_ref=refs[pi]):
                o_ref[...] = _dot(h_ref[...], p_ref[...]).T.astype(BF16)

    return pl.pallas_call(
        body, name="dw_in", grid=(D_IN // tn,),
        in_specs=[pl.BlockSpec((d, t), lambda j: (0, 0))] + [pl.BlockSpec((t, tn), col_block(pi)) for pi in range(5)],
        out_specs=pl.BlockSpec((tn, d), lambda j: (j, 0)),
        out_shape=SDS((D_IN, d), BF16),
        compiler_params=_params(("arbitrary",)),
    )(h_t, *pieces)


def _adamw_math(w, g, m, v):
    mn = ADAM_B1 * m + (1.0 - ADAM_B1) * g
    vn = ADAM_B2 * v + (1.0 - ADAM_B2) * (g * g)
    m_hat = mn / (1.0 - ADAM_B1 ** ADAM_STEP)
    v_hat = vn / (1.0 - ADAM_B2 ** ADAM_STEP)
    return -ADAM_LR * (m_hat / (jnp.sqrt(v_hat) + ADAM_EPS) + ADAM_WD * w), mn, vn


def _sum_adamw(recvs, w, m, v, lane0, tn, layer0=0, prev=None, own=None):
    _, r, c = w.shape
    j0 = lane0 // tn
    n = len(recvs)
    has_own = own is not None

    def body(*refs):
        mine_ref, refs = (refs[0], refs[1:]) if has_own else (None, refs)
        w_ref, m_ref, v_ref = refs[n:n + 3]
        g_ref, d_ref, mo_ref, vo_ref = refs[-4:]

        def run(r_ref):
            def slot(s):
                if has_own:
                    return jnp.where(mine_ref[0] == s, refs[n + 3][...], r_ref[s]).astype(F32)
                return r_ref[s].astype(F32)

            g = slot(0)
            for s in range(1, N_DEV):
                g = g + slot(s)
            g_ref[0] = g
            d_ref[0], mo_ref[0], vo_ref[0] = _adamw_math(w_ref[0], g, m_ref[0], v_ref[0])

        for i in range(n):
            pl.when(pl.program_id(0) == i)(functools.partial(run, refs[i]))

    slots = pl.BlockSpec((N_DEV, r, tn), lambda i, j, *_: (0, 0, j0 + j))
    blk = pl.BlockSpec((1, r, tn), lambda i, j, *_: (layer0 + i, 0, j))
    before = [] if prev is None else list(prev)
    in_specs, args = [slots] * n + [blk] * 3, [*recvs, w, m, v]
    if has_own:
        assert n == 1
        in_specs.append(pl.BlockSpec((r, tn), lambda i, j, mine: (mine[0], j0 + j)))
        args.append(own[0])
    n_pre = len(args) + has_own
    return pl.pallas_call(
        body, name="sum_adamw",
        grid_spec=pltpu.PrefetchScalarGridSpec(
            num_scalar_prefetch=int(has_own), grid=(n, c // tn),
            in_specs=in_specs + [ANY] * len(before), out_specs=[blk] * 4),
        out_shape=[SDS(w.shape, F32)] * 4,
        input_output_aliases={n_pre + k: k for k in range(len(before))},
        compiler_params=_params(("parallel", "parallel")),
    )(*([own[1]] if has_own else []), *args, *before)


def _adamw(w, g, m, v):
    rows, cols = w.shape
    tr = 256 if rows % 256 == 0 else rows

    def body(w_ref, g_ref, m_ref, v_ref, d_ref, mo_ref, vo_ref):
        d_ref[...], mo_ref[...], vo_ref[...] = _adamw_math(w_ref[...], g_ref[...], m_ref[...], v_ref[...])

    blk = pl.BlockSpec((tr, cols), lambda i: (i, 0))
    return pl.pallas_call(
        body, name="adamw", grid=(rows // tr,),
        in_specs=[blk] * 4, out_specs=[blk] * 3, out_shape=[SDS((rows, cols), F32)] * 3,
        compiler_params=_params(("parallel",)),
    )(w, g, m, v)


def _all_gather(shards):
    na = len(shards)
    chips = (4, 2, 6)

    def body(*refs):
        ins, outs = refs[:na], refs[na:2 * na]
        send_sems, recv_sems, local_sems = refs[2 * na:]
        _, mine = _flip(0)

        def rows(a, idx):
            r = shards[a].shape[0]
            return outs[a].at[pl.ds(pl.multiple_of(idx * r, 16), r), :]

        def copy(a, slot, block_idx, to, src=None):
            return pltpu.make_async_remote_copy(
                src_ref=rows(a, block_idx) if src is None else src, dst_ref=rows(a, block_idx),
                send_sem=send_sems.at[a, slot], recv_sem=recv_sems.at[a, slot],
                device_id=to, device_id_type=MESH_ID)

        sibling, sibling_idx = _flip(1)
        local, started = [], []
        for a in range(na):
            cp = pltpu.make_async_copy(ins[a], rows(a, mine), local_sems.at[a])
            cp.start()
            local.append(cp)
            first = [copy(a, 0, mine, sibling, src=ins[a])]
            first += [copy(a, 1 + j, mine, _flip(k)[0], src=ins[a]) for j, k in enumerate(chips)]
            for cp in first:
                cp.start()
            started += first
        for a in range(na):
            for j, k in enumerate(chips):
                _, theirs = _flip(k)
                copy(a, 1 + j, theirs, _flip(0)[0]).wait_recv()
                fwd = copy(a, 4 + j, theirs, sibling)
                fwd.start()
                started.append(fwd)
        for a in range(na):
            copy(a, 0, sibling_idx, _flip(0)[0]).wait_recv()
            for j, k in enumerate(chips):
                _, theirs = _flip(k | 1)
                copy(a, 4 + j, theirs, _flip(0)[0]).wait_recv()
        for cp in started:
            cp.wait_send()
        for cp in local:
            cp.wait()

    return pl.pallas_call(
        body, name="all_gather_weights",
        in_specs=[ANY] * na, out_specs=[ANY] * na,
        out_shape=[SDS((N_DEV * s.shape[0], s.shape[1]), s.dtype) for s in shards],
        scratch_shapes=[pltpu.SemaphoreType.DMA((na, 7)), pltpu.SemaphoreType.DMA((na, 7)),
                        pltpu.SemaphoreType.DMA((na,))],
        compiler_params=pltpu.CompilerParams(has_side_effects=True),
    )(*shards)


def _scatter_blocks_of(g_ref, rows, idx):
    return g_ref.at[pl.ds(pl.multiple_of(idx * rows, 16), rows), :]


def _scatter_start(g):
    rows = g.shape[0] // N_DEV
    land_shape = (N_DEV, rows, g.shape[1])

    def body(g_ref, land_ref, send_sems, recv_sems, g_thru, land_thru, token):
        _, mine = _flip(0)
        for k in range(1, N_DEV):
            peer, theirs = _flip(k)
            pltpu.make_async_remote_copy(
                src_ref=_scatter_blocks_of(g_ref, rows, theirs), dst_ref=land_ref.at[mine],
                send_sem=send_sems.at[k - 1], recv_sem=recv_sems.at[k - 1],
                device_id=peer, device_id_type=MESH_ID).start()
        token[...] = jnp.zeros_like(token)

    hbm, sem = pl.BlockSpec(memory_space=pltpu.HBM), pl.BlockSpec(memory_space=pltpu.SEMAPHORE)
    return pl.pallas_call(
        body, name="scatter_start",
        out_shape=(pltpu.SemaphoreType.DMA((N_DEV - 1,)), pltpu.SemaphoreType.DMA((N_DEV - 1,)),
                   pltpu.HBM(g.shape, g.dtype), pltpu.HBM(land_shape, g.dtype), SDS((8, 128), F32)),
        in_specs=(hbm, hbm), out_specs=(sem, sem, hbm, hbm, pl.BlockSpec(memory_space=pltpu.VMEM)),
        input_output_aliases={0: 2, 1: 3},
        compiler_params=pltpu.CompilerParams(has_side_effects=pltpu.SideEffectType.DATAFLOW_SIDE_EFFECTING),
    )(pltpu.with_memory_space_constraint(g, pltpu.HBM),
      pltpu.with_memory_space_constraint(lax.empty(land_shape, g.dtype), pltpu.HBM))


def _scatter_wait(send_sems, recv_sems, g_thru, land_thru, after):
    rows = g_thru.shape[0] // N_DEV

    def body(g_ref, land_ref, send_sems, recv_sems, *rest):
        me, _ = _flip(0)
        for k in range(1, N_DEV):
            _, theirs = _flip(k)
            copy = pltpu.make_async_remote_copy(
                src_ref=_scatter_blocks_of(g_ref, rows, theirs), dst_ref=land_ref.at[theirs],
                send_sem=send_sems.at[k - 1], recv_sem=recv_sems.at[k - 1],
                device_id=me, device_id_type=MESH_ID)
            copy.wait_send()
            copy.wait_recv()

    hbm, sem = pl.BlockSpec(memory_space=pltpu.HBM), pl.BlockSpec(memory_space=pltpu.SEMAPHORE)
    return pl.pallas_call(
        body, name="scatter_wait",
        out_shape=(pltpu.HBM(g_thru.shape, g_thru.dtype), pltpu.HBM(land_thru.shape, land_thru.dtype)),
        in_specs=(hbm, hbm, sem, sem) + (ANY,) * len(after), out_specs=(hbm, hbm), input_output_aliases={0: 0, 1: 1},
        compiler_params=pltpu.CompilerParams(has_side_effects=pltpu.SideEffectType.DATAFLOW_SIDE_EFFECTING),
    )(g_thru, land_thru, send_sems, recv_sems, *after)


def _all_reduce_small(packed):
    shape = packed.shape

    def body(p_ref, o_ref, slots, send_sems, recv_sems):
        me, mine = _flip(0)
        slots[mine] = p_ref[...]
        sends = []
        for k in range(1, N_DEV):
            peer, _ = _flip(k)
            cp = pltpu.make_async_remote_copy(
                src_ref=p_ref, dst_ref=slots.at[mine], send_sem=send_sems.at[k - 1], recv_sem=recv_sems.at[k - 1],
                device_id=peer, device_id_type=MESH_ID)
            cp.start()
            sends.append(cp)
        for k in range(1, N_DEV):
            _, theirs = _flip(k)
            pltpu.make_async_remote_copy(
                src_ref=p_ref, dst_ref=slots.at[theirs], send_sem=send_sems.at[k - 1],
                recv_sem=recv_sems.at[k - 1], device_id=me, device_id_type=MESH_ID).wait_recv()
        for cp in sends:
            cp.wait_send()
        acc = slots[0]
        for s in range(1, N_DEV):
            acc = acc + slots[s]
        o_ref[...] = acc

    vm = pl.BlockSpec(memory_space=pltpu.VMEM)
    return pl.pallas_call(
        body, name="all_reduce_small", in_specs=[vm], out_specs=vm, out_shape=SDS(shape, F32),
        scratch_shapes=[pltpu.VMEM((N_DEV,) + shape, F32), pltpu.SemaphoreType.DMA((7,)),
                        pltpu.SemaphoreType.DMA((7,))],
        compiler_params=pltpu.CompilerParams(has_side_effects=True),
    )(packed)


def _layer_fwd(x, p, tabs, ex):
    z, h_t, q, qt, k, v, vt, qrot, krot, vb = _in_proj(x, p["norm_g"], p["w_in_t"], p["qn"], p["kn"], tabs["ca"],
                                                       tabs["sa"], tabs["ones"], tabs["cr"], tabs["sr"])
    oa, lse, *gathered = _attn_fwd(q, k, vt, ex)
    orr, on = _ret_fwd(qrot, krot, vb, p["lgf"], p["lgb"], p["gnw"])
    return z, h_t, q, qt, k, v, lse, oa, qrot, krot, vb, orr, on, gathered


def _layer_bwd(dxo, s, p, tabs, ex_attn, scatter_w_in):
    doa, don, dz_m, d_wout, d_wb_t = _merge_bwd(dxo, s["z"], s["oa"], s["on"], s["ya"], s["yb"], p["wb_t"], p["w_out"])
    dq_a, dk_a, dv_a, *recv_attn = _attn_bwd(s["q"], s["qt"], s["k"], s["v"], doa, s["oa"], s["lse"],
                                              ex_attn(d_wb_t, d_wout))
    dz_a, d_qn, d_kn = _attn_post_bwd(dq_a, dk_a, dv_a, s["z"], p["qn"], p["kn"], tabs["ca"], tabs["sa"],
                                      tabs["ones"])
    dqr, dkr, dvr, d_gnw, d_lgf, d_lgb = _ret_bwd(s["qrot"], s["krot"], s["vb"], s["orr"], don, p["gnw"],
                                                  p["lgf"], p["lgb"], tabs["cr"], tabs["sr"])
    buf = _dw_in(s["h_t"], dz_a, dz_m, dqr, dkr, dvr)
    pending, token = None, None
    if scatter_w_in:
        *pending, token = _scatter_start(buf)
    dx, d_norm_g = _in_bwd(dxo, s["x"], p["norm_g"], p["w_in_t"], dz_a, dz_m, dqr, dkr, dvr, token)
    grads = dict(w_in_t=buf, wb_t=d_wb_t, w_out=d_wout, norm_g=d_norm_g, gnw=d_gnw,
                 qn=d_qn.reshape(ATTN_Q_HEADS, ATTN_HEAD_DIM).sum(axis=0),
                 kn=d_kn.reshape(ATTN_KV_HEADS, ATTN_HEAD_DIM).sum(axis=0),
                 lgf=d_lgf[:, 0, 0], lgb=d_lgb[:, 0, 0])
    return dx, grads, recv_attn, pending


def _adamw_nd(w, g, m, v):
    shape = w.shape
    two_d = (1, shape[0]) if w.ndim == 1 else (-1, shape[-1])
    out = _adamw(w.reshape(two_d), g.reshape(two_d), m.reshape(two_d), v.reshape(two_d))
    return tuple(o.reshape(shape) for o in out)


def kernel(x, norm_g, w_in, attn_q_norm, attn_k_norm, ret_decay_fwd, ret_decay_bwd, ret_gn_w, w_branch_attn, w_branch_ret, w_out, final_norm_g, loss_target, m_norm_g, m_w_in, m_attn_q_norm, m_attn_k_norm, m_ret_decay_fwd, m_ret_decay_bwd, m_ret_gn_w, m_w_branch_attn, m_w_branch_ret, m_w_out, m_final_norm_g, v_norm_g, v_w_in, v_attn_q_norm, v_attn_k_norm, v_ret_decay_fwd, v_ret_decay_bwd, v_ret_gn_w, v_w_branch_attn, v_w_branch_ret, v_w_out, v_final_norm_g):
    t, d = x.shape[1], x.shape[2]
    x2, target = x[0], loss_target[0]

    w_in_sh, wb_sh, wout_sh = [], [], []
    for l in range(DEPTH):
        w_in_sh.append(jnp.swapaxes(w_in[l], 0, 1).astype(BF16))
        wb_sh.append(jnp.concatenate([w_branch_attn[l].T, w_branch_ret[l].T], axis=1).astype(BF16))
        wout_sh.append(w_out[l].astype(BF16))

    ca, sa = _rope_tables(t, ATTN_HEAD_DIM)
    cr, sr = _rope_tables(t, RET_HEAD_DIM)
    grp = jnp.arange(ATTN_WIDTH) // ATTN_HEAD_DIM
    tabs = dict(ca=jnp.tile(ca, (1, 2)), sa=jnp.tile(sa, (1, 2)), cr=cr, sr=sr,
                ones=jnp.where(grp[:, None] == grp[None, :], 1.0 / ATTN_HEAD_DIM, 0.0).astype(BF16))
    layers = []
    for l in range(DEPTH):
        layers.append(dict(
            norm_g=norm_g[l][None], qn=jnp.tile(attn_q_norm[l], ATTN_Q_HEADS)[None],
            kn=jnp.tile(attn_k_norm[l], ATTN_KV_HEADS)[None], gnw=ret_gn_w[l][None],
            lgf=jax.nn.log_sigmoid(ret_decay_fwd[l]), lgb=jax.nn.log_sigmoid(ret_decay_bwd[l])))

    layers[0]["w_in_t"], = _all_gather([w_in_sh[0]])
    gathers = [_Exchange("gather", [wb_sh[0], wout_sh[0], w_in_sh[1]]), _Exchange("gather", [wb_sh[1], wout_sh[1]])]
    h = x2
    saved = []
    for l in range(DEPTH):
        p = layers[l]
        z, h_t, q, qt, k, v, lse, oa, qrot, krot, vb, orr, on, got = _layer_fwd(h, p, tabs, gathers[l])
        p["wb_t"], p["w_out"] = got[0], got[1]
        if l == 0:
            layers[1]["w_in_t"] = got[2]
        last = (final_norm_g[None], target) if l == DEPTH - 1 else None
        xn, ya, yb, *loss_head = _merge_fwd(h, z, oa, on, p["wb_t"], p["w_out"], last)
        saved.append(dict(x=h, z=z, h_t=h_t, q=q, qt=qt, k=k, v=v, lse=lse, oa=oa, qrot=qrot, krot=krot, vb=vb,
                          orr=orr, on=on, ya=ya, yb=yb))
        h = xn
    dx, (d_final_g, loss_part) = h, loss_head

    grads = [None] * DEPTH
    dx, grads[1], _, _ = _layer_bwd(dx, saved[1], layers[1], tabs, lambda *a: None, False)
    g1 = grads[1]
    ex_attn = lambda d_wb_t, d_wout: _Exchange("scatter", [g1["w_in_t"], g1["wb_t"], g1["w_out"], d_wb_t, d_wout])
    dx, grads[0], recv_attn, pending = _layer_bwd(dx, saved[0], layers[0], tabs, ex_attn, True)
    recv = [None, recv_attn[3], recv_attn[4], recv_attn[0], recv_attn[1], recv_attn[2]]
    tr = lambda a: jnp.swapaxes(a, 1, 2)
    w_in_t = (tr(w_in), tr(m_w_in), tr(v_w_in))
    sharded = {}
    w_in_l1 = _sum_adamw([recv[3]], *w_in_t, 0, 256, layer0=1)
    sharded[id(w_branch_attn)] = [tr(o) for o in _sum_adamw(
        [recv[1], recv[4]], tr(w_branch_attn), tr(m_w_branch_attn), tr(v_w_branch_attn), 0, 512)]
    sharded[id(w_branch_ret)] = [tr(o) for o in _sum_adamw(
        [recv[1], recv[4]], tr(w_branch_ret), tr(m_w_branch_ret), tr(v_w_branch_ret), 512, 512)]
    sharded[id(w_out)] = _sum_adamw([recv[2], recv[5]], w_out, m_w_out, v_w_out, 0, 256)
    g_wba, g_wbr, g_wout = (sharded[id(w)][0] for w in (w_branch_attn, w_branch_ret, w_out))

    packed = jnp.zeros((8, 1024), F32)
    for l in range(DEPTH):
        gl = grads[l]
        packed = packed.at[l].set(gl["norm_g"][0])
        packed = packed.at[2, 512 * l:512 * (l + 1)].set(gl["gnw"][0])
        packed = packed.at[4, 128 * l:128 * l + 64].set(gl["qn"])
        packed = packed.at[4, 256 + 128 * l:256 + 128 * l + 64].set(gl["kn"])
        packed = packed.at[4, 512 + 128 * l:512 + 128 * l + 4].set(gl["lgf"])
        packed = packed.at[4, 768 + 128 * l:768 + 128 * l + 4].set(gl["lgb"])
    packed = packed.at[3].set(d_final_g[0])
    packed = packed.at[5, 0].set(loss_part[0, 0])
    red = _all_reduce_small(packed)
    loss = red[5, 0]
    g_norm_g = red[0:2]
    g_gnw = red[2].reshape(DEPTH, RET_WIDTH)
    g_final = red[3]
    g_qn = jnp.stack([red[4, 128 * l:128 * l + 64] for l in range(DEPTH)])
    g_kn = jnp.stack([red[4, 256 + 128 * l:256 + 128 * l + 64] for l in range(DEPTH)])
    g_lgf = jnp.stack([red[4, 512 + 128 * l:512 + 128 * l + 4] for l in range(DEPTH)])
    g_lgb = jnp.stack([red[4, 768 + 128 * l:768 + 128 * l + 4] for l in range(DEPTH)])
    g_df = g_lgf * jax.nn.sigmoid(-ret_decay_fwd)
    g_db = g_lgb * jax.nn.sigmoid(-ret_decay_bwd)

    grad_w = [g_norm_g, None, g_qn, g_kn, g_df, g_db, g_gnw, g_wba, g_wbr, g_wout, g_final]
    weights = [norm_g, w_in, attn_q_norm, attn_k_norm, ret_decay_fwd, ret_decay_bwd, ret_gn_w, w_branch_attn,
               w_branch_ret, w_out, final_norm_g]
    ms = [m_norm_g, m_w_in, m_attn_q_norm, m_attn_k_norm, m_ret_decay_fwd, m_ret_decay_bwd, m_ret_gn_w,
          m_w_branch_attn, m_w_branch_ret, m_w_out, m_final_norm_g]
    vs = [v_norm_g, v_w_in, v_attn_q_norm, v_attn_k_norm, v_ret_decay_fwd, v_ret_decay_bwd, v_ret_gn_w,
          v_w_branch_attn, v_w_branch_ret, v_w_out, v_final_norm_g]
    upd = [None if w is w_in else sharded[id(w)][1:] if id(w) in sharded else _adamw_nd(w, g, m, v)
           for w, g, m, v in zip(weights, grad_w, ms, vs)]

    done = [dx, w_in_l1[0], g_wout] + [u[0] for w, u in zip(weights, upd) if u is not None and id(w) not in sharded]
    g_full, recv[0] = _scatter_wait(*pending, done)
    mine = (4 * lax.axis_index("x") + 2 * lax.axis_index("y") + lax.axis_index("c")).astype(jnp.int32)[None]
    w_in_upd = [tr(o) for o in _sum_adamw([recv[0]], *w_in_t, 0, 256, layer0=0, prev=w_in_l1, own=(g_full, mine))]
    grad_w[1], upd[1] = w_in_upd[0], w_in_upd[1:]
    return (loss, dx[None], *grad_w, *[u[0] for u in upd], *[u[1] for u in upd], *[u[2] for u in upd])
```

```python
import functools

import jax
import jax.numpy as jnp
from jax import lax
from jax.experimental import pallas as pl
from jax.experimental.pallas import tpu as pltpu

F32 = jnp.float32
BF16 = jnp.bfloat16
SDS = jax.ShapeDtypeStruct

D_MODEL = 1024
DEPTH = 2
GRID_W = 64
ATTN_Q_HEADS = 8
ATTN_KV_HEADS = 2
ATTN_HEAD_DIM = 64
ATTN_WIDTH = 512
ATTN_KV_WIDTH = 128
RET_HEADS = 4
RET_HEAD_DIM = 128
RET_WIDTH = 512
RET_CHUNK = 128
ATTN_KEY_CHUNK = 512
ATTN_BWD_KEY_CHUNK = 1024
ATTN_BWD_QUERY_TILE = 1024
ATTN_FWD_QUERY_TILE = 512
QK_DOTS_PER_CHUNK = 4
EXP_LAG = 3
ROPE_THETA = 10000.0
EPS = 1e-6
D_IN = 5376
N_DEV = 8

ADAM_LR = 0.001
ADAM_B1 = 0.9
ADAM_B2 = 0.999
ADAM_EPS = 1e-08
ADAM_WD = 0.01
ADAM_STEP = 10

SEG = {
    "qa": (0, 512, 0),
    "ga": (768, 512, 512),
    "qr": (1280, 512, 1024),
    "kr": (1792, 512, 1536),
    "vr": (2304, 512, 2048),
    "gr": (2816, 512, 2560),
    "gm": (3328, 2048, 3072),
    "ka": (512, 128, 5120),
    "va": (640, 128, 5248),
}

VMEM_LIMIT = 60 * 1024 * 1024
NT = (((1,), (1,)), ((), ()))
TN = (((0,), (0,)), ((), ()))
MESH_ID = pl.DeviceIdType.MESH
ANY = pl.BlockSpec(memory_space=pl.ANY)


def _params(sem=None, vmem=VMEM_LIMIT):
    return pltpu.CompilerParams(dimension_semantics=sem, vmem_limit_bytes=vmem)


def _dot(a, b, dims=None):
    if dims is None:
        return jnp.dot(a, b, preferred_element_type=F32)
    return lax.dot_general(a, b, dims, preferred_element_type=F32)


def _sigmoid(x):
    return 1.0 / (1.0 + jnp.exp(-x))


def _swap_halves(x, q):
    n = x.shape[-1]
    axis = x.ndim - 1
    lane = lax.broadcasted_iota(jnp.int32, x.shape, axis)
    first = (lane % (2 * q)) < q
    return jnp.where(first, pltpu.roll(x, n - q, axis), pltpu.roll(x, q, axis))


def _rope(x, cos, sin_signed, q):
    return x * cos + _swap_halves(x, q) * sin_signed


def _rope_bwd(dy, cos, sin_signed, q):
    return dy * cos - _swap_halves(dy, q) * sin_signed


def _group_mean(v, ones_bd):
    hi = v.astype(BF16)
    lo = (v - hi.astype(F32)).astype(BF16)
    return _dot(hi, ones_bd) + _dot(lo, ones_bd)


def _rope_tables(t, head_dim):
    n_rows = t // GRID_W
    d_axis = head_dim // 2
    inv_freq = ROPE_THETA ** (-jnp.arange(0, d_axis, 2, dtype=F32) / d_axis)
    ar = jnp.arange(n_rows, dtype=F32)[:, None] * inv_freq
    ac = jnp.arange(GRID_W, dtype=F32)[:, None] * inv_freq
    by_row = lambda a: jnp.repeat(a, GRID_W, axis=0)
    by_col = lambda a: jnp.tile(a, (n_rows, 1))
    cr, sr, cc, sc = by_row(jnp.cos(ar)), by_row(jnp.sin(ar)), by_col(jnp.cos(ac)), by_col(jnp.sin(ac))
    return jnp.concatenate([cr, cr, cc, cc], axis=-1), jnp.concatenate([-sr, sr, -sc, sc], axis=-1)


def _me():
    return lax.axis_index("x"), lax.axis_index("y"), lax.axis_index("c")


def _flip(k):
    x, y, c = _me()
    px = 1 - x if k & 4 else x
    py = 1 - y if k & 2 else y
    pc = 1 - c if k & 1 else c
    return (px, py, pc), 4 * px + 2 * py + pc


class _Exchange:
    def __init__(self, kind, srcs):
        self.kind, self.srcs, self.n = kind, list(srcs), len(srcs)
        self.rows = [a.shape[0] if kind == "gather" else a.shape[0] // N_DEV for a in srcs]
        if kind == "gather":
            self.out_shape = [SDS((N_DEV * a.shape[0], a.shape[1]), a.dtype) for a in srcs]
        else:
            self.out_shape = [SDS((N_DEV, a.shape[0] // N_DEV, a.shape[1]), a.dtype) for a in srcs]
        self.scratch = [pltpu.SemaphoreType.DMA((self.n, N_DEV - 1)), pltpu.SemaphoreType.DMA((self.n, N_DEV - 1)),
                        pltpu.SemaphoreType.DMA((self.n,))]

    def _block(self, ref, a, idx):
        r = self.rows[a]
        return ref.at[pl.ds(pl.multiple_of(idx * r, 16), r), :]

    def _src(self, ins, a, idx):
        return ins[a] if self.kind == "gather" else self._block(ins[a], a, idx)

    def _dst(self, outs, a, idx):
        return self._block(outs[a], a, idx) if self.kind == "gather" else outs[a].at[idx]

    def _copies(self, ins, outs, sems):
        send_sems, recv_sems, local_sems = sems
        me, mine = _flip(0)
        local, sends, recvs = [], [], []
        for a in range(self.n):
            local.append(pltpu.make_async_copy(self._src(ins, a, mine), self._dst(outs, a, mine), local_sems.at[a]))
            for k in range(1, N_DEV):
                peer, theirs = _flip(k)
                sem = dict(send_sem=send_sems.at[a, k - 1], recv_sem=recv_sems.at[a, k - 1])
                sends.append(pltpu.make_async_remote_copy(
                    src_ref=self._src(ins, a, theirs), dst_ref=self._dst(outs, a, mine),
                    device_id=peer, device_id_type=MESH_ID, **sem))
                recvs.append(pltpu.make_async_remote_copy(
                    src_ref=self._dst(outs, a, theirs), dst_ref=self._dst(outs, a, theirs),
                    device_id=me, device_id_type=MESH_ID, **sem))
        return local, sends, recvs

    def start(self, ins, outs, sems):
        local, sends, _ = self._copies(ins, outs, sems)
        for cp in local + sends:
            cp.start()

    def wait(self, ins, outs, sems):
        local, sends, recvs = self._copies(ins, outs, sems)
        for cp in sends:
            cp.wait_send()
        for cp in recvs:
            cp.wait_recv()
        for cp in local:
            cp.wait()


def _with_exchange(body, n_in, n_out, n_scratch, ex, first, last):
    if ex is None:
        return body

    def wrapped(*refs):
        ins = refs[:n_in]
        ex_ins = refs[n_in:n_in + ex.n]
        outs = refs[n_in + ex.n:n_in + ex.n + n_out]
        ex_outs = refs[n_in + ex.n + n_out:n_in + 2 * ex.n + n_out]
        rest = refs[n_in + 2 * ex.n + n_out:]
        scratch, sems = rest[:n_scratch], rest[n_scratch:]

        @pl.when(first())
        def _():
            ex.start(ex_ins, ex_outs, sems)

        body(*ins, *outs, *scratch)

        @pl.when(last())
        def _():
            ex.wait(ex_ins, ex_outs, sems)

    return wrapped


def _ex_args(ex):
    if ex is None:
        return [], [], [], [], []
    return [ANY] * ex.n, [ANY] * ex.n, list(ex.out_shape), list(ex.scratch), list(ex.srcs)


def _in_proj(x, g, w_t, qn, kn, cos, sin, ones_bd, cos_r, sin_r):
    t, d = x.shape
    tm = min(256, t)
    tk = min(ATTN_KEY_CHUNK, t)
    per_chunk = tk // tm
    hd = ATTN_HEAD_DIM

    def body(x_ref, g_ref, w_ref, qn_ref, kn_ref, c_ref, s_ref, b_ref, cr_ref, sr_ref,
             z_ref, ht_ref, q_out, qt_out, k_out, v_out, vt_out, qr_out, kr_out, vr_out):
        xv = x_ref[...]
        r = lax.rsqrt(jnp.mean(xv * xv, axis=-1, keepdims=True) + EPS)
        h = xv * r * g_ref[...]
        ht_ref[...] = h.T.astype(BF16)
        hb = h.astype(BF16)
        def project(name):
            nat, w, off = SEG[name]
            zs = _dot(hb, w_ref[nat:nat + w, :], NT)
            z_ref[:, off:off + w] = zs
            return zs

        seg = {name: project(name) for name in ("qa", "ka", "va")}
        bd = b_ref[...]
        c2, s2 = c_ref[...], s_ref[...]
        cq = jnp.concatenate([c2] * 4, axis=-1)
        sq = jnp.concatenate([s2] * 4, axis=-1)
        xq, xk, xvv = seg["qa"], seg["ka"], seg["va"]
        yq = xq * lax.rsqrt(_group_mean(xq * xq, bd) + EPS) * qn_ref[...]
        yq = _rope(yq, cq, sq, hd // 4) * (hd ** -0.5)
        yqt = yq.T
        for hh in range(ATTN_Q_HEADS):
            q_out[hh] = yq[:, hh * hd:(hh + 1) * hd].astype(BF16)
            qt_out[hh] = yqt[hh * hd:(hh + 1) * hd, :].astype(BF16)
        yk = xk * lax.rsqrt(_group_mean(xk * xk, bd[:ATTN_KV_WIDTH, :ATTN_KV_WIDTH]) + EPS) * kn_ref[...]
        yk = _rope(yk, c2, s2, hd // 4)
        xvt = xvv.T
        ones = jnp.ones((hd, tm), F32)
        for hh in range(ATTN_KV_HEADS):
            k_out[hh] = yk[:, hh * hd:(hh + 1) * hd].astype(BF16)
            v_out[hh] = xvv[:, hh * hd:(hh + 1) * hd].astype(BF16)
            vt_out[hh, 0] = jnp.concatenate([xvt[hh * hd:(hh + 1) * hd, :], ones], axis=0).astype(BF16)
        rd = RET_HEAD_DIM
        cr = jnp.concatenate([cr_ref[...]] * RET_HEADS, axis=-1)
        sr = jnp.concatenate([sr_ref[...]] * RET_HEADS, axis=-1)
        qr_out[...] = _rope(project("qr"), cr, sr, rd // 4).astype(BF16)
        kr_out[...] = (_rope(project("kr"), cr, sr, rd // 4) * (rd ** -0.5)).astype(BF16)
        vr_out[...] = project("vr").astype(BF16)
        for name in ("ga", "gr", "gm"):
            project(name)

    const = lambda shape: pl.BlockSpec(shape, lambda i: (0,) * len(shape))
    rows = lambda w: pl.BlockSpec((tm, w), lambda i: (i, 0))
    return pl.pallas_call(
        body, name="in_proj", grid=(t // tm,),
        in_specs=[rows(d), const((1, d)), const((D_IN, d)), const((1, 512)), const((1, 128)), rows(128), rows(128),
                  const((512, 512)), rows(128), rows(128)],
        out_specs=[rows(D_IN), pl.BlockSpec((d, tm), lambda i: (0, i)),
                   pl.BlockSpec((ATTN_Q_HEADS, tm, hd), lambda i: (0, i, 0)),
                   pl.BlockSpec((ATTN_Q_HEADS, hd, tm), lambda i: (0, 0, i)),
                   pl.BlockSpec((ATTN_KV_HEADS, tm, hd), lambda i: (0, i, 0)),
                   pl.BlockSpec((ATTN_KV_HEADS, tm, hd), lambda i: (0, i, 0)),
                   pl.BlockSpec((ATTN_KV_HEADS, 1, 2 * hd, tm), lambda i: (0, i // per_chunk, 0, i % per_chunk)),
                   rows(RET_WIDTH), rows(RET_WIDTH), rows(RET_WIDTH)],
        out_shape=[SDS((t, D_IN), F32), SDS((d, t), BF16),
                   SDS((ATTN_Q_HEADS, t, hd), BF16), SDS((ATTN_Q_HEADS, hd, t), BF16),
                   SDS((ATTN_KV_HEADS, t, hd), BF16), SDS((ATTN_KV_HEADS, t, hd), BF16),
                   SDS((ATTN_KV_HEADS, t // tk, 2 * hd, tk), BF16)] + [SDS((t, RET_WIDTH), BF16)] * 3,
        compiler_params=_params(("parallel",)),
    )(x, g, w_t, qn, kn, cos, sin, ones_bd, cos_r, sin_r)


def _attn_fwd(q, k, vt, ex=None):
    t = q.shape[1]
    tq = min(ATTN_FWD_QUERY_TILE, t)
    nk, tk = vt.shape[1], vt.shape[3]
    hd = ATTN_HEAD_DIM
    g = ATTN_Q_HEADS // ATTN_KV_HEADS

    def body(q_ref, k_ref, vt_ref, o_ref, lse_ref, s_scr):
        def pass_a(h, c, m8):
            part = tk // QK_DOTS_PER_CHUNK
            for lo in range(c * tk, (c + 1) * tk, part):
                st = _dot(k_ref[0, lo:lo + part, :], q_ref[h], NT)
                s_scr[h % 2, lo:lo + part, :] = st
                m8 = jnp.maximum(m8, jnp.max(st.reshape(part // 8, 8, tq), axis=0))
            return m8

        def pass_b(h, c, m, acc, after):
            e = jnp.exp(s_scr[h % 2, c * tk:(c + 1) * tk, :] - (m + after * 0.0)).astype(BF16)
            return acc + _dot(vt_ref[0, c], e)

        neg = jnp.full((8, tq), -jnp.inf, F32)
        m8 = neg
        for c in range(nk):
            m8 = pass_a(0, c, m8)
        outs = []
        for h in range(g):
            m = jnp.max(m8, axis=0, keepdims=True)
            acc = jnp.zeros((2 * hd, tq), F32)
            m8 = neg
            done = [m] * EXP_LAG
            for c in range(nk):
                if h + 1 < g:
                    m8 = pass_a(h + 1, c, m8)
                acc = pass_b(h, c, m, acc, done[-EXP_LAG])
                done.append(m8[0:1, :] if h + 1 < g else acc[hd:hd + 1, :])
            l = acc[hd:hd + 1, :]
            outs.append((acc[:hd, :] / l).T)
            lse_ref[h] = m + jnp.log(l)
        o_ref[...] = jnp.concatenate(outs, axis=-1)

    nq = t // tq
    first = lambda: jnp.logical_and(pl.program_id(0) == 0, pl.program_id(1) == 0)
    last = lambda: jnp.logical_and(pl.program_id(0) == ATTN_KV_HEADS - 1, pl.program_id(1) == nq - 1)
    xi, xo, xs, xscr, xargs = _ex_args(ex)
    return pl.pallas_call(
        _with_exchange(body, 3, 2, 1, ex, first, last), name="attn_fwd", grid=(ATTN_KV_HEADS, nq),
        in_specs=[pl.BlockSpec((g, tq, hd), lambda p, i: (p, i, 0)),
                  pl.BlockSpec((1, t, hd), lambda p, i: (p, 0, 0)),
                  pl.BlockSpec((1, nk, 2 * hd, tk), lambda p, i: (p, 0, 0, 0))] + xi,
        out_specs=[pl.BlockSpec((tq, g * hd), lambda p, i: (i, p)),
                   pl.BlockSpec((g, 1, tq), lambda p, i: (p, 0, i))] + xo,
        out_shape=[SDS((t, ATTN_WIDTH), F32), SDS((ATTN_Q_HEADS, 1, t), F32)] + xs,
        scratch_shapes=[pltpu.VMEM((2, t, tq), F32)] + xscr,
        compiler_params=_params(("arbitrary", "arbitrary")),
    )(q, k, vt, *xargs)


class _Dir:
    def __init__(self, lg, strict_future):
        c = RET_CHUNK
        ia = lax.broadcasted_iota(jnp.int32, (c, c), 0).astype(F32)
        ib = lax.broadcasted_iota(jnp.int32, (c, c), 1).astype(F32)
        col = lax.broadcasted_iota(jnp.int32, (c, 1), 0).astype(F32)
        row = lax.broadcasted_iota(jnp.int32, (1, c), 1).astype(F32)
        if strict_future:
            dist = ib - ia
            mask = dist > 0
            self.wq, self.wk, wk_row = c - col, col, row
        else:
            dist = ia - ib
            mask = dist >= 0
            self.wq, self.wk, wk_row = col + 1.0, c - 1.0 - col, c - 1.0 - row
        self.dist = jnp.maximum(dist, 0.0)
        self.d = jnp.where(mask, jnp.exp(self.dist * lg), 0.0)
        self.qd = jnp.exp(self.wq * lg)
        self.kd_col = jnp.exp(self.wk * lg)
        self.kd_row = jnp.exp(wk_row * lg)
        self.cd = jnp.exp(jnp.full((1, 1), float(c), F32) * lg)


def _ret_fwd(qrot, krot, vb, lgf, lgb, gnw):
    t = qrot.shape[0]
    c = RET_CHUNK
    nc = t // c
    hd = RET_HEAD_DIM
    unroll = 4 if nc % 4 == 0 else 1

    def body(lgf_ref, lgb_ref, qo_ref, ko_ref, vo_ref, w_ref, orr_ref, on_ref, kt, uf, ub, sfa, sba):
        h = pl.program_id(0)
        fw = _Dir(lgf_ref[h], False)
        bw = _Dir(lgb_ref[h], True)
        for i in range(nc):
            kt[i] = ko_ref[i * c:(i + 1) * c, :].astype(F32).T.astype(BF16)

        def rows(ci):
            return pl.ds(pl.multiple_of(ci * c, c), c)

        def kv_products(ci, carry):
            vv = vo_ref[rows(ci), :]
            ktf = kt[ci].astype(F32)
            uf[ci] = _dot((ktf * fw.kd_row).astype(BF16), vv)
            ub[ci] = _dot((ktf * bw.kd_row).astype(BF16), vv)
            return carry

        lax.fori_loop(0, nc, kv_products, 0, unroll=unroll)

        def scan(i, carry):
            sf, sb = carry
            j = nc - 1 - i
            sfa[i] = sf.astype(BF16)
            sba[j] = sb.astype(BF16)
            return sf * fw.cd + uf[i], sb * bw.cd + ub[j]

        zero = jnp.zeros((hd, hd), F32)
        lax.fori_loop(0, nc, scan, (zero, zero))
        gw = w_ref[...]

        def outputs(ci, carry):
            sl = rows(ci)
            qq, kk, vv = qo_ref[sl, :], ko_ref[sl, :], vo_ref[sl, :]
            a = _dot(qq, kk, NT)
            o = (_dot((a * fw.d).astype(BF16), vv) + _dot(qq, sfa[ci]) * fw.qd
                 + _dot((a * bw.d).astype(BF16), vv) + _dot(qq, sba[ci]) * bw.qd)
            orr_ref[sl, :] = o
            xc = o - jnp.mean(o, axis=-1, keepdims=True)
            var = jnp.mean(xc * xc, axis=-1, keepdims=True)
            on_ref[sl, :] = xc * lax.rsqrt(var + EPS) * gw
            return carry

        group = 32 if nc % 32 == 0 else 1

        def output_group(i, carry):
            for j in range(group):
                outputs(i * group + j, carry)
            return carry

        lax.fori_loop(0, nc // group, output_group, 0)

    smem = pl.BlockSpec(memory_space=pltpu.SMEM)
    head = pl.BlockSpec((t, 128), lambda h: (0, h))
    return pl.pallas_call(
        body, name="ret_fwd", grid=(RET_HEADS,),
        in_specs=[smem, smem, head, head, head, pl.BlockSpec((1, 128), lambda h: (0, h))],
        out_specs=[head, head],
        out_shape=[SDS((t, RET_WIDTH), F32)] * 2,
        scratch_shapes=[pltpu.VMEM((nc, hd, c), BF16), pltpu.VMEM((nc, hd, hd), F32), pltpu.VMEM((nc, hd, hd), F32),
                        pltpu.VMEM((nc, hd, hd), BF16), pltpu.VMEM((nc, hd, hd), BF16)],
        compiler_params=_params(("parallel",)),
    )(lgf, lgb, qrot, krot, vb, gnw)


def _merge_fwd(x, z, oa, on, wb_t, wout, head=None):
    t, d = x.shape
    tm = min(256, t)
    n = t // tm

    def body(x_ref, ga_ref, gr_ref, gm0_ref, gm1_ref, oa_ref, on_ref, wb_ref, wo_ref, *rest):
        ga, gr = ga_ref[...], gr_ref[...]
        ua = ga * _sigmoid(ga) * oa_ref[...]
        ub = gr * _sigmoid(gr) * on_ref[...]
        ya = _dot(ua.astype(BF16), wb_ref[:, :512], NT)
        yb = _dot(ub.astype(BF16), wb_ref[:, 512:], NT)
        merged = _sigmoid(gm0_ref[...]) * ya + _sigmoid(gm1_ref[...]) * yb
        xn = x_ref[...] + _dot(merged.astype(BF16), wo_ref[...])
        if head is None:
            xn_ref, ya_ref, yb_ref = rest
            xn_ref[...] = xn
        else:
            g_ref, t_ref, dx_ref, ya_ref, yb_ref, dg_ref, loss_ref, acc_g, acc_l = rest
            i = pl.program_id(0)

            @pl.when(i == 0)
            def _():
                acc_g[...] = jnp.zeros_like(acc_g)
                acc_l[...] = jnp.zeros_like(acc_l)

            gv = g_ref[...]
            r = lax.rsqrt(jnp.mean(xn * xn, axis=-1, keepdims=True) + EPS)
            xh = xn * r
            err = xh * gv - t_ref[...]
            dy = err * (1.0 / d)
            gy = dy * gv
            dx_ref[...] = r * (gy - xh * jnp.mean(gy * xh, axis=-1, keepdims=True))
            acc_g[...] += jnp.sum((dy * xh).reshape(tm // 8, 8, d), axis=0)
            acc_l[...] += jnp.sum((err * err).reshape(tm // 8, 8, d), axis=0)

            @pl.when(i == n - 1)
            def _():
                dg_ref[...] = jnp.sum(acc_g[...], axis=0, keepdims=True)
                tot = jnp.sum(jnp.sum(acc_l[...], axis=0, keepdims=True), axis=1, keepdims=True)
                loss_ref[...] = jnp.broadcast_to(tot * (0.5 / d), (1, 128))
        ya_ref[...] = ya.astype(BF16)
        yb_ref[...] = yb.astype(BF16)

    row = lambda w, j: pl.BlockSpec((tm, w), lambda i: (i, j))
    const = lambda shape: pl.BlockSpec(shape, lambda i: (0, 0))
    in_specs = [row(d, 0), row(512, SEG["ga"][2] // 512), row(512, SEG["gr"][2] // 512),
                row(1024, SEG["gm"][2] // 1024), row(1024, SEG["gm"][2] // 1024 + 1),
                row(512, 0), row(512, 0), const((d, 1024)), const((d, d))]
    out_specs = [row(d, 0), row(d, 0), row(d, 0)]
    out_shape = [SDS((t, d), F32), SDS((t, d), BF16), SDS((t, d), BF16)]
    args, scratch = [x, z, z, z, z, oa, on, wb_t, wout], []
    if head is not None:
        in_specs += [const((1, d)), row(d, 0)]
        out_specs += [const((1, d)), const((1, 128))]
        out_shape += [SDS((1, d), F32), SDS((1, 128), F32)]
        args += list(head)
        scratch = [pltpu.VMEM((8, d), F32), pltpu.VMEM((8, d), F32)]
    return pl.pallas_call(
        body, name="merge_fwd", grid=(n,), in_specs=in_specs, out_specs=out_specs, out_shape=out_shape,
        scratch_shapes=scratch,
        compiler_params=_params(("arbitrary",) if head is not None else ("parallel",)),
    )(*args)


def _merge_bwd(dxo, z, oa, on, ya, yb, wb_t, wout):
    t, d = dxo.shape
    tm = min(256, t)
    n = t // tm

    def body(dx_ref, ga_ref, gr_ref, gm0_ref, gm1_ref, oa_ref, on_ref, ya_ref, yb_ref, wb_ref, wo_ref,
             doa_ref, don_ref, dz_ref, dwo_ref, dwb_ref, acc_o, acc_b):
        i = pl.program_id(0)

        @pl.when(i == 0)
        def _():
            acc_o[...] = jnp.zeros_like(acc_o)
            acc_b[...] = jnp.zeros_like(acc_b)

        dxb = dx_ref[...].astype(BF16)
        ya, yb = ya_ref[...].astype(F32), yb_ref[...].astype(F32)
        g0, g1 = _sigmoid(gm0_ref[...]), _sigmoid(gm1_ref[...])
        mb = (g0 * ya + g1 * yb).astype(BF16)
        dm = _dot(dxb, wo_ref[...], NT)
        dya = (dm * g0).astype(BF16)
        dyb = (dm * g1).astype(BF16)
        dz_ref[:, 1024:2048] = (dm * ya * g0 * (1.0 - g0)).astype(BF16)
        dz_ref[:, 2048:3072] = (dm * yb * g1 * (1.0 - g1)).astype(BF16)

        def branch(g_ref, o_ref, dy, w, do_ref, lo):
            gv, ov = g_ref[...], o_ref[...]
            sg = _sigmoid(gv)
            silu = gv * sg
            du = _dot(dy, w)
            do_ref[...] = du * silu
            dz_ref[:, lo:lo + 512] = (du * ov * (sg * (1.0 + gv * (1.0 - sg)))).astype(BF16)
            acc_b[:, lo:lo + 512] += _dot(dy, (silu * ov).astype(BF16), TN)

        branch(ga_ref, oa_ref, dya, wb_ref[:, :512], doa_ref, 0)
        branch(gr_ref, on_ref, dyb, wb_ref[:, 512:], don_ref, 512)
        acc_o[...] += _dot(mb, dxb, TN)

        @pl.when(i == n - 1)
        def _():
            dwo_ref[...] = acc_o[...].astype(BF16)
            dwb_ref[...] = acc_b[...].astype(BF16)

    row = lambda w, j: pl.BlockSpec((tm, w), lambda i: (i, j))
    const = lambda shape: pl.BlockSpec(shape, lambda i: (0, 0))
    return pl.pallas_call(
        body, name="merge_bwd", grid=(n,),
        in_specs=[row(d, 0), row(512, SEG["ga"][2] // 512), row(512, SEG["gr"][2] // 512),
                  row(1024, SEG["gm"][2] // 1024), row(1024, SEG["gm"][2] // 1024 + 1),
                  row(512, 0), row(512, 0), row(d, 0), row(d, 0), const((d, 1024)), const((d, d))],
        out_specs=[row(512, 0), row(512, 0), row(3072, 0), const((d, d)), const((d, 1024))],
        out_shape=[SDS((t, 512), F32), SDS((t, 512), F32), SDS((t, 3072), BF16), SDS((d, d), BF16),
                   SDS((d, 1024), BF16)],
        scratch_shapes=[pltpu.VMEM((d, d), F32), pltpu.VMEM((d, 1024), F32)],
        compiler_params=_params(("arbitrary",)),
    )(dxo, z, z, z, z, oa, on, ya, yb, wb_t, wout)


def _ret_bwd(qrot, krot, vb, orr, don, gnw, lgf, lgb, cos, sin):
    t = qrot.shape[0]
    c = RET_CHUNK
    nc = t // c
    hd = RET_HEAD_DIM
    unroll = 4 if nc % 4 == 0 else 1

    def body(lgf_ref, lgb_ref, q_ref, k_ref, v_ref, o_ref, dn_ref, w_ref, c_ref, s_ref,
             dq_ref, dk_ref, dv_ref, dw_ref, dlf_ref, dlb_ref, qt, kt, dob, uf, ub, wf, wb, sfa, sba, gfa, gba):
        h = pl.program_id(0)
        fw = _Dir(lgf_ref[h], False)
        bw = _Dir(lgb_ref[h], True)
        fw.dt, bw.dt = fw.d.T, bw.d.T

        o = o_ref[...]
        xc = o - jnp.mean(o, axis=-1, keepdims=True)
        r = lax.rsqrt(jnp.mean(xc * xc, axis=-1, keepdims=True) + EPS)
        xh = xc * r
        dn = dn_ref[...]
        gy = dn * w_ref[...]
        d_o = r * (gy - jnp.mean(gy, axis=-1, keepdims=True) - xh * jnp.mean(gy * xh, axis=-1, keepdims=True))
        dw_ref[...] = jnp.sum(dn * xh, axis=0, keepdims=True)
        dob[...] = d_o.astype(BF16)
        for i in range(nc):
            qt[i] = q_ref[i * c:(i + 1) * c, :].astype(F32).T.astype(BF16)
            kt[i] = k_ref[i * c:(i + 1) * c, :].astype(F32).T.astype(BF16)

        def rows(ci):
            return pl.ds(pl.multiple_of(ci * c, c), c)

        def products(ci, carry):
            sl = rows(ci)
            vv, do32 = v_ref[sl, :], dob[sl, :].astype(F32)
            ktf = kt[ci].astype(F32)
            uf[ci] = _dot((ktf * fw.kd_row).astype(BF16), vv)
            ub[ci] = _dot((ktf * bw.kd_row).astype(BF16), vv)
            wf[ci] = _dot(qt[ci], (do32 * fw.qd).astype(BF16))
            wb[ci] = _dot(qt[ci], (do32 * bw.qd).astype(BF16))
            return carry

        lax.fori_loop(0, nc, products, 0, unroll=unroll)

        def scan(i, carry):
            sf, sb, gf, gb = carry
            j = nc - 1 - i
            sfa[i] = sf.astype(BF16)
            sba[j] = sb.astype(BF16)
            gfa[j] = gf.astype(BF16)
            gba[i] = gb.astype(BF16)
            return sf * fw.cd + uf[i], sb * bw.cd + ub[j], gf * fw.cd + wf[j], gb * bw.cd + wb[i]

        zero = jnp.zeros((hd, hd), F32)
        lax.fori_loop(0, nc, scan, (zero, zero, zero, zero))

        def one_dir(p, s_all, g_all, ci, qq, kk, vv, do, a, bm):
            sb, gb = s_all[ci], g_all[ci]
            doq = (do.astype(F32) * p.qd).astype(BF16)
            dqc = _dot(doq, sb, NT)
            kkd = (kk.astype(F32) * p.kd_col).astype(BF16)
            dk2 = _dot(vv, gb, NT) * p.kd_col
            terms = (p.dist * p.d * a * bm + p.wq * qq.astype(F32) * dqc + p.wk * kk.astype(F32) * dk2
                     + (float(c) * p.cd) * gb.astype(F32) * sb.astype(F32))
            return dqc, dk2, _dot(kkd, gb), terms

        d_both, dt_both = fw.d + bw.d, fw.dt + bw.dt

        def chunk(ci, carry):
            af, ab = carry
            sl = rows(ci)
            qq, kk, vv, do = q_ref[sl, :], k_ref[sl, :], v_ref[sl, :], dob[sl, :]
            a, bm = _dot(qq, kk, NT), _dot(do, vv, NT)
            at, bt = _dot(kk, qq, NT), _dot(vv, do, NT)
            dqf, dkf, dvf, tf = one_dir(fw, sfa, gfa, ci, qq, kk, vv, do, a, bm)
            dqb, dkb, dvb, tb = one_dir(bw, sba, gba, ci, qq, kk, vv, do, a, bm)
            cc, ss = c_ref[sl, :], s_ref[sl, :]
            dq = _dot((bm * d_both).astype(BF16), kk) + dqf + dqb
            dk = _dot((bt * dt_both).astype(BF16), qq) + dkf + dkb
            dq_ref[sl, :] = _rope_bwd(dq, cc, ss, hd // 4).astype(BF16)
            dk_ref[sl, :] = (_rope_bwd(dk, cc, ss, hd // 4) * (hd ** -0.5)).astype(BF16)
            dv_ref[sl, :] = (_dot((at * dt_both).astype(BF16), do) + dvf + dvb).astype(BF16)
            return af + tf, ab + tb

        pair = 8 if nc % 8 == 0 else 1

        def chunks(i, carry):
            for j in range(pair):
                carry = chunk(i * pair + j, carry)
            return carry

        af, ab = lax.fori_loop(0, nc // pair, chunks, (zero, zero))
        tot = lambda m: jnp.sum(jnp.sum(m, axis=0, keepdims=True), axis=1, keepdims=True)
        dlf_ref[...] = jnp.broadcast_to(tot(af).reshape(1, 1, 1), (1, 8, 128))
        dlb_ref[...] = jnp.broadcast_to(tot(ab).reshape(1, 1, 1), (1, 8, 128))

    smem = pl.BlockSpec(memory_space=pltpu.SMEM)
    head = pl.BlockSpec((t, 128), lambda h: (0, h))
    vec = pl.BlockSpec((1, 128), lambda h: (0, h))
    scal = pl.BlockSpec((1, 8, 128), lambda h: (h, 0, 0))
    table = pl.BlockSpec((t, 128), lambda h: (0, 0))
    mats = lambda dt: pltpu.VMEM((nc, hd, hd), dt)
    return pl.pallas_call(
        body, name="ret_bwd", grid=(RET_HEADS,),
        in_specs=[smem, smem, head, head, head, head, head, vec, table, table],
        out_specs=[head, head, head, vec, scal, scal],
        out_shape=[SDS((t, RET_WIDTH), BF16)] * 3 + [SDS((1, RET_WIDTH), F32), SDS((RET_HEADS, 8, 128), F32),
                                                    SDS((RET_HEADS, 8, 128), F32)],
        scratch_shapes=[pltpu.VMEM((nc, hd, c), BF16), pltpu.VMEM((nc, hd, c), BF16), pltpu.VMEM((t, hd), BF16),
                        mats(F32), mats(F32), mats(F32), mats(F32), mats(BF16), mats(BF16), mats(BF16), mats(BF16)],
        compiler_params=_params(("parallel",)),
    )(lgf, lgb, qrot, krot, vb, orr, don, gnw, cos, sin)


def _attn_bwd(q, qt, k, v, doa, oa, lse, ex=None):
    t = q.shape[1]
    tq = min(ATTN_BWD_QUERY_TILE, t)
    nq = t // tq
    tk = min(ATTN_BWD_KEY_CHUNK, t)
    nk = t // tk
    hd = ATTN_HEAD_DIM
    scale = hd ** -0.5

    def body(q_ref, qt_ref, k_ref, v_ref, do_ref, o_ref, lse_ref, dq_ref, dkt_ref, dvt_ref):
        p, i = pl.program_id(0), pl.program_id(1)

        @pl.when(jnp.logical_and(p % 2 == 0, i == 0))
        def _():
            dkt_ref[...] = jnp.zeros_like(dkt_ref)
            dvt_ref[...] = jnp.zeros_like(dvt_ref)

        dov, ov = do_ref[...], o_ref[...]
        dovt = dov.T
        lanes = lambda col: jnp.concatenate([col] * (tk // 128), axis=1)
        outs = []
        for j in range(2):
            qq, qqt = q_ref[j], qt_ref[j]
            do32 = dov[:, j * hd:(j + 1) * hd]
            do, dot_ = do32.astype(BF16), dovt[j * hd:(j + 1) * hd, :].astype(BF16)
            dd = lanes(jnp.broadcast_to(jnp.sum(do32 * ov[:, j * hd:(j + 1) * hd], axis=1, keepdims=True), (tq, 128)))
            lse_j = lanes(jnp.broadcast_to(lse_ref[j], (128, tq)).T)
            dq = jnp.zeros((tq, hd), F32)
            for c in range(nk):
                sl = slice(c * tk, (c + 1) * tk)
                kc, vc = k_ref[0, sl, :], v_ref[0, sl, :]
                pr = jnp.exp(_dot(qq, kc, NT) - lse_j)
                ds = (pr * (_dot(do, vc, NT) - dd)).astype(BF16)
                dvt_ref[0, :, sl] += _dot(dot_, pr.astype(BF16))
                dkt_ref[0, :, sl] += _dot(qqt, ds)
                dq = dq + _dot(ds, kc)
            outs.append(dq * scale)
        dq_ref[...] = jnp.concatenate(outs, axis=-1)

    kv = pl.BlockSpec((1, t, hd), lambda p, i: (p // 2, 0, 0))
    kvt = pl.BlockSpec((1, hd, t), lambda p, i: (p // 2, 0, 0))
    pair = pl.BlockSpec((tq, 128), lambda p, i: (i, p))
    first = lambda: jnp.logical_and(pl.program_id(0) == 0, pl.program_id(1) == 0)
    last = lambda: jnp.logical_and(pl.program_id(0) == 3, pl.program_id(1) == nq - 1)
    xi, xo, xs, xscr, xargs = _ex_args(ex)
    return pl.pallas_call(
        _with_exchange(body, 7, 3, 0, ex, first, last), name="attn_bwd", grid=(4, nq),
        in_specs=[pl.BlockSpec((2, tq, hd), lambda p, i: (p, i, 0)), pl.BlockSpec((2, hd, tq), lambda p, i: (p, 0, i)),
                  kv, kv, pair, pair, pl.BlockSpec((2, 1, tq), lambda p, i: (p, 0, i))] + xi,
        out_specs=[pair, kvt, kvt] + xo,
        out_shape=[SDS((t, ATTN_WIDTH), F32), SDS((ATTN_KV_HEADS, hd, t), F32),
                   SDS((ATTN_KV_HEADS, hd, t), F32)] + xs,
        scratch_shapes=xscr,
        compiler_params=_params(("arbitrary", "arbitrary")),
    )(q, qt, k, v, doa, oa, lse, *xargs)


def _attn_post_bwd(dq, dk, dv, z, qn, kn, cos, sin, ones_bd):
    t = z.shape[0]
    tm = min(512, t)
    n = t // tm
    hd = ATTN_HEAD_DIM

    def body(dq_ref, dk_ref, dv_ref, zq_ref, zkv_ref, qn_ref, kn_ref, c_ref, s_ref, b_ref,
             dz_ref, dqn_ref, dkn_ref, acc_q, acc_k):
        i = pl.program_id(0)

        @pl.when(i == 0)
        def _():
            acc_q[...] = jnp.zeros_like(acc_q)
            acc_k[...] = jnp.zeros_like(acc_k)

        bd = b_ref[...]
        c2, s2 = c_ref[...], s_ref[...]

        def norm_bwd(dy, x, w, ones, cos_t, sin_t, acc):
            dyr = _rope_bwd(dy, cos_t, sin_t, hd // 4)
            r = lax.rsqrt(_group_mean(x * x, ones) + EPS)
            xh = x * r
            gy = dyr * w
            acc[...] += jnp.sum((dyr * xh).reshape(tm // 8, 8, x.shape[-1]), axis=0)
            return r * (gy - xh * _group_mean(gy * xh, ones))

        cq = jnp.concatenate([c2] * 4, axis=-1)
        sq = jnp.concatenate([s2] * 4, axis=-1)
        dz_ref[:, :512] = norm_bwd(dq_ref[...], zq_ref[...], qn_ref[...], bd, cq, sq, acc_q).astype(BF16)
        zkv = zkv_ref[...]
        dkk = jnp.concatenate([dk_ref[0], dk_ref[1]], axis=0).T
        dz_ref[:, 512:640] = norm_bwd(dkk, zkv[:, :128], kn_ref[...], bd[:128, :128], c2, s2, acc_k).astype(BF16)
        dz_ref[:, 640:768] = jnp.concatenate([dv_ref[0], dv_ref[1]], axis=0).T.astype(BF16)

        @pl.when(i == n - 1)
        def _():
            dqn_ref[...] = jnp.sum(acc_q[...], axis=0, keepdims=True)
            dkn_ref[...] = jnp.sum(acc_k[...], axis=0, keepdims=True)

    kv_blk = SEG["ka"][2] // 256
    kvs = pl.BlockSpec((ATTN_KV_HEADS, hd, tm), lambda i: (0, 0, i))
    const = lambda shape: pl.BlockSpec(shape, lambda i: (0, 0))
    return pl.pallas_call(
        body, name="attn_post_bwd", grid=(n,),
        in_specs=[pl.BlockSpec((tm, 512), lambda i: (i, 0)), kvs, kvs,
                  pl.BlockSpec((tm, 512), lambda i: (i, 0)), pl.BlockSpec((tm, 256), lambda i: (i, kv_blk)),
                  const((1, 512)), const((1, 128)),
                  pl.BlockSpec((tm, 128), lambda i: (i, 0)), pl.BlockSpec((tm, 128), lambda i: (i, 0)),
                  const((512, 512))],
        out_specs=[pl.BlockSpec((tm, 768), lambda i: (i, 0)), const((1, 512)), const((1, 128))],
        out_shape=[SDS((t, 768), BF16), SDS((1, 512), F32), SDS((1, 128), F32)],
        scratch_shapes=[pltpu.VMEM((8, 512), F32), pltpu.VMEM((8, 128), F32)],
        compiler_params=_params(("arbitrary",)),
    )(dq, dk, dv, z, z, qn, kn, cos, sin, ones_bd)


def _in_bwd(dxo, x, g, w_t, dz_a, dz_m, dqr, dkr, dvr, after=None):
    t, d = x.shape
    tm = min(256, t)
    n = t // tm
    parts = [(0, 0, 768, 0), (1, 0, 512, SEG["ga"][0]), (2, 0, 512, SEG["qr"][0]), (3, 0, 512, SEG["kr"][0]),
             (4, 0, 512, SEG["vr"][0]), (1, 512, 2560, SEG["gr"][0])]

    def body(dx_ref, x_ref, g_ref, w_ref, a_ref, m_ref, q_ref, k_ref, v_ref, o_ref, dg_ref, acc):
        i = pl.program_id(0)

        @pl.when(i == 0)
        def _():
            acc[...] = jnp.zeros_like(acc)

        pieces = [a_ref, m_ref, q_ref, k_ref, v_ref]
        dh = jnp.zeros((tm, d), F32)
        for pi, lo, w, row in parts:
            dh = dh + _dot(pieces[pi][:, lo:lo + w], w_ref[row:row + w, :])
        xv = x_ref[...]
        r = lax.rsqrt(jnp.mean(xv * xv, axis=-1, keepdims=True) + EPS)
        xh = xv * r
        gy = dh * g_ref[...]
        o_ref[...] = dx_ref[...] + r * (gy - xh * jnp.mean(gy * xh, axis=-1, keepdims=True))
        acc[...] += jnp.sum((dh * xh).reshape(tm // 8, 8, d), axis=0)

        @pl.when(i == n - 1)
        def _():
            dg_ref[...] = jnp.sum(acc[...], axis=0, keepdims=True)

    row = lambda w: pl.BlockSpec((tm, w), lambda i: (i, 0))
    const = lambda shape: pl.BlockSpec(shape, lambda i: (0, 0))
    extra = [] if after is None else [after]
    return pl.pallas_call(
        (lambda *refs: body(*refs[:9], *refs[9 + len(extra):])), name="in_bwd", grid=(n,),
        in_specs=[row(d), row(d), const((1, d)), const((D_IN, d)), row(768), row(3072), row(512), row(512),
                  row(512)] + [const(a.shape) for a in extra],
        out_specs=[row(d), const((1, d))],
        out_shape=[SDS((t, d), F32), SDS((1, d), F32)],
        scratch_shapes=[pltpu.VMEM((8, d), F32)],
        compiler_params=_params(("arbitrary",)),
    )(dxo, x, g, w_t, dz_a, dz_m, dqr, dkr, dvr, *extra)


def _dw_in(h_t, dz_a, dz_m, dqr, dkr, dvr):
    d, t = h_t.shape
    tn = 256
    parts = [(0, 0, 0, 3), (1, 0, SEG["ga"][0] // tn, 2), (2, 0, SEG["qr"][0] // tn, 2),
             (3, 0, SEG["kr"][0] // tn, 2), (4, 0, SEG["vr"][0] // tn, 2), (1, 2, SEG["gr"][0] // tn, 10)]
    pieces = [dz_a, dz_m, dqr, dkr, dvr]

    def col_block(pi):
        mine = [(c0, r0, n) for q, c0, r0, n in parts if q == pi]

        def index(j):
            c0, r0, n = mine[0]
            blk = c0 + jnp.clip(j - r0, 0, n - 1)
            for c0, r0, n in mine[1:]:
                blk = jnp.where(j >= r0, c0 + jnp.clip(j - r0, 0, n - 1), blk)
            return 0, blk

        return index

    def body(h_ref, *refs):
        o_ref = refs[-1]
        j = pl.program_id(0)
        for pi, _, r0, n in parts:
            @pl.when(jnp.logical_and(j >= r0, j < r0 + n))
            def _(p_ref=refs[pi]):
                o_ref[...] = _dot(h_ref[...], p_ref[...]).T.astype(BF16)

    return pl.pallas_call(
        body, name="dw_in", grid=(D_IN // tn,),
        in_specs=[pl.BlockSpec((d, t), lambda j: (0, 0))] + [pl.BlockSpec((t, tn), col_block(pi)) for pi in range(5)],
        out_specs=pl.BlockSpec((tn, d), lambda j: (j, 0)),
        out_shape=SDS((D_IN, d), BF16),
        compiler_params=_params(("arbitrary",)),
    )(h_t, *pieces)


def _adamw_math(w, g, m, v):
    mn = ADAM_B1 * m + (1.0 - ADAM_B1) * g
    vn = ADAM_B2 * v + (1.0 - ADAM_B2) * (g * g)
    m_hat = mn / (1.0 - ADAM_B1 ** ADAM_STEP)
    v_hat = vn / (1.0 - ADAM_B2 ** ADAM_STEP)
    return -ADAM_LR * (m_hat / (jnp.sqrt(v_hat) + ADAM_EPS) + ADAM_WD * w), mn, vn


def _sum_adamw(recvs, w, m, v, lane0, tn, layer0=0, prev=None, own=None):
    _, r, c = w.shape
    j0 = lane0 // tn
    n = len(recvs)
    has_own = own is not None

    def body(*refs):
        mine_ref, refs = (refs[0], refs[1:]) if has_own else (None, refs)
        w_ref, m_ref, v_ref = refs[n:n + 3]
        g_ref, d_ref, mo_ref, vo_ref = refs[-4:]

        def run(r_ref):
            def slot(s):
                if has_own:
                    return jnp.where(mine_ref[0] == s, refs[n + 3][...], r_ref[s]).astype(F32)
                return r_ref[s].astype(F32)

            g = slot(0)
            for s in range(1, N_DEV):
                g = g + slot(s)
            g_ref[0] = g
            d_ref[0], mo_ref[0], vo_ref[0] = _adamw_math(w_ref[0], g, m_ref[0], v_ref[0])

        for i in range(n):
            pl.when(pl.program_id(0) == i)(functools.partial(run, refs[i]))

    slots = pl.BlockSpec((N_DEV, r, tn), lambda i, j, *_: (0, 0, j0 + j))
    blk = pl.BlockSpec((1, r, tn), lambda i, j, *_: (layer0 + i, 0, j))
    before = [] if prev is None else list(prev)
    in_specs, args = [slots] * n + [blk] * 3, [*recvs, w, m, v]
    if has_own:
        assert n == 1
        in_specs.append(pl.BlockSpec((r, tn), lambda i, j, mine: (mine[0], j0 + j)))
        args.append(own[0])
    n_pre = len(args) + has_own
    return pl.pallas_call(
        body, name="sum_adamw",
        grid_spec=pltpu.PrefetchScalarGridSpec(
            num_scalar_prefetch=int(has_own), grid=(n, c // tn),
            in_specs=in_specs + [ANY] * len(before), out_specs=[blk] * 4),
        out_shape=[SDS(w.shape, F32)] * 4,
        input_output_aliases={n_pre + k: k for k in range(len(before))},
        compiler_params=_params(("parallel", "parallel")),
    )(*([own[1]] if has_own else []), *args, *before)


def _adamw(w, g, m, v):
    rows, cols = w.shape
    tr = 256 if rows % 256 == 0 else rows

    def body(w_ref, g_ref, m_ref, v_ref, d_ref, mo_ref, vo_ref):
        d_ref[...], mo_ref[...], vo_ref[...] = _adamw_math(w_ref[...], g_ref[...], m_ref[...], v_ref[...])

    blk = pl.BlockSpec((tr, cols), lambda i: (i, 0))
    return pl.pallas_call(
        body, name="adamw", grid=(rows // tr,),
        in_specs=[blk] * 4, out_specs=[blk] * 3, out_shape=[SDS((rows, cols), F32)] * 3,
        compiler_params=_params(("parallel",)),
    )(w, g, m, v)


def _all_gather(shards):
    na = len(shards)
    chips = (4, 2, 6)

    def body(*refs):
        ins, outs = refs[:na], refs[na:2 * na]
        send_sems, recv_sems, local_sems = refs[2 * na:]
        _, mine = _flip(0)

        def rows(a, idx):
            r = shards[a].shape[0]
            return outs[a].at[pl.ds(pl.multiple_of(idx * r, 16), r), :]

        def copy(a, slot, block_idx, to, src=None):
            return pltpu.make_async_remote_copy(
                src_ref=rows(a, block_idx) if src is None else src, dst_ref=rows(a, block_idx),
                send_sem=send_sems.at[a, slot], recv_sem=recv_sems.at[a, slot],
                device_id=to, device_id_type=MESH_ID)

        sibling, sibling_idx = _flip(1)
        local, started = [], []
        for a in range(na):
            cp = pltpu.make_async_copy(ins[a], rows(a, mine), local_sems.at[a])
            cp.start()
            local.append(cp)
            first = [copy(a, 0, mine, sibling, src=ins[a])]
            first += [copy(a, 1 + j, mine, _flip(k)[0], src=ins[a]) for j, k in enumerate(chips)]
            for cp in first:
                cp.start()
            started += first
        for a in range(na):
            for j, k in enumerate(chips):
                _, theirs = _flip(k)
                copy(a, 1 + j, theirs, _flip(0)[0]).wait_recv()
                fwd = copy(a, 4 + j, theirs, sibling)
                fwd.start()
                started.append(fwd)
        for a in range(na):
            copy(a, 0, sibling_idx, _flip(0)[0]).wait_recv()
            for j, k in enumerate(chips):
                _, theirs = _flip(k | 1)
                copy(a, 4 + j, theirs, _flip(0)[0]).wait_recv()
        for cp in started:
            cp.wait_send()
        for cp in local:
            cp.wait()

    return pl.pallas_call(
        body, name="all_gather_weights",
        in_specs=[ANY] * na, out_specs=[ANY] * na,
        out_shape=[SDS((N_DEV * s.shape[0], s.shape[1]), s.dtype) for s in shards],
        scratch_shapes=[pltpu.SemaphoreType.DMA((na, 7)), pltpu.SemaphoreType.DMA((na, 7)),
                        pltpu.SemaphoreType.DMA((na,))],
        compiler_params=pltpu.CompilerParams(has_side_effects=True),
    )(*shards)


def _scatter_blocks_of(g_ref, rows, idx):
    return g_ref.at[pl.ds(pl.multiple_of(idx * rows, 16), rows), :]


def _scatter_start(g):
    rows = g.shape[0] // N_DEV
    land_shape = (N_DEV, rows, g.shape[1])

    def body(g_ref, land_ref, send_sems, recv_sems, g_thru, land_thru, token):
        _, mine = _flip(0)
        for k in range(1, N_DEV):
            peer, theirs = _flip(k)
            pltpu.make_async_remote_copy(
                src_ref=_scatter_blocks_of(g_ref, rows, theirs), dst_ref=land_ref.at[mine],
                send_sem=send_sems.at[k - 1], recv_sem=recv_sems.at[k - 1],
                device_id=peer, device_id_type=MESH_ID).start()
        token[...] = jnp.zeros_like(token)

    hbm, sem = pl.BlockSpec(memory_space=pltpu.HBM), pl.BlockSpec(memory_space=pltpu.SEMAPHORE)
    return pl.pallas_call(
        body, name="scatter_start",
        out_shape=(pltpu.SemaphoreType.DMA((N_DEV - 1,)), pltpu.SemaphoreType.DMA((N_DEV - 1,)),
                   pltpu.HBM(g.shape, g.dtype), pltpu.HBM(land_shape, g.dtype), SDS((8, 128), F32)),
        in_specs=(hbm, hbm), out_specs=(sem, sem, hbm, hbm, pl.BlockSpec(memory_space=pltpu.VMEM)),
        input_output_aliases={0: 2, 1: 3},
        compiler_params=pltpu.CompilerParams(has_side_effects=pltpu.SideEffectType.DATAFLOW_SIDE_EFFECTING),
    )(pltpu.with_memory_space_constraint(g, pltpu.HBM),
      pltpu.with_memory_space_constraint(lax.empty(land_shape, g.dtype), pltpu.HBM))


def _scatter_wait(send_sems, recv_sems, g_thru, land_thru, after):
    rows = g_thru.shape[0] // N_DEV

    def body(g_ref, land_ref, send_sems, recv_sems, *rest):
        me, _ = _flip(0)
        for k in range(1, N_DEV):
            _, theirs = _flip(k)
            copy = pltpu.make_async_remote_copy(
                src_ref=_scatter_blocks_of(g_ref, rows, theirs), dst_ref=land_ref.at[theirs],
                send_sem=send_sems.at[k - 1], recv_sem=recv_sems.at[k - 1],
                device_id=me, device_id_type=MESH_ID)
            copy.wait_send()
            copy.wait_recv()

    hbm, sem = pl.BlockSpec(memory_space=pltpu.HBM), pl.BlockSpec(memory_space=pltpu.SEMAPHORE)
    return pl.pallas_call(
        body, name="scatter_wait",
        out_shape=(pltpu.HBM(g_thru.shape, g_thru.dtype), pltpu.HBM(land_thru.shape, land_thru.dtype)),
        in_specs=(hbm, hbm, sem, sem) + (ANY,) * len(after), out_specs=(hbm, hbm), input_output_aliases={0: 0, 1: 1},
        compiler_params=pltpu.CompilerParams(has_side_effects=pltpu.SideEffectType.DATAFLOW_SIDE_EFFECTING),
    )(g_thru, land_thru, send_sems, recv_sems, *after)


def _all_reduce_small(packed):
    shape = packed.shape

    def body(p_ref, o_ref, slots, send_sems, recv_sems):
        me, mine = _flip(0)
        slots[mine] = p_ref[...]
        sends = []
        for k in range(1, N_DEV):
            peer, _ = _flip(k)
            cp = pltpu.make_async_remote_copy(
                src_ref=p_ref, dst_ref=slots.at[mine], send_sem=send_sems.at[k - 1], recv_sem=recv_sems.at[k - 1],
                device_id=peer, device_id_type=MESH_ID)
            cp.start()
            sends.append(cp)
        for k in range(1, N_DEV):
            _, theirs = _flip(k)
            pltpu.make_async_remote_copy(
                src_ref=p_ref, dst_ref=slots.at[theirs], send_sem=send_sems.at[k - 1],
                recv_sem=recv_sems.at[k - 1], device_id=me, device_id_type=MESH_ID).wait_recv()
        for cp in sends:
            cp.wait_send()
        acc = slots[0]
        for s in range(1, N_DEV):
            acc = acc + slots[s]
        o_ref[...] = acc

    vm = pl.BlockSpec(memory_space=pltpu.VMEM)
    return pl.pallas_call(
        body, name="all_reduce_small", in_specs=[vm], out_specs=vm, out_shape=SDS(shape, F32),
        scratch_shapes=[pltpu.VMEM((N_DEV,) + shape, F32), pltpu.SemaphoreType.DMA((7,)),
                        pltpu.SemaphoreType.DMA((7,))],
        compiler_params=pltpu.CompilerParams(has_side_effects=True),
    )(packed)


def _layer_fwd(x, p, tabs, ex):
    z, h_t, q, qt, k, v, vt, qrot, krot, vb = _in_proj(x, p["norm_g"], p["w_in_t"], p["qn"], p["kn"], tabs["ca"],
                                                       tabs["sa"], tabs["ones"], tabs["cr"], tabs["sr"])
    oa, lse, *gathered = _attn_fwd(q, k, vt, ex)
    orr, on = _ret_fwd(qrot, krot, vb, p["lgf"], p["lgb"], p["gnw"])
    return z, h_t, q, qt, k, v, lse, oa, qrot, krot, vb, orr, on, gathered


def _layer_bwd(dxo, s, p, tabs, ex_attn, scatter_w_in):
    doa, don, dz_m, d_wout, d_wb_t = _merge_bwd(dxo, s["z"], s["oa"], s["on"], s["ya"], s["yb"], p["wb_t"], p["w_out"])
    dq_a, dk_a, dv_a, *recv_attn = _attn_bwd(s["q"], s["qt"], s["k"], s["v"], doa, s["oa"], s["lse"],
                                              ex_attn(d_wb_t, d_wout))
    dz_a, d_qn, d_kn = _attn_post_bwd(dq_a, dk_a, dv_a, s["z"], p["qn"], p["kn"], tabs["ca"], tabs["sa"],
                                      tabs["ones"])
    dqr, dkr, dvr, d_gnw, d_lgf, d_lgb = _ret_bwd(s["qrot"], s["krot"], s["vb"], s["orr"], don, p["gnw"],
                                                  p["lgf"], p["lgb"], tabs["cr"], tabs["sr"])
    buf = _dw_in(s["h_t"], dz_a, dz_m, dqr, dkr, dvr)
    pending, token = None, None
    if scatter_w_in:
        *pending, token = _scatter_start(buf)
    dx, d_norm_g = _in_bwd(dxo, s["x"], p["norm_g"], p["w_in_t"], dz_a, dz_m, dqr, dkr, dvr, token)
    grads = dict(w_in_t=buf, wb_t=d_wb_t, w_out=d_wout, norm_g=d_norm_g, gnw=d_gnw,
                 qn=d_qn.reshape(ATTN_Q_HEADS, ATTN_HEAD_DIM).sum(axis=0),
                 kn=d_kn.reshape(ATTN_KV_HEADS, ATTN_HEAD_DIM).sum(axis=0),
                 lgf=d_lgf[:, 0, 0], lgb=d_lgb[:, 0, 0])
    return dx, grads, recv_attn, pending


def _adamw_nd(w, g, m, v):
    shape = w.shape
    two_d = (1, shape[0]) if w.ndim == 1 else (-1, shape[-1])
    out = _adamw(w.reshape(two_d), g.reshape(two_d), m.reshape(two_d), v.reshape(two_d))
    return tuple(o.reshape(shape) for o in out)


def kernel(x, norm_g, w_in, attn_q_norm, attn_k_norm, ret_decay_fwd, ret_decay_bwd, ret_gn_w, w_branch_attn, w_branch_ret, w_out, final_norm_g, loss_target, m_norm_g, m_w_in, m_attn_q_norm, m_attn_k_norm, m_ret_decay_fwd, m_ret_decay_bwd, m_ret_gn_w, m_w_branch_attn, m_w_branch_ret, m_w_out, m_final_norm_g, v_norm_g, v_w_in, v_attn_q_norm, v_attn_k_norm, v_ret_decay_fwd, v_ret_decay_bwd, v_ret_gn_w, v_w_branch_attn, v_w_branch_ret, v_w_out, v_final_norm_g):
    t, d = x.shape[1], x.shape[2]
    x2, target = x[0], loss_target[0]

    w_in_sh, wb_sh, wout_sh = [], [], []
    for l in range(DEPTH):
        w_in_sh.append(jnp.swapaxes(w_in[l], 0, 1).astype(BF16))
        wb_sh.append(jnp.concatenate([w_branch_attn[l].T, w_branch_ret[l].T], axis=1).astype(BF16))
        wout_sh.append(w_out[l].astype(BF16))

    ca, sa = _rope_tables(t, ATTN_HEAD_DIM)
    cr, sr = _rope_tables(t, RET_HEAD_DIM)
    grp = jnp.arange(ATTN_WIDTH) // ATTN_HEAD_DIM
    tabs = dict(ca=jnp.tile(ca, (1, 2)), sa=jnp.tile(sa, (1, 2)), cr=cr, sr=sr,
                ones=jnp.where(grp[:, None] == grp[None, :], 1.0 / ATTN_HEAD_DIM, 0.0).astype(BF16))
    layers = []
    for l in range(DEPTH):
        layers.append(dict(
            norm_g=norm_g[l][None], qn=jnp.tile(attn_q_norm[l], ATTN_Q_HEADS)[None],
            kn=jnp.tile(attn_k_norm[l], ATTN_KV_HEADS)[None], gnw=ret_gn_w[l][None],
            lgf=jax.nn.log_sigmoid(ret_decay_fwd[l]), lgb=jax.nn.log_sigmoid(ret_decay_bwd[l])))

    layers[0]["w_in_t"], = _all_gather([w_in_sh[0]])
    gathers = [_Exchange("gather", [wb_sh[0], wout_sh[0], w_in_sh[1]]), _Exchange("gather", [wb_sh[1], wout_sh[1]])]
    h = x2
    saved = []
    for l in range(DEPTH):
        p = layers[l]
        z, h_t, q, qt, k, v, lse, oa, qrot, krot, vb, orr, on, got = _layer_fwd(h, p, tabs, gathers[l])
        p["wb_t"], p["w_out"] = got[0], got[1]
        if l == 0:
            layers[1]["w_in_t"] = got[2]
        last = (final_norm_g[None], target) if l == DEPTH - 1 else None
        xn, ya, yb, *loss_head = _merge_fwd(h, z, oa, on, p["wb_t"], p["w_out"], last)
        saved.append(dict(x=h, z=z, h_t=h_t, q=q, qt=qt, k=k, v=v, lse=lse, oa=oa, qrot=qrot, krot=krot, vb=vb,
                          orr=orr, on=on, ya=ya, yb=yb))
        h = xn
    dx, (d_final_g, loss_part) = h, loss_head

    grads = [None] * DEPTH
    dx, grads[1], _, _ = _layer_bwd(dx, saved[1], layers[1], tabs, lambda *a: None, False)
    g1 = grads[1]
    ex_attn = lambda d_wb_t, d_wout: _Exchange("scatter", [g1["w_in_t"], g1["wb_t"], g1["w_out"], d_wb_t, d_wout])
    dx, grads[0], recv_attn, pending = _layer_bwd(dx, saved[0], layers[0], tabs, ex_attn, True)
    recv = [None, recv_attn[3], recv_attn[4], recv_attn[0], recv_attn[1], recv_attn[2]]
    tr = lambda a: jnp.swapaxes(a, 1, 2)
    w_in_t = (tr(w_in), tr(m_w_in), tr(v_w_in))
    sharded = {}
    w_in_l1 = _sum_adamw([recv[3]], *w_in_t, 0, 256, layer0=1)
    sharded[id(w_branch_attn)] = [tr(o) for o in _sum_adamw(
        [recv[1], recv[4]], tr(w_branch_attn), tr(m_w_branch_attn), tr(v_w_branch_attn), 0, 512)]
    sharded[id(w_branch_ret)] = [tr(o) for o in _sum_adamw(
        [recv[1], recv[4]], tr(w_branch_ret), tr(m_w_branch_ret), tr(v_w_branch_ret), 512, 512)]
    sharded[id(w_out)] = _sum_adamw([recv[2], recv[5]], w_out, m_w_out, v_w_out, 0, 256)
    g_wba, g_wbr, g_wout = (sharded[id(w)][0] for w in (w_branch_attn, w_branch_ret, w_out))

    packed = jnp.zeros((8, 1024), F32)
    for l in range(DEPTH):
        gl = grads[l]
        packed = packed.at[l].set(gl["norm_g"][0])
        packed = packed.at[2, 512 * l:512 * (l + 1)].set(gl["gnw"][0])
        packed = packed.at[4, 128 * l:128 * l + 64].set(gl["qn"])
        packed = packed.at[4, 256 + 128 * l:256 + 128 * l + 64].set(gl["kn"])
        packed = packed.at[4, 512 + 128 * l:512 + 128 * l + 4].set(gl["lgf"])
        packed = packed.at[4, 768 + 128 * l:768 + 128 * l + 4].set(gl["lgb"])
    packed = packed.at[3].set(d_final_g[0])
    packed = packed.at[5, 0].set(loss_part[0, 0])
    red = _all_reduce_small(packed)
    loss = red[5, 0]
    g_norm_g = red[0:2]
    g_gnw = red[2].reshape(DEPTH, RET_WIDTH)
    g_final = red[3]
    g_qn = jnp.stack([red[4, 128 * l:128 * l + 64] for l in range(DEPTH)])
    g_kn = jnp.stack([red[4, 256 + 128 * l:256 + 128 * l + 64] for l in range(DEPTH)])
    g_lgf = jnp.stack([red[4, 512 + 128 * l:512 + 128 * l + 4] for l in range(DEPTH)])
    g_lgb = jnp.stack([red[4, 768 + 128 * l:768 + 128 * l + 4] for l in range(DEPTH)])
    g_df = g_lgf * jax.nn.sigmoid(-ret_decay_fwd)
    g_db = g_lgb * jax.nn.sigmoid(-ret_decay_bwd)

    grad_w = [g_norm_g, None, g_qn, g_kn, g_df, g_db, g_gnw, g_wba, g_wbr, g_wout, g_final]
    weights = [norm_g, w_in, attn_q_norm, attn_k_norm, ret_decay_fwd, ret_decay_bwd, ret_gn_w, w_branch_attn,
               w_branch_ret, w_out, final_norm_g]
    ms = [m_norm_g, m_w_in, m_attn_q_norm, m_attn_k_norm, m_ret_decay_fwd, m_ret_decay_bwd, m_ret_gn_w,
          m_w_branch_attn, m_w_branch_ret, m_w_out, m_final_norm_g]
    vs = [v_norm_g, v_w_in, v_attn_q_norm, v_attn_k_norm, v_ret_decay_fwd, v_ret_decay_bwd, v_ret_gn_w,
          v_w_branch_attn, v_w_branch_ret, v_w_out, v_final_norm_g]
    upd = [None if w is w_in else sharded[id(w)][1:] if id(w) in sharded else _adamw_nd(w, g, m, v)
           for w, g, m, v in zip(weights, grad_w, ms, vs)]

    done = [dx, w_in_l1[0], g_wout] + [u[0] for w, u in zip(weights, upd) if u is not None and id(w) not in sharded]
    g_full, recv[0] = _scatter_wait(*pending, done)
    mine = (4 * lax.axis_index("x") + 2 * lax.axis_index("y") + lax.axis_index("c")).astype(jnp.int32)[None]
    w_in_upd = [tr(o) for o in _sum_adamw([recv[0]], *w_in_t, 0, 256, layer0=0, prev=w_in_l1, own=(g_full, mine))]
    grad_w[1], upd[1] = w_in_upd[0], w_in_upd[1:]
    return (loss, dx[None], *grad_w, *[u[0] for u in upd], *[u[1] for u in upd], *[u[2] for u in upd])
```

```python
import functools

import jax
import jax.numpy as jnp
from jax import lax
from jax.experimental import pallas as pl
from jax.experimental.pallas import tpu as pltpu

F32 = jnp.float32
BF16 = jnp.bfloat16
SDS = jax.ShapeDtypeStruct

D_MODEL = 1024
DEPTH = 2
GRID_W = 64
ATTN_Q_HEADS = 8
ATTN_KV_HEADS = 2
ATTN_HEAD_DIM = 64
ATTN_WIDTH = 512
ATTN_KV_WIDTH = 128
RET_HEADS = 4
RET_HEAD_DIM = 128
RET_WIDTH = 512
RET_CHUNK = 128
ATTN_KEY_CHUNK = 512
ATTN_BWD_KEY_CHUNK = 1024
ATTN_BWD_QUERY_TILE = 1024
ATTN_FWD_QUERY_TILE = 512
ATTN_FWD_KV_PER_STEP = 2
QK_DOTS_PER_CHUNK = 4
EXP_LAG = 3
ROPE_THETA = 10000.0
EPS = 1e-6
D_IN = 5376
N_DEV = 8

ADAM_LR = 0.001
ADAM_B1 = 0.9
ADAM_B2 = 0.999
ADAM_EPS = 1e-08
ADAM_WD = 0.01
ADAM_STEP = 10

SEG = {
    "qa": (0, 512, 0),
    "ga": (768, 512, 512),
    "qr": (1280, 512, 1024),
    "kr": (1792, 512, 1536),
    "vr": (2304, 512, 2048),
    "gr": (2816, 512, 2560),
    "gm": (3328, 2048, 3072),
    "ka": (512, 128, 5120),
    "va": (640, 128, 5248),
}

VMEM_LIMIT = 60 * 1024 * 1024
NT = (((1,), (1,)), ((), ()))
TN = (((0,), (0,)), ((), ()))
MESH_ID = pl.DeviceIdType.MESH
ANY = pl.BlockSpec(memory_space=pl.ANY)


def _params(sem=None, vmem=VMEM_LIMIT):
    return pltpu.CompilerParams(dimension_semantics=sem, vmem_limit_bytes=vmem)


def _dot(a, b, dims=None):
    if dims is None:
        return jnp.dot(a, b, preferred_element_type=F32)
    return lax.dot_general(a, b, dims, preferred_element_type=F32)


def _sigmoid(x):
    return 1.0 / (1.0 + jnp.exp(-x))


def _swap_halves(x, q):
    n = x.shape[-1]
    axis = x.ndim - 1
    lane = lax.broadcasted_iota(jnp.int32, x.shape, axis)
    first = (lane % (2 * q)) < q
    return jnp.where(first, pltpu.roll(x, n - q, axis), pltpu.roll(x, q, axis))


def _rope(x, cos, sin_signed, q):
    return x * cos + _swap_halves(x, q) * sin_signed


def _rope_bwd(dy, cos, sin_signed, q):
    return dy * cos - _swap_halves(dy, q) * sin_signed


def _group_mean(v, ones_bd):
    hi = v.astype(BF16)
    lo = (v - hi.astype(F32)).astype(BF16)
    return _dot(hi, ones_bd) + _dot(lo, ones_bd)


def _rope_tables(t, head_dim):
    n_rows = t // GRID_W
    d_axis = head_dim // 2
    inv_freq = ROPE_THETA ** (-jnp.arange(0, d_axis, 2, dtype=F32) / d_axis)
    ar = jnp.arange(n_rows, dtype=F32)[:, None] * inv_freq
    ac = jnp.arange(GRID_W, dtype=F32)[:, None] * inv_freq
    by_row = lambda a: jnp.repeat(a, GRID_W, axis=0)
    by_col = lambda a: jnp.tile(a, (n_rows, 1))
    cr, sr, cc, sc = by_row(jnp.cos(ar)), by_row(jnp.sin(ar)), by_col(jnp.cos(ac)), by_col(jnp.sin(ac))
    return jnp.concatenate([cr, cr, cc, cc], axis=-1), jnp.concatenate([-sr, sr, -sc, sc], axis=-1)


def _me():
    return lax.axis_index("x"), lax.axis_index("y"), lax.axis_index("c")


def _flip(k):
    x, y, c = _me()
    px = 1 - x if k & 4 else x
    py = 1 - y if k & 2 else y
    pc = 1 - c if k & 1 else c
    return (px, py, pc), 4 * px + 2 * py + pc


class _Exchange:
    def __init__(self, kind, srcs):
        self.kind, self.srcs, self.n = kind, list(srcs), len(srcs)
        self.rows = [a.shape[0] if kind == "gather" else a.shape[0] // N_DEV for a in srcs]
        if kind == "gather":
            self.out_shape = [SDS((N_DEV * a.shape[0], a.shape[1]), a.dtype) for a in srcs]
        else:
            self.out_shape = [SDS((N_DEV, a.shape[0] // N_DEV, a.shape[1]), a.dtype) for a in srcs]
        self.scratch = [pltpu.SemaphoreType.DMA((self.n, N_DEV - 1)), pltpu.SemaphoreType.DMA((self.n, N_DEV - 1)),
                        pltpu.SemaphoreType.DMA((self.n,))]

    def _block(self, ref, a, idx):
        r = self.rows[a]
        return ref.at[pl.ds(pl.multiple_of(idx * r, 16), r), :]

    def _src(self, ins, a, idx):
        return ins[a] if self.kind == "gather" else self._block(ins[a], a, idx)

    def _dst(self, outs, a, idx):
        return self._block(outs[a], a, idx) if self.kind == "gather" else outs[a].at[idx]

    def _copies(self, ins, outs, sems):
        send_sems, recv_sems, local_sems = sems
        me, mine = _flip(0)
        local, sends, recvs = [], [], []
        for a in range(self.n):
            local.append(pltpu.make_async_copy(self._src(ins, a, mine), self._dst(outs, a, mine), local_sems.at[a]))
            for k in range(1, N_DEV):
                peer, theirs = _flip(k)
                sem = dict(send_sem=send_sems.at[a, k - 1], recv_sem=recv_sems.at[a, k - 1])
                sends.append(pltpu.make_async_remote_copy(
                    src_ref=self._src(ins, a, theirs), dst_ref=self._dst(outs, a, mine),
                    device_id=peer, device_id_type=MESH_ID, **sem))
                recvs.append(pltpu.make_async_remote_copy(
                    src_ref=self._dst(outs, a, theirs), dst_ref=self._dst(outs, a, theirs),
                    device_id=me, device_id_type=MESH_ID, **sem))
        return local, sends, recvs

    def start(self, ins, outs, sems):
        local, sends, _ = self._copies(ins, outs, sems)
        for cp in local + sends:
            cp.start()

    def wait(self, ins, outs, sems):
        local, sends, recvs = self._copies(ins, outs, sems)
        for cp in sends:
            cp.wait_send()
        for cp in recvs:
            cp.wait_recv()
        for cp in local:
            cp.wait()


def _with_exchange(body, n_in, n_out, n_scratch, ex, first, last):
    if ex is None:
        return body

    def wrapped(*refs):
        ins = refs[:n_in]
        ex_ins = refs[n_in:n_in + ex.n]
        outs = refs[n_in + ex.n:n_in + ex.n + n_out]
        ex_outs = refs[n_in + ex.n + n_out:n_in + 2 * ex.n + n_out]
        rest = refs[n_in + 2 * ex.n + n_out:]
        scratch, sems = rest[:n_scratch], rest[n_scratch:]

        @pl.when(first())
        def _():
            ex.start(ex_ins, ex_outs, sems)

        body(*ins, *outs, *scratch)

        @pl.when(last())
        def _():
            ex.wait(ex_ins, ex_outs, sems)

    return wrapped


def _ex_args(ex):
    if ex is None:
        return [], [], [], [], []
    return [ANY] * ex.n, [ANY] * ex.n, list(ex.out_shape), list(ex.scratch), list(ex.srcs)


def _in_proj(x, g, w_t, qn, kn, cos, sin, ones_bd, cos_r, sin_r):
    t, d = x.shape
    tm = min(256, t)
    tk = min(ATTN_KEY_CHUNK, t)
    per_chunk = tk // tm
    hd = ATTN_HEAD_DIM

    def body(x_ref, g_ref, w_ref, qn_ref, kn_ref, c_ref, s_ref, b_ref, cr_ref, sr_ref,
             z_ref, ht_ref, q_out, qt_out, k_out, v_out, vt_out, qr_out, kr_out, vr_out):
        xv = x_ref[...]
        r = lax.rsqrt(jnp.mean(xv * xv, axis=-1, keepdims=True) + EPS)
        h = xv * r * g_ref[...]
        ht_ref[...] = h.T.astype(BF16)
        hb = h.astype(BF16)
        def project(name):
            nat, w, off = SEG[name]
            zs = _dot(hb, w_ref[nat:nat + w, :], NT)
            z_ref[:, off:off + w] = zs
            return zs

        seg = {name: project(name) for name in ("qa", "ka", "va")}
        bd = b_ref[...]
        c2, s2 = c_ref[...], s_ref[...]
        cq = jnp.concatenate([c2] * 4, axis=-1)
        sq = jnp.concatenate([s2] * 4, axis=-1)
        xq, xk, xvv = seg["qa"], seg["ka"], seg["va"]
        yq = xq * lax.rsqrt(_group_mean(xq * xq, bd) + EPS) * qn_ref[...]
        yq = _rope(yq, cq, sq, hd // 4) * (hd ** -0.5)
        yqt = yq.T
        for hh in range(ATTN_Q_HEADS):
            q_out[hh] = yq[:, hh * hd:(hh + 1) * hd].astype(BF16)
            qt_out[hh] = yqt[hh * hd:(hh + 1) * hd, :].astype(BF16)
        yk = xk * lax.rsqrt(_group_mean(xk * xk, bd[:ATTN_KV_WIDTH, :ATTN_KV_WIDTH]) + EPS) * kn_ref[...]
        yk = _rope(yk, c2, s2, hd // 4)
        xvt = xvv.T
        ones = jnp.ones((hd, tm), F32)
        for hh in range(ATTN_KV_HEADS):
            k_out[hh] = yk[:, hh * hd:(hh + 1) * hd].astype(BF16)
            v_out[hh] = xvv[:, hh * hd:(hh + 1) * hd].astype(BF16)
            vt_out[hh, 0] = jnp.concatenate([xvt[hh * hd:(hh + 1) * hd, :], ones], axis=0).astype(BF16)
        rd = RET_HEAD_DIM
        cr = jnp.concatenate([cr_ref[...]] * RET_HEADS, axis=-1)
        sr = jnp.concatenate([sr_ref[...]] * RET_HEADS, axis=-1)
        qr_out[...] = _rope(project("qr"), cr, sr, rd // 4).astype(BF16)
        kr_out[...] = (_rope(project("kr"), cr, sr, rd // 4) * (rd ** -0.5)).astype(BF16)
        vr_out[...] = project("vr").astype(BF16)
        for name in ("ga", "gr", "gm"):
            project(name)

    const = lambda shape: pl.BlockSpec(shape, lambda i: (0,) * len(shape))
    rows = lambda w: pl.BlockSpec((tm, w), lambda i: (i, 0))
    return pl.pallas_call(
        body, name="in_proj", grid=(t // tm,),
        in_specs=[rows(d), const((1, d)), const((D_IN, d)), const((1, 512)), const((1, 128)), rows(128), rows(128),
                  const((512, 512)), rows(128), rows(128)],
        out_specs=[rows(D_IN), pl.BlockSpec((d, tm), lambda i: (0, i)),
                   pl.BlockSpec((ATTN_Q_HEADS, tm, hd), lambda i: (0, i, 0)),
                   pl.BlockSpec((ATTN_Q_HEADS, hd, tm), lambda i: (0, 0, i)),
                   pl.BlockSpec((ATTN_KV_HEADS, tm, hd), lambda i: (0, i, 0)),
                   pl.BlockSpec((ATTN_KV_HEADS, tm, hd), lambda i: (0, i, 0)),
                   pl.BlockSpec((ATTN_KV_HEADS, 1, 2 * hd, tm), lambda i: (0, i // per_chunk, 0, i % per_chunk)),
                   rows(RET_WIDTH), rows(RET_WIDTH), rows(RET_WIDTH)],
        out_shape=[SDS((t, D_IN), F32), SDS((d, t), BF16),
                   SDS((ATTN_Q_HEADS, t, hd), BF16), SDS((ATTN_Q_HEADS, hd, t), BF16),
                   SDS((ATTN_KV_HEADS, t, hd), BF16), SDS((ATTN_KV_HEADS, t, hd), BF16),
                   SDS((ATTN_KV_HEADS, t // tk, 2 * hd, tk), BF16)] + [SDS((t, RET_WIDTH), BF16)] * 3,
        compiler_params=_params(("parallel",)),
    )(x, g, w_t, qn, kn, cos, sin, ones_bd, cos_r, sin_r)


def _attn_fwd(q, k, vt, ex=None):
    t = q.shape[1]
    tq = min(ATTN_FWD_QUERY_TILE, t)
    nk, tk = vt.shape[1], vt.shape[3]
    hd = ATTN_HEAD_DIM
    g = ATTN_Q_HEADS // ATTN_KV_HEADS
    kvs = ATTN_FWD_KV_PER_STEP
    heads = g * kvs

    def body(q_ref, k_ref, vt_ref, o_ref, lse_ref, s_scr):
        def pass_a(h, c, m8):
            part = tk // QK_DOTS_PER_CHUNK
            for lo in range(c * tk, (c + 1) * tk, part):
                st = _dot(k_ref[h // g, lo:lo + part, :], q_ref[h], NT)
                s_scr[h % 2, lo:lo + part, :] = st
                m8 = jnp.maximum(m8, jnp.max(st.reshape(part // 8, 8, tq), axis=0))
            return m8

        def pass_b(h, c, m, acc, after):
            e = jnp.exp(s_scr[h % 2, c * tk:(c + 1) * tk, :] - (m + after * 0.0)).astype(BF16)
            return acc + _dot(vt_ref[h // g, c], e)

        neg = jnp.full((8, tq), -jnp.inf, F32)
        m8 = neg
        for c in range(nk):
            m8 = pass_a(0, c, m8)
        outs = []
        for h in range(heads):
            m = jnp.max(m8, axis=0, keepdims=True)
            acc = jnp.zeros((2 * hd, tq), F32)
            m8 = neg
            done = [m] * EXP_LAG
            for c in range(nk):
                if h + 1 < heads:
                    m8 = pass_a(h + 1, c, m8)
                acc = pass_b(h, c, m, acc, done[-EXP_LAG])
                done.append(m8[0:1, :] if h + 1 < heads else acc[hd:hd + 1, :])
            l = acc[hd:hd + 1, :]
            outs.append((acc[:hd, :] / l).T)
            lse_ref[h] = m + jnp.log(l)
        o_ref[...] = jnp.concatenate(outs, axis=-1)

    nq = t // tq
    first = lambda: jnp.logical_and(pl.program_id(0) == 0, pl.program_id(1) == 0)
    last = lambda: jnp.logical_and(pl.program_id(0) == ATTN_KV_HEADS // kvs - 1, pl.program_id(1) == nq - 1)
    xi, xo, xs, xscr, xargs = _ex_args(ex)
    return pl.pallas_call(
        _with_exchange(body, 3, 2, 1, ex, first, last), name="attn_fwd", grid=(ATTN_KV_HEADS // kvs, nq),
        in_specs=[pl.BlockSpec((heads, tq, hd), lambda p, i: (p, i, 0)),
                  pl.BlockSpec((kvs, t, hd), lambda p, i: (p, 0, 0)),
                  pl.BlockSpec((kvs, nk, 2 * hd, tk), lambda p, i: (p, 0, 0, 0))] + xi,
        out_specs=[pl.BlockSpec((tq, heads * hd), lambda p, i: (i, p)),
                   pl.BlockSpec((heads, 1, tq), lambda p, i: (p, 0, i))] + xo,
        out_shape=[SDS((t, ATTN_WIDTH), F32), SDS((ATTN_Q_HEADS, 1, t), F32)] + xs,
        scratch_shapes=[pltpu.VMEM((2, t, tq), F32)] + xscr,
        compiler_params=_params(("arbitrary", "arbitrary")),
    )(q, k, vt, *xargs)


class _Dir:
    def __init__(self, lg, strict_future):
        c = RET_CHUNK
        ia = lax.broadcasted_iota(jnp.int32, (c, c), 0).astype(F32)
        ib = lax.broadcasted_iota(jnp.int32, (c, c), 1).astype(F32)
        col = lax.broadcasted_iota(jnp.int32, (c, 1), 0).astype(F32)
        row = lax.broadcasted_iota(jnp.int32, (1, c), 1).astype(F32)
        if strict_future:
            dist = ib - ia
            mask = dist > 0
            self.wq, self.wk, wk_row = c - col, col, row
        else:
            dist = ia - ib
            mask = dist >= 0
            self.wq, self.wk, wk_row = col + 1.0, c - 1.0 - col, c - 1.0 - row
        self.dist = jnp.maximum(dist, 0.0)
        self.d = jnp.where(mask, jnp.exp(self.dist * lg), 0.0)
        self.qd = jnp.exp(self.wq * lg)
        self.kd_col = jnp.exp(self.wk * lg)
        self.kd_row = jnp.exp(wk_row * lg)
        self.cd = jnp.exp(jnp.full((1, 1), float(c), F32) * lg)


def _ret_fwd(qrot, krot, vb, lgf, lgb, gnw):
    t = qrot.shape[0]
    c = RET_CHUNK
    nc = t // c
    hd = RET_HEAD_DIM
    unroll = 4 if nc % 4 == 0 else 1

    def body(lgf_ref, lgb_ref, qo_ref, ko_ref, vo_ref, w_ref, orr_ref, on_ref, kt, uf, ub, sfa, sba):
        h = pl.program_id(0)
        fw = _Dir(lgf_ref[h], False)
        bw = _Dir(lgb_ref[h], True)
        for i in range(nc):
            kt[i] = ko_ref[i * c:(i + 1) * c, :].astype(F32).T.astype(BF16)

        def rows(ci):
            return pl.ds(pl.multiple_of(ci * c, c), c)

        def kv_products(ci, carry):
            vv = vo_ref[rows(ci), :]
            ktf = kt[ci].astype(F32)
            uf[ci] = _dot((ktf * fw.kd_row).astype(BF16), vv)
            ub[ci] = _dot((ktf * bw.kd_row).astype(BF16), vv)
            return carry

        lax.fori_loop(0, nc, kv_products, 0, unroll=unroll)

        def scan(i, carry):
            sf, sb = carry
            j = nc - 1 - i
            sfa[i] = sf.astype(BF16)
            sba[j] = sb.astype(BF16)
            return sf * fw.cd + uf[i], sb * bw.cd + ub[j]

        zero = jnp.zeros((hd, hd), F32)
        lax.fori_loop(0, nc, scan, (zero, zero))
        gw = w_ref[...]

        def outputs(ci, carry):
            sl = rows(ci)
            qq, kk, vv = qo_ref[sl, :], ko_ref[sl, :], vo_ref[sl, :]
            a = _dot(qq, kk, NT)
            o = (_dot((a * fw.d).astype(BF16), vv) + _dot(qq, sfa[ci]) * fw.qd
                 + _dot((a * bw.d).astype(BF16), vv) + _dot(qq, sba[ci]) * bw.qd)
            orr_ref[sl, :] = o
            xc = o - jnp.mean(o, axis=-1, keepdims=True)
            var = jnp.mean(xc * xc, axis=-1, keepdims=True)
            on_ref[sl, :] = xc * lax.rsqrt(var + EPS) * gw
            return carry

        group = 32 if nc % 32 == 0 else 1

        def output_group(i, carry):
            for j in range(group):
                outputs(i * group + j, carry)
            return carry

        lax.fori_loop(0, nc // group, output_group, 0)

    smem = pl.BlockSpec(memory_space=pltpu.SMEM)
    head = pl.BlockSpec((t, 128), lambda h: (0, h))
    return pl.pallas_call(
        body, name="ret_fwd", grid=(RET_HEADS,),
        in_specs=[smem, smem, head, head, head, pl.BlockSpec((1, 128), lambda h: (0, h))],
        out_specs=[head, head],
        out_shape=[SDS((t, RET_WIDTH), F32)] * 2,
        scratch_shapes=[pltpu.VMEM((nc, hd, c), BF16), pltpu.VMEM((nc, hd, hd), F32), pltpu.VMEM((nc, hd, hd), F32),
                        pltpu.VMEM((nc, hd, hd), BF16), pltpu.VMEM((nc, hd, hd), BF16)],
        compiler_params=_params(("parallel",)),
    )(lgf, lgb, qrot, krot, vb, gnw)


def _merge_fwd(x, z, oa, on, wb_t, wout, head=None):
    t, d = x.shape
    tm = min(256, t)
    n = t // tm

    def body(x_ref, ga_ref, gr_ref, gm0_ref, gm1_ref, oa_ref, on_ref, wb_ref, wo_ref, *rest):
        ga, gr = ga_ref[...], gr_ref[...]
        ua = ga * _sigmoid(ga) * oa_ref[...]
        ub = gr * _sigmoid(gr) * on_ref[...]
        ya = _dot(ua.astype(BF16), wb_ref[:, :512], NT)
        yb = _dot(ub.astype(BF16), wb_ref[:, 512:], NT)
        merged = _sigmoid(gm0_ref[...]) * ya + _sigmoid(gm1_ref[...]) * yb
        xn = x_ref[...] + _dot(merged.astype(BF16), wo_ref[...])
        if head is None:
            xn_ref, ya_ref, yb_ref = rest
            xn_ref[...] = xn
        else:
            g_ref, t_ref, dx_ref, ya_ref, yb_ref, dg_ref, loss_ref, acc_g, acc_l = rest
            i = pl.program_id(0)

            @pl.when(i == 0)
            def _():
                acc_g[...] = jnp.zeros_like(acc_g)
                acc_l[...] = jnp.zeros_like(acc_l)

            gv = g_ref[...]
            r = lax.rsqrt(jnp.mean(xn * xn, axis=-1, keepdims=True) + EPS)
            xh = xn * r
            err = xh * gv - t_ref[...]
            dy = err * (1.0 / d)
            gy = dy * gv
            dx_ref[...] = r * (gy - xh * jnp.mean(gy * xh, axis=-1, keepdims=True))
            acc_g[...] += jnp.sum((dy * xh).reshape(tm // 8, 8, d), axis=0)
            acc_l[...] += jnp.sum((err * err).reshape(tm // 8, 8, d), axis=0)

            @pl.when(i == n - 1)
            def _():
                dg_ref[...] = jnp.sum(acc_g[...], axis=0, keepdims=True)
                tot = jnp.sum(jnp.sum(acc_l[...], axis=0, keepdims=True), axis=1, keepdims=True)
                loss_ref[...] = jnp.broadcast_to(tot * (0.5 / d), (1, 128))
        ya_ref[...] = ya.astype(BF16)
        yb_ref[...] = yb.astype(BF16)

    row = lambda w, j: pl.BlockSpec((tm, w), lambda i: (i, j))
    const = lambda shape: pl.BlockSpec(shape, lambda i: (0, 0))
    in_specs = [row(d, 0), row(512, SEG["ga"][2] // 512), row(512, SEG["gr"][2] // 512),
                row(1024, SEG["gm"][2] // 1024), row(1024, SEG["gm"][2] // 1024 + 1),
                row(512, 0), row(512, 0), const((d, 1024)), const((d, d))]
    out_specs = [row(d, 0), row(d, 0), row(d, 0)]
    out_shape = [SDS((t, d), F32), SDS((t, d), BF16), SDS((t, d), BF16)]
    args, scratch = [x, z, z, z, z, oa, on, wb_t, wout], []
    if head is not None:
        in_specs += [const((1, d)), row(d, 0)]
        out_specs += [const((1, d)), const((1, 128))]
        out_shape += [SDS((1, d), F32), SDS((1, 128), F32)]
        args += list(head)
        scratch = [pltpu.VMEM((8, d), F32), pltpu.VMEM((8, d), F32)]
    return pl.pallas_call(
        body, name="merge_fwd", grid=(n,), in_specs=in_specs, out_specs=out_specs, out_shape=out_shape,
        scratch_shapes=scratch,
        compiler_params=_params(("arbitrary",) if head is not None else ("parallel",)),
    )(*args)


def _merge_bwd(dxo, z, oa, on, ya, yb, wb_t, wout):
    t, d = dxo.shape
    tm = min(256, t)
    n = t // tm

    def body(dx_ref, ga_ref, gr_ref, gm0_ref, gm1_ref, oa_ref, on_ref, ya_ref, yb_ref, wb_ref, wo_ref,
             doa_ref, don_ref, dz_ref, dwo_ref, dwb_ref, acc_o, acc_b):
        i = pl.program_id(0)

        @pl.when(i == 0)
        def _():
            acc_o[...] = jnp.zeros_like(acc_o)
            acc_b[...] = jnp.zeros_like(acc_b)

        dxb = dx_ref[...].astype(BF16)
        ya, yb = ya_ref[...].astype(F32), yb_ref[...].astype(F32)
        g0, g1 = _sigmoid(gm0_ref[...]), _sigmoid(gm1_ref[...])
        mb = (g0 * ya + g1 * yb).astype(BF16)
        dm = _dot(dxb, wo_ref[...], NT)
        dya = (dm * g0).astype(BF16)
        dyb = (dm * g1).astype(BF16)
        dz_ref[:, 1024:2048] = (dm * ya * g0 * (1.0 - g0)).astype(BF16)
        dz_ref[:, 2048:3072] = (dm * yb * g1 * (1.0 - g1)).astype(BF16)

        def branch(g_ref, o_ref, dy, w, do_ref, lo):
            gv, ov = g_ref[...], o_ref[...]
            sg = _sigmoid(gv)
            silu = gv * sg
            du = _dot(dy, w)
            do_ref[...] = du * silu
            dz_ref[:, lo:lo + 512] = (du * ov * (sg * (1.0 + gv * (1.0 - sg)))).astype(BF16)
            acc_b[:, lo:lo + 512] += _dot(dy, (silu * ov).astype(BF16), TN)

        branch(ga_ref, oa_ref, dya, wb_ref[:, :512], doa_ref, 0)
        branch(gr_ref, on_ref, dyb, wb_ref[:, 512:], don_ref, 512)
        acc_o[...] += _dot(mb, dxb, TN)

        @pl.when(i == n - 1)
        def _():
            dwo_ref[...] = acc_o[...].astype(BF16)
            dwb_ref[...] = acc_b[...].astype(BF16)

    row = lambda w, j: pl.BlockSpec((tm, w), lambda i: (i, j))
    const = lambda shape: pl.BlockSpec(shape, lambda i: (0, 0))
    return pl.pallas_call(
        body, name="merge_bwd", grid=(n,),
        in_specs=[row(d, 0), row(512, SEG["ga"][2] // 512), row(512, SEG["gr"][2] // 512),
                  row(1024, SEG["gm"][2] // 1024), row(1024, SEG["gm"][2] // 1024 + 1),
                  row(512, 0), row(512, 0), row(d, 0), row(d, 0), const((d, 1024)), const((d, d))],
        out_specs=[row(512, 0), row(512, 0), row(3072, 0), const((d, d)), const((d, 1024))],
        out_shape=[SDS((t, 512), F32), SDS((t, 512), F32), SDS((t, 3072), BF16), SDS((d, d), BF16),
                   SDS((d, 1024), BF16)],
        scratch_shapes=[pltpu.VMEM((d, d), F32), pltpu.VMEM((d, 1024), F32)],
        compiler_params=_params(("arbitrary",)),
    )(dxo, z, z, z, z, oa, on, ya, yb, wb_t, wout)


def _ret_bwd(qrot, krot, vb, orr, don, gnw, lgf, lgb, cos, sin):
    t = qrot.shape[0]
    c = RET_CHUNK
    nc = t // c
    hd = RET_HEAD_DIM
    unroll = 4 if nc % 4 == 0 else 1

    def body(lgf_ref, lgb_ref, q_ref, k_ref, v_ref, o_ref, dn_ref, w_ref, c_ref, s_ref,
             dq_ref, dk_ref, dv_ref, dw_ref, dlf_ref, dlb_ref, qt, kt, dob, uf, ub, wf, wb, sfa, sba, gfa, gba):
        h = pl.program_id(0)
        fw = _Dir(lgf_ref[h], False)
        bw = _Dir(lgb_ref[h], True)
        fw.dt, bw.dt = fw.d.T, bw.d.T

        o = o_ref[...]
        xc = o - jnp.mean(o, axis=-1, keepdims=True)
        r = lax.rsqrt(jnp.mean(xc * xc, axis=-1, keepdims=True) + EPS)
        xh = xc * r
        dn = dn_ref[...]
        gy = dn * w_ref[...]
        d_o = r * (gy - jnp.mean(gy, axis=-1, keepdims=True) - xh * jnp.mean(gy * xh, axis=-1, keepdims=True))
        dw_ref[...] = jnp.sum(dn * xh, axis=0, keepdims=True)
        dob[...] = d_o.astype(BF16)
        for i in range(nc):
            qt[i] = q_ref[i * c:(i + 1) * c, :].astype(F32).T.astype(BF16)
            kt[i] = k_ref[i * c:(i + 1) * c, :].astype(F32).T.astype(BF16)

        def rows(ci):
            return pl.ds(pl.multiple_of(ci * c, c), c)

        def products(ci, carry):
            sl = rows(ci)
            vv, do32 = v_ref[sl, :], dob[sl, :].astype(F32)
            ktf = kt[ci].astype(F32)
            uf[ci] = _dot((ktf * fw.kd_row).astype(BF16), vv)
            ub[ci] = _dot((ktf * bw.kd_row).astype(BF16), vv)
            wf[ci] = _dot(qt[ci], (do32 * fw.qd).astype(BF16))
            wb[ci] = _dot(qt[ci], (do32 * bw.qd).astype(BF16))
            return carry

        lax.fori_loop(0, nc, products, 0, unroll=unroll)

        def scan(i, carry):
            sf, sb, gf, gb = carry
            j = nc - 1 - i
            sfa[i] = sf.astype(BF16)
            sba[j] = sb.astype(BF16)
            gfa[j] = gf.astype(BF16)
            gba[i] = gb.astype(BF16)
            return sf * fw.cd + uf[i], sb * bw.cd + ub[j], gf * fw.cd + wf[j], gb * bw.cd + wb[i]

        zero = jnp.zeros((hd, hd), F32)
        lax.fori_loop(0, nc, scan, (zero, zero, zero, zero))

        def one_dir(p, s_all, g_all, ci, qq, kk, vv, do, a, bm):
            sb, gb = s_all[ci], g_all[ci]
            doq = (do.astype(F32) * p.qd).astype(BF16)
            dqc = _dot(doq, sb, NT)
            kkd = (kk.astype(F32) * p.kd_col).astype(BF16)
            dk2 = _dot(vv, gb, NT) * p.kd_col
            terms = (p.dist * p.d * a * bm + p.wq * qq.astype(F32) * dqc + p.wk * kk.astype(F32) * dk2
                     + (float(c) * p.cd) * gb.astype(F32) * sb.astype(F32))
            return dqc, dk2, _dot(kkd, gb), terms

        d_both, dt_both = fw.d + bw.d, fw.dt + bw.dt

        def chunk(ci, carry):
            af, ab = carry
            sl = rows(ci)
            qq, kk, vv, do = q_ref[sl, :], k_ref[sl, :], v_ref[sl, :], dob[sl, :]
            a, bm = _dot(qq, kk, NT), _dot(do, vv, NT)
            at, bt = _dot(kk, qq, NT), _dot(vv, do, NT)
            dqf, dkf, dvf, tf = one_dir(fw, sfa, gfa, ci, qq, kk, vv, do, a, bm)
            dqb, dkb, dvb, tb = one_dir(bw, sba, gba, ci, qq, kk, vv, do, a, bm)
            cc, ss = c_ref[sl, :], s_ref[sl, :]
            dq = _dot((bm * d_both).astype(BF16), kk) + dqf + dqb
            dk = _dot((bt * dt_both).astype(BF16), qq) + dkf + dkb
            dq_ref[sl, :] = _rope_bwd(dq, cc, ss, hd // 4).astype(BF16)
            dk_ref[sl, :] = (_rope_bwd(dk, cc, ss, hd // 4) * (hd ** -0.5)).astype(BF16)
            dv_ref[sl, :] = (_dot((at * dt_both).astype(BF16), do) + dvf + dvb).astype(BF16)
            return af + tf, ab + tb

        pair = 8 if nc % 8 == 0 else 1

        def chunks(i, carry):
            for j in range(pair):
                carry = chunk(i * pair + j, carry)
            return carry

        af, ab = lax.fori_loop(0, nc // pair, chunks, (zero, zero))
        tot = lambda m: jnp.sum(jnp.sum(m, axis=0, keepdims=True), axis=1, keepdims=True)
        dlf_ref[...] = jnp.broadcast_to(tot(af).reshape(1, 1, 1), (1, 8, 128))
        dlb_ref[...] = jnp.broadcast_to(tot(ab).reshape(1, 1, 1), (1, 8, 128))

    smem = pl.BlockSpec(memory_space=pltpu.SMEM)
    head = pl.BlockSpec((t, 128), lambda h: (0, h))
    vec = pl.BlockSpec((1, 128), lambda h: (0, h))
    scal = pl.BlockSpec((1, 8, 128), lambda h: (h, 0, 0))
    table = pl.BlockSpec((t, 128), lambda h: (0, 0))
    mats = lambda dt: pltpu.VMEM((nc, hd, hd), dt)
    return pl.pallas_call(
        body, name="ret_bwd", grid=(RET_HEADS,),
        in_specs=[smem, smem, head, head, head, head, head, vec, table, table],
        out_specs=[head, head, head, vec, scal, scal],
        out_shape=[SDS((t, RET_WIDTH), BF16)] * 3 + [SDS((1, RET_WIDTH), F32), SDS((RET_HEADS, 8, 128), F32),
                                                    SDS((RET_HEADS, 8, 128), F32)],
        scratch_shapes=[pltpu.VMEM((nc, hd, c), BF16), pltpu.VMEM((nc, hd, c), BF16), pltpu.VMEM((t, hd), BF16),
                        mats(F32), mats(F32), mats(F32), mats(F32), mats(BF16), mats(BF16), mats(BF16), mats(BF16)],
        compiler_params=_params(("parallel",)),
    )(lgf, lgb, qrot, krot, vb, orr, don, gnw, cos, sin)


def _attn_bwd(q, qt, k, v, doa, oa, lse, ex=None):
    t = q.shape[1]
    tq = min(ATTN_BWD_QUERY_TILE, t)
    nq = t // tq
    tk = min(ATTN_BWD_KEY_CHUNK, t)
    nk = t // tk
    hd = ATTN_HEAD_DIM
    scale = hd ** -0.5

    def body(q_ref, qt_ref, k_ref, v_ref, do_ref, o_ref, lse_ref, dq_ref, dkt_ref, dvt_ref):
        p, i = pl.program_id(0), pl.program_id(1)

        @pl.when(jnp.logical_and(p % 2 == 0, i == 0))
        def _():
            dkt_ref[...] = jnp.zeros_like(dkt_ref)
            dvt_ref[...] = jnp.zeros_like(dvt_ref)

        dov, ov = do_ref[...], o_ref[...]
        dovt = dov.T
        lanes = lambda col: jnp.concatenate([col] * (tk // 128), axis=1)
        outs = []
        for j in range(2):
            qq, qqt = q_ref[j], qt_ref[j]
            do32 = dov[:, j * hd:(j + 1) * hd]
            do, dot_ = do32.astype(BF16), dovt[j * hd:(j + 1) * hd, :].astype(BF16)
            dd = lanes(jnp.broadcast_to(jnp.sum(do32 * ov[:, j * hd:(j + 1) * hd], axis=1, keepdims=True), (tq, 128)))
            lse_j = lanes(jnp.broadcast_to(lse_ref[j], (128, tq)).T)
            dq = jnp.zeros((tq, hd), F32)
            for c in range(nk):
                sl = slice(c * tk, (c + 1) * tk)
                kc, vc = k_ref[0, sl, :], v_ref[0, sl, :]
                pr = jnp.exp(_dot(qq, kc, NT) - lse_j)
                ds = (pr * (_dot(do, vc, NT) - dd)).astype(BF16)
                dvt_ref[0, :, sl] += _dot(dot_, pr.astype(BF16))
                dkt_ref[0, :, sl] += _dot(qqt, ds)
                dq = dq + _dot(ds, kc)
            outs.append(dq * scale)
        dq_ref[...] = jnp.concatenate(outs, axis=-1)

    kv = pl.BlockSpec((1, t, hd), lambda p, i: (p // 2, 0, 0))
    kvt = pl.BlockSpec((1, hd, t), lambda p, i: (p // 2, 0, 0))
    pair = pl.BlockSpec((tq, 128), lambda p, i: (i, p))
    first = lambda: jnp.logical_and(pl.program_id(0) == 0, pl.program_id(1) == 0)
    last = lambda: jnp.logical_and(pl.program_id(0) == 3, pl.program_id(1) == nq - 1)
    xi, xo, xs, xscr, xargs = _ex_args(ex)
    return pl.pallas_call(
        _with_exchange(body, 7, 3, 0, ex, first, last), name="attn_bwd", grid=(4, nq),
        in_specs=[pl.BlockSpec((2, tq, hd), lambda p, i: (p, i, 0)), pl.BlockSpec((2, hd, tq), lambda p, i: (p, 0, i)),
                  kv, kv, pair, pair, pl.BlockSpec((2, 1, tq), lambda p, i: (p, 0, i))] + xi,
        out_specs=[pair, kvt, kvt] + xo,
        out_shape=[SDS((t, ATTN_WIDTH), F32), SDS((ATTN_KV_HEADS, hd, t), F32),
                   SDS((ATTN_KV_HEADS, hd, t), F32)] + xs,
        scratch_shapes=xscr,
        compiler_params=_params(("arbitrary", "arbitrary")),
    )(q, qt, k, v, doa, oa, lse, *xargs)


def _attn_post_bwd(dq, dk, dv, z, qn, kn, cos, sin, ones_bd):
    t = z.shape[0]
    tm = min(512, t)
    n = t // tm
    hd = ATTN_HEAD_DIM

    def body(dq_ref, dk_ref, dv_ref, zq_ref, zkv_ref, qn_ref, kn_ref, c_ref, s_ref, b_ref,
             dz_ref, dqn_ref, dkn_ref, acc_q, acc_k):
        i = pl.program_id(0)

        @pl.when(i == 0)
        def _():
            acc_q[...] = jnp.zeros_like(acc_q)
            acc_k[...] = jnp.zeros_like(acc_k)

        bd = b_ref[...]
        c2, s2 = c_ref[...], s_ref[...]

        def norm_bwd(dy, x, w, ones, cos_t, sin_t, acc):
            dyr = _rope_bwd(dy, cos_t, sin_t, hd // 4)
            r = lax.rsqrt(_group_mean(x * x, ones) + EPS)
            xh = x * r
            gy = dyr * w
            acc[...] += jnp.sum((dyr * xh).reshape(tm // 8, 8, x.shape[-1]), axis=0)
            return r * (gy - xh * _group_mean(gy * xh, ones))

        cq = jnp.concatenate([c2] * 4, axis=-1)
        sq = jnp.concatenate([s2] * 4, axis=-1)
        dz_ref[:, :512] = norm_bwd(dq_ref[...], zq_ref[...], qn_ref[...], bd, cq, sq, acc_q).astype(BF16)
        zkv = zkv_ref[...]
        dkk = jnp.concatenate([dk_ref[0], dk_ref[1]], axis=0).T
        dz_ref[:, 512:640] = norm_bwd(dkk, zkv[:, :128], kn_ref[...], bd[:128, :128], c2, s2, acc_k).astype(BF16)
        dz_ref[:, 640:768] = jnp.concatenate([dv_ref[0], dv_ref[1]], axis=0).T.astype(BF16)

        @pl.when(i == n - 1)
        def _():
            dqn_ref[...] = jnp.sum(acc_q[...], axis=0, keepdims=True)
            dkn_ref[...] = jnp.sum(acc_k[...], axis=0, keepdims=True)

    kv_blk = SEG["ka"][2] // 256
    kvs = pl.BlockSpec((ATTN_KV_HEADS, hd, tm), lambda i: (0, 0, i))
    const = lambda shape: pl.BlockSpec(shape, lambda i: (0, 0))
    return pl.pallas_call(
        body, name="attn_post_bwd", grid=(n,),
        in_specs=[pl.BlockSpec((tm, 512), lambda i: (i, 0)), kvs, kvs,
                  pl.BlockSpec((tm, 512), lambda i: (i, 0)), pl.BlockSpec((tm, 256), lambda i: (i, kv_blk)),
                  const((1, 512)), const((1, 128)),
                  pl.BlockSpec((tm, 128), lambda i: (i, 0)), pl.BlockSpec((tm, 128), lambda i: (i, 0)),
                  const((512, 512))],
        out_specs=[pl.BlockSpec((tm, 768), lambda i: (i, 0)), const((1, 512)), const((1, 128))],
        out_shape=[SDS((t, 768), BF16), SDS((1, 512), F32), SDS((1, 128), F32)],
        scratch_shapes=[pltpu.VMEM((8, 512), F32), pltpu.VMEM((8, 128), F32)],
        compiler_params=_params(("arbitrary",)),
    )(dq, dk, dv, z, z, qn, kn, cos, sin, ones_bd)


def _in_bwd(dxo, x, g, w_t, dz_a, dz_m, dqr, dkr, dvr, after=None):
    t, d = x.shape
    tm = min(256, t)
    n = t // tm
    parts = [(0, 0, 768, 0), (1, 0, 512, SEG["ga"][0]), (2, 0, 512, SEG["qr"][0]), (3, 0, 512, SEG["kr"][0]),
             (4, 0, 512, SEG["vr"][0]), (1, 512, 2560, SEG["gr"][0])]

    def body(dx_ref, x_ref, g_ref, w_ref, a_ref, m_ref, q_ref, k_ref, v_ref, o_ref, dg_ref, acc):
        i = pl.program_id(0)

        @pl.when(i == 0)
        def _():
            acc[...] = jnp.zeros_like(acc)

        pieces = [a_ref, m_ref, q_ref, k_ref, v_ref]
        dh = jnp.zeros((tm, d), F32)
        for pi, lo, w, row in parts:
            dh = dh + _dot(pieces[pi][:, lo:lo + w], w_ref[row:row + w, :])
        xv = x_ref[...]
        r = lax.rsqrt(jnp.mean(xv * xv, axis=-1, keepdims=True) + EPS)
        xh = xv * r
        gy = dh * g_ref[...]
        o_ref[...] = dx_ref[...] + r * (gy - xh * jnp.mean(gy * xh, axis=-1, keepdims=True))
        acc[...] += jnp.sum((dh * xh).reshape(tm // 8, 8, d), axis=0)

        @pl.when(i == n - 1)
        def _():
            dg_ref[...] = jnp.sum(acc[...], axis=0, keepdims=True)

    row = lambda w: pl.BlockSpec((tm, w), lambda i: (i, 0))
    const = lambda shape: pl.BlockSpec(shape, lambda i: (0, 0))
    extra = [] if after is None else [after]
    return pl.pallas_call(
        (lambda *refs: body(*refs[:9], *refs[9 + len(extra):])), name="in_bwd", grid=(n,),
        in_specs=[row(d), row(d), const((1, d)), const((D_IN, d)), row(768), row(3072), row(512), row(512),
                  row(512)] + [const(a.shape) for a in extra],
        out_specs=[row(d), const((1, d))],
        out_shape=[SDS((t, d), F32), SDS((1, d), F32)],
        scratch_shapes=[pltpu.VMEM((8, d), F32)],
        compiler_params=_params(("arbitrary",)),
    )(dxo, x, g, w_t, dz_a, dz_m, dqr, dkr, dvr, *extra)


def _dw_in(h_t, dz_a, dz_m, dqr, dkr, dvr):
    d, t = h_t.shape
    tn = 256
    parts = [(0, 0, 0, 3), (1, 0, SEG["ga"][0] // tn, 2), (2, 0, SEG["qr"][0] // tn, 2),
             (3, 0, SEG["kr"][0] // tn, 2), (4, 0, SEG["vr"][0] // tn, 2), (1, 2, SEG["gr"][0] // tn, 10)]
    pieces = [dz_a, dz_m, dqr, dkr, dvr]

    def col_block(pi):
        mine = [(c0, r0, n) for q, c0, r0, n in parts if q == pi]

        def index(j):
            c0, r0, n = mine[0]
            blk = c0 + jnp.clip(j - r0, 0, n - 1)
            for c0, r0, n in mine[1:]:
                blk = jnp.where(j >= r0, c0 + jnp.clip(j - r0, 0, n - 1), blk)
            return 0, blk

        return index

    def body(h_ref, *refs):
        o_ref = refs[-1]
        j = pl.program_id(0)
        for pi, _, r0, n in parts:
            @pl.when(jnp.logical_and(j >= r0, j < r0 + n))
            def _(p_ref=refs[pi]):
                o_ref[...] = _dot(h_ref[...], p_ref[...]).T.astype(BF16)

    return pl.pallas_call(
        body, name="dw_in", grid=(D_IN // tn,),
        in_specs=[pl.BlockSpec((d, t), lambda j: (0, 0))] + [pl.BlockSpec((t, tn), col_block(pi)) for pi in range(5)],
        out_specs=pl.BlockSpec((tn, d), lambda j: (j, 0)),
        out_shape=SDS((D_IN, d), BF16),
        compiler_params=_params(("arbitrary",)),
    )(h_t, *pieces)


def _adamw_math(w, g, m, v):
    mn = ADAM_B1 * m + (1.0 - ADAM_B1) * g
    vn = ADAM_B2 * v + (1.0 - ADAM_B2) * (g * g)
    m_hat = mn / (1.0 - ADAM_B1 ** ADAM_STEP)
    v_hat = vn / (1.0 - ADAM_B2 ** ADAM_STEP)
    return -ADAM_LR * (m_hat / (jnp.sqrt(v_hat) + ADAM_EPS) + ADAM_WD * w), mn, vn


def _sum_adamw(recvs, w, m, v, lane0, tn, layer0=0, prev=None, own=None):
    _, r, c = w.shape
    j0 = lane0 // tn
    n = len(recvs)
    has_own = own is not None

    def body(*refs):
        mine_ref, refs = (refs[0], refs[1:]) if has_own else (None, refs)
        w_ref, m_ref, v_ref = refs[n:n + 3]
        g_ref, d_ref, mo_ref, vo_ref = refs[-4:]

        def run(r_ref):
            def slot(s):
                if has_own:
                    return jnp.where(mine_ref[0] == s, refs[n + 3][...], r_ref[s]).astype(F32)
                return r_ref[s].astype(F32)

            g = slot(0)
            for s in range(1, N_DEV):
                g = g + slot(s)
            g_ref[0] = g
            d_ref[0], mo_ref[0], vo_ref[0] = _adamw_math(w_ref[0], g, m_ref[0], v_ref[0])

        for i in range(n):
            pl.when(pl.program_id(0) == i)(functools.partial(run, refs[i]))

    slots = pl.BlockSpec((N_DEV, r, tn), lambda i, j, *_: (0, 0, j0 + j))
    blk = pl.BlockSpec((1, r, tn), lambda i, j, *_: (layer0 + i, 0, j))
    before = [] if prev is None else list(prev)
    in_specs, args = [slots] * n + [blk] * 3, [*recvs, w, m, v]
    if has_own:
        assert n == 1
        in_specs.append(pl.BlockSpec((r, tn), lambda i, j, mine: (mine[0], j0 + j)))
        args.append(own[0])
    n_pre = len(args) + has_own
    return pl.pallas_call(
        body, name="sum_adamw",
        grid_spec=pltpu.PrefetchScalarGridSpec(
            num_scalar_prefetch=int(has_own), grid=(n, c // tn),
            in_specs=in_specs + [ANY] * len(before), out_specs=[blk] * 4),
        out_shape=[SDS(w.shape, F32)] * 4,
        input_output_aliases={n_pre + k: k for k in range(len(before))},
        compiler_params=_params(("parallel", "parallel")),
    )(*([own[1]] if has_own else []), *args, *before)


def _adamw(w, g, m, v):
    rows, cols = w.shape
    tr = 256 if rows % 256 == 0 else rows

    def body(w_ref, g_ref, m_ref, v_ref, d_ref, mo_ref, vo_ref):
        d_ref[...], mo_ref[...], vo_ref[...] = _adamw_math(w_ref[...], g_ref[...], m_ref[...], v_ref[...])

    blk = pl.BlockSpec((tr, cols), lambda i: (i, 0))
    return pl.pallas_call(
        body, name="adamw", grid=(rows // tr,),
        in_specs=[blk] * 4, out_specs=[blk] * 3, out_shape=[SDS((rows, cols), F32)] * 3,
        compiler_params=_params(("parallel",)),
    )(w, g, m, v)


def _all_gather(shards):
    na = len(shards)
    chips = (4, 2, 6)

    def body(*refs):
        ins, outs = refs[:na], refs[na:2 * na]
        send_sems, recv_sems, local_sems = refs[2 * na:]
        _, mine = _flip(0)

        def rows(a, idx):
            r = shards[a].shape[0]
            return outs[a].at[pl.ds(pl.multiple_of(idx * r, 16), r), :]

        def copy(a, slot, block_idx, to, src=None):
            return pltpu.make_async_remote_copy(
                src_ref=rows(a, block_idx) if src is None else src, dst_ref=rows(a, block_idx),
                send_sem=send_sems.at[a, slot], recv_sem=recv_sems.at[a, slot],
                device_id=to, device_id_type=MESH_ID)

        sibling, sibling_idx = _flip(1)
        local, started = [], []
        for a in range(na):
            cp = pltpu.make_async_copy(ins[a], rows(a, mine), local_sems.at[a])
            cp.start()
            local.append(cp)
            first = [copy(a, 0, mine, sibling, src=ins[a])]
            first += [copy(a, 1 + j, mine, _flip(k)[0], src=ins[a]) for j, k in enumerate(chips)]
            for cp in first:
                cp.start()
            started += first
        for a in range(na):
            for j, k in enumerate(chips):
                _, theirs = _flip(k)
                copy(a, 1 + j, theirs, _flip(0)[0]).wait_recv()
                fwd = copy(a, 4 + j, theirs, sibling)
                fwd.start()
                started.append(fwd)
        for a in range(na):
            copy(a, 0, sibling_idx, _flip(0)[0]).wait_recv()
            for j, k in enumerate(chips):
                _, theirs = _flip(k | 1)
                copy(a, 4 + j, theirs, _flip(0)[0]).wait_recv()
        for cp in started:
            cp.wait_send()
        for cp in local:
            cp.wait()

    return pl.pallas_call(
        body, name="all_gather_weights",
        in_specs=[ANY] * na, out_specs=[ANY] * na,
        out_shape=[SDS((N_DEV * s.shape[0], s.shape[1]), s.dtype) for s in shards],
        scratch_shapes=[pltpu.SemaphoreType.DMA((na, 7)), pltpu.SemaphoreType.DMA((na, 7)),
                        pltpu.SemaphoreType.DMA((na,))],
        compiler_params=pltpu.CompilerParams(has_side_effects=True),
    )(*shards)


def _scatter_blocks_of(g_ref, rows, idx):
    return g_ref.at[pl.ds(pl.multiple_of(idx * rows, 16), rows), :]


def _scatter_start(g):
    rows = g.shape[0] // N_DEV
    land_shape = (N_DEV, rows, g.shape[1])

    def body(g_ref, land_ref, send_sems, recv_sems, g_thru, land_thru, token):
        _, mine = _flip(0)
        for k in range(1, N_DEV):
            peer, theirs = _flip(k)
            pltpu.make_async_remote_copy(
                src_ref=_scatter_blocks_of(g_ref, rows, theirs), dst_ref=land_ref.at[mine],
                send_sem=send_sems.at[k - 1], recv_sem=recv_sems.at[k - 1],
                device_id=peer, device_id_type=MESH_ID).start()
        token[...] = jnp.zeros_like(token)

    hbm, sem = pl.BlockSpec(memory_space=pltpu.HBM), pl.BlockSpec(memory_space=pltpu.SEMAPHORE)
    return pl.pallas_call(
        body, name="scatter_start",
        out_shape=(pltpu.SemaphoreType.DMA((N_DEV - 1,)), pltpu.SemaphoreType.DMA((N_DEV - 1,)),
                   pltpu.HBM(g.shape, g.dtype), pltpu.HBM(land_shape, g.dtype), SDS((8, 128), F32)),
        in_specs=(hbm, hbm), out_specs=(sem, sem, hbm, hbm, pl.BlockSpec(memory_space=pltpu.VMEM)),
        input_output_aliases={0: 2, 1: 3},
        compiler_params=pltpu.CompilerParams(has_side_effects=pltpu.SideEffectType.DATAFLOW_SIDE_EFFECTING),
    )(pltpu.with_memory_space_constraint(g, pltpu.HBM),
      pltpu.with_memory_space_constraint(lax.empty(land_shape, g.dtype), pltpu.HBM))


def _scatter_wait(send_sems, recv_sems, g_thru, land_thru, after):
    rows = g_thru.shape[0] // N_DEV

    def body(g_ref, land_ref, send_sems, recv_sems, *rest):
        me, _ = _flip(0)
        for k in range(1, N_DEV):
            _, theirs = _flip(k)
            copy = pltpu.make_async_remote_copy(
                src_ref=_scatter_blocks_of(g_ref, rows, theirs), dst_ref=land_ref.at[theirs],
                send_sem=send_sems.at[k - 1], recv_sem=recv_sems.at[k - 1],
                device_id=me, device_id_type=MESH_ID)
            copy.wait_send()
            copy.wait_recv()

    hbm, sem = pl.BlockSpec(memory_space=pltpu.HBM), pl.BlockSpec(memory_space=pltpu.SEMAPHORE)
    return pl.pallas_call(
        body, name="scatter_wait",
        out_shape=(pltpu.HBM(g_thru.shape, g_thru.dtype), pltpu.HBM(land_thru.shape, land_thru.dtype)),
        in_specs=(hbm, hbm, sem, sem) + (ANY,) * len(after), out_specs=(hbm, hbm), input_output_aliases={0: 0, 1: 1},
        compiler_params=pltpu.CompilerParams(has_side_effects=pltpu.SideEffectType.DATAFLOW_SIDE_EFFECTING),
    )(g_thru, land_thru, send_sems, recv_sems, *after)


def _all_reduce_small(packed):
    shape = packed.shape

    def body(p_ref, o_ref, slots, send_sems, recv_sems):
        me, mine = _flip(0)
        slots[mine] = p_ref[...]
        sends = []
        for k in range(1, N_DEV):
            peer, _ = _flip(k)
            cp = pltpu.make_async_remote_copy(
                src_ref=p_ref, dst_ref=slots.at[mine], send_sem=send_sems.at[k - 1], recv_sem=recv_sems.at[k - 1],
                device_id=peer, device_id_type=MESH_ID)
            cp.start()
            sends.append(cp)
        for k in range(1, N_DEV):
            _, theirs = _flip(k)
            pltpu.make_async_remote_copy(
                src_ref=p_ref, dst_ref=slots.at[theirs], send_sem=send_sems.at[k - 1],
                recv_sem=recv_sems.at[k - 1], device_id=me, device_id_type=MESH_ID).wait_recv()
        for cp in sends:
            cp.wait_send()
        acc = slots[0]
        for s in range(1, N_DEV):
            acc = acc + slots[s]
        o_ref[...] = acc

    vm = pl.BlockSpec(memory_space=pltpu.VMEM)
    return pl.pallas_call(
        body, name="all_reduce_small", in_specs=[vm], out_specs=vm, out_shape=SDS(shape, F32),
        scratch_shapes=[pltpu.VMEM((N_DEV,) + shape, F32), pltpu.SemaphoreType.DMA((7,)),
                        pltpu.SemaphoreType.DMA((7,))],
        compiler_params=pltpu.CompilerParams(has_side_effects=True),
    )(packed)


def _layer_fwd(x, p, tabs, ex):
    z, h_t, q, qt, k, v, vt, qrot, krot, vb = _in_proj(x, p["norm_g"], p["w_in_t"], p["qn"], p["kn"], tabs["ca"],
                                                       tabs["sa"], tabs["ones"], tabs["cr"], tabs["sr"])
    oa, lse, *gathered = _attn_fwd(q, k, vt, ex)
    orr, on = _ret_fwd(qrot, krot, vb, p["lgf"], p["lgb"], p["gnw"])
    return z, h_t, q, qt, k, v, lse, oa, qrot, krot, vb, orr, on, gathered


def _layer_bwd(dxo, s, p, tabs, ex_attn, scatter_w_in):
    doa, don, dz_m, d_wout, d_wb_t = _merge_bwd(dxo, s["z"], s["oa"], s["on"], s["ya"], s["yb"], p["wb_t"], p["w_out"])
    dq_a, dk_a, dv_a, *recv_attn = _attn_bwd(s["q"], s["qt"], s["k"], s["v"], doa, s["oa"], s["lse"],
                                              ex_attn(d_wb_t, d_wout))
    dz_a, d_qn, d_kn = _attn_post_bwd(dq_a, dk_a, dv_a, s["z"], p["qn"], p["kn"], tabs["ca"], tabs["sa"],
                                      tabs["ones"])
    dqr, dkr, dvr, d_gnw, d_lgf, d_lgb = _ret_bwd(s["qrot"], s["krot"], s["vb"], s["orr"], don, p["gnw"],
                                                  p["lgf"], p["lgb"], tabs["cr"], tabs["sr"])
    buf = _dw_in(s["h_t"], dz_a, dz_m, dqr, dkr, dvr)
    pending, token = None, None
    if scatter_w_in:
        *pending, token = _scatter_start(buf)
    dx, d_norm_g = _in_bwd(dxo, s["x"], p["norm_g"], p["w_in_t"], dz_a, dz_m, dqr, dkr, dvr, token)
    grads = dict(w_in_t=buf, wb_t=d_wb_t, w_out=d_wout, norm_g=d_norm_g, gnw=d_gnw,
                 qn=d_qn.reshape(ATTN_Q_HEADS, ATTN_HEAD_DIM).sum(axis=0),
                 kn=d_kn.reshape(ATTN_KV_HEADS, ATTN_HEAD_DIM).sum(axis=0),
                 lgf=d_lgf[:, 0, 0], lgb=d_lgb[:, 0, 0])
    return dx, grads, recv_attn, pending


def _adamw_nd(w, g, m, v):
    shape = w.shape
    two_d = (1, shape[0]) if w.ndim == 1 else (-1, shape[-1])
    out = _adamw(w.reshape(two_d), g.reshape(two_d), m.reshape(two_d), v.reshape(two_d))
    return tuple(o.reshape(shape) for o in out)


def kernel(x, norm_g, w_in, attn_q_norm, attn_k_norm, ret_decay_fwd, ret_decay_bwd, ret_gn_w, w_branch_attn, w_branch_ret, w_out, final_norm_g, loss_target, m_norm_g, m_w_in, m_attn_q_norm, m_attn_k_norm, m_ret_decay_fwd, m_ret_decay_bwd, m_ret_gn_w, m_w_branch_attn, m_w_branch_ret, m_w_out, m_final_norm_g, v_norm_g, v_w_in, v_attn_q_norm, v_attn_k_norm, v_ret_decay_fwd, v_ret_decay_bwd, v_ret_gn_w, v_w_branch_attn, v_w_branch_ret, v_w_out, v_final_norm_g):
    t, d = x.shape[1], x.shape[2]
    x2, target = x[0], loss_target[0]

    w_in_sh, wb_sh, wout_sh = [], [], []
    for l in range(DEPTH):
        w_in_sh.append(jnp.swapaxes(w_in[l], 0, 1).astype(BF16))
        wb_sh.append(jnp.concatenate([w_branch_attn[l].T, w_branch_ret[l].T], axis=1).astype(BF16))
        wout_sh.append(w_out[l].astype(BF16))

    ca, sa = _rope_tables(t, ATTN_HEAD_DIM)
    cr, sr = _rope_tables(t, RET_HEAD_DIM)
    grp = jnp.arange(ATTN_WIDTH) // ATTN_HEAD_DIM
    tabs = dict(ca=jnp.tile(ca, (1, 2)), sa=jnp.tile(sa, (1, 2)), cr=cr, sr=sr,
                ones=jnp.where(grp[:, None] == grp[None, :], 1.0 / ATTN_HEAD_DIM, 0.0).astype(BF16))
    layers = []
    for l in range(DEPTH):
        layers.append(dict(
            norm_g=norm_g[l][None], qn=jnp.tile(attn_q_norm[l], ATTN_Q_HEADS)[None],
            kn=jnp.tile(attn_k_norm[l], ATTN_KV_HEADS)[None], gnw=ret_gn_w[l][None],
            lgf=jax.nn.log_sigmoid(ret_decay_fwd[l]), lgb=jax.nn.log_sigmoid(ret_decay_bwd[l])))

    layers[0]["w_in_t"], = _all_gather([w_in_sh[0]])
    gathers = [_Exchange("gather", [wb_sh[0], wout_sh[0], w_in_sh[1]]), _Exchange("gather", [wb_sh[1], wout_sh[1]])]
    h = x2
    saved = []
    for l in range(DEPTH):
        p = layers[l]
        z, h_t, q, qt, k, v, lse, oa, qrot, krot, vb, orr, on, got = _layer_fwd(h, p, tabs, gathers[l])
        p["wb_t"], p["w_out"] = got[0], got[1]
        if l == 0:
            layers[1]["w_in_t"] = got[2]
        last = (final_norm_g[None], target) if l == DEPTH - 1 else None
        xn, ya, yb, *loss_head = _merge_fwd(h, z, oa, on, p["wb_t"], p["w_out"], last)
        saved.append(dict(x=h, z=z, h_t=h_t, q=q, qt=qt, k=k, v=v, lse=lse, oa=oa, qrot=qrot, krot=krot, vb=vb,
                          orr=orr, on=on, ya=ya, yb=yb))
        h = xn
    dx, (d_final_g, loss_part) = h, loss_head

    grads = [None] * DEPTH
    dx, grads[1], _, _ = _layer_bwd(dx, saved[1], layers[1], tabs, lambda *a: None, False)
    g1 = grads[1]
    ex_attn = lambda d_wb_t, d_wout: _Exchange("scatter", [g1["w_in_t"], g1["wb_t"], g1["w_out"], d_wb_t, d_wout])
    dx, grads[0], recv_attn, pending = _layer_bwd(dx, saved[0], layers[0], tabs, ex_attn, True)
    recv = [None, recv_attn[3], recv_attn[4], recv_attn[0], recv_attn[1], recv_attn[2]]
    tr = lambda a: jnp.swapaxes(a, 1, 2)
    w_in_t = (tr(w_in), tr(m_w_in), tr(v_w_in))
    sharded = {}
    w_in_l1 = _sum_adamw([recv[3]], *w_in_t, 0, 256, layer0=1)
    sharded[id(w_branch_attn)] = [tr(o) for o in _sum_adamw(
        [recv[1], recv[4]], tr(w_branch_attn), tr(m_w_branch_attn), tr(v_w_branch_attn), 0, 512)]
    sharded[id(w_branch_ret)] = [tr(o) for o in _sum_adamw(
        [recv[1], recv[4]], tr(w_branch_ret), tr(m_w_branch_ret), tr(v_w_branch_ret), 512, 512)]
    sharded[id(w_out)] = _sum_adamw([recv[2], recv[5]], w_out, m_w_out, v_w_out, 0, 256)
    g_wba, g_wbr, g_wout = (sharded[id(w)][0] for w in (w_branch_attn, w_branch_ret, w_out))

    packed = jnp.zeros((8, 1024), F32)
    for l in range(DEPTH):
        gl = grads[l]
        packed = packed.at[l].set(gl["norm_g"][0])
        packed = packed.at[2, 512 * l:512 * (l + 1)].set(gl["gnw"][0])
        packed = packed.at[4, 128 * l:128 * l + 64].set(gl["qn"])
        packed = packed.at[4, 256 + 128 * l:256 + 128 * l + 64].set(gl["kn"])
        packed = packed.at[4, 512 + 128 * l:512 + 128 * l + 4].set(gl["lgf"])
        packed = packed.at[4, 768 + 128 * l:768 + 128 * l + 4].set(gl["lgb"])
    packed = packed.at[3].set(d_final_g[0])
    packed = packed.at[5, 0].set(loss_part[0, 0])
    red = _all_reduce_small(packed)
    loss = red[5, 0]
    g_norm_g = red[0:2]
    g_gnw = red[2].reshape(DEPTH, RET_WIDTH)
    g_final = red[3]
    g_qn = jnp.stack([red[4, 128 * l:128 * l + 64] for l in range(DEPTH)])
    g_kn = jnp.stack([red[4, 256 + 128 * l:256 + 128 * l + 64] for l in range(DEPTH)])
    g_lgf = jnp.stack([red[4, 512 + 128 * l:512 + 128 * l + 4] for l in range(DEPTH)])
    g_lgb = jnp.stack([red[4, 768 + 128 * l:768 + 128 * l + 4] for l in range(DEPTH)])
    g_df = g_lgf * jax.nn.sigmoid(-ret_decay_fwd)
    g_db = g_lgb * jax.nn.sigmoid(-ret_decay_bwd)

    grad_w = [g_norm_g, None, g_qn, g_kn, g_df, g_db, g_gnw, g_wba, g_wbr, g_wout, g_final]
    weights = [norm_g, w_in, attn_q_norm, attn_k_norm, ret_decay_fwd, ret_decay_bwd, ret_gn_w, w_branch_attn,
               w_branch_ret, w_out, final_norm_g]
    ms = [m_norm_g, m_w_in, m_attn_q_norm, m_attn_k_norm, m_ret_decay_fwd, m_ret_decay_bwd, m_ret_gn_w,
          m_w_branch_attn, m_w_branch_ret, m_w_out, m_final_norm_g]
    vs = [v_norm_g, v_w_in, v_attn_q_norm, v_attn_k_norm, v_ret_decay_fwd, v_ret_decay_bwd, v_ret_gn_w,
          v_w_branch_attn, v_w_branch_ret, v_w_out, v_final_norm_g]
    upd = [None if w is w_in else sharded[id(w)][1:] if id(w) in sharded else _adamw_nd(w, g, m, v)
           for w, g, m, v in zip(weights, grad_w, ms, vs)]

    done = [dx, w_in_l1[0], g_wout] + [u[0] for w, u in zip(weights, upd) if u is not None and id(w) not in sharded]
    g_full, recv[0] = _scatter_wait(*pending, done)
    mine = (4 * lax.axis_index("x") + 2 * lax.axis_index("y") + lax.axis_index("c")).astype(jnp.int32)[None]
    w_in_upd = [tr(o) for o in _sum_adamw([recv[0]], *w_in_t, 0, 256, layer0=0, prev=w_in_l1, own=(g_full, mine))]
    grad_w[1], upd[1] = w_in_upd[0], w_in_upd[1:]
    return (loss, dx[None], *grad_w, *[u[0] for u in upd], *[u[1] for u in upd], *[u[2] for u in upd])
```

```python
import functools

import jax
import jax.numpy as jnp
from jax import lax
from jax.experimental import pallas as pl
from jax.experimental.pallas import tpu as pltpu

F32 = jnp.float32
BF16 = jnp.bfloat16
SDS = jax.ShapeDtypeStruct

D_MODEL = 1024
DEPTH = 2
GRID_W = 64
ATTN_Q_HEADS = 8
ATTN_KV_HEADS = 2
ATTN_HEAD_DIM = 64
ATTN_WIDTH = 512
ATTN_KV_WIDTH = 128
RET_HEADS = 4
RET_HEAD_DIM = 128
RET_WIDTH = 512
RET_CHUNK = 128
ATTN_KEY_CHUNK = 512
ATTN_BWD_KEY_CHUNK = 1024
ATTN_BWD_QUERY_TILE = 1024
ATTN_FWD_QUERY_TILE = 512
VT_ROWS = ATTN_HEAD_DIM + 16
ATTN_FWD_KV_PER_STEP = 1
QK_DOTS_PER_CHUNK = 4
EXP_LAG = 3
ROPE_THETA = 10000.0
EPS = 1e-6
D_IN = 5376
N_DEV = 8

ADAM_LR = 0.001
ADAM_B1 = 0.9
ADAM_B2 = 0.999
ADAM_EPS = 1e-08
ADAM_WD = 0.01
ADAM_STEP = 10

SEG = {
    "qa": (0, 512, 0),
    "ga": (768, 512, 512),
    "qr": (1280, 512, 1024),
    "kr": (1792, 512, 1536),
    "vr": (2304, 512, 2048),
    "gr": (2816, 512, 2560),
    "gm": (3328, 2048, 3072),
    "ka": (512, 128, 5120),
    "va": (640, 128, 5248),
}

VMEM_LIMIT = 60 * 1024 * 1024
NT = (((1,), (1,)), ((), ()))
TN = (((0,), (0,)), ((), ()))
MESH_ID = pl.DeviceIdType.MESH
ANY = pl.BlockSpec(memory_space=pl.ANY)


def _params(sem=None, vmem=VMEM_LIMIT):
    return pltpu.CompilerParams(dimension_semantics=sem, vmem_limit_bytes=vmem)


def _dot(a, b, dims=None):
    if dims is None:
        return jnp.dot(a, b, preferred_element_type=F32)
    return lax.dot_general(a, b, dims, preferred_element_type=F32)


def _sigmoid(x):
    return 1.0 / (1.0 + jnp.exp(-x))


def _swap_halves(x, q):
    n = x.shape[-1]
    axis = x.ndim - 1
    lane = lax.broadcasted_iota(jnp.int32, x.shape, axis)
    first = (lane % (2 * q)) < q
    return jnp.where(first, pltpu.roll(x, n - q, axis), pltpu.roll(x, q, axis))


def _rope(x, cos, sin_signed, q):
    return x * cos + _swap_halves(x, q) * sin_signed


def _rope_bwd(dy, cos, sin_signed, q):
    return dy * cos - _swap_halves(dy, q) * sin_signed


def _group_mean(v, ones_bd):
    hi = v.astype(BF16)
    lo = (v - hi.astype(F32)).astype(BF16)
    return _dot(hi, ones_bd) + _dot(lo, ones_bd)


def _rope_tables(t, head_dim):
    n_rows = t // GRID_W
    d_axis = head_dim // 2
    inv_freq = ROPE_THETA ** (-jnp.arange(0, d_axis, 2, dtype=F32) / d_axis)
    ar = jnp.arange(n_rows, dtype=F32)[:, None] * inv_freq
    ac = jnp.arange(GRID_W, dtype=F32)[:, None] * inv_freq
    by_row = lambda a: jnp.repeat(a, GRID_W, axis=0)
    by_col = lambda a: jnp.tile(a, (n_rows, 1))
    cr, sr, cc, sc = by_row(jnp.cos(ar)), by_row(jnp.sin(ar)), by_col(jnp.cos(ac)), by_col(jnp.sin(ac))
    return jnp.concatenate([cr, cr, cc, cc], axis=-1), jnp.concatenate([-sr, sr, -sc, sc], axis=-1)


def _me():
    return lax.axis_index("x"), lax.axis_index("y"), lax.axis_index("c")


def _flip(k):
    x, y, c = _me()
    px = 1 - x if k & 4 else x
    py = 1 - y if k & 2 else y
    pc = 1 - c if k & 1 else c
    return (px, py, pc), 4 * px + 2 * py + pc


class _Exchange:
    def __init__(self, kind, srcs):
        self.kind, self.srcs, self.n = kind, list(srcs), len(srcs)
        self.rows = [a.shape[0] if kind == "gather" else a.shape[0] // N_DEV for a in srcs]
        if kind == "gather":
            self.out_shape = [SDS((N_DEV * a.shape[0], a.shape[1]), a.dtype) for a in srcs]
        else:
            self.out_shape = [SDS((N_DEV, a.shape[0] // N_DEV, a.shape[1]), a.dtype) for a in srcs]
        self.scratch = [pltpu.SemaphoreType.DMA((self.n, N_DEV - 1)), pltpu.SemaphoreType.DMA((self.n, N_DEV - 1)),
                        pltpu.SemaphoreType.DMA((self.n,))]

    def _block(self, ref, a, idx):
        r = self.rows[a]
        return ref.at[pl.ds(pl.multiple_of(idx * r, 16), r), :]

    def _src(self, ins, a, idx):
        return ins[a] if self.kind == "gather" else self._block(ins[a], a, idx)

    def _dst(self, outs, a, idx):
        return self._block(outs[a], a, idx) if self.kind == "gather" else outs[a].at[idx]

    def _copies(self, ins, outs, sems):
        send_sems, recv_sems, local_sems = sems
        me, mine = _flip(0)
        local, sends, recvs = [], [], []
        for a in range(self.n):
            local.append(pltpu.make_async_copy(self._src(ins, a, mine), self._dst(outs, a, mine), local_sems.at[a]))
            for k in range(1, N_DEV):
                peer, theirs = _flip(k)
                sem = dict(send_sem=send_sems.at[a, k - 1], recv_sem=recv_sems.at[a, k - 1])
                sends.append(pltpu.make_async_remote_copy(
                    src_ref=self._src(ins, a, theirs), dst_ref=self._dst(outs, a, mine),
                    device_id=peer, device_id_type=MESH_ID, **sem))
                recvs.append(pltpu.make_async_remote_copy(
                    src_ref=self._dst(outs, a, theirs), dst_ref=self._dst(outs, a, theirs),
                    device_id=me, device_id_type=MESH_ID, **sem))
        return local, sends, recvs

    def start(self, ins, outs, sems):
        local, sends, _ = self._copies(ins, outs, sems)
        for cp in local + sends:
            cp.start()

    def wait(self, ins, outs, sems):
        local, sends, recvs = self._copies(ins, outs, sems)
        for cp in sends:
            cp.wait_send()
        for cp in recvs:
            cp.wait_recv()
        for cp in local:
            cp.wait()


def _with_exchange(body, n_in, n_out, n_scratch, ex, first, last):
    if ex is None:
        return body

    def wrapped(*refs):
        ins = refs[:n_in]
        ex_ins = refs[n_in:n_in + ex.n]
        outs = refs[n_in + ex.n:n_in + ex.n + n_out]
        ex_outs = refs[n_in + ex.n + n_out:n_in + 2 * ex.n + n_out]
        rest = refs[n_in + 2 * ex.n + n_out:]
        scratch, sems = rest[:n_scratch], rest[n_scratch:]

        @pl.when(first())
        def _():
            ex.start(ex_ins, ex_outs, sems)

        body(*ins, *outs, *scratch)

        @pl.when(last())
        def _():
            ex.wait(ex_ins, ex_outs, sems)

    return wrapped


def _ex_args(ex):
    if ex is None:
        return [], [], [], [], []
    return [ANY] * ex.n, [ANY] * ex.n, list(ex.out_shape), list(ex.scratch), list(ex.srcs)


def _in_proj(x, g, w_t, qn, kn, cos, sin, ones_bd, cos_r, sin_r):
    t, d = x.shape
    tm = min(256, t)
    tk = min(ATTN_KEY_CHUNK, t)
    per_chunk = tk // tm
    hd = ATTN_HEAD_DIM

    def body(x_ref, g_ref, w_ref, qn_ref, kn_ref, c_ref, s_ref, b_ref, cr_ref, sr_ref,
             z_ref, ht_ref, q_out, qt_out, k_out, v_out, vt_out, qr_out, kr_out, vr_out):
        xv = x_ref[...]
        r = lax.rsqrt(jnp.mean(xv * xv, axis=-1, keepdims=True) + EPS)
        h = xv * r * g_ref[...]
        ht_ref[...] = h.T.astype(BF16)
        hb = h.astype(BF16)
        def project(name):
            nat, w, off = SEG[name]
            zs = _dot(hb, w_ref[nat:nat + w, :], NT)
            z_ref[:, off:off + w] = zs
            return zs

        seg = {name: project(name) for name in ("qa", "ka", "va")}
        bd = b_ref[...]
        c2, s2 = c_ref[...], s_ref[...]
        cq = jnp.concatenate([c2] * 4, axis=-1)
        sq = jnp.concatenate([s2] * 4, axis=-1)
        xq, xk, xvv = seg["qa"], seg["ka"], seg["va"]
        yq = xq * lax.rsqrt(_group_mean(xq * xq, bd) + EPS) * qn_ref[...]
        yq = _rope(yq, cq, sq, hd // 4) * (hd ** -0.5)
        yqt = yq.T
        for hh in range(ATTN_Q_HEADS):
            q_out[hh] = yq[:, hh * hd:(hh + 1) * hd].astype(BF16)
            qt_out[hh] = yqt[hh * hd:(hh + 1) * hd, :].astype(BF16)
        yk = xk * lax.rsqrt(_group_mean(xk * xk, bd[:ATTN_KV_WIDTH, :ATTN_KV_WIDTH]) + EPS) * kn_ref[...]
        yk = _rope(yk, c2, s2, hd // 4)
        xvt = xvv.T
        ones = jnp.ones((VT_ROWS - hd, tm), F32)
        for hh in range(ATTN_KV_HEADS):
            k_out[hh] = yk[:, hh * hd:(hh + 1) * hd].astype(BF16)
            v_out[hh] = xvv[:, hh * hd:(hh + 1) * hd].astype(BF16)
            vt_out[hh, 0] = jnp.concatenate([xvt[hh * hd:(hh + 1) * hd, :], ones], axis=0).astype(BF16)
        rd = RET_HEAD_DIM
        cr = jnp.concatenate([cr_ref[...]] * RET_HEADS, axis=-1)
        sr = jnp.concatenate([sr_ref[...]] * RET_HEADS, axis=-1)
        qr_out[...] = _rope(project("qr"), cr, sr, rd // 4).astype(BF16)
        kr_out[...] = (_rope(project("kr"), cr, sr, rd // 4) * (rd ** -0.5)).astype(BF16)
        vr_out[...] = project("vr").astype(BF16)
        for name in ("ga", "gr", "gm"):
            project(name)

    const = lambda shape: pl.BlockSpec(shape, lambda i: (0,) * len(shape))
    rows = lambda w: pl.BlockSpec((tm, w), lambda i: (i, 0))
    return pl.pallas_call(
        body, name="in_proj", grid=(t // tm,),
        in_specs=[rows(d), const((1, d)), const((D_IN, d)), const((1, 512)), const((1, 128)), rows(128), rows(128),
                  const((512, 512)), rows(128), rows(128)],
        out_specs=[rows(D_IN), pl.BlockSpec((d, tm), lambda i: (0, i)),
                   pl.BlockSpec((ATTN_Q_HEADS, tm, hd), lambda i: (0, i, 0)),
                   pl.BlockSpec((ATTN_Q_HEADS, hd, tm), lambda i: (0, 0, i)),
                   pl.BlockSpec((ATTN_KV_HEADS, tm, hd), lambda i: (0, i, 0)),
                   pl.BlockSpec((ATTN_KV_HEADS, tm, hd), lambda i: (0, i, 0)),
                   pl.BlockSpec((ATTN_KV_HEADS, 1, VT_ROWS, tm), lambda i: (0, i // per_chunk, 0, i % per_chunk)),
                   rows(RET_WIDTH), rows(RET_WIDTH), rows(RET_WIDTH)],
        out_shape=[SDS((t, D_IN), F32), SDS((d, t), BF16),
                   SDS((ATTN_Q_HEADS, t, hd), BF16), SDS((ATTN_Q_HEADS, hd, t), BF16),
                   SDS((ATTN_KV_HEADS, t, hd), BF16), SDS((ATTN_KV_HEADS, t, hd), BF16),
                   SDS((ATTN_KV_HEADS, t // tk, VT_ROWS, tk), BF16)] + [SDS((t, RET_WIDTH), BF16)] * 3,
        compiler_params=_params(("parallel",)),
    )(x, g, w_t, qn, kn, cos, sin, ones_bd, cos_r, sin_r)


def _attn_fwd(q, k, vt, ex=None):
    t = q.shape[1]
    tq = min(ATTN_FWD_QUERY_TILE, t)
    nk, tk = vt.shape[1], vt.shape[3]
    hd = ATTN_HEAD_DIM
    g = ATTN_Q_HEADS // ATTN_KV_HEADS
    kvs = ATTN_FWD_KV_PER_STEP
    heads = g * kvs

    def body(q_ref, k_ref, vt_ref, o_ref, lse_ref, s_scr):
        def pass_a(h, c, m8):
            part = tk // QK_DOTS_PER_CHUNK
            for lo in range(c * tk, (c + 1) * tk, part):
                st = _dot(k_ref[h // g, lo:lo + part, :], q_ref[h], NT)
                s_scr[h % 2, lo:lo + part, :] = st
                m8 = jnp.maximum(m8, jnp.max(st.reshape(part // 8, 8, tq), axis=0))
            return m8

        def pass_b(h, c, m, acc, after):
            e = jnp.exp(s_scr[h % 2, c * tk:(c + 1) * tk, :] - (m + after * 0.0)).astype(BF16)
            return acc + _dot(vt_ref[h // g, c], e)

        neg = jnp.full((8, tq), -jnp.inf, F32)
        m8 = neg
        for c in range(nk):
            m8 = pass_a(0, c, m8)
        outs = []
        for h in range(heads):
            m = jnp.max(m8, axis=0, keepdims=True)
            acc = jnp.zeros((VT_ROWS, tq), F32)
            m8 = neg
            done = [m] * EXP_LAG
            for c in range(nk):
                if h + 1 < heads:
                    m8 = pass_a(h + 1, c, m8)
                acc = pass_b(h, c, m, acc, done[-EXP_LAG])
                done.append(m8[0:1, :] if h + 1 < heads else acc[hd:hd + 1, :])
            l = acc[hd:hd + 1, :]
            outs.append((acc[:hd, :] / l).T)
            lse_ref[h] = m + jnp.log(l)
        o_ref[...] = jnp.concatenate(outs, axis=-1)

    nq = t // tq
    first = lambda: jnp.logical_and(pl.program_id(0) == 0, pl.program_id(1) == 0)
    last = lambda: jnp.logical_and(pl.program_id(0) == ATTN_KV_HEADS // kvs - 1, pl.program_id(1) == nq - 1)
    xi, xo, xs, xscr, xargs = _ex_args(ex)
    return pl.pallas_call(
        _with_exchange(body, 3, 2, 1, ex, first, last), name="attn_fwd", grid=(ATTN_KV_HEADS // kvs, nq),
        in_specs=[pl.BlockSpec((heads, tq, hd), lambda p, i: (p, i, 0)),
                  pl.BlockSpec((kvs, t, hd), lambda p, i: (p, 0, 0)),
                  pl.BlockSpec((kvs, nk, VT_ROWS, tk), lambda p, i: (p, 0, 0, 0))] + xi,
        out_specs=[pl.BlockSpec((tq, heads * hd), lambda p, i: (i, p)),
                   pl.BlockSpec((heads, 1, tq), lambda p, i: (p, 0, i))] + xo,
        out_shape=[SDS((t, ATTN_WIDTH), F32), SDS((ATTN_Q_HEADS, 1, t), F32)] + xs,
        scratch_shapes=[pltpu.VMEM((2, t, tq), F32)] + xscr,
        compiler_params=_params(("arbitrary", "arbitrary")),
    )(q, k, vt, *xargs)


class _Dir:
    def __init__(self, lg, strict_future):
        c = RET_CHUNK
        ia = lax.broadcasted_iota(jnp.int32, (c, c), 0).astype(F32)
        ib = lax.broadcasted_iota(jnp.int32, (c, c), 1).astype(F32)
        col = lax.broadcasted_iota(jnp.int32, (c, 1), 0).astype(F32)
        row = lax.broadcasted_iota(jnp.int32, (1, c), 1).astype(F32)
        if strict_future:
            dist = ib - ia
            mask = dist > 0
            self.wq, self.wk, wk_row = c - col, col, row
        else:
            dist = ia - ib
            mask = dist >= 0
            self.wq, self.wk, wk_row = col + 1.0, c - 1.0 - col, c - 1.0 - row
        self.dist = jnp.maximum(dist, 0.0)
        self.d = jnp.where(mask, jnp.exp(self.dist * lg), 0.0)
        self.qd = jnp.exp(self.wq * lg)
        self.kd_col = jnp.exp(self.wk * lg)
        self.kd_row = jnp.exp(wk_row * lg)
        self.cd = jnp.exp(jnp.full((1, 1), float(c), F32) * lg)


def _ret_fwd(qrot, krot, vb, lgf, lgb, gnw):
    t = qrot.shape[0]
    c = RET_CHUNK
    nc = t // c
    hd = RET_HEAD_DIM
    unroll = 4 if nc % 4 == 0 else 1

    def body(lgf_ref, lgb_ref, qo_ref, ko_ref, vo_ref, w_ref, orr_ref, on_ref, kt, uf, ub, sfa, sba):
        h = pl.program_id(0)
        fw = _Dir(lgf_ref[h], False)
        bw = _Dir(lgb_ref[h], True)
        for i in range(nc):
            kt[i] = ko_ref[i * c:(i + 1) * c, :].astype(F32).T.astype(BF16)

        def rows(ci):
            return pl.ds(pl.multiple_of(ci * c, c), c)

        def kv_products(ci, carry):
            vv = vo_ref[rows(ci), :]
            ktf = kt[ci].astype(F32)
            uf[ci] = _dot((ktf * fw.kd_row).astype(BF16), vv)
            ub[ci] = _dot((ktf * bw.kd_row).astype(BF16), vv)
            return carry

        lax.fori_loop(0, nc, kv_products, 0, unroll=unroll)

        def scan(i, carry):
            sf, sb = carry
            j = nc - 1 - i
            sfa[i] = sf.astype(BF16)
            sba[j] = sb.astype(BF16)
            return sf * fw.cd + uf[i], sb * bw.cd + ub[j]

        zero = jnp.zeros((hd, hd), F32)
        lax.fori_loop(0, nc, scan, (zero, zero))
        gw = w_ref[...]

        def outputs(ci, carry):
            sl = rows(ci)
            qq, kk, vv = qo_ref[sl, :], ko_ref[sl, :], vo_ref[sl, :]
            a = _dot(qq, kk, NT)
            o = (_dot((a * fw.d).astype(BF16), vv) + _dot(qq, sfa[ci]) * fw.qd
                 + _dot((a * bw.d).astype(BF16), vv) + _dot(qq, sba[ci]) * bw.qd)
            orr_ref[sl, :] = o
            xc = o - jnp.mean(o, axis=-1, keepdims=True)
            var = jnp.mean(xc * xc, axis=-1, keepdims=True)
            on_ref[sl, :] = xc * lax.rsqrt(var + EPS) * gw
            return carry

        group = 32 if nc % 32 == 0 else 1

        def output_group(i, carry):
            for j in range(group):
                outputs(i * group + j, carry)
            return carry

        lax.fori_loop(0, nc // group, output_group, 0)

    smem = pl.BlockSpec(memory_space=pltpu.SMEM)
    head = pl.BlockSpec((t, 128), lambda h: (0, h))
    return pl.pallas_call(
        body, name="ret_fwd", grid=(RET_HEADS,),
        in_specs=[smem, smem, head, head, head, pl.BlockSpec((1, 128), lambda h: (0, h))],
        out_specs=[head, head],
        out_shape=[SDS((t, RET_WIDTH), F32)] * 2,
        scratch_shapes=[pltpu.VMEM((nc, hd, c), BF16), pltpu.VMEM((nc, hd, hd), F32), pltpu.VMEM((nc, hd, hd), F32),
                        pltpu.VMEM((nc, hd, hd), BF16), pltpu.VMEM((nc, hd, hd), BF16)],
        compiler_params=_params(("parallel",)),
    )(lgf, lgb, qrot, krot, vb, gnw)


def _merge_fwd(x, z, oa, on, wb_t, wout, head=None):
    t, d = x.shape
    tm = min(256, t)
    n = t // tm

    def body(x_ref, ga_ref, gr_ref, gm0_ref, gm1_ref, oa_ref, on_ref, wb_ref, wo_ref, *rest):
        ga, gr = ga_ref[...], gr_ref[...]
        ua = ga * _sigmoid(ga) * oa_ref[...]
        ub = gr * _sigmoid(gr) * on_ref[...]
        ya = _dot(ua.astype(BF16), wb_ref[:, :512], NT)
        yb = _dot(ub.astype(BF16), wb_ref[:, 512:], NT)
        merged = _sigmoid(gm0_ref[...]) * ya + _sigmoid(gm1_ref[...]) * yb
        xn = x_ref[...] + _dot(merged.astype(BF16), wo_ref[...])
        if head is None:
            xn_ref, ya_ref, yb_ref = rest
            xn_ref[...] = xn
        else:
            g_ref, t_ref, dx_ref, ya_ref, yb_ref, dg_ref, loss_ref, acc_g, acc_l = rest
            i = pl.program_id(0)

            @pl.when(i == 0)
            def _():
                acc_g[...] = jnp.zeros_like(acc_g)
                acc_l[...] = jnp.zeros_like(acc_l)

            gv = g_ref[...]
            r = lax.rsqrt(jnp.mean(xn * xn, axis=-1, keepdims=True) + EPS)
            xh = xn * r
            err = xh * gv - t_ref[...]
            dy = err * (1.0 / d)
            gy = dy * gv
            dx_ref[...] = r * (gy - xh * jnp.mean(gy * xh, axis=-1, keepdims=True))
            acc_g[...] += jnp.sum((dy * xh).reshape(tm // 8, 8, d), axis=0)
            acc_l[...] += jnp.sum((err * err).reshape(tm // 8, 8, d), axis=0)

            @pl.when(i == n - 1)
            def _():
                dg_ref[...] = jnp.sum(acc_g[...], axis=0, keepdims=True)
                tot = jnp.sum(jnp.sum(acc_l[...], axis=0, keepdims=True), axis=1, keepdims=True)
                loss_ref[...] = jnp.broadcast_to(tot * (0.5 / d), (1, 128))
        ya_ref[...] = ya.astype(BF16)
        yb_ref[...] = yb.astype(BF16)

    row = lambda w, j: pl.BlockSpec((tm, w), lambda i: (i, j))
    const = lambda shape: pl.BlockSpec(shape, lambda i: (0, 0))
    in_specs = [row(d, 0), row(512, SEG["ga"][2] // 512), row(512, SEG["gr"][2] // 512),
                row(1024, SEG["gm"][2] // 1024), row(1024, SEG["gm"][2] // 1024 + 1),
                row(512, 0), row(512, 0), const((d, 1024)), const((d, d))]
    out_specs = [row(d, 0), row(d, 0), row(d, 0)]
    out_shape = [SDS((t, d), F32), SDS((t, d), BF16), SDS((t, d), BF16)]
    args, scratch = [x, z, z, z, z, oa, on, wb_t, wout], []
    if head is not None:
        in_specs += [const((1, d)), row(d, 0)]
        out_specs += [const((1, d)), const((1, 128))]
        out_shape += [SDS((1, d), F32), SDS((1, 128), F32)]
        args += list(head)
        scratch = [pltpu.VMEM((8, d), F32), pltpu.VMEM((8, d), F32)]
    return pl.pallas_call(
        body, name="merge_fwd", grid=(n,), in_specs=in_specs, out_specs=out_specs, out_shape=out_shape,
        scratch_shapes=scratch,
        compiler_params=_params(("arbitrary",) if head is not None else ("parallel",)),
    )(*args)


def _merge_bwd(dxo, z, oa, on, ya, yb, wb_t, wout):
    t, d = dxo.shape
    tm = min(256, t)
    n = t // tm

    def body(dx_ref, ga_ref, gr_ref, gm0_ref, gm1_ref, oa_ref, on_ref, ya_ref, yb_ref, wb_ref, wo_ref,
             doa_ref, don_ref, dz_ref, dwo_ref, dwb_ref, acc_o, acc_b):
        i = pl.program_id(0)

        @pl.when(i == 0)
        def _():
            acc_o[...] = jnp.zeros_like(acc_o)
            acc_b[...] = jnp.zeros_like(acc_b)

        dxb = dx_ref[...].astype(BF16)
        ya, yb = ya_ref[...].astype(F32), yb_ref[...].astype(F32)
        g0, g1 = _sigmoid(gm0_ref[...]), _sigmoid(gm1_ref[...])
        mb = (g0 * ya + g1 * yb).astype(BF16)
        dm = _dot(dxb, wo_ref[...], NT)
        dya = (dm * g0).astype(BF16)
        dyb = (dm * g1).astype(BF16)
        dz_ref[:, 1024:2048] = (dm * ya * g0 * (1.0 - g0)).astype(BF16)
        dz_ref[:, 2048:3072] = (dm * yb * g1 * (1.0 - g1)).astype(BF16)

        def branch(g_ref, o_ref, dy, w, do_ref, lo):
            gv, ov = g_ref[...], o_ref[...]
            sg = _sigmoid(gv)
            silu = gv * sg
            du = _dot(dy, w)
            do_ref[...] = du * silu
            dz_ref[:, lo:lo + 512] = (du * ov * (sg * (1.0 + gv * (1.0 - sg)))).astype(BF16)
            acc_b[:, lo:lo + 512] += _dot(dy, (silu * ov).astype(BF16), TN)

        branch(ga_ref, oa_ref, dya, wb_ref[:, :512], doa_ref, 0)
        branch(gr_ref, on_ref, dyb, wb_ref[:, 512:], don_ref, 512)
        acc_o[...] += _dot(mb, dxb, TN)

        @pl.when(i == n - 1)
        def _():
            dwo_ref[...] = acc_o[...].astype(BF16)
            dwb_ref[...] = acc_b[...].astype(BF16)

    row = lambda w, j: pl.BlockSpec((tm, w), lambda i: (i, j))
    const = lambda shape: pl.BlockSpec(shape, lambda i: (0, 0))
    return pl.pallas_call(
        body, name="merge_bwd", grid=(n,),
        in_specs=[row(d, 0), row(512, SEG["ga"][2] // 512), row(512, SEG["gr"][2] // 512),
                  row(1024, SEG["gm"][2] // 1024), row(1024, SEG["gm"][2] // 1024 + 1),
                  row(512, 0), row(512, 0), row(d, 0), row(d, 0), const((d, 1024)), const((d, d))],
        out_specs=[row(512, 0), row(512, 0), row(3072, 0), const((d, d)), const((d, 1024))],
        out_shape=[SDS((t, 512), F32), SDS((t, 512), F32), SDS((t, 3072), BF16), SDS((d, d), BF16),
                   SDS((d, 1024), BF16)],
        scratch_shapes=[pltpu.VMEM((d, d), F32), pltpu.VMEM((d, 1024), F32)],
        compiler_params=_params(("arbitrary",)),
    )(dxo, z, z, z, z, oa, on, ya, yb, wb_t, wout)


def _ret_bwd(qrot, krot, vb, orr, don, gnw, lgf, lgb, cos, sin):
    t = qrot.shape[0]
    c = RET_CHUNK
    nc = t // c
    hd = RET_HEAD_DIM
    unroll = 4 if nc % 4 == 0 else 1

    def body(lgf_ref, lgb_ref, q_ref, k_ref, v_ref, o_ref, dn_ref, w_ref, c_ref, s_ref,
             dq_ref, dk_ref, dv_ref, dw_ref, dlf_ref, dlb_ref, qt, kt, dob, uf, ub, wf, wb, sfa, sba, gfa, gba):
        h = pl.program_id(0)
        fw = _Dir(lgf_ref[h], False)
        bw = _Dir(lgb_ref[h], True)
        fw.dt, bw.dt = fw.d.T, bw.d.T

        o = o_ref[...]
        xc = o - jnp.mean(o, axis=-1, keepdims=True)
        r = lax.rsqrt(jnp.mean(xc * xc, axis=-1, keepdims=True) + EPS)
        xh = xc * r
        dn = dn_ref[...]
        gy = dn * w_ref[...]
        d_o = r * (gy - jnp.mean(gy, axis=-1, keepdims=True) - xh * jnp.mean(gy * xh, axis=-1, keepdims=True))
        dw_ref[...] = jnp.sum(dn * xh, axis=0, keepdims=True)
        dob[...] = d_o.astype(BF16)
        for i in range(nc):
            qt[i] = q_ref[i * c:(i + 1) * c, :].astype(F32).T.astype(BF16)
            kt[i] = k_ref[i * c:(i + 1) * c, :].astype(F32).T.astype(BF16)

        def rows(ci):
            return pl.ds(pl.multiple_of(ci * c, c), c)

        def products(ci, carry):
            sl = rows(ci)
            vv, do32 = v_ref[sl, :], dob[sl, :].astype(F32)
            ktf = kt[ci].astype(F32)
            uf[ci] = _dot((ktf * fw.kd_row).astype(BF16), vv)
            ub[ci] = _dot((ktf * bw.kd_row).astype(BF16), vv)
            wf[ci] = _dot(qt[ci], (do32 * fw.qd).astype(BF16))
            wb[ci] = _dot(qt[ci], (do32 * bw.qd).astype(BF16))
            return carry

        lax.fori_loop(0, nc, products, 0, unroll=unroll)

        def scan(i, carry):
            sf, sb, gf, gb = carry
            j = nc - 1 - i
            sfa[i] = sf.astype(BF16)
            sba[j] = sb.astype(BF16)
            gfa[j] = gf.astype(BF16)
            gba[i] = gb.astype(BF16)
            return sf * fw.cd + uf[i], sb * bw.cd + ub[j], gf * fw.cd + wf[j], gb * bw.cd + wb[i]

        zero = jnp.zeros((hd, hd), F32)
        lax.fori_loop(0, nc, scan, (zero, zero, zero, zero))

        def one_dir(p, s_all, g_all, ci, qq, kk, vv, do, a, bm):
            sb, gb = s_all[ci], g_all[ci]
            doq = (do.astype(F32) * p.qd).astype(BF16)
            dqc = _dot(doq, sb, NT)
            kkd = (kk.astype(F32) * p.kd_col).astype(BF16)
            dk2 = _dot(vv, gb, NT) * p.kd_col
            terms = (p.dist * p.d * a * bm + p.wq * qq.astype(F32) * dqc + p.wk * kk.astype(F32) * dk2
                     + (float(c) * p.cd) * gb.astype(F32) * sb.astype(F32))
            return dqc, dk2, _dot(kkd, gb), terms

        d_both, dt_both = fw.d + bw.d, fw.dt + bw.dt

        def chunk(ci, carry):
            af, ab = carry
            sl = rows(ci)
            qq, kk, vv, do = q_ref[sl, :], k_ref[sl, :], v_ref[sl, :], dob[sl, :]
            a, bm = _dot(qq, kk, NT), _dot(do, vv, NT)
            at, bt = _dot(kk, qq, NT), _dot(vv, do, NT)
            dqf, dkf, dvf, tf = one_dir(fw, sfa, gfa, ci, qq, kk, vv, do, a, bm)
            dqb, dkb, dvb, tb = one_dir(bw, sba, gba, ci, qq, kk, vv, do, a, bm)
            cc, ss = c_ref[sl, :], s_ref[sl, :]
            dq = _dot((bm * d_both).astype(BF16), kk) + dqf + dqb
            dk = _dot((bt * dt_both).astype(BF16), qq) + dkf + dkb
            dq_ref[sl, :] = _rope_bwd(dq, cc, ss, hd // 4).astype(BF16)
            dk_ref[sl, :] = (_rope_bwd(dk, cc, ss, hd // 4) * (hd ** -0.5)).astype(BF16)
            dv_ref[sl, :] = (_dot((at * dt_both).astype(BF16), do) + dvf + dvb).astype(BF16)
            return af + tf, ab + tb

        pair = 8 if nc % 8 == 0 else 1

        def chunks(i, carry):
            for j in range(pair):
                carry = chunk(i * pair + j, carry)
            return carry

        af, ab = lax.fori_loop(0, nc // pair, chunks, (zero, zero))
        tot = lambda m: jnp.sum(jnp.sum(m, axis=0, keepdims=True), axis=1, keepdims=True)
        dlf_ref[...] = jnp.broadcast_to(tot(af).reshape(1, 1, 1), (1, 8, 128))
        dlb_ref[...] = jnp.broadcast_to(tot(ab).reshape(1, 1, 1), (1, 8, 128))

    smem = pl.BlockSpec(memory_space=pltpu.SMEM)
    head = pl.BlockSpec((t, 128), lambda h: (0, h))
    vec = pl.BlockSpec((1, 128), lambda h: (0, h))
    scal = pl.BlockSpec((1, 8, 128), lambda h: (h, 0, 0))
    table = pl.BlockSpec((t, 128), lambda h: (0, 0))
    mats = lambda dt: pltpu.VMEM((nc, hd, hd), dt)
    return pl.pallas_call(
        body, name="ret_bwd", grid=(RET_HEADS,),
        in_specs=[smem, smem, head, head, head, head, head, vec, table, table],
        out_specs=[head, head, head, vec, scal, scal],
        out_shape=[SDS((t, RET_WIDTH), BF16)] * 3 + [SDS((1, RET_WIDTH), F32), SDS((RET_HEADS, 8, 128), F32),
                                                    SDS((RET_HEADS, 8, 128), F32)],
        scratch_shapes=[pltpu.VMEM((nc, hd, c), BF16), pltpu.VMEM((nc, hd, c), BF16), pltpu.VMEM((t, hd), BF16),
                        mats(F32), mats(F32), mats(F32), mats(F32), mats(BF16), mats(BF16), mats(BF16), mats(BF16)],
        compiler_params=_params(("parallel",)),
    )(lgf, lgb, qrot, krot, vb, orr, don, gnw, cos, sin)


def _attn_bwd(q, qt, k, v, doa, oa, lse, ex=None):
    t = q.shape[1]
    tq = min(ATTN_BWD_QUERY_TILE, t)
    nq = t // tq
    tk = min(ATTN_BWD_KEY_CHUNK, t)
    nk = t // tk
    hd = ATTN_HEAD_DIM
    scale = hd ** -0.5

    def body(q_ref, qt_ref, k_ref, v_ref, do_ref, o_ref, lse_ref, dq_ref, dkt_ref, dvt_ref):
        p, i = pl.program_id(0), pl.program_id(1)

        @pl.when(jnp.logical_and(p % 2 == 0, i == 0))
        def _():
            dkt_ref[...] = jnp.zeros_like(dkt_ref)
            dvt_ref[...] = jnp.zeros_like(dvt_ref)

        dov, ov = do_ref[...], o_ref[...]
        dovt = dov.T
        lanes = lambda col: jnp.concatenate([col] * (tk // 128), axis=1)
        outs = []
        for j in range(2):
            qq, qqt = q_ref[j], qt_ref[j]
            do32 = dov[:, j * hd:(j + 1) * hd]
            do, dot_ = do32.astype(BF16), dovt[j * hd:(j + 1) * hd, :].astype(BF16)
            dd = lanes(jnp.broadcast_to(jnp.sum(do32 * ov[:, j * hd:(j + 1) * hd], axis=1, keepdims=True), (tq, 128)))
            lse_j = lanes(jnp.broadcast_to(lse_ref[j], (128, tq)).T)
            dq = jnp.zeros((tq, hd), F32)
            for c in range(nk):
                sl = slice(c * tk, (c + 1) * tk)
                kc, vc = k_ref[0, sl, :], v_ref[0, sl, :]
                pr = jnp.exp(_dot(qq, kc, NT) - lse_j)
                ds = (pr * (_dot(do, vc, NT) - dd)).astype(BF16)
                dvt_ref[0, :, sl] += _dot(dot_, pr.astype(BF16))
                dkt_ref[0, :, sl] += _dot(qqt, ds)
                dq = dq + _dot(ds, kc)
            outs.append(dq * scale)
        dq_ref[...] = jnp.concatenate(outs, axis=-1)

    kv = pl.BlockSpec((1, t, hd), lambda p, i: (p // 2, 0, 0))
    kvt = pl.BlockSpec((1, hd, t), lambda p, i: (p // 2, 0, 0))
    pair = pl.BlockSpec((tq, 128), lambda p, i: (i, p))
    first = lambda: jnp.logical_and(pl.program_id(0) == 0, pl.program_id(1) == 0)
    last = lambda: jnp.logical_and(pl.program_id(0) == 3, pl.program_id(1) == nq - 1)
    xi, xo, xs, xscr, xargs = _ex_args(ex)
    return pl.pallas_call(
        _with_exchange(body, 7, 3, 0, ex, first, last), name="attn_bwd", grid=(4, nq),
        in_specs=[pl.BlockSpec((2, tq, hd), lambda p, i: (p, i, 0)), pl.BlockSpec((2, hd, tq), lambda p, i: (p, 0, i)),
                  kv, kv, pair, pair, pl.BlockSpec((2, 1, tq), lambda p, i: (p, 0, i))] + xi,
        out_specs=[pair, kvt, kvt] + xo,
        out_shape=[SDS((t, ATTN_WIDTH), F32), SDS((ATTN_KV_HEADS, hd, t), F32),
                   SDS((ATTN_KV_HEADS, hd, t), F32)] + xs,
        scratch_shapes=xscr,
        compiler_params=_params(("arbitrary", "arbitrary")),
    )(q, qt, k, v, doa, oa, lse, *xargs)


def _attn_post_bwd(dq, dk, dv, z, qn, kn, cos, sin, ones_bd):
    t = z.shape[0]
    tm = min(512, t)
    n = t // tm
    hd = ATTN_HEAD_DIM

    def body(dq_ref, dk_ref, dv_ref, zq_ref, zkv_ref, qn_ref, kn_ref, c_ref, s_ref, b_ref,
             dz_ref, dqn_ref, dkn_ref, acc_q, acc_k):
        i = pl.program_id(0)

        @pl.when(i == 0)
        def _():
            acc_q[...] = jnp.zeros_like(acc_q)
            acc_k[...] = jnp.zeros_like(acc_k)

        bd = b_ref[...]
        c2, s2 = c_ref[...], s_ref[...]

        def norm_bwd(dy, x, w, ones, cos_t, sin_t, acc):
            dyr = _rope_bwd(dy, cos_t, sin_t, hd // 4)
            r = lax.rsqrt(_group_mean(x * x, ones) + EPS)
            xh = x * r
            gy = dyr * w
            acc[...] += jnp.sum((dyr * xh).reshape(tm // 8, 8, x.shape[-1]), axis=0)
            return r * (gy - xh * _group_mean(gy * xh, ones))

        cq = jnp.concatenate([c2] * 4, axis=-1)
        sq = jnp.concatenate([s2] * 4, axis=-1)
        dz_ref[:, :512] = norm_bwd(dq_ref[...], zq_ref[...], qn_ref[...], bd, cq, sq, acc_q).astype(BF16)
        zkv = zkv_ref[...]
        dkk = jnp.concatenate([dk_ref[0], dk_ref[1]], axis=0).T
        dz_ref[:, 512:640] = norm_bwd(dkk, zkv[:, :128], kn_ref[...], bd[:128, :128], c2, s2, acc_k).astype(BF16)
        dz_ref[:, 640:768] = jnp.concatenate([dv_ref[0], dv_ref[1]], axis=0).T.astype(BF16)

        @pl.when(i == n - 1)
        def _():
            dqn_ref[...] = jnp.sum(acc_q[...], axis=0, keepdims=True)
            dkn_ref[...] = jnp.sum(acc_k[...], axis=0, keepdims=True)

    kv_blk = SEG["ka"][2] // 256
    kvs = pl.BlockSpec((ATTN_KV_HEADS, hd, tm), lambda i: (0, 0, i))
    const = lambda shape: pl.BlockSpec(shape, lambda i: (0, 0))
    return pl.pallas_call(
        body, name="attn_post_bwd", grid=(n,),
        in_specs=[pl.BlockSpec((tm, 512), lambda i: (i, 0)), kvs, kvs,
                  pl.BlockSpec((tm, 512), lambda i: (i, 0)), pl.BlockSpec((tm, 256), lambda i: (i, kv_blk)),
                  const((1, 512)), const((1, 128)),
                  pl.BlockSpec((tm, 128), lambda i: (i, 0)), pl.BlockSpec((tm, 128), lambda i: (i, 0)),
                  const((512, 512))],
        out_specs=[pl.BlockSpec((tm, 768), lambda i: (i, 0)), const((1, 512)), const((1, 128))],
        out_shape=[SDS((t, 768), BF16), SDS((1, 512), F32), SDS((1, 128), F32)],
        scratch_shapes=[pltpu.VMEM((8, 512), F32), pltpu.VMEM((8, 128), F32)],
        compiler_params=_params(("arbitrary",)),
    )(dq, dk, dv, z, z, qn, kn, cos, sin, ones_bd)


def _in_bwd(dxo, x, g, w_t, dz_a, dz_m, dqr, dkr, dvr, after=None):
    t, d = x.shape
    tm = min(512, t)
    n = t // tm
    parts = [(0, 0, 768, 0), (1, 0, 512, SEG["ga"][0]), (2, 0, 512, SEG["qr"][0]), (3, 0, 512, SEG["kr"][0]),
             (4, 0, 512, SEG["vr"][0]), (1, 512, 2560, SEG["gr"][0])]

    def body(dx_ref, x_ref, g_ref, w_ref, a_ref, m_ref, q_ref, k_ref, v_ref, o_ref, dg_ref, acc):
        i = pl.program_id(0)

        @pl.when(i == 0)
        def _():
            acc[...] = jnp.zeros_like(acc)

        pieces = [a_ref, m_ref, q_ref, k_ref, v_ref]
        dh = jnp.zeros((tm, d), F32)
        for pi, lo, w, row in parts:
            dh = dh + _dot(pieces[pi][:, lo:lo + w], w_ref[row:row + w, :])
        xv = x_ref[...]
        r = lax.rsqrt(jnp.mean(xv * xv, axis=-1, keepdims=True) + EPS)
        xh = xv * r
        gy = dh * g_ref[...]
        o_ref[...] = dx_ref[...] + r * (gy - xh * jnp.mean(gy * xh, axis=-1, keepdims=True))
        acc[...] += jnp.sum((dh * xh).reshape(tm // 8, 8, d), axis=0)

        @pl.when(i == n - 1)
        def _():
            dg_ref[...] = jnp.sum(acc[...], axis=0, keepdims=True)

    row = lambda w: pl.BlockSpec((tm, w), lambda i: (i, 0))
    const = lambda shape: pl.BlockSpec(shape, lambda i: (0, 0))
    extra = [] if after is None else [after]
    return pl.pallas_call(
        (lambda *refs: body(*refs[:9], *refs[9 + len(extra):])), name="in_bwd", grid=(n,),
        in_specs=[row(d), row(d), const((1, d)), const((D_IN, d)), row(768), row(3072), row(512), row(512),
                  row(512)] + [const(a.shape) for a in extra],
        out_specs=[row(d), const((1, d))],
        out_shape=[SDS((t, d), F32), SDS((1, d), F32)],
        scratch_shapes=[pltpu.VMEM((8, d), F32)],
        compiler_params=_params(("arbitrary",)),
    )(dxo, x, g, w_t, dz_a, dz_m, dqr, dkr, dvr, *extra)


def _dw_in(h_t, dz_a, dz_m, dqr, dkr, dvr):
    d, t = h_t.shape
    tn = 256
    parts = [(0, 0, 0, 3), (1, 0, SEG["ga"][0] // tn, 2), (2, 0, SEG["qr"][0] // tn, 2),
             (3, 0, SEG["kr"][0] // tn, 2), (4, 0, SEG["vr"][0] // tn, 2), (1, 2, SEG["gr"][0] // tn, 10)]
    pieces = [dz_a, dz_m, dqr, dkr, dvr]

    def col_block(pi):
        mine = [(c0, r0, n) for q, c0, r0, n in parts if q == pi]

        def index(j):
            c0, r0, n = mine[0]
            blk = c0 + jnp.clip(j - r0, 0, n - 1)
            for c0, r0, n in mine[1:]:
                blk = jnp.where(j >= r0, c0 + jnp.clip(j - r0, 0, n - 1), blk)
            return 0, blk

        return index

    def body(h_ref, *refs):
        o_ref = refs[-1]
        j = pl.program_id(0)
        for pi, _, r0, n in parts:
            @pl.when(jnp.logical_and(j >= r0, j < r0 + n))
            def _(p_ref=refs[pi]):
                o_ref[...] = _dot(h_ref[...], p_ref[...]).T.astype(BF16)

    return pl.pallas_call(
        body, name="dw_in", grid=(D_IN // tn,),
        in_specs=[pl.BlockSpec((d, t), lambda j: (0, 0))] + [pl.BlockSpec((t, tn), col_block(pi)) for pi in range(5)],
        out_specs=pl.BlockSpec((tn, d), lambda j: (j, 0)),
        out_shape=SDS((D_IN, d), BF16),
        compiler_params=_params(("arbitrary",)),
    )(h_t, *pieces)


def _adamw_math(w, g, m, v):
    mn = ADAM_B1 * m + (1.0 - ADAM_B1) * g
    vn = ADAM_B2 * v + (1.0 - ADAM_B2) * (g * g)
    m_hat = mn / (1.0 - ADAM_B1 ** ADAM_STEP)
    v_hat = vn / (1.0 - ADAM_B2 ** ADAM_STEP)
    return -ADAM_LR * (m_hat / (jnp.sqrt(v_hat) + ADAM_EPS) + ADAM_WD * w), mn, vn


def _sum_adamw(recvs, w, m, v, lane0, tn, layer0=0, prev=None, own=None):
    _, r, c = w.shape
    j0 = lane0 // tn
    n = len(recvs)
    has_own = own is not None

    def body(*refs):
        mine_ref, refs = (refs[0], refs[1:]) if has_own else (None, refs)
        w_ref, m_ref, v_ref = refs[n:n + 3]
        g_ref, d_ref, mo_ref, vo_ref = refs[-4:]

        def run(r_ref):
            def slot(s):
                if has_own:
                    return jnp.where(mine_ref[0] == s, refs[n + 3][...], r_ref[s]).astype(F32)
                return r_ref[s].astype(F32)

            g = slot(0)
            for s in range(1, N_DEV):
                g = g + slot(s)
            g_ref[0] = g
            d_ref[0], mo_ref[0], vo_ref[0] = _adamw_math(w_ref[0], g, m_ref[0], v_ref[0])

        for i in range(n):
            pl.when(pl.program_id(0) == i)(functools.partial(run, refs[i]))

    slots = pl.BlockSpec((N_DEV, r, tn), lambda i, j, *_: (0, 0, j0 + j))
    blk = pl.BlockSpec((1, r, tn), lambda i, j, *_: (layer0 + i, 0, j))
    before = [] if prev is None else list(prev)
    in_specs, args = [slots] * n + [blk] * 3, [*recvs, w, m, v]
    if has_own:
        assert n == 1
        in_specs.append(pl.BlockSpec((r, tn), lambda i, j, mine: (mine[0], j0 + j)))
        args.append(own[0])
    n_pre = len(args) + has_own
    return pl.pallas_call(
        body, name="sum_adamw",
        grid_spec=pltpu.PrefetchScalarGridSpec(
            num_scalar_prefetch=int(has_own), grid=(n, c // tn),
            in_specs=in_specs + [ANY] * len(before), out_specs=[blk] * 4),
        out_shape=[SDS(w.shape, F32)] * 4,
        input_output_aliases={n_pre + k: k for k in range(len(before))},
        compiler_params=_params(("parallel", "parallel")),
    )(*([own[1]] if has_own else []), *args, *before)


def _adamw(w, g, m, v):
    rows, cols = w.shape
    tr = 256 if rows % 256 == 0 else rows

    def body(w_ref, g_ref, m_ref, v_ref, d_ref, mo_ref, vo_ref):
        d_ref[...], mo_ref[...], vo_ref[...] = _adamw_math(w_ref[...], g_ref[...], m_ref[...], v_ref[...])

    blk = pl.BlockSpec((tr, cols), lambda i: (i, 0))
    return pl.pallas_call(
        body, name="adamw", grid=(rows // tr,),
        in_specs=[blk] * 4, out_specs=[blk] * 3, out_shape=[SDS((rows, cols), F32)] * 3,
        compiler_params=_params(("parallel",)),
    )(w, g, m, v)


def _all_gather(shards):
    na = len(shards)
    chips = (4, 2, 6)

    def body(*refs):
        ins, outs = refs[:na], refs[na:2 * na]
        send_sems, recv_sems, local_sems = refs[2 * na:]
        _, mine = _flip(0)

        def rows(a, idx):
            r = shards[a].shape[0]
            return outs[a].at[pl.ds(pl.multiple_of(idx * r, 16), r), :]

        def copy(a, slot, block_idx, to, src=None):
            return pltpu.make_async_remote_copy(
                src_ref=rows(a, block_idx) if src is None else src, dst_ref=rows(a, block_idx),
                send_sem=send_sems.at[a, slot], recv_sem=recv_sems.at[a, slot],
                device_id=to, device_id_type=MESH_ID)

        sibling, sibling_idx = _flip(1)
        local, started = [], []
        for a in range(na):
            cp = pltpu.make_async_copy(ins[a], rows(a, mine), local_sems.at[a])
            cp.start()
            local.append(cp)
            first = [copy(a, 0, mine, sibling, src=ins[a])]
            first += [copy(a, 1 + j, mine, _flip(k)[0], src=ins[a]) for j, k in enumerate(chips)]
            for cp in first:
                cp.start()
            started += first
        for a in range(na):
            for j, k in enumerate(chips):
                _, theirs = _flip(k)
                copy(a, 1 + j, theirs, _flip(0)[0]).wait_recv()
                fwd = copy(a, 4 + j, theirs, sibling)
                fwd.start()
                started.append(fwd)
        for a in range(na):
            copy(a, 0, sibling_idx, _flip(0)[0]).wait_recv()
            for j, k in enumerate(chips):
                _, theirs = _flip(k | 1)
                copy(a, 4 + j, theirs, _flip(0)[0]).wait_recv()
        for cp in started:
            cp.wait_send()
        for cp in local:
            cp.wait()

    return pl.pallas_call(
        body, name="all_gather_weights",
        in_specs=[ANY] * na, out_specs=[ANY] * na,
        out_shape=[SDS((N_DEV * s.shape[0], s.shape[1]), s.dtype) for s in shards],
        scratch_shapes=[pltpu.SemaphoreType.DMA((na, 7)), pltpu.SemaphoreType.DMA((na, 7)),
                        pltpu.SemaphoreType.DMA((na,))],
        compiler_params=pltpu.CompilerParams(has_side_effects=True),
    )(*shards)


def _scatter_blocks_of(g_ref, rows, idx):
    return g_ref.at[pl.ds(pl.multiple_of(idx * rows, 16), rows), :]


def _scatter_start(g):
    rows = g.shape[0] // N_DEV
    land_shape = (N_DEV, rows, g.shape[1])

    def body(g_ref, land_ref, send_sems, recv_sems, g_thru, land_thru, token):
        _, mine = _flip(0)
        for k in range(1, N_DEV):
            peer, theirs = _flip(k)
            pltpu.make_async_remote_copy(
                src_ref=_scatter_blocks_of(g_ref, rows, theirs), dst_ref=land_ref.at[mine],
                send_sem=send_sems.at[k - 1], recv_sem=recv_sems.at[k - 1],
                device_id=peer, device_id_type=MESH_ID).start()
        token[...] = jnp.zeros_like(token)

    hbm, sem = pl.BlockSpec(memory_space=pltpu.HBM), pl.BlockSpec(memory_space=pltpu.SEMAPHORE)
    return pl.pallas_call(
        body, name="scatter_start",
        out_shape=(pltpu.SemaphoreType.DMA((N_DEV - 1,)), pltpu.SemaphoreType.DMA((N_DEV - 1,)),
                   pltpu.HBM(g.shape, g.dtype), pltpu.HBM(land_shape, g.dtype), SDS((8, 128), F32)),
        in_specs=(hbm, hbm), out_specs=(sem, sem, hbm, hbm, pl.BlockSpec(memory_space=pltpu.VMEM)),
        input_output_aliases={0: 2, 1: 3},
        compiler_params=pltpu.CompilerParams(has_side_effects=pltpu.SideEffectType.DATAFLOW_SIDE_EFFECTING),
    )(pltpu.with_memory_space_constraint(g, pltpu.HBM),
      pltpu.with_memory_space_constraint(lax.empty(land_shape, g.dtype), pltpu.HBM))


def _scatter_wait(send_sems, recv_sems, g_thru, land_thru, after):
    rows = g_thru.shape[0] // N_DEV

    def body(g_ref, land_ref, send_sems, recv_sems, *rest):
        me, _ = _flip(0)
        for k in range(1, N_DEV):
            _, theirs = _flip(k)
            copy = pltpu.make_async_remote_copy(
                src_ref=_scatter_blocks_of(g_ref, rows, theirs), dst_ref=land_ref.at[theirs],
                send_sem=send_sems.at[k - 1], recv_sem=recv_sems.at[k - 1],
                device_id=me, device_id_type=MESH_ID)
            copy.wait_send()
            copy.wait_recv()

    hbm, sem = pl.BlockSpec(memory_space=pltpu.HBM), pl.BlockSpec(memory_space=pltpu.SEMAPHORE)
    return pl.pallas_call(
        body, name="scatter_wait",
        out_shape=(pltpu.HBM(g_thru.shape, g_thru.dtype), pltpu.HBM(land_thru.shape, land_thru.dtype)),
        in_specs=(hbm, hbm, sem, sem) + (ANY,) * len(after), out_specs=(hbm, hbm), input_output_aliases={0: 0, 1: 1},
        compiler_params=pltpu.CompilerParams(has_side_effects=pltpu.SideEffectType.DATAFLOW_SIDE_EFFECTING),
    )(g_thru, land_thru, send_sems, recv_sems, *after)


def _all_reduce_small(packed):
    shape = packed.shape

    def body(p_ref, o_ref, slots, send_sems, recv_sems):
        me, mine = _flip(0)
        slots[mine] = p_ref[...]
        sends = []
        for k in range(1, N_DEV):
            peer, _ = _flip(k)
            cp = pltpu.make_async_remote_copy(
                src_ref=p_ref, dst_ref=slots.at[mine], send_sem=send_sems.at[k - 1], recv_sem=recv_sems.at[k - 1],
                device_id=peer, device_id_type=MESH_ID)
            cp.start()
            sends.append(cp)
        for k in range(1, N_DEV):
            _, theirs = _flip(k)
            pltpu.make_async_remote_copy(
                src_ref=p_ref, dst_ref=slots.at[theirs], send_sem=send_sems.at[k - 1],
                recv_sem=recv_sems.at[k - 1], device_id=me, device_id_type=MESH_ID).wait_recv()
        for cp in sends:
            cp.wait_send()
        acc = slots[0]
        for s in range(1, N_DEV):
            acc = acc + slots[s]
        o_ref[...] = acc

    vm = pl.BlockSpec(memory_space=pltpu.VMEM)
    return pl.pallas_call(
        body, name="all_reduce_small", in_specs=[vm], out_specs=vm, out_shape=SDS(shape, F32),
        scratch_shapes=[pltpu.VMEM((N_DEV,) + shape, F32), pltpu.SemaphoreType.DMA((7,)),
                        pltpu.SemaphoreType.DMA((7,))],
        compiler_params=pltpu.CompilerParams(has_side_effects=True),
    )(packed)


def _layer_fwd(x, p, tabs, ex):
    z, h_t, q, qt, k, v, vt, qrot, krot, vb = _in_proj(x, p["norm_g"], p["w_in_t"], p["qn"], p["kn"], tabs["ca"],
                                                       tabs["sa"], tabs["ones"], tabs["cr"], tabs["sr"])
    oa, lse, *gathered = _attn_fwd(q, k, vt, ex)
    orr, on = _ret_fwd(qrot, krot, vb, p["lgf"], p["lgb"], p["gnw"])
    return z, h_t, q, qt, k, v, lse, oa, qrot, krot, vb, orr, on, gathered


def _layer_bwd(dxo, s, p, tabs, ex_attn, scatter_w_in):
    doa, don, dz_m, d_wout, d_wb_t = _merge_bwd(dxo, s["z"], s["oa"], s["on"], s["ya"], s["yb"], p["wb_t"], p["w_out"])
    dq_a, dk_a, dv_a, *recv_attn = _attn_bwd(s["q"], s["qt"], s["k"], s["v"], doa, s["oa"], s["lse"],
                                              ex_attn(d_wb_t, d_wout))
    dz_a, d_qn, d_kn = _attn_post_bwd(dq_a, dk_a, dv_a, s["z"], p["qn"], p["kn"], tabs["ca"], tabs["sa"],
                                      tabs["ones"])
    dqr, dkr, dvr, d_gnw, d_lgf, d_lgb = _ret_bwd(s["qrot"], s["krot"], s["vb"], s["orr"], don, p["gnw"],
                                                  p["lgf"], p["lgb"], tabs["cr"], tabs["sr"])
    buf = _dw_in(s["h_t"], dz_a, dz_m, dqr, dkr, dvr)
    pending, token = None, None
    if scatter_w_in:
        *pending, token = _scatter_start(buf)
    dx, d_norm_g = _in_bwd(dxo, s["x"], p["norm_g"], p["w_in_t"], dz_a, dz_m, dqr, dkr, dvr, token)
    grads = dict(w_in_t=buf, wb_t=d_wb_t, w_out=d_wout, norm_g=d_norm_g, gnw=d_gnw,
                 qn=d_qn.reshape(ATTN_Q_HEADS, ATTN_HEAD_DIM).sum(axis=0),
                 kn=d_kn.reshape(ATTN_KV_HEADS, ATTN_HEAD_DIM).sum(axis=0),
                 lgf=d_lgf[:, 0, 0], lgb=d_lgb[:, 0, 0])
    return dx, grads, recv_attn, pending


def _adamw_nd(w, g, m, v):
    shape = w.shape
    two_d = (1, shape[0]) if w.ndim == 1 else (-1, shape[-1])
    out = _adamw(w.reshape(two_d), g.reshape(two_d), m.reshape(two_d), v.reshape(two_d))
    return tuple(o.reshape(shape) for o in out)


def kernel(x, norm_g, w_in, attn_q_norm, attn_k_norm, ret_decay_fwd, ret_decay_bwd, ret_gn_w, w_branch_attn, w_branch_ret, w_out, final_norm_g, loss_target, m_norm_g, m_w_in, m_attn_q_norm, m_attn_k_norm, m_ret_decay_fwd, m_ret_decay_bwd, m_ret_gn_w, m_w_branch_attn, m_w_branch_ret, m_w_out, m_final_norm_g, v_norm_g, v_w_in, v_attn_q_norm, v_attn_k_norm, v_ret_decay_fwd, v_ret_decay_bwd, v_ret_gn_w, v_w_branch_attn, v_w_branch_ret, v_w_out, v_final_norm_g):
    t, d = x.shape[1], x.shape[2]
    x2, target = x[0], loss_target[0]

    w_in_sh, wb_sh, wout_sh = [], [], []
    for l in range(DEPTH):
        w_in_sh.append(jnp.swapaxes(w_in[l], 0, 1).astype(BF16))
        wb_sh.append(jnp.concatenate([w_branch_attn[l].T, w_branch_ret[l].T], axis=1).astype(BF16))
        wout_sh.append(w_out[l].astype(BF16))

    ca, sa = _rope_tables(t, ATTN_HEAD_DIM)
    cr, sr = _rope_tables(t, RET_HEAD_DIM)
    grp = jnp.arange(ATTN_WIDTH) // ATTN_HEAD_DIM
    tabs = dict(ca=jnp.tile(ca, (1, 2)), sa=jnp.tile(sa, (1, 2)), cr=cr, sr=sr,
                ones=jnp.where(grp[:, None] == grp[None, :], 1.0 / ATTN_HEAD_DIM, 0.0).astype(BF16))
    layers = []
    for l in range(DEPTH):
        layers.append(dict(
            norm_g=norm_g[l][None], qn=jnp.tile(attn_q_norm[l], ATTN_Q_HEADS)[None],
            kn=jnp.tile(attn_k_norm[l], ATTN_KV_HEADS)[None], gnw=ret_gn_w[l][None],
            lgf=jax.nn.log_sigmoid(ret_decay_fwd[l]), lgb=jax.nn.log_sigmoid(ret_decay_bwd[l])))

    layers[0]["w_in_t"], = _all_gather([w_in_sh[0]])
    gathers = [_Exchange("gather", [wb_sh[0], wout_sh[0], w_in_sh[1]]), _Exchange("gather", [wb_sh[1], wout_sh[1]])]
    h = x2
    saved = []
    for l in range(DEPTH):
        p = layers[l]
        z, h_t, q, qt, k, v, lse, oa, qrot, krot, vb, orr, on, got = _layer_fwd(h, p, tabs, gathers[l])
        p["wb_t"], p["w_out"] = got[0], got[1]
        if l == 0:
            layers[1]["w_in_t"] = got[2]
        last = (final_norm_g[None], target) if l == DEPTH - 1 else None
        xn, ya, yb, *loss_head = _merge_fwd(h, z, oa, on, p["wb_t"], p["w_out"], last)
        saved.append(dict(x=h, z=z, h_t=h_t, q=q, qt=qt, k=k, v=v, lse=lse, oa=oa, qrot=qrot, krot=krot, vb=vb,
                          orr=orr, on=on, ya=ya, yb=yb))
        h = xn
    dx, (d_final_g, loss_part) = h, loss_head

    grads = [None] * DEPTH
    dx, grads[1], _, _ = _layer_bwd(dx, saved[1], layers[1], tabs, lambda *a: None, False)
    g1 = grads[1]
    ex_attn = lambda d_wb_t, d_wout: _Exchange("scatter", [g1["w_in_t"], g1["wb_t"], g1["w_out"], d_wb_t, d_wout])
    dx, grads[0], recv_attn, pending = _layer_bwd(dx, saved[0], layers[0], tabs, ex_attn, True)
    recv = [None, recv_attn[3], recv_attn[4], recv_attn[0], recv_attn[1], recv_attn[2]]
    tr = lambda a: jnp.swapaxes(a, 1, 2)
    w_in_t = (tr(w_in), tr(m_w_in), tr(v_w_in))
    sharded = {}
    w_in_l1 = _sum_adamw([recv[3]], *w_in_t, 0, 256, layer0=1)
    sharded[id(w_branch_attn)] = [tr(o) for o in _sum_adamw(
        [recv[1], recv[4]], tr(w_branch_attn), tr(m_w_branch_attn), tr(v_w_branch_attn), 0, 512)]
    sharded[id(w_branch_ret)] = [tr(o) for o in _sum_adamw(
        [recv[1], recv[4]], tr(w_branch_ret), tr(m_w_branch_ret), tr(v_w_branch_ret), 512, 512)]
    sharded[id(w_out)] = _sum_adamw([recv[2], recv[5]], w_out, m_w_out, v_w_out, 0, 256)
    g_wba, g_wbr, g_wout = (sharded[id(w)][0] for w in (w_branch_attn, w_branch_ret, w_out))

    packed = jnp.zeros((8, 1024), F32)
    for l in range(DEPTH):
        gl = grads[l]
        packed = packed.at[l].set(gl["norm_g"][0])
        packed = packed.at[2, 512 * l:512 * (l + 1)].set(gl["gnw"][0])
        packed = packed.at[4, 128 * l:128 * l + 64].set(gl["qn"])
        packed = packed.at[4, 256 + 128 * l:256 + 128 * l + 64].set(gl["kn"])
        packed = packed.at[4, 512 + 128 * l:512 + 128 * l + 4].set(gl["lgf"])
        packed = packed.at[4, 768 + 128 * l:768 + 128 * l + 4].set(gl["lgb"])
    packed = packed.at[3].set(d_final_g[0])
    packed = packed.at[5, 0].set(loss_part[0, 0])
    red = _all_reduce_small(packed)
    loss = red[5, 0]
    g_norm_g = red[0:2]
    g_gnw = red[2].reshape(DEPTH, RET_WIDTH)
    g_final = red[3]
    g_qn = jnp.stack([red[4, 128 * l:128 * l + 64] for l in range(DEPTH)])
    g_kn = jnp.stack([red[4, 256 + 128 * l:256 + 128 * l + 64] for l in range(DEPTH)])
    g_lgf = jnp.stack([red[4, 512 + 128 * l:512 + 128 * l + 4] for l in range(DEPTH)])
    g_lgb = jnp.stack([red[4, 768 + 128 * l:768 + 128 * l + 4] for l in range(DEPTH)])
    g_df = g_lgf * jax.nn.sigmoid(-ret_decay_fwd)
    g_db = g_lgb * jax.nn.sigmoid(-ret_decay_bwd)

    grad_w = [g_norm_g, None, g_qn, g_kn, g_df, g_db, g_gnw, g_wba, g_wbr, g_wout, g_final]
    weights = [norm_g, w_in, attn_q_norm, attn_k_norm, ret_decay_fwd, ret_decay_bwd, ret_gn_w, w_branch_attn,
               w_branch_ret, w_out, final_norm_g]
    ms = [m_norm_g, m_w_in, m_attn_q_norm, m_attn_k_norm, m_ret_decay_fwd, m_ret_decay_bwd, m_ret_gn_w,
          m_w_branch_attn, m_w_branch_ret, m_w_out, m_final_norm_g]
    vs = [v_norm_g, v_w_in, v_attn_q_norm, v_attn_k_norm, v_ret_decay_fwd, v_ret_decay_bwd, v_ret_gn_w,
          v_w_branch_attn, v_w_branch_ret, v_w_out, v_final_norm_g]
    upd = [None if w is w_in else sharded[id(w)][1:] if id(w) in sharded else _adamw_nd(w, g, m, v)
           for w, g, m, v in zip(weights, grad_w, ms, vs)]

    done = [dx, w_in_l1[0], g_wout] + [u[0] for w, u in zip(weights, upd) if u is not None and id(w) not in sharded]
    g_full, recv[0] = _scatter_wait(*pending, done)
    mine = (4 * lax.axis_index("x") + 2 * lax.axis_index("y") + lax.axis_index("c")).astype(jnp.int32)[None]
    w_in_upd = [tr(o) for o in _sum_adamw([recv[0]], *w_in_t, 0, 256, layer0=0, prev=w_in_l1, own=(g_full, mine))]
    grad_w[1], upd[1] = w_in_upd[0], w_in_upd[1:]
    return (loss, dx[None], *grad_w, *[u[0] for u in upd], *[u[1] for u in upd], *[u[2] for u in upd])
```

```python
import functools

import jax
import jax.numpy as jnp
from jax import lax
from jax.experimental import pallas as pl
from jax.experimental.pallas import tpu as pltpu

F32 = jnp.float32
BF16 = jnp.bfloat16
SDS = jax.ShapeDtypeStruct

D_MODEL = 1024
DEPTH = 2
GRID_W = 64
ATTN_Q_HEADS = 8
ATTN_KV_HEADS = 2
ATTN_HEAD_DIM = 64
ATTN_WIDTH = 512
ATTN_KV_WIDTH = 128
RET_HEADS = 4
RET_HEAD_DIM = 128
RET_WIDTH = 512
RET_CHUNK = 128
ATTN_KEY_CHUNK = 512
ATTN_BWD_KEY_CHUNK = 512
ATTN_BWD_QUERY_TILE = 1024
ATTN_FWD_QUERY_TILE = 512
QK_DOTS_PER_CHUNK = 4
EXP_LAG = 3
ROPE_THETA = 10000.0
EPS = 1e-6
D_IN = 5376
N_DEV = 8

ADAM_LR = 0.001
ADAM_B1 = 0.9
ADAM_B2 = 0.999
ADAM_EPS = 1e-08
ADAM_WD = 0.01
ADAM_STEP = 10

SEG = {
    "qa": (0, 512, 0),
    "ga": (768, 512, 512),
    "qr": (1280, 512, 1024),
    "kr": (1792, 512, 1536),
    "vr": (2304, 512, 2048),
    "gr": (2816, 512, 2560),
    "gm": (3328, 2048, 3072),
    "ka": (512, 128, 5120),
    "va": (640, 128, 5248),
}

VMEM_LIMIT = 60 * 1024 * 1024
NT = (((1,), (1,)), ((), ()))
TN = (((0,), (0,)), ((), ()))
MESH_ID = pl.DeviceIdType.MESH
ANY = pl.BlockSpec(memory_space=pl.ANY)


def _params(sem=None, vmem=VMEM_LIMIT):
    return pltpu.CompilerParams(dimension_semantics=sem, vmem_limit_bytes=vmem)


def _dot(a, b, dims=None):
    if dims is None:
        return jnp.dot(a, b, preferred_element_type=F32)
    return lax.dot_general(a, b, dims, preferred_element_type=F32)


def _sigmoid(x):
    return 1.0 / (1.0 + jnp.exp(-x))


def _swap_halves(x, q):
    n = x.shape[-1]
    axis = x.ndim - 1
    lane = lax.broadcasted_iota(jnp.int32, x.shape, axis)
    first = (lane % (2 * q)) < q
    return jnp.where(first, pltpu.roll(x, n - q, axis), pltpu.roll(x, q, axis))


def _rope(x, cos, sin_signed, q):
    return x * cos + _swap_halves(x, q) * sin_signed


def _rope_bwd(dy, cos, sin_signed, q):
    return dy * cos - _swap_halves(dy, q) * sin_signed


def _group_mean(v, ones_bd):
    hi = v.astype(BF16)
    lo = (v - hi.astype(F32)).astype(BF16)
    return _dot(hi, ones_bd) + _dot(lo, ones_bd)


def _rope_tables(t, head_dim):
    n_rows = t // GRID_W
    d_axis = head_dim // 2
    inv_freq = ROPE_THETA ** (-jnp.arange(0, d_axis, 2, dtype=F32) / d_axis)
    ar = jnp.arange(n_rows, dtype=F32)[:, None] * inv_freq
    ac = jnp.arange(GRID_W, dtype=F32)[:, None] * inv_freq
    by_row = lambda a: jnp.repeat(a, GRID_W, axis=0)
    by_col = lambda a: jnp.tile(a, (n_rows, 1))
    cr, sr, cc, sc = by_row(jnp.cos(ar)), by_row(jnp.sin(ar)), by_col(jnp.cos(ac)), by_col(jnp.sin(ac))
    return jnp.concatenate([cr, cr, cc, cc], axis=-1), jnp.concatenate([-sr, sr, -sc, sc], axis=-1)


def _me():
    return lax.axis_index("x"), lax.axis_index("y"), lax.axis_index("c")


def _flip(k):
    x, y, c = _me()
    px = 1 - x if k & 4 else x
    py = 1 - y if k & 2 else y
    pc = 1 - c if k & 1 else c
    return (px, py, pc), 4 * px + 2 * py + pc


class _Exchange:
    def __init__(self, kind, srcs):
        self.kind, self.srcs, self.n = kind, list(srcs), len(srcs)
        self.rows = [a.shape[0] if kind == "gather" else a.shape[0] // N_DEV for a in srcs]
        if kind == "gather":
            self.out_shape = [SDS((N_DEV * a.shape[0], a.shape[1]), a.dtype) for a in srcs]
        else:
            self.out_shape = [SDS((N_DEV, a.shape[0] // N_DEV, a.shape[1]), a.dtype) for a in srcs]
        self.scratch = [pltpu.SemaphoreType.DMA((self.n, N_DEV - 1)), pltpu.SemaphoreType.DMA((self.n, N_DEV - 1)),
                        pltpu.SemaphoreType.DMA((self.n,))]

    def _block(self, ref, a, idx):
        r = self.rows[a]
        return ref.at[pl.ds(pl.multiple_of(idx * r, 16), r), :]

    def _src(self, ins, a, idx):
        return ins[a] if self.kind == "gather" else self._block(ins[a], a, idx)

    def _dst(self, outs, a, idx):
        return self._block(outs[a], a, idx) if self.kind == "gather" else outs[a].at[idx]

    def _copies(self, ins, outs, sems):
        send_sems, recv_sems, local_sems = sems
        me, mine = _flip(0)
        local, sends, recvs = [], [], []
        for a in range(self.n):
            local.append(pltpu.make_async_copy(self._src(ins, a, mine), self._dst(outs, a, mine), local_sems.at[a]))
            for k in range(1, N_DEV):
                peer, theirs = _flip(k)
                sem = dict(send_sem=send_sems.at[a, k - 1], recv_sem=recv_sems.at[a, k - 1])
                sends.append(pltpu.make_async_remote_copy(
                    src_ref=self._src(ins, a, theirs), dst_ref=self._dst(outs, a, mine),
                    device_id=peer, device_id_type=MESH_ID, **sem))
                recvs.append(pltpu.make_async_remote_copy(
                    src_ref=self._dst(outs, a, theirs), dst_ref=self._dst(outs, a, theirs),
                    device_id=me, device_id_type=MESH_ID, **sem))
        return local, sends, recvs

    def start(self, ins, outs, sems):
        local, sends, _ = self._copies(ins, outs, sems)
        for cp in local + sends:
            cp.start()

    def wait(self, ins, outs, sems):
        local, sends, recvs = self._copies(ins, outs, sems)
        for cp in sends:
            cp.wait_send()
        for cp in recvs:
            cp.wait_recv()
        for cp in local:
            cp.wait()


def _with_exchange(body, n_in, n_out, n_scratch, ex, first, last):
    if ex is None:
        return body

    def wrapped(*refs):
        ins = refs[:n_in]
        ex_ins = refs[n_in:n_in + ex.n]
        outs = refs[n_in + ex.n:n_in + ex.n + n_out]
        ex_outs = refs[n_in + ex.n + n_out:n_in + 2 * ex.n + n_out]
        rest = refs[n_in + 2 * ex.n + n_out:]
        scratch, sems = rest[:n_scratch], rest[n_scratch:]

        @pl.when(first())
        def _():
            ex.start(ex_ins, ex_outs, sems)

        body(*ins, *outs, *scratch)

        @pl.when(last())
        def _():
            ex.wait(ex_ins, ex_outs, sems)

    return wrapped


def _ex_args(ex):
    if ex is None:
        return [], [], [], [], []
    return [ANY] * ex.n, [ANY] * ex.n, list(ex.out_shape), list(ex.scratch), list(ex.srcs)


def _in_proj(x, g, w_t, qn, kn, cos, sin, ones_bd, cos_r, sin_r):
    t, d = x.shape
    tm = min(256, t)
    tk = min(ATTN_KEY_CHUNK, t)
    per_chunk = tk // tm
    hd = ATTN_HEAD_DIM

    def body(x_ref, g_ref, w_ref, qn_ref, kn_ref, c_ref, s_ref, b_ref, cr_ref, sr_ref,
             z_ref, ht_ref, q_out, qt_out, k_out, v_out, vt_out, qr_out, kr_out, vr_out):
        xv = x_ref[...]
        r = lax.rsqrt(jnp.mean(xv * xv, axis=-1, keepdims=True) + EPS)
        h = xv * r * g_ref[...]
        ht_ref[...] = h.T.astype(BF16)
        hb = h.astype(BF16)
        def project(name):
            nat, w, off = SEG[name]
            zs = _dot(hb, w_ref[nat:nat + w, :], NT)
            z_ref[:, off:off + w] = zs
            return zs

        seg = {name: project(name) for name in ("qa", "ka", "va")}
        bd = b_ref[...]
        c2, s2 = c_ref[...], s_ref[...]
        cq = jnp.concatenate([c2] * 4, axis=-1)
        sq = jnp.concatenate([s2] * 4, axis=-1)
        xq, xk, xvv = seg["qa"], seg["ka"], seg["va"]
        yq = xq * lax.rsqrt(_group_mean(xq * xq, bd) + EPS) * qn_ref[...]
        yq = _rope(yq, cq, sq, hd // 4) * (hd ** -0.5)
        yqt = yq.T
        for hh in range(ATTN_Q_HEADS):
            q_out[hh] = yq[:, hh * hd:(hh + 1) * hd].astype(BF16)
            qt_out[hh] = yqt[hh * hd:(hh + 1) * hd, :].astype(BF16)
        yk = xk * lax.rsqrt(_group_mean(xk * xk, bd[:ATTN_KV_WIDTH, :ATTN_KV_WIDTH]) + EPS) * kn_ref[...]
        yk = _rope(yk, c2, s2, hd // 4)
        xvt = xvv.T
        ones = jnp.ones((hd, tm), F32)
        for hh in range(ATTN_KV_HEADS):
            k_out[hh] = yk[:, hh * hd:(hh + 1) * hd].astype(BF16)
            v_out[hh] = xvv[:, hh * hd:(hh + 1) * hd].astype(BF16)
            vt_out[hh, 0] = jnp.concatenate([xvt[hh * hd:(hh + 1) * hd, :], ones], axis=0).astype(BF16)
        rd = RET_HEAD_DIM
        cr = jnp.concatenate([cr_ref[...]] * RET_HEADS, axis=-1)
        sr = jnp.concatenate([sr_ref[...]] * RET_HEADS, axis=-1)
        qr_out[...] = _rope(project("qr"), cr, sr, rd // 4).astype(BF16)
        kr_out[...] = (_rope(project("kr"), cr, sr, rd // 4) * (rd ** -0.5)).astype(BF16)
        vr_out[...] = project("vr").astype(BF16)
        for name in ("ga", "gr", "gm"):
            project(name)

    const = lambda shape: pl.BlockSpec(shape, lambda i: (0,) * len(shape))
    rows = lambda w: pl.BlockSpec((tm, w), lambda i: (i, 0))
    return pl.pallas_call(
        body, name="in_proj", grid=(t // tm,),
        in_specs=[rows(d), const((1, d)), const((D_IN, d)), const((1, 512)), const((1, 128)), rows(128), rows(128),
                  const((512, 512)), rows(128), rows(128)],
        out_specs=[rows(D_IN), pl.BlockSpec((d, tm), lambda i: (0, i)),
                   pl.BlockSpec((ATTN_Q_HEADS, tm, hd), lambda i: (0, i, 0)),
                   pl.BlockSpec((ATTN_Q_HEADS, hd, tm), lambda i: (0, 0, i)),
                   pl.BlockSpec((ATTN_KV_HEADS, tm, hd), lambda i: (0, i, 0)),
                   pl.BlockSpec((ATTN_KV_HEADS, tm, hd), lambda i: (0, i, 0)),
                   pl.BlockSpec((ATTN_KV_HEADS, 1, 2 * hd, tm), lambda i: (0, i // per_chunk, 0, i % per_chunk)),
                   rows(RET_WIDTH), rows(RET_WIDTH), rows(RET_WIDTH)],
        out_shape=[SDS((t, D_IN), F32), SDS((d, t), BF16),
                   SDS((ATTN_Q_HEADS, t, hd), BF16), SDS((ATTN_Q_HEADS, hd, t), BF16),
                   SDS((ATTN_KV_HEADS, t, hd), BF16), SDS((ATTN_KV_HEADS, t, hd), BF16),
                   SDS((ATTN_KV_HEADS, t // tk, 2 * hd, tk), BF16)] + [SDS((t, RET_WIDTH), BF16)] * 3,
        compiler_params=_params(("parallel",)),
    )(x, g, w_t, qn, kn, cos, sin, ones_bd, cos_r, sin_r)


def _attn_fwd(q, k, vt, ex=None):
    t = q.shape[1]
    tq = min(ATTN_FWD_QUERY_TILE, t)
    nk, tk = vt.shape[1], vt.shape[3]
    hd = ATTN_HEAD_DIM
    g = ATTN_Q_HEADS // ATTN_KV_HEADS

    def body(q_ref, k_ref, vt_ref, o_ref, lse_ref, s_scr):
        def pass_a(h, c, m8):
            part = tk // QK_DOTS_PER_CHUNK
            for lo in range(c * tk, (c + 1) * tk, part):
                st = _dot(k_ref[0, lo:lo + part, :], q_ref[h], NT)
                s_scr[h % 2, lo:lo + part, :] = st
                m8 = jnp.maximum(m8, jnp.max(st.reshape(part // 8, 8, tq), axis=0))
            return m8

        def pass_b(h, c, m, acc, after):
            e = jnp.exp(s_scr[h % 2, c * tk:(c + 1) * tk, :] - (m + after * 0.0)).astype(BF16)
            return acc + _dot(vt_ref[0, c], e)

        neg = jnp.full((8, tq), -jnp.inf, F32)
        m8 = neg
        for c in range(nk):
            m8 = pass_a(0, c, m8)
        outs = []
        for h in range(g):
            m = jnp.max(m8, axis=0, keepdims=True)
            acc = jnp.zeros((2 * hd, tq), F32)
            m8 = neg
            done = [m] * EXP_LAG
            for c in range(nk):
                if h + 1 < g:
                    m8 = pass_a(h + 1, c, m8)
                acc = pass_b(h, c, m, acc, done[-EXP_LAG])
                done.append(m8[0:1, :] if h + 1 < g else acc[hd:hd + 1, :])
            l = acc[hd:hd + 1, :]
            outs.append((acc[:hd, :] / l).T)
            lse_ref[h] = m + jnp.log(l)
        o_ref[...] = jnp.concatenate(outs, axis=-1)

    nq = t // tq
    first = lambda: jnp.logical_and(pl.program_id(0) == 0, pl.program_id(1) == 0)
    last = lambda: jnp.logical_and(pl.program_id(0) == ATTN_KV_HEADS - 1, pl.program_id(1) == nq - 1)
    xi, xo, xs, xscr, xargs = _ex_args(ex)
    return pl.pallas_call(
        _with_exchange(body, 3, 2, 1, ex, first, last), name="attn_fwd", grid=(ATTN_KV_HEADS, nq),
        in_specs=[pl.BlockSpec((g, tq, hd), lambda p, i: (p, i, 0)),
                  pl.BlockSpec((1, t, hd), lambda p, i: (p, 0, 0)),
                  pl.BlockSpec((1, nk, 2 * hd, tk), lambda p, i: (p, 0, 0, 0))] + xi,
        out_specs=[pl.BlockSpec((tq, g * hd), lambda p, i: (i, p)),
                   pl.BlockSpec((g, 1, tq), lambda p, i: (p, 0, i))] + xo,
        out_shape=[SDS((t, ATTN_WIDTH), F32), SDS((ATTN_Q_HEADS, 1, t), F32)] + xs,
        scratch_shapes=[pltpu.VMEM((2, t, tq), F32)] + xscr,
        compiler_params=_params(("arbitrary", "arbitrary")),
    )(q, k, vt, *xargs)


class _Dir:
    def __init__(self, lg, strict_future):
        c = RET_CHUNK
        ia = lax.broadcasted_iota(jnp.int32, (c, c), 0).astype(F32)
        ib = lax.broadcasted_iota(jnp.int32, (c, c), 1).astype(F32)
        col = lax.broadcasted_iota(jnp.int32, (c, 1), 0).astype(F32)
        row = lax.broadcasted_iota(jnp.int32, (1, c), 1).astype(F32)
        if strict_future:
            dist = ib - ia
            mask = dist > 0
            self.wq, self.wk, wk_row = c - col, col, row
        else:
            dist = ia - ib
            mask = dist >= 0
            self.wq, self.wk, wk_row = col + 1.0, c - 1.0 - col, c - 1.0 - row
        self.dist = jnp.maximum(dist, 0.0)
        self.d = jnp.where(mask, jnp.exp(self.dist * lg), 0.0)
        self.qd = jnp.exp(self.wq * lg)
        self.kd_col = jnp.exp(self.wk * lg)
        self.kd_row = jnp.exp(wk_row * lg)
        self.cd = jnp.exp(jnp.full((1, 1), float(c), F32) * lg)


def _ret_fwd(qrot, krot, vb, lgf, lgb, gnw):
    t = qrot.shape[0]
    c = RET_CHUNK
    nc = t // c
    hd = RET_HEAD_DIM
    unroll = 4 if nc % 4 == 0 else 1

    def body(lgf_ref, lgb_ref, qo_ref, ko_ref, vo_ref, w_ref, orr_ref, on_ref, kt, uf, ub, sfa, sba):
        h = pl.program_id(0)
        fw = _Dir(lgf_ref[h], False)
        bw = _Dir(lgb_ref[h], True)
        for i in range(nc):
            kt[i] = ko_ref[i * c:(i + 1) * c, :].astype(F32).T.astype(BF16)

        def rows(ci):
            return pl.ds(pl.multiple_of(ci * c, c), c)

        def kv_products(ci, carry):
            vv = vo_ref[rows(ci), :]
            ktf = kt[ci].astype(F32)
            uf[ci] = _dot((ktf * fw.kd_row).astype(BF16), vv)
            ub[ci] = _dot((ktf * bw.kd_row).astype(BF16), vv)
            return carry

        lax.fori_loop(0, nc, kv_products, 0, unroll=16 if nc % 16 == 0 else unroll)

        def scan(i, carry):
            sf, sb = carry
            j = nc - 1 - i
            sfa[i] = sf.astype(BF16)
            sba[j] = sb.astype(BF16)
            return sf * fw.cd + uf[i], sb * bw.cd + ub[j]

        zero = jnp.zeros((hd, hd), F32)
        lax.fori_loop(0, nc, scan, (zero, zero))
        gw = w_ref[...]

        def outputs(ci, carry):
            sl = rows(ci)
            qq, kk, vv = qo_ref[sl, :], ko_ref[sl, :], vo_ref[sl, :]
            a = _dot(qq, kk, NT)
            o = (_dot((a * fw.d).astype(BF16), vv) + _dot(qq, sfa[ci]) * fw.qd
                 + _dot((a * bw.d).astype(BF16), vv) + _dot(qq, sba[ci]) * bw.qd)
            orr_ref[sl, :] = o
            xc = o - jnp.mean(o, axis=-1, keepdims=True)
            var = jnp.mean(xc * xc, axis=-1, keepdims=True)
            on_ref[sl, :] = xc * lax.rsqrt(var + EPS) * gw
            return carry

        group = 32 if nc % 32 == 0 else 1

        def output_group(i, carry):
            for j in range(group):
                outputs(i * group + j, carry)
            return carry

        lax.fori_loop(0, nc // group, output_group, 0)

    smem = pl.BlockSpec(memory_space=pltpu.SMEM)
    head = pl.BlockSpec((t, 128), lambda h: (0, h))
    return pl.pallas_call(
        body, name="ret_fwd", grid=(RET_HEADS,),
        in_specs=[smem, smem, head, head, head, pl.BlockSpec((1, 128), lambda h: (0, h))],
        out_specs=[head, head],
        out_shape=[SDS((t, RET_WIDTH), F32)] * 2,
        scratch_shapes=[pltpu.VMEM((nc, hd, c), BF16), pltpu.VMEM((nc, hd, hd), F32), pltpu.VMEM((nc, hd, hd), F32),
                        pltpu.VMEM((nc, hd, hd), BF16), pltpu.VMEM((nc, hd, hd), BF16)],
        compiler_params=_params(("parallel",)),
    )(lgf, lgb, qrot, krot, vb, gnw)


def _merge_fwd(x, z, oa, on, wb_t, wout, head=None):
    t, d = x.shape
    tm = min(256, t)
    n = t // tm

    def body(x_ref, ga_ref, gr_ref, gm0_ref, gm1_ref, oa_ref, on_ref, wb_ref, wo_ref, *rest):
        ga, gr = ga_ref[...], gr_ref[...]
        ua = ga * _sigmoid(ga) * oa_ref[...]
        ub = gr * _sigmoid(gr) * on_ref[...]
        ya = _dot(ua.astype(BF16), wb_ref[:, :512], NT)
        yb = _dot(ub.astype(BF16), wb_ref[:, 512:], NT)
        merged = _sigmoid(gm0_ref[...]) * ya + _sigmoid(gm1_ref[...]) * yb
        xn = x_ref[...] + _dot(merged.astype(BF16), wo_ref[...])
        if head is None:
            xn_ref, ya_ref, yb_ref = rest
            xn_ref[...] = xn
        else:
            g_ref, t_ref, dx_ref, ya_ref, yb_ref, dg_ref, loss_ref, acc_g, acc_l = rest
            i = pl.program_id(0)

            @pl.when(i == 0)
            def _():
                acc_g[...] = jnp.zeros_like(acc_g)
                acc_l[...] = jnp.zeros_like(acc_l)

            gv = g_ref[...]
            r = lax.rsqrt(jnp.mean(xn * xn, axis=-1, keepdims=True) + EPS)
            xh = xn * r
            err = xh * gv - t_ref[...]
            dy = err * (1.0 / d)
            gy = dy * gv
            dx_ref[...] = r * (gy - xh * jnp.mean(gy * xh, axis=-1, keepdims=True))
            acc_g[...] += jnp.sum((dy * xh).reshape(tm // 8, 8, d), axis=0)
            acc_l[...] += jnp.sum((err * err).reshape(tm // 8, 8, d), axis=0)

            @pl.when(i == n - 1)
            def _():
                dg_ref[...] = jnp.sum(acc_g[...], axis=0, keepdims=True)
                tot = jnp.sum(jnp.sum(acc_l[...], axis=0, keepdims=True), axis=1, keepdims=True)
                loss_ref[...] = jnp.broadcast_to(tot * (0.5 / d), (1, 128))
        ya_ref[...] = ya.astype(BF16)
        yb_ref[...] = yb.astype(BF16)

    row = lambda w, j: pl.BlockSpec((tm, w), lambda i: (i, j))
    const = lambda shape: pl.BlockSpec(shape, lambda i: (0, 0))
    in_specs = [row(d, 0), row(512, SEG["ga"][2] // 512), row(512, SEG["gr"][2] // 512),
                row(1024, SEG["gm"][2] // 1024), row(1024, SEG["gm"][2] // 1024 + 1),
                row(512, 0), row(512, 0), const((d, 1024)), const((d, d))]
    out_specs = [row(d, 0), row(d, 0), row(d, 0)]
    out_shape = [SDS((t, d), F32), SDS((t, d), BF16), SDS((t, d), BF16)]
    args, scratch = [x, z, z, z, z, oa, on, wb_t, wout], []
    if head is not None:
        in_specs += [const((1, d)), row(d, 0)]
        out_specs += [const((1, d)), const((1, 128))]
        out_shape += [SDS((1, d), F32), SDS((1, 128), F32)]
        args += list(head)
        scratch = [pltpu.VMEM((8, d), F32), pltpu.VMEM((8, d), F32)]
    return pl.pallas_call(
        body, name="merge_fwd", grid=(n,), in_specs=in_specs, out_specs=out_specs, out_shape=out_shape,
        scratch_shapes=scratch,
        compiler_params=_params(("arbitrary",) if head is not None else ("parallel",)),
    )(*args)


def _merge_bwd(dxo, z, oa, on, ya, yb, wb_t, wout):
    t, d = dxo.shape
    tm = min(256, t)
    n = t // tm

    def body(dx_ref, ga_ref, gr_ref, gm0_ref, gm1_ref, oa_ref, on_ref, ya_ref, yb_ref, wb_ref, wo_ref,
             doa_ref, don_ref, dz_ref, dwo_ref, dwb_ref, acc_o, acc_b):
        i = pl.program_id(0)

        @pl.when(i == 0)
        def _():
            acc_o[...] = jnp.zeros_like(acc_o)
            acc_b[...] = jnp.zeros_like(acc_b)

        dxb = dx_ref[...].astype(BF16)
        ya, yb = ya_ref[...].astype(F32), yb_ref[...].astype(F32)
        g0, g1 = _sigmoid(gm0_ref[...]), _sigmoid(gm1_ref[...])
        mb = (g0 * ya + g1 * yb).astype(BF16)
        dm = _dot(dxb, wo_ref[...], NT)
        dya = (dm * g0).astype(BF16)
        dyb = (dm * g1).astype(BF16)
        dz_ref[:, 1024:2048] = (dm * ya * g0 * (1.0 - g0)).astype(BF16)
        dz_ref[:, 2048:3072] = (dm * yb * g1 * (1.0 - g1)).astype(BF16)

        def branch(g_ref, o_ref, dy, w, do_ref, lo):
            gv, ov = g_ref[...], o_ref[...]
            sg = _sigmoid(gv)
            silu = gv * sg
            du = _dot(dy, w)
            do_ref[...] = du * silu
            dz_ref[:, lo:lo + 512] = (du * ov * (sg * (1.0 + gv * (1.0 - sg)))).astype(BF16)
            acc_b[:, lo:lo + 512] += _dot(dy, (silu * ov).astype(BF16), TN)

        branch(ga_ref, oa_ref, dya, wb_ref[:, :512], doa_ref, 0)
        branch(gr_ref, on_ref, dyb, wb_ref[:, 512:], don_ref, 512)
        acc_o[...] += _dot(mb, dxb, TN)

        @pl.when(i == n - 1)
        def _():
            dwo_ref[...] = acc_o[...].astype(BF16)
            dwb_ref[...] = acc_b[...].astype(BF16)

    row = lambda w, j: pl.BlockSpec((tm, w), lambda i: (i, j))
    const = lambda shape: pl.BlockSpec(shape, lambda i: (0, 0))
    return pl.pallas_call(
        body, name="merge_bwd", grid=(n,),
        in_specs=[row(d, 0), row(512, SEG["ga"][2] // 512), row(512, SEG["gr"][2] // 512),
                  row(1024, SEG["gm"][2] // 1024), row(1024, SEG["gm"][2] // 1024 + 1),
                  row(512, 0), row(512, 0), row(d, 0), row(d, 0), const((d, 1024)), const((d, d))],
        out_specs=[row(512, 0), row(512, 0), row(3072, 0), const((d, d)), const((d, 1024))],
        out_shape=[SDS((t, 512), F32), SDS((t, 512), F32), SDS((t, 3072), BF16), SDS((d, d), BF16),
                   SDS((d, 1024), BF16)],
        scratch_shapes=[pltpu.VMEM((d, d), F32), pltpu.VMEM((d, 1024), F32)],
        compiler_params=_params(("arbitrary",)),
    )(dxo, z, z, z, z, oa, on, ya, yb, wb_t, wout)


def _ret_bwd(qrot, krot, vb, orr, don, gnw, lgf, lgb, cos, sin):
    t = qrot.shape[0]
    c = RET_CHUNK
    nc = t // c
    hd = RET_HEAD_DIM
    unroll = 4 if nc % 4 == 0 else 1

    def body(lgf_ref, lgb_ref, q_ref, k_ref, v_ref, o_ref, dn_ref, w_ref, c_ref, s_ref,
             dq_ref, dk_ref, dv_ref, dw_ref, dlf_ref, dlb_ref, qt, kt, dob, uf, ub, wf, wb, sfa, sba, gfa, gba):
        h = pl.program_id(0)
        fw = _Dir(lgf_ref[h], False)
        bw = _Dir(lgb_ref[h], True)
        fw.dt, bw.dt = fw.d.T, bw.d.T

        o = o_ref[...]
        xc = o - jnp.mean(o, axis=-1, keepdims=True)
        r = lax.rsqrt(jnp.mean(xc * xc, axis=-1, keepdims=True) + EPS)
        xh = xc * r
        dn = dn_ref[...]
        gy = dn * w_ref[...]
        d_o = r * (gy - jnp.mean(gy, axis=-1, keepdims=True) - xh * jnp.mean(gy * xh, axis=-1, keepdims=True))
        dw_ref[...] = jnp.sum(dn * xh, axis=0, keepdims=True)
        dob[...] = d_o.astype(BF16)
        for i in range(nc):
            qt[i] = q_ref[i * c:(i + 1) * c, :].astype(F32).T.astype(BF16)
            kt[i] = k_ref[i * c:(i + 1) * c, :].astype(F32).T.astype(BF16)

        def rows(ci):
            return pl.ds(pl.multiple_of(ci * c, c), c)

        def products(ci, carry):
            sl = rows(ci)
            vv, do32 = v_ref[sl, :], dob[sl, :].astype(F32)
            ktf = kt[ci].astype(F32)
            uf[ci] = _dot((ktf * fw.kd_row).astype(BF16), vv)
            ub[ci] = _dot((ktf * bw.kd_row).astype(BF16), vv)
            wf[ci] = _dot(qt[ci], (do32 * fw.qd).astype(BF16))
            wb[ci] = _dot(qt[ci], (do32 * bw.qd).astype(BF16))
            return carry

        lax.fori_loop(0, nc, products, 0, unroll=16 if nc % 16 == 0 else unroll)

        def scan(i, carry):
            sf, sb, gf, gb = carry
            j = nc - 1 - i
            sfa[i] = sf.astype(BF16)
            sba[j] = sb.astype(BF16)
            gfa[j] = gf.astype(BF16)
            gba[i] = gb.astype(BF16)
            return sf * fw.cd + uf[i], sb * bw.cd + ub[j], gf * fw.cd + wf[j], gb * bw.cd + wb[i]

        zero = jnp.zeros((hd, hd), F32)
        lax.fori_loop(0, nc, scan, (zero, zero, zero, zero))

        def one_dir(p, s_all, g_all, ci, qq, kk, vv, do, a, bm):
            sb, gb = s_all[ci], g_all[ci]
            doq = (do.astype(F32) * p.qd).astype(BF16)
            dqc = _dot(doq, sb, NT)
            kkd = (kk.astype(F32) * p.kd_col).astype(BF16)
            dk2 = _dot(vv, gb, NT) * p.kd_col
            terms = (p.dist * p.d * a * bm + p.wq * qq.astype(F32) * dqc + p.wk * kk.astype(F32) * dk2
                     + (float(c) * p.cd) * gb.astype(F32) * sb.astype(F32))
            return dqc, dk2, _dot(kkd, gb), terms

        d_both, dt_both = fw.d + bw.d, fw.dt + bw.dt

        def chunk(ci, carry):
            af, ab = carry
            sl = rows(ci)
            qq, kk, vv, do = q_ref[sl, :], k_ref[sl, :], v_ref[sl, :], dob[sl, :]
            a, bm = _dot(qq, kk, NT), _dot(do, vv, NT)
            at, bt = _dot(kk, qq, NT), _dot(vv, do, NT)
            dqf, dkf, dvf, tf = one_dir(fw, sfa, gfa, ci, qq, kk, vv, do, a, bm)
            dqb, dkb, dvb, tb = one_dir(bw, sba, gba, ci, qq, kk, vv, do, a, bm)
            cc, ss = c_ref[sl, :], s_ref[sl, :]
            dq = _dot((bm * d_both).astype(BF16), kk) + dqf + dqb
            dk = _dot((bt * dt_both).astype(BF16), qq) + dkf + dkb
            dq_ref[sl, :] = _rope_bwd(dq, cc, ss, hd // 4).astype(BF16)
            dk_ref[sl, :] = (_rope_bwd(dk, cc, ss, hd // 4) * (hd ** -0.5)).astype(BF16)
            dv_ref[sl, :] = (_dot((at * dt_both).astype(BF16), do) + dvf + dvb).astype(BF16)
            return af + tf, ab + tb

        pair = 8 if nc % 8 == 0 else 1

        def chunks(i, carry):
            for j in range(pair):
                carry = chunk(i * pair + j, carry)
            return carry

        af, ab = lax.fori_loop(0, nc // pair, chunks, (zero, zero))
        tot = lambda m: jnp.sum(jnp.sum(m, axis=0, keepdims=True), axis=1, keepdims=True)
        dlf_ref[...] = jnp.broadcast_to(tot(af).reshape(1, 1, 1), (1, 8, 128))
        dlb_ref[...] = jnp.broadcast_to(tot(ab).reshape(1, 1, 1), (1, 8, 128))

    smem = pl.BlockSpec(memory_space=pltpu.SMEM)
    head = pl.BlockSpec((t, 128), lambda h: (0, h))
    vec = pl.BlockSpec((1, 128), lambda h: (0, h))
    scal = pl.BlockSpec((1, 8, 128), lambda h: (h, 0, 0))
    table = pl.BlockSpec((t, 128), lambda h: (0, 0))
    mats = lambda dt: pltpu.VMEM((nc, hd, hd), dt)
    return pl.pallas_call(
        body, name="ret_bwd", grid=(RET_HEADS,),
        in_specs=[smem, smem, head, head, head, head, head, vec, table, table],
        out_specs=[head, head, head, vec, scal, scal],
        out_shape=[SDS((t, RET_WIDTH), BF16)] * 3 + [SDS((1, RET_WIDTH), F32), SDS((RET_HEADS, 8, 128), F32),
                                                    SDS((RET_HEADS, 8, 128), F32)],
        scratch_shapes=[pltpu.VMEM((nc, hd, c), BF16), pltpu.VMEM((nc, hd, c), BF16), pltpu.VMEM((t, hd), BF16),
                        mats(F32), mats(F32), mats(F32), mats(F32), mats(BF16), mats(BF16), mats(BF16), mats(BF16)],
        compiler_params=_params(("parallel",)),
    )(lgf, lgb, qrot, krot, vb, orr, don, gnw, cos, sin)


def _attn_bwd(q, qt, k, v, doa, oa, lse, ex=None):
    t = q.shape[1]
    tq = min(ATTN_BWD_QUERY_TILE, t)
    nq = t // tq
    tk = min(ATTN_BWD_KEY_CHUNK, t)
    nk = t // tk
    hd = ATTN_HEAD_DIM
    scale = hd ** -0.5

    def body(q_ref, qt_ref, k_ref, v_ref, do_ref, o_ref, lse_ref, dq_ref, dkt_ref, dvt_ref):
        p, i = pl.program_id(0), pl.program_id(1)

        @pl.when(jnp.logical_and(p % 2 == 0, i == 0))
        def _():
            dkt_ref[...] = jnp.zeros_like(dkt_ref)
            dvt_ref[...] = jnp.zeros_like(dvt_ref)

        dov, ov = do_ref[...], o_ref[...]
        dovt = dov.T
        lanes = lambda col: jnp.concatenate([col] * (tk // 128), axis=1)
        outs = []
        for j in range(2):
            qq, qqt = q_ref[j], qt_ref[j]
            do32 = dov[:, j * hd:(j + 1) * hd]
            do, dot_ = do32.astype(BF16), dovt[j * hd:(j + 1) * hd, :].astype(BF16)
            dd = lanes(jnp.broadcast_to(jnp.sum(do32 * ov[:, j * hd:(j + 1) * hd], axis=1, keepdims=True), (tq, 128)))
            lse_j = lanes(jnp.broadcast_to(lse_ref[j], (128, tq)).T)
            dq = jnp.zeros((tq, hd), F32)
            for c in range(nk):
                sl = slice(c * tk, (c + 1) * tk)
                kc, vc = k_ref[0, sl, :], v_ref[0, sl, :]
                pr = jnp.exp(_dot(qq, kc, NT) - lse_j)
                ds = (pr * (_dot(do, vc, NT) - dd)).astype(BF16)
                dvt_ref[0, :, sl] += _dot(dot_, pr.astype(BF16))
                dkt_ref[0, :, sl] += _dot(qqt, ds)
                dq = dq + _dot(ds, kc)
            outs.append(dq * scale)
        dq_ref[...] = jnp.concatenate(outs, axis=-1)

    kv = pl.BlockSpec((1, t, hd), lambda p, i: (p // 2, 0, 0))
    kvt = pl.BlockSpec((1, hd, t), lambda p, i: (p // 2, 0, 0))
    pair = pl.BlockSpec((tq, 128), lambda p, i: (i, p))
    first = lambda: jnp.logical_and(pl.program_id(0) == 0, pl.program_id(1) == 0)
    last = lambda: jnp.logical_and(pl.program_id(0) == 3, pl.program_id(1) == nq - 1)
    xi, xo, xs, xscr, xargs = _ex_args(ex)
    return pl.pallas_call(
        _with_exchange(body, 7, 3, 0, ex, first, last), name="attn_bwd", grid=(4, nq),
        in_specs=[pl.BlockSpec((2, tq, hd), lambda p, i: (p, i, 0)), pl.BlockSpec((2, hd, tq), lambda p, i: (p, 0, i)),
                  kv, kv, pair, pair, pl.BlockSpec((2, 1, tq), lambda p, i: (p, 0, i))] + xi,
        out_specs=[pair, kvt, kvt] + xo,
        out_shape=[SDS((t, ATTN_WIDTH), F32), SDS((ATTN_KV_HEADS, hd, t), F32),
                   SDS((ATTN_KV_HEADS, hd, t), F32)] + xs,
        scratch_shapes=xscr,
        compiler_params=_params(("arbitrary", "arbitrary")),
    )(q, qt, k, v, doa, oa, lse, *xargs)


def _attn_post_bwd(dq, dk, dv, z, qn, kn, cos, sin, ones_bd):
    t = z.shape[0]
    tm = min(512, t)
    n = t // tm
    hd = ATTN_HEAD_DIM

    def body(dq_ref, dk_ref, dv_ref, zq_ref, zkv_ref, qn_ref, kn_ref, c_ref, s_ref, b_ref,
             dz_ref, dqn_ref, dkn_ref, acc_q, acc_k):
        i = pl.program_id(0)

        @pl.when(i == 0)
        def _():
            acc_q[...] = jnp.zeros_like(acc_q)
            acc_k[...] = jnp.zeros_like(acc_k)

        bd = b_ref[...]
        c2, s2 = c_ref[...], s_ref[...]

        def norm_bwd(dy, x, w, ones, cos_t, sin_t, acc):
            dyr = _rope_bwd(dy, cos_t, sin_t, hd // 4)
            r = lax.rsqrt(_group_mean(x * x, ones) + EPS)
            xh = x * r
            gy = dyr * w
            acc[...] += jnp.sum((dyr * xh).reshape(tm // 8, 8, x.shape[-1]), axis=0)
            return r * (gy - xh * _group_mean(gy * xh, ones))

        cq = jnp.concatenate([c2] * 4, axis=-1)
        sq = jnp.concatenate([s2] * 4, axis=-1)
        dz_ref[:, :512] = norm_bwd(dq_ref[...], zq_ref[...], qn_ref[...], bd, cq, sq, acc_q).astype(BF16)
        zkv = zkv_ref[...]
        dkk = jnp.concatenate([dk_ref[0], dk_ref[1]], axis=0).T
        dz_ref[:, 512:640] = norm_bwd(dkk, zkv[:, :128], kn_ref[...], bd[:128, :128], c2, s2, acc_k).astype(BF16)
        dz_ref[:, 640:768] = jnp.concatenate([dv_ref[0], dv_ref[1]], axis=0).T.astype(BF16)

        @pl.when(i == n - 1)
        def _():
            dqn_ref[...] = jnp.sum(acc_q[...], axis=0, keepdims=True)
            dkn_ref[...] = jnp.sum(acc_k[...], axis=0, keepdims=True)

    kv_blk = SEG["ka"][2] // 256
    kvs = pl.BlockSpec((ATTN_KV_HEADS, hd, tm), lambda i: (0, 0, i))
    const = lambda shape: pl.BlockSpec(shape, lambda i: (0, 0))
    return pl.pallas_call(
        body, name="attn_post_bwd", grid=(n,),
        in_specs=[pl.BlockSpec((tm, 512), lambda i: (i, 0)), kvs, kvs,
                  pl.BlockSpec((tm, 512), lambda i: (i, 0)), pl.BlockSpec((tm, 256), lambda i: (i, kv_blk)),
                  const((1, 512)), const((1, 128)),
                  pl.BlockSpec((tm, 128), lambda i: (i, 0)), pl.BlockSpec((tm, 128), lambda i: (i, 0)),
                  const((512, 512))],
        out_specs=[pl.BlockSpec((tm, 768), lambda i: (i, 0)), const((1, 512)), const((1, 128))],
        out_shape=[SDS((t, 768), BF16), SDS((1, 512), F32), SDS((1, 128), F32)],
        scratch_shapes=[pltpu.VMEM((8, 512), F32), pltpu.VMEM((8, 128), F32)],
        compiler_params=_params(("arbitrary",)),
    )(dq, dk, dv, z, z, qn, kn, cos, sin, ones_bd)


def _in_bwd(dxo, x, g, w_t, dz_a, dz_m, dqr, dkr, dvr, after=None):
    t, d = x.shape
    tm = min(256, t)
    n = t // tm
    parts = [(0, 0, 768, 0), (1, 0, 512, SEG["ga"][0]), (2, 0, 512, SEG["qr"][0]), (3, 0, 512, SEG["kr"][0]),
             (4, 0, 512, SEG["vr"][0]), (1, 512, 2560, SEG["gr"][0])]

    def body(dx_ref, x_ref, g_ref, w_ref, a_ref, m_ref, q_ref, k_ref, v_ref, o_ref, dg_ref, acc):
        i = pl.program_id(0)

        @pl.when(i == 0)
        def _():
            acc[...] = jnp.zeros_like(acc)

        pieces = [a_ref, m_ref, q_ref, k_ref, v_ref]
        dh = jnp.zeros((tm, d), F32)
        for pi, lo, w, row in parts:
            dh = dh + _dot(pieces[pi][:, lo:lo + w], w_ref[row:row + w, :])
        xv = x_ref[...]
        r = lax.rsqrt(jnp.mean(xv * xv, axis=-1, keepdims=True) + EPS)
        xh = xv * r
        gy = dh * g_ref[...]
        o_ref[...] = dx_ref[...] + r * (gy - xh * jnp.mean(gy * xh, axis=-1, keepdims=True))
        acc[...] += jnp.sum((dh * xh).reshape(tm // 8, 8, d), axis=0)

        @pl.when(i == n - 1)
        def _():
            dg_ref[...] = jnp.sum(acc[...], axis=0, keepdims=True)

    row = lambda w: pl.BlockSpec((tm, w), lambda i: (i, 0))
    const = lambda shape: pl.BlockSpec(shape, lambda i: (0, 0))
    extra = [] if after is None else [after]
    return pl.pallas_call(
        (lambda *refs: body(*refs[:9], *refs[9 + len(extra):])), name="in_bwd", grid=(n,),
        in_specs=[row(d), row(d), const((1, d)), const((D_IN, d)), row(768), row(3072), row(512), row(512),
                  row(512)] + [const(a.shape) for a in extra],
        out_specs=[row(d), const((1, d))],
        out_shape=[SDS((t, d), F32), SDS((1, d), F32)],
        scratch_shapes=[pltpu.VMEM((8, d), F32)],
        compiler_params=_params(("arbitrary",)),
    )(dxo, x, g, w_t, dz_a, dz_m, dqr, dkr, dvr, *extra)


def _dw_in(h_t, dz_a, dz_m, dqr, dkr, dvr):
    d, t = h_t.shape
    tn = 256
    parts = [(0, 0, 0, 3), (1, 0, SEG["ga"][0] // tn, 2), (2, 0, SEG["qr"][0] // tn, 2),
             (3, 0, SEG["kr"][0] // tn, 2), (4, 0, SEG["vr"][0] // tn, 2), (1, 2, SEG["gr"][0] // tn, 10)]
    pieces = [dz_a, dz_m, dqr, dkr, dvr]

    def col_block(pi):
        mine = [(c0, r0, n) for q, c0, r0, n in parts if q == pi]

        def index(j):
            c0, r0, n = mine[0]
            blk = c0 + jnp.clip(j - r0, 0, n - 1)
            for c0, r0, n in mine[1:]:
                blk = jnp.where(j >= r0, c0 + jnp.clip(j - r0, 0, n - 1), blk)
            return 0, blk

        return index

    def body(h_ref, *refs):
        o_ref = refs[-1]
        j = pl.program_id(0)
        for pi, _, r0, n in parts:
            @pl.when(jnp.logical_and(j >= r0, j < r0 + n))
            def _(p_ref=refs[pi]):
                o_ref[...] = _dot(h_ref[...], p_ref[...]).T.astype(BF16)

    return pl.pallas_call(
        body, name="dw_in", grid=(D_IN // tn,),
        in_specs=[pl.BlockSpec((d, t), lambda j: (0, 0))] + [pl.BlockSpec((t, tn), col_block(pi)) for pi in range(5)],
        out_specs=pl.BlockSpec((tn, d), lambda j: (j, 0)),
        out_shape=SDS((D_IN, d), BF16),
        compiler_params=_params(("arbitrary",)),
    )(h_t, *pieces)


def _adamw_math(w, g, m, v):
    mn = ADAM_B1 * m + (1.0 - ADAM_B1) * g
    vn = ADAM_B2 * v + (1.0 - ADAM_B2) * (g * g)
    m_hat = mn / (1.0 - ADAM_B1 ** ADAM_STEP)
    v_hat = vn / (1.0 - ADAM_B2 ** ADAM_STEP)
    return -ADAM_LR * (m_hat / (jnp.sqrt(v_hat) + ADAM_EPS) + ADAM_WD * w), mn, vn


def _sum_adamw(recvs, w, m, v, lane0, tn, layer0=0, prev=None, own=None):
    _, r, c = w.shape
    j0 = lane0 // tn
    n = len(recvs)
    has_own = own is not None

    def body(*refs):
        mine_ref, refs = (refs[0], refs[1:]) if has_own else (None, refs)
        w_ref, m_ref, v_ref = refs[n:n + 3]
        g_ref, d_ref, mo_ref, vo_ref = refs[-4:]

        def run(r_ref):
            def slot(s):
                if has_own:
                    return jnp.where(mine_ref[0] == s, refs[n + 3][...], r_ref[s]).astype(F32)
                return r_ref[s].astype(F32)

            g = slot(0)
            for s in range(1, N_DEV):
                g = g + slot(s)
            g_ref[0] = g
            d_ref[0], mo_ref[0], vo_ref[0] = _adamw_math(w_ref[0], g, m_ref[0], v_ref[0])

        for i in range(n):
            pl.when(pl.program_id(0) == i)(functools.partial(run, refs[i]))

    slots = pl.BlockSpec((N_DEV, r, tn), lambda i, j, *_: (0, 0, j0 + j))
    blk = pl.BlockSpec((1, r, tn), lambda i, j, *_: (layer0 + i, 0, j))
    before = [] if prev is None else list(prev)
    in_specs, args = [slots] * n + [blk] * 3, [*recvs, w, m, v]
    if has_own:
        assert n == 1
        in_specs.append(pl.BlockSpec((r, tn), lambda i, j, mine: (mine[0], j0 + j)))
        args.append(own[0])
    n_pre = len(args) + has_own
    return pl.pallas_call(
        body, name="sum_adamw",
        grid_spec=pltpu.PrefetchScalarGridSpec(
            num_scalar_prefetch=int(has_own), grid=(n, c // tn),
            in_specs=in_specs + [ANY] * len(before), out_specs=[blk] * 4),
        out_shape=[SDS(w.shape, F32)] * 4,
        input_output_aliases={n_pre + k: k for k in range(len(before))},
        compiler_params=_params(("parallel", "parallel")),
    )(*([own[1]] if has_own else []), *args, *before)


def _adamw(w, g, m, v):
    rows, cols = w.shape
    tr = 256 if rows % 256 == 0 else rows

    def body(w_ref, g_ref, m_ref, v_ref, d_ref, mo_ref, vo_ref):
        d_ref[...], mo_ref[...], vo_ref[...] = _adamw_math(w_ref[...], g_ref[...], m_ref[...], v_ref[...])

    blk = pl.BlockSpec((tr, cols), lambda i: (i, 0))
    return pl.pallas_call(
        body, name="adamw", grid=(rows // tr,),
        in_specs=[blk] * 4, out_specs=[blk] * 3, out_shape=[SDS((rows, cols), F32)] * 3,
        compiler_params=_params(("parallel",)),
    )(w, g, m, v)


def _all_gather(shards):
    na = len(shards)
    chips = (4, 2, 6)

    def body(*refs):
        ins, outs = refs[:na], refs[na:2 * na]
        send_sems, recv_sems, local_sems = refs[2 * na:]
        _, mine = _flip(0)

        def rows(a, idx):
            r = shards[a].shape[0]
            return outs[a].at[pl.ds(pl.multiple_of(idx * r, 16), r), :]

        def copy(a, slot, block_idx, to, src=None):
            return pltpu.make_async_remote_copy(
                src_ref=rows(a, block_idx) if src is None else src, dst_ref=rows(a, block_idx),
                send_sem=send_sems.at[a, slot], recv_sem=recv_sems.at[a, slot],
                device_id=to, device_id_type=MESH_ID)

        sibling, sibling_idx = _flip(1)
        local, started = [], []
        for a in range(na):
            cp = pltpu.make_async_copy(ins[a], rows(a, mine), local_sems.at[a])
            cp.start()
            local.append(cp)
            first = [copy(a, 0, mine, sibling, src=ins[a])]
            first += [copy(a, 1 + j, mine, _flip(k)[0], src=ins[a]) for j, k in enumerate(chips)]
            for cp in first:
                cp.start()
            started += first
        for a in range(na):
            for j, k in enumerate(chips):
                _, theirs = _flip(k)
                copy(a, 1 + j, theirs, _flip(0)[0]).wait_recv()
                fwd = copy(a, 4 + j, theirs, sibling)
                fwd.start()
                started.append(fwd)
        for a in range(na):
            copy(a, 0, sibling_idx, _flip(0)[0]).wait_recv()
            for j, k in enumerate(chips):
                _, theirs = _flip(k | 1)
                copy(a, 4 + j, theirs, _flip(0)[0]).wait_recv()
        for cp in started:
            cp.wait_send()
        for cp in local:
            cp.wait()

    return pl.pallas_call(
        body, name="all_gather_weights",
        in_specs=[ANY] * na, out_specs=[ANY] * na,
        out_shape=[SDS((N_DEV * s.shape[0], s.shape[1]), s.dtype) for s in shards],
        scratch_shapes=[pltpu.SemaphoreType.DMA((na, 7)), pltpu.SemaphoreType.DMA((na, 7)),
                        pltpu.SemaphoreType.DMA((na,))],
        compiler_params=pltpu.CompilerParams(has_side_effects=True),
    )(*shards)


def _scatter_blocks_of(g_ref, rows, idx):
    return g_ref.at[pl.ds(pl.multiple_of(idx * rows, 16), rows), :]


def _scatter_start(g):
    rows = g.shape[0] // N_DEV
    land_shape = (N_DEV, rows, g.shape[1])

    def body(g_ref, land_ref, send_sems, recv_sems, g_thru, land_thru, token):
        _, mine = _flip(0)
        for k in range(1, N_DEV):
            peer, theirs = _flip(k)
            pltpu.make_async_remote_copy(
                src_ref=_scatter_blocks_of(g_ref, rows, theirs), dst_ref=land_ref.at[mine],
                send_sem=send_sems.at[k - 1], recv_sem=recv_sems.at[k - 1],
                device_id=peer, device_id_type=MESH_ID).start()
        token[...] = jnp.zeros_like(token)

    hbm, sem = pl.BlockSpec(memory_space=pltpu.HBM), pl.BlockSpec(memory_space=pltpu.SEMAPHORE)
    return pl.pallas_call(
        body, name="scatter_start",
        out_shape=(pltpu.SemaphoreType.DMA((N_DEV - 1,)), pltpu.SemaphoreType.DMA((N_DEV - 1,)),
                   pltpu.HBM(g.shape, g.dtype), pltpu.HBM(land_shape, g.dtype), SDS((8, 128), F32)),
        in_specs=(hbm, hbm), out_specs=(sem, sem, hbm, hbm, pl.BlockSpec(memory_space=pltpu.VMEM)),
        input_output_aliases={0: 2, 1: 3},
        compiler_params=pltpu.CompilerParams(has_side_effects=pltpu.SideEffectType.DATAFLOW_SIDE_EFFECTING),
    )(pltpu.with_memory_space_constraint(g, pltpu.HBM),
      pltpu.with_memory_space_constraint(lax.empty(land_shape, g.dtype), pltpu.HBM))


def _scatter_wait(send_sems, recv_sems, g_thru, land_thru, after):
    rows = g_thru.shape[0] // N_DEV

    def body(g_ref, land_ref, send_sems, recv_sems, *rest):
        me, _ = _flip(0)
        for k in range(1, N_DEV):
            _, theirs = _flip(k)
            copy = pltpu.make_async_remote_copy(
                src_ref=_scatter_blocks_of(g_ref, rows, theirs), dst_ref=land_ref.at[theirs],
                send_sem=send_sems.at[k - 1], recv_sem=recv_sems.at[k - 1],
                device_id=me, device_id_type=MESH_ID)
            copy.wait_send()
            copy.wait_recv()

    hbm, sem = pl.BlockSpec(memory_space=pltpu.HBM), pl.BlockSpec(memory_space=pltpu.SEMAPHORE)
    return pl.pallas_call(
        body, name="scatter_wait",
        out_shape=(pltpu.HBM(g_thru.shape, g_thru.dtype), pltpu.HBM(land_thru.shape, land_thru.dtype)),
        in_specs=(hbm, hbm, sem, sem) + (ANY,) * len(after), out_specs=(hbm, hbm), input_output_aliases={0: 0, 1: 1},
        compiler_params=pltpu.CompilerParams(has_side_effects=pltpu.SideEffectType.DATAFLOW_SIDE_EFFECTING),
    )(g_thru, land_thru, send_sems, recv_sems, *after)


def _all_reduce_small(packed):
    shape = packed.shape

    def body(p_ref, o_ref, slots, send_sems, recv_sems):
        me, mine = _flip(0)
        slots[mine] = p_ref[...]
        sends = []
        for k in range(1, N_DEV):
            peer, _ = _flip(k)
            cp = pltpu.make_async_remote_copy(
                src_ref=p_ref, dst_ref=slots.at[mine], send_sem=send_sems.at[k - 1], recv_sem=recv_sems.at[k - 1],
                device_id=peer, device_id_type=MESH_ID)
            cp.start()
            sends.append(cp)
        for k in range(1, N_DEV):
            _, theirs = _flip(k)
            pltpu.make_async_remote_copy(
                src_ref=p_ref, dst_ref=slots.at[theirs], send_sem=send_sems.at[k - 1],
                recv_sem=recv_sems.at[k - 1], device_id=me, device_id_type=MESH_ID).wait_recv()
        for cp in sends:
            cp.wait_send()
        acc = slots[0]
        for s in range(1, N_DEV):
            acc = acc + slots[s]
        o_ref[...] = acc

    vm = pl.BlockSpec(memory_space=pltpu.VMEM)
    return pl.pallas_call(
        body, name="all_reduce_small", in_specs=[vm], out_specs=vm, out_shape=SDS(shape, F32),
        scratch_shapes=[pltpu.VMEM((N_DEV,) + shape, F32), pltpu.SemaphoreType.DMA((7,)),
                        pltpu.SemaphoreType.DMA((7,))],
        compiler_params=pltpu.CompilerParams(has_side_effects=True),
    )(packed)


def _layer_fwd(x, p, tabs, ex):
    z, h_t, q, qt, k, v, vt, qrot, krot, vb = _in_proj(x, p["norm_g"], p["w_in_t"], p["qn"], p["kn"], tabs["ca"],
                                                       tabs["sa"], tabs["ones"], tabs["cr"], tabs["sr"])
    oa, lse, *gathered = _attn_fwd(q, k, vt, ex)
    orr, on = _ret_fwd(qrot, krot, vb, p["lgf"], p["lgb"], p["gnw"])
    return z, h_t, q, qt, k, v, lse, oa, qrot, krot, vb, orr, on, gathered


def _layer_bwd(dxo, s, p, tabs, ex_attn, scatter_w_in):
    doa, don, dz_m, d_wout, d_wb_t = _merge_bwd(dxo, s["z"], s["oa"], s["on"], s["ya"], s["yb"], p["wb_t"], p["w_out"])
    dq_a, dk_a, dv_a, *recv_attn = _attn_bwd(s["q"], s["qt"], s["k"], s["v"], doa, s["oa"], s["lse"],
                                              ex_attn(d_wb_t, d_wout))
    dz_a, d_qn, d_kn = _attn_post_bwd(dq_a, dk_a, dv_a, s["z"], p["qn"], p["kn"], tabs["ca"], tabs["sa"],
                                      tabs["ones"])
    dqr, dkr, dvr, d_gnw, d_lgf, d_lgb = _ret_bwd(s["qrot"], s["krot"], s["vb"], s["orr"], don, p["gnw"],
                                                  p["lgf"], p["lgb"], tabs["cr"], tabs["sr"])
    buf = _dw_in(s["h_t"], dz_a, dz_m, dqr, dkr, dvr)
    pending, token = None, None
    if scatter_w_in:
        *pending, token = _scatter_start(buf)
    dx, d_norm_g = _in_bwd(dxo, s["x"], p["norm_g"], p["w_in_t"], dz_a, dz_m, dqr, dkr, dvr, token)
    grads = dict(w_in_t=buf, wb_t=d_wb_t, w_out=d_wout, norm_g=d_norm_g, gnw=d_gnw,
                 qn=d_qn.reshape(ATTN_Q_HEADS, ATTN_HEAD_DIM).sum(axis=0),
                 kn=d_kn.reshape(ATTN_KV_HEADS, ATTN_HEAD_DIM).sum(axis=0),
                 lgf=d_lgf[:, 0, 0], lgb=d_lgb[:, 0, 0])
    return dx, grads, recv_attn, pending


def _adamw_nd(w, g, m, v):
    shape = w.shape
    two_d = (1, shape[0]) if w.ndim == 1 else (-1, shape[-1])
    out = _adamw(w.reshape(two_d), g.reshape(two_d), m.reshape(two_d), v.reshape(two_d))
    return tuple(o.reshape(shape) for o in out)


def kernel(x, norm_g, w_in, attn_q_norm, attn_k_norm, ret_decay_fwd, ret_decay_bwd, ret_gn_w, w_branch_attn, w_branch_ret, w_out, final_norm_g, loss_target, m_norm_g, m_w_in, m_attn_q_norm, m_attn_k_norm, m_ret_decay_fwd, m_ret_decay_bwd, m_ret_gn_w, m_w_branch_attn, m_w_branch_ret, m_w_out, m_final_norm_g, v_norm_g, v_w_in, v_attn_q_norm, v_attn_k_norm, v_ret_decay_fwd, v_ret_decay_bwd, v_ret_gn_w, v_w_branch_attn, v_w_branch_ret, v_w_out, v_final_norm_g):
    t, d = x.shape[1], x.shape[2]
    x2, target = x[0], loss_target[0]

    w_in_sh, wb_sh, wout_sh = [], [], []
    for l in range(DEPTH):
        w_in_sh.append(jnp.swapaxes(w_in[l], 0, 1).astype(BF16))
        wb_sh.append(jnp.concatenate([w_branch_attn[l].T, w_branch_ret[l].T], axis=1).astype(BF16))
        wout_sh.append(w_out[l].astype(BF16))

    ca, sa = _rope_tables(t, ATTN_HEAD_DIM)
    cr, sr = _rope_tables(t, RET_HEAD_DIM)
    grp = jnp.arange(ATTN_WIDTH) // ATTN_HEAD_DIM
    tabs = dict(ca=jnp.tile(ca, (1, 2)), sa=jnp.tile(sa, (1, 2)), cr=cr, sr=sr,
                ones=jnp.where(grp[:, None] == grp[None, :], 1.0 / ATTN_HEAD_DIM, 0.0).astype(BF16))
    layers = []
    for l in range(DEPTH):
        layers.append(dict(
            norm_g=norm_g[l][None], qn=jnp.tile(attn_q_norm[l], ATTN_Q_HEADS)[None],
            kn=jnp.tile(attn_k_norm[l], ATTN_KV_HEADS)[None], gnw=ret_gn_w[l][None],
            lgf=jax.nn.log_sigmoid(ret_decay_fwd[l]), lgb=jax.nn.log_sigmoid(ret_decay_bwd[l])))

    layers[0]["w_in_t"], = _all_gather([w_in_sh[0]])
    gathers = [_Exchange("gather", [wb_sh[0], wout_sh[0], w_in_sh[1]]), _Exchange("gather", [wb_sh[1], wout_sh[1]])]
    h = x2
    saved = []
    for l in range(DEPTH):
        p = layers[l]
        z, h_t, q, qt, k, v, lse, oa, qrot, krot, vb, orr, on, got = _layer_fwd(h, p, tabs, gathers[l])
        p["wb_t"], p["w_out"] = got[0], got[1]
        if l == 0:
            layers[1]["w_in_t"] = got[2]
        last = (final_norm_g[None], target) if l == DEPTH - 1 else None
        xn, ya, yb, *loss_head = _merge_fwd(h, z, oa, on, p["wb_t"], p["w_out"], last)
        saved.append(dict(x=h, z=z, h_t=h_t, q=q, qt=qt, k=k, v=v, lse=lse, oa=oa, qrot=qrot, krot=krot, vb=vb,
                          orr=orr, on=on, ya=ya, yb=yb))
        h = xn
    dx, (d_final_g, loss_part) = h, loss_head

    grads = [None] * DEPTH
    dx, grads[1], _, _ = _layer_bwd(dx, saved[1], layers[1], tabs, lambda *a: None, False)
    g1 = grads[1]
    ex_attn = lambda d_wb_t, d_wout: _Exchange("scatter", [g1["w_in_t"], g1["wb_t"], g1["w_out"], d_wb_t, d_wout])
    dx, grads[0], recv_attn, pending = _layer_bwd(dx, saved[0], layers[0], tabs, ex_attn, True)
    recv = [None, recv_attn[3], recv_attn[4], recv_attn[0], recv_attn[1], recv_attn[2]]
    tr = lambda a: jnp.swapaxes(a, 1, 2)
    w_in_t = (tr(w_in), tr(m_w_in), tr(v_w_in))
    sharded = {}
    w_in_l1 = _sum_adamw([recv[3]], *w_in_t, 0, 256, layer0=1)
    sharded[id(w_branch_attn)] = [tr(o) for o in _sum_adamw(
        [recv[1], recv[4]], tr(w_branch_attn), tr(m_w_branch_attn), tr(v_w_branch_attn), 0, 512)]
    sharded[id(w_branch_ret)] = [tr(o) for o in _sum_adamw(
        [recv[1], recv[4]], tr(w_branch_ret), tr(m_w_branch_ret), tr(v_w_branch_ret), 512, 512)]
    sharded[id(w_out)] = _sum_adamw([recv[2], recv[5]], w_out, m_w_out, v_w_out, 0, 256)
    g_wba, g_wbr, g_wout = (sharded[id(w)][0] for w in (w_branch_attn, w_branch_ret, w_out))

    packed = jnp.zeros((8, 1024), F32)
    for l in range(DEPTH):
        gl = grads[l]
        packed = packed.at[l].set(gl["norm_g"][0])
        packed = packed.at[2, 512 * l:512 * (l + 1)].set(gl["gnw"][0])
        packed = packed.at[4, 128 * l:128 * l + 64].set(gl["qn"])
        packed = packed.at[4, 256 + 128 * l:256 + 128 * l + 64].set(gl["kn"])
        packed = packed.at[4, 512 + 128 * l:512 + 128 * l + 4].set(gl["lgf"])
        packed = packed.at[4, 768 + 128 * l:768 + 128 * l + 4].set(gl["lgb"])
    packed = packed.at[3].set(d_final_g[0])
    packed = packed.at[5, 0].set(loss_part[0, 0])
    red = _all_reduce_small(packed)
    loss = red[5, 0]
    g_norm_g = red[0:2]
    g_gnw = red[2].reshape(DEPTH, RET_WIDTH)
    g_final = red[3]
    g_qn = jnp.stack([red[4, 128 * l:128 * l + 64] for l in range(DEPTH)])
    g_kn = jnp.stack([red[4, 256 + 128 * l:256 + 128 * l + 64] for l in range(DEPTH)])
    g_lgf = jnp.stack([red[4, 512 + 128 * l:512 + 128 * l + 4] for l in range(DEPTH)])
    g_lgb = jnp.stack([red[4, 768 + 128 * l:768 + 128 * l + 4] for l in range(DEPTH)])
    g_df = g_lgf * jax.nn.sigmoid(-ret_decay_fwd)
    g_db = g_lgb * jax.nn.sigmoid(-ret_decay_bwd)

    grad_w = [g_norm_g, None, g_qn, g_kn, g_df, g_db, g_gnw, g_wba, g_wbr, g_wout, g_final]
    weights = [norm_g, w_in, attn_q_norm, attn_k_norm, ret_decay_fwd, ret_decay_bwd, ret_gn_w, w_branch_attn,
               w_branch_ret, w_out, final_norm_g]
    ms = [m_norm_g, m_w_in, m_attn_q_norm, m_attn_k_norm, m_ret_decay_fwd, m_ret_decay_bwd, m_ret_gn_w,
          m_w_branch_attn, m_w_branch_ret, m_w_out, m_final_norm_g]
    vs = [v_norm_g, v_w_in, v_attn_q_norm, v_attn_k_norm, v_ret_decay_fwd, v_ret_decay_bwd, v_ret_gn_w,
          v_w_branch_attn, v_w_branch_ret, v_w_out, v_final_norm_g]
    upd = [None if w is w_in else sharded[id(w)][1:] if id(w) in sharded else _adamw_nd(w, g, m, v)
           for w, g, m, v in zip(weights, grad_w, ms, vs)]

    done = [dx, w_in_l1[0], g_wout] + [u[0] for w, u in zip(weights, upd) if u is not None and id(w) not in sharded]
    g_full, recv[0] = _scatter_wait(*pending, done)
    mine = (4 * lax.axis_index("x") + 2 * lax.axis_index("y") + lax.axis_index("c")).astype(jnp.int32)[None]
    w_in_upd = [tr(o) for o in _sum_adamw([recv[0]], *w_in_t, 0, 256, layer0=0, prev=w_in_l1, own=(g_full, mine))]
    grad_w[1], upd[1] = w_in_upd[0], w_in_upd[1:]
    return (loss, dx[None], *grad_w, *[u[0] for u in upd], *[u[1] for u in upd], *[u[2] for u in upd])
```

```python
import functools

import jax
import jax.numpy as jnp
from jax import lax
from jax.experimental import pallas as pl
from jax.experimental.pallas import tpu as pltpu

F32 = jnp.float32
BF16 = jnp.bfloat16
SDS = jax.ShapeDtypeStruct

D_MODEL = 1024
DEPTH = 2
GRID_W = 64
ATTN_Q_HEADS = 8
ATTN_KV_HEADS = 2
ATTN_HEAD_DIM = 64
ATTN_WIDTH = 512
ATTN_KV_WIDTH = 128
RET_HEADS = 4
RET_HEAD_DIM = 128
RET_WIDTH = 512
RET_CHUNK = 128
ATTN_KEY_CHUNK = 512
ATTN_BWD_KEY_CHUNK = 512
ATTN_BWD_QUERY_TILE = 1024
ATTN_FWD_QUERY_TILE = 512
QK_DOTS_PER_CHUNK = 1
EXP_LAG = 3
ROPE_THETA = 10000.0
EPS = 1e-6
D_IN = 5376
N_DEV = 8

ADAM_LR = 0.001
ADAM_B1 = 0.9
ADAM_B2 = 0.999
ADAM_EPS = 1e-08
ADAM_WD = 0.01
ADAM_STEP = 10

SEG = {
    "qa": (0, 512, 0),
    "ga": (768, 512, 512),
    "qr": (1280, 512, 1024),
    "kr": (1792, 512, 1536),
    "vr": (2304, 512, 2048),
    "gr": (2816, 512, 2560),
    "gm": (3328, 2048, 3072),
    "ka": (512, 128, 5120),
    "va": (640, 128, 5248),
}

VMEM_LIMIT = 60 * 1024 * 1024
NT = (((1,), (1,)), ((), ()))
TN = (((0,), (0,)), ((), ()))
MESH_ID = pl.DeviceIdType.MESH
ANY = pl.BlockSpec(memory_space=pl.ANY)


def _params(sem=None, vmem=VMEM_LIMIT):
    return pltpu.CompilerParams(dimension_semantics=sem, vmem_limit_bytes=vmem)


def _dot(a, b, dims=None):
    if dims is None:
        return jnp.dot(a, b, preferred_element_type=F32)
    return lax.dot_general(a, b, dims, preferred_element_type=F32)


def _sigmoid(x):
    return 1.0 / (1.0 + jnp.exp(-x))


def _swap_halves(x, q):
    n = x.shape[-1]
    axis = x.ndim - 1
    lane = lax.broadcasted_iota(jnp.int32, x.shape, axis)
    first = (lane % (2 * q)) < q
    return jnp.where(first, pltpu.roll(x, n - q, axis), pltpu.roll(x, q, axis))


def _rope(x, cos, sin_signed, q):
    return x * cos + _swap_halves(x, q) * sin_signed


def _rope_bwd(dy, cos, sin_signed, q):
    return dy * cos - _swap_halves(dy, q) * sin_signed


def _group_mean(v, ones_bd):
    hi = v.astype(BF16)
    lo = (v - hi.astype(F32)).astype(BF16)
    return _dot(hi, ones_bd) + _dot(lo, ones_bd)


def _rope_tables(t, head_dim):
    n_rows = t // GRID_W
    d_axis = head_dim // 2
    inv_freq = ROPE_THETA ** (-jnp.arange(0, d_axis, 2, dtype=F32) / d_axis)
    ar = jnp.arange(n_rows, dtype=F32)[:, None] * inv_freq
    ac = jnp.arange(GRID_W, dtype=F32)[:, None] * inv_freq
    by_row = lambda a: jnp.repeat(a, GRID_W, axis=0)
    by_col = lambda a: jnp.tile(a, (n_rows, 1))
    cr, sr, cc, sc = by_row(jnp.cos(ar)), by_row(jnp.sin(ar)), by_col(jnp.cos(ac)), by_col(jnp.sin(ac))
    return jnp.concatenate([cr, cr, cc, cc], axis=-1), jnp.concatenate([-sr, sr, -sc, sc], axis=-1)


def _me():
    return lax.axis_index("x"), lax.axis_index("y"), lax.axis_index("c")


def _flip(k):
    x, y, c = _me()
    px = 1 - x if k & 4 else x
    py = 1 - y if k & 2 else y
    pc = 1 - c if k & 1 else c
    return (px, py, pc), 4 * px + 2 * py + pc


class _Exchange:
    def __init__(self, kind, srcs):
        self.kind, self.srcs, self.n = kind, list(srcs), len(srcs)
        self.rows = [a.shape[0] if kind == "gather" else a.shape[0] // N_DEV for a in srcs]
        if kind == "gather":
            self.out_shape = [SDS((N_DEV * a.shape[0], a.shape[1]), a.dtype) for a in srcs]
        else:
            self.out_shape = [SDS((N_DEV, a.shape[0] // N_DEV, a.shape[1]), a.dtype) for a in srcs]
        self.scratch = [pltpu.SemaphoreType.DMA((self.n, N_DEV - 1)), pltpu.SemaphoreType.DMA((self.n, N_DEV - 1)),
                        pltpu.SemaphoreType.DMA((self.n,))]

    def _block(self, ref, a, idx):
        r = self.rows[a]
        return ref.at[pl.ds(pl.multiple_of(idx * r, 16), r), :]

    def _src(self, ins, a, idx):
        return ins[a] if self.kind == "gather" else self._block(ins[a], a, idx)

    def _dst(self, outs, a, idx):
        return self._block(outs[a], a, idx) if self.kind == "gather" else outs[a].at[idx]

    def _copies(self, ins, outs, sems):
        send_sems, recv_sems, local_sems = sems
        me, mine = _flip(0)
        local, sends, recvs = [], [], []
        for a in range(self.n):
            local.append(pltpu.make_async_copy(self._src(ins, a, mine), self._dst(outs, a, mine), local_sems.at[a]))
            for k in range(1, N_DEV):
                peer, theirs = _flip(k)
                sem = dict(send_sem=send_sems.at[a, k - 1], recv_sem=recv_sems.at[a, k - 1])
                sends.append(pltpu.make_async_remote_copy(
                    src_ref=self._src(ins, a, theirs), dst_ref=self._dst(outs, a, mine),
                    device_id=peer, device_id_type=MESH_ID, **sem))
                recvs.append(pltpu.make_async_remote_copy(
                    src_ref=self._dst(outs, a, theirs), dst_ref=self._dst(outs, a, theirs),
                    device_id=me, device_id_type=MESH_ID, **sem))
        return local, sends, recvs

    def start(self, ins, outs, sems):
        local, sends, _ = self._copies(ins, outs, sems)
        for cp in local + sends:
            cp.start()

    def wait(self, ins, outs, sems):
        local, sends, recvs = self._copies(ins, outs, sems)
        for cp in sends:
            cp.wait_send()
        for cp in recvs:
            cp.wait_recv()
        for cp in local:
            cp.wait()


def _with_exchange(body, n_in, n_out, n_scratch, ex, first, last):
    if ex is None:
        return body

    def wrapped(*refs):
        ins = refs[:n_in]
        ex_ins = refs[n_in:n_in + ex.n]
        outs = refs[n_in + ex.n:n_in + ex.n + n_out]
        ex_outs = refs[n_in + ex.n + n_out:n_in + 2 * ex.n + n_out]
        rest = refs[n_in + 2 * ex.n + n_out:]
        scratch, sems = rest[:n_scratch], rest[n_scratch:]

        @pl.when(first())
        def _():
            ex.start(ex_ins, ex_outs, sems)

        body(*ins, *outs, *scratch)

        @pl.when(last())
        def _():
            ex.wait(ex_ins, ex_outs, sems)

    return wrapped


def _ex_args(ex):
    if ex is None:
        return [], [], [], [], []
    return [ANY] * ex.n, [ANY] * ex.n, list(ex.out_shape), list(ex.scratch), list(ex.srcs)


def _in_proj(x, g, w_t, qn, kn, cos, sin, ones_bd, cos_r, sin_r):
    t, d = x.shape
    tm = min(256, t)
    tk = min(ATTN_KEY_CHUNK, t)
    per_chunk = tk // tm
    hd = ATTN_HEAD_DIM

    def body(x_ref, g_ref, w_ref, qn_ref, kn_ref, c_ref, s_ref, b_ref, cr_ref, sr_ref,
             z_ref, ht_ref, q_out, qt_out, k_out, v_out, vt_out, qr_out, kr_out, vr_out):
        xv = x_ref[...]
        r = lax.rsqrt(jnp.mean(xv * xv, axis=-1, keepdims=True) + EPS)
        h = xv * r * g_ref[...]
        ht_ref[...] = h.T.astype(BF16)
        hb = h.astype(BF16)
        def project(name):
            nat, w, off = SEG[name]
            zs = _dot(hb, w_ref[nat:nat + w, :], NT)
            z_ref[:, off:off + w] = zs
            return zs

        seg = {name: project(name) for name in ("qa", "ka", "va")}
        bd = b_ref[...]
        c2, s2 = c_ref[...], s_ref[...]
        cq = jnp.concatenate([c2] * 4, axis=-1)
        sq = jnp.concatenate([s2] * 4, axis=-1)
        xq, xk, xvv = seg["qa"], seg["ka"], seg["va"]
        yq = xq * lax.rsqrt(_group_mean(xq * xq, bd) + EPS) * qn_ref[...]
        yq = _rope(yq, cq, sq, hd // 4) * (hd ** -0.5)
        yqt = yq.T
        for hh in range(ATTN_Q_HEADS):
            q_out[hh] = yq[:, hh * hd:(hh + 1) * hd].astype(BF16)
            qt_out[hh] = yqt[hh * hd:(hh + 1) * hd, :].astype(BF16)
        yk = xk * lax.rsqrt(_group_mean(xk * xk, bd[:ATTN_KV_WIDTH, :ATTN_KV_WIDTH]) + EPS) * kn_ref[...]
        yk = _rope(yk, c2, s2, hd // 4)
        xvt = xvv.T
        ones = jnp.ones((hd, tm), F32)
        for hh in range(ATTN_KV_HEADS):
            k_out[hh] = yk[:, hh * hd:(hh + 1) * hd].astype(BF16)
            v_out[hh] = xvv[:, hh * hd:(hh + 1) * hd].astype(BF16)
            vt_out[hh, 0] = jnp.concatenate([xvt[hh * hd:(hh + 1) * hd, :], ones], axis=0).astype(BF16)
        rd = RET_HEAD_DIM
        cr = jnp.concatenate([cr_ref[...]] * RET_HEADS, axis=-1)
        sr = jnp.concatenate([sr_ref[...]] * RET_HEADS, axis=-1)
        qr_out[...] = _rope(project("qr"), cr, sr, rd // 4).astype(BF16)
        kr_out[...] = (_rope(project("kr"), cr, sr, rd // 4) * (rd ** -0.5)).astype(BF16)
        vr_out[...] = project("vr").astype(BF16)
        for name in ("ga", "gr", "gm"):
            project(name)

    const = lambda shape: pl.BlockSpec(shape, lambda i: (0,) * len(shape))
    rows = lambda w: pl.BlockSpec((tm, w), lambda i: (i, 0))
    return pl.pallas_call(
        body, name="in_proj", grid=(t // tm,),
        in_specs=[rows(d), const((1, d)), const((D_IN, d)), const((1, 512)), const((1, 128)), rows(128), rows(128),
                  const((512, 512)), rows(128), rows(128)],
        out_specs=[rows(D_IN), pl.BlockSpec((d, tm), lambda i: (0, i)),
                   pl.BlockSpec((ATTN_Q_HEADS, tm, hd), lambda i: (0, i, 0)),
                   pl.BlockSpec((ATTN_Q_HEADS, hd, tm), lambda i: (0, 0, i)),
                   pl.BlockSpec((ATTN_KV_HEADS, tm, hd), lambda i: (0, i, 0)),
                   pl.BlockSpec((ATTN_KV_HEADS, tm, hd), lambda i: (0, i, 0)),
                   pl.BlockSpec((ATTN_KV_HEADS, 1, 2 * hd, tm), lambda i: (0, i // per_chunk, 0, i % per_chunk)),
                   rows(RET_WIDTH), rows(RET_WIDTH), rows(RET_WIDTH)],
        out_shape=[SDS((t, D_IN), F32), SDS((d, t), BF16),
                   SDS((ATTN_Q_HEADS, t, hd), BF16), SDS((ATTN_Q_HEADS, hd, t), BF16),
                   SDS((ATTN_KV_HEADS, t, hd), BF16), SDS((ATTN_KV_HEADS, t, hd), BF16),
                   SDS((ATTN_KV_HEADS, t // tk, 2 * hd, tk), BF16)] + [SDS((t, RET_WIDTH), BF16)] * 3,
        compiler_params=_params(("parallel",)),
    )(x, g, w_t, qn, kn, cos, sin, ones_bd, cos_r, sin_r)


def _attn_fwd(q, k, vt, ex=None):
    t = q.shape[1]
    tq = min(ATTN_FWD_QUERY_TILE, t)
    nk, tk = vt.shape[1], vt.shape[3]
    hd = ATTN_HEAD_DIM
    g = ATTN_Q_HEADS // ATTN_KV_HEADS

    def body(q_ref, k_ref, vt_ref, o_ref, lse_ref, s_scr):
        def pass_a(h, c, m8):
            part = tk // QK_DOTS_PER_CHUNK
            for lo in range(c * tk, (c + 1) * tk, part):
                st = _dot(k_ref[0, lo:lo + part, :], q_ref[h], NT)
                s_scr[h % 2, lo:lo + part, :] = st
                m8 = jnp.maximum(m8, jnp.max(st.reshape(part // 8, 8, tq), axis=0))
            return m8

        def pass_b(h, c, m, acc, after):
            e = jnp.exp(s_scr[h % 2, c * tk:(c + 1) * tk, :] - (m + after * 0.0)).astype(BF16)
            return acc + _dot(vt_ref[0, c], e)

        neg = jnp.full((8, tq), -jnp.inf, F32)
        m8 = neg
        for c in range(nk):
            m8 = pass_a(0, c, m8)
        outs = []
        for h in range(g):
            m = jnp.max(m8, axis=0, keepdims=True)
            acc = jnp.zeros((2 * hd, tq), F32)
            m8 = neg
            done = [m] * EXP_LAG
            for c in range(nk):
                if h + 1 < g:
                    m8 = pass_a(h + 1, c, m8)
                acc = pass_b(h, c, m, acc, done[-EXP_LAG])
                done.append(m8[0:1, :] if h + 1 < g else acc[hd:hd + 1, :])
            l = acc[hd:hd + 1, :]
            outs.append((acc[:hd, :] / l).T)
            lse_ref[h] = m + jnp.log(l)
        o_ref[...] = jnp.concatenate(outs, axis=-1)

    nq = t // tq
    first = lambda: jnp.logical_and(pl.program_id(0) == 0, pl.program_id(1) == 0)
    last = lambda: jnp.logical_and(pl.program_id(0) == ATTN_KV_HEADS - 1, pl.program_id(1) == nq - 1)
    xi, xo, xs, xscr, xargs = _ex_args(ex)
    return pl.pallas_call(
        _with_exchange(body, 3, 2, 1, ex, first, last), name="attn_fwd", grid=(ATTN_KV_HEADS, nq),
        in_specs=[pl.BlockSpec((g, tq, hd), lambda p, i: (p, i, 0)),
                  pl.BlockSpec((1, t, hd), lambda p, i: (p, 0, 0)),
                  pl.BlockSpec((1, nk, 2 * hd, tk), lambda p, i: (p, 0, 0, 0))] + xi,
        out_specs=[pl.BlockSpec((tq, g * hd), lambda p, i: (i, p)),
                   pl.BlockSpec((g, 1, tq), lambda p, i: (p, 0, i))] + xo,
        out_shape=[SDS((t, ATTN_WIDTH), F32), SDS((ATTN_Q_HEADS, 1, t), F32)] + xs,
        scratch_shapes=[pltpu.VMEM((2, t, tq), F32)] + xscr,
        compiler_params=_params(("arbitrary", "arbitrary")),
    )(q, k, vt, *xargs)


class _Dir:
    def __init__(self, lg, strict_future):
        c = RET_CHUNK
        ia = lax.broadcasted_iota(jnp.int32, (c, c), 0).astype(F32)
        ib = lax.broadcasted_iota(jnp.int32, (c, c), 1).astype(F32)
        col = lax.broadcasted_iota(jnp.int32, (c, 1), 0).astype(F32)
        row = lax.broadcasted_iota(jnp.int32, (1, c), 1).astype(F32)
        if strict_future:
            dist = ib - ia
            mask = dist > 0
            self.wq, self.wk, wk_row = c - col, col, row
        else:
            dist = ia - ib
            mask = dist >= 0
            self.wq, self.wk, wk_row = col + 1.0, c - 1.0 - col, c - 1.0 - row
        self.dist = jnp.maximum(dist, 0.0)
        self.d = jnp.where(mask, jnp.exp(self.dist * lg), 0.0)
        self.qd = jnp.exp(self.wq * lg)
        self.kd_col = jnp.exp(self.wk * lg)
        self.kd_row = jnp.exp(wk_row * lg)
        self.cd = jnp.exp(jnp.full((1, 1), float(c), F32) * lg)


def _ret_fwd(qrot, krot, vb, lgf, lgb, gnw):
    t = qrot.shape[0]
    c = RET_CHUNK
    nc = t // c
    hd = RET_HEAD_DIM
    unroll = 4 if nc % 4 == 0 else 1

    def body(lgf_ref, lgb_ref, qo_ref, ko_ref, vo_ref, w_ref, orr_ref, on_ref, kt, uf, ub, sfa, sba):
        h = pl.program_id(0)
        fw = _Dir(lgf_ref[h], False)
        bw = _Dir(lgb_ref[h], True)
        for i in range(nc):
            kt[i] = ko_ref[i * c:(i + 1) * c, :].astype(F32).T.astype(BF16)

        def rows(ci):
            return pl.ds(pl.multiple_of(ci * c, c), c)

        def kv_products(ci, carry):
            vv = vo_ref[rows(ci), :]
            ktf = kt[ci].astype(F32)
            uf[ci] = _dot((ktf * fw.kd_row).astype(BF16), vv)
            ub[ci] = _dot((ktf * bw.kd_row).astype(BF16), vv)
            return carry

        lax.fori_loop(0, nc, kv_products, 0, unroll=16 if nc % 16 == 0 else unroll)

        def scan(i, carry):
            sf, sb = carry
            j = nc - 1 - i
            sfa[i] = sf.astype(BF16)
            sba[j] = sb.astype(BF16)
            return sf * fw.cd + uf[i], sb * bw.cd + ub[j]

        zero = jnp.zeros((hd, hd), F32)
        lax.fori_loop(0, nc, scan, (zero, zero))
        gw = w_ref[...]

        def outputs(ci, carry):
            sl = rows(ci)
            qq, kk, vv = qo_ref[sl, :], ko_ref[sl, :], vo_ref[sl, :]
            a = _dot(qq, kk, NT)
            o = (_dot((a * fw.d).astype(BF16), vv) + _dot(qq, sfa[ci]) * fw.qd
                 + _dot((a * bw.d).astype(BF16), vv) + _dot(qq, sba[ci]) * bw.qd)
            orr_ref[sl, :] = o
            xc = o - jnp.mean(o, axis=-1, keepdims=True)
            var = jnp.mean(xc * xc, axis=-1, keepdims=True)
            on_ref[sl, :] = xc * lax.rsqrt(var + EPS) * gw
            return carry

        group = 32 if nc % 32 == 0 else 1

        def output_group(i, carry):
            for j in range(group):
                outputs(i * group + j, carry)
            return carry

        lax.fori_loop(0, nc // group, output_group, 0)

    smem = pl.BlockSpec(memory_space=pltpu.SMEM)
    head = pl.BlockSpec((t, 128), lambda h: (0, h))
    return pl.pallas_call(
        body, name="ret_fwd", grid=(RET_HEADS,),
        in_specs=[smem, smem, head, head, head, pl.BlockSpec((1, 128), lambda h: (0, h))],
        out_specs=[head, head],
        out_shape=[SDS((t, RET_WIDTH), F32)] * 2,
        scratch_shapes=[pltpu.VMEM((nc, hd, c), BF16), pltpu.VMEM((nc, hd, hd), F32), pltpu.VMEM((nc, hd, hd), F32),
                        pltpu.VMEM((nc, hd, hd), BF16), pltpu.VMEM((nc, hd, hd), BF16)],
        compiler_params=_params(("parallel",)),
    )(lgf, lgb, qrot, krot, vb, gnw)


def _merge_fwd(x, z, oa, on, wb_t, wout, head=None):
    t, d = x.shape
    tm = min(256, t)
    n = t // tm

    def body(x_ref, ga_ref, gr_ref, gm0_ref, gm1_ref, oa_ref, on_ref, wb_ref, wo_ref, *rest):
        ga, gr = ga_ref[...], gr_ref[...]
        ua = ga * _sigmoid(ga) * oa_ref[...]
        ub = gr * _sigmoid(gr) * on_ref[...]
        ya = _dot(ua.astype(BF16), wb_ref[:, :512], NT)
        yb = _dot(ub.astype(BF16), wb_ref[:, 512:], NT)
        merged = _sigmoid(gm0_ref[...]) * ya + _sigmoid(gm1_ref[...]) * yb
        xn = x_ref[...] + _dot(merged.astype(BF16), wo_ref[...])
        if head is None:
            xn_ref, ya_ref, yb_ref = rest
            xn_ref[...] = xn
        else:
            g_ref, t_ref, dx_ref, ya_ref, yb_ref, dg_ref, loss_ref, acc_g, acc_l = rest
            i = pl.program_id(0)

            @pl.when(i == 0)
            def _():
                acc_g[...] = jnp.zeros_like(acc_g)
                acc_l[...] = jnp.zeros_like(acc_l)

            gv = g_ref[...]
            r = lax.rsqrt(jnp.mean(xn * xn, axis=-1, keepdims=True) + EPS)
            xh = xn * r
            err = xh * gv - t_ref[...]
            dy = err * (1.0 / d)
            gy = dy * gv
            dx_ref[...] = r * (gy - xh * jnp.mean(gy * xh, axis=-1, keepdims=True))
            acc_g[...] += jnp.sum((dy * xh).reshape(tm // 8, 8, d), axis=0)
            acc_l[...] += jnp.sum((err * err).reshape(tm // 8, 8, d), axis=0)

            @pl.when(i == n - 1)
            def _():
                dg_ref[...] = jnp.sum(acc_g[...], axis=0, keepdims=True)
                tot = jnp.sum(jnp.sum(acc_l[...], axis=0, keepdims=True), axis=1, keepdims=True)
                loss_ref[...] = jnp.broadcast_to(tot * (0.5 / d), (1, 128))
        ya_ref[...] = ya.astype(BF16)
        yb_ref[...] = yb.astype(BF16)

    row = lambda w, j: pl.BlockSpec((tm, w), lambda i: (i, j))
    const = lambda shape: pl.BlockSpec(shape, lambda i: (0, 0))
    in_specs = [row(d, 0), row(512, SEG["ga"][2] // 512), row(512, SEG["gr"][2] // 512),
                row(1024, SEG["gm"][2] // 1024), row(1024, SEG["gm"][2] // 1024 + 1),
                row(512, 0), row(512, 0), const((d, 1024)), const((d, d))]
    out_specs = [row(d, 0), row(d, 0), row(d, 0)]
    out_shape = [SDS((t, d), F32), SDS((t, d), BF16), SDS((t, d), BF16)]
    args, scratch = [x, z, z, z, z, oa, on, wb_t, wout], []
    if head is not None:
        in_specs += [const((1, d)), row(d, 0)]
        out_specs += [const((1, d)), const((1, 128))]
        out_shape += [SDS((1, d), F32), SDS((1, 128), F32)]
        args += list(head)
        scratch = [pltpu.VMEM((8, d), F32), pltpu.VMEM((8, d), F32)]
    return pl.pallas_call(
        body, name="merge_fwd", grid=(n,), in_specs=in_specs, out_specs=out_specs, out_shape=out_shape,
        scratch_shapes=scratch,
        compiler_params=_params(("arbitrary",) if head is not None else ("parallel",)),
    )(*args)


def _merge_bwd(dxo, z, oa, on, ya, yb, wb_t, wout):
    t, d = dxo.shape
    tm = min(256, t)
    n = t // tm

    def body(dx_ref, ga_ref, gr_ref, gm0_ref, gm1_ref, oa_ref, on_ref, ya_ref, yb_ref, wb_ref, wo_ref,
             doa_ref, don_ref, dz_ref, dwo_ref, dwb_ref, acc_o, acc_b):
        i = pl.program_id(0)

        @pl.when(i == 0)
        def _():
            acc_o[...] = jnp.zeros_like(acc_o)
            acc_b[...] = jnp.zeros_like(acc_b)

        dxb = dx_ref[...].astype(BF16)
        ya, yb = ya_ref[...].astype(F32), yb_ref[...].astype(F32)
        g0, g1 = _sigmoid(gm0_ref[...]), _sigmoid(gm1_ref[...])
        mb = (g0 * ya + g1 * yb).astype(BF16)
        dm = _dot(dxb, wo_ref[...], NT)
        dya = (dm * g0).astype(BF16)
        dyb = (dm * g1).astype(BF16)
        dz_ref[:, 1024:2048] = (dm * ya * g0 * (1.0 - g0)).astype(BF16)
        dz_ref[:, 2048:3072] = (dm * yb * g1 * (1.0 - g1)).astype(BF16)

        def branch(g_ref, o_ref, dy, w, do_ref, lo):
            gv, ov = g_ref[...], o_ref[...]
            sg = _sigmoid(gv)
            silu = gv * sg
            du = _dot(dy, w)
            do_ref[...] = du * silu
            dz_ref[:, lo:lo + 512] = (du * ov * (sg * (1.0 + gv * (1.0 - sg)))).astype(BF16)
            acc_b[:, lo:lo + 512] += _dot(dy, (silu * ov).astype(BF16), TN)

        branch(ga_ref, oa_ref, dya, wb_ref[:, :512], doa_ref, 0)
        branch(gr_ref, on_ref, dyb, wb_ref[:, 512:], don_ref, 512)
        acc_o[...] += _dot(mb, dxb, TN)

        @pl.when(i == n - 1)
        def _():
            dwo_ref[...] = acc_o[...].astype(BF16)
            dwb_ref[...] = acc_b[...].astype(BF16)

    row = lambda w, j: pl.BlockSpec((tm, w), lambda i: (i, j))
    const = lambda shape: pl.BlockSpec(shape, lambda i: (0, 0))
    return pl.pallas_call(
        body, name="merge_bwd", grid=(n,),
        in_specs=[row(d, 0), row(512, SEG["ga"][2] // 512), row(512, SEG["gr"][2] // 512),
                  row(1024, SEG["gm"][2] // 1024), row(1024, SEG["gm"][2] // 1024 + 1),
                  row(512, 0), row(512, 0), row(d, 0), row(d, 0), const((d, 1024)), const((d, d))],
        out_specs=[row(512, 0), row(512, 0), row(3072, 0), const((d, d)), const((d, 1024))],
        out_shape=[SDS((t, 512), F32), SDS((t, 512), F32), SDS((t, 3072), BF16), SDS((d, d), BF16),
                   SDS((d, 1024), BF16)],
        scratch_shapes=[pltpu.VMEM((d, d), F32), pltpu.VMEM((d, 1024), F32)],
        compiler_params=_params(("arbitrary",)),
    )(dxo, z, z, z, z, oa, on, ya, yb, wb_t, wout)


def _ret_bwd(qrot, krot, vb, orr, don, gnw, lgf, lgb, cos, sin):
    t = qrot.shape[0]
    c = RET_CHUNK
    nc = t // c
    hd = RET_HEAD_DIM
    unroll = 4 if nc % 4 == 0 else 1

    def body(lgf_ref, lgb_ref, q_ref, k_ref, v_ref, o_ref, dn_ref, w_ref, c_ref, s_ref,
             dq_ref, dk_ref, dv_ref, dw_ref, dlf_ref, dlb_ref, qt, kt, dob, uf, ub, wf, wb, sfa, sba, gfa, gba):
        h = pl.program_id(0)
        fw = _Dir(lgf_ref[h], False)
        bw = _Dir(lgb_ref[h], True)
        fw.dt, bw.dt = fw.d.T, bw.d.T

        o = o_ref[...]
        xc = o - jnp.mean(o, axis=-1, keepdims=True)
        r = lax.rsqrt(jnp.mean(xc * xc, axis=-1, keepdims=True) + EPS)
        xh = xc * r
        dn = dn_ref[...]
        gy = dn * w_ref[...]
        d_o = r * (gy - jnp.mean(gy, axis=-1, keepdims=True) - xh * jnp.mean(gy * xh, axis=-1, keepdims=True))
        dw_ref[...] = jnp.sum(dn * xh, axis=0, keepdims=True)
        dob[...] = d_o.astype(BF16)
        for i in range(nc):
            qt[i] = q_ref[i * c:(i + 1) * c, :].astype(F32).T.astype(BF16)
            kt[i] = k_ref[i * c:(i + 1) * c, :].astype(F32).T.astype(BF16)

        def rows(ci):
            return pl.ds(pl.multiple_of(ci * c, c), c)

        def products(ci, carry):
            sl = rows(ci)
            vv, do32 = v_ref[sl, :], dob[sl, :].astype(F32)
            ktf = kt[ci].astype(F32)
            uf[ci] = _dot((ktf * fw.kd_row).astype(BF16), vv)
            ub[ci] = _dot((ktf * bw.kd_row).astype(BF16), vv)
            wf[ci] = _dot(qt[ci], (do32 * fw.qd).astype(BF16))
            wb[ci] = _dot(qt[ci], (do32 * bw.qd).astype(BF16))
            return carry

        lax.fori_loop(0, nc, products, 0, unroll=16 if nc % 16 == 0 else unroll)

        def scan(i, carry):
            sf, sb, gf, gb = carry
            j = nc - 1 - i
            sfa[i] = sf.astype(BF16)
            sba[j] = sb.astype(BF16)
            gfa[j] = gf.astype(BF16)
            gba[i] = gb.astype(BF16)
            return sf * fw.cd + uf[i], sb * bw.cd + ub[j], gf * fw.cd + wf[j], gb * bw.cd + wb[i]

        zero = jnp.zeros((hd, hd), F32)
        lax.fori_loop(0, nc, scan, (zero, zero, zero, zero))

        def one_dir(p, s_all, g_all, ci, qq, kk, vv, do, a, bm):
            sb, gb = s_all[ci], g_all[ci]
            doq = (do.astype(F32) * p.qd).astype(BF16)
            dqc = _dot(doq, sb, NT)
            kkd = (kk.astype(F32) * p.kd_col).astype(BF16)
            dk2 = _dot(vv, gb, NT) * p.kd_col
            terms = (p.dist * p.d * a * bm + p.wq * qq.astype(F32) * dqc + p.wk * kk.astype(F32) * dk2
                     + (float(c) * p.cd) * gb.astype(F32) * sb.astype(F32))
            return dqc, dk2, _dot(kkd, gb), terms

        d_both, dt_both = fw.d + bw.d, fw.dt + bw.dt

        def chunk(ci, carry):
            af, ab = carry
            sl = rows(ci)
            qq, kk, vv, do = q_ref[sl, :], k_ref[sl, :], v_ref[sl, :], dob[sl, :]
            a, bm = _dot(qq, kk, NT), _dot(do, vv, NT)
            at, bt = _dot(kk, qq, NT), _dot(vv, do, NT)
            dqf, dkf, dvf, tf = one_dir(fw, sfa, gfa, ci, qq, kk, vv, do, a, bm)
            dqb, dkb, dvb, tb = one_dir(bw, sba, gba, ci, qq, kk, vv, do, a, bm)
            cc, ss = c_ref[sl, :], s_ref[sl, :]
            dq = _dot((bm * d_both).astype(BF16), kk) + dqf + dqb
            dk = _dot((bt * dt_both).astype(BF16), qq) + dkf + dkb
            dq_ref[sl, :] = _rope_bwd(dq, cc, ss, hd // 4).astype(BF16)
            dk_ref[sl, :] = (_rope_bwd(dk, cc, ss, hd // 4) * (hd ** -0.5)).astype(BF16)
            dv_ref[sl, :] = (_dot((at * dt_both).astype(BF16), do) + dvf + dvb).astype(BF16)
            return af + tf, ab + tb

        pair = 16 if nc % 16 == 0 else 1

        def chunks(i, carry):
            for j in range(pair):
                carry = chunk(i * pair + j, carry)
            return carry

        af, ab = lax.fori_loop(0, nc // pair, chunks, (zero, zero))
        tot = lambda m: jnp.sum(jnp.sum(m, axis=0, keepdims=True), axis=1, keepdims=True)
        dlf_ref[...] = jnp.broadcast_to(tot(af).reshape(1, 1, 1), (1, 8, 128))
        dlb_ref[...] = jnp.broadcast_to(tot(ab).reshape(1, 1, 1), (1, 8, 128))

    smem = pl.BlockSpec(memory_space=pltpu.SMEM)
    head = pl.BlockSpec((t, 128), lambda h: (0, h))
    vec = pl.BlockSpec((1, 128), lambda h: (0, h))
    scal = pl.BlockSpec((1, 8, 128), lambda h: (h, 0, 0))
    table = pl.BlockSpec((t, 128), lambda h: (0, 0))
    mats = lambda dt: pltpu.VMEM((nc, hd, hd), dt)
    return pl.pallas_call(
        body, name="ret_bwd", grid=(RET_HEADS,),
        in_specs=[smem, smem, head, head, head, head, head, vec, table, table],
        out_specs=[head, head, head, vec, scal, scal],
        out_shape=[SDS((t, RET_WIDTH), BF16)] * 3 + [SDS((1, RET_WIDTH), F32), SDS((RET_HEADS, 8, 128), F32),
                                                    SDS((RET_HEADS, 8, 128), F32)],
        scratch_shapes=[pltpu.VMEM((nc, hd, c), BF16), pltpu.VMEM((nc, hd, c), BF16), pltpu.VMEM((t, hd), BF16),
                        mats(F32), mats(F32), mats(F32), mats(F32), mats(BF16), mats(BF16), mats(BF16), mats(BF16)],
        compiler_params=_params(("parallel",)),
    )(lgf, lgb, qrot, krot, vb, orr, don, gnw, cos, sin)


def _attn_bwd(q, qt, k, v, doa, oa, lse, ex=None):
    t = q.shape[1]
    tq = min(ATTN_BWD_QUERY_TILE, t)
    nq = t // tq
    tk = min(ATTN_BWD_KEY_CHUNK, t)
    nk = t // tk
    hd = ATTN_HEAD_DIM
    scale = hd ** -0.5

    def body(q_ref, qt_ref, k_ref, v_ref, do_ref, o_ref, lse_ref, dq_ref, dkt_ref, dvt_ref):
        p, i = pl.program_id(0), pl.program_id(1)

        @pl.when(jnp.logical_and(p % 2 == 0, i == 0))
        def _():
            dkt_ref[...] = jnp.zeros_like(dkt_ref)
            dvt_ref[...] = jnp.zeros_like(dvt_ref)

        dov, ov = do_ref[...], o_ref[...]
        dovt = dov.T
        lanes = lambda col: jnp.concatenate([col] * (tk // 128), axis=1)
        outs = []
        for j in range(2):
            qq, qqt = q_ref[j], qt_ref[j]
            do32 = dov[:, j * hd:(j + 1) * hd]
            do, dot_ = do32.astype(BF16), dovt[j * hd:(j + 1) * hd, :].astype(BF16)
            dd = lanes(jnp.broadcast_to(jnp.sum(do32 * ov[:, j * hd:(j + 1) * hd], axis=1, keepdims=True), (tq, 128)))
            lse_j = lanes(jnp.broadcast_to(lse_ref[j], (128, tq)).T)
            dq = jnp.zeros((tq, hd), F32)
            for c in range(nk):
                sl = slice(c * tk, (c + 1) * tk)
                kc, vc = k_ref[0, sl, :], v_ref[0, sl, :]
                pr = jnp.exp(_dot(qq, kc, NT) - lse_j)
                ds = (pr * (_dot(do, vc, NT) - dd)).astype(BF16)
                dvt_ref[0, :, sl] += _dot(dot_, pr.astype(BF16))
                dkt_ref[0, :, sl] += _dot(qqt, ds)
                dq = dq + _dot(ds, kc)
            outs.append(dq * scale)
        dq_ref[...] = jnp.concatenate(outs, axis=-1)

    kv = pl.BlockSpec((1, t, hd), lambda p, i: (p // 2, 0, 0))
    kvt = pl.BlockSpec((1, hd, t), lambda p, i: (p // 2, 0, 0))
    pair = pl.BlockSpec((tq, 128), lambda p, i: (i, p))
    first = lambda: jnp.logical_and(pl.program_id(0) == 0, pl.program_id(1) == 0)
    last = lambda: jnp.logical_and(pl.program_id(0) == 3, pl.program_id(1) == nq - 1)
    xi, xo, xs, xscr, xargs = _ex_args(ex)
    return pl.pallas_call(
        _with_exchange(body, 7, 3, 0, ex, first, last), name="attn_bwd", grid=(4, nq),
        in_specs=[pl.BlockSpec((2, tq, hd), lambda p, i: (p, i, 0)), pl.BlockSpec((2, hd, tq), lambda p, i: (p, 0, i)),
                  kv, kv, pair, pair, pl.BlockSpec((2, 1, tq), lambda p, i: (p, 0, i))] + xi,
        out_specs=[pair, kvt, kvt] + xo,
        out_shape=[SDS((t, ATTN_WIDTH), F32), SDS((ATTN_KV_HEADS, hd, t), F32),
                   SDS((ATTN_KV_HEADS, hd, t), F32)] + xs,
        scratch_shapes=xscr,
        compiler_params=_params(("arbitrary", "arbitrary")),
    )(q, qt, k, v, doa, oa, lse, *xargs)


def _attn_post_bwd(dq, dk, dv, z, qn, kn, cos, sin, ones_bd):
    t = z.shape[0]
    tm = min(512, t)
    n = t // tm
    hd = ATTN_HEAD_DIM

    def body(dq_ref, dk_ref, dv_ref, zq_ref, zkv_ref, qn_ref, kn_ref, c_ref, s_ref, b_ref,
             dz_ref, dqn_ref, dkn_ref, acc_q, acc_k):
        i = pl.program_id(0)

        @pl.when(i == 0)
        def _():
            acc_q[...] = jnp.zeros_like(acc_q)
            acc_k[...] = jnp.zeros_like(acc_k)

        bd = b_ref[...]
        c2, s2 = c_ref[...], s_ref[...]

        def norm_bwd(dy, x, w, ones, cos_t, sin_t, acc):
            dyr = _rope_bwd(dy, cos_t, sin_t, hd // 4)
            r = lax.rsqrt(_group_mean(x * x, ones) + EPS)
            xh = x * r
            gy = dyr * w
            acc[...] += jnp.sum((dyr * xh).reshape(tm // 8, 8, x.shape[-1]), axis=0)
            return r * (gy - xh * _group_mean(gy * xh, ones))

        cq = jnp.concatenate([c2] * 4, axis=-1)
        sq = jnp.concatenate([s2] * 4, axis=-1)
        dz_ref[:, :512] = norm_bwd(dq_ref[...], zq_ref[...], qn_ref[...], bd, cq, sq, acc_q).astype(BF16)
        zkv = zkv_ref[...]
        dkk = jnp.concatenate([dk_ref[0], dk_ref[1]], axis=0).T
        dz_ref[:, 512:640] = norm_bwd(dkk, zkv[:, :128], kn_ref[...], bd[:128, :128], c2, s2, acc_k).astype(BF16)
        dz_ref[:, 640:768] = jnp.concatenate([dv_ref[0], dv_ref[1]], axis=0).T.astype(BF16)

        @pl.when(i == n - 1)
        def _():
            dqn_ref[...] = jnp.sum(acc_q[...], axis=0, keepdims=True)
            dkn_ref[...] = jnp.sum(acc_k[...], axis=0, keepdims=True)

    kv_blk = SEG["ka"][2] // 256
    kvs = pl.BlockSpec((ATTN_KV_HEADS, hd, tm), lambda i: (0, 0, i))
    const = lambda shape: pl.BlockSpec(shape, lambda i: (0, 0))
    return pl.pallas_call(
        body, name="attn_post_bwd", grid=(n,),
        in_specs=[pl.BlockSpec((tm, 512), lambda i: (i, 0)), kvs, kvs,
                  pl.BlockSpec((tm, 512), lambda i: (i, 0)), pl.BlockSpec((tm, 256), lambda i: (i, kv_blk)),
                  const((1, 512)), const((1, 128)),
                  pl.BlockSpec((tm, 128), lambda i: (i, 0)), pl.BlockSpec((tm, 128), lambda i: (i, 0)),
                  const((512, 512))],
        out_specs=[pl.BlockSpec((tm, 768), lambda i: (i, 0)), const((1, 512)), const((1, 128))],
        out_shape=[SDS((t, 768), BF16), SDS((1, 512), F32), SDS((1, 128), F32)],
        scratch_shapes=[pltpu.VMEM((8, 512), F32), pltpu.VMEM((8, 128), F32)],
        compiler_params=_params(("arbitrary",)),
    )(dq, dk, dv, z, z, qn, kn, cos, sin, ones_bd)


def _in_bwd(dxo, x, g, w_t, dz_a, dz_m, dqr, dkr, dvr, after=None):
    t, d = x.shape
    tm = min(256, t)
    n = t // tm
    parts = [(0, 0, 768, 0), (1, 0, 512, SEG["ga"][0]), (2, 0, 512, SEG["qr"][0]), (3, 0, 512, SEG["kr"][0]),
             (4, 0, 512, SEG["vr"][0]), (1, 512, 2560, SEG["gr"][0])]

    def body(dx_ref, x_ref, g_ref, w_ref, a_ref, m_ref, q_ref, k_ref, v_ref, o_ref, dg_ref, acc):
        i = pl.program_id(0)

        @pl.when(i == 0)
        def _():
            acc[...] = jnp.zeros_like(acc)

        pieces = [a_ref, m_ref, q_ref, k_ref, v_ref]
        dh = jnp.zeros((tm, d), F32)
        for pi, lo, w, row in parts:
            dh = dh + _dot(pieces[pi][:, lo:lo + w], w_ref[row:row + w, :])
        xv = x_ref[...]
        r = lax.rsqrt(jnp.mean(xv * xv, axis=-1, keepdims=True) + EPS)
        xh = xv * r
        gy = dh * g_ref[...]
        o_ref[...] = dx_ref[...] + r * (gy - xh * jnp.mean(gy * xh, axis=-1, keepdims=True))
        acc[...] += jnp.sum((dh * xh).reshape(tm // 8, 8, d), axis=0)

        @pl.when(i == n - 1)
        def _():
            dg_ref[...] = jnp.sum(acc[...], axis=0, keepdims=True)

    row = lambda w: pl.BlockSpec((tm, w), lambda i: (i, 0))
    const = lambda shape: pl.BlockSpec(shape, lambda i: (0, 0))
    extra = [] if after is None else [after]
    return pl.pallas_call(
        (lambda *refs: body(*refs[:9], *refs[9 + len(extra):])), name="in_bwd", grid=(n,),
        in_specs=[row(d), row(d), const((1, d)), const((D_IN, d)), row(768), row(3072), row(512), row(512),
                  row(512)] + [const(a.shape) for a in extra],
        out_specs=[row(d), const((1, d))],
        out_shape=[SDS((t, d), F32), SDS((1, d), F32)],
        scratch_shapes=[pltpu.VMEM((8, d), F32)],
        compiler_params=_params(("arbitrary",)),
    )(dxo, x, g, w_t, dz_a, dz_m, dqr, dkr, dvr, *extra)


def _dw_in(h_t, dz_a, dz_m, dqr, dkr, dvr):
    d, t = h_t.shape
    tn = 256
    parts = [(0, 0, 0, 3), (1, 0, SEG["ga"][0] // tn, 2), (2, 0, SEG["qr"][0] // tn, 2),
             (3, 0, SEG["kr"][0] // tn, 2), (4, 0, SEG["vr"][0] // tn, 2), (1, 2, SEG["gr"][0] // tn, 10)]
    pieces = [dz_a, dz_m, dqr, dkr, dvr]

    def col_block(pi):
        mine = [(c0, r0, n) for q, c0, r0, n in parts if q == pi]

        def index(j):
            c0, r0, n = mine[0]
            blk = c0 + jnp.clip(j - r0, 0, n - 1)
            for c0, r0, n in mine[1:]:
                blk = jnp.where(j >= r0, c0 + jnp.clip(j - r0, 0, n - 1), blk)
            return 0, blk

        return index

    def body(h_ref, *refs):
        o_ref = refs[-1]
        j = pl.program_id(0)
        for pi, _, r0, n in parts:
            @pl.when(jnp.logical_and(j >= r0, j < r0 + n))
            def _(p_ref=refs[pi]):
                o_ref[...] = _dot(h_ref[...], p_ref[...]).T.astype(BF16)

    return pl.pallas_call(
        body, name="dw_in", grid=(D_IN // tn,),
        in_specs=[pl.BlockSpec((d, t), lambda j: (0, 0))] + [pl.BlockSpec((t, tn), col_block(pi)) for pi in range(5)],
        out_specs=pl.BlockSpec((tn, d), lambda j: (j, 0)),
        out_shape=SDS((D_IN, d), BF16),
        compiler_params=_params(("arbitrary",)),
    )(h_t, *pieces)


def _adamw_math(w, g, m, v):
    mn = ADAM_B1 * m + (1.0 - ADAM_B1) * g
    vn = ADAM_B2 * v + (1.0 - ADAM_B2) * (g * g)
    m_hat = mn / (1.0 - ADAM_B1 ** ADAM_STEP)
    v_hat = vn / (1.0 - ADAM_B2 ** ADAM_STEP)
    return -ADAM_LR * (m_hat / (jnp.sqrt(v_hat) + ADAM_EPS) + ADAM_WD * w), mn, vn


def _sum_adamw(recvs, w, m, v, lane0, tn, layer0=0, prev=None, own=None):
    _, r, c = w.shape
    j0 = lane0 // tn
    n = len(recvs)
    has_own = own is not None

    def body(*refs):
        mine_ref, refs = (refs[0], refs[1:]) if has_own else (None, refs)
        w_ref, m_ref, v_ref = refs[n:n + 3]
        g_ref, d_ref, mo_ref, vo_ref = refs[-4:]

        def run(r_ref):
            def slot(s):
                if has_own:
                    return jnp.where(mine_ref[0] == s, refs[n + 3][...], r_ref[s]).astype(F32)
                return r_ref[s].astype(F32)

            g = slot(0)
            for s in range(1, N_DEV):
                g = g + slot(s)
            g_ref[0] = g
            d_ref[0], mo_ref[0], vo_ref[0] = _adamw_math(w_ref[0], g, m_ref[0], v_ref[0])

        for i in range(n):
            pl.when(pl.program_id(0) == i)(functools.partial(run, refs[i]))

    slots = pl.BlockSpec((N_DEV, r, tn), lambda i, j, *_: (0, 0, j0 + j))
    blk = pl.BlockSpec((1, r, tn), lambda i, j, *_: (layer0 + i, 0, j))
    before = [] if prev is None else list(prev)
    in_specs, args = [slots] * n + [blk] * 3, [*recvs, w, m, v]
    if has_own:
        assert n == 1
        in_specs.append(pl.BlockSpec((r, tn), lambda i, j, mine: (mine[0], j0 + j)))
        args.append(own[0])
    n_pre = len(args) + has_own
    return pl.pallas_call(
        body, name="sum_adamw",
        grid_spec=pltpu.PrefetchScalarGridSpec(
            num_scalar_prefetch=int(has_own), grid=(n, c // tn),
            in_specs=in_specs + [ANY] * len(before), out_specs=[blk] * 4),
        out_shape=[SDS(w.shape, F32)] * 4,
        input_output_aliases={n_pre + k: k for k in range(len(before))},
        compiler_params=_params(("parallel", "parallel")),
    )(*([own[1]] if has_own else []), *args, *before)


def _adamw(w, g, m, v):
    rows, cols = w.shape
    tr = 256 if rows % 256 == 0 else rows

    def body(w_ref, g_ref, m_ref, v_ref, d_ref, mo_ref, vo_ref):
        d_ref[...], mo_ref[...], vo_ref[...] = _adamw_math(w_ref[...], g_ref[...], m_ref[...], v_ref[...])

    blk = pl.BlockSpec((tr, cols), lambda i: (i, 0))
    return pl.pallas_call(
        body, name="adamw", grid=(rows // tr,),
        in_specs=[blk] * 4, out_specs=[blk] * 3, out_shape=[SDS((rows, cols), F32)] * 3,
        compiler_params=_params(("parallel",)),
    )(w, g, m, v)


def _all_gather(shards):
    na = len(shards)
    chips = (4, 2, 6)

    def body(*refs):
        ins, outs = refs[:na], refs[na:2 * na]
        send_sems, recv_sems, local_sems = refs[2 * na:]
        _, mine = _flip(0)

        def rows(a, idx):
            r = shards[a].shape[0]
            return outs[a].at[pl.ds(pl.multiple_of(idx * r, 16), r), :]

        def copy(a, slot, block_idx, to, src=None):
            return pltpu.make_async_remote_copy(
                src_ref=rows(a, block_idx) if src is None else src, dst_ref=rows(a, block_idx),
                send_sem=send_sems.at[a, slot], recv_sem=recv_sems.at[a, slot],
                device_id=to, device_id_type=MESH_ID)

        sibling, sibling_idx = _flip(1)
        local, started = [], []
        for a in range(na):
            cp = pltpu.make_async_copy(ins[a], rows(a, mine), local_sems.at[a])
            cp.start()
            local.append(cp)
            first = [copy(a, 0, mine, sibling, src=ins[a])]
            first += [copy(a, 1 + j, mine, _flip(k)[0], src=ins[a]) for j, k in enumerate(chips)]
            for cp in first:
                cp.start()
            started += first
        for a in range(na):
            for j, k in enumerate(chips):
                _, theirs = _flip(k)
                copy(a, 1 + j, theirs, _flip(0)[0]).wait_recv()
                fwd = copy(a, 4 + j, theirs, sibling)
                fwd.start()
                started.append(fwd)
        for a in range(na):
            copy(a, 0, sibling_idx, _flip(0)[0]).wait_recv()
            for j, k in enumerate(chips):
                _, theirs = _flip(k | 1)
                copy(a, 4 + j, theirs, _flip(0)[0]).wait_recv()
        for cp in started:
            cp.wait_send()
        for cp in local:
            cp.wait()

    return pl.pallas_call(
        body, name="all_gather_weights",
        in_specs=[ANY] * na, out_specs=[ANY] * na,
        out_shape=[SDS((N_DEV * s.shape[0], s.shape[1]), s.dtype) for s in shards],
        scratch_shapes=[pltpu.SemaphoreType.DMA((na, 7)), pltpu.SemaphoreType.DMA((na, 7)),
                        pltpu.SemaphoreType.DMA((na,))],
        compiler_params=pltpu.CompilerParams(has_side_effects=True),
    )(*shards)


def _scatter_blocks_of(g_ref, rows, idx):
    return g_ref.at[pl.ds(pl.multiple_of(idx * rows, 16), rows), :]


def _scatter_start(g):
    rows = g.shape[0] // N_DEV
    land_shape = (N_DEV, rows, g.shape[1])

    def body(g_ref, land_ref, send_sems, recv_sems, g_thru, land_thru, token):
        _, mine = _flip(0)
        for k in range(1, N_DEV):
            peer, theirs = _flip(k)
            pltpu.make_async_remote_copy(
                src_ref=_scatter_blocks_of(g_ref, rows, theirs), dst_ref=land_ref.at[mine],
                send_sem=send_sems.at[k - 1], recv_sem=recv_sems.at[k - 1],
                device_id=peer, device_id_type=MESH_ID).start()
        token[...] = jnp.zeros_like(token)

    hbm, sem = pl.BlockSpec(memory_space=pltpu.HBM), pl.BlockSpec(memory_space=pltpu.SEMAPHORE)
    return pl.pallas_call(
        body, name="scatter_start",
        out_shape=(pltpu.SemaphoreType.DMA((N_DEV - 1,)), pltpu.SemaphoreType.DMA((N_DEV - 1,)),
                   pltpu.HBM(g.shape, g.dtype), pltpu.HBM(land_shape, g.dtype), SDS((8, 128), F32)),
        in_specs=(hbm, hbm), out_specs=(sem, sem, hbm, hbm, pl.BlockSpec(memory_space=pltpu.VMEM)),
        input_output_aliases={0: 2, 1: 3},
        compiler_params=pltpu.CompilerParams(has_side_effects=pltpu.SideEffectType.DATAFLOW_SIDE_EFFECTING),
    )(pltpu.with_memory_space_constraint(g, pltpu.HBM),
      pltpu.with_memory_space_constraint(lax.empty(land_shape, g.dtype), pltpu.HBM))


def _scatter_wait(send_sems, recv_sems, g_thru, land_thru, after):
    rows = g_thru.shape[0] // N_DEV

    def body(g_ref, land_ref, send_sems, recv_sems, *rest):
        me, _ = _flip(0)
        for k in range(1, N_DEV):
            _, theirs = _flip(k)
            copy = pltpu.make_async_remote_copy(
                src_ref=_scatter_blocks_of(g_ref, rows, theirs), dst_ref=land_ref.at[theirs],
                send_sem=send_sems.at[k - 1], recv_sem=recv_sems.at[k - 1],
                device_id=me, device_id_type=MESH_ID)
            copy.wait_send()
            copy.wait_recv()

    hbm, sem = pl.BlockSpec(memory_space=pltpu.HBM), pl.BlockSpec(memory_space=pltpu.SEMAPHORE)
    return pl.pallas_call(
        body, name="scatter_wait",
        out_shape=(pltpu.HBM(g_thru.shape, g_thru.dtype), pltpu.HBM(land_thru.shape, land_thru.dtype)),
        in_specs=(hbm, hbm, sem, sem) + (ANY,) * len(after), out_specs=(hbm, hbm), input_output_aliases={0: 0, 1: 1},
        compiler_params=pltpu.CompilerParams(has_side_effects=pltpu.SideEffectType.DATAFLOW_SIDE_EFFECTING),
    )(g_thru, land_thru, send_sems, recv_sems, *after)


def _all_reduce_small(packed):
    shape = packed.shape

    def body(p_ref, o_ref, slots, send_sems, recv_sems):
        me, mine = _flip(0)
        slots[mine] = p_ref[...]
        sends = []
        for k in range(1, N_DEV):
            peer, _ = _flip(k)
            cp = pltpu.make_async_remote_copy(
                src_ref=p_ref, dst_ref=slots.at[mine], send_sem=send_sems.at[k - 1], recv_sem=recv_sems.at[k - 1],
                device_id=peer, device_id_type=MESH_ID)
            cp.start()
            sends.append(cp)
        for k in range(1, N_DEV):
            _, theirs = _flip(k)
            pltpu.make_async_remote_copy(
                src_ref=p_ref, dst_ref=slots.at[theirs], send_sem=send_sems.at[k - 1],
                recv_sem=recv_sems.at[k - 1], device_id=me, device_id_type=MESH_ID).wait_recv()
        for cp in sends:
            cp.wait_send()
        acc = slots[0]
        for s in range(1, N_DEV):
            acc = acc + slots[s]
        o_ref[...] = acc

    vm = pl.BlockSpec(memory_space=pltpu.VMEM)
    return pl.pallas_call(
        body, name="all_reduce_small", in_specs=[vm], out_specs=vm, out_shape=SDS(shape, F32),
        scratch_shapes=[pltpu.VMEM((N_DEV,) + shape, F32), pltpu.SemaphoreType.DMA((7,)),
                        pltpu.SemaphoreType.DMA((7,))],
        compiler_params=pltpu.CompilerParams(has_side_effects=True),
    )(packed)


def _layer_fwd(x, p, tabs, ex):
    z, h_t, q, qt, k, v, vt, qrot, krot, vb = _in_proj(x, p["norm_g"], p["w_in_t"], p["qn"], p["kn"], tabs["ca"],
                                                       tabs["sa"], tabs["ones"], tabs["cr"], tabs["sr"])
    oa, lse, *gathered = _attn_fwd(q, k, vt, ex)
    orr, on = _ret_fwd(qrot, krot, vb, p["lgf"], p["lgb"], p["gnw"])
    return z, h_t, q, qt, k, v, lse, oa, qrot, krot, vb, orr, on, gathered


def _layer_bwd(dxo, s, p, tabs, ex_attn, scatter_w_in):
    doa, don, dz_m, d_wout, d_wb_t = _merge_bwd(dxo, s["z"], s["oa"], s["on"], s["ya"], s["yb"], p["wb_t"], p["w_out"])
    dq_a, dk_a, dv_a, *recv_attn = _attn_bwd(s["q"], s["qt"], s["k"], s["v"], doa, s["oa"], s["lse"],
                                              ex_attn(d_wb_t, d_wout))
    dz_a, d_qn, d_kn = _attn_post_bwd(dq_a, dk_a, dv_a, s["z"], p["qn"], p["kn"], tabs["ca"], tabs["sa"],
                                      tabs["ones"])
    dqr, dkr, dvr, d_gnw, d_lgf, d_lgb = _ret_bwd(s["qrot"], s["krot"], s["vb"], s["orr"], don, p["gnw"],
                                                  p["lgf"], p["lgb"], tabs["cr"], tabs["sr"])
    buf = _dw_in(s["h_t"], dz_a, dz_m, dqr, dkr, dvr)
    pending, token = None, None
    if scatter_w_in:
        *pending, token = _scatter_start(buf)
    dx, d_norm_g = _in_bwd(dxo, s["x"], p["norm_g"], p["w_in_t"], dz_a, dz_m, dqr, dkr, dvr, token)
    grads = dict(w_in_t=buf, wb_t=d_wb_t, w_out=d_wout, norm_g=d_norm_g, gnw=d_gnw,
                 qn=d_qn.reshape(ATTN_Q_HEADS, ATTN_HEAD_DIM).sum(axis=0),
                 kn=d_kn.reshape(ATTN_KV_HEADS, ATTN_HEAD_DIM).sum(axis=0),
                 lgf=d_lgf[:, 0, 0], lgb=d_lgb[:, 0, 0])
    return dx, grads, recv_attn, pending


def _adamw_nd(w, g, m, v):
    shape = w.shape
    two_d = (1, shape[0]) if w.ndim == 1 else (-1, shape[-1])
    out = _adamw(w.reshape(two_d), g.reshape(two_d), m.reshape(two_d), v.reshape(two_d))
    return tuple(o.reshape(shape) for o in out)


def kernel(x, norm_g, w_in, attn_q_norm, attn_k_norm, ret_decay_fwd, ret_decay_bwd, ret_gn_w, w_branch_attn, w_branch_ret, w_out, final_norm_g, loss_target, m_norm_g, m_w_in, m_attn_q_norm, m_attn_k_norm, m_ret_decay_fwd, m_ret_decay_bwd, m_ret_gn_w, m_w_branch_attn, m_w_branch_ret, m_w_out, m_final_norm_g, v_norm_g, v_w_in, v_attn_q_norm, v_attn_k_norm, v_ret_decay_fwd, v_ret_decay_bwd, v_ret_gn_w, v_w_branch_attn, v_w_branch_ret, v_w_out, v_final_norm_g):
    t, d = x.shape[1], x.shape[2]
    x2, target = x[0], loss_target[0]

    w_in_sh, wb_sh, wout_sh = [], [], []
    for l in range(DEPTH):
        w_in_sh.append(jnp.swapaxes(w_in[l], 0, 1).astype(BF16))
        wb_sh.append(jnp.concatenate([w_branch_attn[l].T, w_branch_ret[l].T], axis=1).astype(BF16))
        wout_sh.append(w_out[l].astype(BF16))

    ca, sa = _rope_tables(t, ATTN_HEAD_DIM)
    cr, sr = _rope_tables(t, RET_HEAD_DIM)
    grp = jnp.arange(ATTN_WIDTH) // ATTN_HEAD_DIM
    tabs = dict(ca=jnp.tile(ca, (1, 2)), sa=jnp.tile(sa, (1, 2)), cr=cr, sr=sr,
                ones=jnp.where(grp[:, None] == grp[None, :], 1.0 / ATTN_HEAD_DIM, 0.0).astype(BF16))
    layers = []
    for l in range(DEPTH):
        layers.append(dict(
            norm_g=norm_g[l][None], qn=jnp.tile(attn_q_norm[l], ATTN_Q_HEADS)[None],
            kn=jnp.tile(attn_k_norm[l], ATTN_KV_HEADS)[None], gnw=ret_gn_w[l][None],
            lgf=jax.nn.log_sigmoid(ret_decay_fwd[l]), lgb=jax.nn.log_sigmoid(ret_decay_bwd[l])))

    layers[0]["w_in_t"], = _all_gather([w_in_sh[0]])
    gathers = [_Exchange("gather", [wb_sh[0], wout_sh[0], w_in_sh[1]]), _Exchange("gather", [wb_sh[1], wout_sh[1]])]
    h = x2
    saved = []
    for l in range(DEPTH):
        p = layers[l]
        z, h_t, q, qt, k, v, lse, oa, qrot, krot, vb, orr, on, got = _layer_fwd(h, p, tabs, gathers[l])
        p["wb_t"], p["w_out"] = got[0], got[1]
        if l == 0:
            layers[1]["w_in_t"] = got[2]
        last = (final_norm_g[None], target) if l == DEPTH - 1 else None
        xn, ya, yb, *loss_head = _merge_fwd(h, z, oa, on, p["wb_t"], p["w_out"], last)
        saved.append(dict(x=h, z=z, h_t=h_t, q=q, qt=qt, k=k, v=v, lse=lse, oa=oa, qrot=qrot, krot=krot, vb=vb,
                          orr=orr, on=on, ya=ya, yb=yb))
        h = xn
    dx, (d_final_g, loss_part) = h, loss_head

    grads = [None] * DEPTH
    dx, grads[1], _, _ = _layer_bwd(dx, saved[1], layers[1], tabs, lambda *a: None, False)
    g1 = grads[1]
    ex_attn = lambda d_wb_t, d_wout: _Exchange("scatter", [g1["w_in_t"], g1["wb_t"], g1["w_out"], d_wb_t, d_wout])
    dx, grads[0], recv_attn, pending = _layer_bwd(dx, saved[0], layers[0], tabs, ex_attn, True)
    recv = [None, recv_attn[3], recv_attn[4], recv_attn[0], recv_attn[1], recv_attn[2]]
    tr = lambda a: jnp.swapaxes(a, 1, 2)
    w_in_t = (tr(w_in), tr(m_w_in), tr(v_w_in))
    sharded = {}
    w_in_l1 = _sum_adamw([recv[3]], *w_in_t, 0, 256, layer0=1)
    sharded[id(w_branch_attn)] = [tr(o) for o in _sum_adamw(
        [recv[1], recv[4]], tr(w_branch_attn), tr(m_w_branch_attn), tr(v_w_branch_attn), 0, 512)]
    sharded[id(w_branch_ret)] = [tr(o) for o in _sum_adamw(
        [recv[1], recv[4]], tr(w_branch_ret), tr(m_w_branch_ret), tr(v_w_branch_ret), 512, 512)]
    sharded[id(w_out)] = _sum_adamw([recv[2], recv[5]], w_out, m_w_out, v_w_out, 0, 256)
    g_wba, g_wbr, g_wout = (sharded[id(w)][0] for w in (w_branch_attn, w_branch_ret, w_out))

    packed = jnp.zeros((8, 1024), F32)
    for l in range(DEPTH):
        gl = grads[l]
        packed = packed.at[l].set(gl["norm_g"][0])
        packed = packed.at[2, 512 * l:512 * (l + 1)].set(gl["gnw"][0])
        packed = packed.at[4, 128 * l:128 * l + 64].set(gl["qn"])
        packed = packed.at[4, 256 + 128 * l:256 + 128 * l + 64].set(gl["kn"])
        packed = packed.at[4, 512 + 128 * l:512 + 128 * l + 4].set(gl["lgf"])
        packed = packed.at[4, 768 + 128 * l:768 + 128 * l + 4].set(gl["lgb"])
    packed = packed.at[3].set(d_final_g[0])
    packed = packed.at[5, 0].set(loss_part[0, 0])
    red = _all_reduce_small(packed)
    loss = red[5, 0]
    g_norm_g = red[0:2]
    g_gnw = red[2].reshape(DEPTH, RET_WIDTH)
    g_final = red[3]
    g_qn = jnp.stack([red[4, 128 * l:128 * l + 64] for l in range(DEPTH)])
    g_kn = jnp.stack([red[4, 256 + 128 * l:256 + 128 * l + 64] for l in range(DEPTH)])
    g_lgf = jnp.stack([red[4, 512 + 128 * l:512 + 128 * l + 4] for l in range(DEPTH)])
    g_lgb = jnp.stack([red[4, 768 + 128 * l:768 + 128 * l + 4] for l in range(DEPTH)])
    g_df = g_lgf * jax.nn.sigmoid(-ret_decay_fwd)
    g_db = g_lgb * jax.nn.sigmoid(-ret_decay_bwd)

    grad_w = [g_norm_g, None, g_qn, g_kn, g_df, g_db, g_gnw, g_wba, g_wbr, g_wout, g_final]
    weights = [norm_g, w_in, attn_q_norm, attn_k_norm, ret_decay_fwd, ret_decay_bwd, ret_gn_w, w_branch_attn,
               w_branch_ret, w_out, final_norm_g]
    ms = [m_norm_g, m_w_in, m_attn_q_norm, m_attn_k_norm, m_ret_decay_fwd, m_ret_decay_bwd, m_ret_gn_w,
          m_w_branch_attn, m_w_branch_ret, m_w_out, m_final_norm_g]
    vs = [v_norm_g, v_w_in, v_attn_q_norm, v_attn_k_norm, v_ret_decay_fwd, v_ret_decay_bwd, v_ret_gn_w,
          v_w_branch_attn, v_w_branch_ret, v_w_out, v_final_norm_g]
    upd = [None if w is w_in else sharded[id(w)][1:] if id(w) in sharded else _adamw_nd(w, g, m, v)
           for w, g, m, v in zip(weights, grad_w, ms, vs)]

    done = [dx, w_in_l1[0], g_wout] + [u[0] for w, u in zip(weights, upd) if u is not None and id(w) not in sharded]
    g_full, recv[0] = _scatter_wait(*pending, done)
    mine = (4 * lax.axis_index("x") + 2 * lax.axis_index("y") + lax.axis_index("c")).astype(jnp.int32)[None]
    w_in_upd = [tr(o) for o in _sum_adamw([recv[0]], *w_in_t, 0, 256, layer0=0, prev=w_in_l1, own=(g_full, mine))]
    grad_w[1], upd[1] = w_in_upd[0], w_in_upd[1:]
    return (loss, dx[None], *grad_w, *[u[0] for u in upd], *[u[1] for u in upd], *[u[2] for u in upd])
```

```python
import functools

import jax
import jax.numpy as jnp
from jax import lax
from jax.experimental import pallas as pl
from jax.experimental.pallas import tpu as pltpu

F32 = jnp.float32
BF16 = jnp.bfloat16
SDS = jax.ShapeDtypeStruct

D_MODEL = 1024
DEPTH = 2
GRID_W = 64
ATTN_Q_HEADS = 8
ATTN_KV_HEADS = 2
ATTN_HEAD_DIM = 64
ATTN_WIDTH = 512
ATTN_KV_WIDTH = 128
RET_HEADS = 4
RET_HEAD_DIM = 128
RET_WIDTH = 512
RET_CHUNK = 128
ATTN_KEY_CHUNK = 512
ATTN_BWD_KEY_CHUNK = 512
ATTN_BWD_QUERY_TILE = 1024
ATTN_FWD_QUERY_TILE = 512
QK_DOTS_PER_CHUNK = 1
EXP_LAG = 3
ROPE_THETA = 10000.0
EPS = 1e-6
D_IN = 5376
N_DEV = 8

ADAM_LR = 0.001
ADAM_B1 = 0.9
ADAM_B2 = 0.999
ADAM_EPS = 1e-08
ADAM_WD = 0.01
ADAM_STEP = 10

SEG = {
    "qa": (0, 512, 0),
    "ga": (768, 512, 512),
    "qr": (1280, 512, 1024),
    "kr": (1792, 512, 1536),
    "vr": (2304, 512, 2048),
    "gr": (2816, 512, 2560),
    "gm": (3328, 2048, 3072),
    "ka": (512, 128, 5120),
    "va": (640, 128, 5248),
}

VMEM_LIMIT = 60 * 1024 * 1024
NT = (((1,), (1,)), ((), ()))
TN = (((0,), (0,)), ((), ()))
MESH_ID = pl.DeviceIdType.MESH
ANY = pl.BlockSpec(memory_space=pl.ANY)


def _params(sem=None, vmem=VMEM_LIMIT):
    return pltpu.CompilerParams(dimension_semantics=sem, vmem_limit_bytes=vmem)


def _dot(a, b, dims=None):
    if dims is None:
        return jnp.dot(a, b, preferred_element_type=F32)
    return lax.dot_general(a, b, dims, preferred_element_type=F32)


def _sigmoid(x):
    return 1.0 / (1.0 + jnp.exp(-x))


def _swap_halves(x, q):
    n = x.shape[-1]
    axis = x.ndim - 1
    lane = lax.broadcasted_iota(jnp.int32, x.shape, axis)
    first = (lane % (2 * q)) < q
    return jnp.where(first, pltpu.roll(x, n - q, axis), pltpu.roll(x, q, axis))


def _rope(x, cos, sin_signed, q):
    return x * cos + _swap_halves(x, q) * sin_signed


def _rope_bwd(dy, cos, sin_signed, q):
    return dy * cos - _swap_halves(dy, q) * sin_signed


def _group_mean(v, ones_bd):
    hi = v.astype(BF16)
    lo = (v - hi.astype(F32)).astype(BF16)
    return _dot(hi, ones_bd) + _dot(lo, ones_bd)


def _rope_tables(t, head_dim):
    n_rows = t // GRID_W
    d_axis = head_dim // 2
    inv_freq = ROPE_THETA ** (-jnp.arange(0, d_axis, 2, dtype=F32) / d_axis)
    ar = jnp.arange(n_rows, dtype=F32)[:, None] * inv_freq
    ac = jnp.arange(GRID_W, dtype=F32)[:, None] * inv_freq
    by_row = lambda a: jnp.repeat(a, GRID_W, axis=0)
    by_col = lambda a: jnp.tile(a, (n_rows, 1))
    cr, sr, cc, sc = by_row(jnp.cos(ar)), by_row(jnp.sin(ar)), by_col(jnp.cos(ac)), by_col(jnp.sin(ac))
    return jnp.concatenate([cr, cr, cc, cc], axis=-1), jnp.concatenate([-sr, sr, -sc, sc], axis=-1)


def _me():
    return lax.axis_index("x"), lax.axis_index("y"), lax.axis_index("c")


def _flip(k):
    x, y, c = _me()
    px = 1 - x if k & 4 else x
    py = 1 - y if k & 2 else y
    pc = 1 - c if k & 1 else c
    return (px, py, pc), 4 * px + 2 * py + pc


class _Exchange:
    def __init__(self, kind, srcs):
        self.kind, self.srcs, self.n = kind, list(srcs), len(srcs)
        self.rows = [a.shape[0] if kind == "gather" else a.shape[0] // N_DEV for a in srcs]
        if kind == "gather":
            self.out_shape = [SDS((N_DEV * a.shape[0], a.shape[1]), a.dtype) for a in srcs]
        else:
            self.out_shape = [SDS((N_DEV, a.shape[0] // N_DEV, a.shape[1]), a.dtype) for a in srcs]
        self.scratch = [pltpu.SemaphoreType.DMA((self.n, N_DEV - 1)), pltpu.SemaphoreType.DMA((self.n, N_DEV - 1)),
                        pltpu.SemaphoreType.DMA((self.n,))]

    def _block(self, ref, a, idx):
        r = self.rows[a]
        return ref.at[pl.ds(pl.multiple_of(idx * r, 16), r), :]

    def _src(self, ins, a, idx):
        return ins[a] if self.kind == "gather" else self._block(ins[a], a, idx)

    def _dst(self, outs, a, idx):
        return self._block(outs[a], a, idx) if self.kind == "gather" else outs[a].at[idx]

    def _copies(self, ins, outs, sems):
        send_sems, recv_sems, local_sems = sems
        me, mine = _flip(0)
        local, sends, recvs = [], [], []
        for a in range(self.n):
            local.append(pltpu.make_async_copy(self._src(ins, a, mine), self._dst(outs, a, mine), local_sems.at[a]))
            for k in range(1, N_DEV):
                peer, theirs = _flip(k)
                sem = dict(send_sem=send_sems.at[a, k - 1], recv_sem=recv_sems.at[a, k - 1])
                sends.append(pltpu.make_async_remote_copy(
                    src_ref=self._src(ins, a, theirs), dst_ref=self._dst(outs, a, mine),
                    device_id=peer, device_id_type=MESH_ID, **sem))
                recvs.append(pltpu.make_async_remote_copy(
                    src_ref=self._dst(outs, a, theirs), dst_ref=self._dst(outs, a, theirs),
                    device_id=me, device_id_type=MESH_ID, **sem))
        return local, sends, recvs

    def start(self, ins, outs, sems):
        local, sends, _ = self._copies(ins, outs, sems)
        for cp in local + sends:
            cp.start()

    def wait(self, ins, outs, sems):
        local, sends, recvs = self._copies(ins, outs, sems)
        for cp in sends:
            cp.wait_send()
        for cp in recvs:
            cp.wait_recv()
        for cp in local:
            cp.wait()


def _with_exchange(body, n_in, n_out, n_scratch, ex, first, last):
    if ex is None:
        return body

    def wrapped(*refs):
        ins = refs[:n_in]
        ex_ins = refs[n_in:n_in + ex.n]
        outs = refs[n_in + ex.n:n_in + ex.n + n_out]
        ex_outs = refs[n_in + ex.n + n_out:n_in + 2 * ex.n + n_out]
        rest = refs[n_in + 2 * ex.n + n_out:]
        scratch, sems = rest[:n_scratch], rest[n_scratch:]

        @pl.when(first())
        def _():
            ex.start(ex_ins, ex_outs, sems)

        body(*ins, *outs, *scratch)

        @pl.when(last())
        def _():
            ex.wait(ex_ins, ex_outs, sems)

    return wrapped


def _ex_args(ex):
    if ex is None:
        return [], [], [], [], []
    return [ANY] * ex.n, [ANY] * ex.n, list(ex.out_shape), list(ex.scratch), list(ex.srcs)


def _in_proj(x, g, w_t, qn, kn, cos, sin, ones_bd, cos_r, sin_r):
    t, d = x.shape
    tm = min(512, t)
    tk = min(ATTN_KEY_CHUNK, t)
    per_chunk = tk // tm
    hd = ATTN_HEAD_DIM

    def body(x_ref, g_ref, w_ref, qn_ref, kn_ref, c_ref, s_ref, b_ref, cr_ref, sr_ref,
             z_ref, ht_ref, q_out, qt_out, k_out, v_out, vt_out, qr_out, kr_out, vr_out):
        xv = x_ref[...]
        r = lax.rsqrt(jnp.mean(xv * xv, axis=-1, keepdims=True) + EPS)
        h = xv * r * g_ref[...]
        ht_ref[...] = h.T.astype(BF16)
        hb = h.astype(BF16)
        def project(name):
            nat, w, off = SEG[name]
            zs = _dot(hb, w_ref[nat:nat + w, :], NT)
            z_ref[:, off:off + w] = zs
            return zs

        seg = {name: project(name) for name in ("qa", "ka", "va")}
        bd = b_ref[...]
        c2, s2 = c_ref[...], s_ref[...]
        cq = jnp.concatenate([c2] * 4, axis=-1)
        sq = jnp.concatenate([s2] * 4, axis=-1)
        xq, xk, xvv = seg["qa"], seg["ka"], seg["va"]
        yq = xq * lax.rsqrt(_group_mean(xq * xq, bd) + EPS) * qn_ref[...]
        yq = _rope(yq, cq, sq, hd // 4) * (hd ** -0.5)
        yqt = yq.T
        for hh in range(ATTN_Q_HEADS):
            q_out[hh] = yq[:, hh * hd:(hh + 1) * hd].astype(BF16)
            qt_out[hh] = yqt[hh * hd:(hh + 1) * hd, :].astype(BF16)
        yk = xk * lax.rsqrt(_group_mean(xk * xk, bd[:ATTN_KV_WIDTH, :ATTN_KV_WIDTH]) + EPS) * kn_ref[...]
        yk = _rope(yk, c2, s2, hd // 4)
        xvt = xvv.T
        ones = jnp.ones((hd, tm), F32)
        for hh in range(ATTN_KV_HEADS):
            k_out[hh] = yk[:, hh * hd:(hh + 1) * hd].astype(BF16)
            v_out[hh] = xvv[:, hh * hd:(hh + 1) * hd].astype(BF16)
            vt_out[hh, 0] = jnp.concatenate([xvt[hh * hd:(hh + 1) * hd, :], ones], axis=0).astype(BF16)
        rd = RET_HEAD_DIM
        cr = jnp.concatenate([cr_ref[...]] * RET_HEADS, axis=-1)
        sr = jnp.concatenate([sr_ref[...]] * RET_HEADS, axis=-1)
        qr_out[...] = _rope(project("qr"), cr, sr, rd // 4).astype(BF16)
        kr_out[...] = (_rope(project("kr"), cr, sr, rd // 4) * (rd ** -0.5)).astype(BF16)
        vr_out[...] = project("vr").astype(BF16)
        for name in ("ga", "gr", "gm"):
            project(name)

    const = lambda shape: pl.BlockSpec(shape, lambda i: (0,) * len(shape))
    rows = lambda w: pl.BlockSpec((tm, w), lambda i: (i, 0))
    return pl.pallas_call(
        body, name="in_proj", grid=(t // tm,),
        in_specs=[rows(d), const((1, d)),
                  pl.BlockSpec((D_IN, d), lambda i: (0, 0), pipeline_mode=pl.Buffered(1)),
                  const((1, 512)), const((1, 128)), rows(128), rows(128),
                  const((512, 512)), rows(128), rows(128)],
        out_specs=[rows(D_IN), pl.BlockSpec((d, tm), lambda i: (0, i)),
                   pl.BlockSpec((ATTN_Q_HEADS, tm, hd), lambda i: (0, i, 0)),
                   pl.BlockSpec((ATTN_Q_HEADS, hd, tm), lambda i: (0, 0, i)),
                   pl.BlockSpec((ATTN_KV_HEADS, tm, hd), lambda i: (0, i, 0)),
                   pl.BlockSpec((ATTN_KV_HEADS, tm, hd), lambda i: (0, i, 0)),
                   pl.BlockSpec((ATTN_KV_HEADS, 1, 2 * hd, tm), lambda i: (0, i // per_chunk, 0, i % per_chunk)),
                   rows(RET_WIDTH), rows(RET_WIDTH), rows(RET_WIDTH)],
        out_shape=[SDS((t, D_IN), F32), SDS((d, t), BF16),
                   SDS((ATTN_Q_HEADS, t, hd), BF16), SDS((ATTN_Q_HEADS, hd, t), BF16),
                   SDS((ATTN_KV_HEADS, t, hd), BF16), SDS((ATTN_KV_HEADS, t, hd), BF16),
                   SDS((ATTN_KV_HEADS, t // tk, 2 * hd, tk), BF16)] + [SDS((t, RET_WIDTH), BF16)] * 3,
        compiler_params=_params(("parallel",)),
    )(x, g, w_t, qn, kn, cos, sin, ones_bd, cos_r, sin_r)


def _attn_fwd(q, k, vt, ex=None):
    t = q.shape[1]
    tq = min(ATTN_FWD_QUERY_TILE, t)
    nk, tk = vt.shape[1], vt.shape[3]
    hd = ATTN_HEAD_DIM
    g = ATTN_Q_HEADS // ATTN_KV_HEADS

    def body(q_ref, k_ref, vt_ref, o_ref, lse_ref, s_scr):
        def pass_a(h, c, m8):
            part = tk // QK_DOTS_PER_CHUNK
            for lo in range(c * tk, (c + 1) * tk, part):
                st = _dot(k_ref[0, lo:lo + part, :], q_ref[h], NT)
                s_scr[h % 2, lo:lo + part, :] = st
                m8 = jnp.maximum(m8, jnp.max(st.reshape(part // 8, 8, tq), axis=0))
            return m8

        def pass_b(h, c, m, acc, after):
            e = jnp.exp(s_scr[h % 2, c * tk:(c + 1) * tk, :] - (m + after * 0.0)).astype(BF16)
            return acc + _dot(vt_ref[0, c], e)

        neg = jnp.full((8, tq), -jnp.inf, F32)
        m8 = neg
        for c in range(nk):
            m8 = pass_a(0, c, m8)
        outs = []
        for h in range(g):
            m = jnp.max(m8, axis=0, keepdims=True)
            acc = jnp.zeros((2 * hd, tq), F32)
            m8 = neg
            done = [m] * EXP_LAG
            for c in range(nk):
                if h + 1 < g:
                    m8 = pass_a(h + 1, c, m8)
                acc = pass_b(h, c, m, acc, done[-EXP_LAG])
                done.append(m8[0:1, :] if h + 1 < g else acc[hd:hd + 1, :])
            l = acc[hd:hd + 1, :]
            outs.append((acc[:hd, :] / l).T)
            lse_ref[h] = m + jnp.log(l)
        o_ref[...] = jnp.concatenate(outs, axis=-1)

    nq = t // tq
    first = lambda: jnp.logical_and(pl.program_id(0) == 0, pl.program_id(1) == 0)
    last = lambda: jnp.logical_and(pl.program_id(0) == ATTN_KV_HEADS - 1, pl.program_id(1) == nq - 1)
    xi, xo, xs, xscr, xargs = _ex_args(ex)
    return pl.pallas_call(
        _with_exchange(body, 3, 2, 1, ex, first, last), name="attn_fwd", grid=(ATTN_KV_HEADS, nq),
        in_specs=[pl.BlockSpec((g, tq, hd), lambda p, i: (p, i, 0)),
                  pl.BlockSpec((1, t, hd), lambda p, i: (p, 0, 0)),
                  pl.BlockSpec((1, nk, 2 * hd, tk), lambda p, i: (p, 0, 0, 0))] + xi,
        out_specs=[pl.BlockSpec((tq, g * hd), lambda p, i: (i, p)),
                   pl.BlockSpec((g, 1, tq), lambda p, i: (p, 0, i))] + xo,
        out_shape=[SDS((t, ATTN_WIDTH), F32), SDS((ATTN_Q_HEADS, 1, t), F32)] + xs,
        scratch_shapes=[pltpu.VMEM((2, t, tq), F32)] + xscr,
        compiler_params=_params(("arbitrary", "arbitrary")),
    )(q, k, vt, *xargs)


class _Dir:
    def __init__(self, lg, strict_future):
        c = RET_CHUNK
        ia = lax.broadcasted_iota(jnp.int32, (c, c), 0).astype(F32)
        ib = lax.broadcasted_iota(jnp.int32, (c, c), 1).astype(F32)
        col = lax.broadcasted_iota(jnp.int32, (c, 1), 0).astype(F32)
        row = lax.broadcasted_iota(jnp.int32, (1, c), 1).astype(F32)
        if strict_future:
            dist = ib - ia
            mask = dist > 0
            self.wq, self.wk, wk_row = c - col, col, row
        else:
            dist = ia - ib
            mask = dist >= 0
            self.wq, self.wk, wk_row = col + 1.0, c - 1.0 - col, c - 1.0 - row
        self.dist = jnp.maximum(dist, 0.0)
        self.d = jnp.where(mask, jnp.exp(self.dist * lg), 0.0)
        self.qd = jnp.exp(self.wq * lg)
        self.kd_col = jnp.exp(self.wk * lg)
        self.kd_row = jnp.exp(wk_row * lg)
        self.cd = jnp.exp(jnp.full((1, 1), float(c), F32) * lg)


def _ret_fwd(qrot, krot, vb, lgf, lgb, gnw):
    t = qrot.shape[0]
    c = RET_CHUNK
    nc = t // c
    hd = RET_HEAD_DIM
    unroll = 4 if nc % 4 == 0 else 1

    def body(lgf_ref, lgb_ref, qo_ref, ko_ref, vo_ref, w_ref, orr_ref, on_ref, kt, uf, ub, sfa, sba):
        h = pl.program_id(0)
        fw = _Dir(lgf_ref[h], False)
        bw = _Dir(lgb_ref[h], True)
        for i in range(nc):
            kt[i] = ko_ref[i * c:(i + 1) * c, :].astype(F32).T.astype(BF16)

        def rows(ci):
            return pl.ds(pl.multiple_of(ci * c, c), c)

        def kv_products(ci, carry):
            vv = vo_ref[rows(ci), :]
            ktf = kt[ci].astype(F32)
            uf[ci] = _dot((ktf * fw.kd_row).astype(BF16), vv)
            ub[ci] = _dot((ktf * bw.kd_row).astype(BF16), vv)
            return carry

        lax.fori_loop(0, nc, kv_products, 0, unroll=16 if nc % 16 == 0 else unroll)

        def scan(i, carry):
            sf, sb = carry
            j = nc - 1 - i
            sfa[i] = sf.astype(BF16)
            sba[j] = sb.astype(BF16)
            return sf * fw.cd + uf[i], sb * bw.cd + ub[j]

        zero = jnp.zeros((hd, hd), F32)
        lax.fori_loop(0, nc, scan, (zero, zero))
        gw = w_ref[...]

        def outputs(ci, carry):
            sl = rows(ci)
            qq, kk, vv = qo_ref[sl, :], ko_ref[sl, :], vo_ref[sl, :]
            a = _dot(qq, kk, NT)
            o = (_dot((a * fw.d).astype(BF16), vv) + _dot(qq, sfa[ci]) * fw.qd
                 + _dot((a * bw.d).astype(BF16), vv) + _dot(qq, sba[ci]) * bw.qd)
            orr_ref[sl, :] = o
            xc = o - jnp.mean(o, axis=-1, keepdims=True)
            var = jnp.mean(xc * xc, axis=-1, keepdims=True)
            on_ref[sl, :] = xc * lax.rsqrt(var + EPS) * gw
            return carry

        group = 32 if nc % 32 == 0 else 1

        def output_group(i, carry):
            for j in range(group):
                outputs(i * group + j, carry)
            return carry

        lax.fori_loop(0, nc // group, output_group, 0)

    smem = pl.BlockSpec(memory_space=pltpu.SMEM)
    head = pl.BlockSpec((t, 128), lambda h: (0, h))
    return pl.pallas_call(
        body, name="ret_fwd", grid=(RET_HEADS,),
        in_specs=[smem, smem, head, head, head, pl.BlockSpec((1, 128), lambda h: (0, h))],
        out_specs=[head, head],
        out_shape=[SDS((t, RET_WIDTH), F32)] * 2,
        scratch_shapes=[pltpu.VMEM((nc, hd, c), BF16), pltpu.VMEM((nc, hd, hd), F32), pltpu.VMEM((nc, hd, hd), F32),
                        pltpu.VMEM((nc, hd, hd), BF16), pltpu.VMEM((nc, hd, hd), BF16)],
        compiler_params=_params(("parallel",)),
    )(lgf, lgb, qrot, krot, vb, gnw)


def _merge_fwd(x, z, oa, on, wb_t, wout, head=None):
    t, d = x.shape
    tm = min(256, t)
    n = t // tm

    def body(x_ref, ga_ref, gr_ref, gm0_ref, gm1_ref, oa_ref, on_ref, wb_ref, wo_ref, *rest):
        ga, gr = ga_ref[...], gr_ref[...]
        ua = ga * _sigmoid(ga) * oa_ref[...]
        ub = gr * _sigmoid(gr) * on_ref[...]
        ya = _dot(ua.astype(BF16), wb_ref[:, :512], NT)
        yb = _dot(ub.astype(BF16), wb_ref[:, 512:], NT)
        merged = _sigmoid(gm0_ref[...]) * ya + _sigmoid(gm1_ref[...]) * yb
        xn = x_ref[...] + _dot(merged.astype(BF16), wo_ref[...])
        if head is None:
            xn_ref, ya_ref, yb_ref = rest
            xn_ref[...] = xn
        else:
            g_ref, t_ref, dx_ref, ya_ref, yb_ref, dg_ref, loss_ref, acc_g, acc_l = rest
            i = pl.program_id(0)

            @pl.when(i == 0)
            def _():
                acc_g[...] = jnp.zeros_like(acc_g)
                acc_l[...] = jnp.zeros_like(acc_l)

            gv = g_ref[...]
            r = lax.rsqrt(jnp.mean(xn * xn, axis=-1, keepdims=True) + EPS)
            xh = xn * r
            err = xh * gv - t_ref[...]
            dy = err * (1.0 / d)
            gy = dy * gv
            dx_ref[...] = r * (gy - xh * jnp.mean(gy * xh, axis=-1, keepdims=True))
            acc_g[...] += jnp.sum((dy * xh).reshape(tm // 8, 8, d), axis=0)
            acc_l[...] += jnp.sum((err * err).reshape(tm // 8, 8, d), axis=0)

            @pl.when(i == n - 1)
            def _():
                dg_ref[...] = jnp.sum(acc_g[...], axis=0, keepdims=True)
                tot = jnp.sum(jnp.sum(acc_l[...], axis=0, keepdims=True), axis=1, keepdims=True)
                loss_ref[...] = jnp.broadcast_to(tot * (0.5 / d), (1, 128))
        ya_ref[...] = ya.astype(BF16)
        yb_ref[...] = yb.astype(BF16)

    row = lambda w, j: pl.BlockSpec((tm, w), lambda i: (i, j))
    const = lambda shape: pl.BlockSpec(shape, lambda i: (0, 0))
    in_specs = [row(d, 0), row(512, SEG["ga"][2] // 512), row(512, SEG["gr"][2] // 512),
                row(1024, SEG["gm"][2] // 1024), row(1024, SEG["gm"][2] // 1024 + 1),
                row(512, 0), row(512, 0), const((d, 1024)), const((d, d))]
    out_specs = [row(d, 0), row(d, 0), row(d, 0)]
    out_shape = [SDS((t, d), F32), SDS((t, d), BF16), SDS((t, d), BF16)]
    args, scratch = [x, z, z, z, z, oa, on, wb_t, wout], []
    if head is not None:
        in_specs += [const((1, d)), row(d, 0)]
        out_specs += [const((1, d)), const((1, 128))]
        out_shape += [SDS((1, d), F32), SDS((1, 128), F32)]
        args += list(head)
        scratch = [pltpu.VMEM((8, d), F32), pltpu.VMEM((8, d), F32)]
    return pl.pallas_call(
        body, name="merge_fwd", grid=(n,), in_specs=in_specs, out_specs=out_specs, out_shape=out_shape,
        scratch_shapes=scratch,
        compiler_params=_params(("arbitrary",) if head is not None else ("parallel",)),
    )(*args)


def _merge_bwd(dxo, z, oa, on, ya, yb, wb_t, wout):
    t, d = dxo.shape
    tm = min(256, t)
    n = t // tm

    def body(dx_ref, ga_ref, gr_ref, gm0_ref, gm1_ref, oa_ref, on_ref, ya_ref, yb_ref, wb_ref, wo_ref,
             doa_ref, don_ref, dz_ref, dwo_ref, dwb_ref, acc_o, acc_b):
        i = pl.program_id(0)

        @pl.when(i == 0)
        def _():
            acc_o[...] = jnp.zeros_like(acc_o)
            acc_b[...] = jnp.zeros_like(acc_b)

        dxb = dx_ref[...].astype(BF16)
        ya, yb = ya_ref[...].astype(F32), yb_ref[...].astype(F32)
        g0, g1 = _sigmoid(gm0_ref[...]), _sigmoid(gm1_ref[...])
        mb = (g0 * ya + g1 * yb).astype(BF16)
        dm = _dot(dxb, wo_ref[...], NT)
        dya = (dm * g0).astype(BF16)
        dyb = (dm * g1).astype(BF16)
        dz_ref[:, 1024:2048] = (dm * ya * g0 * (1.0 - g0)).astype(BF16)
        dz_ref[:, 2048:3072] = (dm * yb * g1 * (1.0 - g1)).astype(BF16)

        def branch(g_ref, o_ref, dy, w, do_ref, lo):
            gv, ov = g_ref[...], o_ref[...]
            sg = _sigmoid(gv)
            silu = gv * sg
            du = _dot(dy, w)
            do_ref[...] = du * silu
            dz_ref[:, lo:lo + 512] = (du * ov * (sg * (1.0 + gv * (1.0 - sg)))).astype(BF16)
            acc_b[:, lo:lo + 512] += _dot(dy, (silu * ov).astype(BF16), TN)

        branch(ga_ref, oa_ref, dya, wb_ref[:, :512], doa_ref, 0)
        branch(gr_ref, on_ref, dyb, wb_ref[:, 512:], don_ref, 512)
        acc_o[...] += _dot(mb, dxb, TN)

        @pl.when(i == n - 1)
        def _():
            dwo_ref[...] = acc_o[...].astype(BF16)
            dwb_ref[...] = acc_b[...].astype(BF16)

    row = lambda w, j: pl.BlockSpec((tm, w), lambda i: (i, j))
    const = lambda shape: pl.BlockSpec(shape, lambda i: (0, 0))
    return pl.pallas_call(
        body, name="merge_bwd", grid=(n,),
        in_specs=[row(d, 0), row(512, SEG["ga"][2] // 512), row(512, SEG["gr"][2] // 512),
                  row(1024, SEG["gm"][2] // 1024), row(1024, SEG["gm"][2] // 1024 + 1),
                  row(512, 0), row(512, 0), row(d, 0), row(d, 0), const((d, 1024)), const((d, d))],
        out_specs=[row(512, 0), row(512, 0), row(3072, 0), const((d, d)), const((d, 1024))],
        out_shape=[SDS((t, 512), F32), SDS((t, 512), F32), SDS((t, 3072), BF16), SDS((d, d), BF16),
                   SDS((d, 1024), BF16)],
        scratch_shapes=[pltpu.VMEM((d, d), F32), pltpu.VMEM((d, 1024), F32)],
        compiler_params=_params(("arbitrary",)),
    )(dxo, z, z, z, z, oa, on, ya, yb, wb_t, wout)


def _ret_bwd(qrot, krot, vb, orr, don, gnw, lgf, lgb, cos, sin):
    t = qrot.shape[0]
    c = RET_CHUNK
    nc = t // c
    hd = RET_HEAD_DIM
    unroll = 4 if nc % 4 == 0 else 1

    def body(lgf_ref, lgb_ref, q_ref, k_ref, v_ref, o_ref, dn_ref, w_ref, c_ref, s_ref,
             dq_ref, dk_ref, dv_ref, dw_ref, dlf_ref, dlb_ref, qt, kt, dob, uf, ub, wf, wb, sfa, sba, gfa, gba):
        h = pl.program_id(0)
        fw = _Dir(lgf_ref[h], False)
        bw = _Dir(lgb_ref[h], True)
        fw.dt, bw.dt = fw.d.T, bw.d.T

        o = o_ref[...]
        xc = o - jnp.mean(o, axis=-1, keepdims=True)
        r = lax.rsqrt(jnp.mean(xc * xc, axis=-1, keepdims=True) + EPS)
        xh = xc * r
        dn = dn_ref[...]
        gy = dn * w_ref[...]
        d_o = r * (gy - jnp.mean(gy, axis=-1, keepdims=True) - xh * jnp.mean(gy * xh, axis=-1, keepdims=True))
        dw_ref[...] = jnp.sum(dn * xh, axis=0, keepdims=True)
        dob[...] = d_o.astype(BF16)
        for i in range(nc):
            qt[i] = q_ref[i * c:(i + 1) * c, :].astype(F32).T.astype(BF16)
            kt[i] = k_ref[i * c:(i + 1) * c, :].astype(F32).T.astype(BF16)

        def rows(ci):
            return pl.ds(pl.multiple_of(ci * c, c), c)

        def products(ci, carry):
            sl = rows(ci)
            vv, do32 = v_ref[sl, :], dob[sl, :].astype(F32)
            ktf = kt[ci].astype(F32)
            uf[ci] = _dot((ktf * fw.kd_row).astype(BF16), vv)
            ub[ci] = _dot((ktf * bw.kd_row).astype(BF16), vv)
            wf[ci] = _dot(qt[ci], (do32 * fw.qd).astype(BF16))
            wb[ci] = _dot(qt[ci], (do32 * bw.qd).astype(BF16))
            return carry

        lax.fori_loop(0, nc, products, 0, unroll=16 if nc % 16 == 0 else unroll)

        def scan(i, carry):
            sf, sb, gf, gb = carry
            j = nc - 1 - i
            sfa[i] = sf.astype(BF16)
            sba[j] = sb.astype(BF16)
            gfa[j] = gf.astype(BF16)
            gba[i] = gb.astype(BF16)
            return sf * fw.cd + uf[i], sb * bw.cd + ub[j], gf * fw.cd + wf[j], gb * bw.cd + wb[i]

        zero = jnp.zeros((hd, hd), F32)
        lax.fori_loop(0, nc, scan, (zero, zero, zero, zero))

        def one_dir(p, s_all, g_all, ci, qq, kk, vv, do, a, bm):
            sb, gb = s_all[ci], g_all[ci]
            doq = (do.astype(F32) * p.qd).astype(BF16)
            dqc = _dot(doq, sb, NT)
            kkd = (kk.astype(F32) * p.kd_col).astype(BF16)
            dk2 = _dot(vv, gb, NT) * p.kd_col
            terms = (p.dist * p.d * a * bm + p.wq * qq.astype(F32) * dqc + p.wk * kk.astype(F32) * dk2
                     + (float(c) * p.cd) * gb.astype(F32) * sb.astype(F32))
            return dqc, dk2, _dot(kkd, gb), terms

        d_both, dt_both = fw.d + bw.d, fw.dt + bw.dt

        def chunk(ci, carry):
            af, ab = carry
            sl = rows(ci)
            qq, kk, vv, do = q_ref[sl, :], k_ref[sl, :], v_ref[sl, :], dob[sl, :]
            a, bm = _dot(qq, kk, NT), _dot(do, vv, NT)
            at, bt = _dot(kk, qq, NT), _dot(vv, do, NT)
            dqf, dkf, dvf, tf = one_dir(fw, sfa, gfa, ci, qq, kk, vv, do, a, bm)
            dqb, dkb, dvb, tb = one_dir(bw, sba, gba, ci, qq, kk, vv, do, a, bm)
            cc, ss = c_ref[sl, :], s_ref[sl, :]
            dq = _dot((bm * d_both).astype(BF16), kk) + dqf + dqb
            dk = _dot((bt * dt_both).astype(BF16), qq) + dkf + dkb
            dq_ref[sl, :] = _rope_bwd(dq, cc, ss, hd // 4).astype(BF16)
            dk_ref[sl, :] = (_rope_bwd(dk, cc, ss, hd // 4) * (hd ** -0.5)).astype(BF16)
            dv_ref[sl, :] = (_dot((at * dt_both).astype(BF16), do) + dvf + dvb).astype(BF16)
            return af + tf, ab + tb

        pair = 16 if nc % 16 == 0 else 1

        def chunks(i, carry):
            for j in range(pair):
                carry = chunk(i * pair + j, carry)
            return carry

        af, ab = lax.fori_loop(0, nc // pair, chunks, (zero, zero))
        tot = lambda m: jnp.sum(jnp.sum(m, axis=0, keepdims=True), axis=1, keepdims=True)
        dlf_ref[...] = jnp.broadcast_to(tot(af).reshape(1, 1, 1), (1, 8, 128))
        dlb_ref[...] = jnp.broadcast_to(tot(ab).reshape(1, 1, 1), (1, 8, 128))

    smem = pl.BlockSpec(memory_space=pltpu.SMEM)
    head = pl.BlockSpec((t, 128), lambda h: (0, h))
    vec = pl.BlockSpec((1, 128), lambda h: (0, h))
    scal = pl.BlockSpec((1, 8, 128), lambda h: (h, 0, 0))
    table = pl.BlockSpec((t, 128), lambda h: (0, 0))
    mats = lambda dt: pltpu.VMEM((nc, hd, hd), dt)
    return pl.pallas_call(
        body, name="ret_bwd", grid=(RET_HEADS,),
        in_specs=[smem, smem, head, head, head, head, head, vec, table, table],
        out_specs=[head, head, head, vec, scal, scal],
        out_shape=[SDS((t, RET_WIDTH), BF16)] * 3 + [SDS((1, RET_WIDTH), F32), SDS((RET_HEADS, 8, 128), F32),
                                                    SDS((RET_HEADS, 8, 128), F32)],
        scratch_shapes=[pltpu.VMEM((nc, hd, c), BF16), pltpu.VMEM((nc, hd, c), BF16), pltpu.VMEM((t, hd), BF16),
                        mats(F32), mats(F32), mats(F32), mats(F32), mats(BF16), mats(BF16), mats(BF16), mats(BF16)],
        compiler_params=_params(("parallel",)),
    )(lgf, lgb, qrot, krot, vb, orr, don, gnw, cos, sin)


def _attn_bwd(q, qt, k, v, doa, oa, lse, ex=None):
    t = q.shape[1]
    tq = min(ATTN_BWD_QUERY_TILE, t)
    nq = t // tq
    tk = min(ATTN_BWD_KEY_CHUNK, t)
    nk = t // tk
    hd = ATTN_HEAD_DIM
    scale = hd ** -0.5

    def body(q_ref, qt_ref, k_ref, v_ref, do_ref, o_ref, lse_ref, dq_ref, dkt_ref, dvt_ref):
        p, i = pl.program_id(0), pl.program_id(1)

        @pl.when(jnp.logical_and(p % 2 == 0, i == 0))
        def _():
            dkt_ref[...] = jnp.zeros_like(dkt_ref)
            dvt_ref[...] = jnp.zeros_like(dvt_ref)

        dov, ov = do_ref[...], o_ref[...]
        dovt = dov.T
        lanes = lambda col: jnp.concatenate([col] * (tk // 128), axis=1)
        outs = []
        for j in range(2):
            qq, qqt = q_ref[j], qt_ref[j]
            do32 = dov[:, j * hd:(j + 1) * hd]
            do, dot_ = do32.astype(BF16), dovt[j * hd:(j + 1) * hd, :].astype(BF16)
            dd = lanes(jnp.broadcast_to(jnp.sum(do32 * ov[:, j * hd:(j + 1) * hd], axis=1, keepdims=True), (tq, 128)))
            lse_j = lanes(jnp.broadcast_to(lse_ref[j], (128, tq)).T)
            dq = jnp.zeros((tq, hd), F32)
            for c in range(nk):
                sl = slice(c * tk, (c + 1) * tk)
                kc, vc = k_ref[0, sl, :], v_ref[0, sl, :]
                pr = jnp.exp(_dot(qq, kc, NT) - lse_j)
                ds = (pr * (_dot(do, vc, NT) - dd)).astype(BF16)
                dvt_ref[0, :, sl] += _dot(dot_, pr.astype(BF16))
                dkt_ref[0, :, sl] += _dot(qqt, ds)
                dq = dq + _dot(ds, kc)
            outs.append(dq * scale)
        dq_ref[...] = jnp.concatenate(outs, axis=-1)

    kv = pl.BlockSpec((1, t, hd), lambda p, i: (p // 2, 0, 0))
    kvt = pl.BlockSpec((1, hd, t), lambda p, i: (p // 2, 0, 0))
    pair = pl.BlockSpec((tq, 128), lambda p, i: (i, p))
    first = lambda: jnp.logical_and(pl.program_id(0) == 0, pl.program_id(1) == 0)
    last = lambda: jnp.logical_and(pl.program_id(0) == 3, pl.program_id(1) == nq - 1)
    xi, xo, xs, xscr, xargs = _ex_args(ex)
    return pl.pallas_call(
        _with_exchange(body, 7, 3, 0, ex, first, last), name="attn_bwd", grid=(4, nq),
        in_specs=[pl.BlockSpec((2, tq, hd), lambda p, i: (p, i, 0)), pl.BlockSpec((2, hd, tq), lambda p, i: (p, 0, i)),
                  kv, kv, pair, pair, pl.BlockSpec((2, 1, tq), lambda p, i: (p, 0, i))] + xi,
        out_specs=[pair, kvt, kvt] + xo,
        out_shape=[SDS((t, ATTN_WIDTH), F32), SDS((ATTN_KV_HEADS, hd, t), F32),
                   SDS((ATTN_KV_HEADS, hd, t), F32)] + xs,
        scratch_shapes=xscr,
        compiler_params=_params(("arbitrary", "arbitrary")),
    )(q, qt, k, v, doa, oa, lse, *xargs)


def _attn_post_bwd(dq, dk, dv, z, qn, kn, cos, sin, ones_bd):
    t = z.shape[0]
    tm = min(512, t)
    n = t // tm
    hd = ATTN_HEAD_DIM

    def body(dq_ref, dk_ref, dv_ref, zq_ref, zkv_ref, qn_ref, kn_ref, c_ref, s_ref, b_ref,
             dz_ref, dqn_ref, dkn_ref, acc_q, acc_k):
        i = pl.program_id(0)

        @pl.when(i == 0)
        def _():
            acc_q[...] = jnp.zeros_like(acc_q)
            acc_k[...] = jnp.zeros_like(acc_k)

        bd = b_ref[...]
        c2, s2 = c_ref[...], s_ref[...]

        def norm_bwd(dy, x, w, ones, cos_t, sin_t, acc):
            dyr = _rope_bwd(dy, cos_t, sin_t, hd // 4)
            r = lax.rsqrt(_group_mean(x * x, ones) + EPS)
            xh = x * r
            gy = dyr * w
            acc[...] += jnp.sum((dyr * xh).reshape(tm // 8, 8, x.shape[-1]), axis=0)
            return r * (gy - xh * _group_mean(gy * xh, ones))

        cq = jnp.concatenate([c2] * 4, axis=-1)
        sq = jnp.concatenate([s2] * 4, axis=-1)
        dz_ref[:, :512] = norm_bwd(dq_ref[...], zq_ref[...], qn_ref[...], bd, cq, sq, acc_q).astype(BF16)
        zkv = zkv_ref[...]
        dkk = jnp.concatenate([dk_ref[0], dk_ref[1]], axis=0).T
        dz_ref[:, 512:640] = norm_bwd(dkk, zkv[:, :128], kn_ref[...], bd[:128, :128], c2, s2, acc_k).astype(BF16)
        dz_ref[:, 640:768] = jnp.concatenate([dv_ref[0], dv_ref[1]], axis=0).T.astype(BF16)

        @pl.when(i == n - 1)
        def _():
            dqn_ref[...] = jnp.sum(acc_q[...], axis=0, keepdims=True)
            dkn_ref[...] = jnp.sum(acc_k[...], axis=0, keepdims=True)

    kv_blk = SEG["ka"][2] // 256
    kvs = pl.BlockSpec((ATTN_KV_HEADS, hd, tm), lambda i: (0, 0, i))
    const = lambda shape: pl.BlockSpec(shape, lambda i: (0, 0))
    return pl.pallas_call(
        body, name="attn_post_bwd", grid=(n,),
        in_specs=[pl.BlockSpec((tm, 512), lambda i: (i, 0)), kvs, kvs,
                  pl.BlockSpec((tm, 512), lambda i: (i, 0)), pl.BlockSpec((tm, 256), lambda i: (i, kv_blk)),
                  const((1, 512)), const((1, 128)),
                  pl.BlockSpec((tm, 128), lambda i: (i, 0)), pl.BlockSpec((tm, 128), lambda i: (i, 0)),
                  const((512, 512))],
        out_specs=[pl.BlockSpec((tm, 768), lambda i: (i, 0)), const((1, 512)), const((1, 128))],
        out_shape=[SDS((t, 768), BF16), SDS((1, 512), F32), SDS((1, 128), F32)],
        scratch_shapes=[pltpu.VMEM((8, 512), F32), pltpu.VMEM((8, 128), F32)],
        compiler_params=_params(("arbitrary",)),
    )(dq, dk, dv, z, z, qn, kn, cos, sin, ones_bd)


def _in_bwd(dxo, x, g, w_t, dz_a, dz_m, dqr, dkr, dvr, after=None):
    t, d = x.shape
    tm = min(256, t)
    n = t // tm
    parts = [(0, 0, 768, 0), (1, 0, 512, SEG["ga"][0]), (2, 0, 512, SEG["qr"][0]), (3, 0, 512, SEG["kr"][0]),
             (4, 0, 512, SEG["vr"][0]), (1, 512, 2560, SEG["gr"][0])]

    def body(dx_ref, x_ref, g_ref, w_ref, a_ref, m_ref, q_ref, k_ref, v_ref, o_ref, dg_ref, acc):
        i = pl.program_id(0)

        @pl.when(i == 0)
        def _():
            acc[...] = jnp.zeros_like(acc)

        pieces = [a_ref, m_ref, q_ref, k_ref, v_ref]
        dh = jnp.zeros((tm, d), F32)
        for pi, lo, w, row in parts:
            dh = dh + _dot(pieces[pi][:, lo:lo + w], w_ref[row:row + w, :])
        xv = x_ref[...]
        r = lax.rsqrt(jnp.mean(xv * xv, axis=-1, keepdims=True) + EPS)
        xh = xv * r
        gy = dh * g_ref[...]
        o_ref[...] = dx_ref[...] + r * (gy - xh * jnp.mean(gy * xh, axis=-1, keepdims=True))
        acc[...] += jnp.sum((dh * xh).reshape(tm // 8, 8, d), axis=0)

        @pl.when(i == n - 1)
        def _():
            dg_ref[...] = jnp.sum(acc[...], axis=0, keepdims=True)

    row = lambda w: pl.BlockSpec((tm, w), lambda i: (i, 0))
    const = lambda shape: pl.BlockSpec(shape, lambda i: (0, 0))
    extra = [] if after is None else [after]
    return pl.pallas_call(
        (lambda *refs: body(*refs[:9], *refs[9 + len(extra):])), name="in_bwd", grid=(n,),
        in_specs=[row(d), row(d), const((1, d)), const((D_IN, d)), row(768), row(3072), row(512), row(512),
                  row(512)] + [const(a.shape) for a in extra],
        out_specs=[row(d), const((1, d))],
        out_shape=[SDS((t, d), F32), SDS((1, d), F32)],
        scratch_shapes=[pltpu.VMEM((8, d), F32)],
        compiler_params=_params(("arbitrary",)),
    )(dxo, x, g, w_t, dz_a, dz_m, dqr, dkr, dvr, *extra)


def _dw_in(h_t, dz_a, dz_m, dqr, dkr, dvr):
    d, t = h_t.shape
    tn = 256
    parts = [(0, 0, 0, 3), (1, 0, SEG["ga"][0] // tn, 2), (2, 0, SEG["qr"][0] // tn, 2),
             (3, 0, SEG["kr"][0] // tn, 2), (4, 0, SEG["vr"][0] // tn, 2), (1, 2, SEG["gr"][0] // tn, 10)]
    pieces = [dz_a, dz_m, dqr, dkr, dvr]

    def col_block(pi):
        mine = [(c0, r0, n) for q, c0, r0, n in parts if q == pi]

        def index(j):
            c0, r0, n = mine[0]
            blk = c0 + jnp.clip(j - r0, 0, n - 1)
            for c0, r0, n in mine[1:]:
                blk = jnp.where(j >= r0, c0 + jnp.clip(j - r0, 0, n - 1), blk)
            return 0, blk

        return index

    def body(h_ref, *refs):
        o_ref = refs[-1]
        j = pl.program_id(0)
        for pi, _, r0, n in parts:
            @pl.when(jnp.logical_and(j >= r0, j < r0 + n))
            def _(p_ref=refs[pi]):
                o_ref[...] = _dot(h_ref[...], p_ref[...]).T.astype(BF16)

    return pl.pallas_call(
        body, name="dw_in", grid=(D_IN // tn,),
        in_specs=[pl.BlockSpec((d, t), lambda j: (0, 0))] + [pl.BlockSpec((t, tn), col_block(pi)) for pi in range(5)],
        out_specs=pl.BlockSpec((tn, d), lambda j: (j, 0)),
        out_shape=SDS((D_IN, d), BF16),
        compiler_params=_params(("arbitrary",)),
    )(h_t, *pieces)


def _adamw_math(w, g, m, v):
    mn = ADAM_B1 * m + (1.0 - ADAM_B1) * g
    vn = ADAM_B2 * v + (1.0 - ADAM_B2) * (g * g)
    m_hat = mn / (1.0 - ADAM_B1 ** ADAM_STEP)
    v_hat = vn / (1.0 - ADAM_B2 ** ADAM_STEP)
    return -ADAM_LR * (m_hat / (jnp.sqrt(v_hat) + ADAM_EPS) + ADAM_WD * w), mn, vn


def _sum_adamw(recvs, w, m, v, lane0, tn, layer0=0, prev=None, own=None):
    _, r, c = w.shape
    j0 = lane0 // tn
    n = len(recvs)
    has_own = own is not None

    def body(*refs):
        mine_ref, refs = (refs[0], refs[1:]) if has_own else (None, refs)
        w_ref, m_ref, v_ref = refs[n:n + 3]
        g_ref, d_ref, mo_ref, vo_ref = refs[-4:]

        def run(r_ref):
            def slot(s):
                if has_own:
                    return jnp.where(mine_ref[0] == s, refs[n + 3][...], r_ref[s]).astype(F32)
                return r_ref[s].astype(F32)

            g = slot(0)
            for s in range(1, N_DEV):
                g = g + slot(s)
            g_ref[0] = g
            d_ref[0], mo_ref[0], vo_ref[0] = _adamw_math(w_ref[0], g, m_ref[0], v_ref[0])

        for i in range(n):
            pl.when(pl.program_id(0) == i)(functools.partial(run, refs[i]))

    slots = pl.BlockSpec((N_DEV, r, tn), lambda i, j, *_: (0, 0, j0 + j))
    blk = pl.BlockSpec((1, r, tn), lambda i, j, *_: (layer0 + i, 0, j))
    before = [] if prev is None else list(prev)
    in_specs, args = [slots] * n + [blk] * 3, [*recvs, w, m, v]
    if has_own:
        assert n == 1
        in_specs.append(pl.BlockSpec((r, tn), lambda i, j, mine: (mine[0], j0 + j)))
        args.append(own[0])
    n_pre = len(args) + has_own
    return pl.pallas_call(
        body, name="sum_adamw",
        grid_spec=pltpu.PrefetchScalarGridSpec(
            num_scalar_prefetch=int(has_own), grid=(n, c // tn),
            in_specs=in_specs + [ANY] * len(before), out_specs=[blk] * 4),
        out_shape=[SDS(w.shape, F32)] * 4,
        input_output_aliases={n_pre + k: k for k in range(len(before))},
        compiler_params=_params(("parallel", "parallel")),
    )(*([own[1]] if has_own else []), *args, *before)


def _adamw(w, g, m, v):
    rows, cols = w.shape
    tr = 256 if rows % 256 == 0 else rows

    def body(w_ref, g_ref, m_ref, v_ref, d_ref, mo_ref, vo_ref):
        d_ref[...], mo_ref[...], vo_ref[...] = _adamw_math(w_ref[...], g_ref[...], m_ref[...], v_ref[...])

    blk = pl.BlockSpec((tr, cols), lambda i: (i, 0))
    return pl.pallas_call(
        body, name="adamw", grid=(rows // tr,),
        in_specs=[blk] * 4, out_specs=[blk] * 3, out_shape=[SDS((rows, cols), F32)] * 3,
        compiler_params=_params(("parallel",)),
    )(w, g, m, v)


def _all_gather(shards):
    na = len(shards)
    chips = (4, 2, 6)

    def body(*refs):
        ins, outs = refs[:na], refs[na:2 * na]
        send_sems, recv_sems, local_sems = refs[2 * na:]
        _, mine = _flip(0)

        def rows(a, idx):
            r = shards[a].shape[0]
            return outs[a].at[pl.ds(pl.multiple_of(idx * r, 16), r), :]

        def copy(a, slot, block_idx, to, src=None):
            return pltpu.make_async_remote_copy(
                src_ref=rows(a, block_idx) if src is None else src, dst_ref=rows(a, block_idx),
                send_sem=send_sems.at[a, slot], recv_sem=recv_sems.at[a, slot],
                device_id=to, device_id_type=MESH_ID)

        sibling, sibling_idx = _flip(1)
        local, started = [], []
        for a in range(na):
            cp = pltpu.make_async_copy(ins[a], rows(a, mine), local_sems.at[a])
            cp.start()
            local.append(cp)
            first = [copy(a, 0, mine, sibling, src=ins[a])]
            first += [copy(a, 1 + j, mine, _flip(k)[0], src=ins[a]) for j, k in enumerate(chips)]
            for cp in first:
                cp.start()
            started += first
        for a in range(na):
            for j, k in enumerate(chips):
                _, theirs = _flip(k)
                copy(a, 1 + j, theirs, _flip(0)[0]).wait_recv()
                fwd = copy(a, 4 + j, theirs, sibling)
                fwd.start()
                started.append(fwd)
        for a in range(na):
            copy(a, 0, sibling_idx, _flip(0)[0]).wait_recv()
            for j, k in enumerate(chips):
                _, theirs = _flip(k | 1)
                copy(a, 4 + j, theirs, _flip(0)[0]).wait_recv()
        for cp in started:
            cp.wait_send()
        for cp in local:
            cp.wait()

    return pl.pallas_call(
        body, name="all_gather_weights",
        in_specs=[ANY] * na, out_specs=[ANY] * na,
        out_shape=[SDS((N_DEV * s.shape[0], s.shape[1]), s.dtype) for s in shards],
        scratch_shapes=[pltpu.SemaphoreType.DMA((na, 7)), pltpu.SemaphoreType.DMA((na, 7)),
                        pltpu.SemaphoreType.DMA((na,))],
        compiler_params=pltpu.CompilerParams(has_side_effects=True),
    )(*shards)


def _scatter_blocks_of(g_ref, rows, idx):
    return g_ref.at[pl.ds(pl.multiple_of(idx * rows, 16), rows), :]


def _scatter_start(g):
    rows = g.shape[0] // N_DEV
    land_shape = (N_DEV, rows, g.shape[1])

    def body(g_ref, land_ref, send_sems, recv_sems, g_thru, land_thru, token):
        _, mine = _flip(0)
        for k in range(1, N_DEV):
            peer, theirs = _flip(k)
            pltpu.make_async_remote_copy(
                src_ref=_scatter_blocks_of(g_ref, rows, theirs), dst_ref=land_ref.at[mine],
                send_sem=send_sems.at[k - 1], recv_sem=recv_sems.at[k - 1],
                device_id=peer, device_id_type=MESH_ID).start()
        token[...] = jnp.zeros_like(token)

    hbm, sem = pl.BlockSpec(memory_space=pltpu.HBM), pl.BlockSpec(memory_space=pltpu.SEMAPHORE)
    return pl.pallas_call(
        body, name="scatter_start",
        out_shape=(pltpu.SemaphoreType.DMA((N_DEV - 1,)), pltpu.SemaphoreType.DMA((N_DEV - 1,)),
                   pltpu.HBM(g.shape, g.dtype), pltpu.HBM(land_shape, g.dtype), SDS((8, 128), F32)),
        in_specs=(hbm, hbm), out_specs=(sem, sem, hbm, hbm, pl.BlockSpec(memory_space=pltpu.VMEM)),
        input_output_aliases={0: 2, 1: 3},
        compiler_params=pltpu.CompilerParams(has_side_effects=pltpu.SideEffectType.DATAFLOW_SIDE_EFFECTING),
    )(pltpu.with_memory_space_constraint(g, pltpu.HBM),
      pltpu.with_memory_space_constraint(lax.empty(land_shape, g.dtype), pltpu.HBM))


def _scatter_wait(send_sems, recv_sems, g_thru, land_thru, after):
    rows = g_thru.shape[0] // N_DEV

    def body(g_ref, land_ref, send_sems, recv_sems, *rest):
        me, _ = _flip(0)
        for k in range(1, N_DEV):
            _, theirs = _flip(k)
            copy = pltpu.make_async_remote_copy(
                src_ref=_scatter_blocks_of(g_ref, rows, theirs), dst_ref=land_ref.at[theirs],
                send_sem=send_sems.at[k - 1], recv_sem=recv_sems.at[k - 1],
                device_id=me, device_id_type=MESH_ID)
            copy.wait_send()
            copy.wait_recv()

    hbm, sem = pl.BlockSpec(memory_space=pltpu.HBM), pl.BlockSpec(memory_space=pltpu.SEMAPHORE)
    return pl.pallas_call(
        body, name="scatter_wait",
        out_shape=(pltpu.HBM(g_thru.shape, g_thru.dtype), pltpu.HBM(land_thru.shape, land_thru.dtype)),
        in_specs=(hbm, hbm, sem, sem) + (ANY,) * len(after), out_specs=(hbm, hbm), input_output_aliases={0: 0, 1: 1},
        compiler_params=pltpu.CompilerParams(has_side_effects=pltpu.SideEffectType.DATAFLOW_SIDE_EFFECTING),
    )(g_thru, land_thru, send_sems, recv_sems, *after)


def _all_reduce_small(packed):
    shape = packed.shape

    def body(p_ref, o_ref, slots, send_sems, recv_sems):
        me, mine = _flip(0)
        slots[mine] = p_ref[...]
        sends = []
        for k in range(1, N_DEV):
            peer, _ = _flip(k)
            cp = pltpu.make_async_remote_copy(
                src_ref=p_ref, dst_ref=slots.at[mine], send_sem=send_sems.at[k - 1], recv_sem=recv_sems.at[k - 1],
                device_id=peer, device_id_type=MESH_ID)
            cp.start()
            sends.append(cp)
        for k in range(1, N_DEV):
            _, theirs = _flip(k)
            pltpu.make_async_remote_copy(
                src_ref=p_ref, dst_ref=slots.at[theirs], send_sem=send_sems.at[k - 1],
                recv_sem=recv_sems.at[k - 1], device_id=me, device_id_type=MESH_ID).wait_recv()
        for cp in sends:
            cp.wait_send()
        acc = slots[0]
        for s in range(1, N_DEV):
            acc = acc + slots[s]
        o_ref[...] = acc

    vm = pl.BlockSpec(memory_space=pltpu.VMEM)
    return pl.pallas_call(
        body, name="all_reduce_small", in_specs=[vm], out_specs=vm, out_shape=SDS(shape, F32),
        scratch_shapes=[pltpu.VMEM((N_DEV,) + shape, F32), pltpu.SemaphoreType.DMA((7,)),
                        pltpu.SemaphoreType.DMA((7,))],
        compiler_params=pltpu.CompilerParams(has_side_effects=True),
    )(packed)


def _layer_fwd(x, p, tabs, ex):
    z, h_t, q, qt, k, v, vt, qrot, krot, vb = _in_proj(x, p["norm_g"], p["w_in_t"], p["qn"], p["kn"], tabs["ca"],
                                                       tabs["sa"], tabs["ones"], tabs["cr"], tabs["sr"])
    oa, lse, *gathered = _attn_fwd(q, k, vt, ex)
    orr, on = _ret_fwd(qrot, krot, vb, p["lgf"], p["lgb"], p["gnw"])
    return z, h_t, q, qt, k, v, lse, oa, qrot, krot, vb, orr, on, gathered


def _layer_bwd(dxo, s, p, tabs, ex_attn, scatter_w_in):
    doa, don, dz_m, d_wout, d_wb_t = _merge_bwd(dxo, s["z"], s["oa"], s["on"], s["ya"], s["yb"], p["wb_t"], p["w_out"])
    dq_a, dk_a, dv_a, *recv_attn = _attn_bwd(s["q"], s["qt"], s["k"], s["v"], doa, s["oa"], s["lse"],
                                              ex_attn(d_wb_t, d_wout))
    dz_a, d_qn, d_kn = _attn_post_bwd(dq_a, dk_a, dv_a, s["z"], p["qn"], p["kn"], tabs["ca"], tabs["sa"],
                                      tabs["ones"])
    dqr, dkr, dvr, d_gnw, d_lgf, d_lgb = _ret_bwd(s["qrot"], s["krot"], s["vb"], s["orr"], don, p["gnw"],
                                                  p["lgf"], p["lgb"], tabs["cr"], tabs["sr"])
    buf = _dw_in(s["h_t"], dz_a, dz_m, dqr, dkr, dvr)
    pending, token = None, None
    if scatter_w_in:
        *pending, token = _scatter_start(buf)
    dx, d_norm_g = _in_bwd(dxo, s["x"], p["norm_g"], p["w_in_t"], dz_a, dz_m, dqr, dkr, dvr, token)
    grads = dict(w_in_t=buf, wb_t=d_wb_t, w_out=d_wout, norm_g=d_norm_g, gnw=d_gnw,
                 qn=d_qn.reshape(ATTN_Q_HEADS, ATTN_HEAD_DIM).sum(axis=0),
                 kn=d_kn.reshape(ATTN_KV_HEADS, ATTN_HEAD_DIM).sum(axis=0),
                 lgf=d_lgf[:, 0, 0], lgb=d_lgb[:, 0, 0])
    return dx, grads, recv_attn, pending


def _adamw_nd(w, g, m, v):
    shape = w.shape
    two_d = (1, shape[0]) if w.ndim == 1 else (-1, shape[-1])
    out = _adamw(w.reshape(two_d), g.reshape(two_d), m.reshape(two_d), v.reshape(two_d))
    return tuple(o.reshape(shape) for o in out)


def kernel(x, norm_g, w_in, attn_q_norm, attn_k_norm, ret_decay_fwd, ret_decay_bwd, ret_gn_w, w_branch_attn, w_branch_ret, w_out, final_norm_g, loss_target, m_norm_g, m_w_in, m_attn_q_norm, m_attn_k_norm, m_ret_decay_fwd, m_ret_decay_bwd, m_ret_gn_w, m_w_branch_attn, m_w_branch_ret, m_w_out, m_final_norm_g, v_norm_g, v_w_in, v_attn_q_norm, v_attn_k_norm, v_ret_decay_fwd, v_ret_decay_bwd, v_ret_gn_w, v_w_branch_attn, v_w_branch_ret, v_w_out, v_final_norm_g):
    t, d = x.shape[1], x.shape[2]
    x2, target = x[0], loss_target[0]

    w_in_sh, wb_sh, wout_sh = [], [], []
    for l in range(DEPTH):
        w_in_sh.append(jnp.swapaxes(w_in[l], 0, 1).astype(BF16))
        wb_sh.append(jnp.concatenate([w_branch_attn[l].T, w_branch_ret[l].T], axis=1).astype(BF16))
        wout_sh.append(w_out[l].astype(BF16))

    ca, sa = _rope_tables(t, ATTN_HEAD_DIM)
    cr, sr = _rope_tables(t, RET_HEAD_DIM)
    grp = jnp.arange(ATTN_WIDTH) // ATTN_HEAD_DIM
    tabs = dict(ca=jnp.tile(ca, (1, 2)), sa=jnp.tile(sa, (1, 2)), cr=cr, sr=sr,
                ones=jnp.where(grp[:, None] == grp[None, :], 1.0 / ATTN_HEAD_DIM, 0.0).astype(BF16))
    layers = []
    for l in range(DEPTH):
        layers.append(dict(
            norm_g=norm_g[l][None], qn=jnp.tile(attn_q_norm[l], ATTN_Q_HEADS)[None],
            kn=jnp.tile(attn_k_norm[l], ATTN_KV_HEADS)[None], gnw=ret_gn_w[l][None],
            lgf=jax.nn.log_sigmoid(ret_decay_fwd[l]), lgb=jax.nn.log_sigmoid(ret_decay_bwd[l])))

    layers[0]["w_in_t"], = _all_gather([w_in_sh[0]])
    gathers = [_Exchange("gather", [wb_sh[0], wout_sh[0], w_in_sh[1]]), _Exchange("gather", [wb_sh[1], wout_sh[1]])]
    h = x2
    saved = []
    for l in range(DEPTH):
        p = layers[l]
        z, h_t, q, qt, k, v, lse, oa, qrot, krot, vb, orr, on, got = _layer_fwd(h, p, tabs, gathers[l])
        p["wb_t"], p["w_out"] = got[0], got[1]
        if l == 0:
            layers[1]["w_in_t"] = got[2]
        last = (final_norm_g[None], target) if l == DEPTH - 1 else None
        xn, ya, yb, *loss_head = _merge_fwd(h, z, oa, on, p["wb_t"], p["w_out"], last)
        saved.append(dict(x=h, z=z, h_t=h_t, q=q, qt=qt, k=k, v=v, lse=lse, oa=oa, qrot=qrot, krot=krot, vb=vb,
                          orr=orr, on=on, ya=ya, yb=yb))
        h = xn
    dx, (d_final_g, loss_part) = h, loss_head

    grads = [None] * DEPTH
    dx, grads[1], _, _ = _layer_bwd(dx, saved[1], layers[1], tabs, lambda *a: None, False)
    g1 = grads[1]
    ex_attn = lambda d_wb_t, d_wout: _Exchange("scatter", [g1["w_in_t"], g1["wb_t"], g1["w_out"], d_wb_t, d_wout])
    dx, grads[0], recv_attn, pending = _layer_bwd(dx, saved[0], layers[0], tabs, ex_attn, True)
    recv = [None, recv_attn[3], recv_attn[4], recv_attn[0], recv_attn[1], recv_attn[2]]
    tr = lambda a: jnp.swapaxes(a, 1, 2)
    w_in_t = (tr(w_in), tr(m_w_in), tr(v_w_in))
    sharded = {}
    w_in_l1 = _sum_adamw([recv[3]], *w_in_t, 0, 256, layer0=1)
    sharded[id(w_branch_attn)] = [tr(o) for o in _sum_adamw(
        [recv[1], recv[4]], tr(w_branch_attn), tr(m_w_branch_attn), tr(v_w_branch_attn), 0, 512)]
    sharded[id(w_branch_ret)] = [tr(o) for o in _sum_adamw(
        [recv[1], recv[4]], tr(w_branch_ret), tr(m_w_branch_ret), tr(v_w_branch_ret), 512, 512)]
    sharded[id(w_out)] = _sum_adamw([recv[2], recv[5]], w_out, m_w_out, v_w_out, 0, 256)
    g_wba, g_wbr, g_wout = (sharded[id(w)][0] for w in (w_branch_attn, w_branch_ret, w_out))

    packed = jnp.zeros((8, 1024), F32)
    for l in range(DEPTH):
        gl = grads[l]
        packed = packed.at[l].set(gl["norm_g"][0])
        packed = packed.at[2, 512 * l:512 * (l + 1)].set(gl["gnw"][0])
        packed = packed.at[4, 128 * l:128 * l + 64].set(gl["qn"])
        packed = packed.at[4, 256 + 128 * l:256 + 128 * l + 64].set(gl["kn"])
        packed = packed.at[4, 512 + 128 * l:512 + 128 * l + 4].set(gl["lgf"])
        packed = packed.at[4, 768 + 128 * l:768 + 128 * l + 4].set(gl["lgb"])
    packed = packed.at[3].set(d_final_g[0])
    packed = packed.at[5, 0].set(loss_part[0, 0])
    red = _all_reduce_small(packed)
    loss = red[5, 0]
    g_norm_g = red[0:2]
    g_gnw = red[2].reshape(DEPTH, RET_WIDTH)
    g_final = red[3]
    g_qn = jnp.stack([red[4, 128 * l:128 * l + 64] for l in range(DEPTH)])
    g_kn = jnp.stack([red[4, 256 + 128 * l:256 + 128 * l + 64] for l in range(DEPTH)])
    g_lgf = jnp.stack([red[4, 512 + 128 * l:512 + 128 * l + 4] for l in range(DEPTH)])
    g_lgb = jnp.stack([red[4, 768 + 128 * l:768 + 128 * l + 4] for l in range(DEPTH)])
    g_df = g_lgf * jax.nn.sigmoid(-ret_decay_fwd)
    g_db = g_lgb * jax.nn.sigmoid(-ret_decay_bwd)

    grad_w = [g_norm_g, None, g_qn, g_kn, g_df, g_db, g_gnw, g_wba, g_wbr, g_wout, g_final]
    weights = [norm_g, w_in, attn_q_norm, attn_k_norm, ret_decay_fwd, ret_decay_bwd, ret_gn_w, w_branch_attn,
               w_branch_ret, w_out, final_norm_g]
    ms = [m_norm_g, m_w_in, m_attn_q_norm, m_attn_k_norm, m_ret_decay_fwd, m_ret_decay_bwd, m_ret_gn_w,
          m_w_branch_attn, m_w_branch_ret, m_w_out, m_final_norm_g]
    vs = [v_norm_g, v_w_in, v_attn_q_norm, v_attn_k_norm, v_ret_decay_fwd, v_ret_decay_bwd, v_ret_gn_w,
          v_w_branch_attn, v_w_branch_ret, v_w_out, v_final_norm_g]
    upd = [None if w is w_in else sharded[id(w)][1:] if id(w) in sharded else _adamw_nd(w, g, m, v)
           for w, g, m, v in zip(weights, grad_w, ms, vs)]

    done = [dx, w_in_l1[0], g_wout] + [u[0] for w, u in zip(weights, upd) if u is not None and id(w) not in sharded]
    g_full, recv[0] = _scatter_wait(*pending, done)
    mine = (4 * lax.axis_index("x") + 2 * lax.axis_index("y") + lax.axis_index("c")).astype(jnp.int32)[None]
    w_in_upd = [tr(o) for o in _sum_adamw([recv[0]], *w_in_t, 0, 256, layer0=0, prev=w_in_l1, own=(g_full, mine))]
    grad_w[1], upd[1] = w_in_upd[0], w_in_upd[1:]
    return (loss, dx[None], *grad_w, *[u[0] for u in upd], *[u[1] for u in upd], *[u[2] for u in upd])
```

```python
import functools

import jax
import jax.numpy as jnp
from jax import lax
from jax.experimental import pallas as pl
from jax.experimental.pallas import tpu as pltpu

F32 = jnp.float32
BF16 = jnp.bfloat16
SDS = jax.ShapeDtypeStruct

D_MODEL = 1024
DEPTH = 2
GRID_W = 64
ATTN_Q_HEADS = 8
ATTN_KV_HEADS = 2
ATTN_HEAD_DIM = 64
ATTN_WIDTH = 512
ATTN_KV_WIDTH = 128
RET_HEADS = 4
RET_HEAD_DIM = 128
RET_WIDTH = 512
RET_CHUNK = 128
ATTN_KEY_CHUNK = 512
ATTN_BWD_KEY_CHUNK = 512
ATTN_BWD_QUERY_TILE = 1024
ATTN_FWD_QUERY_TILE = 512
QK_DOTS_PER_CHUNK = 1
EXP_LAG = 3
ROPE_THETA = 10000.0
EPS = 1e-6
D_IN = 5376
N_DEV = 8

ADAM_LR = 0.001
ADAM_B1 = 0.9
ADAM_B2 = 0.999
ADAM_EPS = 1e-08
ADAM_WD = 0.01
ADAM_STEP = 10

SEG = {
    "qa": (0, 512, 0),
    "ga": (768, 512, 512),
    "qr": (1280, 512, 1024),
    "kr": (1792, 512, 1536),
    "vr": (2304, 512, 2048),
    "gr": (2816, 512, 2560),
    "gm": (3328, 2048, 3072),
    "ka": (512, 128, 5120),
    "va": (640, 128, 5248),
}

VMEM_LIMIT = 60 * 1024 * 1024
NT = (((1,), (1,)), ((), ()))
TN = (((0,), (0,)), ((), ()))
MESH_ID = pl.DeviceIdType.MESH
ANY = pl.BlockSpec(memory_space=pl.ANY)


def _params(sem=None, vmem=VMEM_LIMIT):
    return pltpu.CompilerParams(dimension_semantics=sem, vmem_limit_bytes=vmem)


def _dot(a, b, dims=None):
    if dims is None:
        return jnp.dot(a, b, preferred_element_type=F32)
    return lax.dot_general(a, b, dims, preferred_element_type=F32)


def _sigmoid(x):
    return 1.0 / (1.0 + jnp.exp(-x))


def _swap_halves(x, q):
    n = x.shape[-1]
    axis = x.ndim - 1
    lane = lax.broadcasted_iota(jnp.int32, x.shape, axis)
    first = (lane % (2 * q)) < q
    return jnp.where(first, pltpu.roll(x, n - q, axis), pltpu.roll(x, q, axis))


def _rope(x, cos, sin_signed, q):
    return x * cos + _swap_halves(x, q) * sin_signed


def _rope_bwd(dy, cos, sin_signed, q):
    return dy * cos - _swap_halves(dy, q) * sin_signed


def _group_mean(v, ones_bd):
    hi = v.astype(BF16)
    lo = (v - hi.astype(F32)).astype(BF16)
    return _dot(hi, ones_bd) + _dot(lo, ones_bd)


def _rope_tables(t, head_dim):
    n_rows = t // GRID_W
    d_axis = head_dim // 2
    inv_freq = ROPE_THETA ** (-jnp.arange(0, d_axis, 2, dtype=F32) / d_axis)
    ar = jnp.arange(n_rows, dtype=F32)[:, None] * inv_freq
    ac = jnp.arange(GRID_W, dtype=F32)[:, None] * inv_freq
    by_row = lambda a: jnp.repeat(a, GRID_W, axis=0)
    by_col = lambda a: jnp.tile(a, (n_rows, 1))
    cr, sr, cc, sc = by_row(jnp.cos(ar)), by_row(jnp.sin(ar)), by_col(jnp.cos(ac)), by_col(jnp.sin(ac))
    return jnp.concatenate([cr, cr, cc, cc], axis=-1), jnp.concatenate([-sr, sr, -sc, sc], axis=-1)


def _me():
    return lax.axis_index("x"), lax.axis_index("y"), lax.axis_index("c")


def _flip(k):
    x, y, c = _me()
    px = 1 - x if k & 4 else x
    py = 1 - y if k & 2 else y
    pc = 1 - c if k & 1 else c
    return (px, py, pc), 4 * px + 2 * py + pc


class _Exchange:
    def __init__(self, kind, srcs):
        self.kind, self.srcs, self.n = kind, list(srcs), len(srcs)
        self.rows = [a.shape[0] if kind == "gather" else a.shape[0] // N_DEV for a in srcs]
        if kind == "gather":
            self.out_shape = [SDS((N_DEV * a.shape[0], a.shape[1]), a.dtype) for a in srcs]
        else:
            self.out_shape = [SDS((N_DEV, a.shape[0] // N_DEV, a.shape[1]), a.dtype) for a in srcs]
        self.scratch = [pltpu.SemaphoreType.DMA((self.n, N_DEV - 1)), pltpu.SemaphoreType.DMA((self.n, N_DEV - 1)),
                        pltpu.SemaphoreType.DMA((self.n,))]

    def _block(self, ref, a, idx):
        r = self.rows[a]
        return ref.at[pl.ds(pl.multiple_of(idx * r, 16), r), :]

    def _src(self, ins, a, idx):
        return ins[a] if self.kind == "gather" else self._block(ins[a], a, idx)

    def _dst(self, outs, a, idx):
        return self._block(outs[a], a, idx) if self.kind == "gather" else outs[a].at[idx]

    def _copies(self, ins, outs, sems):
        send_sems, recv_sems, local_sems = sems
        me, mine = _flip(0)
        local, sends, recvs = [], [], []
        for a in range(self.n):
            local.append(pltpu.make_async_copy(self._src(ins, a, mine), self._dst(outs, a, mine), local_sems.at[a]))
            for k in range(1, N_DEV):
                peer, theirs = _flip(k)
                sem = dict(send_sem=send_sems.at[a, k - 1], recv_sem=recv_sems.at[a, k - 1])
                sends.append(pltpu.make_async_remote_copy(
                    src_ref=self._src(ins, a, theirs), dst_ref=self._dst(outs, a, mine),
                    device_id=peer, device_id_type=MESH_ID, **sem))
                recvs.append(pltpu.make_async_remote_copy(
                    src_ref=self._dst(outs, a, theirs), dst_ref=self._dst(outs, a, theirs),
                    device_id=me, device_id_type=MESH_ID, **sem))
        return local, sends, recvs

    def start(self, ins, outs, sems):
        local, sends, _ = self._copies(ins, outs, sems)
        for cp in local + sends:
            cp.start()

    def wait(self, ins, outs, sems):
        local, sends, recvs = self._copies(ins, outs, sems)
        for cp in sends:
            cp.wait_send()
        for cp in recvs:
            cp.wait_recv()
        for cp in local:
            cp.wait()


def _with_exchange(body, n_in, n_out, n_scratch, ex, first, last):
    if ex is None:
        return body

    def wrapped(*refs):
        ins = refs[:n_in]
        ex_ins = refs[n_in:n_in + ex.n]
        outs = refs[n_in + ex.n:n_in + ex.n + n_out]
        ex_outs = refs[n_in + ex.n + n_out:n_in + 2 * ex.n + n_out]
        rest = refs[n_in + 2 * ex.n + n_out:]
        scratch, sems = rest[:n_scratch], rest[n_scratch:]

        @pl.when(first())
        def _():
            ex.start(ex_ins, ex_outs, sems)

        body(*ins, *outs, *scratch)

        @pl.when(last())
        def _():
            ex.wait(ex_ins, ex_outs, sems)

    return wrapped


def _ex_args(ex):
    if ex is None:
        return [], [], [], [], []
    return [ANY] * ex.n, [ANY] * ex.n, list(ex.out_shape), list(ex.scratch), list(ex.srcs)


def _in_proj(x, g, w_t, qn, kn, cos, sin, ones_bd, cos_r, sin_r):
    t, d = x.shape
    tm = min(512, t)
    tk = min(ATTN_KEY_CHUNK, t)
    per_chunk = tk // tm
    hd = ATTN_HEAD_DIM

    def body(x_ref, g_ref, w_ref, qn_ref, kn_ref, c_ref, s_ref, b_ref, cr_ref, sr_ref,
             z_ref, ht_ref, q_out, qt_out, k_out, v_out, vt_out, qr_out, kr_out, vr_out):
        xv = x_ref[...]
        r = lax.rsqrt(jnp.mean(xv * xv, axis=-1, keepdims=True) + EPS)
        h = xv * r * g_ref[...]
        ht_ref[...] = h.T.astype(BF16)
        hb = h.astype(BF16)
        def project(name):
            nat, w, off = SEG[name]
            zs = _dot(hb, w_ref[nat:nat + w, :], NT)
            z_ref[:, off:off + w] = zs
            return zs

        seg = {name: project(name) for name in ("qa", "ka", "va")}
        bd = b_ref[...]
        c2, s2 = c_ref[...], s_ref[...]
        cq = jnp.concatenate([c2] * 4, axis=-1)
        sq = jnp.concatenate([s2] * 4, axis=-1)
        xq, xk, xvv = seg["qa"], seg["ka"], seg["va"]
        yq = xq * lax.rsqrt(_group_mean(xq * xq, bd) + EPS) * qn_ref[...]
        yq = _rope(yq, cq, sq, hd // 4) * (hd ** -0.5)
        yqt = yq.T
        for hh in range(ATTN_Q_HEADS):
            q_out[hh] = yq[:, hh * hd:(hh + 1) * hd].astype(BF16)
            qt_out[hh] = yqt[hh * hd:(hh + 1) * hd, :].astype(BF16)
        yk = xk * lax.rsqrt(_group_mean(xk * xk, bd[:ATTN_KV_WIDTH, :ATTN_KV_WIDTH]) + EPS) * kn_ref[...]
        yk = _rope(yk, c2, s2, hd // 4)
        xvt = xvv.T
        ones = jnp.ones((hd, tm), F32)
        for hh in range(ATTN_KV_HEADS):
            k_out[hh] = yk[:, hh * hd:(hh + 1) * hd].astype(BF16)
            v_out[hh] = xvv[:, hh * hd:(hh + 1) * hd].astype(BF16)
            vt_out[hh, 0] = jnp.concatenate([xvt[hh * hd:(hh + 1) * hd, :], ones], axis=0).astype(BF16)
        rd = RET_HEAD_DIM
        cr = jnp.concatenate([cr_ref[...]] * RET_HEADS, axis=-1)
        sr = jnp.concatenate([sr_ref[...]] * RET_HEADS, axis=-1)
        qr_out[...] = _rope(project("qr"), cr, sr, rd // 4).astype(BF16)
        kr_out[...] = (_rope(project("kr"), cr, sr, rd // 4) * (rd ** -0.5)).astype(BF16)
        vr_out[...] = project("vr").astype(BF16)
        for name in ("ga", "gr", "gm"):
            project(name)

    const = lambda shape: pl.BlockSpec(shape, lambda i: (0,) * len(shape))
    rows = lambda w: pl.BlockSpec((tm, w), lambda i: (i, 0))
    return pl.pallas_call(
        body, name="in_proj", grid=(t // tm,),
        in_specs=[rows(d), const((1, d)),
                  pl.BlockSpec((D_IN, d), lambda i: (0, 0), pipeline_mode=pl.Buffered(1)),
                  const((1, 512)), const((1, 128)), rows(128), rows(128),
                  const((512, 512)), rows(128), rows(128)],
        out_specs=[rows(D_IN), pl.BlockSpec((d, tm), lambda i: (0, i)),
                   pl.BlockSpec((ATTN_Q_HEADS, tm, hd), lambda i: (0, i, 0)),
                   pl.BlockSpec((ATTN_Q_HEADS, hd, tm), lambda i: (0, 0, i)),
                   pl.BlockSpec((ATTN_KV_HEADS, tm, hd), lambda i: (0, i, 0)),
                   pl.BlockSpec((ATTN_KV_HEADS, tm, hd), lambda i: (0, i, 0)),
                   pl.BlockSpec((ATTN_KV_HEADS, 1, 2 * hd, tm), lambda i: (0, i // per_chunk, 0, i % per_chunk)),
                   rows(RET_WIDTH), rows(RET_WIDTH), rows(RET_WIDTH)],
        out_shape=[SDS((t, D_IN), F32), SDS((d, t), BF16),
                   SDS((ATTN_Q_HEADS, t, hd), BF16), SDS((ATTN_Q_HEADS, hd, t), BF16),
                   SDS((ATTN_KV_HEADS, t, hd), BF16), SDS((ATTN_KV_HEADS, t, hd), BF16),
                   SDS((ATTN_KV_HEADS, t // tk, 2 * hd, tk), BF16)] + [SDS((t, RET_WIDTH), BF16)] * 3,
        compiler_params=_params(("parallel",)),
    )(x, g, w_t, qn, kn, cos, sin, ones_bd, cos_r, sin_r)


def _attn_fwd(q, k, vt, ex=None):
    t = q.shape[1]
    tq = min(ATTN_FWD_QUERY_TILE, t)
    nk, tk = vt.shape[1], vt.shape[3]
    hd = ATTN_HEAD_DIM
    g = ATTN_Q_HEADS // ATTN_KV_HEADS

    def body(q_ref, k_ref, vt_ref, o_ref, lse_ref, s_scr):
        def pass_a(h, c, m8):
            part = tk // QK_DOTS_PER_CHUNK
            for lo in range(c * tk, (c + 1) * tk, part):
                st = _dot(k_ref[0, lo:lo + part, :], q_ref[h], NT)
                s_scr[h % 2, lo:lo + part, :] = st
                m8 = jnp.maximum(m8, jnp.max(st.reshape(part // 8, 8, tq), axis=0))
            return m8

        def pass_b(h, c, m, acc, after):
            e = jnp.exp(s_scr[h % 2, c * tk:(c + 1) * tk, :] - (m + after * 0.0)).astype(BF16)
            return acc + _dot(vt_ref[0, c], e)

        neg = jnp.full((8, tq), -jnp.inf, F32)
        m8 = neg
        for c in range(nk):
            m8 = pass_a(0, c, m8)
        outs = []
        for h in range(g):
            m = jnp.max(m8, axis=0, keepdims=True)
            acc = jnp.zeros((2 * hd, tq), F32)
            m8 = neg
            done = [m] * EXP_LAG
            for c in range(nk):
                if h + 1 < g:
                    m8 = pass_a(h + 1, c, m8)
                acc = pass_b(h, c, m, acc, done[-EXP_LAG])
                done.append(m8[0:1, :] if h + 1 < g else acc[hd:hd + 1, :])
            l = acc[hd:hd + 1, :]
            outs.append((acc[:hd, :] / l).T)
            lse_ref[h] = m + jnp.log(l)
        o_ref[...] = jnp.concatenate(outs, axis=-1)

    nq = t // tq
    first = lambda: jnp.logical_and(pl.program_id(0) == 0, pl.program_id(1) == 0)
    last = lambda: jnp.logical_and(pl.program_id(0) == ATTN_KV_HEADS - 1, pl.program_id(1) == nq - 1)
    xi, xo, xs, xscr, xargs = _ex_args(ex)
    return pl.pallas_call(
        _with_exchange(body, 3, 2, 1, ex, first, last), name="attn_fwd", grid=(ATTN_KV_HEADS, nq),
        in_specs=[pl.BlockSpec((g, tq, hd), lambda p, i: (p, i, 0)),
                  pl.BlockSpec((1, t, hd), lambda p, i: (p, 0, 0)),
                  pl.BlockSpec((1, nk, 2 * hd, tk), lambda p, i: (p, 0, 0, 0))] + xi,
        out_specs=[pl.BlockSpec((tq, g * hd), lambda p, i: (i, p)),
                   pl.BlockSpec((g, 1, tq), lambda p, i: (p, 0, i))] + xo,
        out_shape=[SDS((t, ATTN_WIDTH), F32), SDS((ATTN_Q_HEADS, 1, t), F32)] + xs,
        scratch_shapes=[pltpu.VMEM((2, t, tq), F32)] + xscr,
        compiler_params=_params(("arbitrary", "arbitrary")),
    )(q, k, vt, *xargs)


class _Dir:
    def __init__(self, lg, strict_future):
        c = RET_CHUNK
        ia = lax.broadcasted_iota(jnp.int32, (c, c), 0).astype(F32)
        ib = lax.broadcasted_iota(jnp.int32, (c, c), 1).astype(F32)
        col = lax.broadcasted_iota(jnp.int32, (c, 1), 0).astype(F32)
        row = lax.broadcasted_iota(jnp.int32, (1, c), 1).astype(F32)
        if strict_future:
            dist = ib - ia
            mask = dist > 0
            self.wq, self.wk, wk_row = c - col, col, row
        else:
            dist = ia - ib
            mask = dist >= 0
            self.wq, self.wk, wk_row = col + 1.0, c - 1.0 - col, c - 1.0 - row
        self.dist = jnp.maximum(dist, 0.0)
        self.d = jnp.where(mask, jnp.exp(self.dist * lg), 0.0)
        self.qd = jnp.exp(self.wq * lg)
        self.kd_col = jnp.exp(self.wk * lg)
        self.kd_row = jnp.exp(wk_row * lg)
        self.cd = jnp.exp(jnp.full((1, 1), float(c), F32) * lg)


def _ret_fwd(qrot, krot, vb, lgf, lgb, gnw):
    t = qrot.shape[0]
    c = RET_CHUNK
    nc = t // c
    hd = RET_HEAD_DIM
    unroll = 4 if nc % 4 == 0 else 1

    def body(lgf_ref, lgb_ref, qo_ref, ko_ref, vo_ref, w_ref, orr_ref, on_ref, kt, uf, ub, sfa, sba):
        h = pl.program_id(0)
        fw = _Dir(lgf_ref[h], False)
        bw = _Dir(lgb_ref[h], True)
        for i in range(nc):
            kt[i] = ko_ref[i * c:(i + 1) * c, :].astype(F32).T.astype(BF16)

        def rows(ci):
            return pl.ds(pl.multiple_of(ci * c, c), c)

        def kv_products(ci, carry):
            vv = vo_ref[rows(ci), :]
            ktf = kt[ci].astype(F32)
            uf[ci] = _dot((ktf * fw.kd_row).astype(BF16), vv)
            ub[ci] = _dot((ktf * bw.kd_row).astype(BF16), vv)
            return carry

        lax.fori_loop(0, nc, kv_products, 0, unroll=16 if nc % 16 == 0 else unroll)

        def scan(i, carry):
            sf, sb = carry
            j = nc - 1 - i
            sfa[i] = sf.astype(BF16)
            sba[j] = sb.astype(BF16)
            return sf * fw.cd + uf[i], sb * bw.cd + ub[j]

        zero = jnp.zeros((hd, hd), F32)
        lax.fori_loop(0, nc, scan, (zero, zero))
        gw = w_ref[...]

        def outputs(ci, carry):
            sl = rows(ci)
            qq, kk, vv = qo_ref[sl, :], ko_ref[sl, :], vo_ref[sl, :]
            a = _dot(qq, kk, NT)
            o = (_dot((a * fw.d).astype(BF16), vv) + _dot(qq, sfa[ci]) * fw.qd
                 + _dot((a * bw.d).astype(BF16), vv) + _dot(qq, sba[ci]) * bw.qd)
            orr_ref[sl, :] = o
            xc = o - jnp.mean(o, axis=-1, keepdims=True)
            var = jnp.mean(xc * xc, axis=-1, keepdims=True)
            on_ref[sl, :] = xc * lax.rsqrt(var + EPS) * gw
            return carry

        group = 32 if nc % 32 == 0 else 1

        def output_group(i, carry):
            for j in range(group):
                outputs(i * group + j, carry)
            return carry

        lax.fori_loop(0, nc // group, output_group, 0)

    smem = pl.BlockSpec(memory_space=pltpu.SMEM)
    head = pl.BlockSpec((t, 128), lambda h: (0, h))
    return pl.pallas_call(
        body, name="ret_fwd", grid=(RET_HEADS,),
        in_specs=[smem, smem, head, head, head, pl.BlockSpec((1, 128), lambda h: (0, h))],
        out_specs=[head, head],
        out_shape=[SDS((t, RET_WIDTH), F32)] * 2,
        scratch_shapes=[pltpu.VMEM((nc, hd, c), BF16), pltpu.VMEM((nc, hd, hd), F32), pltpu.VMEM((nc, hd, hd), F32),
                        pltpu.VMEM((nc, hd, hd), BF16), pltpu.VMEM((nc, hd, hd), BF16)],
        compiler_params=_params(("parallel",)),
    )(lgf, lgb, qrot, krot, vb, gnw)


def _merge_fwd(x, z, oa, on, wb_t, wout, head=None):
    t, d = x.shape
    tm = min(256, t)
    n = t // tm

    def body(x_ref, ga_ref, gr_ref, gm0_ref, gm1_ref, oa_ref, on_ref, wb_ref, wo_ref, *rest):
        ga, gr = ga_ref[...], gr_ref[...]
        ua = ga * _sigmoid(ga) * oa_ref[...]
        ub = gr * _sigmoid(gr) * on_ref[...]
        ya = _dot(ua.astype(BF16), wb_ref[:, :512], NT)
        yb = _dot(ub.astype(BF16), wb_ref[:, 512:], NT)
        merged = _sigmoid(gm0_ref[...]) * ya + _sigmoid(gm1_ref[...]) * yb
        xn = x_ref[...] + _dot(merged.astype(BF16), wo_ref[...])
        if head is None:
            xn_ref, ya_ref, yb_ref = rest
            xn_ref[...] = xn
        else:
            g_ref, t_ref, dx_ref, ya_ref, yb_ref, dg_ref, loss_ref, acc_g, acc_l = rest
            i = pl.program_id(0)

            @pl.when(i == 0)
            def _():
                acc_g[...] = jnp.zeros_like(acc_g)
                acc_l[...] = jnp.zeros_like(acc_l)

            gv = g_ref[...]
            r = lax.rsqrt(jnp.mean(xn * xn, axis=-1, keepdims=True) + EPS)
            xh = xn * r
            err = xh * gv - t_ref[...]
            dy = err * (1.0 / d)
            gy = dy * gv
            dx_ref[...] = r * (gy - xh * jnp.mean(gy * xh, axis=-1, keepdims=True))
            acc_g[...] += jnp.sum((dy * xh).reshape(tm // 8, 8, d), axis=0)
            acc_l[...] += jnp.sum((err * err).reshape(tm // 8, 8, d), axis=0)

            @pl.when(i == n - 1)
            def _():
                dg_ref[...] = jnp.sum(acc_g[...], axis=0, keepdims=True)
                tot = jnp.sum(jnp.sum(acc_l[...], axis=0, keepdims=True), axis=1, keepdims=True)
                loss_ref[...] = jnp.broadcast_to(tot * (0.5 / d), (1, 128))
        ya_ref[...] = ya.astype(BF16)
        yb_ref[...] = yb.astype(BF16)

    row = lambda w, j: pl.BlockSpec((tm, w), lambda i: (i, j))
    const = lambda shape: pl.BlockSpec(shape, lambda i: (0, 0))
    in_specs = [row(d, 0), row(512, SEG["ga"][2] // 512), row(512, SEG["gr"][2] // 512),
                row(1024, SEG["gm"][2] // 1024), row(1024, SEG["gm"][2] // 1024 + 1),
                row(512, 0), row(512, 0), const((d, 1024)), const((d, d))]
    out_specs = [row(d, 0), row(d, 0), row(d, 0)]
    out_shape = [SDS((t, d), F32), SDS((t, d), BF16), SDS((t, d), BF16)]
    args, scratch = [x, z, z, z, z, oa, on, wb_t, wout], []
    if head is not None:
        in_specs += [const((1, d)), row(d, 0)]
        out_specs += [const((1, d)), const((1, 128))]
        out_shape += [SDS((1, d), F32), SDS((1, 128), F32)]
        args += list(head)
        scratch = [pltpu.VMEM((8, d), F32), pltpu.VMEM((8, d), F32)]
    return pl.pallas_call(
        body, name="merge_fwd", grid=(n,), in_specs=in_specs, out_specs=out_specs, out_shape=out_shape,
        scratch_shapes=scratch,
        compiler_params=_params(("arbitrary",) if head is not None else ("parallel",)),
    )(*args)


def _merge_bwd(dxo, z, oa, on, ya, yb, wb_t, wout):
    t, d = dxo.shape
    tm = min(256, t)
    n = t // tm

    def body(dx_ref, ga_ref, gr_ref, gm0_ref, gm1_ref, oa_ref, on_ref, ya_ref, yb_ref, wb_ref, wo_ref,
             doa_ref, don_ref, dz_ref, dwo_ref, dwb_ref, acc_o, acc_b):
        i = pl.program_id(0)

        @pl.when(i == 0)
        def _():
            acc_o[...] = jnp.zeros_like(acc_o)
            acc_b[...] = jnp.zeros_like(acc_b)

        dxb = dx_ref[...].astype(BF16)
        ya, yb = ya_ref[...].astype(F32), yb_ref[...].astype(F32)
        g0, g1 = _sigmoid(gm0_ref[...]), _sigmoid(gm1_ref[...])
        mb = (g0 * ya + g1 * yb).astype(BF16)
        dm = _dot(dxb, wo_ref[...], NT)
        dya = (dm * g0).astype(BF16)
        dyb = (dm * g1).astype(BF16)
        dz_ref[:, 1024:2048] = (dm * ya * g0 * (1.0 - g0)).astype(BF16)
        dz_ref[:, 2048:3072] = (dm * yb * g1 * (1.0 - g1)).astype(BF16)

        def branch(g_ref, o_ref, dy, w, do_ref, lo):
            gv, ov = g_ref[...], o_ref[...]
            sg = _sigmoid(gv)
            silu = gv * sg
            du = _dot(dy, w)
            do_ref[...] = du * silu
            dz_ref[:, lo:lo + 512] = (du * ov * (sg * (1.0 + gv * (1.0 - sg)))).astype(BF16)
            acc_b[:, lo:lo + 512] += _dot(dy, (silu * ov).astype(BF16), TN)

        branch(ga_ref, oa_ref, dya, wb_ref[:, :512], doa_ref, 0)
        branch(gr_ref, on_ref, dyb, wb_ref[:, 512:], don_ref, 512)
        acc_o[...] += _dot(mb, dxb, TN)

        @pl.when(i == n - 1)
        def _():
            dwo_ref[...] = acc_o[...].astype(BF16)
            dwb_ref[...] = acc_b[...].astype(BF16)

    row = lambda w, j: pl.BlockSpec((tm, w), lambda i: (i, j))
    const = lambda shape: pl.BlockSpec(shape, lambda i: (0, 0))
    return pl.pallas_call(
        body, name="merge_bwd", grid=(n,),
        in_specs=[row(d, 0), row(512, SEG["ga"][2] // 512), row(512, SEG["gr"][2] // 512),
                  row(1024, SEG["gm"][2] // 1024), row(1024, SEG["gm"][2] // 1024 + 1),
                  row(512, 0), row(512, 0), row(d, 0), row(d, 0), const((d, 1024)), const((d, d))],
        out_specs=[row(512, 0), row(512, 0), row(3072, 0), const((d, d)), const((d, 1024))],
        out_shape=[SDS((t, 512), F32), SDS((t, 512), F32), SDS((t, 3072), BF16), SDS((d, d), BF16),
                   SDS((d, 1024), BF16)],
        scratch_shapes=[pltpu.VMEM((d, d), F32), pltpu.VMEM((d, 1024), F32)],
        compiler_params=_params(("arbitrary",)),
    )(dxo, z, z, z, z, oa, on, ya, yb, wb_t, wout)


def _ret_bwd(qrot, krot, vb, orr, don, gnw, lgf, lgb, cos, sin):
    t = qrot.shape[0]
    c = RET_CHUNK
    nc = t // c
    hd = RET_HEAD_DIM
    unroll = 4 if nc % 4 == 0 else 1

    def body(lgf_ref, lgb_ref, q_ref, k_ref, v_ref, o_ref, dn_ref, w_ref, c_ref, s_ref,
             dq_ref, dk_ref, dv_ref, dw_ref, dlf_ref, dlb_ref, qt, kt, dob, uf, ub, wf, wb, sfa, sba, gfa, gba):
        h = pl.program_id(0)
        fw = _Dir(lgf_ref[h], False)
        bw = _Dir(lgb_ref[h], True)
        fw.dt, bw.dt = fw.d.T, bw.d.T

        o = o_ref[...]
        xc = o - jnp.mean(o, axis=-1, keepdims=True)
        r = lax.rsqrt(jnp.mean(xc * xc, axis=-1, keepdims=True) + EPS)
        xh = xc * r
        dn = dn_ref[...]
        gy = dn * w_ref[...]
        d_o = r * (gy - jnp.mean(gy, axis=-1, keepdims=True) - xh * jnp.mean(gy * xh, axis=-1, keepdims=True))
        dw_ref[...] = jnp.sum(dn * xh, axis=0, keepdims=True)
        dob[...] = d_o.astype(BF16)
        for i in range(nc):
            qt[i] = q_ref[i * c:(i + 1) * c, :].astype(F32).T.astype(BF16)
            kt[i] = k_ref[i * c:(i + 1) * c, :].astype(F32).T.astype(BF16)

        def rows(ci):
            return pl.ds(pl.multiple_of(ci * c, c), c)

        def products(ci, carry):
            sl = rows(ci)
            vv, do32 = v_ref[sl, :], dob[sl, :].astype(F32)
            ktf = kt[ci].astype(F32)
            uf[ci] = _dot((ktf * fw.kd_row).astype(BF16), vv)
            ub[ci] = _dot((ktf * bw.kd_row).astype(BF16), vv)
            wf[ci] = _dot(qt[ci], (do32 * fw.qd).astype(BF16))
            wb[ci] = _dot(qt[ci], (do32 * bw.qd).astype(BF16))
            return carry

        lax.fori_loop(0, nc, products, 0, unroll=16 if nc % 16 == 0 else unroll)

        def scan(i, carry):
            sf, sb, gf, gb = carry
            j = nc - 1 - i
            sfa[i] = sf.astype(BF16)
            sba[j] = sb.astype(BF16)
            gfa[j] = gf.astype(BF16)
            gba[i] = gb.astype(BF16)
            return sf * fw.cd + uf[i], sb * bw.cd + ub[j], gf * fw.cd + wf[j], gb * bw.cd + wb[i]

        zero = jnp.zeros((hd, hd), F32)
        lax.fori_loop(0, nc, scan, (zero, zero, zero, zero))

        def one_dir(p, s_all, g_all, ci, qq, kk, vv, do, a, bm):
            sb, gb = s_all[ci], g_all[ci]
            doq = (do.astype(F32) * p.qd).astype(BF16)
            dqc = _dot(doq, sb, NT)
            kkd = (kk.astype(F32) * p.kd_col).astype(BF16)
            dk2 = _dot(vv, gb, NT) * p.kd_col
            terms = (p.dist * p.d * a * bm + p.wq * qq.astype(F32) * dqc + p.wk * kk.astype(F32) * dk2
                     + (float(c) * p.cd) * gb.astype(F32) * sb.astype(F32))
            return dqc, dk2, _dot(kkd, gb), terms

        d_both, dt_both = fw.d + bw.d, fw.dt + bw.dt

        def chunk(ci, carry):
            af, ab = carry
            sl = rows(ci)
            qq, kk, vv, do = q_ref[sl, :], k_ref[sl, :], v_ref[sl, :], dob[sl, :]
            a, bm = _dot(qq, kk, NT), _dot(do, vv, NT)
            at, bt = _dot(kk, qq, NT), _dot(vv, do, NT)
            dqf, dkf, dvf, tf = one_dir(fw, sfa, gfa, ci, qq, kk, vv, do, a, bm)
            dqb, dkb, dvb, tb = one_dir(bw, sba, gba, ci, qq, kk, vv, do, a, bm)
            cc, ss = c_ref[sl, :], s_ref[sl, :]
            dq = _dot((bm * d_both).astype(BF16), kk) + dqf + dqb
            dk = _dot((bt * dt_both).astype(BF16), qq) + dkf + dkb
            dq_ref[sl, :] = _rope_bwd(dq, cc, ss, hd // 4).astype(BF16)
            dk_ref[sl, :] = (_rope_bwd(dk, cc, ss, hd // 4) * (hd ** -0.5)).astype(BF16)
            dv_ref[sl, :] = (_dot((at * dt_both).astype(BF16), do) + dvf + dvb).astype(BF16)
            return af + tf, ab + tb

        pair = 16 if nc % 16 == 0 else 1

        def chunks(i, carry):
            for j in range(pair):
                carry = chunk(i * pair + j, carry)
            return carry

        af, ab = lax.fori_loop(0, nc // pair, chunks, (zero, zero))
        tot = lambda m: jnp.sum(jnp.sum(m, axis=0, keepdims=True), axis=1, keepdims=True)
        dlf_ref[...] = jnp.broadcast_to(tot(af).reshape(1, 1, 1), (1, 8, 128))
        dlb_ref[...] = jnp.broadcast_to(tot(ab).reshape(1, 1, 1), (1, 8, 128))

    smem = pl.BlockSpec(memory_space=pltpu.SMEM)
    head = pl.BlockSpec((t, 128), lambda h: (0, h))
    vec = pl.BlockSpec((1, 128), lambda h: (0, h))
    scal = pl.BlockSpec((1, 8, 128), lambda h: (h, 0, 0))
    table = pl.BlockSpec((t, 128), lambda h: (0, 0))
    mats = lambda dt: pltpu.VMEM((nc, hd, hd), dt)
    return pl.pallas_call(
        body, name="ret_bwd", grid=(RET_HEADS,),
        in_specs=[smem, smem, head, head, head, head, head, vec, table, table],
        out_specs=[head, head, head, vec, scal, scal],
        out_shape=[SDS((t, RET_WIDTH), BF16)] * 3 + [SDS((1, RET_WIDTH), F32), SDS((RET_HEADS, 8, 128), F32),
                                                    SDS((RET_HEADS, 8, 128), F32)],
        scratch_shapes=[pltpu.VMEM((nc, hd, c), BF16), pltpu.VMEM((nc, hd, c), BF16), pltpu.VMEM((t, hd), BF16),
                        mats(F32), mats(F32), mats(F32), mats(F32), mats(BF16), mats(BF16), mats(BF16), mats(BF16)],
        compiler_params=_params(("parallel",)),
    )(lgf, lgb, qrot, krot, vb, orr, don, gnw, cos, sin)


def _attn_bwd(q, qt, k, v, doa, oa, lse, ex=None):
    t = q.shape[1]
    tq = min(ATTN_BWD_QUERY_TILE, t)
    nq = t // tq
    tk = min(ATTN_BWD_KEY_CHUNK, t)
    nk = t // tk
    hd = ATTN_HEAD_DIM
    scale = hd ** -0.5

    def body(q_ref, qt_ref, k_ref, v_ref, do_ref, o_ref, lse_ref, dq_ref, dkt_ref, dvt_ref):
        p, i = pl.program_id(0), pl.program_id(1)

        @pl.when(jnp.logical_and(p % 2 == 0, i == 0))
        def _():
            dkt_ref[...] = jnp.zeros_like(dkt_ref)
            dvt_ref[...] = jnp.zeros_like(dvt_ref)

        dov, ov = do_ref[...], o_ref[...]
        dovt = dov.T
        lanes = lambda col: jnp.concatenate([col] * (tk // 128), axis=1)
        outs = []
        for j in range(2):
            qq, qqt = q_ref[j], qt_ref[j]
            do32 = dov[:, j * hd:(j + 1) * hd]
            do, dot_ = do32.astype(BF16), dovt[j * hd:(j + 1) * hd, :].astype(BF16)
            dd = lanes(jnp.broadcast_to(jnp.sum(do32 * ov[:, j * hd:(j + 1) * hd], axis=1, keepdims=True), (tq, 128)))
            lse_j = lanes(jnp.broadcast_to(lse_ref[j], (128, tq)).T)
            dq = jnp.zeros((tq, hd), F32)
            for c in range(nk):
                sl = slice(c * tk, (c + 1) * tk)
                kc, vc = k_ref[0, sl, :], v_ref[0, sl, :]
                pr = jnp.exp(_dot(qq, kc, NT) - lse_j)
                ds = (pr * (_dot(do, vc, NT) - dd)).astype(BF16)
                dvt_ref[0, :, sl] += _dot(dot_, pr.astype(BF16))
                dkt_ref[0, :, sl] += _dot(qqt, ds)
                dq = dq + _dot(ds, kc)
            outs.append(dq * scale)
        dq_ref[...] = jnp.concatenate(outs, axis=-1)

    kv = pl.BlockSpec((1, t, hd), lambda p, i: (p // 2, 0, 0))
    kvt = pl.BlockSpec((1, hd, t), lambda p, i: (p // 2, 0, 0))
    pair = pl.BlockSpec((tq, 128), lambda p, i: (i, p))
    first = lambda: jnp.logical_and(pl.program_id(0) == 0, pl.program_id(1) == 0)
    last = lambda: jnp.logical_and(pl.program_id(0) == 3, pl.program_id(1) == nq - 1)
    xi, xo, xs, xscr, xargs = _ex_args(ex)
    return pl.pallas_call(
        _with_exchange(body, 7, 3, 0, ex, first, last), name="attn_bwd", grid=(4, nq),
        in_specs=[pl.BlockSpec((2, tq, hd), lambda p, i: (p, i, 0)), pl.BlockSpec((2, hd, tq), lambda p, i: (p, 0, i)),
                  kv, kv, pair, pair, pl.BlockSpec((2, 1, tq), lambda p, i: (p, 0, i))] + xi,
        out_specs=[pair, kvt, kvt] + xo,
        out_shape=[SDS((t, ATTN_WIDTH), F32), SDS((ATTN_KV_HEADS, hd, t), F32),
                   SDS((ATTN_KV_HEADS, hd, t), F32)] + xs,
        scratch_shapes=xscr,
        compiler_params=_params(("arbitrary", "arbitrary")),
    )(q, qt, k, v, doa, oa, lse, *xargs)


def _attn_post_bwd(dq, dk, dv, z, qn, kn, cos, sin, ones_bd):
    t = z.shape[0]
    tm = min(512, t)
    n = t // tm
    hd = ATTN_HEAD_DIM

    def body(dq_ref, dk_ref, dv_ref, zq_ref, zkv_ref, qn_ref, kn_ref, c_ref, s_ref, b_ref,
             dz_ref, dqn_ref, dkn_ref, acc_q, acc_k):
        i = pl.program_id(0)

        @pl.when(i == 0)
        def _():
            acc_q[...] = jnp.zeros_like(acc_q)
            acc_k[...] = jnp.zeros_like(acc_k)

        bd = b_ref[...]
        c2, s2 = c_ref[...], s_ref[...]

        def norm_bwd(dy, x, w, ones, cos_t, sin_t, acc):
            dyr = _rope_bwd(dy, cos_t, sin_t, hd // 4)
            r = lax.rsqrt(_group_mean(x * x, ones) + EPS)
            xh = x * r
            gy = dyr * w
            acc[...] += jnp.sum((dyr * xh).reshape(tm // 8, 8, x.shape[-1]), axis=0)
            return r * (gy - xh * _group_mean(gy * xh, ones))

        cq = jnp.concatenate([c2] * 4, axis=-1)
        sq = jnp.concatenate([s2] * 4, axis=-1)
        dz_ref[:, :512] = norm_bwd(dq_ref[...], zq_ref[...], qn_ref[...], bd, cq, sq, acc_q).astype(BF16)
        zkv = zkv_ref[...]
        dkk = jnp.concatenate([dk_ref[0], dk_ref[1]], axis=0).T
        dz_ref[:, 512:640] = norm_bwd(dkk, zkv[:, :128], kn_ref[...], bd[:128, :128], c2, s2, acc_k).astype(BF16)
        dz_ref[:, 640:768] = jnp.concatenate([dv_ref[0], dv_ref[1]], axis=0).T.astype(BF16)

        @pl.when(i == n - 1)
        def _():
            dqn_ref[...] = jnp.sum(acc_q[...], axis=0, keepdims=True)
            dkn_ref[...] = jnp.sum(acc_k[...], axis=0, keepdims=True)

    kv_blk = SEG["ka"][2] // 256
    kvs = pl.BlockSpec((ATTN_KV_HEADS, hd, tm), lambda i: (0, 0, i))
    const = lambda shape: pl.BlockSpec(shape, lambda i: (0, 0))
    return pl.pallas_call(
        body, name="attn_post_bwd", grid=(n,),
        in_specs=[pl.BlockSpec((tm, 512), lambda i: (i, 0)), kvs, kvs,
                  pl.BlockSpec((tm, 512), lambda i: (i, 0)), pl.BlockSpec((tm, 256), lambda i: (i, kv_blk)),
                  const((1, 512)), const((1, 128)),
                  pl.BlockSpec((tm, 128), lambda i: (i, 0)), pl.BlockSpec((tm, 128), lambda i: (i, 0)),
                  const((512, 512))],
        out_specs=[pl.BlockSpec((tm, 768), lambda i: (i, 0)), const((1, 512)), const((1, 128))],
        out_shape=[SDS((t, 768), BF16), SDS((1, 512), F32), SDS((1, 128), F32)],
        scratch_shapes=[pltpu.VMEM((8, 512), F32), pltpu.VMEM((8, 128), F32)],
        compiler_params=_params(("arbitrary",)),
    )(dq, dk, dv, z, z, qn, kn, cos, sin, ones_bd)


def _in_bwd(dxo, x, g, w_t, dz_a, dz_m, dqr, dkr, dvr, after=None):
    t, d = x.shape
    tm = min(512, t)
    n = t // tm
    parts = [(0, 0, 768, 0), (1, 0, 512, SEG["ga"][0]), (2, 0, 512, SEG["qr"][0]), (3, 0, 512, SEG["kr"][0]),
             (4, 0, 512, SEG["vr"][0]), (1, 512, 2560, SEG["gr"][0])]

    def body(dx_ref, x_ref, g_ref, w_ref, a_ref, m_ref, q_ref, k_ref, v_ref, o_ref, dg_ref, acc):
        i = pl.program_id(0)

        @pl.when(i == 0)
        def _():
            acc[...] = jnp.zeros_like(acc)

        pieces = [a_ref, m_ref, q_ref, k_ref, v_ref]
        dh = jnp.zeros((tm, d), F32)
        for pi, lo, w, row in parts:
            dh = dh + _dot(pieces[pi][:, lo:lo + w], w_ref[row:row + w, :])
        xv = x_ref[...]
        r = lax.rsqrt(jnp.mean(xv * xv, axis=-1, keepdims=True) + EPS)
        xh = xv * r
        gy = dh * g_ref[...]
        o_ref[...] = dx_ref[...] + r * (gy - xh * jnp.mean(gy * xh, axis=-1, keepdims=True))
        acc[...] += jnp.sum((dh * xh).reshape(tm // 8, 8, d), axis=0)

        @pl.when(i == n - 1)
        def _():
            dg_ref[...] = jnp.sum(acc[...], axis=0, keepdims=True)

    row = lambda w: pl.BlockSpec((tm, w), lambda i: (i, 0))
    const = lambda shape: pl.BlockSpec(shape, lambda i: (0, 0))
    extra = [] if after is None else [after]
    return pl.pallas_call(
        (lambda *refs: body(*refs[:9], *refs[9 + len(extra):])), name="in_bwd", grid=(n,),
        in_specs=[row(d), row(d), const((1, d)),
                  pl.BlockSpec((D_IN, d), lambda i: (0, 0), pipeline_mode=pl.Buffered(1)),
                  row(768), row(3072), row(512), row(512),
                  row(512)] + [const(a.shape) for a in extra],
        out_specs=[row(d), const((1, d))],
        out_shape=[SDS((t, d), F32), SDS((1, d), F32)],
        scratch_shapes=[pltpu.VMEM((8, d), F32)],
        compiler_params=_params(("arbitrary",)),
    )(dxo, x, g, w_t, dz_a, dz_m, dqr, dkr, dvr, *extra)


def _dw_in(h_t, dz_a, dz_m, dqr, dkr, dvr):
    d, t = h_t.shape
    tn = 256
    parts = [(0, 0, 0, 3), (1, 0, SEG["ga"][0] // tn, 2), (2, 0, SEG["qr"][0] // tn, 2),
             (3, 0, SEG["kr"][0] // tn, 2), (4, 0, SEG["vr"][0] // tn, 2), (1, 2, SEG["gr"][0] // tn, 10)]
    pieces = [dz_a, dz_m, dqr, dkr, dvr]

    def col_block(pi):
        mine = [(c0, r0, n) for q, c0, r0, n in parts if q == pi]

        def index(j):
            c0, r0, n = mine[0]
            blk = c0 + jnp.clip(j - r0, 0, n - 1)
            for c0, r0, n in mine[1:]:
                blk = jnp.where(j >= r0, c0 + jnp.clip(j - r0, 0, n - 1), blk)
            return 0, blk

        return index

    def body(h_ref, *refs):
        o_ref = refs[-1]
        j = pl.program_id(0)
        for pi, _, r0, n in parts:
            @pl.when(jnp.logical_and(j >= r0, j < r0 + n))
            def _(p_ref=refs[pi]):
                o_ref[...] = _dot(h_ref[...], p_ref[...]).T.astype(BF16)

    return pl.pallas_call(
        body, name="dw_in", grid=(D_IN // tn,),
        in_specs=[pl.BlockSpec((d, t), lambda j: (0, 0))] + [pl.BlockSpec((t, tn), col_block(pi)) for pi in range(5)],
        out_specs=pl.BlockSpec((tn, d), lambda j: (j, 0)),
        out_shape=SDS((D_IN, d), BF16),
        compiler_params=_params(("arbitrary",)),
    )(h_t, *pieces)


def _adamw_math(w, g, m, v):
    mn = ADAM_B1 * m + (1.0 - ADAM_B1) * g
    vn = ADAM_B2 * v + (1.0 - ADAM_B2) * (g * g)
    m_hat = mn / (1.0 - ADAM_B1 ** ADAM_STEP)
    v_hat = vn / (1.0 - ADAM_B2 ** ADAM_STEP)
    return -ADAM_LR * (m_hat / (jnp.sqrt(v_hat) + ADAM_EPS) + ADAM_WD * w), mn, vn


def _sum_adamw(recvs, w, m, v, lane0, tn, layer0=0, prev=None, own=None):
    _, r, c = w.shape
    j0 = lane0 // tn
    n = len(recvs)
    has_own = own is not None

    def body(*refs):
        mine_ref, refs = (refs[0], refs[1:]) if has_own else (None, refs)
        w_ref, m_ref, v_ref = refs[n:n + 3]
        g_ref, d_ref, mo_ref, vo_ref = refs[-4:]

        def run(r_ref):
            def slot(s):
                if has_own:
                    return jnp.where(mine_ref[0] == s, refs[n + 3][...], r_ref[s]).astype(F32)
                return r_ref[s].astype(F32)

            g = slot(0)
            for s in range(1, N_DEV):
                g = g + slot(s)
            g_ref[0] = g
            d_ref[0], mo_ref[0], vo_ref[0] = _adamw_math(w_ref[0], g, m_ref[0], v_ref[0])

        for i in range(n):
            pl.when(pl.program_id(0) == i)(functools.partial(run, refs[i]))

    slots = pl.BlockSpec((N_DEV, r, tn), lambda i, j, *_: (0, 0, j0 + j))
    blk = pl.BlockSpec((1, r, tn), lambda i, j, *_: (layer0 + i, 0, j))
    before = [] if prev is None else list(prev)
    in_specs, args = [slots] * n + [blk] * 3, [*recvs, w, m, v]
    if has_own:
        assert n == 1
        in_specs.append(pl.BlockSpec((r, tn), lambda i, j, mine: (mine[0], j0 + j)))
        args.append(own[0])
    n_pre = len(args) + has_own
    return pl.pallas_call(
        body, name="sum_adamw",
        grid_spec=pltpu.PrefetchScalarGridSpec(
            num_scalar_prefetch=int(has_own), grid=(n, c // tn),
            in_specs=in_specs + [ANY] * len(before), out_specs=[blk] * 4),
        out_shape=[SDS(w.shape, F32)] * 4,
        input_output_aliases={n_pre + k: k for k in range(len(before))},
        compiler_params=_params(("parallel", "parallel")),
    )(*([own[1]] if has_own else []), *args, *before)


def _adamw(w, g, m, v):
    rows, cols = w.shape
    tr = 256 if rows % 256 == 0 else rows

    def body(w_ref, g_ref, m_ref, v_ref, d_ref, mo_ref, vo_ref):
        d_ref[...], mo_ref[...], vo_ref[...] = _adamw_math(w_ref[...], g_ref[...], m_ref[...], v_ref[...])

    blk = pl.BlockSpec((tr, cols), lambda i: (i, 0))
    return pl.pallas_call(
        body, name="adamw", grid=(rows // tr,),
        in_specs=[blk] * 4, out_specs=[blk] * 3, out_shape=[SDS((rows, cols), F32)] * 3,
        compiler_params=_params(("parallel",)),
    )(w, g, m, v)


def _all_gather(shards):
    na = len(shards)
    chips = (4, 2, 6)

    def body(*refs):
        ins, outs = refs[:na], refs[na:2 * na]
        send_sems, recv_sems, local_sems = refs[2 * na:]
        _, mine = _flip(0)

        def rows(a, idx):
            r = shards[a].shape[0]
            return outs[a].at[pl.ds(pl.multiple_of(idx * r, 16), r), :]

        def copy(a, slot, block_idx, to, src=None):
            return pltpu.make_async_remote_copy(
                src_ref=rows(a, block_idx) if src is None else src, dst_ref=rows(a, block_idx),
                send_sem=send_sems.at[a, slot], recv_sem=recv_sems.at[a, slot],
                device_id=to, device_id_type=MESH_ID)

        sibling, sibling_idx = _flip(1)
        local, started = [], []
        for a in range(na):
            cp = pltpu.make_async_copy(ins[a], rows(a, mine), local_sems.at[a])
            cp.start()
            local.append(cp)
            first = [copy(a, 0, mine, sibling, src=ins[a])]
            first += [copy(a, 1 + j, mine, _flip(k)[0], src=ins[a]) for j, k in enumerate(chips)]
            for cp in first:
                cp.start()
            started += first
        for a in range(na):
            for j, k in enumerate(chips):
                _, theirs = _flip(k)
                copy(a, 1 + j, theirs, _flip(0)[0]).wait_recv()
                fwd = copy(a, 4 + j, theirs, sibling)
                fwd.start()
                started.append(fwd)
        for a in range(na):
            copy(a, 0, sibling_idx, _flip(0)[0]).wait_recv()
            for j, k in enumerate(chips):
                _, theirs = _flip(k | 1)
                copy(a, 4 + j, theirs, _flip(0)[0]).wait_recv()
        for cp in started:
            cp.wait_send()
        for cp in local:
            cp.wait()

    return pl.pallas_call(
        body, name="all_gather_weights",
        in_specs=[ANY] * na, out_specs=[ANY] * na,
        out_shape=[SDS((N_DEV * s.shape[0], s.shape[1]), s.dtype) for s in shards],
        scratch_shapes=[pltpu.SemaphoreType.DMA((na, 7)), pltpu.SemaphoreType.DMA((na, 7)),
                        pltpu.SemaphoreType.DMA((na,))],
        compiler_params=pltpu.CompilerParams(has_side_effects=True),
    )(*shards)


def _scatter_blocks_of(g_ref, rows, idx):
    return g_ref.at[pl.ds(pl.multiple_of(idx * rows, 16), rows), :]


def _scatter_start(g):
    rows = g.shape[0] // N_DEV
    land_shape = (N_DEV, rows, g.shape[1])

    def body(g_ref, land_ref, send_sems, recv_sems, g_thru, land_thru, token):
        _, mine = _flip(0)
        for k in range(1, N_DEV):
            peer, theirs = _flip(k)
            pltpu.make_async_remote_copy(
                src_ref=_scatter_blocks_of(g_ref, rows, theirs), dst_ref=land_ref.at[mine],
                send_sem=send_sems.at[k - 1], recv_sem=recv_sems.at[k - 1],
                device_id=peer, device_id_type=MESH_ID).start()
        token[...] = jnp.zeros_like(token)

    hbm, sem = pl.BlockSpec(memory_space=pltpu.HBM), pl.BlockSpec(memory_space=pltpu.SEMAPHORE)
    return pl.pallas_call(
        body, name="scatter_start",
        out_shape=(pltpu.SemaphoreType.DMA((N_DEV - 1,)), pltpu.SemaphoreType.DMA((N_DEV - 1,)),
                   pltpu.HBM(g.shape, g.dtype), pltpu.HBM(land_shape, g.dtype), SDS((8, 128), F32)),
        in_specs=(hbm, hbm), out_specs=(sem, sem, hbm, hbm, pl.BlockSpec(memory_space=pltpu.VMEM)),
        input_output_aliases={0: 2, 1: 3},
        compiler_params=pltpu.CompilerParams(has_side_effects=pltpu.SideEffectType.DATAFLOW_SIDE_EFFECTING),
    )(pltpu.with_memory_space_constraint(g, pltpu.HBM),
      pltpu.with_memory_space_constraint(lax.empty(land_shape, g.dtype), pltpu.HBM))


def _scatter_wait(send_sems, recv_sems, g_thru, land_thru, after):
    rows = g_thru.shape[0] // N_DEV

    def body(g_ref, land_ref, send_sems, recv_sems, *rest):
        me, _ = _flip(0)
        for k in range(1, N_DEV):
            _, theirs = _flip(k)
            copy = pltpu.make_async_remote_copy(
                src_ref=_scatter_blocks_of(g_ref, rows, theirs), dst_ref=land_ref.at[theirs],
                send_sem=send_sems.at[k - 1], recv_sem=recv_sems.at[k - 1],
                device_id=me, device_id_type=MESH_ID)
            copy.wait_send()
            copy.wait_recv()

    hbm, sem = pl.BlockSpec(memory_space=pltpu.HBM), pl.BlockSpec(memory_space=pltpu.SEMAPHORE)
    return pl.pallas_call(
        body, name="scatter_wait",
        out_shape=(pltpu.HBM(g_thru.shape, g_thru.dtype), pltpu.HBM(land_thru.shape, land_thru.dtype)),
        in_specs=(hbm, hbm, sem, sem) + (ANY,) * len(after), out_specs=(hbm, hbm), input_output_aliases={0: 0, 1: 1},
        compiler_params=pltpu.CompilerParams(has_side_effects=pltpu.SideEffectType.DATAFLOW_SIDE_EFFECTING),
    )(g_thru, land_thru, send_sems, recv_sems, *after)


def _all_reduce_small(packed):
    shape = packed.shape

    def body(p_ref, o_ref, slots, send_sems, recv_sems):
        me, mine = _flip(0)
        slots[mine] = p_ref[...]
        sends = []
        for k in range(1, N_DEV):
            peer, _ = _flip(k)
            cp = pltpu.make_async_remote_copy(
                src_ref=p_ref, dst_ref=slots.at[mine], send_sem=send_sems.at[k - 1], recv_sem=recv_sems.at[k - 1],
                device_id=peer, device_id_type=MESH_ID)
            cp.start()
            sends.append(cp)
        for k in range(1, N_DEV):
            _, theirs = _flip(k)
            pltpu.make_async_remote_copy(
                src_ref=p_ref, dst_ref=slots.at[theirs], send_sem=send_sems.at[k - 1],
                recv_sem=recv_sems.at[k - 1], device_id=me, device_id_type=MESH_ID).wait_recv()
        for cp in sends:
            cp.wait_send()
        acc = slots[0]
        for s in range(1, N_DEV):
            acc = acc + slots[s]
        o_ref[...] = acc

    vm = pl.BlockSpec(memory_space=pltpu.VMEM)
    return pl.pallas_call(
        body, name="all_reduce_small", in_specs=[vm], out_specs=vm, out_shape=SDS(shape, F32),
        scratch_shapes=[pltpu.VMEM((N_DEV,) + shape, F32), pltpu.SemaphoreType.DMA((7,)),
                        pltpu.SemaphoreType.DMA((7,))],
        compiler_params=pltpu.CompilerParams(has_side_effects=True),
    )(packed)


def _layer_fwd(x, p, tabs, ex):
    z, h_t, q, qt, k, v, vt, qrot, krot, vb = _in_proj(x, p["norm_g"], p["w_in_t"], p["qn"], p["kn"], tabs["ca"],
                                                       tabs["sa"], tabs["ones"], tabs["cr"], tabs["sr"])
    oa, lse, *gathered = _attn_fwd(q, k, vt, ex)
    orr, on = _ret_fwd(qrot, krot, vb, p["lgf"], p["lgb"], p["gnw"])
    return z, h_t, q, qt, k, v, lse, oa, qrot, krot, vb, orr, on, gathered


def _layer_bwd(dxo, s, p, tabs, ex_attn, scatter_w_in):
    doa, don, dz_m, d_wout, d_wb_t = _merge_bwd(dxo, s["z"], s["oa"], s["on"], s["ya"], s["yb"], p["wb_t"], p["w_out"])
    dq_a, dk_a, dv_a, *recv_attn = _attn_bwd(s["q"], s["qt"], s["k"], s["v"], doa, s["oa"], s["lse"],
                                              ex_attn(d_wb_t, d_wout))
    dz_a, d_qn, d_kn = _attn_post_bwd(dq_a, dk_a, dv_a, s["z"], p["qn"], p["kn"], tabs["ca"], tabs["sa"],
                                      tabs["ones"])
    dqr, dkr, dvr, d_gnw, d_lgf, d_lgb = _ret_bwd(s["qrot"], s["krot"], s["vb"], s["orr"], don, p["gnw"],
                                                  p["lgf"], p["lgb"], tabs["cr"], tabs["sr"])
    buf = _dw_in(s["h_t"], dz_a, dz_m, dqr, dkr, dvr)
    pending, token = None, None
    if scatter_w_in:
        *pending, token = _scatter_start(buf)
    dx, d_norm_g = _in_bwd(dxo, s["x"], p["norm_g"], p["w_in_t"], dz_a, dz_m, dqr, dkr, dvr, token)
    grads = dict(w_in_t=buf, wb_t=d_wb_t, w_out=d_wout, norm_g=d_norm_g, gnw=d_gnw,
                 qn=d_qn.reshape(ATTN_Q_HEADS, ATTN_HEAD_DIM).sum(axis=0),
                 kn=d_kn.reshape(ATTN_KV_HEADS, ATTN_HEAD_DIM).sum(axis=0),
                 lgf=d_lgf[:, 0, 0], lgb=d_lgb[:, 0, 0])
    return dx, grads, recv_attn, pending


def _adamw_nd(w, g, m, v):
    shape = w.shape
    two_d = (1, shape[0]) if w.ndim == 1 else (-1, shape[-1])
    out = _adamw(w.reshape(two_d), g.reshape(two_d), m.reshape(two_d), v.reshape(two_d))
    return tuple(o.reshape(shape) for o in out)


def kernel(x, norm_g, w_in, attn_q_norm, attn_k_norm, ret_decay_fwd, ret_decay_bwd, ret_gn_w, w_branch_attn, w_branch_ret, w_out, final_norm_g, loss_target, m_norm_g, m_w_in, m_attn_q_norm, m_attn_k_norm, m_ret_decay_fwd, m_ret_decay_bwd, m_ret_gn_w, m_w_branch_attn, m_w_branch_ret, m_w_out, m_final_norm_g, v_norm_g, v_w_in, v_attn_q_norm, v_attn_k_norm, v_ret_decay_fwd, v_ret_decay_bwd, v_ret_gn_w, v_w_branch_attn, v_w_branch_ret, v_w_out, v_final_norm_g):
    t, d = x.shape[1], x.shape[2]
    x2, target = x[0], loss_target[0]

    w_in_sh, wb_sh, wout_sh = [], [], []
    for l in range(DEPTH):
        w_in_sh.append(jnp.swapaxes(w_in[l], 0, 1).astype(BF16))
        wb_sh.append(jnp.concatenate([w_branch_attn[l].T, w_branch_ret[l].T], axis=1).astype(BF16))
        wout_sh.append(w_out[l].astype(BF16))

    ca, sa = _rope_tables(t, ATTN_HEAD_DIM)
    cr, sr = _rope_tables(t, RET_HEAD_DIM)
    grp = jnp.arange(ATTN_WIDTH) // ATTN_HEAD_DIM
    tabs = dict(ca=jnp.tile(ca, (1, 2)), sa=jnp.tile(sa, (1, 2)), cr=cr, sr=sr,
                ones=jnp.where(grp[:, None] == grp[None, :], 1.0 / ATTN_HEAD_DIM, 0.0).astype(BF16))
    layers = []
    for l in range(DEPTH):
        layers.append(dict(
            norm_g=norm_g[l][None], qn=jnp.tile(attn_q_norm[l], ATTN_Q_HEADS)[None],
            kn=jnp.tile(attn_k_norm[l], ATTN_KV_HEADS)[None], gnw=ret_gn_w[l][None],
            lgf=jax.nn.log_sigmoid(ret_decay_fwd[l]), lgb=jax.nn.log_sigmoid(ret_decay_bwd[l])))

    layers[0]["w_in_t"], = _all_gather([w_in_sh[0]])
    gathers = [_Exchange("gather", [wb_sh[0], wout_sh[0], w_in_sh[1]]), _Exchange("gather", [wb_sh[1], wout_sh[1]])]
    h = x2
    saved = []
    for l in range(DEPTH):
        p = layers[l]
        z, h_t, q, qt, k, v, lse, oa, qrot, krot, vb, orr, on, got = _layer_fwd(h, p, tabs, gathers[l])
        p["wb_t"], p["w_out"] = got[0], got[1]
        if l == 0:
            layers[1]["w_in_t"] = got[2]
        last = (final_norm_g[None], target) if l == DEPTH - 1 else None
        xn, ya, yb, *loss_head = _merge_fwd(h, z, oa, on, p["wb_t"], p["w_out"], last)
        saved.append(dict(x=h, z=z, h_t=h_t, q=q, qt=qt, k=k, v=v, lse=lse, oa=oa, qrot=qrot, krot=krot, vb=vb,
                          orr=orr, on=on, ya=ya, yb=yb))
        h = xn
    dx, (d_final_g, loss_part) = h, loss_head

    grads = [None] * DEPTH
    dx, grads[1], _, _ = _layer_bwd(dx, saved[1], layers[1], tabs, lambda *a: None, False)
    g1 = grads[1]
    ex_attn = lambda d_wb_t, d_wout: _Exchange("scatter", [g1["w_in_t"], g1["wb_t"], g1["w_out"], d_wb_t, d_wout])
    dx, grads[0], recv_attn, pending = _layer_bwd(dx, saved[0], layers[0], tabs, ex_attn, True)
    recv = [None, recv_attn[3], recv_attn[4], recv_attn[0], recv_attn[1], recv_attn[2]]
    tr = lambda a: jnp.swapaxes(a, 1, 2)
    w_in_t = (tr(w_in), tr(m_w_in), tr(v_w_in))
    sharded = {}
    w_in_l1 = _sum_adamw([recv[3]], *w_in_t, 0, 256, layer0=1)
    sharded[id(w_branch_attn)] = [tr(o) for o in _sum_adamw(
        [recv[1], recv[4]], tr(w_branch_attn), tr(m_w_branch_attn), tr(v_w_branch_attn), 0, 512)]
    sharded[id(w_branch_ret)] = [tr(o) for o in _sum_adamw(
        [recv[1], recv[4]], tr(w_branch_ret), tr(m_w_branch_ret), tr(v_w_branch_ret), 512, 512)]
    sharded[id(w_out)] = _sum_adamw([recv[2], recv[5]], w_out, m_w_out, v_w_out, 0, 256)
    g_wba, g_wbr, g_wout = (sharded[id(w)][0] for w in (w_branch_attn, w_branch_ret, w_out))

    packed = jnp.zeros((8, 1024), F32)
    for l in range(DEPTH):
        gl = grads[l]
        packed = packed.at[l].set(gl["norm_g"][0])
        packed = packed.at[2, 512 * l:512 * (l + 1)].set(gl["gnw"][0])
        packed = packed.at[4, 128 * l:128 * l + 64].set(gl["qn"])
        packed = packed.at[4, 256 + 128 * l:256 + 128 * l + 64].set(gl["kn"])
        packed = packed.at[4, 512 + 128 * l:512 + 128 * l + 4].set(gl["lgf"])
        packed = packed.at[4, 768 + 128 * l:768 + 128 * l + 4].set(gl["lgb"])
    packed = packed.at[3].set(d_final_g[0])
    packed = packed.at[5, 0].set(loss_part[0, 0])
    red = _all_reduce_small(packed)
    loss = red[5, 0]
    g_norm_g = red[0:2]
    g_gnw = red[2].reshape(DEPTH, RET_WIDTH)
    g_final = red[3]
    g_qn = jnp.stack([red[4, 128 * l:128 * l + 64] for l in range(DEPTH)])
    g_kn = jnp.stack([red[4, 256 + 128 * l:256 + 128 * l + 64] for l in range(DEPTH)])
    g_lgf = jnp.stack([red[4, 512 + 128 * l:512 + 128 * l + 4] for l in range(DEPTH)])
    g_lgb = jnp.stack([red[4, 768 + 128 * l:768 + 128 * l + 4] for l in range(DEPTH)])
    g_df = g_lgf * jax.nn.sigmoid(-ret_decay_fwd)
    g_db = g_lgb * jax.nn.sigmoid(-ret_decay_bwd)

    grad_w = [g_norm_g, None, g_qn, g_kn, g_df, g_db, g_gnw, g_wba, g_wbr, g_wout, g_final]
    weights = [norm_g, w_in, attn_q_norm, attn_k_norm, ret_decay_fwd, ret_decay_bwd, ret_gn_w, w_branch_attn,
               w_branch_ret, w_out, final_norm_g]
    ms = [m_norm_g, m_w_in, m_attn_q_norm, m_attn_k_norm, m_ret_decay_fwd, m_ret_decay_bwd, m_ret_gn_w,
          m_w_branch_attn, m_w_branch_ret, m_w_out, m_final_norm_g]
    vs = [v_norm_g, v_w_in, v_attn_q_norm, v_attn_k_norm, v_ret_decay_fwd, v_ret_decay_bwd, v_ret_gn_w,
          v_w_branch_attn, v_w_branch_ret, v_w_out, v_final_norm_g]
    upd = [None if w is w_in else sharded[id(w)][1:] if id(w) in sharded else _adamw_nd(w, g, m, v)
           for w, g, m, v in zip(weights, grad_w, ms, vs)]

    done = [dx, w_in_l1[0], g_wout] + [u[0] for w, u in zip(weights, upd) if u is not None and id(w) not in sharded]
    g_full, recv[0] = _scatter_wait(*pending, done)
    mine = (4 * lax.axis_index("x") + 2 * lax.axis_index("y") + lax.axis_index("c")).astype(jnp.int32)[None]
    w_in_upd = [tr(o) for o in _sum_adamw([recv[0]], *w_in_t, 0, 256, layer0=0, prev=w_in_l1, own=(g_full, mine))]
    grad_w[1], upd[1] = w_in_upd[0], w_in_upd[1:]
    return (loss, dx[None], *grad_w, *[u[0] for u in upd], *[u[1] for u in upd], *[u[2] for u in upd])
```
